```python
import jax, jax.numpy as jnp
from jax import lax
import numpy as np

D_MODEL = 1024
BATCH = 8
SEQ = 8192
DEPTH = 1

GRID_W = 64
CTX_LEN = 256
HEAD_DIM = 64
N_Q_HEADS = 8
N_KV_HEADS = 2
GQA_GROUP = N_Q_HEADS // N_KV_HEADS
WINDOW = 128
BLOCK = 128
ROPE_BASE = 10000.0
ROPE_PAIRS = HEAD_DIM // 4
N_GMLP_GROUPS = 8
GMLP_GROUP_DIM = 64
GMLP_WIDTH = N_GMLP_GROUPS * GMLP_GROUP_DIM
CHUNK = 128
FFN_HIDDEN = ((8 * D_MODEL // 3 + 255) // 256) * 256
Q_W = N_Q_HEADS * HEAD_DIM
KV_W = N_KV_HEADS * HEAD_DIM
IN_SPLITS = (Q_W, Q_W + KV_W, Q_W + 2 * KV_W, Q_W + 2 * KV_W + GMLP_WIDTH,
             Q_W + 2 * KV_W + 2 * GMLP_WIDTH, Q_W + 2 * KV_W + 2 * GMLP_WIDTH + D_MODEL)
IN_W = Q_W + 2 * KV_W + 2 * GMLP_WIDTH + 2 * D_MODEL
LN_EPS = 1e-5
NEG_INF = -1e30
DEEPNORM_ALPHA = (2 * DEPTH) ** 0.25
DEEPNORM_BETA = (8 * DEPTH) ** -0.25

kernel_name = 'hybrid_window_gqa_gmlp_dit_block'


def layer_norm(x, g=None, b=None):
    xf = x.astype(jnp.float32)
    mu = jnp.mean(xf, axis=-1, keepdims=True)
    var = jnp.mean(jnp.square(xf - mu), axis=-1, keepdims=True)
    y = (xf - mu) * lax.rsqrt(var + LN_EPS)
    if g is not None:
        y = y * g.astype(jnp.float32) + b.astype(jnp.float32)
    return y.astype(x.dtype)


def modulate(y, shift, scale):
    return y * (1 + scale[..., None, :]) + shift[..., None, :]


def axial_rope(t, rows, cols):
    inv = ROPE_BASE ** (-jnp.arange(ROPE_PAIRS, dtype=jnp.float32) / ROPE_PAIRS)

    def rot(xa, pos):
        ang = pos.astype(jnp.float32)[:, None] * inv
        cos = jnp.cos(ang)[:, None, :].astype(t.dtype)
        sin = jnp.sin(ang)[:, None, :].astype(t.dtype)
        x1, x2 = xa[..., :ROPE_PAIRS], xa[..., ROPE_PAIRS:]
        return jnp.concatenate([x1 * cos - x2 * sin, x1 * sin + x2 * cos], axis=-1)

    half = HEAD_DIM // 2
    return jnp.concatenate([rot(t[..., :half], rows), rot(t[..., half:], cols)], axis=-1)


def window_attention(q, k, v, kc, vc, sink):
    B, L = q.shape[:2]
    nb = L // BLOCK
    C = kc.shape[1]
    qb = q.reshape(B, nb, BLOCK, N_KV_HEADS, GQA_GROUP, HEAD_DIM)

    def band(t):
        tp = jnp.pad(t, ((0, 0), (BLOCK, BLOCK), (0, 0), (0, 0)))
        tp = tp.reshape(B, nb + 2, BLOCK, N_KV_HEADS, HEAD_DIM)
        return jnp.concatenate([tp[:, :-2], tp[:, 1:-1], tp[:, 2:]], axis=2)

    kw, vw = band(k), band(v)
    scale = HEAD_DIM ** -0.5
    s_loc = jnp.einsum('bnqhgd,bnjhd->bnhgqj', qb, kw).astype(jnp.float32) * scale
    s_ctx = jnp.einsum('bnqhgd,bchd->bnhgqc', qb, kc).astype(jnp.float32) * scale
    qi = jnp.arange(BLOCK)[:, None]
    kj = jnp.arange(3 * BLOCK)[None, :]
    rel = kj - BLOCK - qi
    kpos = jnp.arange(nb)[:, None, None] * BLOCK - BLOCK + kj[None]
    valid = (jnp.abs(rel) <= WINDOW)[None] & (kpos >= 0) & (kpos < L)
    s_loc = jnp.where(valid[None, :, None, None], s_loc, NEG_INF)
    sink_l = jnp.broadcast_to(
        sink.astype(jnp.float32).reshape(N_KV_HEADS, GQA_GROUP)[None, None, :, :, None, None],
        s_loc.shape[:-1] + (1,))
    logits = jnp.concatenate([sink_l, s_ctx, s_loc], axis=-1)
    probs = jax.nn.softmax(logits, axis=-1).astype(v.dtype)
    p_ctx, p_loc = probs[..., 1:1 + C], probs[..., 1 + C:]
    out = (jnp.einsum('bnhgqc,bchd->bnqhgd', p_ctx, vc)
           + jnp.einsum('bnhgqj,bnjhd->bnqhgd', p_loc, vw))
    return out.reshape(B, L, Q_W)


def context_attention(q, k, v, sink):
    B, C = q.shape[:2]
    qg = q.reshape(B, C, N_KV_HEADS, GQA_GROUP, HEAD_DIM)
    s = jnp.einsum('bqhgd,bkhd->bhgqk', qg, k).astype(jnp.float32) * HEAD_DIM ** -0.5
    sink_l = jnp.broadcast_to(
        sink.astype(jnp.float32).reshape(N_KV_HEADS, GQA_GROUP)[None, :, :, None, None],
        s.shape[:-1] + (1,))
    probs = jax.nn.softmax(jnp.concatenate([sink_l, s], axis=-1), axis=-1).astype(v.dtype)
    out = jnp.einsum('bhgqk,bkhd->bqhgd', probs[..., 1:], v)
    return out.reshape(B, C, Q_W)


def chunk_gmlp(u, vb, ln_g, ln_b, w_s, b_s):
    B, L = u.shape[:2]
    nc = L // CHUNK
    vn = layer_norm(vb, ln_g, ln_b).reshape(B, nc, CHUNK, N_GMLP_GROUPS, GMLP_GROUP_DIM)
    s = jnp.einsum('gij,bnjgd->bnigd', w_s, vn) + b_s.T[None, None, :, :, None]
    return u * s.reshape(B, L, GMLP_WIDTH)


def token_mixer(h, w_in, sink, gmlp_g, gmlp_b, w_s, b_s, w_a, w_b, w_o, kc, vc, pos):
    B, L = h.shape[:2]
    q, k, v, u, vb, ga, gb = jnp.split(h @ w_in, IN_SPLITS, axis=-1)
    q = q.reshape(B, L, N_Q_HEADS, HEAD_DIM)
    k = k.reshape(B, L, N_KV_HEADS, HEAD_DIM)
    v = v.reshape(B, L, N_KV_HEADS, HEAD_DIM)
    if pos is None:
        ya = context_attention(q, k, v, sink)
    else:
        rows, cols = pos
        ya = window_attention(axial_rope(q, rows, cols), axial_rope(k, rows, cols), v, kc, vc, sink)
    yb = chunk_gmlp(jax.nn.gelu(u), jax.nn.gelu(vb), gmlp_g, gmlp_b, w_s, b_s)
    merged = jax.nn.sigmoid(ga) * (ya @ w_a) + jax.nn.sigmoid(gb) * (yb @ w_b)
    return merged @ w_o


def context_kv(hc, w_in):
    B, C = hc.shape[:2]
    k, v = jnp.split(hc @ w_in[:, Q_W:Q_W + 2 * KV_W], 2, axis=-1)
    return (k.reshape(B, C, N_KV_HEADS, HEAD_DIM), v.reshape(B, C, N_KV_HEADS, HEAD_DIM))


def swiglu(h, w_ffn_in, w_ffn_out):
    gate, up = jnp.split(h @ w_ffn_in, 2, axis=-1)
    return (jax.nn.silu(gate) * up) @ w_ffn_out


def post_norm(res, out, g, b):
    return layer_norm(DEEPNORM_ALPHA * res + out, g, b)


def _fwd_setup_inputs(seed: int = 0) -> dict:
    key = jax.random.key(seed)
    ks = jax.random.split(key, 21)
    f32 = jnp.float32

    def nrm(k, shape, s):
        return jax.random.normal(k, shape, f32) * s

    return {
        'x': nrm(ks[0], (BATCH, SEQ, D_MODEL), 1.0),
        'c': nrm(ks[1], (BATCH, D_MODEL), 1.0),
        'ctx': nrm(ks[2], (BATCH, CTX_LEN, D_MODEL), 1.0),
        'c_ctx': nrm(ks[3], (D_MODEL,), 1.0),
        'w_ada': nrm(ks[4], (DEPTH, D_MODEL, 6 * D_MODEL), 0.5 * D_MODEL ** -0.5),
        'b_ada': nrm(ks[5], (DEPTH, 6 * D_MODEL), 0.02),
        'w_in': nrm(ks[6], (DEPTH, D_MODEL, IN_W), D_MODEL ** -0.5),
        'attn_sink': nrm(ks[7], (DEPTH, N_Q_HEADS), 0.5),
        'gmlp_ln_g': 1.0 + nrm(ks[8], (DEPTH, GMLP_WIDTH), 0.02),
        'gmlp_ln_b': nrm(ks[9], (DEPTH, GMLP_WIDTH), 0.02),
        'w_spatial': nrm(ks[10], (DEPTH, N_GMLP_GROUPS, CHUNK, CHUNK), CHUNK ** -0.5),
        'b_spatial': 1.0 + nrm(ks[11], (DEPTH, N_GMLP_GROUPS, CHUNK), 0.02),
        'w_branch_a': nrm(ks[12], (DEPTH, Q_W, D_MODEL), Q_W ** -0.5),
        'w_branch_b': nrm(ks[13], (DEPTH, GMLP_WIDTH, D_MODEL), GMLP_WIDTH ** -0.5),
        'w_out': nrm(ks[14], (DEPTH, D_MODEL, D_MODEL), DEEPNORM_BETA * D_MODEL ** -0.5),
        'ln1_g': 1.0 + nrm(ks[15], (DEPTH, D_MODEL), 0.02),
        'ln1_b': nrm(ks[16], (DEPTH, D_MODEL), 0.02),
        'w_ffn_in': nrm(ks[17], (DEPTH, D_MODEL, 2 * FFN_HIDDEN), D_MODEL ** -0.5),
        'w_ffn_out': nrm(ks[18], (DEPTH, FFN_HIDDEN, D_MODEL), DEEPNORM_BETA * FFN_HIDDEN ** -0.5),
        'ln2_g': 1.0 + nrm(ks[19], (DEPTH, D_MODEL), 0.02),
        'ln2_b': nrm(ks[20], (DEPTH, D_MODEL), 0.02),
    }


def _fwd_reference(x, c, ctx, c_ctx, w_ada, b_ada, w_in, attn_sink, gmlp_ln_g, gmlp_ln_b,
              w_spatial, b_spatial, w_branch_a, w_branch_b, w_out, ln1_g, ln1_b,
              w_ffn_in, w_ffn_out, ln2_g, ln2_b):
    L = x.shape[1]
    n_rows = L // GRID_W
    rows = jnp.repeat(jnp.arange(n_rows, dtype=jnp.int32), GRID_W)
    cols = jnp.tile(jnp.arange(GRID_W, dtype=jnp.int32), n_rows)

    for layer in range(DEPTH):
        mod_x = jnp.split(jax.nn.silu(c) @ w_ada[layer] + b_ada[layer], 6, axis=-1)
        mod_c = jnp.split(jax.nn.silu(c_ctx) @ w_ada[layer] + b_ada[layer], 6, axis=-1)
        mix_params = (w_in[layer], attn_sink[layer], gmlp_ln_g[layer], gmlp_ln_b[layer],
                      w_spatial[layer], b_spatial[layer], w_branch_a[layer], w_branch_b[layer],
                      w_out[layer])

        hc = modulate(layer_norm(ctx), mod_c[0], mod_c[1])
        kc, vc = context_kv(hc, w_in[layer])

        h = modulate(layer_norm(x), mod_x[0], mod_x[1])
        mix = token_mixer(h, *mix_params, kc, vc, (rows, cols))
        x_mid = post_norm(x, mod_x[2][:, None, :] * mix, ln1_g[layer], ln1_b[layer])
        h2 = modulate(layer_norm(x_mid), mod_x[3], mod_x[4])
        x_new = post_norm(x_mid, mod_x[5][:, None, :] * swiglu(h2, w_ffn_in[layer], w_ffn_out[layer]),
                          ln2_g[layer], ln2_b[layer])

        if layer < DEPTH - 1:
            mix_c = token_mixer(hc, *mix_params, None, None, None)
            ctx_mid = post_norm(ctx, mod_c[2] * mix_c, ln1_g[layer], ln1_b[layer])
            h2c = modulate(layer_norm(ctx_mid), mod_c[3], mod_c[4])
            ctx = post_norm(ctx_mid, mod_c[5] * swiglu(h2c, w_ffn_in[layer], w_ffn_out[layer]),
                            ln2_g[layer], ln2_b[layer])
        x = x_new
    return x


import jax as _jax
import jax.numpy as _jnp

TWIN_FORMAT = 'train_step'
FWD_PARAMS = ['x', 'c', 'ctx', 'c_ctx', 'w_ada', 'b_ada', 'w_in', 'attn_sink', 'gmlp_ln_g', 'gmlp_ln_b', 'w_spatial', 'b_spatial', 'w_branch_a', 'w_branch_b', 'w_out', 'ln1_g', 'ln1_b', 'w_ffn_in', 'w_ffn_out', 'ln2_g', 'ln2_b']
TWIN_WEIGHTS = ['c_ctx', 'w_ada', 'b_ada', 'w_in', 'attn_sink', 'gmlp_ln_g', 'gmlp_ln_b', 'w_spatial', 'b_spatial', 'w_branch_a', 'w_branch_b', 'w_out', 'ln1_g', 'ln1_b', 'w_ffn_in', 'w_ffn_out', 'ln2_g', 'ln2_b']
TWIN_DIFF_INPUT = 'x'
TWIN_INPUTS = ['x', 'c', 'ctx', 'c_ctx', 'w_ada', 'b_ada', 'w_in', 'attn_sink', 'gmlp_ln_g', 'gmlp_ln_b', 'w_spatial', 'b_spatial', 'w_branch_a', 'w_branch_b', 'w_out', 'ln1_g', 'ln1_b', 'w_ffn_in', 'w_ffn_out', 'ln2_g', 'ln2_b', 'loss_target', 'm_c_ctx', 'm_w_ada', 'm_b_ada', 'm_w_in', 'm_attn_sink', 'm_gmlp_ln_g', 'm_gmlp_ln_b', 'm_w_spatial', 'm_b_spatial', 'm_w_branch_a', 'm_w_branch_b', 'm_w_out', 'm_ln1_g', 'm_ln1_b', 'm_w_ffn_in', 'm_w_ffn_out', 'm_ln2_g', 'm_ln2_b', 'v_c_ctx', 'v_w_ada', 'v_b_ada', 'v_w_in', 'v_attn_sink', 'v_gmlp_ln_g', 'v_gmlp_ln_b', 'v_w_spatial', 'v_b_spatial', 'v_w_branch_a', 'v_w_branch_b', 'v_w_out', 'v_ln1_g', 'v_ln1_b', 'v_w_ffn_in', 'v_w_ffn_out', 'v_ln2_g', 'v_ln2_b']
TWIN_OUTPUTS = ['loss', 'grad_x', 'grad_c_ctx', 'grad_w_ada', 'grad_b_ada', 'grad_w_in', 'grad_attn_sink', 'grad_gmlp_ln_g', 'grad_gmlp_ln_b', 'grad_w_spatial', 'grad_b_spatial', 'grad_w_branch_a', 'grad_w_branch_b', 'grad_w_out', 'grad_ln1_g', 'grad_ln1_b', 'grad_w_ffn_in', 'grad_w_ffn_out', 'grad_ln2_g', 'grad_ln2_b', 'delta_c_ctx', 'delta_w_ada', 'delta_b_ada', 'delta_w_in', 'delta_attn_sink', 'delta_gmlp_ln_g', 'delta_gmlp_ln_b', 'delta_w_spatial', 'delta_b_spatial', 'delta_w_branch_a', 'delta_w_branch_b', 'delta_w_out', 'delta_ln1_g', 'delta_ln1_b', 'delta_w_ffn_in', 'delta_w_ffn_out', 'delta_ln2_g', 'delta_ln2_b', 'new_m_c_ctx', 'new_m_w_ada', 'new_m_b_ada', 'new_m_w_in', 'new_m_attn_sink', 'new_m_gmlp_ln_g', 'new_m_gmlp_ln_b', 'new_m_w_spatial', 'new_m_b_spatial', 'new_m_w_branch_a', 'new_m_w_branch_b', 'new_m_w_out', 'new_m_ln1_g', 'new_m_ln1_b', 'new_m_w_ffn_in', 'new_m_w_ffn_out', 'new_m_ln2_g', 'new_m_ln2_b', 'new_v_c_ctx', 'new_v_w_ada', 'new_v_b_ada', 'new_v_w_in', 'new_v_attn_sink', 'new_v_gmlp_ln_g', 'new_v_gmlp_ln_b', 'new_v_w_spatial', 'new_v_b_spatial', 'new_v_w_branch_a', 'new_v_w_branch_b', 'new_v_w_out', 'new_v_ln1_g', 'new_v_ln1_b', 'new_v_w_ffn_in', 'new_v_w_ffn_out', 'new_v_ln2_g', 'new_v_ln2_b']
TWIN_LEAF_KINDS = {'loss': 'loss', 'grad_x': 'grad_x', 'grad_c_ctx': 'grad_w', 'grad_w_ada': 'grad_w', 'grad_b_ada': 'grad_w', 'grad_w_in': 'grad_w', 'grad_attn_sink': 'grad_w', 'grad_gmlp_ln_g': 'grad_w', 'grad_gmlp_ln_b': 'grad_w', 'grad_w_spatial': 'grad_w', 'grad_b_spatial': 'grad_w', 'grad_w_branch_a': 'grad_w', 'grad_w_branch_b': 'grad_w', 'grad_w_out': 'grad_w', 'grad_ln1_g': 'grad_w', 'grad_ln1_b': 'grad_w', 'grad_w_ffn_in': 'grad_w', 'grad_w_ffn_out': 'grad_w', 'grad_ln2_g': 'grad_w', 'grad_ln2_b': 'grad_w', 'delta_c_ctx': 'delta_w', 'delta_w_ada': 'delta_w', 'delta_b_ada': 'delta_w', 'delta_w_in': 'delta_w', 'delta_attn_sink': 'delta_w', 'delta_gmlp_ln_g': 'delta_w', 'delta_gmlp_ln_b': 'delta_w', 'delta_w_spatial': 'delta_w', 'delta_b_spatial': 'delta_w', 'delta_w_branch_a': 'delta_w', 'delta_w_branch_b': 'delta_w', 'delta_w_out': 'delta_w', 'delta_ln1_g': 'delta_w', 'delta_ln1_b': 'delta_w', 'delta_w_ffn_in': 'delta_w', 'delta_w_ffn_out': 'delta_w', 'delta_ln2_g': 'delta_w', 'delta_ln2_b': 'delta_w', 'new_m_c_ctx': 'new_m', 'new_m_w_ada': 'new_m', 'new_m_b_ada': 'new_m', 'new_m_w_in': 'new_m', 'new_m_attn_sink': 'new_m', 'new_m_gmlp_ln_g': 'new_m', 'new_m_gmlp_ln_b': 'new_m', 'new_m_w_spatial': 'new_m', 'new_m_b_spatial': 'new_m', 'new_m_w_branch_a': 'new_m', 'new_m_w_branch_b': 'new_m', 'new_m_w_out': 'new_m', 'new_m_ln1_g': 'new_m', 'new_m_ln1_b': 'new_m', 'new_m_w_ffn_in': 'new_m', 'new_m_w_ffn_out': 'new_m', 'new_m_ln2_g': 'new_m', 'new_m_ln2_b': 'new_m', 'new_v_c_ctx': 'new_v', 'new_v_w_ada': 'new_v', 'new_v_b_ada': 'new_v', 'new_v_w_in': 'new_v', 'new_v_attn_sink': 'new_v', 'new_v_gmlp_ln_g': 'new_v', 'new_v_gmlp_ln_b': 'new_v', 'new_v_w_spatial': 'new_v', 'new_v_b_spatial': 'new_v', 'new_v_w_branch_a': 'new_v', 'new_v_w_branch_b': 'new_v', 'new_v_w_out': 'new_v', 'new_v_ln1_g': 'new_v', 'new_v_ln1_b': 'new_v', 'new_v_w_ffn_in': 'new_v', 'new_v_w_ffn_out': 'new_v', 'new_v_ln2_g': 'new_v', 'new_v_ln2_b': 'new_v'}


def _forward(args):
    return _fwd_reference(*[args[k] for k in FWD_PARAMS])


def _output_shape():
    def fwd():
        inp = _fwd_setup_inputs(0)
        return _fwd_reference(*[inp[k] for k in FWD_PARAMS])
    out = _jax.eval_shape(fwd)
    return out.shape, out.dtype

N_MICROBATCH = 1
ADAM_LR = 0.001
ADAM_B1 = 0.9
ADAM_B2 = 0.999
ADAM_EPS = 1e-08
ADAM_WD = 0.01
ADAM_STEP = 10
PER_EXAMPLE_BATCH_AXIS = {'x': 0, 'c': 0, 'ctx': 0, 'loss_target': 0}
SHARED_INPUTS = []
_WEIGHT_DTYPES = {'c_ctx': _jnp.float32, 'w_ada': _jnp.float32, 'b_ada': _jnp.float32, 'w_in': _jnp.float32, 'attn_sink': _jnp.float32, 'gmlp_ln_g': _jnp.float32, 'gmlp_ln_b': _jnp.float32, 'w_spatial': _jnp.float32, 'b_spatial': _jnp.float32, 'w_branch_a': _jnp.float32, 'w_branch_b': _jnp.float32, 'w_out': _jnp.float32, 'ln1_g': _jnp.float32, 'ln1_b': _jnp.float32, 'w_ffn_in': _jnp.float32, 'w_ffn_out': _jnp.float32, 'ln2_g': _jnp.float32, 'ln2_b': _jnp.float32}
MOMENT_SCALE = {'c_ctx': 4.687359e-03, 'w_ada': 3.427619e-02, 'b_ada': 6.290545e-02, 'w_in': 1.571715e-02, 'attn_sink': 2.019512e-04, 'gmlp_ln_g': 2.308218e-02, 'gmlp_ln_b': 2.191468e-02, 'w_spatial': 1.483250e-02, 'b_spatial': 1.508949e-02, 'w_branch_a': 6.716754e-03, 'w_branch_b': 2.211657e-02, 'w_out': 3.873289e-02, 'ln1_g': 2.207508e+00, 'ln1_b': 9.791911e-01, 'w_ffn_in': 1.654848e-02, 'w_ffn_out': 4.569711e-02, 'ln2_g': 6.398484e+01, 'ln2_b': 1.663535e+00}


def _to_microbatches(a, axis):
    t = _jnp.moveaxis(a, axis, 0)
    t = t.reshape((N_MICROBATCH, t.shape[0] // N_MICROBATCH) + t.shape[1:])
    return _jnp.moveaxis(t, 1, axis + 1)


def setup_inputs(seed: int = 0) -> dict:
    inp = _fwd_setup_inputs(seed)
    key = _jax.random.fold_in(_jax.random.key(seed), 7919)
    shape, _ = _output_shape()
    out = dict(inp)
    out["loss_target"] = _jax.random.normal(_jax.random.fold_in(key, 0), shape, _jnp.float32)
    for i, name in enumerate(TWIN_WEIGHTS):
        w = inp[name].astype(_jnp.float32)
        if MOMENT_SCALE is None:
            s = _jnp.sqrt(_jnp.mean(_jnp.square(w)) + 1e-30)
        else:
            s = MOMENT_SCALE[name]
        km, kv = _jax.random.split(_jax.random.fold_in(key, i + 1))
        out[name] = w
        out["m_" + name] = s * _jax.random.normal(km, w.shape, _jnp.float32)
        out["v_" + name] = (s * s) * _jax.random.uniform(kv, w.shape, _jnp.float32, 0.5, 1.5)
    if N_MICROBATCH > 1:
        for name, axis in PER_EXAMPLE_BATCH_AXIS.items():
            out[name] = _to_microbatches(out[name], axis)
    return {'x': out['x'], 'c': out['c'], 'ctx': out['ctx'], 'c_ctx': out['c_ctx'], 'w_ada': out['w_ada'], 'b_ada': out['b_ada'], 'w_in': out['w_in'], 'attn_sink': out['attn_sink'], 'gmlp_ln_g': out['gmlp_ln_g'], 'gmlp_ln_b': out['gmlp_ln_b'], 'w_spatial': out['w_spatial'], 'b_spatial': out['b_spatial'], 'w_branch_a': out['w_branch_a'], 'w_branch_b': out['w_branch_b'], 'w_out': out['w_out'], 'ln1_g': out['ln1_g'], 'ln1_b': out['ln1_b'], 'w_ffn_in': out['w_ffn_in'], 'w_ffn_out': out['w_ffn_out'], 'ln2_g': out['ln2_g'], 'ln2_b': out['ln2_b'], 'loss_target': out['loss_target'], 'm_c_ctx': out['m_c_ctx'], 'm_w_ada': out['m_w_ada'], 'm_b_ada': out['m_b_ada'], 'm_w_in': out['m_w_in'], 'm_attn_sink': out['m_attn_sink'], 'm_gmlp_ln_g': out['m_gmlp_ln_g'], 'm_gmlp_ln_b': out['m_gmlp_ln_b'], 'm_w_spatial': out['m_w_spatial'], 'm_b_spatial': out['m_b_spatial'], 'm_w_branch_a': out['m_w_branch_a'], 'm_w_branch_b': out['m_w_branch_b'], 'm_w_out': out['m_w_out'], 'm_ln1_g': out['m_ln1_g'], 'm_ln1_b': out['m_ln1_b'], 'm_w_ffn_in': out['m_w_ffn_in'], 'm_w_ffn_out': out['m_w_ffn_out'], 'm_ln2_g': out['m_ln2_g'], 'm_ln2_b': out['m_ln2_b'], 'v_c_ctx': out['v_c_ctx'], 'v_w_ada': out['v_w_ada'], 'v_b_ada': out['v_b_ada'], 'v_w_in': out['v_w_in'], 'v_attn_sink': out['v_attn_sink'], 'v_gmlp_ln_g': out['v_gmlp_ln_g'], 'v_gmlp_ln_b': out['v_gmlp_ln_b'], 'v_w_spatial': out['v_w_spatial'], 'v_b_spatial': out['v_b_spatial'], 'v_w_branch_a': out['v_w_branch_a'], 'v_w_branch_b': out['v_w_branch_b'], 'v_w_out': out['v_w_out'], 'v_ln1_g': out['v_ln1_g'], 'v_ln1_b': out['v_ln1_b'], 'v_w_ffn_in': out['v_w_ffn_in'], 'v_w_ffn_out': out['v_w_ffn_out'], 'v_ln2_g': out['v_ln2_g'], 'v_ln2_b': out['v_ln2_b']}


def _loss(weights, diff, rest, loss_target):
    with _jax.named_scope("forward"):
        args = {**rest, TWIN_DIFF_INPUT: diff, **{k: w.astype(_WEIGHT_DTYPES[k]) for k, w in weights.items()}}
        y = _forward(args)
    with _jax.named_scope("loss_head"):
        err = _jnp.square(y.astype(_jnp.float32) - loss_target)
        return 0.5 * _jnp.sum(_jnp.mean(err, axis=-1)) if err.ndim else 0.5 * err


def _adamw(w, g, m, v):
    m = ADAM_B1 * m + (1.0 - ADAM_B1) * g
    v = ADAM_B2 * v + (1.0 - ADAM_B2) * _jnp.square(g)
    m_hat = m / (1.0 - ADAM_B1 ** ADAM_STEP)
    v_hat = v / (1.0 - ADAM_B2 ** ADAM_STEP)
    delta = -ADAM_LR * (m_hat / (_jnp.sqrt(v_hat) + ADAM_EPS) + ADAM_WD * w)
    return delta, m, v


def reference(x, c, ctx, c_ctx, w_ada, b_ada, w_in, attn_sink, gmlp_ln_g, gmlp_ln_b, w_spatial, b_spatial, w_branch_a, w_branch_b, w_out, ln1_g, ln1_b, w_ffn_in, w_ffn_out, ln2_g, ln2_b, loss_target, m_c_ctx, m_w_ada, m_b_ada, m_w_in, m_attn_sink, m_gmlp_ln_g, m_gmlp_ln_b, m_w_spatial, m_b_spatial, m_w_branch_a, m_w_branch_b, m_w_out, m_ln1_g, m_ln1_b, m_w_ffn_in, m_w_ffn_out, m_ln2_g, m_ln2_b, v_c_ctx, v_w_ada, v_b_ada, v_w_in, v_attn_sink, v_gmlp_ln_g, v_gmlp_ln_b, v_w_spatial, v_b_spatial, v_w_branch_a, v_w_branch_b, v_w_out, v_ln1_g, v_ln1_b, v_w_ffn_in, v_w_ffn_out, v_ln2_g, v_ln2_b):
    given = dict(x=x, c=c, ctx=ctx, c_ctx=c_ctx, w_ada=w_ada, b_ada=b_ada, w_in=w_in, attn_sink=attn_sink, gmlp_ln_g=gmlp_ln_g, gmlp_ln_b=gmlp_ln_b, w_spatial=w_spatial, b_spatial=b_spatial, w_branch_a=w_branch_a, w_branch_b=w_branch_b, w_out=w_out, ln1_g=ln1_g, ln1_b=ln1_b, w_ffn_in=w_ffn_in, w_ffn_out=w_ffn_out, ln2_g=ln2_g, ln2_b=ln2_b, loss_target=loss_target, m_c_ctx=m_c_ctx, m_w_ada=m_w_ada, m_b_ada=m_b_ada, m_w_in=m_w_in, m_attn_sink=m_attn_sink, m_gmlp_ln_g=m_gmlp_ln_g, m_gmlp_ln_b=m_gmlp_ln_b, m_w_spatial=m_w_spatial, m_b_spatial=m_b_spatial, m_w_branch_a=m_w_branch_a, m_w_branch_b=m_w_branch_b, m_w_out=m_w_out, m_ln1_g=m_ln1_g, m_ln1_b=m_ln1_b, m_w_ffn_in=m_w_ffn_in, m_w_ffn_out=m_w_ffn_out, m_ln2_g=m_ln2_g, m_ln2_b=m_ln2_b, v_c_ctx=v_c_ctx, v_w_ada=v_w_ada, v_b_ada=v_b_ada, v_w_in=v_w_in, v_attn_sink=v_attn_sink, v_gmlp_ln_g=v_gmlp_ln_g, v_gmlp_ln_b=v_gmlp_ln_b, v_w_spatial=v_w_spatial, v_b_spatial=v_b_spatial, v_w_branch_a=v_w_branch_a, v_w_branch_b=v_w_branch_b, v_w_out=v_w_out, v_ln1_g=v_ln1_g, v_ln1_b=v_ln1_b, v_w_ffn_in=v_w_ffn_in, v_w_ffn_out=v_w_ffn_out, v_ln2_g=v_ln2_g, v_ln2_b=v_ln2_b)
    weights = {n: given[n] for n in TWIN_WEIGHTS}
    shared = {n: given[n] for n in SHARED_INPUTS}
    per_example = {n: given[n] for n in ['x', 'c', 'ctx']}
    grad_fn = _jax.value_and_grad(_loss, argnums=(0, 1))

    def one_microbatch(ex, loss_target):
        ex = dict(ex)
        diff = ex.pop(TWIN_DIFF_INPUT)
        return grad_fn(weights, diff, {**shared, **ex}, loss_target)

    if N_MICROBATCH == 1:
        loss, (grad_w, grad_x) = one_microbatch(per_example, given["loss_target"])
    else:
        def body(carry, xs):
            loss_sum, grad_sum = carry
            l_k, (gw_k, gx_k) = one_microbatch(xs[0], xs[1])
            with _jax.named_scope("update"):
                return (loss_sum + l_k, _jax.tree.map(_jnp.add, grad_sum, gw_k)), gx_k

        init = (_jnp.zeros((), _jnp.float32), _jax.tree.map(_jnp.zeros_like, weights))
        (loss, grad_w), grad_x = _jax.lax.scan(body, init, (per_example, given["loss_target"]))
    with _jax.named_scope("update"):
        delta_w, new_m, new_v = {}, {}, {}
        for n in TWIN_WEIGHTS:
            delta_w[n], new_m[n], new_v[n] = _adamw(weights[n], grad_w[n], given["m_" + n], given["v_" + n])
    return (loss, grad_x, *[grad_w[n] for n in TWIN_WEIGHTS], *[delta_w[n] for n in TWIN_WEIGHTS],
            *[new_m[n] for n in TWIN_WEIGHTS], *[new_v[n] for n in TWIN_WEIGHTS])
```

```python
import functools
import math

import jax
import jax.numpy as jnp
from jax import lax
from jax.experimental import pallas as pl
from jax.experimental.pallas import tpu as pltpu

F32 = jnp.float32
BF16 = jnp.bfloat16

D = 1024
HEAD = 64
N_KV = 2
GROUP = 4
Q_W = 512
KV_W = 128
G_W = 512
BLK = 128
N_GRP = 8
GRP_D = 64
FH = 2816
IN_W = 3840
GRID_W = 64
ROPE_BASE = 10000.0
LN_EPS = 1e-5
NEG = -1e30
ALPHA = (2 * 1) ** 0.25
SCALE = HEAD ** -0.5
GELU_K = math.sqrt(2.0 / math.pi)
GELU_A = 0.044715
ADAM_LR = 0.001
ADAM_B1 = 0.9
ADAM_B2 = 0.999
ADAM_EPS = 1e-08
ADAM_WD = 0.01
ADAM_STEP = 10
N_DEV = 8
LANES = 128
VMEM_LIMIT = 56 * 1024 * 1024
MESH = pl.DeviceIdType.MESH


def _cp(*sem):
    return pltpu.CompilerParams(dimension_semantics=sem, vmem_limit_bytes=VMEM_LIMIT)


def _resident(shape):
    return pl.BlockSpec(shape, lambda *_: (0,) * len(shape), pipeline_mode=pl.Buffered(1))


def _rows(tm, width):
    return pl.BlockSpec((tm, width), lambda i: (i, 0))


def _acc(shape):
    return pl.BlockSpec(shape, lambda *_: (0,) * len(shape))


def _dot(a, b):
    return jnp.dot(a, b, preferred_element_type=F32)


def _dot_nt(a, b):
    return lax.dot_general(a, b, (((1,), (1,)), ((), ())), preferred_element_type=F32)


def _dot_tn(a, b):
    return lax.dot_general(a, b, (((0,), (0,)), ((), ())), preferred_element_type=F32)


def _ln(x):
    mu = jnp.mean(x, axis=-1, keepdims=True)
    xc = x - mu
    var = jnp.mean(xc * xc, axis=-1, keepdims=True)
    rstd = lax.rsqrt(var + LN_EPS)
    return xc * rstd, rstd


def _ln_bwd(dxhat, xhat, rstd):
    return (dxhat - jnp.mean(dxhat, axis=-1, keepdims=True)
            - xhat * jnp.mean(dxhat * xhat, axis=-1, keepdims=True)) * rstd


def _sig(x):
    return 1.0 / (1.0 + jnp.exp(-x))


def _gelu(x):
    t = jnp.tanh(GELU_K * (x + GELU_A * x * x * x))
    return 0.5 * x * (1.0 + t), t


def _gelu_grad(x, t):
    return 0.5 * (1.0 + t) + 0.5 * x * (1.0 - t * t) * GELU_K * (1.0 + 3.0 * GELU_A * x * x)


def _colsum(v):
    return jnp.sum(v, axis=0, keepdims=True)


def _partner(x):
    w = x.shape[1]
    lane = lax.broadcasted_iota(jnp.int32, x.shape, 1)
    return jnp.where((lane & 31) < 16, pltpu.roll(x, w - 16, 1), pltpu.roll(x, 16, 1))


def _rope(x, cos, sin):
    return x * cos + _partner(x) * sin


def _unrope(g, cos, sin):
    return g * cos + _partner(g * sin)


def _rope_tables(seq):
    inv = ROPE_BASE ** (-jnp.arange(HEAD // 4, dtype=F32) / (HEAD // 4))
    pos = jnp.arange(seq, dtype=jnp.int32)
    ar = (pos // GRID_W).astype(F32)[:, None] * inv
    ac = (pos % GRID_W).astype(F32)[:, None] * inv
    cos = jnp.concatenate([jnp.cos(ar), jnp.cos(ar), jnp.cos(ac), jnp.cos(ac)], axis=-1)
    sin = jnp.concatenate([-jnp.sin(ar), jnp.sin(ar), -jnp.sin(ac), jnp.sin(ac)], axis=-1)
    return jnp.tile(cos, (1, LANES // HEAD)), jnp.tile(sin, (1, LANES // HEAD))


def _ctx_fwd(ctx, modc, w_kv):
    n_ctx = ctx.shape[0]

    def body(ctx_ref, mod_ref, w_ref, hc_ref, kvc_ref):
        xhat, _ = _ln(ctx_ref[...])
        hc = (xhat * (1.0 + mod_ref[1:2, :]) + mod_ref[0:1, :]).astype(BF16)
        hc_ref[...] = hc
        kvc_ref[...] = _dot(hc, w_ref[...]).astype(BF16)

    return pl.pallas_call(
        body, name="ctx_fwd", grid=(1,),
        in_specs=[_acc((n_ctx, D)), _acc((8, D)), _acc((D, 2 * KV_W))],
        out_specs=[_acc((n_ctx, D)), _acc((n_ctx, 2 * KV_W))],
        out_shape=[jax.ShapeDtypeStruct((n_ctx, D), BF16), jax.ShapeDtypeStruct((n_ctx, 2 * KV_W), BF16)],
        compiler_params=_cp("arbitrary"),
    )(ctx, modc, w_kv)


def _proj_fwd(x, modx, w_in, cos, sin, tm):
    seq = x.shape[0]

    def body(x_ref, mod_ref, w_ref, cos_ref, sin_ref, h_ref, q_ref, kv_ref, uv_ref, gab_ref):
        xhat, _ = _ln(x_ref[...])
        h = (xhat * (1.0 + mod_ref[1:2, :]) + mod_ref[0:1, :]).astype(BF16)
        h_ref[...] = h
        cos1, sin1 = cos_ref[...], sin_ref[...]
        cos2 = jnp.concatenate([cos1, cos1], axis=1)
        sin2 = jnp.concatenate([sin1, sin1], axis=1)
        for j in range(Q_W // 256):
            t = _dot(h, w_ref[:, 256 * j:256 * (j + 1)])
            q_ref[:, 256 * j:256 * (j + 1)] = _rope(t, cos2, sin2).astype(BF16)
        t = _dot(h, w_ref[:, Q_W:Q_W + 2 * KV_W])
        kv_ref[:, :KV_W] = _rope(t[:, :KV_W], cos1, sin1).astype(BF16)
        kv_ref[:, KV_W:] = t[:, KV_W:].astype(BF16)
        o = Q_W + 2 * KV_W
        for j in range(2):
            uv_ref[:, G_W * j:G_W * (j + 1)] = _dot(h, w_ref[:, o + G_W * j:o + G_W * (j + 1)])
        o += 2 * G_W
        for j in range(4):
            gab_ref[:, 512 * j:512 * (j + 1)] = _dot(h, w_ref[:, o + 512 * j:o + 512 * (j + 1)])

    return pl.pallas_call(
        body, name="proj_fwd", grid=(seq // tm,),
        in_specs=[_rows(tm, D), _acc((8, D)), _resident((D, IN_W)), _rows(tm, LANES), _rows(tm, LANES)],
        out_specs=[_rows(tm, D), _rows(tm, Q_W), _rows(tm, 2 * KV_W), _rows(tm, 2 * G_W), _rows(tm, 2 * D)],
        out_shape=[jax.ShapeDtypeStruct((seq, D), BF16), jax.ShapeDtypeStruct((seq, Q_W), BF16),
                   jax.ShapeDtypeStruct((seq, 2 * KV_W), BF16), jax.ShapeDtypeStruct((seq, 2 * G_W), F32),
                   jax.ShapeDtypeStruct((seq, 2 * D), F32)],
        compiler_params=_cp("arbitrary"),
    )(x, modx, w_in, cos, sin)


def _attn_pieces(q_ref, kvp_ref, kvm_ref, kvn_ref, kvc_ref, sink_ref, hk, n, nb):
    q = q_ref[...]
    q4 = jnp.concatenate([q[:, (hk * GROUP + g) * HEAD:(hk * GROUP + g + 1) * HEAD] for g in range(GROUP)], axis=0)
    ks = [r[:, hk * HEAD:(hk + 1) * HEAD] for r in (kvc_ref, kvp_ref, kvm_ref, kvn_ref)]
    vs = [r[:, KV_W + hk * HEAD:KV_W + (hk + 1) * HEAD] for r in (kvc_ref, kvp_ref, kvm_ref, kvn_ref)]
    rows = GROUP * BLK
    qi = lax.broadcasted_iota(jnp.int32, (rows, BLK), 0) & (BLK - 1)
    kj = lax.broadcasted_iota(jnp.int32, (rows, BLK), 1)
    s = [_dot_nt(q4, k) * SCALE for k in ks]
    s[1] = jnp.where((kj >= qi) & (n > 0), s[1], NEG)
    s[3] = jnp.where((kj <= qi) & (n < nb - 1), s[3], NEG)
    rg = lax.broadcasted_iota(jnp.int32, (rows, 1), 0) >> 7
    sink_v = jnp.full((rows, 1), sink_ref[0, hk * GROUP], F32)
    for g in range(1, GROUP):
        sink_v = jnp.where(rg == g, sink_ref[0, hk * GROUP + g], sink_v)
    m = sink_v
    for t in s:
        m = jnp.maximum(m, jnp.max(t, axis=-1, keepdims=True))
    e = [jnp.exp(t - m) for t in s]
    e_sink = jnp.exp(sink_v - m)
    denom = e_sink
    for t in e:
        denom = denom + jnp.sum(t, axis=-1, keepdims=True)
    inv = 1.0 / denom
    p = [t * inv for t in e]
    return q4, ks, vs, p, e_sink * inv


def _kv_specs(nb):
    return [pl.BlockSpec((BLK, 2 * KV_W), lambda n: (jnp.maximum(n - 1, 0), 0)),
            pl.BlockSpec((BLK, 2 * KV_W), lambda n: (n, 0)),
            pl.BlockSpec((BLK, 2 * KV_W), lambda n: (jnp.minimum(n + 1, nb - 1), 0))]


def _attn_fwd(q, kv, kvc, sink):
    seq = q.shape[0]
    nb = seq // BLK
    n_ctx = kvc.shape[0]

    def body(q_ref, kvp_ref, kvm_ref, kvn_ref, kvc_ref, sink_ref, o_ref):
        n = pl.program_id(0)
        outs = []
        for hk in range(N_KV):
            _, _, vs, p, _ = _attn_pieces(q_ref, kvp_ref, kvm_ref, kvn_ref, kvc_ref, sink_ref, hk, n, nb)
            o = _dot(p[0].astype(BF16), vs[0])
            for t, v in zip(p[1:], vs[1:]):
                o = o + _dot(t.astype(BF16), v)
            outs += [o[g * BLK:(g + 1) * BLK, :] for g in range(GROUP)]
        o_ref[...] = jnp.concatenate(outs, axis=1).astype(BF16)

    return pl.pallas_call(
        body, name="attn_fwd", grid=(nb,),
        in_specs=[_rows(BLK, Q_W)] + _kv_specs(nb) + [_acc((n_ctx, 2 * KV_W)), pl.BlockSpec(memory_space=pltpu.SMEM)],
        out_specs=_rows(BLK, Q_W),
        out_shape=jax.ShapeDtypeStruct((seq, Q_W), BF16),
        compiler_params=_cp("arbitrary"),
    )(q, kv, kv, kv, kvc, sink)


def _gmlp_chunk(u, vb, gp_ref, ws_ref, bias_ref):
    gu, tu = _gelu(u)
    gv, tv = _gelu(vb)
    vhat, rstd = _ln(gv)
    vn = (vhat * gp_ref[0:1, :] + gp_ref[1:2, :]).astype(BF16)
    r = _dot(ws_ref[...], vn)
    grp = lax.broadcasted_iota(jnp.int32, (BLK, G_W), 1) >> 6
    s = bias_ref[...]
    for g in range(N_GRP):
        s = s + jnp.where(grp == g, r[g * BLK:(g + 1) * BLK, :], 0.0)
    return gu, tu, tv, vhat, rstd, vn, s, grp


def _mix_fwd(uv, gab, ya, gp, ws_stack, bias_full, w_a, w_b, w_o, tm):
    seq = uv.shape[0]

    def body(uv_ref, gab_ref, ya_ref, gp_ref, ws_ref, bias_ref, wa_ref, wb_ref, wo_ref,
             a_ref, b_ref, mix_ref, merged_ref, yb_ref):
        for c in range(tm // BLK):
            rs = slice(c * BLK, (c + 1) * BLK)
            gu, _, _, _, _, _, s, _ = _gmlp_chunk(uv_ref[rs, :G_W], uv_ref[rs, G_W:], gp_ref, ws_ref, bias_ref)
            yb_ref[rs, :] = (gu * s).astype(BF16)
        a = _dot(ya_ref[...], wa_ref[...])
        b = _dot(yb_ref[...], wb_ref[...])
        a_ref[...] = a
        b_ref[...] = b
        merged = (_sig(gab_ref[:, :D]) * a + _sig(gab_ref[:, D:]) * b).astype(BF16)
        merged_ref[...] = merged
        mix_ref[...] = _dot(merged, wo_ref[...])

    return pl.pallas_call(
        body, name="mix_fwd", grid=(seq // tm,),
        in_specs=[_rows(tm, 2 * G_W), _rows(tm, 2 * D), _rows(tm, Q_W), _acc((8, G_W)),
                  _resident((N_GRP * BLK, BLK)), _acc((BLK, G_W)),
                  _resident((Q_W, D)), _resident((G_W, D)), _resident((D, D))],
        out_specs=[_rows(tm, D), _rows(tm, D), _rows(tm, D), _rows(tm, D), _rows(tm, G_W)],
        out_shape=[jax.ShapeDtypeStruct((seq, D), F32), jax.ShapeDtypeStruct((seq, D), F32),
                   jax.ShapeDtypeStruct((seq, D), F32), jax.ShapeDtypeStruct((seq, D), BF16),
                   jax.ShapeDtypeStruct((seq, G_W), BF16)],
        compiler_params=_cp("arbitrary"),
    )(uv, gab, ya, gp, ws_stack, bias_full, w_a, w_b, w_o)


FFN_CHUNK = 256


def _mid_recompute(x_ref, mix_ref, vec_ref):
    r1 = ALPHA * x_ref[...] + vec_ref[0:1, :] * mix_ref[...]
    xh1, rstd1 = _ln(r1)
    xmid = xh1 * vec_ref[1:2, :] + vec_ref[2:3, :]
    xh2, rstd2 = _ln(xmid)
    return xh1, rstd1, xmid, xh2, rstd2


def _ffn_fwd(x, mix, tgt, vec, w_fi, w_fo, tm):
    seq = x.shape[0]

    def body(x_ref, mix_ref, tgt_ref, vec_ref, wi_ref, wo_ref, gu_ref, act_ref, h2_ref, dr2_ref, st_ref):
        @pl.when(pl.program_id(0) == 0)
        def _():
            st_ref[...] = jnp.zeros_like(st_ref)

        _, _, xmid, xh2, _ = _mid_recompute(x_ref, mix_ref, vec_ref)
        h2 = (xh2 * (1.0 + vec_ref[4:5, :]) + vec_ref[3:4, :]).astype(BF16)
        h2_ref[...] = h2
        f = jnp.zeros((tm, D), F32)
        for j in range(FH // FFN_CHUNK):
            cs = slice(j * FFN_CHUNK, (j + 1) * FFN_CHUNK)
            cu = slice(FH + j * FFN_CHUNK, FH + (j + 1) * FFN_CHUNK)
            g = _dot(h2, wi_ref[:, cs])
            u = _dot(h2, wi_ref[:, cu])
            gu_ref[:, cs] = g
            gu_ref[:, cu] = u
            a = (g * _sig(g) * u).astype(BF16)
            act_ref[:, cs] = a
            f = f + _dot(a, wo_ref[cs, :])
        r2 = ALPHA * xmid + vec_ref[5:6, :] * f
        yh, rstd = _ln(r2)
        y = yh * vec_ref[6:7, :] + vec_ref[7:8, :]
        err = y - tgt_ref[...]
        dy = err / D
        dr2 = _ln_bwd(dy * vec_ref[6:7, :], yh, rstd)
        dr2_ref[...] = dr2
        st_ref[0:1, :] += _colsum(err * err)
        st_ref[1:2, :] += _colsum(dy * yh)
        st_ref[2:3, :] += _colsum(dy)
        st_ref[3:4, :] += _colsum(dr2 * f)

    return pl.pallas_call(
        body, name="ffn_fwd", grid=(seq // tm,),
        in_specs=[_rows(tm, D), _rows(tm, D), _rows(tm, D), _acc((8, D)), _resident((D, 2 * FH)), _resident((FH, D))],
        out_specs=[_rows(tm, 2 * FH), _rows(tm, FH), _rows(tm, D), _rows(tm, D), _acc((8, D))],
        out_shape=[jax.ShapeDtypeStruct((seq, 2 * FH), F32), jax.ShapeDtypeStruct((seq, FH), BF16),
                   jax.ShapeDtypeStruct((seq, D), BF16), jax.ShapeDtypeStruct((seq, D), F32),
                   jax.ShapeDtypeStruct((8, D), F32)],
        compiler_params=_cp("arbitrary"),
    )(x, mix, tgt, vec, w_fi, w_fo)


def _ffn_bwd(dr2, gu, x, mix, vec, w_fi, w_fo, tm):
    seq = x.shape[0]

    def body(dr2_ref, gu_ref, x_ref, mix_ref, vec_ref, wi_ref, wo_ref, dff_ref, df_ref, dr1_ref, st_ref):
        @pl.when(pl.program_id(0) == 0)
        def _():
            st_ref[...] = jnp.zeros_like(st_ref)

        dr2 = dr2_ref[...]
        df = (dr2 * vec_ref[5:6, :]).astype(BF16)
        df_ref[...] = df
        dh2 = jnp.zeros((tm, D), F32)
        for j in range(FH // FFN_CHUNK):
            cs = slice(j * FFN_CHUNK, (j + 1) * FFN_CHUNK)
            cu = slice(FH + j * FFN_CHUNK, FH + (j + 1) * FFN_CHUNK)
            da = _dot_nt(df, wo_ref[cs, :])
            g = gu_ref[:, cs]
            u = gu_ref[:, cu]
            sg = _sig(g)
            dg = (da * u * sg * (1.0 + g * (1.0 - sg))).astype(BF16)
            du = (da * g * sg).astype(BF16)
            dff_ref[:, cs] = dg
            dff_ref[:, cu] = du
            dh2 = dh2 + _dot_nt(dg, wi_ref[:, cs]) + _dot_nt(du, wi_ref[:, cu])
        xh1, rstd1, _, xh2, rstd2 = _mid_recompute(x_ref, mix_ref, vec_ref)
        dxmid = _ln_bwd(dh2 * (1.0 + vec_ref[4:5, :]), xh2, rstd2) + ALPHA * dr2
        dr1 = _ln_bwd(dxmid * vec_ref[1:2, :], xh1, rstd1)
        dr1_ref[...] = dr1
        st_ref[0:1, :] += _colsum(dh2 * xh2)
        st_ref[1:2, :] += _colsum(dh2)
        st_ref[2:3, :] += _colsum(dxmid * xh1)
        st_ref[3:4, :] += _colsum(dxmid)
        st_ref[4:5, :] += _colsum(dr1 * mix_ref[...])

    return pl.pallas_call(
        body, name="ffn_bwd", grid=(seq // tm,),
        in_specs=[_rows(tm, D), _rows(tm, 2 * FH), _rows(tm, D), _rows(tm, D), _acc((8, D)),
                  _resident((D, 2 * FH)), _resident((FH, D))],
        out_specs=[_rows(tm, 2 * FH), _rows(tm, D), _rows(tm, D), _acc((8, D))],
        out_shape=[jax.ShapeDtypeStruct((seq, 2 * FH), BF16), jax.ShapeDtypeStruct((seq, D), BF16),
                   jax.ShapeDtypeStruct((seq, D), F32), jax.ShapeDtypeStruct((8, D), F32)],
        compiler_params=_cp("arbitrary"),
    )(dr2, gu, x, mix, vec, w_fi, w_fo)


def _mix_bwd(dr1, a, b, gab, uv, vec, gp, ws_stack, ws_stack_t, bias_full, w_a, w_b, w_o, tm):
    seq = dr1.shape[0]

    def body(dr1_ref, a_ref, b_ref, gab_ref, uv_ref, vec_ref, gp_ref, ws_ref, wst_ref, bias_ref, wa_ref, wb_ref, wo_ref,
             dmix_ref, da_ref, db_ref, dya_ref, dp_ref, dws_ref, dbs_ref, st_ref):
        @pl.when(pl.program_id(0) == 0)
        def _():
            dws_ref[...] = jnp.zeros_like(dws_ref)
            dbs_ref[...] = jnp.zeros_like(dbs_ref)
            st_ref[...] = jnp.zeros_like(st_ref)

        dmix = (dr1_ref[...] * vec_ref[0:1, :]).astype(BF16)
        dmix_ref[...] = dmix
        dmerged = _dot_nt(dmix, wo_ref[...])
        sa = _sig(gab_ref[:, :D])
        sb = _sig(gab_ref[:, D:])
        da = (dmerged * sa).astype(BF16)
        db = (dmerged * sb).astype(BF16)
        da_ref[...] = da
        db_ref[...] = db
        dp_ref[:, 2 * G_W:2 * G_W + D] = (dmerged * a_ref[...] * sa * (1.0 - sa)).astype(BF16)
        dp_ref[:, 2 * G_W + D:] = (dmerged * b_ref[...] * sb * (1.0 - sb)).astype(BF16)
        dya_ref[...] = _dot_nt(da, wa_ref[...]).astype(BF16)
        dyb = _dot_nt(db, wb_ref[...])
        for c in range(tm // BLK):
            rs = slice(c * BLK, (c + 1) * BLK)
            u = uv_ref[rs, :G_W]
            vb = uv_ref[rs, G_W:]
            gu, tu, tv, vhat, rstd, vn, s, grp = _gmlp_chunk(u, vb, gp_ref, ws_ref, bias_ref)
            dyb_c = dyb[rs, :]
            ds = dyb_c * gu
            du = dyb_c * s * _gelu_grad(u, tu)
            dstack = jnp.concatenate([jnp.where(grp == g, ds, 0.0) for g in range(N_GRP)], axis=0).astype(BF16)
            dvn = _dot(wst_ref[...], dstack)
            dws_ref[...] += _dot_nt(dstack, vn)
            dbs_ref[...] += ds
            st_ref[0:1, :] += _colsum(dvn * vhat)
            st_ref[1:2, :] += _colsum(dvn)
            dgv = _ln_bwd(dvn * gp_ref[0:1, :], vhat, rstd)
            dvb = dgv * _gelu_grad(vb, tv)
            dp_ref[rs, :G_W] = du.astype(BF16)
            dp_ref[rs, G_W:2 * G_W] = dvb.astype(BF16)

    pw = 2 * G_W + 2 * D
    return pl.pallas_call(
        body, name="mix_bwd", grid=(seq // tm,),
        in_specs=[_rows(tm, D), _rows(tm, D), _rows(tm, D), _rows(tm, 2 * D), _rows(tm, 2 * G_W), _acc((8, D)), _acc((8, G_W)),
                  _resident((N_GRP * BLK, BLK)), _resident((BLK, N_GRP * BLK)), _acc((BLK, G_W)),
                  _resident((Q_W, D)), _resident((G_W, D)), _resident((D, D))],
        out_specs=[_rows(tm, D), _rows(tm, D), _rows(tm, D), _rows(tm, Q_W), _rows(tm, pw),
                   _acc((N_GRP * BLK, BLK)), _acc((BLK, G_W)), _acc((8, G_W))],
        out_shape=[jax.ShapeDtypeStruct((seq, D), BF16), jax.ShapeDtypeStruct((seq, D), BF16),
                   jax.ShapeDtypeStruct((seq, D), BF16), jax.ShapeDtypeStruct((seq, Q_W), BF16),
                   jax.ShapeDtypeStruct((seq, pw), BF16), jax.ShapeDtypeStruct((N_GRP * BLK, BLK), F32),
                   jax.ShapeDtypeStruct((BLK, G_W), F32), jax.ShapeDtypeStruct((8, G_W), F32)],
        compiler_params=_cp("arbitrary"),
    )(dr1, a, b, gab, uv, vec, gp, ws_stack, ws_stack_t, bias_full, w_a, w_b, w_o)


def _attn_bwd(q, kv, kvc, sink, dya):
    seq = q.shape[0]
    nb = seq // BLK
    n_ctx = kvc.shape[0]

    def body(q_ref, kvp_ref, kvm_ref, kvn_ref, kvc_ref, sink_ref, do_ref, dq_ref, dkv_ref, dkvc_ref, dsink_ref):
        n = pl.program_id(0)

        @pl.when(n == 0)
        def _():
            dkv_ref[...] = jnp.zeros_like(dkv_ref)
            dkvc_ref[...] = jnp.zeros_like(dkvc_ref)
            dsink_ref[...] = jnp.zeros_like(dsink_ref)

        do = do_ref[...]
        dqs, dks, dvs = [], [], []
        for hk in range(N_KV):
            q4, ks, vs, p, p_sink = _attn_pieces(q_ref, kvp_ref, kvm_ref, kvn_ref, kvc_ref, sink_ref, hk, n, nb)
            do4 = jnp.concatenate([do[:, (hk * GROUP + g) * HEAD:(hk * GROUP + g + 1) * HEAD] for g in range(GROUP)], axis=0)
            dp = [_dot_nt(do4, v) for v in vs]
            delta = jnp.zeros((GROUP * BLK, 1), F32)
            for t, d in zip(p, dp):
                delta = delta + jnp.sum(t * d, axis=-1, keepdims=True)
            ds = [(t * (d - delta) * SCALE).astype(BF16) for t, d in zip(p, dp)]
            dq4 = _dot(ds[0], ks[0])
            for t, k in zip(ds[1:], ks[1:]):
                dq4 = dq4 + _dot(t, k)
            dqs += [dq4[g * BLK:(g + 1) * BLK, :] for g in range(GROUP)]
            dks.append([_dot_tn(t, q4) for t in ds])
            dvs.append([_dot_tn(t.astype(BF16), do4) for t in p])
            ps = p_sink * delta
            for g in range(GROUP):
                h = hk * GROUP + g
                dsink_ref[h:h + 1, :] += jnp.broadcast_to(-jnp.sum(ps[g * BLK:(g + 1) * BLK, :], axis=0, keepdims=True), (1, LANES))
        dq_ref[...] = jnp.concatenate(dqs, axis=1)

        def piece(i):
            return jnp.concatenate([dks[0][i], dks[1][i], dvs[0][i], dvs[1][i]], axis=1)

        dkvc_ref[...] += piece(0)
        starts = (jnp.maximum(n - 1, 0), n, jnp.minimum(n + 1, nb - 1))
        for i, st in enumerate(starts):
            r = pl.ds(pl.multiple_of(st * BLK, BLK), BLK)
            dkv_ref[r, :] += piece(i + 1)

    return pl.pallas_call(
        body, name="attn_bwd", grid=(nb,),
        in_specs=[_rows(BLK, Q_W)] + _kv_specs(nb) + [_acc((n_ctx, 2 * KV_W)), pl.BlockSpec(memory_space=pltpu.SMEM),
                                                      _rows(BLK, Q_W)],
        out_specs=[_rows(BLK, Q_W), _acc((seq, 2 * KV_W)), _acc((n_ctx, 2 * KV_W)), _acc((8, LANES))],
        out_shape=[jax.ShapeDtypeStruct((seq, Q_W), F32), jax.ShapeDtypeStruct((seq, 2 * KV_W), F32),
                   jax.ShapeDtypeStruct((n_ctx, 2 * KV_W), F32), jax.ShapeDtypeStruct((8, LANES), F32)],
        compiler_params=_cp("arbitrary"),
    )(q, kv, kv, kv, kvc, sink, dya)


def _proj_bwd(dq, dkv, dpb, x, dr1, modx, w_in, cos, sin, tm):
    seq = x.shape[0]
    pw = IN_W - Q_W - 2 * KV_W

    def body(dq_ref, dkv_ref, dpb_ref, x_ref, dr1_ref, mod_ref, w_ref, cos_ref, sin_ref, dqkv_ref, gx_ref, st_ref):
        @pl.when(pl.program_id(0) == 0)
        def _():
            st_ref[...] = jnp.zeros_like(st_ref)

        cos1, sin1 = cos_ref[...], sin_ref[...]
        cos2 = jnp.concatenate([cos1, cos1], axis=1)
        sin2 = jnp.concatenate([sin1, sin1], axis=1)
        for j in range(Q_W // 256):
            cs = slice(256 * j, 256 * (j + 1))
            dqkv_ref[:, cs] = _unrope(dq_ref[:, cs], cos2, sin2).astype(BF16)
        dqkv_ref[:, Q_W:Q_W + KV_W] = _unrope(dkv_ref[:, :KV_W], cos1, sin1).astype(BF16)
        dqkv_ref[:, Q_W + KV_W:] = dkv_ref[:, KV_W:].astype(BF16)
        o = Q_W + 2 * KV_W
        dh = _dot_nt(dqkv_ref[...], w_ref[:, :o]) + _dot_nt(dpb_ref[...], w_ref[:, o:])
        xhat, rstd = _ln(x_ref[...])
        st_ref[0:1, :] += _colsum(dh)
        st_ref[1:2, :] += _colsum(dh * xhat)
        gx_ref[...] = _ln_bwd(dh * (1.0 + mod_ref[1:2, :]), xhat, rstd) + ALPHA * dr1_ref[...]

    return pl.pallas_call(
        body, name="proj_bwd", grid=(seq // tm,),
        in_specs=[_rows(tm, Q_W), _rows(tm, 2 * KV_W), _rows(tm, pw), _rows(tm, D), _rows(tm, D), _acc((8, D)),
                  _resident((D, IN_W)), _rows(tm, LANES), _rows(tm, LANES)],
        out_specs=[_rows(tm, Q_W + 2 * KV_W), _rows(tm, D), _acc((8, D))],
        out_shape=[jax.ShapeDtypeStruct((seq, Q_W + 2 * KV_W), BF16), jax.ShapeDtypeStruct((seq, D), F32),
                   jax.ShapeDtypeStruct((8, D), F32)],
        compiler_params=_cp("arbitrary"),
    )(dq, dkv, dpb, x, dr1, modx, w_in, cos, sin)


def _ctx_bwd(dkvc, ctx, hc, w_kv):
    n_ctx = ctx.shape[0]

    def body(dkvc_ref, ctx_ref, hc_ref, w_ref, dw_ref, st_ref):
        d = dkvc_ref[...].astype(BF16)
        dw_ref[...] = _dot_tn(hc_ref[...], d)
        dhc = _dot_nt(d, w_ref[...])
        xhat, _ = _ln(ctx_ref[...])
        st_ref[...] = jnp.zeros_like(st_ref)
        st_ref[0:1, :] = _colsum(dhc)
        st_ref[1:2, :] = _colsum(dhc * xhat)

    return pl.pallas_call(
        body, name="ctx_bwd", grid=(1,),
        in_specs=[_acc((n_ctx, 2 * KV_W)), _acc((n_ctx, D)), _acc((n_ctx, D)), _acc((D, 2 * KV_W))],
        out_specs=[_acc((D, 2 * KV_W)), _acc((8, D))],
        out_shape=[jax.ShapeDtypeStruct((D, 2 * KV_W), F32), jax.ShapeDtypeStruct((8, D), F32)],
        compiler_params=_cp("arbitrary"),
    )(dkvc, ctx, hc, w_kv)


def _tn_matmul(a, b, tn, name, init=None, tk=512):
    t, ka = a.shape
    n = b.shape[1]
    tk = min(tk, t)
    has_init = init is not None

    def body(*refs):
        if has_init:
            a_ref, b_ref, i_ref, o_ref = refs
        else:
            a_ref, b_ref, o_ref = refs

        @pl.when(pl.program_id(1) == 0)
        def _():
            o_ref[...] = i_ref[...] if has_init else jnp.zeros_like(o_ref)

        o_ref[...] += _dot_tn(a_ref[...], b_ref[...])

    in_specs = [pl.BlockSpec((tk, ka), lambda j, k: (k, 0)), pl.BlockSpec((tk, tn), lambda j, k: (k, j))]
    args = [a, b]
    if has_init:
        in_specs.append(pl.BlockSpec((ka, tn), lambda j, k: (0, j)))
        args.append(init)
    return pl.pallas_call(
        body, name=name, grid=(n // tn, t // tk), in_specs=in_specs,
        out_specs=pl.BlockSpec((ka, tn), lambda j, k: (0, j)),
        out_shape=jax.ShapeDtypeStruct((ka, n), F32),
        compiler_params=_cp("arbitrary", "arbitrary"),
    )(*args)


ADA_TILE = 512


def _ada_fwd(sc_all, w_ada):
    cs = w_ada.shape[1]

    def body(s_ref, w_ref, o_ref):
        o_ref[...] = _dot(s_ref[...].astype(BF16), w_ref[...].astype(BF16))

    return pl.pallas_call(
        body, name="ada_fwd", grid=(cs // ADA_TILE,),
        in_specs=[_acc((16, D)), pl.BlockSpec((D, ADA_TILE), lambda j: (0, j))],
        out_specs=pl.BlockSpec((16, ADA_TILE), lambda j: (0, j)),
        out_shape=jax.ShapeDtypeStruct((16, cs), F32),
        compiler_params=_cp("arbitrary"),
    )(sc_all, w_ada)


def _ada_bwd(sc_all_t, dm_all, dmc, w_ada):
    cs = w_ada.shape[1]

    def body(st_ref, dm_ref, dmc_ref, w_ref, gw_ref, part_ref):
        @pl.when(pl.program_id(0) == 0)
        def _():
            part_ref[...] = jnp.zeros_like(part_ref)

        gw_ref[...] = _dot(st_ref[...].astype(BF16), dm_ref[...].astype(BF16))
        part_ref[...] += _dot_nt(dmc_ref[...].astype(BF16), w_ref[...].astype(BF16))

    return pl.pallas_call(
        body, name="ada_bwd", grid=(cs // ADA_TILE,),
        in_specs=[_acc((D, 16)), pl.BlockSpec((16, ADA_TILE), lambda j: (0, j)), pl.BlockSpec((8, ADA_TILE), lambda j: (0, j)),
                  pl.BlockSpec((D, ADA_TILE), lambda j: (0, j))],
        out_specs=[pl.BlockSpec((D, ADA_TILE), lambda j: (0, j)), _acc((8, D))],
        out_shape=[jax.ShapeDtypeStruct((D, cs), F32), jax.ShapeDtypeStruct((8, D), F32)],
        compiler_params=_cp("arbitrary"),
    )(sc_all_t, dm_all, dmc, w_ada)


def _sum8(x, name, tr=512):
    r = x.shape[1]
    tr = min(tr, r)
    while r % tr:
        tr -= 8

    def body(x_ref, o_ref):
        acc = x_ref[0].astype(F32)
        for i in range(1, N_DEV):
            acc = acc + x_ref[i].astype(F32)
        o_ref[...] = acc

    return pl.pallas_call(
        body, name=name, grid=(r // tr,),
        in_specs=[pl.BlockSpec((N_DEV, tr, LANES), lambda i: (0, i, 0))],
        out_specs=pl.BlockSpec((tr, LANES), lambda i: (i, 0)),
        out_shape=jax.ShapeDtypeStruct((r, LANES), F32),
        compiler_params=_cp("arbitrary"),
    )(x)


def _adamw(w, g, m, v, name):
    r, c = w.shape
    tr = min(256, r)
    while r % tr:
        tr -= 8

    def body(w_ref, g_ref, m_ref, v_ref, d_ref, nm_ref, nv_ref):
        gr = g_ref[...]
        nm = ADAM_B1 * m_ref[...] + (1.0 - ADAM_B1) * gr
        nv = ADAM_B2 * v_ref[...] + (1.0 - ADAM_B2) * (gr * gr)
        m_hat = nm / (1.0 - ADAM_B1 ** ADAM_STEP)
        v_hat = nv / (1.0 - ADAM_B2 ** ADAM_STEP)
        d_ref[...] = -ADAM_LR * (m_hat / (jnp.sqrt(v_hat) + ADAM_EPS) + ADAM_WD * w_ref[...])
        nm_ref[...] = nm
        nv_ref[...] = nv

    spec = pl.BlockSpec((tr, c), lambda i: (i, 0))
    shp = jax.ShapeDtypeStruct((r, c), F32)
    return pl.pallas_call(
        body, name=name, grid=(r // tr,), in_specs=[spec] * 4, out_specs=[spec] * 3, out_shape=[shp] * 3,
        compiler_params=_cp("arbitrary"),
    )(w, g, m, v)


def _my_pos():
    return lax.axis_index("x"), lax.axis_index("y"), lax.axis_index("c")


def _all_gather(xs, name):
    r, cdim = xs.shape

    def body(x_ref, out_ref, send_sems, recv_sems, local_sem):
        x, y, c = _my_pos()
        me, sibling = (x, y, c), (x, y, 1 - c)
        chips = [(1 - x, y), (x, 1 - y), (1 - x, 1 - y)]

        def rows(px, py, pc):
            return out_ref.at[4 * px + 2 * py + pc]

        def copy(k, block, to, src=None):
            return pltpu.make_async_remote_copy(
                src_ref=rows(*block) if src is None else src, dst_ref=rows(*block),
                send_sem=send_sems.at[k], recv_sem=recv_sems.at[k], device_id=to, device_id_type=MESH)

        mine = pltpu.make_async_copy(x_ref, rows(*me), local_sem)
        mine.start()
        first = [copy(0, me, sibling, src=x_ref)]
        first += [copy(1 + j, me, (*chip, c), src=x_ref) for j, chip in enumerate(chips)]
        for cp in first:
            cp.start()
        passed = [copy(4 + j, (*chip, c), sibling) for j, chip in enumerate(chips)]
        for j, chip in enumerate(chips):
            copy(1 + j, (*chip, c), me).wait_recv()
            passed[j].start()
        copy(0, sibling, me).wait_recv()
        for j, chip in enumerate(chips):
            copy(4 + j, (*chip, 1 - c), me).wait_recv()
        for cp in first + passed:
            cp.wait_send()
        mine.wait()

    return pl.pallas_call(
        body, name=name,
        out_shape=jax.ShapeDtypeStruct((N_DEV, r, cdim), xs.dtype),
        in_specs=[pl.BlockSpec(memory_space=pl.ANY)],
        out_specs=pl.BlockSpec(memory_space=pl.ANY),
        scratch_shapes=[pltpu.SemaphoreType.DMA((7,)), pltpu.SemaphoreType.DMA((7,)), pltpu.SemaphoreType.DMA],
    )(xs)


def _all_to_all(blocks, name):
    _, r, cdim = blocks.shape

    def body(x_ref, out_ref, send_sems, recv_sems, local_sem):
        x, y, c = _my_pos()
        me = 4 * x + 2 * y + c

        def peer(k):
            return (x ^ ((k >> 2) & 1), y ^ ((k >> 1) & 1), c ^ (k & 1))

        def index(p):
            return 4 * p[0] + 2 * p[1] + p[2]

        def copy(k):
            return pltpu.make_async_remote_copy(
                src_ref=x_ref.at[index(peer(k))], dst_ref=out_ref.at[me],
                send_sem=send_sems.at[k - 1], recv_sem=recv_sems.at[k - 1], device_id=peer(k), device_id_type=MESH)

        def arrival(k):
            return pltpu.make_async_remote_copy(
                src_ref=x_ref.at[me], dst_ref=out_ref.at[index(peer(k))],
                send_sem=send_sems.at[k - 1], recv_sem=recv_sems.at[k - 1], device_id=peer(k), device_id_type=MESH)

        mine = pltpu.make_async_copy(x_ref.at[me], out_ref.at[me], local_sem)
        mine.start()
        sends = [copy(k) for k in range(1, N_DEV)]
        for cp in sends:
            cp.start()
        for k in range(1, N_DEV):
            arrival(k).wait_recv()
        for cp in sends:
            cp.wait_send()
        mine.wait()

    return pl.pallas_call(
        body, name=name,
        out_shape=jax.ShapeDtypeStruct((N_DEV, r, cdim), blocks.dtype),
        in_specs=[pl.BlockSpec(memory_space=pl.ANY)],
        out_specs=pl.BlockSpec(memory_space=pl.ANY),
        scratch_shapes=[pltpu.SemaphoreType.DMA((7,)), pltpu.SemaphoreType.DMA((7,)), pltpu.SemaphoreType.DMA],
    )(blocks)


def _sibling_exchange(xs, name):
    r, cdim = xs.shape

    def body(x_ref, out_ref, send_sem, recv_sem, local_sem):
        x, y, c = _my_pos()
        mine = pltpu.make_async_copy(x_ref, out_ref.at[c], local_sem)
        mine.start()
        push = pltpu.make_async_remote_copy(src_ref=x_ref, dst_ref=out_ref.at[c], send_sem=send_sem, recv_sem=recv_sem,
                                            device_id=(x, y, 1 - c), device_id_type=MESH)
        push.start()
        pltpu.make_async_remote_copy(src_ref=x_ref, dst_ref=out_ref.at[1 - c], send_sem=send_sem, recv_sem=recv_sem,
                                     device_id=(x, y, 1 - c), device_id_type=MESH).wait_recv()
        push.wait_send()
        mine.wait()

    return pl.pallas_call(
        body, name=name,
        out_shape=jax.ShapeDtypeStruct((2, r, cdim), xs.dtype),
        in_specs=[pl.BlockSpec(memory_space=pl.ANY)],
        out_specs=pl.BlockSpec(memory_space=pl.ANY),
        scratch_shapes=[pltpu.SemaphoreType.DMA, pltpu.SemaphoreType.DMA, pltpu.SemaphoreType.DMA],
    )(xs)


def _row_tile(seq, want):
    return min(want, seq)


def _local_step(x, ctx, tgt, mod_x, mod_c, wb, sink, gmlp_g, gmlp_b, w_s, b_s, ln1_g, ln1_b, ln2_g, ln2_b):
    seq = x.shape[0]
    modx1 = jnp.concatenate([mod_x[0:2], jnp.zeros((6, D), F32)], axis=0)
    modc = jnp.concatenate([mod_c[0:2], jnp.zeros((6, D), F32)], axis=0)
    vec = jnp.concatenate([mod_x[2:3], ln1_g, ln1_b, mod_x[3:6], ln2_g, ln2_b], axis=0)
    gp = jnp.concatenate([gmlp_g, gmlp_b, jnp.zeros((6, G_W), F32)], axis=0)
    ws_stack = w_s.reshape(N_GRP * BLK, BLK).astype(BF16)
    ws_stack_t = jnp.transpose(w_s, (2, 0, 1)).reshape(BLK, N_GRP * BLK).astype(BF16)
    bias_full = jnp.repeat(b_s.T, GRP_D, axis=1)
    cos, sin = _rope_tables(seq)
    w_in = wb["w_in"]
    w_kv = w_in[:, Q_W:Q_W + 2 * KV_W]
    tm_big = _row_tile(seq, 512)
    tm_ffn = _row_tile(seq, 256)

    hc, kvc = _ctx_fwd(ctx, modc, w_kv)
    h, q, kv, uv, gab = _proj_fwd(x, modx1, w_in, cos, sin, tm_big)
    ya = _attn_fwd(q, kv, kvc, sink)
    a, b, mix, merged, yb = _mix_fwd(uv, gab, ya, gp, ws_stack, bias_full, wb["w_a"], wb["w_b"], wb["w_o"], tm_big)
    gu, act, h2, dr2, st5 = _ffn_fwd(x, mix, tgt, vec, wb["w_fi"], wb["w_fo"], tm_ffn)

    dff, df, dr1, st5b = _ffn_bwd(dr2, gu, x, mix, vec, wb["w_fi"], wb["w_fo"], tm_ffn)
    g_w_fo = _tn_matmul(act, df, 512, "tn_w_ffn_out")
    g_w_fi = _tn_matmul(h2, dff, 1408, "tn_w_ffn_in")
    dmix, da, db, dya, dpb, dws, dbs_full, st4 = _mix_bwd(
        dr1, a, b, gab, uv, vec, gp, ws_stack, ws_stack_t, bias_full, wb["w_a"], wb["w_b"], wb["w_o"], _row_tile(seq, 256))
    g_w_o = _tn_matmul(merged, dmix, 1024, "tn_w_out")
    g_w_a = _tn_matmul(ya, da, 1024, "tn_w_branch_a")
    g_w_b = _tn_matmul(yb, db, 1024, "tn_w_branch_b")
    dq, dkv, dkvc, dsink = _attn_bwd(q, kv, kvc, sink, dya)
    g_wkv_ctx, st0 = _ctx_bwd(dkvc, ctx, hc, w_kv)
    dqkv, grad_x, st1 = _proj_bwd(dq, dkv, dpb, x, dr1, modx1, w_in, cos, sin, tm_big)
    o = Q_W + 2 * KV_W
    init = jnp.pad(g_wkv_ctx, ((0, 0), (Q_W, 0)))
    g_w_in = jnp.concatenate([_tn_matmul(h, dqkv, o, "tn_w_in_qkv", init=init),
                              _tn_matmul(h, dpb, 1536, "tn_w_in_rest")], axis=1)

    dmod_x = jnp.concatenate([st1[0:2], st5b[4:5], st5b[1:2], st5b[0:1], st5[3:4]], axis=0)
    dmod_c = st0[0:2]
    small = dict(
        loss_cols=st5[0], sink=dsink[:, 0], gmlp_g=st4[0], gmlp_b=st4[1],
        w_s=dws.reshape(N_GRP, BLK, BLK), b_s=jnp.sum(dbs_full.reshape(BLK, N_GRP, GRP_D), axis=2).T,
        ln1_g=st5b[2], ln1_b=st5b[3], ln2_g=st5[1], ln2_b=st5[2])
    big = dict(w_in=g_w_in, w_a=g_w_a, w_b=g_w_b, w_o=g_w_o, w_fi=g_w_fi, w_fo=g_w_fo)
    return grad_x, dmod_x, dmod_c, small, big


BIG = ("w_in", "w_a", "w_b", "w_o", "w_fi", "w_fo")
COL_SHARDED = {"w_in": True, "w_a": True, "w_b": True, "w_o": False, "w_fi": True, "w_fo": False}


def _half_of_shard(shard, c):
    r = shard.shape[0]
    return lax.dynamic_slice_in_dim(shard, c * (r // 2), r // 2, axis=0).reshape(-1)


def _blocks_of_full(name, full):
    r, ccols = full.shape
    if COL_SHARDED[name]:
        return full.reshape(2, r // 2, 4, ccols // 4).transpose(2, 0, 1, 3).reshape(N_DEV, -1)
    return full.reshape(N_DEV, -1)


def _full_of_blocks(name, blocks, shape):
    r, ccols = shape
    if COL_SHARDED[name]:
        return blocks.reshape(4, 2, r // 2, ccols // 4).transpose(1, 2, 0, 3).reshape(r, ccols)
    return blocks.reshape(r, ccols)


def _to_lanes(flat):
    return flat.reshape(-1, LANES)


SMALL_ORDER = ("loss_cols", "sink", "gmlp_g", "gmlp_b", "w_s", "b_s", "ln1_g", "ln1_b", "ln2_g", "ln2_b")


def kernel(x, c, ctx, c_ctx, w_ada, b_ada, w_in, attn_sink, gmlp_ln_g, gmlp_ln_b, w_spatial, b_spatial, w_branch_a, w_branch_b, w_out, ln1_g, ln1_b, w_ffn_in, w_ffn_out, ln2_g, ln2_b, loss_target, m_c_ctx, m_w_ada, m_b_ada, m_w_in, m_attn_sink, m_gmlp_ln_g, m_gmlp_ln_b, m_w_spatial, m_b_spatial, m_w_branch_a, m_w_branch_b, m_w_out, m_ln1_g, m_ln1_b, m_w_ffn_in, m_w_ffn_out, m_ln2_g, m_ln2_b, v_c_ctx, v_w_ada, v_b_ada, v_w_in, v_attn_sink, v_gmlp_ln_g, v_gmlp_ln_b, v_w_spatial, v_b_spatial, v_w_branch_a, v_w_branch_b, v_w_out, v_ln1_g, v_ln1_b, v_w_ffn_in, v_w_ffn_out, v_ln2_g, v_ln2_b):
    mx, my, mc = _my_pos()
    me = 4 * mx + 2 * my + mc
    chip = 2 * mx + my
    shards = dict(w_in=w_in[0], w_a=w_branch_a[0], w_b=w_branch_b[0], w_o=w_out[0], w_fi=w_ffn_in[0], w_fo=w_ffn_out[0])
    full_shapes = {n: ((s.shape[0], 4 * s.shape[1]) if COL_SHARDED[n] else (4 * s.shape[0], s.shape[1])) for n, s in shards.items()}

    mine = jnp.concatenate([_half_of_shard(shards[n], mc) for n in BIG]).astype(BF16)
    gathered = _all_gather(_to_lanes(mine), "gather_weights").reshape(N_DEV, -1)
    wb, off = {}, 0
    for n in BIG:
        size = shards[n].size // 2
        wb[n] = _full_of_blocks(n, gathered[:, off:off + size], full_shapes[n])
        off += size

    c_rows = jnp.concatenate([c, jnp.zeros((7, D), F32)], axis=0)
    c_all = _all_gather(c_rows, "gather_c")[:, 0, :]
    cc = jnp.concatenate([c_all, c_ctx[None, :], jnp.zeros((7, D), F32)], axis=0)
    sig_cc = jax.nn.sigmoid(cc)
    sc_all = cc * sig_cc
    mod_shard = _ada_fwd(sc_all, w_ada[0])
    mod_g = _all_gather(mod_shard, "gather_mod")
    mod_all = jnp.concatenate([mod_g[2 * s] for s in range(4)], axis=1) + b_ada
    mod_x = lax.dynamic_slice_in_dim(mod_all, me, 1, axis=0).reshape(6, D)
    mod_c = mod_all[8].reshape(6, D)[0:2]

    grad_x, dmod_x, dmod_c, small, big = _local_step(
        x[0], ctx[0], loss_target[0], mod_x, mod_c, wb, attn_sink, gmlp_ln_g, gmlp_ln_b, w_spatial[0], b_spatial[0],
        ln1_g, ln1_b, ln2_g, ln2_b)

    blocks = jnp.concatenate([_blocks_of_full(n, big[n]) for n in BIG], axis=1).astype(BF16)
    nrow = blocks.shape[1] // LANES
    recv = _all_to_all(blocks.reshape(N_DEV, nrow, LANES), "scatter_grads")
    red = _sum8(recv, "sum_grads")
    both = _sibling_exchange(red, "exchange_grads").reshape(2, -1)
    g_shard, off = {}, 0
    for n in BIG:
        size = shards[n].size // 2
        g_shard[n] = both[:, off:off + size].reshape(shards[n].shape)
        off += size

    parts = [small[n].reshape(-1) for n in SMALL_ORDER] + [dmod_c.reshape(-1), dmod_x.reshape(-1)]
    sizes = [p.shape[0] for p in parts]
    flat = jnp.concatenate(parts)
    pad = (-flat.shape[0]) % (8 * LANES)
    flat = jnp.concatenate([flat, jnp.zeros((pad,), F32)])
    gath = _all_gather(_to_lanes(flat), "gather_small")
    tot = _sum8(gath, "sum_small").reshape(-1)
    offs = [0]
    for s_ in sizes:
        offs.append(offs[-1] + s_)
    tots = {n: tot[offs[i]:offs[i + 1]] for i, n in enumerate(SMALL_ORDER)}
    dmod_c_tot = tot[offs[-3]:offs[-2]]
    dmod_x_sum = tot[offs[-2]:offs[-1]]
    dmod_x_all = gath.reshape(N_DEV, -1)[:, offs[-2]:offs[-1]]
    loss = 0.5 * jnp.sum(tots["loss_cols"]) / D

    dmod_c_full = jnp.concatenate([dmod_c_tot, jnp.zeros((4 * D,), F32)])
    g_b_ada = (dmod_x_sum + dmod_c_full)[None, :]
    dm_rows = jnp.concatenate([dmod_x_all, dmod_c_full[None, :], jnp.zeros((7, 6 * D), F32)], axis=0)
    cs = w_ada.shape[2]
    dm_shard = lax.dynamic_slice_in_dim(dm_rows, chip * cs, cs, axis=1)
    dmc_shard = jnp.concatenate([dm_shard[8:9], jnp.zeros((7, cs), F32)], axis=0)
    g_w_ada, part = _ada_bwd(sc_all.T, dm_shard, dmc_shard, w_ada[0])
    part = part * (mc == 0).astype(F32)
    part_all = _all_gather(_to_lanes(part.reshape(-1)), "gather_c_ctx")
    dsc = _sum8(part_all, "sum_c_ctx").reshape(8, D)[0]
    sig_ctx = sig_cc[8]
    g_c_ctx = dsc * (sig_ctx * (1.0 + c_ctx * (1.0 - sig_ctx)))

    grads = dict(c_ctx=g_c_ctx, w_ada=g_w_ada[None], b_ada=g_b_ada, w_in=g_shard["w_in"][None],
                 attn_sink=tots["sink"][None, :], gmlp_ln_g=tots["gmlp_g"][None, :], gmlp_ln_b=tots["gmlp_b"][None, :],
                 w_spatial=tots["w_s"].reshape(w_spatial.shape), b_spatial=tots["b_s"].reshape(b_spatial.shape),
                 w_branch_a=g_shard["w_a"][None], w_branch_b=g_shard["w_b"][None], w_out=g_shard["w_o"][None],
                 ln1_g=tots["ln1_g"][None, :], ln1_b=tots["ln1_b"][None, :],
                 w_ffn_in=g_shard["w_fi"][None], w_ffn_out=g_shard["w_fo"][None],
                 ln2_g=tots["ln2_g"][None, :], ln2_b=tots["ln2_b"][None, :])
    weights = dict(c_ctx=c_ctx, w_ada=w_ada, b_ada=b_ada, w_in=w_in, attn_sink=attn_sink, gmlp_ln_g=gmlp_ln_g,
                   gmlp_ln_b=gmlp_ln_b, w_spatial=w_spatial, b_spatial=b_spatial, w_branch_a=w_branch_a,
                   w_branch_b=w_branch_b, w_out=w_out, ln1_g=ln1_g, ln1_b=ln1_b, w_ffn_in=w_ffn_in, w_ffn_out=w_ffn_out,
                   ln2_g=ln2_g, ln2_b=ln2_b)
    ms = dict(c_ctx=m_c_ctx, w_ada=m_w_ada, b_ada=m_b_ada, w_in=m_w_in, attn_sink=m_attn_sink, gmlp_ln_g=m_gmlp_ln_g,
              gmlp_ln_b=m_gmlp_ln_b, w_spatial=m_w_spatial, b_spatial=m_b_spatial, w_branch_a=m_w_branch_a,
              w_branch_b=m_w_branch_b, w_out=m_w_out, ln1_g=m_ln1_g, ln1_b=m_ln1_b, w_ffn_in=m_w_ffn_in,
              w_ffn_out=m_w_ffn_out, ln2_g=m_ln2_g, ln2_b=m_ln2_b)
    vs = dict(c_ctx=v_c_ctx, w_ada=v_w_ada, b_ada=v_b_ada, w_in=v_w_in, attn_sink=v_attn_sink, gmlp_ln_g=v_gmlp_ln_g,
              gmlp_ln_b=v_gmlp_ln_b, w_spatial=v_w_spatial, b_spatial=v_b_spatial, w_branch_a=v_w_branch_a,
              w_branch_b=v_w_branch_b, w_out=v_w_out, ln1_g=v_ln1_g, ln1_b=v_ln1_b, w_ffn_in=v_w_ffn_in,
              w_ffn_out=v_w_ffn_out, ln2_g=v_ln2_g, ln2_b=v_ln2_b)
    order = list(weights)
    large = ("w_ada", "w_in", "w_branch_a", "w_branch_b", "w_out", "w_ffn_in", "w_ffn_out")
    delta, new_m, new_v = {}, {}, {}
    for n in large:
        d_, m_, v_ = _adamw(weights[n][0], grads[n][0], ms[n][0], vs[n][0], "adamw_" + n)
        delta[n], new_m[n], new_v[n] = d_[None], m_[None], v_[None]
    rest = [n for n in order if n not in large]

    def pack(d):
        f = jnp.concatenate([d[n].reshape(-1) for n in rest])
        return _to_lanes(jnp.concatenate([f, jnp.zeros(((-f.shape[0]) % (8 * LANES),), F32)]))

    d_, m_, v_ = _adamw(pack(weights), pack(grads), pack(ms), pack(vs), "adamw_small")
    off = 0
    for n in rest:
        size = weights[n].size
        for out, src in ((delta, d_), (new_m, m_), (new_v, v_)):
            out[n] = src.reshape(-1)[off:off + size].reshape(weights[n].shape)
        off += size

    return (loss, grad_x[None], *[grads[n] for n in order], *[delta[n] for n in order],
            *[new_m[n] for n in order], *[new_v[n] for n in order])
```

```python
import functools
import math

import jax
import jax.numpy as jnp
from jax import lax
from jax.experimental import pallas as pl
from jax.experimental.pallas import tpu as pltpu

F32 = jnp.float32
BF16 = jnp.bfloat16

D = 1024
HEAD = 64
N_KV = 2
GROUP = 4
Q_W = 512
KV_W = 128
G_W = 512
BLK = 128
N_GRP = 8
GRP_D = 64
FH = 2816
IN_W = 3840
GRID_W = 64
ROPE_BASE = 10000.0
LN_EPS = 1e-5
NEG = -1e30
ALPHA = (2 * 1) ** 0.25
SCALE = HEAD ** -0.5
GELU_K = math.sqrt(2.0 / math.pi)
GELU_A = 0.044715
ADAM_LR = 0.001
ADAM_B1 = 0.9
ADAM_B2 = 0.999
ADAM_EPS = 1e-08
ADAM_WD = 0.01
ADAM_STEP = 10
N_DEV = 8
N_SHARD = 4
FH_SHARD = FH // 2
LANES = 128
VMEM_LIMIT = 56 * 1024 * 1024
MESH = pl.DeviceIdType.MESH


def _cp(*sem):
    return pltpu.CompilerParams(dimension_semantics=sem, vmem_limit_bytes=VMEM_LIMIT)


def _resident(shape):
    return pl.BlockSpec(shape, lambda *_: (0,) * len(shape), pipeline_mode=pl.Buffered(1))


def _rows(tm, width):
    return pl.BlockSpec((tm, width), lambda i: (i, 0))


def _acc(shape):
    return pl.BlockSpec(shape, lambda *_: (0,) * len(shape))


def _dot(a, b):
    return jnp.dot(a, b, preferred_element_type=F32)


def _dot_nt(a, b):
    return lax.dot_general(a, b, (((1,), (1,)), ((), ())), preferred_element_type=F32)


def _dot_tn(a, b):
    return lax.dot_general(a, b, (((0,), (0,)), ((), ())), preferred_element_type=F32)


def _ln(x):
    mu = jnp.mean(x, axis=-1, keepdims=True)
    xc = x - mu
    var = jnp.mean(xc * xc, axis=-1, keepdims=True)
    rstd = lax.rsqrt(var + LN_EPS)
    return xc * rstd, rstd


def _ln_bwd(dxhat, xhat, rstd):
    return (dxhat - jnp.mean(dxhat, axis=-1, keepdims=True)
            - xhat * jnp.mean(dxhat * xhat, axis=-1, keepdims=True)) * rstd


def _sig(x):
    return 1.0 / (1.0 + jnp.exp(-x))


def _gelu(x):
    t = jnp.tanh(GELU_K * (x + GELU_A * x * x * x))
    return 0.5 * x * (1.0 + t), t


def _gelu_grad(x, t):
    return 0.5 * (1.0 + t) + 0.5 * x * (1.0 - t * t) * GELU_K * (1.0 + 3.0 * GELU_A * x * x)


def _colsum(v):
    return jnp.sum(v, axis=0, keepdims=True)


def _partner(x):
    w = x.shape[1]
    lane = lax.broadcasted_iota(jnp.int32, x.shape, 1)
    return jnp.where((lane & 31) < 16, pltpu.roll(x, w - 16, 1), pltpu.roll(x, 16, 1))


def _rope(x, cos, sin):
    return x * cos + _partner(x) * sin


def _unrope(g, cos, sin):
    return g * cos + _partner(g * sin)


def _rope_tables(seq):
    inv = ROPE_BASE ** (-jnp.arange(HEAD // 4, dtype=F32) / (HEAD // 4))
    pos = jnp.arange(seq, dtype=jnp.int32)
    ar = (pos // GRID_W).astype(F32)[:, None] * inv
    ac = (pos % GRID_W).astype(F32)[:, None] * inv
    cos = jnp.concatenate([jnp.cos(ar), jnp.cos(ar), jnp.cos(ac), jnp.cos(ac)], axis=-1)
    sin = jnp.concatenate([-jnp.sin(ar), jnp.sin(ar), -jnp.sin(ac), jnp.sin(ac)], axis=-1)
    return jnp.tile(cos, (1, LANES // HEAD)), jnp.tile(sin, (1, LANES // HEAD))


def _ctx_fwd(ctx, modc, w_kv):
    n_ctx = ctx.shape[0]

    def body(ctx_ref, mod_ref, w_ref, hc_ref, kvc_ref):
        xhat, _ = _ln(ctx_ref[...])
        hc = (xhat * (1.0 + mod_ref[1:2, :]) + mod_ref[0:1, :]).astype(BF16)
        hc_ref[...] = hc
        kvc_ref[...] = _dot(hc, w_ref[...]).astype(BF16)

    return pl.pallas_call(
        body, name="ctx_fwd", grid=(1,),
        in_specs=[_acc((n_ctx, D)), _acc((8, D)), _acc((D, 2 * KV_W))],
        out_specs=[_acc((n_ctx, D)), _acc((n_ctx, 2 * KV_W))],
        out_shape=[jax.ShapeDtypeStruct((n_ctx, D), BF16), jax.ShapeDtypeStruct((n_ctx, 2 * KV_W), BF16)],
        compiler_params=_cp("arbitrary"),
    )(ctx, modc, w_kv)


def _proj_fwd(x, modx, w_in, cos, sin, tm):
    seq = x.shape[0]

    def body(x_ref, mod_ref, w_ref, cos_ref, sin_ref, h_ref, q_ref, kv_ref, uv_ref, gab_ref):
        xhat, _ = _ln(x_ref[...])
        h = (xhat * (1.0 + mod_ref[1:2, :]) + mod_ref[0:1, :]).astype(BF16)
        h_ref[...] = h
        cos1, sin1 = cos_ref[...], sin_ref[...]
        cos2 = jnp.concatenate([cos1, cos1], axis=1)
        sin2 = jnp.concatenate([sin1, sin1], axis=1)
        for j in range(Q_W // 256):
            t = _dot(h, w_ref[:, 256 * j:256 * (j + 1)])
            q_ref[:, 256 * j:256 * (j + 1)] = _rope(t, cos2, sin2).astype(BF16)
        t = _dot(h, w_ref[:, Q_W:Q_W + 2 * KV_W])
        kv_ref[:, :KV_W] = _rope(t[:, :KV_W], cos1, sin1).astype(BF16)
        kv_ref[:, KV_W:] = t[:, KV_W:].astype(BF16)
        o = Q_W + 2 * KV_W
        for j in range(2):
            uv_ref[:, G_W * j:G_W * (j + 1)] = _dot(h, w_ref[:, o + G_W * j:o + G_W * (j + 1)])
        o += 2 * G_W
        for j in range(4):
            gab_ref[:, 512 * j:512 * (j + 1)] = _dot(h, w_ref[:, o + 512 * j:o + 512 * (j + 1)])

    return pl.pallas_call(
        body, name="proj_fwd", grid=(seq // tm,),
        in_specs=[_rows(tm, D), _acc((8, D)), _resident((D, IN_W)), _rows(tm, LANES), _rows(tm, LANES)],
        out_specs=[_rows(tm, D), _rows(tm, Q_W), _rows(tm, 2 * KV_W), _rows(tm, 2 * G_W), _rows(tm, 2 * D)],
        out_shape=[jax.ShapeDtypeStruct((seq, D), BF16), jax.ShapeDtypeStruct((seq, Q_W), BF16),
                   jax.ShapeDtypeStruct((seq, 2 * KV_W), BF16), jax.ShapeDtypeStruct((seq, 2 * G_W), F32),
                   jax.ShapeDtypeStruct((seq, 2 * D), F32)],
        compiler_params=_cp("arbitrary"),
    )(x, modx, w_in, cos, sin)


def _attn_pieces(q_ref, kvp_ref, kvm_ref, kvn_ref, kvc_ref, sink_ref, hk, n, nb):
    q = q_ref[...]
    q4 = jnp.concatenate([q[:, (hk * GROUP + g) * HEAD:(hk * GROUP + g + 1) * HEAD] for g in range(GROUP)], axis=0)
    ks = [r[:, hk * HEAD:(hk + 1) * HEAD] for r in (kvc_ref, kvp_ref, kvm_ref, kvn_ref)]
    vs = [r[:, KV_W + hk * HEAD:KV_W + (hk + 1) * HEAD] for r in (kvc_ref, kvp_ref, kvm_ref, kvn_ref)]
    rows = GROUP * BLK
    qi = lax.broadcasted_iota(jnp.int32, (rows, BLK), 0) & (BLK - 1)
    kj = lax.broadcasted_iota(jnp.int32, (rows, BLK), 1)
    s = [_dot_nt(q4, k) * SCALE for k in ks]
    s[1] = jnp.where((kj >= qi) & (n > 0), s[1], NEG)
    s[3] = jnp.where((kj <= qi) & (n < nb - 1), s[3], NEG)
    rg = lax.broadcasted_iota(jnp.int32, (rows, 1), 0) >> 7
    sink_v = jnp.full((rows, 1), sink_ref[0, hk * GROUP], F32)
    for g in range(1, GROUP):
        sink_v = jnp.where(rg == g, sink_ref[0, hk * GROUP + g], sink_v)
    m = sink_v
    for t in s:
        m = jnp.maximum(m, jnp.max(t, axis=-1, keepdims=True))
    e = [jnp.exp(t - m) for t in s]
    e_sink = jnp.exp(sink_v - m)
    denom = e_sink
    for t in e:
        denom = denom + jnp.sum(t, axis=-1, keepdims=True)
    inv = 1.0 / denom
    p = [t * inv for t in e]
    return q4, ks, vs, p, e_sink * inv


def _kv_specs(nb):
    return [pl.BlockSpec((BLK, 2 * KV_W), lambda n: (jnp.maximum(n - 1, 0), 0)),
            pl.BlockSpec((BLK, 2 * KV_W), lambda n: (n, 0)),
            pl.BlockSpec((BLK, 2 * KV_W), lambda n: (jnp.minimum(n + 1, nb - 1), 0))]


def _attn_fwd(q, kv, kvc, sink):
    seq = q.shape[0]
    nb = seq // BLK
    n_ctx = kvc.shape[0]

    def body(q_ref, kvp_ref, kvm_ref, kvn_ref, kvc_ref, sink_ref, o_ref):
        n = pl.program_id(0)
        outs = []
        for hk in range(N_KV):
            _, _, vs, p, _ = _attn_pieces(q_ref, kvp_ref, kvm_ref, kvn_ref, kvc_ref, sink_ref, hk, n, nb)
            o = _dot(p[0].astype(BF16), vs[0])
            for t, v in zip(p[1:], vs[1:]):
                o = o + _dot(t.astype(BF16), v)
            outs += [o[g * BLK:(g + 1) * BLK, :] for g in range(GROUP)]
        o_ref[...] = jnp.concatenate(outs, axis=1).astype(BF16)

    return pl.pallas_call(
        body, name="attn_fwd", grid=(nb,),
        in_specs=[_rows(BLK, Q_W)] + _kv_specs(nb) + [_acc((n_ctx, 2 * KV_W)), pl.BlockSpec(memory_space=pltpu.SMEM)],
        out_specs=_rows(BLK, Q_W),
        out_shape=jax.ShapeDtypeStruct((seq, Q_W), BF16),
        compiler_params=_cp("arbitrary"),
    )(q, kv, kv, kv, kvc, sink)


def _gmlp_chunk(u, vb, gp_ref, ws_ref, bias_ref):
    gu, tu = _gelu(u)
    gv, tv = _gelu(vb)
    vhat, rstd = _ln(gv)
    vn = (vhat * gp_ref[0:1, :] + gp_ref[1:2, :]).astype(BF16)
    r = _dot(ws_ref[...], vn)
    grp = lax.broadcasted_iota(jnp.int32, (BLK, G_W), 1) >> 6
    s = bias_ref[...]
    for g in range(N_GRP):
        s = s + jnp.where(grp == g, r[g * BLK:(g + 1) * BLK, :], 0.0)
    return gu, tu, tv, vhat, rstd, vn, s, grp


def _mix_fwd(uv, gab, ya, gp, ws_stack, bias_full, w_a, w_b, w_o, tm):
    seq = uv.shape[0]

    def body(uv_ref, gab_ref, ya_ref, gp_ref, ws_ref, bias_ref, wa_ref, wb_ref, wo_ref,
             a_ref, b_ref, mix_ref, merged_ref, yb_ref):
        for c in range(tm // BLK):
            rs = slice(c * BLK, (c + 1) * BLK)
            gu, _, _, _, _, _, s, _ = _gmlp_chunk(uv_ref[rs, :G_W], uv_ref[rs, G_W:], gp_ref, ws_ref, bias_ref)
            yb_ref[rs, :] = (gu * s).astype(BF16)
        ya = ya_ref[...]
        yb = yb_ref[...]
        for s in range(N_SHARD):
            cs = slice(s * (D // N_SHARD), (s + 1) * (D // N_SHARD))
            a_ref[:, cs] = _dot(ya, wa_ref[s])
            b_ref[:, cs] = _dot(yb, wb_ref[s])
        merged = (_sig(gab_ref[:, :D]) * a_ref[...] + _sig(gab_ref[:, D:]) * b_ref[...]).astype(BF16)
        merged_ref[...] = merged
        mix_ref[...] = _dot(merged, wo_ref[...])

    return pl.pallas_call(
        body, name="mix_fwd", grid=(seq // tm,),
        in_specs=[_rows(tm, 2 * G_W), _rows(tm, 2 * D), _rows(tm, Q_W), _acc((8, G_W)),
                  _resident((N_GRP * BLK, BLK)), _acc((BLK, G_W)),
                  _resident((N_SHARD, Q_W, D // N_SHARD)), _resident((N_SHARD, G_W, D // N_SHARD)), _resident((D, D))],
        out_specs=[_rows(tm, D), _rows(tm, D), _rows(tm, D), _rows(tm, D), _rows(tm, G_W)],
        out_shape=[jax.ShapeDtypeStruct((seq, D), F32), jax.ShapeDtypeStruct((seq, D), F32),
                   jax.ShapeDtypeStruct((seq, D), F32), jax.ShapeDtypeStruct((seq, D), BF16),
                   jax.ShapeDtypeStruct((seq, G_W), BF16)],
        compiler_params=_cp("arbitrary"),
    )(uv, gab, ya, gp, ws_stack, bias_full, w_a, w_b, w_o)


FFN_CHUNK = 256


def _ffn_chunks():
    out = []
    for hh in range(2):
        off = 0
        while off < FH_SHARD:
            w = min(FFN_CHUNK, FH_SHARD - off)
            out.append((hh, off, w))
            off += w
    return out


def _mid_recompute(x_ref, mix_ref, vec_ref):
    r1 = ALPHA * x_ref[...] + vec_ref[0:1, :] * mix_ref[...]
    xh1, rstd1 = _ln(r1)
    xmid = xh1 * vec_ref[1:2, :] + vec_ref[2:3, :]
    xh2, rstd2 = _ln(xmid)
    return xh1, rstd1, xmid, xh2, rstd2


def _ffn_fwd(x, mix, tgt, vec, w_fi, w_fo, tm):
    seq = x.shape[0]

    def body(x_ref, mix_ref, tgt_ref, vec_ref, wi_ref, wo_ref, gu_ref, act_ref, h2_ref, dr2_ref, st_ref):
        @pl.when(pl.program_id(0) == 0)
        def _():
            st_ref[...] = jnp.zeros_like(st_ref)

        _, _, xmid, xh2, _ = _mid_recompute(x_ref, mix_ref, vec_ref)
        h2 = (xh2 * (1.0 + vec_ref[4:5, :]) + vec_ref[3:4, :]).astype(BF16)
        h2_ref[...] = h2
        f = jnp.zeros((tm, D), F32)
        for hh, off, w in _ffn_chunks():
            cs = slice(hh * FH_SHARD + off, hh * FH_SHARD + off + w)
            cu = slice(FH + hh * FH_SHARD + off, FH + hh * FH_SHARD + off + w)
            g = _dot(h2, wi_ref[hh, :, off:off + w])
            u = _dot(h2, wi_ref[2 + hh, :, off:off + w])
            gu_ref[:, cs] = g
            gu_ref[:, cu] = u
            a = (g * _sig(g) * u).astype(BF16)
            act_ref[:, cs] = a
            f = f + _dot(a, wo_ref[cs, :])
        r2 = ALPHA * xmid + vec_ref[5:6, :] * f
        yh, rstd = _ln(r2)
        y = yh * vec_ref[6:7, :] + vec_ref[7:8, :]
        err = y - tgt_ref[...]
        dy = err / D
        dr2 = _ln_bwd(dy * vec_ref[6:7, :], yh, rstd)
        dr2_ref[...] = dr2
        st_ref[0:1, :] += _colsum(err * err)
        st_ref[1:2, :] += _colsum(dy * yh)
        st_ref[2:3, :] += _colsum(dy)
        st_ref[3:4, :] += _colsum(dr2 * f)

    return pl.pallas_call(
        body, name="ffn_fwd", grid=(seq // tm,),
        in_specs=[_rows(tm, D), _rows(tm, D), _rows(tm, D), _acc((8, D)), _resident((N_SHARD, D, FH_SHARD)), _resident((FH, D))],
        out_specs=[_rows(tm, 2 * FH), _rows(tm, FH), _rows(tm, D), _rows(tm, D), _acc((8, D))],
        out_shape=[jax.ShapeDtypeStruct((seq, 2 * FH), F32), jax.ShapeDtypeStruct((seq, FH), BF16),
                   jax.ShapeDtypeStruct((seq, D), BF16), jax.ShapeDtypeStruct((seq, D), F32),
                   jax.ShapeDtypeStruct((8, D), F32)],
        compiler_params=_cp("arbitrary"),
    )(x, mix, tgt, vec, w_fi, w_fo)


def _ffn_bwd(dr2, gu, x, mix, vec, w_fi, w_fo, tm):
    seq = x.shape[0]

    def body(dr2_ref, gu_ref, x_ref, mix_ref, vec_ref, wi_ref, wo_ref, dff_ref, df_ref, dr1_ref, st_ref):
        @pl.when(pl.program_id(0) == 0)
        def _():
            st_ref[...] = jnp.zeros_like(st_ref)

        dr2 = dr2_ref[...]
        df = (dr2 * vec_ref[5:6, :]).astype(BF16)
        df_ref[...] = df
        dh2 = jnp.zeros((tm, D), F32)
        for hh, off, w in _ffn_chunks():
            cs = slice(hh * FH_SHARD + off, hh * FH_SHARD + off + w)
            cu = slice(FH + hh * FH_SHARD + off, FH + hh * FH_SHARD + off + w)
            da = _dot_nt(df, wo_ref[cs, :])
            g = gu_ref[:, cs]
            u = gu_ref[:, cu]
            sg = _sig(g)
            dg = (da * u * sg * (1.0 + g * (1.0 - sg))).astype(BF16)
            du = (da * g * sg).astype(BF16)
            dff_ref[:, cs] = dg
            dff_ref[:, cu] = du
            dh2 = dh2 + _dot_nt(dg, wi_ref[hh, :, off:off + w]) + _dot_nt(du, wi_ref[2 + hh, :, off:off + w])
        xh1, rstd1, _, xh2, rstd2 = _mid_recompute(x_ref, mix_ref, vec_ref)
        dxmid = _ln_bwd(dh2 * (1.0 + vec_ref[4:5, :]), xh2, rstd2) + ALPHA * dr2
        dr1 = _ln_bwd(dxmid * vec_ref[1:2, :], xh1, rstd1)
        dr1_ref[...] = dr1
        st_ref[0:1, :] += _colsum(dh2 * xh2)
        st_ref[1:2, :] += _colsum(dh2)
        st_ref[2:3, :] += _colsum(dxmid * xh1)
        st_ref[3:4, :] += _colsum(dxmid)
        st_ref[4:5, :] += _colsum(dr1 * mix_ref[...])

    return pl.pallas_call(
        body, name="ffn_bwd", grid=(seq // tm,),
        in_specs=[_rows(tm, D), _rows(tm, 2 * FH), _rows(tm, D), _rows(tm, D), _acc((8, D)),
                  _resident((N_SHARD, D, FH_SHARD)), _resident((FH, D))],
        out_specs=[_rows(tm, 2 * FH), _rows(tm, D), _rows(tm, D), _acc((8, D))],
        out_shape=[jax.ShapeDtypeStruct((seq, 2 * FH), BF16), jax.ShapeDtypeStruct((seq, D), BF16),
                   jax.ShapeDtypeStruct((seq, D), F32), jax.ShapeDtypeStruct((8, D), F32)],
        compiler_params=_cp("arbitrary"),
    )(dr2, gu, x, mix, vec, w_fi, w_fo)


def _mix_bwd(dr1, a, b, gab, uv, vec, gp, ws_stack, ws_stack_t, bias_full, w_a, w_b, w_o, tm):
    seq = dr1.shape[0]

    def body(dr1_ref, a_ref, b_ref, gab_ref, uv_ref, vec_ref, gp_ref, ws_ref, wst_ref, bias_ref, wa_ref, wb_ref, wo_ref,
             dmix_ref, da_ref, db_ref, dya_ref, dp_ref, dws_ref, dbs_ref, st_ref):
        @pl.when(pl.program_id(0) == 0)
        def _():
            dws_ref[...] = jnp.zeros_like(dws_ref)
            dbs_ref[...] = jnp.zeros_like(dbs_ref)
            st_ref[...] = jnp.zeros_like(st_ref)

        dmix = (dr1_ref[...] * vec_ref[0:1, :]).astype(BF16)
        dmix_ref[...] = dmix
        dmerged = _dot_nt(dmix, wo_ref[...])
        sa = _sig(gab_ref[:, :D])
        sb = _sig(gab_ref[:, D:])
        da = (dmerged * sa).astype(BF16)
        db = (dmerged * sb).astype(BF16)
        da_ref[...] = da
        db_ref[...] = db
        dp_ref[:, 2 * G_W:2 * G_W + D] = (dmerged * a_ref[...] * sa * (1.0 - sa)).astype(BF16)
        dp_ref[:, 2 * G_W + D:] = (dmerged * b_ref[...] * sb * (1.0 - sb)).astype(BF16)
        dya = jnp.zeros((tm, Q_W), F32)
        dyb = jnp.zeros((tm, G_W), F32)
        for s in range(N_SHARD):
            cs = slice(s * (D // N_SHARD), (s + 1) * (D // N_SHARD))
            dya = dya + _dot_nt(da[:, cs], wa_ref[s])
            dyb = dyb + _dot_nt(db[:, cs], wb_ref[s])
        dya_ref[...] = dya.astype(BF16)
        for c in range(tm // BLK):
            rs = slice(c * BLK, (c + 1) * BLK)
            u = uv_ref[rs, :G_W]
            vb = uv_ref[rs, G_W:]
            gu, tu, tv, vhat, rstd, vn, s, grp = _gmlp_chunk(u, vb, gp_ref, ws_ref, bias_ref)
            dyb_c = dyb[rs, :]
            ds = dyb_c * gu
            du = dyb_c * s * _gelu_grad(u, tu)
            dstack = jnp.concatenate([jnp.where(grp == g, ds, 0.0) for g in range(N_GRP)], axis=0).astype(BF16)
            dvn = _dot(wst_ref[...], dstack)
            dws_ref[...] += _dot_nt(dstack, vn)
            dbs_ref[...] += ds
            st_ref[0:1, :] += _colsum(dvn * vhat)
            st_ref[1:2, :] += _colsum(dvn)
            dgv = _ln_bwd(dvn * gp_ref[0:1, :], vhat, rstd)
            dvb = dgv * _gelu_grad(vb, tv)
            dp_ref[rs, :G_W] = du.astype(BF16)
            dp_ref[rs, G_W:2 * G_W] = dvb.astype(BF16)

    pw = 2 * G_W + 2 * D
    return pl.pallas_call(
        body, name="mix_bwd", grid=(seq // tm,),
        in_specs=[_rows(tm, D), _rows(tm, D), _rows(tm, D), _rows(tm, 2 * D), _rows(tm, 2 * G_W), _acc((8, D)), _acc((8, G_W)),
                  _resident((N_GRP * BLK, BLK)), _resident((BLK, N_GRP * BLK)), _acc((BLK, G_W)),
                  _resident((N_SHARD, Q_W, D // N_SHARD)), _resident((N_SHARD, G_W, D // N_SHARD)), _resident((D, D))],
        out_specs=[_rows(tm, D), _rows(tm, D), _rows(tm, D), _rows(tm, Q_W), _rows(tm, pw),
                   _acc((N_GRP * BLK, BLK)), _acc((BLK, G_W)), _acc((8, G_W))],
        out_shape=[jax.ShapeDtypeStruct((seq, D), BF16), jax.ShapeDtypeStruct((seq, D), BF16),
                   jax.ShapeDtypeStruct((seq, D), BF16), jax.ShapeDtypeStruct((seq, Q_W), BF16),
                   jax.ShapeDtypeStruct((seq, pw), BF16), jax.ShapeDtypeStruct((N_GRP * BLK, BLK), F32),
                   jax.ShapeDtypeStruct((BLK, G_W), F32), jax.ShapeDtypeStruct((8, G_W), F32)],
        compiler_params=_cp("arbitrary"),
    )(dr1, a, b, gab, uv, vec, gp, ws_stack, ws_stack_t, bias_full, w_a, w_b, w_o)


def _attn_bwd(q, kv, kvc, sink, dya):
    seq = q.shape[0]
    nb = seq // BLK
    n_ctx = kvc.shape[0]

    def body(q_ref, kvp_ref, kvm_ref, kvn_ref, kvc_ref, sink_ref, do_ref, dq_ref, dkv_ref, dkvc_ref, dsink_ref):
        n = pl.program_id(0)

        @pl.when(n == 0)
        def _():
            dkv_ref[...] = jnp.zeros_like(dkv_ref)
            dkvc_ref[...] = jnp.zeros_like(dkvc_ref)
            dsink_ref[...] = jnp.zeros_like(dsink_ref)

        do = do_ref[...]
        dqs, dks, dvs = [], [], []
        for hk in range(N_KV):
            q4, ks, vs, p, p_sink = _attn_pieces(q_ref, kvp_ref, kvm_ref, kvn_ref, kvc_ref, sink_ref, hk, n, nb)
            do4 = jnp.concatenate([do[:, (hk * GROUP + g) * HEAD:(hk * GROUP + g + 1) * HEAD] for g in range(GROUP)], axis=0)
            dp = [_dot_nt(do4, v) for v in vs]
            delta = jnp.zeros((GROUP * BLK, 1), F32)
            for t, d in zip(p, dp):
                delta = delta + jnp.sum(t * d, axis=-1, keepdims=True)
            ds = [(t * (d - delta) * SCALE).astype(BF16) for t, d in zip(p, dp)]
            dq4 = _dot(ds[0], ks[0])
            for t, k in zip(ds[1:], ks[1:]):
                dq4 = dq4 + _dot(t, k)
            dqs += [dq4[g * BLK:(g + 1) * BLK, :] for g in range(GROUP)]
            dks.append([_dot_tn(t, q4) for t in ds])
            dvs.append([_dot_tn(t.astype(BF16), do4) for t in p])
            ps = p_sink * delta
            for g in range(GROUP):
                h = hk * GROUP + g
                dsink_ref[h:h + 1, :] += jnp.broadcast_to(-jnp.sum(ps[g * BLK:(g + 1) * BLK, :], axis=0, keepdims=True), (1, LANES))
        dq_ref[...] = jnp.concatenate(dqs, axis=1)

        def piece(i):
            return jnp.concatenate([dks[0][i], dks[1][i], dvs[0][i], dvs[1][i]], axis=1)

        dkvc_ref[...] += piece(0)
        starts = (jnp.maximum(n - 1, 0), n, jnp.minimum(n + 1, nb - 1))
        for i, st in enumerate(starts):
            r = pl.ds(pl.multiple_of(st * BLK, BLK), BLK)
            dkv_ref[r, :] += piece(i + 1)

    return pl.pallas_call(
        body, name="attn_bwd", grid=(nb,),
        in_specs=[_rows(BLK, Q_W)] + _kv_specs(nb) + [_acc((n_ctx, 2 * KV_W)), pl.BlockSpec(memory_space=pltpu.SMEM),
                                                      _rows(BLK, Q_W)],
        out_specs=[_rows(BLK, Q_W), _acc((seq, 2 * KV_W)), _acc((n_ctx, 2 * KV_W)), _acc((8, LANES))],
        out_shape=[jax.ShapeDtypeStruct((seq, Q_W), F32), jax.ShapeDtypeStruct((seq, 2 * KV_W), F32),
                   jax.ShapeDtypeStruct((n_ctx, 2 * KV_W), F32), jax.ShapeDtypeStruct((8, LANES), F32)],
        compiler_params=_cp("arbitrary"),
    )(q, kv, kv, kv, kvc, sink, dya)


def _proj_bwd(dq, dkv, dpb, x, dr1, modx, w_in, cos, sin, tm):
    seq = x.shape[0]
    pw = IN_W - Q_W - 2 * KV_W

    def body(dq_ref, dkv_ref, dpb_ref, x_ref, dr1_ref, mod_ref, w_ref, cos_ref, sin_ref, dqkv_ref, gx_ref, st_ref):
        @pl.when(pl.program_id(0) == 0)
        def _():
            st_ref[...] = jnp.zeros_like(st_ref)

        cos1, sin1 = cos_ref[...], sin_ref[...]
        cos2 = jnp.concatenate([cos1, cos1], axis=1)
        sin2 = jnp.concatenate([sin1, sin1], axis=1)
        for j in range(Q_W // 256):
            cs = slice(256 * j, 256 * (j + 1))
            dqkv_ref[:, cs] = _unrope(dq_ref[:, cs], cos2, sin2).astype(BF16)
        dqkv_ref[:, Q_W:Q_W + KV_W] = _unrope(dkv_ref[:, :KV_W], cos1, sin1).astype(BF16)
        dqkv_ref[:, Q_W + KV_W:] = dkv_ref[:, KV_W:].astype(BF16)
        o = Q_W + 2 * KV_W
        dh = _dot_nt(dqkv_ref[...], w_ref[:, :o]) + _dot_nt(dpb_ref[...], w_ref[:, o:])
        xhat, rstd = _ln(x_ref[...])
        st_ref[0:1, :] += _colsum(dh)
        st_ref[1:2, :] += _colsum(dh * xhat)
        gx_ref[...] = _ln_bwd(dh * (1.0 + mod_ref[1:2, :]), xhat, rstd) + ALPHA * dr1_ref[...]

    return pl.pallas_call(
        body, name="proj_bwd", grid=(seq // tm,),
        in_specs=[_rows(tm, Q_W), _rows(tm, 2 * KV_W), _rows(tm, pw), _rows(tm, D), _rows(tm, D), _acc((8, D)),
                  _resident((D, IN_W)), _rows(tm, LANES), _rows(tm, LANES)],
        out_specs=[_rows(tm, Q_W + 2 * KV_W), _rows(tm, D), _acc((8, D))],
        out_shape=[jax.ShapeDtypeStruct((seq, Q_W + 2 * KV_W), BF16), jax.ShapeDtypeStruct((seq, D), F32),
                   jax.ShapeDtypeStruct((8, D), F32)],
        compiler_params=_cp("arbitrary"),
    )(dq, dkv, dpb, x, dr1, modx, w_in, cos, sin)


def _ctx_bwd(dkvc, ctx, hc, w_kv):
    n_ctx = ctx.shape[0]

    def body(dkvc_ref, ctx_ref, hc_ref, w_ref, dw_ref, st_ref):
        d = dkvc_ref[...].astype(BF16)
        dw_ref[...] = _dot_tn(hc_ref[...], d)
        dhc = _dot_nt(d, w_ref[...])
        xhat, _ = _ln(ctx_ref[...])
        st_ref[...] = jnp.zeros_like(st_ref)
        st_ref[0:1, :] = _colsum(dhc)
        st_ref[1:2, :] = _colsum(dhc * xhat)

    return pl.pallas_call(
        body, name="ctx_bwd", grid=(1,),
        in_specs=[_acc((n_ctx, 2 * KV_W)), _acc((n_ctx, D)), _acc((n_ctx, D)), _acc((D, 2 * KV_W))],
        out_specs=[_acc((D, 2 * KV_W)), _acc((8, D))],
        out_shape=[jax.ShapeDtypeStruct((D, 2 * KV_W), F32), jax.ShapeDtypeStruct((8, D), F32)],
        compiler_params=_cp("arbitrary"),
    )(dkvc, ctx, hc, w_kv)


def _tn_matmul(a, b, tn, name, out_dtype, shard_major=False, init=None, tk=512):
    t, ka = a.shape
    n = b.shape[1]
    tk = min(tk, t)
    nk = t // tk
    has_init = init is not None

    def body(*refs):
        if has_init:
            a_ref, b_ref, i_ref, o_ref, acc_ref = refs
        else:
            a_ref, b_ref, o_ref, acc_ref = refs
        k = pl.program_id(1)

        @pl.when(k == 0)
        def _():
            acc_ref[...] = i_ref[...] if has_init else jnp.zeros_like(acc_ref)

        acc_ref[...] += _dot_tn(a_ref[...], b_ref[...])

        @pl.when(k == nk - 1)
        def _():
            o_ref[...] = acc_ref[...].astype(out_dtype)

    in_specs = [pl.BlockSpec((tk, ka), lambda j, k: (k, 0)), pl.BlockSpec((tk, tn), lambda j, k: (k, j))]
    args = [a, b]
    if has_init:
        in_specs.append(pl.BlockSpec((ka, tn), lambda j, k: (0, j)))
        args.append(init)
    if shard_major:
        out_spec = pl.BlockSpec((None, ka, tn), lambda j, k: (j, 0, 0))
        out_shape = jax.ShapeDtypeStruct((n // tn, ka, tn), out_dtype)
    else:
        out_spec = pl.BlockSpec((ka, tn), lambda j, k: (0, j))
        out_shape = jax.ShapeDtypeStruct((ka, n), out_dtype)
    return pl.pallas_call(
        body, name=name, grid=(n // tn, nk), in_specs=in_specs, out_specs=out_spec, out_shape=out_shape,
        scratch_shapes=[pltpu.VMEM((ka, tn), F32)],
        compiler_params=_cp("arbitrary", "arbitrary"),
    )(*args)


ADA_TILE = 512


def _ada_fwd(sc_all, w_ada):
    cs = w_ada.shape[1]

    def body(s_ref, w_ref, o_ref):
        o_ref[...] = _dot(s_ref[...].astype(BF16), w_ref[...].astype(BF16))

    return pl.pallas_call(
        body, name="ada_fwd", grid=(cs // ADA_TILE,),
        in_specs=[_acc((16, D)), pl.BlockSpec((D, ADA_TILE), lambda j: (0, j))],
        out_specs=pl.BlockSpec((16, ADA_TILE), lambda j: (0, j)),
        out_shape=jax.ShapeDtypeStruct((16, cs), F32),
        compiler_params=_cp("arbitrary"),
    )(sc_all, w_ada)


def _ada_bwd(sc_all_t, dm_all, dmc, w_ada):
    cs = w_ada.shape[1]

    def body(st_ref, dm_ref, dmc_ref, w_ref, gw_ref, part_ref):
        @pl.when(pl.program_id(0) == 0)
        def _():
            part_ref[...] = jnp.zeros_like(part_ref)

        gw_ref[...] = _dot(st_ref[...].astype(BF16), dm_ref[...].astype(BF16))
        part_ref[...] += _dot_nt(dmc_ref[...].astype(BF16), w_ref[...].astype(BF16))

    return pl.pallas_call(
        body, name="ada_bwd", grid=(cs // ADA_TILE,),
        in_specs=[_acc((D, 16)), pl.BlockSpec((16, ADA_TILE), lambda j: (0, j)), pl.BlockSpec((8, ADA_TILE), lambda j: (0, j)),
                  pl.BlockSpec((D, ADA_TILE), lambda j: (0, j))],
        out_specs=[pl.BlockSpec((D, ADA_TILE), lambda j: (0, j)), _acc((8, D))],
        out_shape=[jax.ShapeDtypeStruct((D, cs), F32), jax.ShapeDtypeStruct((8, D), F32)],
        compiler_params=_cp("arbitrary"),
    )(sc_all_t, dm_all, dmc, w_ada)


def _sum8(x, name, tr=256):
    _, r, c = x.shape
    tr = min(tr, r)
    while r % tr:
        tr -= 16

    def body(x_ref, o_ref):
        acc = x_ref[0].astype(F32)
        for i in range(1, N_DEV):
            acc = acc + x_ref[i].astype(F32)
        o_ref[...] = acc

    return pl.pallas_call(
        body, name=name, grid=(r // tr,),
        in_specs=[pl.BlockSpec((N_DEV, tr, c), lambda i: (0, i, 0))],
        out_specs=pl.BlockSpec((tr, c), lambda i: (i, 0)),
        out_shape=jax.ShapeDtypeStruct((r, c), F32),
        compiler_params=_cp("arbitrary"),
    )(x)


def _adamw(w, g, m, v, name):
    r, c = w.shape
    tr = min(256, r)
    while r % tr:
        tr -= 8

    def body(w_ref, g_ref, m_ref, v_ref, d_ref, nm_ref, nv_ref):
        gr = g_ref[...]
        nm = ADAM_B1 * m_ref[...] + (1.0 - ADAM_B1) * gr
        nv = ADAM_B2 * v_ref[...] + (1.0 - ADAM_B2) * (gr * gr)
        m_hat = nm / (1.0 - ADAM_B1 ** ADAM_STEP)
        v_hat = nv / (1.0 - ADAM_B2 ** ADAM_STEP)
        d_ref[...] = -ADAM_LR * (m_hat / (jnp.sqrt(v_hat) + ADAM_EPS) + ADAM_WD * w_ref[...])
        nm_ref[...] = nm
        nv_ref[...] = nv

    spec = pl.BlockSpec((tr, c), lambda i: (i, 0))
    shp = jax.ShapeDtypeStruct((r, c), F32)
    return pl.pallas_call(
        body, name=name, grid=(r // tr,), in_specs=[spec] * 4, out_specs=[spec] * 3, out_shape=[shp] * 3,
        compiler_params=_cp("arbitrary"),
    )(w, g, m, v)


def _my_pos():
    return lax.axis_index("x"), lax.axis_index("y"), lax.axis_index("c")


N_COPY = 7


class _Gather:
    def __init__(self, x_refs, out_refs, send_sems, recv_sems, local_sems):
        self.x_refs, self.out_refs = x_refs, out_refs
        self.send_sems, self.recv_sems, self.local_sems = send_sems, recv_sems, local_sems
        x, y, c = _my_pos()
        self.c = c
        self.me, self.sibling = (x, y, c), (x, y, 1 - c)
        self.chips = [(1 - x, y), (x, 1 - y), (1 - x, 1 - y)]

    def _copy(self, a, k, block, to, from_input=False):
        px, py, pc = block
        rows = self.out_refs[a].at[4 * px + 2 * py + pc]
        return pltpu.make_async_remote_copy(
            src_ref=self.x_refs[a] if from_input else rows, dst_ref=rows,
            send_sem=self.send_sems.at[a * N_COPY + k], recv_sem=self.recv_sems.at[a * N_COPY + k],
            device_id=to, device_id_type=MESH)

    def _mine(self, a):
        x, y, c = self.me
        return pltpu.make_async_copy(self.x_refs[a], self.out_refs[a].at[4 * x + 2 * y + c], self.local_sems.at[a])

    def start(self):
        n = len(self.x_refs)
        for a in range(n):
            self._mine(a).start()
        for a in range(n):
            self._copy(a, 0, self.me, self.sibling, from_input=True).start()
        for j, chip in enumerate(self.chips):
            for a in range(n):
                self._copy(a, 1 + j, self.me, (*chip, self.c), from_input=True).start()

    def finish(self):
        n = len(self.x_refs)
        c = self.c
        for j, chip in enumerate(self.chips):
            for a in range(n):
                self._copy(a, 1 + j, (*chip, c), self.me).wait_recv()
                self._copy(a, 4 + j, (*chip, c), self.sibling).start()
        for a in range(n):
            self._copy(a, 0, self.sibling, self.me).wait_recv()
        for j, chip in enumerate(self.chips):
            for a in range(n):
                self._copy(a, 4 + j, (*chip, 1 - c), self.me).wait_recv()
        for a in range(n):
            self._copy(a, 0, self.me, self.sibling, from_input=True).wait_send()
            for j, chip in enumerate(self.chips):
                self._copy(a, 1 + j, self.me, (*chip, c), from_input=True).wait_send()
                self._copy(a, 4 + j, (*chip, c), self.sibling).wait_send()
            self._mine(a).wait()


def _comm_scratch(n):
    return [pltpu.SemaphoreType.DMA((n * N_COPY,)), pltpu.SemaphoreType.DMA((n * N_COPY,)), pltpu.SemaphoreType.DMA((n,))]


def _all_gather(xs, name):
    n = len(xs)

    def body(*refs):
        g = _Gather(refs[:n], refs[n:2 * n], *refs[2 * n:])
        g.start()
        g.finish()

    return pl.pallas_call(
        body, name=name,
        out_shape=[jax.ShapeDtypeStruct((N_DEV,) + v.shape, v.dtype) for v in xs],
        in_specs=[pl.BlockSpec(memory_space=pl.ANY)] * n,
        out_specs=[pl.BlockSpec(memory_space=pl.ANY)] * n,
        scratch_shapes=_comm_scratch(n),
    )(*xs)


class _AllToAll:
    def __init__(self, x_refs, out_refs, send_sems, recv_sems, local_sems):
        self.x_refs, self.out_refs = x_refs, out_refs
        self.send_sems, self.recv_sems, self.local_sems = send_sems, recv_sems, local_sems
        self.pos = _my_pos()
        x, y, c = self.pos
        self.me = 4 * x + 2 * y + c

    def _peer(self, k):
        x, y, c = self.pos
        return (x ^ ((k >> 2) & 1), y ^ ((k >> 1) & 1), c ^ (k & 1))

    def _copy(self, a, k):
        p = self._peer(k)
        return pltpu.make_async_remote_copy(
            src_ref=self.x_refs[a].at[4 * p[0] + 2 * p[1] + p[2]], dst_ref=self.out_refs[a].at[self.me],
            send_sem=self.send_sems.at[a * N_COPY + k - 1], recv_sem=self.recv_sems.at[a * N_COPY + k - 1],
            device_id=p, device_id_type=MESH)

    def _mine(self, a):
        return pltpu.make_async_copy(self.x_refs[a].at[self.me], self.out_refs[a].at[self.me], self.local_sems.at[a])

    def start(self):
        for a in range(len(self.x_refs)):
            self._mine(a).start()
        for k in range(1, N_DEV):
            for a in range(len(self.x_refs)):
                self._copy(a, k).start()

    def finish(self):
        for a in range(len(self.x_refs)):
            for k in range(1, N_DEV):
                self._copy(a, k).wait_recv()
            for k in range(1, N_DEV):
                self._copy(a, k).wait_send()
            self._mine(a).wait()


def _all_to_all(blocks, name):
    n = len(blocks)

    def body(*refs):
        t = _AllToAll(refs[:n], refs[n:2 * n], *refs[2 * n:])
        t.start()
        t.finish()

    return pl.pallas_call(
        body, name=name,
        out_shape=[jax.ShapeDtypeStruct(v.shape, v.dtype) for v in blocks],
        in_specs=[pl.BlockSpec(memory_space=pl.ANY)] * n,
        out_specs=[pl.BlockSpec(memory_space=pl.ANY)] * n,
        scratch_shapes=_comm_scratch(n),
    )(*blocks)


def _sibling_exchange(xs, name):
    n = len(xs)

    def body(*refs):
        x_refs, out_refs = refs[:n], refs[n:2 * n]
        send_sems, recv_sems, local_sems = refs[2 * n:]
        x, y, c = _my_pos()

        def push(a, slot):
            return pltpu.make_async_remote_copy(
                src_ref=x_refs[a], dst_ref=out_refs[a].at[slot], send_sem=send_sems.at[a], recv_sem=recv_sems.at[a],
                device_id=(x, y, 1 - c), device_id_type=MESH)

        def mine(a):
            return pltpu.make_async_copy(x_refs[a], out_refs[a].at[c], local_sems.at[a])

        for a in range(n):
            mine(a).start()
            push(a, c).start()
        for a in range(n):
            push(a, 1 - c).wait_recv()
            push(a, c).wait_send()
            mine(a).wait()

    return pl.pallas_call(
        body, name=name,
        out_shape=[jax.ShapeDtypeStruct((2,) + v.shape, v.dtype) for v in xs],
        in_specs=[pl.BlockSpec(memory_space=pl.ANY)] * n,
        out_specs=[pl.BlockSpec(memory_space=pl.ANY)] * n,
        scratch_shapes=[pltpu.SemaphoreType.DMA((n,)), pltpu.SemaphoreType.DMA((n,)), pltpu.SemaphoreType.DMA((n,))],
    )(*xs)


def _row_tile(seq, want):
    return min(want, seq)


def _local_step(x, ctx, tgt, mod_x, mod_c, wb, sink, gmlp_g, gmlp_b, w_s, b_s, ln1_g, ln1_b, ln2_g, ln2_b):
    seq = x.shape[0]
    modx1 = jnp.concatenate([mod_x[0:2], jnp.zeros((6, D), F32)], axis=0)
    modc = jnp.concatenate([mod_c[0:2], jnp.zeros((6, D), F32)], axis=0)
    vec = jnp.concatenate([mod_x[2:3], ln1_g, ln1_b, mod_x[3:6], ln2_g, ln2_b], axis=0)
    gp = jnp.concatenate([gmlp_g, gmlp_b, jnp.zeros((6, G_W), F32)], axis=0)
    ws_stack = w_s.reshape(N_GRP * BLK, BLK).astype(BF16)
    ws_stack_t = jnp.transpose(w_s, (2, 0, 1)).reshape(BLK, N_GRP * BLK).astype(BF16)
    bias_full = jnp.repeat(b_s.T, GRP_D, axis=1)
    cos, sin = _rope_tables(seq)
    w_in = wb["w_in"]
    w_kv = w_in[:, Q_W:Q_W + 2 * KV_W]
    tm_big = _row_tile(seq, 512)
    tm_ffn = _row_tile(seq, 256)

    hc, kvc = _ctx_fwd(ctx, modc, w_kv)
    h, q, kv, uv, gab = _proj_fwd(x, modx1, w_in, cos, sin, tm_big)
    ya = _attn_fwd(q, kv, kvc, sink)
    a, b, mix, merged, yb = _mix_fwd(uv, gab, ya, gp, ws_stack, bias_full, wb["w_a"], wb["w_b"], wb["w_o"], tm_big)
    gu, act, h2, dr2, st5 = _ffn_fwd(x, mix, tgt, vec, wb["w_fi"], wb["w_fo"], tm_ffn)

    dff, df, dr1, st5b = _ffn_bwd(dr2, gu, x, mix, vec, wb["w_fi"], wb["w_fo"], tm_ffn)
    g_w_fo = _tn_matmul(act, df, 512, "tn_w_ffn_out", BF16)
    g_w_fi = _tn_matmul(h2, dff, FH_SHARD, "tn_w_ffn_in", BF16, shard_major=True)
    dmix, da, db, dya, dpb, dws, dbs_full, st4 = _mix_bwd(
        dr1, a, b, gab, uv, vec, gp, ws_stack, ws_stack_t, bias_full, wb["w_a"], wb["w_b"], wb["w_o"], _row_tile(seq, 256))
    g_w_o = _tn_matmul(merged, dmix, 1024, "tn_w_out", BF16)
    g_w_a = _tn_matmul(ya, da, D // N_SHARD, "tn_w_branch_a", BF16, shard_major=True, tk=2048)
    g_w_b = _tn_matmul(yb, db, D // N_SHARD, "tn_w_branch_b", BF16, shard_major=True, tk=2048)
    dq, dkv, dkvc, dsink = _attn_bwd(q, kv, kvc, sink, dya)
    g_wkv_ctx, st0 = _ctx_bwd(dkvc, ctx, hc, w_kv)
    dqkv, grad_x, st1 = _proj_bwd(dq, dkv, dpb, x, dr1, modx1, w_in, cos, sin, tm_big)
    o = Q_W + 2 * KV_W
    init = jnp.pad(g_wkv_ctx, ((0, 0), (Q_W, 0)))
    g_w_in = jnp.concatenate([_tn_matmul(h, dqkv, o, "tn_w_in_qkv", F32, init=init),
                              _tn_matmul(h, dpb, 1536, "tn_w_in_rest", F32)], axis=1)
    g_w_in = g_w_in.reshape(D, N_SHARD, IN_W // N_SHARD).transpose(1, 0, 2).astype(BF16)

    dmod_x = jnp.concatenate([st1[0:2], st5b[4:5], st5b[1:2], st5b[0:1], st5[3:4]], axis=0)
    dmod_c = st0[0:2]
    small = dict(
        loss_cols=st5[0], sink=dsink[:, 0], gmlp_g=st4[0], gmlp_b=st4[1],
        w_s=dws.reshape(N_GRP, BLK, BLK), b_s=jnp.sum(dbs_full.reshape(BLK, N_GRP, GRP_D), axis=2).T,
        ln1_g=st5b[2], ln1_b=st5b[3], ln2_g=st5[1], ln2_b=st5[2])
    big = dict(w_in=g_w_in, w_a=g_w_a, w_b=g_w_b, w_o=g_w_o, w_fi=g_w_fi, w_fo=g_w_fo)
    return grad_x, dmod_x, dmod_c, small, {n: _eighths(v) for n, v in big.items()}


BIG = ("w_in", "w_a", "w_b", "w_o", "w_fi", "w_fo")


def _half_of_shard(shard, c):
    r = shard.shape[0]
    return lax.dynamic_slice_in_dim(shard, c * (r // 2), r // 2, axis=0)


def _eighths(v):
    rows = v.shape[-2] * (v.shape[0] if v.ndim == 3 else 1)
    return v.reshape(N_DEV, rows // N_DEV, v.shape[-1])


def _to_lanes(flat):
    return flat.reshape(-1, LANES)


SMALL_ORDER = ("loss_cols", "sink", "gmlp_g", "gmlp_b", "w_s", "b_s", "ln1_g", "ln1_b", "ln2_g", "ln2_b")


def kernel(x, c, ctx, c_ctx, w_ada, b_ada, w_in, attn_sink, gmlp_ln_g, gmlp_ln_b, w_spatial, b_spatial, w_branch_a, w_branch_b, w_out, ln1_g, ln1_b, w_ffn_in, w_ffn_out, ln2_g, ln2_b, loss_target, m_c_ctx, m_w_ada, m_b_ada, m_w_in, m_attn_sink, m_gmlp_ln_g, m_gmlp_ln_b, m_w_spatial, m_b_spatial, m_w_branch_a, m_w_branch_b, m_w_out, m_ln1_g, m_ln1_b, m_w_ffn_in, m_w_ffn_out, m_ln2_g, m_ln2_b, v_c_ctx, v_w_ada, v_b_ada, v_w_in, v_attn_sink, v_gmlp_ln_g, v_gmlp_ln_b, v_w_spatial, v_b_spatial, v_w_branch_a, v_w_branch_b, v_w_out, v_ln1_g, v_ln1_b, v_w_ffn_in, v_w_ffn_out, v_ln2_g, v_ln2_b):
    mx, my, mc = _my_pos()
    me = 4 * mx + 2 * my + mc
    chip = 2 * mx + my
    shards = dict(w_in=w_in[0], w_a=w_branch_a[0], w_b=w_branch_b[0], w_o=w_out[0], w_fi=w_ffn_in[0], w_fo=w_ffn_out[0])

    gathered = _all_gather([_half_of_shard(shards[n], mc).astype(BF16) for n in BIG], "gather_weights")
    wb = {}
    for n, g in zip(BIG, gathered):
        r, cdim = shards[n].shape
        if n == "w_in":
            wb[n] = g.reshape(N_SHARD, r, cdim).transpose(1, 0, 2).reshape(r, N_SHARD * cdim)
        elif n in ("w_o", "w_fo"):
            wb[n] = g.reshape(N_SHARD * r, cdim)
        else:
            wb[n] = g.reshape(N_SHARD, r, cdim)

    c_rows = jnp.concatenate([c, jnp.zeros((7, D), F32)], axis=0)
    c_all = _all_gather([c_rows], "gather_c")[0][:, 0, :]
    cc = jnp.concatenate([c_all, c_ctx[None, :], jnp.zeros((7, D), F32)], axis=0)
    sig_cc = jax.nn.sigmoid(cc)
    sc_all = cc * sig_cc
    mod_shard = _ada_fwd(sc_all, w_ada[0])
    mod_g = _all_gather([mod_shard], "gather_mod")[0]
    mod_all = jnp.concatenate([mod_g[2 * s] for s in range(4)], axis=1) + b_ada
    mod_x = lax.dynamic_slice_in_dim(mod_all, me, 1, axis=0).reshape(6, D)
    mod_c = mod_all[8].reshape(6, D)[0:2]

    grad_x, dmod_x, dmod_c, small, big = _local_step(
        x[0], ctx[0], loss_target[0], mod_x, mod_c, wb, attn_sink, gmlp_ln_g, gmlp_ln_b, w_spatial[0], b_spatial[0],
        ln1_g, ln1_b, ln2_g, ln2_b)

    recv = _all_to_all([big[n] for n in BIG], "scatter_grads")
    red = [_sum8(v, "sum_grads_" + n) for n, v in zip(BIG, recv)]
    both = _sibling_exchange(red, "exchange_grads")
    g_shard = {n: v.reshape(shards[n].shape) for n, v in zip(BIG, both)}

    parts = [small[n].reshape(-1) for n in SMALL_ORDER] + [dmod_c.reshape(-1), dmod_x.reshape(-1)]
    sizes = [p.shape[0] for p in parts]
    flat = jnp.concatenate(parts)
    pad = (-flat.shape[0]) % (16 * LANES)
    flat = jnp.concatenate([flat, jnp.zeros((pad,), F32)])
    gath = _all_gather([_to_lanes(flat)], "gather_small")[0]
    tot = _sum8(gath, "sum_small").reshape(-1)
    offs = [0]
    for s_ in sizes:
        offs.append(offs[-1] + s_)
    tots = {n: tot[offs[i]:offs[i + 1]] for i, n in enumerate(SMALL_ORDER)}
    dmod_c_tot = tot[offs[-3]:offs[-2]]
    dmod_x_sum = tot[offs[-2]:offs[-1]]
    dmod_x_all = gath.reshape(N_DEV, -1)[:, offs[-2]:offs[-1]]
    loss = 0.5 * jnp.sum(tots["loss_cols"]) / D

    dmod_c_full = jnp.concatenate([dmod_c_tot, jnp.zeros((4 * D,), F32)])
    g_b_ada = (dmod_x_sum + dmod_c_full)[None, :]
    dm_rows = jnp.concatenate([dmod_x_all, dmod_c_full[None, :], jnp.zeros((7, 6 * D), F32)], axis=0)
    cs = w_ada.shape[2]
    dm_shard = lax.dynamic_slice_in_dim(dm_rows, chip * cs, cs, axis=1)
    dmc_shard = jnp.concatenate([dm_shard[8:9], jnp.zeros((7, cs), F32)], axis=0)
    g_w_ada, part = _ada_bwd(sc_all.T, dm_shard, dmc_shard, w_ada[0])
    part = part * (mc == 0).astype(F32)
    part_all = _all_gather([_to_lanes(part.reshape(-1))], "gather_c_ctx")[0]
    dsc = _sum8(part_all, "sum_c_ctx").reshape(8, D)[0]
    sig_ctx = sig_cc[8]
    g_c_ctx = dsc * (sig_ctx * (1.0 + c_ctx * (1.0 - sig_ctx)))

    grads = dict(c_ctx=g_c_ctx, w_ada=g_w_ada[None], b_ada=g_b_ada, w_in=g_shard["w_in"][None],
                 attn_sink=tots["sink"][None, :], gmlp_ln_g=tots["gmlp_g"][None, :], gmlp_ln_b=tots["gmlp_b"][None, :],
                 w_spatial=tots["w_s"].reshape(w_spatial.shape), b_spatial=tots["b_s"].reshape(b_spatial.shape),
                 w_branch_a=g_shard["w_a"][None], w_branch_b=g_shard["w_b"][None], w_out=g_shard["w_o"][None],
                 ln1_g=tots["ln1_g"][None, :], ln1_b=tots["ln1_b"][None, :],
                 w_ffn_in=g_shard["w_fi"][None], w_ffn_out=g_shard["w_fo"][None],
                 ln2_g=tots["ln2_g"][None, :], ln2_b=tots["ln2_b"][None, :])
    weights = dict(c_ctx=c_ctx, w_ada=w_ada, b_ada=b_ada, w_in=w_in, attn_sink=attn_sink, gmlp_ln_g=gmlp_ln_g,
                   gmlp_ln_b=gmlp_ln_b, w_spatial=w_spatial, b_spatial=b_spatial, w_branch_a=w_branch_a,
                   w_branch_b=w_branch_b, w_out=w_out, ln1_g=ln1_g, ln1_b=ln1_b, w_ffn_in=w_ffn_in, w_ffn_out=w_ffn_out,
                   ln2_g=ln2_g, ln2_b=ln2_b)
    ms = dict(c_ctx=m_c_ctx, w_ada=m_w_ada, b_ada=m_b_ada, w_in=m_w_in, attn_sink=m_attn_sink, gmlp_ln_g=m_gmlp_ln_g,
              gmlp_ln_b=m_gmlp_ln_b, w_spatial=m_w_spatial, b_spatial=m_b_spatial, w_branch_a=m_w_branch_a,
              w_branch_b=m_w_branch_b, w_out=m_w_out, ln1_g=m_ln1_g, ln1_b=m_ln1_b, w_ffn_in=m_w_ffn_in,
              w_ffn_out=m_w_ffn_out, ln2_g=m_ln2_g, ln2_b=m_ln2_b)
    vs = dict(c_ctx=v_c_ctx, w_ada=v_w_ada, b_ada=v_b_ada, w_in=v_w_in, attn_sink=v_attn_sink, gmlp_ln_g=v_gmlp_ln_g,
              gmlp_ln_b=v_gmlp_ln_b, w_spatial=v_w_spatial, b_spatial=v_b_spatial, w_branch_a=v_w_branch_a,
              w_branch_b=v_w_branch_b, w_out=v_w_out, ln1_g=v_ln1_g, ln1_b=v_ln1_b, w_ffn_in=v_w_ffn_in,
              w_ffn_out=v_w_ffn_out, ln2_g=v_ln2_g, ln2_b=v_ln2_b)
    order = list(weights)
    large = ("w_ada", "w_in", "w_branch_a", "w_branch_b", "w_out", "w_ffn_in", "w_ffn_out")
    delta, new_m, new_v = {}, {}, {}
    for n in large:
        d_, m_, v_ = _adamw(weights[n][0], grads[n][0], ms[n][0], vs[n][0], "adamw_" + n)
        delta[n], new_m[n], new_v[n] = d_[None], m_[None], v_[None]
    rest = [n for n in order if n not in large]

    def pack(d):
        f = jnp.concatenate([d[n].reshape(-1) for n in rest])
        return _to_lanes(jnp.concatenate([f, jnp.zeros(((-f.shape[0]) % (8 * LANES),), F32)]))

    d_, m_, v_ = _adamw(pack(weights), pack(grads), pack(ms), pack(vs), "adamw_small")
    off = 0
    for n in rest:
        size = weights[n].size
        for out, src in ((delta, d_), (new_m, m_), (new_v, v_)):
            out[n] = src.reshape(-1)[off:off + size].reshape(weights[n].shape)
        off += size

    return (loss, grad_x[None], *[grads[n] for n in order], *[delta[n] for n in order],
            *[new_m[n] for n in order], *[new_v[n] for n in order])
```

```python
import functools
import math

import jax
import jax.numpy as jnp
from jax import lax
from jax.experimental import pallas as pl
from jax.experimental.pallas import tpu as pltpu

F32 = jnp.float32
BF16 = jnp.bfloat16

D = 1024
HEAD = 64
N_KV = 2
GROUP = 4
Q_W = 512
KV_W = 128
G_W = 512
BLK = 128
N_GRP = 8
GRP_D = 64
FH = 2816
IN_W = 3840
GRID_W = 64
ROPE_BASE = 10000.0
LN_EPS = 1e-5
NEG = -1e30
ALPHA = (2 * 1) ** 0.25
SCALE = HEAD ** -0.5
GELU_K = math.sqrt(2.0 / math.pi)
GELU_A = 0.044715
ADAM_LR = 0.001
ADAM_B1 = 0.9
ADAM_B2 = 0.999
ADAM_EPS = 1e-08
ADAM_WD = 0.01
ADAM_STEP = 10
N_DEV = 8
N_SHARD = 4
FH_SHARD = FH // 2
LANES = 128
VMEM_LIMIT = 56 * 1024 * 1024
MESH = pl.DeviceIdType.MESH


def _cp(*sem):
    return pltpu.CompilerParams(dimension_semantics=sem, vmem_limit_bytes=VMEM_LIMIT)


def _resident(shape):
    return pl.BlockSpec(shape, lambda *_: (0,) * len(shape), pipeline_mode=pl.Buffered(1))


def _rows(tm, width):
    return pl.BlockSpec((tm, width), lambda i: (i, 0))


def _acc(shape):
    return pl.BlockSpec(shape, lambda *_: (0,) * len(shape))


def _dot(a, b):
    return jnp.dot(a, b, preferred_element_type=F32)


def _dot_nt(a, b):
    return lax.dot_general(a, b, (((1,), (1,)), ((), ())), preferred_element_type=F32)


def _dot_tn(a, b):
    return lax.dot_general(a, b, (((0,), (0,)), ((), ())), preferred_element_type=F32)


def _ln(x):
    mu = jnp.mean(x, axis=-1, keepdims=True)
    xc = x - mu
    var = jnp.mean(xc * xc, axis=-1, keepdims=True)
    rstd = lax.rsqrt(var + LN_EPS)
    return xc * rstd, rstd


def _ln_bwd(dxhat, xhat, rstd):
    return (dxhat - jnp.mean(dxhat, axis=-1, keepdims=True)
            - xhat * jnp.mean(dxhat * xhat, axis=-1, keepdims=True)) * rstd


def _sig(x):
    return 1.0 / (1.0 + jnp.exp(-x))


def _gelu(x):
    t = jnp.tanh(GELU_K * (x + GELU_A * x * x * x))
    return 0.5 * x * (1.0 + t), t


def _gelu_grad(x, t):
    return 0.5 * (1.0 + t) + 0.5 * x * (1.0 - t * t) * GELU_K * (1.0 + 3.0 * GELU_A * x * x)


def _colsum(v):
    return jnp.sum(v, axis=0, keepdims=True)


def _partner(x):
    w = x.shape[1]
    lane = lax.broadcasted_iota(jnp.int32, x.shape, 1)
    return jnp.where((lane & 31) < 16, pltpu.roll(x, w - 16, 1), pltpu.roll(x, 16, 1))


def _rope(x, cos, sin):
    return x * cos + _partner(x) * sin


def _unrope(g, cos, sin):
    return g * cos + _partner(g * sin)


def _rope_tables(seq):
    inv = ROPE_BASE ** (-jnp.arange(HEAD // 4, dtype=F32) / (HEAD // 4))
    pos = jnp.arange(seq, dtype=jnp.int32)
    ar = (pos // GRID_W).astype(F32)[:, None] * inv
    ac = (pos % GRID_W).astype(F32)[:, None] * inv
    cos = jnp.concatenate([jnp.cos(ar), jnp.cos(ar), jnp.cos(ac), jnp.cos(ac)], axis=-1)
    sin = jnp.concatenate([-jnp.sin(ar), jnp.sin(ar), -jnp.sin(ac), jnp.sin(ac)], axis=-1)
    return jnp.tile(cos, (1, LANES // HEAD)), jnp.tile(sin, (1, LANES // HEAD))


def _ctx_fwd(ctx, modc, w_kv):
    n_ctx = ctx.shape[0]

    def body(ctx_ref, mod_ref, w_ref, hc_ref, kvc_ref):
        xhat, _ = _ln(ctx_ref[...])
        hc = (xhat * (1.0 + mod_ref[1:2, :]) + mod_ref[0:1, :]).astype(BF16)
        hc_ref[...] = hc
        kvc_ref[...] = _dot(hc, w_ref[...]).astype(BF16)

    return pl.pallas_call(
        body, name="ctx_fwd", grid=(1,),
        in_specs=[_acc((n_ctx, D)), _acc((8, D)), _acc((D, 2 * KV_W))],
        out_specs=[_acc((n_ctx, D)), _acc((n_ctx, 2 * KV_W))],
        out_shape=[jax.ShapeDtypeStruct((n_ctx, D), BF16), jax.ShapeDtypeStruct((n_ctx, 2 * KV_W), BF16)],
        compiler_params=_cp("arbitrary"),
    )(ctx, modc, w_kv)


def _host_start(step, comm):
    if comm is not None:
        @pl.when(step == 0)
        def _():
            comm.start()


def _host_finish(step, last, comm):
    if comm is not None:
        @pl.when(step == last)
        def _():
            comm.finish()


def _proj_fwd(x, modx, w_in, cos, sin, tm, gather=()):
    seq = x.shape[0]
    ng = len(gather)

    def body(x_ref, mod_ref, w_ref, cos_ref, sin_ref, *rest):
        h_ref, q_ref, kv_ref, uv_ref, gab_ref = rest[ng:ng + 5]
        comm = _Gather(rest[:ng], rest[ng + 5:2 * ng + 5], *rest[2 * ng + 5:]) if ng else None
        _host_start(pl.program_id(0), comm)
        xhat, _ = _ln(x_ref[...])
        h = (xhat * (1.0 + mod_ref[1:2, :]) + mod_ref[0:1, :]).astype(BF16)
        h_ref[...] = h
        cos1, sin1 = cos_ref[...], sin_ref[...]
        cos2 = jnp.concatenate([cos1, cos1], axis=1)
        sin2 = jnp.concatenate([sin1, sin1], axis=1)
        for j in range(Q_W // 256):
            t = _dot(h, w_ref[:, 256 * j:256 * (j + 1)])
            q_ref[:, 256 * j:256 * (j + 1)] = _rope(t, cos2, sin2).astype(BF16)
        t = _dot(h, w_ref[:, Q_W:Q_W + 2 * KV_W])
        kv_ref[:, :KV_W] = _rope(t[:, :KV_W], cos1, sin1).astype(BF16)
        kv_ref[:, KV_W:] = t[:, KV_W:].astype(BF16)
        o = Q_W + 2 * KV_W
        for j in range(2):
            uv_ref[:, G_W * j:G_W * (j + 1)] = _dot(h, w_ref[:, o + G_W * j:o + G_W * (j + 1)])
        o += 2 * G_W
        for j in range(4):
            gab_ref[:, 512 * j:512 * (j + 1)] = _dot(h, w_ref[:, o + 512 * j:o + 512 * (j + 1)])
        _host_finish(pl.program_id(0), seq // tm - 1, comm)

    out = pl.pallas_call(
        body, name="proj_fwd", grid=(seq // tm,),
        in_specs=[_rows(tm, D), _acc((8, D)), _resident((D, IN_W)), _rows(tm, LANES), _rows(tm, LANES)] + _comm_specs(ng),
        out_specs=[_rows(tm, D), _rows(tm, Q_W), _rows(tm, 2 * KV_W), _rows(tm, 2 * G_W), _rows(tm, 2 * D)] + _comm_specs(ng),
        out_shape=[jax.ShapeDtypeStruct((seq, D), BF16), jax.ShapeDtypeStruct((seq, Q_W), BF16),
                   jax.ShapeDtypeStruct((seq, 2 * KV_W), BF16), jax.ShapeDtypeStruct((seq, 2 * G_W), F32),
                   jax.ShapeDtypeStruct((seq, 2 * D), F32)] + _gathered_shapes(gather),
        scratch_shapes=_comm_scratch(ng) if ng else [],
        compiler_params=_cp("arbitrary"),
    )(x, modx, w_in, cos, sin, *gather)
    return out[:5], out[5:]


def _attn_pieces(q_ref, kvp_ref, kvm_ref, kvn_ref, kvc_ref, sink_ref, hk, n, nb):
    q = q_ref[...]
    q4 = jnp.concatenate([q[:, (hk * GROUP + g) * HEAD:(hk * GROUP + g + 1) * HEAD] for g in range(GROUP)], axis=0)
    ks = [r[:, hk * HEAD:(hk + 1) * HEAD] for r in (kvc_ref, kvp_ref, kvm_ref, kvn_ref)]
    vs = [r[:, KV_W + hk * HEAD:KV_W + (hk + 1) * HEAD] for r in (kvc_ref, kvp_ref, kvm_ref, kvn_ref)]
    rows = GROUP * BLK
    qi = lax.broadcasted_iota(jnp.int32, (rows, BLK), 0) & (BLK - 1)
    kj = lax.broadcasted_iota(jnp.int32, (rows, BLK), 1)
    s = [_dot_nt(q4, k) * SCALE for k in ks]
    s[1] = jnp.where((kj >= qi) & (n > 0), s[1], NEG)
    s[3] = jnp.where((kj <= qi) & (n < nb - 1), s[3], NEG)
    rg = lax.broadcasted_iota(jnp.int32, (rows, 1), 0) >> 7
    sink_v = jnp.full((rows, 1), sink_ref[0, hk * GROUP], F32)
    for g in range(1, GROUP):
        sink_v = jnp.where(rg == g, sink_ref[0, hk * GROUP + g], sink_v)
    m = sink_v
    for t in s:
        m = jnp.maximum(m, jnp.max(t, axis=-1, keepdims=True))
    e = [jnp.exp(t - m) for t in s]
    e_sink = jnp.exp(sink_v - m)
    denom = e_sink
    for t in e:
        denom = denom + jnp.sum(t, axis=-1, keepdims=True)
    inv = 1.0 / denom
    p = [t * inv for t in e]
    return q4, ks, vs, p, e_sink * inv


def _kv_specs(nb):
    return [pl.BlockSpec((BLK, 2 * KV_W), lambda n: (jnp.maximum(n - 1, 0), 0)),
            pl.BlockSpec((BLK, 2 * KV_W), lambda n: (n, 0)),
            pl.BlockSpec((BLK, 2 * KV_W), lambda n: (jnp.minimum(n + 1, nb - 1), 0))]


def _attn_fwd(q, kv, kvc, sink):
    seq = q.shape[0]
    nb = seq // BLK
    n_ctx = kvc.shape[0]

    def body(q_ref, kvp_ref, kvm_ref, kvn_ref, kvc_ref, sink_ref, o_ref):
        n = pl.program_id(0)
        outs = []
        for hk in range(N_KV):
            _, _, vs, p, _ = _attn_pieces(q_ref, kvp_ref, kvm_ref, kvn_ref, kvc_ref, sink_ref, hk, n, nb)
            o = _dot(p[0].astype(BF16), vs[0])
            for t, v in zip(p[1:], vs[1:]):
                o = o + _dot(t.astype(BF16), v)
            outs += [o[g * BLK:(g + 1) * BLK, :] for g in range(GROUP)]
        o_ref[...] = jnp.concatenate(outs, axis=1).astype(BF16)

    return pl.pallas_call(
        body, name="attn_fwd", grid=(nb,),
        in_specs=[_rows(BLK, Q_W)] + _kv_specs(nb) + [_acc((n_ctx, 2 * KV_W)), pl.BlockSpec(memory_space=pltpu.SMEM)],
        out_specs=_rows(BLK, Q_W),
        out_shape=jax.ShapeDtypeStruct((seq, Q_W), BF16),
        compiler_params=_cp("arbitrary"),
    )(q, kv, kv, kv, kvc, sink)


def _gmlp_chunk(u, vb, gp_ref, ws_ref, bias_ref):
    gu, tu = _gelu(u)
    gv, tv = _gelu(vb)
    vhat, rstd = _ln(gv)
    vn = (vhat * gp_ref[0:1, :] + gp_ref[1:2, :]).astype(BF16)
    r = _dot(ws_ref[...], vn)
    grp = lax.broadcasted_iota(jnp.int32, (BLK, G_W), 1) >> 6
    s = bias_ref[...]
    for g in range(N_GRP):
        s = s + jnp.where(grp == g, r[g * BLK:(g + 1) * BLK, :], 0.0)
    return gu, tu, tv, vhat, rstd, vn, s, grp


def _mix_fwd(uv, gab, ya, gp, ws_stack, bias_full, w_a, w_b, w_o, tm):
    seq = uv.shape[0]

    def body(uv_ref, gab_ref, ya_ref, gp_ref, ws_ref, bias_ref, wa_ref, wb_ref, wo_ref,
             a_ref, b_ref, mix_ref, merged_ref, yb_ref):
        for c in range(tm // BLK):
            rs = slice(c * BLK, (c + 1) * BLK)
            gu, _, _, _, _, _, s, _ = _gmlp_chunk(uv_ref[rs, :G_W], uv_ref[rs, G_W:], gp_ref, ws_ref, bias_ref)
            yb_ref[rs, :] = (gu * s).astype(BF16)
        ya = ya_ref[...]
        yb = yb_ref[...]
        for s in range(N_SHARD):
            cs = slice(s * (D // N_SHARD), (s + 1) * (D // N_SHARD))
            a_ref[:, cs] = _dot(ya, wa_ref[s])
            b_ref[:, cs] = _dot(yb, wb_ref[s])
        merged = (_sig(gab_ref[:, :D]) * a_ref[...] + _sig(gab_ref[:, D:]) * b_ref[...]).astype(BF16)
        merged_ref[...] = merged
        mix_ref[...] = _dot(merged, wo_ref[...])

    return pl.pallas_call(
        body, name="mix_fwd", grid=(seq // tm,),
        in_specs=[_rows(tm, 2 * G_W), _rows(tm, 2 * D), _rows(tm, Q_W), _acc((8, G_W)),
                  _resident((N_GRP * BLK, BLK)), _acc((BLK, G_W)),
                  _resident((N_SHARD, Q_W, D // N_SHARD)), _resident((N_SHARD, G_W, D // N_SHARD)), _resident((D, D))],
        out_specs=[_rows(tm, D), _rows(tm, D), _rows(tm, D), _rows(tm, D), _rows(tm, G_W)],
        out_shape=[jax.ShapeDtypeStruct((seq, D), F32), jax.ShapeDtypeStruct((seq, D), F32),
                   jax.ShapeDtypeStruct((seq, D), F32), jax.ShapeDtypeStruct((seq, D), BF16),
                   jax.ShapeDtypeStruct((seq, G_W), BF16)],
        compiler_params=_cp("arbitrary"),
    )(uv, gab, ya, gp, ws_stack, bias_full, w_a, w_b, w_o)


FFN_CHUNK = 512


def _ffn_chunks():
    out = []
    for hh in range(2):
        off = 0
        while off < FH_SHARD:
            w = min(FFN_CHUNK, FH_SHARD - off)
            out.append((hh, off, w))
            off += w
    return out


def _mid_recompute(x_ref, mix_ref, vec_ref):
    r1 = ALPHA * x_ref[...] + vec_ref[0:1, :] * mix_ref[...]
    xh1, rstd1 = _ln(r1)
    xmid = xh1 * vec_ref[1:2, :] + vec_ref[2:3, :]
    xh2, rstd2 = _ln(xmid)
    return xh1, rstd1, xmid, xh2, rstd2


def _ffn_fwd(x, mix, tgt, vec, w_fi, w_fo, tm):
    seq = x.shape[0]

    def body(x_ref, mix_ref, tgt_ref, vec_ref, wi_ref, wo_ref, gu_ref, act_ref, h2_ref, dr2_ref, st_ref):
        @pl.when(pl.program_id(0) == 0)
        def _():
            st_ref[...] = jnp.zeros_like(st_ref)

        _, _, xmid, xh2, _ = _mid_recompute(x_ref, mix_ref, vec_ref)
        h2 = (xh2 * (1.0 + vec_ref[4:5, :]) + vec_ref[3:4, :]).astype(BF16)
        h2_ref[...] = h2
        f = jnp.zeros((tm, D), F32)
        for hh, off, w in _ffn_chunks():
            cs = slice(hh * FH_SHARD + off, hh * FH_SHARD + off + w)
            cu = slice(FH + hh * FH_SHARD + off, FH + hh * FH_SHARD + off + w)
            g = _dot(h2, wi_ref[hh, :, off:off + w])
            u = _dot(h2, wi_ref[2 + hh, :, off:off + w])
            gu_ref[:, cs] = g
            gu_ref[:, cu] = u
            a = (g * _sig(g) * u).astype(BF16)
            act_ref[:, cs] = a
            f = f + _dot(a, wo_ref[cs, :])
        r2 = ALPHA * xmid + vec_ref[5:6, :] * f
        yh, rstd = _ln(r2)
        y = yh * vec_ref[6:7, :] + vec_ref[7:8, :]
        err = y - tgt_ref[...]
        dy = err / D
        dr2 = _ln_bwd(dy * vec_ref[6:7, :], yh, rstd)
        dr2_ref[...] = dr2
        st_ref[0:1, :] += _colsum(err * err)
        st_ref[1:2, :] += _colsum(dy * yh)
        st_ref[2:3, :] += _colsum(dy)
        st_ref[3:4, :] += _colsum(dr2 * f)

    return pl.pallas_call(
        body, name="ffn_fwd", grid=(seq // tm,),
        in_specs=[_rows(tm, D), _rows(tm, D), _rows(tm, D), _acc((8, D)), _resident((N_SHARD, D, FH_SHARD)), _resident((FH, D))],
        out_specs=[_rows(tm, 2 * FH), _rows(tm, FH), _rows(tm, D), _rows(tm, D), _acc((8, D))],
        out_shape=[jax.ShapeDtypeStruct((seq, 2 * FH), F32), jax.ShapeDtypeStruct((seq, FH), BF16),
                   jax.ShapeDtypeStruct((seq, D), BF16), jax.ShapeDtypeStruct((seq, D), F32),
                   jax.ShapeDtypeStruct((8, D), F32)],
        compiler_params=_cp("arbitrary"),
    )(x, mix, tgt, vec, w_fi, w_fo)


def _ffn_bwd(dr2, gu, x, mix, vec, w_fi, w_fo, tm):
    seq = x.shape[0]

    def body(dr2_ref, gu_ref, x_ref, mix_ref, vec_ref, wi_ref, wo_ref, dff_ref, df_ref, dr1_ref, st_ref):
        @pl.when(pl.program_id(0) == 0)
        def _():
            st_ref[...] = jnp.zeros_like(st_ref)

        dr2 = dr2_ref[...]
        df = (dr2 * vec_ref[5:6, :]).astype(BF16)
        df_ref[...] = df
        dh2 = jnp.zeros((tm, D), F32)
        for hh, off, w in _ffn_chunks():
            cs = slice(hh * FH_SHARD + off, hh * FH_SHARD + off + w)
            cu = slice(FH + hh * FH_SHARD + off, FH + hh * FH_SHARD + off + w)
            da = _dot_nt(df, wo_ref[cs, :])
            g = gu_ref[:, cs]
            u = gu_ref[:, cu]
            sg = _sig(g)
            dg = (da * u * sg * (1.0 + g * (1.0 - sg))).astype(BF16)
            du = (da * g * sg).astype(BF16)
            dff_ref[:, cs] = dg
            dff_ref[:, cu] = du
            dh2 = dh2 + _dot_nt(dg, wi_ref[hh, :, off:off + w]) + _dot_nt(du, wi_ref[2 + hh, :, off:off + w])
        xh1, rstd1, _, xh2, rstd2 = _mid_recompute(x_ref, mix_ref, vec_ref)
        dxmid = _ln_bwd(dh2 * (1.0 + vec_ref[4:5, :]), xh2, rstd2) + ALPHA * dr2
        dr1 = _ln_bwd(dxmid * vec_ref[1:2, :], xh1, rstd1)
        dr1_ref[...] = dr1
        st_ref[0:1, :] += _colsum(dh2 * xh2)
        st_ref[1:2, :] += _colsum(dh2)
        st_ref[2:3, :] += _colsum(dxmid * xh1)
        st_ref[3:4, :] += _colsum(dxmid)
        st_ref[4:5, :] += _colsum(dr1 * mix_ref[...])

    return pl.pallas_call(
        body, name="ffn_bwd", grid=(seq // tm,),
        in_specs=[_rows(tm, D), _rows(tm, 2 * FH), _rows(tm, D), _rows(tm, D), _acc((8, D)),
                  _resident((N_SHARD, D, FH_SHARD)), _resident((FH, D))],
        out_specs=[_rows(tm, 2 * FH), _rows(tm, D), _rows(tm, D), _acc((8, D))],
        out_shape=[jax.ShapeDtypeStruct((seq, 2 * FH), BF16), jax.ShapeDtypeStruct((seq, D), BF16),
                   jax.ShapeDtypeStruct((seq, D), F32), jax.ShapeDtypeStruct((8, D), F32)],
        compiler_params=_cp("arbitrary"),
    )(dr2, gu, x, mix, vec, w_fi, w_fo)


def _mix_bwd(dr1, a, b, gab, uv, vec, gp, ws_stack, ws_stack_t, bias_full, w_a, w_b, w_o, tm):
    seq = dr1.shape[0]

    def body(dr1_ref, a_ref, b_ref, gab_ref, uv_ref, vec_ref, gp_ref, ws_ref, wst_ref, bias_ref, wa_ref, wb_ref, wo_ref,
             dmix_ref, da_ref, db_ref, dya_ref, dp_ref, dws_ref, dbs_ref, st_ref):
        @pl.when(pl.program_id(0) == 0)
        def _():
            dws_ref[...] = jnp.zeros_like(dws_ref)
            dbs_ref[...] = jnp.zeros_like(dbs_ref)
            st_ref[...] = jnp.zeros_like(st_ref)

        dmix = (dr1_ref[...] * vec_ref[0:1, :]).astype(BF16)
        dmix_ref[...] = dmix
        dmerged = _dot_nt(dmix, wo_ref[...])
        sa = _sig(gab_ref[:, :D])
        sb = _sig(gab_ref[:, D:])
        da = (dmerged * sa).astype(BF16)
        db = (dmerged * sb).astype(BF16)
        da_ref[...] = da
        db_ref[...] = db
        dp_ref[:, 2 * G_W:2 * G_W + D] = (dmerged * a_ref[...] * sa * (1.0 - sa)).astype(BF16)
        dp_ref[:, 2 * G_W + D:] = (dmerged * b_ref[...] * sb * (1.0 - sb)).astype(BF16)
        dya = jnp.zeros((tm, Q_W), F32)
        dyb = jnp.zeros((tm, G_W), F32)
        for s in range(N_SHARD):
            cs = slice(s * (D // N_SHARD), (s + 1) * (D // N_SHARD))
            dya = dya + _dot_nt(da[:, cs], wa_ref[s])
            dyb = dyb + _dot_nt(db[:, cs], wb_ref[s])
        dya_ref[...] = dya.astype(BF16)
        for c in range(tm // BLK):
            rs = slice(c * BLK, (c + 1) * BLK)
            u = uv_ref[rs, :G_W]
            vb = uv_ref[rs, G_W:]
            gu, tu, tv, vhat, rstd, vn, s, grp = _gmlp_chunk(u, vb, gp_ref, ws_ref, bias_ref)
            dyb_c = dyb[rs, :]
            ds = dyb_c * gu
            du = dyb_c * s * _gelu_grad(u, tu)
            dstack = jnp.concatenate([jnp.where(grp == g, ds, 0.0) for g in range(N_GRP)], axis=0).astype(BF16)
            dvn = _dot(wst_ref[...], dstack)
            dws_ref[...] += _dot_nt(dstack, vn)
            dbs_ref[...] += ds
            st_ref[0:1, :] += _colsum(dvn * vhat)
            st_ref[1:2, :] += _colsum(dvn)
            dgv = _ln_bwd(dvn * gp_ref[0:1, :], vhat, rstd)
            dvb = dgv * _gelu_grad(vb, tv)
            dp_ref[rs, :G_W] = du.astype(BF16)
            dp_ref[rs, G_W:2 * G_W] = dvb.astype(BF16)

    pw = 2 * G_W + 2 * D
    return pl.pallas_call(
        body, name="mix_bwd", grid=(seq // tm,),
        in_specs=[_rows(tm, D), _rows(tm, D), _rows(tm, D), _rows(tm, 2 * D), _rows(tm, 2 * G_W), _acc((8, D)), _acc((8, G_W)),
                  _resident((N_GRP * BLK, BLK)), _resident((BLK, N_GRP * BLK)), _acc((BLK, G_W)),
                  _resident((N_SHARD, Q_W, D // N_SHARD)), _resident((N_SHARD, G_W, D // N_SHARD)), _resident((D, D))],
        out_specs=[_rows(tm, D), _rows(tm, D), _rows(tm, D), _rows(tm, Q_W), _rows(tm, pw),
                   _acc((N_GRP * BLK, BLK)), _acc((BLK, G_W)), _acc((8, G_W))],
        out_shape=[jax.ShapeDtypeStruct((seq, D), BF16), jax.ShapeDtypeStruct((seq, D), BF16),
                   jax.ShapeDtypeStruct((seq, D), BF16), jax.ShapeDtypeStruct((seq, Q_W), BF16),
                   jax.ShapeDtypeStruct((seq, pw), BF16), jax.ShapeDtypeStruct((N_GRP * BLK, BLK), F32),
                   jax.ShapeDtypeStruct((BLK, G_W), F32), jax.ShapeDtypeStruct((8, G_W), F32)],
        compiler_params=_cp("arbitrary"),
    )(dr1, a, b, gab, uv, vec, gp, ws_stack, ws_stack_t, bias_full, w_a, w_b, w_o)


def _attn_bwd(q, kv, kvc, sink, dya, scatter=()):
    seq = q.shape[0]
    nb = seq // BLK
    n_ctx = kvc.shape[0]
    ns = len(scatter)

    def body(q_ref, kvp_ref, kvm_ref, kvn_ref, kvc_ref, sink_ref, do_ref, *rest):
        dq_ref, dkv_ref, dkvc_ref, dsink_ref = rest[ns:ns + 4]
        comm = _AllToAll(rest[:ns], rest[ns + 4:2 * ns + 4], *rest[2 * ns + 4:]) if ns else None
        n = pl.program_id(0)
        _host_start(n, comm)

        @pl.when(n == 0)
        def _():
            dkv_ref[...] = jnp.zeros_like(dkv_ref)
            dkvc_ref[...] = jnp.zeros_like(dkvc_ref)
            dsink_ref[...] = jnp.zeros_like(dsink_ref)

        do = do_ref[...]
        dqs, dks, dvs = [], [], []
        for hk in range(N_KV):
            q4, ks, vs, p, p_sink = _attn_pieces(q_ref, kvp_ref, kvm_ref, kvn_ref, kvc_ref, sink_ref, hk, n, nb)
            do4 = jnp.concatenate([do[:, (hk * GROUP + g) * HEAD:(hk * GROUP + g + 1) * HEAD] for g in range(GROUP)], axis=0)
            dp = [_dot_nt(do4, v) for v in vs]
            delta = jnp.zeros((GROUP * BLK, 1), F32)
            for t, d in zip(p, dp):
                delta = delta + jnp.sum(t * d, axis=-1, keepdims=True)
            ds = [(t * (d - delta) * SCALE).astype(BF16) for t, d in zip(p, dp)]
            dq4 = _dot(ds[0], ks[0])
            for t, k in zip(ds[1:], ks[1:]):
                dq4 = dq4 + _dot(t, k)
            dqs += [dq4[g * BLK:(g + 1) * BLK, :] for g in range(GROUP)]
            dks.append([_dot_tn(t, q4) for t in ds])
            dvs.append([_dot_tn(t.astype(BF16), do4) for t in p])
            ps = p_sink * delta
            for g in range(GROUP):
                h = hk * GROUP + g
                dsink_ref[h:h + 1, :] += jnp.broadcast_to(-jnp.sum(ps[g * BLK:(g + 1) * BLK, :], axis=0, keepdims=True), (1, LANES))
        dq_ref[...] = jnp.concatenate(dqs, axis=1)

        def piece(i):
            return jnp.concatenate([dks[0][i], dks[1][i], dvs[0][i], dvs[1][i]], axis=1)

        dkvc_ref[...] += piece(0)
        starts = (jnp.maximum(n - 1, 0), n, jnp.minimum(n + 1, nb - 1))
        for i, st in enumerate(starts):
            r = pl.ds(pl.multiple_of(st * BLK, BLK), BLK)
            dkv_ref[r, :] += piece(i + 1)
        _host_finish(n, nb - 1, comm)

    out = pl.pallas_call(
        body, name="attn_bwd", grid=(nb,),
        in_specs=[_rows(BLK, Q_W)] + _kv_specs(nb) + [_acc((n_ctx, 2 * KV_W)), pl.BlockSpec(memory_space=pltpu.SMEM),
                                                      _rows(BLK, Q_W)] + _comm_specs(ns),
        out_specs=[_rows(BLK, Q_W), _acc((seq, 2 * KV_W)), _acc((n_ctx, 2 * KV_W)), _acc((8, LANES))] + _comm_specs(ns),
        out_shape=[jax.ShapeDtypeStruct((seq, Q_W), F32), jax.ShapeDtypeStruct((seq, 2 * KV_W), F32),
                   jax.ShapeDtypeStruct((n_ctx, 2 * KV_W), F32), jax.ShapeDtypeStruct((8, LANES), F32)]
        + [jax.ShapeDtypeStruct(v.shape, v.dtype) for v in scatter],
        scratch_shapes=_comm_scratch(ns) if ns else [],
        compiler_params=_cp("arbitrary"),
    )(q, kv, kv, kv, kvc, sink, dya, *scatter)
    return out[:4], out[4:]


def _proj_bwd(dq, dkv, dpb, x, dr1, modx, w_in, cos, sin, tm, scatter=()):
    seq = x.shape[0]
    pw = IN_W - Q_W - 2 * KV_W
    ns = len(scatter)

    def body(dq_ref, dkv_ref, dpb_ref, x_ref, dr1_ref, mod_ref, w_ref, cos_ref, sin_ref, *rest):
        dqkv_ref, gx_ref, st_ref = rest[ns:ns + 3]
        comm = _AllToAll(rest[:ns], rest[ns + 3:2 * ns + 3], *rest[2 * ns + 3:]) if ns else None
        _host_start(pl.program_id(0), comm)

        @pl.when(pl.program_id(0) == 0)
        def _():
            st_ref[...] = jnp.zeros_like(st_ref)

        cos1, sin1 = cos_ref[...], sin_ref[...]
        cos2 = jnp.concatenate([cos1, cos1], axis=1)
        sin2 = jnp.concatenate([sin1, sin1], axis=1)
        for j in range(Q_W // 256):
            cs = slice(256 * j, 256 * (j + 1))
            dqkv_ref[:, cs] = _unrope(dq_ref[:, cs], cos2, sin2).astype(BF16)
        dqkv_ref[:, Q_W:Q_W + KV_W] = _unrope(dkv_ref[:, :KV_W], cos1, sin1).astype(BF16)
        dqkv_ref[:, Q_W + KV_W:] = dkv_ref[:, KV_W:].astype(BF16)
        o = Q_W + 2 * KV_W
        dh = _dot_nt(dqkv_ref[...], w_ref[:, :o]) + _dot_nt(dpb_ref[...], w_ref[:, o:])
        xhat, rstd = _ln(x_ref[...])
        st_ref[0:1, :] += _colsum(dh)
        st_ref[1:2, :] += _colsum(dh * xhat)
        gx_ref[...] = _ln_bwd(dh * (1.0 + mod_ref[1:2, :]), xhat, rstd) + ALPHA * dr1_ref[...]
        _host_finish(pl.program_id(0), seq // tm - 1, comm)

    out = pl.pallas_call(
        body, name="proj_bwd", grid=(seq // tm,),
        in_specs=[_rows(tm, Q_W), _rows(tm, 2 * KV_W), _rows(tm, pw), _rows(tm, D), _rows(tm, D), _acc((8, D)),
                  _resident((D, IN_W)), _rows(tm, LANES), _rows(tm, LANES)] + _comm_specs(ns),
        out_specs=[_rows(tm, Q_W + 2 * KV_W), _rows(tm, D), _acc((8, D))] + _comm_specs(ns),
        out_shape=[jax.ShapeDtypeStruct((seq, Q_W + 2 * KV_W), BF16), jax.ShapeDtypeStruct((seq, D), F32),
                   jax.ShapeDtypeStruct((8, D), F32)] + [jax.ShapeDtypeStruct(v.shape, v.dtype) for v in scatter],
        scratch_shapes=_comm_scratch(ns) if ns else [],
        compiler_params=_cp("arbitrary"),
    )(dq, dkv, dpb, x, dr1, modx, w_in, cos, sin, *scatter)
    return out[:3], out[3:]


def _ctx_bwd(dkvc, ctx, hc, w_kv):
    n_ctx = ctx.shape[0]

    def body(dkvc_ref, ctx_ref, hc_ref, w_ref, dw_ref, st_ref):
        d = dkvc_ref[...].astype(BF16)
        dw_ref[...] = _dot_tn(hc_ref[...], d)
        dhc = _dot_nt(d, w_ref[...])
        xhat, _ = _ln(ctx_ref[...])
        st_ref[...] = jnp.zeros_like(st_ref)
        st_ref[0:1, :] = _colsum(dhc)
        st_ref[1:2, :] = _colsum(dhc * xhat)

    return pl.pallas_call(
        body, name="ctx_bwd", grid=(1,),
        in_specs=[_acc((n_ctx, 2 * KV_W)), _acc((n_ctx, D)), _acc((n_ctx, D)), _acc((D, 2 * KV_W))],
        out_specs=[_acc((D, 2 * KV_W)), _acc((8, D))],
        out_shape=[jax.ShapeDtypeStruct((D, 2 * KV_W), F32), jax.ShapeDtypeStruct((8, D), F32)],
        compiler_params=_cp("arbitrary"),
    )(dkvc, ctx, hc, w_kv)


def _tn_matmul(a, b, tn, name, out_dtype, shard_major=False, init=None, tk=512):
    t, ka = a.shape
    n = b.shape[1]
    tk = min(tk, t)
    nk = t // tk
    has_init = init is not None

    def body(*refs):
        if has_init:
            a_ref, b_ref, i_ref, o_ref, acc_ref = refs
        else:
            a_ref, b_ref, o_ref, acc_ref = refs
        k = pl.program_id(1)

        @pl.when(k == 0)
        def _():
            acc_ref[...] = i_ref[...] if has_init else jnp.zeros_like(acc_ref)

        acc_ref[...] += _dot_tn(a_ref[...], b_ref[...])

        @pl.when(k == nk - 1)
        def _():
            o_ref[...] = acc_ref[...].astype(out_dtype)

    in_specs = [pl.BlockSpec((tk, ka), lambda j, k: (k, 0)), pl.BlockSpec((tk, tn), lambda j, k: (k, j))]
    args = [a, b]
    if has_init:
        in_specs.append(pl.BlockSpec((ka, tn), lambda j, k: (0, j)))
        args.append(init)
    if shard_major:
        out_spec = pl.BlockSpec((None, ka, tn), lambda j, k: (j, 0, 0))
        out_shape = jax.ShapeDtypeStruct((n // tn, ka, tn), out_dtype)
    else:
        out_spec = pl.BlockSpec((ka, tn), lambda j, k: (0, j))
        out_shape = jax.ShapeDtypeStruct((ka, n), out_dtype)
    return pl.pallas_call(
        body, name=name, grid=(n // tn, nk), in_specs=in_specs, out_specs=out_spec, out_shape=out_shape,
        scratch_shapes=[pltpu.VMEM((ka, tn), F32)],
        compiler_params=_cp("arbitrary", "arbitrary"),
    )(*args)


ADA_TILE = 512


def _ada_fwd(sc_all, w_ada):
    cs = w_ada.shape[1]

    def body(s_ref, w_ref, o_ref):
        o_ref[...] = _dot(s_ref[...].astype(BF16), w_ref[...].astype(BF16))

    return pl.pallas_call(
        body, name="ada_fwd", grid=(cs // ADA_TILE,),
        in_specs=[_acc((16, D)), pl.BlockSpec((D, ADA_TILE), lambda j: (0, j))],
        out_specs=pl.BlockSpec((16, ADA_TILE), lambda j: (0, j)),
        out_shape=jax.ShapeDtypeStruct((16, cs), F32),
        compiler_params=_cp("arbitrary"),
    )(sc_all, w_ada)


def _ada_bwd(sc_all_t, dm_all, dmc, w_ada):
    cs = w_ada.shape[1]

    def body(st_ref, dm_ref, dmc_ref, w_ref, gw_ref, part_ref):
        @pl.when(pl.program_id(0) == 0)
        def _():
            part_ref[...] = jnp.zeros_like(part_ref)

        gw_ref[...] = _dot(st_ref[...].astype(BF16), dm_ref[...].astype(BF16))
        part_ref[...] += _dot_nt(dmc_ref[...].astype(BF16), w_ref[...].astype(BF16))

    return pl.pallas_call(
        body, name="ada_bwd", grid=(cs // ADA_TILE,),
        in_specs=[_acc((D, 16)), pl.BlockSpec((16, ADA_TILE), lambda j: (0, j)), pl.BlockSpec((8, ADA_TILE), lambda j: (0, j)),
                  pl.BlockSpec((D, ADA_TILE), lambda j: (0, j))],
        out_specs=[pl.BlockSpec((D, ADA_TILE), lambda j: (0, j)), _acc((8, D))],
        out_shape=[jax.ShapeDtypeStruct((D, cs), F32), jax.ShapeDtypeStruct((8, D), F32)],
        compiler_params=_cp("arbitrary"),
    )(sc_all_t, dm_all, dmc, w_ada)


def _sum8(x, name, tr=256):
    _, r, c = x.shape
    tr = min(tr, r)
    while r % tr:
        tr -= 16

    def body(x_ref, o_ref):
        acc = x_ref[0].astype(F32)
        for i in range(1, N_DEV):
            acc = acc + x_ref[i].astype(F32)
        o_ref[...] = acc

    return pl.pallas_call(
        body, name=name, grid=(r // tr,),
        in_specs=[pl.BlockSpec((N_DEV, tr, c), lambda i: (0, i, 0))],
        out_specs=pl.BlockSpec((tr, c), lambda i: (i, 0)),
        out_shape=jax.ShapeDtypeStruct((r, c), F32),
        compiler_params=_cp("arbitrary"),
    )(x)


def _sum_blocks(recv, src, me, name, tr=256):
    _, r, c = recv.shape
    tr = min(tr, r)
    while r % tr:
        tr -= 16

    def body(me_ref, recv_ref, own_ref, o_ref):
        acc = own_ref[...].astype(F32)
        for k in range(1, N_DEV):
            acc = acc + recv_ref[me_ref[0] ^ k].astype(F32)
        o_ref[...] = acc

    return pl.pallas_call(
        body, name=name,
        grid_spec=pltpu.PrefetchScalarGridSpec(
            num_scalar_prefetch=1, grid=(r // tr,),
            in_specs=[pl.BlockSpec((N_DEV, tr, c), lambda i, me_ref: (0, i, 0)),
                      pl.BlockSpec((None, tr, c), lambda i, me_ref: (me_ref[0], i, 0))],
            out_specs=pl.BlockSpec((tr, c), lambda i, me_ref: (i, 0))),
        out_shape=jax.ShapeDtypeStruct((r, c), F32),
        compiler_params=_cp("arbitrary"),
    )(me, recv, src)


def _adamw(w, g, m, v, name):
    r, c = w.shape
    tr = r if r * c <= 256 * 1024 else min(256, r)
    while r % tr:
        tr -= 8

    def body(w_ref, g_ref, m_ref, v_ref, d_ref, nm_ref, nv_ref):
        gr = g_ref[...]
        nm = ADAM_B1 * m_ref[...] + (1.0 - ADAM_B1) * gr
        nv = ADAM_B2 * v_ref[...] + (1.0 - ADAM_B2) * (gr * gr)
        m_hat = nm / (1.0 - ADAM_B1 ** ADAM_STEP)
        v_hat = nv / (1.0 - ADAM_B2 ** ADAM_STEP)
        d_ref[...] = -ADAM_LR * (m_hat / (jnp.sqrt(v_hat) + ADAM_EPS) + ADAM_WD * w_ref[...])
        nm_ref[...] = nm
        nv_ref[...] = nv

    spec = pl.BlockSpec((tr, c), lambda i: (i, 0))
    shp = jax.ShapeDtypeStruct((r, c), F32)
    return pl.pallas_call(
        body, name=name, grid=(r // tr,), in_specs=[spec] * 4, out_specs=[spec] * 3, out_shape=[shp] * 3,
        compiler_params=_cp("arbitrary"),
    )(w, g, m, v)


def _my_pos():
    return lax.axis_index("x"), lax.axis_index("y"), lax.axis_index("c")


N_COPY = 7


class _Gather:
    def __init__(self, x_refs, out_refs, send_sems, recv_sems):
        self.x_refs, self.out_refs = x_refs, out_refs
        self.send_sems, self.recv_sems = send_sems, recv_sems
        x, y, c = _my_pos()
        self.c = c
        self.me, self.sibling = (x, y, c), (x, y, 1 - c)
        self.chips = [(1 - x, y), (x, 1 - y), (1 - x, 1 - y)]

    def _copy(self, a, k, block, to, from_input=False):
        px, py, pc = block
        rows = self.out_refs[a].at[4 * px + 2 * py + pc]
        return pltpu.make_async_remote_copy(
            src_ref=self.x_refs[a] if from_input else rows, dst_ref=rows,
            send_sem=self.send_sems.at[a * N_COPY + k], recv_sem=self.recv_sems.at[a * N_COPY + k],
            device_id=to, device_id_type=MESH)

    def start(self):
        n = len(self.x_refs)
        for a in range(n):
            self._copy(a, 0, self.me, self.sibling, from_input=True).start()
        for j, chip in enumerate(self.chips):
            for a in range(n):
                self._copy(a, 1 + j, self.me, (*chip, self.c), from_input=True).start()

    def finish(self):
        n = len(self.x_refs)
        c = self.c
        for j, chip in enumerate(self.chips):
            for a in range(n):
                self._copy(a, 1 + j, (*chip, c), self.me).wait_recv()
                self._copy(a, 4 + j, (*chip, c), self.sibling).start()
        for a in range(n):
            self._copy(a, 0, self.sibling, self.me).wait_recv()
        for j, chip in enumerate(self.chips):
            for a in range(n):
                self._copy(a, 4 + j, (*chip, 1 - c), self.me).wait_recv()
        for a in range(n):
            self._copy(a, 0, self.me, self.sibling, from_input=True).wait_send()
            for j, chip in enumerate(self.chips):
                self._copy(a, 1 + j, self.me, (*chip, c), from_input=True).wait_send()
                self._copy(a, 4 + j, (*chip, c), self.sibling).wait_send()


def _comm_scratch(n):
    return [pltpu.SemaphoreType.DMA((n * N_COPY,)), pltpu.SemaphoreType.DMA((n * N_COPY,))]


def _comm_specs(n):
    return [pl.BlockSpec(memory_space=pl.ANY)] * n


def _gathered_shapes(xs):
    return [jax.ShapeDtypeStruct((N_DEV,) + v.shape, v.dtype) for v in xs]


def _with_own(gathered, xs, me):
    return [lax.dynamic_update_index_in_dim(g, v, me, 0) for g, v in zip(gathered, xs)]


def _all_gather(xs, me, name):
    n = len(xs)

    def body(*refs):
        g = _Gather(refs[:n], refs[n:2 * n], *refs[2 * n:])
        g.start()
        g.finish()

    out = pl.pallas_call(
        body, name=name, out_shape=_gathered_shapes(xs), in_specs=_comm_specs(n), out_specs=_comm_specs(n),
        scratch_shapes=_comm_scratch(n),
    )(*xs)
    return _with_own(out, xs, me)


class _AllToAll:
    def __init__(self, x_refs, out_refs, send_sems, recv_sems):
        self.x_refs, self.out_refs = x_refs, out_refs
        self.send_sems, self.recv_sems = send_sems, recv_sems
        self.pos = _my_pos()
        x, y, c = self.pos
        self.me = 4 * x + 2 * y + c

    def _peer(self, k):
        x, y, c = self.pos
        return (x ^ ((k >> 2) & 1), y ^ ((k >> 1) & 1), c ^ (k & 1))

    def _copy(self, a, k):
        p = self._peer(k)
        return pltpu.make_async_remote_copy(
            src_ref=self.x_refs[a].at[4 * p[0] + 2 * p[1] + p[2]], dst_ref=self.out_refs[a].at[self.me],
            send_sem=self.send_sems.at[a * N_COPY + k - 1], recv_sem=self.recv_sems.at[a * N_COPY + k - 1],
            device_id=p, device_id_type=MESH)

    def start(self):
        for k in range(1, N_DEV):
            for a in range(len(self.x_refs)):
                self._copy(a, k).start()

    def finish(self):
        for a in range(len(self.x_refs)):
            for k in range(1, N_DEV):
                self._copy(a, k).wait_recv()
            for k in range(1, N_DEV):
                self._copy(a, k).wait_send()


def _all_to_all(blocks, name):
    n = len(blocks)

    def body(*refs):
        t = _AllToAll(refs[:n], refs[n:2 * n], *refs[2 * n:])
        t.start()
        t.finish()

    return pl.pallas_call(
        body, name=name, out_shape=[jax.ShapeDtypeStruct(v.shape, v.dtype) for v in blocks],
        in_specs=_comm_specs(n), out_specs=_comm_specs(n), scratch_shapes=_comm_scratch(n),
    )(*blocks)


def _sibling_exchange(xs, name):
    n = len(xs)

    def body(*refs):
        x_refs, out_refs = refs[:n], refs[n:2 * n]
        send_sems, recv_sems = refs[2 * n:]
        x, y, c = _my_pos()

        def push(a):
            return pltpu.make_async_remote_copy(
                src_ref=x_refs[a], dst_ref=out_refs[a], send_sem=send_sems.at[a], recv_sem=recv_sems.at[a],
                device_id=(x, y, 1 - c), device_id_type=MESH)

        for a in range(n):
            push(a).start()
        for a in range(n):
            push(a).wait_recv()
            push(a).wait_send()

    return pl.pallas_call(
        body, name=name, out_shape=[jax.ShapeDtypeStruct(v.shape, v.dtype) for v in xs],
        in_specs=_comm_specs(n), out_specs=_comm_specs(n),
        scratch_shapes=[pltpu.SemaphoreType.DMA((n,)), pltpu.SemaphoreType.DMA((n,))],
    )(*xs)


def _row_tile(seq, want):
    return min(want, seq)


def _local_step(x, ctx, tgt, mod_x, mod_c, wb, sink, gmlp_g, gmlp_b, w_s, b_s, ln1_g, ln1_b, ln2_g, ln2_b,
                later=None, me=None):
    seq = x.shape[0]
    on_mesh = me is not None
    modx1 = jnp.concatenate([mod_x[0:2], jnp.zeros((6, D), F32)], axis=0)
    modc = jnp.concatenate([mod_c[0:2], jnp.zeros((6, D), F32)], axis=0)
    vec = jnp.concatenate([mod_x[2:3], ln1_g, ln1_b, mod_x[3:6], ln2_g, ln2_b], axis=0)
    gp = jnp.concatenate([gmlp_g, gmlp_b, jnp.zeros((6, G_W), F32)], axis=0)
    ws_stack = w_s.reshape(N_GRP * BLK, BLK).astype(BF16)
    ws_stack_t = jnp.transpose(w_s, (2, 0, 1)).reshape(BLK, N_GRP * BLK).astype(BF16)
    bias_full = jnp.repeat(b_s.T, GRP_D, axis=1)
    cos, sin = _rope_tables(seq)
    w_in = wb["w_in"]
    w_kv = w_in[:, Q_W:Q_W + 2 * KV_W]
    tm_big = _row_tile(seq, 512)
    tm_ffn = _row_tile(seq, 256)

    hc, kvc = _ctx_fwd(ctx, modc, w_kv)
    halves = [later[n] for n in BIG[1:]] if on_mesh else []
    (h, q, kv, uv, gab), gathered = _proj_fwd(x, modx1, w_in, cos, sin, tm_big, gather=halves)
    if on_mesh:
        wb = dict(wb)
        for n, g in zip(BIG[1:], _with_own(gathered, halves, me)):
            wb[n] = g.reshape(-1, g.shape[2]) if n in ROW_SHARDED else g.reshape(N_SHARD, 2 * g.shape[1], g.shape[2])
    ya = _attn_fwd(q, kv, kvc, sink)
    a, b, mix, merged, yb = _mix_fwd(uv, gab, ya, gp, ws_stack, bias_full, wb["w_a"], wb["w_b"], wb["w_o"], tm_big)
    gu, act, h2, dr2, st5 = _ffn_fwd(x, mix, tgt, vec, wb["w_fi"], wb["w_fo"], tm_ffn)

    dff, df, dr1, st5b = _ffn_bwd(dr2, gu, x, mix, vec, wb["w_fi"], wb["w_fo"], tm_ffn)
    g_w_fo = _tn_matmul(act, df, 512, "tn_w_ffn_out", BF16)
    g_w_fi = _tn_matmul(h2, dff, FH_SHARD, "tn_w_ffn_in", BF16, shard_major=True)
    dmix, da, db, dya, dpb, dws, dbs_full, st4 = _mix_bwd(
        dr1, a, b, gab, uv, vec, gp, ws_stack, ws_stack_t, bias_full, wb["w_a"], wb["w_b"], wb["w_o"], _row_tile(seq, 256))
    g_w_o = _tn_matmul(merged, dmix, 1024, "tn_w_out", BF16)
    g_w_a = _tn_matmul(ya, da, D // N_SHARD, "tn_w_branch_a", BF16, shard_major=True, tk=2048)
    g_w_b = _tn_matmul(yb, db, D // N_SHARD, "tn_w_branch_b", BF16, shard_major=True, tk=2048)
    blocks = dict(w_fi=_eighths(g_w_fi), w_fo=_eighths(g_w_fo), w_o=_eighths(g_w_o), w_a=_eighths(g_w_a), w_b=_eighths(g_w_b))
    behind_attn = ("w_fi", "w_fo") if on_mesh else ()
    behind_proj = ("w_o", "w_a", "w_b") if on_mesh else ()
    (dq, dkv, dkvc, dsink), recv_attn = _attn_bwd(q, kv, kvc, sink, dya, scatter=[blocks[n] for n in behind_attn])
    g_wkv_ctx, st0 = _ctx_bwd(dkvc, ctx, hc, w_kv)
    (dqkv, grad_x, st1), recv_proj = _proj_bwd(dq, dkv, dpb, x, dr1, modx1, w_in, cos, sin, tm_big,
                                               scatter=[blocks[n] for n in behind_proj])
    o = Q_W + 2 * KV_W
    init = jnp.pad(g_wkv_ctx, ((0, 0), (Q_W, 0)))
    g_w_in = jnp.concatenate([_tn_matmul(h, dqkv, o, "tn_w_in_qkv", F32, init=init),
                              _tn_matmul(h, dpb, 1536, "tn_w_in_rest", F32)], axis=1)
    g_w_in = g_w_in.reshape(D, N_SHARD, IN_W // N_SHARD).transpose(1, 0, 2).astype(BF16)

    dmod_x = jnp.concatenate([st1[0:2], st5b[4:5], st5b[1:2], st5b[0:1], st5[3:4]], axis=0)
    dmod_c = st0[0:2]
    small = dict(
        loss_cols=st5[0], sink=dsink[:, 0], gmlp_g=st4[0], gmlp_b=st4[1],
        w_s=dws.reshape(N_GRP, BLK, BLK), b_s=jnp.sum(dbs_full.reshape(BLK, N_GRP, GRP_D), axis=2).T,
        ln1_g=st5b[2], ln1_b=st5b[3], ln2_g=st5[1], ln2_b=st5[2])
    blocks["w_in"] = _eighths(g_w_in)
    if not on_mesh:
        return grad_x, dmod_x, dmod_c, small, blocks
    recv = dict(zip(behind_attn + behind_proj, list(recv_attn) + list(recv_proj)))
    recv["w_in"] = _all_to_all([blocks["w_in"]], "scatter_grads_w_in")[0]
    me_arr = jnp.reshape(me, (1,)).astype(jnp.int32)
    summed = {n: _sum_blocks(recv[n], blocks[n], me_arr, "sum_grads_" + n) for n in BIG}
    return grad_x, dmod_x, dmod_c, small, summed


BIG = ("w_in", "w_a", "w_b", "w_o", "w_fi", "w_fo")
ROW_SHARDED = ("w_o", "w_fo")


def _half_of_shard(shard, c):
    r = shard.shape[0]
    return lax.dynamic_slice_in_dim(shard, c * (r // 2), r // 2, axis=0)


def _eighths(v):
    rows = v.shape[-2] * (v.shape[0] if v.ndim == 3 else 1)
    return v.reshape(N_DEV, rows // N_DEV, v.shape[-1])


def _to_lanes(flat):
    return flat.reshape(-1, LANES)


SMALL_ORDER = ("loss_cols", "sink", "gmlp_g", "gmlp_b", "w_s", "b_s", "ln1_g", "ln1_b", "ln2_g", "ln2_b")


def kernel(x, c, ctx, c_ctx, w_ada, b_ada, w_in, attn_sink, gmlp_ln_g, gmlp_ln_b, w_spatial, b_spatial, w_branch_a, w_branch_b, w_out, ln1_g, ln1_b, w_ffn_in, w_ffn_out, ln2_g, ln2_b, loss_target, m_c_ctx, m_w_ada, m_b_ada, m_w_in, m_attn_sink, m_gmlp_ln_g, m_gmlp_ln_b, m_w_spatial, m_b_spatial, m_w_branch_a, m_w_branch_b, m_w_out, m_ln1_g, m_ln1_b, m_w_ffn_in, m_w_ffn_out, m_ln2_g, m_ln2_b, v_c_ctx, v_w_ada, v_b_ada, v_w_in, v_attn_sink, v_gmlp_ln_g, v_gmlp_ln_b, v_w_spatial, v_b_spatial, v_w_branch_a, v_w_branch_b, v_w_out, v_ln1_g, v_ln1_b, v_w_ffn_in, v_w_ffn_out, v_ln2_g, v_ln2_b):
    mx, my, mc = _my_pos()
    me = 4 * mx + 2 * my + mc
    chip = 2 * mx + my
    shards = dict(w_in=w_in[0], w_a=w_branch_a[0], w_b=w_branch_b[0], w_o=w_out[0], w_fi=w_ffn_in[0], w_fo=w_ffn_out[0])

    halves = {n: _half_of_shard(shards[n], mc).astype(BF16) for n in BIG}
    c_rows = jnp.concatenate([c, jnp.zeros((7, D), F32)], axis=0)
    g_in, c_g = _all_gather([halves["w_in"], c_rows], me, "gather_w_in")
    r, cdim = shards["w_in"].shape
    wb = dict(w_in=g_in.reshape(N_SHARD, r, cdim).transpose(1, 0, 2).reshape(r, N_SHARD * cdim))

    c_all = c_g[:, 0, :]
    cc = jnp.concatenate([c_all, c_ctx[None, :], jnp.zeros((7, D), F32)], axis=0)
    sig_cc = jax.nn.sigmoid(cc)
    sc_all = cc * sig_cc
    mod_shard = _ada_fwd(sc_all, w_ada[0])
    mod_g = _all_gather([mod_shard], me, "gather_mod")[0]
    mod_all = jnp.concatenate([mod_g[2 * s] for s in range(4)], axis=1) + b_ada
    mod_x = lax.dynamic_slice_in_dim(mod_all, me, 1, axis=0).reshape(6, D)
    mod_c = mod_all[8].reshape(6, D)[0:2]

    grad_x, dmod_x, dmod_c, small, summed = _local_step(
        x[0], ctx[0], loss_target[0], mod_x, mod_c, wb, attn_sink, gmlp_ln_g, gmlp_ln_b, w_spatial[0], b_spatial[0],
        ln1_g, ln1_b, ln2_g, ln2_b, later=halves, me=me)

    theirs = _sibling_exchange([summed[n] for n in BIG], "exchange_grads")
    g_shard = {}
    for n, other in zip(BIG, theirs):
        lo = jnp.where(mc == 0, summed[n], other)
        hi = jnp.where(mc == 0, other, summed[n])
        g_shard[n] = jnp.concatenate([lo, hi], axis=0)

    parts = [small[n].reshape(-1) for n in SMALL_ORDER] + [dmod_c.reshape(-1), dmod_x.reshape(-1)]
    sizes = [p.shape[0] for p in parts]
    flat = jnp.concatenate(parts)
    pad = (-flat.shape[0]) % (16 * LANES)
    flat = jnp.concatenate([flat, jnp.zeros((pad,), F32)])
    gath = _all_gather([_to_lanes(flat)], me, "gather_small")[0]
    tot = _sum8(gath, "sum_small").reshape(-1)
    offs = [0]
    for s_ in sizes:
        offs.append(offs[-1] + s_)
    tots = {n: tot[offs[i]:offs[i + 1]] for i, n in enumerate(SMALL_ORDER)}
    dmod_c_tot = tot[offs[-3]:offs[-2]]
    dmod_x_sum = tot[offs[-2]:offs[-1]]
    dmod_x_all = gath.reshape(N_DEV, -1)[:, offs[-2]:offs[-1]]
    loss = 0.5 * jnp.sum(tots["loss_cols"]) / D

    dmod_c_full = jnp.concatenate([dmod_c_tot, jnp.zeros((4 * D,), F32)])
    g_b_ada = (dmod_x_sum + dmod_c_full)[None, :]
    dm_rows = jnp.concatenate([dmod_x_all, dmod_c_full[None, :], jnp.zeros((7, 6 * D), F32)], axis=0)
    cs = w_ada.shape[2]
    dm_shard = lax.dynamic_slice_in_dim(dm_rows, chip * cs, cs, axis=1)
    dmc_shard = jnp.concatenate([dm_shard[8:9], jnp.zeros((7, cs), F32)], axis=0)
    g_w_ada, part = _ada_bwd(sc_all.T, dm_shard, dmc_shard, w_ada[0])
    part = part * (mc == 0).astype(F32)
    part_all = _all_gather([_to_lanes(part.reshape(-1))], me, "gather_c_ctx")[0]
    dsc = _sum8(part_all, "sum_c_ctx").reshape(8, D)[0]
    sig_ctx = sig_cc[8]
    g_c_ctx = dsc * (sig_ctx * (1.0 + c_ctx * (1.0 - sig_ctx)))

    grads = dict(c_ctx=g_c_ctx, w_ada=g_w_ada[None], b_ada=g_b_ada, w_in=g_shard["w_in"][None],
                 attn_sink=tots["sink"][None, :], gmlp_ln_g=tots["gmlp_g"][None, :], gmlp_ln_b=tots["gmlp_b"][None, :],
                 w_spatial=tots["w_s"].reshape(w_spatial.shape), b_spatial=tots["b_s"].reshape(b_spatial.shape),
                 w_branch_a=g_shard["w_a"][None], w_branch_b=g_shard["w_b"][None], w_out=g_shard["w_o"][None],
                 ln1_g=tots["ln1_g"][None, :], ln1_b=tots["ln1_b"][None, :],
                 w_ffn_in=g_shard["w_fi"][None], w_ffn_out=g_shard["w_fo"][None],
                 ln2_g=tots["ln2_g"][None, :], ln2_b=tots["ln2_b"][None, :])
    weights = dict(c_ctx=c_ctx, w_ada=w_ada, b_ada=b_ada, w_in=w_in, attn_sink=attn_sink, gmlp_ln_g=gmlp_ln_g,
                   gmlp_ln_b=gmlp_ln_b, w_spatial=w_spatial, b_spatial=b_spatial, w_branch_a=w_branch_a,
                   w_branch_b=w_branch_b, w_out=w_out, ln1_g=ln1_g, ln1_b=ln1_b, w_ffn_in=w_ffn_in, w_ffn_out=w_ffn_out,
                   ln2_g=ln2_g, ln2_b=ln2_b)
    ms = dict(c_ctx=m_c_ctx, w_ada=m_w_ada, b_ada=m_b_ada, w_in=m_w_in, attn_sink=m_attn_sink, gmlp_ln_g=m_gmlp_ln_g,
              gmlp_ln_b=m_gmlp_ln_b, w_spatial=m_w_spatial, b_spatial=m_b_spatial, w_branch_a=m_w_branch_a,
              w_branch_b=m_w_branch_b, w_out=m_w_out, ln1_g=m_ln1_g, ln1_b=m_ln1_b, w_ffn_in=m_w_ffn_in,
              w_ffn_out=m_w_ffn_out, ln2_g=m_ln2_g, ln2_b=m_ln2_b)
    vs = dict(c_ctx=v_c_ctx, w_ada=v_w_ada, b_ada=v_b_ada, w_in=v_w_in, attn_sink=v_attn_sink, gmlp_ln_g=v_gmlp_ln_g,
              gmlp_ln_b=v_gmlp_ln_b, w_spatial=v_w_spatial, b_spatial=v_b_spatial, w_branch_a=v_w_branch_a,
              w_branch_b=v_w_branch_b, w_out=v_w_out, ln1_g=v_ln1_g, ln1_b=v_ln1_b, w_ffn_in=v_w_ffn_in,
              w_ffn_out=v_w_ffn_out, ln2_g=v_ln2_g, ln2_b=v_ln2_b)
    order = list(weights)
    large = ("w_ada", "w_in", "w_branch_a", "w_branch_b", "w_out", "w_ffn_in", "w_ffn_out")
    delta, new_m, new_v = {}, {}, {}
    for n in large:
        d_, m_, v_ = _adamw(weights[n][0], grads[n][0], ms[n][0], vs[n][0], "adamw_" + n)
        delta[n], new_m[n], new_v[n] = d_[None], m_[None], v_[None]
    rest = [n for n in order if n not in large]

    def pack(d):
        f = jnp.concatenate([d[n].reshape(-1) for n in rest])
        return _to_lanes(jnp.concatenate([f, jnp.zeros(((-f.shape[0]) % (8 * LANES),), F32)]))

    d_, m_, v_ = _adamw(pack(weights), pack(grads), pack(ms), pack(vs), "adamw_small")
    off = 0
    for n in rest:
        size = weights[n].size
        for out, src in ((delta, d_), (new_m, m_), (new_v, v_)):
            out[n] = src.reshape(-1)[off:off + size].reshape(weights[n].shape)
        off += size

    return (loss, grad_x[None], *[grads[n] for n in order], *[delta[n] for n in order],
            *[new_m[n] for n in order], *[new_v[n] for n in order])
```

```python
import functools
import math

import jax
import jax.numpy as jnp
from jax import lax
from jax.experimental import pallas as pl
from jax.experimental.pallas import tpu as pltpu

F32 = jnp.float32
BF16 = jnp.bfloat16

D = 1024
HEAD = 64
N_KV = 2
GROUP = 4
Q_W = 512
KV_W = 128
G_W = 512
BLK = 128
N_GRP = 8
GRP_D = 64
FH = 2816
IN_W = 3840
GRID_W = 64
ROPE_BASE = 10000.0
LN_EPS = 1e-5
NEG = -1e30
ALPHA = (2 * 1) ** 0.25
SCALE = HEAD ** -0.5
GELU_K = math.sqrt(2.0 / math.pi)
GELU_A = 0.044715
ADAM_LR = 0.001
ADAM_B1 = 0.9
ADAM_B2 = 0.999
ADAM_EPS = 1e-08
ADAM_WD = 0.01
ADAM_STEP = 10
N_DEV = 8
N_SHARD = 4
FH_SHARD = FH // 2
LANES = 128
VMEM_LIMIT = 56 * 1024 * 1024
MESH = pl.DeviceIdType.MESH


def _cp(*sem):
    return pltpu.CompilerParams(dimension_semantics=sem, vmem_limit_bytes=VMEM_LIMIT)


def _resident(shape):
    return pl.BlockSpec(shape, lambda *_: (0,) * len(shape), pipeline_mode=pl.Buffered(1))


def _rows(tm, width):
    return pl.BlockSpec((tm, width), lambda i: (i, 0))


def _acc(shape):
    return pl.BlockSpec(shape, lambda *_: (0,) * len(shape))


def _dot(a, b):
    return jnp.dot(a, b, preferred_element_type=F32)


def _dot_nt(a, b):
    return lax.dot_general(a, b, (((1,), (1,)), ((), ())), preferred_element_type=F32)


def _dot_tn(a, b):
    return lax.dot_general(a, b, (((0,), (0,)), ((), ())), preferred_element_type=F32)


def _ln(x):
    mu = jnp.mean(x, axis=-1, keepdims=True)
    xc = x - mu
    var = jnp.mean(xc * xc, axis=-1, keepdims=True)
    rstd = lax.rsqrt(var + LN_EPS)
    return xc * rstd, rstd


def _ln_bwd(dxhat, xhat, rstd):
    return (dxhat - jnp.mean(dxhat, axis=-1, keepdims=True)
            - xhat * jnp.mean(dxhat * xhat, axis=-1, keepdims=True)) * rstd


def _sig(x):
    return 1.0 / (1.0 + jnp.exp(-x))


def _gelu(x):
    t = jnp.tanh(GELU_K * (x + GELU_A * x * x * x))
    return 0.5 * x * (1.0 + t), t


def _gelu_grad(x, t):
    return 0.5 * (1.0 + t) + 0.5 * x * (1.0 - t * t) * GELU_K * (1.0 + 3.0 * GELU_A * x * x)


def _colsum(v):
    return jnp.sum(v, axis=0, keepdims=True)


def _partner(x):
    w = x.shape[1]
    lane = lax.broadcasted_iota(jnp.int32, x.shape, 1)
    return jnp.where((lane & 31) < 16, pltpu.roll(x, w - 16, 1), pltpu.roll(x, 16, 1))


def _rope(x, cos, sin):
    return x * cos + _partner(x) * sin


def _unrope(g, cos, sin):
    return g * cos + _partner(g * sin)


def _rope_tables(seq):
    inv = ROPE_BASE ** (-jnp.arange(HEAD // 4, dtype=F32) / (HEAD // 4))
    pos = jnp.arange(seq, dtype=jnp.int32)
    ar = (pos // GRID_W).astype(F32)[:, None] * inv
    ac = (pos % GRID_W).astype(F32)[:, None] * inv
    cos = jnp.concatenate([jnp.cos(ar), jnp.cos(ar), jnp.cos(ac), jnp.cos(ac)], axis=-1)
    sin = jnp.concatenate([-jnp.sin(ar), jnp.sin(ar), -jnp.sin(ac), jnp.sin(ac)], axis=-1)
    return jnp.tile(cos, (1, LANES // HEAD)), jnp.tile(sin, (1, LANES // HEAD))


def _ctx_fwd(ctx, modc, w_kv):
    n_ctx = ctx.shape[0]

    def body(ctx_ref, mod_ref, w_ref, hc_ref, kvc_ref):
        xhat, _ = _ln(ctx_ref[...])
        hc = (xhat * (1.0 + mod_ref[1:2, :]) + mod_ref[0:1, :]).astype(BF16)
        hc_ref[...] = hc
        kvc_ref[...] = _dot(hc, w_ref[...]).astype(BF16)

    return pl.pallas_call(
        body, name="ctx_fwd", grid=(1,),
        in_specs=[_acc((n_ctx, D)), _acc((8, D)), _acc((D, 2 * KV_W))],
        out_specs=[_acc((n_ctx, D)), _acc((n_ctx, 2 * KV_W))],
        out_shape=[jax.ShapeDtypeStruct((n_ctx, D), BF16), jax.ShapeDtypeStruct((n_ctx, 2 * KV_W), BF16)],
        compiler_params=_cp("arbitrary"),
    )(ctx, modc, w_kv)


def _host_start(step, comm):
    if comm is not None:
        @pl.when(step == 0)
        def _():
            comm.start()


def _host_finish(step, last, comm):
    if comm is not None:
        @pl.when(step == last)
        def _():
            comm.finish()


def _proj_fwd(x, modx, w_in, cos, sin, tm, gather=()):
    seq = x.shape[0]
    ng = len(gather)

    def body(x_ref, mod_ref, w_ref, cos_ref, sin_ref, *rest):
        h_ref, q_ref, kv_ref, uv_ref, gab_ref = rest[ng:ng + 5]
        comm = _Gather(rest[:ng], rest[ng + 5:2 * ng + 5], *rest[2 * ng + 5:]) if ng else None
        _host_start(pl.program_id(0), comm)
        xhat, _ = _ln(x_ref[...])
        h = (xhat * (1.0 + mod_ref[1:2, :]) + mod_ref[0:1, :]).astype(BF16)
        h_ref[...] = h
        cos1, sin1 = cos_ref[...], sin_ref[...]
        cos2 = jnp.concatenate([cos1, cos1], axis=1)
        sin2 = jnp.concatenate([sin1, sin1], axis=1)
        for j in range(Q_W // 256):
            t = _dot(h, w_ref[:, 256 * j:256 * (j + 1)])
            q_ref[:, 256 * j:256 * (j + 1)] = _rope(t, cos2, sin2).astype(BF16)
        t = _dot(h, w_ref[:, Q_W:Q_W + 2 * KV_W])
        kv_ref[:, :KV_W] = _rope(t[:, :KV_W], cos1, sin1).astype(BF16)
        kv_ref[:, KV_W:] = t[:, KV_W:].astype(BF16)
        o = Q_W + 2 * KV_W
        for j in range(2):
            uv_ref[:, G_W * j:G_W * (j + 1)] = _dot(h, w_ref[:, o + G_W * j:o + G_W * (j + 1)])
        o += 2 * G_W
        for j in range(4):
            gab_ref[:, 512 * j:512 * (j + 1)] = _dot(h, w_ref[:, o + 512 * j:o + 512 * (j + 1)])
        _host_finish(pl.program_id(0), seq // tm - 1, comm)

    out = pl.pallas_call(
        body, name="proj_fwd", grid=(seq // tm,),
        in_specs=[_rows(tm, D), _acc((8, D)), _resident((D, IN_W)), _rows(tm, LANES), _rows(tm, LANES)] + _comm_specs(ng),
        out_specs=[_rows(tm, D), _rows(tm, Q_W), _rows(tm, 2 * KV_W), _rows(tm, 2 * G_W), _rows(tm, 2 * D)] + _comm_specs(ng),
        out_shape=[jax.ShapeDtypeStruct((seq, D), BF16), jax.ShapeDtypeStruct((seq, Q_W), BF16),
                   jax.ShapeDtypeStruct((seq, 2 * KV_W), BF16), jax.ShapeDtypeStruct((seq, 2 * G_W), F32),
                   jax.ShapeDtypeStruct((seq, 2 * D), F32)] + _gathered_shapes(gather),
        scratch_shapes=_comm_scratch(ng) if ng else [],
        compiler_params=_cp("arbitrary"),
    )(x, modx, w_in, cos, sin, *gather)
    return out[:5], out[5:]


def _attn_pieces(q_ref, kvp_ref, kvm_ref, kvn_ref, kvc_ref, sink_ref, hk, n, nb):
    q = q_ref[...]
    q4 = jnp.concatenate([q[:, (hk * GROUP + g) * HEAD:(hk * GROUP + g + 1) * HEAD] for g in range(GROUP)], axis=0)
    ks = [r[:, hk * HEAD:(hk + 1) * HEAD] for r in (kvc_ref, kvp_ref, kvm_ref, kvn_ref)]
    vs = [r[:, KV_W + hk * HEAD:KV_W + (hk + 1) * HEAD] for r in (kvc_ref, kvp_ref, kvm_ref, kvn_ref)]
    rows = GROUP * BLK
    qi = lax.broadcasted_iota(jnp.int32, (rows, BLK), 0) & (BLK - 1)
    kj = lax.broadcasted_iota(jnp.int32, (rows, BLK), 1)
    s = [_dot_nt(q4, k) * SCALE for k in ks]
    s[1] = jnp.where((kj >= qi) & (n > 0), s[1], NEG)
    s[3] = jnp.where((kj <= qi) & (n < nb - 1), s[3], NEG)
    rg = lax.broadcasted_iota(jnp.int32, (rows, 1), 0) >> 7
    sink_v = jnp.full((rows, 1), sink_ref[0, hk * GROUP], F32)
    for g in range(1, GROUP):
        sink_v = jnp.where(rg == g, sink_ref[0, hk * GROUP + g], sink_v)
    m = sink_v
    for t in s:
        m = jnp.maximum(m, jnp.max(t, axis=-1, keepdims=True))
    e = [jnp.exp(t - m) for t in s]
    e_sink = jnp.exp(sink_v - m)
    denom = e_sink
    for t in e:
        denom = denom + jnp.sum(t, axis=-1, keepdims=True)
    inv = 1.0 / denom
    p = [t * inv for t in e]
    return q4, ks, vs, p, e_sink * inv


def _kv_specs(nb):
    return [pl.BlockSpec((BLK, 2 * KV_W), lambda n: (jnp.maximum(n - 1, 0), 0)),
            pl.BlockSpec((BLK, 2 * KV_W), lambda n: (n, 0)),
            pl.BlockSpec((BLK, 2 * KV_W), lambda n: (jnp.minimum(n + 1, nb - 1), 0))]


def _attn_fwd(q, kv, kvc, sink, gather=()):
    seq = q.shape[0]
    nb = seq // BLK
    n_ctx = kvc.shape[0]
    ng = len(gather)

    def body(q_ref, kvp_ref, kvm_ref, kvn_ref, kvc_ref, sink_ref, *rest):
        o_ref = rest[ng]
        comm = _Gather(rest[:ng], rest[ng + 1:2 * ng + 1], *rest[2 * ng + 1:]) if ng else None
        n = pl.program_id(0)
        _host_start(n, comm)
        outs = []
        for hk in range(N_KV):
            _, _, vs, p, _ = _attn_pieces(q_ref, kvp_ref, kvm_ref, kvn_ref, kvc_ref, sink_ref, hk, n, nb)
            o = _dot(p[0].astype(BF16), vs[0])
            for t, v in zip(p[1:], vs[1:]):
                o = o + _dot(t.astype(BF16), v)
            outs += [o[g * BLK:(g + 1) * BLK, :] for g in range(GROUP)]
        o_ref[...] = jnp.concatenate(outs, axis=1).astype(BF16)
        _host_finish(n, nb - 1, comm)

    out = pl.pallas_call(
        body, name="attn_fwd", grid=(nb,),
        in_specs=[_rows(BLK, Q_W)] + _kv_specs(nb) + [_acc((n_ctx, 2 * KV_W)), pl.BlockSpec(memory_space=pltpu.SMEM)]
        + _comm_specs(ng),
        out_specs=[_rows(BLK, Q_W)] + _comm_specs(ng),
        out_shape=[jax.ShapeDtypeStruct((seq, Q_W), BF16)] + _gathered_shapes(gather),
        scratch_shapes=_comm_scratch(ng) if ng else [],
        compiler_params=_cp("arbitrary"),
    )(q, kv, kv, kv, kvc, sink, *gather)
    return out[0], out[1:]


def _gmlp_chunk(u, vb, gp_ref, ws_ref, bias_ref):
    gu, tu = _gelu(u)
    gv, tv = _gelu(vb)
    vhat, rstd = _ln(gv)
    vn = (vhat * gp_ref[0:1, :] + gp_ref[1:2, :]).astype(BF16)
    r = _dot(ws_ref[...], vn)
    grp = lax.broadcasted_iota(jnp.int32, (BLK, G_W), 1) >> 6
    s = bias_ref[...]
    for g in range(N_GRP):
        s = s + jnp.where(grp == g, r[g * BLK:(g + 1) * BLK, :], 0.0)
    return gu, tu, tv, vhat, rstd, vn, s, grp


def _mix_fwd(uv, gab, ya, gp, ws_stack, bias_full, w_a, w_b, w_o, tm):
    seq = uv.shape[0]

    def body(uv_ref, gab_ref, ya_ref, gp_ref, ws_ref, bias_ref, wa_ref, wb_ref, wo_ref,
             a_ref, b_ref, mix_ref, merged_ref, yb_ref):
        for c in range(tm // BLK):
            rs = slice(c * BLK, (c + 1) * BLK)
            gu, _, _, _, _, _, s, _ = _gmlp_chunk(uv_ref[rs, :G_W], uv_ref[rs, G_W:], gp_ref, ws_ref, bias_ref)
            yb_ref[rs, :] = (gu * s).astype(BF16)
        ya = ya_ref[...]
        yb = yb_ref[...]
        for s in range(N_SHARD):
            cs = slice(s * (D // N_SHARD), (s + 1) * (D // N_SHARD))
            a_ref[:, cs] = _dot(ya, wa_ref[s])
            b_ref[:, cs] = _dot(yb, wb_ref[s])
        merged = (_sig(gab_ref[:, :D]) * a_ref[...] + _sig(gab_ref[:, D:]) * b_ref[...]).astype(BF16)
        merged_ref[...] = merged
        mix_ref[...] = _dot(merged, wo_ref[...])

    return pl.pallas_call(
        body, name="mix_fwd", grid=(seq // tm,),
        in_specs=[_rows(tm, 2 * G_W), _rows(tm, 2 * D), _rows(tm, Q_W), _acc((8, G_W)),
                  _resident((N_GRP * BLK, BLK)), _acc((BLK, G_W)),
                  _resident((N_SHARD, Q_W, D // N_SHARD)), _resident((N_SHARD, G_W, D // N_SHARD)), _resident((D, D))],
        out_specs=[_rows(tm, D), _rows(tm, D), _rows(tm, D), _rows(tm, D), _rows(tm, G_W)],
        out_shape=[jax.ShapeDtypeStruct((seq, D), F32), jax.ShapeDtypeStruct((seq, D), F32),
                   jax.ShapeDtypeStruct((seq, D), F32), jax.ShapeDtypeStruct((seq, D), BF16),
                   jax.ShapeDtypeStruct((seq, G_W), BF16)],
        compiler_params=_cp("arbitrary"),
    )(uv, gab, ya, gp, ws_stack, bias_full, w_a, w_b, w_o)


FFN_CHUNK = 512


def _ffn_chunks():
    out = []
    for hh in range(2):
        off = 0
        while off < FH_SHARD:
            w = min(FFN_CHUNK, FH_SHARD - off)
            out.append((hh, off, w))
            off += w
    return out


def _mid_recompute(x_ref, mix_ref, vec_ref):
    r1 = ALPHA * x_ref[...] + vec_ref[0:1, :] * mix_ref[...]
    xh1, rstd1 = _ln(r1)
    xmid = xh1 * vec_ref[1:2, :] + vec_ref[2:3, :]
    xh2, rstd2 = _ln(xmid)
    return xh1, rstd1, xmid, xh2, rstd2


def _ffn_fwd(x, mix, tgt, vec, w_fi, w_fo, tm):
    seq = x.shape[0]

    def body(x_ref, mix_ref, tgt_ref, vec_ref, wi_ref, wo_ref, gu_ref, act_ref, h2_ref, dr2_ref, st_ref):
        @pl.when(pl.program_id(0) == 0)
        def _():
            st_ref[...] = jnp.zeros_like(st_ref)

        _, _, xmid, xh2, _ = _mid_recompute(x_ref, mix_ref, vec_ref)
        h2 = (xh2 * (1.0 + vec_ref[4:5, :]) + vec_ref[3:4, :]).astype(BF16)
        h2_ref[...] = h2
        f = jnp.zeros((tm, D), F32)
        for hh, off, w in _ffn_chunks():
            cs = slice(hh * FH_SHARD + off, hh * FH_SHARD + off + w)
            cu = slice(FH + hh * FH_SHARD + off, FH + hh * FH_SHARD + off + w)
            g = _dot(h2, wi_ref[hh, :, off:off + w])
            u = _dot(h2, wi_ref[2 + hh, :, off:off + w])
            gu_ref[:, cs] = g
            gu_ref[:, cu] = u
            a = (g * _sig(g) * u).astype(BF16)
            act_ref[:, cs] = a
            f = f + _dot(a, wo_ref[cs, :])
        r2 = ALPHA * xmid + vec_ref[5:6, :] * f
        yh, rstd = _ln(r2)
        y = yh * vec_ref[6:7, :] + vec_ref[7:8, :]
        err = y - tgt_ref[...]
        dy = err / D
        dr2 = _ln_bwd(dy * vec_ref[6:7, :], yh, rstd)
        dr2_ref[...] = dr2
        st_ref[0:1, :] += _colsum(err * err)
        st_ref[1:2, :] += _colsum(dy * yh)
        st_ref[2:3, :] += _colsum(dy)
        st_ref[3:4, :] += _colsum(dr2 * f)

    return pl.pallas_call(
        body, name="ffn_fwd", grid=(seq // tm,),
        in_specs=[_rows(tm, D), _rows(tm, D), _rows(tm, D), _acc((8, D)), _resident((N_SHARD, D, FH_SHARD)), _resident((FH, D))],
        out_specs=[_rows(tm, 2 * FH), _rows(tm, FH), _rows(tm, D), _rows(tm, D), _acc((8, D))],
        out_shape=[jax.ShapeDtypeStruct((seq, 2 * FH), F32), jax.ShapeDtypeStruct((seq, FH), BF16),
                   jax.ShapeDtypeStruct((seq, D), BF16), jax.ShapeDtypeStruct((seq, D), F32),
                   jax.ShapeDtypeStruct((8, D), F32)],
        compiler_params=_cp("arbitrary"),
    )(x, mix, tgt, vec, w_fi, w_fo)


def _ffn_bwd(dr2, gu, x, mix, vec, w_fi, w_fo, tm):
    seq = x.shape[0]

    def body(dr2_ref, gu_ref, x_ref, mix_ref, vec_ref, wi_ref, wo_ref, dff_ref, df_ref, dr1_ref, st_ref):
        @pl.when(pl.program_id(0) == 0)
        def _():
            st_ref[...] = jnp.zeros_like(st_ref)

        dr2 = dr2_ref[...]
        df = (dr2 * vec_ref[5:6, :]).astype(BF16)
        df_ref[...] = df
        dh2 = jnp.zeros((tm, D), F32)
        for hh, off, w in _ffn_chunks():
            cs = slice(hh * FH_SHARD + off, hh * FH_SHARD + off + w)
            cu = slice(FH + hh * FH_SHARD + off, FH + hh * FH_SHARD + off + w)
            da = _dot_nt(df, wo_ref[cs, :])
            g = gu_ref[:, cs]
            u = gu_ref[:, cu]
            sg = _sig(g)
            dg = (da * u * sg * (1.0 + g * (1.0 - sg))).astype(BF16)
            du = (da * g * sg).astype(BF16)
            dff_ref[:, cs] = dg
            dff_ref[:, cu] = du
            dh2 = dh2 + _dot_nt(dg, wi_ref[hh, :, off:off + w]) + _dot_nt(du, wi_ref[2 + hh, :, off:off + w])
        xh1, rstd1, _, xh2, rstd2 = _mid_recompute(x_ref, mix_ref, vec_ref)
        dxmid = _ln_bwd(dh2 * (1.0 + vec_ref[4:5, :]), xh2, rstd2) + ALPHA * dr2
        dr1 = _ln_bwd(dxmid * vec_ref[1:2, :], xh1, rstd1)
        dr1_ref[...] = dr1
        st_ref[0:1, :] += _colsum(dh2 * xh2)
        st_ref[1:2, :] += _colsum(dh2)
        st_ref[2:3, :] += _colsum(dxmid * xh1)
        st_ref[3:4, :] += _colsum(dxmid)
        st_ref[4:5, :] += _colsum(dr1 * mix_ref[...])

    return pl.pallas_call(
        body, name="ffn_bwd", grid=(seq // tm,),
        in_specs=[_rows(tm, D), _rows(tm, 2 * FH), _rows(tm, D), _rows(tm, D), _acc((8, D)),
                  _resident((N_SHARD, D, FH_SHARD)), _resident((FH, D))],
        out_specs=[_rows(tm, 2 * FH), _rows(tm, D), _rows(tm, D), _acc((8, D))],
        out_shape=[jax.ShapeDtypeStruct((seq, 2 * FH), BF16), jax.ShapeDtypeStruct((seq, D), BF16),
                   jax.ShapeDtypeStruct((seq, D), F32), jax.ShapeDtypeStruct((8, D), F32)],
        compiler_params=_cp("arbitrary"),
    )(dr2, gu, x, mix, vec, w_fi, w_fo)


def _mix_bwd(dr1, a, b, gab, uv, vec, gp, ws_stack, ws_stack_t, bias_full, w_a, w_b, w_o, tm):
    seq = dr1.shape[0]

    def body(dr1_ref, a_ref, b_ref, gab_ref, uv_ref, vec_ref, gp_ref, ws_ref, wst_ref, bias_ref, wa_ref, wb_ref, wo_ref,
             dmix_ref, da_ref, db_ref, dya_ref, dp_ref, dws_ref, dbs_ref, st_ref):
        @pl.when(pl.program_id(0) == 0)
        def _():
            dws_ref[...] = jnp.zeros_like(dws_ref)
            dbs_ref[...] = jnp.zeros_like(dbs_ref)
            st_ref[...] = jnp.zeros_like(st_ref)

        dmix = (dr1_ref[...] * vec_ref[0:1, :]).astype(BF16)
        dmix_ref[...] = dmix
        dmerged = _dot_nt(dmix, wo_ref[...])
        sa = _sig(gab_ref[:, :D])
        sb = _sig(gab_ref[:, D:])
        da = (dmerged * sa).astype(BF16)
        db = (dmerged * sb).astype(BF16)
        da_ref[...] = da
        db_ref[...] = db
        dp_ref[:, 2 * G_W:2 * G_W + D] = (dmerged * a_ref[...] * sa * (1.0 - sa)).astype(BF16)
        dp_ref[:, 2 * G_W + D:] = (dmerged * b_ref[...] * sb * (1.0 - sb)).astype(BF16)
        dya = jnp.zeros((tm, Q_W), F32)
        dyb = jnp.zeros((tm, G_W), F32)
        for s in range(N_SHARD):
            cs = slice(s * (D // N_SHARD), (s + 1) * (D // N_SHARD))
            dya = dya + _dot_nt(da[:, cs], wa_ref[s])
            dyb = dyb + _dot_nt(db[:, cs], wb_ref[s])
        dya_ref[...] = dya.astype(BF16)
        for c in range(tm // BLK):
            rs = slice(c * BLK, (c + 1) * BLK)
            u = uv_ref[rs, :G_W]
            vb = uv_ref[rs, G_W:]
            gu, tu, tv, vhat, rstd, vn, s, grp = _gmlp_chunk(u, vb, gp_ref, ws_ref, bias_ref)
            dyb_c = dyb[rs, :]
            ds = dyb_c * gu
            du = dyb_c * s * _gelu_grad(u, tu)
            dstack = jnp.concatenate([jnp.where(grp == g, ds, 0.0) for g in range(N_GRP)], axis=0).astype(BF16)
            dvn = _dot(wst_ref[...], dstack)
            dws_ref[...] += _dot_nt(dstack, vn)
            dbs_ref[...] += ds
            st_ref[0:1, :] += _colsum(dvn * vhat)
            st_ref[1:2, :] += _colsum(dvn)
            dgv = _ln_bwd(dvn * gp_ref[0:1, :], vhat, rstd)
            dvb = dgv * _gelu_grad(vb, tv)
            dp_ref[rs, :G_W] = du.astype(BF16)
            dp_ref[rs, G_W:2 * G_W] = dvb.astype(BF16)

    pw = 2 * G_W + 2 * D
    return pl.pallas_call(
        body, name="mix_bwd", grid=(seq // tm,),
        in_specs=[_rows(tm, D), _rows(tm, D), _rows(tm, D), _rows(tm, 2 * D), _rows(tm, 2 * G_W), _acc((8, D)), _acc((8, G_W)),
                  _resident((N_GRP * BLK, BLK)), _resident((BLK, N_GRP * BLK)), _acc((BLK, G_W)),
                  _resident((N_SHARD, Q_W, D // N_SHARD)), _resident((N_SHARD, G_W, D // N_SHARD)), _resident((D, D))],
        out_specs=[_rows(tm, D), _rows(tm, D), _rows(tm, D), _rows(tm, Q_W), _rows(tm, pw),
                   _acc((N_GRP * BLK, BLK)), _acc((BLK, G_W)), _acc((8, G_W))],
        out_shape=[jax.ShapeDtypeStruct((seq, D), BF16), jax.ShapeDtypeStruct((seq, D), BF16),
                   jax.ShapeDtypeStruct((seq, D), BF16), jax.ShapeDtypeStruct((seq, Q_W), BF16),
                   jax.ShapeDtypeStruct((seq, pw), BF16), jax.ShapeDtypeStruct((N_GRP * BLK, BLK), F32),
                   jax.ShapeDtypeStruct((BLK, G_W), F32), jax.ShapeDtypeStruct((8, G_W), F32)],
        compiler_params=_cp("arbitrary"),
    )(dr1, a, b, gab, uv, vec, gp, ws_stack, ws_stack_t, bias_full, w_a, w_b, w_o)


def _attn_bwd(q, kv, kvc, sink, dya, scatter=()):
    seq = q.shape[0]
    nb = seq // BLK
    n_ctx = kvc.shape[0]
    ns = len(scatter)

    def body(q_ref, kvp_ref, kvm_ref, kvn_ref, kvc_ref, sink_ref, do_ref, *rest):
        dq_ref, dkv_ref, dkvc_ref, dsink_ref = rest[ns:ns + 4]
        comm = _AllToAll(rest[:ns], rest[ns + 4:2 * ns + 4], *rest[2 * ns + 4:]) if ns else None
        n = pl.program_id(0)
        _host_start(n, comm)

        @pl.when(n == 0)
        def _():
            dkv_ref[...] = jnp.zeros_like(dkv_ref)
            dkvc_ref[...] = jnp.zeros_like(dkvc_ref)
            dsink_ref[...] = jnp.zeros_like(dsink_ref)

        do = do_ref[...]
        dqs, dks, dvs = [], [], []
        for hk in range(N_KV):
            q4, ks, vs, p, p_sink = _attn_pieces(q_ref, kvp_ref, kvm_ref, kvn_ref, kvc_ref, sink_ref, hk, n, nb)
            do4 = jnp.concatenate([do[:, (hk * GROUP + g) * HEAD:(hk * GROUP + g + 1) * HEAD] for g in range(GROUP)], axis=0)
            dp = [_dot_nt(do4, v) for v in vs]
            delta = jnp.zeros((GROUP * BLK, 1), F32)
            for t, d in zip(p, dp):
                delta = delta + jnp.sum(t * d, axis=-1, keepdims=True)
            ds = [(t * (d - delta) * SCALE).astype(BF16) for t, d in zip(p, dp)]
            dq4 = _dot(ds[0], ks[0])
            for t, k in zip(ds[1:], ks[1:]):
                dq4 = dq4 + _dot(t, k)
            dqs += [dq4[g * BLK:(g + 1) * BLK, :] for g in range(GROUP)]
            dks.append([_dot_tn(t, q4) for t in ds])
            dvs.append([_dot_tn(t.astype(BF16), do4) for t in p])
            ps = p_sink * delta
            lane = lax.broadcasted_iota(jnp.int32, (1, LANES), 1)
            for g in range(GROUP):
                part = -jnp.sum(ps[g * BLK:(g + 1) * BLK, :], axis=0, keepdims=True)
                dsink_ref[0:1, :] += jnp.where(lane == hk * GROUP + g, part, 0.0)
        dq_ref[...] = jnp.concatenate(dqs, axis=1)

        def piece(i):
            return jnp.concatenate([dks[0][i], dks[1][i], dvs[0][i], dvs[1][i]], axis=1)

        dkvc_ref[...] += piece(0)
        starts = (jnp.maximum(n - 1, 0), n, jnp.minimum(n + 1, nb - 1))
        for i, st in enumerate(starts):
            r = pl.ds(pl.multiple_of(st * BLK, BLK), BLK)
            dkv_ref[r, :] += piece(i + 1)
        _host_finish(n, nb - 1, comm)

    out = pl.pallas_call(
        body, name="attn_bwd", grid=(nb,),
        in_specs=[_rows(BLK, Q_W)] + _kv_specs(nb) + [_acc((n_ctx, 2 * KV_W)), pl.BlockSpec(memory_space=pltpu.SMEM),
                                                      _rows(BLK, Q_W)] + _comm_specs(ns),
        out_specs=[_rows(BLK, Q_W), _acc((seq, 2 * KV_W)), _acc((n_ctx, 2 * KV_W)), _acc((8, LANES))] + _comm_specs(ns),
        out_shape=[jax.ShapeDtypeStruct((seq, Q_W), F32), jax.ShapeDtypeStruct((seq, 2 * KV_W), F32),
                   jax.ShapeDtypeStruct((n_ctx, 2 * KV_W), F32), jax.ShapeDtypeStruct((8, LANES), F32)]
        + [jax.ShapeDtypeStruct(v.shape, v.dtype) for v in scatter],
        scratch_shapes=_comm_scratch(ns) if ns else [],
        compiler_params=_cp("arbitrary"),
    )(q, kv, kv, kv, kvc, sink, dya, *scatter)
    return out[:4], out[4:]


def _proj_bwd(dq, dkv, dpb, x, dr1, modx, w_in, cos, sin, tm, scatter=()):
    seq = x.shape[0]
    pw = IN_W - Q_W - 2 * KV_W
    ns = len(scatter)

    def body(dq_ref, dkv_ref, dpb_ref, x_ref, dr1_ref, mod_ref, w_ref, cos_ref, sin_ref, *rest):
        dqkv_ref, gx_ref, st_ref = rest[ns:ns + 3]
        comm = _AllToAll(rest[:ns], rest[ns + 3:2 * ns + 3], *rest[2 * ns + 3:]) if ns else None
        _host_start(pl.program_id(0), comm)

        @pl.when(pl.program_id(0) == 0)
        def _():
            st_ref[...] = jnp.zeros_like(st_ref)

        cos1, sin1 = cos_ref[...], sin_ref[...]
        cos2 = jnp.concatenate([cos1, cos1], axis=1)
        sin2 = jnp.concatenate([sin1, sin1], axis=1)
        for j in range(Q_W // 256):
            cs = slice(256 * j, 256 * (j + 1))
            dqkv_ref[:, cs] = _unrope(dq_ref[:, cs], cos2, sin2).astype(BF16)
        dqkv_ref[:, Q_W:Q_W + KV_W] = _unrope(dkv_ref[:, :KV_W], cos1, sin1).astype(BF16)
        dqkv_ref[:, Q_W + KV_W:] = dkv_ref[:, KV_W:].astype(BF16)
        o = Q_W + 2 * KV_W
        dh = _dot_nt(dqkv_ref[...], w_ref[:, :o]) + _dot_nt(dpb_ref[...], w_ref[:, o:])
        xhat, rstd = _ln(x_ref[...])
        st_ref[0:1, :] += _colsum(dh)
        st_ref[1:2, :] += _colsum(dh * xhat)
        gx_ref[...] = _ln_bwd(dh * (1.0 + mod_ref[1:2, :]), xhat, rstd) + ALPHA * dr1_ref[...]
        _host_finish(pl.program_id(0), seq // tm - 1, comm)

    out = pl.pallas_call(
        body, name="proj_bwd", grid=(seq // tm,),
        in_specs=[_rows(tm, Q_W), _rows(tm, 2 * KV_W), _rows(tm, pw), _rows(tm, D), _rows(tm, D), _acc((8, D)),
                  _resident((D, IN_W)), _rows(tm, LANES), _rows(tm, LANES)] + _comm_specs(ns),
        out_specs=[_rows(tm, Q_W + 2 * KV_W), _rows(tm, D), _acc((8, D))] + _comm_specs(ns),
        out_shape=[jax.ShapeDtypeStruct((seq, Q_W + 2 * KV_W), BF16), jax.ShapeDtypeStruct((seq, D), F32),
                   jax.ShapeDtypeStruct((8, D), F32)] + [jax.ShapeDtypeStruct(v.shape, v.dtype) for v in scatter],
        scratch_shapes=_comm_scratch(ns) if ns else [],
        compiler_params=_cp("arbitrary"),
    )(dq, dkv, dpb, x, dr1, modx, w_in, cos, sin, *scatter)
    return out[:3], out[3:]


def _ctx_bwd(dkvc, ctx, hc, w_kv):
    n_ctx = ctx.shape[0]

    def body(dkvc_ref, ctx_ref, hc_ref, w_ref, dw_ref, st_ref):
        d = dkvc_ref[...].astype(BF16)
        dw_ref[...] = _dot_tn(hc_ref[...], d)
        dhc = _dot_nt(d, w_ref[...])
        xhat, _ = _ln(ctx_ref[...])
        st_ref[...] = jnp.zeros_like(st_ref)
        st_ref[0:1, :] = _colsum(dhc)
        st_ref[1:2, :] = _colsum(dhc * xhat)

    return pl.pallas_call(
        body, name="ctx_bwd", grid=(1,),
        in_specs=[_acc((n_ctx, 2 * KV_W)), _acc((n_ctx, D)), _acc((n_ctx, D)), _acc((D, 2 * KV_W))],
        out_specs=[_acc((D, 2 * KV_W)), _acc((8, D))],
        out_shape=[jax.ShapeDtypeStruct((D, 2 * KV_W), F32), jax.ShapeDtypeStruct((8, D), F32)],
        compiler_params=_cp("arbitrary"),
    )(dkvc, ctx, hc, w_kv)


def _tn_matmul(a, b, tn, name, out_dtype, shard_major=False, init=None, tk=512):
    t, ka = a.shape
    n = b.shape[1]
    tk = min(tk, t)
    nk = t // tk
    has_init = init is not None

    def body(*refs):
        if has_init:
            a_ref, b_ref, i_ref, o_ref, acc_ref = refs
        else:
            a_ref, b_ref, o_ref, acc_ref = refs
        k = pl.program_id(1)

        @pl.when(k == 0)
        def _():
            acc_ref[...] = i_ref[...] if has_init else jnp.zeros_like(acc_ref)

        acc_ref[...] += _dot_tn(a_ref[...], b_ref[...])

        @pl.when(k == nk - 1)
        def _():
            o_ref[...] = acc_ref[...].astype(out_dtype)

    in_specs = [pl.BlockSpec((tk, ka), lambda j, k: (k, 0)), pl.BlockSpec((tk, tn), lambda j, k: (k, j))]
    args = [a, b]
    if has_init:
        in_specs.append(pl.BlockSpec((ka, tn), lambda j, k: (0, j)))
        args.append(init)
    if shard_major:
        out_spec = pl.BlockSpec((None, ka, tn), lambda j, k: (j, 0, 0))
        out_shape = jax.ShapeDtypeStruct((n // tn, ka, tn), out_dtype)
    else:
        out_spec = pl.BlockSpec((ka, tn), lambda j, k: (0, j))
        out_shape = jax.ShapeDtypeStruct((ka, n), out_dtype)
    return pl.pallas_call(
        body, name=name, grid=(n // tn, nk), in_specs=in_specs, out_specs=out_spec, out_shape=out_shape,
        scratch_shapes=[pltpu.VMEM((ka, tn), F32)],
        compiler_params=_cp("arbitrary", "arbitrary"),
    )(*args)


ADA_TILE = 512


def _ada_fwd(sc_all, w_ada):
    cs = w_ada.shape[1]

    def body(s_ref, w_ref, o_ref):
        o_ref[...] = _dot(s_ref[...].astype(BF16), w_ref[...].astype(BF16))

    return pl.pallas_call(
        body, name="ada_fwd", grid=(cs // ADA_TILE,),
        in_specs=[_acc((16, D)), pl.BlockSpec((D, ADA_TILE), lambda j: (0, j))],
        out_specs=pl.BlockSpec((16, ADA_TILE), lambda j: (0, j)),
        out_shape=jax.ShapeDtypeStruct((16, cs), F32),
        compiler_params=_cp("arbitrary"),
    )(sc_all, w_ada)


def _ada_bwd(sc_all_t, dm_all, dmc, w_ada):
    cs = w_ada.shape[1]

    def body(st_ref, dm_ref, dmc_ref, w_ref, gw_ref, part_ref):
        @pl.when(pl.program_id(0) == 0)
        def _():
            part_ref[...] = jnp.zeros_like(part_ref)

        gw_ref[...] = _dot(st_ref[...].astype(BF16), dm_ref[...].astype(BF16))
        part_ref[...] += _dot_nt(dmc_ref[...].astype(BF16), w_ref[...].astype(BF16))

    return pl.pallas_call(
        body, name="ada_bwd", grid=(cs // ADA_TILE,),
        in_specs=[_acc((D, 16)), pl.BlockSpec((16, ADA_TILE), lambda j: (0, j)), pl.BlockSpec((8, ADA_TILE), lambda j: (0, j)),
                  pl.BlockSpec((D, ADA_TILE), lambda j: (0, j))],
        out_specs=[pl.BlockSpec((D, ADA_TILE), lambda j: (0, j)), _acc((8, D))],
        out_shape=[jax.ShapeDtypeStruct((D, cs), F32), jax.ShapeDtypeStruct((8, D), F32)],
        compiler_params=_cp("arbitrary"),
    )(sc_all_t, dm_all, dmc, w_ada)


def _sum8(x, name, tr=256):
    _, r, c = x.shape
    tr = min(tr, r)
    while r % tr:
        tr -= 16

    def body(x_ref, o_ref):
        acc = x_ref[0].astype(F32)
        for i in range(1, N_DEV):
            acc = acc + x_ref[i].astype(F32)
        o_ref[...] = acc

    return pl.pallas_call(
        body, name=name, grid=(r // tr,),
        in_specs=[pl.BlockSpec((N_DEV, tr, c), lambda i: (0, i, 0))],
        out_specs=pl.BlockSpec((tr, c), lambda i: (i, 0)),
        out_shape=jax.ShapeDtypeStruct((r, c), F32),
        compiler_params=_cp("arbitrary"),
    )(x)


def _sum_blocks(recv, src, me, name, tr=256):
    _, r, c = recv.shape
    tr = min(tr, r)
    while r % tr:
        tr -= 16

    def body(me_ref, recv_ref, own_ref, o_ref):
        acc = own_ref[...].astype(F32)
        for k in range(1, N_DEV):
            acc = acc + recv_ref[me_ref[0] ^ k].astype(F32)
        o_ref[...] = acc

    return pl.pallas_call(
        body, name=name,
        grid_spec=pltpu.PrefetchScalarGridSpec(
            num_scalar_prefetch=1, grid=(r // tr,),
            in_specs=[pl.BlockSpec((N_DEV, tr, c), lambda i, me_ref: (0, i, 0)),
                      pl.BlockSpec((None, tr, c), lambda i, me_ref: (me_ref[0], i, 0))],
            out_specs=pl.BlockSpec((tr, c), lambda i, me_ref: (i, 0))),
        out_shape=jax.ShapeDtypeStruct((r, c), F32),
        compiler_params=_cp("arbitrary"),
    )(me, recv, src)


def _sum8_many(xs, name):
    n = len(xs)

    def body(*refs):
        for x_ref, o_ref in zip(refs[:n], refs[n:]):
            acc = x_ref[0]
            for i in range(1, N_DEV):
                acc = acc + x_ref[i]
            o_ref[...] = acc

    vmem = pl.BlockSpec(memory_space=pltpu.VMEM)
    return pl.pallas_call(
        body, name=name, in_specs=[vmem] * n, out_specs=[vmem] * n,
        out_shape=[jax.ShapeDtypeStruct(v.shape[1:], v.dtype) for v in xs],
        compiler_params=pltpu.CompilerParams(vmem_limit_bytes=VMEM_LIMIT),
    )(*xs)


def _adam_update(w, g, m, v):
    nm = ADAM_B1 * m + (1.0 - ADAM_B1) * g
    nv = ADAM_B2 * v + (1.0 - ADAM_B2) * (g * g)
    m_hat = nm / (1.0 - ADAM_B1 ** ADAM_STEP)
    v_hat = nv / (1.0 - ADAM_B2 ** ADAM_STEP)
    return -ADAM_LR * (m_hat / (jnp.sqrt(v_hat) + ADAM_EPS) + ADAM_WD * w), nm, nv


ROW_LOSS, ROW_LN2_G, ROW_LN2_B, ROW_LN1_G, ROW_LN1_B = 0, 1, 2, 10, 11
ROWS_DMOD_X = (16, 17, 12, 9, 8, 3)
ROWS_DMOD_C = (24, 25)
SMALL = ("c_ctx", "b_ada", "attn_sink", "gmlp_ln_g", "gmlp_ln_b", "w_spatial", "b_spatial", "ln1_g", "ln1_b", "ln2_g", "ln2_b")


def _adamw_small(sums, dsc, w, m, v):
    n = len(SMALL)

    def body(*refs):
        st_ref, gm_ref, sk_ref, ws_ref, bs_ref, dsc_ref = refs[:6]
        w_refs = dict(zip(SMALL, refs[6:6 + n]))
        m_refs = dict(zip(SMALL, refs[6 + n:6 + 2 * n]))
        v_refs = dict(zip(SMALL, refs[6 + 2 * n:6 + 3 * n]))
        outs = refs[6 + 3 * n:]
        c = w_refs["c_ctx"][...]
        sg = _sig(c)
        dmod = [st_ref[r:r + 1, :] for r in ROWS_DMOD_X]
        dmod[0] = dmod[0] + st_ref[ROWS_DMOD_C[0]:ROWS_DMOD_C[0] + 1, :]
        dmod[1] = dmod[1] + st_ref[ROWS_DMOD_C[1]:ROWS_DMOD_C[1] + 1, :]
        grads = dict(
            c_ctx=dsc_ref[0:1, :] * (sg * (1.0 + c * (1.0 - sg))),
            b_ada=jnp.concatenate(dmod, axis=1),
            attn_sink=sk_ref[0:1, 0:N_KV * GROUP],
            gmlp_ln_g=gm_ref[0:1, :], gmlp_ln_b=gm_ref[1:2, :],
            w_spatial=ws_ref[...], b_spatial=bs_ref[...],
            ln1_g=st_ref[ROW_LN1_G:ROW_LN1_G + 1, :], ln1_b=st_ref[ROW_LN1_B:ROW_LN1_B + 1, :],
            ln2_g=st_ref[ROW_LN2_G:ROW_LN2_G + 1, :], ln2_b=st_ref[ROW_LN2_B:ROW_LN2_B + 1, :])
        for i, name in enumerate(SMALL):
            g = grads[name]
            d, nm, nv = _adam_update(w_refs[name][...], g, m_refs[name][...], v_refs[name][...])
            outs[i][...] = g
            outs[n + i][...] = d
            outs[2 * n + i][...] = nm
            outs[3 * n + i][...] = nv

    vmem = pl.BlockSpec(memory_space=pltpu.VMEM)
    args = list(sums) + [dsc] + [w[k] for k in SMALL] + [m[k] for k in SMALL] + [v[k] for k in SMALL]
    shapes = [jax.ShapeDtypeStruct(w[k].shape, F32) for k in SMALL]
    out = pl.pallas_call(
        body, name="adamw_small", in_specs=[vmem] * len(args), out_specs=[vmem] * (4 * n), out_shape=shapes * 4,
        compiler_params=pltpu.CompilerParams(vmem_limit_bytes=VMEM_LIMIT),
    )(*args)
    return [dict(zip(SMALL, out[i * n:(i + 1) * n])) for i in range(4)]


def _adamw(w, g, m, v, name):
    r, c = w.shape
    tr = r if r * c <= 256 * 1024 else min(256, r)
    while r % tr:
        tr -= 8

    def body(w_ref, g_ref, m_ref, v_ref, d_ref, nm_ref, nv_ref):
        d_ref[...], nm_ref[...], nv_ref[...] = _adam_update(w_ref[...], g_ref[...], m_ref[...], v_ref[...])

    spec = pl.BlockSpec((tr, c), lambda i: (i, 0))
    shp = jax.ShapeDtypeStruct((r, c), F32)
    return pl.pallas_call(
        body, name=name, grid=(r // tr,), in_specs=[spec] * 4, out_specs=[spec] * 3, out_shape=[shp] * 3,
        compiler_params=_cp("arbitrary"),
    )(w, g, m, v)


def _my_pos():
    return lax.axis_index("x"), lax.axis_index("y"), lax.axis_index("c")


N_COPY = 7


class _Gather:
    def __init__(self, x_refs, out_refs, send_sems, recv_sems):
        self.x_refs, self.out_refs = x_refs, out_refs
        self.send_sems, self.recv_sems = send_sems, recv_sems
        x, y, c = _my_pos()
        self.c = c
        self.me, self.sibling = (x, y, c), (x, y, 1 - c)
        self.chips = [(1 - x, y), (x, 1 - y), (1 - x, 1 - y)]

    def _copy(self, a, k, block, to, from_input=False):
        px, py, pc = block
        rows = self.out_refs[a].at[4 * px + 2 * py + pc]
        return pltpu.make_async_remote_copy(
            src_ref=self.x_refs[a] if from_input else rows, dst_ref=rows,
            send_sem=self.send_sems.at[a * N_COPY + k], recv_sem=self.recv_sems.at[a * N_COPY + k],
            device_id=to, device_id_type=MESH)

    def start(self):
        n = len(self.x_refs)
        for a in range(n):
            self._copy(a, 0, self.me, self.sibling, from_input=True).start()
        for j, chip in enumerate(self.chips):
            for a in range(n):
                self._copy(a, 1 + j, self.me, (*chip, self.c), from_input=True).start()

    def finish(self):
        n = len(self.x_refs)
        c = self.c
        for j, chip in enumerate(self.chips):
            for a in range(n):
                self._copy(a, 1 + j, (*chip, c), self.me).wait_recv()
                self._copy(a, 4 + j, (*chip, c), self.sibling).start()
        for a in range(n):
            self._copy(a, 0, self.sibling, self.me).wait_recv()
        for j, chip in enumerate(self.chips):
            for a in range(n):
                self._copy(a, 4 + j, (*chip, 1 - c), self.me).wait_recv()
        for a in range(n):
            self._copy(a, 0, self.me, self.sibling, from_input=True).wait_send()
            for j, chip in enumerate(self.chips):
                self._copy(a, 1 + j, self.me, (*chip, c), from_input=True).wait_send()
                self._copy(a, 4 + j, (*chip, c), self.sibling).wait_send()


def _comm_scratch(n):
    return [pltpu.SemaphoreType.DMA((n * N_COPY,)), pltpu.SemaphoreType.DMA((n * N_COPY,))]


def _comm_specs(n):
    return [pl.BlockSpec(memory_space=pl.ANY)] * n


def _gathered_shapes(xs):
    return [jax.ShapeDtypeStruct((N_DEV,) + v.shape, v.dtype) for v in xs]


def _with_own(gathered, xs, me):
    return [lax.dynamic_update_index_in_dim(g, v, me, 0) for g, v in zip(gathered, xs)]


def _all_gather(xs, me, name):
    n = len(xs)

    def body(*refs):
        g = _Gather(refs[:n], refs[n:2 * n], *refs[2 * n:])
        g.start()
        g.finish()

    out = pl.pallas_call(
        body, name=name, out_shape=_gathered_shapes(xs), in_specs=_comm_specs(n), out_specs=_comm_specs(n),
        scratch_shapes=_comm_scratch(n),
    )(*xs)
    return _with_own(out, xs, me)


class _AllToAll:
    def __init__(self, x_refs, out_refs, send_sems, recv_sems):
        self.x_refs, self.out_refs = x_refs, out_refs
        self.send_sems, self.recv_sems = send_sems, recv_sems
        self.pos = _my_pos()
        x, y, c = self.pos
        self.me = 4 * x + 2 * y + c

    def _peer(self, k):
        x, y, c = self.pos
        return (x ^ ((k >> 2) & 1), y ^ ((k >> 1) & 1), c ^ (k & 1))

    def _copy(self, a, k):
        p = self._peer(k)
        return pltpu.make_async_remote_copy(
            src_ref=self.x_refs[a].at[4 * p[0] + 2 * p[1] + p[2]], dst_ref=self.out_refs[a].at[self.me],
            send_sem=self.send_sems.at[a * N_COPY + k - 1], recv_sem=self.recv_sems.at[a * N_COPY + k - 1],
            device_id=p, device_id_type=MESH)

    def start(self):
        for k in range(1, N_DEV):
            for a in range(len(self.x_refs)):
                self._copy(a, k).start()

    def finish(self):
        for a in range(len(self.x_refs)):
            for k in range(1, N_DEV):
                self._copy(a, k).wait_recv()
            for k in range(1, N_DEV):
                self._copy(a, k).wait_send()


def _all_to_all(blocks, name):
    n = len(blocks)

    def body(*refs):
        t = _AllToAll(refs[:n], refs[n:2 * n], *refs[2 * n:])
        t.start()
        t.finish()

    return pl.pallas_call(
        body, name=name, out_shape=[jax.ShapeDtypeStruct(v.shape, v.dtype) for v in blocks],
        in_specs=_comm_specs(n), out_specs=_comm_specs(n), scratch_shapes=_comm_scratch(n),
    )(*blocks)


def _sibling_exchange(xs, name):
    n = len(xs)

    def body(*refs):
        x_refs, out_refs = refs[:n], refs[n:2 * n]
        send_sems, recv_sems = refs[2 * n:]
        x, y, c = _my_pos()

        def push(a):
            return pltpu.make_async_remote_copy(
                src_ref=x_refs[a], dst_ref=out_refs[a], send_sem=send_sems.at[a], recv_sem=recv_sems.at[a],
                device_id=(x, y, 1 - c), device_id_type=MESH)

        for a in range(n):
            push(a).start()
        for a in range(n):
            push(a).wait_recv()
            push(a).wait_send()

    return pl.pallas_call(
        body, name=name, out_shape=[jax.ShapeDtypeStruct(v.shape, v.dtype) for v in xs],
        in_specs=_comm_specs(n), out_specs=_comm_specs(n),
        scratch_shapes=[pltpu.SemaphoreType.DMA((n,)), pltpu.SemaphoreType.DMA((n,))],
    )(*xs)


def _scatter_and_gather(scatter, gather, name):
    ns, ng = len(scatter), len(gather)

    def body(*refs):
        s_in, g_in = refs[:ns], refs[ns:ns + ng]
        s_out, g_out = refs[ns + ng:2 * ns + ng], refs[2 * ns + ng:2 * (ns + ng)]
        s_send, s_recv, g_send, g_recv = refs[2 * (ns + ng):]
        g = _Gather(g_in, g_out, g_send, g_recv)
        t = _AllToAll(s_in, s_out, s_send, s_recv)
        g.start()
        t.start()
        g.finish()
        t.finish()

    out = pl.pallas_call(
        body, name=name,
        out_shape=[jax.ShapeDtypeStruct(v.shape, v.dtype) for v in scatter] + _gathered_shapes(gather),
        in_specs=_comm_specs(ns + ng), out_specs=_comm_specs(ns + ng),
        scratch_shapes=_comm_scratch(ns) + _comm_scratch(ng),
    )(*scatter, *gather)
    return out[:ns], out[ns:]


def _row_tile(seq, want):
    return min(want, seq)


def _local_step(x, ctx, tgt, mod_x, mod_c, wb, sink, gmlp_g, gmlp_b, w_s, b_s, ln1_g, ln1_b, ln2_g, ln2_b,
                later=None, me=None):
    seq = x.shape[0]
    on_mesh = me is not None
    modx1 = jnp.concatenate([mod_x[0:2], jnp.zeros((6, D), F32)], axis=0)
    modc = jnp.concatenate([mod_c[0:2], jnp.zeros((6, D), F32)], axis=0)
    vec = jnp.concatenate([mod_x[2:3], ln1_g, ln1_b, mod_x[3:6], ln2_g, ln2_b], axis=0)
    gp = jnp.concatenate([gmlp_g, gmlp_b, jnp.zeros((6, G_W), F32)], axis=0)
    ws_stack = w_s.reshape(N_GRP * BLK, BLK).astype(BF16)
    ws_stack_t = jnp.transpose(w_s, (2, 0, 1)).reshape(BLK, N_GRP * BLK).astype(BF16)
    bias_full = jnp.repeat(b_s.T, GRP_D, axis=1)
    cos, sin = _rope_tables(seq)
    w_in = wb["w_in"]
    w_kv = w_in[:, Q_W:Q_W + 2 * KV_W]
    tm_big = _row_tile(seq, 512)
    tm_ffn = _row_tile(seq, 256)

    hc, kvc = _ctx_fwd(ctx, modc, w_kv)
    behind_proj = ("w_a", "w_b", "w_o") if on_mesh else ()
    behind_attn = ("w_fi", "w_fo") if on_mesh else ()
    (h, q, kv, uv, gab), got_proj = _proj_fwd(x, modx1, w_in, cos, sin, tm_big, gather=[later[n] for n in behind_proj])
    ya, got_attn = _attn_fwd(q, kv, kvc, sink, gather=[later[n] for n in behind_attn])
    if on_mesh:
        wb = dict(wb)
        names = behind_proj + behind_attn
        for n, g in zip(names, _with_own(list(got_proj) + list(got_attn), [later[n] for n in names], me)):
            wb[n] = g.reshape(-1, g.shape[2]) if n in ROW_SHARDED else g.reshape(N_SHARD, 2 * g.shape[1], g.shape[2])
    a, b, mix, merged, yb = _mix_fwd(uv, gab, ya, gp, ws_stack, bias_full, wb["w_a"], wb["w_b"], wb["w_o"], tm_big)
    gu, act, h2, dr2, st5 = _ffn_fwd(x, mix, tgt, vec, wb["w_fi"], wb["w_fo"], tm_ffn)

    dff, df, dr1, st5b = _ffn_bwd(dr2, gu, x, mix, vec, wb["w_fi"], wb["w_fo"], tm_ffn)
    g_w_fo = _tn_matmul(act, df, 512, "tn_w_ffn_out", BF16)
    g_w_fi = _tn_matmul(h2, dff, FH_SHARD, "tn_w_ffn_in", BF16, shard_major=True)
    dmix, da, db, dya, dpb, dws, dbs_full, st4 = _mix_bwd(
        dr1, a, b, gab, uv, vec, gp, ws_stack, ws_stack_t, bias_full, wb["w_a"], wb["w_b"], wb["w_o"], _row_tile(seq, 256))
    g_w_o = _tn_matmul(merged, dmix, 1024, "tn_w_out", BF16)
    g_w_a = _tn_matmul(ya, da, D // N_SHARD, "tn_w_branch_a", BF16, shard_major=True, tk=2048)
    g_w_b = _tn_matmul(yb, db, D // N_SHARD, "tn_w_branch_b", BF16, shard_major=True, tk=2048)
    blocks = dict(w_fi=_eighths(g_w_fi), w_fo=_eighths(g_w_fo), w_o=_eighths(g_w_o), w_a=_eighths(g_w_a), w_b=_eighths(g_w_b))
    early = tuple(blocks) if on_mesh else ()
    (dq, dkv, dkvc, dsink), recv_early = _attn_bwd(q, kv, kvc, sink, dya, scatter=[blocks[n] for n in early])
    g_wkv_ctx, st0 = _ctx_bwd(dkvc, ctx, hc, w_kv)
    (dqkv, grad_x, st1), _ = _proj_bwd(dq, dkv, dpb, x, dr1, modx1, w_in, cos, sin, tm_big)
    o = Q_W + 2 * KV_W
    init = jnp.pad(g_wkv_ctx, ((0, 0), (Q_W, 0)))
    g_w_in = jnp.concatenate([_tn_matmul(h, dqkv, o, "tn_w_in_qkv", F32, init=init),
                              _tn_matmul(h, dpb, 1536, "tn_w_in_rest", F32)], axis=1)
    g_w_in = g_w_in.reshape(D, N_SHARD, IN_W // N_SHARD).transpose(1, 0, 2).astype(BF16)

    dbs = jnp.sum(dbs_full.reshape(BLK, N_GRP, GRP_D), axis=2).T
    parts = [jnp.concatenate([st5, st5b, st1, st0], axis=0), st4, dsink, dws, dbs]
    blocks["w_in"] = _eighths(g_w_in)
    return grad_x, parts, blocks, dict(zip(early, recv_early))


BIG = ("w_in", "w_a", "w_b", "w_o", "w_fi", "w_fo")
ROW_SHARDED = ("w_o", "w_fo")


def _half_of_shard(shard, c):
    r = shard.shape[0]
    return lax.dynamic_slice_in_dim(shard, c * (r // 2), r // 2, axis=0)


def _eighths(v):
    rows = v.shape[-2] * (v.shape[0] if v.ndim == 3 else 1)
    return v.reshape(N_DEV, rows // N_DEV, v.shape[-1])


def kernel(x, c, ctx, c_ctx, w_ada, b_ada, w_in, attn_sink, gmlp_ln_g, gmlp_ln_b, w_spatial, b_spatial, w_branch_a, w_branch_b, w_out, ln1_g, ln1_b, w_ffn_in, w_ffn_out, ln2_g, ln2_b, loss_target, m_c_ctx, m_w_ada, m_b_ada, m_w_in, m_attn_sink, m_gmlp_ln_g, m_gmlp_ln_b, m_w_spatial, m_b_spatial, m_w_branch_a, m_w_branch_b, m_w_out, m_ln1_g, m_ln1_b, m_w_ffn_in, m_w_ffn_out, m_ln2_g, m_ln2_b, v_c_ctx, v_w_ada, v_b_ada, v_w_in, v_attn_sink, v_gmlp_ln_g, v_gmlp_ln_b, v_w_spatial, v_b_spatial, v_w_branch_a, v_w_branch_b, v_w_out, v_ln1_g, v_ln1_b, v_w_ffn_in, v_w_ffn_out, v_ln2_g, v_ln2_b):
    mx, my, mc = _my_pos()
    me = 4 * mx + 2 * my + mc
    chip = 2 * mx + my
    shards = dict(w_in=w_in[0], w_a=w_branch_a[0], w_b=w_branch_b[0], w_o=w_out[0], w_fi=w_ffn_in[0], w_fo=w_ffn_out[0])

    halves = {n: _half_of_shard(shards[n], mc).astype(BF16) for n in BIG}
    c_rows = jnp.concatenate([c, jnp.zeros((7, D), F32)], axis=0)
    g_in, c_g = _all_gather([halves["w_in"], c_rows], me, "gather_w_in")
    r, cdim = shards["w_in"].shape
    wb = dict(w_in=g_in.reshape(N_SHARD, r, cdim).transpose(1, 0, 2).reshape(r, N_SHARD * cdim))

    c_all = c_g[:, 0, :]
    cc = jnp.concatenate([c_all, c_ctx[None, :], jnp.zeros((7, D), F32)], axis=0)
    sig_cc = jax.nn.sigmoid(cc)
    sc_all = cc * sig_cc
    mod_shard = _ada_fwd(sc_all, w_ada[0])
    mod_g = _all_gather([mod_shard], me, "gather_mod")[0]
    mod_all = jnp.concatenate([mod_g[2 * s] for s in range(4)], axis=1) + b_ada
    mod_x = lax.dynamic_slice_in_dim(mod_all, me, 1, axis=0).reshape(6, D)
    mod_c = mod_all[8].reshape(6, D)[0:2]

    grad_x, parts, blocks, recv = _local_step(
        x[0], ctx[0], loss_target[0], mod_x, mod_c, wb, attn_sink, gmlp_ln_g, gmlp_ln_b, w_spatial[0], b_spatial[0],
        ln1_g, ln1_b, ln2_g, ln2_b, later=halves, me=me)

    (recv["w_in"],), gathered = _scatter_and_gather([blocks["w_in"]], parts, "scatter_w_in_gather_small")
    gathered = _with_own(gathered, parts, me)

    me_arr = jnp.reshape(me, (1,)).astype(jnp.int32)
    summed = {n: _sum_blocks(recv[n], blocks[n], me_arr, "sum_grads_" + n) for n in BIG}
    theirs = _sibling_exchange([summed[n] for n in BIG], "exchange_grads")
    g_shard = {}
    for n, other in zip(BIG, theirs):
        lo = jnp.where(mc == 0, summed[n], other)
        hi = jnp.where(mc == 0, other, summed[n])
        g_shard[n] = jnp.concatenate([lo, hi], axis=0)

    sums = _sum8_many(gathered, "sum_small")
    stats = sums[0]
    loss = 0.5 * jnp.sum(stats[ROW_LOSS]) / D
    dmod_x_all = jnp.concatenate([gathered[0][:, r_, :] for r_ in ROWS_DMOD_X], axis=1)
    dmod_c_full = jnp.concatenate([stats[r_] for r_ in ROWS_DMOD_C] + [jnp.zeros((4 * D,), F32)])
    dm_rows = jnp.concatenate([dmod_x_all, dmod_c_full[None, :], jnp.zeros((7, 6 * D), F32)], axis=0)
    cs = w_ada.shape[2]
    dm_shard = lax.dynamic_slice_in_dim(dm_rows, chip * cs, cs, axis=1)
    dmc_shard = jnp.concatenate([dm_shard[8:9], jnp.zeros((7, cs), F32)], axis=0)
    g_w_ada, part = _ada_bwd(sc_all.T, dm_shard, dmc_shard, w_ada[0])
    part_all = _all_gather([part * (mc == 0).astype(F32)], me, "gather_c_ctx")[0]
    dsc = _sum8(part_all, "sum_c_ctx")

    grads = dict(w_ada=g_w_ada[None], w_in=g_shard["w_in"][None], w_branch_a=g_shard["w_a"][None],
                 w_branch_b=g_shard["w_b"][None], w_out=g_shard["w_o"][None], w_ffn_in=g_shard["w_fi"][None],
                 w_ffn_out=g_shard["w_fo"][None])
    weights = dict(c_ctx=c_ctx, w_ada=w_ada, b_ada=b_ada, w_in=w_in, attn_sink=attn_sink, gmlp_ln_g=gmlp_ln_g,
                   gmlp_ln_b=gmlp_ln_b, w_spatial=w_spatial, b_spatial=b_spatial, w_branch_a=w_branch_a,
                   w_branch_b=w_branch_b, w_out=w_out, ln1_g=ln1_g, ln1_b=ln1_b, w_ffn_in=w_ffn_in, w_ffn_out=w_ffn_out,
                   ln2_g=ln2_g, ln2_b=ln2_b)
    ms = dict(c_ctx=m_c_ctx, w_ada=m_w_ada, b_ada=m_b_ada, w_in=m_w_in, attn_sink=m_attn_sink, gmlp_ln_g=m_gmlp_ln_g,
              gmlp_ln_b=m_gmlp_ln_b, w_spatial=m_w_spatial, b_spatial=m_b_spatial, w_branch_a=m_w_branch_a,
              w_branch_b=m_w_branch_b, w_out=m_w_out, ln1_g=m_ln1_g, ln1_b=m_ln1_b, w_ffn_in=m_w_ffn_in,
              w_ffn_out=m_w_ffn_out, ln2_g=m_ln2_g, ln2_b=m_ln2_b)
    vs = dict(c_ctx=v_c_ctx, w_ada=v_w_ada, b_ada=v_b_ada, w_in=v_w_in, attn_sink=v_attn_sink, gmlp_ln_g=v_gmlp_ln_g,
              gmlp_ln_b=v_gmlp_ln_b, w_spatial=v_w_spatial, b_spatial=v_b_spatial, w_branch_a=v_w_branch_a,
              w_branch_b=v_w_branch_b, w_out=v_w_out, ln1_g=v_ln1_g, ln1_b=v_ln1_b, w_ffn_in=v_w_ffn_in,
              w_ffn_out=v_w_ffn_out, ln2_g=v_ln2_g, ln2_b=v_ln2_b)
    order = list(weights)
    large = ("w_ada", "w_in", "w_branch_a", "w_branch_b", "w_out", "w_ffn_in", "w_ffn_out")
    delta, new_m, new_v = {}, {}, {}
    for n in large:
        d_, m_, v_ = _adamw(weights[n][0], grads[n][0], ms[n][0], vs[n][0], "adamw_" + n)
        delta[n], new_m[n], new_v[n] = d_[None], m_[None], v_[None]

    def view(a):
        return a.reshape(-1, a.shape[-1]) if a.ndim != 1 else a.reshape(1, -1)

    small = _adamw_small(sums, dsc, *[{n: view(d[n]) for n in SMALL} for d in (weights, ms, vs)])
    for out, src in zip((grads, delta, new_m, new_v), small):
        for n in SMALL:
            out[n] = src[n].reshape(weights[n].shape)

    return (loss, grad_x[None], *[grads[n] for n in order], *[delta[n] for n in order],
            *[new_m[n] for n in order], *[new_v[n] for n in order])
```

```python
import functools
import math

import jax
import jax.numpy as jnp
from jax import lax
from jax.experimental import pallas as pl
from jax.experimental.pallas import tpu as pltpu

F32 = jnp.float32
BF16 = jnp.bfloat16

D = 1024
HEAD = 64
N_KV = 2
GROUP = 4
Q_W = 512
KV_W = 128
G_W = 512
BLK = 128
N_GRP = 8
GRP_D = 64
FH = 2816
IN_W = 3840
GRID_W = 64
ROPE_BASE = 10000.0
LN_EPS = 1e-5
NEG = -1e30
ALPHA = (2 * 1) ** 0.25
SCALE = HEAD ** -0.5
GELU_K = math.sqrt(2.0 / math.pi)
GELU_A = 0.044715
ADAM_LR = 0.001
ADAM_B1 = 0.9
ADAM_B2 = 0.999
ADAM_EPS = 1e-08
ADAM_WD = 0.01
ADAM_STEP = 10
N_DEV = 8
N_SHARD = 4
FH_SHARD = FH // 2
LANES = 128
VMEM_LIMIT = 56 * 1024 * 1024
MESH = pl.DeviceIdType.MESH


def _cp(*sem):
    return pltpu.CompilerParams(dimension_semantics=sem, vmem_limit_bytes=VMEM_LIMIT)


def _resident(shape):
    return pl.BlockSpec(shape, lambda *_: (0,) * len(shape), pipeline_mode=pl.Buffered(1))


def _rows(tm, width):
    return pl.BlockSpec((tm, width), lambda i: (i, 0))


def _acc(shape):
    return pl.BlockSpec(shape, lambda *_: (0,) * len(shape))


def _dot(a, b):
    return jnp.dot(a, b, preferred_element_type=F32)


def _dot_nt(a, b):
    return lax.dot_general(a, b, (((1,), (1,)), ((), ())), preferred_element_type=F32)


def _dot_tn(a, b):
    return lax.dot_general(a, b, (((0,), (0,)), ((), ())), preferred_element_type=F32)


def _ln(x):
    mu = jnp.mean(x, axis=-1, keepdims=True)
    xc = x - mu
    var = jnp.mean(xc * xc, axis=-1, keepdims=True)
    rstd = lax.rsqrt(var + LN_EPS)
    return xc * rstd, rstd


def _ln_bwd(dxhat, xhat, rstd):
    return (dxhat - jnp.mean(dxhat, axis=-1, keepdims=True)
            - xhat * jnp.mean(dxhat * xhat, axis=-1, keepdims=True)) * rstd


def _sig(x):
    return 1.0 / (1.0 + jnp.exp(-x))


def _gelu(x):
    t = jnp.tanh(GELU_K * (x + GELU_A * x * x * x))
    return 0.5 * x * (1.0 + t), t


def _gelu_grad(x, t):
    return 0.5 * (1.0 + t) + 0.5 * x * (1.0 - t * t) * GELU_K * (1.0 + 3.0 * GELU_A * x * x)


def _colsum(v):
    return jnp.sum(v, axis=0, keepdims=True)


def _partner(x):
    w = x.shape[1]
    lane = lax.broadcasted_iota(jnp.int32, x.shape, 1)
    return jnp.where((lane & 31) < 16, pltpu.roll(x, w - 16, 1), pltpu.roll(x, 16, 1))


def _rope(x, cos, sin):
    return x * cos + _partner(x) * sin


def _unrope(g, cos, sin):
    return g * cos + _partner(g * sin)


def _rope_tables(seq):
    inv = ROPE_BASE ** (-jnp.arange(HEAD // 4, dtype=F32) / (HEAD // 4))
    pos = jnp.arange(seq, dtype=jnp.int32)
    ar = (pos // GRID_W).astype(F32)[:, None] * inv
    ac = (pos % GRID_W).astype(F32)[:, None] * inv
    cos = jnp.concatenate([jnp.cos(ar), jnp.cos(ar), jnp.cos(ac), jnp.cos(ac)], axis=-1)
    sin = jnp.concatenate([-jnp.sin(ar), jnp.sin(ar), -jnp.sin(ac), jnp.sin(ac)], axis=-1)
    return jnp.tile(cos, (1, LANES // HEAD)), jnp.tile(sin, (1, LANES // HEAD))


def _ctx_fwd(ctx, modc, w_kv):
    n_ctx = ctx.shape[0]

    def body(ctx_ref, mod_ref, w_ref, hc_ref, kvc_ref, vac_ref):
        xhat, _ = _ln(ctx_ref[...])
        hc = (xhat * (1.0 + mod_ref[1:2, :]) + mod_ref[0:1, :]).astype(BF16)
        hc_ref[...] = hc
        kvc = _dot(hc, w_ref[...]).astype(BF16)
        kvc_ref[...] = kvc
        vac_ref[...] = _with_ones(kvc[:, KV_W:])

    return pl.pallas_call(
        body, name="ctx_fwd", grid=(1,),
        in_specs=[_acc((n_ctx, D)), _acc((8, D)), _acc((D, 2 * KV_W))],
        out_specs=[_acc((n_ctx, D)), _acc((n_ctx, 2 * KV_W)), _acc((n_ctx, 2 * LANES))],
        out_shape=[jax.ShapeDtypeStruct((n_ctx, D), BF16), jax.ShapeDtypeStruct((n_ctx, 2 * KV_W), BF16),
                   jax.ShapeDtypeStruct((n_ctx, 2 * LANES), BF16)],
        compiler_params=_cp("arbitrary"),
    )(ctx, modc, w_kv)


def _host_start(step, comm):
    if comm is not None:
        @pl.when(step == 0)
        def _():
            comm.start()


def _host_finish(step, last, comm):
    if comm is not None:
        @pl.when(step == last)
        def _():
            comm.finish()


def _proj_fwd(x, modx, w_in, cos, sin, tm, gather=()):
    seq = x.shape[0]
    ng = len(gather)

    def body(x_ref, mod_ref, w_ref, cos_ref, sin_ref, *rest):
        h_ref, q_ref, kv_ref, va_ref, uv_ref, gab_ref = rest[ng:ng + 6]
        comm = _Gather(rest[:ng], rest[ng + 6:2 * ng + 6], *rest[2 * ng + 6:]) if ng else None
        _host_start(pl.program_id(0), comm)
        xhat, _ = _ln(x_ref[...])
        h = (xhat * (1.0 + mod_ref[1:2, :]) + mod_ref[0:1, :]).astype(BF16)
        h_ref[...] = h
        cos1, sin1 = cos_ref[...], sin_ref[...]
        cos2 = jnp.concatenate([cos1, cos1], axis=1)
        sin2 = jnp.concatenate([sin1, sin1], axis=1)
        for j in range(Q_W // 256):
            t = _dot(h, w_ref[:, 256 * j:256 * (j + 1)])
            q_ref[:, 256 * j:256 * (j + 1)] = (_rope(t, cos2, sin2) * SCALE).astype(BF16)
        t = _dot(h, w_ref[:, Q_W:Q_W + 2 * KV_W])
        kv_ref[:, :KV_W] = _rope(t[:, :KV_W], cos1, sin1).astype(BF16)
        v = t[:, KV_W:].astype(BF16)
        kv_ref[:, KV_W:] = v
        va_ref[...] = _with_ones(v)
        o = Q_W + 2 * KV_W
        for j in range(2):
            uv_ref[:, G_W * j:G_W * (j + 1)] = _dot(h, w_ref[:, o + G_W * j:o + G_W * (j + 1)])
        o += 2 * G_W
        for j in range(4):
            gab_ref[:, 512 * j:512 * (j + 1)] = _dot(h, w_ref[:, o + 512 * j:o + 512 * (j + 1)])
        _host_finish(pl.program_id(0), seq // tm - 1, comm)

    out = pl.pallas_call(
        body, name="proj_fwd", grid=(seq // tm,),
        in_specs=[_rows(tm, D), _acc((8, D)), _resident((D, IN_W)), _rows(tm, LANES), _rows(tm, LANES)] + _comm_specs(ng),
        out_specs=[_rows(tm, D), _rows(tm, Q_W), _rows(tm, 2 * KV_W), _rows(tm, 2 * LANES), _rows(tm, 2 * G_W),
                   _rows(tm, 2 * D)] + _comm_specs(ng),
        out_shape=[jax.ShapeDtypeStruct((seq, D), BF16), jax.ShapeDtypeStruct((seq, Q_W), BF16),
                   jax.ShapeDtypeStruct((seq, 2 * KV_W), BF16), jax.ShapeDtypeStruct((seq, 2 * LANES), BF16),
                   jax.ShapeDtypeStruct((seq, 2 * G_W), F32), jax.ShapeDtypeStruct((seq, 2 * D), F32)] + _gathered_shapes(gather),
        scratch_shapes=_comm_scratch(ng) if ng else [],
        compiler_params=_cp("arbitrary"),
    )(x, modx, w_in, cos, sin, *gather)
    return out[:6], out[6:]


def _stack_heads(x, hk):
    return jnp.concatenate([x[:, (hk * GROUP + g) * HEAD:(hk * GROUP + g + 1) * HEAD] for g in range(GROUP)], axis=0)


def _attn_scores(q_ref, k_refs, hk, n, nb):
    q4 = _stack_heads(q_ref[...], hk)
    ks = [r[:, hk * HEAD:(hk + 1) * HEAD] for r in k_refs]
    rows = GROUP * BLK
    qi = lax.broadcasted_iota(jnp.int32, (rows, BLK), 0) & (BLK - 1)
    kj = lax.broadcasted_iota(jnp.int32, (rows, BLK), 1)
    s = [_dot_nt(q4, k) for k in ks]
    s[1] = jnp.where((kj >= qi) & (n > 0), s[1], NEG)
    s[3] = jnp.where((kj <= qi) & (n < nb - 1), s[3], NEG)
    return q4, ks, s


def _sink_rows(sink_ref, hk):
    rows = GROUP * BLK
    rg = lax.broadcasted_iota(jnp.int32, (rows, 1), 0) >> 7
    sink_v = jnp.full((rows, 1), sink_ref[0, hk * GROUP], F32)
    for g in range(1, GROUP):
        sink_v = jnp.where(rg == g, sink_ref[0, hk * GROUP + g], sink_v)
    return sink_v


def _with_ones(v):
    ones = jnp.ones((v.shape[0], HEAD), v.dtype)
    return jnp.concatenate([v[:, :HEAD], ones, v[:, HEAD:], ones], axis=1)


def _kv_specs(nb):
    return [pl.BlockSpec((BLK, 2 * KV_W), lambda n: (jnp.maximum(n - 1, 0), 0)),
            pl.BlockSpec((BLK, 2 * KV_W), lambda n: (n, 0)),
            pl.BlockSpec((BLK, 2 * KV_W), lambda n: (jnp.minimum(n + 1, nb - 1), 0))]


def _attn_fwd(q, kv, va, kvc, vac, sink, gather=()):
    seq = q.shape[0]
    nb = seq // BLK
    n_ctx = kvc.shape[0]
    ng = len(gather)

    def body(q_ref, kvp_ref, kvm_ref, kvn_ref, vap_ref, vam_ref, van_ref, kvc_ref, vac_ref, sink_ref, *rest):
        o_ref, lse_ref = rest[ng:ng + 2]
        comm = _Gather(rest[:ng], rest[ng + 2:2 * ng + 2], *rest[2 * ng + 2:]) if ng else None
        n = pl.program_id(0)
        _host_start(n, comm)
        outs = []
        lane = lax.broadcasted_iota(jnp.int32, (BLK, LANES), 1)
        lse_all = jnp.zeros((BLK, LANES), F32)
        for hk in range(N_KV):
            _, _, s = _attn_scores(q_ref, (kvc_ref, kvp_ref, kvm_ref, kvn_ref), hk, n, nb)
            sink_v = _sink_rows(sink_ref, hk)
            m = sink_v
            for t in s:
                m = jnp.maximum(m, jnp.max(t, axis=-1, keepdims=True))
            o = jnp.zeros((GROUP * BLK, LANES), F32)
            for t, va_ref in zip(s, (vac_ref, vap_ref, vam_ref, van_ref)):
                o = o + _dot(jnp.exp((t - m).astype(BF16)), va_ref[:, hk * LANES:(hk + 1) * LANES])
            denom = o[:, HEAD:HEAD + 1] + jnp.exp(sink_v - m)
            o4 = o[:, :HEAD] * (1.0 / denom)
            lse4 = m + jnp.log(denom)
            for g in range(GROUP):
                outs.append(o4[g * BLK:(g + 1) * BLK, :])
                lse_all = jnp.where(lane == hk * GROUP + g, lse4[g * BLK:(g + 1) * BLK, :], lse_all)
        o_ref[...] = jnp.concatenate(outs, axis=1).astype(BF16)
        lse_ref[...] = lse_all
        _host_finish(n, nb - 1, comm)

    out = pl.pallas_call(
        body, name="attn_fwd", grid=(nb,),
        in_specs=[_rows(BLK, Q_W)] + _kv_specs(nb) + _kv_specs(nb)
        + [_acc((n_ctx, 2 * KV_W)), _acc((n_ctx, 2 * LANES)), pl.BlockSpec(memory_space=pltpu.SMEM)] + _comm_specs(ng),
        out_specs=[_rows(BLK, Q_W), _rows(BLK, LANES)] + _comm_specs(ng),
        out_shape=[jax.ShapeDtypeStruct((seq, Q_W), BF16), jax.ShapeDtypeStruct((seq, LANES), F32)] + _gathered_shapes(gather),
        scratch_shapes=_comm_scratch(ng) if ng else [],
        compiler_params=_cp("arbitrary"),
    )(q, kv, kv, kv, va, va, va, kvc, vac, sink, *gather)
    return out[:2], out[2:]


def _gmlp_chunk(u, vb, gp_ref, ws_ref, bias_ref):
    gu, tu = _gelu(u)
    gv, tv = _gelu(vb)
    vhat, rstd = _ln(gv)
    vn = (vhat * gp_ref[0:1, :] + gp_ref[1:2, :]).astype(BF16)
    r = _dot(ws_ref[...], vn)
    grp = lax.broadcasted_iota(jnp.int32, (BLK, G_W), 1) >> 6
    s = bias_ref[...]
    for g in range(N_GRP):
        s = s + jnp.where(grp == g, r[g * BLK:(g + 1) * BLK, :], 0.0)
    return gu, tu, tv, vhat, rstd, vn, s, grp


def _mix_fwd(uv, gab, ya, gp, ws_stack, bias_full, w_a, w_b, w_o, tm):
    seq = uv.shape[0]

    def body(uv_ref, gab_ref, ya_ref, gp_ref, ws_ref, bias_ref, wa_ref, wb_ref, wo_ref,
             a_ref, b_ref, mix_ref, merged_ref, yb_ref):
        for c in range(tm // BLK):
            rs = slice(c * BLK, (c + 1) * BLK)
            gu, _, _, _, _, _, s, _ = _gmlp_chunk(uv_ref[rs, :G_W], uv_ref[rs, G_W:], gp_ref, ws_ref, bias_ref)
            yb_ref[rs, :] = (gu * s).astype(BF16)
        ya = ya_ref[...]
        yb = yb_ref[...]
        for s in range(N_SHARD):
            cs = slice(s * (D // N_SHARD), (s + 1) * (D // N_SHARD))
            a_ref[:, cs] = _dot(ya, wa_ref[s])
            b_ref[:, cs] = _dot(yb, wb_ref[s])
        merged = (_sig(gab_ref[:, :D]) * a_ref[...] + _sig(gab_ref[:, D:]) * b_ref[...]).astype(BF16)
        merged_ref[...] = merged
        mix_ref[...] = _dot(merged, wo_ref[...])

    return pl.pallas_call(
        body, name="mix_fwd", grid=(seq // tm,),
        in_specs=[_rows(tm, 2 * G_W), _rows(tm, 2 * D), _rows(tm, Q_W), _acc((8, G_W)),
                  _resident((N_GRP * BLK, BLK)), _acc((BLK, G_W)),
                  _resident((N_SHARD, Q_W, D // N_SHARD)), _resident((N_SHARD, G_W, D // N_SHARD)), _resident((D, D))],
        out_specs=[_rows(tm, D), _rows(tm, D), _rows(tm, D), _rows(tm, D), _rows(tm, G_W)],
        out_shape=[jax.ShapeDtypeStruct((seq, D), F32), jax.ShapeDtypeStruct((seq, D), F32),
                   jax.ShapeDtypeStruct((seq, D), F32), jax.ShapeDtypeStruct((seq, D), BF16),
                   jax.ShapeDtypeStruct((seq, G_W), BF16)],
        compiler_params=_cp("arbitrary"),
    )(uv, gab, ya, gp, ws_stack, bias_full, w_a, w_b, w_o)


FFN_CHUNK = 512


def _ffn_chunks():
    out = []
    for hh in range(2):
        off = 0
        while off < FH_SHARD:
            w = min(FFN_CHUNK, FH_SHARD - off)
            out.append((hh, off, w))
            off += w
    return out


def _mid_recompute(x_ref, mix_ref, vec_ref):
    r1 = ALPHA * x_ref[...] + vec_ref[0:1, :] * mix_ref[...]
    xh1, rstd1 = _ln(r1)
    xmid = xh1 * vec_ref[1:2, :] + vec_ref[2:3, :]
    xh2, rstd2 = _ln(xmid)
    return xh1, rstd1, xmid, xh2, rstd2


def _ffn_fwd(x, mix, tgt, vec, w_fi, w_fo, tm):
    seq = x.shape[0]

    def body(x_ref, mix_ref, tgt_ref, vec_ref, wi_ref, wo_ref, gu_ref, act_ref, h2_ref, dr2_ref, st_ref):
        @pl.when(pl.program_id(0) == 0)
        def _():
            st_ref[...] = jnp.zeros_like(st_ref)

        _, _, xmid, xh2, _ = _mid_recompute(x_ref, mix_ref, vec_ref)
        h2 = (xh2 * (1.0 + vec_ref[4:5, :]) + vec_ref[3:4, :]).astype(BF16)
        h2_ref[...] = h2
        f = jnp.zeros((tm, D), F32)
        for hh, off, w in _ffn_chunks():
            cs = slice(hh * FH_SHARD + off, hh * FH_SHARD + off + w)
            cu = slice(FH + hh * FH_SHARD + off, FH + hh * FH_SHARD + off + w)
            g = _dot(h2, wi_ref[hh, :, off:off + w])
            u = _dot(h2, wi_ref[2 + hh, :, off:off + w])
            gu_ref[:, cs] = g
            gu_ref[:, cu] = u
            a = (g * _sig(g) * u).astype(BF16)
            act_ref[:, cs] = a
            f = f + _dot(a, wo_ref[cs, :])
        r2 = ALPHA * xmid + vec_ref[5:6, :] * f
        yh, rstd = _ln(r2)
        y = yh * vec_ref[6:7, :] + vec_ref[7:8, :]
        err = y - tgt_ref[...]
        dy = err / D
        dr2 = _ln_bwd(dy * vec_ref[6:7, :], yh, rstd)
        dr2_ref[...] = dr2
        st_ref[0:1, :] += _colsum(err * err)
        st_ref[1:2, :] += _colsum(dy * yh)
        st_ref[2:3, :] += _colsum(dy)
        st_ref[3:4, :] += _colsum(dr2 * f)

    return pl.pallas_call(
        body, name="ffn_fwd", grid=(seq // tm,),
        in_specs=[_rows(tm, D), _rows(tm, D), _rows(tm, D), _acc((8, D)), _resident((N_SHARD, D, FH_SHARD)), _resident((FH, D))],
        out_specs=[_rows(tm, 2 * FH), _rows(tm, FH), _rows(tm, D), _rows(tm, D), _acc((8, D))],
        out_shape=[jax.ShapeDtypeStruct((seq, 2 * FH), F32), jax.ShapeDtypeStruct((seq, FH), BF16),
                   jax.ShapeDtypeStruct((seq, D), BF16), jax.ShapeDtypeStruct((seq, D), F32),
                   jax.ShapeDtypeStruct((8, D), F32)],
        compiler_params=_cp("arbitrary"),
    )(x, mix, tgt, vec, w_fi, w_fo)


def _ffn_bwd(dr2, gu, x, mix, vec, w_fi, w_fo, tm):
    seq = x.shape[0]

    def body(dr2_ref, gu_ref, x_ref, mix_ref, vec_ref, wi_ref, wo_ref, dff_ref, df_ref, dr1_ref, st_ref):
        @pl.when(pl.program_id(0) == 0)
        def _():
            st_ref[...] = jnp.zeros_like(st_ref)

        dr2 = dr2_ref[...]
        df = (dr2 * vec_ref[5:6, :]).astype(BF16)
        df_ref[...] = df
        dh2 = jnp.zeros((tm, D), F32)
        for hh, off, w in _ffn_chunks():
            cs = slice(hh * FH_SHARD + off, hh * FH_SHARD + off + w)
            cu = slice(FH + hh * FH_SHARD + off, FH + hh * FH_SHARD + off + w)
            da = _dot_nt(df, wo_ref[cs, :])
            g = gu_ref[:, cs]
            u = gu_ref[:, cu]
            sg = _sig(g)
            dg = (da * u * sg * (1.0 + g * (1.0 - sg))).astype(BF16)
            du = (da * g * sg).astype(BF16)
            dff_ref[:, cs] = dg
            dff_ref[:, cu] = du
            dh2 = dh2 + _dot_nt(dg, wi_ref[hh, :, off:off + w]) + _dot_nt(du, wi_ref[2 + hh, :, off:off + w])
        xh1, rstd1, _, xh2, rstd2 = _mid_recompute(x_ref, mix_ref, vec_ref)
        dxmid = _ln_bwd(dh2 * (1.0 + vec_ref[4:5, :]), xh2, rstd2) + ALPHA * dr2
        dr1 = _ln_bwd(dxmid * vec_ref[1:2, :], xh1, rstd1)
        dr1_ref[...] = dr1
        st_ref[0:1, :] += _colsum(dh2 * xh2)
        st_ref[1:2, :] += _colsum(dh2)
        st_ref[2:3, :] += _colsum(dxmid * xh1)
        st_ref[3:4, :] += _colsum(dxmid)
        st_ref[4:5, :] += _colsum(dr1 * mix_ref[...])

    return pl.pallas_call(
        body, name="ffn_bwd", grid=(seq // tm,),
        in_specs=[_rows(tm, D), _rows(tm, 2 * FH), _rows(tm, D), _rows(tm, D), _acc((8, D)),
                  _resident((N_SHARD, D, FH_SHARD)), _resident((FH, D))],
        out_specs=[_rows(tm, 2 * FH), _rows(tm, D), _rows(tm, D), _acc((8, D))],
        out_shape=[jax.ShapeDtypeStruct((seq, 2 * FH), BF16), jax.ShapeDtypeStruct((seq, D), BF16),
                   jax.ShapeDtypeStruct((seq, D), F32), jax.ShapeDtypeStruct((8, D), F32)],
        compiler_params=_cp("arbitrary"),
    )(dr2, gu, x, mix, vec, w_fi, w_fo)


def _mix_bwd(dr1, a, b, gab, uv, vec, gp, ws_stack, ws_stack_t, bias_full, w_a, w_b, w_o, tm):
    seq = dr1.shape[0]

    def body(dr1_ref, a_ref, b_ref, gab_ref, uv_ref, vec_ref, gp_ref, ws_ref, wst_ref, bias_ref, wa_ref, wb_ref, wo_ref,
             dmix_ref, da_ref, db_ref, dya_ref, dp_ref, dws_ref, dbs_ref, st_ref):
        @pl.when(pl.program_id(0) == 0)
        def _():
            dws_ref[...] = jnp.zeros_like(dws_ref)
            dbs_ref[...] = jnp.zeros_like(dbs_ref)
            st_ref[...] = jnp.zeros_like(st_ref)

        dmix = (dr1_ref[...] * vec_ref[0:1, :]).astype(BF16)
        dmix_ref[...] = dmix
        dmerged = _dot_nt(dmix, wo_ref[...])
        sa = _sig(gab_ref[:, :D])
        sb = _sig(gab_ref[:, D:])
        da = (dmerged * sa).astype(BF16)
        db = (dmerged * sb).astype(BF16)
        da_ref[...] = da
        db_ref[...] = db
        dp_ref[:, 2 * G_W:2 * G_W + D] = (dmerged * a_ref[...] * sa * (1.0 - sa)).astype(BF16)
        dp_ref[:, 2 * G_W + D:] = (dmerged * b_ref[...] * sb * (1.0 - sb)).astype(BF16)
        dya = jnp.zeros((tm, Q_W), F32)
        dyb = jnp.zeros((tm, G_W), F32)
        for s in range(N_SHARD):
            cs = slice(s * (D // N_SHARD), (s + 1) * (D // N_SHARD))
            dya = dya + _dot_nt(da[:, cs], wa_ref[s])
            dyb = dyb + _dot_nt(db[:, cs], wb_ref[s])
        dya_ref[...] = dya.astype(BF16)
        for c in range(tm // BLK):
            rs = slice(c * BLK, (c + 1) * BLK)
            u = uv_ref[rs, :G_W]
            vb = uv_ref[rs, G_W:]
            gu, tu, tv, vhat, rstd, vn, s, grp = _gmlp_chunk(u, vb, gp_ref, ws_ref, bias_ref)
            dyb_c = dyb[rs, :]
            ds = dyb_c * gu
            du = dyb_c * s * _gelu_grad(u, tu)
            dstack = jnp.concatenate([jnp.where(grp == g, ds, 0.0) for g in range(N_GRP)], axis=0).astype(BF16)
            dvn = _dot(wst_ref[...], dstack)
            dws_ref[...] += _dot_nt(dstack, vn)
            dbs_ref[...] += ds
            st_ref[0:1, :] += _colsum(dvn * vhat)
            st_ref[1:2, :] += _colsum(dvn)
            dgv = _ln_bwd(dvn * gp_ref[0:1, :], vhat, rstd)
            dvb = dgv * _gelu_grad(vb, tv)
            dp_ref[rs, :G_W] = du.astype(BF16)
            dp_ref[rs, G_W:2 * G_W] = dvb.astype(BF16)

    pw = 2 * G_W + 2 * D
    return pl.pallas_call(
        body, name="mix_bwd", grid=(seq // tm,),
        in_specs=[_rows(tm, D), _rows(tm, D), _rows(tm, D), _rows(tm, 2 * D), _rows(tm, 2 * G_W), _acc((8, D)), _acc((8, G_W)),
                  _resident((N_GRP * BLK, BLK)), _resident((BLK, N_GRP * BLK)), _acc((BLK, G_W)),
                  _resident((N_SHARD, Q_W, D // N_SHARD)), _resident((N_SHARD, G_W, D // N_SHARD)), _resident((D, D))],
        out_specs=[_rows(tm, D), _rows(tm, D), _rows(tm, D), _rows(tm, Q_W), _rows(tm, pw),
                   _acc((N_GRP * BLK, BLK)), _acc((BLK, G_W)), _acc((8, G_W))],
        out_shape=[jax.ShapeDtypeStruct((seq, D), BF16), jax.ShapeDtypeStruct((seq, D), BF16),
                   jax.ShapeDtypeStruct((seq, D), BF16), jax.ShapeDtypeStruct((seq, Q_W), BF16),
                   jax.ShapeDtypeStruct((seq, pw), BF16), jax.ShapeDtypeStruct((N_GRP * BLK, BLK), F32),
                   jax.ShapeDtypeStruct((BLK, G_W), F32), jax.ShapeDtypeStruct((8, G_W), F32)],
        compiler_params=_cp("arbitrary"),
    )(dr1, a, b, gab, uv, vec, gp, ws_stack, ws_stack_t, bias_full, w_a, w_b, w_o)


def _attn_bwd(q, kv, kvc, sink, dya, ya, lse, scatter=()):
    seq = q.shape[0]
    nb = seq // BLK
    n_ctx = kvc.shape[0]
    ns = len(scatter)

    def body(q_ref, kvp_ref, kvm_ref, kvn_ref, kvc_ref, sink_ref, do_ref, o_ref, lse_ref, *rest):
        dq_ref, dkv_ref, dkvc_ref, dsink_ref = rest[ns:ns + 4]
        comm = _AllToAll(rest[:ns], rest[ns + 4:2 * ns + 4], *rest[2 * ns + 4:]) if ns else None
        n = pl.program_id(0)
        _host_start(n, comm)

        @pl.when(n == 0)
        def _():
            dkv_ref[...] = jnp.zeros_like(dkv_ref)
            dkvc_ref[...] = jnp.zeros_like(dkvc_ref)
            dsink_ref[...] = jnp.zeros_like(dsink_ref)

        do = do_ref[...]
        out = o_ref[...]
        lse_all = lse_ref[...]
        k_refs = (kvc_ref, kvp_ref, kvm_ref, kvn_ref)
        dqs, dks, dvs = [], [], []
        for hk in range(N_KV):
            q4, ks, s = _attn_scores(q_ref, k_refs, hk, n, nb)
            vs = [r[:, KV_W + hk * HEAD:KV_W + (hk + 1) * HEAD] for r in k_refs]
            lse4 = jnp.concatenate([lse_all[:, hk * GROUP + g:hk * GROUP + g + 1] for g in range(GROUP)], axis=0)
            do4 = _stack_heads(do, hk)
            delta = jnp.sum(do4.astype(F32) * _stack_heads(out, hk).astype(F32), axis=-1, keepdims=True)
            p = [jnp.exp((t - lse4).astype(BF16)) for t in s]
            ds = [t * (_dot_nt(do4, v) - delta).astype(BF16) for t, v in zip(p, vs)]
            dq4 = _dot(ds[0], ks[0])
            for t, k in zip(ds[1:], ks[1:]):
                dq4 = dq4 + _dot(t, k)
            dq4 = dq4 * SCALE
            dqs += [dq4[g * BLK:(g + 1) * BLK, :] for g in range(GROUP)]
            dks.append([_dot_tn(t, q4) for t in ds])
            dvs.append([_dot_tn(t, do4) for t in p])
            ps = jnp.exp(_sink_rows(sink_ref, hk) - lse4) * delta
            lane = lax.broadcasted_iota(jnp.int32, (1, LANES), 1)
            for g in range(GROUP):
                part = -jnp.sum(ps[g * BLK:(g + 1) * BLK, :], axis=0, keepdims=True)
                dsink_ref[0:1, :] += jnp.where(lane == hk * GROUP + g, part, 0.0)
        dq_ref[...] = jnp.concatenate(dqs, axis=1)

        def piece(i):
            return jnp.concatenate([dks[0][i], dks[1][i], dvs[0][i], dvs[1][i]], axis=1)

        dkvc_ref[...] += piece(0)
        starts = (jnp.maximum(n - 1, 0), n, jnp.minimum(n + 1, nb - 1))
        for i, st in enumerate(starts):
            r = pl.ds(pl.multiple_of(st * BLK, BLK), BLK)
            dkv_ref[r, :] += piece(i + 1)
        _host_finish(n, nb - 1, comm)

    out = pl.pallas_call(
        body, name="attn_bwd", grid=(nb,),
        in_specs=[_rows(BLK, Q_W)] + _kv_specs(nb) + [_acc((n_ctx, 2 * KV_W)), pl.BlockSpec(memory_space=pltpu.SMEM),
                                                      _rows(BLK, Q_W), _rows(BLK, Q_W), _rows(BLK, LANES)] + _comm_specs(ns),
        out_specs=[_rows(BLK, Q_W), _acc((seq, 2 * KV_W)), _acc((n_ctx, 2 * KV_W)), _acc((8, LANES))] + _comm_specs(ns),
        out_shape=[jax.ShapeDtypeStruct((seq, Q_W), F32), jax.ShapeDtypeStruct((seq, 2 * KV_W), F32),
                   jax.ShapeDtypeStruct((n_ctx, 2 * KV_W), F32), jax.ShapeDtypeStruct((8, LANES), F32)]
        + [jax.ShapeDtypeStruct(v.shape, v.dtype) for v in scatter],
        scratch_shapes=_comm_scratch(ns) if ns else [],
        compiler_params=_cp("arbitrary"),
    )(q, kv, kv, kv, kvc, sink, dya, ya, lse, *scatter)
    return out[:4], out[4:]


def _proj_bwd(dq, dkv, dpb, x, dr1, modx, w_in, cos, sin, tm, scatter=()):
    seq = x.shape[0]
    pw = IN_W - Q_W - 2 * KV_W
    ns = len(scatter)

    def body(dq_ref, dkv_ref, dpb_ref, x_ref, dr1_ref, mod_ref, w_ref, cos_ref, sin_ref, *rest):
        dqkv_ref, gx_ref, st_ref = rest[ns:ns + 3]
        comm = _AllToAll(rest[:ns], rest[ns + 3:2 * ns + 3], *rest[2 * ns + 3:]) if ns else None
        _host_start(pl.program_id(0), comm)

        @pl.when(pl.program_id(0) == 0)
        def _():
            st_ref[...] = jnp.zeros_like(st_ref)

        cos1, sin1 = cos_ref[...], sin_ref[...]
        cos2 = jnp.concatenate([cos1, cos1], axis=1)
        sin2 = jnp.concatenate([sin1, sin1], axis=1)
        for j in range(Q_W // 256):
            cs = slice(256 * j, 256 * (j + 1))
            dqkv_ref[:, cs] = _unrope(dq_ref[:, cs], cos2, sin2).astype(BF16)
        dqkv_ref[:, Q_W:Q_W + KV_W] = _unrope(dkv_ref[:, :KV_W], cos1, sin1).astype(BF16)
        dqkv_ref[:, Q_W + KV_W:] = dkv_ref[:, KV_W:].astype(BF16)
        o = Q_W + 2 * KV_W
        dh = _dot_nt(dqkv_ref[...], w_ref[:, :o]) + _dot_nt(dpb_ref[...], w_ref[:, o:])
        xhat, rstd = _ln(x_ref[...])
        st_ref[0:1, :] += _colsum(dh)
        st_ref[1:2, :] += _colsum(dh * xhat)
        gx_ref[...] = _ln_bwd(dh * (1.0 + mod_ref[1:2, :]), xhat, rstd) + ALPHA * dr1_ref[...]
        _host_finish(pl.program_id(0), seq // tm - 1, comm)

    out = pl.pallas_call(
        body, name="proj_bwd", grid=(seq // tm,),
        in_specs=[_rows(tm, Q_W), _rows(tm, 2 * KV_W), _rows(tm, pw), _rows(tm, D), _rows(tm, D), _acc((8, D)),
                  _resident((D, IN_W)), _rows(tm, LANES), _rows(tm, LANES)] + _comm_specs(ns),
        out_specs=[_rows(tm, Q_W + 2 * KV_W), _rows(tm, D), _acc((8, D))] + _comm_specs(ns),
        out_shape=[jax.ShapeDtypeStruct((seq, Q_W + 2 * KV_W), BF16), jax.ShapeDtypeStruct((seq, D), F32),
                   jax.ShapeDtypeStruct((8, D), F32)] + [jax.ShapeDtypeStruct(v.shape, v.dtype) for v in scatter],
        scratch_shapes=_comm_scratch(ns) if ns else [],
        compiler_params=_cp("arbitrary"),
    )(dq, dkv, dpb, x, dr1, modx, w_in, cos, sin, *scatter)
    return out[:3], out[3:]


def _ctx_bwd(dkvc, ctx, hc, w_kv):
    n_ctx = ctx.shape[0]

    def body(dkvc_ref, ctx_ref, hc_ref, w_ref, dw_ref, st_ref):
        d = dkvc_ref[...].astype(BF16)
        dw_ref[...] = _dot_tn(hc_ref[...], d)
        dhc = _dot_nt(d, w_ref[...])
        xhat, _ = _ln(ctx_ref[...])
        st_ref[...] = jnp.zeros_like(st_ref)
        st_ref[0:1, :] = _colsum(dhc)
        st_ref[1:2, :] = _colsum(dhc * xhat)

    return pl.pallas_call(
        body, name="ctx_bwd", grid=(1,),
        in_specs=[_acc((n_ctx, 2 * KV_W)), _acc((n_ctx, D)), _acc((n_ctx, D)), _acc((D, 2 * KV_W))],
        out_specs=[_acc((D, 2 * KV_W)), _acc((8, D))],
        out_shape=[jax.ShapeDtypeStruct((D, 2 * KV_W), F32), jax.ShapeDtypeStruct((8, D), F32)],
        compiler_params=_cp("arbitrary"),
    )(dkvc, ctx, hc, w_kv)


def _tn_matmul(a, b, tn, name, out_dtype, shard_major=False, init=None, tk=512):
    t, ka = a.shape
    n = b.shape[1]
    tk = min(tk, t)
    nk = t // tk
    has_init = init is not None

    def body(*refs):
        if has_init:
            a_ref, b_ref, i_ref, o_ref, acc_ref = refs
        else:
            a_ref, b_ref, o_ref, acc_ref = refs
        k = pl.program_id(1)

        @pl.when(k == 0)
        def _():
            acc_ref[...] = i_ref[...] if has_init else jnp.zeros_like(acc_ref)

        acc_ref[...] += _dot_tn(a_ref[...], b_ref[...])

        @pl.when(k == nk - 1)
        def _():
            o_ref[...] = acc_ref[...].astype(out_dtype)

    in_specs = [pl.BlockSpec((tk, ka), lambda j, k: (k, 0)), pl.BlockSpec((tk, tn), lambda j, k: (k, j))]
    args = [a, b]
    if has_init:
        in_specs.append(pl.BlockSpec((ka, tn), lambda j, k: (0, j)))
        args.append(init)
    if shard_major:
        out_spec = pl.BlockSpec((None, ka, tn), lambda j, k: (j, 0, 0))
        out_shape = jax.ShapeDtypeStruct((n // tn, ka, tn), out_dtype)
    else:
        out_spec = pl.BlockSpec((ka, tn), lambda j, k: (0, j))
        out_shape = jax.ShapeDtypeStruct((ka, n), out_dtype)
    return pl.pallas_call(
        body, name=name, grid=(n // tn, nk), in_specs=in_specs, out_specs=out_spec, out_shape=out_shape,
        scratch_shapes=[pltpu.VMEM((ka, tn), F32)],
        compiler_params=_cp("arbitrary", "arbitrary"),
    )(*args)


ADA_TILE = 512


def _ada_fwd(sc_all, w_ada):
    cs = w_ada.shape[1]

    def body(s_ref, w_ref, o_ref):
        o_ref[...] = _dot(s_ref[...].astype(BF16), w_ref[...].astype(BF16))

    return pl.pallas_call(
        body, name="ada_fwd", grid=(cs // ADA_TILE,),
        in_specs=[_acc((16, D)), pl.BlockSpec((D, ADA_TILE), lambda j: (0, j))],
        out_specs=pl.BlockSpec((16, ADA_TILE), lambda j: (0, j)),
        out_shape=jax.ShapeDtypeStruct((16, cs), F32),
        compiler_params=_cp("arbitrary"),
    )(sc_all, w_ada)


def _ada_bwd(sc_all_t, dm_all, dmc, w_ada):
    cs = w_ada.shape[1]

    def body(st_ref, dm_ref, dmc_ref, w_ref, gw_ref, part_ref):
        @pl.when(pl.program_id(0) == 0)
        def _():
            part_ref[...] = jnp.zeros_like(part_ref)

        gw_ref[...] = _dot(st_ref[...].astype(BF16), dm_ref[...].astype(BF16))
        part_ref[...] += _dot_nt(dmc_ref[...].astype(BF16), w_ref[...].astype(BF16))

    return pl.pallas_call(
        body, name="ada_bwd", grid=(cs // ADA_TILE,),
        in_specs=[_acc((D, 16)), pl.BlockSpec((16, ADA_TILE), lambda j: (0, j)), pl.BlockSpec((8, ADA_TILE), lambda j: (0, j)),
                  pl.BlockSpec((D, ADA_TILE), lambda j: (0, j))],
        out_specs=[pl.BlockSpec((D, ADA_TILE), lambda j: (0, j)), _acc((8, D))],
        out_shape=[jax.ShapeDtypeStruct((D, cs), F32), jax.ShapeDtypeStruct((8, D), F32)],
        compiler_params=_cp("arbitrary"),
    )(sc_all_t, dm_all, dmc, w_ada)


def _sum8(x, name, tr=256):
    _, r, c = x.shape
    tr = min(tr, r)
    while r % tr:
        tr -= 16

    def body(x_ref, o_ref):
        acc = x_ref[0].astype(F32)
        for i in range(1, N_DEV):
            acc = acc + x_ref[i].astype(F32)
        o_ref[...] = acc

    return pl.pallas_call(
        body, name=name, grid=(r // tr,),
        in_specs=[pl.BlockSpec((N_DEV, tr, c), lambda i: (0, i, 0))],
        out_specs=pl.BlockSpec((tr, c), lambda i: (i, 0)),
        out_shape=jax.ShapeDtypeStruct((r, c), F32),
        compiler_params=_cp("arbitrary"),
    )(x)


def _sum_blocks(recv, src, me, name, tr=256):
    _, r, c = recv.shape
    tr = min(tr, r)
    while r % tr:
        tr -= 16

    def body(me_ref, recv_ref, own_ref, o_ref):
        acc = own_ref[...].astype(F32)
        for k in range(1, N_DEV):
            acc = acc + recv_ref[me_ref[0] ^ k].astype(F32)
        o_ref[...] = acc

    return pl.pallas_call(
        body, name=name,
        grid_spec=pltpu.PrefetchScalarGridSpec(
            num_scalar_prefetch=1, grid=(r // tr,),
            in_specs=[pl.BlockSpec((N_DEV, tr, c), lambda i, me_ref: (0, i, 0)),
                      pl.BlockSpec((None, tr, c), lambda i, me_ref: (me_ref[0], i, 0))],
            out_specs=pl.BlockSpec((tr, c), lambda i, me_ref: (i, 0))),
        out_shape=jax.ShapeDtypeStruct((r, c), F32),
        compiler_params=_cp("arbitrary"),
    )(me, recv, src)


def _sum8_many(xs, name):
    n = len(xs)

    def body(*refs):
        for x_ref, o_ref in zip(refs[:n], refs[n:]):
            acc = x_ref[0]
            for i in range(1, N_DEV):
                acc = acc + x_ref[i]
            o_ref[...] = acc

    vmem = pl.BlockSpec(memory_space=pltpu.VMEM)
    return pl.pallas_call(
        body, name=name, in_specs=[vmem] * n, out_specs=[vmem] * n,
        out_shape=[jax.ShapeDtypeStruct(v.shape[1:], v.dtype) for v in xs],
        compiler_params=pltpu.CompilerParams(vmem_limit_bytes=VMEM_LIMIT),
    )(*xs)


def _adam_update(w, g, m, v):
    nm = ADAM_B1 * m + (1.0 - ADAM_B1) * g
    nv = ADAM_B2 * v + (1.0 - ADAM_B2) * (g * g)
    m_hat = nm / (1.0 - ADAM_B1 ** ADAM_STEP)
    v_hat = nv / (1.0 - ADAM_B2 ** ADAM_STEP)
    return -ADAM_LR * (m_hat / (jnp.sqrt(v_hat) + ADAM_EPS) + ADAM_WD * w), nm, nv


ROW_LOSS, ROW_LN2_G, ROW_LN2_B, ROW_LN1_G, ROW_LN1_B = 0, 1, 2, 10, 11
ROWS_DMOD_X = (16, 17, 12, 9, 8, 3)
ROWS_DMOD_C = (24, 25)
SMALL = ("c_ctx", "b_ada", "attn_sink", "gmlp_ln_g", "gmlp_ln_b", "w_spatial", "b_spatial", "ln1_g", "ln1_b", "ln2_g", "ln2_b")


def _adamw_small(sums, dsc, w, m, v):
    n = len(SMALL)

    def body(*refs):
        st_ref, gm_ref, sk_ref, ws_ref, bs_ref, dsc_ref = refs[:6]
        w_refs = dict(zip(SMALL, refs[6:6 + n]))
        m_refs = dict(zip(SMALL, refs[6 + n:6 + 2 * n]))
        v_refs = dict(zip(SMALL, refs[6 + 2 * n:6 + 3 * n]))
        outs = refs[6 + 3 * n:]
        c = w_refs["c_ctx"][...]
        sg = _sig(c)
        dmod = [st_ref[r:r + 1, :] for r in ROWS_DMOD_X]
        dmod[0] = dmod[0] + st_ref[ROWS_DMOD_C[0]:ROWS_DMOD_C[0] + 1, :]
        dmod[1] = dmod[1] + st_ref[ROWS_DMOD_C[1]:ROWS_DMOD_C[1] + 1, :]
        grads = dict(
            c_ctx=dsc_ref[0:1, :] * (sg * (1.0 + c * (1.0 - sg))),
            b_ada=jnp.concatenate(dmod, axis=1),
            attn_sink=sk_ref[0:1, 0:N_KV * GROUP],
            gmlp_ln_g=gm_ref[0:1, :], gmlp_ln_b=gm_ref[1:2, :],
            w_spatial=ws_ref[...], b_spatial=bs_ref[...],
            ln1_g=st_ref[ROW_LN1_G:ROW_LN1_G + 1, :], ln1_b=st_ref[ROW_LN1_B:ROW_LN1_B + 1, :],
            ln2_g=st_ref[ROW_LN2_G:ROW_LN2_G + 1, :], ln2_b=st_ref[ROW_LN2_B:ROW_LN2_B + 1, :])
        for i, name in enumerate(SMALL):
            g = grads[name]
            d, nm, nv = _adam_update(w_refs[name][...], g, m_refs[name][...], v_refs[name][...])
            outs[i][...] = g
            outs[n + i][...] = d
            outs[2 * n + i][...] = nm
            outs[3 * n + i][...] = nv

    vmem = pl.BlockSpec(memory_space=pltpu.VMEM)
    args = list(sums) + [dsc] + [w[k] for k in SMALL] + [m[k] for k in SMALL] + [v[k] for k in SMALL]
    shapes = [jax.ShapeDtypeStruct(w[k].shape, F32) for k in SMALL]
    out = pl.pallas_call(
        body, name="adamw_small", in_specs=[vmem] * len(args), out_specs=[vmem] * (4 * n), out_shape=shapes * 4,
        compiler_params=pltpu.CompilerParams(vmem_limit_bytes=VMEM_LIMIT),
    )(*args)
    return [dict(zip(SMALL, out[i * n:(i + 1) * n])) for i in range(4)]


def _adamw(w, g, m, v, name):
    r, c = w.shape
    tr = r if r * c <= 256 * 1024 else min(256, r)
    while r % tr:
        tr -= 8

    def body(w_ref, g_ref, m_ref, v_ref, d_ref, nm_ref, nv_ref):
        d_ref[...], nm_ref[...], nv_ref[...] = _adam_update(w_ref[...], g_ref[...], m_ref[...], v_ref[...])

    spec = pl.BlockSpec((tr, c), lambda i: (i, 0))
    shp = jax.ShapeDtypeStruct((r, c), F32)
    return pl.pallas_call(
        body, name=name, grid=(r // tr,), in_specs=[spec] * 4, out_specs=[spec] * 3, out_shape=[shp] * 3,
        compiler_params=_cp("arbitrary"),
    )(w, g, m, v)


def _my_pos():
    return lax.axis_index("x"), lax.axis_index("y"), lax.axis_index("c")


N_COPY = 7


class _Gather:
    def __init__(self, x_refs, out_refs, send_sems, recv_sems):
        self.x_refs, self.out_refs = x_refs, out_refs
        self.send_sems, self.recv_sems = send_sems, recv_sems
        x, y, c = _my_pos()
        self.c = c
        self.me, self.sibling = (x, y, c), (x, y, 1 - c)
        self.chips = [(1 - x, y), (x, 1 - y), (1 - x, 1 - y)]

    def _copy(self, a, k, block, to, from_input=False):
        px, py, pc = block
        rows = self.out_refs[a].at[4 * px + 2 * py + pc]
        return pltpu.make_async_remote_copy(
            src_ref=self.x_refs[a] if from_input else rows, dst_ref=rows,
            send_sem=self.send_sems.at[a * N_COPY + k], recv_sem=self.recv_sems.at[a * N_COPY + k],
            device_id=to, device_id_type=MESH)

    def start(self):
        n = len(self.x_refs)
        for a in range(n):
            self._copy(a, 0, self.me, self.sibling, from_input=True).start()
        for j, chip in enumerate(self.chips):
            for a in range(n):
                self._copy(a, 1 + j, self.me, (*chip, self.c), from_input=True).start()

    def finish(self):
        n = len(self.x_refs)
        c = self.c
        for j, chip in enumerate(self.chips):
            for a in range(n):
                self._copy(a, 1 + j, (*chip, c), self.me).wait_recv()
                self._copy(a, 4 + j, (*chip, c), self.sibling).start()
        for a in range(n):
            self._copy(a, 0, self.sibling, self.me).wait_recv()
        for j, chip in enumerate(self.chips):
            for a in range(n):
                self._copy(a, 4 + j, (*chip, 1 - c), self.me).wait_recv()
        for a in range(n):
            self._copy(a, 0, self.me, self.sibling, from_input=True).wait_send()
            for j, chip in enumerate(self.chips):
                self._copy(a, 1 + j, self.me, (*chip, c), from_input=True).wait_send()
                self._copy(a, 4 + j, (*chip, c), self.sibling).wait_send()


def _comm_scratch(n):
    return [pltpu.SemaphoreType.DMA((n * N_COPY,)), pltpu.SemaphoreType.DMA((n * N_COPY,))]


def _comm_specs(n):
    return [pl.BlockSpec(memory_space=pl.ANY)] * n


def _gathered_shapes(xs):
    return [jax.ShapeDtypeStruct((N_DEV,) + v.shape, v.dtype) for v in xs]


def _with_own(gathered, xs, me):
    return [lax.dynamic_update_index_in_dim(g, v, me, 0) for g, v in zip(gathered, xs)]


def _all_gather(xs, me, name):
    n = len(xs)

    def body(*refs):
        g = _Gather(refs[:n], refs[n:2 * n], *refs[2 * n:])
        g.start()
        g.finish()

    out = pl.pallas_call(
        body, name=name, out_shape=_gathered_shapes(xs), in_specs=_comm_specs(n), out_specs=_comm_specs(n),
        scratch_shapes=_comm_scratch(n),
    )(*xs)
    return _with_own(out, xs, me)


class _AllToAll:
    def __init__(self, x_refs, out_refs, send_sems, recv_sems):
        self.x_refs, self.out_refs = x_refs, out_refs
        self.send_sems, self.recv_sems = send_sems, recv_sems
        self.pos = _my_pos()
        x, y, c = self.pos
        self.me = 4 * x + 2 * y + c

    def _peer(self, k):
        x, y, c = self.pos
        return (x ^ ((k >> 2) & 1), y ^ ((k >> 1) & 1), c ^ (k & 1))

    def _copy(self, a, k):
        p = self._peer(k)
        return pltpu.make_async_remote_copy(
            src_ref=self.x_refs[a].at[4 * p[0] + 2 * p[1] + p[2]], dst_ref=self.out_refs[a].at[self.me],
            send_sem=self.send_sems.at[a * N_COPY + k - 1], recv_sem=self.recv_sems.at[a * N_COPY + k - 1],
            device_id=p, device_id_type=MESH)

    def start(self):
        for k in range(1, N_DEV):
            for a in range(len(self.x_refs)):
                self._copy(a, k).start()

    def finish(self):
        for a in range(len(self.x_refs)):
            for k in range(1, N_DEV):
                self._copy(a, k).wait_recv()
            for k in range(1, N_DEV):
                self._copy(a, k).wait_send()


def _all_to_all(blocks, name):
    n = len(blocks)

    def body(*refs):
        t = _AllToAll(refs[:n], refs[n:2 * n], *refs[2 * n:])
        t.start()
        t.finish()

    return pl.pallas_call(
        body, name=name, out_shape=[jax.ShapeDtypeStruct(v.shape, v.dtype) for v in blocks],
        in_specs=_comm_specs(n), out_specs=_comm_specs(n), scratch_shapes=_comm_scratch(n),
    )(*blocks)


def _sibling_exchange(xs, name):
    n = len(xs)

    def body(*refs):
        x_refs, out_refs = refs[:n], refs[n:2 * n]
        send_sems, recv_sems = refs[2 * n:]
        x, y, c = _my_pos()

        def push(a):
            return pltpu.make_async_remote_copy(
                src_ref=x_refs[a], dst_ref=out_refs[a], send_sem=send_sems.at[a], recv_sem=recv_sems.at[a],
                device_id=(x, y, 1 - c), device_id_type=MESH)

        for a in range(n):
            push(a).start()
        for a in range(n):
            push(a).wait_recv()
            push(a).wait_send()

    return pl.pallas_call(
        body, name=name, out_shape=[jax.ShapeDtypeStruct(v.shape, v.dtype) for v in xs],
        in_specs=_comm_specs(n), out_specs=_comm_specs(n),
        scratch_shapes=[pltpu.SemaphoreType.DMA((n,)), pltpu.SemaphoreType.DMA((n,))],
    )(*xs)


def _scatter_and_gather(scatter, gather, name):
    ns, ng = len(scatter), len(gather)

    def body(*refs):
        s_in, g_in = refs[:ns], refs[ns:ns + ng]
        s_out, g_out = refs[ns + ng:2 * ns + ng], refs[2 * ns + ng:2 * (ns + ng)]
        s_send, s_recv, g_send, g_recv = refs[2 * (ns + ng):]
        g = _Gather(g_in, g_out, g_send, g_recv)
        t = _AllToAll(s_in, s_out, s_send, s_recv)
        g.start()
        t.start()
        g.finish()
        t.finish()

    out = pl.pallas_call(
        body, name=name,
        out_shape=[jax.ShapeDtypeStruct(v.shape, v.dtype) for v in scatter] + _gathered_shapes(gather),
        in_specs=_comm_specs(ns + ng), out_specs=_comm_specs(ns + ng),
        scratch_shapes=_comm_scratch(ns) + _comm_scratch(ng),
    )(*scatter, *gather)
    return out[:ns], out[ns:]


def _row_tile(seq, want):
    return min(want, seq)


def _local_step(x, ctx, tgt, mod_x, mod_c, wb, sink, gmlp_g, gmlp_b, w_s, b_s, ln1_g, ln1_b, ln2_g, ln2_b,
                later=None, me=None):
    seq = x.shape[0]
    on_mesh = me is not None
    modx1 = jnp.concatenate([mod_x[0:2], jnp.zeros((6, D), F32)], axis=0)
    modc = jnp.concatenate([mod_c[0:2], jnp.zeros((6, D), F32)], axis=0)
    vec = jnp.concatenate([mod_x[2:3], ln1_g, ln1_b, mod_x[3:6], ln2_g, ln2_b], axis=0)
    gp = jnp.concatenate([gmlp_g, gmlp_b, jnp.zeros((6, G_W), F32)], axis=0)
    ws_stack = w_s.reshape(N_GRP * BLK, BLK).astype(BF16)
    ws_stack_t = jnp.transpose(w_s, (2, 0, 1)).reshape(BLK, N_GRP * BLK).astype(BF16)
    bias_full = jnp.repeat(b_s.T, GRP_D, axis=1)
    cos, sin = _rope_tables(seq)
    w_in = wb["w_in"]
    w_kv = w_in[:, Q_W:Q_W + 2 * KV_W]
    tm_big = _row_tile(seq, 512)
    tm_ffn = _row_tile(seq, 256)

    hc, kvc, vac = _ctx_fwd(ctx, modc, w_kv)
    behind_proj = ("w_a", "w_b", "w_o") if on_mesh else ()
    behind_attn = ("w_fi", "w_fo") if on_mesh else ()
    (h, q, kv, va, uv, gab), got_proj = _proj_fwd(x, modx1, w_in, cos, sin, tm_big, gather=[later[n] for n in behind_proj])
    (ya, lse), got_attn = _attn_fwd(q, kv, va, kvc, vac, sink, gather=[later[n] for n in behind_attn])
    if on_mesh:
        wb = dict(wb)
        names = behind_proj + behind_attn
        for n, g in zip(names, _with_own(list(got_proj) + list(got_attn), [later[n] for n in names], me)):
            wb[n] = g.reshape(-1, g.shape[2]) if n in ROW_SHARDED else g.reshape(N_SHARD, 2 * g.shape[1], g.shape[2])
    a, b, mix, merged, yb = _mix_fwd(uv, gab, ya, gp, ws_stack, bias_full, wb["w_a"], wb["w_b"], wb["w_o"], tm_big)
    gu, act, h2, dr2, st5 = _ffn_fwd(x, mix, tgt, vec, wb["w_fi"], wb["w_fo"], tm_ffn)

    dff, df, dr1, st5b = _ffn_bwd(dr2, gu, x, mix, vec, wb["w_fi"], wb["w_fo"], tm_ffn)
    g_w_fo = _tn_matmul(act, df, 512, "tn_w_ffn_out", BF16)
    g_w_fi = _tn_matmul(h2, dff, FH_SHARD, "tn_w_ffn_in", BF16, shard_major=True)
    dmix, da, db, dya, dpb, dws, dbs_full, st4 = _mix_bwd(
        dr1, a, b, gab, uv, vec, gp, ws_stack, ws_stack_t, bias_full, wb["w_a"], wb["w_b"], wb["w_o"], _row_tile(seq, 256))
    g_w_o = _tn_matmul(merged, dmix, 1024, "tn_w_out", BF16)
    g_w_a = _tn_matmul(ya, da, D // N_SHARD, "tn_w_branch_a", BF16, shard_major=True, tk=2048)
    g_w_b = _tn_matmul(yb, db, D // N_SHARD, "tn_w_branch_b", BF16, shard_major=True, tk=2048)
    blocks = dict(w_fi=_eighths(g_w_fi), w_fo=_eighths(g_w_fo), w_o=_eighths(g_w_o), w_a=_eighths(g_w_a), w_b=_eighths(g_w_b))
    early = tuple(blocks) if on_mesh else ()
    (dq, dkv, dkvc, dsink), recv_early = _attn_bwd(q, kv, kvc, sink, dya, ya, lse, scatter=[blocks[n] for n in early])
    g_wkv_ctx, st0 = _ctx_bwd(dkvc, ctx, hc, w_kv)
    (dqkv, grad_x, st1), _ = _proj_bwd(dq, dkv, dpb, x, dr1, modx1, w_in, cos, sin, tm_big)
    o = Q_W + 2 * KV_W
    init = jnp.pad(g_wkv_ctx, ((0, 0), (Q_W, 0)))
    g_w_in = jnp.concatenate([_tn_matmul(h, dqkv, o, "tn_w_in_qkv", F32, init=init),
                              _tn_matmul(h, dpb, 1536, "tn_w_in_rest", F32)], axis=1)
    g_w_in = g_w_in.reshape(D, N_SHARD, IN_W // N_SHARD).transpose(1, 0, 2).astype(BF16)

    dbs = jnp.sum(dbs_full.reshape(BLK, N_GRP, GRP_D), axis=2).T
    parts = [jnp.concatenate([st5, st5b, st1, st0], axis=0), st4, dsink, dws, dbs]
    blocks["w_in"] = _eighths(g_w_in)
    return grad_x, parts, blocks, dict(zip(early, recv_early))


BIG = ("w_in", "w_a", "w_b", "w_o", "w_fi", "w_fo")
ROW_SHARDED = ("w_o", "w_fo")


def _half_of_shard(shard, c):
    r = shard.shape[0]
    return lax.dynamic_slice_in_dim(shard, c * (r // 2), r // 2, axis=0)


def _eighths(v):
    rows = v.shape[-2] * (v.shape[0] if v.ndim == 3 else 1)
    return v.reshape(N_DEV, rows // N_DEV, v.shape[-1])


def kernel(x, c, ctx, c_ctx, w_ada, b_ada, w_in, attn_sink, gmlp_ln_g, gmlp_ln_b, w_spatial, b_spatial, w_branch_a, w_branch_b, w_out, ln1_g, ln1_b, w_ffn_in, w_ffn_out, ln2_g, ln2_b, loss_target, m_c_ctx, m_w_ada, m_b_ada, m_w_in, m_attn_sink, m_gmlp_ln_g, m_gmlp_ln_b, m_w_spatial, m_b_spatial, m_w_branch_a, m_w_branch_b, m_w_out, m_ln1_g, m_ln1_b, m_w_ffn_in, m_w_ffn_out, m_ln2_g, m_ln2_b, v_c_ctx, v_w_ada, v_b_ada, v_w_in, v_attn_sink, v_gmlp_ln_g, v_gmlp_ln_b, v_w_spatial, v_b_spatial, v_w_branch_a, v_w_branch_b, v_w_out, v_ln1_g, v_ln1_b, v_w_ffn_in, v_w_ffn_out, v_ln2_g, v_ln2_b):
    mx, my, mc = _my_pos()
    me = 4 * mx + 2 * my + mc
    chip = 2 * mx + my
    shards = dict(w_in=w_in[0], w_a=w_branch_a[0], w_b=w_branch_b[0], w_o=w_out[0], w_fi=w_ffn_in[0], w_fo=w_ffn_out[0])

    halves = {n: _half_of_shard(shards[n], mc).astype(BF16) for n in BIG}
    c_rows = jnp.concatenate([c, jnp.zeros((7, D), F32)], axis=0)
    g_in, c_g = _all_gather([halves["w_in"], c_rows], me, "gather_w_in")
    r, cdim = shards["w_in"].shape
    wb = dict(w_in=g_in.reshape(N_SHARD, r, cdim).transpose(1, 0, 2).reshape(r, N_SHARD * cdim))

    c_all = c_g[:, 0, :]
    cc = jnp.concatenate([c_all, c_ctx[None, :], jnp.zeros((7, D), F32)], axis=0)
    sig_cc = jax.nn.sigmoid(cc)
    sc_all = cc * sig_cc
    mod_shard = _ada_fwd(sc_all, w_ada[0])
    mod_g = _all_gather([mod_shard], me, "gather_mod")[0]
    mod_all = jnp.concatenate([mod_g[2 * s] for s in range(4)], axis=1) + b_ada
    mod_x = lax.dynamic_slice_in_dim(mod_all, me, 1, axis=0).reshape(6, D)
    mod_c = mod_all[8].reshape(6, D)[0:2]

    grad_x, parts, blocks, recv = _local_step(
        x[0], ctx[0], loss_target[0], mod_x, mod_c, wb, attn_sink, gmlp_ln_g, gmlp_ln_b, w_spatial[0], b_spatial[0],
        ln1_g, ln1_b, ln2_g, ln2_b, later=halves, me=me)

    (recv["w_in"],), gathered = _scatter_and_gather([blocks["w_in"]], parts, "scatter_w_in_gather_small")
    gathered = _with_own(gathered, parts, me)

    me_arr = jnp.reshape(me, (1,)).astype(jnp.int32)
    summed = {n: _sum_blocks(recv[n], blocks[n], me_arr, "sum_grads_" + n) for n in BIG}
    theirs = _sibling_exchange([summed[n] for n in BIG], "exchange_grads")
    g_shard = {}
    for n, other in zip(BIG, theirs):
        lo = jnp.where(mc == 0, summed[n], other)
        hi = jnp.where(mc == 0, other, summed[n])
        g_shard[n] = jnp.concatenate([lo, hi], axis=0)

    sums = _sum8_many(gathered, "sum_small")
    stats = sums[0]
    loss = 0.5 * jnp.sum(stats[ROW_LOSS]) / D
    dmod_x_all = jnp.concatenate([gathered[0][:, r_, :] for r_ in ROWS_DMOD_X], axis=1)
    dmod_c_full = jnp.concatenate([stats[r_] for r_ in ROWS_DMOD_C] + [jnp.zeros((4 * D,), F32)])
    dm_rows = jnp.concatenate([dmod_x_all, dmod_c_full[None, :], jnp.zeros((7, 6 * D), F32)], axis=0)
    cs = w_ada.shape[2]
    dm_shard = lax.dynamic_slice_in_dim(dm_rows, chip * cs, cs, axis=1)
    dmc_shard = jnp.concatenate([dm_shard[8:9], jnp.zeros((7, cs), F32)], axis=0)
    g_w_ada, part = _ada_bwd(sc_all.T, dm_shard, dmc_shard, w_ada[0])
    part_all = _all_gather([part * (mc == 0).astype(F32)], me, "gather_c_ctx")[0]
    dsc = _sum8(part_all, "sum_c_ctx")

    grads = dict(w_ada=g_w_ada[None], w_in=g_shard["w_in"][None], w_branch_a=g_shard["w_a"][None],
                 w_branch_b=g_shard["w_b"][None], w_out=g_shard["w_o"][None], w_ffn_in=g_shard["w_fi"][None],
                 w_ffn_out=g_shard["w_fo"][None])
    weights = dict(c_ctx=c_ctx, w_ada=w_ada, b_ada=b_ada, w_in=w_in, attn_sink=attn_sink, gmlp_ln_g=gmlp_ln_g,
                   gmlp_ln_b=gmlp_ln_b, w_spatial=w_spatial, b_spatial=b_spatial, w_branch_a=w_branch_a,
                   w_branch_b=w_branch_b, w_out=w_out, ln1_g=ln1_g, ln1_b=ln1_b, w_ffn_in=w_ffn_in, w_ffn_out=w_ffn_out,
                   ln2_g=ln2_g, ln2_b=ln2_b)
    ms = dict(c_ctx=m_c_ctx, w_ada=m_w_ada, b_ada=m_b_ada, w_in=m_w_in, attn_sink=m_attn_sink, gmlp_ln_g=m_gmlp_ln_g,
              gmlp_ln_b=m_gmlp_ln_b, w_spatial=m_w_spatial, b_spatial=m_b_spatial, w_branch_a=m_w_branch_a,
              w_branch_b=m_w_branch_b, w_out=m_w_out, ln1_g=m_ln1_g, ln1_b=m_ln1_b, w_ffn_in=m_w_ffn_in,
              w_ffn_out=m_w_ffn_out, ln2_g=m_ln2_g, ln2_b=m_ln2_b)
    vs = dict(c_ctx=v_c_ctx, w_ada=v_w_ada, b_ada=v_b_ada, w_in=v_w_in, attn_sink=v_attn_sink, gmlp_ln_g=v_gmlp_ln_g,
              gmlp_ln_b=v_gmlp_ln_b, w_spatial=v_w_spatial, b_spatial=v_b_spatial, w_branch_a=v_w_branch_a,
              w_branch_b=v_w_branch_b, w_out=v_w_out, ln1_g=v_ln1_g, ln1_b=v_ln1_b, w_ffn_in=v_w_ffn_in,
              w_ffn_out=v_w_ffn_out, ln2_g=v_ln2_g, ln2_b=v_ln2_b)
    order = list(weights)
    large = ("w_ada", "w_in", "w_branch_a", "w_branch_b", "w_out", "w_ffn_in", "w_ffn_out")
    delta, new_m, new_v = {}, {}, {}
    for n in large:
        d_, m_, v_ = _adamw(weights[n][0], grads[n][0], ms[n][0], vs[n][0], "adamw_" + n)
        delta[n], new_m[n], new_v[n] = d_[None], m_[None], v_[None]

    def view(a):
        return a.reshape(-1, a.shape[-1]) if a.ndim != 1 else a.reshape(1, -1)

    small = _adamw_small(sums, dsc, *[{n: view(d[n]) for n in SMALL} for d in (weights, ms, vs)])
    for out, src in zip((grads, delta, new_m, new_v), small):
        for n in SMALL:
            out[n] = src[n].reshape(weights[n].shape)

    return (loss, grad_x[None], *[grads[n] for n in order], *[delta[n] for n in order],
            *[new_m[n] for n in order], *[new_v[n] for n in order])
```

```python
import functools
import math

import jax
import jax.numpy as jnp
from jax import lax
from jax.experimental import pallas as pl
from jax.experimental.pallas import tpu as pltpu

F32 = jnp.float32
BF16 = jnp.bfloat16

D = 1024
HEAD = 64
N_KV = 2
GROUP = 4
Q_W = 512
KV_W = 128
G_W = 512
BLK = 128
N_GRP = 8
GRP_D = 64
FH = 2816
IN_W = 3840
GRID_W = 64
ROPE_BASE = 10000.0
LN_EPS = 1e-5
NEG = -1e30
ALPHA = (2 * 1) ** 0.25
SCALE = HEAD ** -0.5
GELU_K = math.sqrt(2.0 / math.pi)
GELU_A = 0.044715
ADAM_LR = 0.001
ADAM_B1 = 0.9
ADAM_B2 = 0.999
ADAM_EPS = 1e-08
ADAM_WD = 0.01
ADAM_STEP = 10
N_DEV = 8
N_SHARD = 4
FH_SHARD = FH // 2
LANES = 128
VMEM_LIMIT = 56 * 1024 * 1024
MESH = pl.DeviceIdType.MESH


def _cp(*sem):
    return pltpu.CompilerParams(dimension_semantics=sem, vmem_limit_bytes=VMEM_LIMIT)


def _resident(shape):
    return pl.BlockSpec(shape, lambda *_: (0,) * len(shape), pipeline_mode=pl.Buffered(1))


def _rows(tm, width):
    return pl.BlockSpec((tm, width), lambda i: (i, 0))


def _acc(shape):
    return pl.BlockSpec(shape, lambda *_: (0,) * len(shape))


def _dot(a, b):
    return jnp.dot(a, b, preferred_element_type=F32)


def _dot_nt(a, b):
    return lax.dot_general(a, b, (((1,), (1,)), ((), ())), preferred_element_type=F32)


def _dot_tn(a, b):
    return lax.dot_general(a, b, (((0,), (0,)), ((), ())), preferred_element_type=F32)


def _ln(x):
    mu = jnp.mean(x, axis=-1, keepdims=True)
    xc = x - mu
    var = jnp.mean(xc * xc, axis=-1, keepdims=True)
    rstd = lax.rsqrt(var + LN_EPS)
    return xc * rstd, rstd


def _ln_bwd(dxhat, xhat, rstd):
    return (dxhat - jnp.mean(dxhat, axis=-1, keepdims=True)
            - xhat * jnp.mean(dxhat * xhat, axis=-1, keepdims=True)) * rstd


def _sig(x):
    return 1.0 / (1.0 + jnp.exp(-x))


def _gelu(x):
    t = jnp.tanh(GELU_K * (x + GELU_A * x * x * x))
    return 0.5 * x * (1.0 + t), t


def _gelu_grad(x, t):
    return 0.5 * (1.0 + t) + 0.5 * x * (1.0 - t * t) * GELU_K * (1.0 + 3.0 * GELU_A * x * x)


def _colsum(v):
    return jnp.sum(v, axis=0, keepdims=True)


def _partner(x):
    w = x.shape[1]
    lane = lax.broadcasted_iota(jnp.int32, x.shape, 1)
    return jnp.where((lane & 31) < 16, pltpu.roll(x, w - 16, 1), pltpu.roll(x, 16, 1))


def _rope(x, cos, sin):
    return x * cos + _partner(x) * sin


def _unrope(g, cos, sin):
    return g * cos + _partner(g * sin)


def _rope_tables(seq):
    inv = ROPE_BASE ** (-jnp.arange(HEAD // 4, dtype=F32) / (HEAD // 4))
    pos = jnp.arange(seq, dtype=jnp.int32)
    ar = (pos // GRID_W).astype(F32)[:, None] * inv
    ac = (pos % GRID_W).astype(F32)[:, None] * inv
    cos = jnp.concatenate([jnp.cos(ar), jnp.cos(ar), jnp.cos(ac), jnp.cos(ac)], axis=-1)
    sin = jnp.concatenate([-jnp.sin(ar), jnp.sin(ar), -jnp.sin(ac), jnp.sin(ac)], axis=-1)
    return jnp.tile(cos, (1, LANES // HEAD)), jnp.tile(sin, (1, LANES // HEAD))


def _ctx_fwd(ctx, modc, w_kv):
    n_ctx = ctx.shape[0]

    def body(ctx_ref, mod_ref, w_ref, hc_ref, kvc_ref, vac_ref):
        xhat, _ = _ln(ctx_ref[...])
        hc = (xhat * (1.0 + mod_ref[1:2, :]) + mod_ref[0:1, :]).astype(BF16)
        hc_ref[...] = hc
        kvc = _dot(hc, w_ref[...]).astype(BF16)
        kvc_ref[...] = kvc
        vac_ref[...] = _with_ones(kvc[:, KV_W:])

    return pl.pallas_call(
        body, name="ctx_fwd", grid=(1,),
        in_specs=[_acc((n_ctx, D)), _acc((8, D)), _acc((D, 2 * KV_W))],
        out_specs=[_acc((n_ctx, D)), _acc((n_ctx, 2 * KV_W)), _acc((n_ctx, 2 * LANES))],
        out_shape=[jax.ShapeDtypeStruct((n_ctx, D), BF16), jax.ShapeDtypeStruct((n_ctx, 2 * KV_W), BF16),
                   jax.ShapeDtypeStruct((n_ctx, 2 * LANES), BF16)],
        compiler_params=_cp("arbitrary"),
    )(ctx, modc, w_kv)


def _host_start(step, comm):
    if comm is not None:
        @pl.when(step == 0)
        def _():
            comm.start()


def _host_finish(step, last, comm):
    if comm is not None:
        @pl.when(step == last)
        def _():
            comm.finish()


def _proj_fwd(x, modx, w_in, cos, sin, tm, gather=()):
    seq = x.shape[0]
    ng = len(gather)

    def body(x_ref, mod_ref, w_ref, cos_ref, sin_ref, *rest):
        h_ref, q_ref, kv_ref, va_ref, uv_ref, gab_ref = rest[ng:ng + 6]
        comm = _Gather(rest[:ng], rest[ng + 6:2 * ng + 6], *rest[2 * ng + 6:]) if ng else None
        _host_start(pl.program_id(0), comm)
        xhat, _ = _ln(x_ref[...])
        h = (xhat * (1.0 + mod_ref[1:2, :]) + mod_ref[0:1, :]).astype(BF16)
        h_ref[...] = h
        cos1, sin1 = cos_ref[...], sin_ref[...]
        cos2 = jnp.concatenate([cos1, cos1], axis=1)
        sin2 = jnp.concatenate([sin1, sin1], axis=1)
        for j in range(Q_W // 256):
            t = _dot(h, w_ref[:, 256 * j:256 * (j + 1)])
            q_ref[:, 256 * j:256 * (j + 1)] = (_rope(t, cos2, sin2) * SCALE).astype(BF16)
        t = _dot(h, w_ref[:, Q_W:Q_W + 2 * KV_W])
        kv_ref[:, :KV_W] = _rope(t[:, :KV_W], cos1, sin1).astype(BF16)
        v = t[:, KV_W:].astype(BF16)
        kv_ref[:, KV_W:] = v
        va_ref[...] = _with_ones(v)
        o = Q_W + 2 * KV_W
        for j in range(2):
            uv_ref[:, G_W * j:G_W * (j + 1)] = _dot(h, w_ref[:, o + G_W * j:o + G_W * (j + 1)])
        o += 2 * G_W
        for j in range(4):
            gab_ref[:, 512 * j:512 * (j + 1)] = _dot(h, w_ref[:, o + 512 * j:o + 512 * (j + 1)]).astype(BF16)
        _host_finish(pl.program_id(0), seq // tm - 1, comm)

    out = pl.pallas_call(
        body, name="proj_fwd", grid=(seq // tm,),
        in_specs=[_rows(tm, D), _acc((8, D)), _resident((D, IN_W)), _rows(tm, LANES), _rows(tm, LANES)] + _comm_specs(ng),
        out_specs=[_rows(tm, D), _rows(tm, Q_W), _rows(tm, 2 * KV_W), _rows(tm, 2 * LANES), _rows(tm, 2 * G_W),
                   _rows(tm, 2 * D)] + _comm_specs(ng),
        out_shape=[jax.ShapeDtypeStruct((seq, D), BF16), jax.ShapeDtypeStruct((seq, Q_W), BF16),
                   jax.ShapeDtypeStruct((seq, 2 * KV_W), BF16), jax.ShapeDtypeStruct((seq, 2 * LANES), BF16),
                   jax.ShapeDtypeStruct((seq, 2 * G_W), F32), jax.ShapeDtypeStruct((seq, 2 * D), BF16)] + _gathered_shapes(gather),
        scratch_shapes=_comm_scratch(ng) if ng else [],
        compiler_params=_cp("arbitrary"),
    )(x, modx, w_in, cos, sin, *gather)
    return out[:6], out[6:]


def _stack_heads(x, hk):
    return jnp.concatenate([x[:, (hk * GROUP + g) * HEAD:(hk * GROUP + g + 1) * HEAD] for g in range(GROUP)], axis=0)


def _attn_scores(q_ref, k_refs, hk, n, nb):
    q4 = _stack_heads(q_ref[...], hk)
    ks = [r[:, hk * HEAD:(hk + 1) * HEAD] for r in k_refs]
    rows = GROUP * BLK
    qi = lax.broadcasted_iota(jnp.int32, (rows, BLK), 0) & (BLK - 1)
    kj = lax.broadcasted_iota(jnp.int32, (rows, BLK), 1)
    s = [_dot_nt(q4, k) for k in ks]
    s[1] = jnp.where((kj >= qi) & (n > 0), s[1], NEG)
    s[3] = jnp.where((kj <= qi) & (n < nb - 1), s[3], NEG)
    return q4, ks, s


def _sink_rows(sink_ref, hk):
    rows = GROUP * BLK
    rg = lax.broadcasted_iota(jnp.int32, (rows, 1), 0) >> 7
    sink_v = jnp.full((rows, 1), sink_ref[0, hk * GROUP], F32)
    for g in range(1, GROUP):
        sink_v = jnp.where(rg == g, sink_ref[0, hk * GROUP + g], sink_v)
    return sink_v


def _with_ones(v):
    ones = jnp.ones((v.shape[0], HEAD), v.dtype)
    return jnp.concatenate([v[:, :HEAD], ones, v[:, HEAD:], ones], axis=1)


def _kv_specs(nb):
    return [pl.BlockSpec((BLK, 2 * KV_W), lambda n: (jnp.maximum(n - 1, 0), 0)),
            pl.BlockSpec((BLK, 2 * KV_W), lambda n: (n, 0)),
            pl.BlockSpec((BLK, 2 * KV_W), lambda n: (jnp.minimum(n + 1, nb - 1), 0))]


def _attn_fwd(q, kv, va, kvc, vac, sink, gather=()):
    seq = q.shape[0]
    nb = seq // BLK
    n_ctx = kvc.shape[0]
    ng = len(gather)

    def body(q_ref, kvp_ref, kvm_ref, kvn_ref, vap_ref, vam_ref, van_ref, kvc_ref, vac_ref, sink_ref, *rest):
        o_ref, lse_ref = rest[ng:ng + 2]
        comm = _Gather(rest[:ng], rest[ng + 2:2 * ng + 2], *rest[2 * ng + 2:]) if ng else None
        n = pl.program_id(0)
        _host_start(n, comm)
        outs = []
        lane = lax.broadcasted_iota(jnp.int32, (BLK, LANES), 1)
        lse_all = jnp.zeros((BLK, LANES), F32)
        for hk in range(N_KV):
            _, _, s = _attn_scores(q_ref, (kvc_ref, kvp_ref, kvm_ref, kvn_ref), hk, n, nb)
            sink_v = _sink_rows(sink_ref, hk)
            m = sink_v
            for t in s:
                m = jnp.maximum(m, jnp.max(t, axis=-1, keepdims=True))
            o = jnp.zeros((GROUP * BLK, LANES), F32)
            for t, va_ref in zip(s, (vac_ref, vap_ref, vam_ref, van_ref)):
                o = o + _dot(jnp.exp((t - m).astype(BF16)), va_ref[:, hk * LANES:(hk + 1) * LANES])
            denom = o[:, HEAD:HEAD + 1] + jnp.exp(sink_v - m)
            o4 = o[:, :HEAD] * (1.0 / denom)
            lse4 = m + jnp.log(denom)
            for g in range(GROUP):
                outs.append(o4[g * BLK:(g + 1) * BLK, :])
                lse_all = jnp.where(lane == hk * GROUP + g, lse4[g * BLK:(g + 1) * BLK, :], lse_all)
        o_ref[...] = jnp.concatenate(outs, axis=1).astype(BF16)
        lse_ref[...] = lse_all
        _host_finish(n, nb - 1, comm)

    out = pl.pallas_call(
        body, name="attn_fwd", grid=(nb,),
        in_specs=[_rows(BLK, Q_W)] + _kv_specs(nb) + _kv_specs(nb)
        + [_acc((n_ctx, 2 * KV_W)), _acc((n_ctx, 2 * LANES)), pl.BlockSpec(memory_space=pltpu.SMEM)] + _comm_specs(ng),
        out_specs=[_rows(BLK, Q_W), _rows(BLK, LANES)] + _comm_specs(ng),
        out_shape=[jax.ShapeDtypeStruct((seq, Q_W), BF16), jax.ShapeDtypeStruct((seq, LANES), F32)] + _gathered_shapes(gather),
        scratch_shapes=_comm_scratch(ng) if ng else [],
        compiler_params=_cp("arbitrary"),
    )(q, kv, kv, kv, va, va, va, kvc, vac, sink, *gather)
    return out[:2], out[2:]


def _gmlp_chunk(u, vb, gp_ref, ws_ref, bias_ref):
    gu, tu = _gelu(u)
    gv, tv = _gelu(vb)
    vhat, rstd = _ln(gv)
    vn = (vhat * gp_ref[0:1, :] + gp_ref[1:2, :]).astype(BF16)
    s = bias_ref[...] + jnp.concatenate(
        [_dot(ws_ref[g * BLK:(g + 1) * BLK, :], vn[:, g * GRP_D:(g + 1) * GRP_D]) for g in range(N_GRP)], axis=1)
    return gu, tu, tv, vhat, rstd, vn, s


def _mix_fwd(uv, gab, ya, gp, ws_stack, bias_full, w_a, w_b, w_o, tm):
    seq = uv.shape[0]

    def body(uv_ref, gab_ref, ya_ref, gp_ref, ws_ref, bias_ref, wa_ref, wb_ref, wo_ref,
             a_ref, b_ref, mix_ref, merged_ref, yb_ref):
        for c in range(tm // BLK):
            rs = slice(c * BLK, (c + 1) * BLK)
            gu, _, _, _, _, _, s = _gmlp_chunk(uv_ref[rs, :G_W], uv_ref[rs, G_W:], gp_ref, ws_ref, bias_ref)
            yb_ref[rs, :] = (gu * s).astype(BF16)
        ya = ya_ref[...]
        yb = yb_ref[...]
        for s in range(N_SHARD):
            cs = slice(s * (D // N_SHARD), (s + 1) * (D // N_SHARD))
            a = _dot(ya, wa_ref[s])
            b = _dot(yb, wb_ref[s])
            a_ref[:, cs] = a.astype(BF16)
            b_ref[:, cs] = b.astype(BF16)
            ga = gab_ref[:, cs].astype(F32)
            gb = gab_ref[:, D + s * (D // N_SHARD):D + (s + 1) * (D // N_SHARD)].astype(F32)
            merged_ref[:, cs] = (_sig(ga) * a + _sig(gb) * b).astype(BF16)
        mix_ref[...] = _dot(merged_ref[...], wo_ref[...])

    return pl.pallas_call(
        body, name="mix_fwd", grid=(seq // tm,),
        in_specs=[_rows(tm, 2 * G_W), _rows(tm, 2 * D), _rows(tm, Q_W), _acc((8, G_W)),
                  _resident((N_GRP * BLK, BLK)), _acc((BLK, G_W)),
                  _resident((N_SHARD, Q_W, D // N_SHARD)), _resident((N_SHARD, G_W, D // N_SHARD)), _resident((D, D))],
        out_specs=[_rows(tm, D), _rows(tm, D), _rows(tm, D), _rows(tm, D), _rows(tm, G_W)],
        out_shape=[jax.ShapeDtypeStruct((seq, D), BF16), jax.ShapeDtypeStruct((seq, D), BF16),
                   jax.ShapeDtypeStruct((seq, D), F32), jax.ShapeDtypeStruct((seq, D), BF16),
                   jax.ShapeDtypeStruct((seq, G_W), BF16)],
        compiler_params=_cp("arbitrary"),
    )(uv, gab, ya, gp, ws_stack, bias_full, w_a, w_b, w_o)


FFN_CHUNK = 512


def _ffn_chunks():
    out = []
    for hh in range(2):
        off = 0
        while off < FH_SHARD:
            w = min(FFN_CHUNK, FH_SHARD - off)
            out.append((hh, off, w))
            off += w
    return out


def _mid_recompute(x_ref, mix_ref, vec_ref):
    r1 = ALPHA * x_ref[...] + vec_ref[0:1, :] * mix_ref[...]
    xh1, rstd1 = _ln(r1)
    xmid = xh1 * vec_ref[1:2, :] + vec_ref[2:3, :]
    xh2, rstd2 = _ln(xmid)
    return xh1, rstd1, xmid, xh2, rstd2


def _ffn_fwd(x, mix, tgt, vec, w_fi, w_fo, tm):
    seq = x.shape[0]

    def body(x_ref, mix_ref, tgt_ref, vec_ref, wi_ref, wo_ref, gu_ref, act_ref, h2_ref, dr2_ref, st_ref):
        @pl.when(pl.program_id(0) == 0)
        def _():
            st_ref[...] = jnp.zeros_like(st_ref)

        _, _, xmid, xh2, _ = _mid_recompute(x_ref, mix_ref, vec_ref)
        h2 = (xh2 * (1.0 + vec_ref[4:5, :]) + vec_ref[3:4, :]).astype(BF16)
        h2_ref[...] = h2
        f = jnp.zeros((tm, D), F32)
        for hh, off, w in _ffn_chunks():
            cs = slice(hh * FH_SHARD + off, hh * FH_SHARD + off + w)
            cu = slice(FH + hh * FH_SHARD + off, FH + hh * FH_SHARD + off + w)
            g = _dot(h2, wi_ref[hh, :, off:off + w])
            u = _dot(h2, wi_ref[2 + hh, :, off:off + w])
            gu_ref[:, cs] = g.astype(BF16)
            gu_ref[:, cu] = u.astype(BF16)
            a = (g * _sig(g) * u).astype(BF16)
            act_ref[:, cs] = a
            f = f + _dot(a, wo_ref[cs, :])
        r2 = ALPHA * xmid + vec_ref[5:6, :] * f
        yh, rstd = _ln(r2)
        y = yh * vec_ref[6:7, :] + vec_ref[7:8, :]
        err = y - tgt_ref[...]
        dy = err / D
        dr2 = _ln_bwd(dy * vec_ref[6:7, :], yh, rstd)
        dr2_ref[...] = dr2
        st_ref[0:1, :] += _colsum(err * err)
        st_ref[1:2, :] += _colsum(dy * yh)
        st_ref[2:3, :] += _colsum(dy)
        st_ref[3:4, :] += _colsum(dr2 * f)

    return pl.pallas_call(
        body, name="ffn_fwd", grid=(seq // tm,),
        in_specs=[_rows(tm, D), _rows(tm, D), _rows(tm, D), _acc((8, D)), _resident((N_SHARD, D, FH_SHARD)), _resident((FH, D))],
        out_specs=[_rows(tm, 2 * FH), _rows(tm, FH), _rows(tm, D), _rows(tm, D), _acc((8, D))],
        out_shape=[jax.ShapeDtypeStruct((seq, 2 * FH), BF16), jax.ShapeDtypeStruct((seq, FH), BF16),
                   jax.ShapeDtypeStruct((seq, D), BF16), jax.ShapeDtypeStruct((seq, D), F32),
                   jax.ShapeDtypeStruct((8, D), F32)],
        compiler_params=_cp("arbitrary"),
    )(x, mix, tgt, vec, w_fi, w_fo)


def _ffn_bwd(dr2, gu, x, mix, vec, w_fi, w_fo, tm):
    seq = x.shape[0]

    def body(dr2_ref, gu_ref, x_ref, mix_ref, vec_ref, wi_ref, wo_ref, dff_ref, df_ref, dr1_ref, st_ref):
        @pl.when(pl.program_id(0) == 0)
        def _():
            st_ref[...] = jnp.zeros_like(st_ref)

        dr2 = dr2_ref[...]
        df = (dr2 * vec_ref[5:6, :]).astype(BF16)
        df_ref[...] = df
        dh2 = jnp.zeros((tm, D), F32)
        for hh, off, w in _ffn_chunks():
            cs = slice(hh * FH_SHARD + off, hh * FH_SHARD + off + w)
            cu = slice(FH + hh * FH_SHARD + off, FH + hh * FH_SHARD + off + w)
            da = _dot_nt(df, wo_ref[cs, :])
            g = gu_ref[:, cs].astype(F32)
            u = gu_ref[:, cu].astype(F32)
            sg = _sig(g)
            dg = (da * u * sg * (1.0 + g * (1.0 - sg))).astype(BF16)
            du = (da * g * sg).astype(BF16)
            dff_ref[:, cs] = dg
            dff_ref[:, cu] = du
            dh2 = dh2 + _dot_nt(dg, wi_ref[hh, :, off:off + w]) + _dot_nt(du, wi_ref[2 + hh, :, off:off + w])
        xh1, rstd1, _, xh2, rstd2 = _mid_recompute(x_ref, mix_ref, vec_ref)
        dxmid = _ln_bwd(dh2 * (1.0 + vec_ref[4:5, :]), xh2, rstd2) + ALPHA * dr2
        dr1 = _ln_bwd(dxmid * vec_ref[1:2, :], xh1, rstd1)
        dr1_ref[...] = dr1
        st_ref[0:1, :] += _colsum(dh2 * xh2)
        st_ref[1:2, :] += _colsum(dh2)
        st_ref[2:3, :] += _colsum(dxmid * xh1)
        st_ref[3:4, :] += _colsum(dxmid)
        st_ref[4:5, :] += _colsum(dr1 * mix_ref[...])

    return pl.pallas_call(
        body, name="ffn_bwd", grid=(seq // tm,),
        in_specs=[_rows(tm, D), _rows(tm, 2 * FH), _rows(tm, D), _rows(tm, D), _acc((8, D)),
                  _resident((N_SHARD, D, FH_SHARD)), _resident((FH, D))],
        out_specs=[_rows(tm, 2 * FH), _rows(tm, D), _rows(tm, D), _acc((8, D))],
        out_shape=[jax.ShapeDtypeStruct((seq, 2 * FH), BF16), jax.ShapeDtypeStruct((seq, D), BF16),
                   jax.ShapeDtypeStruct((seq, D), F32), jax.ShapeDtypeStruct((8, D), F32)],
        compiler_params=_cp("arbitrary"),
    )(dr2, gu, x, mix, vec, w_fi, w_fo)


def _mix_bwd(dr1, a, b, gab, uv, vec, gp, ws_stack, ws_stack_t, bias_full, w_a, w_b, w_o, tm):
    seq = dr1.shape[0]

    def body(dr1_ref, a_ref, b_ref, gab_ref, uv_ref, vec_ref, gp_ref, ws_ref, wst_ref, bias_ref, wa_ref, wb_ref, wo_ref,
             dmix_ref, da_ref, db_ref, dya_ref, dp_ref, dws_ref, dbs_ref, st_ref):
        @pl.when(pl.program_id(0) == 0)
        def _():
            dws_ref[...] = jnp.zeros_like(dws_ref)
            dbs_ref[...] = jnp.zeros_like(dbs_ref)
            st_ref[...] = jnp.zeros_like(st_ref)

        dmix = (dr1_ref[...] * vec_ref[0:1, :]).astype(BF16)
        dmix_ref[...] = dmix
        dmerged = _dot_nt(dmix, wo_ref[...])
        sa = _sig(gab_ref[:, :D].astype(F32))
        sb = _sig(gab_ref[:, D:].astype(F32))
        da = (dmerged * sa).astype(BF16)
        db = (dmerged * sb).astype(BF16)
        da_ref[...] = da
        db_ref[...] = db
        dp_ref[:, 2 * G_W:2 * G_W + D] = (dmerged * a_ref[...].astype(F32) * sa * (1.0 - sa)).astype(BF16)
        dp_ref[:, 2 * G_W + D:] = (dmerged * b_ref[...].astype(F32) * sb * (1.0 - sb)).astype(BF16)
        dya = jnp.zeros((tm, Q_W), F32)
        dyb = jnp.zeros((tm, G_W), F32)
        for s in range(N_SHARD):
            cs = slice(s * (D // N_SHARD), (s + 1) * (D // N_SHARD))
            dya = dya + _dot_nt(da[:, cs], wa_ref[s])
            dyb = dyb + _dot_nt(db[:, cs], wb_ref[s])
        dya_ref[...] = dya.astype(BF16)
        for c in range(tm // BLK):
            rs = slice(c * BLK, (c + 1) * BLK)
            u = uv_ref[rs, :G_W]
            vb = uv_ref[rs, G_W:]
            gu, tu, tv, vhat, rstd, vn, s = _gmlp_chunk(u, vb, gp_ref, ws_ref, bias_ref)
            dyb_c = dyb[rs, :]
            ds = dyb_c * gu
            du = dyb_c * s * _gelu_grad(u, tu)
            ds_b = ds.astype(BF16)
            dvn_g = []
            for g in range(N_GRP):
                cg = slice(g * GRP_D, (g + 1) * GRP_D)
                dvn_g.append(_dot(wst_ref[:, g * BLK:(g + 1) * BLK], ds_b[:, cg]))
                dws_ref[g * BLK:(g + 1) * BLK, :] += _dot_nt(ds_b[:, cg], vn[:, cg])
            dvn = jnp.concatenate(dvn_g, axis=1)
            dbs_ref[...] += ds
            st_ref[0:1, :] += _colsum(dvn * vhat)
            st_ref[1:2, :] += _colsum(dvn)
            dgv = _ln_bwd(dvn * gp_ref[0:1, :], vhat, rstd)
            dvb = dgv * _gelu_grad(vb, tv)
            dp_ref[rs, :G_W] = du.astype(BF16)
            dp_ref[rs, G_W:2 * G_W] = dvb.astype(BF16)

    pw = 2 * G_W + 2 * D
    return pl.pallas_call(
        body, name="mix_bwd", grid=(seq // tm,),
        in_specs=[_rows(tm, D), _rows(tm, D), _rows(tm, D), _rows(tm, 2 * D), _rows(tm, 2 * G_W), _acc((8, D)), _acc((8, G_W)),
                  _resident((N_GRP * BLK, BLK)), _resident((BLK, N_GRP * BLK)), _acc((BLK, G_W)),
                  _resident((N_SHARD, Q_W, D // N_SHARD)), _resident((N_SHARD, G_W, D // N_SHARD)), _resident((D, D))],
        out_specs=[_rows(tm, D), _rows(tm, D), _rows(tm, D), _rows(tm, Q_W), _rows(tm, pw),
                   _acc((N_GRP * BLK, BLK)), _acc((BLK, G_W)), _acc((8, G_W))],
        out_shape=[jax.ShapeDtypeStruct((seq, D), BF16), jax.ShapeDtypeStruct((seq, D), BF16),
                   jax.ShapeDtypeStruct((seq, D), BF16), jax.ShapeDtypeStruct((seq, Q_W), BF16),
                   jax.ShapeDtypeStruct((seq, pw), BF16), jax.ShapeDtypeStruct((N_GRP * BLK, BLK), F32),
                   jax.ShapeDtypeStruct((BLK, G_W), F32), jax.ShapeDtypeStruct((8, G_W), F32)],
        compiler_params=_cp("arbitrary"),
    )(dr1, a, b, gab, uv, vec, gp, ws_stack, ws_stack_t, bias_full, w_a, w_b, w_o)


def _attn_bwd(q, kv, kvc, sink, dya, ya, lse, scatter=()):
    seq = q.shape[0]
    nb = seq // BLK
    n_ctx = kvc.shape[0]
    ns = len(scatter)

    def body(q_ref, kvp_ref, kvm_ref, kvn_ref, kvc_ref, sink_ref, do_ref, o_ref, lse_ref, *rest):
        dq_ref, dkv_ref, dkvc_ref, dsink_ref = rest[ns:ns + 4]
        comm = _AllToAll(rest[:ns], rest[ns + 4:2 * ns + 4], *rest[2 * ns + 4:]) if ns else None
        n = pl.program_id(0)
        _host_start(n, comm)

        @pl.when(n == 0)
        def _():
            dkv_ref[...] = jnp.zeros_like(dkv_ref)
            dkvc_ref[...] = jnp.zeros_like(dkvc_ref)
            dsink_ref[...] = jnp.zeros_like(dsink_ref)

        do = do_ref[...]
        out = o_ref[...]
        lse_all = lse_ref[...]
        k_refs = (kvc_ref, kvp_ref, kvm_ref, kvn_ref)
        dqs, dks, dvs = [], [], []
        for hk in range(N_KV):
            q4, ks, s = _attn_scores(q_ref, k_refs, hk, n, nb)
            vs = [r[:, KV_W + hk * HEAD:KV_W + (hk + 1) * HEAD] for r in k_refs]
            lse4 = jnp.concatenate([lse_all[:, hk * GROUP + g:hk * GROUP + g + 1] for g in range(GROUP)], axis=0)
            do4 = _stack_heads(do, hk)
            delta = jnp.sum(do4.astype(F32) * _stack_heads(out, hk).astype(F32), axis=-1, keepdims=True)
            p = [jnp.exp((t - lse4).astype(BF16)) for t in s]
            ds = [t * (_dot_nt(do4, v) - delta).astype(BF16) for t, v in zip(p, vs)]
            dq4 = _dot(ds[0], ks[0])
            for t, k in zip(ds[1:], ks[1:]):
                dq4 = dq4 + _dot(t, k)
            dq4 = dq4 * SCALE
            dqs += [dq4[g * BLK:(g + 1) * BLK, :] for g in range(GROUP)]
            dks.append([_dot_tn(t, q4) for t in ds])
            dvs.append([_dot_tn(t, do4) for t in p])
            ps = jnp.exp(_sink_rows(sink_ref, hk) - lse4) * delta
            lane = lax.broadcasted_iota(jnp.int32, (1, LANES), 1)
            for g in range(GROUP):
                part = -jnp.sum(ps[g * BLK:(g + 1) * BLK, :], axis=0, keepdims=True)
                dsink_ref[0:1, :] += jnp.where(lane == hk * GROUP + g, part, 0.0)
        dq_ref[...] = jnp.concatenate(dqs, axis=1)

        def piece(i):
            return jnp.concatenate([dks[0][i], dks[1][i], dvs[0][i], dvs[1][i]], axis=1)

        dkvc_ref[...] += piece(0)
        starts = (jnp.maximum(n - 1, 0), n, jnp.minimum(n + 1, nb - 1))
        for i, st in enumerate(starts):
            r = pl.ds(pl.multiple_of(st * BLK, BLK), BLK)
            dkv_ref[r, :] += piece(i + 1)
        _host_finish(n, nb - 1, comm)

    out = pl.pallas_call(
        body, name="attn_bwd", grid=(nb,),
        in_specs=[_rows(BLK, Q_W)] + _kv_specs(nb) + [_acc((n_ctx, 2 * KV_W)), pl.BlockSpec(memory_space=pltpu.SMEM),
                                                      _rows(BLK, Q_W), _rows(BLK, Q_W), _rows(BLK, LANES)] + _comm_specs(ns),
        out_specs=[_rows(BLK, Q_W), _acc((seq, 2 * KV_W)), _acc((n_ctx, 2 * KV_W)), _acc((8, LANES))] + _comm_specs(ns),
        out_shape=[jax.ShapeDtypeStruct((seq, Q_W), F32), jax.ShapeDtypeStruct((seq, 2 * KV_W), F32),
                   jax.ShapeDtypeStruct((n_ctx, 2 * KV_W), F32), jax.ShapeDtypeStruct((8, LANES), F32)]
        + [jax.ShapeDtypeStruct(v.shape, v.dtype) for v in scatter],
        scratch_shapes=_comm_scratch(ns) if ns else [],
        compiler_params=_cp("arbitrary"),
    )(q, kv, kv, kv, kvc, sink, dya, ya, lse, *scatter)
    return out[:4], out[4:]


def _proj_bwd(dq, dkv, dpb, x, dr1, modx, w_in, cos, sin, tm, scatter=()):
    seq = x.shape[0]
    pw = IN_W - Q_W - 2 * KV_W
    ns = len(scatter)

    def body(dq_ref, dkv_ref, dpb_ref, x_ref, dr1_ref, mod_ref, w_ref, cos_ref, sin_ref, *rest):
        dqkv_ref, gx_ref, st_ref = rest[ns:ns + 3]
        comm = _AllToAll(rest[:ns], rest[ns + 3:2 * ns + 3], *rest[2 * ns + 3:]) if ns else None
        _host_start(pl.program_id(0), comm)

        @pl.when(pl.program_id(0) == 0)
        def _():
            st_ref[...] = jnp.zeros_like(st_ref)

        cos1, sin1 = cos_ref[...], sin_ref[...]
        cos2 = jnp.concatenate([cos1, cos1], axis=1)
        sin2 = jnp.concatenate([sin1, sin1], axis=1)
        for j in range(Q_W // 256):
            cs = slice(256 * j, 256 * (j + 1))
            dqkv_ref[:, cs] = _unrope(dq_ref[:, cs], cos2, sin2).astype(BF16)
        dqkv_ref[:, Q_W:Q_W + KV_W] = _unrope(dkv_ref[:, :KV_W], cos1, sin1).astype(BF16)
        dqkv_ref[:, Q_W + KV_W:] = dkv_ref[:, KV_W:].astype(BF16)
        o = Q_W + 2 * KV_W
        dh = _dot_nt(dqkv_ref[...], w_ref[:, :o]) + _dot_nt(dpb_ref[...], w_ref[:, o:])
        xhat, rstd = _ln(x_ref[...])
        st_ref[0:1, :] += _colsum(dh)
        st_ref[1:2, :] += _colsum(dh * xhat)
        gx_ref[...] = _ln_bwd(dh * (1.0 + mod_ref[1:2, :]), xhat, rstd) + ALPHA * dr1_ref[...]
        _host_finish(pl.program_id(0), seq // tm - 1, comm)

    out = pl.pallas_call(
        body, name="proj_bwd", grid=(seq // tm,),
        in_specs=[_rows(tm, Q_W), _rows(tm, 2 * KV_W), _rows(tm, pw), _rows(tm, D), _rows(tm, D), _acc((8, D)),
                  _resident((D, IN_W)), _rows(tm, LANES), _rows(tm, LANES)] + _comm_specs(ns),
        out_specs=[_rows(tm, Q_W + 2 * KV_W), _rows(tm, D), _acc((8, D))] + _comm_specs(ns),
        out_shape=[jax.ShapeDtypeStruct((seq, Q_W + 2 * KV_W), BF16), jax.ShapeDtypeStruct((seq, D), F32),
                   jax.ShapeDtypeStruct((8, D), F32)] + [jax.ShapeDtypeStruct(v.shape, v.dtype) for v in scatter],
        scratch_shapes=_comm_scratch(ns) if ns else [],
        compiler_params=_cp("arbitrary"),
    )(dq, dkv, dpb, x, dr1, modx, w_in, cos, sin, *scatter)
    return out[:3], out[3:]


def _ctx_bwd(dkvc, ctx, hc, w_kv):
    n_ctx = ctx.shape[0]

    def body(dkvc_ref, ctx_ref, hc_ref, w_ref, dw_ref, st_ref):
        d = dkvc_ref[...].astype(BF16)
        dw_ref[...] = _dot_tn(hc_ref[...], d)
        dhc = _dot_nt(d, w_ref[...])
        xhat, _ = _ln(ctx_ref[...])
        st_ref[...] = jnp.zeros_like(st_ref)
        st_ref[0:1, :] = _colsum(dhc)
        st_ref[1:2, :] = _colsum(dhc * xhat)

    return pl.pallas_call(
        body, name="ctx_bwd", grid=(1,),
        in_specs=[_acc((n_ctx, 2 * KV_W)), _acc((n_ctx, D)), _acc((n_ctx, D)), _acc((D, 2 * KV_W))],
        out_specs=[_acc((D, 2 * KV_W)), _acc((8, D))],
        out_shape=[jax.ShapeDtypeStruct((D, 2 * KV_W), F32), jax.ShapeDtypeStruct((8, D), F32)],
        compiler_params=_cp("arbitrary"),
    )(dkvc, ctx, hc, w_kv)


def _tn_matmul(a, b, tn, name, out_dtype, shard_major=False, init=None, tk=512):
    t, ka = a.shape
    n = b.shape[1]
    tk = min(tk, t)
    nk = t // tk
    has_init = init is not None

    def body(*refs):
        if has_init:
            a_ref, b_ref, i_ref, o_ref, acc_ref = refs
        else:
            a_ref, b_ref, o_ref, acc_ref = refs
        k = pl.program_id(1)

        @pl.when(k == 0)
        def _():
            acc_ref[...] = i_ref[...] if has_init else jnp.zeros_like(acc_ref)

        acc_ref[...] += _dot_tn(a_ref[...], b_ref[...])

        @pl.when(k == nk - 1)
        def _():
            o_ref[...] = acc_ref[...].astype(out_dtype)

    in_specs = [pl.BlockSpec((tk, ka), lambda j, k: (k, 0)), pl.BlockSpec((tk, tn), lambda j, k: (k, j))]
    args = [a, b]
    if has_init:
        in_specs.append(pl.BlockSpec((ka, tn), lambda j, k: (0, j)))
        args.append(init)
    if shard_major:
        out_spec = pl.BlockSpec((None, ka, tn), lambda j, k: (j, 0, 0))
        out_shape = jax.ShapeDtypeStruct((n // tn, ka, tn), out_dtype)
    else:
        out_spec = pl.BlockSpec((ka, tn), lambda j, k: (0, j))
        out_shape = jax.ShapeDtypeStruct((ka, n), out_dtype)
    return pl.pallas_call(
        body, name=name, grid=(n // tn, nk), in_specs=in_specs, out_specs=out_spec, out_shape=out_shape,
        scratch_shapes=[pltpu.VMEM((ka, tn), F32)],
        compiler_params=_cp("arbitrary", "arbitrary"),
    )(*args)


ADA_TILE = 512


def _ada_fwd(sc_all, w_ada):
    cs = w_ada.shape[1]

    def body(s_ref, w_ref, o_ref):
        o_ref[...] = _dot(s_ref[...].astype(BF16), w_ref[...].astype(BF16))

    return pl.pallas_call(
        body, name="ada_fwd", grid=(cs // ADA_TILE,),
        in_specs=[_acc((16, D)), pl.BlockSpec((D, ADA_TILE), lambda j: (0, j))],
        out_specs=pl.BlockSpec((16, ADA_TILE), lambda j: (0, j)),
        out_shape=jax.ShapeDtypeStruct((16, cs), F32),
        compiler_params=_cp("arbitrary"),
    )(sc_all, w_ada)


def _ada_bwd(sc_all_t, dm_all, dmc, w_ada):
    cs = w_ada.shape[1]

    def body(st_ref, dm_ref, dmc_ref, w_ref, gw_ref, part_ref):
        @pl.when(pl.program_id(0) == 0)
        def _():
            part_ref[...] = jnp.zeros_like(part_ref)

        gw_ref[...] = _dot(st_ref[...].astype(BF16), dm_ref[...].astype(BF16))
        part_ref[...] += _dot_nt(dmc_ref[...].astype(BF16), w_ref[...].astype(BF16))

    return pl.pallas_call(
        body, name="ada_bwd", grid=(cs // ADA_TILE,),
        in_specs=[_acc((D, 16)), pl.BlockSpec((16, ADA_TILE), lambda j: (0, j)), pl.BlockSpec((8, ADA_TILE), lambda j: (0, j)),
                  pl.BlockSpec((D, ADA_TILE), lambda j: (0, j))],
        out_specs=[pl.BlockSpec((D, ADA_TILE), lambda j: (0, j)), _acc((8, D))],
        out_shape=[jax.ShapeDtypeStruct((D, cs), F32), jax.ShapeDtypeStruct((8, D), F32)],
        compiler_params=_cp("arbitrary"),
    )(sc_all_t, dm_all, dmc, w_ada)


def _sum8(x, name, tr=256):
    _, r, c = x.shape
    tr = min(tr, r)
    while r % tr:
        tr -= 16

    def body(x_ref, o_ref):
        acc = x_ref[0].astype(F32)
        for i in range(1, N_DEV):
            acc = acc + x_ref[i].astype(F32)
        o_ref[...] = acc

    return pl.pallas_call(
        body, name=name, grid=(r // tr,),
        in_specs=[pl.BlockSpec((N_DEV, tr, c), lambda i: (0, i, 0))],
        out_specs=pl.BlockSpec((tr, c), lambda i: (i, 0)),
        out_shape=jax.ShapeDtypeStruct((r, c), F32),
        compiler_params=_cp("arbitrary"),
    )(x)


def _sum_blocks(recv, src, me, name, tr=256):
    _, r, c = recv.shape
    tr = min(tr, r)
    while r % tr:
        tr -= 16

    def body(me_ref, recv_ref, own_ref, o_ref):
        acc = own_ref[...].astype(F32)
        for k in range(1, N_DEV):
            acc = acc + recv_ref[me_ref[0] ^ k].astype(F32)
        o_ref[...] = acc

    return pl.pallas_call(
        body, name=name,
        grid_spec=pltpu.PrefetchScalarGridSpec(
            num_scalar_prefetch=1, grid=(r // tr,),
            in_specs=[pl.BlockSpec((N_DEV, tr, c), lambda i, me_ref: (0, i, 0)),
                      pl.BlockSpec((None, tr, c), lambda i, me_ref: (me_ref[0], i, 0))],
            out_specs=pl.BlockSpec((tr, c), lambda i, me_ref: (i, 0))),
        out_shape=jax.ShapeDtypeStruct((r, c), F32),
        compiler_params=_cp("arbitrary"),
    )(me, recv, src)


def _sum8_many(xs, name):
    n = len(xs)

    def body(*refs):
        for x_ref, o_ref in zip(refs[:n], refs[n:]):
            acc = x_ref[0]
            for i in range(1, N_DEV):
                acc = acc + x_ref[i]
            o_ref[...] = acc

    vmem = pl.BlockSpec(memory_space=pltpu.VMEM)
    return pl.pallas_call(
        body, name=name, in_specs=[vmem] * n, out_specs=[vmem] * n,
        out_shape=[jax.ShapeDtypeStruct(v.shape[1:], v.dtype) for v in xs],
        compiler_params=pltpu.CompilerParams(vmem_limit_bytes=VMEM_LIMIT),
    )(*xs)


def _adam_update(w, g, m, v):
    nm = ADAM_B1 * m + (1.0 - ADAM_B1) * g
    nv = ADAM_B2 * v + (1.0 - ADAM_B2) * (g * g)
    m_hat = nm / (1.0 - ADAM_B1 ** ADAM_STEP)
    v_hat = nv / (1.0 - ADAM_B2 ** ADAM_STEP)
    return -ADAM_LR * (m_hat / (jnp.sqrt(v_hat) + ADAM_EPS) + ADAM_WD * w), nm, nv


ROW_LOSS, ROW_LN2_G, ROW_LN2_B, ROW_LN1_G, ROW_LN1_B = 0, 1, 2, 10, 11
ROWS_DMOD_X = (16, 17, 12, 9, 8, 3)
ROWS_DMOD_C = (24, 25)
SMALL = ("c_ctx", "b_ada", "attn_sink", "gmlp_ln_g", "gmlp_ln_b", "w_spatial", "b_spatial", "ln1_g", "ln1_b", "ln2_g", "ln2_b")


def _adamw_small(sums, dsc, w, m, v):
    n = len(SMALL)

    def body(*refs):
        st_ref, gm_ref, sk_ref, ws_ref, bs_ref, dsc_ref = refs[:6]
        w_refs = dict(zip(SMALL, refs[6:6 + n]))
        m_refs = dict(zip(SMALL, refs[6 + n:6 + 2 * n]))
        v_refs = dict(zip(SMALL, refs[6 + 2 * n:6 + 3 * n]))
        outs = refs[6 + 3 * n:]
        c = w_refs["c_ctx"][...]
        sg = _sig(c)
        dmod = [st_ref[r:r + 1, :] for r in ROWS_DMOD_X]
        dmod[0] = dmod[0] + st_ref[ROWS_DMOD_C[0]:ROWS_DMOD_C[0] + 1, :]
        dmod[1] = dmod[1] + st_ref[ROWS_DMOD_C[1]:ROWS_DMOD_C[1] + 1, :]
        grads = dict(
            c_ctx=dsc_ref[0:1, :] * (sg * (1.0 + c * (1.0 - sg))),
            b_ada=jnp.concatenate(dmod, axis=1),
            attn_sink=sk_ref[0:1, 0:N_KV * GROUP],
            gmlp_ln_g=gm_ref[0:1, :], gmlp_ln_b=gm_ref[1:2, :],
            w_spatial=ws_ref[...], b_spatial=bs_ref[...],
            ln1_g=st_ref[ROW_LN1_G:ROW_LN1_G + 1, :], ln1_b=st_ref[ROW_LN1_B:ROW_LN1_B + 1, :],
            ln2_g=st_ref[ROW_LN2_G:ROW_LN2_G + 1, :], ln2_b=st_ref[ROW_LN2_B:ROW_LN2_B + 1, :])
        for i, name in enumerate(SMALL):
            g = grads[name]
            d, nm, nv = _adam_update(w_refs[name][...], g, m_refs[name][...], v_refs[name][...])
            outs[i][...] = g
            outs[n + i][...] = d
            outs[2 * n + i][...] = nm
            outs[3 * n + i][...] = nv

    vmem = pl.BlockSpec(memory_space=pltpu.VMEM)
    args = list(sums) + [dsc] + [w[k] for k in SMALL] + [m[k] for k in SMALL] + [v[k] for k in SMALL]
    shapes = [jax.ShapeDtypeStruct(w[k].shape, F32) for k in SMALL]
    out = pl.pallas_call(
        body, name="adamw_small", in_specs=[vmem] * len(args), out_specs=[vmem] * (4 * n), out_shape=shapes * 4,
        compiler_params=pltpu.CompilerParams(vmem_limit_bytes=VMEM_LIMIT),
    )(*args)
    return [dict(zip(SMALL, out[i * n:(i + 1) * n])) for i in range(4)]


def _adamw(w, g, m, v, name):
    r, c = w.shape
    tr = r if r * c <= 256 * 1024 else min(256, r)
    while r % tr:
        tr -= 8

    def body(w_ref, g_ref, m_ref, v_ref, d_ref, nm_ref, nv_ref):
        d_ref[...], nm_ref[...], nv_ref[...] = _adam_update(w_ref[...], g_ref[...], m_ref[...], v_ref[...])

    spec = pl.BlockSpec((tr, c), lambda i: (i, 0))
    shp = jax.ShapeDtypeStruct((r, c), F32)
    return pl.pallas_call(
        body, name=name, grid=(r // tr,), in_specs=[spec] * 4, out_specs=[spec] * 3, out_shape=[shp] * 3,
        compiler_params=_cp("arbitrary"),
    )(w, g, m, v)


def _my_pos():
    return lax.axis_index("x"), lax.axis_index("y"), lax.axis_index("c")


N_COPY = 7


class _Gather:
    def __init__(self, x_refs, out_refs, send_sems, recv_sems):
        self.x_refs, self.out_refs = x_refs, out_refs
        self.send_sems, self.recv_sems = send_sems, recv_sems
        x, y, c = _my_pos()
        self.c = c
        self.me, self.sibling = (x, y, c), (x, y, 1 - c)
        self.chips = [(1 - x, y), (x, 1 - y), (1 - x, 1 - y)]

    def _copy(self, a, k, block, to, from_input=False):
        px, py, pc = block
        rows = self.out_refs[a].at[4 * px + 2 * py + pc]
        return pltpu.make_async_remote_copy(
            src_ref=self.x_refs[a] if from_input else rows, dst_ref=rows,
            send_sem=self.send_sems.at[a * N_COPY + k], recv_sem=self.recv_sems.at[a * N_COPY + k],
            device_id=to, device_id_type=MESH)

    def start(self):
        n = len(self.x_refs)
        for a in range(n):
            self._copy(a, 0, self.me, self.sibling, from_input=True).start()
        for j, chip in enumerate(self.chips):
            for a in range(n):
                self._copy(a, 1 + j, self.me, (*chip, self.c), from_input=True).start()

    def finish(self):
        n = len(self.x_refs)
        c = self.c
        for j, chip in enumerate(self.chips):
            for a in range(n):
                self._copy(a, 1 + j, (*chip, c), self.me).wait_recv()
                self._copy(a, 4 + j, (*chip, c), self.sibling).start()
        for a in range(n):
            self._copy(a, 0, self.sibling, self.me).wait_recv()
        for j, chip in enumerate(self.chips):
            for a in range(n):
                self._copy(a, 4 + j, (*chip, 1 - c), self.me).wait_recv()
        for a in range(n):
            self._copy(a, 0, self.me, self.sibling, from_input=True).wait_send()
            for j, chip in enumerate(self.chips):
                self._copy(a, 1 + j, self.me, (*chip, c), from_input=True).wait_send()
                self._copy(a, 4 + j, (*chip, c), self.sibling).wait_send()


def _comm_scratch(n):
    return [pltpu.SemaphoreType.DMA((n * N_COPY,)), pltpu.SemaphoreType.DMA((n * N_COPY,))]


def _comm_specs(n):
    return [pl.BlockSpec(memory_space=pl.ANY)] * n


def _gathered_shapes(xs):
    return [jax.ShapeDtypeStruct((N_DEV,) + v.shape, v.dtype) for v in xs]


def _with_own(gathered, xs, me):
    return [lax.dynamic_update_index_in_dim(g, v, me, 0) for g, v in zip(gathered, xs)]


def _all_gather(xs, me, name):
    n = len(xs)

    def body(*refs):
        g = _Gather(refs[:n], refs[n:2 * n], *refs[2 * n:])
        g.start()
        g.finish()

    out = pl.pallas_call(
        body, name=name, out_shape=_gathered_shapes(xs), in_specs=_comm_specs(n), out_specs=_comm_specs(n),
        scratch_shapes=_comm_scratch(n),
    )(*xs)
    return _with_own(out, xs, me)


class _AllToAll:
    def __init__(self, x_refs, out_refs, send_sems, recv_sems):
        self.x_refs, self.out_refs = x_refs, out_refs
        self.send_sems, self.recv_sems = send_sems, recv_sems
        self.pos = _my_pos()
        x, y, c = self.pos
        self.me = 4 * x + 2 * y + c

    def _peer(self, k):
        x, y, c = self.pos
        return (x ^ ((k >> 2) & 1), y ^ ((k >> 1) & 1), c ^ (k & 1))

    def _copy(self, a, k):
        p = self._peer(k)
        return pltpu.make_async_remote_copy(
            src_ref=self.x_refs[a].at[4 * p[0] + 2 * p[1] + p[2]], dst_ref=self.out_refs[a].at[self.me],
            send_sem=self.send_sems.at[a * N_COPY + k - 1], recv_sem=self.recv_sems.at[a * N_COPY + k - 1],
            device_id=p, device_id_type=MESH)

    def start(self):
        for k in range(1, N_DEV):
            for a in range(len(self.x_refs)):
                self._copy(a, k).start()

    def finish(self):
        for a in range(len(self.x_refs)):
            for k in range(1, N_DEV):
                self._copy(a, k).wait_recv()
            for k in range(1, N_DEV):
                self._copy(a, k).wait_send()


def _all_to_all(blocks, name):
    n = len(blocks)

    def body(*refs):
        t = _AllToAll(refs[:n], refs[n:2 * n], *refs[2 * n:])
        t.start()
        t.finish()

    return pl.pallas_call(
        body, name=name, out_shape=[jax.ShapeDtypeStruct(v.shape, v.dtype) for v in blocks],
        in_specs=_comm_specs(n), out_specs=_comm_specs(n), scratch_shapes=_comm_scratch(n),
    )(*blocks)


def _sibling_exchange(xs, name):
    n = len(xs)

    def body(*refs):
        x_refs, out_refs = refs[:n], refs[n:2 * n]
        send_sems, recv_sems = refs[2 * n:]
        x, y, c = _my_pos()

        def push(a):
            return pltpu.make_async_remote_copy(
                src_ref=x_refs[a], dst_ref=out_refs[a], send_sem=send_sems.at[a], recv_sem=recv_sems.at[a],
                device_id=(x, y, 1 - c), device_id_type=MESH)

        for a in range(n):
            push(a).start()
        for a in range(n):
            push(a).wait_recv()
            push(a).wait_send()

    return pl.pallas_call(
        body, name=name, out_shape=[jax.ShapeDtypeStruct(v.shape, v.dtype) for v in xs],
        in_specs=_comm_specs(n), out_specs=_comm_specs(n),
        scratch_shapes=[pltpu.SemaphoreType.DMA((n,)), pltpu.SemaphoreType.DMA((n,))],
    )(*xs)


def _scatter_and_gather(scatter, gather, name):
    ns, ng = len(scatter), len(gather)

    def body(*refs):
        s_in, g_in = refs[:ns], refs[ns:ns + ng]
        s_out, g_out = refs[ns + ng:2 * ns + ng], refs[2 * ns + ng:2 * (ns + ng)]
        s_send, s_recv, g_send, g_recv = refs[2 * (ns + ng):]
        g = _Gather(g_in, g_out, g_send, g_recv)
        t = _AllToAll(s_in, s_out, s_send, s_recv)
        g.start()
        t.start()
        g.finish()
        t.finish()

    out = pl.pallas_call(
        body, name=name,
        out_shape=[jax.ShapeDtypeStruct(v.shape, v.dtype) for v in scatter] + _gathered_shapes(gather),
        in_specs=_comm_specs(ns + ng), out_specs=_comm_specs(ns + ng),
        scratch_shapes=_comm_scratch(ns) + _comm_scratch(ng),
    )(*scatter, *gather)
    return out[:ns], out[ns:]


def _row_tile(seq, want):
    return min(want, seq)


def _local_step(x, ctx, tgt, mod_x, mod_c, wb, sink, gmlp_g, gmlp_b, w_s, b_s, ln1_g, ln1_b, ln2_g, ln2_b,
                later=None, me=None):
    seq = x.shape[0]
    on_mesh = me is not None
    modx1 = jnp.concatenate([mod_x[0:2], jnp.zeros((6, D), F32)], axis=0)
    modc = jnp.concatenate([mod_c[0:2], jnp.zeros((6, D), F32)], axis=0)
    vec = jnp.concatenate([mod_x[2:3], ln1_g, ln1_b, mod_x[3:6], ln2_g, ln2_b], axis=0)
    gp = jnp.concatenate([gmlp_g, gmlp_b, jnp.zeros((6, G_W), F32)], axis=0)
    ws_stack = w_s.reshape(N_GRP * BLK, BLK).astype(BF16)
    ws_stack_t = jnp.transpose(w_s, (2, 0, 1)).reshape(BLK, N_GRP * BLK).astype(BF16)
    bias_full = jnp.repeat(b_s.T, GRP_D, axis=1)
    cos, sin = _rope_tables(seq)
    w_in = wb["w_in"]
    w_kv = w_in[:, Q_W:Q_W + 2 * KV_W]
    tm_big = _row_tile(seq, 512)
    tm_ffn = _row_tile(seq, 256)

    hc, kvc, vac = _ctx_fwd(ctx, modc, w_kv)
    behind_proj = ("w_a", "w_b", "w_o") if on_mesh else ()
    behind_attn = ("w_fi", "w_fo") if on_mesh else ()
    (h, q, kv, va, uv, gab), got_proj = _proj_fwd(x, modx1, w_in, cos, sin, tm_big, gather=[later[n] for n in behind_proj])
    (ya, lse), got_attn = _attn_fwd(q, kv, va, kvc, vac, sink, gather=[later[n] for n in behind_attn])
    if on_mesh:
        wb = dict(wb)
        names = behind_proj + behind_attn
        for n, g in zip(names, _with_own(list(got_proj) + list(got_attn), [later[n] for n in names], me)):
            wb[n] = g.reshape(-1, g.shape[2]) if n in ROW_SHARDED else g.reshape(N_SHARD, 2 * g.shape[1], g.shape[2])
    a, b, mix, merged, yb = _mix_fwd(uv, gab, ya, gp, ws_stack, bias_full, wb["w_a"], wb["w_b"], wb["w_o"], tm_big)
    gu, act, h2, dr2, st5 = _ffn_fwd(x, mix, tgt, vec, wb["w_fi"], wb["w_fo"], tm_ffn)

    dff, df, dr1, st5b = _ffn_bwd(dr2, gu, x, mix, vec, wb["w_fi"], wb["w_fo"], tm_ffn)
    g_w_fo = _tn_matmul(act, df, 512, "tn_w_ffn_out", BF16)
    g_w_fi = _tn_matmul(h2, dff, FH_SHARD, "tn_w_ffn_in", BF16, shard_major=True)
    dmix, da, db, dya, dpb, dws, dbs_full, st4 = _mix_bwd(
        dr1, a, b, gab, uv, vec, gp, ws_stack, ws_stack_t, bias_full, wb["w_a"], wb["w_b"], wb["w_o"], _row_tile(seq, 256))
    g_w_o = _tn_matmul(merged, dmix, 1024, "tn_w_out", BF16)
    g_w_a = _tn_matmul(ya, da, D // N_SHARD, "tn_w_branch_a", BF16, shard_major=True, tk=2048)
    g_w_b = _tn_matmul(yb, db, D // N_SHARD, "tn_w_branch_b", BF16, shard_major=True, tk=2048)
    blocks = dict(w_fi=_eighths(g_w_fi), w_fo=_eighths(g_w_fo), w_o=_eighths(g_w_o), w_a=_eighths(g_w_a), w_b=_eighths(g_w_b))
    early = tuple(blocks) if on_mesh else ()
    (dq, dkv, dkvc, dsink), recv_early = _attn_bwd(q, kv, kvc, sink, dya, ya, lse, scatter=[blocks[n] for n in early])
    g_wkv_ctx, st0 = _ctx_bwd(dkvc, ctx, hc, w_kv)
    (dqkv, grad_x, st1), _ = _proj_bwd(dq, dkv, dpb, x, dr1, modx1, w_in, cos, sin, tm_big)
    o = Q_W + 2 * KV_W
    init = jnp.pad(g_wkv_ctx, ((0, 0), (Q_W, 0)))
    g_w_in = jnp.concatenate([_tn_matmul(h, dqkv, o, "tn_w_in_qkv", F32, init=init),
                              _tn_matmul(h, dpb, 1536, "tn_w_in_rest", F32)], axis=1)
    g_w_in = g_w_in.reshape(D, N_SHARD, IN_W // N_SHARD).transpose(1, 0, 2).astype(BF16)

    dbs = jnp.sum(dbs_full.reshape(BLK, N_GRP, GRP_D), axis=2).T
    parts = [jnp.concatenate([st5, st5b, st1, st0], axis=0), st4, dsink, dws, dbs]
    blocks["w_in"] = _eighths(g_w_in)
    return grad_x, parts, blocks, dict(zip(early, recv_early))


BIG = ("w_in", "w_a", "w_b", "w_o", "w_fi", "w_fo")
ROW_SHARDED = ("w_o", "w_fo")


def _half_of_shard(shard, c):
    r = shard.shape[0]
    return lax.dynamic_slice_in_dim(shard, c * (r // 2), r // 2, axis=0)


def _eighths(v):
    rows = v.shape[-2] * (v.shape[0] if v.ndim == 3 else 1)
    return v.reshape(N_DEV, rows // N_DEV, v.shape[-1])


def kernel(x, c, ctx, c_ctx, w_ada, b_ada, w_in, attn_sink, gmlp_ln_g, gmlp_ln_b, w_spatial, b_spatial, w_branch_a, w_branch_b, w_out, ln1_g, ln1_b, w_ffn_in, w_ffn_out, ln2_g, ln2_b, loss_target, m_c_ctx, m_w_ada, m_b_ada, m_w_in, m_attn_sink, m_gmlp_ln_g, m_gmlp_ln_b, m_w_spatial, m_b_spatial, m_w_branch_a, m_w_branch_b, m_w_out, m_ln1_g, m_ln1_b, m_w_ffn_in, m_w_ffn_out, m_ln2_g, m_ln2_b, v_c_ctx, v_w_ada, v_b_ada, v_w_in, v_attn_sink, v_gmlp_ln_g, v_gmlp_ln_b, v_w_spatial, v_b_spatial, v_w_branch_a, v_w_branch_b, v_w_out, v_ln1_g, v_ln1_b, v_w_ffn_in, v_w_ffn_out, v_ln2_g, v_ln2_b):
    mx, my, mc = _my_pos()
    me = 4 * mx + 2 * my + mc
    chip = 2 * mx + my
    shards = dict(w_in=w_in[0], w_a=w_branch_a[0], w_b=w_branch_b[0], w_o=w_out[0], w_fi=w_ffn_in[0], w_fo=w_ffn_out[0])

    halves = {n: _half_of_shard(shards[n], mc).astype(BF16) for n in BIG}
    c_rows = jnp.concatenate([c, jnp.zeros((7, D), F32)], axis=0)
    g_in, c_g = _all_gather([halves["w_in"], c_rows], me, "gather_w_in")
    r, cdim = shards["w_in"].shape
    wb = dict(w_in=g_in.reshape(N_SHARD, r, cdim).transpose(1, 0, 2).reshape(r, N_SHARD * cdim))

    c_all = c_g[:, 0, :]
    cc = jnp.concatenate([c_all, c_ctx[None, :], jnp.zeros((7, D), F32)], axis=0)
    sig_cc = jax.nn.sigmoid(cc)
    sc_all = cc * sig_cc
    mod_shard = _ada_fwd(sc_all, w_ada[0])
    mod_g = _all_gather([mod_shard], me, "gather_mod")[0]
    mod_all = jnp.concatenate([mod_g[2 * s] for s in range(4)], axis=1) + b_ada
    mod_x = lax.dynamic_slice_in_dim(mod_all, me, 1, axis=0).reshape(6, D)
    mod_c = mod_all[8].reshape(6, D)[0:2]

    grad_x, parts, blocks, recv = _local_step(
        x[0], ctx[0], loss_target[0], mod_x, mod_c, wb, attn_sink, gmlp_ln_g, gmlp_ln_b, w_spatial[0], b_spatial[0],
        ln1_g, ln1_b, ln2_g, ln2_b, later=halves, me=me)

    (recv["w_in"],), gathered = _scatter_and_gather([blocks["w_in"]], parts, "scatter_w_in_gather_small")
    gathered = _with_own(gathered, parts, me)

    me_arr = jnp.reshape(me, (1,)).astype(jnp.int32)
    summed = {n: _sum_blocks(recv[n], blocks[n], me_arr, "sum_grads_" + n) for n in BIG}
    theirs = _sibling_exchange([summed[n] for n in BIG], "exchange_grads")
    g_shard = {}
    for n, other in zip(BIG, theirs):
        lo = jnp.where(mc == 0, summed[n], other)
        hi = jnp.where(mc == 0, other, summed[n])
        g_shard[n] = jnp.concatenate([lo, hi], axis=0)

    sums = _sum8_many(gathered, "sum_small")
    stats = sums[0]
    loss = 0.5 * jnp.sum(stats[ROW_LOSS]) / D
    dmod_x_all = jnp.concatenate([gathered[0][:, r_, :] for r_ in ROWS_DMOD_X], axis=1)
    dmod_c_full = jnp.concatenate([stats[r_] for r_ in ROWS_DMOD_C] + [jnp.zeros((4 * D,), F32)])
    dm_rows = jnp.concatenate([dmod_x_all, dmod_c_full[None, :], jnp.zeros((7, 6 * D), F32)], axis=0)
    cs = w_ada.shape[2]
    dm_shard = lax.dynamic_slice_in_dim(dm_rows, chip * cs, cs, axis=1)
    dmc_shard = jnp.concatenate([dm_shard[8:9], jnp.zeros((7, cs), F32)], axis=0)
    g_w_ada, part = _ada_bwd(sc_all.T, dm_shard, dmc_shard, w_ada[0])
    part_all = _all_gather([part * (mc == 0).astype(F32)], me, "gather_c_ctx")[0]
    dsc = _sum8(part_all, "sum_c_ctx")

    grads = dict(w_ada=g_w_ada[None], w_in=g_shard["w_in"][None], w_branch_a=g_shard["w_a"][None],
                 w_branch_b=g_shard["w_b"][None], w_out=g_shard["w_o"][None], w_ffn_in=g_shard["w_fi"][None],
                 w_ffn_out=g_shard["w_fo"][None])
    weights = dict(c_ctx=c_ctx, w_ada=w_ada, b_ada=b_ada, w_in=w_in, attn_sink=attn_sink, gmlp_ln_g=gmlp_ln_g,
                   gmlp_ln_b=gmlp_ln_b, w_spatial=w_spatial, b_spatial=b_spatial, w_branch_a=w_branch_a,
                   w_branch_b=w_branch_b, w_out=w_out, ln1_g=ln1_g, ln1_b=ln1_b, w_ffn_in=w_ffn_in, w_ffn_out=w_ffn_out,
                   ln2_g=ln2_g, ln2_b=ln2_b)
    ms = dict(c_ctx=m_c_ctx, w_ada=m_w_ada, b_ada=m_b_ada, w_in=m_w_in, attn_sink=m_attn_sink, gmlp_ln_g=m_gmlp_ln_g,
              gmlp_ln_b=m_gmlp_ln_b, w_spatial=m_w_spatial, b_spatial=m_b_spatial, w_branch_a=m_w_branch_a,
              w_branch_b=m_w_branch_b, w_out=m_w_out, ln1_g=m_ln1_g, ln1_b=m_ln1_b, w_ffn_in=m_w_ffn_in,
              w_ffn_out=m_w_ffn_out, ln2_g=m_ln2_g, ln2_b=m_ln2_b)
    vs = dict(c_ctx=v_c_ctx, w_ada=v_w_ada, b_ada=v_b_ada, w_in=v_w_in, attn_sink=v_attn_sink, gmlp_ln_g=v_gmlp_ln_g,
              gmlp_ln_b=v_gmlp_ln_b, w_spatial=v_w_spatial, b_spatial=v_b_spatial, w_branch_a=v_w_branch_a,
              w_branch_b=v_w_branch_b, w_out=v_w_out, ln1_g=v_ln1_g, ln1_b=v_ln1_b, w_ffn_in=v_w_ffn_in,
              w_ffn_out=v_w_ffn_out, ln2_g=v_ln2_g, ln2_b=v_ln2_b)
    order = list(weights)
    large = ("w_ada", "w_in", "w_branch_a", "w_branch_b", "w_out", "w_ffn_in", "w_ffn_out")
    delta, new_m, new_v = {}, {}, {}
    for n in large:
        d_, m_, v_ = _adamw(weights[n][0], grads[n][0], ms[n][0], vs[n][0], "adamw_" + n)
        delta[n], new_m[n], new_v[n] = d_[None], m_[None], v_[None]

    def view(a):
        return a.reshape(-1, a.shape[-1]) if a.ndim != 1 else a.reshape(1, -1)

    small = _adamw_small(sums, dsc, *[{n: view(d[n]) for n in SMALL} for d in (weights, ms, vs)])
    for out, src in zip((grads, delta, new_m, new_v), small):
        for n in SMALL:
            out[n] = src[n].reshape(weights[n].shape)

    return (loss, grad_x[None], *[grads[n] for n in order], *[delta[n] for n in order],
            *[new_m[n] for n in order], *[new_v[n] for n in order])
```

```python
import functools
import math

import jax
import jax.numpy as jnp
from jax import lax
from jax.experimental import pallas as pl
from jax.experimental.pallas import tpu as pltpu

F32 = jnp.float32
BF16 = jnp.bfloat16

D = 1024
HEAD = 64
N_KV = 2
GROUP = 4
Q_W = 512
KV_W = 128
G_W = 512
BLK = 128
N_GRP = 8
GRP_D = 64
FH = 2816
IN_W = 3840
GRID_W = 64
ROPE_BASE = 10000.0
LN_EPS = 1e-5
NEG = -1e30
ALPHA = (2 * 1) ** 0.25
SCALE = HEAD ** -0.5
GELU_K = math.sqrt(2.0 / math.pi)
GELU_A = 0.044715
ADAM_LR = 0.001
ADAM_B1 = 0.9
ADAM_B2 = 0.999
ADAM_EPS = 1e-08
ADAM_WD = 0.01
ADAM_STEP = 10
N_DEV = 8
N_SHARD = 4
FH_SHARD = FH // 2
LANES = 128
VMEM_LIMIT = 56 * 1024 * 1024
MESH = pl.DeviceIdType.MESH


def _cp(*sem):
    return pltpu.CompilerParams(dimension_semantics=sem, vmem_limit_bytes=VMEM_LIMIT)


def _resident(shape):
    return pl.BlockSpec(shape, lambda *_: (0,) * len(shape), pipeline_mode=pl.Buffered(1))


def _rows(tm, width):
    return pl.BlockSpec((tm, width), lambda i: (i, 0))


def _acc(shape):
    return pl.BlockSpec(shape, lambda *_: (0,) * len(shape))


def _dot(a, b):
    return jnp.dot(a, b, preferred_element_type=F32)


def _dot_nt(a, b):
    return lax.dot_general(a, b, (((1,), (1,)), ((), ())), preferred_element_type=F32)


def _dot_tn(a, b):
    return lax.dot_general(a, b, (((0,), (0,)), ((), ())), preferred_element_type=F32)


def _ln(x):
    mu = jnp.mean(x, axis=-1, keepdims=True)
    xc = x - mu
    var = jnp.mean(xc * xc, axis=-1, keepdims=True)
    rstd = lax.rsqrt(var + LN_EPS)
    return xc * rstd, rstd


def _ln_bwd(dxhat, xhat, rstd):
    return (dxhat - jnp.mean(dxhat, axis=-1, keepdims=True)
            - xhat * jnp.mean(dxhat * xhat, axis=-1, keepdims=True)) * rstd


def _sig(x):
    return 1.0 / (1.0 + jnp.exp(-x))


def _gelu(x):
    t = jnp.tanh(GELU_K * (x + GELU_A * x * x * x))
    return 0.5 * x * (1.0 + t), t


def _gelu_grad(x, t):
    return 0.5 * (1.0 + t) + 0.5 * x * (1.0 - t * t) * GELU_K * (1.0 + 3.0 * GELU_A * x * x)


def _colsum(v):
    return jnp.sum(v, axis=0, keepdims=True)


def _partner(x):
    w = x.shape[1]
    lane = lax.broadcasted_iota(jnp.int32, x.shape, 1)
    return jnp.where((lane & 31) < 16, pltpu.roll(x, w - 16, 1), pltpu.roll(x, 16, 1))


def _rope(x, cos, sin):
    return x * cos + _partner(x) * sin


def _unrope(g, cos, sin):
    return g * cos + _partner(g * sin)


def _rope_tables(seq):
    inv = ROPE_BASE ** (-jnp.arange(HEAD // 4, dtype=F32) / (HEAD // 4))
    pos = jnp.arange(seq, dtype=jnp.int32)
    ar = (pos // GRID_W).astype(F32)[:, None] * inv
    ac = (pos % GRID_W).astype(F32)[:, None] * inv
    cos = jnp.concatenate([jnp.cos(ar), jnp.cos(ar), jnp.cos(ac), jnp.cos(ac)], axis=-1)
    sin = jnp.concatenate([-jnp.sin(ar), jnp.sin(ar), -jnp.sin(ac), jnp.sin(ac)], axis=-1)
    return jnp.tile(cos, (1, LANES // HEAD)), jnp.tile(sin, (1, LANES // HEAD))


def _ctx_fwd(ctx, modc, w_kv):
    n_ctx = ctx.shape[0]

    def body(ctx_ref, mod_ref, w_ref, hc_ref, kvc_ref, vac_ref):
        xhat, _ = _ln(ctx_ref[...])
        hc = (xhat * (1.0 + mod_ref[1:2, :]) + mod_ref[0:1, :]).astype(BF16)
        hc_ref[...] = hc
        kvc = _dot(hc, w_ref[...]).astype(BF16)
        kvc_ref[...] = kvc
        vac_ref[...] = _with_ones(kvc[:, KV_W:])

    return pl.pallas_call(
        body, name="ctx_fwd", grid=(1,),
        in_specs=[_acc((n_ctx, D)), _acc((8, D)), _acc((D, 2 * KV_W))],
        out_specs=[_acc((n_ctx, D)), _acc((n_ctx, 2 * KV_W)), _acc((n_ctx, 2 * LANES))],
        out_shape=[jax.ShapeDtypeStruct((n_ctx, D), BF16), jax.ShapeDtypeStruct((n_ctx, 2 * KV_W), BF16),
                   jax.ShapeDtypeStruct((n_ctx, 2 * LANES), BF16)],
        compiler_params=_cp("arbitrary"),
    )(ctx, modc, w_kv)


def _host_start(step, comm):
    if comm is not None:
        @pl.when(step == 0)
        def _():
            comm.start()


def _host_finish(step, last, comm):
    if comm is not None:
        @pl.when(step == last)
        def _():
            comm.finish()


def _proj_fwd(x, modx, w_in, cos, sin, tm, gather=()):
    seq = x.shape[0]
    ng = len(gather)

    def body(x_ref, mod_ref, w_ref, cos_ref, sin_ref, *rest):
        h_ref, q_ref, kv_ref, va_ref, uv_ref, gab_ref = rest[ng:ng + 6]
        comm = _Gather(rest[:ng], rest[ng + 6:2 * ng + 6], *rest[2 * ng + 6:]) if ng else None
        _host_start(pl.program_id(0), comm)
        xhat, _ = _ln(x_ref[...])
        h = (xhat * (1.0 + mod_ref[1:2, :]) + mod_ref[0:1, :]).astype(BF16)
        h_ref[...] = h
        cos1, sin1 = cos_ref[...], sin_ref[...]
        cos2 = jnp.concatenate([cos1, cos1], axis=1)
        sin2 = jnp.concatenate([sin1, sin1], axis=1)
        for j in range(Q_W // 256):
            t = _dot(h, w_ref[:, 256 * j:256 * (j + 1)])
            q_ref[:, 256 * j:256 * (j + 1)] = (_rope(t, cos2, sin2) * SCALE).astype(BF16)
        t = _dot(h, w_ref[:, Q_W:Q_W + 2 * KV_W])
        kv_ref[:, :KV_W] = _rope(t[:, :KV_W], cos1, sin1).astype(BF16)
        v = t[:, KV_W:].astype(BF16)
        kv_ref[:, KV_W:] = v
        va_ref[...] = _with_ones(v)
        o = Q_W + 2 * KV_W
        for j in range(2):
            uv_ref[:, G_W * j:G_W * (j + 1)] = _dot(h, w_ref[:, o + G_W * j:o + G_W * (j + 1)])
        o += 2 * G_W
        for j in range(4):
            gab_ref[:, 512 * j:512 * (j + 1)] = _dot(h, w_ref[:, o + 512 * j:o + 512 * (j + 1)]).astype(BF16)
        _host_finish(pl.program_id(0), seq // tm - 1, comm)

    out = pl.pallas_call(
        body, name="proj_fwd", grid=(seq // tm,),
        in_specs=[_rows(tm, D), _acc((8, D)), _resident((D, IN_W)), _rows(tm, LANES), _rows(tm, LANES)] + _comm_specs(ng),
        out_specs=[_rows(tm, D), _rows(tm, Q_W), _rows(tm, 2 * KV_W), _rows(tm, 2 * LANES), _rows(tm, 2 * G_W),
                   _rows(tm, 2 * D)] + _comm_specs(ng),
        out_shape=[jax.ShapeDtypeStruct((seq, D), BF16), jax.ShapeDtypeStruct((seq, Q_W), BF16),
                   jax.ShapeDtypeStruct((seq, 2 * KV_W), BF16), jax.ShapeDtypeStruct((seq, 2 * LANES), BF16),
                   jax.ShapeDtypeStruct((seq, 2 * G_W), F32), jax.ShapeDtypeStruct((seq, 2 * D), BF16)] + _gathered_shapes(gather),
        scratch_shapes=_comm_scratch(ng) if ng else [],
        compiler_params=_cp("arbitrary"),
    )(x, modx, w_in, cos, sin, *gather)
    return out[:6], out[6:]


def _stack_heads(x, hk):
    return jnp.concatenate([x[:, (hk * GROUP + g) * HEAD:(hk * GROUP + g + 1) * HEAD] for g in range(GROUP)], axis=0)


def _attn_scores(q_ref, k_refs, hk, n, nb):
    q4 = _stack_heads(q_ref[...], hk)
    ks = [r[:, hk * HEAD:(hk + 1) * HEAD] for r in k_refs]
    rows = GROUP * BLK
    qi = lax.broadcasted_iota(jnp.int32, (rows, BLK), 0) & (BLK - 1)
    kj = lax.broadcasted_iota(jnp.int32, (rows, BLK), 1)
    s = [_dot_nt(q4, k) for k in ks]
    s[1] = jnp.where((kj >= qi) & (n > 0), s[1], NEG)
    s[3] = jnp.where((kj <= qi) & (n < nb - 1), s[3], NEG)
    return q4, ks, s


def _sink_rows(sink_ref, hk):
    rows = GROUP * BLK
    rg = lax.broadcasted_iota(jnp.int32, (rows, 1), 0) >> 7
    sink_v = jnp.full((rows, 1), sink_ref[0, hk * GROUP], F32)
    for g in range(1, GROUP):
        sink_v = jnp.where(rg == g, sink_ref[0, hk * GROUP + g], sink_v)
    return sink_v


def _with_ones(v):
    ones = jnp.ones((v.shape[0], HEAD), v.dtype)
    return jnp.concatenate([v[:, :HEAD], ones, v[:, HEAD:], ones], axis=1)


def _kv_specs(nb):
    return [pl.BlockSpec((BLK, 2 * KV_W), lambda n: (jnp.maximum(n - 1, 0), 0)),
            pl.BlockSpec((BLK, 2 * KV_W), lambda n: (n, 0)),
            pl.BlockSpec((BLK, 2 * KV_W), lambda n: (jnp.minimum(n + 1, nb - 1), 0))]


def _attn_fwd(q, kv, va, kvc, vac, sink, gather=()):
    seq = q.shape[0]
    nb = seq // BLK
    n_ctx = kvc.shape[0]
    ng = len(gather)

    def body(q_ref, kvp_ref, kvm_ref, kvn_ref, vap_ref, vam_ref, van_ref, kvc_ref, vac_ref, sink_ref, *rest):
        o_ref, lse_ref = rest[ng:ng + 2]
        comm = _Gather(rest[:ng], rest[ng + 2:2 * ng + 2], *rest[2 * ng + 2:]) if ng else None
        n = pl.program_id(0)
        _host_start(n, comm)
        outs = []
        lane = lax.broadcasted_iota(jnp.int32, (BLK, LANES), 1)
        lse_all = jnp.zeros((BLK, LANES), F32)
        for hk in range(N_KV):
            _, _, s = _attn_scores(q_ref, (kvc_ref, kvp_ref, kvm_ref, kvn_ref), hk, n, nb)
            sink_v = _sink_rows(sink_ref, hk)
            m = sink_v
            for t in s:
                m = jnp.maximum(m, jnp.max(t, axis=-1, keepdims=True))
            o = jnp.zeros((GROUP * BLK, LANES), F32)
            for t, va_ref in zip(s, (vac_ref, vap_ref, vam_ref, van_ref)):
                o = o + _dot(jnp.exp((t - m).astype(BF16)), va_ref[:, hk * LANES:(hk + 1) * LANES])
            denom = o[:, HEAD:HEAD + 1] + jnp.exp(sink_v - m)
            o4 = o[:, :HEAD] * (1.0 / denom)
            lse4 = m + jnp.log(denom)
            for g in range(GROUP):
                outs.append(o4[g * BLK:(g + 1) * BLK, :])
                lse_all = jnp.where(lane == hk * GROUP + g, lse4[g * BLK:(g + 1) * BLK, :], lse_all)
        o_ref[...] = jnp.concatenate(outs, axis=1).astype(BF16)
        lse_ref[...] = lse_all
        _host_finish(n, nb - 1, comm)

    out = pl.pallas_call(
        body, name="attn_fwd", grid=(nb,),
        in_specs=[_rows(BLK, Q_W)] + _kv_specs(nb) + _kv_specs(nb)
        + [_acc((n_ctx, 2 * KV_W)), _acc((n_ctx, 2 * LANES)), pl.BlockSpec(memory_space=pltpu.SMEM)] + _comm_specs(ng),
        out_specs=[_rows(BLK, Q_W), _rows(BLK, LANES)] + _comm_specs(ng),
        out_shape=[jax.ShapeDtypeStruct((seq, Q_W), BF16), jax.ShapeDtypeStruct((seq, LANES), F32)] + _gathered_shapes(gather),
        scratch_shapes=_comm_scratch(ng) if ng else [],
        compiler_params=_cp("arbitrary"),
    )(q, kv, kv, kv, va, va, va, kvc, vac, sink, *gather)
    return out[:2], out[2:]


def _gmlp_chunk(u, vb, gp_ref, ws_ref, bias_ref):
    gu, tu = _gelu(u)
    gv, tv = _gelu(vb)
    vhat, rstd = _ln(gv)
    vn = (vhat * gp_ref[0:1, :] + gp_ref[1:2, :]).astype(BF16)
    s = bias_ref[...] + jnp.concatenate(
        [_dot(ws_ref[g * BLK:(g + 1) * BLK, :], vn[:, g * GRP_D:(g + 1) * GRP_D]) for g in range(N_GRP)], axis=1)
    return gu, tu, tv, vhat, rstd, vn, s


def _mix_fwd(uv, gab, ya, gp, ws_stack, bias_full, w_a, w_b, w_o, tm):
    seq = uv.shape[0]

    def body(uv_ref, gab_ref, ya_ref, gp_ref, ws_ref, bias_ref, wa_ref, wb_ref, wo_ref,
             a_ref, b_ref, mix_ref, merged_ref, yb_ref):
        for c in range(tm // BLK):
            rs = slice(c * BLK, (c + 1) * BLK)
            gu, _, _, _, _, _, s = _gmlp_chunk(uv_ref[rs, :G_W], uv_ref[rs, G_W:], gp_ref, ws_ref, bias_ref)
            yb_ref[rs, :] = (gu * s).astype(BF16)
        ya = ya_ref[...]
        yb = yb_ref[...]
        for s in range(N_SHARD):
            cs = slice(s * (D // N_SHARD), (s + 1) * (D // N_SHARD))
            a = _dot(ya, wa_ref[s])
            b = _dot(yb, wb_ref[s])
            a_ref[:, cs] = a.astype(BF16)
            b_ref[:, cs] = b.astype(BF16)
            ga = gab_ref[:, cs].astype(F32)
            gb = gab_ref[:, D + s * (D // N_SHARD):D + (s + 1) * (D // N_SHARD)].astype(F32)
            merged_ref[:, cs] = (_sig(ga) * a + _sig(gb) * b).astype(BF16)
        mix_ref[...] = _dot(merged_ref[...], wo_ref[...])

    return pl.pallas_call(
        body, name="mix_fwd", grid=(seq // tm,),
        in_specs=[_rows(tm, 2 * G_W), _rows(tm, 2 * D), _rows(tm, Q_W), _acc((8, G_W)),
                  _resident((N_GRP * BLK, BLK)), _acc((BLK, G_W)),
                  _resident((N_SHARD, Q_W, D // N_SHARD)), _resident((N_SHARD, G_W, D // N_SHARD)), _resident((D, D))],
        out_specs=[_rows(tm, D), _rows(tm, D), _rows(tm, D), _rows(tm, D), _rows(tm, G_W)],
        out_shape=[jax.ShapeDtypeStruct((seq, D), BF16), jax.ShapeDtypeStruct((seq, D), BF16),
                   jax.ShapeDtypeStruct((seq, D), F32), jax.ShapeDtypeStruct((seq, D), BF16),
                   jax.ShapeDtypeStruct((seq, G_W), BF16)],
        compiler_params=_cp("arbitrary"),
    )(uv, gab, ya, gp, ws_stack, bias_full, w_a, w_b, w_o)


FFN_CHUNK = 512


def _ffn_chunks():
    out = []
    for hh in range(2):
        off = 0
        while off < FH_SHARD:
            w = min(FFN_CHUNK, FH_SHARD - off)
            out.append((hh, off, w))
            off += w
    return out


def _mid_recompute(x_ref, mix_ref, vec_ref):
    r1 = ALPHA * x_ref[...] + vec_ref[0:1, :] * mix_ref[...]
    xh1, rstd1 = _ln(r1)
    xmid = xh1 * vec_ref[1:2, :] + vec_ref[2:3, :]
    xh2, rstd2 = _ln(xmid)
    return xh1, rstd1, xmid, xh2, rstd2


def _ffn(x, mix, tgt, vec, w_fi, w_fo, tm):
    seq = x.shape[0]

    def body(x_ref, mix_ref, tgt_ref, vec_ref, wi_ref, wo_ref, act_ref, h2_ref, dff_ref, df_ref, dr1_ref, st_ref, gu_ref):
        @pl.when(pl.program_id(0) == 0)
        def _():
            st_ref[...] = jnp.zeros_like(st_ref)

        xh1, rstd1, xmid, xh2, rstd2 = _mid_recompute(x_ref, mix_ref, vec_ref)
        h2 = (xh2 * (1.0 + vec_ref[4:5, :]) + vec_ref[3:4, :]).astype(BF16)
        h2_ref[...] = h2
        f = jnp.zeros((tm, D), F32)
        for hh, off, w in _ffn_chunks():
            cs = slice(hh * FH_SHARD + off, hh * FH_SHARD + off + w)
            cu = slice(FH + hh * FH_SHARD + off, FH + hh * FH_SHARD + off + w)
            g = _dot(h2, wi_ref[hh, :, off:off + w])
            u = _dot(h2, wi_ref[2 + hh, :, off:off + w])
            gu_ref[:, cs] = g
            gu_ref[:, cu] = u
            a = (g * _sig(g) * u).astype(BF16)
            act_ref[:, cs] = a
            f = f + _dot(a, wo_ref[cs, :])
        r2 = ALPHA * xmid + vec_ref[5:6, :] * f
        yh, rstd = _ln(r2)
        y = yh * vec_ref[6:7, :] + vec_ref[7:8, :]
        err = y - tgt_ref[...]
        dy = err / D
        dr2 = _ln_bwd(dy * vec_ref[6:7, :], yh, rstd)
        st_ref[0:1, :] += _colsum(err * err)
        st_ref[1:2, :] += _colsum(dy * yh)
        st_ref[2:3, :] += _colsum(dy)
        st_ref[3:4, :] += _colsum(dr2 * f)

        df = (dr2 * vec_ref[5:6, :]).astype(BF16)
        df_ref[...] = df
        dh2 = jnp.zeros((tm, D), F32)
        for hh, off, w in _ffn_chunks():
            cs = slice(hh * FH_SHARD + off, hh * FH_SHARD + off + w)
            cu = slice(FH + hh * FH_SHARD + off, FH + hh * FH_SHARD + off + w)
            da = _dot_nt(df, wo_ref[cs, :])
            g = gu_ref[:, cs]
            u = gu_ref[:, cu]
            sg = _sig(g)
            dg = (da * u * sg * (1.0 + g * (1.0 - sg))).astype(BF16)
            du = (da * g * sg).astype(BF16)
            dff_ref[:, cs] = dg
            dff_ref[:, cu] = du
            dh2 = dh2 + _dot_nt(dg, wi_ref[hh, :, off:off + w]) + _dot_nt(du, wi_ref[2 + hh, :, off:off + w])
        dxmid = _ln_bwd(dh2 * (1.0 + vec_ref[4:5, :]), xh2, rstd2) + ALPHA * dr2
        dr1 = _ln_bwd(dxmid * vec_ref[1:2, :], xh1, rstd1)
        dr1_ref[...] = dr1
        st_ref[8:9, :] += _colsum(dh2 * xh2)
        st_ref[9:10, :] += _colsum(dh2)
        st_ref[10:11, :] += _colsum(dxmid * xh1)
        st_ref[11:12, :] += _colsum(dxmid)
        st_ref[12:13, :] += _colsum(dr1 * mix_ref[...])

    return pl.pallas_call(
        body, name="ffn", grid=(seq // tm,),
        in_specs=[_rows(tm, D), _rows(tm, D), _rows(tm, D), _acc((8, D)), _resident((N_SHARD, D, FH_SHARD)), _resident((FH, D))],
        out_specs=[_rows(tm, FH), _rows(tm, D), _rows(tm, 2 * FH), _rows(tm, D), _rows(tm, D), _acc((16, D))],
        out_shape=[jax.ShapeDtypeStruct((seq, FH), BF16), jax.ShapeDtypeStruct((seq, D), BF16),
                   jax.ShapeDtypeStruct((seq, 2 * FH), BF16), jax.ShapeDtypeStruct((seq, D), BF16),
                   jax.ShapeDtypeStruct((seq, D), F32), jax.ShapeDtypeStruct((16, D), F32)],
        scratch_shapes=[pltpu.VMEM((tm, 2 * FH), F32)],
        compiler_params=_cp("arbitrary"),
    )(x, mix, tgt, vec, w_fi, w_fo)


def _mix_bwd(dr1, a, b, gab, uv, vec, gp, ws_stack, ws_stack_t, bias_full, w_a, w_b, w_o, tm):
    seq = dr1.shape[0]

    def body(dr1_ref, a_ref, b_ref, gab_ref, uv_ref, vec_ref, gp_ref, ws_ref, wst_ref, bias_ref, wa_ref, wb_ref, wo_ref,
             dmix_ref, da_ref, db_ref, dya_ref, dp_ref, dws_ref, dbs_ref, st_ref):
        @pl.when(pl.program_id(0) == 0)
        def _():
            dws_ref[...] = jnp.zeros_like(dws_ref)
            dbs_ref[...] = jnp.zeros_like(dbs_ref)
            st_ref[...] = jnp.zeros_like(st_ref)

        dmix = (dr1_ref[...] * vec_ref[0:1, :]).astype(BF16)
        dmix_ref[...] = dmix
        dmerged = _dot_nt(dmix, wo_ref[...])
        sa = _sig(gab_ref[:, :D].astype(F32))
        sb = _sig(gab_ref[:, D:].astype(F32))
        da = (dmerged * sa).astype(BF16)
        db = (dmerged * sb).astype(BF16)
        da_ref[...] = da
        db_ref[...] = db
        dp_ref[:, 2 * G_W:2 * G_W + D] = (dmerged * a_ref[...].astype(F32) * sa * (1.0 - sa)).astype(BF16)
        dp_ref[:, 2 * G_W + D:] = (dmerged * b_ref[...].astype(F32) * sb * (1.0 - sb)).astype(BF16)
        dya = jnp.zeros((tm, Q_W), F32)
        dyb = jnp.zeros((tm, G_W), F32)
        for s in range(N_SHARD):
            cs = slice(s * (D // N_SHARD), (s + 1) * (D // N_SHARD))
            dya = dya + _dot_nt(da[:, cs], wa_ref[s])
            dyb = dyb + _dot_nt(db[:, cs], wb_ref[s])
        dya_ref[...] = dya.astype(BF16)
        for c in range(tm // BLK):
            rs = slice(c * BLK, (c + 1) * BLK)
            u = uv_ref[rs, :G_W]
            vb = uv_ref[rs, G_W:]
            gu, tu, tv, vhat, rstd, vn, s = _gmlp_chunk(u, vb, gp_ref, ws_ref, bias_ref)
            dyb_c = dyb[rs, :]
            ds = dyb_c * gu
            du = dyb_c * s * _gelu_grad(u, tu)
            ds_b = ds.astype(BF16)
            dvn_g = []
            for g in range(N_GRP):
                cg = slice(g * GRP_D, (g + 1) * GRP_D)
                dvn_g.append(_dot(wst_ref[:, g * BLK:(g + 1) * BLK], ds_b[:, cg]))
                dws_ref[g * BLK:(g + 1) * BLK, :] += _dot_nt(ds_b[:, cg], vn[:, cg])
            dvn = jnp.concatenate(dvn_g, axis=1)
            dbs_ref[...] += ds
            st_ref[0:1, :] += _colsum(dvn * vhat)
            st_ref[1:2, :] += _colsum(dvn)
            dgv = _ln_bwd(dvn * gp_ref[0:1, :], vhat, rstd)
            dvb = dgv * _gelu_grad(vb, tv)
            dp_ref[rs, :G_W] = du.astype(BF16)
            dp_ref[rs, G_W:2 * G_W] = dvb.astype(BF16)

    pw = 2 * G_W + 2 * D
    return pl.pallas_call(
        body, name="mix_bwd", grid=(seq // tm,),
        in_specs=[_rows(tm, D), _rows(tm, D), _rows(tm, D), _rows(tm, 2 * D), _rows(tm, 2 * G_W), _acc((8, D)), _acc((8, G_W)),
                  _resident((N_GRP * BLK, BLK)), _resident((BLK, N_GRP * BLK)), _acc((BLK, G_W)),
                  _resident((N_SHARD, Q_W, D // N_SHARD)), _resident((N_SHARD, G_W, D // N_SHARD)), _resident((D, D))],
        out_specs=[_rows(tm, D), _rows(tm, D), _rows(tm, D), _rows(tm, Q_W), _rows(tm, pw),
                   _acc((N_GRP * BLK, BLK)), _acc((BLK, G_W)), _acc((8, G_W))],
        out_shape=[jax.ShapeDtypeStruct((seq, D), BF16), jax.ShapeDtypeStruct((seq, D), BF16),
                   jax.ShapeDtypeStruct((seq, D), BF16), jax.ShapeDtypeStruct((seq, Q_W), BF16),
                   jax.ShapeDtypeStruct((seq, pw), BF16), jax.ShapeDtypeStruct((N_GRP * BLK, BLK), F32),
                   jax.ShapeDtypeStruct((BLK, G_W), F32), jax.ShapeDtypeStruct((8, G_W), F32)],
        compiler_params=_cp("arbitrary"),
    )(dr1, a, b, gab, uv, vec, gp, ws_stack, ws_stack_t, bias_full, w_a, w_b, w_o)


def _attn_bwd(q, kv, kvc, sink, dya, ya, lse, scatter=()):
    seq = q.shape[0]
    nb = seq // BLK
    n_ctx = kvc.shape[0]
    ns = len(scatter)

    def body(q_ref, kvp_ref, kvm_ref, kvn_ref, kvc_ref, sink_ref, do_ref, o_ref, lse_ref, *rest):
        dq_ref, dkv_ref, dkvc_ref, dsink_ref = rest[ns:ns + 4]
        comm = _AllToAll(rest[:ns], rest[ns + 4:2 * ns + 4], *rest[2 * ns + 4:]) if ns else None
        n = pl.program_id(0)
        _host_start(n, comm)

        @pl.when(n == 0)
        def _():
            dkv_ref[...] = jnp.zeros_like(dkv_ref)
            dkvc_ref[...] = jnp.zeros_like(dkvc_ref)
            dsink_ref[...] = jnp.zeros_like(dsink_ref)

        do = do_ref[...]
        out = o_ref[...]
        lse_all = lse_ref[...]
        k_refs = (kvc_ref, kvp_ref, kvm_ref, kvn_ref)
        dqs, dks, dvs = [], [], []
        for hk in range(N_KV):
            q4, ks, s = _attn_scores(q_ref, k_refs, hk, n, nb)
            vs = [r[:, KV_W + hk * HEAD:KV_W + (hk + 1) * HEAD] for r in k_refs]
            lse4 = jnp.concatenate([lse_all[:, hk * GROUP + g:hk * GROUP + g + 1] for g in range(GROUP)], axis=0)
            do4 = _stack_heads(do, hk)
            delta = jnp.sum(do4.astype(F32) * _stack_heads(out, hk).astype(F32), axis=-1, keepdims=True)
            p = [jnp.exp((t - lse4).astype(BF16)) for t in s]
            ds = [t * (_dot_nt(do4, v) - delta).astype(BF16) for t, v in zip(p, vs)]
            dq4 = _dot(ds[0], ks[0])
            for t, k in zip(ds[1:], ks[1:]):
                dq4 = dq4 + _dot(t, k)
            dq4 = dq4 * SCALE
            dqs += [dq4[g * BLK:(g + 1) * BLK, :] for g in range(GROUP)]
            dks.append([_dot_tn(t, q4) for t in ds])
            dvs.append([_dot_tn(t, do4) for t in p])
            ps = jnp.exp(_sink_rows(sink_ref, hk) - lse4) * delta
            lane = lax.broadcasted_iota(jnp.int32, (1, LANES), 1)
            for g in range(GROUP):
                part = -jnp.sum(ps[g * BLK:(g + 1) * BLK, :], axis=0, keepdims=True)
                dsink_ref[0:1, :] += jnp.where(lane == hk * GROUP + g, part, 0.0)
        dq_ref[...] = jnp.concatenate(dqs, axis=1)

        def piece(i):
            return jnp.concatenate([dks[0][i], dks[1][i], dvs[0][i], dvs[1][i]], axis=1)

        dkvc_ref[...] += piece(0)
        starts = (jnp.maximum(n - 1, 0), n, jnp.minimum(n + 1, nb - 1))
        for i, st in enumerate(starts):
            r = pl.ds(pl.multiple_of(st * BLK, BLK), BLK)
            dkv_ref[r, :] += piece(i + 1)
        _host_finish(n, nb - 1, comm)

    out = pl.pallas_call(
        body, name="attn_bwd", grid=(nb,),
        in_specs=[_rows(BLK, Q_W)] + _kv_specs(nb) + [_acc((n_ctx, 2 * KV_W)), pl.BlockSpec(memory_space=pltpu.SMEM),
                                                      _rows(BLK, Q_W), _rows(BLK, Q_W), _rows(BLK, LANES)] + _comm_specs(ns),
        out_specs=[_rows(BLK, Q_W), _acc((seq, 2 * KV_W)), _acc((n_ctx, 2 * KV_W)), _acc((8, LANES))] + _comm_specs(ns),
        out_shape=[jax.ShapeDtypeStruct((seq, Q_W), F32), jax.ShapeDtypeStruct((seq, 2 * KV_W), F32),
                   jax.ShapeDtypeStruct((n_ctx, 2 * KV_W), F32), jax.ShapeDtypeStruct((8, LANES), F32)]
        + [jax.ShapeDtypeStruct(v.shape, v.dtype) for v in scatter],
        scratch_shapes=_comm_scratch(ns) if ns else [],
        compiler_params=_cp("arbitrary"),
    )(q, kv, kv, kv, kvc, sink, dya, ya, lse, *scatter)
    return out[:4], out[4:]


def _proj_bwd(dq, dkv, dpb, x, dr1, modx, w_in, cos, sin, tm, scatter=()):
    seq = x.shape[0]
    pw = IN_W - Q_W - 2 * KV_W
    ns = len(scatter)

    def body(dq_ref, dkv_ref, dpb_ref, x_ref, dr1_ref, mod_ref, w_ref, cos_ref, sin_ref, *rest):
        dqkv_ref, gx_ref, st_ref = rest[ns:ns + 3]
        comm = _AllToAll(rest[:ns], rest[ns + 3:2 * ns + 3], *rest[2 * ns + 3:]) if ns else None
        _host_start(pl.program_id(0), comm)

        @pl.when(pl.program_id(0) == 0)
        def _():
            st_ref[...] = jnp.zeros_like(st_ref)

        cos1, sin1 = cos_ref[...], sin_ref[...]
        cos2 = jnp.concatenate([cos1, cos1], axis=1)
        sin2 = jnp.concatenate([sin1, sin1], axis=1)
        for j in range(Q_W // 256):
            cs = slice(256 * j, 256 * (j + 1))
            dqkv_ref[:, cs] = _unrope(dq_ref[:, cs], cos2, sin2).astype(BF16)
        dqkv_ref[:, Q_W:Q_W + KV_W] = _unrope(dkv_ref[:, :KV_W], cos1, sin1).astype(BF16)
        dqkv_ref[:, Q_W + KV_W:] = dkv_ref[:, KV_W:].astype(BF16)
        o = Q_W + 2 * KV_W
        dh = _dot_nt(dqkv_ref[...], w_ref[:, :o]) + _dot_nt(dpb_ref[...], w_ref[:, o:])
        xhat, rstd = _ln(x_ref[...])
        st_ref[0:1, :] += _colsum(dh)
        st_ref[1:2, :] += _colsum(dh * xhat)
        gx_ref[...] = _ln_bwd(dh * (1.0 + mod_ref[1:2, :]), xhat, rstd) + ALPHA * dr1_ref[...]
        _host_finish(pl.program_id(0), seq // tm - 1, comm)

    out = pl.pallas_call(
        body, name="proj_bwd", grid=(seq // tm,),
        in_specs=[_rows(tm, Q_W), _rows(tm, 2 * KV_W), _rows(tm, pw), _rows(tm, D), _rows(tm, D), _acc((8, D)),
                  _resident((D, IN_W)), _rows(tm, LANES), _rows(tm, LANES)] + _comm_specs(ns),
        out_specs=[_rows(tm, Q_W + 2 * KV_W), _rows(tm, D), _acc((8, D))] + _comm_specs(ns),
        out_shape=[jax.ShapeDtypeStruct((seq, Q_W + 2 * KV_W), BF16), jax.ShapeDtypeStruct((seq, D), F32),
                   jax.ShapeDtypeStruct((8, D), F32)] + [jax.ShapeDtypeStruct(v.shape, v.dtype) for v in scatter],
        scratch_shapes=_comm_scratch(ns) if ns else [],
        compiler_params=_cp("arbitrary"),
    )(dq, dkv, dpb, x, dr1, modx, w_in, cos, sin, *scatter)
    return out[:3], out[3:]


def _ctx_bwd(dkvc, ctx, hc, w_kv):
    n_ctx = ctx.shape[0]

    def body(dkvc_ref, ctx_ref, hc_ref, w_ref, dw_ref, st_ref):
        d = dkvc_ref[...].astype(BF16)
        dw_ref[...] = _dot_tn(hc_ref[...], d)
        dhc = _dot_nt(d, w_ref[...])
        xhat, _ = _ln(ctx_ref[...])
        st_ref[...] = jnp.zeros_like(st_ref)
        st_ref[0:1, :] = _colsum(dhc)
        st_ref[1:2, :] = _colsum(dhc * xhat)

    return pl.pallas_call(
        body, name="ctx_bwd", grid=(1,),
        in_specs=[_acc((n_ctx, 2 * KV_W)), _acc((n_ctx, D)), _acc((n_ctx, D)), _acc((D, 2 * KV_W))],
        out_specs=[_acc((D, 2 * KV_W)), _acc((8, D))],
        out_shape=[jax.ShapeDtypeStruct((D, 2 * KV_W), F32), jax.ShapeDtypeStruct((8, D), F32)],
        compiler_params=_cp("arbitrary"),
    )(dkvc, ctx, hc, w_kv)


def _tn_matmul(a, b, tn, name, out_dtype, shard_major=False, init=None, tk=512):
    t, ka = a.shape
    n = b.shape[1]
    tk = min(tk, t)
    nk = t // tk
    has_init = init is not None

    def body(*refs):
        if has_init:
            a_ref, b_ref, i_ref, o_ref, acc_ref = refs
        else:
            a_ref, b_ref, o_ref, acc_ref = refs
        k = pl.program_id(1)

        @pl.when(k == 0)
        def _():
            acc_ref[...] = i_ref[...] if has_init else jnp.zeros_like(acc_ref)

        acc_ref[...] += _dot_tn(a_ref[...], b_ref[...])

        @pl.when(k == nk - 1)
        def _():
            o_ref[...] = acc_ref[...].astype(out_dtype)

    in_specs = [pl.BlockSpec((tk, ka), lambda j, k: (k, 0)), pl.BlockSpec((tk, tn), lambda j, k: (k, j))]
    args = [a, b]
    if has_init:
        in_specs.append(pl.BlockSpec((ka, tn), lambda j, k: (0, j)))
        args.append(init)
    if shard_major:
        out_spec = pl.BlockSpec((None, ka, tn), lambda j, k: (j, 0, 0))
        out_shape = jax.ShapeDtypeStruct((n // tn, ka, tn), out_dtype)
    else:
        out_spec = pl.BlockSpec((ka, tn), lambda j, k: (0, j))
        out_shape = jax.ShapeDtypeStruct((ka, n), out_dtype)
    return pl.pallas_call(
        body, name=name, grid=(n // tn, nk), in_specs=in_specs, out_specs=out_spec, out_shape=out_shape,
        scratch_shapes=[pltpu.VMEM((ka, tn), F32)],
        compiler_params=_cp("arbitrary", "arbitrary"),
    )(*args)


ADA_TILE = 512


def _ada_fwd(sc_all, w_ada):
    cs = w_ada.shape[1]

    def body(s_ref, w_ref, o_ref):
        o_ref[...] = _dot(s_ref[...].astype(BF16), w_ref[...].astype(BF16))

    return pl.pallas_call(
        body, name="ada_fwd", grid=(cs // ADA_TILE,),
        in_specs=[_acc((16, D)), pl.BlockSpec((D, ADA_TILE), lambda j: (0, j))],
        out_specs=pl.BlockSpec((16, ADA_TILE), lambda j: (0, j)),
        out_shape=jax.ShapeDtypeStruct((16, cs), F32),
        compiler_params=_cp("arbitrary"),
    )(sc_all, w_ada)


def _ada_bwd(sc_all_t, dm_all, dmc, w_ada):
    cs = w_ada.shape[1]

    def body(st_ref, dm_ref, dmc_ref, w_ref, gw_ref, part_ref):
        @pl.when(pl.program_id(0) == 0)
        def _():
            part_ref[...] = jnp.zeros_like(part_ref)

        gw_ref[...] = _dot(st_ref[...].astype(BF16), dm_ref[...].astype(BF16))
        part_ref[...] += _dot_nt(dmc_ref[...].astype(BF16), w_ref[...].astype(BF16))

    return pl.pallas_call(
        body, name="ada_bwd", grid=(cs // ADA_TILE,),
        in_specs=[_acc((D, 16)), pl.BlockSpec((16, ADA_TILE), lambda j: (0, j)), pl.BlockSpec((8, ADA_TILE), lambda j: (0, j)),
                  pl.BlockSpec((D, ADA_TILE), lambda j: (0, j))],
        out_specs=[pl.BlockSpec((D, ADA_TILE), lambda j: (0, j)), _acc((8, D))],
        out_shape=[jax.ShapeDtypeStruct((D, cs), F32), jax.ShapeDtypeStruct((8, D), F32)],
        compiler_params=_cp("arbitrary"),
    )(sc_all_t, dm_all, dmc, w_ada)


def _sum8(x, name, tr=256):
    _, r, c = x.shape
    tr = min(tr, r)
    while r % tr:
        tr -= 16

    def body(x_ref, o_ref):
        acc = x_ref[0].astype(F32)
        for i in range(1, N_DEV):
            acc = acc + x_ref[i].astype(F32)
        o_ref[...] = acc

    return pl.pallas_call(
        body, name=name, grid=(r // tr,),
        in_specs=[pl.BlockSpec((N_DEV, tr, c), lambda i: (0, i, 0))],
        out_specs=pl.BlockSpec((tr, c), lambda i: (i, 0)),
        out_shape=jax.ShapeDtypeStruct((r, c), F32),
        compiler_params=_cp("arbitrary"),
    )(x)


def _sum_blocks(recv, src, me, name, tr=256):
    _, r, c = recv.shape
    tr = min(tr, r)
    while r % tr:
        tr -= 16

    def body(me_ref, recv_ref, own_ref, o_ref):
        acc = own_ref[...].astype(F32)
        for k in range(1, N_DEV):
            acc = acc + recv_ref[me_ref[0] ^ k].astype(F32)
        o_ref[...] = acc

    return pl.pallas_call(
        body, name=name,
        grid_spec=pltpu.PrefetchScalarGridSpec(
            num_scalar_prefetch=1, grid=(r // tr,),
            in_specs=[pl.BlockSpec((N_DEV, tr, c), lambda i, me_ref: (0, i, 0)),
                      pl.BlockSpec((None, tr, c), lambda i, me_ref: (me_ref[0], i, 0))],
            out_specs=pl.BlockSpec((tr, c), lambda i, me_ref: (i, 0))),
        out_shape=jax.ShapeDtypeStruct((r, c), F32),
        compiler_params=_cp("arbitrary"),
    )(me, recv, src)


def _sum8_many(xs, name):
    n = len(xs)

    def body(*refs):
        for x_ref, o_ref in zip(refs[:n], refs[n:]):
            acc = x_ref[0]
            for i in range(1, N_DEV):
                acc = acc + x_ref[i]
            o_ref[...] = acc

    vmem = pl.BlockSpec(memory_space=pltpu.VMEM)
    return pl.pallas_call(
        body, name=name, in_specs=[vmem] * n, out_specs=[vmem] * n,
        out_shape=[jax.ShapeDtypeStruct(v.shape[1:], v.dtype) for v in xs],
        compiler_params=pltpu.CompilerParams(vmem_limit_bytes=VMEM_LIMIT),
    )(*xs)


def _adam_update(w, g, m, v):
    nm = ADAM_B1 * m + (1.0 - ADAM_B1) * g
    nv = ADAM_B2 * v + (1.0 - ADAM_B2) * (g * g)
    m_hat = nm / (1.0 - ADAM_B1 ** ADAM_STEP)
    v_hat = nv / (1.0 - ADAM_B2 ** ADAM_STEP)
    return -ADAM_LR * (m_hat / (jnp.sqrt(v_hat) + ADAM_EPS) + ADAM_WD * w), nm, nv


ROW_LOSS, ROW_LN2_G, ROW_LN2_B, ROW_LN1_G, ROW_LN1_B = 0, 1, 2, 10, 11
ROWS_DMOD_X = (16, 17, 12, 9, 8, 3)
ROWS_DMOD_C = (24, 25)
SMALL = ("c_ctx", "b_ada", "attn_sink", "gmlp_ln_g", "gmlp_ln_b", "w_spatial", "b_spatial", "ln1_g", "ln1_b", "ln2_g", "ln2_b")


def _adamw_small(sums, dsc, w, m, v):
    n = len(SMALL)

    def body(*refs):
        st_ref, gm_ref, sk_ref, ws_ref, bs_ref, dsc_ref = refs[:6]
        w_refs = dict(zip(SMALL, refs[6:6 + n]))
        m_refs = dict(zip(SMALL, refs[6 + n:6 + 2 * n]))
        v_refs = dict(zip(SMALL, refs[6 + 2 * n:6 + 3 * n]))
        outs = refs[6 + 3 * n:]
        c = w_refs["c_ctx"][...]
        sg = _sig(c)
        dmod = [st_ref[r:r + 1, :] for r in ROWS_DMOD_X]
        dmod[0] = dmod[0] + st_ref[ROWS_DMOD_C[0]:ROWS_DMOD_C[0] + 1, :]
        dmod[1] = dmod[1] + st_ref[ROWS_DMOD_C[1]:ROWS_DMOD_C[1] + 1, :]
        grads = dict(
            c_ctx=dsc_ref[0:1, :] * (sg * (1.0 + c * (1.0 - sg))),
            b_ada=jnp.concatenate(dmod, axis=1),
            attn_sink=sk_ref[0:1, 0:N_KV * GROUP],
            gmlp_ln_g=gm_ref[0:1, :], gmlp_ln_b=gm_ref[1:2, :],
            w_spatial=ws_ref[...], b_spatial=bs_ref[...],
            ln1_g=st_ref[ROW_LN1_G:ROW_LN1_G + 1, :], ln1_b=st_ref[ROW_LN1_B:ROW_LN1_B + 1, :],
            ln2_g=st_ref[ROW_LN2_G:ROW_LN2_G + 1, :], ln2_b=st_ref[ROW_LN2_B:ROW_LN2_B + 1, :])
        for i, name in enumerate(SMALL):
            g = grads[name]
            d, nm, nv = _adam_update(w_refs[name][...], g, m_refs[name][...], v_refs[name][...])
            outs[i][...] = g
            outs[n + i][...] = d
            outs[2 * n + i][...] = nm
            outs[3 * n + i][...] = nv

    vmem = pl.BlockSpec(memory_space=pltpu.VMEM)
    args = list(sums) + [dsc] + [w[k] for k in SMALL] + [m[k] for k in SMALL] + [v[k] for k in SMALL]
    shapes = [jax.ShapeDtypeStruct(w[k].shape, F32) for k in SMALL]
    out = pl.pallas_call(
        body, name="adamw_small", in_specs=[vmem] * len(args), out_specs=[vmem] * (4 * n), out_shape=shapes * 4,
        compiler_params=pltpu.CompilerParams(vmem_limit_bytes=VMEM_LIMIT),
    )(*args)
    return [dict(zip(SMALL, out[i * n:(i + 1) * n])) for i in range(4)]


def _adamw(w, g, m, v, name):
    r, c = w.shape
    tr = r if r * c <= 256 * 1024 else min(256, r)
    while r % tr:
        tr -= 8

    def body(w_ref, g_ref, m_ref, v_ref, d_ref, nm_ref, nv_ref):
        d_ref[...], nm_ref[...], nv_ref[...] = _adam_update(w_ref[...], g_ref[...], m_ref[...], v_ref[...])

    spec = pl.BlockSpec((tr, c), lambda i: (i, 0))
    shp = jax.ShapeDtypeStruct((r, c), F32)
    return pl.pallas_call(
        body, name=name, grid=(r // tr,), in_specs=[spec] * 4, out_specs=[spec] * 3, out_shape=[shp] * 3,
        compiler_params=_cp("arbitrary"),
    )(w, g, m, v)


def _my_pos():
    return lax.axis_index("x"), lax.axis_index("y"), lax.axis_index("c")


N_COPY = 7


class _Gather:
    def __init__(self, x_refs, out_refs, send_sems, recv_sems):
        self.x_refs, self.out_refs = x_refs, out_refs
        self.send_sems, self.recv_sems = send_sems, recv_sems
        x, y, c = _my_pos()
        self.c = c
        self.me, self.sibling = (x, y, c), (x, y, 1 - c)
        self.chips = [(1 - x, y), (x, 1 - y), (1 - x, 1 - y)]

    def _copy(self, a, k, block, to, from_input=False):
        px, py, pc = block
        rows = self.out_refs[a].at[4 * px + 2 * py + pc]
        return pltpu.make_async_remote_copy(
            src_ref=self.x_refs[a] if from_input else rows, dst_ref=rows,
            send_sem=self.send_sems.at[a * N_COPY + k], recv_sem=self.recv_sems.at[a * N_COPY + k],
            device_id=to, device_id_type=MESH)

    def start(self):
        n = len(self.x_refs)
        for a in range(n):
            self._copy(a, 0, self.me, self.sibling, from_input=True).start()
        for j, chip in enumerate(self.chips):
            for a in range(n):
                self._copy(a, 1 + j, self.me, (*chip, self.c), from_input=True).start()

    def finish(self):
        n = len(self.x_refs)
        c = self.c
        for j, chip in enumerate(self.chips):
            for a in range(n):
                self._copy(a, 1 + j, (*chip, c), self.me).wait_recv()
                self._copy(a, 4 + j, (*chip, c), self.sibling).start()
        for a in range(n):
            self._copy(a, 0, self.sibling, self.me).wait_recv()
        for j, chip in enumerate(self.chips):
            for a in range(n):
                self._copy(a, 4 + j, (*chip, 1 - c), self.me).wait_recv()
        for a in range(n):
            self._copy(a, 0, self.me, self.sibling, from_input=True).wait_send()
            for j, chip in enumerate(self.chips):
                self._copy(a, 1 + j, self.me, (*chip, c), from_input=True).wait_send()
                self._copy(a, 4 + j, (*chip, c), self.sibling).wait_send()


def _comm_scratch(n):
    return [pltpu.SemaphoreType.DMA((n * N_COPY,)), pltpu.SemaphoreType.DMA((n * N_COPY,))]


def _comm_specs(n):
    return [pl.BlockSpec(memory_space=pl.ANY)] * n


def _gathered_shapes(xs):
    return [jax.ShapeDtypeStruct((N_DEV,) + v.shape, v.dtype) for v in xs]


def _with_own(gathered, xs, me):
    return [lax.dynamic_update_index_in_dim(g, v, me, 0) for g, v in zip(gathered, xs)]


def _all_gather(xs, me, name):
    n = len(xs)

    def body(*refs):
        g = _Gather(refs[:n], refs[n:2 * n], *refs[2 * n:])
        g.start()
        g.finish()

    out = pl.pallas_call(
        body, name=name, out_shape=_gathered_shapes(xs), in_specs=_comm_specs(n), out_specs=_comm_specs(n),
        scratch_shapes=_comm_scratch(n),
    )(*xs)
    return _with_own(out, xs, me)


class _AllToAll:
    def __init__(self, x_refs, out_refs, send_sems, recv_sems):
        self.x_refs, self.out_refs = x_refs, out_refs
        self.send_sems, self.recv_sems = send_sems, recv_sems
        self.pos = _my_pos()
        x, y, c = self.pos
        self.me = 4 * x + 2 * y + c

    def _peer(self, k):
        x, y, c = self.pos
        return (x ^ ((k >> 2) & 1), y ^ ((k >> 1) & 1), c ^ (k & 1))

    def _copy(self, a, k):
        p = self._peer(k)
        return pltpu.make_async_remote_copy(
            src_ref=self.x_refs[a].at[4 * p[0] + 2 * p[1] + p[2]], dst_ref=self.out_refs[a].at[self.me],
            send_sem=self.send_sems.at[a * N_COPY + k - 1], recv_sem=self.recv_sems.at[a * N_COPY + k - 1],
            device_id=p, device_id_type=MESH)

    def start(self):
        for k in range(1, N_DEV):
            for a in range(len(self.x_refs)):
                self._copy(a, k).start()

    def finish(self):
        for a in range(len(self.x_refs)):
            for k in range(1, N_DEV):
                self._copy(a, k).wait_recv()
            for k in range(1, N_DEV):
                self._copy(a, k).wait_send()


def _all_to_all(blocks, name):
    n = len(blocks)

    def body(*refs):
        t = _AllToAll(refs[:n], refs[n:2 * n], *refs[2 * n:])
        t.start()
        t.finish()

    return pl.pallas_call(
        body, name=name, out_shape=[jax.ShapeDtypeStruct(v.shape, v.dtype) for v in blocks],
        in_specs=_comm_specs(n), out_specs=_comm_specs(n), scratch_shapes=_comm_scratch(n),
    )(*blocks)


def _sibling_exchange(xs, name):
    n = len(xs)

    def body(*refs):
        x_refs, out_refs = refs[:n], refs[n:2 * n]
        send_sems, recv_sems = refs[2 * n:]
        x, y, c = _my_pos()

        def push(a):
            return pltpu.make_async_remote_copy(
                src_ref=x_refs[a], dst_ref=out_refs[a], send_sem=send_sems.at[a], recv_sem=recv_sems.at[a],
                device_id=(x, y, 1 - c), device_id_type=MESH)

        for a in range(n):
            push(a).start()
        for a in range(n):
            push(a).wait_recv()
            push(a).wait_send()

    return pl.pallas_call(
        body, name=name, out_shape=[jax.ShapeDtypeStruct(v.shape, v.dtype) for v in xs],
        in_specs=_comm_specs(n), out_specs=_comm_specs(n),
        scratch_shapes=[pltpu.SemaphoreType.DMA((n,)), pltpu.SemaphoreType.DMA((n,))],
    )(*xs)


def _scatter_and_gather(scatter, gather, name):
    ns, ng = len(scatter), len(gather)

    def body(*refs):
        s_in, g_in = refs[:ns], refs[ns:ns + ng]
        s_out, g_out = refs[ns + ng:2 * ns + ng], refs[2 * ns + ng:2 * (ns + ng)]
        s_send, s_recv, g_send, g_recv = refs[2 * (ns + ng):]
        g = _Gather(g_in, g_out, g_send, g_recv)
        t = _AllToAll(s_in, s_out, s_send, s_recv)
        g.start()
        t.start()
        g.finish()
        t.finish()

    out = pl.pallas_call(
        body, name=name,
        out_shape=[jax.ShapeDtypeStruct(v.shape, v.dtype) for v in scatter] + _gathered_shapes(gather),
        in_specs=_comm_specs(ns + ng), out_specs=_comm_specs(ns + ng),
        scratch_shapes=_comm_scratch(ns) + _comm_scratch(ng),
    )(*scatter, *gather)
    return out[:ns], out[ns:]


def _row_tile(seq, want):
    return min(want, seq)


def _local_step(x, ctx, tgt, mod_x, mod_c, wb, sink, gmlp_g, gmlp_b, w_s, b_s, ln1_g, ln1_b, ln2_g, ln2_b,
                later=None, me=None):
    seq = x.shape[0]
    on_mesh = me is not None
    modx1 = jnp.concatenate([mod_x[0:2], jnp.zeros((6, D), F32)], axis=0)
    modc = jnp.concatenate([mod_c[0:2], jnp.zeros((6, D), F32)], axis=0)
    vec = jnp.concatenate([mod_x[2:3], ln1_g, ln1_b, mod_x[3:6], ln2_g, ln2_b], axis=0)
    gp = jnp.concatenate([gmlp_g, gmlp_b, jnp.zeros((6, G_W), F32)], axis=0)
    ws_stack = w_s.reshape(N_GRP * BLK, BLK).astype(BF16)
    ws_stack_t = jnp.transpose(w_s, (2, 0, 1)).reshape(BLK, N_GRP * BLK).astype(BF16)
    bias_full = jnp.repeat(b_s.T, GRP_D, axis=1)
    cos, sin = _rope_tables(seq)
    w_in = wb["w_in"]
    w_kv = w_in[:, Q_W:Q_W + 2 * KV_W]
    tm_big = _row_tile(seq, 512)
    tm_ffn = _row_tile(seq, 256)

    hc, kvc, vac = _ctx_fwd(ctx, modc, w_kv)
    behind_proj = ("w_a", "w_b", "w_o") if on_mesh else ()
    behind_attn = ("w_fi", "w_fo") if on_mesh else ()
    (h, q, kv, va, uv, gab), got_proj = _proj_fwd(x, modx1, w_in, cos, sin, tm_big, gather=[later[n] for n in behind_proj])
    (ya, lse), got_attn = _attn_fwd(q, kv, va, kvc, vac, sink, gather=[later[n] for n in behind_attn])
    if on_mesh:
        wb = dict(wb)
        names = behind_proj + behind_attn
        for n, g in zip(names, _with_own(list(got_proj) + list(got_attn), [later[n] for n in names], me)):
            wb[n] = g.reshape(-1, g.shape[2]) if n in ROW_SHARDED else g.reshape(N_SHARD, 2 * g.shape[1], g.shape[2])
    a, b, mix, merged, yb = _mix_fwd(uv, gab, ya, gp, ws_stack, bias_full, wb["w_a"], wb["w_b"], wb["w_o"], tm_big)
    act, h2, dff, df, dr1, st_ffn = _ffn(x, mix, tgt, vec, wb["w_fi"], wb["w_fo"], tm_ffn)
    g_w_fo = _tn_matmul(act, df, 512, "tn_w_ffn_out", BF16)
    g_w_fi = _tn_matmul(h2, dff, FH_SHARD, "tn_w_ffn_in", BF16, shard_major=True)
    dmix, da, db, dya, dpb, dws, dbs_full, st4 = _mix_bwd(
        dr1, a, b, gab, uv, vec, gp, ws_stack, ws_stack_t, bias_full, wb["w_a"], wb["w_b"], wb["w_o"], _row_tile(seq, 256))
    g_w_o = _tn_matmul(merged, dmix, 1024, "tn_w_out", BF16)
    g_w_a = _tn_matmul(ya, da, D // N_SHARD, "tn_w_branch_a", BF16, shard_major=True, tk=2048)
    g_w_b = _tn_matmul(yb, db, D // N_SHARD, "tn_w_branch_b", BF16, shard_major=True, tk=2048)
    blocks = dict(w_fi=_eighths(g_w_fi), w_fo=_eighths(g_w_fo), w_o=_eighths(g_w_o), w_a=_eighths(g_w_a), w_b=_eighths(g_w_b))
    early = tuple(blocks) if on_mesh else ()
    (dq, dkv, dkvc, dsink), recv_early = _attn_bwd(q, kv, kvc, sink, dya, ya, lse, scatter=[blocks[n] for n in early])
    g_wkv_ctx, st0 = _ctx_bwd(dkvc, ctx, hc, w_kv)
    (dqkv, grad_x, st1), _ = _proj_bwd(dq, dkv, dpb, x, dr1, modx1, w_in, cos, sin, tm_big)
    o = Q_W + 2 * KV_W
    init = jnp.pad(g_wkv_ctx, ((0, 0), (Q_W, 0)))
    g_w_in = jnp.concatenate([_tn_matmul(h, dqkv, o, "tn_w_in_qkv", F32, init=init),
                              _tn_matmul(h, dpb, 1536, "tn_w_in_rest", F32)], axis=1)
    g_w_in = g_w_in.reshape(D, N_SHARD, IN_W // N_SHARD).transpose(1, 0, 2).astype(BF16)

    dbs = jnp.sum(dbs_full.reshape(BLK, N_GRP, GRP_D), axis=2).T
    parts = [jnp.concatenate([st_ffn, st1, st0], axis=0), st4, dsink, dws, dbs]
    blocks["w_in"] = _eighths(g_w_in)
    return grad_x, parts, blocks, dict(zip(early, recv_early))


BIG = ("w_in", "w_a", "w_b", "w_o", "w_fi", "w_fo")
ROW_SHARDED = ("w_o", "w_fo")


def _half_of_shard(shard, c):
    r = shard.shape[0]
    return lax.dynamic_slice_in_dim(shard, c * (r // 2), r // 2, axis=0)


def _eighths(v):
    rows = v.shape[-2] * (v.shape[0] if v.ndim == 3 else 1)
    return v.reshape(N_DEV, rows // N_DEV, v.shape[-1])


def kernel(x, c, ctx, c_ctx, w_ada, b_ada, w_in, attn_sink, gmlp_ln_g, gmlp_ln_b, w_spatial, b_spatial, w_branch_a, w_branch_b, w_out, ln1_g, ln1_b, w_ffn_in, w_ffn_out, ln2_g, ln2_b, loss_target, m_c_ctx, m_w_ada, m_b_ada, m_w_in, m_attn_sink, m_gmlp_ln_g, m_gmlp_ln_b, m_w_spatial, m_b_spatial, m_w_branch_a, m_w_branch_b, m_w_out, m_ln1_g, m_ln1_b, m_w_ffn_in, m_w_ffn_out, m_ln2_g, m_ln2_b, v_c_ctx, v_w_ada, v_b_ada, v_w_in, v_attn_sink, v_gmlp_ln_g, v_gmlp_ln_b, v_w_spatial, v_b_spatial, v_w_branch_a, v_w_branch_b, v_w_out, v_ln1_g, v_ln1_b, v_w_ffn_in, v_w_ffn_out, v_ln2_g, v_ln2_b):
    mx, my, mc = _my_pos()
    me = 4 * mx + 2 * my + mc
    chip = 2 * mx + my
    shards = dict(w_in=w_in[0], w_a=w_branch_a[0], w_b=w_branch_b[0], w_o=w_out[0], w_fi=w_ffn_in[0], w_fo=w_ffn_out[0])

    halves = {n: _half_of_shard(shards[n], mc).astype(BF16) for n in BIG}
    c_rows = jnp.concatenate([c, jnp.zeros((7, D), F32)], axis=0)
    g_in, c_g = _all_gather([halves["w_in"], c_rows], me, "gather_w_in")
    r, cdim = shards["w_in"].shape
    wb = dict(w_in=g_in.reshape(N_SHARD, r, cdim).transpose(1, 0, 2).reshape(r, N_SHARD * cdim))

    c_all = c_g[:, 0, :]
    cc = jnp.concatenate([c_all, c_ctx[None, :], jnp.zeros((7, D), F32)], axis=0)
    sig_cc = jax.nn.sigmoid(cc)
    sc_all = cc * sig_cc
    mod_shard = _ada_fwd(sc_all, w_ada[0])
    mod_g = _all_gather([mod_shard], me, "gather_mod")[0]
    mod_all = jnp.concatenate([mod_g[2 * s] for s in range(4)], axis=1) + b_ada
    mod_x = lax.dynamic_slice_in_dim(mod_all, me, 1, axis=0).reshape(6, D)
    mod_c = mod_all[8].reshape(6, D)[0:2]

    grad_x, parts, blocks, recv = _local_step(
        x[0], ctx[0], loss_target[0], mod_x, mod_c, wb, attn_sink, gmlp_ln_g, gmlp_ln_b, w_spatial[0], b_spatial[0],
        ln1_g, ln1_b, ln2_g, ln2_b, later=halves, me=me)

    (recv["w_in"],), gathered = _scatter_and_gather([blocks["w_in"]], parts, "scatter_w_in_gather_small")
    gathered = _with_own(gathered, parts, me)

    me_arr = jnp.reshape(me, (1,)).astype(jnp.int32)
    summed = {n: _sum_blocks(recv[n], blocks[n], me_arr, "sum_grads_" + n) for n in BIG}
    theirs = _sibling_exchange([summed[n] for n in BIG], "exchange_grads")
    g_shard = {}
    for n, other in zip(BIG, theirs):
        lo = jnp.where(mc == 0, summed[n], other)
        hi = jnp.where(mc == 0, other, summed[n])
        g_shard[n] = jnp.concatenate([lo, hi], axis=0)

    sums = _sum8_many(gathered, "sum_small")
    stats = sums[0]
    loss = 0.5 * jnp.sum(stats[ROW_LOSS]) / D
    dmod_x_all = jnp.concatenate([gathered[0][:, r_, :] for r_ in ROWS_DMOD_X], axis=1)
    dmod_c_full = jnp.concatenate([stats[r_] for r_ in ROWS_DMOD_C] + [jnp.zeros((4 * D,), F32)])
    dm_rows = jnp.concatenate([dmod_x_all, dmod_c_full[None, :], jnp.zeros((7, 6 * D), F32)], axis=0)
    cs = w_ada.shape[2]
    dm_shard = lax.dynamic_slice_in_dim(dm_rows, chip * cs, cs, axis=1)
    dmc_shard = jnp.concatenate([dm_shard[8:9], jnp.zeros((7, cs), F32)], axis=0)
    g_w_ada, part = _ada_bwd(sc_all.T, dm_shard, dmc_shard, w_ada[0])
    part_all = _all_gather([part * (mc == 0).astype(F32)], me, "gather_c_ctx")[0]
    dsc = _sum8(part_all, "sum_c_ctx")

    grads = dict(w_ada=g_w_ada[None], w_in=g_shard["w_in"][None], w_branch_a=g_shard["w_a"][None],
                 w_branch_b=g_shard["w_b"][None], w_out=g_shard["w_o"][None], w_ffn_in=g_shard["w_fi"][None],
                 w_ffn_out=g_shard["w_fo"][None])
    weights = dict(c_ctx=c_ctx, w_ada=w_ada, b_ada=b_ada, w_in=w_in, attn_sink=attn_sink, gmlp_ln_g=gmlp_ln_g,
                   gmlp_ln_b=gmlp_ln_b, w_spatial=w_spatial, b_spatial=b_spatial, w_branch_a=w_branch_a,
                   w_branch_b=w_branch_b, w_out=w_out, ln1_g=ln1_g, ln1_b=ln1_b, w_ffn_in=w_ffn_in, w_ffn_out=w_ffn_out,
                   ln2_g=ln2_g, ln2_b=ln2_b)
    ms = dict(c_ctx=m_c_ctx, w_ada=m_w_ada, b_ada=m_b_ada, w_in=m_w_in, attn_sink=m_attn_sink, gmlp_ln_g=m_gmlp_ln_g,
              gmlp_ln_b=m_gmlp_ln_b, w_spatial=m_w_spatial, b_spatial=m_b_spatial, w_branch_a=m_w_branch_a,
              w_branch_b=m_w_branch_b, w_out=m_w_out, ln1_g=m_ln1_g, ln1_b=m_ln1_b, w_ffn_in=m_w_ffn_in,
              w_ffn_out=m_w_ffn_out, ln2_g=m_ln2_g, ln2_b=m_ln2_b)
    vs = dict(c_ctx=v_c_ctx, w_ada=v_w_ada, b_ada=v_b_ada, w_in=v_w_in, attn_sink=v_attn_sink, gmlp_ln_g=v_gmlp_ln_g,
              gmlp_ln_b=v_gmlp_ln_b, w_spatial=v_w_spatial, b_spatial=v_b_spatial, w_branch_a=v_w_branch_a,
              w_branch_b=v_w_branch_b, w_out=v_w_out, ln1_g=v_ln1_g, ln1_b=v_ln1_b, w_ffn_in=v_w_ffn_in,
              w_ffn_out=v_w_ffn_out, ln2_g=v_ln2_g, ln2_b=v_ln2_b)
    order = list(weights)
    large = ("w_ada", "w_in", "w_branch_a", "w_branch_b", "w_out", "w_ffn_in", "w_ffn_out")
    delta, new_m, new_v = {}, {}, {}
    for n in large:
        d_, m_, v_ = _adamw(weights[n][0], grads[n][0], ms[n][0], vs[n][0], "adamw_" + n)
        delta[n], new_m[n], new_v[n] = d_[None], m_[None], v_[None]

    def view(a):
        return a.reshape(-1, a.shape[-1]) if a.ndim != 1 else a.reshape(1, -1)

    small = _adamw_small(sums, dsc, *[{n: view(d[n]) for n in SMALL} for d in (weights, ms, vs)])
    for out, src in zip((grads, delta, new_m, new_v), small):
        for n in SMALL:
            out[n] = src[n].reshape(weights[n].shape)

    return (loss, grad_x[None], *[grads[n] for n in order], *[delta[n] for n in order],
            *[new_m[n] for n in order], *[new_v[n] for n in order])
```

```python
import functools
import math

import jax
import jax.numpy as jnp
from jax import lax
from jax.experimental import pallas as pl
from jax.experimental.pallas import tpu as pltpu

F32 = jnp.float32
BF16 = jnp.bfloat16

D = 1024
HEAD = 64
N_KV = 2
GROUP = 4
Q_W = 512
KV_W = 128
G_W = 512
BLK = 128
N_GRP = 8
GRP_D = 64
FH = 2816
IN_W = 3840
GRID_W = 64
ROPE_BASE = 10000.0
LN_EPS = 1e-5
NEG = -1e30
ALPHA = (2 * 1) ** 0.25
SCALE = HEAD ** -0.5
GELU_K = math.sqrt(2.0 / math.pi)
GELU_A = 0.044715
ADAM_LR = 0.001
ADAM_B1 = 0.9
ADAM_B2 = 0.999
ADAM_EPS = 1e-08
ADAM_WD = 0.01
ADAM_STEP = 10
N_DEV = 8
N_SHARD = 4
FH_SHARD = FH // 2
LANES = 128
VMEM_LIMIT = 56 * 1024 * 1024
MESH = pl.DeviceIdType.MESH


def _cp(*sem):
    return pltpu.CompilerParams(dimension_semantics=sem, vmem_limit_bytes=VMEM_LIMIT)


def _resident(shape):
    return pl.BlockSpec(shape, lambda *_: (0,) * len(shape), pipeline_mode=pl.Buffered(1))


def _rows(tm, width):
    return pl.BlockSpec((tm, width), lambda i: (i, 0))


def _acc(shape):
    return pl.BlockSpec(shape, lambda *_: (0,) * len(shape))


def _dot(a, b):
    return jnp.dot(a, b, preferred_element_type=F32)


def _dot_nt(a, b):
    return lax.dot_general(a, b, (((1,), (1,)), ((), ())), preferred_element_type=F32)


def _dot_tn(a, b):
    return lax.dot_general(a, b, (((0,), (0,)), ((), ())), preferred_element_type=F32)


def _ln(x):
    mu = jnp.mean(x, axis=-1, keepdims=True)
    xc = x - mu
    var = jnp.mean(xc * xc, axis=-1, keepdims=True)
    rstd = lax.rsqrt(var + LN_EPS)
    return xc * rstd, rstd


def _ln_bwd(dxhat, xhat, rstd):
    return (dxhat - jnp.mean(dxhat, axis=-1, keepdims=True)
            - xhat * jnp.mean(dxhat * xhat, axis=-1, keepdims=True)) * rstd


def _sig(x):
    return 1.0 / (1.0 + jnp.exp(-x))


def _gelu(x):
    t = jnp.tanh(GELU_K * (x + GELU_A * x * x * x))
    return 0.5 * x * (1.0 + t), t


def _gelu_grad(x, t):
    return 0.5 * (1.0 + t) + 0.5 * x * (1.0 - t * t) * GELU_K * (1.0 + 3.0 * GELU_A * x * x)


def _colsum(v):
    return jnp.sum(v, axis=0, keepdims=True)


def _partner(x):
    w = x.shape[1]
    lane = lax.broadcasted_iota(jnp.int32, x.shape, 1)
    return jnp.where((lane & 31) < 16, pltpu.roll(x, w - 16, 1), pltpu.roll(x, 16, 1))


def _rope(x, cos, sin):
    return x * cos + _partner(x) * sin


def _unrope(g, cos, sin):
    return g * cos + _partner(g * sin)


def _rope_tables(seq):
    inv = ROPE_BASE ** (-jnp.arange(HEAD // 4, dtype=F32) / (HEAD // 4))
    pos = jnp.arange(seq, dtype=jnp.int32)
    ar = (pos // GRID_W).astype(F32)[:, None] * inv
    ac = (pos % GRID_W).astype(F32)[:, None] * inv
    cos = jnp.concatenate([jnp.cos(ar), jnp.cos(ar), jnp.cos(ac), jnp.cos(ac)], axis=-1)
    sin = jnp.concatenate([-jnp.sin(ar), jnp.sin(ar), -jnp.sin(ac), jnp.sin(ac)], axis=-1)
    return jnp.tile(cos, (1, LANES // HEAD)), jnp.tile(sin, (1, LANES // HEAD))


def _ctx_fwd(ctx, modc, w_kv):
    n_ctx = ctx.shape[0]

    def body(ctx_ref, mod_ref, w_ref, hc_ref, kvc_ref, vac_ref):
        xhat, _ = _ln(ctx_ref[...])
        hc = (xhat * (1.0 + mod_ref[1:2, :]) + mod_ref[0:1, :]).astype(BF16)
        hc_ref[...] = hc
        kvc = _dot(hc, w_ref[...]).astype(BF16)
        kvc_ref[...] = kvc
        vac_ref[...] = _with_ones(kvc[:, KV_W:])

    return pl.pallas_call(
        body, name="ctx_fwd", grid=(1,),
        in_specs=[_acc((n_ctx, D)), _acc((8, D)), _acc((D, 2 * KV_W))],
        out_specs=[_acc((n_ctx, D)), _acc((n_ctx, 2 * KV_W)), _acc((n_ctx, 2 * LANES))],
        out_shape=[jax.ShapeDtypeStruct((n_ctx, D), BF16), jax.ShapeDtypeStruct((n_ctx, 2 * KV_W), BF16),
                   jax.ShapeDtypeStruct((n_ctx, 2 * LANES), BF16)],
        compiler_params=_cp("arbitrary"),
    )(ctx, modc, w_kv)


def _host_start(step, comm):
    if comm is not None:
        @pl.when(step == 0)
        def _():
            comm.start()


def _host_finish(step, last, comm):
    if comm is not None:
        @pl.when(step == last)
        def _():
            comm.finish()


def _proj_fwd(x, modx, w_in, cos, sin, tm, gather=()):
    seq = x.shape[0]
    ng = len(gather)

    def body(x_ref, mod_ref, w_ref, cos_ref, sin_ref, *rest):
        h_ref, q_ref, kv_ref, va_ref, uv_ref, gab_ref = rest[ng:ng + 6]
        comm = _Gather(rest[:ng], rest[ng + 6:2 * ng + 6], *rest[2 * ng + 6:]) if ng else None
        _host_start(pl.program_id(0), comm)
        xhat, _ = _ln(x_ref[...])
        h = (xhat * (1.0 + mod_ref[1:2, :]) + mod_ref[0:1, :]).astype(BF16)
        h_ref[...] = h
        cos1, sin1 = cos_ref[...], sin_ref[...]
        cos2 = jnp.concatenate([cos1, cos1], axis=1)
        sin2 = jnp.concatenate([sin1, sin1], axis=1)
        for j in range(Q_W // 256):
            t = _dot(h, w_ref[:, 256 * j:256 * (j + 1)])
            q_ref[:, 256 * j:256 * (j + 1)] = (_rope(t, cos2, sin2) * SCALE).astype(BF16)
        t = _dot(h, w_ref[:, Q_W:Q_W + 2 * KV_W])
        kv_ref[:, :KV_W] = _rope(t[:, :KV_W], cos1, sin1).astype(BF16)
        v = t[:, KV_W:].astype(BF16)
        kv_ref[:, KV_W:] = v
        va_ref[...] = _with_ones(v)
        o = Q_W + 2 * KV_W
        for j in range(2):
            uv_ref[:, G_W * j:G_W * (j + 1)] = _dot(h, w_ref[:, o + G_W * j:o + G_W * (j + 1)])
        o += 2 * G_W
        for j in range(4):
            gab_ref[:, 512 * j:512 * (j + 1)] = _dot(h, w_ref[:, o + 512 * j:o + 512 * (j + 1)]).astype(BF16)
        _host_finish(pl.program_id(0), seq // tm - 1, comm)

    out = pl.pallas_call(
        body, name="proj_fwd", grid=(seq // tm,),
        in_specs=[_rows(tm, D), _acc((8, D)), _resident((D, IN_W)), _rows(tm, LANES), _rows(tm, LANES)] + _comm_specs(ng),
        out_specs=[_rows(tm, D), _rows(tm, Q_W), _rows(tm, 2 * KV_W), _rows(tm, 2 * LANES), _rows(tm, 2 * G_W),
                   _rows(tm, 2 * D)] + _comm_specs(ng),
        out_shape=[jax.ShapeDtypeStruct((seq, D), BF16), jax.ShapeDtypeStruct((seq, Q_W), BF16),
                   jax.ShapeDtypeStruct((seq, 2 * KV_W), BF16), jax.ShapeDtypeStruct((seq, 2 * LANES), BF16),
                   jax.ShapeDtypeStruct((seq, 2 * G_W), F32), jax.ShapeDtypeStruct((seq, 2 * D), BF16)] + _gathered_shapes(gather),
        scratch_shapes=_comm_scratch(ng) if ng else [],
        compiler_params=_cp("arbitrary"),
    )(x, modx, w_in, cos, sin, *gather)
    return out[:6], out[6:]


def _stack_heads(x, hk):
    return jnp.concatenate([x[:, (hk * GROUP + g) * HEAD:(hk * GROUP + g + 1) * HEAD] for g in range(GROUP)], axis=0)


def _attn_scores(q_ref, k_refs, hk, n, nb):
    q4 = _stack_heads(q_ref[...], hk)
    ks = [r[:, hk * HEAD:(hk + 1) * HEAD] for r in k_refs]
    rows = GROUP * BLK
    qi = lax.broadcasted_iota(jnp.int32, (rows, BLK), 0) & (BLK - 1)
    kj = lax.broadcasted_iota(jnp.int32, (rows, BLK), 1)
    s = [_dot_nt(q4, k) for k in ks]
    s[1] = jnp.where((kj >= qi) & (n > 0), s[1], NEG)
    s[3] = jnp.where((kj <= qi) & (n < nb - 1), s[3], NEG)
    return q4, ks, s


def _sink_rows(sink_ref, hk):
    rows = GROUP * BLK
    rg = lax.broadcasted_iota(jnp.int32, (rows, 1), 0) >> 7
    sink_v = jnp.full((rows, 1), sink_ref[0, hk * GROUP], F32)
    for g in range(1, GROUP):
        sink_v = jnp.where(rg == g, sink_ref[0, hk * GROUP + g], sink_v)
    return sink_v


def _with_ones(v):
    ones = jnp.ones((v.shape[0], HEAD), v.dtype)
    return jnp.concatenate([v[:, :HEAD], ones, v[:, HEAD:], ones], axis=1)


def _kv_specs(nb):
    return [pl.BlockSpec((BLK, 2 * KV_W), lambda n: (jnp.maximum(n - 1, 0), 0)),
            pl.BlockSpec((BLK, 2 * KV_W), lambda n: (n, 0)),
            pl.BlockSpec((BLK, 2 * KV_W), lambda n: (jnp.minimum(n + 1, nb - 1), 0))]


def _attn_fwd(q, kv, va, kvc, vac, sink, gather=()):
    seq = q.shape[0]
    nb = seq // BLK
    n_ctx = kvc.shape[0]
    ng = len(gather)

    def body(q_ref, kvp_ref, kvm_ref, kvn_ref, vap_ref, vam_ref, van_ref, kvc_ref, vac_ref, sink_ref, *rest):
        o_ref, lse_ref = rest[ng:ng + 2]
        comm = _Gather(rest[:ng], rest[ng + 2:2 * ng + 2], *rest[2 * ng + 2:]) if ng else None
        n = pl.program_id(0)
        _host_start(n, comm)
        outs = []
        lane = lax.broadcasted_iota(jnp.int32, (BLK, LANES), 1)
        lse_all = jnp.zeros((BLK, LANES), F32)
        for hk in range(N_KV):
            _, _, s = _attn_scores(q_ref, (kvc_ref, kvp_ref, kvm_ref, kvn_ref), hk, n, nb)
            sink_v = _sink_rows(sink_ref, hk)
            m = sink_v
            for t in s:
                m = jnp.maximum(m, jnp.max(t, axis=-1, keepdims=True))
            o = jnp.zeros((GROUP * BLK, LANES), F32)
            for t, va_ref in zip(s, (vac_ref, vap_ref, vam_ref, van_ref)):
                o = o + _dot(jnp.exp((t - m).astype(BF16)), va_ref[:, hk * LANES:(hk + 1) * LANES])
            denom = o[:, HEAD:HEAD + 1] + jnp.exp(sink_v - m)
            o4 = o[:, :HEAD] * (1.0 / denom)
            lse4 = m + jnp.log(denom)
            for g in range(GROUP):
                outs.append(o4[g * BLK:(g + 1) * BLK, :])
                lse_all = jnp.where(lane == hk * GROUP + g, lse4[g * BLK:(g + 1) * BLK, :], lse_all)
        o_ref[...] = jnp.concatenate(outs, axis=1).astype(BF16)
        lse_ref[...] = lse_all
        _host_finish(n, nb - 1, comm)

    out = pl.pallas_call(
        body, name="attn_fwd", grid=(nb,),
        in_specs=[_rows(BLK, Q_W)] + _kv_specs(nb) + _kv_specs(nb)
        + [_acc((n_ctx, 2 * KV_W)), _acc((n_ctx, 2 * LANES)), pl.BlockSpec(memory_space=pltpu.SMEM)] + _comm_specs(ng),
        out_specs=[_rows(BLK, Q_W), _rows(BLK, LANES)] + _comm_specs(ng),
        out_shape=[jax.ShapeDtypeStruct((seq, Q_W), BF16), jax.ShapeDtypeStruct((seq, LANES), F32)] + _gathered_shapes(gather),
        scratch_shapes=_comm_scratch(ng) if ng else [],
        compiler_params=_cp("arbitrary"),
    )(q, kv, kv, kv, va, va, va, kvc, vac, sink, *gather)
    return out[:2], out[2:]


def _gmlp_chunk(u, vb, gp_ref, ws_ref, bias_ref):
    gu, tu = _gelu(u)
    gv, tv = _gelu(vb)
    vhat, rstd = _ln(gv)
    vn = (vhat * gp_ref[0:1, :] + gp_ref[1:2, :]).astype(BF16)
    s = bias_ref[...] + jnp.concatenate(
        [_dot(ws_ref[g * BLK:(g + 1) * BLK, :], vn[:, g * GRP_D:(g + 1) * GRP_D]) for g in range(N_GRP)], axis=1)
    return gu, tu, tv, vhat, rstd, vn, s


def _mix_fwd(uv, gab, ya, gp, ws_stack, bias_full, w_a, w_b, w_o, tm):
    seq = uv.shape[0]

    def body(uv_ref, gab_ref, ya_ref, gp_ref, ws_ref, bias_ref, wa_ref, wb_ref, wo_ref,
             a_ref, b_ref, mix_ref, merged_ref, yb_ref):
        for c in range(tm // BLK):
            rs = slice(c * BLK, (c + 1) * BLK)
            gu, _, _, _, _, _, s = _gmlp_chunk(uv_ref[rs, :G_W], uv_ref[rs, G_W:], gp_ref, ws_ref, bias_ref)
            yb_ref[rs, :] = (gu * s).astype(BF16)
        ya = ya_ref[...]
        yb = yb_ref[...]
        for s in range(N_SHARD):
            cs = slice(s * (D // N_SHARD), (s + 1) * (D // N_SHARD))
            a = _dot(ya, wa_ref[s])
            b = _dot(yb, wb_ref[s])
            a_ref[:, cs] = a.astype(BF16)
            b_ref[:, cs] = b.astype(BF16)
            ga = gab_ref[:, cs].astype(F32)
            gb = gab_ref[:, D + s * (D // N_SHARD):D + (s + 1) * (D // N_SHARD)].astype(F32)
            merged_ref[:, cs] = (_sig(ga) * a + _sig(gb) * b).astype(BF16)
        mix_ref[...] = _dot(merged_ref[...], wo_ref[...])

    return pl.pallas_call(
        body, name="mix_fwd", grid=(seq // tm,),
        in_specs=[_rows(tm, 2 * G_W), _rows(tm, 2 * D), _rows(tm, Q_W), _acc((8, G_W)),
                  _resident((N_GRP * BLK, BLK)), _acc((BLK, G_W)),
                  _resident((N_SHARD, Q_W, D // N_SHARD)), _resident((N_SHARD, G_W, D // N_SHARD)), _resident((D, D))],
        out_specs=[_rows(tm, D), _rows(tm, D), _rows(tm, D), _rows(tm, D), _rows(tm, G_W)],
        out_shape=[jax.ShapeDtypeStruct((seq, D), BF16), jax.ShapeDtypeStruct((seq, D), BF16),
                   jax.ShapeDtypeStruct((seq, D), F32), jax.ShapeDtypeStruct((seq, D), BF16),
                   jax.ShapeDtypeStruct((seq, G_W), BF16)],
        compiler_params=_cp("arbitrary"),
    )(uv, gab, ya, gp, ws_stack, bias_full, w_a, w_b, w_o)


FFN_CHUNK = 512


def _ffn_chunks():
    out = []
    for hh in range(2):
        off = 0
        while off < FH_SHARD:
            w = min(FFN_CHUNK, FH_SHARD - off)
            out.append((hh, off, w))
            off += w
    return out


def _mid_recompute(x_ref, mix_ref, vec_ref):
    r1 = ALPHA * x_ref[...] + vec_ref[0:1, :] * mix_ref[...]
    xh1, rstd1 = _ln(r1)
    xmid = xh1 * vec_ref[1:2, :] + vec_ref[2:3, :]
    xh2, rstd2 = _ln(xmid)
    return xh1, rstd1, xmid, xh2, rstd2


def _ffn(x, mix, tgt, vec, w_fi, w_fo, tm):
    seq = x.shape[0]

    def body(x_ref, mix_ref, tgt_ref, vec_ref, wi_ref, wo_ref, act_ref, h2_ref, dff_ref, df_ref, dr1_ref, st_ref, gu_ref):
        @pl.when(pl.program_id(0) == 0)
        def _():
            st_ref[...] = jnp.zeros_like(st_ref)

        xh1, rstd1, xmid, xh2, rstd2 = _mid_recompute(x_ref, mix_ref, vec_ref)
        h2 = (xh2 * (1.0 + vec_ref[4:5, :]) + vec_ref[3:4, :]).astype(BF16)
        h2_ref[...] = h2
        f = jnp.zeros((tm, D), F32)
        for hh, off, w in _ffn_chunks():
            cs = slice(hh * FH_SHARD + off, hh * FH_SHARD + off + w)
            cu = slice(FH + hh * FH_SHARD + off, FH + hh * FH_SHARD + off + w)
            g = _dot(h2, wi_ref[hh, :, off:off + w])
            u = _dot(h2, wi_ref[2 + hh, :, off:off + w])
            gu_ref[:, cs] = g
            gu_ref[:, cu] = u
            a = (g * _sig(g) * u).astype(BF16)
            act_ref[:, cs] = a
            f = f + _dot(a, wo_ref[cs, :])
        r2 = ALPHA * xmid + vec_ref[5:6, :] * f
        yh, rstd = _ln(r2)
        y = yh * vec_ref[6:7, :] + vec_ref[7:8, :]
        err = y - tgt_ref[...]
        dy = err / D
        dr2 = _ln_bwd(dy * vec_ref[6:7, :], yh, rstd)
        st_ref[0:1, :] += _colsum(err * err)
        st_ref[1:2, :] += _colsum(dy * yh)
        st_ref[2:3, :] += _colsum(dy)
        st_ref[3:4, :] += _colsum(dr2 * f)

        df = (dr2 * vec_ref[5:6, :]).astype(BF16)
        df_ref[...] = df
        dh2 = jnp.zeros((tm, D), F32)
        for hh, off, w in _ffn_chunks():
            cs = slice(hh * FH_SHARD + off, hh * FH_SHARD + off + w)
            cu = slice(FH + hh * FH_SHARD + off, FH + hh * FH_SHARD + off + w)
            da = _dot_nt(df, wo_ref[cs, :])
            g = gu_ref[:, cs]
            u = gu_ref[:, cu]
            sg = _sig(g)
            dg = (da * u * sg * (1.0 + g * (1.0 - sg))).astype(BF16)
            du = (da * g * sg).astype(BF16)
            dff_ref[:, cs] = dg
            dff_ref[:, cu] = du
            dh2 = dh2 + _dot_nt(dg, wi_ref[hh, :, off:off + w]) + _dot_nt(du, wi_ref[2 + hh, :, off:off + w])
        dxmid = _ln_bwd(dh2 * (1.0 + vec_ref[4:5, :]), xh2, rstd2) + ALPHA * dr2
        dr1 = _ln_bwd(dxmid * vec_ref[1:2, :], xh1, rstd1)
        dr1_ref[...] = dr1
        st_ref[8:9, :] += _colsum(dh2 * xh2)
        st_ref[9:10, :] += _colsum(dh2)
        st_ref[10:11, :] += _colsum(dxmid * xh1)
        st_ref[11:12, :] += _colsum(dxmid)
        st_ref[12:13, :] += _colsum(dr1 * mix_ref[...])

    return pl.pallas_call(
        body, name="ffn", grid=(seq // tm,),
        in_specs=[_rows(tm, D), _rows(tm, D), _rows(tm, D), _acc((8, D)), _resident((N_SHARD, D, FH_SHARD)), _resident((FH, D))],
        out_specs=[_rows(tm, FH), _rows(tm, D), _rows(tm, 2 * FH), _rows(tm, D), _rows(tm, D), _acc((16, D))],
        out_shape=[jax.ShapeDtypeStruct((seq, FH), BF16), jax.ShapeDtypeStruct((seq, D), BF16),
                   jax.ShapeDtypeStruct((seq, 2 * FH), BF16), jax.ShapeDtypeStruct((seq, D), BF16),
                   jax.ShapeDtypeStruct((seq, D), F32), jax.ShapeDtypeStruct((16, D), F32)],
        scratch_shapes=[pltpu.VMEM((tm, 2 * FH), F32)],
        compiler_params=_cp("arbitrary"),
    )(x, mix, tgt, vec, w_fi, w_fo)


def _mix_bwd(dr1, a, b, gab, uv, merged, ya, yb, vec, gp, ws_stack, ws_stack_t, bias_full, w_a, w_b, w_o, tm):
    seq = dr1.shape[0]
    last = seq // tm - 1
    cw = D // N_SHARD

    def body(dr1_ref, a_ref, b_ref, gab_ref, uv_ref, mg_ref, ya_ref, yb_ref, vec_ref, gp_ref, ws_ref, wst_ref, bias_ref,
             wa_ref, wb_ref, wo_ref, dya_ref, dp_ref, dws_ref, dbs_ref, st_ref, gwo_ref, gwa_ref, gwb_ref,
             acc_o, acc_a, acc_b):
        @pl.when(pl.program_id(0) == 0)
        def _():
            dws_ref[...] = jnp.zeros_like(dws_ref)
            dbs_ref[...] = jnp.zeros_like(dbs_ref)
            st_ref[...] = jnp.zeros_like(st_ref)
            acc_o[...] = jnp.zeros_like(acc_o)
            acc_a[...] = jnp.zeros_like(acc_a)
            acc_b[...] = jnp.zeros_like(acc_b)

        dmix = (dr1_ref[...] * vec_ref[0:1, :]).astype(BF16)
        acc_o[...] += _dot_tn(mg_ref[...], dmix)
        dmerged = _dot_nt(dmix, wo_ref[...])
        sa = _sig(gab_ref[:, :D].astype(F32))
        sb = _sig(gab_ref[:, D:].astype(F32))
        da = (dmerged * sa).astype(BF16)
        db = (dmerged * sb).astype(BF16)
        dp_ref[:, 2 * G_W:2 * G_W + D] = (dmerged * a_ref[...].astype(F32) * sa * (1.0 - sa)).astype(BF16)
        dp_ref[:, 2 * G_W + D:] = (dmerged * b_ref[...].astype(F32) * sb * (1.0 - sb)).astype(BF16)
        dya = jnp.zeros((tm, Q_W), F32)
        dyb = jnp.zeros((tm, G_W), F32)
        ya = ya_ref[...]
        yb = yb_ref[...]
        for s in range(N_SHARD):
            cs = slice(s * cw, (s + 1) * cw)
            dya = dya + _dot_nt(da[:, cs], wa_ref[s])
            dyb = dyb + _dot_nt(db[:, cs], wb_ref[s])
            acc_a[s] += _dot_tn(ya, da[:, cs])
            acc_b[s] += _dot_tn(yb, db[:, cs])
        dya_ref[...] = dya.astype(BF16)

        @pl.when(pl.program_id(0) == last)
        def _():
            gwo_ref[...] = acc_o[...].astype(BF16)
            gwa_ref[...] = acc_a[...].astype(BF16)
            gwb_ref[...] = acc_b[...].astype(BF16)

        for c in range(tm // BLK):
            rs = slice(c * BLK, (c + 1) * BLK)
            u = uv_ref[rs, :G_W]
            vb = uv_ref[rs, G_W:]
            gu, tu, tv, vhat, rstd, vn, s = _gmlp_chunk(u, vb, gp_ref, ws_ref, bias_ref)
            dyb_c = dyb[rs, :]
            ds = dyb_c * gu
            du = dyb_c * s * _gelu_grad(u, tu)
            ds_b = ds.astype(BF16)
            dvn_g = []
            for g in range(N_GRP):
                cg = slice(g * GRP_D, (g + 1) * GRP_D)
                dvn_g.append(_dot(wst_ref[:, g * BLK:(g + 1) * BLK], ds_b[:, cg]))
                dws_ref[g * BLK:(g + 1) * BLK, :] += _dot_nt(ds_b[:, cg], vn[:, cg])
            dvn = jnp.concatenate(dvn_g, axis=1)
            dbs_ref[...] += ds
            st_ref[0:1, :] += _colsum(dvn * vhat)
            st_ref[1:2, :] += _colsum(dvn)
            dgv = _ln_bwd(dvn * gp_ref[0:1, :], vhat, rstd)
            dvb = dgv * _gelu_grad(vb, tv)
            dp_ref[rs, :G_W] = du.astype(BF16)
            dp_ref[rs, G_W:2 * G_W] = dvb.astype(BF16)

    pw = 2 * G_W + 2 * D
    return pl.pallas_call(
        body, name="mix_bwd", grid=(seq // tm,),
        in_specs=[_rows(tm, D), _rows(tm, D), _rows(tm, D), _rows(tm, 2 * D), _rows(tm, 2 * G_W), _rows(tm, D), _rows(tm, Q_W),
                  _rows(tm, G_W), _acc((8, D)), _acc((8, G_W)),
                  _resident((N_GRP * BLK, BLK)), _resident((BLK, N_GRP * BLK)), _acc((BLK, G_W)),
                  _resident((N_SHARD, Q_W, cw)), _resident((N_SHARD, G_W, cw)), _resident((D, D))],
        out_specs=[_rows(tm, Q_W), _rows(tm, pw), _acc((N_GRP * BLK, BLK)), _acc((BLK, G_W)), _acc((8, G_W)),
                   _acc((D, D)), _acc((N_SHARD, Q_W, cw)), _acc((N_SHARD, G_W, cw))],
        out_shape=[jax.ShapeDtypeStruct((seq, Q_W), BF16), jax.ShapeDtypeStruct((seq, pw), BF16),
                   jax.ShapeDtypeStruct((N_GRP * BLK, BLK), F32), jax.ShapeDtypeStruct((BLK, G_W), F32),
                   jax.ShapeDtypeStruct((8, G_W), F32), jax.ShapeDtypeStruct((D, D), BF16),
                   jax.ShapeDtypeStruct((N_SHARD, Q_W, cw), BF16), jax.ShapeDtypeStruct((N_SHARD, G_W, cw), BF16)],
        scratch_shapes=[pltpu.VMEM((D, D), F32), pltpu.VMEM((N_SHARD, Q_W, cw), F32), pltpu.VMEM((N_SHARD, G_W, cw), F32)],
        compiler_params=_cp("arbitrary"),
    )(dr1, a, b, gab, uv, merged, ya, yb, vec, gp, ws_stack, ws_stack_t, bias_full, w_a, w_b, w_o)


def _attn_bwd(q, kv, kvc, sink, dya, ya, lse, scatter=()):
    seq = q.shape[0]
    nb = seq // BLK
    n_ctx = kvc.shape[0]
    ns = len(scatter)

    def body(q_ref, kvp_ref, kvm_ref, kvn_ref, kvc_ref, sink_ref, do_ref, o_ref, lse_ref, *rest):
        dq_ref, dkv_ref, dkvc_ref, dsink_ref = rest[ns:ns + 4]
        comm = _AllToAll(rest[:ns], rest[ns + 4:2 * ns + 4], *rest[2 * ns + 4:]) if ns else None
        n = pl.program_id(0)
        _host_start(n, comm)

        @pl.when(n == 0)
        def _():
            dkv_ref[...] = jnp.zeros_like(dkv_ref)
            dkvc_ref[...] = jnp.zeros_like(dkvc_ref)
            dsink_ref[...] = jnp.zeros_like(dsink_ref)

        do = do_ref[...]
        out = o_ref[...]
        lse_all = lse_ref[...]
        k_refs = (kvc_ref, kvp_ref, kvm_ref, kvn_ref)
        dqs, dks, dvs = [], [], []
        for hk in range(N_KV):
            q4, ks, s = _attn_scores(q_ref, k_refs, hk, n, nb)
            vs = [r[:, KV_W + hk * HEAD:KV_W + (hk + 1) * HEAD] for r in k_refs]
            lse4 = jnp.concatenate([lse_all[:, hk * GROUP + g:hk * GROUP + g + 1] for g in range(GROUP)], axis=0)
            do4 = _stack_heads(do, hk)
            delta = jnp.sum(do4.astype(F32) * _stack_heads(out, hk).astype(F32), axis=-1, keepdims=True)
            p = [jnp.exp((t - lse4).astype(BF16)) for t in s]
            ds = [t * (_dot_nt(do4, v) - delta).astype(BF16) for t, v in zip(p, vs)]
            dq4 = _dot(ds[0], ks[0])
            for t, k in zip(ds[1:], ks[1:]):
                dq4 = dq4 + _dot(t, k)
            dq4 = dq4 * SCALE
            dqs += [dq4[g * BLK:(g + 1) * BLK, :] for g in range(GROUP)]
            dks.append([_dot_tn(t, q4) for t in ds])
            dvs.append([_dot_tn(t, do4) for t in p])
            ps = jnp.exp(_sink_rows(sink_ref, hk) - lse4) * delta
            lane = lax.broadcasted_iota(jnp.int32, (1, LANES), 1)
            for g in range(GROUP):
                part = -jnp.sum(ps[g * BLK:(g + 1) * BLK, :], axis=0, keepdims=True)
                dsink_ref[0:1, :] += jnp.where(lane == hk * GROUP + g, part, 0.0)
        dq_ref[...] = jnp.concatenate(dqs, axis=1)

        def piece(i):
            return jnp.concatenate([dks[0][i], dks[1][i], dvs[0][i], dvs[1][i]], axis=1)

        dkvc_ref[...] += piece(0)
        starts = (jnp.maximum(n - 1, 0), n, jnp.minimum(n + 1, nb - 1))
        for i, st in enumerate(starts):
            r = pl.ds(pl.multiple_of(st * BLK, BLK), BLK)
            dkv_ref[r, :] += piece(i + 1)
        _host_finish(n, nb - 1, comm)

    out = pl.pallas_call(
        body, name="attn_bwd", grid=(nb,),
        in_specs=[_rows(BLK, Q_W)] + _kv_specs(nb) + [_acc((n_ctx, 2 * KV_W)), pl.BlockSpec(memory_space=pltpu.SMEM),
                                                      _rows(BLK, Q_W), _rows(BLK, Q_W), _rows(BLK, LANES)] + _comm_specs(ns),
        out_specs=[_rows(BLK, Q_W), _acc((seq, 2 * KV_W)), _acc((n_ctx, 2 * KV_W)), _acc((8, LANES))] + _comm_specs(ns),
        out_shape=[jax.ShapeDtypeStruct((seq, Q_W), F32), jax.ShapeDtypeStruct((seq, 2 * KV_W), F32),
                   jax.ShapeDtypeStruct((n_ctx, 2 * KV_W), F32), jax.ShapeDtypeStruct((8, LANES), F32)]
        + [jax.ShapeDtypeStruct(v.shape, v.dtype) for v in scatter],
        scratch_shapes=_comm_scratch(ns) if ns else [],
        compiler_params=_cp("arbitrary"),
    )(q, kv, kv, kv, kvc, sink, dya, ya, lse, *scatter)
    return out[:4], out[4:]


def _proj_bwd(dq, dkv, dpb, x, dr1, modx, w_in, cos, sin, tm, scatter=()):
    seq = x.shape[0]
    pw = IN_W - Q_W - 2 * KV_W
    ns = len(scatter)

    def body(dq_ref, dkv_ref, dpb_ref, x_ref, dr1_ref, mod_ref, w_ref, cos_ref, sin_ref, *rest):
        dqkv_ref, gx_ref, st_ref = rest[ns:ns + 3]
        comm = _AllToAll(rest[:ns], rest[ns + 3:2 * ns + 3], *rest[2 * ns + 3:]) if ns else None
        _host_start(pl.program_id(0), comm)

        @pl.when(pl.program_id(0) == 0)
        def _():
            st_ref[...] = jnp.zeros_like(st_ref)

        cos1, sin1 = cos_ref[...], sin_ref[...]
        cos2 = jnp.concatenate([cos1, cos1], axis=1)
        sin2 = jnp.concatenate([sin1, sin1], axis=1)
        for j in range(Q_W // 256):
            cs = slice(256 * j, 256 * (j + 1))
            dqkv_ref[:, cs] = _unrope(dq_ref[:, cs], cos2, sin2).astype(BF16)
        dqkv_ref[:, Q_W:Q_W + KV_W] = _unrope(dkv_ref[:, :KV_W], cos1, sin1).astype(BF16)
        dqkv_ref[:, Q_W + KV_W:] = dkv_ref[:, KV_W:].astype(BF16)
        o = Q_W + 2 * KV_W
        dh = _dot_nt(dqkv_ref[...], w_ref[:, :o]) + _dot_nt(dpb_ref[...], w_ref[:, o:])
        xhat, rstd = _ln(x_ref[...])
        st_ref[0:1, :] += _colsum(dh)
        st_ref[1:2, :] += _colsum(dh * xhat)
        gx_ref[...] = _ln_bwd(dh * (1.0 + mod_ref[1:2, :]), xhat, rstd) + ALPHA * dr1_ref[...]
        _host_finish(pl.program_id(0), seq // tm - 1, comm)

    out = pl.pallas_call(
        body, name="proj_bwd", grid=(seq // tm,),
        in_specs=[_rows(tm, Q_W), _rows(tm, 2 * KV_W), _rows(tm, pw), _rows(tm, D), _rows(tm, D), _acc((8, D)),
                  _resident((D, IN_W)), _rows(tm, LANES), _rows(tm, LANES)] + _comm_specs(ns),
        out_specs=[_rows(tm, Q_W + 2 * KV_W), _rows(tm, D), _acc((8, D))] + _comm_specs(ns),
        out_shape=[jax.ShapeDtypeStruct((seq, Q_W + 2 * KV_W), BF16), jax.ShapeDtypeStruct((seq, D), F32),
                   jax.ShapeDtypeStruct((8, D), F32)] + [jax.ShapeDtypeStruct(v.shape, v.dtype) for v in scatter],
        scratch_shapes=_comm_scratch(ns) if ns else [],
        compiler_params=_cp("arbitrary"),
    )(dq, dkv, dpb, x, dr1, modx, w_in, cos, sin, *scatter)
    return out[:3], out[3:]


def _ctx_bwd(dkvc, ctx, hc, w_kv):
    n_ctx = ctx.shape[0]

    def body(dkvc_ref, ctx_ref, hc_ref, w_ref, dw_ref, st_ref):
        d = dkvc_ref[...].astype(BF16)
        dw_ref[...] = _dot_tn(hc_ref[...], d)
        dhc = _dot_nt(d, w_ref[...])
        xhat, _ = _ln(ctx_ref[...])
        st_ref[...] = jnp.zeros_like(st_ref)
        st_ref[0:1, :] = _colsum(dhc)
        st_ref[1:2, :] = _colsum(dhc * xhat)

    return pl.pallas_call(
        body, name="ctx_bwd", grid=(1,),
        in_specs=[_acc((n_ctx, 2 * KV_W)), _acc((n_ctx, D)), _acc((n_ctx, D)), _acc((D, 2 * KV_W))],
        out_specs=[_acc((D, 2 * KV_W)), _acc((8, D))],
        out_shape=[jax.ShapeDtypeStruct((D, 2 * KV_W), F32), jax.ShapeDtypeStruct((8, D), F32)],
        compiler_params=_cp("arbitrary"),
    )(dkvc, ctx, hc, w_kv)


def _tn_matmul(a, b, tn, name, out_dtype, shard_major=False, init=None, tk=512):
    t, ka = a.shape
    n = b.shape[1]
    tk = min(tk, t)
    nk = t // tk
    has_init = init is not None

    def body(*refs):
        if has_init:
            a_ref, b_ref, i_ref, o_ref, acc_ref = refs
        else:
            a_ref, b_ref, o_ref, acc_ref = refs
        k = pl.program_id(1)

        @pl.when(k == 0)
        def _():
            acc_ref[...] = i_ref[...] if has_init else jnp.zeros_like(acc_ref)

        acc_ref[...] += _dot_tn(a_ref[...], b_ref[...])

        @pl.when(k == nk - 1)
        def _():
            o_ref[...] = acc_ref[...].astype(out_dtype)

    in_specs = [pl.BlockSpec((tk, ka), lambda j, k: (k, 0)), pl.BlockSpec((tk, tn), lambda j, k: (k, j))]
    args = [a, b]
    if has_init:
        in_specs.append(pl.BlockSpec((ka, tn), lambda j, k: (0, j)))
        args.append(init)
    if shard_major:
        out_spec = pl.BlockSpec((None, ka, tn), lambda j, k: (j, 0, 0))
        out_shape = jax.ShapeDtypeStruct((n // tn, ka, tn), out_dtype)
    else:
        out_spec = pl.BlockSpec((ka, tn), lambda j, k: (0, j))
        out_shape = jax.ShapeDtypeStruct((ka, n), out_dtype)
    return pl.pallas_call(
        body, name=name, grid=(n // tn, nk), in_specs=in_specs, out_specs=out_spec, out_shape=out_shape,
        scratch_shapes=[pltpu.VMEM((ka, tn), F32)],
        compiler_params=_cp("arbitrary", "arbitrary"),
    )(*args)


ADA_TILE = 512


def _ada_fwd(sc_all, w_ada):
    cs = w_ada.shape[1]

    def body(s_ref, w_ref, o_ref):
        o_ref[...] = _dot(s_ref[...].astype(BF16), w_ref[...].astype(BF16))

    return pl.pallas_call(
        body, name="ada_fwd", grid=(cs // ADA_TILE,),
        in_specs=[_acc((16, D)), pl.BlockSpec((D, ADA_TILE), lambda j: (0, j))],
        out_specs=pl.BlockSpec((16, ADA_TILE), lambda j: (0, j)),
        out_shape=jax.ShapeDtypeStruct((16, cs), F32),
        compiler_params=_cp("arbitrary"),
    )(sc_all, w_ada)


def _ada_bwd(sc_all_t, dm_all, dmc, w_ada):
    cs = w_ada.shape[1]

    def body(st_ref, dm_ref, dmc_ref, w_ref, gw_ref, part_ref):
        @pl.when(pl.program_id(0) == 0)
        def _():
            part_ref[...] = jnp.zeros_like(part_ref)

        gw_ref[...] = _dot(st_ref[...].astype(BF16), dm_ref[...].astype(BF16))
        part_ref[...] += _dot_nt(dmc_ref[...].astype(BF16), w_ref[...].astype(BF16))

    return pl.pallas_call(
        body, name="ada_bwd", grid=(cs // ADA_TILE,),
        in_specs=[_acc((D, 16)), pl.BlockSpec((16, ADA_TILE), lambda j: (0, j)), pl.BlockSpec((8, ADA_TILE), lambda j: (0, j)),
                  pl.BlockSpec((D, ADA_TILE), lambda j: (0, j))],
        out_specs=[pl.BlockSpec((D, ADA_TILE), lambda j: (0, j)), _acc((8, D))],
        out_shape=[jax.ShapeDtypeStruct((D, cs), F32), jax.ShapeDtypeStruct((8, D), F32)],
        compiler_params=_cp("arbitrary"),
    )(sc_all_t, dm_all, dmc, w_ada)


def _sum8(x, name, tr=256):
    _, r, c = x.shape
    tr = min(tr, r)
    while r % tr:
        tr -= 16

    def body(x_ref, o_ref):
        acc = x_ref[0].astype(F32)
        for i in range(1, N_DEV):
            acc = acc + x_ref[i].astype(F32)
        o_ref[...] = acc

    return pl.pallas_call(
        body, name=name, grid=(r // tr,),
        in_specs=[pl.BlockSpec((N_DEV, tr, c), lambda i: (0, i, 0))],
        out_specs=pl.BlockSpec((tr, c), lambda i: (i, 0)),
        out_shape=jax.ShapeDtypeStruct((r, c), F32),
        compiler_params=_cp("arbitrary"),
    )(x)


def _sum_blocks(recv, src, me, name, tr=256):
    _, r, c = recv.shape
    tr = min(tr, r)
    while r % tr:
        tr -= 16

    def body(me_ref, recv_ref, own_ref, o_ref):
        acc = own_ref[...].astype(F32)
        for k in range(1, N_DEV):
            acc = acc + recv_ref[me_ref[0] ^ k].astype(F32)
        o_ref[...] = acc

    return pl.pallas_call(
        body, name=name,
        grid_spec=pltpu.PrefetchScalarGridSpec(
            num_scalar_prefetch=1, grid=(r // tr,),
            in_specs=[pl.BlockSpec((N_DEV, tr, c), lambda i, me_ref: (0, i, 0)),
                      pl.BlockSpec((None, tr, c), lambda i, me_ref: (me_ref[0], i, 0))],
            out_specs=pl.BlockSpec((tr, c), lambda i, me_ref: (i, 0))),
        out_shape=jax.ShapeDtypeStruct((r, c), F32),
        compiler_params=_cp("arbitrary"),
    )(me, recv, src)


def _sum8_many(xs, name):
    n = len(xs)

    def body(*refs):
        for x_ref, o_ref in zip(refs[:n], refs[n:]):
            acc = x_ref[0]
            for i in range(1, N_DEV):
                acc = acc + x_ref[i]
            o_ref[...] = acc

    vmem = pl.BlockSpec(memory_space=pltpu.VMEM)
    return pl.pallas_call(
        body, name=name, in_specs=[vmem] * n, out_specs=[vmem] * n,
        out_shape=[jax.ShapeDtypeStruct(v.shape[1:], v.dtype) for v in xs],
        compiler_params=pltpu.CompilerParams(vmem_limit_bytes=VMEM_LIMIT),
    )(*xs)


def _adam_update(w, g, m, v):
    nm = ADAM_B1 * m + (1.0 - ADAM_B1) * g
    nv = ADAM_B2 * v + (1.0 - ADAM_B2) * (g * g)
    m_hat = nm / (1.0 - ADAM_B1 ** ADAM_STEP)
    v_hat = nv / (1.0 - ADAM_B2 ** ADAM_STEP)
    return -ADAM_LR * (m_hat / (jnp.sqrt(v_hat) + ADAM_EPS) + ADAM_WD * w), nm, nv


ROW_LOSS, ROW_LN2_G, ROW_LN2_B, ROW_LN1_G, ROW_LN1_B = 0, 1, 2, 10, 11
ROWS_DMOD_X = (16, 17, 12, 9, 8, 3)
ROWS_DMOD_C = (24, 25)
SMALL = ("c_ctx", "b_ada", "attn_sink", "gmlp_ln_g", "gmlp_ln_b", "w_spatial", "b_spatial", "ln1_g", "ln1_b", "ln2_g", "ln2_b")


def _adamw_small(sums, dsc, w, m, v):
    n = len(SMALL)

    def body(*refs):
        st_ref, gm_ref, sk_ref, ws_ref, bs_ref, dsc_ref = refs[:6]
        w_refs = dict(zip(SMALL, refs[6:6 + n]))
        m_refs = dict(zip(SMALL, refs[6 + n:6 + 2 * n]))
        v_refs = dict(zip(SMALL, refs[6 + 2 * n:6 + 3 * n]))
        outs = refs[6 + 3 * n:]
        c = w_refs["c_ctx"][...]
        sg = _sig(c)
        dmod = [st_ref[r:r + 1, :] for r in ROWS_DMOD_X]
        dmod[0] = dmod[0] + st_ref[ROWS_DMOD_C[0]:ROWS_DMOD_C[0] + 1, :]
        dmod[1] = dmod[1] + st_ref[ROWS_DMOD_C[1]:ROWS_DMOD_C[1] + 1, :]
        grads = dict(
            c_ctx=dsc_ref[0:1, :] * (sg * (1.0 + c * (1.0 - sg))),
            b_ada=jnp.concatenate(dmod, axis=1),
            attn_sink=sk_ref[0:1, 0:N_KV * GROUP],
            gmlp_ln_g=gm_ref[0:1, :], gmlp_ln_b=gm_ref[1:2, :],
            w_spatial=ws_ref[...], b_spatial=bs_ref[...],
            ln1_g=st_ref[ROW_LN1_G:ROW_LN1_G + 1, :], ln1_b=st_ref[ROW_LN1_B:ROW_LN1_B + 1, :],
            ln2_g=st_ref[ROW_LN2_G:ROW_LN2_G + 1, :], ln2_b=st_ref[ROW_LN2_B:ROW_LN2_B + 1, :])
        for i, name in enumerate(SMALL):
            g = grads[name]
            d, nm, nv = _adam_update(w_refs[name][...], g, m_refs[name][...], v_refs[name][...])
            outs[i][...] = g
            outs[n + i][...] = d
            outs[2 * n + i][...] = nm
            outs[3 * n + i][...] = nv

    vmem = pl.BlockSpec(memory_space=pltpu.VMEM)
    args = list(sums) + [dsc] + [w[k] for k in SMALL] + [m[k] for k in SMALL] + [v[k] for k in SMALL]
    shapes = [jax.ShapeDtypeStruct(w[k].shape, F32) for k in SMALL]
    out = pl.pallas_call(
        body, name="adamw_small", in_specs=[vmem] * len(args), out_specs=[vmem] * (4 * n), out_shape=shapes * 4,
        compiler_params=pltpu.CompilerParams(vmem_limit_bytes=VMEM_LIMIT),
    )(*args)
    return [dict(zip(SMALL, out[i * n:(i + 1) * n])) for i in range(4)]


def _adamw_halves(w, mine, theirs, m, v, c_arr, name):
    r, c = w.shape
    tr = min(256, r // 2)
    while (r // 2) % tr:
        tr -= 8
    nt = (r // 2) // tr

    def body(c_ref, w_ref, mine_ref, theirs_ref, m_ref, v_ref, g_ref, d_ref, nm_ref, nv_ref):
        g = jnp.where(pl.program_id(0) == c_ref[0], mine_ref[...], theirs_ref[...])
        g_ref[...] = g
        d_ref[...], nm_ref[...], nv_ref[...] = _adam_update(w_ref[...], g, m_ref[...], v_ref[...])

    whole = pl.BlockSpec((tr, c), lambda hb, i, c_ref: (hb * nt + i, 0))
    half = pl.BlockSpec((tr, c), lambda hb, i, c_ref: (i, 0))
    shp = jax.ShapeDtypeStruct((r, c), F32)
    return pl.pallas_call(
        body, name=name,
        grid_spec=pltpu.PrefetchScalarGridSpec(
            num_scalar_prefetch=1, grid=(2, nt), in_specs=[whole, half, half, whole, whole], out_specs=[whole] * 4),
        out_shape=[shp] * 4,
        compiler_params=_cp("arbitrary", "arbitrary"),
    )(c_arr, w, mine, theirs, m, v)


def _adamw(w, g, m, v, name):
    r, c = w.shape
    tr = r if r * c <= 256 * 1024 else min(256, r)
    while r % tr:
        tr -= 8

    def body(w_ref, g_ref, m_ref, v_ref, d_ref, nm_ref, nv_ref):
        d_ref[...], nm_ref[...], nv_ref[...] = _adam_update(w_ref[...], g_ref[...], m_ref[...], v_ref[...])

    spec = pl.BlockSpec((tr, c), lambda i: (i, 0))
    shp = jax.ShapeDtypeStruct((r, c), F32)
    return pl.pallas_call(
        body, name=name, grid=(r // tr,), in_specs=[spec] * 4, out_specs=[spec] * 3, out_shape=[shp] * 3,
        compiler_params=_cp("arbitrary"),
    )(w, g, m, v)


def _my_pos():
    return lax.axis_index("x"), lax.axis_index("y"), lax.axis_index("c")


N_COPY = 7


class _Gather:
    def __init__(self, x_refs, out_refs, send_sems, recv_sems):
        self.x_refs, self.out_refs = x_refs, out_refs
        self.send_sems, self.recv_sems = send_sems, recv_sems
        x, y, c = _my_pos()
        self.c = c
        self.me, self.sibling = (x, y, c), (x, y, 1 - c)
        self.chips = [(1 - x, y), (x, 1 - y), (1 - x, 1 - y)]

    def _copy(self, a, k, block, to, from_input=False):
        px, py, pc = block
        rows = self.out_refs[a].at[4 * px + 2 * py + pc]
        return pltpu.make_async_remote_copy(
            src_ref=self.x_refs[a] if from_input else rows, dst_ref=rows,
            send_sem=self.send_sems.at[a * N_COPY + k], recv_sem=self.recv_sems.at[a * N_COPY + k],
            device_id=to, device_id_type=MESH)

    def start(self):
        n = len(self.x_refs)
        for a in range(n):
            self._copy(a, 0, self.me, self.sibling, from_input=True).start()
        for j, chip in enumerate(self.chips):
            for a in range(n):
                self._copy(a, 1 + j, self.me, (*chip, self.c), from_input=True).start()

    def finish(self):
        n = len(self.x_refs)
        c = self.c
        for j, chip in enumerate(self.chips):
            for a in range(n):
                self._copy(a, 1 + j, (*chip, c), self.me).wait_recv()
                self._copy(a, 4 + j, (*chip, c), self.sibling).start()
        for a in range(n):
            self._copy(a, 0, self.sibling, self.me).wait_recv()
        for j, chip in enumerate(self.chips):
            for a in range(n):
                self._copy(a, 4 + j, (*chip, 1 - c), self.me).wait_recv()
        for a in range(n):
            self._copy(a, 0, self.me, self.sibling, from_input=True).wait_send()
            for j, chip in enumerate(self.chips):
                self._copy(a, 1 + j, self.me, (*chip, c), from_input=True).wait_send()
                self._copy(a, 4 + j, (*chip, c), self.sibling).wait_send()


def _comm_scratch(n):
    return [pltpu.SemaphoreType.DMA((n * N_COPY,)), pltpu.SemaphoreType.DMA((n * N_COPY,))]


def _comm_specs(n):
    return [pl.BlockSpec(memory_space=pl.ANY)] * n


def _gathered_shapes(xs):
    return [jax.ShapeDtypeStruct((N_DEV,) + v.shape, v.dtype) for v in xs]


def _with_own(gathered, xs, me):
    return [lax.dynamic_update_index_in_dim(g, v, me, 0) for g, v in zip(gathered, xs)]


def _all_gather(xs, me, name):
    n = len(xs)

    def body(*refs):
        g = _Gather(refs[:n], refs[n:2 * n], *refs[2 * n:])
        g.start()
        g.finish()

    out = pl.pallas_call(
        body, name=name, out_shape=_gathered_shapes(xs), in_specs=_comm_specs(n), out_specs=_comm_specs(n),
        scratch_shapes=_comm_scratch(n),
    )(*xs)
    return _with_own(out, xs, me)


class _AllToAll:
    def __init__(self, x_refs, out_refs, send_sems, recv_sems):
        self.x_refs, self.out_refs = x_refs, out_refs
        self.send_sems, self.recv_sems = send_sems, recv_sems
        self.pos = _my_pos()
        x, y, c = self.pos
        self.me = 4 * x + 2 * y + c

    def _peer(self, k):
        x, y, c = self.pos
        return (x ^ ((k >> 2) & 1), y ^ ((k >> 1) & 1), c ^ (k & 1))

    def _copy(self, a, k):
        p = self._peer(k)
        return pltpu.make_async_remote_copy(
            src_ref=self.x_refs[a].at[4 * p[0] + 2 * p[1] + p[2]], dst_ref=self.out_refs[a].at[self.me],
            send_sem=self.send_sems.at[a * N_COPY + k - 1], recv_sem=self.recv_sems.at[a * N_COPY + k - 1],
            device_id=p, device_id_type=MESH)

    def start(self):
        for k in range(1, N_DEV):
            for a in range(len(self.x_refs)):
                self._copy(a, k).start()

    def finish(self):
        for a in range(len(self.x_refs)):
            for k in range(1, N_DEV):
                self._copy(a, k).wait_recv()
            for k in range(1, N_DEV):
                self._copy(a, k).wait_send()


def _all_to_all(blocks, name):
    n = len(blocks)

    def body(*refs):
        t = _AllToAll(refs[:n], refs[n:2 * n], *refs[2 * n:])
        t.start()
        t.finish()

    return pl.pallas_call(
        body, name=name, out_shape=[jax.ShapeDtypeStruct(v.shape, v.dtype) for v in blocks],
        in_specs=_comm_specs(n), out_specs=_comm_specs(n), scratch_shapes=_comm_scratch(n),
    )(*blocks)


def _sibling_exchange(xs, name):
    n = len(xs)

    def body(*refs):
        x_refs, out_refs = refs[:n], refs[n:2 * n]
        send_sems, recv_sems = refs[2 * n:]
        x, y, c = _my_pos()

        def push(a):
            return pltpu.make_async_remote_copy(
                src_ref=x_refs[a], dst_ref=out_refs[a], send_sem=send_sems.at[a], recv_sem=recv_sems.at[a],
                device_id=(x, y, 1 - c), device_id_type=MESH)

        for a in range(n):
            push(a).start()
        for a in range(n):
            push(a).wait_recv()
            push(a).wait_send()

    return pl.pallas_call(
        body, name=name, out_shape=[jax.ShapeDtypeStruct(v.shape, v.dtype) for v in xs],
        in_specs=_comm_specs(n), out_specs=_comm_specs(n),
        scratch_shapes=[pltpu.SemaphoreType.DMA((n,)), pltpu.SemaphoreType.DMA((n,))],
    )(*xs)


def _scatter_and_gather(scatter, gather, name):
    ns, ng = len(scatter), len(gather)

    def body(*refs):
        s_in, g_in = refs[:ns], refs[ns:ns + ng]
        s_out, g_out = refs[ns + ng:2 * ns + ng], refs[2 * ns + ng:2 * (ns + ng)]
        s_send, s_recv, g_send, g_recv = refs[2 * (ns + ng):]
        g = _Gather(g_in, g_out, g_send, g_recv)
        t = _AllToAll(s_in, s_out, s_send, s_recv)
        g.start()
        t.start()
        g.finish()
        t.finish()

    out = pl.pallas_call(
        body, name=name,
        out_shape=[jax.ShapeDtypeStruct(v.shape, v.dtype) for v in scatter] + _gathered_shapes(gather),
        in_specs=_comm_specs(ns + ng), out_specs=_comm_specs(ns + ng),
        scratch_shapes=_comm_scratch(ns) + _comm_scratch(ng),
    )(*scatter, *gather)
    return out[:ns], out[ns:]


def _row_tile(seq, want):
    return min(want, seq)


def _local_step(x, ctx, tgt, mod_x, mod_c, wb, sink, gmlp_g, gmlp_b, w_s, b_s, ln1_g, ln1_b, ln2_g, ln2_b,
                later=None, me=None):
    seq = x.shape[0]
    on_mesh = me is not None
    modx1 = jnp.concatenate([mod_x[0:2], jnp.zeros((6, D), F32)], axis=0)
    modc = jnp.concatenate([mod_c[0:2], jnp.zeros((6, D), F32)], axis=0)
    vec = jnp.concatenate([mod_x[2:3], ln1_g, ln1_b, mod_x[3:6], ln2_g, ln2_b], axis=0)
    gp = jnp.concatenate([gmlp_g, gmlp_b, jnp.zeros((6, G_W), F32)], axis=0)
    ws_stack = w_s.reshape(N_GRP * BLK, BLK).astype(BF16)
    ws_stack_t = jnp.transpose(w_s, (2, 0, 1)).reshape(BLK, N_GRP * BLK).astype(BF16)
    bias_full = jnp.repeat(b_s.T, GRP_D, axis=1)
    cos, sin = _rope_tables(seq)
    w_in = wb["w_in"]
    w_kv = w_in[:, Q_W:Q_W + 2 * KV_W]
    tm_big = _row_tile(seq, 512)
    tm_ffn = _row_tile(seq, 256)

    hc, kvc, vac = _ctx_fwd(ctx, modc, w_kv)
    behind_proj = ("w_a", "w_b", "w_o") if on_mesh else ()
    behind_attn = ("w_fi", "w_fo") if on_mesh else ()
    (h, q, kv, va, uv, gab), got_proj = _proj_fwd(x, modx1, w_in, cos, sin, tm_big, gather=[later[n] for n in behind_proj])
    (ya, lse), got_attn = _attn_fwd(q, kv, va, kvc, vac, sink, gather=[later[n] for n in behind_attn])
    if on_mesh:
        wb = dict(wb)
        names = behind_proj + behind_attn
        for n, g in zip(names, _with_own(list(got_proj) + list(got_attn), [later[n] for n in names], me)):
            wb[n] = g.reshape(-1, g.shape[2]) if n in ROW_SHARDED else g.reshape(N_SHARD, 2 * g.shape[1], g.shape[2])
    a, b, mix, merged, yb = _mix_fwd(uv, gab, ya, gp, ws_stack, bias_full, wb["w_a"], wb["w_b"], wb["w_o"], tm_big)
    act, h2, dff, df, dr1, st_ffn = _ffn(x, mix, tgt, vec, wb["w_fi"], wb["w_fo"], tm_ffn)
    g_w_fo = _tn_matmul(act, df, 512, "tn_w_ffn_out", BF16)
    g_w_fi = _tn_matmul(h2, dff, FH_SHARD, "tn_w_ffn_in", BF16, shard_major=True)
    dya, dpb, dws, dbs_full, st4, g_w_o, g_w_a, g_w_b = _mix_bwd(
        dr1, a, b, gab, uv, merged, ya, yb, vec, gp, ws_stack, ws_stack_t, bias_full, wb["w_a"], wb["w_b"], wb["w_o"],
        tm_big)
    blocks = dict(w_fi=_eighths(g_w_fi), w_fo=_eighths(g_w_fo), w_o=_eighths(g_w_o), w_a=_eighths(g_w_a), w_b=_eighths(g_w_b))
    early = tuple(blocks) if on_mesh else ()
    (dq, dkv, dkvc, dsink), recv_early = _attn_bwd(q, kv, kvc, sink, dya, ya, lse, scatter=[blocks[n] for n in early])
    g_wkv_ctx, st0 = _ctx_bwd(dkvc, ctx, hc, w_kv)
    (dqkv, grad_x, st1), _ = _proj_bwd(dq, dkv, dpb, x, dr1, modx1, w_in, cos, sin, tm_big)
    o = Q_W + 2 * KV_W
    init = jnp.pad(g_wkv_ctx, ((0, 0), (Q_W, 0)))
    g_w_in = jnp.concatenate([_tn_matmul(h, dqkv, o, "tn_w_in_qkv", F32, init=init),
                              _tn_matmul(h, dpb, 1536, "tn_w_in_rest", F32)], axis=1)
    g_w_in = g_w_in.reshape(D, N_SHARD, IN_W // N_SHARD).transpose(1, 0, 2).astype(BF16)

    dbs = jnp.sum(dbs_full.reshape(BLK, N_GRP, GRP_D), axis=2).T
    parts = [jnp.concatenate([st_ffn, st1, st0], axis=0), st4, dsink, dws, dbs]
    blocks["w_in"] = _eighths(g_w_in)
    return grad_x, parts, blocks, dict(zip(early, recv_early))


BIG = ("w_in", "w_a", "w_b", "w_o", "w_fi", "w_fo")
ROW_SHARDED = ("w_o", "w_fo")


def _half_of_shard(shard, c):
    r = shard.shape[0]
    return lax.dynamic_slice_in_dim(shard, c * (r // 2), r // 2, axis=0)


def _eighths(v):
    rows = v.shape[-2] * (v.shape[0] if v.ndim == 3 else 1)
    return v.reshape(N_DEV, rows // N_DEV, v.shape[-1])


def kernel(x, c, ctx, c_ctx, w_ada, b_ada, w_in, attn_sink, gmlp_ln_g, gmlp_ln_b, w_spatial, b_spatial, w_branch_a, w_branch_b, w_out, ln1_g, ln1_b, w_ffn_in, w_ffn_out, ln2_g, ln2_b, loss_target, m_c_ctx, m_w_ada, m_b_ada, m_w_in, m_attn_sink, m_gmlp_ln_g, m_gmlp_ln_b, m_w_spatial, m_b_spatial, m_w_branch_a, m_w_branch_b, m_w_out, m_ln1_g, m_ln1_b, m_w_ffn_in, m_w_ffn_out, m_ln2_g, m_ln2_b, v_c_ctx, v_w_ada, v_b_ada, v_w_in, v_attn_sink, v_gmlp_ln_g, v_gmlp_ln_b, v_w_spatial, v_b_spatial, v_w_branch_a, v_w_branch_b, v_w_out, v_ln1_g, v_ln1_b, v_w_ffn_in, v_w_ffn_out, v_ln2_g, v_ln2_b):
    mx, my, mc = _my_pos()
    me = 4 * mx + 2 * my + mc
    chip = 2 * mx + my
    shards = dict(w_in=w_in[0], w_a=w_branch_a[0], w_b=w_branch_b[0], w_o=w_out[0], w_fi=w_ffn_in[0], w_fo=w_ffn_out[0])

    halves = {n: _half_of_shard(shards[n], mc).astype(BF16) for n in BIG}
    c_rows = jnp.concatenate([c, jnp.zeros((7, D), F32)], axis=0)
    g_in, c_g = _all_gather([halves["w_in"], c_rows], me, "gather_w_in")
    r, cdim = shards["w_in"].shape
    wb = dict(w_in=g_in.reshape(N_SHARD, r, cdim).transpose(1, 0, 2).reshape(r, N_SHARD * cdim))

    c_all = c_g[:, 0, :]
    cc = jnp.concatenate([c_all, c_ctx[None, :], jnp.zeros((7, D), F32)], axis=0)
    sig_cc = jax.nn.sigmoid(cc)
    sc_all = cc * sig_cc
    mod_shard = _ada_fwd(sc_all, w_ada[0])
    mod_g = _all_gather([mod_shard], me, "gather_mod")[0]
    mod_all = jnp.concatenate([mod_g[2 * s] for s in range(4)], axis=1) + b_ada
    mod_x = lax.dynamic_slice_in_dim(mod_all, me, 1, axis=0).reshape(6, D)
    mod_c = mod_all[8].reshape(6, D)[0:2]

    grad_x, parts, blocks, recv = _local_step(
        x[0], ctx[0], loss_target[0], mod_x, mod_c, wb, attn_sink, gmlp_ln_g, gmlp_ln_b, w_spatial[0], b_spatial[0],
        ln1_g, ln1_b, ln2_g, ln2_b, later=halves, me=me)

    (recv["w_in"],), gathered = _scatter_and_gather([blocks["w_in"]], parts, "scatter_w_in_gather_small")
    gathered = _with_own(gathered, parts, me)

    me_arr = jnp.reshape(me, (1,)).astype(jnp.int32)
    summed = {n: _sum_blocks(recv[n], blocks[n], me_arr, "sum_grads_" + n) for n in BIG}
    theirs = dict(zip(BIG, _sibling_exchange([summed[n] for n in BIG], "exchange_grads")))

    sums = _sum8_many(gathered, "sum_small")
    stats = sums[0]
    loss = 0.5 * jnp.sum(stats[ROW_LOSS]) / D
    dmod_x_all = jnp.concatenate([gathered[0][:, r_, :] for r_ in ROWS_DMOD_X], axis=1)
    dmod_c_full = jnp.concatenate([stats[r_] for r_ in ROWS_DMOD_C] + [jnp.zeros((4 * D,), F32)])
    dm_rows = jnp.concatenate([dmod_x_all, dmod_c_full[None, :], jnp.zeros((7, 6 * D), F32)], axis=0)
    cs = w_ada.shape[2]
    dm_shard = lax.dynamic_slice_in_dim(dm_rows, chip * cs, cs, axis=1)
    dmc_shard = jnp.concatenate([dm_shard[8:9], jnp.zeros((7, cs), F32)], axis=0)
    g_w_ada, part = _ada_bwd(sc_all.T, dm_shard, dmc_shard, w_ada[0])
    part_all = _all_gather([part * (mc == 0).astype(F32)], me, "gather_c_ctx")[0]
    dsc = _sum8(part_all, "sum_c_ctx")

    grads = dict(w_ada=g_w_ada[None])
    weights = dict(c_ctx=c_ctx, w_ada=w_ada, b_ada=b_ada, w_in=w_in, attn_sink=attn_sink, gmlp_ln_g=gmlp_ln_g,
                   gmlp_ln_b=gmlp_ln_b, w_spatial=w_spatial, b_spatial=b_spatial, w_branch_a=w_branch_a,
                   w_branch_b=w_branch_b, w_out=w_out, ln1_g=ln1_g, ln1_b=ln1_b, w_ffn_in=w_ffn_in, w_ffn_out=w_ffn_out,
                   ln2_g=ln2_g, ln2_b=ln2_b)
    ms = dict(c_ctx=m_c_ctx, w_ada=m_w_ada, b_ada=m_b_ada, w_in=m_w_in, attn_sink=m_attn_sink, gmlp_ln_g=m_gmlp_ln_g,
              gmlp_ln_b=m_gmlp_ln_b, w_spatial=m_w_spatial, b_spatial=m_b_spatial, w_branch_a=m_w_branch_a,
              w_branch_b=m_w_branch_b, w_out=m_w_out, ln1_g=m_ln1_g, ln1_b=m_ln1_b, w_ffn_in=m_w_ffn_in,
              w_ffn_out=m_w_ffn_out, ln2_g=m_ln2_g, ln2_b=m_ln2_b)
    vs = dict(c_ctx=v_c_ctx, w_ada=v_w_ada, b_ada=v_b_ada, w_in=v_w_in, attn_sink=v_attn_sink, gmlp_ln_g=v_gmlp_ln_g,
              gmlp_ln_b=v_gmlp_ln_b, w_spatial=v_w_spatial, b_spatial=v_b_spatial, w_branch_a=v_w_branch_a,
              w_branch_b=v_w_branch_b, w_out=v_w_out, ln1_g=v_ln1_g, ln1_b=v_ln1_b, w_ffn_in=v_w_ffn_in,
              w_ffn_out=v_w_ffn_out, ln2_g=v_ln2_g, ln2_b=v_ln2_b)
    order = list(weights)
    delta, new_m, new_v = {}, {}, {}
    d_, m_, v_ = _adamw(w_ada[0], g_w_ada, m_w_ada[0], v_w_ada[0], "adamw_w_ada")
    delta["w_ada"], new_m["w_ada"], new_v["w_ada"] = d_[None], m_[None], v_[None]
    c_arr = jnp.reshape(mc, (1,)).astype(jnp.int32)
    names = dict(w_in="w_in", w_a="w_branch_a", w_b="w_branch_b", w_o="w_out", w_fi="w_ffn_in", w_fo="w_ffn_out")
    for k, n in names.items():
        g_, d_, m_, v_ = _adamw_halves(weights[n][0], summed[k], theirs[k], ms[n][0], vs[n][0], c_arr, "adamw_" + n)
        grads[n], delta[n], new_m[n], new_v[n] = g_[None], d_[None], m_[None], v_[None]

    def view(a):
        return a.reshape(-1, a.shape[-1]) if a.ndim != 1 else a.reshape(1, -1)

    small = _adamw_small(sums, dsc, *[{n: view(d[n]) for n in SMALL} for d in (weights, ms, vs)])
    for out, src in zip((grads, delta, new_m, new_v), small):
        for n in SMALL:
            out[n] = src[n].reshape(weights[n].shape)

    return (loss, grad_x[None], *[grads[n] for n in order], *[delta[n] for n in order],
            *[new_m[n] for n in order], *[new_v[n] for n in order])
```

```python
import functools
import math

import jax
import jax.numpy as jnp
from jax import lax
from jax.experimental import pallas as pl
from jax.experimental.pallas import tpu as pltpu

F32 = jnp.float32
BF16 = jnp.bfloat16

D = 1024
HEAD = 64
N_KV = 2
GROUP = 4
Q_W = 512
KV_W = 128
G_W = 512
BLK = 128
N_GRP = 8
GRP_D = 64
FH = 2816
IN_W = 3840
GRID_W = 64
ROPE_BASE = 10000.0
LN_EPS = 1e-5
NEG = -1e30
ALPHA = (2 * 1) ** 0.25
SCALE = HEAD ** -0.5
GELU_K = math.sqrt(2.0 / math.pi)
GELU_A = 0.044715
ADAM_LR = 0.001
ADAM_B1 = 0.9
ADAM_B2 = 0.999
ADAM_EPS = 1e-08
ADAM_WD = 0.01
ADAM_STEP = 10
N_DEV = 8
N_SHARD = 4
FH_SHARD = FH // 2
LANES = 128
VMEM_LIMIT = 56 * 1024 * 1024
MESH = pl.DeviceIdType.MESH


def _cp(*sem):
    return pltpu.CompilerParams(dimension_semantics=sem, vmem_limit_bytes=VMEM_LIMIT)


def _resident(shape):
    return pl.BlockSpec(shape, lambda *_: (0,) * len(shape), pipeline_mode=pl.Buffered(1))


def _rows(tm, width):
    return pl.BlockSpec((tm, width), lambda i: (i, 0))


def _acc(shape):
    return pl.BlockSpec(shape, lambda *_: (0,) * len(shape))


def _dot(a, b):
    return jnp.dot(a, b, preferred_element_type=F32)


def _dot_nt(a, b):
    return lax.dot_general(a, b, (((1,), (1,)), ((), ())), preferred_element_type=F32)


def _dot_tn(a, b):
    return lax.dot_general(a, b, (((0,), (0,)), ((), ())), preferred_element_type=F32)


def _ln(x):
    mu = jnp.mean(x, axis=-1, keepdims=True)
    xc = x - mu
    var = jnp.mean(xc * xc, axis=-1, keepdims=True)
    rstd = lax.rsqrt(var + LN_EPS)
    return xc * rstd, rstd


def _ln_bwd(dxhat, xhat, rstd):
    return (dxhat - jnp.mean(dxhat, axis=-1, keepdims=True)
            - xhat * jnp.mean(dxhat * xhat, axis=-1, keepdims=True)) * rstd


def _sig(x):
    return 1.0 / (1.0 + jnp.exp(-x))


def _gelu(x):
    t = jnp.tanh(GELU_K * (x + GELU_A * x * x * x))
    return 0.5 * x * (1.0 + t), t


def _gelu_grad(x, t):
    return 0.5 * (1.0 + t) + 0.5 * x * (1.0 - t * t) * GELU_K * (1.0 + 3.0 * GELU_A * x * x)


def _colsum(v):
    return jnp.sum(v, axis=0, keepdims=True)


def _partner(x):
    w = x.shape[1]
    lane = lax.broadcasted_iota(jnp.int32, x.shape, 1)
    return jnp.where((lane & 31) < 16, pltpu.roll(x, w - 16, 1), pltpu.roll(x, 16, 1))


def _rope(x, cos, sin):
    return x * cos + _partner(x) * sin


def _unrope(g, cos, sin):
    return g * cos + _partner(g * sin)


def _rope_tables(seq):
    inv = ROPE_BASE ** (-jnp.arange(HEAD // 4, dtype=F32) / (HEAD // 4))
    pos = jnp.arange(seq, dtype=jnp.int32)
    ar = (pos // GRID_W).astype(F32)[:, None] * inv
    ac = (pos % GRID_W).astype(F32)[:, None] * inv
    cos = jnp.concatenate([jnp.cos(ar), jnp.cos(ar), jnp.cos(ac), jnp.cos(ac)], axis=-1)
    sin = jnp.concatenate([-jnp.sin(ar), jnp.sin(ar), -jnp.sin(ac), jnp.sin(ac)], axis=-1)
    return jnp.tile(cos, (1, LANES // HEAD)), jnp.tile(sin, (1, LANES // HEAD))


def _ctx_fwd(ctx, modc, w_kv):
    n_ctx = ctx.shape[0]

    def body(ctx_ref, mod_ref, w_ref, hc_ref, kvc_ref, vac_ref):
        xhat, _ = _ln(ctx_ref[...])
        hc = (xhat * (1.0 + mod_ref[1:2, :]) + mod_ref[0:1, :]).astype(BF16)
        hc_ref[...] = hc
        kvc = _dot_nt(hc, w_ref[...]).astype(BF16)
        kvc_ref[...] = kvc
        vac_ref[...] = _with_ones(kvc[:, KV_W:])

    return pl.pallas_call(
        body, name="ctx_fwd", grid=(1,),
        in_specs=[_acc((n_ctx, D)), _acc((8, D)), _acc((2 * KV_W, D))],
        out_specs=[_acc((n_ctx, D)), _acc((n_ctx, 2 * KV_W)), _acc((n_ctx, 2 * LANES))],
        out_shape=[jax.ShapeDtypeStruct((n_ctx, D), BF16), jax.ShapeDtypeStruct((n_ctx, 2 * KV_W), BF16),
                   jax.ShapeDtypeStruct((n_ctx, 2 * LANES), BF16)],
        compiler_params=_cp("arbitrary"),
    )(ctx, modc, w_kv)


def _host_start(step, comm):
    if comm is not None:
        @pl.when(step == 0)
        def _():
            comm.start()


def _host_finish(step, last, comm):
    if comm is not None:
        @pl.when(step == last)
        def _():
            comm.finish()


def _proj_fwd(x, modx, w_in, cos, sin, tm, gather=()):
    seq = x.shape[0]
    ng = len(gather)

    def body(x_ref, mod_ref, w_ref, cos_ref, sin_ref, *rest):
        h_ref, q_ref, kv_ref, va_ref, uv_ref, gab_ref = rest[ng:ng + 6]
        comm = _Gather(rest[:ng], rest[ng + 6:2 * ng + 6], *rest[2 * ng + 6:]) if ng else None
        _host_start(pl.program_id(0), comm)
        xhat, _ = _ln(x_ref[...])
        h = (xhat * (1.0 + mod_ref[1:2, :]) + mod_ref[0:1, :]).astype(BF16)
        h_ref[...] = h
        cos1, sin1 = cos_ref[...], sin_ref[...]
        cos2 = jnp.concatenate([cos1, cos1], axis=1)
        sin2 = jnp.concatenate([sin1, sin1], axis=1)
        for j in range(Q_W // 256):
            t = _dot_nt(h, w_ref[256 * j:256 * (j + 1), :])
            q_ref[:, 256 * j:256 * (j + 1)] = (_rope(t, cos2, sin2) * SCALE).astype(BF16)
        t = _dot_nt(h, w_ref[Q_W:Q_W + 2 * KV_W, :])
        kv_ref[:, :KV_W] = _rope(t[:, :KV_W], cos1, sin1).astype(BF16)
        v = t[:, KV_W:].astype(BF16)
        kv_ref[:, KV_W:] = v
        va_ref[...] = _with_ones(v)
        o = Q_W + 2 * KV_W
        for j in range(2):
            uv_ref[:, G_W * j:G_W * (j + 1)] = _dot_nt(h, w_ref[o + G_W * j:o + G_W * (j + 1), :])
        o += 2 * G_W
        for j in range(4):
            gab_ref[:, 512 * j:512 * (j + 1)] = _dot_nt(h, w_ref[o + 512 * j:o + 512 * (j + 1), :]).astype(BF16)
        _host_finish(pl.program_id(0), seq // tm - 1, comm)

    out = pl.pallas_call(
        body, name="proj_fwd", grid=(seq // tm,),
        in_specs=[_rows(tm, D), _acc((8, D)), _resident((IN_W, D)), _rows(tm, LANES), _rows(tm, LANES)] + _comm_specs(ng),
        out_specs=[_rows(tm, D), _rows(tm, Q_W), _rows(tm, 2 * KV_W), _rows(tm, 2 * LANES), _rows(tm, 2 * G_W),
                   _rows(tm, 2 * D)] + _comm_specs(ng),
        out_shape=[jax.ShapeDtypeStruct((seq, D), BF16), jax.ShapeDtypeStruct((seq, Q_W), BF16),
                   jax.ShapeDtypeStruct((seq, 2 * KV_W), BF16), jax.ShapeDtypeStruct((seq, 2 * LANES), BF16),
                   jax.ShapeDtypeStruct((seq, 2 * G_W), F32), jax.ShapeDtypeStruct((seq, 2 * D), BF16)] + _gathered_shapes(gather),
        scratch_shapes=_comm_scratch(ng) if ng else [],
        compiler_params=_cp("arbitrary"),
    )(x, modx, w_in, cos, sin, *gather)
    return out[:6], out[6:]


def _stack_heads(x, hk):
    return jnp.concatenate([x[:, (hk * GROUP + g) * HEAD:(hk * GROUP + g + 1) * HEAD] for g in range(GROUP)], axis=0)


def _attn_scores(q_ref, k_refs, hk, n, nb):
    q4 = _stack_heads(q_ref[...], hk)
    ks = [r[:, hk * HEAD:(hk + 1) * HEAD] for r in k_refs]
    rows = GROUP * BLK
    qi = lax.broadcasted_iota(jnp.int32, (rows, BLK), 0) & (BLK - 1)
    kj = lax.broadcasted_iota(jnp.int32, (rows, BLK), 1)
    s = [_dot_nt(q4, k) for k in ks]
    s[1] = jnp.where((kj >= qi) & (n > 0), s[1], NEG)
    s[3] = jnp.where((kj <= qi) & (n < nb - 1), s[3], NEG)
    return q4, ks, s


def _sink_rows(sink_ref, hk):
    rows = GROUP * BLK
    rg = lax.broadcasted_iota(jnp.int32, (rows, 1), 0) >> 7
    sink_v = jnp.full((rows, 1), sink_ref[0, hk * GROUP], F32)
    for g in range(1, GROUP):
        sink_v = jnp.where(rg == g, sink_ref[0, hk * GROUP + g], sink_v)
    return sink_v


def _with_ones(v):
    ones = jnp.ones((v.shape[0], HEAD), v.dtype)
    return jnp.concatenate([v[:, :HEAD], ones, v[:, HEAD:], ones], axis=1)


def _kv_specs(nb):
    return [pl.BlockSpec((BLK, 2 * KV_W), lambda n: (jnp.maximum(n - 1, 0), 0)),
            pl.BlockSpec((BLK, 2 * KV_W), lambda n: (n, 0)),
            pl.BlockSpec((BLK, 2 * KV_W), lambda n: (jnp.minimum(n + 1, nb - 1), 0))]


def _attn_fwd(q, kv, va, kvc, vac, sink, gather=()):
    seq = q.shape[0]
    nb = seq // BLK
    n_ctx = kvc.shape[0]
    ng = len(gather)

    def body(q_ref, kvp_ref, kvm_ref, kvn_ref, vap_ref, vam_ref, van_ref, kvc_ref, vac_ref, sink_ref, *rest):
        o_ref, lse_ref = rest[ng:ng + 2]
        comm = _Gather(rest[:ng], rest[ng + 2:2 * ng + 2], *rest[2 * ng + 2:]) if ng else None
        n = pl.program_id(0)
        _host_start(n, comm)
        outs = []
        lane = lax.broadcasted_iota(jnp.int32, (BLK, LANES), 1)
        lse_all = jnp.zeros((BLK, LANES), F32)
        for hk in range(N_KV):
            _, _, s = _attn_scores(q_ref, (kvc_ref, kvp_ref, kvm_ref, kvn_ref), hk, n, nb)
            sink_v = _sink_rows(sink_ref, hk)
            m = sink_v
            for t in s:
                m = jnp.maximum(m, jnp.max(t, axis=-1, keepdims=True))
            o = jnp.zeros((GROUP * BLK, LANES), F32)
            for t, va_ref in zip(s, (vac_ref, vap_ref, vam_ref, van_ref)):
                o = o + _dot(jnp.exp((t - m).astype(BF16)), va_ref[:, hk * LANES:(hk + 1) * LANES])
            denom = o[:, HEAD:HEAD + 1] + jnp.exp(sink_v - m)
            o4 = o[:, :HEAD] * (1.0 / denom)
            lse4 = m + jnp.log(denom)
            for g in range(GROUP):
                outs.append(o4[g * BLK:(g + 1) * BLK, :])
                lse_all = jnp.where(lane == hk * GROUP + g, lse4[g * BLK:(g + 1) * BLK, :], lse_all)
        o_ref[...] = jnp.concatenate(outs, axis=1).astype(BF16)
        lse_ref[...] = lse_all
        _host_finish(n, nb - 1, comm)

    out = pl.pallas_call(
        body, name="attn_fwd", grid=(nb,),
        in_specs=[_rows(BLK, Q_W)] + _kv_specs(nb) + _kv_specs(nb)
        + [_acc((n_ctx, 2 * KV_W)), _acc((n_ctx, 2 * LANES)), pl.BlockSpec(memory_space=pltpu.SMEM)] + _comm_specs(ng),
        out_specs=[_rows(BLK, Q_W), _rows(BLK, LANES)] + _comm_specs(ng),
        out_shape=[jax.ShapeDtypeStruct((seq, Q_W), BF16), jax.ShapeDtypeStruct((seq, LANES), F32)] + _gathered_shapes(gather),
        scratch_shapes=_comm_scratch(ng) if ng else [],
        compiler_params=_cp("arbitrary"),
    )(q, kv, kv, kv, va, va, va, kvc, vac, sink, *gather)
    return out[:2], out[2:]


def _gmlp_chunk(u, vb, gp_ref, ws_ref, bias_ref):
    gu, tu = _gelu(u)
    gv, tv = _gelu(vb)
    vhat, rstd = _ln(gv)
    vn = (vhat * gp_ref[0:1, :] + gp_ref[1:2, :]).astype(BF16)
    s = bias_ref[...] + jnp.concatenate(
        [_dot(ws_ref[g * BLK:(g + 1) * BLK, :], vn[:, g * GRP_D:(g + 1) * GRP_D]) for g in range(N_GRP)], axis=1)
    return gu, tu, tv, vhat, rstd, vn, s


def _mix_fwd(uv, gab, ya, gp, ws_stack, bias_full, w_a, w_b, w_o, tm):
    seq = uv.shape[0]

    def body(uv_ref, gab_ref, ya_ref, gp_ref, ws_ref, bias_ref, wa_ref, wb_ref, wo_ref,
             a_ref, b_ref, mix_ref, merged_ref, yb_ref):
        for c in range(tm // BLK):
            rs = slice(c * BLK, (c + 1) * BLK)
            gu, _, _, _, _, _, s = _gmlp_chunk(uv_ref[rs, :G_W], uv_ref[rs, G_W:], gp_ref, ws_ref, bias_ref)
            yb_ref[rs, :] = (gu * s).astype(BF16)
        ya = ya_ref[...]
        yb = yb_ref[...]
        for s in range(N_SHARD):
            cs = slice(s * (D // N_SHARD), (s + 1) * (D // N_SHARD))
            a = _dot(ya, wa_ref[s])
            b = _dot(yb, wb_ref[s])
            a_ref[:, cs] = a.astype(BF16)
            b_ref[:, cs] = b.astype(BF16)
            ga = gab_ref[:, cs].astype(F32)
            gb = gab_ref[:, D + s * (D // N_SHARD):D + (s + 1) * (D // N_SHARD)].astype(F32)
            merged_ref[:, cs] = (_sig(ga) * a + _sig(gb) * b).astype(BF16)
        mix_ref[...] = _dot(merged_ref[...], wo_ref[...])

    return pl.pallas_call(
        body, name="mix_fwd", grid=(seq // tm,),
        in_specs=[_rows(tm, 2 * G_W), _rows(tm, 2 * D), _rows(tm, Q_W), _acc((8, G_W)),
                  _resident((N_GRP * BLK, BLK)), _acc((BLK, G_W)),
                  _resident((N_SHARD, Q_W, D // N_SHARD)), _resident((N_SHARD, G_W, D // N_SHARD)), _resident((D, D))],
        out_specs=[_rows(tm, D), _rows(tm, D), _rows(tm, D), _rows(tm, D), _rows(tm, G_W)],
        out_shape=[jax.ShapeDtypeStruct((seq, D), BF16), jax.ShapeDtypeStruct((seq, D), BF16),
                   jax.ShapeDtypeStruct((seq, D), F32), jax.ShapeDtypeStruct((seq, D), BF16),
                   jax.ShapeDtypeStruct((seq, G_W), BF16)],
        compiler_params=_cp("arbitrary"),
    )(uv, gab, ya, gp, ws_stack, bias_full, w_a, w_b, w_o)


FFN_CHUNK = 512


def _ffn_chunks():
    out = []
    for hh in range(2):
        off = 0
        while off < FH_SHARD:
            w = min(FFN_CHUNK, FH_SHARD - off)
            out.append((hh, off, w))
            off += w
    return out


def _mid_recompute(x_ref, mix_ref, vec_ref):
    r1 = ALPHA * x_ref[...] + vec_ref[0:1, :] * mix_ref[...]
    xh1, rstd1 = _ln(r1)
    xmid = xh1 * vec_ref[1:2, :] + vec_ref[2:3, :]
    xh2, rstd2 = _ln(xmid)
    return xh1, rstd1, xmid, xh2, rstd2


def _ffn(x, mix, tgt, vec, w_fi, w_fo, tm):
    seq = x.shape[0]

    def body(x_ref, mix_ref, tgt_ref, vec_ref, wi_ref, wo_ref, act_ref, h2_ref, dff_ref, df_ref, dr1_ref, st_ref, gu_ref):
        @pl.when(pl.program_id(0) == 0)
        def _():
            st_ref[...] = jnp.zeros_like(st_ref)

        xh1, rstd1, xmid, xh2, rstd2 = _mid_recompute(x_ref, mix_ref, vec_ref)
        h2 = (xh2 * (1.0 + vec_ref[4:5, :]) + vec_ref[3:4, :]).astype(BF16)
        h2_ref[...] = h2
        f = jnp.zeros((tm, D), F32)
        for hh, off, w in _ffn_chunks():
            cs = slice(hh * FH_SHARD + off, hh * FH_SHARD + off + w)
            cu = slice(FH + hh * FH_SHARD + off, FH + hh * FH_SHARD + off + w)
            g = _dot(h2, wi_ref[hh, :, off:off + w])
            u = _dot(h2, wi_ref[2 + hh, :, off:off + w])
            gu_ref[:, cs] = g
            gu_ref[:, cu] = u
            a = (g * _sig(g) * u).astype(BF16)
            act_ref[:, cs] = a
            f = f + _dot(a, wo_ref[cs, :])
        r2 = ALPHA * xmid + vec_ref[5:6, :] * f
        yh, rstd = _ln(r2)
        y = yh * vec_ref[6:7, :] + vec_ref[7:8, :]
        err = y - tgt_ref[...]
        dy = err / D
        dr2 = _ln_bwd(dy * vec_ref[6:7, :], yh, rstd)
        st_ref[0:1, :] += _colsum(err * err)
        st_ref[1:2, :] += _colsum(dy * yh)
        st_ref[2:3, :] += _colsum(dy)
        st_ref[3:4, :] += _colsum(dr2 * f)

        df = (dr2 * vec_ref[5:6, :]).astype(BF16)
        df_ref[...] = df
        dh2 = jnp.zeros((tm, D), F32)
        for hh, off, w in _ffn_chunks():
            cs = slice(hh * FH_SHARD + off, hh * FH_SHARD + off + w)
            cu = slice(FH + hh * FH_SHARD + off, FH + hh * FH_SHARD + off + w)
            da = _dot_nt(df, wo_ref[cs, :])
            g = gu_ref[:, cs]
            u = gu_ref[:, cu]
            sg = _sig(g)
            dg = (da * u * sg * (1.0 + g * (1.0 - sg))).astype(BF16)
            du = (da * g * sg).astype(BF16)
            dff_ref[:, cs] = dg
            dff_ref[:, cu] = du
            dh2 = dh2 + _dot_nt(dg, wi_ref[hh, :, off:off + w]) + _dot_nt(du, wi_ref[2 + hh, :, off:off + w])
        dxmid = _ln_bwd(dh2 * (1.0 + vec_ref[4:5, :]), xh2, rstd2) + ALPHA * dr2
        dr1 = _ln_bwd(dxmid * vec_ref[1:2, :], xh1, rstd1)
        dr1_ref[...] = dr1
        st_ref[8:9, :] += _colsum(dh2 * xh2)
        st_ref[9:10, :] += _colsum(dh2)
        st_ref[10:11, :] += _colsum(dxmid * xh1)
        st_ref[11:12, :] += _colsum(dxmid)
        st_ref[12:13, :] += _colsum(dr1 * mix_ref[...])

    return pl.pallas_call(
        body, name="ffn", grid=(seq // tm,),
        in_specs=[_rows(tm, D), _rows(tm, D), _rows(tm, D), _acc((8, D)), _resident((N_SHARD, D, FH_SHARD)), _resident((FH, D))],
        out_specs=[_rows(tm, FH), _rows(tm, D), _rows(tm, 2 * FH), _rows(tm, D), _rows(tm, D), _acc((16, D))],
        out_shape=[jax.ShapeDtypeStruct((seq, FH), BF16), jax.ShapeDtypeStruct((seq, D), BF16),
                   jax.ShapeDtypeStruct((seq, 2 * FH), BF16), jax.ShapeDtypeStruct((seq, D), BF16),
                   jax.ShapeDtypeStruct((seq, D), F32), jax.ShapeDtypeStruct((16, D), F32)],
        scratch_shapes=[pltpu.VMEM((tm, 2 * FH), F32)],
        compiler_params=_cp("arbitrary"),
    )(x, mix, tgt, vec, w_fi, w_fo)


def _mix_bwd(dr1, a, b, gab, uv, merged, ya, yb, vec, gp, ws_stack, ws_stack_t, bias_full, w_a, w_b, w_o, tm):
    seq = dr1.shape[0]
    last = seq // tm - 1
    cw = D // N_SHARD

    def body(dr1_ref, a_ref, b_ref, gab_ref, uv_ref, mg_ref, ya_ref, yb_ref, vec_ref, gp_ref, ws_ref, wst_ref, bias_ref,
             wa_ref, wb_ref, wo_ref, dya_ref, dp_ref, dws_ref, dbs_ref, st_ref, gwo_ref, gwa_ref, gwb_ref,
             acc_o, acc_a, acc_b):
        @pl.when(pl.program_id(0) == 0)
        def _():
            dws_ref[...] = jnp.zeros_like(dws_ref)
            dbs_ref[...] = jnp.zeros_like(dbs_ref)
            st_ref[...] = jnp.zeros_like(st_ref)
            acc_o[...] = jnp.zeros_like(acc_o)
            acc_a[...] = jnp.zeros_like(acc_a)
            acc_b[...] = jnp.zeros_like(acc_b)

        dmix = (dr1_ref[...] * vec_ref[0:1, :]).astype(BF16)
        acc_o[...] += _dot_tn(mg_ref[...], dmix)
        dmerged = _dot_nt(dmix, wo_ref[...])
        sa = _sig(gab_ref[:, :D].astype(F32))
        sb = _sig(gab_ref[:, D:].astype(F32))
        da = (dmerged * sa).astype(BF16)
        db = (dmerged * sb).astype(BF16)
        dp_ref[:, 2 * G_W:2 * G_W + D] = (dmerged * a_ref[...].astype(F32) * sa * (1.0 - sa)).astype(BF16)
        dp_ref[:, 2 * G_W + D:] = (dmerged * b_ref[...].astype(F32) * sb * (1.0 - sb)).astype(BF16)
        dya = jnp.zeros((tm, Q_W), F32)
        dyb = jnp.zeros((tm, G_W), F32)
        ya = ya_ref[...]
        yb = yb_ref[...]
        for s in range(N_SHARD):
            cs = slice(s * cw, (s + 1) * cw)
            dya = dya + _dot_nt(da[:, cs], wa_ref[s])
            dyb = dyb + _dot_nt(db[:, cs], wb_ref[s])
            acc_a[s] += _dot_tn(ya, da[:, cs])
            acc_b[s] += _dot_tn(yb, db[:, cs])
        dya_ref[...] = dya.astype(BF16)

        @pl.when(pl.program_id(0) == last)
        def _():
            gwo_ref[...] = acc_o[...].astype(BF16)
            gwa_ref[...] = acc_a[...].astype(BF16)
            gwb_ref[...] = acc_b[...].astype(BF16)

        for c in range(tm // BLK):
            rs = slice(c * BLK, (c + 1) * BLK)
            u = uv_ref[rs, :G_W]
            vb = uv_ref[rs, G_W:]
            gu, tu, tv, vhat, rstd, vn, s = _gmlp_chunk(u, vb, gp_ref, ws_ref, bias_ref)
            dyb_c = dyb[rs, :]
            ds = dyb_c * gu
            du = dyb_c * s * _gelu_grad(u, tu)
            ds_b = ds.astype(BF16)
            dvn_g = []
            for g in range(N_GRP):
                cg = slice(g * GRP_D, (g + 1) * GRP_D)
                dvn_g.append(_dot(wst_ref[:, g * BLK:(g + 1) * BLK], ds_b[:, cg]))
                dws_ref[g * BLK:(g + 1) * BLK, :] += _dot_nt(ds_b[:, cg], vn[:, cg])
            dvn = jnp.concatenate(dvn_g, axis=1)
            dbs_ref[...] += ds
            st_ref[0:1, :] += _colsum(dvn * vhat)
            st_ref[1:2, :] += _colsum(dvn)
            dgv = _ln_bwd(dvn * gp_ref[0:1, :], vhat, rstd)
            dvb = dgv * _gelu_grad(vb, tv)
            dp_ref[rs, :G_W] = du.astype(BF16)
            dp_ref[rs, G_W:2 * G_W] = dvb.astype(BF16)

    pw = 2 * G_W + 2 * D
    return pl.pallas_call(
        body, name="mix_bwd", grid=(seq // tm,),
        in_specs=[_rows(tm, D), _rows(tm, D), _rows(tm, D), _rows(tm, 2 * D), _rows(tm, 2 * G_W), _rows(tm, D), _rows(tm, Q_W),
                  _rows(tm, G_W), _acc((8, D)), _acc((8, G_W)),
                  _resident((N_GRP * BLK, BLK)), _resident((BLK, N_GRP * BLK)), _acc((BLK, G_W)),
                  _resident((N_SHARD, Q_W, cw)), _resident((N_SHARD, G_W, cw)), _resident((D, D))],
        out_specs=[_rows(tm, Q_W), _rows(tm, pw), _acc((N_GRP * BLK, BLK)), _acc((BLK, G_W)), _acc((8, G_W)),
                   _acc((D, D)), _acc((N_SHARD, Q_W, cw)), _acc((N_SHARD, G_W, cw))],
        out_shape=[jax.ShapeDtypeStruct((seq, Q_W), BF16), jax.ShapeDtypeStruct((seq, pw), BF16),
                   jax.ShapeDtypeStruct((N_GRP * BLK, BLK), F32), jax.ShapeDtypeStruct((BLK, G_W), F32),
                   jax.ShapeDtypeStruct((8, G_W), F32), jax.ShapeDtypeStruct((D, D), BF16),
                   jax.ShapeDtypeStruct((N_SHARD, Q_W, cw), BF16), jax.ShapeDtypeStruct((N_SHARD, G_W, cw), BF16)],
        scratch_shapes=[pltpu.VMEM((D, D), F32), pltpu.VMEM((N_SHARD, Q_W, cw), F32), pltpu.VMEM((N_SHARD, G_W, cw), F32)],
        compiler_params=_cp("arbitrary"),
    )(dr1, a, b, gab, uv, merged, ya, yb, vec, gp, ws_stack, ws_stack_t, bias_full, w_a, w_b, w_o)


def _attn_bwd(q, kv, kvc, sink, dya, ya, lse, scatter=()):
    seq = q.shape[0]
    nb = seq // BLK
    n_ctx = kvc.shape[0]
    ns = len(scatter)

    def body(q_ref, kvp_ref, kvm_ref, kvn_ref, kvc_ref, sink_ref, do_ref, o_ref, lse_ref, *rest):
        dq_ref, dkv_ref, dkvc_ref, dsink_ref = rest[ns:ns + 4]
        comm = _AllToAll(rest[:ns], rest[ns + 4:2 * ns + 4], *rest[2 * ns + 4:]) if ns else None
        n = pl.program_id(0)
        _host_start(n, comm)

        @pl.when(n == 0)
        def _():
            dkv_ref[...] = jnp.zeros_like(dkv_ref)
            dkvc_ref[...] = jnp.zeros_like(dkvc_ref)
            dsink_ref[...] = jnp.zeros_like(dsink_ref)

        do = do_ref[...]
        out = o_ref[...]
        lse_all = lse_ref[...]
        k_refs = (kvc_ref, kvp_ref, kvm_ref, kvn_ref)
        dqs, dks, dvs = [], [], []
        for hk in range(N_KV):
            q4, ks, s = _attn_scores(q_ref, k_refs, hk, n, nb)
            vs = [r[:, KV_W + hk * HEAD:KV_W + (hk + 1) * HEAD] for r in k_refs]
            lse4 = jnp.concatenate([lse_all[:, hk * GROUP + g:hk * GROUP + g + 1] for g in range(GROUP)], axis=0)
            do4 = _stack_heads(do, hk)
            delta = jnp.sum(do4.astype(F32) * _stack_heads(out, hk).astype(F32), axis=-1, keepdims=True)
            p = [jnp.exp((t - lse4).astype(BF16)) for t in s]
            ds = [t * (_dot_nt(do4, v) - delta).astype(BF16) for t, v in zip(p, vs)]
            dq4 = _dot(ds[0], ks[0])
            for t, k in zip(ds[1:], ks[1:]):
                dq4 = dq4 + _dot(t, k)
            dq4 = dq4 * SCALE
            dqs += [dq4[g * BLK:(g + 1) * BLK, :] for g in range(GROUP)]
            dks.append([_dot_tn(t, q4) for t in ds])
            dvs.append([_dot_tn(t, do4) for t in p])
            ps = jnp.exp(_sink_rows(sink_ref, hk) - lse4) * delta
            lane = lax.broadcasted_iota(jnp.int32, (1, LANES), 1)
            for g in range(GROUP):
                part = -jnp.sum(ps[g * BLK:(g + 1) * BLK, :], axis=0, keepdims=True)
                dsink_ref[0:1, :] += jnp.where(lane == hk * GROUP + g, part, 0.0)
        dq_ref[...] = jnp.concatenate(dqs, axis=1)

        def piece(i):
            return jnp.concatenate([dks[0][i], dks[1][i], dvs[0][i], dvs[1][i]], axis=1)

        dkvc_ref[...] += piece(0)
        starts = (jnp.maximum(n - 1, 0), n, jnp.minimum(n + 1, nb - 1))
        for i, st in enumerate(starts):
            r = pl.ds(pl.multiple_of(st * BLK, BLK), BLK)
            dkv_ref[r, :] += piece(i + 1)
        _host_finish(n, nb - 1, comm)

    out = pl.pallas_call(
        body, name="attn_bwd", grid=(nb,),
        in_specs=[_rows(BLK, Q_W)] + _kv_specs(nb) + [_acc((n_ctx, 2 * KV_W)), pl.BlockSpec(memory_space=pltpu.SMEM),
                                                      _rows(BLK, Q_W), _rows(BLK, Q_W), _rows(BLK, LANES)] + _comm_specs(ns),
        out_specs=[_rows(BLK, Q_W), _acc((seq, 2 * KV_W)), _acc((n_ctx, 2 * KV_W)), _acc((8, LANES))] + _comm_specs(ns),
        out_shape=[jax.ShapeDtypeStruct((seq, Q_W), F32), jax.ShapeDtypeStruct((seq, 2 * KV_W), F32),
                   jax.ShapeDtypeStruct((n_ctx, 2 * KV_W), F32), jax.ShapeDtypeStruct((8, LANES), F32)]
        + [jax.ShapeDtypeStruct(v.shape, v.dtype) for v in scatter],
        scratch_shapes=_comm_scratch(ns) if ns else [],
        compiler_params=_cp("arbitrary"),
    )(q, kv, kv, kv, kvc, sink, dya, ya, lse, *scatter)
    return out[:4], out[4:]


def _proj_bwd(dq, dkv, dpb, x, dr1, modx, w_in, cos, sin, tm, scatter=()):
    seq = x.shape[0]
    pw = IN_W - Q_W - 2 * KV_W
    ns = len(scatter)

    def body(dq_ref, dkv_ref, dpb_ref, x_ref, dr1_ref, mod_ref, w_ref, cos_ref, sin_ref, *rest):
        dqkv_ref, gx_ref, st_ref = rest[ns:ns + 3]
        comm = _AllToAll(rest[:ns], rest[ns + 3:2 * ns + 3], *rest[2 * ns + 3:]) if ns else None
        _host_start(pl.program_id(0), comm)

        @pl.when(pl.program_id(0) == 0)
        def _():
            st_ref[...] = jnp.zeros_like(st_ref)

        cos1, sin1 = cos_ref[...], sin_ref[...]
        cos2 = jnp.concatenate([cos1, cos1], axis=1)
        sin2 = jnp.concatenate([sin1, sin1], axis=1)
        for j in range(Q_W // 256):
            cs = slice(256 * j, 256 * (j + 1))
            dqkv_ref[:, cs] = _unrope(dq_ref[:, cs], cos2, sin2).astype(BF16)
        dqkv_ref[:, Q_W:Q_W + KV_W] = _unrope(dkv_ref[:, :KV_W], cos1, sin1).astype(BF16)
        dqkv_ref[:, Q_W + KV_W:] = dkv_ref[:, KV_W:].astype(BF16)
        o = Q_W + 2 * KV_W
        dh = _dot(dqkv_ref[...], w_ref[:o, :]) + _dot(dpb_ref[...], w_ref[o:, :])
        xhat, rstd = _ln(x_ref[...])
        st_ref[0:1, :] += _colsum(dh)
        st_ref[1:2, :] += _colsum(dh * xhat)
        gx_ref[...] = _ln_bwd(dh * (1.0 + mod_ref[1:2, :]), xhat, rstd) + ALPHA * dr1_ref[...]
        _host_finish(pl.program_id(0), seq // tm - 1, comm)

    out = pl.pallas_call(
        body, name="proj_bwd", grid=(seq // tm,),
        in_specs=[_rows(tm, Q_W), _rows(tm, 2 * KV_W), _rows(tm, pw), _rows(tm, D), _rows(tm, D), _acc((8, D)),
                  _resident((IN_W, D)), _rows(tm, LANES), _rows(tm, LANES)] + _comm_specs(ns),
        out_specs=[_rows(tm, Q_W + 2 * KV_W), _rows(tm, D), _acc((8, D))] + _comm_specs(ns),
        out_shape=[jax.ShapeDtypeStruct((seq, Q_W + 2 * KV_W), BF16), jax.ShapeDtypeStruct((seq, D), F32),
                   jax.ShapeDtypeStruct((8, D), F32)] + [jax.ShapeDtypeStruct(v.shape, v.dtype) for v in scatter],
        scratch_shapes=_comm_scratch(ns) if ns else [],
        compiler_params=_cp("arbitrary"),
    )(dq, dkv, dpb, x, dr1, modx, w_in, cos, sin, *scatter)
    return out[:3], out[3:]


def _ctx_bwd(dkvc, ctx, hc, w_kv):
    n_ctx = ctx.shape[0]

    def body(dkvc_ref, ctx_ref, hc_ref, w_ref, dw_ref, st_ref):
        d = dkvc_ref[...].astype(BF16)
        dw_ref[...] = _dot_tn(d, hc_ref[...])
        dhc = _dot(d, w_ref[...])
        xhat, _ = _ln(ctx_ref[...])
        st_ref[...] = jnp.zeros_like(st_ref)
        st_ref[0:1, :] = _colsum(dhc)
        st_ref[1:2, :] = _colsum(dhc * xhat)

    return pl.pallas_call(
        body, name="ctx_bwd", grid=(1,),
        in_specs=[_acc((n_ctx, 2 * KV_W)), _acc((n_ctx, D)), _acc((n_ctx, D)), _acc((2 * KV_W, D))],
        out_specs=[_acc((2 * KV_W, D)), _acc((8, D))],
        out_shape=[jax.ShapeDtypeStruct((2 * KV_W, D), F32), jax.ShapeDtypeStruct((8, D), F32)],
        compiler_params=_cp("arbitrary"),
    )(dkvc, ctx, hc, w_kv)


def _tn_matmul(a, b, tn, name, out_dtype, shard_major=False, init=None, tk=512):
    t, ka = a.shape
    n = b.shape[1]
    tk = min(tk, t)
    nk = t // tk
    has_init = init is not None

    def body(*refs):
        if has_init:
            a_ref, b_ref, i_ref, o_ref, acc_ref = refs
        else:
            a_ref, b_ref, o_ref, acc_ref = refs
        k = pl.program_id(1)

        @pl.when(k == 0)
        def _():
            acc_ref[...] = i_ref[...] if has_init else jnp.zeros_like(acc_ref)

        acc_ref[...] += _dot_tn(a_ref[...], b_ref[...])

        @pl.when(k == nk - 1)
        def _():
            o_ref[...] = acc_ref[...].astype(out_dtype)

    in_specs = [pl.BlockSpec((tk, ka), lambda j, k: (k, 0)), pl.BlockSpec((tk, tn), lambda j, k: (k, j))]
    args = [a, b]
    if has_init:
        in_specs.append(pl.BlockSpec((ka, tn), lambda j, k: (0, j)))
        args.append(init)
    if shard_major:
        out_spec = pl.BlockSpec((None, ka, tn), lambda j, k: (j, 0, 0))
        out_shape = jax.ShapeDtypeStruct((n // tn, ka, tn), out_dtype)
    else:
        out_spec = pl.BlockSpec((ka, tn), lambda j, k: (0, j))
        out_shape = jax.ShapeDtypeStruct((ka, n), out_dtype)
    return pl.pallas_call(
        body, name=name, grid=(n // tn, nk), in_specs=in_specs, out_specs=out_spec, out_shape=out_shape,
        scratch_shapes=[pltpu.VMEM((ka, tn), F32)],
        compiler_params=_cp("arbitrary", "arbitrary"),
    )(*args)


ADA_TILE = 512


def _ada_fwd(sc_all, w_ada):
    cs = w_ada.shape[1]

    def body(s_ref, w_ref, o_ref):
        o_ref[...] = _dot(s_ref[...].astype(BF16), w_ref[...].astype(BF16))

    return pl.pallas_call(
        body, name="ada_fwd", grid=(cs // ADA_TILE,),
        in_specs=[_acc((16, D)), pl.BlockSpec((D, ADA_TILE), lambda j: (0, j))],
        out_specs=pl.BlockSpec((16, ADA_TILE), lambda j: (0, j)),
        out_shape=jax.ShapeDtypeStruct((16, cs), F32),
        compiler_params=_cp("arbitrary"),
    )(sc_all, w_ada)


def _ada_bwd(sc_all_t, dm_all, dmc, w_ada):
    cs = w_ada.shape[1]

    def body(st_ref, dm_ref, dmc_ref, w_ref, gw_ref, part_ref):
        @pl.when(pl.program_id(0) == 0)
        def _():
            part_ref[...] = jnp.zeros_like(part_ref)

        gw_ref[...] = _dot(st_ref[...].astype(BF16), dm_ref[...].astype(BF16))
        part_ref[...] += _dot_nt(dmc_ref[...].astype(BF16), w_ref[...].astype(BF16))

    return pl.pallas_call(
        body, name="ada_bwd", grid=(cs // ADA_TILE,),
        in_specs=[_acc((D, 16)), pl.BlockSpec((16, ADA_TILE), lambda j: (0, j)), pl.BlockSpec((8, ADA_TILE), lambda j: (0, j)),
                  pl.BlockSpec((D, ADA_TILE), lambda j: (0, j))],
        out_specs=[pl.BlockSpec((D, ADA_TILE), lambda j: (0, j)), _acc((8, D))],
        out_shape=[jax.ShapeDtypeStruct((D, cs), F32), jax.ShapeDtypeStruct((8, D), F32)],
        compiler_params=_cp("arbitrary"),
    )(sc_all_t, dm_all, dmc, w_ada)


def _sum8(x, name, tr=256):
    _, r, c = x.shape
    tr = min(tr, r)
    while r % tr:
        tr -= 16

    def body(x_ref, o_ref):
        acc = x_ref[0].astype(F32)
        for i in range(1, N_DEV):
            acc = acc + x_ref[i].astype(F32)
        o_ref[...] = acc

    return pl.pallas_call(
        body, name=name, grid=(r // tr,),
        in_specs=[pl.BlockSpec((N_DEV, tr, c), lambda i: (0, i, 0))],
        out_specs=pl.BlockSpec((tr, c), lambda i: (i, 0)),
        out_shape=jax.ShapeDtypeStruct((r, c), F32),
        compiler_params=_cp("arbitrary"),
    )(x)


def _sum_blocks(recv, src, me, name, tr=256):
    _, r, c = recv.shape
    tr = min(tr, r)
    while r % tr:
        tr -= 16

    def body(me_ref, recv_ref, own_ref, o_ref):
        acc = own_ref[...].astype(F32)
        for k in range(1, N_DEV):
            acc = acc + recv_ref[me_ref[0] ^ k].astype(F32)
        o_ref[...] = acc

    return pl.pallas_call(
        body, name=name,
        grid_spec=pltpu.PrefetchScalarGridSpec(
            num_scalar_prefetch=1, grid=(r // tr,),
            in_specs=[pl.BlockSpec((N_DEV, tr, c), lambda i, me_ref: (0, i, 0)),
                      pl.BlockSpec((None, tr, c), lambda i, me_ref: (me_ref[0], i, 0))],
            out_specs=pl.BlockSpec((tr, c), lambda i, me_ref: (i, 0))),
        out_shape=jax.ShapeDtypeStruct((r, c), F32),
        compiler_params=_cp("arbitrary"),
    )(me, recv, src)


def _sum8_many(xs, name):
    n = len(xs)

    def body(*refs):
        for x_ref, o_ref in zip(refs[:n], refs[n:]):
            acc = x_ref[0]
            for i in range(1, N_DEV):
                acc = acc + x_ref[i]
            o_ref[...] = acc

    vmem = pl.BlockSpec(memory_space=pltpu.VMEM)
    return pl.pallas_call(
        body, name=name, in_specs=[vmem] * n, out_specs=[vmem] * n,
        out_shape=[jax.ShapeDtypeStruct(v.shape[1:], v.dtype) for v in xs],
        compiler_params=pltpu.CompilerParams(vmem_limit_bytes=VMEM_LIMIT),
    )(*xs)


def _adam_update(w, g, m, v):
    nm = ADAM_B1 * m + (1.0 - ADAM_B1) * g
    nv = ADAM_B2 * v + (1.0 - ADAM_B2) * (g * g)
    m_hat = nm / (1.0 - ADAM_B1 ** ADAM_STEP)
    v_hat = nv / (1.0 - ADAM_B2 ** ADAM_STEP)
    return -ADAM_LR * (m_hat / (jnp.sqrt(v_hat) + ADAM_EPS) + ADAM_WD * w), nm, nv


ROW_LOSS, ROW_LN2_G, ROW_LN2_B, ROW_LN1_G, ROW_LN1_B = 0, 1, 2, 10, 11
ROWS_DMOD_X = (16, 17, 12, 9, 8, 3)
ROWS_DMOD_C = (24, 25)
SMALL = ("c_ctx", "b_ada", "attn_sink", "gmlp_ln_g", "gmlp_ln_b", "w_spatial", "b_spatial", "ln1_g", "ln1_b", "ln2_g", "ln2_b")


def _adamw_small(sums, dsc, w, m, v):
    n = len(SMALL)

    def body(*refs):
        st_ref, gm_ref, sk_ref, ws_ref, bs_ref, dsc_ref = refs[:6]
        w_refs = dict(zip(SMALL, refs[6:6 + n]))
        m_refs = dict(zip(SMALL, refs[6 + n:6 + 2 * n]))
        v_refs = dict(zip(SMALL, refs[6 + 2 * n:6 + 3 * n]))
        outs = refs[6 + 3 * n:]
        c = w_refs["c_ctx"][...]
        sg = _sig(c)
        dmod = [st_ref[r:r + 1, :] for r in ROWS_DMOD_X]
        dmod[0] = dmod[0] + st_ref[ROWS_DMOD_C[0]:ROWS_DMOD_C[0] + 1, :]
        dmod[1] = dmod[1] + st_ref[ROWS_DMOD_C[1]:ROWS_DMOD_C[1] + 1, :]
        grads = dict(
            c_ctx=dsc_ref[0:1, :] * (sg * (1.0 + c * (1.0 - sg))),
            b_ada=jnp.concatenate(dmod, axis=1),
            attn_sink=sk_ref[0:1, 0:N_KV * GROUP],
            gmlp_ln_g=gm_ref[0:1, :], gmlp_ln_b=gm_ref[1:2, :],
            w_spatial=ws_ref[...], b_spatial=bs_ref[...],
            ln1_g=st_ref[ROW_LN1_G:ROW_LN1_G + 1, :], ln1_b=st_ref[ROW_LN1_B:ROW_LN1_B + 1, :],
            ln2_g=st_ref[ROW_LN2_G:ROW_LN2_G + 1, :], ln2_b=st_ref[ROW_LN2_B:ROW_LN2_B + 1, :])
        for i, name in enumerate(SMALL):
            g = grads[name]
            d, nm, nv = _adam_update(w_refs[name][...], g, m_refs[name][...], v_refs[name][...])
            outs[i][...] = g
            outs[n + i][...] = d
            outs[2 * n + i][...] = nm
            outs[3 * n + i][...] = nv

    vmem = pl.BlockSpec(memory_space=pltpu.VMEM)
    args = list(sums) + [dsc] + [w[k] for k in SMALL] + [m[k] for k in SMALL] + [v[k] for k in SMALL]
    shapes = [jax.ShapeDtypeStruct(w[k].shape, F32) for k in SMALL]
    out = pl.pallas_call(
        body, name="adamw_small", in_specs=[vmem] * len(args), out_specs=[vmem] * (4 * n), out_shape=shapes * 4,
        compiler_params=pltpu.CompilerParams(vmem_limit_bytes=VMEM_LIMIT),
    )(*args)
    return [dict(zip(SMALL, out[i * n:(i + 1) * n])) for i in range(4)]


def _adamw_halves(w, mine, theirs, m, v, c_arr, name):
    r, c = w.shape
    tr = min(256, r // 2)
    while (r // 2) % tr:
        tr -= 8
    nt = (r // 2) // tr

    def body(c_ref, w_ref, mine_ref, theirs_ref, m_ref, v_ref, g_ref, d_ref, nm_ref, nv_ref):
        g = jnp.where(pl.program_id(0) == c_ref[0], mine_ref[...], theirs_ref[...])
        g_ref[...] = g
        d_ref[...], nm_ref[...], nv_ref[...] = _adam_update(w_ref[...], g, m_ref[...], v_ref[...])

    whole = pl.BlockSpec((tr, c), lambda hb, i, c_ref: (hb * nt + i, 0))
    half = pl.BlockSpec((tr, c), lambda hb, i, c_ref: (i, 0))
    shp = jax.ShapeDtypeStruct((r, c), F32)
    return pl.pallas_call(
        body, name=name,
        grid_spec=pltpu.PrefetchScalarGridSpec(
            num_scalar_prefetch=1, grid=(2, nt), in_specs=[whole, half, half, whole, whole], out_specs=[whole] * 4),
        out_shape=[shp] * 4,
        compiler_params=_cp("arbitrary", "arbitrary"),
    )(c_arr, w, mine, theirs, m, v)


def _adamw(w, g, m, v, name):
    r, c = w.shape
    tr = r if r * c <= 256 * 1024 else min(256, r)
    while r % tr:
        tr -= 8

    def body(w_ref, g_ref, m_ref, v_ref, d_ref, nm_ref, nv_ref):
        d_ref[...], nm_ref[...], nv_ref[...] = _adam_update(w_ref[...], g_ref[...], m_ref[...], v_ref[...])

    spec = pl.BlockSpec((tr, c), lambda i: (i, 0))
    shp = jax.ShapeDtypeStruct((r, c), F32)
    return pl.pallas_call(
        body, name=name, grid=(r // tr,), in_specs=[spec] * 4, out_specs=[spec] * 3, out_shape=[shp] * 3,
        compiler_params=_cp("arbitrary"),
    )(w, g, m, v)


def _my_pos():
    return lax.axis_index("x"), lax.axis_index("y"), lax.axis_index("c")


N_COPY = 7


class _Gather:
    def __init__(self, x_refs, out_refs, send_sems, recv_sems):
        self.x_refs, self.out_refs = x_refs, out_refs
        self.send_sems, self.recv_sems = send_sems, recv_sems
        x, y, c = _my_pos()
        self.c = c
        self.me, self.sibling = (x, y, c), (x, y, 1 - c)
        self.chips = [(1 - x, y), (x, 1 - y), (1 - x, 1 - y)]

    def _copy(self, a, k, block, to, from_input=False):
        px, py, pc = block
        rows = self.out_refs[a].at[4 * px + 2 * py + pc]
        return pltpu.make_async_remote_copy(
            src_ref=self.x_refs[a] if from_input else rows, dst_ref=rows,
            send_sem=self.send_sems.at[a * N_COPY + k], recv_sem=self.recv_sems.at[a * N_COPY + k],
            device_id=to, device_id_type=MESH)

    def start(self):
        n = len(self.x_refs)
        for a in range(n):
            self._copy(a, 0, self.me, self.sibling, from_input=True).start()
        for j, chip in enumerate(self.chips):
            for a in range(n):
                self._copy(a, 1 + j, self.me, (*chip, self.c), from_input=True).start()

    def finish(self):
        n = len(self.x_refs)
        c = self.c
        for j, chip in enumerate(self.chips):
            for a in range(n):
                self._copy(a, 1 + j, (*chip, c), self.me).wait_recv()
                self._copy(a, 4 + j, (*chip, c), self.sibling).start()
        for a in range(n):
            self._copy(a, 0, self.sibling, self.me).wait_recv()
        for j, chip in enumerate(self.chips):
            for a in range(n):
                self._copy(a, 4 + j, (*chip, 1 - c), self.me).wait_recv()
        for a in range(n):
            self._copy(a, 0, self.me, self.sibling, from_input=True).wait_send()
            for j, chip in enumerate(self.chips):
                self._copy(a, 1 + j, self.me, (*chip, c), from_input=True).wait_send()
                self._copy(a, 4 + j, (*chip, c), self.sibling).wait_send()


def _comm_scratch(n):
    return [pltpu.SemaphoreType.DMA((n * N_COPY,)), pltpu.SemaphoreType.DMA((n * N_COPY,))]


def _comm_specs(n):
    return [pl.BlockSpec(memory_space=pl.ANY)] * n


def _gathered_shapes(xs):
    return [jax.ShapeDtypeStruct((N_DEV,) + v.shape, v.dtype) for v in xs]


def _with_own(gathered, xs, me):
    return [lax.dynamic_update_index_in_dim(g, v, me, 0) for g, v in zip(gathered, xs)]


def _all_gather(xs, me, name):
    n = len(xs)

    def body(*refs):
        g = _Gather(refs[:n], refs[n:2 * n], *refs[2 * n:])
        g.start()
        g.finish()

    out = pl.pallas_call(
        body, name=name, out_shape=_gathered_shapes(xs), in_specs=_comm_specs(n), out_specs=_comm_specs(n),
        scratch_shapes=_comm_scratch(n),
    )(*xs)
    return _with_own(out, xs, me)


class _AllToAll:
    def __init__(self, x_refs, out_refs, send_sems, recv_sems):
        self.x_refs, self.out_refs = x_refs, out_refs
        self.send_sems, self.recv_sems = send_sems, recv_sems
        self.pos = _my_pos()
        x, y, c = self.pos
        self.me = 4 * x + 2 * y + c

    def _peer(self, k):
        x, y, c = self.pos
        return (x ^ ((k >> 2) & 1), y ^ ((k >> 1) & 1), c ^ (k & 1))

    def _copy(self, a, k):
        p = self._peer(k)
        return pltpu.make_async_remote_copy(
            src_ref=self.x_refs[a].at[4 * p[0] + 2 * p[1] + p[2]], dst_ref=self.out_refs[a].at[self.me],
            send_sem=self.send_sems.at[a * N_COPY + k - 1], recv_sem=self.recv_sems.at[a * N_COPY + k - 1],
            device_id=p, device_id_type=MESH)

    def start(self):
        for k in range(1, N_DEV):
            for a in range(len(self.x_refs)):
                self._copy(a, k).start()

    def finish(self):
        for a in range(len(self.x_refs)):
            for k in range(1, N_DEV):
                self._copy(a, k).wait_recv()
            for k in range(1, N_DEV):
                self._copy(a, k).wait_send()


def _all_to_all(blocks, name):
    n = len(blocks)

    def body(*refs):
        t = _AllToAll(refs[:n], refs[n:2 * n], *refs[2 * n:])
        t.start()
        t.finish()

    return pl.pallas_call(
        body, name=name, out_shape=[jax.ShapeDtypeStruct(v.shape, v.dtype) for v in blocks],
        in_specs=_comm_specs(n), out_specs=_comm_specs(n), scratch_shapes=_comm_scratch(n),
    )(*blocks)


def _sibling_exchange(xs, name):
    n = len(xs)

    def body(*refs):
        x_refs, out_refs = refs[:n], refs[n:2 * n]
        send_sems, recv_sems = refs[2 * n:]
        x, y, c = _my_pos()

        def push(a):
            return pltpu.make_async_remote_copy(
                src_ref=x_refs[a], dst_ref=out_refs[a], send_sem=send_sems.at[a], recv_sem=recv_sems.at[a],
                device_id=(x, y, 1 - c), device_id_type=MESH)

        for a in range(n):
            push(a).start()
        for a in range(n):
            push(a).wait_recv()
            push(a).wait_send()

    return pl.pallas_call(
        body, name=name, out_shape=[jax.ShapeDtypeStruct(v.shape, v.dtype) for v in xs],
        in_specs=_comm_specs(n), out_specs=_comm_specs(n),
        scratch_shapes=[pltpu.SemaphoreType.DMA((n,)), pltpu.SemaphoreType.DMA((n,))],
    )(*xs)


def _scatter_and_gather(scatter, gather, name):
    ns, ng = len(scatter), len(gather)

    def body(*refs):
        s_in, g_in = refs[:ns], refs[ns:ns + ng]
        s_out, g_out = refs[ns + ng:2 * ns + ng], refs[2 * ns + ng:2 * (ns + ng)]
        s_send, s_recv, g_send, g_recv = refs[2 * (ns + ng):]
        g = _Gather(g_in, g_out, g_send, g_recv)
        t = _AllToAll(s_in, s_out, s_send, s_recv)
        g.start()
        t.start()
        g.finish()
        t.finish()

    out = pl.pallas_call(
        body, name=name,
        out_shape=[jax.ShapeDtypeStruct(v.shape, v.dtype) for v in scatter] + _gathered_shapes(gather),
        in_specs=_comm_specs(ns + ng), out_specs=_comm_specs(ns + ng),
        scratch_shapes=_comm_scratch(ns) + _comm_scratch(ng),
    )(*scatter, *gather)
    return out[:ns], out[ns:]


def _row_tile(seq, want):
    return min(want, seq)


def _local_step(x, ctx, tgt, mod_x, mod_c, wb, sink, gmlp_g, gmlp_b, w_s, b_s, ln1_g, ln1_b, ln2_g, ln2_b,
                later=None, me=None):
    seq = x.shape[0]
    on_mesh = me is not None
    modx1 = jnp.concatenate([mod_x[0:2], jnp.zeros((6, D), F32)], axis=0)
    modc = jnp.concatenate([mod_c[0:2], jnp.zeros((6, D), F32)], axis=0)
    vec = jnp.concatenate([mod_x[2:3], ln1_g, ln1_b, mod_x[3:6], ln2_g, ln2_b], axis=0)
    gp = jnp.concatenate([gmlp_g, gmlp_b, jnp.zeros((6, G_W), F32)], axis=0)
    ws_stack = w_s.reshape(N_GRP * BLK, BLK).astype(BF16)
    ws_stack_t = jnp.transpose(w_s, (2, 0, 1)).reshape(BLK, N_GRP * BLK).astype(BF16)
    bias_full = jnp.repeat(b_s.T, GRP_D, axis=1)
    cos, sin = _rope_tables(seq)
    w_in = wb["w_in"]
    w_kv = w_in[Q_W:Q_W + 2 * KV_W, :]
    tm_big = _row_tile(seq, 512)
    tm_ffn = _row_tile(seq, 256)

    hc, kvc, vac = _ctx_fwd(ctx, modc, w_kv)
    behind_proj = ("w_a", "w_b", "w_o") if on_mesh else ()
    behind_attn = ("w_fi", "w_fo") if on_mesh else ()
    (h, q, kv, va, uv, gab), got_proj = _proj_fwd(x, modx1, w_in, cos, sin, tm_big, gather=[later[n] for n in behind_proj])
    (ya, lse), got_attn = _attn_fwd(q, kv, va, kvc, vac, sink, gather=[later[n] for n in behind_attn])
    if on_mesh:
        wb = dict(wb)
        names = behind_proj + behind_attn
        for n, g in zip(names, _with_own(list(got_proj) + list(got_attn), [later[n] for n in names], me)):
            wb[n] = g.reshape(-1, g.shape[2]) if n in ROW_SHARDED else g.reshape(N_SHARD, 2 * g.shape[1], g.shape[2])
    a, b, mix, merged, yb = _mix_fwd(uv, gab, ya, gp, ws_stack, bias_full, wb["w_a"], wb["w_b"], wb["w_o"], tm_big)
    act, h2, dff, df, dr1, st_ffn = _ffn(x, mix, tgt, vec, wb["w_fi"], wb["w_fo"], tm_ffn)
    g_w_fo = _tn_matmul(act, df, 512, "tn_w_ffn_out", BF16)
    g_w_fi = _tn_matmul(h2, dff, FH_SHARD, "tn_w_ffn_in", BF16, shard_major=True)
    dya, dpb, dws, dbs_full, st4, g_w_o, g_w_a, g_w_b = _mix_bwd(
        dr1, a, b, gab, uv, merged, ya, yb, vec, gp, ws_stack, ws_stack_t, bias_full, wb["w_a"], wb["w_b"], wb["w_o"],
        tm_big)
    blocks = dict(w_fi=_eighths(g_w_fi), w_fo=_eighths(g_w_fo), w_o=_eighths(g_w_o), w_a=_eighths(g_w_a), w_b=_eighths(g_w_b))
    early = tuple(blocks) if on_mesh else ()
    (dq, dkv, dkvc, dsink), recv_early = _attn_bwd(q, kv, kvc, sink, dya, ya, lse, scatter=[blocks[n] for n in early])
    g_wkv_ctx, st0 = _ctx_bwd(dkvc, ctx, hc, w_kv)
    (dqkv, grad_x, st1), _ = _proj_bwd(dq, dkv, dpb, x, dr1, modx1, w_in, cos, sin, tm_big)
    init = jnp.pad(g_wkv_ctx, ((Q_W, 0), (0, 0)))
    g_w_in = jnp.concatenate([_tn_matmul(dqkv, h, D, "tn_w_in_qkv", BF16, init=init),
                              _tn_matmul(dpb, h, D, "tn_w_in_rest", BF16)], axis=0)

    dbs = jnp.sum(dbs_full.reshape(BLK, N_GRP, GRP_D), axis=2).T
    parts = [jnp.concatenate([st_ffn, st1, st0], axis=0), st4, dsink, dws, dbs]
    blocks["w_in"] = _eighths(g_w_in)
    return grad_x, parts, blocks, dict(zip(early, recv_early))


BIG = ("w_in", "w_a", "w_b", "w_o", "w_fi", "w_fo")
ROW_SHARDED = ("w_o", "w_fo")


def _half_of_shard(shard, c):
    r = shard.shape[0]
    return lax.dynamic_slice_in_dim(shard, c * (r // 2), r // 2, axis=0)


def _eighths(v):
    rows = v.shape[-2] * (v.shape[0] if v.ndim == 3 else 1)
    return v.reshape(N_DEV, rows // N_DEV, v.shape[-1])


def kernel(x, c, ctx, c_ctx, w_ada, b_ada, w_in, attn_sink, gmlp_ln_g, gmlp_ln_b, w_spatial, b_spatial, w_branch_a, w_branch_b, w_out, ln1_g, ln1_b, w_ffn_in, w_ffn_out, ln2_g, ln2_b, loss_target, m_c_ctx, m_w_ada, m_b_ada, m_w_in, m_attn_sink, m_gmlp_ln_g, m_gmlp_ln_b, m_w_spatial, m_b_spatial, m_w_branch_a, m_w_branch_b, m_w_out, m_ln1_g, m_ln1_b, m_w_ffn_in, m_w_ffn_out, m_ln2_g, m_ln2_b, v_c_ctx, v_w_ada, v_b_ada, v_w_in, v_attn_sink, v_gmlp_ln_g, v_gmlp_ln_b, v_w_spatial, v_b_spatial, v_w_branch_a, v_w_branch_b, v_w_out, v_ln1_g, v_ln1_b, v_w_ffn_in, v_w_ffn_out, v_ln2_g, v_ln2_b):
    mx, my, mc = _my_pos()
    me = 4 * mx + 2 * my + mc
    chip = 2 * mx + my
    shards = dict(w_in=w_in[0].T, w_a=w_branch_a[0], w_b=w_branch_b[0], w_o=w_out[0], w_fi=w_ffn_in[0], w_fo=w_ffn_out[0])

    halves = {n: _half_of_shard(shards[n], mc).astype(BF16) for n in BIG}
    c_rows = jnp.concatenate([c, jnp.zeros((7, D), F32)], axis=0)
    g_in, c_g = _all_gather([halves["w_in"], c_rows], me, "gather_w_in")
    wb = dict(w_in=g_in.reshape(IN_W, D))

    c_all = c_g[:, 0, :]
    cc = jnp.concatenate([c_all, c_ctx[None, :], jnp.zeros((7, D), F32)], axis=0)
    sig_cc = jax.nn.sigmoid(cc)
    sc_all = cc * sig_cc
    mod_shard = _ada_fwd(sc_all, w_ada[0])
    mod_g = _all_gather([mod_shard], me, "gather_mod")[0]
    mod_all = jnp.concatenate([mod_g[2 * s] for s in range(4)], axis=1) + b_ada
    mod_x = lax.dynamic_slice_in_dim(mod_all, me, 1, axis=0).reshape(6, D)
    mod_c = mod_all[8].reshape(6, D)[0:2]

    grad_x, parts, blocks, recv = _local_step(
        x[0], ctx[0], loss_target[0], mod_x, mod_c, wb, attn_sink, gmlp_ln_g, gmlp_ln_b, w_spatial[0], b_spatial[0],
        ln1_g, ln1_b, ln2_g, ln2_b, later=halves, me=me)

    (recv["w_in"],), gathered = _scatter_and_gather([blocks["w_in"]], parts, "scatter_w_in_gather_small")
    gathered = _with_own(gathered, parts, me)

    me_arr = jnp.reshape(me, (1,)).astype(jnp.int32)
    summed = {n: _sum_blocks(recv[n], blocks[n], me_arr, "sum_grads_" + n) for n in BIG}
    theirs = dict(zip(BIG, _sibling_exchange([summed[n] for n in BIG], "exchange_grads")))

    sums = _sum8_many(gathered, "sum_small")
    stats = sums[0]
    loss = 0.5 * jnp.sum(stats[ROW_LOSS]) / D
    dmod_x_all = jnp.concatenate([gathered[0][:, r_, :] for r_ in ROWS_DMOD_X], axis=1)
    dmod_c_full = jnp.concatenate([stats[r_] for r_ in ROWS_DMOD_C] + [jnp.zeros((4 * D,), F32)])
    dm_rows = jnp.concatenate([dmod_x_all, dmod_c_full[None, :], jnp.zeros((7, 6 * D), F32)], axis=0)
    cs = w_ada.shape[2]
    dm_shard = lax.dynamic_slice_in_dim(dm_rows, chip * cs, cs, axis=1)
    dmc_shard = jnp.concatenate([dm_shard[8:9], jnp.zeros((7, cs), F32)], axis=0)
    g_w_ada, part = _ada_bwd(sc_all.T, dm_shard, dmc_shard, w_ada[0])
    part_all = _all_gather([part * (mc == 0).astype(F32)], me, "gather_c_ctx")[0]
    dsc = _sum8(part_all, "sum_c_ctx")

    grads = dict(w_ada=g_w_ada[None])
    weights = dict(c_ctx=c_ctx, w_ada=w_ada, b_ada=b_ada, w_in=w_in, attn_sink=attn_sink, gmlp_ln_g=gmlp_ln_g,
                   gmlp_ln_b=gmlp_ln_b, w_spatial=w_spatial, b_spatial=b_spatial, w_branch_a=w_branch_a,
                   w_branch_b=w_branch_b, w_out=w_out, ln1_g=ln1_g, ln1_b=ln1_b, w_ffn_in=w_ffn_in, w_ffn_out=w_ffn_out,
                   ln2_g=ln2_g, ln2_b=ln2_b)
    ms = dict(c_ctx=m_c_ctx, w_ada=m_w_ada, b_ada=m_b_ada, w_in=m_w_in, attn_sink=m_attn_sink, gmlp_ln_g=m_gmlp_ln_g,
              gmlp_ln_b=m_gmlp_ln_b, w_spatial=m_w_spatial, b_spatial=m_b_spatial, w_branch_a=m_w_branch_a,
              w_branch_b=m_w_branch_b, w_out=m_w_out, ln1_g=m_ln1_g, ln1_b=m_ln1_b, w_ffn_in=m_w_ffn_in,
              w_ffn_out=m_w_ffn_out, ln2_g=m_ln2_g, ln2_b=m_ln2_b)
    vs = dict(c_ctx=v_c_ctx, w_ada=v_w_ada, b_ada=v_b_ada, w_in=v_w_in, attn_sink=v_attn_sink, gmlp_ln_g=v_gmlp_ln_g,
              gmlp_ln_b=v_gmlp_ln_b, w_spatial=v_w_spatial, b_spatial=v_b_spatial, w_branch_a=v_w_branch_a,
              w_branch_b=v_w_branch_b, w_out=v_w_out, ln1_g=v_ln1_g, ln1_b=v_ln1_b, w_ffn_in=v_w_ffn_in,
              w_ffn_out=v_w_ffn_out, ln2_g=v_ln2_g, ln2_b=v_ln2_b)
    order = list(weights)
    delta, new_m, new_v = {}, {}, {}
    d_, m_, v_ = _adamw(w_ada[0], g_w_ada, m_w_ada[0], v_w_ada[0], "adamw_w_ada")
    delta["w_ada"], new_m["w_ada"], new_v["w_ada"] = d_[None], m_[None], v_[None]
    c_arr = jnp.reshape(mc, (1,)).astype(jnp.int32)
    names = dict(w_in="w_in", w_a="w_branch_a", w_b="w_branch_b", w_o="w_out", w_fi="w_ffn_in", w_fo="w_ffn_out")
    for k, n in names.items():
        flip = (lambda t: t.T) if k == "w_in" else (lambda t: t)
        outs = _adamw_halves(flip(weights[n][0]), summed[k], theirs[k], flip(ms[n][0]), flip(vs[n][0]), c_arr, "adamw_" + n)
        grads[n], delta[n], new_m[n], new_v[n] = [flip(t)[None] for t in outs]

    def view(a):
        return a.reshape(-1, a.shape[-1]) if a.ndim != 1 else a.reshape(1, -1)

    small = _adamw_small(sums, dsc, *[{n: view(d[n]) for n in SMALL} for d in (weights, ms, vs)])
    for out, src in zip((grads, delta, new_m, new_v), small):
        for n in SMALL:
            out[n] = src[n].reshape(weights[n].shape)

    return (loss, grad_x[None], *[grads[n] for n in order], *[delta[n] for n in order],
            *[new_m[n] for n in order], *[new_v[n] for n in order])
```

```python
import functools
import math

import jax
import jax.numpy as jnp
from jax import lax
from jax.experimental import pallas as pl
from jax.experimental.pallas import tpu as pltpu

F32 = jnp.float32
BF16 = jnp.bfloat16

D = 1024
HEAD = 64
N_KV = 2
GROUP = 4
Q_W = 512
KV_W = 128
G_W = 512
BLK = 128
N_GRP = 8
GRP_D = 64
FH = 2816
IN_W = 3840
GRID_W = 64
ROPE_BASE = 10000.0
LN_EPS = 1e-5
NEG = -1e30
ALPHA = (2 * 1) ** 0.25
SCALE = HEAD ** -0.5
GELU_K = math.sqrt(2.0 / math.pi)
GELU_A = 0.044715
ADAM_LR = 0.001
ADAM_B1 = 0.9
ADAM_B2 = 0.999
ADAM_EPS = 1e-08
ADAM_WD = 0.01
ADAM_STEP = 10
N_DEV = 8
N_SHARD = 4
FH_SHARD = FH // 2
LANES = 128
VMEM_LIMIT = 56 * 1024 * 1024
MESH = pl.DeviceIdType.MESH


def _cp(*sem):
    return pltpu.CompilerParams(dimension_semantics=sem, vmem_limit_bytes=VMEM_LIMIT)


def _resident(shape):
    return pl.BlockSpec(shape, lambda *_: (0,) * len(shape), pipeline_mode=pl.Buffered(1))


def _rows(tm, width):
    return pl.BlockSpec((tm, width), lambda i: (i, 0))


def _acc(shape):
    return pl.BlockSpec(shape, lambda *_: (0,) * len(shape))


def _dot(a, b):
    return jnp.dot(a, b, preferred_element_type=F32)


def _dot_nt(a, b):
    return lax.dot_general(a, b, (((1,), (1,)), ((), ())), preferred_element_type=F32)


def _dot_tn(a, b):
    return lax.dot_general(a, b, (((0,), (0,)), ((), ())), preferred_element_type=F32)


def _ln(x):
    mu = jnp.mean(x, axis=-1, keepdims=True)
    xc = x - mu
    var = jnp.mean(xc * xc, axis=-1, keepdims=True)
    rstd = lax.rsqrt(var + LN_EPS)
    return xc * rstd, rstd


def _ln_bwd(dxhat, xhat, rstd):
    return (dxhat - jnp.mean(dxhat, axis=-1, keepdims=True)
            - xhat * jnp.mean(dxhat * xhat, axis=-1, keepdims=True)) * rstd


def _sig(x):
    return 1.0 / (1.0 + jnp.exp(-x))


def _gelu(x):
    t = jnp.tanh(GELU_K * (x + GELU_A * x * x * x))
    return 0.5 * x * (1.0 + t), t


def _gelu_grad(x, t):
    return 0.5 * (1.0 + t) + 0.5 * x * (1.0 - t * t) * GELU_K * (1.0 + 3.0 * GELU_A * x * x)


def _colsum(v):
    return jnp.sum(v, axis=0, keepdims=True)


def _partner(x):
    w = x.shape[1]
    lane = lax.broadcasted_iota(jnp.int32, x.shape, 1)
    return jnp.where((lane & 31) < 16, pltpu.roll(x, w - 16, 1), pltpu.roll(x, 16, 1))


def _rope(x, cos, sin):
    return x * cos + _partner(x) * sin


def _unrope(g, cos, sin):
    return g * cos + _partner(g * sin)


def _rope_tables(seq):
    inv = ROPE_BASE ** (-jnp.arange(HEAD // 4, dtype=F32) / (HEAD // 4))
    pos = jnp.arange(seq, dtype=jnp.int32)
    ar = (pos // GRID_W).astype(F32)[:, None] * inv
    ac = (pos % GRID_W).astype(F32)[:, None] * inv
    cos = jnp.concatenate([jnp.cos(ar), jnp.cos(ar), jnp.cos(ac), jnp.cos(ac)], axis=-1)
    sin = jnp.concatenate([-jnp.sin(ar), jnp.sin(ar), -jnp.sin(ac), jnp.sin(ac)], axis=-1)
    return jnp.tile(cos, (1, LANES // HEAD)), jnp.tile(sin, (1, LANES // HEAD))


def _ctx_fwd(ctx, modc, w_kv):
    n_ctx = ctx.shape[0]

    def body(ctx_ref, mod_ref, w_ref, hc_ref, kvc_ref, vac_ref):
        xhat, _ = _ln(ctx_ref[...])
        hc = (xhat * (1.0 + mod_ref[1:2, :]) + mod_ref[0:1, :]).astype(BF16)
        hc_ref[...] = hc
        kvc = _dot_nt(hc, w_ref[...]).astype(BF16)
        kvc_ref[...] = kvc
        vac_ref[...] = _with_ones(kvc[:, KV_W:])

    return pl.pallas_call(
        body, name="ctx_fwd", grid=(1,),
        in_specs=[_acc((n_ctx, D)), _acc((8, D)), _acc((2 * KV_W, D))],
        out_specs=[_acc((n_ctx, D)), _acc((n_ctx, 2 * KV_W)), _acc((n_ctx, 2 * LANES))],
        out_shape=[jax.ShapeDtypeStruct((n_ctx, D), BF16), jax.ShapeDtypeStruct((n_ctx, 2 * KV_W), BF16),
                   jax.ShapeDtypeStruct((n_ctx, 2 * LANES), BF16)],
        compiler_params=_cp("arbitrary"),
    )(ctx, modc, w_kv)


def _host_start(step, comm):
    if comm is not None:
        @pl.when(step == 0)
        def _():
            comm.start()


def _host_finish(step, last, comm):
    if comm is not None:
        @pl.when(step == last)
        def _():
            comm.finish()


def _proj_fwd(x, modx, w_in, cos, sin, tm, gather=()):
    seq = x.shape[0]
    ng = len(gather)

    def body(x_ref, mod_ref, w_ref, cos_ref, sin_ref, *rest):
        h_ref, q_ref, kv_ref, va_ref, uv_ref, gab_ref = rest[ng:ng + 6]
        comm = _Gather(rest[:ng], rest[ng + 6:2 * ng + 6], *rest[2 * ng + 6:]) if ng else None
        _host_start(pl.program_id(0), comm)
        xhat, _ = _ln(x_ref[...])
        h = (xhat * (1.0 + mod_ref[1:2, :]) + mod_ref[0:1, :]).astype(BF16)
        h_ref[...] = h
        cos1, sin1 = cos_ref[...], sin_ref[...]
        cos2 = jnp.concatenate([cos1, cos1], axis=1)
        sin2 = jnp.concatenate([sin1, sin1], axis=1)
        for j in range(Q_W // 256):
            t = _dot_nt(h, w_ref[256 * j:256 * (j + 1), :])
            q_ref[:, 256 * j:256 * (j + 1)] = (_rope(t, cos2, sin2) * SCALE).astype(BF16)
        t = _dot_nt(h, w_ref[Q_W:Q_W + 2 * KV_W, :])
        kv_ref[:, :KV_W] = _rope(t[:, :KV_W], cos1, sin1).astype(BF16)
        v = t[:, KV_W:].astype(BF16)
        kv_ref[:, KV_W:] = v
        va_ref[...] = _with_ones(v)
        o = Q_W + 2 * KV_W
        for j in range(2):
            uv_ref[:, G_W * j:G_W * (j + 1)] = _dot_nt(h, w_ref[o + G_W * j:o + G_W * (j + 1), :])
        o += 2 * G_W
        for j in range(4):
            gab_ref[:, 512 * j:512 * (j + 1)] = _dot_nt(h, w_ref[o + 512 * j:o + 512 * (j + 1), :]).astype(BF16)
        _host_finish(pl.program_id(0), seq // tm - 1, comm)

    out = pl.pallas_call(
        body, name="proj_fwd", grid=(seq // tm,),
        in_specs=[_rows(tm, D), _acc((8, D)), _resident((IN_W, D)), _rows(tm, LANES), _rows(tm, LANES)] + _comm_specs(ng),
        out_specs=[_rows(tm, D), _rows(tm, Q_W), _rows(tm, 2 * KV_W), _rows(tm, 2 * LANES), _rows(tm, 2 * G_W),
                   _rows(tm, 2 * D)] + _comm_specs(ng),
        out_shape=[jax.ShapeDtypeStruct((seq, D), BF16), jax.ShapeDtypeStruct((seq, Q_W), BF16),
                   jax.ShapeDtypeStruct((seq, 2 * KV_W), BF16), jax.ShapeDtypeStruct((seq, 2 * LANES), BF16),
                   jax.ShapeDtypeStruct((seq, 2 * G_W), F32), jax.ShapeDtypeStruct((seq, 2 * D), BF16)] + _gathered_shapes(gather),
        scratch_shapes=_comm_scratch(ng) if ng else [],
        compiler_params=_cp("arbitrary"),
    )(x, modx, w_in, cos, sin, *gather)
    return out[:6], out[6:]


def _stack_heads(x, hk):
    return jnp.concatenate([x[:, (hk * GROUP + g) * HEAD:(hk * GROUP + g + 1) * HEAD] for g in range(GROUP)], axis=0)


def _attn_scores(q, k_refs, hk, n, nb):
    q4 = _stack_heads(q, hk)
    ks = [r[:, hk * HEAD:(hk + 1) * HEAD] for r in k_refs]
    rows = GROUP * BLK
    qi = lax.broadcasted_iota(jnp.int32, (rows, BLK), 0) & (BLK - 1)
    kj = lax.broadcasted_iota(jnp.int32, (rows, BLK), 1)
    s = [_dot_nt(q4, k) for k in ks]
    s[1] = jnp.where((kj >= qi) & (n > 0), s[1], NEG)
    s[3] = jnp.where((kj <= qi) & (n < nb - 1), s[3], NEG)
    return q4, ks, s


def _sink_rows(sink_ref, hk):
    rows = GROUP * BLK
    rg = lax.broadcasted_iota(jnp.int32, (rows, 1), 0) >> 7
    sink_v = jnp.full((rows, 1), sink_ref[0, hk * GROUP], F32)
    for g in range(1, GROUP):
        sink_v = jnp.where(rg == g, sink_ref[0, hk * GROUP + g], sink_v)
    return sink_v


def _with_ones(v):
    ones = jnp.ones((v.shape[0], HEAD), v.dtype)
    return jnp.concatenate([v[:, :HEAD], ones, v[:, HEAD:], ones], axis=1)


def _kv_specs(nb, qb):
    def spec(d):
        return pl.BlockSpec((BLK, 2 * KV_W), lambda n: (jnp.clip(qb * n + d, 0, nb - 1), 0))
    return [spec(d) for d in range(-1, qb + 1)]


def _attn_fwd(q, kv, va, kvc, vac, sink, gather=()):
    seq = q.shape[0]
    nb = seq // BLK
    n_ctx = kvc.shape[0]
    ng = len(gather)
    Q_BLOCKS = 1
    nkv = Q_BLOCKS + 2
    steps = nb // Q_BLOCKS

    def body(q_ref, *rest):
        kv_refs, va_refs = rest[:nkv], rest[nkv:2 * nkv]
        kvc_ref, vac_ref, sink_ref = rest[2 * nkv:2 * nkv + 3]
        rest = rest[2 * nkv + 3:]
        o_ref, lse_ref = rest[ng:ng + 2]
        comm = _Gather(rest[:ng], rest[ng + 2:2 * ng + 2], *rest[2 * ng + 2:]) if ng else None
        n = pl.program_id(0)
        _host_start(n, comm)
        lane = lax.broadcasted_iota(jnp.int32, (BLK, LANES), 1)
        for sub in range(Q_BLOCKS):
            rs = slice(sub * BLK, (sub + 1) * BLK)
            q = q_ref[rs, :]
            outs = []
            lse_all = jnp.zeros((BLK, LANES), F32)
            for hk in range(N_KV):
                _, _, s = _attn_scores(q, (kvc_ref,) + kv_refs[sub:sub + 3], hk, Q_BLOCKS * n + sub, nb)
                sink_v = _sink_rows(sink_ref, hk)
                m = sink_v
                for t in s:
                    m = jnp.maximum(m, jnp.max(t, axis=-1, keepdims=True))
                o = jnp.zeros((GROUP * BLK, LANES), F32)
                for t, va_ref in zip(s, (vac_ref,) + va_refs[sub:sub + 3]):
                    o = o + _dot(jnp.exp((t - m).astype(BF16)), va_ref[:, hk * LANES:(hk + 1) * LANES])
                denom = o[:, HEAD:HEAD + 1] + jnp.exp(sink_v - m)
                o4 = o[:, :HEAD] * (1.0 / denom)
                lse4 = m + jnp.log(denom)
                for g in range(GROUP):
                    outs.append(o4[g * BLK:(g + 1) * BLK, :])
                    lse_all = jnp.where(lane == hk * GROUP + g, lse4[g * BLK:(g + 1) * BLK, :], lse_all)
            o_ref[rs, :] = jnp.concatenate(outs, axis=1).astype(BF16)
            lse_ref[rs, :] = lse_all
        _host_finish(n, steps - 1, comm)

    tq = Q_BLOCKS * BLK
    out = pl.pallas_call(
        body, name="attn_fwd", grid=(steps,),
        in_specs=[_rows(tq, Q_W)] + _kv_specs(nb, Q_BLOCKS) + _kv_specs(nb, Q_BLOCKS)
        + [_acc((n_ctx, 2 * KV_W)), _acc((n_ctx, 2 * LANES)), pl.BlockSpec(memory_space=pltpu.SMEM)] + _comm_specs(ng),
        out_specs=[_rows(tq, Q_W), _rows(tq, LANES)] + _comm_specs(ng),
        out_shape=[jax.ShapeDtypeStruct((seq, Q_W), BF16), jax.ShapeDtypeStruct((seq, LANES), F32)] + _gathered_shapes(gather),
        scratch_shapes=_comm_scratch(ng) if ng else [],
        compiler_params=_cp("arbitrary"),
    )(q, *([kv] * nkv), *([va] * nkv), kvc, vac, sink, *gather)
    return out[:2], out[2:]


def _gmlp_chunk(u, vb, gp_ref, ws_ref, bias_ref):
    gu, tu = _gelu(u)
    gv, tv = _gelu(vb)
    vhat, rstd = _ln(gv)
    vn = (vhat * gp_ref[0:1, :] + gp_ref[1:2, :]).astype(BF16)
    s = bias_ref[...] + jnp.concatenate(
        [_dot(ws_ref[g * BLK:(g + 1) * BLK, :], vn[:, g * GRP_D:(g + 1) * GRP_D]) for g in range(N_GRP)], axis=1)
    return gu, tu, tv, vhat, rstd, vn, s


def _mix_fwd(uv, gab, ya, gp, ws_stack, bias_full, w_a, w_b, w_o, tm):
    seq = uv.shape[0]

    def body(uv_ref, gab_ref, ya_ref, gp_ref, ws_ref, bias_ref, wa_ref, wb_ref, wo_ref,
             a_ref, b_ref, mix_ref, merged_ref, yb_ref):
        for c in range(tm // BLK):
            rs = slice(c * BLK, (c + 1) * BLK)
            gu, _, _, _, _, _, s = _gmlp_chunk(uv_ref[rs, :G_W], uv_ref[rs, G_W:], gp_ref, ws_ref, bias_ref)
            yb_ref[rs, :] = (gu * s).astype(BF16)
        ya = ya_ref[...]
        yb = yb_ref[...]
        for s in range(N_SHARD):
            cs = slice(s * (D // N_SHARD), (s + 1) * (D // N_SHARD))
            a = _dot(ya, wa_ref[s])
            b = _dot(yb, wb_ref[s])
            a_ref[:, cs] = a.astype(BF16)
            b_ref[:, cs] = b.astype(BF16)
            ga = gab_ref[:, cs].astype(F32)
            gb = gab_ref[:, D + s * (D // N_SHARD):D + (s + 1) * (D // N_SHARD)].astype(F32)
            merged_ref[:, cs] = (_sig(ga) * a + _sig(gb) * b).astype(BF16)
        mix_ref[...] = _dot(merged_ref[...], wo_ref[...])

    return pl.pallas_call(
        body, name="mix_fwd", grid=(seq // tm,),
        in_specs=[_rows(tm, 2 * G_W), _rows(tm, 2 * D), _rows(tm, Q_W), _acc((8, G_W)),
                  _resident((N_GRP * BLK, BLK)), _acc((BLK, G_W)),
                  _resident((N_SHARD, Q_W, D // N_SHARD)), _resident((N_SHARD, G_W, D // N_SHARD)), _resident((D, D))],
        out_specs=[_rows(tm, D), _rows(tm, D), _rows(tm, D), _rows(tm, D), _rows(tm, G_W)],
        out_shape=[jax.ShapeDtypeStruct((seq, D), BF16), jax.ShapeDtypeStruct((seq, D), BF16),
                   jax.ShapeDtypeStruct((seq, D), F32), jax.ShapeDtypeStruct((seq, D), BF16),
                   jax.ShapeDtypeStruct((seq, G_W), BF16)],
        compiler_params=_cp("arbitrary"),
    )(uv, gab, ya, gp, ws_stack, bias_full, w_a, w_b, w_o)


FFN_CHUNK = 512


def _ffn_chunks():
    out = []
    for hh in range(2):
        off = 0
        while off < FH_SHARD:
            w = min(FFN_CHUNK, FH_SHARD - off)
            out.append((hh, off, w))
            off += w
    return out


def _mid_recompute(x_ref, mix_ref, vec_ref):
    r1 = ALPHA * x_ref[...] + vec_ref[0:1, :] * mix_ref[...]
    xh1, rstd1 = _ln(r1)
    xmid = xh1 * vec_ref[1:2, :] + vec_ref[2:3, :]
    xh2, rstd2 = _ln(xmid)
    return xh1, rstd1, xmid, xh2, rstd2


def _ffn(x, mix, tgt, vec, w_fi, w_fo, tm):
    seq = x.shape[0]

    def body(x_ref, mix_ref, tgt_ref, vec_ref, wi_ref, wo_ref, act_ref, h2_ref, dff_ref, df_ref, dr1_ref, st_ref, gu_ref):
        @pl.when(pl.program_id(0) == 0)
        def _():
            st_ref[...] = jnp.zeros_like(st_ref)

        xh1, rstd1, xmid, xh2, rstd2 = _mid_recompute(x_ref, mix_ref, vec_ref)
        h2 = (xh2 * (1.0 + vec_ref[4:5, :]) + vec_ref[3:4, :]).astype(BF16)
        h2_ref[...] = h2
        f = jnp.zeros((tm, D), F32)
        for hh, off, w in _ffn_chunks():
            cs = slice(hh * FH_SHARD + off, hh * FH_SHARD + off + w)
            cu = slice(FH + hh * FH_SHARD + off, FH + hh * FH_SHARD + off + w)
            g = _dot(h2, wi_ref[hh, :, off:off + w])
            u = _dot(h2, wi_ref[2 + hh, :, off:off + w])
            gu_ref[:, cs] = g
            gu_ref[:, cu] = u
            a = (g * _sig(g) * u).astype(BF16)
            act_ref[:, cs] = a
            f = f + _dot(a, wo_ref[cs, :])
        r2 = ALPHA * xmid + vec_ref[5:6, :] * f
        yh, rstd = _ln(r2)
        y = yh * vec_ref[6:7, :] + vec_ref[7:8, :]
        err = y - tgt_ref[...]
        dy = err / D
        dr2 = _ln_bwd(dy * vec_ref[6:7, :], yh, rstd)
        st_ref[0:1, :] += _colsum(err * err)
        st_ref[1:2, :] += _colsum(dy * yh)
        st_ref[2:3, :] += _colsum(dy)
        st_ref[3:4, :] += _colsum(dr2 * f)

        df = (dr2 * vec_ref[5:6, :]).astype(BF16)
        df_ref[...] = df
        dh2 = jnp.zeros((tm, D), F32)
        for hh, off, w in _ffn_chunks():
            cs = slice(hh * FH_SHARD + off, hh * FH_SHARD + off + w)
            cu = slice(FH + hh * FH_SHARD + off, FH + hh * FH_SHARD + off + w)
            da = _dot_nt(df, wo_ref[cs, :])
            g = gu_ref[:, cs]
            u = gu_ref[:, cu]
            sg = _sig(g)
            dg = (da * u * sg * (1.0 + g * (1.0 - sg))).astype(BF16)
            du = (da * g * sg).astype(BF16)
            dff_ref[:, cs] = dg
            dff_ref[:, cu] = du
            dh2 = dh2 + _dot_nt(dg, wi_ref[hh, :, off:off + w]) + _dot_nt(du, wi_ref[2 + hh, :, off:off + w])
        dxmid = _ln_bwd(dh2 * (1.0 + vec_ref[4:5, :]), xh2, rstd2) + ALPHA * dr2
        dr1 = _ln_bwd(dxmid * vec_ref[1:2, :], xh1, rstd1)
        dr1_ref[...] = dr1
        st_ref[8:9, :] += _colsum(dh2 * xh2)
        st_ref[9:10, :] += _colsum(dh2)
        st_ref[10:11, :] += _colsum(dxmid * xh1)
        st_ref[11:12, :] += _colsum(dxmid)
        st_ref[12:13, :] += _colsum(dr1 * mix_ref[...])

    return pl.pallas_call(
        body, name="ffn", grid=(seq // tm,),
        in_specs=[_rows(tm, D), _rows(tm, D), _rows(tm, D), _acc((8, D)), _resident((N_SHARD, D, FH_SHARD)), _resident((FH, D))],
        out_specs=[_rows(tm, FH), _rows(tm, D), _rows(tm, 2 * FH), _rows(tm, D), _rows(tm, D), _acc((16, D))],
        out_shape=[jax.ShapeDtypeStruct((seq, FH), BF16), jax.ShapeDtypeStruct((seq, D), BF16),
                   jax.ShapeDtypeStruct((seq, 2 * FH), BF16), jax.ShapeDtypeStruct((seq, D), BF16),
                   jax.ShapeDtypeStruct((seq, D), F32), jax.ShapeDtypeStruct((16, D), F32)],
        scratch_shapes=[pltpu.VMEM((tm, 2 * FH), F32)],
        compiler_params=_cp("arbitrary"),
    )(x, mix, tgt, vec, w_fi, w_fo)


def _mix_bwd(dr1, a, b, gab, uv, merged, ya, yb, vec, gp, ws_stack, ws_stack_t, bias_full, w_a, w_b, w_o, tm):
    seq = dr1.shape[0]
    last = seq // tm - 1
    cw = D // N_SHARD

    def body(dr1_ref, a_ref, b_ref, gab_ref, uv_ref, mg_ref, ya_ref, yb_ref, vec_ref, gp_ref, ws_ref, wst_ref, bias_ref,
             wa_ref, wb_ref, wo_ref, dya_ref, dp_ref, dws_ref, dbs_ref, st_ref, gwo_ref, gwa_ref, gwb_ref,
             acc_o, acc_a, acc_b):
        @pl.when(pl.program_id(0) == 0)
        def _():
            dws_ref[...] = jnp.zeros_like(dws_ref)
            dbs_ref[...] = jnp.zeros_like(dbs_ref)
            st_ref[...] = jnp.zeros_like(st_ref)
            acc_o[...] = jnp.zeros_like(acc_o)
            acc_a[...] = jnp.zeros_like(acc_a)
            acc_b[...] = jnp.zeros_like(acc_b)

        dmix = (dr1_ref[...] * vec_ref[0:1, :]).astype(BF16)
        acc_o[...] += _dot_tn(mg_ref[...], dmix)
        dmerged = _dot_nt(dmix, wo_ref[...])
        sa = _sig(gab_ref[:, :D].astype(F32))
        sb = _sig(gab_ref[:, D:].astype(F32))
        da = (dmerged * sa).astype(BF16)
        db = (dmerged * sb).astype(BF16)
        dp_ref[:, 2 * G_W:2 * G_W + D] = (dmerged * a_ref[...].astype(F32) * sa * (1.0 - sa)).astype(BF16)
        dp_ref[:, 2 * G_W + D:] = (dmerged * b_ref[...].astype(F32) * sb * (1.0 - sb)).astype(BF16)
        dya = jnp.zeros((tm, Q_W), F32)
        dyb = jnp.zeros((tm, G_W), F32)
        ya = ya_ref[...]
        yb = yb_ref[...]
        for s in range(N_SHARD):
            cs = slice(s * cw, (s + 1) * cw)
            dya = dya + _dot_nt(da[:, cs], wa_ref[s])
            dyb = dyb + _dot_nt(db[:, cs], wb_ref[s])
            acc_a[s] += _dot_tn(ya, da[:, cs])
            acc_b[s] += _dot_tn(yb, db[:, cs])
        dya_ref[...] = dya.astype(BF16)

        @pl.when(pl.program_id(0) == last)
        def _():
            gwo_ref[...] = acc_o[...].astype(BF16)
            gwa_ref[...] = acc_a[...].astype(BF16)
            gwb_ref[...] = acc_b[...].astype(BF16)

        for c in range(tm // BLK):
            rs = slice(c * BLK, (c + 1) * BLK)
            u = uv_ref[rs, :G_W]
            vb = uv_ref[rs, G_W:]
            gu, tu, tv, vhat, rstd, vn, s = _gmlp_chunk(u, vb, gp_ref, ws_ref, bias_ref)
            dyb_c = dyb[rs, :]
            ds = dyb_c * gu
            du = dyb_c * s * _gelu_grad(u, tu)
            ds_b = ds.astype(BF16)
            dvn_g = []
            for g in range(N_GRP):
                cg = slice(g * GRP_D, (g + 1) * GRP_D)
                dvn_g.append(_dot(wst_ref[:, g * BLK:(g + 1) * BLK], ds_b[:, cg]))
                dws_ref[g * BLK:(g + 1) * BLK, :] += _dot_nt(ds_b[:, cg], vn[:, cg])
            dvn = jnp.concatenate(dvn_g, axis=1)
            dbs_ref[...] += ds
            st_ref[0:1, :] += _colsum(dvn * vhat)
            st_ref[1:2, :] += _colsum(dvn)
            dgv = _ln_bwd(dvn * gp_ref[0:1, :], vhat, rstd)
            dvb = dgv * _gelu_grad(vb, tv)
            dp_ref[rs, :G_W] = du.astype(BF16)
            dp_ref[rs, G_W:2 * G_W] = dvb.astype(BF16)

    pw = 2 * G_W + 2 * D
    return pl.pallas_call(
        body, name="mix_bwd", grid=(seq // tm,),
        in_specs=[_rows(tm, D), _rows(tm, D), _rows(tm, D), _rows(tm, 2 * D), _rows(tm, 2 * G_W), _rows(tm, D), _rows(tm, Q_W),
                  _rows(tm, G_W), _acc((8, D)), _acc((8, G_W)),
                  _resident((N_GRP * BLK, BLK)), _resident((BLK, N_GRP * BLK)), _acc((BLK, G_W)),
                  _resident((N_SHARD, Q_W, cw)), _resident((N_SHARD, G_W, cw)), _resident((D, D))],
        out_specs=[_rows(tm, Q_W), _rows(tm, pw), _acc((N_GRP * BLK, BLK)), _acc((BLK, G_W)), _acc((8, G_W)),
                   _acc((D, D)), _acc((N_SHARD, Q_W, cw)), _acc((N_SHARD, G_W, cw))],
        out_shape=[jax.ShapeDtypeStruct((seq, Q_W), BF16), jax.ShapeDtypeStruct((seq, pw), BF16),
                   jax.ShapeDtypeStruct((N_GRP * BLK, BLK), F32), jax.ShapeDtypeStruct((BLK, G_W), F32),
                   jax.ShapeDtypeStruct((8, G_W), F32), jax.ShapeDtypeStruct((D, D), BF16),
                   jax.ShapeDtypeStruct((N_SHARD, Q_W, cw), BF16), jax.ShapeDtypeStruct((N_SHARD, G_W, cw), BF16)],
        scratch_shapes=[pltpu.VMEM((D, D), F32), pltpu.VMEM((N_SHARD, Q_W, cw), F32), pltpu.VMEM((N_SHARD, G_W, cw), F32)],
        compiler_params=_cp("arbitrary"),
    )(dr1, a, b, gab, uv, merged, ya, yb, vec, gp, ws_stack, ws_stack_t, bias_full, w_a, w_b, w_o)


def _attn_bwd(q, kv, kvc, sink, dya, ya, lse, scatter=()):
    seq = q.shape[0]
    nb = seq // BLK
    n_ctx = kvc.shape[0]
    ns = len(scatter)
    Q_BLOCKS = 2
    nkv = Q_BLOCKS + 2
    steps = nb // Q_BLOCKS

    def body(q_ref, *rest):
        kv_refs = rest[:nkv]
        kvc_ref, sink_ref, do_ref, o_ref, lse_ref = rest[nkv:nkv + 5]
        rest = rest[nkv + 5:]
        dq_ref, dkv_ref, dkvc_ref, dsink_ref = rest[ns:ns + 4]
        comm = _AllToAll(rest[:ns], rest[ns + 4:2 * ns + 4], *rest[2 * ns + 4:]) if ns else None
        n = pl.program_id(0)
        _host_start(n, comm)

        @pl.when(n == 0)
        def _():
            dkv_ref[...] = jnp.zeros_like(dkv_ref)
            dkvc_ref[...] = jnp.zeros_like(dkvc_ref)
            dsink_ref[...] = jnp.zeros_like(dsink_ref)

        lane = lax.broadcasted_iota(jnp.int32, (1, LANES), 1)
        for sub in range(Q_BLOCKS):
            rs = slice(sub * BLK, (sub + 1) * BLK)
            blk = Q_BLOCKS * n + sub
            q = q_ref[rs, :]
            do = do_ref[rs, :]
            out = o_ref[rs, :]
            lse_all = lse_ref[rs, :]
            k_refs = (kvc_ref,) + kv_refs[sub:sub + 3]
            dqs, dks, dvs = [], [], []
            for hk in range(N_KV):
                q4, ks, s = _attn_scores(q, k_refs, hk, blk, nb)
                vs = [r[:, KV_W + hk * HEAD:KV_W + (hk + 1) * HEAD] for r in k_refs]
                lse4 = jnp.concatenate([lse_all[:, hk * GROUP + g:hk * GROUP + g + 1] for g in range(GROUP)], axis=0)
                do4 = _stack_heads(do, hk)
                delta = jnp.sum(do4.astype(F32) * _stack_heads(out, hk).astype(F32), axis=-1, keepdims=True)
                p = [jnp.exp((t - lse4).astype(BF16)) for t in s]
                ds = [t * (_dot_nt(do4, v) - delta).astype(BF16) for t, v in zip(p, vs)]
                dq4 = _dot(ds[0], ks[0])
                for t, k in zip(ds[1:], ks[1:]):
                    dq4 = dq4 + _dot(t, k)
                dq4 = dq4 * SCALE
                dqs += [dq4[g * BLK:(g + 1) * BLK, :] for g in range(GROUP)]
                dks.append([_dot_tn(t, q4) for t in ds])
                dvs.append([_dot_tn(t, do4) for t in p])
                ps = jnp.exp(_sink_rows(sink_ref, hk) - lse4) * delta
                for g in range(GROUP):
                    part = -jnp.sum(ps[g * BLK:(g + 1) * BLK, :], axis=0, keepdims=True)
                    dsink_ref[0:1, :] += jnp.where(lane == hk * GROUP + g, part, 0.0)
            dq_ref[rs, :] = jnp.concatenate(dqs, axis=1)

            def piece(i):
                return jnp.concatenate([dks[0][i], dks[1][i], dvs[0][i], dvs[1][i]], axis=1)

            dkvc_ref[...] += piece(0)
            starts = (jnp.maximum(blk - 1, 0), blk, jnp.minimum(blk + 1, nb - 1))
            for i, st in enumerate(starts):
                r = pl.ds(pl.multiple_of(st * BLK, BLK), BLK)
                dkv_ref[r, :] += piece(i + 1)
        _host_finish(n, steps - 1, comm)

    tq = Q_BLOCKS * BLK
    out = pl.pallas_call(
        body, name="attn_bwd", grid=(steps,),
        in_specs=[_rows(tq, Q_W)] + _kv_specs(nb, Q_BLOCKS) + [_acc((n_ctx, 2 * KV_W)), pl.BlockSpec(memory_space=pltpu.SMEM),
                                                     _rows(tq, Q_W), _rows(tq, Q_W), _rows(tq, LANES)] + _comm_specs(ns),
        out_specs=[_rows(tq, Q_W), _acc((seq, 2 * KV_W)), _acc((n_ctx, 2 * KV_W)), _acc((8, LANES))] + _comm_specs(ns),
        out_shape=[jax.ShapeDtypeStruct((seq, Q_W), F32), jax.ShapeDtypeStruct((seq, 2 * KV_W), F32),
                   jax.ShapeDtypeStruct((n_ctx, 2 * KV_W), F32), jax.ShapeDtypeStruct((8, LANES), F32)]
        + [jax.ShapeDtypeStruct(v.shape, v.dtype) for v in scatter],
        scratch_shapes=_comm_scratch(ns) if ns else [],
        compiler_params=_cp("arbitrary"),
    )(q, *([kv] * nkv), kvc, sink, dya, ya, lse, *scatter)
    return out[:4], out[4:]


def _proj_bwd(dq, dkv, dpb, x, dr1, modx, w_in, cos, sin, tm, scatter=()):
    seq = x.shape[0]
    pw = IN_W - Q_W - 2 * KV_W
    ns = len(scatter)

    def body(dq_ref, dkv_ref, dpb_ref, x_ref, dr1_ref, mod_ref, w_ref, cos_ref, sin_ref, *rest):
        dqkv_ref, gx_ref, st_ref = rest[ns:ns + 3]
        comm = _AllToAll(rest[:ns], rest[ns + 3:2 * ns + 3], *rest[2 * ns + 3:]) if ns else None
        _host_start(pl.program_id(0), comm)

        @pl.when(pl.program_id(0) == 0)
        def _():
            st_ref[...] = jnp.zeros_like(st_ref)

        cos1, sin1 = cos_ref[...], sin_ref[...]
        cos2 = jnp.concatenate([cos1, cos1], axis=1)
        sin2 = jnp.concatenate([sin1, sin1], axis=1)
        for j in range(Q_W // 256):
            cs = slice(256 * j, 256 * (j + 1))
            dqkv_ref[:, cs] = _unrope(dq_ref[:, cs], cos2, sin2).astype(BF16)
        dqkv_ref[:, Q_W:Q_W + KV_W] = _unrope(dkv_ref[:, :KV_W], cos1, sin1).astype(BF16)
        dqkv_ref[:, Q_W + KV_W:] = dkv_ref[:, KV_W:].astype(BF16)
        o = Q_W + 2 * KV_W
        dh = _dot(dqkv_ref[...], w_ref[:o, :]) + _dot(dpb_ref[...], w_ref[o:, :])
        xhat, rstd = _ln(x_ref[...])
        st_ref[0:1, :] += _colsum(dh)
        st_ref[1:2, :] += _colsum(dh * xhat)
        gx_ref[...] = _ln_bwd(dh * (1.0 + mod_ref[1:2, :]), xhat, rstd) + ALPHA * dr1_ref[...]
        _host_finish(pl.program_id(0), seq // tm - 1, comm)

    out = pl.pallas_call(
        body, name="proj_bwd", grid=(seq // tm,),
        in_specs=[_rows(tm, Q_W), _rows(tm, 2 * KV_W), _rows(tm, pw), _rows(tm, D), _rows(tm, D), _acc((8, D)),
                  _resident((IN_W, D)), _rows(tm, LANES), _rows(tm, LANES)] + _comm_specs(ns),
        out_specs=[_rows(tm, Q_W + 2 * KV_W), _rows(tm, D), _acc((8, D))] + _comm_specs(ns),
        out_shape=[jax.ShapeDtypeStruct((seq, Q_W + 2 * KV_W), BF16), jax.ShapeDtypeStruct((seq, D), F32),
                   jax.ShapeDtypeStruct((8, D), F32)] + [jax.ShapeDtypeStruct(v.shape, v.dtype) for v in scatter],
        scratch_shapes=_comm_scratch(ns) if ns else [],
        compiler_params=_cp("arbitrary"),
    )(dq, dkv, dpb, x, dr1, modx, w_in, cos, sin, *scatter)
    return out[:3], out[3:]


def _ctx_bwd(dkvc, ctx, hc, w_kv):
    n_ctx = ctx.shape[0]

    def body(dkvc_ref, ctx_ref, hc_ref, w_ref, dw_ref, st_ref):
        d = dkvc_ref[...].astype(BF16)
        dw_ref[...] = _dot_tn(d, hc_ref[...])
        dhc = _dot(d, w_ref[...])
        xhat, _ = _ln(ctx_ref[...])
        st_ref[...] = jnp.zeros_like(st_ref)
        st_ref[0:1, :] = _colsum(dhc)
        st_ref[1:2, :] = _colsum(dhc * xhat)

    return pl.pallas_call(
        body, name="ctx_bwd", grid=(1,),
        in_specs=[_acc((n_ctx, 2 * KV_W)), _acc((n_ctx, D)), _acc((n_ctx, D)), _acc((2 * KV_W, D))],
        out_specs=[_acc((2 * KV_W, D)), _acc((8, D))],
        out_shape=[jax.ShapeDtypeStruct((2 * KV_W, D), F32), jax.ShapeDtypeStruct((8, D), F32)],
        compiler_params=_cp("arbitrary"),
    )(dkvc, ctx, hc, w_kv)


def _tn_matmul(a, b, tn, name, out_dtype, shard_major=False, init=None, tk=512):
    t, ka = a.shape
    n = b.shape[1]
    tk = min(tk, t)
    nk = t // tk
    has_init = init is not None

    def body(*refs):
        if has_init:
            a_ref, b_ref, i_ref, o_ref, acc_ref = refs
        else:
            a_ref, b_ref, o_ref, acc_ref = refs
        k = pl.program_id(1)

        @pl.when(k == 0)
        def _():
            acc_ref[...] = i_ref[...] if has_init else jnp.zeros_like(acc_ref)

        acc_ref[...] += _dot_tn(a_ref[...], b_ref[...])

        @pl.when(k == nk - 1)
        def _():
            o_ref[...] = acc_ref[...].astype(out_dtype)

    in_specs = [pl.BlockSpec((tk, ka), lambda j, k: (k, 0)), pl.BlockSpec((tk, tn), lambda j, k: (k, j))]
    args = [a, b]
    if has_init:
        in_specs.append(pl.BlockSpec((ka, tn), lambda j, k: (0, j)))
        args.append(init)
    if shard_major:
        out_spec = pl.BlockSpec((None, ka, tn), lambda j, k: (j, 0, 0))
        out_shape = jax.ShapeDtypeStruct((n // tn, ka, tn), out_dtype)
    else:
        out_spec = pl.BlockSpec((ka, tn), lambda j, k: (0, j))
        out_shape = jax.ShapeDtypeStruct((ka, n), out_dtype)
    return pl.pallas_call(
        body, name=name, grid=(n // tn, nk), in_specs=in_specs, out_specs=out_spec, out_shape=out_shape,
        scratch_shapes=[pltpu.VMEM((ka, tn), F32)],
        compiler_params=_cp("arbitrary", "arbitrary"),
    )(*args)


ADA_TILE = 512


def _ada_fwd(sc_all, w_ada):
    cs = w_ada.shape[1]

    def body(s_ref, w_ref, o_ref):
        o_ref[...] = _dot(s_ref[...].astype(BF16), w_ref[...].astype(BF16))

    return pl.pallas_call(
        body, name="ada_fwd", grid=(cs // ADA_TILE,),
        in_specs=[_acc((16, D)), pl.BlockSpec((D, ADA_TILE), lambda j: (0, j))],
        out_specs=pl.BlockSpec((16, ADA_TILE), lambda j: (0, j)),
        out_shape=jax.ShapeDtypeStruct((16, cs), F32),
        compiler_params=_cp("arbitrary"),
    )(sc_all, w_ada)


def _ada_bwd(sc_all_t, dm_all, dmc, w_ada):
    cs = w_ada.shape[1]

    def body(st_ref, dm_ref, dmc_ref, w_ref, gw_ref, part_ref):
        @pl.when(pl.program_id(0) == 0)
        def _():
            part_ref[...] = jnp.zeros_like(part_ref)

        gw_ref[...] = _dot(st_ref[...].astype(BF16), dm_ref[...].astype(BF16))
        part_ref[...] += _dot_nt(dmc_ref[...].astype(BF16), w_ref[...].astype(BF16))

    return pl.pallas_call(
        body, name="ada_bwd", grid=(cs // ADA_TILE,),
        in_specs=[_acc((D, 16)), pl.BlockSpec((16, ADA_TILE), lambda j: (0, j)), pl.BlockSpec((8, ADA_TILE), lambda j: (0, j)),
                  pl.BlockSpec((D, ADA_TILE), lambda j: (0, j))],
        out_specs=[pl.BlockSpec((D, ADA_TILE), lambda j: (0, j)), _acc((8, D))],
        out_shape=[jax.ShapeDtypeStruct((D, cs), F32), jax.ShapeDtypeStruct((8, D), F32)],
        compiler_params=_cp("arbitrary"),
    )(sc_all_t, dm_all, dmc, w_ada)


def _sum8(x, name, tr=256):
    _, r, c = x.shape
    tr = min(tr, r)
    while r % tr:
        tr -= 16

    def body(x_ref, o_ref):
        acc = x_ref[0].astype(F32)
        for i in range(1, N_DEV):
            acc = acc + x_ref[i].astype(F32)
        o_ref[...] = acc

    return pl.pallas_call(
        body, name=name, grid=(r // tr,),
        in_specs=[pl.BlockSpec((N_DEV, tr, c), lambda i: (0, i, 0))],
        out_specs=pl.BlockSpec((tr, c), lambda i: (i, 0)),
        out_shape=jax.ShapeDtypeStruct((r, c), F32),
        compiler_params=_cp("arbitrary"),
    )(x)


def _sum_blocks(recv, src, me, name, tr=256):
    _, r, c = recv.shape
    tr = min(tr, r)
    while r % tr:
        tr -= 16

    def body(me_ref, recv_ref, own_ref, o_ref):
        acc = own_ref[...].astype(F32)
        for k in range(1, N_DEV):
            acc = acc + recv_ref[me_ref[0] ^ k].astype(F32)
        o_ref[...] = acc

    return pl.pallas_call(
        body, name=name,
        grid_spec=pltpu.PrefetchScalarGridSpec(
            num_scalar_prefetch=1, grid=(r // tr,),
            in_specs=[pl.BlockSpec((N_DEV, tr, c), lambda i, me_ref: (0, i, 0)),
                      pl.BlockSpec((None, tr, c), lambda i, me_ref: (me_ref[0], i, 0))],
            out_specs=pl.BlockSpec((tr, c), lambda i, me_ref: (i, 0))),
        out_shape=jax.ShapeDtypeStruct((r, c), F32),
        compiler_params=_cp("arbitrary"),
    )(me, recv, src)


def _sum8_many(xs, name):
    n = len(xs)

    def body(*refs):
        for x_ref, o_ref in zip(refs[:n], refs[n:]):
            acc = x_ref[0]
            for i in range(1, N_DEV):
                acc = acc + x_ref[i]
            o_ref[...] = acc

    vmem = pl.BlockSpec(memory_space=pltpu.VMEM)
    return pl.pallas_call(
        body, name=name, in_specs=[vmem] * n, out_specs=[vmem] * n,
        out_shape=[jax.ShapeDtypeStruct(v.shape[1:], v.dtype) for v in xs],
        compiler_params=pltpu.CompilerParams(vmem_limit_bytes=VMEM_LIMIT),
    )(*xs)


def _adam_update(w, g, m, v):
    nm = ADAM_B1 * m + (1.0 - ADAM_B1) * g
    nv = ADAM_B2 * v + (1.0 - ADAM_B2) * (g * g)
    m_hat = nm / (1.0 - ADAM_B1 ** ADAM_STEP)
    v_hat = nv / (1.0 - ADAM_B2 ** ADAM_STEP)
    return -ADAM_LR * (m_hat / (jnp.sqrt(v_hat) + ADAM_EPS) + ADAM_WD * w), nm, nv


ROW_LOSS, ROW_LN2_G, ROW_LN2_B, ROW_LN1_G, ROW_LN1_B = 0, 1, 2, 10, 11
ROWS_DMOD_X = (16, 17, 12, 9, 8, 3)
ROWS_DMOD_C = (24, 25)
SMALL = ("c_ctx", "b_ada", "attn_sink", "gmlp_ln_g", "gmlp_ln_b", "w_spatial", "b_spatial", "ln1_g", "ln1_b", "ln2_g", "ln2_b")


def _adamw_small(sums, dsc, w, m, v):
    n = len(SMALL)

    def body(*refs):
        st_ref, gm_ref, sk_ref, ws_ref, bs_ref, dsc_ref = refs[:6]
        w_refs = dict(zip(SMALL, refs[6:6 + n]))
        m_refs = dict(zip(SMALL, refs[6 + n:6 + 2 * n]))
        v_refs = dict(zip(SMALL, refs[6 + 2 * n:6 + 3 * n]))
        outs = refs[6 + 3 * n:]
        c = w_refs["c_ctx"][...]
        sg = _sig(c)
        dmod = [st_ref[r:r + 1, :] for r in ROWS_DMOD_X]
        dmod[0] = dmod[0] + st_ref[ROWS_DMOD_C[0]:ROWS_DMOD_C[0] + 1, :]
        dmod[1] = dmod[1] + st_ref[ROWS_DMOD_C[1]:ROWS_DMOD_C[1] + 1, :]
        grads = dict(
            c_ctx=dsc_ref[0:1, :] * (sg * (1.0 + c * (1.0 - sg))),
            b_ada=jnp.concatenate(dmod, axis=1),
            attn_sink=sk_ref[0:1, 0:N_KV * GROUP],
            gmlp_ln_g=gm_ref[0:1, :], gmlp_ln_b=gm_ref[1:2, :],
            w_spatial=ws_ref[...], b_spatial=bs_ref[...],
            ln1_g=st_ref[ROW_LN1_G:ROW_LN1_G + 1, :], ln1_b=st_ref[ROW_LN1_B:ROW_LN1_B + 1, :],
            ln2_g=st_ref[ROW_LN2_G:ROW_LN2_G + 1, :], ln2_b=st_ref[ROW_LN2_B:ROW_LN2_B + 1, :])
        for i, name in enumerate(SMALL):
            g = grads[name]
            d, nm, nv = _adam_update(w_refs[name][...], g, m_refs[name][...], v_refs[name][...])
            outs[i][...] = g
            outs[n + i][...] = d
            outs[2 * n + i][...] = nm
            outs[3 * n + i][...] = nv

    vmem = pl.BlockSpec(memory_space=pltpu.VMEM)
    args = list(sums) + [dsc] + [w[k] for k in SMALL] + [m[k] for k in SMALL] + [v[k] for k in SMALL]
    shapes = [jax.ShapeDtypeStruct(w[k].shape, F32) for k in SMALL]
    out = pl.pallas_call(
        body, name="adamw_small", in_specs=[vmem] * len(args), out_specs=[vmem] * (4 * n), out_shape=shapes * 4,
        compiler_params=pltpu.CompilerParams(vmem_limit_bytes=VMEM_LIMIT),
    )(*args)
    return [dict(zip(SMALL, out[i * n:(i + 1) * n])) for i in range(4)]


def _adamw_halves(w, mine, theirs, m, v, c_arr, name):
    r, c = w.shape
    tr = min(256, r // 2)
    while (r // 2) % tr:
        tr -= 8
    nt = (r // 2) // tr

    def body(c_ref, w_ref, mine_ref, theirs_ref, m_ref, v_ref, g_ref, d_ref, nm_ref, nv_ref):
        g = jnp.where(pl.program_id(0) == c_ref[0], mine_ref[...], theirs_ref[...])
        g_ref[...] = g
        d_ref[...], nm_ref[...], nv_ref[...] = _adam_update(w_ref[...], g, m_ref[...], v_ref[...])

    whole = pl.BlockSpec((tr, c), lambda hb, i, c_ref: (hb * nt + i, 0))
    half = pl.BlockSpec((tr, c), lambda hb, i, c_ref: (i, 0))
    shp = jax.ShapeDtypeStruct((r, c), F32)
    return pl.pallas_call(
        body, name=name,
        grid_spec=pltpu.PrefetchScalarGridSpec(
            num_scalar_prefetch=1, grid=(2, nt), in_specs=[whole, half, half, whole, whole], out_specs=[whole] * 4),
        out_shape=[shp] * 4,
        compiler_params=_cp("arbitrary", "arbitrary"),
    )(c_arr, w, mine, theirs, m, v)


def _adamw(w, g, m, v, name):
    r, c = w.shape
    tr = r if r * c <= 256 * 1024 else min(256, r)
    while r % tr:
        tr -= 8

    def body(w_ref, g_ref, m_ref, v_ref, d_ref, nm_ref, nv_ref):
        d_ref[...], nm_ref[...], nv_ref[...] = _adam_update(w_ref[...], g_ref[...], m_ref[...], v_ref[...])

    spec = pl.BlockSpec((tr, c), lambda i: (i, 0))
    shp = jax.ShapeDtypeStruct((r, c), F32)
    return pl.pallas_call(
        body, name=name, grid=(r // tr,), in_specs=[spec] * 4, out_specs=[spec] * 3, out_shape=[shp] * 3,
        compiler_params=_cp("arbitrary"),
    )(w, g, m, v)


def _my_pos():
    return lax.axis_index("x"), lax.axis_index("y"), lax.axis_index("c")


N_COPY = 7


class _Gather:
    def __init__(self, x_refs, out_refs, send_sems, recv_sems):
        self.x_refs, self.out_refs = x_refs, out_refs
        self.send_sems, self.recv_sems = send_sems, recv_sems
        x, y, c = _my_pos()
        self.c = c
        self.me, self.sibling = (x, y, c), (x, y, 1 - c)
        self.chips = [(1 - x, y), (x, 1 - y), (1 - x, 1 - y)]

    def _copy(self, a, k, block, to, from_input=False):
        px, py, pc = block
        rows = self.out_refs[a].at[4 * px + 2 * py + pc]
        return pltpu.make_async_remote_copy(
            src_ref=self.x_refs[a] if from_input else rows, dst_ref=rows,
            send_sem=self.send_sems.at[a * N_COPY + k], recv_sem=self.recv_sems.at[a * N_COPY + k],
            device_id=to, device_id_type=MESH)

    def start(self):
        n = len(self.x_refs)
        for a in range(n):
            self._copy(a, 0, self.me, self.sibling, from_input=True).start()
        for j, chip in enumerate(self.chips):
            for a in range(n):
                self._copy(a, 1 + j, self.me, (*chip, self.c), from_input=True).start()

    def finish(self):
        n = len(self.x_refs)
        c = self.c
        for j, chip in enumerate(self.chips):
            for a in range(n):
                self._copy(a, 1 + j, (*chip, c), self.me).wait_recv()
                self._copy(a, 4 + j, (*chip, c), self.sibling).start()
        for a in range(n):
            self._copy(a, 0, self.sibling, self.me).wait_recv()
        for j, chip in enumerate(self.chips):
            for a in range(n):
                self._copy(a, 4 + j, (*chip, 1 - c), self.me).wait_recv()
        for a in range(n):
            self._copy(a, 0, self.me, self.sibling, from_input=True).wait_send()
            for j, chip in enumerate(self.chips):
                self._copy(a, 1 + j, self.me, (*chip, c), from_input=True).wait_send()
                self._copy(a, 4 + j, (*chip, c), self.sibling).wait_send()


def _comm_scratch(n):
    return [pltpu.SemaphoreType.DMA((n * N_COPY,)), pltpu.SemaphoreType.DMA((n * N_COPY,))]


def _comm_specs(n):
    return [pl.BlockSpec(memory_space=pl.ANY)] * n


def _gathered_shapes(xs):
    return [jax.ShapeDtypeStruct((N_DEV,) + v.shape, v.dtype) for v in xs]


def _with_own(gathered, xs, me):
    return [lax.dynamic_update_index_in_dim(g, v, me, 0) for g, v in zip(gathered, xs)]


def _all_gather(xs, me, name):
    n = len(xs)

    def body(*refs):
        g = _Gather(refs[:n], refs[n:2 * n], *refs[2 * n:])
        g.start()
        g.finish()

    out = pl.pallas_call(
        body, name=name, out_shape=_gathered_shapes(xs), in_specs=_comm_specs(n), out_specs=_comm_specs(n),
        scratch_shapes=_comm_scratch(n),
    )(*xs)
    return _with_own(out, xs, me)


class _AllToAll:
    def __init__(self, x_refs, out_refs, send_sems, recv_sems):
        self.x_refs, self.out_refs = x_refs, out_refs
        self.send_sems, self.recv_sems = send_sems, recv_sems
        self.pos = _my_pos()
        x, y, c = self.pos
        self.me = 4 * x + 2 * y + c

    def _peer(self, k):
        x, y, c = self.pos
        return (x ^ ((k >> 2) & 1), y ^ ((k >> 1) & 1), c ^ (k & 1))

    def _copy(self, a, k):
        p = self._peer(k)
        return pltpu.make_async_remote_copy(
            src_ref=self.x_refs[a].at[4 * p[0] + 2 * p[1] + p[2]], dst_ref=self.out_refs[a].at[self.me],
            send_sem=self.send_sems.at[a * N_COPY + k - 1], recv_sem=self.recv_sems.at[a * N_COPY + k - 1],
            device_id=p, device_id_type=MESH)

    def start(self):
        for k in range(1, N_DEV):
            for a in range(len(self.x_refs)):
                self._copy(a, k).start()

    def finish(self):
        for a in range(len(self.x_refs)):
            for k in range(1, N_DEV):
                self._copy(a, k).wait_recv()
            for k in range(1, N_DEV):
                self._copy(a, k).wait_send()


def _all_to_all(blocks, name):
    n = len(blocks)

    def body(*refs):
        t = _AllToAll(refs[:n], refs[n:2 * n], *refs[2 * n:])
        t.start()
        t.finish()

    return pl.pallas_call(
        body, name=name, out_shape=[jax.ShapeDtypeStruct(v.shape, v.dtype) for v in blocks],
        in_specs=_comm_specs(n), out_specs=_comm_specs(n), scratch_shapes=_comm_scratch(n),
    )(*blocks)


def _sibling_exchange(xs, name):
    n = len(xs)

    def body(*refs):
        x_refs, out_refs = refs[:n], refs[n:2 * n]
        send_sems, recv_sems = refs[2 * n:]
        x, y, c = _my_pos()

        def push(a):
            return pltpu.make_async_remote_copy(
                src_ref=x_refs[a], dst_ref=out_refs[a], send_sem=send_sems.at[a], recv_sem=recv_sems.at[a],
                device_id=(x, y, 1 - c), device_id_type=MESH)

        for a in range(n):
            push(a).start()
        for a in range(n):
            push(a).wait_recv()
            push(a).wait_send()

    return pl.pallas_call(
        body, name=name, out_shape=[jax.ShapeDtypeStruct(v.shape, v.dtype) for v in xs],
        in_specs=_comm_specs(n), out_specs=_comm_specs(n),
        scratch_shapes=[pltpu.SemaphoreType.DMA((n,)), pltpu.SemaphoreType.DMA((n,))],
    )(*xs)


def _scatter_and_gather(scatter, gather, name):
    ns, ng = len(scatter), len(gather)

    def body(*refs):
        s_in, g_in = refs[:ns], refs[ns:ns + ng]
        s_out, g_out = refs[ns + ng:2 * ns + ng], refs[2 * ns + ng:2 * (ns + ng)]
        s_send, s_recv, g_send, g_recv = refs[2 * (ns + ng):]
        g = _Gather(g_in, g_out, g_send, g_recv)
        t = _AllToAll(s_in, s_out, s_send, s_recv)
        g.start()
        t.start()
        g.finish()
        t.finish()

    out = pl.pallas_call(
        body, name=name,
        out_shape=[jax.ShapeDtypeStruct(v.shape, v.dtype) for v in scatter] + _gathered_shapes(gather),
        in_specs=_comm_specs(ns + ng), out_specs=_comm_specs(ns + ng),
        scratch_shapes=_comm_scratch(ns) + _comm_scratch(ng),
    )(*scatter, *gather)
    return out[:ns], out[ns:]


def _row_tile(seq, want):
    return min(want, seq)


def _local_step(x, ctx, tgt, mod_x, mod_c, wb, sink, gmlp_g, gmlp_b, w_s, b_s, ln1_g, ln1_b, ln2_g, ln2_b,
                later=None, me=None):
    seq = x.shape[0]
    on_mesh = me is not None
    modx1 = jnp.concatenate([mod_x[0:2], jnp.zeros((6, D), F32)], axis=0)
    modc = jnp.concatenate([mod_c[0:2], jnp.zeros((6, D), F32)], axis=0)
    vec = jnp.concatenate([mod_x[2:3], ln1_g, ln1_b, mod_x[3:6], ln2_g, ln2_b], axis=0)
    gp = jnp.concatenate([gmlp_g, gmlp_b, jnp.zeros((6, G_W), F32)], axis=0)
    ws_stack = w_s.reshape(N_GRP * BLK, BLK).astype(BF16)
    ws_stack_t = jnp.transpose(w_s, (2, 0, 1)).reshape(BLK, N_GRP * BLK).astype(BF16)
    bias_full = jnp.repeat(b_s.T, GRP_D, axis=1)
    cos, sin = _rope_tables(seq)
    w_in = wb["w_in"]
    w_kv = w_in[Q_W:Q_W + 2 * KV_W, :]
    tm_big = _row_tile(seq, 512)
    tm_ffn = _row_tile(seq, 256)

    hc, kvc, vac = _ctx_fwd(ctx, modc, w_kv)
    behind_proj = ("w_a", "w_b", "w_o") if on_mesh else ()
    behind_attn = ("w_fi", "w_fo") if on_mesh else ()
    (h, q, kv, va, uv, gab), got_proj = _proj_fwd(x, modx1, w_in, cos, sin, tm_big, gather=[later[n] for n in behind_proj])
    (ya, lse), got_attn = _attn_fwd(q, kv, va, kvc, vac, sink, gather=[later[n] for n in behind_attn])
    if on_mesh:
        wb = dict(wb)
        names = behind_proj + behind_attn
        for n, g in zip(names, _with_own(list(got_proj) + list(got_attn), [later[n] for n in names], me)):
            wb[n] = g.reshape(-1, g.shape[2]) if n in ROW_SHARDED else g.reshape(N_SHARD, 2 * g.shape[1], g.shape[2])
    a, b, mix, merged, yb = _mix_fwd(uv, gab, ya, gp, ws_stack, bias_full, wb["w_a"], wb["w_b"], wb["w_o"], tm_big)
    act, h2, dff, df, dr1, st_ffn = _ffn(x, mix, tgt, vec, wb["w_fi"], wb["w_fo"], tm_ffn)
    g_w_fo = _tn_matmul(act, df, 512, "tn_w_ffn_out", BF16, tk=2048)
    g_w_fi = _tn_matmul(h2, dff, FH_SHARD, "tn_w_ffn_in", BF16, shard_major=True, tk=2048)
    dya, dpb, dws, dbs_full, st4, g_w_o, g_w_a, g_w_b = _mix_bwd(
        dr1, a, b, gab, uv, merged, ya, yb, vec, gp, ws_stack, ws_stack_t, bias_full, wb["w_a"], wb["w_b"], wb["w_o"],
        tm_big)
    blocks = dict(w_fi=_eighths(g_w_fi), w_fo=_eighths(g_w_fo), w_o=_eighths(g_w_o), w_a=_eighths(g_w_a), w_b=_eighths(g_w_b))
    early = tuple(blocks) if on_mesh else ()
    (dq, dkv, dkvc, dsink), recv_early = _attn_bwd(q, kv, kvc, sink, dya, ya, lse, scatter=[blocks[n] for n in early])
    g_wkv_ctx, st0 = _ctx_bwd(dkvc, ctx, hc, w_kv)
    (dqkv, grad_x, st1), _ = _proj_bwd(dq, dkv, dpb, x, dr1, modx1, w_in, cos, sin, tm_big)
    init = jnp.pad(g_wkv_ctx, ((Q_W, 0), (0, 0)))
    g_w_in = jnp.concatenate([_tn_matmul(dqkv, h, D, "tn_w_in_qkv", BF16, init=init, tk=1024),
                              _tn_matmul(dpb, h, D, "tn_w_in_rest", BF16, tk=1024)], axis=0)

    dbs = jnp.sum(dbs_full.reshape(BLK, N_GRP, GRP_D), axis=2).T
    parts = [jnp.concatenate([st_ffn, st1, st0], axis=0), st4, dsink, dws, dbs]
    blocks["w_in"] = _eighths(g_w_in)
    return grad_x, parts, blocks, dict(zip(early, recv_early))


BIG = ("w_in", "w_a", "w_b", "w_o", "w_fi", "w_fo")
ROW_SHARDED = ("w_o", "w_fo")


def _half_of_shard(shard, c):
    r = shard.shape[0]
    return lax.dynamic_slice_in_dim(shard, c * (r // 2), r // 2, axis=0)


def _eighths(v):
    rows = v.shape[-2] * (v.shape[0] if v.ndim == 3 else 1)
    return v.reshape(N_DEV, rows // N_DEV, v.shape[-1])


def kernel(x, c, ctx, c_ctx, w_ada, b_ada, w_in, attn_sink, gmlp_ln_g, gmlp_ln_b, w_spatial, b_spatial, w_branch_a, w_branch_b, w_out, ln1_g, ln1_b, w_ffn_in, w_ffn_out, ln2_g, ln2_b, loss_target, m_c_ctx, m_w_ada, m_b_ada, m_w_in, m_attn_sink, m_gmlp_ln_g, m_gmlp_ln_b, m_w_spatial, m_b_spatial, m_w_branch_a, m_w_branch_b, m_w_out, m_ln1_g, m_ln1_b, m_w_ffn_in, m_w_ffn_out, m_ln2_g, m_ln2_b, v_c_ctx, v_w_ada, v_b_ada, v_w_in, v_attn_sink, v_gmlp_ln_g, v_gmlp_ln_b, v_w_spatial, v_b_spatial, v_w_branch_a, v_w_branch_b, v_w_out, v_ln1_g, v_ln1_b, v_w_ffn_in, v_w_ffn_out, v_ln2_g, v_ln2_b):
    mx, my, mc = _my_pos()
    me = 4 * mx + 2 * my + mc
    chip = 2 * mx + my
    shards = dict(w_in=w_in[0].T, w_a=w_branch_a[0], w_b=w_branch_b[0], w_o=w_out[0], w_fi=w_ffn_in[0], w_fo=w_ffn_out[0])

    halves = {n: _half_of_shard(shards[n], mc).astype(BF16) for n in BIG}
    c_rows = jnp.concatenate([c, jnp.zeros((7, D), F32)], axis=0)
    g_in, c_g = _all_gather([halves["w_in"], c_rows], me, "gather_w_in")
    wb = dict(w_in=g_in.reshape(IN_W, D))

    c_all = c_g[:, 0, :]
    cc = jnp.concatenate([c_all, c_ctx[None, :], jnp.zeros((7, D), F32)], axis=0)
    sig_cc = jax.nn.sigmoid(cc)
    sc_all = cc * sig_cc
    mod_shard = _ada_fwd(sc_all, w_ada[0])
    mod_g = _all_gather([mod_shard], me, "gather_mod")[0]
    mod_all = jnp.concatenate([mod_g[2 * s] for s in range(4)], axis=1) + b_ada
    mod_x = lax.dynamic_slice_in_dim(mod_all, me, 1, axis=0).reshape(6, D)
    mod_c = mod_all[8].reshape(6, D)[0:2]

    grad_x, parts, blocks, recv = _local_step(
        x[0], ctx[0], loss_target[0], mod_x, mod_c, wb, attn_sink, gmlp_ln_g, gmlp_ln_b, w_spatial[0], b_spatial[0],
        ln1_g, ln1_b, ln2_g, ln2_b, later=halves, me=me)

    (recv["w_in"],), gathered = _scatter_and_gather([blocks["w_in"]], parts, "scatter_w_in_gather_small")
    gathered = _with_own(gathered, parts, me)

    me_arr = jnp.reshape(me, (1,)).astype(jnp.int32)
    summed = {n: _sum_blocks(recv[n], blocks[n], me_arr, "sum_grads_" + n) for n in BIG}
    theirs = dict(zip(BIG, _sibling_exchange([summed[n] for n in BIG], "exchange_grads")))

    sums = _sum8_many(gathered, "sum_small")
    stats = sums[0]
    loss = 0.5 * jnp.sum(stats[ROW_LOSS]) / D
    dmod_x_all = jnp.concatenate([gathered[0][:, r_, :] for r_ in ROWS_DMOD_X], axis=1)
    dmod_c_full = jnp.concatenate([stats[r_] for r_ in ROWS_DMOD_C] + [jnp.zeros((4 * D,), F32)])
    dm_rows = jnp.concatenate([dmod_x_all, dmod_c_full[None, :], jnp.zeros((7, 6 * D), F32)], axis=0)
    cs = w_ada.shape[2]
    dm_shard = lax.dynamic_slice_in_dim(dm_rows, chip * cs, cs, axis=1)
    dmc_shard = jnp.concatenate([dm_shard[8:9], jnp.zeros((7, cs), F32)], axis=0)
    g_w_ada, part = _ada_bwd(sc_all.T, dm_shard, dmc_shard, w_ada[0])
    part_all = _all_gather([part * (mc == 0).astype(F32)], me, "gather_c_ctx")[0]
    dsc = _sum8(part_all, "sum_c_ctx")

    grads = dict(w_ada=g_w_ada[None])
    weights = dict(c_ctx=c_ctx, w_ada=w_ada, b_ada=b_ada, w_in=w_in, attn_sink=attn_sink, gmlp_ln_g=gmlp_ln_g,
                   gmlp_ln_b=gmlp_ln_b, w_spatial=w_spatial, b_spatial=b_spatial, w_branch_a=w_branch_a,
                   w_branch_b=w_branch_b, w_out=w_out, ln1_g=ln1_g, ln1_b=ln1_b, w_ffn_in=w_ffn_in, w_ffn_out=w_ffn_out,
                   ln2_g=ln2_g, ln2_b=ln2_b)
    ms = dict(c_ctx=m_c_ctx, w_ada=m_w_ada, b_ada=m_b_ada, w_in=m_w_in, attn_sink=m_attn_sink, gmlp_ln_g=m_gmlp_ln_g,
              gmlp_ln_b=m_gmlp_ln_b, w_spatial=m_w_spatial, b_spatial=m_b_spatial, w_branch_a=m_w_branch_a,
              w_branch_b=m_w_branch_b, w_out=m_w_out, ln1_g=m_ln1_g, ln1_b=m_ln1_b, w_ffn_in=m_w_ffn_in,
              w_ffn_out=m_w_ffn_out, ln2_g=m_ln2_g, ln2_b=m_ln2_b)
    vs = dict(c_ctx=v_c_ctx, w_ada=v_w_ada, b_ada=v_b_ada, w_in=v_w_in, attn_sink=v_attn_sink, gmlp_ln_g=v_gmlp_ln_g,
              gmlp_ln_b=v_gmlp_ln_b, w_spatial=v_w_spatial, b_spatial=v_b_spatial, w_branch_a=v_w_branch_a,
              w_branch_b=v_w_branch_b, w_out=v_w_out, ln1_g=v_ln1_g, ln1_b=v_ln1_b, w_ffn_in=v_w_ffn_in,
              w_ffn_out=v_w_ffn_out, ln2_g=v_ln2_g, ln2_b=v_ln2_b)
    order = list(weights)
    delta, new_m, new_v = {}, {}, {}
    d_, m_, v_ = _adamw(w_ada[0], g_w_ada, m_w_ada[0], v_w_ada[0], "adamw_w_ada")
    delta["w_ada"], new_m["w_ada"], new_v["w_ada"] = d_[None], m_[None], v_[None]
    c_arr = jnp.reshape(mc, (1,)).astype(jnp.int32)
    names = dict(w_in="w_in", w_a="w_branch_a", w_b="w_branch_b", w_o="w_out", w_fi="w_ffn_in", w_fo="w_ffn_out")
    for k, n in names.items():
        flip = (lambda t: t.T) if k == "w_in" else (lambda t: t)
        outs = _adamw_halves(flip(weights[n][0]), summed[k], theirs[k], flip(ms[n][0]), flip(vs[n][0]), c_arr, "adamw_" + n)
        grads[n], delta[n], new_m[n], new_v[n] = [flip(t)[None] for t in outs]

    def view(a):
        return a.reshape(-1, a.shape[-1]) if a.ndim != 1 else a.reshape(1, -1)

    small = _adamw_small(sums, dsc, *[{n: view(d[n]) for n in SMALL} for d in (weights, ms, vs)])
    for out, src in zip((grads, delta, new_m, new_v), small):
        for n in SMALL:
            out[n] = src[n].reshape(weights[n].shape)

    return (loss, grad_x[None], *[grads[n] for n in order], *[delta[n] for n in order],
            *[new_m[n] for n in order], *[new_v[n] for n in order])
```

```python
import functools
import math

import jax
import jax.numpy as jnp
from jax import lax
from jax.experimental import pallas as pl
from jax.experimental.pallas import tpu as pltpu

F32 = jnp.float32
BF16 = jnp.bfloat16

D = 1024
HEAD = 64
N_KV = 2
GROUP = 4
Q_W = 512
KV_W = 128
G_W = 512
BLK = 128
N_GRP = 8
GRP_D = 64
FH = 2816
IN_W = 3840
GRID_W = 64
ROPE_BASE = 10000.0
LN_EPS = 1e-5
NEG = -1e30
ALPHA = (2 * 1) ** 0.25
SCALE = HEAD ** -0.5
GELU_K = math.sqrt(2.0 / math.pi)
GELU_A = 0.044715
ADAM_LR = 0.001
ADAM_B1 = 0.9
ADAM_B2 = 0.999
ADAM_EPS = 1e-08
ADAM_WD = 0.01
ADAM_STEP = 10
N_DEV = 8
N_SHARD = 4
FH_SHARD = FH // 2
LANES = 128
VMEM_LIMIT = 56 * 1024 * 1024
MESH = pl.DeviceIdType.MESH


def _cp(*sem):
    return pltpu.CompilerParams(dimension_semantics=sem, vmem_limit_bytes=VMEM_LIMIT)


def _resident(shape):
    return pl.BlockSpec(shape, lambda *_: (0,) * len(shape), pipeline_mode=pl.Buffered(1))


def _rows(tm, width):
    return pl.BlockSpec((tm, width), lambda i: (i, 0))


def _acc(shape):
    return pl.BlockSpec(shape, lambda *_: (0,) * len(shape))


def _dot(a, b):
    return jnp.dot(a, b, preferred_element_type=F32)


def _dot_nt(a, b):
    return lax.dot_general(a, b, (((1,), (1,)), ((), ())), preferred_element_type=F32)


def _dot_tn(a, b):
    return lax.dot_general(a, b, (((0,), (0,)), ((), ())), preferred_element_type=F32)


def _ln(x):
    mu = jnp.mean(x, axis=-1, keepdims=True)
    xc = x - mu
    var = jnp.mean(xc * xc, axis=-1, keepdims=True)
    rstd = lax.rsqrt(var + LN_EPS)
    return xc * rstd, rstd


def _ln_bwd(dxhat, xhat, rstd):
    return (dxhat - jnp.mean(dxhat, axis=-1, keepdims=True)
            - xhat * jnp.mean(dxhat * xhat, axis=-1, keepdims=True)) * rstd


def _sig(x):
    return 1.0 / (1.0 + jnp.exp(-x))


def _gelu(x):
    t = jnp.tanh(GELU_K * (x + GELU_A * x * x * x))
    return 0.5 * x * (1.0 + t), t


def _gelu_grad(x, t):
    return 0.5 * (1.0 + t) + 0.5 * x * (1.0 - t * t) * GELU_K * (1.0 + 3.0 * GELU_A * x * x)


def _colsum(v):
    return jnp.sum(v, axis=0, keepdims=True)


def _partner(x):
    w = x.shape[1]
    lane = lax.broadcasted_iota(jnp.int32, x.shape, 1)
    return jnp.where((lane & 31) < 16, pltpu.roll(x, w - 16, 1), pltpu.roll(x, 16, 1))


def _rope(x, cos, sin):
    return x * cos + _partner(x) * sin


def _unrope(g, cos, sin):
    return g * cos + _partner(g * sin)


def _rope_tables(seq):
    inv = ROPE_BASE ** (-jnp.arange(HEAD // 4, dtype=F32) / (HEAD // 4))
    pos = jnp.arange(seq, dtype=jnp.int32)
    ar = (pos // GRID_W).astype(F32)[:, None] * inv
    ac = (pos % GRID_W).astype(F32)[:, None] * inv
    cos = jnp.concatenate([jnp.cos(ar), jnp.cos(ar), jnp.cos(ac), jnp.cos(ac)], axis=-1)
    sin = jnp.concatenate([-jnp.sin(ar), jnp.sin(ar), -jnp.sin(ac), jnp.sin(ac)], axis=-1)
    return jnp.tile(cos, (1, LANES // HEAD)), jnp.tile(sin, (1, LANES // HEAD))


def _ctx_fwd(ctx, modc, w_kv):
    n_ctx = ctx.shape[0]

    def body(ctx_ref, mod_ref, w_ref, hc_ref, kvc_ref, vac_ref):
        xhat, _ = _ln(ctx_ref[...])
        hc = (xhat * (1.0 + mod_ref[1:2, :]) + mod_ref[0:1, :]).astype(BF16)
        hc_ref[...] = hc
        kvc = _dot_nt(hc, w_ref[...]).astype(BF16)
        kvc_ref[...] = kvc
        vac_ref[...] = _with_ones(kvc[:, KV_W:])

    return pl.pallas_call(
        body, name="ctx_fwd", grid=(1,),
        in_specs=[_acc((n_ctx, D)), _acc((8, D)), _acc((2 * KV_W, D))],
        out_specs=[_acc((n_ctx, D)), _acc((n_ctx, 2 * KV_W)), _acc((n_ctx, 2 * LANES))],
        out_shape=[jax.ShapeDtypeStruct((n_ctx, D), BF16), jax.ShapeDtypeStruct((n_ctx, 2 * KV_W), BF16),
                   jax.ShapeDtypeStruct((n_ctx, 2 * LANES), BF16)],
        compiler_params=_cp("arbitrary"),
    )(ctx, modc, w_kv)


def _host_start(step, comm):
    if comm is not None:
        @pl.when(step == 0)
        def _():
            comm.start()


def _host_finish(step, last, comm):
    if comm is not None:
        @pl.when(step == last)
        def _():
            comm.finish()


def _proj_fwd(x, modx, w_in, cos, sin, tm, gather=()):
    seq = x.shape[0]
    ng = len(gather)

    def body(x_ref, mod_ref, w_ref, cos_ref, sin_ref, *rest):
        h_ref, q_ref, kv_ref, va_ref, uv_ref, gab_ref = rest[ng:ng + 6]
        comm = _Gather(rest[:ng], rest[ng + 6:2 * ng + 6], *rest[2 * ng + 6:]) if ng else None
        _host_start(pl.program_id(0), comm)
        xhat, _ = _ln(x_ref[...])
        h = (xhat * (1.0 + mod_ref[1:2, :]) + mod_ref[0:1, :]).astype(BF16)
        h_ref[...] = h
        cos1, sin1 = cos_ref[...], sin_ref[...]
        cos2 = jnp.concatenate([cos1, cos1], axis=1)
        sin2 = jnp.concatenate([sin1, sin1], axis=1)
        for j in range(Q_W // 256):
            t = _dot_nt(h, w_ref[256 * j:256 * (j + 1), :])
            q_ref[:, 256 * j:256 * (j + 1)] = (_rope(t, cos2, sin2) * SCALE).astype(BF16)
        t = _dot_nt(h, w_ref[Q_W:Q_W + 2 * KV_W, :])
        kv_ref[:, :KV_W] = _rope(t[:, :KV_W], cos1, sin1).astype(BF16)
        v = t[:, KV_W:].astype(BF16)
        kv_ref[:, KV_W:] = v
        va_ref[...] = _with_ones(v)
        o = Q_W + 2 * KV_W
        for j in range(2):
            uv_ref[:, G_W * j:G_W * (j + 1)] = _dot_nt(h, w_ref[o + G_W * j:o + G_W * (j + 1), :])
        o += 2 * G_W
        for j in range(4):
            gab_ref[:, 512 * j:512 * (j + 1)] = _dot_nt(h, w_ref[o + 512 * j:o + 512 * (j + 1), :]).astype(BF16)
        _host_finish(pl.program_id(0), seq // tm - 1, comm)

    out = pl.pallas_call(
        body, name="proj_fwd", grid=(seq // tm,),
        in_specs=[_rows(tm, D), _acc((8, D)), _resident((IN_W, D)), _rows(tm, LANES), _rows(tm, LANES)] + _comm_specs(ng),
        out_specs=[_rows(tm, D), _rows(tm, Q_W), _rows(tm, 2 * KV_W), _rows(tm, 2 * LANES), _rows(tm, 2 * G_W),
                   _rows(tm, 2 * D)] + _comm_specs(ng),
        out_shape=[jax.ShapeDtypeStruct((seq, D), BF16), jax.ShapeDtypeStruct((seq, Q_W), BF16),
                   jax.ShapeDtypeStruct((seq, 2 * KV_W), BF16), jax.ShapeDtypeStruct((seq, 2 * LANES), BF16),
                   jax.ShapeDtypeStruct((seq, 2 * G_W), F32), jax.ShapeDtypeStruct((seq, 2 * D), BF16)] + _gathered_shapes(gather),
        scratch_shapes=_comm_scratch(ng) if ng else [],
        compiler_params=_cp("arbitrary"),
    )(x, modx, w_in, cos, sin, *gather)
    return out[:6], out[6:]


def _stack_heads(x, hk):
    return jnp.concatenate([x[:, (hk * GROUP + g) * HEAD:(hk * GROUP + g + 1) * HEAD] for g in range(GROUP)], axis=0)


def _attn_scores(q, k_refs, hk, n, nb):
    q4 = _stack_heads(q, hk)
    ks = [r[:, hk * HEAD:(hk + 1) * HEAD] for r in k_refs]
    rows = GROUP * BLK
    qi = lax.broadcasted_iota(jnp.int32, (rows, BLK), 0) & (BLK - 1)
    kj = lax.broadcasted_iota(jnp.int32, (rows, BLK), 1)
    s = [_dot_nt(q4, k) for k in ks]
    s[1] = jnp.where((kj >= qi) & (n > 0), s[1], NEG)
    s[3] = jnp.where((kj <= qi) & (n < nb - 1), s[3], NEG)
    return q4, ks, s


def _sink_rows(sink_ref, hk):
    rows = GROUP * BLK
    rg = lax.broadcasted_iota(jnp.int32, (rows, 1), 0) >> 7
    sink_v = jnp.full((rows, 1), sink_ref[0, hk * GROUP], F32)
    for g in range(1, GROUP):
        sink_v = jnp.where(rg == g, sink_ref[0, hk * GROUP + g], sink_v)
    return sink_v


def _with_ones(v):
    ones = jnp.ones((v.shape[0], HEAD), v.dtype)
    return jnp.concatenate([v[:, :HEAD], ones, v[:, HEAD:], ones], axis=1)


def _kv_specs(nb, qb):
    def spec(d):
        return pl.BlockSpec((BLK, 2 * KV_W), lambda n: (jnp.clip(qb * n + d, 0, nb - 1), 0))
    return [spec(d) for d in range(-1, qb + 1)]


def _attn_fwd(q, kv, va, kvc, vac, sink, gather=()):
    seq = q.shape[0]
    nb = seq // BLK
    n_ctx = kvc.shape[0]
    ng = len(gather)
    Q_BLOCKS = 1
    nkv = Q_BLOCKS + 2
    steps = nb // Q_BLOCKS

    def body(q_ref, *rest):
        kv_refs, va_refs = rest[:nkv], rest[nkv:2 * nkv]
        kvc_ref, vac_ref, sink_ref = rest[2 * nkv:2 * nkv + 3]
        rest = rest[2 * nkv + 3:]
        o_ref, lse_ref = rest[ng:ng + 2]
        comm = _Gather(rest[:ng], rest[ng + 2:2 * ng + 2], *rest[2 * ng + 2:]) if ng else None
        n = pl.program_id(0)
        _host_start(n, comm)
        lane = lax.broadcasted_iota(jnp.int32, (BLK, LANES), 1)
        for sub in range(Q_BLOCKS):
            rs = slice(sub * BLK, (sub + 1) * BLK)
            q = q_ref[rs, :]
            outs = []
            lse_all = jnp.zeros((BLK, LANES), F32)
            for hk in range(N_KV):
                _, _, s = _attn_scores(q, (kvc_ref,) + kv_refs[sub:sub + 3], hk, Q_BLOCKS * n + sub, nb)
                sink_v = _sink_rows(sink_ref, hk)
                m = sink_v
                for t in s:
                    m = jnp.maximum(m, jnp.max(t, axis=-1, keepdims=True))
                o = jnp.zeros((GROUP * BLK, LANES), F32)
                for t, va_ref in zip(s, (vac_ref,) + va_refs[sub:sub + 3]):
                    o = o + _dot(jnp.exp((t - m).astype(BF16)), va_ref[:, hk * LANES:(hk + 1) * LANES])
                denom = o[:, HEAD:HEAD + 1] + jnp.exp(sink_v - m)
                o4 = o[:, :HEAD] * (1.0 / denom)
                lse4 = m + jnp.log(denom)
                for g in range(GROUP):
                    outs.append(o4[g * BLK:(g + 1) * BLK, :])
                    lse_all = jnp.where(lane == hk * GROUP + g, lse4[g * BLK:(g + 1) * BLK, :], lse_all)
            o_ref[rs, :] = jnp.concatenate(outs, axis=1).astype(BF16)
            lse_ref[rs, :] = lse_all
        _host_finish(n, steps - 1, comm)

    tq = Q_BLOCKS * BLK
    out = pl.pallas_call(
        body, name="attn_fwd", grid=(steps,),
        in_specs=[_rows(tq, Q_W)] + _kv_specs(nb, Q_BLOCKS) + _kv_specs(nb, Q_BLOCKS)
        + [_acc((n_ctx, 2 * KV_W)), _acc((n_ctx, 2 * LANES)), pl.BlockSpec(memory_space=pltpu.SMEM)] + _comm_specs(ng),
        out_specs=[_rows(tq, Q_W), _rows(tq, LANES)] + _comm_specs(ng),
        out_shape=[jax.ShapeDtypeStruct((seq, Q_W), BF16), jax.ShapeDtypeStruct((seq, LANES), F32)] + _gathered_shapes(gather),
        scratch_shapes=_comm_scratch(ng) if ng else [],
        compiler_params=_cp("arbitrary"),
    )(q, *([kv] * nkv), *([va] * nkv), kvc, vac, sink, *gather)
    return out[:2], out[2:]


def _gmlp_chunk(u, vb, gp_ref, ws_ref, bias_ref):
    gu, tu = _gelu(u)
    gv, tv = _gelu(vb)
    vhat, rstd = _ln(gv)
    vn = (vhat * gp_ref[0:1, :] + gp_ref[1:2, :]).astype(BF16)
    s = bias_ref[...] + jnp.concatenate(
        [_dot(ws_ref[g * BLK:(g + 1) * BLK, :], vn[:, g * GRP_D:(g + 1) * GRP_D]) for g in range(N_GRP)], axis=1)
    return gu, tu, tv, vhat, rstd, vn, s


def _mix_fwd(uv, gab, ya, gp, ws_stack, bias_full, w_a, w_b, w_o, tm):
    seq = uv.shape[0]

    def body(uv_ref, gab_ref, ya_ref, gp_ref, ws_ref, bias_ref, wa_ref, wb_ref, wo_ref,
             a_ref, b_ref, mix_ref, merged_ref, yb_ref):
        for c in range(tm // BLK):
            rs = slice(c * BLK, (c + 1) * BLK)
            gu, _, _, _, _, _, s = _gmlp_chunk(uv_ref[rs, :G_W], uv_ref[rs, G_W:], gp_ref, ws_ref, bias_ref)
            yb_ref[rs, :] = (gu * s).astype(BF16)
        ya = ya_ref[...]
        yb = yb_ref[...]
        for s in range(N_SHARD):
            cs = slice(s * (D // N_SHARD), (s + 1) * (D // N_SHARD))
            a = _dot(ya, wa_ref[s])
            b = _dot(yb, wb_ref[s])
            a_ref[:, cs] = a.astype(BF16)
            b_ref[:, cs] = b.astype(BF16)
            ga = gab_ref[:, cs].astype(F32)
            gb = gab_ref[:, D + s * (D // N_SHARD):D + (s + 1) * (D // N_SHARD)].astype(F32)
            merged_ref[:, cs] = (_sig(ga) * a + _sig(gb) * b).astype(BF16)
        mix_ref[...] = _dot(merged_ref[...], wo_ref[...])

    return pl.pallas_call(
        body, name="mix_fwd", grid=(seq // tm,),
        in_specs=[_rows(tm, 2 * G_W), _rows(tm, 2 * D), _rows(tm, Q_W), _acc((8, G_W)),
                  _resident((N_GRP * BLK, BLK)), _acc((BLK, G_W)),
                  _resident((N_SHARD, Q_W, D // N_SHARD)), _resident((N_SHARD, G_W, D // N_SHARD)), _resident((D, D))],
        out_specs=[_rows(tm, D), _rows(tm, D), _rows(tm, D), _rows(tm, D), _rows(tm, G_W)],
        out_shape=[jax.ShapeDtypeStruct((seq, D), BF16), jax.ShapeDtypeStruct((seq, D), BF16),
                   jax.ShapeDtypeStruct((seq, D), F32), jax.ShapeDtypeStruct((seq, D), BF16),
                   jax.ShapeDtypeStruct((seq, G_W), BF16)],
        compiler_params=_cp("arbitrary"),
    )(uv, gab, ya, gp, ws_stack, bias_full, w_a, w_b, w_o)


FFN_CHUNK = 512


def _ffn_chunks():
    out = []
    for hh in range(2):
        off = 0
        while off < FH_SHARD:
            w = min(FFN_CHUNK, FH_SHARD - off)
            out.append((hh, off, w))
            off += w
    return out


def _mid_recompute(x_ref, mix_ref, vec_ref):
    r1 = ALPHA * x_ref[...] + vec_ref[0:1, :] * mix_ref[...]
    xh1, rstd1 = _ln(r1)
    xmid = xh1 * vec_ref[1:2, :] + vec_ref[2:3, :]
    xh2, rstd2 = _ln(xmid)
    return xh1, rstd1, xmid, xh2, rstd2


def _ffn(x, mix, tgt, vec, w_fi, w_fo, tm):
    seq = x.shape[0]

    def body(x_ref, mix_ref, tgt_ref, vec_ref, wi_ref, wo_ref, act_ref, h2_ref, dff_ref, df_ref, dr1_ref, st_ref, gu_ref):
        @pl.when(pl.program_id(0) == 0)
        def _():
            st_ref[...] = jnp.zeros_like(st_ref)

        xh1, rstd1, xmid, xh2, rstd2 = _mid_recompute(x_ref, mix_ref, vec_ref)
        h2 = (xh2 * (1.0 + vec_ref[4:5, :]) + vec_ref[3:4, :]).astype(BF16)
        h2_ref[...] = h2
        f = jnp.zeros((tm, D), F32)
        for hh, off, w in _ffn_chunks():
            cs = slice(hh * FH_SHARD + off, hh * FH_SHARD + off + w)
            cu = slice(FH + hh * FH_SHARD + off, FH + hh * FH_SHARD + off + w)
            g = _dot(h2, wi_ref[hh, :, off:off + w])
            u = _dot(h2, wi_ref[2 + hh, :, off:off + w])
            gu_ref[:, cs] = g
            gu_ref[:, cu] = u
            a = (g * _sig(g) * u).astype(BF16)
            act_ref[:, cs] = a
            f = f + _dot(a, wo_ref[cs, :])
        r2 = ALPHA * xmid + vec_ref[5:6, :] * f
        yh, rstd = _ln(r2)
        y = yh * vec_ref[6:7, :] + vec_ref[7:8, :]
        err = y - tgt_ref[...]
        dy = err / D
        dr2 = _ln_bwd(dy * vec_ref[6:7, :], yh, rstd)
        st_ref[0:1, :] += _colsum(err * err)
        st_ref[1:2, :] += _colsum(dy * yh)
        st_ref[2:3, :] += _colsum(dy)
        st_ref[3:4, :] += _colsum(dr2 * f)

        df = (dr2 * vec_ref[5:6, :]).astype(BF16)
        df_ref[...] = df
        dh2 = jnp.zeros((tm, D), F32)
        for hh, off, w in _ffn_chunks():
            cs = slice(hh * FH_SHARD + off, hh * FH_SHARD + off + w)
            cu = slice(FH + hh * FH_SHARD + off, FH + hh * FH_SHARD + off + w)
            da = _dot_nt(df, wo_ref[cs, :])
            g = gu_ref[:, cs]
            u = gu_ref[:, cu]
            sg = _sig(g)
            dg = (da * u * sg * (1.0 + g * (1.0 - sg))).astype(BF16)
            du = (da * g * sg).astype(BF16)
            dff_ref[:, cs] = dg
            dff_ref[:, cu] = du
            dh2 = dh2 + _dot_nt(dg, wi_ref[hh, :, off:off + w]) + _dot_nt(du, wi_ref[2 + hh, :, off:off + w])
        dxmid = _ln_bwd(dh2 * (1.0 + vec_ref[4:5, :]), xh2, rstd2) + ALPHA * dr2
        dr1 = _ln_bwd(dxmid * vec_ref[1:2, :], xh1, rstd1)
        dr1_ref[...] = dr1
        st_ref[8:9, :] += _colsum(dh2 * xh2)
        st_ref[9:10, :] += _colsum(dh2)
        st_ref[10:11, :] += _colsum(dxmid * xh1)
        st_ref[11:12, :] += _colsum(dxmid)
        st_ref[12:13, :] += _colsum(dr1 * mix_ref[...])

    return pl.pallas_call(
        body, name="ffn", grid=(seq // tm,),
        in_specs=[_rows(tm, D), _rows(tm, D), _rows(tm, D), _acc((8, D)), _resident((N_SHARD, D, FH_SHARD)), _resident((FH, D))],
        out_specs=[_rows(tm, FH), _rows(tm, D), _rows(tm, 2 * FH), _rows(tm, D), _rows(tm, D), _acc((16, D))],
        out_shape=[jax.ShapeDtypeStruct((seq, FH), BF16), jax.ShapeDtypeStruct((seq, D), BF16),
                   jax.ShapeDtypeStruct((seq, 2 * FH), BF16), jax.ShapeDtypeStruct((seq, D), BF16),
                   jax.ShapeDtypeStruct((seq, D), F32), jax.ShapeDtypeStruct((16, D), F32)],
        scratch_shapes=[pltpu.VMEM((tm, 2 * FH), F32)],
        compiler_params=_cp("arbitrary"),
    )(x, mix, tgt, vec, w_fi, w_fo)


def _mix_bwd(dr1, a, b, gab, uv, merged, ya, yb, vec, gp, ws_stack, ws_stack_t, bias_full, w_a, w_b, w_o, tm, scatter=()):
    seq = dr1.shape[0]
    last = seq // tm - 1
    cw = D // N_SHARD
    ns = len(scatter)

    def body(dr1_ref, a_ref, b_ref, gab_ref, uv_ref, mg_ref, ya_ref, yb_ref, vec_ref, gp_ref, ws_ref, wst_ref, bias_ref,
             wa_ref, wb_ref, wo_ref, *rest):
        dya_ref, dp_ref, dws_ref, dbs_ref, st_ref, gwo_ref, gwa_ref, gwb_ref = rest[ns:ns + 8]
        acc_o, acc_a, acc_b = rest[2 * ns + 8:2 * ns + 11]
        comm = _AllToAll(rest[:ns], rest[ns + 8:2 * ns + 8], *rest[2 * ns + 11:]) if ns else None
        _host_start(pl.program_id(0), comm)

        @pl.when(pl.program_id(0) == 0)
        def _():
            dws_ref[...] = jnp.zeros_like(dws_ref)
            dbs_ref[...] = jnp.zeros_like(dbs_ref)
            st_ref[...] = jnp.zeros_like(st_ref)
            acc_o[...] = jnp.zeros_like(acc_o)
            acc_a[...] = jnp.zeros_like(acc_a)
            acc_b[...] = jnp.zeros_like(acc_b)

        dmix = (dr1_ref[...] * vec_ref[0:1, :]).astype(BF16)
        acc_o[...] += _dot_tn(mg_ref[...], dmix)
        dmerged = _dot_nt(dmix, wo_ref[...])
        sa = _sig(gab_ref[:, :D].astype(F32))
        sb = _sig(gab_ref[:, D:].astype(F32))
        da = (dmerged * sa).astype(BF16)
        db = (dmerged * sb).astype(BF16)
        dp_ref[:, 2 * G_W:2 * G_W + D] = (dmerged * a_ref[...].astype(F32) * sa * (1.0 - sa)).astype(BF16)
        dp_ref[:, 2 * G_W + D:] = (dmerged * b_ref[...].astype(F32) * sb * (1.0 - sb)).astype(BF16)
        dya = jnp.zeros((tm, Q_W), F32)
        dyb = jnp.zeros((tm, G_W), F32)
        ya = ya_ref[...]
        yb = yb_ref[...]
        for s in range(N_SHARD):
            cs = slice(s * cw, (s + 1) * cw)
            dya = dya + _dot_nt(da[:, cs], wa_ref[s])
            dyb = dyb + _dot_nt(db[:, cs], wb_ref[s])
            acc_a[s] += _dot_tn(ya, da[:, cs])
            acc_b[s] += _dot_tn(yb, db[:, cs])
        dya_ref[...] = dya.astype(BF16)

        @pl.when(pl.program_id(0) == last)
        def _():
            gwo_ref[...] = acc_o[...].astype(BF16)
            gwa_ref[...] = acc_a[...].astype(BF16)
            gwb_ref[...] = acc_b[...].astype(BF16)

        for c in range(tm // BLK):
            rs = slice(c * BLK, (c + 1) * BLK)
            u = uv_ref[rs, :G_W]
            vb = uv_ref[rs, G_W:]
            gu, tu, tv, vhat, rstd, vn, s = _gmlp_chunk(u, vb, gp_ref, ws_ref, bias_ref)
            dyb_c = dyb[rs, :]
            ds = dyb_c * gu
            du = dyb_c * s * _gelu_grad(u, tu)
            ds_b = ds.astype(BF16)
            dvn_g = []
            for g in range(N_GRP):
                cg = slice(g * GRP_D, (g + 1) * GRP_D)
                dvn_g.append(_dot(wst_ref[:, g * BLK:(g + 1) * BLK], ds_b[:, cg]))
                dws_ref[g * BLK:(g + 1) * BLK, :] += _dot_nt(ds_b[:, cg], vn[:, cg])
            dvn = jnp.concatenate(dvn_g, axis=1)
            dbs_ref[...] += ds
            st_ref[0:1, :] += _colsum(dvn * vhat)
            st_ref[1:2, :] += _colsum(dvn)
            dgv = _ln_bwd(dvn * gp_ref[0:1, :], vhat, rstd)
            dvb = dgv * _gelu_grad(vb, tv)
            dp_ref[rs, :G_W] = du.astype(BF16)
            dp_ref[rs, G_W:2 * G_W] = dvb.astype(BF16)
        _host_finish(pl.program_id(0), last, comm)

    pw = 2 * G_W + 2 * D
    out = pl.pallas_call(
        body, name="mix_bwd", grid=(seq // tm,),
        in_specs=[_rows(tm, D), _rows(tm, D), _rows(tm, D), _rows(tm, 2 * D), _rows(tm, 2 * G_W), _rows(tm, D), _rows(tm, Q_W),
                  _rows(tm, G_W), _acc((8, D)), _acc((8, G_W)),
                  _resident((N_GRP * BLK, BLK)), _resident((BLK, N_GRP * BLK)), _acc((BLK, G_W)),
                  _resident((N_SHARD, Q_W, cw)), _resident((N_SHARD, G_W, cw)), _resident((D, D))] + _comm_specs(ns),
        out_specs=[_rows(tm, Q_W), _rows(tm, pw), _acc((N_GRP * BLK, BLK)), _acc((BLK, G_W)), _acc((8, G_W)),
                   _acc((D, D)), _acc((N_SHARD, Q_W, cw)), _acc((N_SHARD, G_W, cw))] + _comm_specs(ns),
        out_shape=[jax.ShapeDtypeStruct((seq, Q_W), BF16), jax.ShapeDtypeStruct((seq, pw), BF16),
                   jax.ShapeDtypeStruct((N_GRP * BLK, BLK), F32), jax.ShapeDtypeStruct((BLK, G_W), F32),
                   jax.ShapeDtypeStruct((8, G_W), F32), jax.ShapeDtypeStruct((D, D), BF16),
                   jax.ShapeDtypeStruct((N_SHARD, Q_W, cw), BF16), jax.ShapeDtypeStruct((N_SHARD, G_W, cw), BF16)]
        + [jax.ShapeDtypeStruct(v.shape, v.dtype) for v in scatter],
        scratch_shapes=[pltpu.VMEM((D, D), F32), pltpu.VMEM((N_SHARD, Q_W, cw), F32), pltpu.VMEM((N_SHARD, G_W, cw), F32)]
        + (_comm_scratch(ns) if ns else []),
        compiler_params=_cp("arbitrary"),
    )(dr1, a, b, gab, uv, merged, ya, yb, vec, gp, ws_stack, ws_stack_t, bias_full, w_a, w_b, w_o, *scatter)
    return out[:8], out[8:]


def _attn_bwd(q, kv, kvc, sink, dya, ya, lse, scatter=()):
    seq = q.shape[0]
    nb = seq // BLK
    n_ctx = kvc.shape[0]
    ns = len(scatter)
    Q_BLOCKS = 2
    nkv = Q_BLOCKS + 2
    steps = nb // Q_BLOCKS

    def body(q_ref, *rest):
        kv_refs = rest[:nkv]
        kvc_ref, sink_ref, do_ref, o_ref, lse_ref = rest[nkv:nkv + 5]
        rest = rest[nkv + 5:]
        dq_ref, dkv_ref, dkvc_ref, dsink_ref = rest[ns:ns + 4]
        comm = _AllToAll(rest[:ns], rest[ns + 4:2 * ns + 4], *rest[2 * ns + 4:]) if ns else None
        n = pl.program_id(0)
        _host_start(n, comm)

        @pl.when(n == 0)
        def _():
            dkv_ref[...] = jnp.zeros_like(dkv_ref)
            dkvc_ref[...] = jnp.zeros_like(dkvc_ref)
            dsink_ref[...] = jnp.zeros_like(dsink_ref)

        lane = lax.broadcasted_iota(jnp.int32, (1, LANES), 1)
        for sub in range(Q_BLOCKS):
            rs = slice(sub * BLK, (sub + 1) * BLK)
            blk = Q_BLOCKS * n + sub
            q = q_ref[rs, :]
            do = do_ref[rs, :]
            out = o_ref[rs, :]
            lse_all = lse_ref[rs, :]
            k_refs = (kvc_ref,) + kv_refs[sub:sub + 3]
            dqs, dks, dvs = [], [], []
            for hk in range(N_KV):
                q4, ks, s = _attn_scores(q, k_refs, hk, blk, nb)
                vs = [r[:, KV_W + hk * HEAD:KV_W + (hk + 1) * HEAD] for r in k_refs]
                lse4 = jnp.concatenate([lse_all[:, hk * GROUP + g:hk * GROUP + g + 1] for g in range(GROUP)], axis=0)
                do4 = _stack_heads(do, hk)
                delta = jnp.sum(do4.astype(F32) * _stack_heads(out, hk).astype(F32), axis=-1, keepdims=True)
                p = [jnp.exp((t - lse4).astype(BF16)) for t in s]
                ds = [t * (_dot_nt(do4, v) - delta).astype(BF16) for t, v in zip(p, vs)]
                dq4 = _dot(ds[0], ks[0])
                for t, k in zip(ds[1:], ks[1:]):
                    dq4 = dq4 + _dot(t, k)
                dq4 = dq4 * SCALE
                dqs += [dq4[g * BLK:(g + 1) * BLK, :] for g in range(GROUP)]
                dks.append([_dot_tn(t, q4) for t in ds])
                dvs.append([_dot_tn(t, do4) for t in p])
                ps = jnp.exp(_sink_rows(sink_ref, hk) - lse4) * delta
                for g in range(GROUP):
                    part = -jnp.sum(ps[g * BLK:(g + 1) * BLK, :], axis=0, keepdims=True)
                    dsink_ref[0:1, :] += jnp.where(lane == hk * GROUP + g, part, 0.0)
            dq_ref[rs, :] = jnp.concatenate(dqs, axis=1)

            def piece(i):
                return jnp.concatenate([dks[0][i], dks[1][i], dvs[0][i], dvs[1][i]], axis=1)

            dkvc_ref[...] += piece(0)
            starts = (jnp.maximum(blk - 1, 0), blk, jnp.minimum(blk + 1, nb - 1))
            for i, st in enumerate(starts):
                r = pl.ds(pl.multiple_of(st * BLK, BLK), BLK)
                dkv_ref[r, :] += piece(i + 1)
        _host_finish(n, steps - 1, comm)

    tq = Q_BLOCKS * BLK
    out = pl.pallas_call(
        body, name="attn_bwd", grid=(steps,),
        in_specs=[_rows(tq, Q_W)] + _kv_specs(nb, Q_BLOCKS) + [_acc((n_ctx, 2 * KV_W)), pl.BlockSpec(memory_space=pltpu.SMEM),
                                                     _rows(tq, Q_W), _rows(tq, Q_W), _rows(tq, LANES)] + _comm_specs(ns),
        out_specs=[_rows(tq, Q_W), _acc((seq, 2 * KV_W)), _acc((n_ctx, 2 * KV_W)), _acc((8, LANES))] + _comm_specs(ns),
        out_shape=[jax.ShapeDtypeStruct((seq, Q_W), F32), jax.ShapeDtypeStruct((seq, 2 * KV_W), F32),
                   jax.ShapeDtypeStruct((n_ctx, 2 * KV_W), F32), jax.ShapeDtypeStruct((8, LANES), F32)]
        + [jax.ShapeDtypeStruct(v.shape, v.dtype) for v in scatter],
        scratch_shapes=_comm_scratch(ns) if ns else [],
        compiler_params=_cp("arbitrary"),
    )(q, *([kv] * nkv), kvc, sink, dya, ya, lse, *scatter)
    return out[:4], out[4:]


def _proj_bwd(dq, dkv, dpb, x, dr1, modx, w_in, cos, sin, tm, scatter=()):
    seq = x.shape[0]
    pw = IN_W - Q_W - 2 * KV_W
    ns = len(scatter)

    def body(dq_ref, dkv_ref, dpb_ref, x_ref, dr1_ref, mod_ref, w_ref, cos_ref, sin_ref, *rest):
        dqkv_ref, gx_ref, st_ref = rest[ns:ns + 3]
        comm = _AllToAll(rest[:ns], rest[ns + 3:2 * ns + 3], *rest[2 * ns + 3:]) if ns else None
        _host_start(pl.program_id(0), comm)

        @pl.when(pl.program_id(0) == 0)
        def _():
            st_ref[...] = jnp.zeros_like(st_ref)

        cos1, sin1 = cos_ref[...], sin_ref[...]
        cos2 = jnp.concatenate([cos1, cos1], axis=1)
        sin2 = jnp.concatenate([sin1, sin1], axis=1)
        for j in range(Q_W // 256):
            cs = slice(256 * j, 256 * (j + 1))
            dqkv_ref[:, cs] = _unrope(dq_ref[:, cs], cos2, sin2).astype(BF16)
        dqkv_ref[:, Q_W:Q_W + KV_W] = _unrope(dkv_ref[:, :KV_W], cos1, sin1).astype(BF16)
        dqkv_ref[:, Q_W + KV_W:] = dkv_ref[:, KV_W:].astype(BF16)
        o = Q_W + 2 * KV_W
        dh = _dot(dqkv_ref[...], w_ref[:o, :]) + _dot(dpb_ref[...], w_ref[o:, :])
        xhat, rstd = _ln(x_ref[...])
        st_ref[0:1, :] += _colsum(dh)
        st_ref[1:2, :] += _colsum(dh * xhat)
        gx_ref[...] = _ln_bwd(dh * (1.0 + mod_ref[1:2, :]), xhat, rstd) + ALPHA * dr1_ref[...]
        _host_finish(pl.program_id(0), seq // tm - 1, comm)

    out = pl.pallas_call(
        body, name="proj_bwd", grid=(seq // tm,),
        in_specs=[_rows(tm, Q_W), _rows(tm, 2 * KV_W), _rows(tm, pw), _rows(tm, D), _rows(tm, D), _acc((8, D)),
                  _resident((IN_W, D)), _rows(tm, LANES), _rows(tm, LANES)] + _comm_specs(ns),
        out_specs=[_rows(tm, Q_W + 2 * KV_W), _rows(tm, D), _acc((8, D))] + _comm_specs(ns),
        out_shape=[jax.ShapeDtypeStruct((seq, Q_W + 2 * KV_W), BF16), jax.ShapeDtypeStruct((seq, D), F32),
                   jax.ShapeDtypeStruct((8, D), F32)] + [jax.ShapeDtypeStruct(v.shape, v.dtype) for v in scatter],
        scratch_shapes=_comm_scratch(ns) if ns else [],
        compiler_params=_cp("arbitrary"),
    )(dq, dkv, dpb, x, dr1, modx, w_in, cos, sin, *scatter)
    return out[:3], out[3:]


def _ctx_bwd(dkvc, ctx, hc, w_kv):
    n_ctx = ctx.shape[0]

    def body(dkvc_ref, ctx_ref, hc_ref, w_ref, dw_ref, st_ref):
        d = dkvc_ref[...].astype(BF16)
        dw_ref[...] = _dot_tn(d, hc_ref[...])
        dhc = _dot(d, w_ref[...])
        xhat, _ = _ln(ctx_ref[...])
        st_ref[...] = jnp.zeros_like(st_ref)
        st_ref[0:1, :] = _colsum(dhc)
        st_ref[1:2, :] = _colsum(dhc * xhat)

    return pl.pallas_call(
        body, name="ctx_bwd", grid=(1,),
        in_specs=[_acc((n_ctx, 2 * KV_W)), _acc((n_ctx, D)), _acc((n_ctx, D)), _acc((2 * KV_W, D))],
        out_specs=[_acc((2 * KV_W, D)), _acc((8, D))],
        out_shape=[jax.ShapeDtypeStruct((2 * KV_W, D), F32), jax.ShapeDtypeStruct((8, D), F32)],
        compiler_params=_cp("arbitrary"),
    )(dkvc, ctx, hc, w_kv)


def _tn_matmul(a, b, tn, name, out_dtype, shard_major=False, init=None, tk=512, scatter=()):
    t, ka = a.shape
    n = b.shape[1]
    tk = min(tk, t)
    nk = t // tk
    nj = n // tn
    has_init = init is not None
    ns = len(scatter)
    n_in = 3 if has_init else 2

    def body(*refs):
        a_ref, b_ref = refs[:2]
        i_ref = refs[2] if has_init else None
        rest = refs[n_in:]
        o_ref = rest[ns]
        acc_ref = rest[2 * ns + 1]
        comm = _AllToAll(rest[:ns], rest[ns + 1:2 * ns + 1], *rest[2 * ns + 2:]) if ns else None
        k = pl.program_id(1)
        step = pl.program_id(0) * nk + k
        _host_start(step, comm)

        @pl.when(k == 0)
        def _():
            acc_ref[...] = i_ref[...] if has_init else jnp.zeros_like(acc_ref)

        acc_ref[...] += _dot_tn(a_ref[...], b_ref[...])

        @pl.when(k == nk - 1)
        def _():
            o_ref[...] = acc_ref[...].astype(out_dtype)

        _host_finish(step, nj * nk - 1, comm)

    in_specs = [pl.BlockSpec((tk, ka), lambda j, k: (k, 0)), pl.BlockSpec((tk, tn), lambda j, k: (k, j))]
    args = [a, b]
    if has_init:
        in_specs.append(pl.BlockSpec((ka, tn), lambda j, k: (0, j)))
        args.append(init)
    if shard_major:
        out_spec = pl.BlockSpec((None, ka, tn), lambda j, k: (j, 0, 0))
        out_shape = jax.ShapeDtypeStruct((nj, ka, tn), out_dtype)
    else:
        out_spec = pl.BlockSpec((ka, tn), lambda j, k: (0, j))
        out_shape = jax.ShapeDtypeStruct((ka, n), out_dtype)
    out = pl.pallas_call(
        body, name=name, grid=(nj, nk), in_specs=in_specs + _comm_specs(ns), out_specs=[out_spec] + _comm_specs(ns),
        out_shape=[out_shape] + [jax.ShapeDtypeStruct(v.shape, v.dtype) for v in scatter],
        scratch_shapes=[pltpu.VMEM((ka, tn), F32)] + (_comm_scratch(ns) if ns else []),
        compiler_params=_cp("arbitrary", "arbitrary"),
    )(*args, *scatter)
    return (out[0], out[1:]) if ns else out[0]


ADA_TILE = 512


def _ada_fwd(sc_all, w_ada):
    cs = w_ada.shape[1]

    def body(s_ref, w_ref, o_ref):
        o_ref[...] = _dot(s_ref[...].astype(BF16), w_ref[...].astype(BF16))

    return pl.pallas_call(
        body, name="ada_fwd", grid=(cs // ADA_TILE,),
        in_specs=[_acc((16, D)), pl.BlockSpec((D, ADA_TILE), lambda j: (0, j))],
        out_specs=pl.BlockSpec((16, ADA_TILE), lambda j: (0, j)),
        out_shape=jax.ShapeDtypeStruct((16, cs), F32),
        compiler_params=_cp("arbitrary"),
    )(sc_all, w_ada)


def _ada_bwd(sc_all_t, dm_all, dmc, w_ada):
    cs = w_ada.shape[1]

    def body(st_ref, dm_ref, dmc_ref, w_ref, gw_ref, part_ref):
        @pl.when(pl.program_id(0) == 0)
        def _():
            part_ref[...] = jnp.zeros_like(part_ref)

        gw_ref[...] = _dot(st_ref[...].astype(BF16), dm_ref[...].astype(BF16))
        part_ref[...] += _dot_nt(dmc_ref[...].astype(BF16), w_ref[...].astype(BF16))

    return pl.pallas_call(
        body, name="ada_bwd", grid=(cs // ADA_TILE,),
        in_specs=[_acc((D, 16)), pl.BlockSpec((16, ADA_TILE), lambda j: (0, j)), pl.BlockSpec((8, ADA_TILE), lambda j: (0, j)),
                  pl.BlockSpec((D, ADA_TILE), lambda j: (0, j))],
        out_specs=[pl.BlockSpec((D, ADA_TILE), lambda j: (0, j)), _acc((8, D))],
        out_shape=[jax.ShapeDtypeStruct((D, cs), F32), jax.ShapeDtypeStruct((8, D), F32)],
        compiler_params=_cp("arbitrary"),
    )(sc_all_t, dm_all, dmc, w_ada)


def _sum8(x, name, tr=256):
    _, r, c = x.shape
    tr = min(tr, r)
    while r % tr:
        tr -= 16

    def body(x_ref, o_ref):
        acc = x_ref[0].astype(F32)
        for i in range(1, N_DEV):
            acc = acc + x_ref[i].astype(F32)
        o_ref[...] = acc

    return pl.pallas_call(
        body, name=name, grid=(r // tr,),
        in_specs=[pl.BlockSpec((N_DEV, tr, c), lambda i: (0, i, 0))],
        out_specs=pl.BlockSpec((tr, c), lambda i: (i, 0)),
        out_shape=jax.ShapeDtypeStruct((r, c), F32),
        compiler_params=_cp("arbitrary"),
    )(x)


def _sum_blocks(recv, src, me, name, tr=256):
    _, r, c = recv.shape
    tr = min(tr, r)
    while r % tr:
        tr -= 16

    def body(me_ref, recv_ref, own_ref, o_ref):
        acc = own_ref[...].astype(F32)
        for k in range(1, N_DEV):
            acc = acc + recv_ref[me_ref[0] ^ k].astype(F32)
        o_ref[...] = acc

    return pl.pallas_call(
        body, name=name,
        grid_spec=pltpu.PrefetchScalarGridSpec(
            num_scalar_prefetch=1, grid=(r // tr,),
            in_specs=[pl.BlockSpec((N_DEV, tr, c), lambda i, me_ref: (0, i, 0)),
                      pl.BlockSpec((None, tr, c), lambda i, me_ref: (me_ref[0], i, 0))],
            out_specs=pl.BlockSpec((tr, c), lambda i, me_ref: (i, 0))),
        out_shape=jax.ShapeDtypeStruct((r, c), F32),
        compiler_params=_cp("arbitrary"),
    )(me, recv, src)


def _sum8_many(xs, name):
    n = len(xs)

    def body(*refs):
        for x_ref, o_ref in zip(refs[:n], refs[n:]):
            acc = x_ref[0]
            for i in range(1, N_DEV):
                acc = acc + x_ref[i]
            o_ref[...] = acc

    vmem = pl.BlockSpec(memory_space=pltpu.VMEM)
    return pl.pallas_call(
        body, name=name, in_specs=[vmem] * n, out_specs=[vmem] * n,
        out_shape=[jax.ShapeDtypeStruct(v.shape[1:], v.dtype) for v in xs],
        compiler_params=pltpu.CompilerParams(vmem_limit_bytes=VMEM_LIMIT),
    )(*xs)


def _adam_update(w, g, m, v):
    nm = ADAM_B1 * m + (1.0 - ADAM_B1) * g
    nv = ADAM_B2 * v + (1.0 - ADAM_B2) * (g * g)
    m_hat = nm / (1.0 - ADAM_B1 ** ADAM_STEP)
    v_hat = nv / (1.0 - ADAM_B2 ** ADAM_STEP)
    return -ADAM_LR * (m_hat / (jnp.sqrt(v_hat) + ADAM_EPS) + ADAM_WD * w), nm, nv


ROW_LOSS, ROW_LN2_G, ROW_LN2_B, ROW_LN1_G, ROW_LN1_B = 0, 1, 2, 10, 11
ROWS_DMOD_X = (16, 17, 12, 9, 8, 3)
ROWS_DMOD_C = (24, 25)
SMALL = ("c_ctx", "b_ada", "attn_sink", "gmlp_ln_g", "gmlp_ln_b", "w_spatial", "b_spatial", "ln1_g", "ln1_b", "ln2_g", "ln2_b")


def _adamw_small(sums, dsc, w, m, v):
    n = len(SMALL)

    def body(*refs):
        st_ref, gm_ref, sk_ref, ws_ref, bs_ref, dsc_ref = refs[:6]
        w_refs = dict(zip(SMALL, refs[6:6 + n]))
        m_refs = dict(zip(SMALL, refs[6 + n:6 + 2 * n]))
        v_refs = dict(zip(SMALL, refs[6 + 2 * n:6 + 3 * n]))
        outs = refs[6 + 3 * n:]
        c = w_refs["c_ctx"][...]
        sg = _sig(c)
        dmod = [st_ref[r:r + 1, :] for r in ROWS_DMOD_X]
        dmod[0] = dmod[0] + st_ref[ROWS_DMOD_C[0]:ROWS_DMOD_C[0] + 1, :]
        dmod[1] = dmod[1] + st_ref[ROWS_DMOD_C[1]:ROWS_DMOD_C[1] + 1, :]
        grads = dict(
            c_ctx=dsc_ref[0:1, :] * (sg * (1.0 + c * (1.0 - sg))),
            b_ada=jnp.concatenate(dmod, axis=1),
            attn_sink=sk_ref[0:1, 0:N_KV * GROUP],
            gmlp_ln_g=gm_ref[0:1, :], gmlp_ln_b=gm_ref[1:2, :],
            w_spatial=ws_ref[...], b_spatial=bs_ref[...],
            ln1_g=st_ref[ROW_LN1_G:ROW_LN1_G + 1, :], ln1_b=st_ref[ROW_LN1_B:ROW_LN1_B + 1, :],
            ln2_g=st_ref[ROW_LN2_G:ROW_LN2_G + 1, :], ln2_b=st_ref[ROW_LN2_B:ROW_LN2_B + 1, :])
        for i, name in enumerate(SMALL):
            g = grads[name]
            d, nm, nv = _adam_update(w_refs[name][...], g, m_refs[name][...], v_refs[name][...])
            outs[i][...] = g
            outs[n + i][...] = d
            outs[2 * n + i][...] = nm
            outs[3 * n + i][...] = nv

    vmem = pl.BlockSpec(memory_space=pltpu.VMEM)
    args = list(sums) + [dsc] + [w[k] for k in SMALL] + [m[k] for k in SMALL] + [v[k] for k in SMALL]
    shapes = [jax.ShapeDtypeStruct(w[k].shape, F32) for k in SMALL]
    out = pl.pallas_call(
        body, name="adamw_small", in_specs=[vmem] * len(args), out_specs=[vmem] * (4 * n), out_shape=shapes * 4,
        compiler_params=pltpu.CompilerParams(vmem_limit_bytes=VMEM_LIMIT),
    )(*args)
    return [dict(zip(SMALL, out[i * n:(i + 1) * n])) for i in range(4)]


def _adamw_halves(w, mine, theirs, m, v, c_arr, name):
    r, c = w.shape
    tr = min(256, r // 2)
    while (r // 2) % tr:
        tr -= 8
    nt = (r // 2) // tr

    def body(c_ref, w_ref, mine_ref, theirs_ref, m_ref, v_ref, g_ref, d_ref, nm_ref, nv_ref):
        g = jnp.where(pl.program_id(0) == c_ref[0], mine_ref[...], theirs_ref[...])
        g_ref[...] = g
        d_ref[...], nm_ref[...], nv_ref[...] = _adam_update(w_ref[...], g, m_ref[...], v_ref[...])

    whole = pl.BlockSpec((tr, c), lambda hb, i, c_ref: (hb * nt + i, 0))
    half = pl.BlockSpec((tr, c), lambda hb, i, c_ref: (i, 0))
    shp = jax.ShapeDtypeStruct((r, c), F32)
    return pl.pallas_call(
        body, name=name,
        grid_spec=pltpu.PrefetchScalarGridSpec(
            num_scalar_prefetch=1, grid=(2, nt), in_specs=[whole, half, half, whole, whole], out_specs=[whole] * 4),
        out_shape=[shp] * 4,
        compiler_params=_cp("arbitrary", "arbitrary"),
    )(c_arr, w, mine, theirs, m, v)


def _adamw(w, g, m, v, name):
    r, c = w.shape
    tr = r if r * c <= 256 * 1024 else min(256, r)
    while r % tr:
        tr -= 8

    def body(w_ref, g_ref, m_ref, v_ref, d_ref, nm_ref, nv_ref):
        d_ref[...], nm_ref[...], nv_ref[...] = _adam_update(w_ref[...], g_ref[...], m_ref[...], v_ref[...])

    spec = pl.BlockSpec((tr, c), lambda i: (i, 0))
    shp = jax.ShapeDtypeStruct((r, c), F32)
    return pl.pallas_call(
        body, name=name, grid=(r // tr,), in_specs=[spec] * 4, out_specs=[spec] * 3, out_shape=[shp] * 3,
        compiler_params=_cp("arbitrary"),
    )(w, g, m, v)


def _my_pos():
    return lax.axis_index("x"), lax.axis_index("y"), lax.axis_index("c")


N_COPY = 7


class _Gather:
    def __init__(self, x_refs, out_refs, send_sems, recv_sems):
        self.x_refs, self.out_refs = x_refs, out_refs
        self.send_sems, self.recv_sems = send_sems, recv_sems
        x, y, c = _my_pos()
        self.c = c
        self.me, self.sibling = (x, y, c), (x, y, 1 - c)
        self.chips = [(1 - x, y), (x, 1 - y), (1 - x, 1 - y)]

    def _copy(self, a, k, block, to, from_input=False):
        px, py, pc = block
        rows = self.out_refs[a].at[4 * px + 2 * py + pc]
        return pltpu.make_async_remote_copy(
            src_ref=self.x_refs[a] if from_input else rows, dst_ref=rows,
            send_sem=self.send_sems.at[a * N_COPY + k], recv_sem=self.recv_sems.at[a * N_COPY + k],
            device_id=to, device_id_type=MESH)

    def start(self):
        n = len(self.x_refs)
        for a in range(n):
            self._copy(a, 0, self.me, self.sibling, from_input=True).start()
        for j, chip in enumerate(self.chips):
            for a in range(n):
                self._copy(a, 1 + j, self.me, (*chip, self.c), from_input=True).start()

    def finish(self):
        n = len(self.x_refs)
        c = self.c
        for j, chip in enumerate(self.chips):
            for a in range(n):
                self._copy(a, 1 + j, (*chip, c), self.me).wait_recv()
                self._copy(a, 4 + j, (*chip, c), self.sibling).start()
        for a in range(n):
            self._copy(a, 0, self.sibling, self.me).wait_recv()
        for j, chip in enumerate(self.chips):
            for a in range(n):
                self._copy(a, 4 + j, (*chip, 1 - c), self.me).wait_recv()
        for a in range(n):
            self._copy(a, 0, self.me, self.sibling, from_input=True).wait_send()
            for j, chip in enumerate(self.chips):
                self._copy(a, 1 + j, self.me, (*chip, c), from_input=True).wait_send()
                self._copy(a, 4 + j, (*chip, c), self.sibling).wait_send()


def _comm_scratch(n):
    return [pltpu.SemaphoreType.DMA((n * N_COPY,)), pltpu.SemaphoreType.DMA((n * N_COPY,))]


def _comm_specs(n):
    return [pl.BlockSpec(memory_space=pl.ANY)] * n


def _gathered_shapes(xs):
    return [jax.ShapeDtypeStruct((N_DEV,) + v.shape, v.dtype) for v in xs]


def _with_own(gathered, xs, me):
    return [lax.dynamic_update_index_in_dim(g, v, me, 0) for g, v in zip(gathered, xs)]


def _all_gather(xs, me, name):
    n = len(xs)

    def body(*refs):
        g = _Gather(refs[:n], refs[n:2 * n], *refs[2 * n:])
        g.start()
        g.finish()

    out = pl.pallas_call(
        body, name=name, out_shape=_gathered_shapes(xs), in_specs=_comm_specs(n), out_specs=_comm_specs(n),
        scratch_shapes=_comm_scratch(n),
    )(*xs)
    return _with_own(out, xs, me)


class _AllToAll:
    def __init__(self, x_refs, out_refs, send_sems, recv_sems):
        self.x_refs, self.out_refs = x_refs, out_refs
        self.send_sems, self.recv_sems = send_sems, recv_sems
        self.pos = _my_pos()
        x, y, c = self.pos
        self.me = 4 * x + 2 * y + c

    def _peer(self, k):
        x, y, c = self.pos
        return (x ^ ((k >> 2) & 1), y ^ ((k >> 1) & 1), c ^ (k & 1))

    def _copy(self, a, k):
        p = self._peer(k)
        return pltpu.make_async_remote_copy(
            src_ref=self.x_refs[a].at[4 * p[0] + 2 * p[1] + p[2]], dst_ref=self.out_refs[a].at[self.me],
            send_sem=self.send_sems.at[a * N_COPY + k - 1], recv_sem=self.recv_sems.at[a * N_COPY + k - 1],
            device_id=p, device_id_type=MESH)

    def start(self):
        for k in range(1, N_DEV):
            for a in range(len(self.x_refs)):
                self._copy(a, k).start()

    def finish(self):
        for a in range(len(self.x_refs)):
            for k in range(1, N_DEV):
                self._copy(a, k).wait_recv()
            for k in range(1, N_DEV):
                self._copy(a, k).wait_send()


def _all_to_all(blocks, name):
    n = len(blocks)

    def body(*refs):
        t = _AllToAll(refs[:n], refs[n:2 * n], *refs[2 * n:])
        t.start()
        t.finish()

    return pl.pallas_call(
        body, name=name, out_shape=[jax.ShapeDtypeStruct(v.shape, v.dtype) for v in blocks],
        in_specs=_comm_specs(n), out_specs=_comm_specs(n), scratch_shapes=_comm_scratch(n),
    )(*blocks)


def _sibling_exchange(xs, name):
    n = len(xs)

    def body(*refs):
        x_refs, out_refs = refs[:n], refs[n:2 * n]
        send_sems, recv_sems = refs[2 * n:]
        x, y, c = _my_pos()

        def push(a):
            return pltpu.make_async_remote_copy(
                src_ref=x_refs[a], dst_ref=out_refs[a], send_sem=send_sems.at[a], recv_sem=recv_sems.at[a],
                device_id=(x, y, 1 - c), device_id_type=MESH)

        for a in range(n):
            push(a).start()
        for a in range(n):
            push(a).wait_recv()
            push(a).wait_send()

    return pl.pallas_call(
        body, name=name, out_shape=[jax.ShapeDtypeStruct(v.shape, v.dtype) for v in xs],
        in_specs=_comm_specs(n), out_specs=_comm_specs(n),
        scratch_shapes=[pltpu.SemaphoreType.DMA((n,)), pltpu.SemaphoreType.DMA((n,))],
    )(*xs)


def _scatter_and_gather(scatter, gather, name):
    ns, ng = len(scatter), len(gather)

    def body(*refs):
        s_in, g_in = refs[:ns], refs[ns:ns + ng]
        s_out, g_out = refs[ns + ng:2 * ns + ng], refs[2 * ns + ng:2 * (ns + ng)]
        s_send, s_recv, g_send, g_recv = refs[2 * (ns + ng):]
        g = _Gather(g_in, g_out, g_send, g_recv)
        t = _AllToAll(s_in, s_out, s_send, s_recv)
        g.start()
        t.start()
        g.finish()
        t.finish()

    out = pl.pallas_call(
        body, name=name,
        out_shape=[jax.ShapeDtypeStruct(v.shape, v.dtype) for v in scatter] + _gathered_shapes(gather),
        in_specs=_comm_specs(ns + ng), out_specs=_comm_specs(ns + ng),
        scratch_shapes=_comm_scratch(ns) + _comm_scratch(ng),
    )(*scatter, *gather)
    return out[:ns], out[ns:]


def _row_tile(seq, want):
    return min(want, seq)


def _local_step(x, ctx, tgt, mod_x, mod_c, wb, sink, gmlp_g, gmlp_b, w_s, b_s, ln1_g, ln1_b, ln2_g, ln2_b,
                later=None, me=None):
    seq = x.shape[0]
    on_mesh = me is not None
    modx1 = jnp.concatenate([mod_x[0:2], jnp.zeros((6, D), F32)], axis=0)
    modc = jnp.concatenate([mod_c[0:2], jnp.zeros((6, D), F32)], axis=0)
    vec = jnp.concatenate([mod_x[2:3], ln1_g, ln1_b, mod_x[3:6], ln2_g, ln2_b], axis=0)
    gp = jnp.concatenate([gmlp_g, gmlp_b, jnp.zeros((6, G_W), F32)], axis=0)
    ws_stack = w_s.reshape(N_GRP * BLK, BLK).astype(BF16)
    ws_stack_t = jnp.transpose(w_s, (2, 0, 1)).reshape(BLK, N_GRP * BLK).astype(BF16)
    bias_full = jnp.repeat(b_s.T, GRP_D, axis=1)
    cos, sin = _rope_tables(seq)
    w_in = wb["w_in"]
    w_kv = w_in[Q_W:Q_W + 2 * KV_W, :]
    tm_big = _row_tile(seq, 512)
    tm_ffn = _row_tile(seq, 256)

    hc, kvc, vac = _ctx_fwd(ctx, modc, w_kv)
    behind_proj = ("w_a", "w_b", "w_o") if on_mesh else ()
    behind_attn = ("w_fi", "w_fo") if on_mesh else ()
    (h, q, kv, va, uv, gab), got_proj = _proj_fwd(x, modx1, w_in, cos, sin, tm_big, gather=[later[n] for n in behind_proj])
    (ya, lse), got_attn = _attn_fwd(q, kv, va, kvc, vac, sink, gather=[later[n] for n in behind_attn])
    if on_mesh:
        wb = dict(wb)
        names = behind_proj + behind_attn
        for n, g in zip(names, _with_own(list(got_proj) + list(got_attn), [later[n] for n in names], me)):
            wb[n] = g.reshape(-1, g.shape[2]) if n in ROW_SHARDED else g.reshape(N_SHARD, 2 * g.shape[1], g.shape[2])
    a, b, mix, merged, yb = _mix_fwd(uv, gab, ya, gp, ws_stack, bias_full, wb["w_a"], wb["w_b"], wb["w_o"], tm_big)
    act, h2, dff, df, dr1, st_ffn = _ffn(x, mix, tgt, vec, wb["w_fi"], wb["w_fo"], tm_ffn)
    blocks, recv = {}, {}
    blocks["w_fo"] = _eighths(_tn_matmul(act, df, 512, "tn_w_ffn_out", BF16, tk=2048))
    if on_mesh:
        g_w_fi, (recv["w_fo"],) = _tn_matmul(h2, dff, FH_SHARD, "tn_w_ffn_in", BF16, shard_major=True, tk=2048,
                                            scatter=[blocks["w_fo"]])
    else:
        g_w_fi = _tn_matmul(h2, dff, FH_SHARD, "tn_w_ffn_in", BF16, shard_major=True, tk=2048)
    blocks["w_fi"] = _eighths(g_w_fi)
    (dya, dpb, dws, dbs_full, st4, g_w_o, g_w_a, g_w_b), got = _mix_bwd(
        dr1, a, b, gab, uv, merged, ya, yb, vec, gp, ws_stack, ws_stack_t, bias_full, wb["w_a"], wb["w_b"], wb["w_o"],
        tm_big, scatter=[blocks["w_fi"]] if on_mesh else ())
    recv.update(zip(("w_fi",), got))
    blocks.update(w_o=_eighths(g_w_o), w_a=_eighths(g_w_a), w_b=_eighths(g_w_b))
    mixer = ("w_o", "w_a", "w_b") if on_mesh else ()
    (dq, dkv, dkvc, dsink), got = _attn_bwd(q, kv, kvc, sink, dya, ya, lse, scatter=[blocks[n] for n in mixer])
    recv.update(zip(mixer, got))
    g_wkv_ctx, st0 = _ctx_bwd(dkvc, ctx, hc, w_kv)
    (dqkv, grad_x, st1), _ = _proj_bwd(dq, dkv, dpb, x, dr1, modx1, w_in, cos, sin, tm_big)
    init = jnp.pad(g_wkv_ctx, ((Q_W, 0), (0, 0)))
    g_w_in = jnp.concatenate([_tn_matmul(dqkv, h, D, "tn_w_in_qkv", BF16, init=init, tk=1024),
                              _tn_matmul(dpb, h, D, "tn_w_in_rest", BF16, tk=1024)], axis=0)

    dbs = jnp.sum(dbs_full.reshape(BLK, N_GRP, GRP_D), axis=2).T
    parts = [jnp.concatenate([st_ffn, st1, st0], axis=0), st4, dsink, dws, dbs]
    blocks["w_in"] = _eighths(g_w_in)
    return grad_x, parts, blocks, recv


BIG = ("w_in", "w_a", "w_b", "w_o", "w_fi", "w_fo")
ROW_SHARDED = ("w_o", "w_fo")


def _half_of_shard(shard, c):
    r = shard.shape[0]
    return lax.dynamic_slice_in_dim(shard, c * (r // 2), r // 2, axis=0)


def _eighths(v):
    rows = v.shape[-2] * (v.shape[0] if v.ndim == 3 else 1)
    return v.reshape(N_DEV, rows // N_DEV, v.shape[-1])


def kernel(x, c, ctx, c_ctx, w_ada, b_ada, w_in, attn_sink, gmlp_ln_g, gmlp_ln_b, w_spatial, b_spatial, w_branch_a, w_branch_b, w_out, ln1_g, ln1_b, w_ffn_in, w_ffn_out, ln2_g, ln2_b, loss_target, m_c_ctx, m_w_ada, m_b_ada, m_w_in, m_attn_sink, m_gmlp_ln_g, m_gmlp_ln_b, m_w_spatial, m_b_spatial, m_w_branch_a, m_w_branch_b, m_w_out, m_ln1_g, m_ln1_b, m_w_ffn_in, m_w_ffn_out, m_ln2_g, m_ln2_b, v_c_ctx, v_w_ada, v_b_ada, v_w_in, v_attn_sink, v_gmlp_ln_g, v_gmlp_ln_b, v_w_spatial, v_b_spatial, v_w_branch_a, v_w_branch_b, v_w_out, v_ln1_g, v_ln1_b, v_w_ffn_in, v_w_ffn_out, v_ln2_g, v_ln2_b):
    mx, my, mc = _my_pos()
    me = 4 * mx + 2 * my + mc
    chip = 2 * mx + my
    shards = dict(w_in=w_in[0].T, w_a=w_branch_a[0], w_b=w_branch_b[0], w_o=w_out[0], w_fi=w_ffn_in[0], w_fo=w_ffn_out[0])

    halves = {n: _half_of_shard(shards[n], mc).astype(BF16) for n in BIG}
    c_rows = jnp.concatenate([c, jnp.zeros((7, D), F32)], axis=0)
    g_in, c_g = _all_gather([halves["w_in"], c_rows], me, "gather_w_in")
    wb = dict(w_in=g_in.reshape(IN_W, D))

    c_all = c_g[:, 0, :]
    cc = jnp.concatenate([c_all, c_ctx[None, :], jnp.zeros((7, D), F32)], axis=0)
    sig_cc = jax.nn.sigmoid(cc)
    sc_all = cc * sig_cc
    mod_shard = _ada_fwd(sc_all, w_ada[0])
    mod_g = _all_gather([mod_shard], me, "gather_mod")[0]
    mod_all = jnp.concatenate([mod_g[2 * s] for s in range(4)], axis=1) + b_ada
    mod_x = lax.dynamic_slice_in_dim(mod_all, me, 1, axis=0).reshape(6, D)
    mod_c = mod_all[8].reshape(6, D)[0:2]

    grad_x, parts, blocks, recv = _local_step(
        x[0], ctx[0], loss_target[0], mod_x, mod_c, wb, attn_sink, gmlp_ln_g, gmlp_ln_b, w_spatial[0], b_spatial[0],
        ln1_g, ln1_b, ln2_g, ln2_b, later=halves, me=me)

    (recv["w_in"],), gathered = _scatter_and_gather([blocks["w_in"]], parts, "scatter_w_in_gather_small")
    gathered = _with_own(gathered, parts, me)

    me_arr = jnp.reshape(me, (1,)).astype(jnp.int32)
    summed = {n: _sum_blocks(recv[n], blocks[n], me_arr, "sum_grads_" + n) for n in BIG}
    theirs = dict(zip(BIG, _sibling_exchange([summed[n] for n in BIG], "exchange_grads")))

    sums = _sum8_many(gathered, "sum_small")
    stats = sums[0]
    loss = 0.5 * jnp.sum(stats[ROW_LOSS]) / D
    dmod_x_all = jnp.concatenate([gathered[0][:, r_, :] for r_ in ROWS_DMOD_X], axis=1)
    dmod_c_full = jnp.concatenate([stats[r_] for r_ in ROWS_DMOD_C] + [jnp.zeros((4 * D,), F32)])
    dm_rows = jnp.concatenate([dmod_x_all, dmod_c_full[None, :], jnp.zeros((7, 6 * D), F32)], axis=0)
    cs = w_ada.shape[2]
    dm_shard = lax.dynamic_slice_in_dim(dm_rows, chip * cs, cs, axis=1)
    dmc_shard = jnp.concatenate([dm_shard[8:9], jnp.zeros((7, cs), F32)], axis=0)
    g_w_ada, part = _ada_bwd(sc_all.T, dm_shard, dmc_shard, w_ada[0])
    part_all = _all_gather([part * (mc == 0).astype(F32)], me, "gather_c_ctx")[0]
    dsc = _sum8(part_all, "sum_c_ctx")

    grads = dict(w_ada=g_w_ada[None])
    weights = dict(c_ctx=c_ctx, w_ada=w_ada, b_ada=b_ada, w_in=w_in, attn_sink=attn_sink, gmlp_ln_g=gmlp_ln_g,
                   gmlp_ln_b=gmlp_ln_b, w_spatial=w_spatial, b_spatial=b_spatial, w_branch_a=w_branch_a,
                   w_branch_b=w_branch_b, w_out=w_out, ln1_g=ln1_g, ln1_b=ln1_b, w_ffn_in=w_ffn_in, w_ffn_out=w_ffn_out,
                   ln2_g=ln2_g, ln2_b=ln2_b)
    ms = dict(c_ctx=m_c_ctx, w_ada=m_w_ada, b_ada=m_b_ada, w_in=m_w_in, attn_sink=m_attn_sink, gmlp_ln_g=m_gmlp_ln_g,
              gmlp_ln_b=m_gmlp_ln_b, w_spatial=m_w_spatial, b_spatial=m_b_spatial, w_branch_a=m_w_branch_a,
              w_branch_b=m_w_branch_b, w_out=m_w_out, ln1_g=m_ln1_g, ln1_b=m_ln1_b, w_ffn_in=m_w_ffn_in,
              w_ffn_out=m_w_ffn_out, ln2_g=m_ln2_g, ln2_b=m_ln2_b)
    vs = dict(c_ctx=v_c_ctx, w_ada=v_w_ada, b_ada=v_b_ada, w_in=v_w_in, attn_sink=v_attn_sink, gmlp_ln_g=v_gmlp_ln_g,
              gmlp_ln_b=v_gmlp_ln_b, w_spatial=v_w_spatial, b_spatial=v_b_spatial, w_branch_a=v_w_branch_a,
              w_branch_b=v_w_branch_b, w_out=v_w_out, ln1_g=v_ln1_g, ln1_b=v_ln1_b, w_ffn_in=v_w_ffn_in,
              w_ffn_out=v_w_ffn_out, ln2_g=v_ln2_g, ln2_b=v_ln2_b)
    order = list(weights)
    delta, new_m, new_v = {}, {}, {}
    d_, m_, v_ = _adamw(w_ada[0], g_w_ada, m_w_ada[0], v_w_ada[0], "adamw_w_ada")
    delta["w_ada"], new_m["w_ada"], new_v["w_ada"] = d_[None], m_[None], v_[None]
    c_arr = jnp.reshape(mc, (1,)).astype(jnp.int32)
    names = dict(w_in="w_in", w_a="w_branch_a", w_b="w_branch_b", w_o="w_out", w_fi="w_ffn_in", w_fo="w_ffn_out")
    for k, n in names.items():
        flip = (lambda t: t.T) if k == "w_in" else (lambda t: t)
        outs = _adamw_halves(flip(weights[n][0]), summed[k], theirs[k], flip(ms[n][0]), flip(vs[n][0]), c_arr, "adamw_" + n)
        grads[n], delta[n], new_m[n], new_v[n] = [flip(t)[None] for t in outs]

    def view(a):
        return a.reshape(-1, a.shape[-1]) if a.ndim != 1 else a.reshape(1, -1)

    small = _adamw_small(sums, dsc, *[{n: view(d[n]) for n in SMALL} for d in (weights, ms, vs)])
    for out, src in zip((grads, delta, new_m, new_v), small):
        for n in SMALL:
            out[n] = src[n].reshape(weights[n].shape)

    return (loss, grad_x[None], *[grads[n] for n in order], *[delta[n] for n in order],
            *[new_m[n] for n in order], *[new_v[n] for n in order])
```

```python
import functools
import math

import jax
import jax.numpy as jnp
from jax import lax
from jax.experimental import pallas as pl
from jax.experimental.pallas import tpu as pltpu

F32 = jnp.float32
BF16 = jnp.bfloat16

D = 1024
HEAD = 64
N_KV = 2
GROUP = 4
Q_W = 512
KV_W = 128
G_W = 512
BLK = 128
N_GRP = 8
GRP_D = 64
FH = 2816
IN_W = 3840
GRID_W = 64
ROPE_BASE = 10000.0
LN_EPS = 1e-5
NEG = -1e30
ALPHA = (2 * 1) ** 0.25
SCALE = HEAD ** -0.5
GELU_K = math.sqrt(2.0 / math.pi)
GELU_A = 0.044715
ADAM_LR = 0.001
ADAM_B1 = 0.9
ADAM_B2 = 0.999
ADAM_EPS = 1e-08
ADAM_WD = 0.01
ADAM_STEP = 10
N_DEV = 8
N_SHARD = 4
FH_SHARD = FH // 2
LANES = 128
VMEM_LIMIT = 56 * 1024 * 1024
MESH = pl.DeviceIdType.MESH


def _cp(*sem):
    return pltpu.CompilerParams(dimension_semantics=sem, vmem_limit_bytes=VMEM_LIMIT)


def _resident(shape):
    return pl.BlockSpec(shape, lambda *_: (0,) * len(shape), pipeline_mode=pl.Buffered(1))


def _rows(tm, width):
    return pl.BlockSpec((tm, width), lambda i: (i, 0))


def _acc(shape):
    return pl.BlockSpec(shape, lambda *_: (0,) * len(shape))


def _dot(a, b):
    return jnp.dot(a, b, preferred_element_type=F32)


def _dot_nt(a, b):
    return lax.dot_general(a, b, (((1,), (1,)), ((), ())), preferred_element_type=F32)


def _dot_tn(a, b):
    return lax.dot_general(a, b, (((0,), (0,)), ((), ())), preferred_element_type=F32)


def _ln(x):
    mu = jnp.mean(x, axis=-1, keepdims=True)
    xc = x - mu
    var = jnp.mean(xc * xc, axis=-1, keepdims=True)
    rstd = lax.rsqrt(var + LN_EPS)
    return xc * rstd, rstd


def _ln_bwd(dxhat, xhat, rstd):
    return (dxhat - jnp.mean(dxhat, axis=-1, keepdims=True)
            - xhat * jnp.mean(dxhat * xhat, axis=-1, keepdims=True)) * rstd


def _sig(x):
    return 1.0 / (1.0 + jnp.exp(-x))


def _gelu(x):
    t = jnp.tanh(GELU_K * (x + GELU_A * x * x * x))
    return 0.5 * x * (1.0 + t), t


def _gelu_grad(x, t):
    return 0.5 * (1.0 + t) + 0.5 * x * (1.0 - t * t) * GELU_K * (1.0 + 3.0 * GELU_A * x * x)


def _colsum(v):
    return jnp.sum(v, axis=0, keepdims=True)


def _partner(x):
    w = x.shape[1]
    lane = lax.broadcasted_iota(jnp.int32, x.shape, 1)
    return jnp.where((lane & 31) < 16, pltpu.roll(x, w - 16, 1), pltpu.roll(x, 16, 1))


def _rope(x, cos, sin):
    return x * cos + _partner(x) * sin


def _unrope(g, cos, sin):
    return g * cos + _partner(g * sin)


def _rope_tables(seq):
    inv = ROPE_BASE ** (-jnp.arange(HEAD // 4, dtype=F32) / (HEAD // 4))
    pos = jnp.arange(seq, dtype=jnp.int32)
    ar = (pos // GRID_W).astype(F32)[:, None] * inv
    ac = (pos % GRID_W).astype(F32)[:, None] * inv
    cos = jnp.concatenate([jnp.cos(ar), jnp.cos(ar), jnp.cos(ac), jnp.cos(ac)], axis=-1)
    sin = jnp.concatenate([-jnp.sin(ar), jnp.sin(ar), -jnp.sin(ac), jnp.sin(ac)], axis=-1)
    return jnp.tile(cos, (1, LANES // HEAD)), jnp.tile(sin, (1, LANES // HEAD))


def _ctx_fwd(ctx, modc, w_kv):
    n_ctx = ctx.shape[0]

    def body(ctx_ref, mod_ref, w_ref, hc_ref, kvc_ref, vac_ref):
        xhat, _ = _ln(ctx_ref[...])
        hc = (xhat * (1.0 + mod_ref[1:2, :]) + mod_ref[0:1, :]).astype(BF16)
        hc_ref[...] = hc
        kvc = _dot_nt(hc, w_ref[...]).astype(BF16)
        kvc_ref[...] = kvc
        vac_ref[...] = _with_ones(kvc[:, KV_W:])

    return pl.pallas_call(
        body, name="ctx_fwd", grid=(1,),
        in_specs=[_acc((n_ctx, D)), _acc((8, D)), _acc((2 * KV_W, D))],
        out_specs=[_acc((n_ctx, D)), _acc((n_ctx, 2 * KV_W)), _acc((n_ctx, 2 * LANES))],
        out_shape=[jax.ShapeDtypeStruct((n_ctx, D), BF16), jax.ShapeDtypeStruct((n_ctx, 2 * KV_W), BF16),
                   jax.ShapeDtypeStruct((n_ctx, 2 * LANES), BF16)],
        compiler_params=_cp("arbitrary"),
    )(ctx, modc, w_kv)


def _host_start(step, comm):
    if comm is not None:
        @pl.when(step == 0)
        def _():
            comm.start()


def _host_finish(step, last, comm):
    if comm is not None:
        @pl.when(step == last)
        def _():
            comm.finish()


def _proj_fwd(x, modx, w_in, cos, sin, tm, gather=()):
    seq = x.shape[0]
    ng = len(gather)

    def body(x_ref, mod_ref, w_ref, cos_ref, sin_ref, *rest):
        h_ref, q_ref, kv_ref, va_ref, uv_ref, gab_ref = rest[ng:ng + 6]
        comm = _Gather(rest[:ng], rest[ng + 6:2 * ng + 6], *rest[2 * ng + 6:]) if ng else None
        _host_start(pl.program_id(0), comm)
        xhat, _ = _ln(x_ref[...])
        h = (xhat * (1.0 + mod_ref[1:2, :]) + mod_ref[0:1, :]).astype(BF16)
        h_ref[...] = h
        cos1, sin1 = cos_ref[...], sin_ref[...]
        cos2 = jnp.concatenate([cos1, cos1], axis=1)
        sin2 = jnp.concatenate([sin1, sin1], axis=1)
        for j in range(Q_W // 256):
            t = _dot_nt(h, w_ref[256 * j:256 * (j + 1), :])
            q_ref[:, 256 * j:256 * (j + 1)] = (_rope(t, cos2, sin2) * SCALE).astype(BF16)
        t = _dot_nt(h, w_ref[Q_W:Q_W + 2 * KV_W, :])
        kv_ref[:, :KV_W] = _rope(t[:, :KV_W], cos1, sin1).astype(BF16)
        v = t[:, KV_W:].astype(BF16)
        kv_ref[:, KV_W:] = v
        va_ref[...] = _with_ones(v)
        o = Q_W + 2 * KV_W
        for j in range(2):
            uv_ref[:, G_W * j:G_W * (j + 1)] = _dot_nt(h, w_ref[o + G_W * j:o + G_W * (j + 1), :]).astype(BF16)
        o += 2 * G_W
        for j in range(4):
            gab_ref[:, 512 * j:512 * (j + 1)] = _dot_nt(h, w_ref[o + 512 * j:o + 512 * (j + 1), :]).astype(BF16)
        _host_finish(pl.program_id(0), seq // tm - 1, comm)

    out = pl.pallas_call(
        body, name="proj_fwd", grid=(seq // tm,),
        in_specs=[_rows(tm, D), _acc((8, D)), _resident((IN_W, D)), _rows(tm, LANES), _rows(tm, LANES)] + _comm_specs(ng),
        out_specs=[_rows(tm, D), _rows(tm, Q_W), _rows(tm, 2 * KV_W), _rows(tm, 2 * LANES), _rows(tm, 2 * G_W),
                   _rows(tm, 2 * D)] + _comm_specs(ng),
        out_shape=[jax.ShapeDtypeStruct((seq, D), BF16), jax.ShapeDtypeStruct((seq, Q_W), BF16),
                   jax.ShapeDtypeStruct((seq, 2 * KV_W), BF16), jax.ShapeDtypeStruct((seq, 2 * LANES), BF16),
                   jax.ShapeDtypeStruct((seq, 2 * G_W), BF16), jax.ShapeDtypeStruct((seq, 2 * D), BF16)] + _gathered_shapes(gather),
        scratch_shapes=_comm_scratch(ng) if ng else [],
        compiler_params=_cp("arbitrary"),
    )(x, modx, w_in, cos, sin, *gather)
    return out[:6], out[6:]


def _stack_heads(x, hk):
    return jnp.concatenate([x[:, (hk * GROUP + g) * HEAD:(hk * GROUP + g + 1) * HEAD] for g in range(GROUP)], axis=0)


def _attn_scores(q, k_refs, hk, n, nb):
    q4 = _stack_heads(q, hk)
    ks = [r[:, hk * HEAD:(hk + 1) * HEAD] for r in k_refs]
    rows = GROUP * BLK
    qi = lax.broadcasted_iota(jnp.int32, (rows, BLK), 0) & (BLK - 1)
    kj = lax.broadcasted_iota(jnp.int32, (rows, BLK), 1)
    s = [_dot_nt(q4, k) for k in ks]
    s[1] = jnp.where((kj >= qi) & (n > 0), s[1], NEG)
    s[3] = jnp.where((kj <= qi) & (n < nb - 1), s[3], NEG)
    return q4, ks, s


def _sink_rows(sink_ref, hk):
    rows = GROUP * BLK
    rg = lax.broadcasted_iota(jnp.int32, (rows, 1), 0) >> 7
    sink_v = jnp.full((rows, 1), sink_ref[0, hk * GROUP], F32)
    for g in range(1, GROUP):
        sink_v = jnp.where(rg == g, sink_ref[0, hk * GROUP + g], sink_v)
    return sink_v


def _with_ones(v):
    ones = jnp.ones((v.shape[0], HEAD), v.dtype)
    return jnp.concatenate([v[:, :HEAD], ones, v[:, HEAD:], ones], axis=1)


def _kv_specs(nb, qb):
    def spec(d):
        return pl.BlockSpec((BLK, 2 * KV_W), lambda n: (jnp.clip(qb * n + d, 0, nb - 1), 0))
    return [spec(d) for d in range(-1, qb + 1)]


def _attn_fwd(q, kv, va, kvc, vac, sink, gather=()):
    seq = q.shape[0]
    nb = seq // BLK
    n_ctx = kvc.shape[0]
    ng = len(gather)
    Q_BLOCKS = 1
    nkv = Q_BLOCKS + 2
    steps = nb // Q_BLOCKS

    def body(q_ref, *rest):
        kv_refs, va_refs = rest[:nkv], rest[nkv:2 * nkv]
        kvc_ref, vac_ref, sink_ref = rest[2 * nkv:2 * nkv + 3]
        rest = rest[2 * nkv + 3:]
        o_ref, lse_ref = rest[ng:ng + 2]
        comm = _Gather(rest[:ng], rest[ng + 2:2 * ng + 2], *rest[2 * ng + 2:]) if ng else None
        n = pl.program_id(0)
        _host_start(n, comm)
        lane = lax.broadcasted_iota(jnp.int32, (BLK, LANES), 1)
        for sub in range(Q_BLOCKS):
            rs = slice(sub * BLK, (sub + 1) * BLK)
            q = q_ref[rs, :]
            outs = []
            lse_all = jnp.zeros((BLK, LANES), F32)
            for hk in range(N_KV):
                _, _, s = _attn_scores(q, (kvc_ref,) + kv_refs[sub:sub + 3], hk, Q_BLOCKS * n + sub, nb)
                sink_v = _sink_rows(sink_ref, hk)
                m = sink_v
                for t in s:
                    m = jnp.maximum(m, jnp.max(t, axis=-1, keepdims=True))
                o = jnp.zeros((GROUP * BLK, LANES), F32)
                for t, va_ref in zip(s, (vac_ref,) + va_refs[sub:sub + 3]):
                    o = o + _dot(jnp.exp((t - m).astype(BF16)), va_ref[:, hk * LANES:(hk + 1) * LANES])
                denom = o[:, HEAD:HEAD + 1] + jnp.exp(sink_v - m)
                o4 = o[:, :HEAD] * (1.0 / denom)
                lse4 = m + jnp.log(denom)
                for g in range(GROUP):
                    outs.append(o4[g * BLK:(g + 1) * BLK, :])
                    lse_all = jnp.where(lane == hk * GROUP + g, lse4[g * BLK:(g + 1) * BLK, :], lse_all)
            o_ref[rs, :] = jnp.concatenate(outs, axis=1).astype(BF16)
            lse_ref[rs, :] = lse_all
        _host_finish(n, steps - 1, comm)

    tq = Q_BLOCKS * BLK
    out = pl.pallas_call(
        body, name="attn_fwd", grid=(steps,),
        in_specs=[_rows(tq, Q_W)] + _kv_specs(nb, Q_BLOCKS) + _kv_specs(nb, Q_BLOCKS)
        + [_acc((n_ctx, 2 * KV_W)), _acc((n_ctx, 2 * LANES)), pl.BlockSpec(memory_space=pltpu.SMEM)] + _comm_specs(ng),
        out_specs=[_rows(tq, Q_W), _rows(tq, LANES)] + _comm_specs(ng),
        out_shape=[jax.ShapeDtypeStruct((seq, Q_W), BF16), jax.ShapeDtypeStruct((seq, LANES), F32)] + _gathered_shapes(gather),
        scratch_shapes=_comm_scratch(ng) if ng else [],
        compiler_params=_cp("arbitrary"),
    )(q, *([kv] * nkv), *([va] * nkv), kvc, vac, sink, *gather)
    return out[:2], out[2:]


def _gmlp_chunk(u, vb, gp_ref, ws_ref, bias_ref):
    gu, tu = _gelu(u)
    gv, tv = _gelu(vb)
    vhat, rstd = _ln(gv)
    vn = (vhat * gp_ref[0:1, :] + gp_ref[1:2, :]).astype(BF16)
    s = bias_ref[...] + jnp.concatenate(
        [_dot(ws_ref[g * BLK:(g + 1) * BLK, :], vn[:, g * GRP_D:(g + 1) * GRP_D]) for g in range(N_GRP)], axis=1)
    return gu, tu, tv, vhat, rstd, vn, s


def _mix_fwd(uv, gab, ya, gp, ws_stack, bias_full, w_a, w_b, w_o, tm):
    seq = uv.shape[0]

    def body(uv_ref, gab_ref, ya_ref, gp_ref, ws_ref, bias_ref, wa_ref, wb_ref, wo_ref,
             a_ref, b_ref, mix_ref, merged_ref, yb_ref):
        for c in range(tm // BLK):
            rs = slice(c * BLK, (c + 1) * BLK)
            gu, _, _, _, _, _, s = _gmlp_chunk(uv_ref[rs, :G_W].astype(F32), uv_ref[rs, G_W:].astype(F32), gp_ref, ws_ref, bias_ref)
            yb_ref[rs, :] = (gu * s).astype(BF16)
        ya = ya_ref[...]
        yb = yb_ref[...]
        for s in range(N_SHARD):
            cs = slice(s * (D // N_SHARD), (s + 1) * (D // N_SHARD))
            a = _dot(ya, wa_ref[s])
            b = _dot(yb, wb_ref[s])
            a_ref[:, cs] = a.astype(BF16)
            b_ref[:, cs] = b.astype(BF16)
            ga = gab_ref[:, cs].astype(F32)
            gb = gab_ref[:, D + s * (D // N_SHARD):D + (s + 1) * (D // N_SHARD)].astype(F32)
            merged_ref[:, cs] = (_sig(ga) * a + _sig(gb) * b).astype(BF16)
        mix_ref[...] = _dot(merged_ref[...], wo_ref[...])

    return pl.pallas_call(
        body, name="mix_fwd", grid=(seq // tm,),
        in_specs=[_rows(tm, 2 * G_W), _rows(tm, 2 * D), _rows(tm, Q_W), _acc((8, G_W)),
                  _resident((N_GRP * BLK, BLK)), _acc((BLK, G_W)),
                  _resident((N_SHARD, Q_W, D // N_SHARD)), _resident((N_SHARD, G_W, D // N_SHARD)), _resident((D, D))],
        out_specs=[_rows(tm, D), _rows(tm, D), _rows(tm, D), _rows(tm, D), _rows(tm, G_W)],
        out_shape=[jax.ShapeDtypeStruct((seq, D), BF16), jax.ShapeDtypeStruct((seq, D), BF16),
                   jax.ShapeDtypeStruct((seq, D), F32), jax.ShapeDtypeStruct((seq, D), BF16),
                   jax.ShapeDtypeStruct((seq, G_W), BF16)],
        compiler_params=_cp("arbitrary"),
    )(uv, gab, ya, gp, ws_stack, bias_full, w_a, w_b, w_o)


FFN_CHUNK = 512


def _ffn_chunks():
    out = []
    for hh in range(2):
        off = 0
        while off < FH_SHARD:
            w = min(FFN_CHUNK, FH_SHARD - off)
            out.append((hh, off, w))
            off += w
    return out


def _mid_recompute(x_ref, mix_ref, vec_ref):
    r1 = ALPHA * x_ref[...] + vec_ref[0:1, :] * mix_ref[...]
    xh1, rstd1 = _ln(r1)
    xmid = xh1 * vec_ref[1:2, :] + vec_ref[2:3, :]
    xh2, rstd2 = _ln(xmid)
    return xh1, rstd1, xmid, xh2, rstd2


def _ffn(x, mix, tgt, vec, w_fi, w_fo, tm):
    seq = x.shape[0]

    def body(x_ref, mix_ref, tgt_ref, vec_ref, wi_ref, wo_ref, act_ref, h2_ref, dff_ref, df_ref, dr1_ref, st_ref, gu_ref):
        @pl.when(pl.program_id(0) == 0)
        def _():
            st_ref[...] = jnp.zeros_like(st_ref)

        xh1, rstd1, xmid, xh2, rstd2 = _mid_recompute(x_ref, mix_ref, vec_ref)
        h2 = (xh2 * (1.0 + vec_ref[4:5, :]) + vec_ref[3:4, :]).astype(BF16)
        h2_ref[...] = h2
        f = jnp.zeros((tm, D), F32)
        for hh, off, w in _ffn_chunks():
            cs = slice(hh * FH_SHARD + off, hh * FH_SHARD + off + w)
            cu = slice(FH + hh * FH_SHARD + off, FH + hh * FH_SHARD + off + w)
            g = _dot(h2, wi_ref[hh, :, off:off + w])
            u = _dot(h2, wi_ref[2 + hh, :, off:off + w])
            gu_ref[:, cs] = g
            gu_ref[:, cu] = u
            a = (g * _sig(g) * u).astype(BF16)
            act_ref[:, cs] = a
            f = f + _dot(a, wo_ref[cs, :])
        r2 = ALPHA * xmid + vec_ref[5:6, :] * f
        yh, rstd = _ln(r2)
        y = yh * vec_ref[6:7, :] + vec_ref[7:8, :]
        err = y - tgt_ref[...]
        dy = err / D
        dr2 = _ln_bwd(dy * vec_ref[6:7, :], yh, rstd)
        st_ref[0:1, :] += _colsum(err * err)
        st_ref[1:2, :] += _colsum(dy * yh)
        st_ref[2:3, :] += _colsum(dy)
        st_ref[3:4, :] += _colsum(dr2 * f)

        df = (dr2 * vec_ref[5:6, :]).astype(BF16)
        df_ref[...] = df
        dh2 = jnp.zeros((tm, D), F32)
        for hh, off, w in _ffn_chunks():
            cs = slice(hh * FH_SHARD + off, hh * FH_SHARD + off + w)
            cu = slice(FH + hh * FH_SHARD + off, FH + hh * FH_SHARD + off + w)
            da = _dot_nt(df, wo_ref[cs, :])
            g = gu_ref[:, cs]
            u = gu_ref[:, cu]
            sg = _sig(g)
            dg = (da * u * sg * (1.0 + g * (1.0 - sg))).astype(BF16)
            du = (da * g * sg).astype(BF16)
            dff_ref[:, cs] = dg
            dff_ref[:, cu] = du
            dh2 = dh2 + _dot_nt(dg, wi_ref[hh, :, off:off + w]) + _dot_nt(du, wi_ref[2 + hh, :, off:off + w])
        dxmid = _ln_bwd(dh2 * (1.0 + vec_ref[4:5, :]), xh2, rstd2) + ALPHA * dr2
        dr1 = _ln_bwd(dxmid * vec_ref[1:2, :], xh1, rstd1)
        dr1_ref[...] = dr1
        st_ref[8:9, :] += _colsum(dh2 * xh2)
        st_ref[9:10, :] += _colsum(dh2)
        st_ref[10:11, :] += _colsum(dxmid * xh1)
        st_ref[11:12, :] += _colsum(dxmid)
        st_ref[12:13, :] += _colsum(dr1 * mix_ref[...])

    return pl.pallas_call(
        body, name="ffn", grid=(seq // tm,),
        in_specs=[_rows(tm, D), _rows(tm, D), _rows(tm, D), _acc((8, D)), _resident((N_SHARD, D, FH_SHARD)), _resident((FH, D))],
        out_specs=[_rows(tm, FH), _rows(tm, D), _rows(tm, 2 * FH), _rows(tm, D), _rows(tm, D), _acc((16, D))],
        out_shape=[jax.ShapeDtypeStruct((seq, FH), BF16), jax.ShapeDtypeStruct((seq, D), BF16),
                   jax.ShapeDtypeStruct((seq, 2 * FH), BF16), jax.ShapeDtypeStruct((seq, D), BF16),
                   jax.ShapeDtypeStruct((seq, D), F32), jax.ShapeDtypeStruct((16, D), F32)],
        scratch_shapes=[pltpu.VMEM((tm, 2 * FH), F32)],
        compiler_params=_cp("arbitrary"),
    )(x, mix, tgt, vec, w_fi, w_fo)


def _mix_bwd(dr1, a, b, gab, uv, merged, ya, yb, vec, gp, ws_stack, ws_stack_t, bias_full, w_a, w_b, w_o, tm, scatter=()):
    seq = dr1.shape[0]
    last = seq // tm - 1
    cw = D // N_SHARD
    ns = len(scatter)

    def body(dr1_ref, a_ref, b_ref, gab_ref, uv_ref, mg_ref, ya_ref, yb_ref, vec_ref, gp_ref, ws_ref, wst_ref, bias_ref,
             wa_ref, wb_ref, wo_ref, *rest):
        dya_ref, dp_ref, dws_ref, dbs_ref, st_ref, gwo_ref, gwa_ref, gwb_ref = rest[ns:ns + 8]
        acc_o, acc_a, acc_b = rest[2 * ns + 8:2 * ns + 11]
        comm = _AllToAll(rest[:ns], rest[ns + 8:2 * ns + 8], *rest[2 * ns + 11:]) if ns else None
        _host_start(pl.program_id(0), comm)

        @pl.when(pl.program_id(0) == 0)
        def _():
            dws_ref[...] = jnp.zeros_like(dws_ref)
            dbs_ref[...] = jnp.zeros_like(dbs_ref)
            st_ref[...] = jnp.zeros_like(st_ref)
            acc_o[...] = jnp.zeros_like(acc_o)
            acc_a[...] = jnp.zeros_like(acc_a)
            acc_b[...] = jnp.zeros_like(acc_b)

        dmix = (dr1_ref[...] * vec_ref[0:1, :]).astype(BF16)
        acc_o[...] += _dot_tn(mg_ref[...], dmix)
        dmerged = _dot_nt(dmix, wo_ref[...])
        sa = _sig(gab_ref[:, :D].astype(F32))
        sb = _sig(gab_ref[:, D:].astype(F32))
        da = (dmerged * sa).astype(BF16)
        db = (dmerged * sb).astype(BF16)
        dp_ref[:, 2 * G_W:2 * G_W + D] = (dmerged * a_ref[...].astype(F32) * sa * (1.0 - sa)).astype(BF16)
        dp_ref[:, 2 * G_W + D:] = (dmerged * b_ref[...].astype(F32) * sb * (1.0 - sb)).astype(BF16)
        dya = jnp.zeros((tm, Q_W), F32)
        dyb = jnp.zeros((tm, G_W), F32)
        ya = ya_ref[...]
        yb = yb_ref[...]
        for s in range(N_SHARD):
            cs = slice(s * cw, (s + 1) * cw)
            dya = dya + _dot_nt(da[:, cs], wa_ref[s])
            dyb = dyb + _dot_nt(db[:, cs], wb_ref[s])
            acc_a[s] += _dot_tn(ya, da[:, cs])
            acc_b[s] += _dot_tn(yb, db[:, cs])
        dya_ref[...] = dya.astype(BF16)

        @pl.when(pl.program_id(0) == last)
        def _():
            gwo_ref[...] = acc_o[...].astype(BF16)
            gwa_ref[...] = acc_a[...].astype(BF16)
            gwb_ref[...] = acc_b[...].astype(BF16)

        for c in range(tm // BLK):
            rs = slice(c * BLK, (c + 1) * BLK)
            u = uv_ref[rs, :G_W].astype(F32)
            vb = uv_ref[rs, G_W:].astype(F32)
            gu, tu, tv, vhat, rstd, vn, s = _gmlp_chunk(u, vb, gp_ref, ws_ref, bias_ref)
            dyb_c = dyb[rs, :]
            ds = dyb_c * gu
            du = dyb_c * s * _gelu_grad(u, tu)
            ds_b = ds.astype(BF16)
            dvn_g = []
            for g in range(N_GRP):
                cg = slice(g * GRP_D, (g + 1) * GRP_D)
                dvn_g.append(_dot(wst_ref[:, g * BLK:(g + 1) * BLK], ds_b[:, cg]))
                dws_ref[g * BLK:(g + 1) * BLK, :] += _dot_nt(ds_b[:, cg], vn[:, cg])
            dvn = jnp.concatenate(dvn_g, axis=1)
            dbs_ref[...] += ds
            st_ref[0:1, :] += _colsum(dvn * vhat)
            st_ref[1:2, :] += _colsum(dvn)
            dgv = _ln_bwd(dvn * gp_ref[0:1, :], vhat, rstd)
            dvb = dgv * _gelu_grad(vb, tv)
            dp_ref[rs, :G_W] = du.astype(BF16)
            dp_ref[rs, G_W:2 * G_W] = dvb.astype(BF16)
        _host_finish(pl.program_id(0), last, comm)

    pw = 2 * G_W + 2 * D
    out = pl.pallas_call(
        body, name="mix_bwd", grid=(seq // tm,),
        in_specs=[_rows(tm, D), _rows(tm, D), _rows(tm, D), _rows(tm, 2 * D), _rows(tm, 2 * G_W), _rows(tm, D), _rows(tm, Q_W),
                  _rows(tm, G_W), _acc((8, D)), _acc((8, G_W)),
                  _resident((N_GRP * BLK, BLK)), _resident((BLK, N_GRP * BLK)), _acc((BLK, G_W)),
                  _resident((N_SHARD, Q_W, cw)), _resident((N_SHARD, G_W, cw)), _resident((D, D))] + _comm_specs(ns),
        out_specs=[_rows(tm, Q_W), _rows(tm, pw), _acc((N_GRP * BLK, BLK)), _acc((BLK, G_W)), _acc((8, G_W)),
                   _acc((D, D)), _acc((N_SHARD, Q_W, cw)), _acc((N_SHARD, G_W, cw))] + _comm_specs(ns),
        out_shape=[jax.ShapeDtypeStruct((seq, Q_W), BF16), jax.ShapeDtypeStruct((seq, pw), BF16),
                   jax.ShapeDtypeStruct((N_GRP * BLK, BLK), F32), jax.ShapeDtypeStruct((BLK, G_W), F32),
                   jax.ShapeDtypeStruct((8, G_W), F32), jax.ShapeDtypeStruct((D, D), BF16),
                   jax.ShapeDtypeStruct((N_SHARD, Q_W, cw), BF16), jax.ShapeDtypeStruct((N_SHARD, G_W, cw), BF16)]
        + [jax.ShapeDtypeStruct(v.shape, v.dtype) for v in scatter],
        scratch_shapes=[pltpu.VMEM((D, D), F32), pltpu.VMEM((N_SHARD, Q_W, cw), F32), pltpu.VMEM((N_SHARD, G_W, cw), F32)]
        + (_comm_scratch(ns) if ns else []),
        compiler_params=_cp("arbitrary"),
    )(dr1, a, b, gab, uv, merged, ya, yb, vec, gp, ws_stack, ws_stack_t, bias_full, w_a, w_b, w_o, *scatter)
    return out[:8], out[8:]


def _attn_bwd(q, kv, kvc, sink, dya, ya, lse, scatter=()):
    seq = q.shape[0]
    nb = seq // BLK
    n_ctx = kvc.shape[0]
    ns = len(scatter)
    Q_BLOCKS = 2
    nkv = Q_BLOCKS + 2
    steps = nb // Q_BLOCKS

    def body(q_ref, *rest):
        kv_refs = rest[:nkv]
        kvc_ref, sink_ref, do_ref, o_ref, lse_ref = rest[nkv:nkv + 5]
        rest = rest[nkv + 5:]
        dq_ref, dkv_ref, dkvc_ref, dsink_ref = rest[ns:ns + 4]
        comm = _AllToAll(rest[:ns], rest[ns + 4:2 * ns + 4], *rest[2 * ns + 4:]) if ns else None
        n = pl.program_id(0)
        _host_start(n, comm)

        @pl.when(n == 0)
        def _():
            dkv_ref[...] = jnp.zeros_like(dkv_ref)
            dkvc_ref[...] = jnp.zeros_like(dkvc_ref)
            dsink_ref[...] = jnp.zeros_like(dsink_ref)

        lane = lax.broadcasted_iota(jnp.int32, (1, LANES), 1)
        for sub in range(Q_BLOCKS):
            rs = slice(sub * BLK, (sub + 1) * BLK)
            blk = Q_BLOCKS * n + sub
            q = q_ref[rs, :]
            do = do_ref[rs, :]
            out = o_ref[rs, :]
            lse_all = lse_ref[rs, :]
            k_refs = (kvc_ref,) + kv_refs[sub:sub + 3]
            dqs, dks, dvs = [], [], []
            for hk in range(N_KV):
                q4, ks, s = _attn_scores(q, k_refs, hk, blk, nb)
                vs = [r[:, KV_W + hk * HEAD:KV_W + (hk + 1) * HEAD] for r in k_refs]
                lse4 = jnp.concatenate([lse_all[:, hk * GROUP + g:hk * GROUP + g + 1] for g in range(GROUP)], axis=0)
                do4 = _stack_heads(do, hk)
                delta = jnp.sum(do4.astype(F32) * _stack_heads(out, hk).astype(F32), axis=-1, keepdims=True)
                p = [jnp.exp((t - lse4).astype(BF16)) for t in s]
                ds = [t * (_dot_nt(do4, v) - delta).astype(BF16) for t, v in zip(p, vs)]
                dq4 = _dot(ds[0], ks[0])
                for t, k in zip(ds[1:], ks[1:]):
                    dq4 = dq4 + _dot(t, k)
                dq4 = dq4 * SCALE
                dqs += [dq4[g * BLK:(g + 1) * BLK, :] for g in range(GROUP)]
                dks.append([_dot_tn(t, q4) for t in ds])
                dvs.append([_dot_tn(t, do4) for t in p])
                ps = jnp.exp(_sink_rows(sink_ref, hk) - lse4) * delta
                for g in range(GROUP):
                    part = -jnp.sum(ps[g * BLK:(g + 1) * BLK, :], axis=0, keepdims=True)
                    dsink_ref[0:1, :] += jnp.where(lane == hk * GROUP + g, part, 0.0)
            dq_ref[rs, :] = jnp.concatenate(dqs, axis=1)

            def piece(i):
                return jnp.concatenate([dks[0][i], dks[1][i], dvs[0][i], dvs[1][i]], axis=1)

            dkvc_ref[...] += piece(0)
            starts = (jnp.maximum(blk - 1, 0), blk, jnp.minimum(blk + 1, nb - 1))
            for i, st in enumerate(starts):
                r = pl.ds(pl.multiple_of(st * BLK, BLK), BLK)
                dkv_ref[r, :] += piece(i + 1)
        _host_finish(n, steps - 1, comm)

    tq = Q_BLOCKS * BLK
    out = pl.pallas_call(
        body, name="attn_bwd", grid=(steps,),
        in_specs=[_rows(tq, Q_W)] + _kv_specs(nb, Q_BLOCKS) + [_acc((n_ctx, 2 * KV_W)), pl.BlockSpec(memory_space=pltpu.SMEM),
                                                     _rows(tq, Q_W), _rows(tq, Q_W), _rows(tq, LANES)] + _comm_specs(ns),
        out_specs=[_rows(tq, Q_W), _acc((seq, 2 * KV_W)), _acc((n_ctx, 2 * KV_W)), _acc((8, LANES))] + _comm_specs(ns),
        out_shape=[jax.ShapeDtypeStruct((seq, Q_W), F32), jax.ShapeDtypeStruct((seq, 2 * KV_W), F32),
                   jax.ShapeDtypeStruct((n_ctx, 2 * KV_W), F32), jax.ShapeDtypeStruct((8, LANES), F32)]
        + [jax.ShapeDtypeStruct(v.shape, v.dtype) for v in scatter],
        scratch_shapes=_comm_scratch(ns) if ns else [],
        compiler_params=_cp("arbitrary"),
    )(q, *([kv] * nkv), kvc, sink, dya, ya, lse, *scatter)
    return out[:4], out[4:]


def _proj_bwd(dq, dkv, dpb, x, dr1, modx, w_in, cos, sin, tm, scatter=()):
    seq = x.shape[0]
    pw = IN_W - Q_W - 2 * KV_W
    ns = len(scatter)

    def body(dq_ref, dkv_ref, dpb_ref, x_ref, dr1_ref, mod_ref, w_ref, cos_ref, sin_ref, *rest):
        dqkv_ref, gx_ref, st_ref = rest[ns:ns + 3]
        comm = _AllToAll(rest[:ns], rest[ns + 3:2 * ns + 3], *rest[2 * ns + 3:]) if ns else None
        _host_start(pl.program_id(0), comm)

        @pl.when(pl.program_id(0) == 0)
        def _():
            st_ref[...] = jnp.zeros_like(st_ref)

        cos1, sin1 = cos_ref[...], sin_ref[...]
        cos2 = jnp.concatenate([cos1, cos1], axis=1)
        sin2 = jnp.concatenate([sin1, sin1], axis=1)
        for j in range(Q_W // 256):
            cs = slice(256 * j, 256 * (j + 1))
            dqkv_ref[:, cs] = _unrope(dq_ref[:, cs], cos2, sin2).astype(BF16)
        dqkv_ref[:, Q_W:Q_W + KV_W] = _unrope(dkv_ref[:, :KV_W], cos1, sin1).astype(BF16)
        dqkv_ref[:, Q_W + KV_W:] = dkv_ref[:, KV_W:].astype(BF16)
        o = Q_W + 2 * KV_W
        dh = _dot(dqkv_ref[...], w_ref[:o, :]) + _dot(dpb_ref[...], w_ref[o:, :])
        xhat, rstd = _ln(x_ref[...])
        st_ref[0:1, :] += _colsum(dh)
        st_ref[1:2, :] += _colsum(dh * xhat)
        gx_ref[...] = _ln_bwd(dh * (1.0 + mod_ref[1:2, :]), xhat, rstd) + ALPHA * dr1_ref[...]
        _host_finish(pl.program_id(0), seq // tm - 1, comm)

    out = pl.pallas_call(
        body, name="proj_bwd", grid=(seq // tm,),
        in_specs=[_rows(tm, Q_W), _rows(tm, 2 * KV_W), _rows(tm, pw), _rows(tm, D), _rows(tm, D), _acc((8, D)),
                  _resident((IN_W, D)), _rows(tm, LANES), _rows(tm, LANES)] + _comm_specs(ns),
        out_specs=[_rows(tm, Q_W + 2 * KV_W), _rows(tm, D), _acc((8, D))] + _comm_specs(ns),
        out_shape=[jax.ShapeDtypeStruct((seq, Q_W + 2 * KV_W), BF16), jax.ShapeDtypeStruct((seq, D), F32),
                   jax.ShapeDtypeStruct((8, D), F32)] + [jax.ShapeDtypeStruct(v.shape, v.dtype) for v in scatter],
        scratch_shapes=_comm_scratch(ns) if ns else [],
        compiler_params=_cp("arbitrary"),
    )(dq, dkv, dpb, x, dr1, modx, w_in, cos, sin, *scatter)
    return out[:3], out[3:]


def _ctx_bwd(dkvc, ctx, hc, w_kv):
    n_ctx = ctx.shape[0]

    def body(dkvc_ref, ctx_ref, hc_ref, w_ref, dw_ref, st_ref):
        d = dkvc_ref[...].astype(BF16)
        dw_ref[...] = _dot_tn(d, hc_ref[...])
        dhc = _dot(d, w_ref[...])
        xhat, _ = _ln(ctx_ref[...])
        st_ref[...] = jnp.zeros_like(st_ref)
        st_ref[0:1, :] = _colsum(dhc)
        st_ref[1:2, :] = _colsum(dhc * xhat)

    return pl.pallas_call(
        body, name="ctx_bwd", grid=(1,),
        in_specs=[_acc((n_ctx, 2 * KV_W)), _acc((n_ctx, D)), _acc((n_ctx, D)), _acc((2 * KV_W, D))],
        out_specs=[_acc((2 * KV_W, D)), _acc((8, D))],
        out_shape=[jax.ShapeDtypeStruct((2 * KV_W, D), F32), jax.ShapeDtypeStruct((8, D), F32)],
        compiler_params=_cp("arbitrary"),
    )(dkvc, ctx, hc, w_kv)


def _tn_matmul(a, b, tn, name, out_dtype, shard_major=False, init=None, tk=512, scatter=(), gather=()):
    t, ka = a.shape
    n = b.shape[1]
    tk = min(tk, t)
    nk = t // tk
    nj = n // tn
    has_init = init is not None
    assert not (scatter and gather)
    moved = list(scatter) + list(gather)
    pattern = _AllToAll if scatter else _Gather
    ns = len(moved)
    n_in = 3 if has_init else 2

    def body(*refs):
        a_ref, b_ref = refs[:2]
        i_ref = refs[2] if has_init else None
        rest = refs[n_in:]
        o_ref = rest[ns]
        acc_ref = rest[2 * ns + 1]
        comm = pattern(rest[:ns], rest[ns + 1:2 * ns + 1], *rest[2 * ns + 2:]) if ns else None
        k = pl.program_id(1)
        step = pl.program_id(0) * nk + k
        _host_start(step, comm)

        @pl.when(k == 0)
        def _():
            acc_ref[...] = i_ref[...] if has_init else jnp.zeros_like(acc_ref)

        acc_ref[...] += _dot_tn(a_ref[...], b_ref[...])

        @pl.when(k == nk - 1)
        def _():
            o_ref[...] = acc_ref[...].astype(out_dtype)

        _host_finish(step, nj * nk - 1, comm)

    in_specs = [pl.BlockSpec((tk, ka), lambda j, k: (k, 0)), pl.BlockSpec((tk, tn), lambda j, k: (k, j))]
    args = [a, b]
    if has_init:
        in_specs.append(pl.BlockSpec((ka, tn), lambda j, k: (0, j)))
        args.append(init)
    if shard_major:
        out_spec = pl.BlockSpec((None, ka, tn), lambda j, k: (j, 0, 0))
        out_shape = jax.ShapeDtypeStruct((nj, ka, tn), out_dtype)
    else:
        out_spec = pl.BlockSpec((ka, tn), lambda j, k: (0, j))
        out_shape = jax.ShapeDtypeStruct((ka, n), out_dtype)
    out = pl.pallas_call(
        body, name=name, grid=(nj, nk), in_specs=in_specs + _comm_specs(ns), out_specs=[out_spec] + _comm_specs(ns),
        out_shape=[out_shape] + [jax.ShapeDtypeStruct(v.shape, v.dtype) for v in scatter] + _gathered_shapes(gather),
        scratch_shapes=[pltpu.VMEM((ka, tn), F32)] + (_comm_scratch(ns) if ns else []),
        compiler_params=_cp("arbitrary", "arbitrary"),
    )(*args, *moved)
    return (out[0], out[1:]) if ns else out[0]


ADA_TILE = 512


def _ada_fwd(sc_all, w_ada):
    cs = w_ada.shape[1]

    def body(s_ref, w_ref, o_ref):
        o_ref[...] = _dot(s_ref[...].astype(BF16), w_ref[...].astype(BF16))

    return pl.pallas_call(
        body, name="ada_fwd", grid=(cs // ADA_TILE,),
        in_specs=[_acc((16, D)), pl.BlockSpec((D, ADA_TILE), lambda j: (0, j))],
        out_specs=pl.BlockSpec((16, ADA_TILE), lambda j: (0, j)),
        out_shape=jax.ShapeDtypeStruct((16, cs), F32),
        compiler_params=_cp("arbitrary"),
    )(sc_all, w_ada)


def _ada_bwd(sc_all_t, dm_all, dmc, w_ada):
    cs = w_ada.shape[1]

    def body(st_ref, dm_ref, dmc_ref, w_ref, gw_ref, part_ref):
        @pl.when(pl.program_id(0) == 0)
        def _():
            part_ref[...] = jnp.zeros_like(part_ref)

        gw_ref[...] = _dot(st_ref[...].astype(BF16), dm_ref[...].astype(BF16))
        part_ref[...] += _dot_nt(dmc_ref[...].astype(BF16), w_ref[...].astype(BF16))

    return pl.pallas_call(
        body, name="ada_bwd", grid=(cs // ADA_TILE,),
        in_specs=[_acc((D, 16)), pl.BlockSpec((16, ADA_TILE), lambda j: (0, j)), pl.BlockSpec((8, ADA_TILE), lambda j: (0, j)),
                  pl.BlockSpec((D, ADA_TILE), lambda j: (0, j))],
        out_specs=[pl.BlockSpec((D, ADA_TILE), lambda j: (0, j)), _acc((8, D))],
        out_shape=[jax.ShapeDtypeStruct((D, cs), F32), jax.ShapeDtypeStruct((8, D), F32)],
        compiler_params=_cp("arbitrary"),
    )(sc_all_t, dm_all, dmc, w_ada)


def _sum8(x, name, tr=256):
    _, r, c = x.shape
    tr = min(tr, r)
    while r % tr:
        tr -= 16

    def body(x_ref, o_ref):
        acc = x_ref[0].astype(F32)
        for i in range(1, N_DEV):
            acc = acc + x_ref[i].astype(F32)
        o_ref[...] = acc

    return pl.pallas_call(
        body, name=name, grid=(r // tr,),
        in_specs=[pl.BlockSpec((N_DEV, tr, c), lambda i: (0, i, 0))],
        out_specs=pl.BlockSpec((tr, c), lambda i: (i, 0)),
        out_shape=jax.ShapeDtypeStruct((r, c), F32),
        compiler_params=_cp("arbitrary"),
    )(x)


def _sum_blocks(recv, src, me, name, tr=256):
    _, r, c = recv.shape
    tr = min(tr, r)
    while r % tr:
        tr -= 16

    def body(me_ref, recv_ref, own_ref, o_ref):
        acc = own_ref[...].astype(F32)
        for k in range(1, N_DEV):
            acc = acc + recv_ref[me_ref[0] ^ k].astype(F32)
        o_ref[...] = acc

    return pl.pallas_call(
        body, name=name,
        grid_spec=pltpu.PrefetchScalarGridSpec(
            num_scalar_prefetch=1, grid=(r // tr,),
            in_specs=[pl.BlockSpec((N_DEV, tr, c), lambda i, me_ref: (0, i, 0)),
                      pl.BlockSpec((None, tr, c), lambda i, me_ref: (me_ref[0], i, 0))],
            out_specs=pl.BlockSpec((tr, c), lambda i, me_ref: (i, 0))),
        out_shape=jax.ShapeDtypeStruct((r, c), F32),
        compiler_params=_cp("arbitrary"),
    )(me, recv, src)


def _sum8_many(xs, name):
    n = len(xs)

    def body(*refs):
        for x_ref, o_ref in zip(refs[:n], refs[n:]):
            acc = x_ref[0]
            for i in range(1, N_DEV):
                acc = acc + x_ref[i]
            o_ref[...] = acc

    vmem = pl.BlockSpec(memory_space=pltpu.VMEM)
    return pl.pallas_call(
        body, name=name, in_specs=[vmem] * n, out_specs=[vmem] * n,
        out_shape=[jax.ShapeDtypeStruct(v.shape[1:], v.dtype) for v in xs],
        compiler_params=pltpu.CompilerParams(vmem_limit_bytes=VMEM_LIMIT),
    )(*xs)


def _adam_update(w, g, m, v):
    nm = ADAM_B1 * m + (1.0 - ADAM_B1) * g
    nv = ADAM_B2 * v + (1.0 - ADAM_B2) * (g * g)
    m_hat = nm / (1.0 - ADAM_B1 ** ADAM_STEP)
    v_hat = nv / (1.0 - ADAM_B2 ** ADAM_STEP)
    return -ADAM_LR * (m_hat / (jnp.sqrt(v_hat) + ADAM_EPS) + ADAM_WD * w), nm, nv


ROW_LOSS, ROW_LN2_G, ROW_LN2_B, ROW_LN1_G, ROW_LN1_B = 0, 1, 2, 10, 11
ROWS_DMOD_X = (16, 17, 12, 9, 8, 3)
ROWS_DMOD_C = (24, 25)
SMALL = ("c_ctx", "b_ada", "attn_sink", "gmlp_ln_g", "gmlp_ln_b", "w_spatial", "b_spatial", "ln1_g", "ln1_b", "ln2_g", "ln2_b")


def _adamw_small(sums, dsc, w, m, v):
    n = len(SMALL)

    def body(*refs):
        st_ref, gm_ref, sk_ref, ws_ref, bs_ref, dsc_ref = refs[:6]
        w_refs = dict(zip(SMALL, refs[6:6 + n]))
        m_refs = dict(zip(SMALL, refs[6 + n:6 + 2 * n]))
        v_refs = dict(zip(SMALL, refs[6 + 2 * n:6 + 3 * n]))
        outs = refs[6 + 3 * n:]
        c = w_refs["c_ctx"][...]
        sg = _sig(c)
        dmod = [st_ref[r:r + 1, :] for r in ROWS_DMOD_X]
        dmod[0] = dmod[0] + st_ref[ROWS_DMOD_C[0]:ROWS_DMOD_C[0] + 1, :]
        dmod[1] = dmod[1] + st_ref[ROWS_DMOD_C[1]:ROWS_DMOD_C[1] + 1, :]
        grads = dict(
            c_ctx=dsc_ref[0:1, :] * (sg * (1.0 + c * (1.0 - sg))),
            b_ada=jnp.concatenate(dmod, axis=1),
            attn_sink=sk_ref[0:1, 0:N_KV * GROUP],
            gmlp_ln_g=gm_ref[0:1, :], gmlp_ln_b=gm_ref[1:2, :],
            w_spatial=ws_ref[...], b_spatial=bs_ref[...],
            ln1_g=st_ref[ROW_LN1_G:ROW_LN1_G + 1, :], ln1_b=st_ref[ROW_LN1_B:ROW_LN1_B + 1, :],
            ln2_g=st_ref[ROW_LN2_G:ROW_LN2_G + 1, :], ln2_b=st_ref[ROW_LN2_B:ROW_LN2_B + 1, :])
        for i, name in enumerate(SMALL):
            g = grads[name]
            d, nm, nv = _adam_update(w_refs[name][...], g, m_refs[name][...], v_refs[name][...])
            outs[i][...] = g
            outs[n + i][...] = d
            outs[2 * n + i][...] = nm
            outs[3 * n + i][...] = nv

    vmem = pl.BlockSpec(memory_space=pltpu.VMEM)
    args = list(sums) + [dsc] + [w[k] for k in SMALL] + [m[k] for k in SMALL] + [v[k] for k in SMALL]
    shapes = [jax.ShapeDtypeStruct(w[k].shape, F32) for k in SMALL]
    out = pl.pallas_call(
        body, name="adamw_small", in_specs=[vmem] * len(args), out_specs=[vmem] * (4 * n), out_shape=shapes * 4,
        compiler_params=pltpu.CompilerParams(vmem_limit_bytes=VMEM_LIMIT),
    )(*args)
    return [dict(zip(SMALL, out[i * n:(i + 1) * n])) for i in range(4)]


def _adamw_halves(w, mine, theirs, m, v, c_arr, name):
    r, c = w.shape
    tr = min(256, r // 2)
    while (r // 2) % tr:
        tr -= 8
    nt = (r // 2) // tr

    def body(c_ref, w_ref, mine_ref, theirs_ref, m_ref, v_ref, g_ref, d_ref, nm_ref, nv_ref):
        g = jnp.where(pl.program_id(0) == c_ref[0], mine_ref[...], theirs_ref[...])
        g_ref[...] = g
        d_ref[...], nm_ref[...], nv_ref[...] = _adam_update(w_ref[...], g, m_ref[...], v_ref[...])

    whole = pl.BlockSpec((tr, c), lambda hb, i, c_ref: (hb * nt + i, 0))
    half = pl.BlockSpec((tr, c), lambda hb, i, c_ref: (i, 0))
    shp = jax.ShapeDtypeStruct((r, c), F32)
    return pl.pallas_call(
        body, name=name,
        grid_spec=pltpu.PrefetchScalarGridSpec(
            num_scalar_prefetch=1, grid=(2, nt), in_specs=[whole, half, half, whole, whole], out_specs=[whole] * 4),
        out_shape=[shp] * 4,
        compiler_params=_cp("arbitrary", "arbitrary"),
    )(c_arr, w, mine, theirs, m, v)


def _adamw(w, g, m, v, name):
    r, c = w.shape
    tr = r if r * c <= 256 * 1024 else min(256, r)
    while r % tr:
        tr -= 8

    def body(w_ref, g_ref, m_ref, v_ref, d_ref, nm_ref, nv_ref):
        d_ref[...], nm_ref[...], nv_ref[...] = _adam_update(w_ref[...], g_ref[...], m_ref[...], v_ref[...])

    spec = pl.BlockSpec((tr, c), lambda i: (i, 0))
    shp = jax.ShapeDtypeStruct((r, c), F32)
    return pl.pallas_call(
        body, name=name, grid=(r // tr,), in_specs=[spec] * 4, out_specs=[spec] * 3, out_shape=[shp] * 3,
        compiler_params=_cp("arbitrary"),
    )(w, g, m, v)


def _my_pos():
    return lax.axis_index("x"), lax.axis_index("y"), lax.axis_index("c")


N_COPY = 7


class _Gather:
    def __init__(self, x_refs, out_refs, send_sems, recv_sems):
        self.x_refs, self.out_refs = x_refs, out_refs
        self.send_sems, self.recv_sems = send_sems, recv_sems
        x, y, c = _my_pos()
        self.c = c
        self.me, self.sibling = (x, y, c), (x, y, 1 - c)
        self.chips = [(1 - x, y), (x, 1 - y), (1 - x, 1 - y)]

    def _copy(self, a, k, block, to, from_input=False):
        px, py, pc = block
        rows = self.out_refs[a].at[4 * px + 2 * py + pc]
        return pltpu.make_async_remote_copy(
            src_ref=self.x_refs[a] if from_input else rows, dst_ref=rows,
            send_sem=self.send_sems.at[a * N_COPY + k], recv_sem=self.recv_sems.at[a * N_COPY + k],
            device_id=to, device_id_type=MESH)

    def start(self):
        n = len(self.x_refs)
        for a in range(n):
            self._copy(a, 0, self.me, self.sibling, from_input=True).start()
        for j, chip in enumerate(self.chips):
            for a in range(n):
                self._copy(a, 1 + j, self.me, (*chip, self.c), from_input=True).start()

    def finish(self):
        n = len(self.x_refs)
        c = self.c
        for j, chip in enumerate(self.chips):
            for a in range(n):
                self._copy(a, 1 + j, (*chip, c), self.me).wait_recv()
                self._copy(a, 4 + j, (*chip, c), self.sibling).start()
        for a in range(n):
            self._copy(a, 0, self.sibling, self.me).wait_recv()
        for j, chip in enumerate(self.chips):
            for a in range(n):
                self._copy(a, 4 + j, (*chip, 1 - c), self.me).wait_recv()
        for a in range(n):
            self._copy(a, 0, self.me, self.sibling, from_input=True).wait_send()
            for j, chip in enumerate(self.chips):
                self._copy(a, 1 + j, self.me, (*chip, c), from_input=True).wait_send()
                self._copy(a, 4 + j, (*chip, c), self.sibling).wait_send()


def _comm_scratch(n):
    return [pltpu.SemaphoreType.DMA((n * N_COPY,)), pltpu.SemaphoreType.DMA((n * N_COPY,))]


def _comm_specs(n):
    return [pl.BlockSpec(memory_space=pl.ANY)] * n


def _gathered_shapes(xs):
    return [jax.ShapeDtypeStruct((N_DEV,) + v.shape, v.dtype) for v in xs]


def _with_own(gathered, xs, me):
    return [lax.dynamic_update_index_in_dim(g, v, me, 0) for g, v in zip(gathered, xs)]


def _all_gather(xs, me, name):
    n = len(xs)

    def body(*refs):
        g = _Gather(refs[:n], refs[n:2 * n], *refs[2 * n:])
        g.start()
        g.finish()

    out = pl.pallas_call(
        body, name=name, out_shape=_gathered_shapes(xs), in_specs=_comm_specs(n), out_specs=_comm_specs(n),
        scratch_shapes=_comm_scratch(n),
    )(*xs)
    return _with_own(out, xs, me)


class _AllToAll:
    def __init__(self, x_refs, out_refs, send_sems, recv_sems):
        self.x_refs, self.out_refs = x_refs, out_refs
        self.send_sems, self.recv_sems = send_sems, recv_sems
        self.pos = _my_pos()
        x, y, c = self.pos
        self.me = 4 * x + 2 * y + c

    def _peer(self, k):
        x, y, c = self.pos
        return (x ^ ((k >> 2) & 1), y ^ ((k >> 1) & 1), c ^ (k & 1))

    def _copy(self, a, k):
        p = self._peer(k)
        return pltpu.make_async_remote_copy(
            src_ref=self.x_refs[a].at[4 * p[0] + 2 * p[1] + p[2]], dst_ref=self.out_refs[a].at[self.me],
            send_sem=self.send_sems.at[a * N_COPY + k - 1], recv_sem=self.recv_sems.at[a * N_COPY + k - 1],
            device_id=p, device_id_type=MESH)

    def start(self):
        for k in range(1, N_DEV):
            for a in range(len(self.x_refs)):
                self._copy(a, k).start()

    def finish(self):
        for a in range(len(self.x_refs)):
            for k in range(1, N_DEV):
                self._copy(a, k).wait_recv()
            for k in range(1, N_DEV):
                self._copy(a, k).wait_send()


def _all_to_all(blocks, name):
    n = len(blocks)

    def body(*refs):
        t = _AllToAll(refs[:n], refs[n:2 * n], *refs[2 * n:])
        t.start()
        t.finish()

    return pl.pallas_call(
        body, name=name, out_shape=[jax.ShapeDtypeStruct(v.shape, v.dtype) for v in blocks],
        in_specs=_comm_specs(n), out_specs=_comm_specs(n), scratch_shapes=_comm_scratch(n),
    )(*blocks)


def _sibling_exchange(xs, name):
    n = len(xs)

    def body(*refs):
        x_refs, out_refs = refs[:n], refs[n:2 * n]
        send_sems, recv_sems = refs[2 * n:]
        x, y, c = _my_pos()

        def push(a):
            return pltpu.make_async_remote_copy(
                src_ref=x_refs[a], dst_ref=out_refs[a], send_sem=send_sems.at[a], recv_sem=recv_sems.at[a],
                device_id=(x, y, 1 - c), device_id_type=MESH)

        for a in range(n):
            push(a).start()
        for a in range(n):
            push(a).wait_recv()
            push(a).wait_send()

    return pl.pallas_call(
        body, name=name, out_shape=[jax.ShapeDtypeStruct(v.shape, v.dtype) for v in xs],
        in_specs=_comm_specs(n), out_specs=_comm_specs(n),
        scratch_shapes=[pltpu.SemaphoreType.DMA((n,)), pltpu.SemaphoreType.DMA((n,))],
    )(*xs)


def _scatter_and_gather(scatter, gather, name):
    ns, ng = len(scatter), len(gather)

    def body(*refs):
        s_in, g_in = refs[:ns], refs[ns:ns + ng]
        s_out, g_out = refs[ns + ng:2 * ns + ng], refs[2 * ns + ng:2 * (ns + ng)]
        s_send, s_recv, g_send, g_recv = refs[2 * (ns + ng):]
        g = _Gather(g_in, g_out, g_send, g_recv)
        t = _AllToAll(s_in, s_out, s_send, s_recv)
        g.start()
        t.start()
        g.finish()
        t.finish()

    out = pl.pallas_call(
        body, name=name,
        out_shape=[jax.ShapeDtypeStruct(v.shape, v.dtype) for v in scatter] + _gathered_shapes(gather),
        in_specs=_comm_specs(ns + ng), out_specs=_comm_specs(ns + ng),
        scratch_shapes=_comm_scratch(ns) + _comm_scratch(ng),
    )(*scatter, *gather)
    return out[:ns], out[ns:]


def _row_tile(seq, want):
    return min(want, seq)


def _local_step(x, ctx, tgt, mod_x, mod_c, wb, sink, gmlp_g, gmlp_b, w_s, b_s, ln1_g, ln1_b, ln2_g, ln2_b,
                later=None, me=None):
    seq = x.shape[0]
    on_mesh = me is not None
    modx1 = jnp.concatenate([mod_x[0:2], jnp.zeros((6, D), F32)], axis=0)
    modc = jnp.concatenate([mod_c[0:2], jnp.zeros((6, D), F32)], axis=0)
    vec = jnp.concatenate([mod_x[2:3], ln1_g, ln1_b, mod_x[3:6], ln2_g, ln2_b], axis=0)
    gp = jnp.concatenate([gmlp_g, gmlp_b, jnp.zeros((6, G_W), F32)], axis=0)
    ws_stack = w_s.reshape(N_GRP * BLK, BLK).astype(BF16)
    ws_stack_t = jnp.transpose(w_s, (2, 0, 1)).reshape(BLK, N_GRP * BLK).astype(BF16)
    bias_full = jnp.repeat(b_s.T, GRP_D, axis=1)
    cos, sin = _rope_tables(seq)
    w_in = wb["w_in"]
    w_kv = w_in[Q_W:Q_W + 2 * KV_W, :]
    tm_big = _row_tile(seq, 512)
    tm_ffn = _row_tile(seq, 256)

    hc, kvc, vac = _ctx_fwd(ctx, modc, w_kv)
    behind_proj = ("w_a", "w_b", "w_o") if on_mesh else ()
    behind_attn = ("w_fi", "w_fo") if on_mesh else ()
    (h, q, kv, va, uv, gab), got_proj = _proj_fwd(x, modx1, w_in, cos, sin, tm_big, gather=[later[n] for n in behind_proj])
    (ya, lse), got_attn = _attn_fwd(q, kv, va, kvc, vac, sink, gather=[later[n] for n in behind_attn])
    if on_mesh:
        wb = dict(wb)
        names = behind_proj + behind_attn
        for n, g in zip(names, _with_own(list(got_proj) + list(got_attn), [later[n] for n in names], me)):
            wb[n] = g.reshape(-1, g.shape[2]) if n in ROW_SHARDED else g.reshape(N_SHARD, 2 * g.shape[1], g.shape[2])
    a, b, mix, merged, yb = _mix_fwd(uv, gab, ya, gp, ws_stack, bias_full, wb["w_a"], wb["w_b"], wb["w_o"], tm_big)
    act, h2, dff, df, dr1, st_ffn = _ffn(x, mix, tgt, vec, wb["w_fi"], wb["w_fo"], tm_ffn)
    blocks, recv = {}, {}
    blocks["w_fo"] = _eighths(_tn_matmul(act, df, 512, "tn_w_ffn_out", BF16, tk=2048))
    if on_mesh:
        g_w_fi, (recv["w_fo"],) = _tn_matmul(h2, dff, FH_SHARD, "tn_w_ffn_in", BF16, shard_major=True, tk=2048,
                                            scatter=[blocks["w_fo"]])
    else:
        g_w_fi = _tn_matmul(h2, dff, FH_SHARD, "tn_w_ffn_in", BF16, shard_major=True, tk=2048)
    blocks["w_fi"] = _eighths(g_w_fi)
    (dya, dpb, dws, dbs_full, st4, g_w_o, g_w_a, g_w_b), got = _mix_bwd(
        dr1, a, b, gab, uv, merged, ya, yb, vec, gp, ws_stack, ws_stack_t, bias_full, wb["w_a"], wb["w_b"], wb["w_o"],
        tm_big, scatter=[blocks["w_fi"]] if on_mesh else ())
    recv.update(zip(("w_fi",), got))
    blocks.update(w_o=_eighths(g_w_o), w_a=_eighths(g_w_a), w_b=_eighths(g_w_b))
    mixer = ("w_o", "w_a", "w_b") if on_mesh else ()
    (dq, dkv, dkvc, dsink), got = _attn_bwd(q, kv, kvc, sink, dya, ya, lse, scatter=[blocks[n] for n in mixer])
    recv.update(zip(mixer, got))
    g_wkv_ctx, st0 = _ctx_bwd(dkvc, ctx, hc, w_kv)
    (dqkv, grad_x, st1), _ = _proj_bwd(dq, dkv, dpb, x, dr1, modx1, w_in, cos, sin, tm_big)
    dbs = jnp.sum(dbs_full.reshape(BLK, N_GRP, GRP_D), axis=2).T
    early = [jnp.concatenate([st_ffn, st0], axis=0), st4, dsink, dws, dbs]
    init = jnp.pad(g_wkv_ctx, ((Q_W, 0), (0, 0)))
    g_qkv = _tn_matmul(dqkv, h, D, "tn_w_in_qkv", BF16, init=init, tk=1024)
    if on_mesh:
        g_rest, early_gathered = _tn_matmul(dpb, h, D, "tn_w_in_rest", BF16, tk=1024, gather=early)
    else:
        g_rest, early_gathered = _tn_matmul(dpb, h, D, "tn_w_in_rest", BF16, tk=1024), None
    blocks["w_in"] = _eighths(jnp.concatenate([g_qkv, g_rest], axis=0))
    return grad_x, dict(early=early, early_gathered=early_gathered, late=st1), blocks, recv


BIG = ("w_in", "w_a", "w_b", "w_o", "w_fi", "w_fo")
ROW_SHARDED = ("w_o", "w_fo")


def _half_of_shard(shard, c):
    r = shard.shape[0]
    return lax.dynamic_slice_in_dim(shard, c * (r // 2), r // 2, axis=0)


def _eighths(v):
    rows = v.shape[-2] * (v.shape[0] if v.ndim == 3 else 1)
    return v.reshape(N_DEV, rows // N_DEV, v.shape[-1])


def kernel(x, c, ctx, c_ctx, w_ada, b_ada, w_in, attn_sink, gmlp_ln_g, gmlp_ln_b, w_spatial, b_spatial, w_branch_a, w_branch_b, w_out, ln1_g, ln1_b, w_ffn_in, w_ffn_out, ln2_g, ln2_b, loss_target, m_c_ctx, m_w_ada, m_b_ada, m_w_in, m_attn_sink, m_gmlp_ln_g, m_gmlp_ln_b, m_w_spatial, m_b_spatial, m_w_branch_a, m_w_branch_b, m_w_out, m_ln1_g, m_ln1_b, m_w_ffn_in, m_w_ffn_out, m_ln2_g, m_ln2_b, v_c_ctx, v_w_ada, v_b_ada, v_w_in, v_attn_sink, v_gmlp_ln_g, v_gmlp_ln_b, v_w_spatial, v_b_spatial, v_w_branch_a, v_w_branch_b, v_w_out, v_ln1_g, v_ln1_b, v_w_ffn_in, v_w_ffn_out, v_ln2_g, v_ln2_b):
    mx, my, mc = _my_pos()
    me = 4 * mx + 2 * my + mc
    chip = 2 * mx + my
    shards = dict(w_in=w_in[0].T, w_a=w_branch_a[0], w_b=w_branch_b[0], w_o=w_out[0], w_fi=w_ffn_in[0], w_fo=w_ffn_out[0])

    halves = {n: _half_of_shard(shards[n], mc).astype(BF16) for n in BIG}
    c_rows = jnp.concatenate([c, jnp.zeros((7, D), F32)], axis=0)
    g_in, c_g = _all_gather([halves["w_in"], c_rows], me, "gather_w_in")
    wb = dict(w_in=g_in.reshape(IN_W, D))

    c_all = c_g[:, 0, :]
    cc = jnp.concatenate([c_all, c_ctx[None, :], jnp.zeros((7, D), F32)], axis=0)
    sig_cc = jax.nn.sigmoid(cc)
    sc_all = cc * sig_cc
    mod_shard = _ada_fwd(sc_all, w_ada[0])
    mod_g = _all_gather([mod_shard], me, "gather_mod")[0]
    mod_all = jnp.concatenate([mod_g[2 * s] for s in range(4)], axis=1) + b_ada
    mod_x = lax.dynamic_slice_in_dim(mod_all, me, 1, axis=0).reshape(6, D)
    mod_c = mod_all[8].reshape(6, D)[0:2]

    grad_x, small, blocks, recv = _local_step(
        x[0], ctx[0], loss_target[0], mod_x, mod_c, wb, attn_sink, gmlp_ln_g, gmlp_ln_b, w_spatial[0], b_spatial[0],
        ln1_g, ln1_b, ln2_g, ln2_b, later=halves, me=me)

    (recv["w_in"],), late = _scatter_and_gather([blocks["w_in"]], [small["late"]], "scatter_w_in_gather_small")
    late = _with_own(late, [small["late"]], me)[0]
    gathered = _with_own(small["early_gathered"], small["early"], me)
    gathered[0] = jnp.concatenate([gathered[0][:, :16], late, gathered[0][:, 16:]], axis=1)

    me_arr = jnp.reshape(me, (1,)).astype(jnp.int32)
    summed = {n: _sum_blocks(recv[n], blocks[n], me_arr, "sum_grads_" + n) for n in BIG}
    theirs = dict(zip(BIG, _sibling_exchange([summed[n] for n in BIG], "exchange_grads")))

    sums = _sum8_many(gathered, "sum_small")
    stats = sums[0]
    loss = 0.5 * jnp.sum(stats[ROW_LOSS]) / D
    dmod_x_all = jnp.concatenate([gathered[0][:, r_, :] for r_ in ROWS_DMOD_X], axis=1)
    dmod_c_full = jnp.concatenate([stats[r_] for r_ in ROWS_DMOD_C] + [jnp.zeros((4 * D,), F32)])
    dm_rows = jnp.concatenate([dmod_x_all, dmod_c_full[None, :], jnp.zeros((7, 6 * D), F32)], axis=0)
    cs = w_ada.shape[2]
    dm_shard = lax.dynamic_slice_in_dim(dm_rows, chip * cs, cs, axis=1)
    dmc_shard = jnp.concatenate([dm_shard[8:9], jnp.zeros((7, cs), F32)], axis=0)
    g_w_ada, part = _ada_bwd(sc_all.T, dm_shard, dmc_shard, w_ada[0])
    part_all = _all_gather([part * (mc == 0).astype(F32)], me, "gather_c_ctx")[0]
    dsc = _sum8(part_all, "sum_c_ctx")

    grads = dict(w_ada=g_w_ada[None])
    weights = dict(c_ctx=c_ctx, w_ada=w_ada, b_ada=b_ada, w_in=w_in, attn_sink=attn_sink, gmlp_ln_g=gmlp_ln_g,
                   gmlp_ln_b=gmlp_ln_b, w_spatial=w_spatial, b_spatial=b_spatial, w_branch_a=w_branch_a,
                   w_branch_b=w_branch_b, w_out=w_out, ln1_g=ln1_g, ln1_b=ln1_b, w_ffn_in=w_ffn_in, w_ffn_out=w_ffn_out,
                   ln2_g=ln2_g, ln2_b=ln2_b)
    ms = dict(c_ctx=m_c_ctx, w_ada=m_w_ada, b_ada=m_b_ada, w_in=m_w_in, attn_sink=m_attn_sink, gmlp_ln_g=m_gmlp_ln_g,
              gmlp_ln_b=m_gmlp_ln_b, w_spatial=m_w_spatial, b_spatial=m_b_spatial, w_branch_a=m_w_branch_a,
              w_branch_b=m_w_branch_b, w_out=m_w_out, ln1_g=m_ln1_g, ln1_b=m_ln1_b, w_ffn_in=m_w_ffn_in,
              w_ffn_out=m_w_ffn_out, ln2_g=m_ln2_g, ln2_b=m_ln2_b)
    vs = dict(c_ctx=v_c_ctx, w_ada=v_w_ada, b_ada=v_b_ada, w_in=v_w_in, attn_sink=v_attn_sink, gmlp_ln_g=v_gmlp_ln_g,
              gmlp_ln_b=v_gmlp_ln_b, w_spatial=v_w_spatial, b_spatial=v_b_spatial, w_branch_a=v_w_branch_a,
              w_branch_b=v_w_branch_b, w_out=v_w_out, ln1_g=v_ln1_g, ln1_b=v_ln1_b, w_ffn_in=v_w_ffn_in,
              w_ffn_out=v_w_ffn_out, ln2_g=v_ln2_g, ln2_b=v_ln2_b)
    order = list(weights)
    delta, new_m, new_v = {}, {}, {}
    d_, m_, v_ = _adamw(w_ada[0], g_w_ada, m_w_ada[0], v_w_ada[0], "adamw_w_ada")
    delta["w_ada"], new_m["w_ada"], new_v["w_ada"] = d_[None], m_[None], v_[None]
    c_arr = jnp.reshape(mc, (1,)).astype(jnp.int32)
    names = dict(w_in="w_in", w_a="w_branch_a", w_b="w_branch_b", w_o="w_out", w_fi="w_ffn_in", w_fo="w_ffn_out")
    for k, n in names.items():
        flip = (lambda t: t.T) if k == "w_in" else (lambda t: t)
        outs = _adamw_halves(flip(weights[n][0]), summed[k], theirs[k], flip(ms[n][0]), flip(vs[n][0]), c_arr, "adamw_" + n)
        grads[n], delta[n], new_m[n], new_v[n] = [flip(t)[None] for t in outs]

    def view(a):
        return a.reshape(-1, a.shape[-1]) if a.ndim != 1 else a.reshape(1, -1)

    small = _adamw_small(sums, dsc, *[{n: view(d[n]) for n in SMALL} for d in (weights, ms, vs)])
    for out, src in zip((grads, delta, new_m, new_v), small):
        for n in SMALL:
            out[n] = src[n].reshape(weights[n].shape)

    return (loss, grad_x[None], *[grads[n] for n in order], *[delta[n] for n in order],
            *[new_m[n] for n in order], *[new_v[n] for n in order])
```

```python
import functools
import math

import jax
import jax.numpy as jnp
from jax import lax
from jax.experimental import pallas as pl
from jax.experimental.pallas import tpu as pltpu

F32 = jnp.float32
BF16 = jnp.bfloat16

D = 1024
HEAD = 64
N_KV = 2
GROUP = 4
Q_W = 512
KV_W = 128
G_W = 512
BLK = 128
N_GRP = 8
GRP_D = 64
FH = 2816
IN_W = 3840
GRID_W = 64
ROPE_BASE = 10000.0
LN_EPS = 1e-5
NEG = -1e30
ALPHA = (2 * 1) ** 0.25
SCALE = HEAD ** -0.5
GELU_K = math.sqrt(2.0 / math.pi)
GELU_A = 0.044715
ADAM_LR = 0.001
ADAM_B1 = 0.9
ADAM_B2 = 0.999
ADAM_EPS = 1e-08
ADAM_WD = 0.01
ADAM_STEP = 10
N_DEV = 8
N_SHARD = 4
FH_SHARD = FH // 2
LANES = 128
VMEM_LIMIT = 56 * 1024 * 1024
MESH = pl.DeviceIdType.MESH


def _cp(*sem):
    return pltpu.CompilerParams(dimension_semantics=sem, vmem_limit_bytes=VMEM_LIMIT)


def _resident(shape):
    return pl.BlockSpec(shape, lambda *_: (0,) * len(shape), pipeline_mode=pl.Buffered(1))


def _rows(tm, width):
    return pl.BlockSpec((tm, width), lambda i: (i, 0))


def _acc(shape):
    return pl.BlockSpec(shape, lambda *_: (0,) * len(shape))


def _dot(a, b):
    return jnp.dot(a, b, preferred_element_type=F32)


def _dot_nt(a, b):
    return lax.dot_general(a, b, (((1,), (1,)), ((), ())), preferred_element_type=F32)


def _dot_tn(a, b):
    return lax.dot_general(a, b, (((0,), (0,)), ((), ())), preferred_element_type=F32)


def _ln(x):
    mu = jnp.mean(x, axis=-1, keepdims=True)
    xc = x - mu
    var = jnp.mean(xc * xc, axis=-1, keepdims=True)
    rstd = lax.rsqrt(var + LN_EPS)
    return xc * rstd, rstd


def _ln_bwd(dxhat, xhat, rstd):
    return (dxhat - jnp.mean(dxhat, axis=-1, keepdims=True)
            - xhat * jnp.mean(dxhat * xhat, axis=-1, keepdims=True)) * rstd


def _sig(x):
    return 1.0 / (1.0 + jnp.exp(-x))


def _gelu(x):
    t = jnp.tanh(GELU_K * (x + GELU_A * x * x * x))
    return 0.5 * x * (1.0 + t), t


def _gelu_grad(x, t):
    return 0.5 * (1.0 + t) + 0.5 * x * (1.0 - t * t) * GELU_K * (1.0 + 3.0 * GELU_A * x * x)


def _colsum(v):
    return jnp.sum(v, axis=0, keepdims=True)


def _partner(x):
    w = x.shape[1]
    lane = lax.broadcasted_iota(jnp.int32, x.shape, 1)
    return jnp.where((lane & 31) < 16, pltpu.roll(x, w - 16, 1), pltpu.roll(x, 16, 1))


def _rope(x, cos, sin):
    return x * cos + _partner(x) * sin


def _unrope(g, cos, sin):
    return g * cos + _partner(g * sin)


def _rope_tables(seq):
    inv = ROPE_BASE ** (-jnp.arange(HEAD // 4, dtype=F32) / (HEAD // 4))
    pos = jnp.arange(seq, dtype=jnp.int32)
    ar = (pos // GRID_W).astype(F32)[:, None] * inv
    ac = (pos % GRID_W).astype(F32)[:, None] * inv
    cos = jnp.concatenate([jnp.cos(ar), jnp.cos(ar), jnp.cos(ac), jnp.cos(ac)], axis=-1)
    sin = jnp.concatenate([-jnp.sin(ar), jnp.sin(ar), -jnp.sin(ac), jnp.sin(ac)], axis=-1)
    return jnp.tile(cos, (1, LANES // HEAD)), jnp.tile(sin, (1, LANES // HEAD))


def _ctx_fwd(ctx, modc, w_kv):
    n_ctx = ctx.shape[0]

    def body(ctx_ref, mod_ref, w_ref, hc_ref, kvc_ref, vac_ref):
        xhat, _ = _ln(ctx_ref[...])
        hc = (xhat * (1.0 + mod_ref[1:2, :]) + mod_ref[0:1, :]).astype(BF16)
        hc_ref[...] = hc
        kvc = _dot_nt(hc, w_ref[...]).astype(BF16)
        kvc_ref[...] = kvc
        vac_ref[...] = _with_ones(kvc[:, KV_W:])

    return pl.pallas_call(
        body, name="ctx_fwd", grid=(1,),
        in_specs=[_acc((n_ctx, D)), _acc((8, D)), _acc((2 * KV_W, D))],
        out_specs=[_acc((n_ctx, D)), _acc((n_ctx, 2 * KV_W)), _acc((n_ctx, 2 * LANES))],
        out_shape=[jax.ShapeDtypeStruct((n_ctx, D), BF16), jax.ShapeDtypeStruct((n_ctx, 2 * KV_W), BF16),
                   jax.ShapeDtypeStruct((n_ctx, 2 * LANES), BF16)],
        compiler_params=_cp("arbitrary"),
    )(ctx, modc, w_kv)


def _host_start(step, comm):
    if comm is not None:
        @pl.when(step == 0)
        def _():
            comm.start()


def _host_finish(step, last, comm):
    if comm is not None:
        @pl.when(step == last)
        def _():
            comm.finish()


def _proj_fwd(x, modx, w_in, cos, sin, tm, gather=()):
    seq = x.shape[0]
    ng = len(gather)

    def body(x_ref, mod_ref, w_ref, cos_ref, sin_ref, *rest):
        h_ref, q_ref, kv_ref, va_ref, uv_ref, gab_ref = rest[ng:ng + 6]
        comm = _Gather(rest[:ng], rest[ng + 6:2 * ng + 6], *rest[2 * ng + 6:]) if ng else None
        _host_start(pl.program_id(0), comm)
        xhat, _ = _ln(x_ref[...])
        h = (xhat * (1.0 + mod_ref[1:2, :]) + mod_ref[0:1, :]).astype(BF16)
        h_ref[...] = h
        cos1, sin1 = cos_ref[...], sin_ref[...]
        cos2 = jnp.concatenate([cos1, cos1], axis=1)
        sin2 = jnp.concatenate([sin1, sin1], axis=1)
        for j in range(Q_W // 256):
            t = _dot_nt(h, w_ref[256 * j:256 * (j + 1), :])
            q_ref[:, 256 * j:256 * (j + 1)] = (_rope(t, cos2, sin2) * SCALE).astype(BF16)
        t = _dot_nt(h, w_ref[Q_W:Q_W + 2 * KV_W, :])
        kv_ref[:, :KV_W] = _rope(t[:, :KV_W], cos1, sin1).astype(BF16)
        v = t[:, KV_W:].astype(BF16)
        kv_ref[:, KV_W:] = v
        va_ref[...] = _with_ones(v)
        o = Q_W + 2 * KV_W
        for j in range(2):
            uv_ref[:, G_W * j:G_W * (j + 1)] = _dot_nt(h, w_ref[o + G_W * j:o + G_W * (j + 1), :]).astype(BF16)
        o += 2 * G_W
        for j in range(4):
            gab_ref[:, 512 * j:512 * (j + 1)] = _dot_nt(h, w_ref[o + 512 * j:o + 512 * (j + 1), :]).astype(BF16)
        _host_finish(pl.program_id(0), seq // tm - 1, comm)

    out = pl.pallas_call(
        body, name="proj_fwd", grid=(seq // tm,),
        in_specs=[_rows(tm, D), _acc((8, D)), _resident((IN_W, D)), _rows(tm, LANES), _rows(tm, LANES)] + _comm_specs(ng),
        out_specs=[_rows(tm, D), _rows(tm, Q_W), _rows(tm, 2 * KV_W), _rows(tm, 2 * LANES), _rows(tm, 2 * G_W),
                   _rows(tm, 2 * D)] + _comm_specs(ng),
        out_shape=[jax.ShapeDtypeStruct((seq, D), BF16), jax.ShapeDtypeStruct((seq, Q_W), BF16),
                   jax.ShapeDtypeStruct((seq, 2 * KV_W), BF16), jax.ShapeDtypeStruct((seq, 2 * LANES), BF16),
                   jax.ShapeDtypeStruct((seq, 2 * G_W), BF16), jax.ShapeDtypeStruct((seq, 2 * D), BF16)] + _gathered_shapes(gather),
        scratch_shapes=_comm_scratch(ng) if ng else [],
        compiler_params=_cp("arbitrary"),
    )(x, modx, w_in, cos, sin, *gather)
    return out[:6], out[6:]


def _stack_heads(x, hk):
    return jnp.concatenate([x[:, (hk * GROUP + g) * HEAD:(hk * GROUP + g + 1) * HEAD] for g in range(GROUP)], axis=0)


def _attn_scores(q, k_refs, hk, n, nb):
    q4 = _stack_heads(q, hk)
    ks = [r[:, hk * HEAD:(hk + 1) * HEAD] for r in k_refs]
    rows = GROUP * BLK
    qi = lax.broadcasted_iota(jnp.int32, (rows, BLK), 0) & (BLK - 1)
    kj = lax.broadcasted_iota(jnp.int32, (rows, BLK), 1)
    s = [_dot_nt(q4, k) for k in ks]
    s[1] = jnp.where((kj >= qi) & (n > 0), s[1], NEG)
    s[3] = jnp.where((kj <= qi) & (n < nb - 1), s[3], NEG)
    return q4, ks, s


def _sink_rows(sink_ref, hk):
    rows = GROUP * BLK
    rg = lax.broadcasted_iota(jnp.int32, (rows, 1), 0) >> 7
    sink_v = jnp.full((rows, 1), sink_ref[0, hk * GROUP], F32)
    for g in range(1, GROUP):
        sink_v = jnp.where(rg == g, sink_ref[0, hk * GROUP + g], sink_v)
    return sink_v


def _with_ones(v):
    ones = jnp.ones((v.shape[0], HEAD), v.dtype)
    return jnp.concatenate([v[:, :HEAD], ones, v[:, HEAD:], ones], axis=1)


def _kv_specs(nb, qb):
    def spec(d):
        return pl.BlockSpec((BLK, 2 * KV_W), lambda n: (jnp.clip(qb * n + d, 0, nb - 1), 0))
    return [spec(d) for d in range(-1, qb + 1)]


def _attn_fwd(q, kv, va, kvc, vac, sink, gather=()):
    seq = q.shape[0]
    nb = seq // BLK
    n_ctx = kvc.shape[0]
    ng = len(gather)
    Q_BLOCKS = 1
    nkv = Q_BLOCKS + 2
    steps = nb // Q_BLOCKS

    def body(q_ref, *rest):
        kv_refs, va_refs = rest[:nkv], rest[nkv:2 * nkv]
        kvc_ref, vac_ref, sink_ref = rest[2 * nkv:2 * nkv + 3]
        rest = rest[2 * nkv + 3:]
        o_ref, lse_ref = rest[ng:ng + 2]
        comm = _Gather(rest[:ng], rest[ng + 2:2 * ng + 2], *rest[2 * ng + 2:]) if ng else None
        n = pl.program_id(0)
        _host_start(n, comm)
        lane = lax.broadcasted_iota(jnp.int32, (BLK, LANES), 1)
        for sub in range(Q_BLOCKS):
            rs = slice(sub * BLK, (sub + 1) * BLK)
            q = q_ref[rs, :]
            outs = []
            lse_all = jnp.zeros((BLK, LANES), F32)
            for hk in range(N_KV):
                _, _, s = _attn_scores(q, (kvc_ref,) + kv_refs[sub:sub + 3], hk, Q_BLOCKS * n + sub, nb)
                sink_v = _sink_rows(sink_ref, hk)
                m = sink_v
                for t in s:
                    m = jnp.maximum(m, jnp.max(t, axis=-1, keepdims=True))
                o = jnp.zeros((GROUP * BLK, LANES), F32)
                for t, va_ref in zip(s, (vac_ref,) + va_refs[sub:sub + 3]):
                    o = o + _dot(jnp.exp((t - m).astype(BF16)), va_ref[:, hk * LANES:(hk + 1) * LANES])
                denom = o[:, HEAD:HEAD + 1] + jnp.exp(sink_v - m)
                o4 = o[:, :HEAD] * (1.0 / denom)
                lse4 = m + jnp.log(denom)
                for g in range(GROUP):
                    outs.append(o4[g * BLK:(g + 1) * BLK, :])
                    lse_all = jnp.where(lane == hk * GROUP + g, lse4[g * BLK:(g + 1) * BLK, :], lse_all)
            o_ref[rs, :] = jnp.concatenate(outs, axis=1).astype(BF16)
            lse_ref[rs, :] = lse_all
        _host_finish(n, steps - 1, comm)

    tq = Q_BLOCKS * BLK
    out = pl.pallas_call(
        body, name="attn_fwd", grid=(steps,),
        in_specs=[_rows(tq, Q_W)] + _kv_specs(nb, Q_BLOCKS) + _kv_specs(nb, Q_BLOCKS)
        + [_acc((n_ctx, 2 * KV_W)), _acc((n_ctx, 2 * LANES)), pl.BlockSpec(memory_space=pltpu.SMEM)] + _comm_specs(ng),
        out_specs=[_rows(tq, Q_W), _rows(tq, LANES)] + _comm_specs(ng),
        out_shape=[jax.ShapeDtypeStruct((seq, Q_W), BF16), jax.ShapeDtypeStruct((seq, LANES), F32)] + _gathered_shapes(gather),
        scratch_shapes=_comm_scratch(ng) if ng else [],
        compiler_params=_cp("arbitrary"),
    )(q, *([kv] * nkv), *([va] * nkv), kvc, vac, sink, *gather)
    return out[:2], out[2:]


def _gmlp_chunk(u, vb, gp_ref, ws_ref, bias_ref):
    gu, tu = _gelu(u)
    gv, tv = _gelu(vb)
    vhat, rstd = _ln(gv)
    vn = (vhat * gp_ref[0:1, :] + gp_ref[1:2, :]).astype(BF16)
    s = bias_ref[...] + jnp.concatenate(
        [_dot(ws_ref[g * BLK:(g + 1) * BLK, :], vn[:, g * GRP_D:(g + 1) * GRP_D]) for g in range(N_GRP)], axis=1)
    return gu, tu, tv, vhat, rstd, vn, s


def _mix_fwd(uv, gab, ya, gp, ws_stack, bias_full, w_a, w_b, w_o, tm):
    seq = uv.shape[0]

    def body(uv_ref, gab_ref, ya_ref, gp_ref, ws_ref, bias_ref, wa_ref, wb_ref, wo_ref,
             a_ref, b_ref, mix_ref, merged_ref, yb_ref):
        for c in range(tm // BLK):
            rs = slice(c * BLK, (c + 1) * BLK)
            gu, _, _, _, _, _, s = _gmlp_chunk(uv_ref[rs, :G_W].astype(F32), uv_ref[rs, G_W:].astype(F32), gp_ref, ws_ref, bias_ref)
            yb_ref[rs, :] = (gu * s).astype(BF16)
        ya = ya_ref[...]
        yb = yb_ref[...]
        for s in range(N_SHARD):
            cs = slice(s * (D // N_SHARD), (s + 1) * (D // N_SHARD))
            a = _dot(ya, wa_ref[s])
            b = _dot(yb, wb_ref[s])
            a_ref[:, cs] = a.astype(BF16)
            b_ref[:, cs] = b.astype(BF16)
            ga = gab_ref[:, cs].astype(F32)
            gb = gab_ref[:, D + s * (D // N_SHARD):D + (s + 1) * (D // N_SHARD)].astype(F32)
            merged_ref[:, cs] = (_sig(ga) * a + _sig(gb) * b).astype(BF16)
        mix_ref[...] = _dot(merged_ref[...], wo_ref[...])

    return pl.pallas_call(
        body, name="mix_fwd", grid=(seq // tm,),
        in_specs=[_rows(tm, 2 * G_W), _rows(tm, 2 * D), _rows(tm, Q_W), _acc((8, G_W)),
                  _resident((N_GRP * BLK, BLK)), _acc((BLK, G_W)),
                  _resident((N_SHARD, Q_W, D // N_SHARD)), _resident((N_SHARD, G_W, D // N_SHARD)), _resident((D, D))],
        out_specs=[_rows(tm, D), _rows(tm, D), _rows(tm, D), _rows(tm, D), _rows(tm, G_W)],
        out_shape=[jax.ShapeDtypeStruct((seq, D), BF16), jax.ShapeDtypeStruct((seq, D), BF16),
                   jax.ShapeDtypeStruct((seq, D), F32), jax.ShapeDtypeStruct((seq, D), BF16),
                   jax.ShapeDtypeStruct((seq, G_W), BF16)],
        compiler_params=_cp("arbitrary"),
    )(uv, gab, ya, gp, ws_stack, bias_full, w_a, w_b, w_o)


FFN_CHUNK = 512


def _ffn_chunks():
    out = []
    for hh in range(2):
        off = 0
        while off < FH_SHARD:
            w = min(FFN_CHUNK, FH_SHARD - off)
            out.append((hh, off, w))
            off += w
    return out


def _mid_recompute(x_ref, mix_ref, vec_ref):
    r1 = ALPHA * x_ref[...] + vec_ref[0:1, :] * mix_ref[...]
    xh1, rstd1 = _ln(r1)
    xmid = xh1 * vec_ref[1:2, :] + vec_ref[2:3, :]
    xh2, rstd2 = _ln(xmid)
    return xh1, rstd1, xmid, xh2, rstd2


def _ffn(x, mix, tgt, vec, w_fi, w_fo, tm):
    seq = x.shape[0]

    def body(x_ref, mix_ref, tgt_ref, vec_ref, wi_ref, wo_ref, act_ref, h2_ref, dff_ref, df_ref, dr1_ref, st_ref, gu_ref):
        @pl.when(pl.program_id(0) == 0)
        def _():
            st_ref[...] = jnp.zeros_like(st_ref)

        xh1, rstd1, xmid, xh2, rstd2 = _mid_recompute(x_ref, mix_ref, vec_ref)
        h2 = (xh2 * (1.0 + vec_ref[4:5, :]) + vec_ref[3:4, :]).astype(BF16)
        h2_ref[...] = h2
        halves = [(slice(hh * FH_SHARD, (hh + 1) * FH_SHARD), slice(FH + hh * FH_SHARD, FH + (hh + 1) * FH_SHARD))
                  for hh in range(2)]
        for hh, (cs, cu) in enumerate(halves):
            g = _dot(h2, wi_ref[hh])
            u = _dot(h2, wi_ref[2 + hh])
            gu_ref[:, cs] = g
            gu_ref[:, cu] = u
            act_ref[:, cs] = (g * _sig(g) * u).astype(BF16)
        f = _dot(act_ref[...], wo_ref[...])
        r2 = ALPHA * xmid + vec_ref[5:6, :] * f
        yh, rstd = _ln(r2)
        y = yh * vec_ref[6:7, :] + vec_ref[7:8, :]
        err = y - tgt_ref[...]
        dy = err / D
        dr2 = _ln_bwd(dy * vec_ref[6:7, :], yh, rstd)
        st_ref[0:1, :] += _colsum(err * err)
        st_ref[1:2, :] += _colsum(dy * yh)
        st_ref[2:3, :] += _colsum(dy)
        st_ref[3:4, :] += _colsum(dr2 * f)

        df = (dr2 * vec_ref[5:6, :]).astype(BF16)
        df_ref[...] = df
        da_all = _dot_nt(df, wo_ref[...])
        for cs, cu in halves:
            da = da_all[:, cs]
            g = gu_ref[:, cs]
            u = gu_ref[:, cu]
            sg = _sig(g)
            dff_ref[:, cs] = (da * u * sg * (1.0 + g * (1.0 - sg))).astype(BF16)
            dff_ref[:, cu] = (da * g * sg).astype(BF16)
        dh2 = _dot_nt(dff_ref[:, :FH_SHARD], wi_ref[0])
        for s in range(1, N_SHARD):
            dh2 = dh2 + _dot_nt(dff_ref[:, s * FH_SHARD:(s + 1) * FH_SHARD], wi_ref[s])
        dxmid = _ln_bwd(dh2 * (1.0 + vec_ref[4:5, :]), xh2, rstd2) + ALPHA * dr2
        dr1 = _ln_bwd(dxmid * vec_ref[1:2, :], xh1, rstd1)
        dr1_ref[...] = dr1
        st_ref[8:9, :] += _colsum(dh2 * xh2)
        st_ref[9:10, :] += _colsum(dh2)
        st_ref[10:11, :] += _colsum(dxmid * xh1)
        st_ref[11:12, :] += _colsum(dxmid)
        st_ref[12:13, :] += _colsum(dr1 * mix_ref[...])

    return pl.pallas_call(
        body, name="ffn", grid=(seq // tm,),
        in_specs=[_rows(tm, D), _rows(tm, D), _rows(tm, D), _acc((8, D)), _resident((N_SHARD, D, FH_SHARD)), _resident((FH, D))],
        out_specs=[_rows(tm, FH), _rows(tm, D), _rows(tm, 2 * FH), _rows(tm, D), _rows(tm, D), _acc((16, D))],
        out_shape=[jax.ShapeDtypeStruct((seq, FH), BF16), jax.ShapeDtypeStruct((seq, D), BF16),
                   jax.ShapeDtypeStruct((seq, 2 * FH), BF16), jax.ShapeDtypeStruct((seq, D), BF16),
                   jax.ShapeDtypeStruct((seq, D), F32), jax.ShapeDtypeStruct((16, D), F32)],
        scratch_shapes=[pltpu.VMEM((tm, 2 * FH), F32)],
        compiler_params=_cp("arbitrary"),
    )(x, mix, tgt, vec, w_fi, w_fo)


def _mix_bwd(dr1, a, b, gab, uv, merged, ya, yb, vec, gp, ws_stack, ws_stack_t, bias_full, w_a, w_b, w_o, tm, scatter=()):
    seq = dr1.shape[0]
    last = seq // tm - 1
    cw = D // N_SHARD
    ns = len(scatter)

    def body(dr1_ref, a_ref, b_ref, gab_ref, uv_ref, mg_ref, ya_ref, yb_ref, vec_ref, gp_ref, ws_ref, wst_ref, bias_ref,
             wa_ref, wb_ref, wo_ref, *rest):
        dya_ref, dp_ref, dws_ref, dbs_ref, st_ref, gwo_ref, gwa_ref, gwb_ref = rest[ns:ns + 8]
        acc_o, acc_a, acc_b = rest[2 * ns + 8:2 * ns + 11]
        comm = _AllToAll(rest[:ns], rest[ns + 8:2 * ns + 8], *rest[2 * ns + 11:]) if ns else None
        _host_start(pl.program_id(0), comm)

        @pl.when(pl.program_id(0) == 0)
        def _():
            dws_ref[...] = jnp.zeros_like(dws_ref)
            dbs_ref[...] = jnp.zeros_like(dbs_ref)
            st_ref[...] = jnp.zeros_like(st_ref)
            acc_o[...] = jnp.zeros_like(acc_o)
            acc_a[...] = jnp.zeros_like(acc_a)
            acc_b[...] = jnp.zeros_like(acc_b)

        dmix = (dr1_ref[...] * vec_ref[0:1, :]).astype(BF16)
        acc_o[...] += _dot_tn(mg_ref[...], dmix)
        dmerged = _dot_nt(dmix, wo_ref[...])
        sa = _sig(gab_ref[:, :D].astype(F32))
        sb = _sig(gab_ref[:, D:].astype(F32))
        da = (dmerged * sa).astype(BF16)
        db = (dmerged * sb).astype(BF16)
        dp_ref[:, 2 * G_W:2 * G_W + D] = (dmerged * a_ref[...].astype(F32) * sa * (1.0 - sa)).astype(BF16)
        dp_ref[:, 2 * G_W + D:] = (dmerged * b_ref[...].astype(F32) * sb * (1.0 - sb)).astype(BF16)
        dya = jnp.zeros((tm, Q_W), F32)
        dyb = jnp.zeros((tm, G_W), F32)
        ya = ya_ref[...]
        yb = yb_ref[...]
        for s in range(N_SHARD):
            cs = slice(s * cw, (s + 1) * cw)
            dya = dya + _dot_nt(da[:, cs], wa_ref[s])
            dyb = dyb + _dot_nt(db[:, cs], wb_ref[s])
            acc_a[s] += _dot_tn(ya, da[:, cs])
            acc_b[s] += _dot_tn(yb, db[:, cs])
        dya_ref[...] = dya.astype(BF16)

        @pl.when(pl.program_id(0) == last)
        def _():
            gwo_ref[...] = acc_o[...].astype(BF16)
            gwa_ref[...] = acc_a[...].astype(BF16)
            gwb_ref[...] = acc_b[...].astype(BF16)

        for c in range(tm // BLK):
            rs = slice(c * BLK, (c + 1) * BLK)
            u = uv_ref[rs, :G_W].astype(F32)
            vb = uv_ref[rs, G_W:].astype(F32)
            gu, tu, tv, vhat, rstd, vn, s = _gmlp_chunk(u, vb, gp_ref, ws_ref, bias_ref)
            dyb_c = dyb[rs, :]
            ds = dyb_c * gu
            du = dyb_c * s * _gelu_grad(u, tu)
            ds_b = ds.astype(BF16)
            dvn_g = []
            for g in range(N_GRP):
                cg = slice(g * GRP_D, (g + 1) * GRP_D)
                dvn_g.append(_dot(wst_ref[:, g * BLK:(g + 1) * BLK], ds_b[:, cg]))
                dws_ref[g * BLK:(g + 1) * BLK, :] += _dot_nt(ds_b[:, cg], vn[:, cg])
            dvn = jnp.concatenate(dvn_g, axis=1)
            dbs_ref[...] += ds
            st_ref[0:1, :] += _colsum(dvn * vhat)
            st_ref[1:2, :] += _colsum(dvn)
            dgv = _ln_bwd(dvn * gp_ref[0:1, :], vhat, rstd)
            dvb = dgv * _gelu_grad(vb, tv)
            dp_ref[rs, :G_W] = du.astype(BF16)
            dp_ref[rs, G_W:2 * G_W] = dvb.astype(BF16)
        _host_finish(pl.program_id(0), last, comm)

    pw = 2 * G_W + 2 * D
    out = pl.pallas_call(
        body, name="mix_bwd", grid=(seq // tm,),
        in_specs=[_rows(tm, D), _rows(tm, D), _rows(tm, D), _rows(tm, 2 * D), _rows(tm, 2 * G_W), _rows(tm, D), _rows(tm, Q_W),
                  _rows(tm, G_W), _acc((8, D)), _acc((8, G_W)),
                  _resident((N_GRP * BLK, BLK)), _resident((BLK, N_GRP * BLK)), _acc((BLK, G_W)),
                  _resident((N_SHARD, Q_W, cw)), _resident((N_SHARD, G_W, cw)), _resident((D, D))] + _comm_specs(ns),
        out_specs=[_rows(tm, Q_W), _rows(tm, pw), _acc((N_GRP * BLK, BLK)), _acc((BLK, G_W)), _acc((8, G_W)),
                   _acc((D, D)), _acc((N_SHARD, Q_W, cw)), _acc((N_SHARD, G_W, cw))] + _comm_specs(ns),
        out_shape=[jax.ShapeDtypeStruct((seq, Q_W), BF16), jax.ShapeDtypeStruct((seq, pw), BF16),
                   jax.ShapeDtypeStruct((N_GRP * BLK, BLK), F32), jax.ShapeDtypeStruct((BLK, G_W), F32),
                   jax.ShapeDtypeStruct((8, G_W), F32), jax.ShapeDtypeStruct((D, D), BF16),
                   jax.ShapeDtypeStruct((N_SHARD, Q_W, cw), BF16), jax.ShapeDtypeStruct((N_SHARD, G_W, cw), BF16)]
        + [jax.ShapeDtypeStruct(v.shape, v.dtype) for v in scatter],
        scratch_shapes=[pltpu.VMEM((D, D), F32), pltpu.VMEM((N_SHARD, Q_W, cw), F32), pltpu.VMEM((N_SHARD, G_W, cw), F32)]
        + (_comm_scratch(ns) if ns else []),
        compiler_params=_cp("arbitrary"),
    )(dr1, a, b, gab, uv, merged, ya, yb, vec, gp, ws_stack, ws_stack_t, bias_full, w_a, w_b, w_o, *scatter)
    return out[:8], out[8:]


def _attn_bwd(q, kv, kvc, sink, dya, ya, lse, scatter=()):
    seq = q.shape[0]
    nb = seq // BLK
    n_ctx = kvc.shape[0]
    ns = len(scatter)
    Q_BLOCKS = 2
    nkv = Q_BLOCKS + 2
    steps = nb // Q_BLOCKS

    def body(q_ref, *rest):
        kv_refs = rest[:nkv]
        kvc_ref, sink_ref, do_ref, o_ref, lse_ref = rest[nkv:nkv + 5]
        rest = rest[nkv + 5:]
        dq_ref, dkv_ref, dkvc_ref, dsink_ref = rest[ns:ns + 4]
        comm = _AllToAll(rest[:ns], rest[ns + 4:2 * ns + 4], *rest[2 * ns + 4:]) if ns else None
        n = pl.program_id(0)
        _host_start(n, comm)

        @pl.when(n == 0)
        def _():
            dkv_ref[...] = jnp.zeros_like(dkv_ref)
            dkvc_ref[...] = jnp.zeros_like(dkvc_ref)
            dsink_ref[...] = jnp.zeros_like(dsink_ref)

        lane = lax.broadcasted_iota(jnp.int32, (1, LANES), 1)
        for sub in range(Q_BLOCKS):
            rs = slice(sub * BLK, (sub + 1) * BLK)
            blk = Q_BLOCKS * n + sub
            q = q_ref[rs, :]
            do = do_ref[rs, :]
            out = o_ref[rs, :]
            lse_all = lse_ref[rs, :]
            k_refs = (kvc_ref,) + kv_refs[sub:sub + 3]
            dqs, dks, dvs = [], [], []
            for hk in range(N_KV):
                q4, ks, s = _attn_scores(q, k_refs, hk, blk, nb)
                vs = [r[:, KV_W + hk * HEAD:KV_W + (hk + 1) * HEAD] for r in k_refs]
                lse4 = jnp.concatenate([lse_all[:, hk * GROUP + g:hk * GROUP + g + 1] for g in range(GROUP)], axis=0)
                do4 = _stack_heads(do, hk)
                delta = jnp.sum(do4.astype(F32) * _stack_heads(out, hk).astype(F32), axis=-1, keepdims=True)
                p = [jnp.exp((t - lse4).astype(BF16)) for t in s]
                ds = [t * (_dot_nt(do4, v) - delta).astype(BF16) for t, v in zip(p, vs)]
                dq4 = _dot(ds[0], ks[0])
                for t, k in zip(ds[1:], ks[1:]):
                    dq4 = dq4 + _dot(t, k)
                dq4 = dq4 * SCALE
                dqs += [dq4[g * BLK:(g + 1) * BLK, :] for g in range(GROUP)]
                dks.append([_dot_tn(t, q4) for t in ds])
                dvs.append([_dot_tn(t, do4) for t in p])
                ps = jnp.exp(_sink_rows(sink_ref, hk) - lse4) * delta
                for g in range(GROUP):
                    part = -jnp.sum(ps[g * BLK:(g + 1) * BLK, :], axis=0, keepdims=True)
                    dsink_ref[0:1, :] += jnp.where(lane == hk * GROUP + g, part, 0.0)
            dq_ref[rs, :] = jnp.concatenate(dqs, axis=1)

            def piece(i):
                return jnp.concatenate([dks[0][i], dks[1][i], dvs[0][i], dvs[1][i]], axis=1)

            dkvc_ref[...] += piece(0)
            starts = (jnp.maximum(blk - 1, 0), blk, jnp.minimum(blk + 1, nb - 1))
            for i, st in enumerate(starts):
                r = pl.ds(pl.multiple_of(st * BLK, BLK), BLK)
                dkv_ref[r, :] += piece(i + 1)
        _host_finish(n, steps - 1, comm)

    tq = Q_BLOCKS * BLK
    out = pl.pallas_call(
        body, name="attn_bwd", grid=(steps,),
        in_specs=[_rows(tq, Q_W)] + _kv_specs(nb, Q_BLOCKS) + [_acc((n_ctx, 2 * KV_W)), pl.BlockSpec(memory_space=pltpu.SMEM),
                                                     _rows(tq, Q_W), _rows(tq, Q_W), _rows(tq, LANES)] + _comm_specs(ns),
        out_specs=[_rows(tq, Q_W), _acc((seq, 2 * KV_W)), _acc((n_ctx, 2 * KV_W)), _acc((8, LANES))] + _comm_specs(ns),
        out_shape=[jax.ShapeDtypeStruct((seq, Q_W), F32), jax.ShapeDtypeStruct((seq, 2 * KV_W), F32),
                   jax.ShapeDtypeStruct((n_ctx, 2 * KV_W), F32), jax.ShapeDtypeStruct((8, LANES), F32)]
        + [jax.ShapeDtypeStruct(v.shape, v.dtype) for v in scatter],
        scratch_shapes=_comm_scratch(ns) if ns else [],
        compiler_params=_cp("arbitrary"),
    )(q, *([kv] * nkv), kvc, sink, dya, ya, lse, *scatter)
    return out[:4], out[4:]


def _proj_bwd(dq, dkv, dpb, x, dr1, modx, w_in, cos, sin, tm, scatter=()):
    seq = x.shape[0]
    pw = IN_W - Q_W - 2 * KV_W
    ns = len(scatter)

    def body(dq_ref, dkv_ref, dpb_ref, x_ref, dr1_ref, mod_ref, w_ref, cos_ref, sin_ref, *rest):
        dqkv_ref, gx_ref, st_ref = rest[ns:ns + 3]
        comm = _AllToAll(rest[:ns], rest[ns + 3:2 * ns + 3], *rest[2 * ns + 3:]) if ns else None
        _host_start(pl.program_id(0), comm)

        @pl.when(pl.program_id(0) == 0)
        def _():
            st_ref[...] = jnp.zeros_like(st_ref)

        cos1, sin1 = cos_ref[...], sin_ref[...]
        cos2 = jnp.concatenate([cos1, cos1], axis=1)
        sin2 = jnp.concatenate([sin1, sin1], axis=1)
        for j in range(Q_W // 256):
            cs = slice(256 * j, 256 * (j + 1))
            dqkv_ref[:, cs] = _unrope(dq_ref[:, cs], cos2, sin2).astype(BF16)
        dqkv_ref[:, Q_W:Q_W + KV_W] = _unrope(dkv_ref[:, :KV_W], cos1, sin1).astype(BF16)
        dqkv_ref[:, Q_W + KV_W:] = dkv_ref[:, KV_W:].astype(BF16)
        o = Q_W + 2 * KV_W
        dh = _dot(dqkv_ref[...], w_ref[:o, :]) + _dot(dpb_ref[...], w_ref[o:, :])
        xhat, rstd = _ln(x_ref[...])
        st_ref[0:1, :] += _colsum(dh)
        st_ref[1:2, :] += _colsum(dh * xhat)
        gx_ref[...] = _ln_bwd(dh * (1.0 + mod_ref[1:2, :]), xhat, rstd) + ALPHA * dr1_ref[...]
        _host_finish(pl.program_id(0), seq // tm - 1, comm)

    out = pl.pallas_call(
        body, name="proj_bwd", grid=(seq // tm,),
        in_specs=[_rows(tm, Q_W), _rows(tm, 2 * KV_W), _rows(tm, pw), _rows(tm, D), _rows(tm, D), _acc((8, D)),
                  _resident((IN_W, D)), _rows(tm, LANES), _rows(tm, LANES)] + _comm_specs(ns),
        out_specs=[_rows(tm, Q_W + 2 * KV_W), _rows(tm, D), _acc((8, D))] + _comm_specs(ns),
        out_shape=[jax.ShapeDtypeStruct((seq, Q_W + 2 * KV_W), BF16), jax.ShapeDtypeStruct((seq, D), F32),
                   jax.ShapeDtypeStruct((8, D), F32)] + [jax.ShapeDtypeStruct(v.shape, v.dtype) for v in scatter],
        scratch_shapes=_comm_scratch(ns) if ns else [],
        compiler_params=_cp("arbitrary"),
    )(dq, dkv, dpb, x, dr1, modx, w_in, cos, sin, *scatter)
    return out[:3], out[3:]


def _ctx_bwd(dkvc, ctx, hc, w_kv):
    n_ctx = ctx.shape[0]

    def body(dkvc_ref, ctx_ref, hc_ref, w_ref, dw_ref, st_ref):
        d = dkvc_ref[...].astype(BF16)
        dw_ref[...] = _dot_tn(d, hc_ref[...])
        dhc = _dot(d, w_ref[...])
        xhat, _ = _ln(ctx_ref[...])
        st_ref[...] = jnp.zeros_like(st_ref)
        st_ref[0:1, :] = _colsum(dhc)
        st_ref[1:2, :] = _colsum(dhc * xhat)

    return pl.pallas_call(
        body, name="ctx_bwd", grid=(1,),
        in_specs=[_acc((n_ctx, 2 * KV_W)), _acc((n_ctx, D)), _acc((n_ctx, D)), _acc((2 * KV_W, D))],
        out_specs=[_acc((2 * KV_W, D)), _acc((8, D))],
        out_shape=[jax.ShapeDtypeStruct((2 * KV_W, D), F32), jax.ShapeDtypeStruct((8, D), F32)],
        compiler_params=_cp("arbitrary"),
    )(dkvc, ctx, hc, w_kv)


def _tn_matmul(a, b, tn, name, out_dtype, shard_major=False, init=None, tk=512, scatter=(), gather=()):
    t, ka = a.shape
    n = b.shape[1]
    tk = min(tk, t)
    nk = t // tk
    nj = n // tn
    has_init = init is not None
    assert not (scatter and gather)
    moved = list(scatter) + list(gather)
    pattern = _AllToAll if scatter else _Gather
    ns = len(moved)
    n_in = 3 if has_init else 2

    def body(*refs):
        a_ref, b_ref = refs[:2]
        i_ref = refs[2] if has_init else None
        rest = refs[n_in:]
        o_ref = rest[ns]
        acc_ref = rest[2 * ns + 1]
        comm = pattern(rest[:ns], rest[ns + 1:2 * ns + 1], *rest[2 * ns + 2:]) if ns else None
        k = pl.program_id(1)
        step = pl.program_id(0) * nk + k
        _host_start(step, comm)

        @pl.when(k == 0)
        def _():
            acc_ref[...] = i_ref[...] if has_init else jnp.zeros_like(acc_ref)

        acc_ref[...] += _dot_tn(a_ref[...], b_ref[...])

        @pl.when(k == nk - 1)
        def _():
            o_ref[...] = acc_ref[...].astype(out_dtype)

        _host_finish(step, nj * nk - 1, comm)

    in_specs = [pl.BlockSpec((tk, ka), lambda j, k: (k, 0)), pl.BlockSpec((tk, tn), lambda j, k: (k, j))]
    args = [a, b]
    if has_init:
        in_specs.append(pl.BlockSpec((ka, tn), lambda j, k: (0, j)))
        args.append(init)
    if shard_major:
        out_spec = pl.BlockSpec((None, ka, tn), lambda j, k: (j, 0, 0))
        out_shape = jax.ShapeDtypeStruct((nj, ka, tn), out_dtype)
    else:
        out_spec = pl.BlockSpec((ka, tn), lambda j, k: (0, j))
        out_shape = jax.ShapeDtypeStruct((ka, n), out_dtype)
    out = pl.pallas_call(
        body, name=name, grid=(nj, nk), in_specs=in_specs + _comm_specs(ns), out_specs=[out_spec] + _comm_specs(ns),
        out_shape=[out_shape] + [jax.ShapeDtypeStruct(v.shape, v.dtype) for v in scatter] + _gathered_shapes(gather),
        scratch_shapes=[pltpu.VMEM((ka, tn), F32)] + (_comm_scratch(ns) if ns else []),
        compiler_params=_cp("arbitrary", "arbitrary"),
    )(*args, *moved)
    return (out[0], out[1:]) if ns else out[0]


ADA_TILE = 512


def _ada_fwd(sc_all, w_ada):
    cs = w_ada.shape[1]

    def body(s_ref, w_ref, o_ref):
        o_ref[...] = _dot(s_ref[...].astype(BF16), w_ref[...].astype(BF16))

    return pl.pallas_call(
        body, name="ada_fwd", grid=(cs // ADA_TILE,),
        in_specs=[_acc((16, D)), pl.BlockSpec((D, ADA_TILE), lambda j: (0, j))],
        out_specs=pl.BlockSpec((16, ADA_TILE), lambda j: (0, j)),
        out_shape=jax.ShapeDtypeStruct((16, cs), F32),
        compiler_params=_cp("arbitrary"),
    )(sc_all, w_ada)


def _ada_bwd(sc_all_t, dm_all, dmc, w_ada):
    cs = w_ada.shape[1]

    def body(st_ref, dm_ref, dmc_ref, w_ref, gw_ref, part_ref):
        @pl.when(pl.program_id(0) == 0)
        def _():
            part_ref[...] = jnp.zeros_like(part_ref)

        gw_ref[...] = _dot(st_ref[...].astype(BF16), dm_ref[...].astype(BF16))
        part_ref[...] += _dot_nt(dmc_ref[...].astype(BF16), w_ref[...].astype(BF16))

    return pl.pallas_call(
        body, name="ada_bwd", grid=(cs // ADA_TILE,),
        in_specs=[_acc((D, 16)), pl.BlockSpec((16, ADA_TILE), lambda j: (0, j)), pl.BlockSpec((8, ADA_TILE), lambda j: (0, j)),
                  pl.BlockSpec((D, ADA_TILE), lambda j: (0, j))],
        out_specs=[pl.BlockSpec((D, ADA_TILE), lambda j: (0, j)), _acc((8, D))],
        out_shape=[jax.ShapeDtypeStruct((D, cs), F32), jax.ShapeDtypeStruct((8, D), F32)],
        compiler_params=_cp("arbitrary"),
    )(sc_all_t, dm_all, dmc, w_ada)


def _sum8(x, name, tr=256):
    _, r, c = x.shape
    tr = min(tr, r)
    while r % tr:
        tr -= 16

    def body(x_ref, o_ref):
        acc = x_ref[0].astype(F32)
        for i in range(1, N_DEV):
            acc = acc + x_ref[i].astype(F32)
        o_ref[...] = acc

    return pl.pallas_call(
        body, name=name, grid=(r // tr,),
        in_specs=[pl.BlockSpec((N_DEV, tr, c), lambda i: (0, i, 0))],
        out_specs=pl.BlockSpec((tr, c), lambda i: (i, 0)),
        out_shape=jax.ShapeDtypeStruct((r, c), F32),
        compiler_params=_cp("arbitrary"),
    )(x)


def _sum_blocks(recv, src, me, name, tr=256):
    _, r, c = recv.shape
    tr = min(tr, r)
    while r % tr:
        tr -= 16

    def body(me_ref, recv_ref, own_ref, o_ref):
        acc = own_ref[...].astype(F32)
        for k in range(1, N_DEV):
            acc = acc + recv_ref[me_ref[0] ^ k].astype(F32)
        o_ref[...] = acc

    return pl.pallas_call(
        body, name=name,
        grid_spec=pltpu.PrefetchScalarGridSpec(
            num_scalar_prefetch=1, grid=(r // tr,),
            in_specs=[pl.BlockSpec((N_DEV, tr, c), lambda i, me_ref: (0, i, 0)),
                      pl.BlockSpec((None, tr, c), lambda i, me_ref: (me_ref[0], i, 0))],
            out_specs=pl.BlockSpec((tr, c), lambda i, me_ref: (i, 0))),
        out_shape=jax.ShapeDtypeStruct((r, c), F32),
        compiler_params=_cp("arbitrary"),
    )(me, recv, src)


def _sum8_many(xs, name):
    n = len(xs)

    def body(*refs):
        for x_ref, o_ref in zip(refs[:n], refs[n:]):
            acc = x_ref[0]
            for i in range(1, N_DEV):
                acc = acc + x_ref[i]
            o_ref[...] = acc

    vmem = pl.BlockSpec(memory_space=pltpu.VMEM)
    return pl.pallas_call(
        body, name=name, in_specs=[vmem] * n, out_specs=[vmem] * n,
        out_shape=[jax.ShapeDtypeStruct(v.shape[1:], v.dtype) for v in xs],
        compiler_params=pltpu.CompilerParams(vmem_limit_bytes=VMEM_LIMIT),
    )(*xs)


def _adam_update(w, g, m, v):
    nm = ADAM_B1 * m + (1.0 - ADAM_B1) * g
    nv = ADAM_B2 * v + (1.0 - ADAM_B2) * (g * g)
    m_hat = nm / (1.0 - ADAM_B1 ** ADAM_STEP)
    v_hat = nv / (1.0 - ADAM_B2 ** ADAM_STEP)
    return -ADAM_LR * (m_hat / (jnp.sqrt(v_hat) + ADAM_EPS) + ADAM_WD * w), nm, nv


ROW_LOSS, ROW_LN2_G, ROW_LN2_B, ROW_LN1_G, ROW_LN1_B = 0, 1, 2, 10, 11
ROWS_DMOD_X = (16, 17, 12, 9, 8, 3)
ROWS_DMOD_C = (24, 25)
SMALL = ("c_ctx", "b_ada", "attn_sink", "gmlp_ln_g", "gmlp_ln_b", "w_spatial", "b_spatial", "ln1_g", "ln1_b", "ln2_g", "ln2_b")


def _adamw_small(sums, dsc, w, m, v):
    n = len(SMALL)

    def body(*refs):
        st_ref, gm_ref, sk_ref, ws_ref, bs_ref, dsc_ref = refs[:6]
        w_refs = dict(zip(SMALL, refs[6:6 + n]))
        m_refs = dict(zip(SMALL, refs[6 + n:6 + 2 * n]))
        v_refs = dict(zip(SMALL, refs[6 + 2 * n:6 + 3 * n]))
        outs = refs[6 + 3 * n:]
        c = w_refs["c_ctx"][...]
        sg = _sig(c)
        dmod = [st_ref[r:r + 1, :] for r in ROWS_DMOD_X]
        dmod[0] = dmod[0] + st_ref[ROWS_DMOD_C[0]:ROWS_DMOD_C[0] + 1, :]
        dmod[1] = dmod[1] + st_ref[ROWS_DMOD_C[1]:ROWS_DMOD_C[1] + 1, :]
        grads = dict(
            c_ctx=dsc_ref[0:1, :] * (sg * (1.0 + c * (1.0 - sg))),
            b_ada=jnp.concatenate(dmod, axis=1),
            attn_sink=sk_ref[0:1, 0:N_KV * GROUP],
            gmlp_ln_g=gm_ref[0:1, :], gmlp_ln_b=gm_ref[1:2, :],
            w_spatial=ws_ref[...], b_spatial=bs_ref[...],
            ln1_g=st_ref[ROW_LN1_G:ROW_LN1_G + 1, :], ln1_b=st_ref[ROW_LN1_B:ROW_LN1_B + 1, :],
            ln2_g=st_ref[ROW_LN2_G:ROW_LN2_G + 1, :], ln2_b=st_ref[ROW_LN2_B:ROW_LN2_B + 1, :])
        for i, name in enumerate(SMALL):
            g = grads[name]
            d, nm, nv = _adam_update(w_refs[name][...], g, m_refs[name][...], v_refs[name][...])
            outs[i][...] = g
            outs[n + i][...] = d
            outs[2 * n + i][...] = nm
            outs[3 * n + i][...] = nv

    vmem = pl.BlockSpec(memory_space=pltpu.VMEM)
    args = list(sums) + [dsc] + [w[k] for k in SMALL] + [m[k] for k in SMALL] + [v[k] for k in SMALL]
    shapes = [jax.ShapeDtypeStruct(w[k].shape, F32) for k in SMALL]
    out = pl.pallas_call(
        body, name="adamw_small", in_specs=[vmem] * len(args), out_specs=[vmem] * (4 * n), out_shape=shapes * 4,
        compiler_params=pltpu.CompilerParams(vmem_limit_bytes=VMEM_LIMIT),
    )(*args)
    return [dict(zip(SMALL, out[i * n:(i + 1) * n])) for i in range(4)]


def _adamw_halves(w, mine, theirs, m, v, c_arr, name):
    r, c = w.shape
    tr = min(256, r // 2)
    while (r // 2) % tr:
        tr -= 8
    nt = (r // 2) // tr

    def body(c_ref, w_ref, mine_ref, theirs_ref, m_ref, v_ref, g_ref, d_ref, nm_ref, nv_ref):
        g = jnp.where(pl.program_id(0) == c_ref[0], mine_ref[...], theirs_ref[...])
        g_ref[...] = g
        d_ref[...], nm_ref[...], nv_ref[...] = _adam_update(w_ref[...], g, m_ref[...], v_ref[...])

    whole = pl.BlockSpec((tr, c), lambda hb, i, c_ref: (hb * nt + i, 0))
    half = pl.BlockSpec((tr, c), lambda hb, i, c_ref: (i, 0))
    shp = jax.ShapeDtypeStruct((r, c), F32)
    return pl.pallas_call(
        body, name=name,
        grid_spec=pltpu.PrefetchScalarGridSpec(
            num_scalar_prefetch=1, grid=(2, nt), in_specs=[whole, half, half, whole, whole], out_specs=[whole] * 4),
        out_shape=[shp] * 4,
        compiler_params=_cp("arbitrary", "arbitrary"),
    )(c_arr, w, mine, theirs, m, v)


def _adamw(w, g, m, v, name):
    r, c = w.shape
    tr = r if r * c <= 256 * 1024 else min(256, r)
    while r % tr:
        tr -= 8

    def body(w_ref, g_ref, m_ref, v_ref, d_ref, nm_ref, nv_ref):
        d_ref[...], nm_ref[...], nv_ref[...] = _adam_update(w_ref[...], g_ref[...], m_ref[...], v_ref[...])

    spec = pl.BlockSpec((tr, c), lambda i: (i, 0))
    shp = jax.ShapeDtypeStruct((r, c), F32)
    return pl.pallas_call(
        body, name=name, grid=(r // tr,), in_specs=[spec] * 4, out_specs=[spec] * 3, out_shape=[shp] * 3,
        compiler_params=_cp("arbitrary"),
    )(w, g, m, v)


def _my_pos():
    return lax.axis_index("x"), lax.axis_index("y"), lax.axis_index("c")


N_COPY = 7


class _Gather:
    def __init__(self, x_refs, out_refs, send_sems, recv_sems):
        self.x_refs, self.out_refs = x_refs, out_refs
        self.send_sems, self.recv_sems = send_sems, recv_sems
        x, y, c = _my_pos()
        self.c = c
        self.me, self.sibling = (x, y, c), (x, y, 1 - c)
        self.chips = [(1 - x, y), (x, 1 - y), (1 - x, 1 - y)]

    def _copy(self, a, k, block, to, from_input=False):
        px, py, pc = block
        rows = self.out_refs[a].at[4 * px + 2 * py + pc]
        return pltpu.make_async_remote_copy(
            src_ref=self.x_refs[a] if from_input else rows, dst_ref=rows,
            send_sem=self.send_sems.at[a * N_COPY + k], recv_sem=self.recv_sems.at[a * N_COPY + k],
            device_id=to, device_id_type=MESH)

    def start(self):
        n = len(self.x_refs)
        for a in range(n):
            self._copy(a, 0, self.me, self.sibling, from_input=True).start()
        for j, chip in enumerate(self.chips):
            for a in range(n):
                self._copy(a, 1 + j, self.me, (*chip, self.c), from_input=True).start()

    def finish(self):
        n = len(self.x_refs)
        c = self.c
        for j, chip in enumerate(self.chips):
            for a in range(n):
                self._copy(a, 1 + j, (*chip, c), self.me).wait_recv()
                self._copy(a, 4 + j, (*chip, c), self.sibling).start()
        for a in range(n):
            self._copy(a, 0, self.sibling, self.me).wait_recv()
        for j, chip in enumerate(self.chips):
            for a in range(n):
                self._copy(a, 4 + j, (*chip, 1 - c), self.me).wait_recv()
        for a in range(n):
            self._copy(a, 0, self.me, self.sibling, from_input=True).wait_send()
            for j, chip in enumerate(self.chips):
                self._copy(a, 1 + j, self.me, (*chip, c), from_input=True).wait_send()
                self._copy(a, 4 + j, (*chip, c), self.sibling).wait_send()


def _comm_scratch(n):
    return [pltpu.SemaphoreType.DMA((n * N_COPY,)), pltpu.SemaphoreType.DMA((n * N_COPY,))]


def _comm_specs(n):
    return [pl.BlockSpec(memory_space=pl.ANY)] * n


def _gathered_shapes(xs):
    return [jax.ShapeDtypeStruct((N_DEV,) + v.shape, v.dtype) for v in xs]


def _with_own(gathered, xs, me):
    return [lax.dynamic_update_index_in_dim(g, v, me, 0) for g, v in zip(gathered, xs)]


def _all_gather(xs, me, name):
    n = len(xs)

    def body(*refs):
        g = _Gather(refs[:n], refs[n:2 * n], *refs[2 * n:])
        g.start()
        g.finish()

    out = pl.pallas_call(
        body, name=name, out_shape=_gathered_shapes(xs), in_specs=_comm_specs(n), out_specs=_comm_specs(n),
        scratch_shapes=_comm_scratch(n),
    )(*xs)
    return _with_own(out, xs, me)


class _AllToAll:
    def __init__(self, x_refs, out_refs, send_sems, recv_sems):
        self.x_refs, self.out_refs = x_refs, out_refs
        self.send_sems, self.recv_sems = send_sems, recv_sems
        self.pos = _my_pos()
        x, y, c = self.pos
        self.me = 4 * x + 2 * y + c

    def _peer(self, k):
        x, y, c = self.pos
        return (x ^ ((k >> 2) & 1), y ^ ((k >> 1) & 1), c ^ (k & 1))

    def _copy(self, a, k):
        p = self._peer(k)
        return pltpu.make_async_remote_copy(
            src_ref=self.x_refs[a].at[4 * p[0] + 2 * p[1] + p[2]], dst_ref=self.out_refs[a].at[self.me],
            send_sem=self.send_sems.at[a * N_COPY + k - 1], recv_sem=self.recv_sems.at[a * N_COPY + k - 1],
            device_id=p, device_id_type=MESH)

    def start(self):
        for k in range(1, N_DEV):
            for a in range(len(self.x_refs)):
                self._copy(a, k).start()

    def finish(self):
        for a in range(len(self.x_refs)):
            for k in range(1, N_DEV):
                self._copy(a, k).wait_recv()
            for k in range(1, N_DEV):
                self._copy(a, k).wait_send()


def _all_to_all(blocks, name):
    n = len(blocks)

    def body(*refs):
        t = _AllToAll(refs[:n], refs[n:2 * n], *refs[2 * n:])
        t.start()
        t.finish()

    return pl.pallas_call(
        body, name=name, out_shape=[jax.ShapeDtypeStruct(v.shape, v.dtype) for v in blocks],
        in_specs=_comm_specs(n), out_specs=_comm_specs(n), scratch_shapes=_comm_scratch(n),
    )(*blocks)


def _sibling_exchange(xs, name):
    n = len(xs)

    def body(*refs):
        x_refs, out_refs = refs[:n], refs[n:2 * n]
        send_sems, recv_sems = refs[2 * n:]
        x, y, c = _my_pos()

        def push(a):
            return pltpu.make_async_remote_copy(
                src_ref=x_refs[a], dst_ref=out_refs[a], send_sem=send_sems.at[a], recv_sem=recv_sems.at[a],
                device_id=(x, y, 1 - c), device_id_type=MESH)

        for a in range(n):
            push(a).start()
        for a in range(n):
            push(a).wait_recv()
            push(a).wait_send()

    return pl.pallas_call(
        body, name=name, out_shape=[jax.ShapeDtypeStruct(v.shape, v.dtype) for v in xs],
        in_specs=_comm_specs(n), out_specs=_comm_specs(n),
        scratch_shapes=[pltpu.SemaphoreType.DMA((n,)), pltpu.SemaphoreType.DMA((n,))],
    )(*xs)


def _scatter_and_gather(scatter, gather, name):
    ns, ng = len(scatter), len(gather)

    def body(*refs):
        s_in, g_in = refs[:ns], refs[ns:ns + ng]
        s_out, g_out = refs[ns + ng:2 * ns + ng], refs[2 * ns + ng:2 * (ns + ng)]
        s_send, s_recv, g_send, g_recv = refs[2 * (ns + ng):]
        g = _Gather(g_in, g_out, g_send, g_recv)
        t = _AllToAll(s_in, s_out, s_send, s_recv)
        g.start()
        t.start()
        g.finish()
        t.finish()

    out = pl.pallas_call(
        body, name=name,
        out_shape=[jax.ShapeDtypeStruct(v.shape, v.dtype) for v in scatter] + _gathered_shapes(gather),
        in_specs=_comm_specs(ns + ng), out_specs=_comm_specs(ns + ng),
        scratch_shapes=_comm_scratch(ns) + _comm_scratch(ng),
    )(*scatter, *gather)
    return out[:ns], out[ns:]


def _row_tile(seq, want):
    return min(want, seq)


def _local_step(x, ctx, tgt, mod_x, mod_c, wb, sink, gmlp_g, gmlp_b, w_s, b_s, ln1_g, ln1_b, ln2_g, ln2_b,
                later=None, me=None):
    seq = x.shape[0]
    on_mesh = me is not None
    modx1 = jnp.concatenate([mod_x[0:2], jnp.zeros((6, D), F32)], axis=0)
    modc = jnp.concatenate([mod_c[0:2], jnp.zeros((6, D), F32)], axis=0)
    vec = jnp.concatenate([mod_x[2:3], ln1_g, ln1_b, mod_x[3:6], ln2_g, ln2_b], axis=0)
    gp = jnp.concatenate([gmlp_g, gmlp_b, jnp.zeros((6, G_W), F32)], axis=0)
    ws_stack = w_s.reshape(N_GRP * BLK, BLK).astype(BF16)
    ws_stack_t = jnp.transpose(w_s, (2, 0, 1)).reshape(BLK, N_GRP * BLK).astype(BF16)
    bias_full = jnp.repeat(b_s.T, GRP_D, axis=1)
    cos, sin = _rope_tables(seq)
    w_in = wb["w_in"]
    w_kv = w_in[Q_W:Q_W + 2 * KV_W, :]
    tm_big = _row_tile(seq, 512)
    tm_ffn = _row_tile(seq, 256)

    hc, kvc, vac = _ctx_fwd(ctx, modc, w_kv)
    behind_proj = ("w_a", "w_b", "w_o") if on_mesh else ()
    behind_attn = ("w_fi", "w_fo") if on_mesh else ()
    (h, q, kv, va, uv, gab), got_proj = _proj_fwd(x, modx1, w_in, cos, sin, tm_big, gather=[later[n] for n in behind_proj])
    (ya, lse), got_attn = _attn_fwd(q, kv, va, kvc, vac, sink, gather=[later[n] for n in behind_attn])
    if on_mesh:
        wb = dict(wb)
        names = behind_proj + behind_attn
        for n, g in zip(names, _with_own(list(got_proj) + list(got_attn), [later[n] for n in names], me)):
            wb[n] = g.reshape(-1, g.shape[2]) if n in ROW_SHARDED else g.reshape(N_SHARD, 2 * g.shape[1], g.shape[2])
    a, b, mix, merged, yb = _mix_fwd(uv, gab, ya, gp, ws_stack, bias_full, wb["w_a"], wb["w_b"], wb["w_o"], tm_big)
    act, h2, dff, df, dr1, st_ffn = _ffn(x, mix, tgt, vec, wb["w_fi"], wb["w_fo"], tm_ffn)
    blocks, recv = {}, {}
    blocks["w_fo"] = _eighths(_tn_matmul(act, df, 512, "tn_w_ffn_out", BF16, tk=2048))
    if on_mesh:
        g_w_fi, (recv["w_fo"],) = _tn_matmul(h2, dff, FH_SHARD, "tn_w_ffn_in", BF16, shard_major=True, tk=2048,
                                            scatter=[blocks["w_fo"]])
    else:
        g_w_fi = _tn_matmul(h2, dff, FH_SHARD, "tn_w_ffn_in", BF16, shard_major=True, tk=2048)
    blocks["w_fi"] = _eighths(g_w_fi)
    (dya, dpb, dws, dbs_full, st4, g_w_o, g_w_a, g_w_b), got = _mix_bwd(
        dr1, a, b, gab, uv, merged, ya, yb, vec, gp, ws_stack, ws_stack_t, bias_full, wb["w_a"], wb["w_b"], wb["w_o"],
        tm_big, scatter=[blocks["w_fi"]] if on_mesh else ())
    recv.update(zip(("w_fi",), got))
    blocks.update(w_o=_eighths(g_w_o), w_a=_eighths(g_w_a), w_b=_eighths(g_w_b))
    mixer = ("w_o", "w_a", "w_b") if on_mesh else ()
    (dq, dkv, dkvc, dsink), got = _attn_bwd(q, kv, kvc, sink, dya, ya, lse, scatter=[blocks[n] for n in mixer])
    recv.update(zip(mixer, got))
    g_wkv_ctx, st0 = _ctx_bwd(dkvc, ctx, hc, w_kv)
    (dqkv, grad_x, st1), _ = _proj_bwd(dq, dkv, dpb, x, dr1, modx1, w_in, cos, sin, tm_big)
    dbs = jnp.sum(dbs_full.reshape(BLK, N_GRP, GRP_D), axis=2).T
    early = [jnp.concatenate([st_ffn, st0], axis=0), st4, dsink, dws, dbs]
    init = jnp.pad(g_wkv_ctx, ((Q_W, 0), (0, 0)))
    g_qkv = _tn_matmul(dqkv, h, D, "tn_w_in_qkv", BF16, init=init, tk=1024)
    if on_mesh:
        g_rest, early_gathered = _tn_matmul(dpb, h, D, "tn_w_in_rest", BF16, tk=1024, gather=early)
    else:
        g_rest, early_gathered = _tn_matmul(dpb, h, D, "tn_w_in_rest", BF16, tk=1024), None
    blocks["w_in"] = _eighths(jnp.concatenate([g_qkv, g_rest], axis=0))
    return grad_x, dict(early=early, early_gathered=early_gathered, late=st1), blocks, recv


BIG = ("w_in", "w_a", "w_b", "w_o", "w_fi", "w_fo")
ROW_SHARDED = ("w_o", "w_fo")


def _half_of_shard(shard, c):
    r = shard.shape[0]
    return lax.dynamic_slice_in_dim(shard, c * (r // 2), r // 2, axis=0)


def _eighths(v):
    rows = v.shape[-2] * (v.shape[0] if v.ndim == 3 else 1)
    return v.reshape(N_DEV, rows // N_DEV, v.shape[-1])


def kernel(x, c, ctx, c_ctx, w_ada, b_ada, w_in, attn_sink, gmlp_ln_g, gmlp_ln_b, w_spatial, b_spatial, w_branch_a, w_branch_b, w_out, ln1_g, ln1_b, w_ffn_in, w_ffn_out, ln2_g, ln2_b, loss_target, m_c_ctx, m_w_ada, m_b_ada, m_w_in, m_attn_sink, m_gmlp_ln_g, m_gmlp_ln_b, m_w_spatial, m_b_spatial, m_w_branch_a, m_w_branch_b, m_w_out, m_ln1_g, m_ln1_b, m_w_ffn_in, m_w_ffn_out, m_ln2_g, m_ln2_b, v_c_ctx, v_w_ada, v_b_ada, v_w_in, v_attn_sink, v_gmlp_ln_g, v_gmlp_ln_b, v_w_spatial, v_b_spatial, v_w_branch_a, v_w_branch_b, v_w_out, v_ln1_g, v_ln1_b, v_w_ffn_in, v_w_ffn_out, v_ln2_g, v_ln2_b):
    mx, my, mc = _my_pos()
    me = 4 * mx + 2 * my + mc
    chip = 2 * mx + my
    shards = dict(w_in=w_in[0].T, w_a=w_branch_a[0], w_b=w_branch_b[0], w_o=w_out[0], w_fi=w_ffn_in[0], w_fo=w_ffn_out[0])

    halves = {n: _half_of_shard(shards[n], mc).astype(BF16) for n in BIG}
    c_rows = jnp.concatenate([c, jnp.zeros((7, D), F32)], axis=0)
    g_in, c_g = _all_gather([halves["w_in"], c_rows], me, "gather_w_in")
    wb = dict(w_in=g_in.reshape(IN_W, D))

    c_all = c_g[:, 0, :]
    cc = jnp.concatenate([c_all, c_ctx[None, :], jnp.zeros((7, D), F32)], axis=0)
    sig_cc = jax.nn.sigmoid(cc)
    sc_all = cc * sig_cc
    mod_shard = _ada_fwd(sc_all, w_ada[0])
    mod_g = _all_gather([mod_shard], me, "gather_mod")[0]
    mod_all = jnp.concatenate([mod_g[2 * s] for s in range(4)], axis=1) + b_ada
    mod_x = lax.dynamic_slice_in_dim(mod_all, me, 1, axis=0).reshape(6, D)
    mod_c = mod_all[8].reshape(6, D)[0:2]

    grad_x, small, blocks, recv = _local_step(
        x[0], ctx[0], loss_target[0], mod_x, mod_c, wb, attn_sink, gmlp_ln_g, gmlp_ln_b, w_spatial[0], b_spatial[0],
        ln1_g, ln1_b, ln2_g, ln2_b, later=halves, me=me)

    (recv["w_in"],), late = _scatter_and_gather([blocks["w_in"]], [small["late"]], "scatter_w_in_gather_small")
    late = _with_own(late, [small["late"]], me)[0]
    gathered = _with_own(small["early_gathered"], small["early"], me)
    gathered[0] = jnp.concatenate([gathered[0][:, :16], late, gathered[0][:, 16:]], axis=1)

    me_arr = jnp.reshape(me, (1,)).astype(jnp.int32)
    summed = {n: _sum_blocks(recv[n], blocks[n], me_arr, "sum_grads_" + n) for n in BIG}
    theirs = dict(zip(BIG, _sibling_exchange([summed[n] for n in BIG], "exchange_grads")))

    sums = _sum8_many(gathered, "sum_small")
    stats = sums[0]
    loss = 0.5 * jnp.sum(stats[ROW_LOSS]) / D
    dmod_x_all = jnp.concatenate([gathered[0][:, r_, :] for r_ in ROWS_DMOD_X], axis=1)
    dmod_c_full = jnp.concatenate([stats[r_] for r_ in ROWS_DMOD_C] + [jnp.zeros((4 * D,), F32)])
    dm_rows = jnp.concatenate([dmod_x_all, dmod_c_full[None, :], jnp.zeros((7, 6 * D), F32)], axis=0)
    cs = w_ada.shape[2]
    dm_shard = lax.dynamic_slice_in_dim(dm_rows, chip * cs, cs, axis=1)
    dmc_shard = jnp.concatenate([dm_shard[8:9], jnp.zeros((7, cs), F32)], axis=0)
    g_w_ada, part = _ada_bwd(sc_all.T, dm_shard, dmc_shard, w_ada[0])
    part_all = _all_gather([part * (mc == 0).astype(F32)], me, "gather_c_ctx")[0]
    dsc = _sum8(part_all, "sum_c_ctx")

    grads = dict(w_ada=g_w_ada[None])
    weights = dict(c_ctx=c_ctx, w_ada=w_ada, b_ada=b_ada, w_in=w_in, attn_sink=attn_sink, gmlp_ln_g=gmlp_ln_g,
                   gmlp_ln_b=gmlp_ln_b, w_spatial=w_spatial, b_spatial=b_spatial, w_branch_a=w_branch_a,
                   w_branch_b=w_branch_b, w_out=w_out, ln1_g=ln1_g, ln1_b=ln1_b, w_ffn_in=w_ffn_in, w_ffn_out=w_ffn_out,
                   ln2_g=ln2_g, ln2_b=ln2_b)
    ms = dict(c_ctx=m_c_ctx, w_ada=m_w_ada, b_ada=m_b_ada, w_in=m_w_in, attn_sink=m_attn_sink, gmlp_ln_g=m_gmlp_ln_g,
              gmlp_ln_b=m_gmlp_ln_b, w_spatial=m_w_spatial, b_spatial=m_b_spatial, w_branch_a=m_w_branch_a,
              w_branch_b=m_w_branch_b, w_out=m_w_out, ln1_g=m_ln1_g, ln1_b=m_ln1_b, w_ffn_in=m_w_ffn_in,
              w_ffn_out=m_w_ffn_out, ln2_g=m_ln2_g, ln2_b=m_ln2_b)
    vs = dict(c_ctx=v_c_ctx, w_ada=v_w_ada, b_ada=v_b_ada, w_in=v_w_in, attn_sink=v_attn_sink, gmlp_ln_g=v_gmlp_ln_g,
              gmlp_ln_b=v_gmlp_ln_b, w_spatial=v_w_spatial, b_spatial=v_b_spatial, w_branch_a=v_w_branch_a,
              w_branch_b=v_w_branch_b, w_out=v_w_out, ln1_g=v_ln1_g, ln1_b=v_ln1_b, w_ffn_in=v_w_ffn_in,
              w_ffn_out=v_w_ffn_out, ln2_g=v_ln2_g, ln2_b=v_ln2_b)
    order = list(weights)
    delta, new_m, new_v = {}, {}, {}
    d_, m_, v_ = _adamw(w_ada[0], g_w_ada, m_w_ada[0], v_w_ada[0], "adamw_w_ada")
    delta["w_ada"], new_m["w_ada"], new_v["w_ada"] = d_[None], m_[None], v_[None]
    c_arr = jnp.reshape(mc, (1,)).astype(jnp.int32)
    names = dict(w_in="w_in", w_a="w_branch_a", w_b="w_branch_b", w_o="w_out", w_fi="w_ffn_in", w_fo="w_ffn_out")
    for k, n in names.items():
        flip = (lambda t: t.T) if k == "w_in" else (lambda t: t)
        outs = _adamw_halves(flip(weights[n][0]), summed[k], theirs[k], flip(ms[n][0]), flip(vs[n][0]), c_arr, "adamw_" + n)
        grads[n], delta[n], new_m[n], new_v[n] = [flip(t)[None] for t in outs]

    def view(a):
        return a.reshape(-1, a.shape[-1]) if a.ndim != 1 else a.reshape(1, -1)

    small = _adamw_small(sums, dsc, *[{n: view(d[n]) for n in SMALL} for d in (weights, ms, vs)])
    for out, src in zip((grads, delta, new_m, new_v), small):
        for n in SMALL:
            out[n] = src[n].reshape(weights[n].shape)

    return (loss, grad_x[None], *[grads[n] for n in order], *[delta[n] for n in order],
            *[new_m[n] for n in order], *[new_v[n] for n in order])
```

```python
import functools
import math

import jax
import jax.numpy as jnp
from jax import lax
from jax.experimental import pallas as pl
from jax.experimental.pallas import tpu as pltpu

F32 = jnp.float32
BF16 = jnp.bfloat16

D = 1024
HEAD = 64
N_KV = 2
GROUP = 4
Q_W = 512
KV_W = 128
G_W = 512
BLK = 128
N_GRP = 8
GRP_D = 64
FH = 2816
IN_W = 3840
GRID_W = 64
ROPE_BASE = 10000.0
LN_EPS = 1e-5
NEG = -1e30
ALPHA = (2 * 1) ** 0.25
SCALE = HEAD ** -0.5
GELU_K = math.sqrt(2.0 / math.pi)
GELU_A = 0.044715
ADAM_LR = 0.001
ADAM_B1 = 0.9
ADAM_B2 = 0.999
ADAM_EPS = 1e-08
ADAM_WD = 0.01
ADAM_STEP = 10
N_DEV = 8
N_SHARD = 4
FH_SHARD = FH // 2
LANES = 128
VMEM_LIMIT = 56 * 1024 * 1024
MESH = pl.DeviceIdType.MESH


def _cp(*sem):
    return pltpu.CompilerParams(dimension_semantics=sem, vmem_limit_bytes=VMEM_LIMIT)


def _resident(shape):
    return pl.BlockSpec(shape, lambda *_: (0,) * len(shape), pipeline_mode=pl.Buffered(1))


def _rows(tm, width):
    return pl.BlockSpec((tm, width), lambda i: (i, 0))


def _acc(shape):
    return pl.BlockSpec(shape, lambda *_: (0,) * len(shape))


def _dot(a, b):
    return jnp.dot(a, b, preferred_element_type=F32)


def _dot_nt(a, b):
    return lax.dot_general(a, b, (((1,), (1,)), ((), ())), preferred_element_type=F32)


def _dot_tn(a, b):
    return lax.dot_general(a, b, (((0,), (0,)), ((), ())), preferred_element_type=F32)


def _ln(x):
    mu = jnp.mean(x, axis=-1, keepdims=True)
    xc = x - mu
    var = jnp.mean(xc * xc, axis=-1, keepdims=True)
    rstd = lax.rsqrt(var + LN_EPS)
    return xc * rstd, rstd


def _ln_bwd(dxhat, xhat, rstd):
    return (dxhat - jnp.mean(dxhat, axis=-1, keepdims=True)
            - xhat * jnp.mean(dxhat * xhat, axis=-1, keepdims=True)) * rstd


def _sig(x):
    return 1.0 / (1.0 + jnp.exp(-x))


def _gelu(x):
    t = jnp.tanh(GELU_K * (x + GELU_A * x * x * x))
    return 0.5 * x * (1.0 + t), t


def _gelu_grad(x, t):
    return 0.5 * (1.0 + t) + 0.5 * x * (1.0 - t * t) * GELU_K * (1.0 + 3.0 * GELU_A * x * x)


def _colsum(v):
    return jnp.sum(v, axis=0, keepdims=True)


def _partner(x):
    w = x.shape[1]
    lane = lax.broadcasted_iota(jnp.int32, x.shape, 1)
    return jnp.where((lane & 31) < 16, pltpu.roll(x, w - 16, 1), pltpu.roll(x, 16, 1))


def _rope(x, cos, sin):
    return x * cos + _partner(x) * sin


def _unrope(g, cos, sin):
    return g * cos + _partner(g * sin)


def _rope_tables(seq):
    inv = ROPE_BASE ** (-jnp.arange(HEAD // 4, dtype=F32) / (HEAD // 4))
    pos = jnp.arange(seq, dtype=jnp.int32)
    ar = (pos // GRID_W).astype(F32)[:, None] * inv
    ac = (pos % GRID_W).astype(F32)[:, None] * inv
    cos = jnp.concatenate([jnp.cos(ar), jnp.cos(ar), jnp.cos(ac), jnp.cos(ac)], axis=-1)
    sin = jnp.concatenate([-jnp.sin(ar), jnp.sin(ar), -jnp.sin(ac), jnp.sin(ac)], axis=-1)
    return jnp.tile(cos, (1, LANES // HEAD)), jnp.tile(sin, (1, LANES // HEAD))


def _ctx_fwd(ctx, modc, w_kv):
    n_ctx = ctx.shape[0]

    def body(ctx_ref, mod_ref, w_ref, hc_ref, kvc_ref, vac_ref):
        xhat, _ = _ln(ctx_ref[...])
        hc = (xhat * (1.0 + mod_ref[1:2, :]) + mod_ref[0:1, :]).astype(BF16)
        hc_ref[...] = hc
        kvc = _dot_nt(hc, w_ref[...]).astype(BF16)
        kvc_ref[...] = kvc
        vac_ref[...] = _with_ones(kvc[:, KV_W:])

    return pl.pallas_call(
        body, name="ctx_fwd", grid=(1,),
        in_specs=[_acc((n_ctx, D)), _acc((8, D)), _acc((2 * KV_W, D))],
        out_specs=[_acc((n_ctx, D)), _acc((n_ctx, 2 * KV_W)), _acc((n_ctx, 2 * LANES))],
        out_shape=[jax.ShapeDtypeStruct((n_ctx, D), BF16), jax.ShapeDtypeStruct((n_ctx, 2 * KV_W), BF16),
                   jax.ShapeDtypeStruct((n_ctx, 2 * LANES), BF16)],
        compiler_params=_cp("arbitrary"),
    )(ctx, modc, w_kv)


def _host_start(step, comm):
    if comm is not None:
        @pl.when(step == 0)
        def _():
            comm.start()


def _host_finish(step, last, comm):
    if comm is not None:
        @pl.when(step == last)
        def _():
            comm.finish()


def _proj_fwd(x, modx, w_in, cos, sin, tm, gather=()):
    seq = x.shape[0]
    ng = len(gather)

    def body(x_ref, mod_ref, w_ref, cos_ref, sin_ref, *rest):
        h_ref, q_ref, kv_ref, va_ref, uv_ref, gab_ref = rest[ng:ng + 6]
        comm = _Gather(rest[:ng], rest[ng + 6:2 * ng + 6], *rest[2 * ng + 6:]) if ng else None
        _host_start(pl.program_id(0), comm)
        xhat, _ = _ln(x_ref[...])
        h = (xhat * (1.0 + mod_ref[1:2, :]) + mod_ref[0:1, :]).astype(BF16)
        h_ref[...] = h
        cos1, sin1 = cos_ref[...], sin_ref[...]
        cos2 = jnp.concatenate([cos1, cos1], axis=1)
        sin2 = jnp.concatenate([sin1, sin1], axis=1)
        for j in range(Q_W // 256):
            t = _dot_nt(h, w_ref[256 * j:256 * (j + 1), :])
            q_ref[:, 256 * j:256 * (j + 1)] = (_rope(t, cos2, sin2) * SCALE).astype(BF16)
        t = _dot_nt(h, w_ref[Q_W:Q_W + 2 * KV_W, :])
        kv_ref[:, :KV_W] = _rope(t[:, :KV_W], cos1, sin1).astype(BF16)
        v = t[:, KV_W:].astype(BF16)
        kv_ref[:, KV_W:] = v
        va_ref[...] = _with_ones(v)
        o = Q_W + 2 * KV_W
        for j in range(2):
            uv_ref[:, G_W * j:G_W * (j + 1)] = _dot_nt(h, w_ref[o + G_W * j:o + G_W * (j + 1), :]).astype(BF16)
        o += 2 * G_W
        for j in range(4):
            gab_ref[:, 512 * j:512 * (j + 1)] = _dot_nt(h, w_ref[o + 512 * j:o + 512 * (j + 1), :]).astype(BF16)
        _host_finish(pl.program_id(0), seq // tm - 1, comm)

    out = pl.pallas_call(
        body, name="proj_fwd", grid=(seq // tm,),
        in_specs=[_rows(tm, D), _acc((8, D)), _resident((IN_W, D)), _rows(tm, LANES), _rows(tm, LANES)] + _comm_specs(ng),
        out_specs=[_rows(tm, D), _rows(tm, Q_W), _rows(tm, 2 * KV_W), _rows(tm, 2 * LANES), _rows(tm, 2 * G_W),
                   _rows(tm, 2 * D)] + _comm_specs(ng),
        out_shape=[jax.ShapeDtypeStruct((seq, D), BF16), jax.ShapeDtypeStruct((seq, Q_W), BF16),
                   jax.ShapeDtypeStruct((seq, 2 * KV_W), BF16), jax.ShapeDtypeStruct((seq, 2 * LANES), BF16),
                   jax.ShapeDtypeStruct((seq, 2 * G_W), BF16), jax.ShapeDtypeStruct((seq, 2 * D), BF16)] + _gathered_shapes(gather),
        scratch_shapes=_comm_scratch(ng) if ng else [],
        compiler_params=_cp("arbitrary"),
    )(x, modx, w_in, cos, sin, *gather)
    return out[:6], out[6:]


def _stack_heads(x, hk):
    return jnp.concatenate([x[:, (hk * GROUP + g) * HEAD:(hk * GROUP + g + 1) * HEAD] for g in range(GROUP)], axis=0)


def _attn_scores(q, k_refs, hk, n, nb):
    q4 = _stack_heads(q, hk)
    ks = [r[:, hk * HEAD:(hk + 1) * HEAD] for r in k_refs]
    rows = GROUP * BLK
    qi = lax.broadcasted_iota(jnp.int32, (rows, BLK), 0) & (BLK - 1)
    kj = lax.broadcasted_iota(jnp.int32, (rows, BLK), 1)
    s = [_dot_nt(q4, k) for k in ks]
    s[1] = jnp.where((kj >= qi) & (n > 0), s[1], NEG)
    s[3] = jnp.where((kj <= qi) & (n < nb - 1), s[3], NEG)
    return q4, ks, s


def _sink_rows(sink_ref, hk):
    rows = GROUP * BLK
    rg = lax.broadcasted_iota(jnp.int32, (rows, 1), 0) >> 7
    sink_v = jnp.full((rows, 1), sink_ref[0, hk * GROUP], F32)
    for g in range(1, GROUP):
        sink_v = jnp.where(rg == g, sink_ref[0, hk * GROUP + g], sink_v)
    return sink_v


def _with_ones(v):
    ones = jnp.ones((v.shape[0], HEAD), v.dtype)
    return jnp.concatenate([v[:, :HEAD], ones, v[:, HEAD:], ones], axis=1)


def _kv_specs(nb, qb):
    def spec(d):
        return pl.BlockSpec((BLK, 2 * KV_W), lambda n: (jnp.clip(qb * n + d, 0, nb - 1), 0))
    return [spec(d) for d in range(-1, qb + 1)]


def _attn_fwd(q, kv, va, kvc, vac, sink, gather=()):
    seq = q.shape[0]
    nb = seq // BLK
    n_ctx = kvc.shape[0]
    ng = len(gather)
    Q_BLOCKS = 1
    nkv = Q_BLOCKS + 2
    steps = nb // Q_BLOCKS

    def body(q_ref, *rest):
        kv_refs, va_refs = rest[:nkv], rest[nkv:2 * nkv]
        kvc_ref, vac_ref, sink_ref = rest[2 * nkv:2 * nkv + 3]
        rest = rest[2 * nkv + 3:]
        o_ref, lse_ref = rest[ng:ng + 2]
        comm = _Gather(rest[:ng], rest[ng + 2:2 * ng + 2], *rest[2 * ng + 2:]) if ng else None
        n = pl.program_id(0)
        _host_start(n, comm)
        lane = lax.broadcasted_iota(jnp.int32, (BLK, LANES), 1)
        for sub in range(Q_BLOCKS):
            rs = slice(sub * BLK, (sub + 1) * BLK)
            q = q_ref[rs, :]
            outs = []
            lse_all = jnp.zeros((BLK, LANES), F32)
            for hk in range(N_KV):
                _, _, s = _attn_scores(q, (kvc_ref,) + kv_refs[sub:sub + 3], hk, Q_BLOCKS * n + sub, nb)
                sink_v = _sink_rows(sink_ref, hk)
                m = sink_v
                for t in s:
                    m = jnp.maximum(m, jnp.max(t, axis=-1, keepdims=True))
                o = jnp.zeros((GROUP * BLK, LANES), F32)
                for t, va_ref in zip(s, (vac_ref,) + va_refs[sub:sub + 3]):
                    o = o + _dot(jnp.exp((t - m).astype(BF16)), va_ref[:, hk * LANES:(hk + 1) * LANES])
                denom = o[:, HEAD:HEAD + 1] + jnp.exp(sink_v - m)
                o4 = o[:, :HEAD] * (1.0 / denom)
                lse4 = m + jnp.log(denom)
                for g in range(GROUP):
                    outs.append(o4[g * BLK:(g + 1) * BLK, :])
                    lse_all = jnp.where(lane == hk * GROUP + g, lse4[g * BLK:(g + 1) * BLK, :], lse_all)
            o_ref[rs, :] = jnp.concatenate(outs, axis=1).astype(BF16)
            lse_ref[rs, :] = lse_all
        _host_finish(n, steps - 1, comm)

    tq = Q_BLOCKS * BLK
    out = pl.pallas_call(
        body, name="attn_fwd", grid=(steps,),
        in_specs=[_rows(tq, Q_W)] + _kv_specs(nb, Q_BLOCKS) + _kv_specs(nb, Q_BLOCKS)
        + [_acc((n_ctx, 2 * KV_W)), _acc((n_ctx, 2 * LANES)), pl.BlockSpec(memory_space=pltpu.SMEM)] + _comm_specs(ng),
        out_specs=[_rows(tq, Q_W), _rows(tq, LANES)] + _comm_specs(ng),
        out_shape=[jax.ShapeDtypeStruct((seq, Q_W), BF16), jax.ShapeDtypeStruct((seq, LANES), F32)] + _gathered_shapes(gather),
        scratch_shapes=_comm_scratch(ng) if ng else [],
        compiler_params=_cp("arbitrary"),
    )(q, *([kv] * nkv), *([va] * nkv), kvc, vac, sink, *gather)
    return out[:2], out[2:]


def _gmlp_chunk(u, vb, gp_ref, ws_ref, bias_ref):
    gu, tu = _gelu(u)
    gv, tv = _gelu(vb)
    vhat, rstd = _ln(gv)
    vn = (vhat * gp_ref[0:1, :] + gp_ref[1:2, :]).astype(BF16)
    s = bias_ref[...] + jnp.concatenate(
        [_dot(ws_ref[g * BLK:(g + 1) * BLK, :], vn[:, g * GRP_D:(g + 1) * GRP_D]) for g in range(N_GRP)], axis=1)
    return gu, tu, tv, vhat, rstd, vn, s


def _mix_fwd(uv, gab, ya, gp, ws_stack, bias_full, w_a, w_b, w_o, tm):
    seq = uv.shape[0]

    def body(uv_ref, gab_ref, ya_ref, gp_ref, ws_ref, bias_ref, wa_ref, wb_ref, wo_ref,
             a_ref, b_ref, mix_ref, merged_ref, yb_ref):
        for c in range(tm // BLK):
            rs = slice(c * BLK, (c + 1) * BLK)
            gu, _, _, _, _, _, s = _gmlp_chunk(uv_ref[rs, :G_W].astype(F32), uv_ref[rs, G_W:].astype(F32), gp_ref, ws_ref, bias_ref)
            yb_ref[rs, :] = (gu * s).astype(BF16)
        a = _dot(ya_ref[...], wa_ref[...])
        b = _dot(yb_ref[...], wb_ref[...])
        a_ref[...] = a.astype(BF16)
        b_ref[...] = b.astype(BF16)
        merged = (_sig(gab_ref[:, :D].astype(F32)) * a + _sig(gab_ref[:, D:].astype(F32)) * b).astype(BF16)
        merged_ref[...] = merged
        mix_ref[...] = _dot(merged, wo_ref[...])

    return pl.pallas_call(
        body, name="mix_fwd", grid=(seq // tm,),
        in_specs=[_rows(tm, 2 * G_W), _rows(tm, 2 * D), _rows(tm, Q_W), _acc((8, G_W)),
                  _resident((N_GRP * BLK, BLK)), _acc((BLK, G_W)),
                  _resident((Q_W, D)), _resident((G_W, D)), _resident((D, D))],
        out_specs=[_rows(tm, D), _rows(tm, D), _rows(tm, D), _rows(tm, D), _rows(tm, G_W)],
        out_shape=[jax.ShapeDtypeStruct((seq, D), BF16), jax.ShapeDtypeStruct((seq, D), BF16),
                   jax.ShapeDtypeStruct((seq, D), F32), jax.ShapeDtypeStruct((seq, D), BF16),
                   jax.ShapeDtypeStruct((seq, G_W), BF16)],
        compiler_params=_cp("arbitrary"),
    )(uv, gab, ya, gp, ws_stack, bias_full, w_a, w_b, w_o)


FFN_CHUNK = 512


def _ffn_chunks():
    out = []
    for hh in range(2):
        off = 0
        while off < FH_SHARD:
            w = min(FFN_CHUNK, FH_SHARD - off)
            out.append((hh, off, w))
            off += w
    return out


def _mid_recompute(x_ref, mix_ref, vec_ref):
    r1 = ALPHA * x_ref[...] + vec_ref[0:1, :] * mix_ref[...]
    xh1, rstd1 = _ln(r1)
    xmid = xh1 * vec_ref[1:2, :] + vec_ref[2:3, :]
    xh2, rstd2 = _ln(xmid)
    return xh1, rstd1, xmid, xh2, rstd2


def _ffn(x, mix, tgt, vec, w_fi, w_fo, tm):
    seq = x.shape[0]

    def body(x_ref, mix_ref, tgt_ref, vec_ref, wi_ref, wo_ref, act_ref, h2_ref, dff_ref, df_ref, dr1_ref, st_ref, gu_ref):
        @pl.when(pl.program_id(0) == 0)
        def _():
            st_ref[...] = jnp.zeros_like(st_ref)

        xh1, rstd1, xmid, xh2, rstd2 = _mid_recompute(x_ref, mix_ref, vec_ref)
        h2 = (xh2 * (1.0 + vec_ref[4:5, :]) + vec_ref[3:4, :]).astype(BF16)
        h2_ref[...] = h2
        halves = [(slice(hh * FH_SHARD, (hh + 1) * FH_SHARD), slice(FH + hh * FH_SHARD, FH + (hh + 1) * FH_SHARD))
                  for hh in range(2)]
        for hh, (cs, cu) in enumerate(halves):
            g = _dot(h2, wi_ref[hh])
            u = _dot(h2, wi_ref[2 + hh])
            gu_ref[:, cs] = g
            gu_ref[:, cu] = u
            act_ref[:, cs] = (g * _sig(g) * u).astype(BF16)
        f = _dot(act_ref[...], wo_ref[...])
        r2 = ALPHA * xmid + vec_ref[5:6, :] * f
        yh, rstd = _ln(r2)
        y = yh * vec_ref[6:7, :] + vec_ref[7:8, :]
        err = y - tgt_ref[...]
        dy = err / D
        dr2 = _ln_bwd(dy * vec_ref[6:7, :], yh, rstd)
        st_ref[0:1, :] += _colsum(err * err)
        st_ref[1:2, :] += _colsum(dy * yh)
        st_ref[2:3, :] += _colsum(dy)
        st_ref[3:4, :] += _colsum(dr2 * f)

        df = (dr2 * vec_ref[5:6, :]).astype(BF16)
        df_ref[...] = df
        da_all = _dot_nt(df, wo_ref[...])
        for cs, cu in halves:
            da = da_all[:, cs]
            g = gu_ref[:, cs]
            u = gu_ref[:, cu]
            sg = _sig(g)
            dff_ref[:, cs] = (da * u * sg * (1.0 + g * (1.0 - sg))).astype(BF16)
            dff_ref[:, cu] = (da * g * sg).astype(BF16)
        dh2 = _dot_nt(dff_ref[:, :FH_SHARD], wi_ref[0])
        for s in range(1, N_SHARD):
            dh2 = dh2 + _dot_nt(dff_ref[:, s * FH_SHARD:(s + 1) * FH_SHARD], wi_ref[s])
        dxmid = _ln_bwd(dh2 * (1.0 + vec_ref[4:5, :]), xh2, rstd2) + ALPHA * dr2
        dr1 = _ln_bwd(dxmid * vec_ref[1:2, :], xh1, rstd1)
        dr1_ref[...] = dr1
        st_ref[8:9, :] += _colsum(dh2 * xh2)
        st_ref[9:10, :] += _colsum(dh2)
        st_ref[10:11, :] += _colsum(dxmid * xh1)
        st_ref[11:12, :] += _colsum(dxmid)
        st_ref[12:13, :] += _colsum(dr1 * mix_ref[...])

    return pl.pallas_call(
        body, name="ffn", grid=(seq // tm,),
        in_specs=[_rows(tm, D), _rows(tm, D), _rows(tm, D), _acc((8, D)), _resident((N_SHARD, D, FH_SHARD)), _resident((FH, D))],
        out_specs=[_rows(tm, FH), _rows(tm, D), _rows(tm, 2 * FH), _rows(tm, D), _rows(tm, D), _acc((16, D))],
        out_shape=[jax.ShapeDtypeStruct((seq, FH), BF16), jax.ShapeDtypeStruct((seq, D), BF16),
                   jax.ShapeDtypeStruct((seq, 2 * FH), BF16), jax.ShapeDtypeStruct((seq, D), BF16),
                   jax.ShapeDtypeStruct((seq, D), F32), jax.ShapeDtypeStruct((16, D), F32)],
        scratch_shapes=[pltpu.VMEM((tm, 2 * FH), F32)],
        compiler_params=_cp("arbitrary"),
    )(x, mix, tgt, vec, w_fi, w_fo)


def _mix_bwd(dr1, a, b, gab, uv, merged, ya, yb, vec, gp, ws_stack, ws_stack_t, bias_full, w_a, w_b, w_o, tm, scatter=()):
    seq = dr1.shape[0]
    last = seq // tm - 1
    ns = len(scatter)

    def body(dr1_ref, a_ref, b_ref, gab_ref, uv_ref, mg_ref, ya_ref, yb_ref, vec_ref, gp_ref, ws_ref, wst_ref, bias_ref,
             wa_ref, wb_ref, wo_ref, *rest):
        dya_ref, dp_ref, dws_ref, dbs_ref, st_ref, gwo_ref, gwa_ref, gwb_ref = rest[ns:ns + 8]
        acc_o, acc_a, acc_b = rest[2 * ns + 8:2 * ns + 11]
        comm = _AllToAll(rest[:ns], rest[ns + 8:2 * ns + 8], *rest[2 * ns + 11:]) if ns else None
        _host_start(pl.program_id(0), comm)

        @pl.when(pl.program_id(0) == 0)
        def _():
            dws_ref[...] = jnp.zeros_like(dws_ref)
            dbs_ref[...] = jnp.zeros_like(dbs_ref)
            st_ref[...] = jnp.zeros_like(st_ref)
            acc_o[...] = jnp.zeros_like(acc_o)
            acc_a[...] = jnp.zeros_like(acc_a)
            acc_b[...] = jnp.zeros_like(acc_b)

        dmix = (dr1_ref[...] * vec_ref[0:1, :]).astype(BF16)
        acc_o[...] += _dot_tn(mg_ref[...], dmix)
        dmerged = _dot_nt(dmix, wo_ref[...])
        sa = _sig(gab_ref[:, :D].astype(F32))
        sb = _sig(gab_ref[:, D:].astype(F32))
        da = (dmerged * sa).astype(BF16)
        db = (dmerged * sb).astype(BF16)
        dp_ref[:, 2 * G_W:2 * G_W + D] = (dmerged * a_ref[...].astype(F32) * sa * (1.0 - sa)).astype(BF16)
        dp_ref[:, 2 * G_W + D:] = (dmerged * b_ref[...].astype(F32) * sb * (1.0 - sb)).astype(BF16)
        dya_ref[...] = _dot_nt(da, wa_ref[...]).astype(BF16)
        dyb = _dot_nt(db, wb_ref[...])
        acc_a[...] += _dot_tn(ya_ref[...], da)
        acc_b[...] += _dot_tn(yb_ref[...], db)

        @pl.when(pl.program_id(0) == last)
        def _():
            gwo_ref[...] = acc_o[...].astype(BF16)
            gwa_ref[...] = acc_a[...].astype(BF16)
            gwb_ref[...] = acc_b[...].astype(BF16)

        for c in range(tm // BLK):
            rs = slice(c * BLK, (c + 1) * BLK)
            u = uv_ref[rs, :G_W].astype(F32)
            vb = uv_ref[rs, G_W:].astype(F32)
            gu, tu, tv, vhat, rstd, vn, s = _gmlp_chunk(u, vb, gp_ref, ws_ref, bias_ref)
            dyb_c = dyb[rs, :]
            ds = dyb_c * gu
            du = dyb_c * s * _gelu_grad(u, tu)
            ds_b = ds.astype(BF16)
            dvn_g = []
            for g in range(N_GRP):
                cg = slice(g * GRP_D, (g + 1) * GRP_D)
                dvn_g.append(_dot(wst_ref[:, g * BLK:(g + 1) * BLK], ds_b[:, cg]))
                dws_ref[g * BLK:(g + 1) * BLK, :] += _dot_nt(ds_b[:, cg], vn[:, cg])
            dvn = jnp.concatenate(dvn_g, axis=1)
            dbs_ref[...] += ds
            st_ref[0:1, :] += _colsum(dvn * vhat)
            st_ref[1:2, :] += _colsum(dvn)
            dgv = _ln_bwd(dvn * gp_ref[0:1, :], vhat, rstd)
            dvb = dgv * _gelu_grad(vb, tv)
            dp_ref[rs, :G_W] = du.astype(BF16)
            dp_ref[rs, G_W:2 * G_W] = dvb.astype(BF16)
        _host_finish(pl.program_id(0), last, comm)

    pw = 2 * G_W + 2 * D
    out = pl.pallas_call(
        body, name="mix_bwd", grid=(seq // tm,),
        in_specs=[_rows(tm, D), _rows(tm, D), _rows(tm, D), _rows(tm, 2 * D), _rows(tm, 2 * G_W), _rows(tm, D), _rows(tm, Q_W),
                  _rows(tm, G_W), _acc((8, D)), _acc((8, G_W)),
                  _resident((N_GRP * BLK, BLK)), _resident((BLK, N_GRP * BLK)), _acc((BLK, G_W)),
                  _resident((Q_W, D)), _resident((G_W, D)), _resident((D, D))] + _comm_specs(ns),
        out_specs=[_rows(tm, Q_W), _rows(tm, pw), _acc((N_GRP * BLK, BLK)), _acc((BLK, G_W)), _acc((8, G_W)),
                   _acc((D, D)), _acc((Q_W, D)), _acc((G_W, D))] + _comm_specs(ns),
        out_shape=[jax.ShapeDtypeStruct((seq, Q_W), BF16), jax.ShapeDtypeStruct((seq, pw), BF16),
                   jax.ShapeDtypeStruct((N_GRP * BLK, BLK), F32), jax.ShapeDtypeStruct((BLK, G_W), F32),
                   jax.ShapeDtypeStruct((8, G_W), F32), jax.ShapeDtypeStruct((D, D), BF16),
                   jax.ShapeDtypeStruct((Q_W, D), BF16), jax.ShapeDtypeStruct((G_W, D), BF16)]
        + [jax.ShapeDtypeStruct(v.shape, v.dtype) for v in scatter],
        scratch_shapes=[pltpu.VMEM((D, D), F32), pltpu.VMEM((Q_W, D), F32), pltpu.VMEM((G_W, D), F32)]
        + (_comm_scratch(ns) if ns else []),
        compiler_params=_cp("arbitrary"),
    )(dr1, a, b, gab, uv, merged, ya, yb, vec, gp, ws_stack, ws_stack_t, bias_full, w_a, w_b, w_o, *scatter)
    return out[:8], out[8:]


def _attn_bwd(q, kv, kvc, sink, dya, ya, lse, scatter=()):
    seq = q.shape[0]
    nb = seq // BLK
    n_ctx = kvc.shape[0]
    ns = len(scatter)
    Q_BLOCKS = 2
    nkv = Q_BLOCKS + 2
    steps = nb // Q_BLOCKS

    def body(q_ref, *rest):
        kv_refs = rest[:nkv]
        kvc_ref, sink_ref, do_ref, o_ref, lse_ref = rest[nkv:nkv + 5]
        rest = rest[nkv + 5:]
        dq_ref, dkv_ref, dkvc_ref, dsink_ref = rest[ns:ns + 4]
        comm = _AllToAll(rest[:ns], rest[ns + 4:2 * ns + 4], *rest[2 * ns + 4:]) if ns else None
        n = pl.program_id(0)
        _host_start(n, comm)

        @pl.when(n == 0)
        def _():
            dkv_ref[...] = jnp.zeros_like(dkv_ref)
            dkvc_ref[...] = jnp.zeros_like(dkvc_ref)
            dsink_ref[...] = jnp.zeros_like(dsink_ref)

        lane = lax.broadcasted_iota(jnp.int32, (1, LANES), 1)
        for sub in range(Q_BLOCKS):
            rs = slice(sub * BLK, (sub + 1) * BLK)
            blk = Q_BLOCKS * n + sub
            q = q_ref[rs, :]
            do = do_ref[rs, :]
            out = o_ref[rs, :]
            lse_all = lse_ref[rs, :]
            k_refs = (kvc_ref,) + kv_refs[sub:sub + 3]
            dqs, dks, dvs = [], [], []
            for hk in range(N_KV):
                q4, ks, s = _attn_scores(q, k_refs, hk, blk, nb)
                vs = [r[:, KV_W + hk * HEAD:KV_W + (hk + 1) * HEAD] for r in k_refs]
                lse4 = jnp.concatenate([lse_all[:, hk * GROUP + g:hk * GROUP + g + 1] for g in range(GROUP)], axis=0)
                do4 = _stack_heads(do, hk)
                delta = jnp.sum(do4.astype(F32) * _stack_heads(out, hk).astype(F32), axis=-1, keepdims=True)
                p = [jnp.exp((t - lse4).astype(BF16)) for t in s]
                ds = [t * (_dot_nt(do4, v) - delta).astype(BF16) for t, v in zip(p, vs)]
                dq4 = _dot(ds[0], ks[0])
                for t, k in zip(ds[1:], ks[1:]):
                    dq4 = dq4 + _dot(t, k)
                dq4 = dq4 * SCALE
                dqs += [dq4[g * BLK:(g + 1) * BLK, :] for g in range(GROUP)]
                dks.append([_dot_tn(t, q4) for t in ds])
                dvs.append([_dot_tn(t, do4) for t in p])
                ps = jnp.exp(_sink_rows(sink_ref, hk) - lse4) * delta
                for g in range(GROUP):
                    part = -jnp.sum(ps[g * BLK:(g + 1) * BLK, :], axis=0, keepdims=True)
                    dsink_ref[0:1, :] += jnp.where(lane == hk * GROUP + g, part, 0.0)
            dq_ref[rs, :] = jnp.concatenate(dqs, axis=1)

            def piece(i):
                return jnp.concatenate([dks[0][i], dks[1][i], dvs[0][i], dvs[1][i]], axis=1)

            dkvc_ref[...] += piece(0)
            starts = (jnp.maximum(blk - 1, 0), blk, jnp.minimum(blk + 1, nb - 1))
            for i, st in enumerate(starts):
                r = pl.ds(pl.multiple_of(st * BLK, BLK), BLK)
                dkv_ref[r, :] += piece(i + 1)
        _host_finish(n, steps - 1, comm)

    tq = Q_BLOCKS * BLK
    out = pl.pallas_call(
        body, name="attn_bwd", grid=(steps,),
        in_specs=[_rows(tq, Q_W)] + _kv_specs(nb, Q_BLOCKS) + [_acc((n_ctx, 2 * KV_W)), pl.BlockSpec(memory_space=pltpu.SMEM),
                                                     _rows(tq, Q_W), _rows(tq, Q_W), _rows(tq, LANES)] + _comm_specs(ns),
        out_specs=[_rows(tq, Q_W), _acc((seq, 2 * KV_W)), _acc((n_ctx, 2 * KV_W)), _acc((8, LANES))] + _comm_specs(ns),
        out_shape=[jax.ShapeDtypeStruct((seq, Q_W), F32), jax.ShapeDtypeStruct((seq, 2 * KV_W), F32),
                   jax.ShapeDtypeStruct((n_ctx, 2 * KV_W), F32), jax.ShapeDtypeStruct((8, LANES), F32)]
        + [jax.ShapeDtypeStruct(v.shape, v.dtype) for v in scatter],
        scratch_shapes=_comm_scratch(ns) if ns else [],
        compiler_params=_cp("arbitrary"),
    )(q, *([kv] * nkv), kvc, sink, dya, ya, lse, *scatter)
    return out[:4], out[4:]


def _proj_bwd(dq, dkv, dpb, x, dr1, modx, w_in, cos, sin, tm, scatter=()):
    seq = x.shape[0]
    pw = IN_W - Q_W - 2 * KV_W
    ns = len(scatter)

    def body(dq_ref, dkv_ref, dpb_ref, x_ref, dr1_ref, mod_ref, w_ref, cos_ref, sin_ref, *rest):
        dqkv_ref, gx_ref, st_ref = rest[ns:ns + 3]
        comm = _AllToAll(rest[:ns], rest[ns + 3:2 * ns + 3], *rest[2 * ns + 3:]) if ns else None
        _host_start(pl.program_id(0), comm)

        @pl.when(pl.program_id(0) == 0)
        def _():
            st_ref[...] = jnp.zeros_like(st_ref)

        cos1, sin1 = cos_ref[...], sin_ref[...]
        cos2 = jnp.concatenate([cos1, cos1], axis=1)
        sin2 = jnp.concatenate([sin1, sin1], axis=1)
        for j in range(Q_W // 256):
            cs = slice(256 * j, 256 * (j + 1))
            dqkv_ref[:, cs] = _unrope(dq_ref[:, cs], cos2, sin2).astype(BF16)
        dqkv_ref[:, Q_W:Q_W + KV_W] = _unrope(dkv_ref[:, :KV_W], cos1, sin1).astype(BF16)
        dqkv_ref[:, Q_W + KV_W:] = dkv_ref[:, KV_W:].astype(BF16)
        o = Q_W + 2 * KV_W
        dh = _dot(dqkv_ref[...], w_ref[:o, :]) + _dot(dpb_ref[...], w_ref[o:, :])
        xhat, rstd = _ln(x_ref[...])
        st_ref[0:1, :] += _colsum(dh)
        st_ref[1:2, :] += _colsum(dh * xhat)
        gx_ref[...] = _ln_bwd(dh * (1.0 + mod_ref[1:2, :]), xhat, rstd) + ALPHA * dr1_ref[...]
        _host_finish(pl.program_id(0), seq // tm - 1, comm)

    out = pl.pallas_call(
        body, name="proj_bwd", grid=(seq // tm,),
        in_specs=[_rows(tm, Q_W), _rows(tm, 2 * KV_W), _rows(tm, pw), _rows(tm, D), _rows(tm, D), _acc((8, D)),
                  _resident((IN_W, D)), _rows(tm, LANES), _rows(tm, LANES)] + _comm_specs(ns),
        out_specs=[_rows(tm, Q_W + 2 * KV_W), _rows(tm, D), _acc((8, D))] + _comm_specs(ns),
        out_shape=[jax.ShapeDtypeStruct((seq, Q_W + 2 * KV_W), BF16), jax.ShapeDtypeStruct((seq, D), F32),
                   jax.ShapeDtypeStruct((8, D), F32)] + [jax.ShapeDtypeStruct(v.shape, v.dtype) for v in scatter],
        scratch_shapes=_comm_scratch(ns) if ns else [],
        compiler_params=_cp("arbitrary"),
    )(dq, dkv, dpb, x, dr1, modx, w_in, cos, sin, *scatter)
    return out[:3], out[3:]


def _ctx_bwd(dkvc, ctx, hc, w_kv):
    n_ctx = ctx.shape[0]

    def body(dkvc_ref, ctx_ref, hc_ref, w_ref, dw_ref, st_ref):
        d = dkvc_ref[...].astype(BF16)
        dw_ref[...] = _dot_tn(d, hc_ref[...])
        dhc = _dot(d, w_ref[...])
        xhat, _ = _ln(ctx_ref[...])
        st_ref[...] = jnp.zeros_like(st_ref)
        st_ref[0:1, :] = _colsum(dhc)
        st_ref[1:2, :] = _colsum(dhc * xhat)

    return pl.pallas_call(
        body, name="ctx_bwd", grid=(1,),
        in_specs=[_acc((n_ctx, 2 * KV_W)), _acc((n_ctx, D)), _acc((n_ctx, D)), _acc((2 * KV_W, D))],
        out_specs=[_acc((2 * KV_W, D)), _acc((8, D))],
        out_shape=[jax.ShapeDtypeStruct((2 * KV_W, D), F32), jax.ShapeDtypeStruct((8, D), F32)],
        compiler_params=_cp("arbitrary"),
    )(dkvc, ctx, hc, w_kv)


def _tn_matmul(a, b, tn, name, out_dtype, shard_major=False, init=None, tk=512, scatter=(), gather=()):
    t, ka = a.shape
    n = b.shape[1]
    tk = min(tk, t)
    nk = t // tk
    nj = n // tn
    has_init = init is not None
    assert not (scatter and gather)
    moved = list(scatter) + list(gather)
    pattern = _AllToAll if scatter else _Gather
    ns = len(moved)
    n_in = 3 if has_init else 2

    def body(*refs):
        a_ref, b_ref = refs[:2]
        i_ref = refs[2] if has_init else None
        rest = refs[n_in:]
        o_ref = rest[ns]
        acc_ref = rest[2 * ns + 1]
        comm = pattern(rest[:ns], rest[ns + 1:2 * ns + 1], *rest[2 * ns + 2:]) if ns else None
        k = pl.program_id(1)
        step = pl.program_id(0) * nk + k
        _host_start(step, comm)

        @pl.when(k == 0)
        def _():
            acc_ref[...] = i_ref[...] if has_init else jnp.zeros_like(acc_ref)

        acc_ref[...] += _dot_tn(a_ref[...], b_ref[...])

        @pl.when(k == nk - 1)
        def _():
            o_ref[...] = acc_ref[...].astype(out_dtype)

        _host_finish(step, nj * nk - 1, comm)

    in_specs = [pl.BlockSpec((tk, ka), lambda j, k: (k, 0)), pl.BlockSpec((tk, tn), lambda j, k: (k, j))]
    args = [a, b]
    if has_init:
        in_specs.append(pl.BlockSpec((ka, tn), lambda j, k: (0, j)))
        args.append(init)
    if shard_major:
        out_spec = pl.BlockSpec((None, ka, tn), lambda j, k: (j, 0, 0))
        out_shape = jax.ShapeDtypeStruct((nj, ka, tn), out_dtype)
    else:
        out_spec = pl.BlockSpec((ka, tn), lambda j, k: (0, j))
        out_shape = jax.ShapeDtypeStruct((ka, n), out_dtype)
    out = pl.pallas_call(
        body, name=name, grid=(nj, nk), in_specs=in_specs + _comm_specs(ns), out_specs=[out_spec] + _comm_specs(ns),
        out_shape=[out_shape] + [jax.ShapeDtypeStruct(v.shape, v.dtype) for v in scatter] + _gathered_shapes(gather),
        scratch_shapes=[pltpu.VMEM((ka, tn), F32)] + (_comm_scratch(ns) if ns else []),
        compiler_params=_cp("arbitrary", "arbitrary"),
    )(*args, *moved)
    return (out[0], out[1:]) if ns else out[0]


ADA_TILE = 512


def _ada_fwd(sc_all, w_ada):
    cs = w_ada.shape[1]

    def body(s_ref, w_ref, o_ref):
        o_ref[...] = _dot(s_ref[...].astype(BF16), w_ref[...].astype(BF16))

    return pl.pallas_call(
        body, name="ada_fwd", grid=(cs // ADA_TILE,),
        in_specs=[_acc((16, D)), pl.BlockSpec((D, ADA_TILE), lambda j: (0, j))],
        out_specs=pl.BlockSpec((16, ADA_TILE), lambda j: (0, j)),
        out_shape=jax.ShapeDtypeStruct((16, cs), F32),
        compiler_params=_cp("arbitrary"),
    )(sc_all, w_ada)


def _ada_bwd(sc_all_t, dm_all, dmc, w_ada):
    cs = w_ada.shape[1]

    def body(st_ref, dm_ref, dmc_ref, w_ref, gw_ref, part_ref):
        @pl.when(pl.program_id(0) == 0)
        def _():
            part_ref[...] = jnp.zeros_like(part_ref)

        gw_ref[...] = _dot(st_ref[...].astype(BF16), dm_ref[...].astype(BF16))
        part_ref[...] += _dot_nt(dmc_ref[...].astype(BF16), w_ref[...].astype(BF16))

    return pl.pallas_call(
        body, name="ada_bwd", grid=(cs // ADA_TILE,),
        in_specs=[_acc((D, 16)), pl.BlockSpec((16, ADA_TILE), lambda j: (0, j)), pl.BlockSpec((8, ADA_TILE), lambda j: (0, j)),
                  pl.BlockSpec((D, ADA_TILE), lambda j: (0, j))],
        out_specs=[pl.BlockSpec((D, ADA_TILE), lambda j: (0, j)), _acc((8, D))],
        out_shape=[jax.ShapeDtypeStruct((D, cs), F32), jax.ShapeDtypeStruct((8, D), F32)],
        compiler_params=_cp("arbitrary"),
    )(sc_all_t, dm_all, dmc, w_ada)


def _sum8(x, name, tr=256):
    _, r, c = x.shape
    tr = min(tr, r)
    while r % tr:
        tr -= 16

    def body(x_ref, o_ref):
        acc = x_ref[0].astype(F32)
        for i in range(1, N_DEV):
            acc = acc + x_ref[i].astype(F32)
        o_ref[...] = acc

    return pl.pallas_call(
        body, name=name, grid=(r // tr,),
        in_specs=[pl.BlockSpec((N_DEV, tr, c), lambda i: (0, i, 0))],
        out_specs=pl.BlockSpec((tr, c), lambda i: (i, 0)),
        out_shape=jax.ShapeDtypeStruct((r, c), F32),
        compiler_params=_cp("arbitrary"),
    )(x)


def _sum_blocks(recv, src, me, name, tr=256):
    _, r, c = recv.shape
    tr = min(tr, r)
    while r % tr:
        tr -= 16

    def body(me_ref, recv_ref, own_ref, o_ref):
        acc = own_ref[...].astype(F32)
        for k in range(1, N_DEV):
            acc = acc + recv_ref[me_ref[0] ^ k].astype(F32)
        o_ref[...] = acc

    return pl.pallas_call(
        body, name=name,
        grid_spec=pltpu.PrefetchScalarGridSpec(
            num_scalar_prefetch=1, grid=(r // tr,),
            in_specs=[pl.BlockSpec((N_DEV, tr, c), lambda i, me_ref: (0, i, 0)),
                      pl.BlockSpec((None, tr, c), lambda i, me_ref: (me_ref[0], i, 0))],
            out_specs=pl.BlockSpec((tr, c), lambda i, me_ref: (i, 0))),
        out_shape=jax.ShapeDtypeStruct((r, c), F32),
        compiler_params=_cp("arbitrary"),
    )(me, recv, src)


def _sum8_many(xs, name):
    n = len(xs)

    def body(*refs):
        for x_ref, o_ref in zip(refs[:n], refs[n:]):
            acc = x_ref[0]
            for i in range(1, N_DEV):
                acc = acc + x_ref[i]
            o_ref[...] = acc

    vmem = pl.BlockSpec(memory_space=pltpu.VMEM)
    return pl.pallas_call(
        body, name=name, in_specs=[vmem] * n, out_specs=[vmem] * n,
        out_shape=[jax.ShapeDtypeStruct(v.shape[1:], v.dtype) for v in xs],
        compiler_params=pltpu.CompilerParams(vmem_limit_bytes=VMEM_LIMIT),
    )(*xs)


def _adam_update(w, g, m, v):
    nm = ADAM_B1 * m + (1.0 - ADAM_B1) * g
    nv = ADAM_B2 * v + (1.0 - ADAM_B2) * (g * g)
    m_hat = nm / (1.0 - ADAM_B1 ** ADAM_STEP)
    v_hat = nv / (1.0 - ADAM_B2 ** ADAM_STEP)
    return -ADAM_LR * (m_hat / (jnp.sqrt(v_hat) + ADAM_EPS) + ADAM_WD * w), nm, nv


ROW_LOSS, ROW_LN2_G, ROW_LN2_B, ROW_LN1_G, ROW_LN1_B = 0, 1, 2, 10, 11
ROWS_DMOD_X = (16, 17, 12, 9, 8, 3)
ROWS_DMOD_C = (24, 25)
SMALL = ("c_ctx", "b_ada", "attn_sink", "gmlp_ln_g", "gmlp_ln_b", "w_spatial", "b_spatial", "ln1_g", "ln1_b", "ln2_g", "ln2_b")


def _adamw_small(sums, dsc, w, m, v):
    n = len(SMALL)

    def body(*refs):
        st_ref, gm_ref, sk_ref, ws_ref, bs_ref, dsc_ref = refs[:6]
        w_refs = dict(zip(SMALL, refs[6:6 + n]))
        m_refs = dict(zip(SMALL, refs[6 + n:6 + 2 * n]))
        v_refs = dict(zip(SMALL, refs[6 + 2 * n:6 + 3 * n]))
        outs = refs[6 + 3 * n:]
        c = w_refs["c_ctx"][...]
        sg = _sig(c)
        dmod = [st_ref[r:r + 1, :] for r in ROWS_DMOD_X]
        dmod[0] = dmod[0] + st_ref[ROWS_DMOD_C[0]:ROWS_DMOD_C[0] + 1, :]
        dmod[1] = dmod[1] + st_ref[ROWS_DMOD_C[1]:ROWS_DMOD_C[1] + 1, :]
        grads = dict(
            c_ctx=dsc_ref[0:1, :] * (sg * (1.0 + c * (1.0 - sg))),
            b_ada=jnp.concatenate(dmod, axis=1),
            attn_sink=sk_ref[0:1, 0:N_KV * GROUP],
            gmlp_ln_g=gm_ref[0:1, :], gmlp_ln_b=gm_ref[1:2, :],
            w_spatial=ws_ref[...], b_spatial=bs_ref[...],
            ln1_g=st_ref[ROW_LN1_G:ROW_LN1_G + 1, :], ln1_b=st_ref[ROW_LN1_B:ROW_LN1_B + 1, :],
            ln2_g=st_ref[ROW_LN2_G:ROW_LN2_G + 1, :], ln2_b=st_ref[ROW_LN2_B:ROW_LN2_B + 1, :])
        for i, name in enumerate(SMALL):
            g = grads[name]
            d, nm, nv = _adam_update(w_refs[name][...], g, m_refs[name][...], v_refs[name][...])
            outs[i][...] = g
            outs[n + i][...] = d
            outs[2 * n + i][...] = nm
            outs[3 * n + i][...] = nv

    vmem = pl.BlockSpec(memory_space=pltpu.VMEM)
    args = list(sums) + [dsc] + [w[k] for k in SMALL] + [m[k] for k in SMALL] + [v[k] for k in SMALL]
    shapes = [jax.ShapeDtypeStruct(w[k].shape, F32) for k in SMALL]
    out = pl.pallas_call(
        body, name="adamw_small", in_specs=[vmem] * len(args), out_specs=[vmem] * (4 * n), out_shape=shapes * 4,
        compiler_params=pltpu.CompilerParams(vmem_limit_bytes=VMEM_LIMIT),
    )(*args)
    return [dict(zip(SMALL, out[i * n:(i + 1) * n])) for i in range(4)]


def _adamw_halves(w, mine, theirs, m, v, c_arr, name):
    r, c = w.shape
    tr = min(256, r // 2)
    while (r // 2) % tr:
        tr -= 8
    nt = (r // 2) // tr

    def body(c_ref, w_ref, mine_ref, theirs_ref, m_ref, v_ref, g_ref, d_ref, nm_ref, nv_ref):
        g = jnp.where(pl.program_id(0) == c_ref[0], mine_ref[...], theirs_ref[...])
        g_ref[...] = g
        d_ref[...], nm_ref[...], nv_ref[...] = _adam_update(w_ref[...], g, m_ref[...], v_ref[...])

    whole = pl.BlockSpec((tr, c), lambda hb, i, c_ref: (hb * nt + i, 0))
    half = pl.BlockSpec((tr, c), lambda hb, i, c_ref: (i, 0))
    shp = jax.ShapeDtypeStruct((r, c), F32)
    return pl.pallas_call(
        body, name=name,
        grid_spec=pltpu.PrefetchScalarGridSpec(
            num_scalar_prefetch=1, grid=(2, nt), in_specs=[whole, half, half, whole, whole], out_specs=[whole] * 4),
        out_shape=[shp] * 4,
        compiler_params=_cp("arbitrary", "arbitrary"),
    )(c_arr, w, mine, theirs, m, v)


def _adamw(w, g, m, v, name):
    r, c = w.shape
    tr = r if r * c <= 256 * 1024 else min(256, r)
    while r % tr:
        tr -= 8

    def body(w_ref, g_ref, m_ref, v_ref, d_ref, nm_ref, nv_ref):
        d_ref[...], nm_ref[...], nv_ref[...] = _adam_update(w_ref[...], g_ref[...], m_ref[...], v_ref[...])

    spec = pl.BlockSpec((tr, c), lambda i: (i, 0))
    shp = jax.ShapeDtypeStruct((r, c), F32)
    return pl.pallas_call(
        body, name=name, grid=(r // tr,), in_specs=[spec] * 4, out_specs=[spec] * 3, out_shape=[shp] * 3,
        compiler_params=_cp("arbitrary"),
    )(w, g, m, v)


def _my_pos():
    return lax.axis_index("x"), lax.axis_index("y"), lax.axis_index("c")


N_COPY = 7


class _Gather:
    def __init__(self, x_refs, out_refs, send_sems, recv_sems):
        self.x_refs, self.out_refs = x_refs, out_refs
        self.send_sems, self.recv_sems = send_sems, recv_sems
        x, y, c = _my_pos()
        self.c = c
        self.me, self.sibling = (x, y, c), (x, y, 1 - c)
        self.chips = [(1 - x, y), (x, 1 - y), (1 - x, 1 - y)]

    def _copy(self, a, k, block, to, from_input=False):
        px, py, pc = block
        rows = self.out_refs[a].at[4 * px + 2 * py + pc]
        return pltpu.make_async_remote_copy(
            src_ref=self.x_refs[a] if from_input else rows, dst_ref=rows,
            send_sem=self.send_sems.at[a * N_COPY + k], recv_sem=self.recv_sems.at[a * N_COPY + k],
            device_id=to, device_id_type=MESH)

    def start(self):
        n = len(self.x_refs)
        for a in range(n):
            self._copy(a, 0, self.me, self.sibling, from_input=True).start()
        for j, chip in enumerate(self.chips):
            for a in range(n):
                self._copy(a, 1 + j, self.me, (*chip, self.c), from_input=True).start()

    def finish(self):
        n = len(self.x_refs)
        c = self.c
        for j, chip in enumerate(self.chips):
            for a in range(n):
                self._copy(a, 1 + j, (*chip, c), self.me).wait_recv()
                self._copy(a, 4 + j, (*chip, c), self.sibling).start()
        for a in range(n):
            self._copy(a, 0, self.sibling, self.me).wait_recv()
        for j, chip in enumerate(self.chips):
            for a in range(n):
                self._copy(a, 4 + j, (*chip, 1 - c), self.me).wait_recv()
        for a in range(n):
            self._copy(a, 0, self.me, self.sibling, from_input=True).wait_send()
            for j, chip in enumerate(self.chips):
                self._copy(a, 1 + j, self.me, (*chip, c), from_input=True).wait_send()
                self._copy(a, 4 + j, (*chip, c), self.sibling).wait_send()


def _comm_scratch(n):
    return [pltpu.SemaphoreType.DMA((n * N_COPY,)), pltpu.SemaphoreType.DMA((n * N_COPY,))]


def _comm_specs(n):
    return [pl.BlockSpec(memory_space=pl.ANY)] * n


def _gathered_shapes(xs):
    return [jax.ShapeDtypeStruct((N_DEV,) + v.shape, v.dtype) for v in xs]


def _with_own(gathered, xs, me):
    return [lax.dynamic_update_index_in_dim(g, v, me, 0) for g, v in zip(gathered, xs)]


def _all_gather(xs, me, name):
    n = len(xs)

    def body(*refs):
        g = _Gather(refs[:n], refs[n:2 * n], *refs[2 * n:])
        g.start()
        g.finish()

    out = pl.pallas_call(
        body, name=name, out_shape=_gathered_shapes(xs), in_specs=_comm_specs(n), out_specs=_comm_specs(n),
        scratch_shapes=_comm_scratch(n),
    )(*xs)
    return _with_own(out, xs, me)


class _AllToAll:
    def __init__(self, x_refs, out_refs, send_sems, recv_sems):
        self.x_refs, self.out_refs = x_refs, out_refs
        self.send_sems, self.recv_sems = send_sems, recv_sems
        self.pos = _my_pos()
        x, y, c = self.pos
        self.me = 4 * x + 2 * y + c

    def _peer(self, k):
        x, y, c = self.pos
        return (x ^ ((k >> 2) & 1), y ^ ((k >> 1) & 1), c ^ (k & 1))

    def _copy(self, a, k):
        p = self._peer(k)
        return pltpu.make_async_remote_copy(
            src_ref=self.x_refs[a].at[4 * p[0] + 2 * p[1] + p[2]], dst_ref=self.out_refs[a].at[self.me],
            send_sem=self.send_sems.at[a * N_COPY + k - 1], recv_sem=self.recv_sems.at[a * N_COPY + k - 1],
            device_id=p, device_id_type=MESH)

    def start(self):
        for k in range(1, N_DEV):
            for a in range(len(self.x_refs)):
                self._copy(a, k).start()

    def finish(self):
        for a in range(len(self.x_refs)):
            for k in range(1, N_DEV):
                self._copy(a, k).wait_recv()
            for k in range(1, N_DEV):
                self._copy(a, k).wait_send()


def _all_to_all(blocks, name):
    n = len(blocks)

    def body(*refs):
        t = _AllToAll(refs[:n], refs[n:2 * n], *refs[2 * n:])
        t.start()
        t.finish()

    return pl.pallas_call(
        body, name=name, out_shape=[jax.ShapeDtypeStruct(v.shape, v.dtype) for v in blocks],
        in_specs=_comm_specs(n), out_specs=_comm_specs(n), scratch_shapes=_comm_scratch(n),
    )(*blocks)


def _sibling_exchange(xs, name):
    n = len(xs)

    def body(*refs):
        x_refs, out_refs = refs[:n], refs[n:2 * n]
        send_sems, recv_sems = refs[2 * n:]
        x, y, c = _my_pos()

        def push(a):
            return pltpu.make_async_remote_copy(
                src_ref=x_refs[a], dst_ref=out_refs[a], send_sem=send_sems.at[a], recv_sem=recv_sems.at[a],
                device_id=(x, y, 1 - c), device_id_type=MESH)

        for a in range(n):
            push(a).start()
        for a in range(n):
            push(a).wait_recv()
            push(a).wait_send()

    return pl.pallas_call(
        body, name=name, out_shape=[jax.ShapeDtypeStruct(v.shape, v.dtype) for v in xs],
        in_specs=_comm_specs(n), out_specs=_comm_specs(n),
        scratch_shapes=[pltpu.SemaphoreType.DMA((n,)), pltpu.SemaphoreType.DMA((n,))],
    )(*xs)


def _scatter_and_gather(scatter, gather, name):
    ns, ng = len(scatter), len(gather)

    def body(*refs):
        s_in, g_in = refs[:ns], refs[ns:ns + ng]
        s_out, g_out = refs[ns + ng:2 * ns + ng], refs[2 * ns + ng:2 * (ns + ng)]
        s_send, s_recv, g_send, g_recv = refs[2 * (ns + ng):]
        g = _Gather(g_in, g_out, g_send, g_recv)
        t = _AllToAll(s_in, s_out, s_send, s_recv)
        g.start()
        t.start()
        g.finish()
        t.finish()

    out = pl.pallas_call(
        body, name=name,
        out_shape=[jax.ShapeDtypeStruct(v.shape, v.dtype) for v in scatter] + _gathered_shapes(gather),
        in_specs=_comm_specs(ns + ng), out_specs=_comm_specs(ns + ng),
        scratch_shapes=_comm_scratch(ns) + _comm_scratch(ng),
    )(*scatter, *gather)
    return out[:ns], out[ns:]


def _row_tile(seq, want):
    return min(want, seq)


def _local_step(x, ctx, tgt, mod_x, mod_c, wb, sink, gmlp_g, gmlp_b, w_s, b_s, ln1_g, ln1_b, ln2_g, ln2_b,
                later=None, me=None):
    seq = x.shape[0]
    on_mesh = me is not None
    modx1 = jnp.concatenate([mod_x[0:2], jnp.zeros((6, D), F32)], axis=0)
    modc = jnp.concatenate([mod_c[0:2], jnp.zeros((6, D), F32)], axis=0)
    vec = jnp.concatenate([mod_x[2:3], ln1_g, ln1_b, mod_x[3:6], ln2_g, ln2_b], axis=0)
    gp = jnp.concatenate([gmlp_g, gmlp_b, jnp.zeros((6, G_W), F32)], axis=0)
    ws_stack = w_s.reshape(N_GRP * BLK, BLK).astype(BF16)
    ws_stack_t = jnp.transpose(w_s, (2, 0, 1)).reshape(BLK, N_GRP * BLK).astype(BF16)
    bias_full = jnp.repeat(b_s.T, GRP_D, axis=1)
    cos, sin = _rope_tables(seq)
    w_in = wb["w_in"]
    w_kv = w_in[Q_W:Q_W + 2 * KV_W, :]
    tm_big = _row_tile(seq, 512)
    tm_ffn = _row_tile(seq, 256)

    hc, kvc, vac = _ctx_fwd(ctx, modc, w_kv)
    behind_proj = ("w_a", "w_b", "w_o") if on_mesh else ()
    behind_attn = ("w_fi", "w_fo") if on_mesh else ()
    (h, q, kv, va, uv, gab), got_proj = _proj_fwd(x, modx1, w_in, cos, sin, tm_big, gather=[later[n] for n in behind_proj])
    (ya, lse), got_attn = _attn_fwd(q, kv, va, kvc, vac, sink, gather=[later[n] for n in behind_attn])
    if on_mesh:
        wb = dict(wb)
        names = behind_proj + behind_attn
        for n, g in zip(names, _with_own(list(got_proj) + list(got_attn), [later[n] for n in names], me)):
            wb[n] = g.reshape(-1, g.shape[2]) if n in ROW_SHARDED else g.reshape(N_SHARD, 2 * g.shape[1], g.shape[2])
        for n in ("w_a", "w_b"):
            wb[n] = wb[n].transpose(1, 0, 2).reshape(wb[n].shape[1], D)
    a, b, mix, merged, yb = _mix_fwd(uv, gab, ya, gp, ws_stack, bias_full, wb["w_a"], wb["w_b"], wb["w_o"], tm_big)
    act, h2, dff, df, dr1, st_ffn = _ffn(x, mix, tgt, vec, wb["w_fi"], wb["w_fo"], tm_ffn)
    blocks, recv = {}, {}
    blocks["w_fo"] = _eighths(_tn_matmul(act, df, 512, "tn_w_ffn_out", BF16, tk=2048))
    if on_mesh:
        g_w_fi, (recv["w_fo"],) = _tn_matmul(h2, dff, FH_SHARD, "tn_w_ffn_in", BF16, shard_major=True, tk=2048,
                                            scatter=[blocks["w_fo"]])
    else:
        g_w_fi = _tn_matmul(h2, dff, FH_SHARD, "tn_w_ffn_in", BF16, shard_major=True, tk=2048)
    blocks["w_fi"] = _eighths(g_w_fi)
    (dya, dpb, dws, dbs_full, st4, g_w_o, g_w_a, g_w_b), got = _mix_bwd(
        dr1, a, b, gab, uv, merged, ya, yb, vec, gp, ws_stack, ws_stack_t, bias_full, wb["w_a"], wb["w_b"], wb["w_o"],
        tm_big, scatter=[blocks["w_fi"]] if on_mesh else ())
    recv.update(zip(("w_fi",), got))
    shard_major = [g.reshape(g.shape[0], N_SHARD, D // N_SHARD).transpose(1, 0, 2) for g in (g_w_a, g_w_b)]
    blocks.update(w_o=_eighths(g_w_o), w_a=_eighths(shard_major[0]), w_b=_eighths(shard_major[1]))
    mixer = ("w_o", "w_a", "w_b") if on_mesh else ()
    (dq, dkv, dkvc, dsink), got = _attn_bwd(q, kv, kvc, sink, dya, ya, lse, scatter=[blocks[n] for n in mixer])
    recv.update(zip(mixer, got))
    g_wkv_ctx, st0 = _ctx_bwd(dkvc, ctx, hc, w_kv)
    (dqkv, grad_x, st1), _ = _proj_bwd(dq, dkv, dpb, x, dr1, modx1, w_in, cos, sin, tm_big)
    dbs = jnp.sum(dbs_full.reshape(BLK, N_GRP, GRP_D), axis=2).T
    early = [jnp.concatenate([st_ffn, st0], axis=0), st4, dsink, dws, dbs]
    init = jnp.pad(g_wkv_ctx, ((Q_W, 0), (0, 0)))
    g_qkv = _tn_matmul(dqkv, h, D, "tn_w_in_qkv", BF16, init=init, tk=1024)
    if on_mesh:
        g_rest, early_gathered = _tn_matmul(dpb, h, D, "tn_w_in_rest", BF16, tk=1024, gather=early)
    else:
        g_rest, early_gathered = _tn_matmul(dpb, h, D, "tn_w_in_rest", BF16, tk=1024), None
    blocks["w_in"] = _eighths(jnp.concatenate([g_qkv, g_rest], axis=0))
    return grad_x, dict(early=early, early_gathered=early_gathered, late=st1), blocks, recv


BIG = ("w_in", "w_a", "w_b", "w_o", "w_fi", "w_fo")
ROW_SHARDED = ("w_o", "w_fo")


def _half_of_shard(shard, c):
    r = shard.shape[0]
    return lax.dynamic_slice_in_dim(shard, c * (r // 2), r // 2, axis=0)


def _eighths(v):
    rows = v.shape[-2] * (v.shape[0] if v.ndim == 3 else 1)
    return v.reshape(N_DEV, rows // N_DEV, v.shape[-1])


def kernel(x, c, ctx, c_ctx, w_ada, b_ada, w_in, attn_sink, gmlp_ln_g, gmlp_ln_b, w_spatial, b_spatial, w_branch_a, w_branch_b, w_out, ln1_g, ln1_b, w_ffn_in, w_ffn_out, ln2_g, ln2_b, loss_target, m_c_ctx, m_w_ada, m_b_ada, m_w_in, m_attn_sink, m_gmlp_ln_g, m_gmlp_ln_b, m_w_spatial, m_b_spatial, m_w_branch_a, m_w_branch_b, m_w_out, m_ln1_g, m_ln1_b, m_w_ffn_in, m_w_ffn_out, m_ln2_g, m_ln2_b, v_c_ctx, v_w_ada, v_b_ada, v_w_in, v_attn_sink, v_gmlp_ln_g, v_gmlp_ln_b, v_w_spatial, v_b_spatial, v_w_branch_a, v_w_branch_b, v_w_out, v_ln1_g, v_ln1_b, v_w_ffn_in, v_w_ffn_out, v_ln2_g, v_ln2_b):
    mx, my, mc = _my_pos()
    me = 4 * mx + 2 * my + mc
    chip = 2 * mx + my
    shards = dict(w_in=w_in[0].T, w_a=w_branch_a[0], w_b=w_branch_b[0], w_o=w_out[0], w_fi=w_ffn_in[0], w_fo=w_ffn_out[0])

    halves = {n: _half_of_shard(shards[n], mc).astype(BF16) for n in BIG}
    c_rows = jnp.concatenate([c, jnp.zeros((7, D), F32)], axis=0)
    g_in, c_g = _all_gather([halves["w_in"], c_rows], me, "gather_w_in")
    wb = dict(w_in=g_in.reshape(IN_W, D))

    c_all = c_g[:, 0, :]
    cc = jnp.concatenate([c_all, c_ctx[None, :], jnp.zeros((7, D), F32)], axis=0)
    sig_cc = jax.nn.sigmoid(cc)
    sc_all = cc * sig_cc
    mod_shard = _ada_fwd(sc_all, w_ada[0])
    mod_g = _all_gather([mod_shard], me, "gather_mod")[0]
    mod_all = jnp.concatenate([mod_g[2 * s] for s in range(4)], axis=1) + b_ada
    mod_x = lax.dynamic_slice_in_dim(mod_all, me, 1, axis=0).reshape(6, D)
    mod_c = mod_all[8].reshape(6, D)[0:2]

    grad_x, small, blocks, recv = _local_step(
        x[0], ctx[0], loss_target[0], mod_x, mod_c, wb, attn_sink, gmlp_ln_g, gmlp_ln_b, w_spatial[0], b_spatial[0],
        ln1_g, ln1_b, ln2_g, ln2_b, later=halves, me=me)

    (recv["w_in"],), late = _scatter_and_gather([blocks["w_in"]], [small["late"]], "scatter_w_in_gather_small")
    late = _with_own(late, [small["late"]], me)[0]
    gathered = _with_own(small["early_gathered"], small["early"], me)
    gathered[0] = jnp.concatenate([gathered[0][:, :16], late, gathered[0][:, 16:]], axis=1)

    me_arr = jnp.reshape(me, (1,)).astype(jnp.int32)
    summed = {n: _sum_blocks(recv[n], blocks[n], me_arr, "sum_grads_" + n) for n in BIG}
    theirs = dict(zip(BIG, _sibling_exchange([summed[n] for n in BIG], "exchange_grads")))

    sums = _sum8_many(gathered, "sum_small")
    stats = sums[0]
    loss = 0.5 * jnp.sum(stats[ROW_LOSS]) / D
    dmod_x_all = jnp.concatenate([gathered[0][:, r_, :] for r_ in ROWS_DMOD_X], axis=1)
    dmod_c_full = jnp.concatenate([stats[r_] for r_ in ROWS_DMOD_C] + [jnp.zeros((4 * D,), F32)])
    dm_rows = jnp.concatenate([dmod_x_all, dmod_c_full[None, :], jnp.zeros((7, 6 * D), F32)], axis=0)
    cs = w_ada.shape[2]
    dm_shard = lax.dynamic_slice_in_dim(dm_rows, chip * cs, cs, axis=1)
    dmc_shard = jnp.concatenate([dm_shard[8:9], jnp.zeros((7, cs), F32)], axis=0)
    g_w_ada, part = _ada_bwd(sc_all.T, dm_shard, dmc_shard, w_ada[0])
    part_all = _all_gather([part * (mc == 0).astype(F32)], me, "gather_c_ctx")[0]
    dsc = _sum8(part_all, "sum_c_ctx")

    grads = dict(w_ada=g_w_ada[None])
    weights = dict(c_ctx=c_ctx, w_ada=w_ada, b_ada=b_ada, w_in=w_in, attn_sink=attn_sink, gmlp_ln_g=gmlp_ln_g,
                   gmlp_ln_b=gmlp_ln_b, w_spatial=w_spatial, b_spatial=b_spatial, w_branch_a=w_branch_a,
                   w_branch_b=w_branch_b, w_out=w_out, ln1_g=ln1_g, ln1_b=ln1_b, w_ffn_in=w_ffn_in, w_ffn_out=w_ffn_out,
                   ln2_g=ln2_g, ln2_b=ln2_b)
    ms = dict(c_ctx=m_c_ctx, w_ada=m_w_ada, b_ada=m_b_ada, w_in=m_w_in, attn_sink=m_attn_sink, gmlp_ln_g=m_gmlp_ln_g,
              gmlp_ln_b=m_gmlp_ln_b, w_spatial=m_w_spatial, b_spatial=m_b_spatial, w_branch_a=m_w_branch_a,
              w_branch_b=m_w_branch_b, w_out=m_w_out, ln1_g=m_ln1_g, ln1_b=m_ln1_b, w_ffn_in=m_w_ffn_in,
              w_ffn_out=m_w_ffn_out, ln2_g=m_ln2_g, ln2_b=m_ln2_b)
    vs = dict(c_ctx=v_c_ctx, w_ada=v_w_ada, b_ada=v_b_ada, w_in=v_w_in, attn_sink=v_attn_sink, gmlp_ln_g=v_gmlp_ln_g,
              gmlp_ln_b=v_gmlp_ln_b, w_spatial=v_w_spatial, b_spatial=v_b_spatial, w_branch_a=v_w_branch_a,
              w_branch_b=v_w_branch_b, w_out=v_w_out, ln1_g=v_ln1_g, ln1_b=v_ln1_b, w_ffn_in=v_w_ffn_in,
              w_ffn_out=v_w_ffn_out, ln2_g=v_ln2_g, ln2_b=v_ln2_b)
    order = list(weights)
    delta, new_m, new_v = {}, {}, {}
    d_, m_, v_ = _adamw(w_ada[0], g_w_ada, m_w_ada[0], v_w_ada[0], "adamw_w_ada")
    delta["w_ada"], new_m["w_ada"], new_v["w_ada"] = d_[None], m_[None], v_[None]
    c_arr = jnp.reshape(mc, (1,)).astype(jnp.int32)
    names = dict(w_in="w_in", w_a="w_branch_a", w_b="w_branch_b", w_o="w_out", w_fi="w_ffn_in", w_fo="w_ffn_out")
    for k, n in names.items():
        flip = (lambda t: t.T) if k == "w_in" else (lambda t: t)
        outs = _adamw_halves(flip(weights[n][0]), summed[k], theirs[k], flip(ms[n][0]), flip(vs[n][0]), c_arr, "adamw_" + n)
        grads[n], delta[n], new_m[n], new_v[n] = [flip(t)[None] for t in outs]

    def view(a):
        return a.reshape(-1, a.shape[-1]) if a.ndim != 1 else a.reshape(1, -1)

    small = _adamw_small(sums, dsc, *[{n: view(d[n]) for n in SMALL} for d in (weights, ms, vs)])
    for out, src in zip((grads, delta, new_m, new_v), small):
        for n in SMALL:
            out[n] = src[n].reshape(weights[n].shape)

    return (loss, grad_x[None], *[grads[n] for n in order], *[delta[n] for n in order],
            *[new_m[n] for n in order], *[new_v[n] for n in order])
```

```python
import functools
import math

import jax
import jax.numpy as jnp
from jax import lax
from jax.experimental import pallas as pl
from jax.experimental.pallas import tpu as pltpu

F32 = jnp.float32
BF16 = jnp.bfloat16

D = 1024
HEAD = 64
N_KV = 2
GROUP = 4
Q_W = 512
KV_W = 128
G_W = 512
BLK = 128
N_GRP = 8
GRP_D = 64
FH = 2816
IN_W = 3840
GRID_W = 64
ROPE_BASE = 10000.0
LN_EPS = 1e-5
NEG = -1e30
ALPHA = (2 * 1) ** 0.25
SCALE = HEAD ** -0.5
GELU_K = math.sqrt(2.0 / math.pi)
GELU_A = 0.044715
ADAM_LR = 0.001
ADAM_B1 = 0.9
ADAM_B2 = 0.999
ADAM_EPS = 1e-08
ADAM_WD = 0.01
ADAM_STEP = 10
N_DEV = 8
N_SHARD = 4
FH_SHARD = FH // 2
LANES = 128
VMEM_LIMIT = 56 * 1024 * 1024
MESH = pl.DeviceIdType.MESH


def _cp(*sem):
    return pltpu.CompilerParams(dimension_semantics=sem, vmem_limit_bytes=VMEM_LIMIT)


def _resident(shape):
    return pl.BlockSpec(shape, lambda *_: (0,) * len(shape), pipeline_mode=pl.Buffered(1))


def _rows(tm, width):
    return pl.BlockSpec((tm, width), lambda i: (i, 0))


def _acc(shape):
    return pl.BlockSpec(shape, lambda *_: (0,) * len(shape))


def _dot(a, b):
    return jnp.dot(a, b, preferred_element_type=F32)


def _dot_nt(a, b):
    return lax.dot_general(a, b, (((1,), (1,)), ((), ())), preferred_element_type=F32)


def _dot_tn(a, b):
    return lax.dot_general(a, b, (((0,), (0,)), ((), ())), preferred_element_type=F32)


def _ln(x):
    mu = jnp.mean(x, axis=-1, keepdims=True)
    xc = x - mu
    var = jnp.mean(xc * xc, axis=-1, keepdims=True)
    rstd = lax.rsqrt(var + LN_EPS)
    return xc * rstd, rstd


def _ln_bwd(dxhat, xhat, rstd):
    return (dxhat - jnp.mean(dxhat, axis=-1, keepdims=True)
            - xhat * jnp.mean(dxhat * xhat, axis=-1, keepdims=True)) * rstd


def _sig(x):
    return 0.5 + 0.5 * jnp.tanh(0.5 * x)


def _gelu(x):
    t = jnp.tanh(x * (GELU_K + (GELU_K * GELU_A) * (x * x)))
    hx = 0.5 * x
    return hx + hx * t, t


def _gelu_grad(x, t):
    return 0.5 + 0.5 * t + (0.5 * x) * (1.0 - t * t) * (GELU_K + (3.0 * GELU_K * GELU_A) * (x * x))


def _colsum(v):
    return jnp.sum(v, axis=0, keepdims=True)


def _partner(x):
    w = x.shape[1]
    lane = lax.broadcasted_iota(jnp.int32, x.shape, 1)
    return jnp.where((lane & 31) < 16, pltpu.roll(x, w - 16, 1), pltpu.roll(x, 16, 1))


def _rope(x, cos, sin):
    return x * cos + _partner(x) * sin


def _unrope(g, cos, sin):
    return g * cos + _partner(g * sin)


def _rope_tables(seq):
    inv = ROPE_BASE ** (-jnp.arange(HEAD // 4, dtype=F32) / (HEAD // 4))
    pos = jnp.arange(seq, dtype=jnp.int32)
    ar = (pos // GRID_W).astype(F32)[:, None] * inv
    ac = (pos % GRID_W).astype(F32)[:, None] * inv
    cos = jnp.concatenate([jnp.cos(ar), jnp.cos(ar), jnp.cos(ac), jnp.cos(ac)], axis=-1)
    sin = jnp.concatenate([-jnp.sin(ar), jnp.sin(ar), -jnp.sin(ac), jnp.sin(ac)], axis=-1)
    return jnp.tile(cos, (1, LANES // HEAD)), jnp.tile(sin, (1, LANES // HEAD))


def _ctx_fwd(ctx, modc, w_kv):
    n_ctx = ctx.shape[0]

    def body(ctx_ref, mod_ref, w_ref, hc_ref, kvc_ref, vac_ref):
        xhat, _ = _ln(ctx_ref[...])
        hc = (xhat * (1.0 + mod_ref[1:2, :]) + mod_ref[0:1, :]).astype(BF16)
        hc_ref[...] = hc
        kvc = _dot_nt(hc, w_ref[...]).astype(BF16)
        kvc_ref[...] = kvc
        vac_ref[...] = _with_ones(kvc[:, KV_W:])

    return pl.pallas_call(
        body, name="ctx_fwd", grid=(1,),
        in_specs=[_acc((n_ctx, D)), _acc((8, D)), _acc((2 * KV_W, D))],
        out_specs=[_acc((n_ctx, D)), _acc((n_ctx, 2 * KV_W)), _acc((n_ctx, 2 * LANES))],
        out_shape=[jax.ShapeDtypeStruct((n_ctx, D), BF16), jax.ShapeDtypeStruct((n_ctx, 2 * KV_W), BF16),
                   jax.ShapeDtypeStruct((n_ctx, 2 * LANES), BF16)],
        compiler_params=_cp("arbitrary"),
    )(ctx, modc, w_kv)


def _host_start(step, comm):
    if comm is not None:
        @pl.when(step == 0)
        def _():
            comm.start()


def _host_finish(step, last, comm):
    if comm is not None:
        @pl.when(step == last)
        def _():
            comm.finish()


def _proj_fwd(x, modx, w_in, cos, sin, tm, gather=()):
    seq = x.shape[0]
    ng = len(gather)

    def body(x_ref, mod_ref, w_ref, cos_ref, sin_ref, *rest):
        h_ref, q_ref, kv_ref, va_ref, uv_ref, gab_ref = rest[ng:ng + 6]
        comm = _Gather(rest[:ng], rest[ng + 6:2 * ng + 6], *rest[2 * ng + 6:]) if ng else None
        _host_start(pl.program_id(0), comm)
        xhat, _ = _ln(x_ref[...])
        h = (xhat * (1.0 + mod_ref[1:2, :]) + mod_ref[0:1, :]).astype(BF16)
        h_ref[...] = h
        cos1, sin1 = cos_ref[...], sin_ref[...]
        cos2 = jnp.concatenate([cos1, cos1], axis=1)
        sin2 = jnp.concatenate([sin1, sin1], axis=1)
        for j in range(Q_W // 256):
            t = _dot_nt(h, w_ref[256 * j:256 * (j + 1), :])
            q_ref[:, 256 * j:256 * (j + 1)] = (_rope(t, cos2, sin2) * SCALE).astype(BF16)
        t = _dot_nt(h, w_ref[Q_W:Q_W + 2 * KV_W, :])
        kv_ref[:, :KV_W] = _rope(t[:, :KV_W], cos1, sin1).astype(BF16)
        v = t[:, KV_W:].astype(BF16)
        kv_ref[:, KV_W:] = v
        va_ref[...] = _with_ones(v)
        o = Q_W + 2 * KV_W
        for j in range(2):
            uv_ref[:, G_W * j:G_W * (j + 1)] = _dot_nt(h, w_ref[o + G_W * j:o + G_W * (j + 1), :]).astype(BF16)
        o += 2 * G_W
        for j in range(4):
            gab_ref[:, 512 * j:512 * (j + 1)] = _dot_nt(h, w_ref[o + 512 * j:o + 512 * (j + 1), :]).astype(BF16)
        _host_finish(pl.program_id(0), seq // tm - 1, comm)

    out = pl.pallas_call(
        body, name="proj_fwd", grid=(seq // tm,),
        in_specs=[_rows(tm, D), _acc((8, D)), _resident((IN_W, D)), _rows(tm, LANES), _rows(tm, LANES)] + _comm_specs(ng),
        out_specs=[_rows(tm, D), _rows(tm, Q_W), _rows(tm, 2 * KV_W), _rows(tm, 2 * LANES), _rows(tm, 2 * G_W),
                   _rows(tm, 2 * D)] + _comm_specs(ng),
        out_shape=[jax.ShapeDtypeStruct((seq, D), BF16), jax.ShapeDtypeStruct((seq, Q_W), BF16),
                   jax.ShapeDtypeStruct((seq, 2 * KV_W), BF16), jax.ShapeDtypeStruct((seq, 2 * LANES), BF16),
                   jax.ShapeDtypeStruct((seq, 2 * G_W), BF16), jax.ShapeDtypeStruct((seq, 2 * D), BF16)] + _gathered_shapes(gather),
        scratch_shapes=_comm_scratch(ng) if ng else [],
        compiler_params=_cp("arbitrary"),
    )(x, modx, w_in, cos, sin, *gather)
    return out[:6], out[6:]


def _stack_heads(x, hk):
    return jnp.concatenate([x[:, (hk * GROUP + g) * HEAD:(hk * GROUP + g + 1) * HEAD] for g in range(GROUP)], axis=0)


def _attn_scores(q, k_refs, hk, n, nb):
    q4 = _stack_heads(q, hk)
    ks = [r[:, hk * HEAD:(hk + 1) * HEAD] for r in k_refs]
    rows = GROUP * BLK
    qi = lax.broadcasted_iota(jnp.int32, (rows, BLK), 0) & (BLK - 1)
    kj = lax.broadcasted_iota(jnp.int32, (rows, BLK), 1)
    s = [_dot_nt(q4, k) for k in ks]
    s[1] = jnp.where((kj >= qi) & (n > 0), s[1], NEG)
    s[3] = jnp.where((kj <= qi) & (n < nb - 1), s[3], NEG)
    return q4, ks, s


def _sink_rows(sink_ref, hk):
    rows = GROUP * BLK
    rg = lax.broadcasted_iota(jnp.int32, (rows, 1), 0) >> 7
    sink_v = jnp.full((rows, 1), sink_ref[0, hk * GROUP], F32)
    for g in range(1, GROUP):
        sink_v = jnp.where(rg == g, sink_ref[0, hk * GROUP + g], sink_v)
    return sink_v


def _with_ones(v):
    ones = jnp.ones((v.shape[0], HEAD), v.dtype)
    return jnp.concatenate([v[:, :HEAD], ones, v[:, HEAD:], ones], axis=1)


def _kv_specs(nb, qb):
    def spec(d):
        return pl.BlockSpec((BLK, 2 * KV_W), lambda n: (jnp.clip(qb * n + d, 0, nb - 1), 0))
    return [spec(d) for d in range(-1, qb + 1)]


def _attn_fwd(q, kv, va, kvc, vac, sink, gather=()):
    seq = q.shape[0]
    nb = seq // BLK
    n_ctx = kvc.shape[0]
    ng = len(gather)
    Q_BLOCKS = 1
    nkv = Q_BLOCKS + 2
    steps = nb // Q_BLOCKS

    def body(q_ref, *rest):
        kv_refs, va_refs = rest[:nkv], rest[nkv:2 * nkv]
        kvc_ref, vac_ref, sink_ref = rest[2 * nkv:2 * nkv + 3]
        rest = rest[2 * nkv + 3:]
        o_ref, lse_ref = rest[ng:ng + 2]
        comm = _Gather(rest[:ng], rest[ng + 2:2 * ng + 2], *rest[2 * ng + 2:]) if ng else None
        n = pl.program_id(0)
        _host_start(n, comm)
        lane = lax.broadcasted_iota(jnp.int32, (BLK, LANES), 1)
        for sub in range(Q_BLOCKS):
            rs = slice(sub * BLK, (sub + 1) * BLK)
            q = q_ref[rs, :]
            outs = []
            lse_all = jnp.zeros((BLK, LANES), F32)
            for hk in range(N_KV):
                _, _, s = _attn_scores(q, (kvc_ref,) + kv_refs[sub:sub + 3], hk, Q_BLOCKS * n + sub, nb)
                sink_v = _sink_rows(sink_ref, hk)
                m = sink_v
                for t in s:
                    m = jnp.maximum(m, jnp.max(t, axis=-1, keepdims=True))
                o = jnp.zeros((GROUP * BLK, LANES), F32)
                for t, va_ref in zip(s, (vac_ref,) + va_refs[sub:sub + 3]):
                    o = o + _dot(jnp.exp((t - m).astype(BF16)), va_ref[:, hk * LANES:(hk + 1) * LANES])
                denom = o[:, HEAD:HEAD + 1] + jnp.exp(sink_v - m)
                o4 = o[:, :HEAD] * (1.0 / denom)
                lse4 = m + jnp.log(denom)
                for g in range(GROUP):
                    outs.append(o4[g * BLK:(g + 1) * BLK, :])
                    lse_all = jnp.where(lane == hk * GROUP + g, lse4[g * BLK:(g + 1) * BLK, :], lse_all)
            o_ref[rs, :] = jnp.concatenate(outs, axis=1).astype(BF16)
            lse_ref[rs, :] = lse_all
        _host_finish(n, steps - 1, comm)

    tq = Q_BLOCKS * BLK
    out = pl.pallas_call(
        body, name="attn_fwd", grid=(steps,),
        in_specs=[_rows(tq, Q_W)] + _kv_specs(nb, Q_BLOCKS) + _kv_specs(nb, Q_BLOCKS)
        + [_acc((n_ctx, 2 * KV_W)), _acc((n_ctx, 2 * LANES)), pl.BlockSpec(memory_space=pltpu.SMEM)] + _comm_specs(ng),
        out_specs=[_rows(tq, Q_W), _rows(tq, LANES)] + _comm_specs(ng),
        out_shape=[jax.ShapeDtypeStruct((seq, Q_W), BF16), jax.ShapeDtypeStruct((seq, LANES), F32)] + _gathered_shapes(gather),
        scratch_shapes=_comm_scratch(ng) if ng else [],
        compiler_params=_cp("arbitrary"),
    )(q, *([kv] * nkv), *([va] * nkv), kvc, vac, sink, *gather)
    return out[:2], out[2:]


def _gmlp_chunk(u, vb, gp_ref, ws_ref, bias_ref):
    gu, tu = _gelu(u)
    gv, tv = _gelu(vb)
    vhat, rstd = _ln(gv)
    vn = (vhat * gp_ref[0:1, :] + gp_ref[1:2, :]).astype(BF16)
    s = bias_ref[...] + jnp.concatenate(
        [_dot(ws_ref[g * BLK:(g + 1) * BLK, :], vn[:, g * GRP_D:(g + 1) * GRP_D]) for g in range(N_GRP)], axis=1)
    return gu, tu, tv, vhat, rstd, vn, s


def _mix_fwd(uv, gab, ya, gp, ws_stack, bias_full, w_a, w_b, w_o, tm):
    seq = uv.shape[0]

    def body(uv_ref, gab_ref, ya_ref, gp_ref, ws_ref, bias_ref, wa_ref, wb_ref, wo_ref,
             a_ref, b_ref, mix_ref, merged_ref, yb_ref):
        for c in range(tm // BLK):
            rs = slice(c * BLK, (c + 1) * BLK)
            gu, _, _, _, _, _, s = _gmlp_chunk(uv_ref[rs, :G_W].astype(F32), uv_ref[rs, G_W:].astype(F32), gp_ref, ws_ref, bias_ref)
            yb_ref[rs, :] = (gu * s).astype(BF16)
        a = _dot(ya_ref[...], wa_ref[...])
        b = _dot(yb_ref[...], wb_ref[...])
        a_ref[...] = a.astype(BF16)
        b_ref[...] = b.astype(BF16)
        merged = (_sig(gab_ref[:, :D].astype(F32)) * a + _sig(gab_ref[:, D:].astype(F32)) * b).astype(BF16)
        merged_ref[...] = merged
        mix_ref[...] = _dot(merged, wo_ref[...])

    return pl.pallas_call(
        body, name="mix_fwd", grid=(seq // tm,),
        in_specs=[_rows(tm, 2 * G_W), _rows(tm, 2 * D), _rows(tm, Q_W), _acc((8, G_W)),
                  _resident((N_GRP * BLK, BLK)), _acc((BLK, G_W)),
                  _resident((Q_W, D)), _resident((G_W, D)), _resident((D, D))],
        out_specs=[_rows(tm, D), _rows(tm, D), _rows(tm, D), _rows(tm, D), _rows(tm, G_W)],
        out_shape=[jax.ShapeDtypeStruct((seq, D), BF16), jax.ShapeDtypeStruct((seq, D), BF16),
                   jax.ShapeDtypeStruct((seq, D), F32), jax.ShapeDtypeStruct((seq, D), BF16),
                   jax.ShapeDtypeStruct((seq, G_W), BF16)],
        compiler_params=_cp("arbitrary"),
    )(uv, gab, ya, gp, ws_stack, bias_full, w_a, w_b, w_o)


FFN_CHUNK = 512


def _ffn_chunks():
    out = []
    for hh in range(2):
        off = 0
        while off < FH_SHARD:
            w = min(FFN_CHUNK, FH_SHARD - off)
            out.append((hh, off, w))
            off += w
    return out


def _mid_recompute(x_ref, mix_ref, vec_ref):
    r1 = ALPHA * x_ref[...] + vec_ref[0:1, :] * mix_ref[...]
    xh1, rstd1 = _ln(r1)
    xmid = xh1 * vec_ref[1:2, :] + vec_ref[2:3, :]
    xh2, rstd2 = _ln(xmid)
    return xh1, rstd1, xmid, xh2, rstd2


def _ffn(x, mix, tgt, vec, w_fi, w_fo, tm):
    seq = x.shape[0]

    def body(x_ref, mix_ref, tgt_ref, vec_ref, wi_ref, wo_ref, act_ref, h2_ref, dff_ref, df_ref, dr1_ref, st_ref, gu_ref):
        @pl.when(pl.program_id(0) == 0)
        def _():
            st_ref[...] = jnp.zeros_like(st_ref)

        xh1, rstd1, xmid, xh2, rstd2 = _mid_recompute(x_ref, mix_ref, vec_ref)
        h2 = (xh2 * (1.0 + vec_ref[4:5, :]) + vec_ref[3:4, :]).astype(BF16)
        h2_ref[...] = h2
        halves = [(slice(hh * FH_SHARD, (hh + 1) * FH_SHARD), slice(FH + hh * FH_SHARD, FH + (hh + 1) * FH_SHARD))
                  for hh in range(2)]
        for hh, (cs, cu) in enumerate(halves):
            g = _dot(h2, wi_ref[hh])
            u = _dot(h2, wi_ref[2 + hh])
            gu_ref[:, cs] = g
            gu_ref[:, cu] = u
            act_ref[:, cs] = (g * _sig(g) * u).astype(BF16)
        f = _dot(act_ref[...], wo_ref[...])
        r2 = ALPHA * xmid + vec_ref[5:6, :] * f
        yh, rstd = _ln(r2)
        y = yh * vec_ref[6:7, :] + vec_ref[7:8, :]
        err = y - tgt_ref[...]
        dy = err / D
        dr2 = _ln_bwd(dy * vec_ref[6:7, :], yh, rstd)
        st_ref[0:1, :] += _colsum(err * err)
        st_ref[1:2, :] += _colsum(dy * yh)
        st_ref[2:3, :] += _colsum(dy)
        st_ref[3:4, :] += _colsum(dr2 * f)

        df = (dr2 * vec_ref[5:6, :]).astype(BF16)
        df_ref[...] = df
        da_all = _dot_nt(df, wo_ref[...])
        for cs, cu in halves:
            da = da_all[:, cs]
            g = gu_ref[:, cs]
            u = gu_ref[:, cu]
            sg = _sig(g)
            dff_ref[:, cs] = (da * u * sg * (1.0 + g * (1.0 - sg))).astype(BF16)
            dff_ref[:, cu] = (da * g * sg).astype(BF16)
        dh2 = _dot_nt(dff_ref[:, :FH_SHARD], wi_ref[0])
        for s in range(1, N_SHARD):
            dh2 = dh2 + _dot_nt(dff_ref[:, s * FH_SHARD:(s + 1) * FH_SHARD], wi_ref[s])
        dxmid = _ln_bwd(dh2 * (1.0 + vec_ref[4:5, :]), xh2, rstd2) + ALPHA * dr2
        dr1 = _ln_bwd(dxmid * vec_ref[1:2, :], xh1, rstd1)
        dr1_ref[...] = dr1
        st_ref[8:9, :] += _colsum(dh2 * xh2)
        st_ref[9:10, :] += _colsum(dh2)
        st_ref[10:11, :] += _colsum(dxmid * xh1)
        st_ref[11:12, :] += _colsum(dxmid)
        st_ref[12:13, :] += _colsum(dr1 * mix_ref[...])

    return pl.pallas_call(
        body, name="ffn", grid=(seq // tm,),
        in_specs=[_rows(tm, D), _rows(tm, D), _rows(tm, D), _acc((8, D)), _resident((N_SHARD, D, FH_SHARD)), _resident((FH, D))],
        out_specs=[_rows(tm, FH), _rows(tm, D), _rows(tm, 2 * FH), _rows(tm, D), _rows(tm, D), _acc((16, D))],
        out_shape=[jax.ShapeDtypeStruct((seq, FH), BF16), jax.ShapeDtypeStruct((seq, D), BF16),
                   jax.ShapeDtypeStruct((seq, 2 * FH), BF16), jax.ShapeDtypeStruct((seq, D), BF16),
                   jax.ShapeDtypeStruct((seq, D), F32), jax.ShapeDtypeStruct((16, D), F32)],
        scratch_shapes=[pltpu.VMEM((tm, 2 * FH), F32)],
        compiler_params=_cp("arbitrary"),
    )(x, mix, tgt, vec, w_fi, w_fo)


def _mix_bwd(dr1, a, b, gab, uv, merged, ya, yb, vec, gp, ws_stack, ws_stack_t, bias_full, w_a, w_b, w_o, tm, scatter=()):
    seq = dr1.shape[0]
    last = seq // tm - 1
    ns = len(scatter)

    def body(dr1_ref, a_ref, b_ref, gab_ref, uv_ref, mg_ref, ya_ref, yb_ref, vec_ref, gp_ref, ws_ref, wst_ref, bias_ref,
             wa_ref, wb_ref, wo_ref, *rest):
        dya_ref, dp_ref, dws_ref, dbs_ref, st_ref, gwo_ref, gwa_ref, gwb_ref = rest[ns:ns + 8]
        acc_o, acc_a, acc_b = rest[2 * ns + 8:2 * ns + 11]
        comm = _AllToAll(rest[:ns], rest[ns + 8:2 * ns + 8], *rest[2 * ns + 11:]) if ns else None
        _host_start(pl.program_id(0), comm)

        @pl.when(pl.program_id(0) == 0)
        def _():
            dws_ref[...] = jnp.zeros_like(dws_ref)
            dbs_ref[...] = jnp.zeros_like(dbs_ref)
            st_ref[...] = jnp.zeros_like(st_ref)
            acc_o[...] = jnp.zeros_like(acc_o)
            acc_a[...] = jnp.zeros_like(acc_a)
            acc_b[...] = jnp.zeros_like(acc_b)

        dmix = (dr1_ref[...] * vec_ref[0:1, :]).astype(BF16)
        acc_o[...] += _dot_tn(mg_ref[...], dmix)
        dmerged = _dot_nt(dmix, wo_ref[...])
        sa = _sig(gab_ref[:, :D].astype(F32))
        sb = _sig(gab_ref[:, D:].astype(F32))
        da = (dmerged * sa).astype(BF16)
        db = (dmerged * sb).astype(BF16)
        dp_ref[:, 2 * G_W:2 * G_W + D] = (dmerged * a_ref[...].astype(F32) * sa * (1.0 - sa)).astype(BF16)
        dp_ref[:, 2 * G_W + D:] = (dmerged * b_ref[...].astype(F32) * sb * (1.0 - sb)).astype(BF16)
        dya_ref[...] = _dot_nt(da, wa_ref[...]).astype(BF16)
        dyb = _dot_nt(db, wb_ref[...])
        acc_a[...] += _dot_tn(ya_ref[...], da)
        acc_b[...] += _dot_tn(yb_ref[...], db)

        @pl.when(pl.program_id(0) == last)
        def _():
            gwo_ref[...] = acc_o[...].astype(BF16)
            gwa_ref[...] = acc_a[...].astype(BF16)
            gwb_ref[...] = acc_b[...].astype(BF16)

        for c in range(tm // BLK):
            rs = slice(c * BLK, (c + 1) * BLK)
            u = uv_ref[rs, :G_W].astype(F32)
            vb = uv_ref[rs, G_W:].astype(F32)
            gu, tu, tv, vhat, rstd, vn, s = _gmlp_chunk(u, vb, gp_ref, ws_ref, bias_ref)
            dyb_c = dyb[rs, :]
            ds = dyb_c * gu
            du = dyb_c * s * _gelu_grad(u, tu)
            ds_b = ds.astype(BF16)
            dvn_g = []
            for g in range(N_GRP):
                cg = slice(g * GRP_D, (g + 1) * GRP_D)
                dvn_g.append(_dot(wst_ref[:, g * BLK:(g + 1) * BLK], ds_b[:, cg]))
                dws_ref[g * BLK:(g + 1) * BLK, :] += _dot_nt(ds_b[:, cg], vn[:, cg])
            dvn = jnp.concatenate(dvn_g, axis=1)
            dbs_ref[...] += ds
            st_ref[0:1, :] += _colsum(dvn * vhat)
            st_ref[1:2, :] += _colsum(dvn)
            dgv = _ln_bwd(dvn * gp_ref[0:1, :], vhat, rstd)
            dvb = dgv * _gelu_grad(vb, tv)
            dp_ref[rs, :G_W] = du.astype(BF16)
            dp_ref[rs, G_W:2 * G_W] = dvb.astype(BF16)
        _host_finish(pl.program_id(0), last, comm)

    pw = 2 * G_W + 2 * D
    out = pl.pallas_call(
        body, name="mix_bwd", grid=(seq // tm,),
        in_specs=[_rows(tm, D), _rows(tm, D), _rows(tm, D), _rows(tm, 2 * D), _rows(tm, 2 * G_W), _rows(tm, D), _rows(tm, Q_W),
                  _rows(tm, G_W), _acc((8, D)), _acc((8, G_W)),
                  _resident((N_GRP * BLK, BLK)), _resident((BLK, N_GRP * BLK)), _acc((BLK, G_W)),
                  _resident((Q_W, D)), _resident((G_W, D)), _resident((D, D))] + _comm_specs(ns),
        out_specs=[_rows(tm, Q_W), _rows(tm, pw), _acc((N_GRP * BLK, BLK)), _acc((BLK, G_W)), _acc((8, G_W)),
                   _acc((D, D)), _acc((Q_W, D)), _acc((G_W, D))] + _comm_specs(ns),
        out_shape=[jax.ShapeDtypeStruct((seq, Q_W), BF16), jax.ShapeDtypeStruct((seq, pw), BF16),
                   jax.ShapeDtypeStruct((N_GRP * BLK, BLK), F32), jax.ShapeDtypeStruct((BLK, G_W), F32),
                   jax.ShapeDtypeStruct((8, G_W), F32), jax.ShapeDtypeStruct((D, D), BF16),
                   jax.ShapeDtypeStruct((Q_W, D), BF16), jax.ShapeDtypeStruct((G_W, D), BF16)]
        + [jax.ShapeDtypeStruct(v.shape, v.dtype) for v in scatter],
        scratch_shapes=[pltpu.VMEM((D, D), F32), pltpu.VMEM((Q_W, D), F32), pltpu.VMEM((G_W, D), F32)]
        + (_comm_scratch(ns) if ns else []),
        compiler_params=_cp("arbitrary"),
    )(dr1, a, b, gab, uv, merged, ya, yb, vec, gp, ws_stack, ws_stack_t, bias_full, w_a, w_b, w_o, *scatter)
    return out[:8], out[8:]


def _attn_bwd(q, kv, kvc, sink, dya, ya, lse, scatter=()):
    seq = q.shape[0]
    nb = seq // BLK
    n_ctx = kvc.shape[0]
    ns = len(scatter)
    Q_BLOCKS = 2
    nkv = Q_BLOCKS + 2
    steps = nb // Q_BLOCKS

    def body(q_ref, *rest):
        kv_refs = rest[:nkv]
        kvc_ref, sink_ref, do_ref, o_ref, lse_ref = rest[nkv:nkv + 5]
        rest = rest[nkv + 5:]
        dq_ref, dkv_ref, dkvc_ref, dsink_ref = rest[ns:ns + 4]
        comm = _AllToAll(rest[:ns], rest[ns + 4:2 * ns + 4], *rest[2 * ns + 4:]) if ns else None
        n = pl.program_id(0)
        _host_start(n, comm)

        @pl.when(n == 0)
        def _():
            dkv_ref[...] = jnp.zeros_like(dkv_ref)
            dkvc_ref[...] = jnp.zeros_like(dkvc_ref)
            dsink_ref[...] = jnp.zeros_like(dsink_ref)

        lane = lax.broadcasted_iota(jnp.int32, (1, LANES), 1)
        for sub in range(Q_BLOCKS):
            rs = slice(sub * BLK, (sub + 1) * BLK)
            blk = Q_BLOCKS * n + sub
            q = q_ref[rs, :]
            do = do_ref[rs, :]
            out = o_ref[rs, :]
            lse_all = lse_ref[rs, :]
            k_refs = (kvc_ref,) + kv_refs[sub:sub + 3]
            dqs, dks, dvs = [], [], []
            for hk in range(N_KV):
                q4, ks, s = _attn_scores(q, k_refs, hk, blk, nb)
                vs = [r[:, KV_W + hk * HEAD:KV_W + (hk + 1) * HEAD] for r in k_refs]
                lse4 = jnp.concatenate([lse_all[:, hk * GROUP + g:hk * GROUP + g + 1] for g in range(GROUP)], axis=0)
                do4 = _stack_heads(do, hk)
                delta = jnp.sum(do4.astype(F32) * _stack_heads(out, hk).astype(F32), axis=-1, keepdims=True)
                p = [jnp.exp((t - lse4).astype(BF16)) for t in s]
                ds = [t * (_dot_nt(do4, v) - delta).astype(BF16) for t, v in zip(p, vs)]
                dq4 = _dot(ds[0], ks[0])
                for t, k in zip(ds[1:], ks[1:]):
                    dq4 = dq4 + _dot(t, k)
                dq4 = dq4 * SCALE
                dqs += [dq4[g * BLK:(g + 1) * BLK, :] for g in range(GROUP)]
                dks.append([_dot_tn(t, q4) for t in ds])
                dvs.append([_dot_tn(t, do4) for t in p])
                ps = jnp.exp(_sink_rows(sink_ref, hk) - lse4) * delta
                for g in range(GROUP):
                    part = -jnp.sum(ps[g * BLK:(g + 1) * BLK, :], axis=0, keepdims=True)
                    dsink_ref[0:1, :] += jnp.where(lane == hk * GROUP + g, part, 0.0)
            dq_ref[rs, :] = jnp.concatenate(dqs, axis=1)

            def piece(i):
                return jnp.concatenate([dks[0][i], dks[1][i], dvs[0][i], dvs[1][i]], axis=1)

            dkvc_ref[...] += piece(0)
            starts = (jnp.maximum(blk - 1, 0), blk, jnp.minimum(blk + 1, nb - 1))
            for i, st in enumerate(starts):
                r = pl.ds(pl.multiple_of(st * BLK, BLK), BLK)
                dkv_ref[r, :] += piece(i + 1)
        _host_finish(n, steps - 1, comm)

    tq = Q_BLOCKS * BLK
    out = pl.pallas_call(
        body, name="attn_bwd", grid=(steps,),
        in_specs=[_rows(tq, Q_W)] + _kv_specs(nb, Q_BLOCKS) + [_acc((n_ctx, 2 * KV_W)), pl.BlockSpec(memory_space=pltpu.SMEM),
                                                     _rows(tq, Q_W), _rows(tq, Q_W), _rows(tq, LANES)] + _comm_specs(ns),
        out_specs=[_rows(tq, Q_W), _acc((seq, 2 * KV_W)), _acc((n_ctx, 2 * KV_W)), _acc((8, LANES))] + _comm_specs(ns),
        out_shape=[jax.ShapeDtypeStruct((seq, Q_W), F32), jax.ShapeDtypeStruct((seq, 2 * KV_W), F32),
                   jax.ShapeDtypeStruct((n_ctx, 2 * KV_W), F32), jax.ShapeDtypeStruct((8, LANES), F32)]
        + [jax.ShapeDtypeStruct(v.shape, v.dtype) for v in scatter],
        scratch_shapes=_comm_scratch(ns) if ns else [],
        compiler_params=_cp("arbitrary"),
    )(q, *([kv] * nkv), kvc, sink, dya, ya, lse, *scatter)
    return out[:4], out[4:]


def _proj_bwd(dq, dkv, dpb, x, dr1, modx, w_in, cos, sin, tm, scatter=()):
    seq = x.shape[0]
    pw = IN_W - Q_W - 2 * KV_W
    ns = len(scatter)

    def body(dq_ref, dkv_ref, dpb_ref, x_ref, dr1_ref, mod_ref, w_ref, cos_ref, sin_ref, *rest):
        dqkv_ref, gx_ref, st_ref = rest[ns:ns + 3]
        comm = _AllToAll(rest[:ns], rest[ns + 3:2 * ns + 3], *rest[2 * ns + 3:]) if ns else None
        _host_start(pl.program_id(0), comm)

        @pl.when(pl.program_id(0) == 0)
        def _():
            st_ref[...] = jnp.zeros_like(st_ref)

        cos1, sin1 = cos_ref[...], sin_ref[...]
        cos2 = jnp.concatenate([cos1, cos1], axis=1)
        sin2 = jnp.concatenate([sin1, sin1], axis=1)
        for j in range(Q_W // 256):
            cs = slice(256 * j, 256 * (j + 1))
            dqkv_ref[:, cs] = _unrope(dq_ref[:, cs], cos2, sin2).astype(BF16)
        dqkv_ref[:, Q_W:Q_W + KV_W] = _unrope(dkv_ref[:, :KV_W], cos1, sin1).astype(BF16)
        dqkv_ref[:, Q_W + KV_W:] = dkv_ref[:, KV_W:].astype(BF16)
        o = Q_W + 2 * KV_W
        dh = _dot(dqkv_ref[...], w_ref[:o, :]) + _dot(dpb_ref[...], w_ref[o:, :])
        xhat, rstd = _ln(x_ref[...])
        st_ref[0:1, :] += _colsum(dh)
        st_ref[1:2, :] += _colsum(dh * xhat)
        gx_ref[...] = _ln_bwd(dh * (1.0 + mod_ref[1:2, :]), xhat, rstd) + ALPHA * dr1_ref[...]
        _host_finish(pl.program_id(0), seq // tm - 1, comm)

    out = pl.pallas_call(
        body, name="proj_bwd", grid=(seq // tm,),
        in_specs=[_rows(tm, Q_W), _rows(tm, 2 * KV_W), _rows(tm, pw), _rows(tm, D), _rows(tm, D), _acc((8, D)),
                  _resident((IN_W, D)), _rows(tm, LANES), _rows(tm, LANES)] + _comm_specs(ns),
        out_specs=[_rows(tm, Q_W + 2 * KV_W), _rows(tm, D), _acc((8, D))] + _comm_specs(ns),
        out_shape=[jax.ShapeDtypeStruct((seq, Q_W + 2 * KV_W), BF16), jax.ShapeDtypeStruct((seq, D), F32),
                   jax.ShapeDtypeStruct((8, D), F32)] + [jax.ShapeDtypeStruct(v.shape, v.dtype) for v in scatter],
        scratch_shapes=_comm_scratch(ns) if ns else [],
        compiler_params=_cp("arbitrary"),
    )(dq, dkv, dpb, x, dr1, modx, w_in, cos, sin, *scatter)
    return out[:3], out[3:]


def _ctx_bwd(dkvc, ctx, hc, w_kv):
    n_ctx = ctx.shape[0]

    def body(dkvc_ref, ctx_ref, hc_ref, w_ref, dw_ref, st_ref):
        d = dkvc_ref[...].astype(BF16)
        dw_ref[...] = _dot_tn(d, hc_ref[...])
        dhc = _dot(d, w_ref[...])
        xhat, _ = _ln(ctx_ref[...])
        st_ref[...] = jnp.zeros_like(st_ref)
        st_ref[0:1, :] = _colsum(dhc)
        st_ref[1:2, :] = _colsum(dhc * xhat)

    return pl.pallas_call(
        body, name="ctx_bwd", grid=(1,),
        in_specs=[_acc((n_ctx, 2 * KV_W)), _acc((n_ctx, D)), _acc((n_ctx, D)), _acc((2 * KV_W, D))],
        out_specs=[_acc((2 * KV_W, D)), _acc((8, D))],
        out_shape=[jax.ShapeDtypeStruct((2 * KV_W, D), F32), jax.ShapeDtypeStruct((8, D), F32)],
        compiler_params=_cp("arbitrary"),
    )(dkvc, ctx, hc, w_kv)


def _tn_matmul(a, b, tn, name, out_dtype, shard_major=False, init=None, tk=512, scatter=(), gather=()):
    t, ka = a.shape
    n = b.shape[1]
    tk = min(tk, t)
    nk = t // tk
    nj = n // tn
    has_init = init is not None
    assert not (scatter and gather)
    moved = list(scatter) + list(gather)
    pattern = _AllToAll if scatter else _Gather
    ns = len(moved)
    n_in = 3 if has_init else 2

    def body(*refs):
        a_ref, b_ref = refs[:2]
        i_ref = refs[2] if has_init else None
        rest = refs[n_in:]
        o_ref = rest[ns]
        acc_ref = rest[2 * ns + 1]
        comm = pattern(rest[:ns], rest[ns + 1:2 * ns + 1], *rest[2 * ns + 2:]) if ns else None
        k = pl.program_id(1)
        step = pl.program_id(0) * nk + k
        _host_start(step, comm)

        @pl.when(k == 0)
        def _():
            acc_ref[...] = i_ref[...] if has_init else jnp.zeros_like(acc_ref)

        acc_ref[...] += _dot_tn(a_ref[...], b_ref[...])

        @pl.when(k == nk - 1)
        def _():
            o_ref[...] = acc_ref[...].astype(out_dtype)

        _host_finish(step, nj * nk - 1, comm)

    in_specs = [pl.BlockSpec((tk, ka), lambda j, k: (k, 0)), pl.BlockSpec((tk, tn), lambda j, k: (k, j))]
    args = [a, b]
    if has_init:
        in_specs.append(pl.BlockSpec((ka, tn), lambda j, k: (0, j)))
        args.append(init)
    if shard_major:
        out_spec = pl.BlockSpec((None, ka, tn), lambda j, k: (j, 0, 0))
        out_shape = jax.ShapeDtypeStruct((nj, ka, tn), out_dtype)
    else:
        out_spec = pl.BlockSpec((ka, tn), lambda j, k: (0, j))
        out_shape = jax.ShapeDtypeStruct((ka, n), out_dtype)
    out = pl.pallas_call(
        body, name=name, grid=(nj, nk), in_specs=in_specs + _comm_specs(ns), out_specs=[out_spec] + _comm_specs(ns),
        out_shape=[out_shape] + [jax.ShapeDtypeStruct(v.shape, v.dtype) for v in scatter] + _gathered_shapes(gather),
        scratch_shapes=[pltpu.VMEM((ka, tn), F32)] + (_comm_scratch(ns) if ns else []),
        compiler_params=_cp("arbitrary", "arbitrary"),
    )(*args, *moved)
    return (out[0], out[1:]) if ns else out[0]


ADA_TILE = 512


def _ada_fwd(sc_all, w_ada):
    cs = w_ada.shape[1]

    def body(s_ref, w_ref, o_ref):
        o_ref[...] = _dot(s_ref[...].astype(BF16), w_ref[...].astype(BF16))

    return pl.pallas_call(
        body, name="ada_fwd", grid=(cs // ADA_TILE,),
        in_specs=[_acc((16, D)), pl.BlockSpec((D, ADA_TILE), lambda j: (0, j))],
        out_specs=pl.BlockSpec((16, ADA_TILE), lambda j: (0, j)),
        out_shape=jax.ShapeDtypeStruct((16, cs), F32),
        compiler_params=_cp("arbitrary"),
    )(sc_all, w_ada)


def _ada_bwd(sc_all_t, dm_all, dmc, w_ada):
    cs = w_ada.shape[1]

    def body(st_ref, dm_ref, dmc_ref, w_ref, gw_ref, part_ref):
        @pl.when(pl.program_id(0) == 0)
        def _():
            part_ref[...] = jnp.zeros_like(part_ref)

        gw_ref[...] = _dot(st_ref[...].astype(BF16), dm_ref[...].astype(BF16))
        part_ref[...] += _dot_nt(dmc_ref[...].astype(BF16), w_ref[...].astype(BF16))

    return pl.pallas_call(
        body, name="ada_bwd", grid=(cs // ADA_TILE,),
        in_specs=[_acc((D, 16)), pl.BlockSpec((16, ADA_TILE), lambda j: (0, j)), pl.BlockSpec((8, ADA_TILE), lambda j: (0, j)),
                  pl.BlockSpec((D, ADA_TILE), lambda j: (0, j))],
        out_specs=[pl.BlockSpec((D, ADA_TILE), lambda j: (0, j)), _acc((8, D))],
        out_shape=[jax.ShapeDtypeStruct((D, cs), F32), jax.ShapeDtypeStruct((8, D), F32)],
        compiler_params=_cp("arbitrary"),
    )(sc_all_t, dm_all, dmc, w_ada)


def _sum8(x, name, tr=256):
    _, r, c = x.shape
    tr = min(tr, r)
    while r % tr:
        tr -= 16

    def body(x_ref, o_ref):
        acc = x_ref[0].astype(F32)
        for i in range(1, N_DEV):
            acc = acc + x_ref[i].astype(F32)
        o_ref[...] = acc

    return pl.pallas_call(
        body, name=name, grid=(r // tr,),
        in_specs=[pl.BlockSpec((N_DEV, tr, c), lambda i: (0, i, 0))],
        out_specs=pl.BlockSpec((tr, c), lambda i: (i, 0)),
        out_shape=jax.ShapeDtypeStruct((r, c), F32),
        compiler_params=_cp("arbitrary"),
    )(x)


def _sum_blocks(recv, src, me, name, tr=256):
    _, r, c = recv.shape
    tr = min(tr, r)
    while r % tr:
        tr -= 16

    def body(me_ref, recv_ref, own_ref, o_ref):
        acc = own_ref[...].astype(F32)
        for k in range(1, N_DEV):
            acc = acc + recv_ref[me_ref[0] ^ k].astype(F32)
        o_ref[...] = acc

    return pl.pallas_call(
        body, name=name,
        grid_spec=pltpu.PrefetchScalarGridSpec(
            num_scalar_prefetch=1, grid=(r // tr,),
            in_specs=[pl.BlockSpec((N_DEV, tr, c), lambda i, me_ref: (0, i, 0)),
                      pl.BlockSpec((None, tr, c), lambda i, me_ref: (me_ref[0], i, 0))],
            out_specs=pl.BlockSpec((tr, c), lambda i, me_ref: (i, 0))),
        out_shape=jax.ShapeDtypeStruct((r, c), F32),
        compiler_params=_cp("arbitrary"),
    )(me, recv, src)


def _sum8_many(xs, name):
    n = len(xs)

    def body(*refs):
        for x_ref, o_ref in zip(refs[:n], refs[n:]):
            acc = x_ref[0]
            for i in range(1, N_DEV):
                acc = acc + x_ref[i]
            o_ref[...] = acc

    vmem = pl.BlockSpec(memory_space=pltpu.VMEM)
    return pl.pallas_call(
        body, name=name, in_specs=[vmem] * n, out_specs=[vmem] * n,
        out_shape=[jax.ShapeDtypeStruct(v.shape[1:], v.dtype) for v in xs],
        compiler_params=pltpu.CompilerParams(vmem_limit_bytes=VMEM_LIMIT),
    )(*xs)


def _adam_update(w, g, m, v):
    nm = ADAM_B1 * m + (1.0 - ADAM_B1) * g
    nv = ADAM_B2 * v + (1.0 - ADAM_B2) * (g * g)
    m_hat = nm / (1.0 - ADAM_B1 ** ADAM_STEP)
    v_hat = nv / (1.0 - ADAM_B2 ** ADAM_STEP)
    return -ADAM_LR * (m_hat / (jnp.sqrt(v_hat) + ADAM_EPS) + ADAM_WD * w), nm, nv


ROW_LOSS, ROW_LN2_G, ROW_LN2_B, ROW_LN1_G, ROW_LN1_B = 0, 1, 2, 10, 11
ROWS_DMOD_X = (16, 17, 12, 9, 8, 3)
ROWS_DMOD_C = (24, 25)
SMALL = ("c_ctx", "b_ada", "attn_sink", "gmlp_ln_g", "gmlp_ln_b", "w_spatial", "b_spatial", "ln1_g", "ln1_b", "ln2_g", "ln2_b")


def _adamw_small(sums, dsc, w, m, v):
    n = len(SMALL)

    def body(*refs):
        st_ref, gm_ref, sk_ref, ws_ref, bs_ref, dsc_ref = refs[:6]
        w_refs = dict(zip(SMALL, refs[6:6 + n]))
        m_refs = dict(zip(SMALL, refs[6 + n:6 + 2 * n]))
        v_refs = dict(zip(SMALL, refs[6 + 2 * n:6 + 3 * n]))
        outs = refs[6 + 3 * n:]
        c = w_refs["c_ctx"][...]
        sg = _sig(c)
        dmod = [st_ref[r:r + 1, :] for r in ROWS_DMOD_X]
        dmod[0] = dmod[0] + st_ref[ROWS_DMOD_C[0]:ROWS_DMOD_C[0] + 1, :]
        dmod[1] = dmod[1] + st_ref[ROWS_DMOD_C[1]:ROWS_DMOD_C[1] + 1, :]
        grads = dict(
            c_ctx=dsc_ref[0:1, :] * (sg * (1.0 + c * (1.0 - sg))),
            b_ada=jnp.concatenate(dmod, axis=1),
            attn_sink=sk_ref[0:1, 0:N_KV * GROUP],
            gmlp_ln_g=gm_ref[0:1, :], gmlp_ln_b=gm_ref[1:2, :],
            w_spatial=ws_ref[...], b_spatial=bs_ref[...],
            ln1_g=st_ref[ROW_LN1_G:ROW_LN1_G + 1, :], ln1_b=st_ref[ROW_LN1_B:ROW_LN1_B + 1, :],
            ln2_g=st_ref[ROW_LN2_G:ROW_LN2_G + 1, :], ln2_b=st_ref[ROW_LN2_B:ROW_LN2_B + 1, :])
        for i, name in enumerate(SMALL):
            g = grads[name]
            d, nm, nv = _adam_update(w_refs[name][...], g, m_refs[name][...], v_refs[name][...])
            outs[i][...] = g
            outs[n + i][...] = d
            outs[2 * n + i][...] = nm
            outs[3 * n + i][...] = nv

    vmem = pl.BlockSpec(memory_space=pltpu.VMEM)
    args = list(sums) + [dsc] + [w[k] for k in SMALL] + [m[k] for k in SMALL] + [v[k] for k in SMALL]
    shapes = [jax.ShapeDtypeStruct(w[k].shape, F32) for k in SMALL]
    out = pl.pallas_call(
        body, name="adamw_small", in_specs=[vmem] * len(args), out_specs=[vmem] * (4 * n), out_shape=shapes * 4,
        compiler_params=pltpu.CompilerParams(vmem_limit_bytes=VMEM_LIMIT),
    )(*args)
    return [dict(zip(SMALL, out[i * n:(i + 1) * n])) for i in range(4)]


def _adamw_halves(w, mine, theirs, m, v, c_arr, name):
    r, c = w.shape
    tr = min(256, r // 2)
    while (r // 2) % tr:
        tr -= 8
    nt = (r // 2) // tr

    def body(c_ref, w_ref, mine_ref, theirs_ref, m_ref, v_ref, g_ref, d_ref, nm_ref, nv_ref):
        g = jnp.where(pl.program_id(0) == c_ref[0], mine_ref[...], theirs_ref[...])
        g_ref[...] = g
        d_ref[...], nm_ref[...], nv_ref[...] = _adam_update(w_ref[...], g, m_ref[...], v_ref[...])

    whole = pl.BlockSpec((tr, c), lambda hb, i, c_ref: (hb * nt + i, 0))
    half = pl.BlockSpec((tr, c), lambda hb, i, c_ref: (i, 0))
    shp = jax.ShapeDtypeStruct((r, c), F32)
    return pl.pallas_call(
        body, name=name,
        grid_spec=pltpu.PrefetchScalarGridSpec(
            num_scalar_prefetch=1, grid=(2, nt), in_specs=[whole, half, half, whole, whole], out_specs=[whole] * 4),
        out_shape=[shp] * 4,
        compiler_params=_cp("arbitrary", "arbitrary"),
    )(c_arr, w, mine, theirs, m, v)


def _adamw(w, g, m, v, name):
    r, c = w.shape
    tr = r if r * c <= 256 * 1024 else min(256, r)
    while r % tr:
        tr -= 8

    def body(w_ref, g_ref, m_ref, v_ref, d_ref, nm_ref, nv_ref):
        d_ref[...], nm_ref[...], nv_ref[...] = _adam_update(w_ref[...], g_ref[...], m_ref[...], v_ref[...])

    spec = pl.BlockSpec((tr, c), lambda i: (i, 0))
    shp = jax.ShapeDtypeStruct((r, c), F32)
    return pl.pallas_call(
        body, name=name, grid=(r // tr,), in_specs=[spec] * 4, out_specs=[spec] * 3, out_shape=[shp] * 3,
        compiler_params=_cp("arbitrary"),
    )(w, g, m, v)


def _my_pos():
    return lax.axis_index("x"), lax.axis_index("y"), lax.axis_index("c")


N_COPY = 7


class _Gather:
    def __init__(self, x_refs, out_refs, send_sems, recv_sems):
        self.x_refs, self.out_refs = x_refs, out_refs
        self.send_sems, self.recv_sems = send_sems, recv_sems
        x, y, c = _my_pos()
        self.c = c
        self.me, self.sibling = (x, y, c), (x, y, 1 - c)
        self.chips = [(1 - x, y), (x, 1 - y), (1 - x, 1 - y)]

    def _copy(self, a, k, block, to, from_input=False):
        px, py, pc = block
        rows = self.out_refs[a].at[4 * px + 2 * py + pc]
        return pltpu.make_async_remote_copy(
            src_ref=self.x_refs[a] if from_input else rows, dst_ref=rows,
            send_sem=self.send_sems.at[a * N_COPY + k], recv_sem=self.recv_sems.at[a * N_COPY + k],
            device_id=to, device_id_type=MESH)

    def start(self):
        n = len(self.x_refs)
        for a in range(n):
            self._copy(a, 0, self.me, self.sibling, from_input=True).start()
        for j, chip in enumerate(self.chips):
            for a in range(n):
                self._copy(a, 1 + j, self.me, (*chip, self.c), from_input=True).start()

    def finish(self):
        n = len(self.x_refs)
        c = self.c
        for j, chip in enumerate(self.chips):
            for a in range(n):
                self._copy(a, 1 + j, (*chip, c), self.me).wait_recv()
                self._copy(a, 4 + j, (*chip, c), self.sibling).start()
        for a in range(n):
            self._copy(a, 0, self.sibling, self.me).wait_recv()
        for j, chip in enumerate(self.chips):
            for a in range(n):
                self._copy(a, 4 + j, (*chip, 1 - c), self.me).wait_recv()
        for a in range(n):
            self._copy(a, 0, self.me, self.sibling, from_input=True).wait_send()
            for j, chip in enumerate(self.chips):
                self._copy(a, 1 + j, self.me, (*chip, c), from_input=True).wait_send()
                self._copy(a, 4 + j, (*chip, c), self.sibling).wait_send()


def _comm_scratch(n):
    return [pltpu.SemaphoreType.DMA((n * N_COPY,)), pltpu.SemaphoreType.DMA((n * N_COPY,))]


def _comm_specs(n):
    return [pl.BlockSpec(memory_space=pl.ANY)] * n


def _gathered_shapes(xs):
    return [jax.ShapeDtypeStruct((N_DEV,) + v.shape, v.dtype) for v in xs]


def _with_own(gathered, xs, me):
    return [lax.dynamic_update_index_in_dim(g, v, me, 0) for g, v in zip(gathered, xs)]


def _all_gather(xs, me, name):
    n = len(xs)

    def body(*refs):
        g = _Gather(refs[:n], refs[n:2 * n], *refs[2 * n:])
        g.start()
        g.finish()

    out = pl.pallas_call(
        body, name=name, out_shape=_gathered_shapes(xs), in_specs=_comm_specs(n), out_specs=_comm_specs(n),
        scratch_shapes=_comm_scratch(n),
    )(*xs)
    return _with_own(out, xs, me)


class _AllToAll:
    def __init__(self, x_refs, out_refs, send_sems, recv_sems):
        self.x_refs, self.out_refs = x_refs, out_refs
        self.send_sems, self.recv_sems = send_sems, recv_sems
        self.pos = _my_pos()
        x, y, c = self.pos
        self.me = 4 * x + 2 * y + c

    def _peer(self, k):
        x, y, c = self.pos
        return (x ^ ((k >> 2) & 1), y ^ ((k >> 1) & 1), c ^ (k & 1))

    def _copy(self, a, k):
        p = self._peer(k)
        return pltpu.make_async_remote_copy(
            src_ref=self.x_refs[a].at[4 * p[0] + 2 * p[1] + p[2]], dst_ref=self.out_refs[a].at[self.me],
            send_sem=self.send_sems.at[a * N_COPY + k - 1], recv_sem=self.recv_sems.at[a * N_COPY + k - 1],
            device_id=p, device_id_type=MESH)

    def start(self):
        for k in range(1, N_DEV):
            for a in range(len(self.x_refs)):
                self._copy(a, k).start()

    def finish(self):
        for a in range(len(self.x_refs)):
            for k in range(1, N_DEV):
                self._copy(a, k).wait_recv()
            for k in range(1, N_DEV):
                self._copy(a, k).wait_send()


def _all_to_all(blocks, name):
    n = len(blocks)

    def body(*refs):
        t = _AllToAll(refs[:n], refs[n:2 * n], *refs[2 * n:])
        t.start()
        t.finish()

    return pl.pallas_call(
        body, name=name, out_shape=[jax.ShapeDtypeStruct(v.shape, v.dtype) for v in blocks],
        in_specs=_comm_specs(n), out_specs=_comm_specs(n), scratch_shapes=_comm_scratch(n),
    )(*blocks)


def _sibling_exchange(xs, name):
    n = len(xs)

    def body(*refs):
        x_refs, out_refs = refs[:n], refs[n:2 * n]
        send_sems, recv_sems = refs[2 * n:]
        x, y, c = _my_pos()

        def push(a):
            return pltpu.make_async_remote_copy(
                src_ref=x_refs[a], dst_ref=out_refs[a], send_sem=send_sems.at[a], recv_sem=recv_sems.at[a],
                device_id=(x, y, 1 - c), device_id_type=MESH)

        for a in range(n):
            push(a).start()
        for a in range(n):
            push(a).wait_recv()
            push(a).wait_send()

    return pl.pallas_call(
        body, name=name, out_shape=[jax.ShapeDtypeStruct(v.shape, v.dtype) for v in xs],
        in_specs=_comm_specs(n), out_specs=_comm_specs(n),
        scratch_shapes=[pltpu.SemaphoreType.DMA((n,)), pltpu.SemaphoreType.DMA((n,))],
    )(*xs)


def _scatter_and_gather(scatter, gather, name):
    ns, ng = len(scatter), len(gather)

    def body(*refs):
        s_in, g_in = refs[:ns], refs[ns:ns + ng]
        s_out, g_out = refs[ns + ng:2 * ns + ng], refs[2 * ns + ng:2 * (ns + ng)]
        s_send, s_recv, g_send, g_recv = refs[2 * (ns + ng):]
        g = _Gather(g_in, g_out, g_send, g_recv)
        t = _AllToAll(s_in, s_out, s_send, s_recv)
        g.start()
        t.start()
        g.finish()
        t.finish()

    out = pl.pallas_call(
        body, name=name,
        out_shape=[jax.ShapeDtypeStruct(v.shape, v.dtype) for v in scatter] + _gathered_shapes(gather),
        in_specs=_comm_specs(ns + ng), out_specs=_comm_specs(ns + ng),
        scratch_shapes=_comm_scratch(ns) + _comm_scratch(ng),
    )(*scatter, *gather)
    return out[:ns], out[ns:]


def _row_tile(seq, want):
    return min(want, seq)


def _local_step(x, ctx, tgt, mod_x, mod_c, wb, sink, gmlp_g, gmlp_b, w_s, b_s, ln1_g, ln1_b, ln2_g, ln2_b,
                later=None, me=None):
    seq = x.shape[0]
    on_mesh = me is not None
    modx1 = jnp.concatenate([mod_x[0:2], jnp.zeros((6, D), F32)], axis=0)
    modc = jnp.concatenate([mod_c[0:2], jnp.zeros((6, D), F32)], axis=0)
    vec = jnp.concatenate([mod_x[2:3], ln1_g, ln1_b, mod_x[3:6], ln2_g, ln2_b], axis=0)
    gp = jnp.concatenate([gmlp_g, gmlp_b, jnp.zeros((6, G_W), F32)], axis=0)
    ws_stack = w_s.reshape(N_GRP * BLK, BLK).astype(BF16)
    ws_stack_t = jnp.transpose(w_s, (2, 0, 1)).reshape(BLK, N_GRP * BLK).astype(BF16)
    bias_full = jnp.repeat(b_s.T, GRP_D, axis=1)
    cos, sin = _rope_tables(seq)
    w_in = wb["w_in"]
    w_kv = w_in[Q_W:Q_W + 2 * KV_W, :]
    tm_big = _row_tile(seq, 512)
    tm_ffn = _row_tile(seq, 256)

    hc, kvc, vac = _ctx_fwd(ctx, modc, w_kv)
    behind_proj = ("w_a", "w_b", "w_o") if on_mesh else ()
    behind_attn = ("w_fi", "w_fo") if on_mesh else ()
    (h, q, kv, va, uv, gab), got_proj = _proj_fwd(x, modx1, w_in, cos, sin, tm_big, gather=[later[n] for n in behind_proj])
    (ya, lse), got_attn = _attn_fwd(q, kv, va, kvc, vac, sink, gather=[later[n] for n in behind_attn])
    if on_mesh:
        wb = dict(wb)
        names = behind_proj + behind_attn
        for n, g in zip(names, _with_own(list(got_proj) + list(got_attn), [later[n] for n in names], me)):
            wb[n] = g.reshape(-1, g.shape[2]) if n in ROW_SHARDED else g.reshape(N_SHARD, 2 * g.shape[1], g.shape[2])
        for n in ("w_a", "w_b"):
            wb[n] = wb[n].transpose(1, 0, 2).reshape(wb[n].shape[1], D)
    a, b, mix, merged, yb = _mix_fwd(uv, gab, ya, gp, ws_stack, bias_full, wb["w_a"], wb["w_b"], wb["w_o"], tm_big)
    act, h2, dff, df, dr1, st_ffn = _ffn(x, mix, tgt, vec, wb["w_fi"], wb["w_fo"], tm_ffn)
    blocks, recv = {}, {}
    blocks["w_fo"] = _eighths(_tn_matmul(act, df, 512, "tn_w_ffn_out", BF16, tk=2048))
    if on_mesh:
        g_w_fi, (recv["w_fo"],) = _tn_matmul(h2, dff, FH_SHARD, "tn_w_ffn_in", BF16, shard_major=True, tk=2048,
                                            scatter=[blocks["w_fo"]])
    else:
        g_w_fi = _tn_matmul(h2, dff, FH_SHARD, "tn_w_ffn_in", BF16, shard_major=True, tk=2048)
    blocks["w_fi"] = _eighths(g_w_fi)
    (dya, dpb, dws, dbs_full, st4, g_w_o, g_w_a, g_w_b), got = _mix_bwd(
        dr1, a, b, gab, uv, merged, ya, yb, vec, gp, ws_stack, ws_stack_t, bias_full, wb["w_a"], wb["w_b"], wb["w_o"],
        tm_big, scatter=[blocks["w_fi"]] if on_mesh else ())
    recv.update(zip(("w_fi",), got))
    shard_major = [g.reshape(g.shape[0], N_SHARD, D // N_SHARD).transpose(1, 0, 2) for g in (g_w_a, g_w_b)]
    blocks.update(w_o=_eighths(g_w_o), w_a=_eighths(shard_major[0]), w_b=_eighths(shard_major[1]))
    mixer = ("w_o", "w_a", "w_b") if on_mesh else ()
    (dq, dkv, dkvc, dsink), got = _attn_bwd(q, kv, kvc, sink, dya, ya, lse, scatter=[blocks[n] for n in mixer])
    recv.update(zip(mixer, got))
    g_wkv_ctx, st0 = _ctx_bwd(dkvc, ctx, hc, w_kv)
    (dqkv, grad_x, st1), _ = _proj_bwd(dq, dkv, dpb, x, dr1, modx1, w_in, cos, sin, tm_big)
    dbs = jnp.sum(dbs_full.reshape(BLK, N_GRP, GRP_D), axis=2).T
    early = [jnp.concatenate([st_ffn, st0], axis=0), st4, dsink, dws, dbs]
    init = jnp.pad(g_wkv_ctx, ((Q_W, 0), (0, 0)))
    g_qkv = _tn_matmul(dqkv, h, D, "tn_w_in_qkv", BF16, init=init, tk=1024)
    if on_mesh:
        g_rest, early_gathered = _tn_matmul(dpb, h, D, "tn_w_in_rest", BF16, tk=1024, gather=early)
    else:
        g_rest, early_gathered = _tn_matmul(dpb, h, D, "tn_w_in_rest", BF16, tk=1024), None
    blocks["w_in"] = _eighths(jnp.concatenate([g_qkv, g_rest], axis=0))
    return grad_x, dict(early=early, early_gathered=early_gathered, late=st1), blocks, recv


BIG = ("w_in", "w_a", "w_b", "w_o", "w_fi", "w_fo")
ROW_SHARDED = ("w_o", "w_fo")


def _half_of_shard(shard, c):
    r = shard.shape[0]
    return lax.dynamic_slice_in_dim(shard, c * (r // 2), r // 2, axis=0)


def _eighths(v):
    rows = v.shape[-2] * (v.shape[0] if v.ndim == 3 else 1)
    return v.reshape(N_DEV, rows // N_DEV, v.shape[-1])


def kernel(x, c, ctx, c_ctx, w_ada, b_ada, w_in, attn_sink, gmlp_ln_g, gmlp_ln_b, w_spatial, b_spatial, w_branch_a, w_branch_b, w_out, ln1_g, ln1_b, w_ffn_in, w_ffn_out, ln2_g, ln2_b, loss_target, m_c_ctx, m_w_ada, m_b_ada, m_w_in, m_attn_sink, m_gmlp_ln_g, m_gmlp_ln_b, m_w_spatial, m_b_spatial, m_w_branch_a, m_w_branch_b, m_w_out, m_ln1_g, m_ln1_b, m_w_ffn_in, m_w_ffn_out, m_ln2_g, m_ln2_b, v_c_ctx, v_w_ada, v_b_ada, v_w_in, v_attn_sink, v_gmlp_ln_g, v_gmlp_ln_b, v_w_spatial, v_b_spatial, v_w_branch_a, v_w_branch_b, v_w_out, v_ln1_g, v_ln1_b, v_w_ffn_in, v_w_ffn_out, v_ln2_g, v_ln2_b):
    mx, my, mc = _my_pos()
    me = 4 * mx + 2 * my + mc
    chip = 2 * mx + my
    shards = dict(w_in=w_in[0].T, w_a=w_branch_a[0], w_b=w_branch_b[0], w_o=w_out[0], w_fi=w_ffn_in[0], w_fo=w_ffn_out[0])

    halves = {n: _half_of_shard(shards[n], mc).astype(BF16) for n in BIG}
    c_rows = jnp.concatenate([c, jnp.zeros((7, D), F32)], axis=0)
    g_in, c_g = _all_gather([halves["w_in"], c_rows], me, "gather_w_in")
    wb = dict(w_in=g_in.reshape(IN_W, D))

    c_all = c_g[:, 0, :]
    cc = jnp.concatenate([c_all, c_ctx[None, :], jnp.zeros((7, D), F32)], axis=0)
    sig_cc = jax.nn.sigmoid(cc)
    sc_all = cc * sig_cc
    mod_shard = _ada_fwd(sc_all, w_ada[0])
    mod_g = _all_gather([mod_shard], me, "gather_mod")[0]
    mod_all = jnp.concatenate([mod_g[2 * s] for s in range(4)], axis=1) + b_ada
    mod_x = lax.dynamic_slice_in_dim(mod_all, me, 1, axis=0).reshape(6, D)
    mod_c = mod_all[8].reshape(6, D)[0:2]

    grad_x, small, blocks, recv = _local_step(
        x[0], ctx[0], loss_target[0], mod_x, mod_c, wb, attn_sink, gmlp_ln_g, gmlp_ln_b, w_spatial[0], b_spatial[0],
        ln1_g, ln1_b, ln2_g, ln2_b, later=halves, me=me)

    (recv["w_in"],), late = _scatter_and_gather([blocks["w_in"]], [small["late"]], "scatter_w_in_gather_small")
    late = _with_own(late, [small["late"]], me)[0]
    gathered = _with_own(small["early_gathered"], small["early"], me)
    gathered[0] = jnp.concatenate([gathered[0][:, :16], late, gathered[0][:, 16:]], axis=1)

    me_arr = jnp.reshape(me, (1,)).astype(jnp.int32)
    summed = {n: _sum_blocks(recv[n], blocks[n], me_arr, "sum_grads_" + n) for n in BIG}
    theirs = dict(zip(BIG, _sibling_exchange([summed[n] for n in BIG], "exchange_grads")))

    sums = _sum8_many(gathered, "sum_small")
    stats = sums[0]
    loss = 0.5 * jnp.sum(stats[ROW_LOSS]) / D
    dmod_x_all = jnp.concatenate([gathered[0][:, r_, :] for r_ in ROWS_DMOD_X], axis=1)
    dmod_c_full = jnp.concatenate([stats[r_] for r_ in ROWS_DMOD_C] + [jnp.zeros((4 * D,), F32)])
    dm_rows = jnp.concatenate([dmod_x_all, dmod_c_full[None, :], jnp.zeros((7, 6 * D), F32)], axis=0)
    cs = w_ada.shape[2]
    dm_shard = lax.dynamic_slice_in_dim(dm_rows, chip * cs, cs, axis=1)
    dmc_shard = jnp.concatenate([dm_shard[8:9], jnp.zeros((7, cs), F32)], axis=0)
    g_w_ada, part = _ada_bwd(sc_all.T, dm_shard, dmc_shard, w_ada[0])
    part_all = _all_gather([part * (mc == 0).astype(F32)], me, "gather_c_ctx")[0]
    dsc = _sum8(part_all, "sum_c_ctx")

    grads = dict(w_ada=g_w_ada[None])
    weights = dict(c_ctx=c_ctx, w_ada=w_ada, b_ada=b_ada, w_in=w_in, attn_sink=attn_sink, gmlp_ln_g=gmlp_ln_g,
                   gmlp_ln_b=gmlp_ln_b, w_spatial=w_spatial, b_spatial=b_spatial, w_branch_a=w_branch_a,
                   w_branch_b=w_branch_b, w_out=w_out, ln1_g=ln1_g, ln1_b=ln1_b, w_ffn_in=w_ffn_in, w_ffn_out=w_ffn_out,
                   ln2_g=ln2_g, ln2_b=ln2_b)
    ms = dict(c_ctx=m_c_ctx, w_ada=m_w_ada, b_ada=m_b_ada, w_in=m_w_in, attn_sink=m_attn_sink, gmlp_ln_g=m_gmlp_ln_g,
              gmlp_ln_b=m_gmlp_ln_b, w_spatial=m_w_spatial, b_spatial=m_b_spatial, w_branch_a=m_w_branch_a,
              w_branch_b=m_w_branch_b, w_out=m_w_out, ln1_g=m_ln1_g, ln1_b=m_ln1_b, w_ffn_in=m_w_ffn_in,
              w_ffn_out=m_w_ffn_out, ln2_g=m_ln2_g, ln2_b=m_ln2_b)
    vs = dict(c_ctx=v_c_ctx, w_ada=v_w_ada, b_ada=v_b_ada, w_in=v_w_in, attn_sink=v_attn_sink, gmlp_ln_g=v_gmlp_ln_g,
              gmlp_ln_b=v_gmlp_ln_b, w_spatial=v_w_spatial, b_spatial=v_b_spatial, w_branch_a=v_w_branch_a,
              w_branch_b=v_w_branch_b, w_out=v_w_out, ln1_g=v_ln1_g, ln1_b=v_ln1_b, w_ffn_in=v_w_ffn_in,
              w_ffn_out=v_w_ffn_out, ln2_g=v_ln2_g, ln2_b=v_ln2_b)
    order = list(weights)
    delta, new_m, new_v = {}, {}, {}
    d_, m_, v_ = _adamw(w_ada[0], g_w_ada, m_w_ada[0], v_w_ada[0], "adamw_w_ada")
    delta["w_ada"], new_m["w_ada"], new_v["w_ada"] = d_[None], m_[None], v_[None]
    c_arr = jnp.reshape(mc, (1,)).astype(jnp.int32)
    names = dict(w_in="w_in", w_a="w_branch_a", w_b="w_branch_b", w_o="w_out", w_fi="w_ffn_in", w_fo="w_ffn_out")
    for k, n in names.items():
        flip = (lambda t: t.T) if k == "w_in" else (lambda t: t)
        outs = _adamw_halves(flip(weights[n][0]), summed[k], theirs[k], flip(ms[n][0]), flip(vs[n][0]), c_arr, "adamw_" + n)
        grads[n], delta[n], new_m[n], new_v[n] = [flip(t)[None] for t in outs]

    def view(a):
        return a.reshape(-1, a.shape[-1]) if a.ndim != 1 else a.reshape(1, -1)

    small = _adamw_small(sums, dsc, *[{n: view(d[n]) for n in SMALL} for d in (weights, ms, vs)])
    for out, src in zip((grads, delta, new_m, new_v), small):
        for n in SMALL:
            out[n] = src[n].reshape(weights[n].shape)

    return (loss, grad_x[None], *[grads[n] for n in order], *[delta[n] for n in order],
            *[new_m[n] for n in order], *[new_v[n] for n in order])
```

```python
import functools
import math

import jax
import jax.numpy as jnp
from jax import lax
from jax.experimental import pallas as pl
from jax.experimental.pallas import tpu as pltpu

F32 = jnp.float32
BF16 = jnp.bfloat16

D = 1024
HEAD = 64
N_KV = 2
GROUP = 4
Q_W = 512
KV_W = 128
G_W = 512
BLK = 128
N_GRP = 8
GRP_D = 64
FH = 2816
IN_W = 3840
GRID_W = 64
ROPE_BASE = 10000.0
LN_EPS = 1e-5
NEG = -1e30
ALPHA = (2 * 1) ** 0.25
SCALE = HEAD ** -0.5
GELU_K = math.sqrt(2.0 / math.pi)
GELU_A = 0.044715
ADAM_LR = 0.001
ADAM_B1 = 0.9
ADAM_B2 = 0.999
ADAM_EPS = 1e-08
ADAM_WD = 0.01
ADAM_STEP = 10
N_DEV = 8
N_SHARD = 4
FH_SHARD = FH // 2
LANES = 128
VMEM_LIMIT = 56 * 1024 * 1024
MESH = pl.DeviceIdType.MESH


def _cp(*sem):
    return pltpu.CompilerParams(dimension_semantics=sem, vmem_limit_bytes=VMEM_LIMIT)


def _resident(shape):
    return pl.BlockSpec(shape, lambda *_: (0,) * len(shape), pipeline_mode=pl.Buffered(1))


def _rows(tm, width):
    return pl.BlockSpec((tm, width), lambda i: (i, 0))


def _acc(shape):
    return pl.BlockSpec(shape, lambda *_: (0,) * len(shape))


def _dot(a, b):
    return jnp.dot(a, b, preferred_element_type=F32)


def _dot_nt(a, b):
    return lax.dot_general(a, b, (((1,), (1,)), ((), ())), preferred_element_type=F32)


def _dot_tn(a, b):
    return lax.dot_general(a, b, (((0,), (0,)), ((), ())), preferred_element_type=F32)


def _ln(x):
    mu = jnp.mean(x, axis=-1, keepdims=True)
    xc = x - mu
    var = jnp.mean(xc * xc, axis=-1, keepdims=True)
    rstd = lax.rsqrt(var + LN_EPS)
    return xc * rstd, rstd


def _ln_bwd(dxhat, xhat, rstd):
    return (dxhat - jnp.mean(dxhat, axis=-1, keepdims=True)
            - xhat * jnp.mean(dxhat * xhat, axis=-1, keepdims=True)) * rstd


def _sig(x):
    return 0.5 + 0.5 * jnp.tanh(0.5 * x)


def _gelu(x):
    t = jnp.tanh(x * (GELU_K + (GELU_K * GELU_A) * (x * x)))
    hx = 0.5 * x
    return hx + hx * t, t


def _gelu_grad(x, t):
    return 0.5 + 0.5 * t + (0.5 * x) * (1.0 - t * t) * (GELU_K + (3.0 * GELU_K * GELU_A) * (x * x))


def _colsum(v):
    return jnp.sum(v, axis=0, keepdims=True)


def _partner(x):
    w = x.shape[1]
    lane = lax.broadcasted_iota(jnp.int32, x.shape, 1)
    return jnp.where((lane & 31) < 16, pltpu.roll(x, w - 16, 1), pltpu.roll(x, 16, 1))


def _rope(x, cos, sin):
    return x * cos + _partner(x) * sin


def _unrope(g, cos, sin):
    return g * cos + _partner(g * sin)


def _rope_tables(seq):
    inv = ROPE_BASE ** (-jnp.arange(HEAD // 4, dtype=F32) / (HEAD // 4))
    pos = jnp.arange(seq, dtype=jnp.int32)
    ar = (pos // GRID_W).astype(F32)[:, None] * inv
    ac = (pos % GRID_W).astype(F32)[:, None] * inv
    cos = jnp.concatenate([jnp.cos(ar), jnp.cos(ar), jnp.cos(ac), jnp.cos(ac)], axis=-1)
    sin = jnp.concatenate([-jnp.sin(ar), jnp.sin(ar), -jnp.sin(ac), jnp.sin(ac)], axis=-1)
    return jnp.tile(cos, (1, LANES // HEAD)), jnp.tile(sin, (1, LANES // HEAD))


def _ctx_fwd(ctx, modc, w_kv):
    n_ctx = ctx.shape[0]

    def body(ctx_ref, mod_ref, w_ref, hc_ref, kvc_ref, vac_ref):
        xhat, _ = _ln(ctx_ref[...])
        hc = (xhat * (1.0 + mod_ref[1:2, :]) + mod_ref[0:1, :]).astype(BF16)
        hc_ref[...] = hc
        kvc = _dot_nt(hc, w_ref[...]).astype(BF16)
        kvc_ref[...] = kvc
        vac_ref[...] = _with_ones(kvc[:, KV_W:])

    return pl.pallas_call(
        body, name="ctx_fwd", grid=(1,),
        in_specs=[_acc((n_ctx, D)), _acc((8, D)), _acc((2 * KV_W, D))],
        out_specs=[_acc((n_ctx, D)), _acc((n_ctx, 2 * KV_W)), _acc((n_ctx, 2 * LANES))],
        out_shape=[jax.ShapeDtypeStruct((n_ctx, D), BF16), jax.ShapeDtypeStruct((n_ctx, 2 * KV_W), BF16),
                   jax.ShapeDtypeStruct((n_ctx, 2 * LANES), BF16)],
        compiler_params=_cp("arbitrary"),
    )(ctx, modc, w_kv)


def _host_start(step, comm):
    if comm is not None:
        @pl.when(step == 0)
        def _():
            comm.start()


def _host_finish(step, last, comm, forward_at=None):
    if comm is None:
        return
    if forward_at is None or forward_at >= last:
        @pl.when(step == last)
        def _():
            comm.finish()
    else:
        @pl.when(step == forward_at)
        def _():
            comm.forward()

        @pl.when(step == last)
        def _():
            comm.drain()


def _proj_fwd(x, modx, w_in, cos, sin, tm, gather=()):
    seq = x.shape[0]
    ng = len(gather)

    def body(x_ref, mod_ref, w_ref, cos_ref, sin_ref, *rest):
        h_ref, q_ref, kv_ref, va_ref, uv_ref, gab_ref = rest[ng:ng + 6]
        comm = _Gather(rest[:ng], rest[ng + 6:2 * ng + 6], *rest[2 * ng + 6:]) if ng else None
        _host_start(pl.program_id(0), comm)
        xhat, _ = _ln(x_ref[...])
        h = (xhat * (1.0 + mod_ref[1:2, :]) + mod_ref[0:1, :]).astype(BF16)
        h_ref[...] = h
        cos1, sin1 = cos_ref[...], sin_ref[...]
        cos2 = jnp.concatenate([cos1, cos1], axis=1)
        sin2 = jnp.concatenate([sin1, sin1], axis=1)
        for j in range(Q_W // 256):
            t = _dot_nt(h, w_ref[256 * j:256 * (j + 1), :])
            q_ref[:, 256 * j:256 * (j + 1)] = (_rope(t, cos2, sin2) * SCALE).astype(BF16)
        t = _dot_nt(h, w_ref[Q_W:Q_W + 2 * KV_W, :])
        kv_ref[:, :KV_W] = _rope(t[:, :KV_W], cos1, sin1).astype(BF16)
        v = t[:, KV_W:].astype(BF16)
        kv_ref[:, KV_W:] = v
        va_ref[...] = _with_ones(v)
        o = Q_W + 2 * KV_W
        for j in range(2):
            uv_ref[:, G_W * j:G_W * (j + 1)] = _dot_nt(h, w_ref[o + G_W * j:o + G_W * (j + 1), :]).astype(BF16)
        o += 2 * G_W
        for j in range(4):
            gab_ref[:, 512 * j:512 * (j + 1)] = _dot_nt(h, w_ref[o + 512 * j:o + 512 * (j + 1), :]).astype(BF16)
        _host_finish(pl.program_id(0), seq // tm - 1, comm, forward_at=(seq // tm) // 2)

    out = pl.pallas_call(
        body, name="proj_fwd", grid=(seq // tm,),
        in_specs=[_rows(tm, D), _acc((8, D)), _resident((IN_W, D)), _rows(tm, LANES), _rows(tm, LANES)] + _comm_specs(ng),
        out_specs=[_rows(tm, D), _rows(tm, Q_W), _rows(tm, 2 * KV_W), _rows(tm, 2 * LANES), _rows(tm, 2 * G_W),
                   _rows(tm, 2 * D)] + _comm_specs(ng),
        out_shape=[jax.ShapeDtypeStruct((seq, D), BF16), jax.ShapeDtypeStruct((seq, Q_W), BF16),
                   jax.ShapeDtypeStruct((seq, 2 * KV_W), BF16), jax.ShapeDtypeStruct((seq, 2 * LANES), BF16),
                   jax.ShapeDtypeStruct((seq, 2 * G_W), BF16), jax.ShapeDtypeStruct((seq, 2 * D), BF16)] + _gathered_shapes(gather),
        scratch_shapes=_comm_scratch(ng) if ng else [],
        compiler_params=_cp("arbitrary"),
    )(x, modx, w_in, cos, sin, *gather)
    return out[:6], out[6:]


def _stack_heads(x, hk):
    return jnp.concatenate([x[:, (hk * GROUP + g) * HEAD:(hk * GROUP + g + 1) * HEAD] for g in range(GROUP)], axis=0)


def _band_masks(n, nb):
    rows = GROUP * BLK
    qi = lax.broadcasted_iota(jnp.int32, (rows, BLK), 0) & (BLK - 1)
    kj = lax.broadcasted_iota(jnp.int32, (rows, BLK), 1)
    return (kj >= qi) & (n > 0), (kj <= qi) & (n < nb - 1)


def _attn_scores(q, k_refs, hk, masks):
    q4 = _stack_heads(q, hk)
    ks = [r[:, hk * HEAD:(hk + 1) * HEAD] for r in k_refs]
    s = [_dot_nt(q4, k) for k in ks]
    s[1] = jnp.where(masks[0], s[1], NEG)
    s[3] = jnp.where(masks[1], s[3], NEG)
    return q4, ks, s


def _sink_rows(sink_ref, hk):
    rows = GROUP * BLK
    rg = lax.broadcasted_iota(jnp.int32, (rows, 1), 0) >> 7
    sink_v = jnp.full((rows, 1), sink_ref[0, hk * GROUP], F32)
    for g in range(1, GROUP):
        sink_v = jnp.where(rg == g, sink_ref[0, hk * GROUP + g], sink_v)
    return sink_v


def _with_ones(v):
    ones = jnp.ones((v.shape[0], HEAD), v.dtype)
    return jnp.concatenate([v[:, :HEAD], ones, v[:, HEAD:], ones], axis=1)


def _kv_specs(nb, qb):
    def spec(d):
        return pl.BlockSpec((BLK, 2 * KV_W), lambda n: (jnp.clip(qb * n + d, 0, nb - 1), 0))
    return [spec(d) for d in range(-1, qb + 1)]


def _attn_fwd(q, kv, va, kvc, vac, sink, gather=()):
    seq = q.shape[0]
    nb = seq // BLK
    n_ctx = kvc.shape[0]
    ng = len(gather)
    Q_BLOCKS = 1
    nkv = Q_BLOCKS + 2
    steps = nb // Q_BLOCKS

    def body(q_ref, *rest):
        kv_refs, va_refs = rest[:nkv], rest[nkv:2 * nkv]
        kvc_ref, vac_ref, sink_ref = rest[2 * nkv:2 * nkv + 3]
        rest = rest[2 * nkv + 3:]
        o_ref, lse_ref = rest[ng:ng + 2]
        comm = _Gather(rest[:ng], rest[ng + 2:2 * ng + 2], *rest[2 * ng + 2:]) if ng else None
        n = pl.program_id(0)
        _host_start(n, comm)
        lane = lax.broadcasted_iota(jnp.int32, (BLK, LANES), 1)
        for sub in range(Q_BLOCKS):
            rs = slice(sub * BLK, (sub + 1) * BLK)
            q = q_ref[rs, :]
            outs = []
            lse_all = jnp.zeros((BLK, LANES), F32)
            masks = _band_masks(Q_BLOCKS * n + sub, nb)
            for hk in range(N_KV):
                _, _, s = _attn_scores(q, (kvc_ref,) + kv_refs[sub:sub + 3], hk, masks)
                sink_v = _sink_rows(sink_ref, hk)
                tile_max = s[1]
                for t in [s[0][:, i * LANES:(i + 1) * LANES] for i in range(n_ctx // LANES)] + s[2:]:
                    tile_max = jnp.maximum(tile_max, t)
                m = jnp.maximum(sink_v, jnp.max(tile_max, axis=-1, keepdims=True))
                o = jnp.zeros((GROUP * BLK, LANES), F32)
                for t, va_ref in zip(s, (vac_ref,) + va_refs[sub:sub + 3]):
                    o = o + _dot(jnp.exp((t - m).astype(BF16)), va_ref[:, hk * LANES:(hk + 1) * LANES])
                denom = o[:, HEAD:HEAD + 1] + jnp.exp(sink_v - m)
                o4 = o[:, :HEAD] * (1.0 / denom)
                lse4 = m + jnp.log(denom)
                for g in range(GROUP):
                    outs.append(o4[g * BLK:(g + 1) * BLK, :])
                    lse_all = jnp.where(lane == hk * GROUP + g, lse4[g * BLK:(g + 1) * BLK, :], lse_all)
            o_ref[rs, :] = jnp.concatenate(outs, axis=1).astype(BF16)
            lse_ref[rs, :] = lse_all
        _host_finish(n, steps - 1, comm, forward_at=(3 * steps) // 4)

    tq = Q_BLOCKS * BLK
    out = pl.pallas_call(
        body, name="attn_fwd", grid=(steps,),
        in_specs=[_rows(tq, Q_W)] + _kv_specs(nb, Q_BLOCKS) + _kv_specs(nb, Q_BLOCKS)
        + [_acc((n_ctx, 2 * KV_W)), _acc((n_ctx, 2 * LANES)), pl.BlockSpec(memory_space=pltpu.SMEM)] + _comm_specs(ng),
        out_specs=[_rows(tq, Q_W), _rows(tq, LANES)] + _comm_specs(ng),
        out_shape=[jax.ShapeDtypeStruct((seq, Q_W), BF16), jax.ShapeDtypeStruct((seq, LANES), F32)] + _gathered_shapes(gather),
        scratch_shapes=_comm_scratch(ng) if ng else [],
        compiler_params=_cp("arbitrary"),
    )(q, *([kv] * nkv), *([va] * nkv), kvc, vac, sink, *gather)
    return out[:2], out[2:]


def _gmlp_chunk(u, vb, gp_ref, ws_ref, bias_ref):
    gu, tu = _gelu(u)
    gv, tv = _gelu(vb)
    vhat, rstd = _ln(gv)
    vn = (vhat * gp_ref[0:1, :] + gp_ref[1:2, :]).astype(BF16)
    s = bias_ref[...] + jnp.concatenate(
        [_dot(ws_ref[g * BLK:(g + 1) * BLK, :], vn[:, g * GRP_D:(g + 1) * GRP_D]) for g in range(N_GRP)], axis=1)
    return gu, tu, tv, vhat, rstd, vn, s


def _mix_fwd(uv, gab, ya, gp, ws_stack, bias_full, w_a, w_b, w_o, tm, gather=()):
    seq = uv.shape[0]
    ng = len(gather)
    steps = seq // tm

    def body(uv_ref, gab_ref, ya_ref, gp_ref, ws_ref, bias_ref, wa_ref, wb_ref, wo_ref, *rest):
        a_ref, b_ref, mix_ref, merged_ref, yb_ref = rest[ng:ng + 5]
        comm = _Gather(rest[:ng], rest[ng + 5:2 * ng + 5], *rest[2 * ng + 5:]) if ng else None
        _host_start(pl.program_id(0), comm)
        for c in range(tm // BLK):
            rs = slice(c * BLK, (c + 1) * BLK)
            gu, _, _, _, _, _, s = _gmlp_chunk(uv_ref[rs, :G_W].astype(F32), uv_ref[rs, G_W:].astype(F32), gp_ref, ws_ref, bias_ref)
            yb_ref[rs, :] = (gu * s).astype(BF16)
        a = _dot(ya_ref[...], wa_ref[...])
        b = _dot(yb_ref[...], wb_ref[...])
        a_ref[...] = a.astype(BF16)
        b_ref[...] = b.astype(BF16)
        merged = (_sig(gab_ref[:, :D].astype(F32)) * a + _sig(gab_ref[:, D:].astype(F32)) * b).astype(BF16)
        merged_ref[...] = merged
        mix_ref[...] = _dot(merged, wo_ref[...])
        _host_finish(pl.program_id(0), steps - 1, comm, forward_at=(3 * steps) // 4)

    out = pl.pallas_call(
        body, name="mix_fwd", grid=(steps,),
        in_specs=[_rows(tm, 2 * G_W), _rows(tm, 2 * D), _rows(tm, Q_W), _acc((8, G_W)),
                  _resident((N_GRP * BLK, BLK)), _acc((BLK, G_W)),
                  _resident((Q_W, D)), _resident((G_W, D)), _resident((D, D))] + _comm_specs(ng),
        out_specs=[_rows(tm, D), _rows(tm, D), _rows(tm, D), _rows(tm, D), _rows(tm, G_W)] + _comm_specs(ng),
        out_shape=[jax.ShapeDtypeStruct((seq, D), BF16), jax.ShapeDtypeStruct((seq, D), BF16),
                   jax.ShapeDtypeStruct((seq, D), F32), jax.ShapeDtypeStruct((seq, D), BF16),
                   jax.ShapeDtypeStruct((seq, G_W), BF16)] + _gathered_shapes(gather),
        scratch_shapes=_comm_scratch(ng) if ng else [],
        compiler_params=_cp("arbitrary"),
    )(uv, gab, ya, gp, ws_stack, bias_full, w_a, w_b, w_o, *gather)
    return out[:5], out[5:]


FFN_CHUNK = 512


def _ffn_chunks():
    out = []
    for hh in range(2):
        off = 0
        while off < FH_SHARD:
            w = min(FFN_CHUNK, FH_SHARD - off)
            out.append((hh, off, w))
            off += w
    return out


def _mid_recompute(x_ref, mix_ref, vec_ref):
    r1 = ALPHA * x_ref[...] + vec_ref[0:1, :] * mix_ref[...]
    xh1, rstd1 = _ln(r1)
    xmid = xh1 * vec_ref[1:2, :] + vec_ref[2:3, :]
    xh2, rstd2 = _ln(xmid)
    return xh1, rstd1, xmid, xh2, rstd2


def _ffn(x, mix, tgt, vec, w_fi, w_fo, tm):
    seq = x.shape[0]

    def body(x_ref, mix_ref, tgt_ref, vec_ref, wi_ref, wo_ref, act_ref, h2_ref, dff_ref, df_ref, dr1_ref, st_ref, gu_ref):
        @pl.when(pl.program_id(0) == 0)
        def _():
            st_ref[...] = jnp.zeros_like(st_ref)

        xh1, rstd1, xmid, xh2, rstd2 = _mid_recompute(x_ref, mix_ref, vec_ref)
        h2 = (xh2 * (1.0 + vec_ref[4:5, :]) + vec_ref[3:4, :]).astype(BF16)
        h2_ref[...] = h2
        halves = [(slice(hh * FH_SHARD, (hh + 1) * FH_SHARD), slice(FH + hh * FH_SHARD, FH + (hh + 1) * FH_SHARD))
                  for hh in range(2)]
        for hh, (cs, cu) in enumerate(halves):
            g = _dot(h2, wi_ref[hh])
            u = _dot(h2, wi_ref[2 + hh])
            gu_ref[:, cs] = g
            gu_ref[:, cu] = u
            act_ref[:, cs] = (g * _sig(g) * u).astype(BF16)
        f = _dot(act_ref[...], wo_ref[...])
        r2 = ALPHA * xmid + vec_ref[5:6, :] * f
        yh, rstd = _ln(r2)
        y = yh * vec_ref[6:7, :] + vec_ref[7:8, :]
        err = y - tgt_ref[...]
        dy = err / D
        dr2 = _ln_bwd(dy * vec_ref[6:7, :], yh, rstd)
        st_ref[0:1, :] += _colsum(err * err)
        st_ref[1:2, :] += _colsum(dy * yh)
        st_ref[2:3, :] += _colsum(dy)
        st_ref[3:4, :] += _colsum(dr2 * f)

        df = (dr2 * vec_ref[5:6, :]).astype(BF16)
        df_ref[...] = df
        da_all = _dot_nt(df, wo_ref[...])
        for cs, cu in halves:
            da = da_all[:, cs]
            g = gu_ref[:, cs]
            u = gu_ref[:, cu]
            sg = _sig(g)
            dff_ref[:, cs] = (da * u * sg * (1.0 + g * (1.0 - sg))).astype(BF16)
            dff_ref[:, cu] = (da * g * sg).astype(BF16)
        dh2 = _dot_nt(dff_ref[:, :FH_SHARD], wi_ref[0])
        for s in range(1, N_SHARD):
            dh2 = dh2 + _dot_nt(dff_ref[:, s * FH_SHARD:(s + 1) * FH_SHARD], wi_ref[s])
        dxmid = _ln_bwd(dh2 * (1.0 + vec_ref[4:5, :]), xh2, rstd2) + ALPHA * dr2
        dr1 = _ln_bwd(dxmid * vec_ref[1:2, :], xh1, rstd1)
        dr1_ref[...] = dr1
        st_ref[8:9, :] += _colsum(dh2 * xh2)
        st_ref[9:10, :] += _colsum(dh2)
        st_ref[10:11, :] += _colsum(dxmid * xh1)
        st_ref[11:12, :] += _colsum(dxmid)
        st_ref[12:13, :] += _colsum(dr1 * mix_ref[...])

    return pl.pallas_call(
        body, name="ffn", grid=(seq // tm,),
        in_specs=[_rows(tm, D), _rows(tm, D), _rows(tm, D), _acc((8, D)), _resident((N_SHARD, D, FH_SHARD)), _resident((FH, D))],
        out_specs=[_rows(tm, FH), _rows(tm, D), _rows(tm, 2 * FH), _rows(tm, D), _rows(tm, D), _acc((16, D))],
        out_shape=[jax.ShapeDtypeStruct((seq, FH), BF16), jax.ShapeDtypeStruct((seq, D), BF16),
                   jax.ShapeDtypeStruct((seq, 2 * FH), BF16), jax.ShapeDtypeStruct((seq, D), BF16),
                   jax.ShapeDtypeStruct((seq, D), F32), jax.ShapeDtypeStruct((16, D), F32)],
        scratch_shapes=[pltpu.VMEM((tm, 2 * FH), F32)],
        compiler_params=_cp("arbitrary"),
    )(x, mix, tgt, vec, w_fi, w_fo)


def _mix_bwd(dr1, a, b, gab, uv, merged, ya, yb, vec, gp, ws_stack, ws_stack_t, bias_full, w_a, w_b, w_o, tm, scatter=()):
    seq = dr1.shape[0]
    last = seq // tm - 1
    ns = len(scatter)

    def body(dr1_ref, a_ref, b_ref, gab_ref, uv_ref, mg_ref, ya_ref, yb_ref, vec_ref, gp_ref, ws_ref, wst_ref, bias_ref,
             wa_ref, wb_ref, wo_ref, *rest):
        dya_ref, dp_ref, dws_ref, dbs_ref, st_ref, gwo_ref, gwa_ref, gwb_ref = rest[ns:ns + 8]
        acc_o, acc_a, acc_b = rest[2 * ns + 8:2 * ns + 11]
        comm = _AllToAll(rest[:ns], rest[ns + 8:2 * ns + 8], *rest[2 * ns + 11:]) if ns else None
        _host_start(pl.program_id(0), comm)

        @pl.when(pl.program_id(0) == 0)
        def _():
            dws_ref[...] = jnp.zeros_like(dws_ref)
            dbs_ref[...] = jnp.zeros_like(dbs_ref)
            st_ref[...] = jnp.zeros_like(st_ref)
            acc_o[...] = jnp.zeros_like(acc_o)
            acc_a[...] = jnp.zeros_like(acc_a)
            acc_b[...] = jnp.zeros_like(acc_b)

        dmix = (dr1_ref[...] * vec_ref[0:1, :]).astype(BF16)
        acc_o[...] += _dot_tn(mg_ref[...], dmix)
        dmerged = _dot_nt(dmix, wo_ref[...])
        sa = _sig(gab_ref[:, :D].astype(F32))
        sb = _sig(gab_ref[:, D:].astype(F32))
        da = (dmerged * sa).astype(BF16)
        db = (dmerged * sb).astype(BF16)
        dp_ref[:, 2 * G_W:2 * G_W + D] = (dmerged * a_ref[...].astype(F32) * sa * (1.0 - sa)).astype(BF16)
        dp_ref[:, 2 * G_W + D:] = (dmerged * b_ref[...].astype(F32) * sb * (1.0 - sb)).astype(BF16)
        dya_ref[...] = _dot_nt(da, wa_ref[...]).astype(BF16)
        dyb = _dot_nt(db, wb_ref[...])
        acc_a[...] += _dot_tn(ya_ref[...], da)
        acc_b[...] += _dot_tn(yb_ref[...], db)

        @pl.when(pl.program_id(0) == last)
        def _():
            gwo_ref[...] = acc_o[...].astype(BF16)
            gwa_ref[...] = acc_a[...].astype(BF16)
            gwb_ref[...] = acc_b[...].astype(BF16)

        for c in range(tm // BLK):
            rs = slice(c * BLK, (c + 1) * BLK)
            u = uv_ref[rs, :G_W].astype(F32)
            vb = uv_ref[rs, G_W:].astype(F32)
            gu, tu, tv, vhat, rstd, vn, s = _gmlp_chunk(u, vb, gp_ref, ws_ref, bias_ref)
            dyb_c = dyb[rs, :]
            ds = dyb_c * gu
            du = dyb_c * s * _gelu_grad(u, tu)
            ds_b = ds.astype(BF16)
            dvn_g = []
            for g in range(N_GRP):
                cg = slice(g * GRP_D, (g + 1) * GRP_D)
                dvn_g.append(_dot(wst_ref[:, g * BLK:(g + 1) * BLK], ds_b[:, cg]))
                dws_ref[g * BLK:(g + 1) * BLK, :] += _dot_nt(ds_b[:, cg], vn[:, cg])
            dvn = jnp.concatenate(dvn_g, axis=1)
            dbs_ref[...] += ds
            st_ref[0:1, :] += _colsum(dvn * vhat)
            st_ref[1:2, :] += _colsum(dvn)
            dgv = _ln_bwd(dvn * gp_ref[0:1, :], vhat, rstd)
            dvb = dgv * _gelu_grad(vb, tv)
            dp_ref[rs, :G_W] = du.astype(BF16)
            dp_ref[rs, G_W:2 * G_W] = dvb.astype(BF16)
        _host_finish(pl.program_id(0), last, comm)

    pw = 2 * G_W + 2 * D
    out = pl.pallas_call(
        body, name="mix_bwd", grid=(seq // tm,),
        in_specs=[_rows(tm, D), _rows(tm, D), _rows(tm, D), _rows(tm, 2 * D), _rows(tm, 2 * G_W), _rows(tm, D), _rows(tm, Q_W),
                  _rows(tm, G_W), _acc((8, D)), _acc((8, G_W)),
                  _resident((N_GRP * BLK, BLK)), _resident((BLK, N_GRP * BLK)), _acc((BLK, G_W)),
                  _resident((Q_W, D)), _resident((G_W, D)), _resident((D, D))] + _comm_specs(ns),
        out_specs=[_rows(tm, Q_W), _rows(tm, pw), _acc((N_GRP * BLK, BLK)), _acc((BLK, G_W)), _acc((8, G_W)),
                   _acc((D, D)), _acc((Q_W, D)), _acc((G_W, D))] + _comm_specs(ns),
        out_shape=[jax.ShapeDtypeStruct((seq, Q_W), BF16), jax.ShapeDtypeStruct((seq, pw), BF16),
                   jax.ShapeDtypeStruct((N_GRP * BLK, BLK), F32), jax.ShapeDtypeStruct((BLK, G_W), F32),
                   jax.ShapeDtypeStruct((8, G_W), F32), jax.ShapeDtypeStruct((D, D), BF16),
                   jax.ShapeDtypeStruct((Q_W, D), BF16), jax.ShapeDtypeStruct((G_W, D), BF16)]
        + [jax.ShapeDtypeStruct(v.shape, v.dtype) for v in scatter],
        scratch_shapes=[pltpu.VMEM((D, D), F32), pltpu.VMEM((Q_W, D), F32), pltpu.VMEM((G_W, D), F32)]
        + (_comm_scratch(ns) if ns else []),
        compiler_params=_cp("arbitrary"),
    )(dr1, a, b, gab, uv, merged, ya, yb, vec, gp, ws_stack, ws_stack_t, bias_full, w_a, w_b, w_o, *scatter)
    return out[:8], out[8:]


def _attn_bwd(q, kv, kvc, sink, dya, ya, lse, scatter=()):
    seq = q.shape[0]
    nb = seq // BLK
    n_ctx = kvc.shape[0]
    ns = len(scatter)
    Q_BLOCKS = 2
    nkv = Q_BLOCKS + 2
    steps = nb // Q_BLOCKS

    def body(q_ref, *rest):
        kv_refs = rest[:nkv]
        kvc_ref, sink_ref, do_ref, o_ref, lse_ref = rest[nkv:nkv + 5]
        rest = rest[nkv + 5:]
        dq_ref, dkv_ref, dkvc_ref, dsink_ref = rest[ns:ns + 4]
        comm = _AllToAll(rest[:ns], rest[ns + 4:2 * ns + 4], *rest[2 * ns + 4:]) if ns else None
        n = pl.program_id(0)
        _host_start(n, comm)

        @pl.when(n == 0)
        def _():
            dkv_ref[...] = jnp.zeros_like(dkv_ref)
            dkvc_ref[...] = jnp.zeros_like(dkvc_ref)
            dsink_ref[...] = jnp.zeros_like(dsink_ref)

        lane = lax.broadcasted_iota(jnp.int32, (1, LANES), 1)
        for sub in range(Q_BLOCKS):
            rs = slice(sub * BLK, (sub + 1) * BLK)
            blk = Q_BLOCKS * n + sub
            q = q_ref[rs, :]
            do = do_ref[rs, :]
            out = o_ref[rs, :]
            lse_all = lse_ref[rs, :]
            k_refs = (kvc_ref,) + kv_refs[sub:sub + 3]
            masks = _band_masks(blk, nb)
            dqs, dks, dvs = [], [], []
            for hk in range(N_KV):
                q4, ks, s = _attn_scores(q, k_refs, hk, masks)
                vs = [r[:, KV_W + hk * HEAD:KV_W + (hk + 1) * HEAD] for r in k_refs]
                lse4 = jnp.concatenate([lse_all[:, hk * GROUP + g:hk * GROUP + g + 1] for g in range(GROUP)], axis=0)
                do4 = _stack_heads(do, hk)
                delta = jnp.sum(do4.astype(F32) * _stack_heads(out, hk).astype(F32), axis=-1, keepdims=True)
                p = [jnp.exp((t - lse4).astype(BF16)) for t in s]
                ds = [t * (_dot_nt(do4, v) - delta).astype(BF16) for t, v in zip(p, vs)]
                dq4 = _dot(ds[0], ks[0])
                for t, k in zip(ds[1:], ks[1:]):
                    dq4 = dq4 + _dot(t, k)
                dq4 = dq4 * SCALE
                dqs += [dq4[g * BLK:(g + 1) * BLK, :] for g in range(GROUP)]
                dks.append([_dot_tn(t, q4) for t in ds])
                dvs.append([_dot_tn(t, do4) for t in p])
                ps = jnp.exp(_sink_rows(sink_ref, hk) - lse4) * delta
                for g in range(GROUP):
                    part = -jnp.sum(ps[g * BLK:(g + 1) * BLK, :], axis=0, keepdims=True)
                    dsink_ref[0:1, :] += jnp.where(lane == hk * GROUP + g, part, 0.0)
            dq_ref[rs, :] = jnp.concatenate(dqs, axis=1)

            def piece(i):
                return jnp.concatenate([dks[0][i], dks[1][i], dvs[0][i], dvs[1][i]], axis=1)

            dkvc_ref[...] += piece(0)
            starts = (jnp.maximum(blk - 1, 0), blk, jnp.minimum(blk + 1, nb - 1))
            for i, st in enumerate(starts):
                r = pl.ds(pl.multiple_of(st * BLK, BLK), BLK)
                dkv_ref[r, :] += piece(i + 1)
        _host_finish(n, steps - 1, comm)

    tq = Q_BLOCKS * BLK
    out = pl.pallas_call(
        body, name="attn_bwd", grid=(steps,),
        in_specs=[_rows(tq, Q_W)] + _kv_specs(nb, Q_BLOCKS) + [_acc((n_ctx, 2 * KV_W)), pl.BlockSpec(memory_space=pltpu.SMEM),
                                                     _rows(tq, Q_W), _rows(tq, Q_W), _rows(tq, LANES)] + _comm_specs(ns),
        out_specs=[_rows(tq, Q_W), _acc((seq, 2 * KV_W)), _acc((n_ctx, 2 * KV_W)), _acc((8, LANES))] + _comm_specs(ns),
        out_shape=[jax.ShapeDtypeStruct((seq, Q_W), F32), jax.ShapeDtypeStruct((seq, 2 * KV_W), F32),
                   jax.ShapeDtypeStruct((n_ctx, 2 * KV_W), F32), jax.ShapeDtypeStruct((8, LANES), F32)]
        + [jax.ShapeDtypeStruct(v.shape, v.dtype) for v in scatter],
        scratch_shapes=_comm_scratch(ns) if ns else [],
        compiler_params=_cp("arbitrary"),
    )(q, *([kv] * nkv), kvc, sink, dya, ya, lse, *scatter)
    return out[:4], out[4:]


def _proj_bwd(dq, dkv, dpb, x, dr1, modx, w_in, cos, sin, tm, scatter=()):
    seq = x.shape[0]
    pw = IN_W - Q_W - 2 * KV_W
    ns = len(scatter)

    def body(dq_ref, dkv_ref, dpb_ref, x_ref, dr1_ref, mod_ref, w_ref, cos_ref, sin_ref, *rest):
        dqkv_ref, gx_ref, st_ref = rest[ns:ns + 3]
        comm = _AllToAll(rest[:ns], rest[ns + 3:2 * ns + 3], *rest[2 * ns + 3:]) if ns else None
        _host_start(pl.program_id(0), comm)

        @pl.when(pl.program_id(0) == 0)
        def _():
            st_ref[...] = jnp.zeros_like(st_ref)

        cos1, sin1 = cos_ref[...], sin_ref[...]
        cos2 = jnp.concatenate([cos1, cos1], axis=1)
        sin2 = jnp.concatenate([sin1, sin1], axis=1)
        for j in range(Q_W // 256):
            cs = slice(256 * j, 256 * (j + 1))
            dqkv_ref[:, cs] = _unrope(dq_ref[:, cs], cos2, sin2).astype(BF16)
        dqkv_ref[:, Q_W:Q_W + KV_W] = _unrope(dkv_ref[:, :KV_W], cos1, sin1).astype(BF16)
        dqkv_ref[:, Q_W + KV_W:] = dkv_ref[:, KV_W:].astype(BF16)
        o = Q_W + 2 * KV_W
        dh = _dot(dqkv_ref[...], w_ref[:o, :]) + _dot(dpb_ref[...], w_ref[o:, :])
        xhat, rstd = _ln(x_ref[...])
        st_ref[0:1, :] += _colsum(dh)
        st_ref[1:2, :] += _colsum(dh * xhat)
        gx_ref[...] = _ln_bwd(dh * (1.0 + mod_ref[1:2, :]), xhat, rstd) + ALPHA * dr1_ref[...]
        _host_finish(pl.program_id(0), seq // tm - 1, comm)

    out = pl.pallas_call(
        body, name="proj_bwd", grid=(seq // tm,),
        in_specs=[_rows(tm, Q_W), _rows(tm, 2 * KV_W), _rows(tm, pw), _rows(tm, D), _rows(tm, D), _acc((8, D)),
                  _resident((IN_W, D)), _rows(tm, LANES), _rows(tm, LANES)] + _comm_specs(ns),
        out_specs=[_rows(tm, Q_W + 2 * KV_W), _rows(tm, D), _acc((8, D))] + _comm_specs(ns),
        out_shape=[jax.ShapeDtypeStruct((seq, Q_W + 2 * KV_W), BF16), jax.ShapeDtypeStruct((seq, D), F32),
                   jax.ShapeDtypeStruct((8, D), F32)] + [jax.ShapeDtypeStruct(v.shape, v.dtype) for v in scatter],
        scratch_shapes=_comm_scratch(ns) if ns else [],
        compiler_params=_cp("arbitrary"),
    )(dq, dkv, dpb, x, dr1, modx, w_in, cos, sin, *scatter)
    return out[:3], out[3:]


def _ctx_bwd(dkvc, ctx, hc, w_kv):
    n_ctx = ctx.shape[0]

    def body(dkvc_ref, ctx_ref, hc_ref, w_ref, dw_ref, st_ref):
        d = dkvc_ref[...].astype(BF16)
        dw_ref[...] = _dot_tn(d, hc_ref[...])
        dhc = _dot(d, w_ref[...])
        xhat, _ = _ln(ctx_ref[...])
        st_ref[...] = jnp.zeros_like(st_ref)
        st_ref[0:1, :] = _colsum(dhc)
        st_ref[1:2, :] = _colsum(dhc * xhat)

    return pl.pallas_call(
        body, name="ctx_bwd", grid=(1,),
        in_specs=[_acc((n_ctx, 2 * KV_W)), _acc((n_ctx, D)), _acc((n_ctx, D)), _acc((2 * KV_W, D))],
        out_specs=[_acc((2 * KV_W, D)), _acc((8, D))],
        out_shape=[jax.ShapeDtypeStruct((2 * KV_W, D), F32), jax.ShapeDtypeStruct((8, D), F32)],
        compiler_params=_cp("arbitrary"),
    )(dkvc, ctx, hc, w_kv)


def _tn_matmul(a, b, tn, name, out_dtype, shard_major=False, init=None, tk=512, scatter=(), gather=()):
    t, ka = a.shape
    n = b.shape[1]
    tk = min(tk, t)
    nk = t // tk
    nj = n // tn
    has_init = init is not None
    assert not (scatter and gather)
    moved = list(scatter) + list(gather)
    pattern = _AllToAll if scatter else _Gather
    ns = len(moved)
    n_in = 3 if has_init else 2

    def body(*refs):
        a_ref, b_ref = refs[:2]
        i_ref = refs[2] if has_init else None
        rest = refs[n_in:]
        o_ref = rest[ns]
        acc_ref = rest[2 * ns + 1]
        comm = pattern(rest[:ns], rest[ns + 1:2 * ns + 1], *rest[2 * ns + 2:]) if ns else None
        k = pl.program_id(1)
        step = pl.program_id(0) * nk + k
        _host_start(step, comm)

        @pl.when(k == 0)
        def _():
            acc_ref[...] = i_ref[...] if has_init else jnp.zeros_like(acc_ref)

        acc_ref[...] += _dot_tn(a_ref[...], b_ref[...])

        @pl.when(k == nk - 1)
        def _():
            o_ref[...] = acc_ref[...].astype(out_dtype)

        _host_finish(step, nj * nk - 1, comm)

    in_specs = [pl.BlockSpec((tk, ka), lambda j, k: (k, 0)), pl.BlockSpec((tk, tn), lambda j, k: (k, j))]
    args = [a, b]
    if has_init:
        in_specs.append(pl.BlockSpec((ka, tn), lambda j, k: (0, j)))
        args.append(init)
    if shard_major:
        out_spec = pl.BlockSpec((None, ka, tn), lambda j, k: (j, 0, 0))
        out_shape = jax.ShapeDtypeStruct((nj, ka, tn), out_dtype)
    else:
        out_spec = pl.BlockSpec((ka, tn), lambda j, k: (0, j))
        out_shape = jax.ShapeDtypeStruct((ka, n), out_dtype)
    out = pl.pallas_call(
        body, name=name, grid=(nj, nk), in_specs=in_specs + _comm_specs(ns), out_specs=[out_spec] + _comm_specs(ns),
        out_shape=[out_shape] + [jax.ShapeDtypeStruct(v.shape, v.dtype) for v in scatter] + _gathered_shapes(gather),
        scratch_shapes=[pltpu.VMEM((ka, tn), F32)] + (_comm_scratch(ns) if ns else []),
        compiler_params=_cp("arbitrary", "arbitrary"),
    )(*args, *moved)
    return (out[0], out[1:]) if ns else out[0]


ADA_TILE = 512


def _ada_fwd(sc_all, w_ada):
    cs = w_ada.shape[1]

    def body(s_ref, w_ref, o_ref):
        o_ref[...] = _dot(s_ref[...].astype(BF16), w_ref[...].astype(BF16))

    return pl.pallas_call(
        body, name="ada_fwd", grid=(cs // ADA_TILE,),
        in_specs=[_acc((16, D)), pl.BlockSpec((D, ADA_TILE), lambda j: (0, j))],
        out_specs=pl.BlockSpec((16, ADA_TILE), lambda j: (0, j)),
        out_shape=jax.ShapeDtypeStruct((16, cs), F32),
        compiler_params=_cp("arbitrary"),
    )(sc_all, w_ada)


def _ada_bwd(sc_all_t, dm_all, dmc, w_ada):
    cs = w_ada.shape[1]

    def body(st_ref, dm_ref, dmc_ref, w_ref, gw_ref, part_ref):
        @pl.when(pl.program_id(0) == 0)
        def _():
            part_ref[...] = jnp.zeros_like(part_ref)

        gw_ref[...] = _dot(st_ref[...].astype(BF16), dm_ref[...].astype(BF16))
        part_ref[...] += _dot_nt(dmc_ref[...].astype(BF16), w_ref[...].astype(BF16))

    return pl.pallas_call(
        body, name="ada_bwd", grid=(cs // ADA_TILE,),
        in_specs=[_acc((D, 16)), pl.BlockSpec((16, ADA_TILE), lambda j: (0, j)), pl.BlockSpec((8, ADA_TILE), lambda j: (0, j)),
                  pl.BlockSpec((D, ADA_TILE), lambda j: (0, j))],
        out_specs=[pl.BlockSpec((D, ADA_TILE), lambda j: (0, j)), _acc((8, D))],
        out_shape=[jax.ShapeDtypeStruct((D, cs), F32), jax.ShapeDtypeStruct((8, D), F32)],
        compiler_params=_cp("arbitrary"),
    )(sc_all_t, dm_all, dmc, w_ada)


def _sum8(x, name, tr=256):
    _, r, c = x.shape
    tr = min(tr, r)
    while r % tr:
        tr -= 16

    def body(x_ref, o_ref):
        acc = x_ref[0].astype(F32)
        for i in range(1, N_DEV):
            acc = acc + x_ref[i].astype(F32)
        o_ref[...] = acc

    return pl.pallas_call(
        body, name=name, grid=(r // tr,),
        in_specs=[pl.BlockSpec((N_DEV, tr, c), lambda i: (0, i, 0))],
        out_specs=pl.BlockSpec((tr, c), lambda i: (i, 0)),
        out_shape=jax.ShapeDtypeStruct((r, c), F32),
        compiler_params=_cp("arbitrary"),
    )(x)


def _sum_blocks(recv, src, me, name, tr=256):
    _, r, c = recv.shape
    tr = min(tr, r)
    while r % tr:
        tr -= 16

    def body(me_ref, recv_ref, own_ref, o_ref):
        acc = own_ref[...].astype(F32)
        for k in range(1, N_DEV):
            acc = acc + recv_ref[me_ref[0] ^ k].astype(F32)
        o_ref[...] = acc

    return pl.pallas_call(
        body, name=name,
        grid_spec=pltpu.PrefetchScalarGridSpec(
            num_scalar_prefetch=1, grid=(r // tr,),
            in_specs=[pl.BlockSpec((N_DEV, tr, c), lambda i, me_ref: (0, i, 0)),
                      pl.BlockSpec((None, tr, c), lambda i, me_ref: (me_ref[0], i, 0))],
            out_specs=pl.BlockSpec((tr, c), lambda i, me_ref: (i, 0))),
        out_shape=jax.ShapeDtypeStruct((r, c), F32),
        compiler_params=_cp("arbitrary"),
    )(me, recv, src)


def _sum8_many(xs, name):
    n = len(xs)

    def body(*refs):
        for x_ref, o_ref in zip(refs[:n], refs[n:]):
            acc = x_ref[0]
            for i in range(1, N_DEV):
                acc = acc + x_ref[i]
            o_ref[...] = acc

    vmem = pl.BlockSpec(memory_space=pltpu.VMEM)
    return pl.pallas_call(
        body, name=name, in_specs=[vmem] * n, out_specs=[vmem] * n,
        out_shape=[jax.ShapeDtypeStruct(v.shape[1:], v.dtype) for v in xs],
        compiler_params=pltpu.CompilerParams(vmem_limit_bytes=VMEM_LIMIT),
    )(*xs)


def _adam_update(w, g, m, v):
    nm = ADAM_B1 * m + (1.0 - ADAM_B1) * g
    nv = ADAM_B2 * v + (1.0 - ADAM_B2) * (g * g)
    m_hat = nm / (1.0 - ADAM_B1 ** ADAM_STEP)
    v_hat = nv / (1.0 - ADAM_B2 ** ADAM_STEP)
    return -ADAM_LR * (m_hat / (jnp.sqrt(v_hat) + ADAM_EPS) + ADAM_WD * w), nm, nv


ROW_LOSS, ROW_LN2_G, ROW_LN2_B, ROW_LN1_G, ROW_LN1_B = 0, 1, 2, 10, 11
ROWS_DMOD_X = (16, 17, 12, 9, 8, 3)
ROWS_DMOD_C = (24, 25)
SMALL = ("c_ctx", "b_ada", "attn_sink", "gmlp_ln_g", "gmlp_ln_b", "w_spatial", "b_spatial", "ln1_g", "ln1_b", "ln2_g", "ln2_b")


def _adamw_small(sums, dsc, w, m, v):
    n = len(SMALL)

    def body(*refs):
        st_ref, gm_ref, sk_ref, ws_ref, bs_ref, dsc_ref = refs[:6]
        w_refs = dict(zip(SMALL, refs[6:6 + n]))
        m_refs = dict(zip(SMALL, refs[6 + n:6 + 2 * n]))
        v_refs = dict(zip(SMALL, refs[6 + 2 * n:6 + 3 * n]))
        outs = refs[6 + 3 * n:]
        c = w_refs["c_ctx"][...]
        sg = _sig(c)
        dmod = [st_ref[r:r + 1, :] for r in ROWS_DMOD_X]
        dmod[0] = dmod[0] + st_ref[ROWS_DMOD_C[0]:ROWS_DMOD_C[0] + 1, :]
        dmod[1] = dmod[1] + st_ref[ROWS_DMOD_C[1]:ROWS_DMOD_C[1] + 1, :]
        grads = dict(
            c_ctx=dsc_ref[0:1, :] * (sg * (1.0 + c * (1.0 - sg))),
            b_ada=jnp.concatenate(dmod, axis=1),
            attn_sink=sk_ref[0:1, 0:N_KV * GROUP],
            gmlp_ln_g=gm_ref[0:1, :], gmlp_ln_b=gm_ref[1:2, :],
            w_spatial=ws_ref[...], b_spatial=bs_ref[...],
            ln1_g=st_ref[ROW_LN1_G:ROW_LN1_G + 1, :], ln1_b=st_ref[ROW_LN1_B:ROW_LN1_B + 1, :],
            ln2_g=st_ref[ROW_LN2_G:ROW_LN2_G + 1, :], ln2_b=st_ref[ROW_LN2_B:ROW_LN2_B + 1, :])
        for i, name in enumerate(SMALL):
            g = grads[name]
            d, nm, nv = _adam_update(w_refs[name][...], g, m_refs[name][...], v_refs[name][...])
            outs[i][...] = g
            outs[n + i][...] = d
            outs[2 * n + i][...] = nm
            outs[3 * n + i][...] = nv

    vmem = pl.BlockSpec(memory_space=pltpu.VMEM)
    args = list(sums) + [dsc] + [w[k] for k in SMALL] + [m[k] for k in SMALL] + [v[k] for k in SMALL]
    shapes = [jax.ShapeDtypeStruct(w[k].shape, F32) for k in SMALL]
    out = pl.pallas_call(
        body, name="adamw_small", in_specs=[vmem] * len(args), out_specs=[vmem] * (4 * n), out_shape=shapes * 4,
        compiler_params=pltpu.CompilerParams(vmem_limit_bytes=VMEM_LIMIT),
    )(*args)
    return [dict(zip(SMALL, out[i * n:(i + 1) * n])) for i in range(4)]


def _adamw_halves(w, mine, theirs, m, v, c_arr, name):
    r, c = w.shape
    tr = min(256, r // 2)
    while (r // 2) % tr:
        tr -= 8
    nt = (r // 2) // tr

    def body(c_ref, w_ref, mine_ref, theirs_ref, m_ref, v_ref, g_ref, d_ref, nm_ref, nv_ref):
        g = jnp.where(pl.program_id(0) == c_ref[0], mine_ref[...], theirs_ref[...])
        g_ref[...] = g
        d_ref[...], nm_ref[...], nv_ref[...] = _adam_update(w_ref[...], g, m_ref[...], v_ref[...])

    whole = pl.BlockSpec((tr, c), lambda hb, i, c_ref: (hb * nt + i, 0))
    half = pl.BlockSpec((tr, c), lambda hb, i, c_ref: (i, 0))
    shp = jax.ShapeDtypeStruct((r, c), F32)
    return pl.pallas_call(
        body, name=name,
        grid_spec=pltpu.PrefetchScalarGridSpec(
            num_scalar_prefetch=1, grid=(2, nt), in_specs=[whole, half, half, whole, whole], out_specs=[whole] * 4),
        out_shape=[shp] * 4,
        compiler_params=_cp("arbitrary", "arbitrary"),
    )(c_arr, w, mine, theirs, m, v)


def _adamw(w, g, m, v, name):
    r, c = w.shape
    tr = r if r * c <= 256 * 1024 else min(256, r)
    while r % tr:
        tr -= 8

    def body(w_ref, g_ref, m_ref, v_ref, d_ref, nm_ref, nv_ref):
        d_ref[...], nm_ref[...], nv_ref[...] = _adam_update(w_ref[...], g_ref[...], m_ref[...], v_ref[...])

    spec = pl.BlockSpec((tr, c), lambda i: (i, 0))
    shp = jax.ShapeDtypeStruct((r, c), F32)
    return pl.pallas_call(
        body, name=name, grid=(r // tr,), in_specs=[spec] * 4, out_specs=[spec] * 3, out_shape=[shp] * 3,
        compiler_params=_cp("arbitrary"),
    )(w, g, m, v)


def _my_pos():
    return lax.axis_index("x"), lax.axis_index("y"), lax.axis_index("c")


N_COPY = 7


class _Gather:
    def __init__(self, x_refs, out_refs, send_sems, recv_sems):
        self.x_refs, self.out_refs = x_refs, out_refs
        self.send_sems, self.recv_sems = send_sems, recv_sems
        x, y, c = _my_pos()
        self.c = c
        self.me, self.sibling = (x, y, c), (x, y, 1 - c)
        self.chips = [(1 - x, y), (x, 1 - y), (1 - x, 1 - y)]

    def _copy(self, a, k, block, to, from_input=False):
        px, py, pc = block
        rows = self.out_refs[a].at[4 * px + 2 * py + pc]
        return pltpu.make_async_remote_copy(
            src_ref=self.x_refs[a] if from_input else rows, dst_ref=rows,
            send_sem=self.send_sems.at[a * N_COPY + k], recv_sem=self.recv_sems.at[a * N_COPY + k],
            device_id=to, device_id_type=MESH)

    def start(self):
        n = len(self.x_refs)
        for a in range(n):
            self._copy(a, 0, self.me, self.sibling, from_input=True).start()
        for j, chip in enumerate(self.chips):
            for a in range(n):
                self._copy(a, 1 + j, self.me, (*chip, self.c), from_input=True).start()

    def forward(self):
        c = self.c
        for j, chip in enumerate(self.chips):
            for a in range(len(self.x_refs)):
                self._copy(a, 1 + j, (*chip, c), self.me).wait_recv()
                self._copy(a, 4 + j, (*chip, c), self.sibling).start()

    def finish(self):
        self.forward()
        self.drain()

    def drain(self):
        n = len(self.x_refs)
        c = self.c
        for a in range(n):
            self._copy(a, 0, self.sibling, self.me).wait_recv()
        for j, chip in enumerate(self.chips):
            for a in range(n):
                self._copy(a, 4 + j, (*chip, 1 - c), self.me).wait_recv()
        for a in range(n):
            self._copy(a, 0, self.me, self.sibling, from_input=True).wait_send()
            for j, chip in enumerate(self.chips):
                self._copy(a, 1 + j, self.me, (*chip, c), from_input=True).wait_send()
                self._copy(a, 4 + j, (*chip, c), self.sibling).wait_send()


def _comm_scratch(n):
    return [pltpu.SemaphoreType.DMA((n * N_COPY,)), pltpu.SemaphoreType.DMA((n * N_COPY,))]


def _comm_specs(n):
    return [pl.BlockSpec(memory_space=pl.ANY)] * n


def _gathered_shapes(xs):
    return [jax.ShapeDtypeStruct((N_DEV,) + v.shape, v.dtype) for v in xs]


def _with_own(gathered, xs, me):
    return [lax.dynamic_update_index_in_dim(g, v, me, 0) for g, v in zip(gathered, xs)]


def _all_gather(xs, me, name):
    n = len(xs)

    def body(*refs):
        g = _Gather(refs[:n], refs[n:2 * n], *refs[2 * n:])
        g.start()
        g.finish()

    out = pl.pallas_call(
        body, name=name, out_shape=_gathered_shapes(xs), in_specs=_comm_specs(n), out_specs=_comm_specs(n),
        scratch_shapes=_comm_scratch(n),
    )(*xs)
    return _with_own(out, xs, me)


class _AllToAll:
    def __init__(self, x_refs, out_refs, send_sems, recv_sems):
        self.x_refs, self.out_refs = x_refs, out_refs
        self.send_sems, self.recv_sems = send_sems, recv_sems
        self.pos = _my_pos()
        x, y, c = self.pos
        self.me = 4 * x + 2 * y + c

    def _peer(self, k):
        x, y, c = self.pos
        return (x ^ ((k >> 2) & 1), y ^ ((k >> 1) & 1), c ^ (k & 1))

    def _copy(self, a, k):
        p = self._peer(k)
        return pltpu.make_async_remote_copy(
            src_ref=self.x_refs[a].at[4 * p[0] + 2 * p[1] + p[2]], dst_ref=self.out_refs[a].at[self.me],
            send_sem=self.send_sems.at[a * N_COPY + k - 1], recv_sem=self.recv_sems.at[a * N_COPY + k - 1],
            device_id=p, device_id_type=MESH)

    def start(self):
        for k in range(1, N_DEV):
            for a in range(len(self.x_refs)):
                self._copy(a, k).start()

    def finish(self):
        for a in range(len(self.x_refs)):
            for k in range(1, N_DEV):
                self._copy(a, k).wait_recv()
            for k in range(1, N_DEV):
                self._copy(a, k).wait_send()


def _all_to_all(blocks, name):
    n = len(blocks)

    def body(*refs):
        t = _AllToAll(refs[:n], refs[n:2 * n], *refs[2 * n:])
        t.start()
        t.finish()

    return pl.pallas_call(
        body, name=name, out_shape=[jax.ShapeDtypeStruct(v.shape, v.dtype) for v in blocks],
        in_specs=_comm_specs(n), out_specs=_comm_specs(n), scratch_shapes=_comm_scratch(n),
    )(*blocks)


def _sibling_exchange(xs, name):
    n = len(xs)

    def body(*refs):
        x_refs, out_refs = refs[:n], refs[n:2 * n]
        send_sems, recv_sems = refs[2 * n:]
        x, y, c = _my_pos()

        def push(a):
            return pltpu.make_async_remote_copy(
                src_ref=x_refs[a], dst_ref=out_refs[a], send_sem=send_sems.at[a], recv_sem=recv_sems.at[a],
                device_id=(x, y, 1 - c), device_id_type=MESH)

        for a in range(n):
            push(a).start()
        for a in range(n):
            push(a).wait_recv()
            push(a).wait_send()

    return pl.pallas_call(
        body, name=name, out_shape=[jax.ShapeDtypeStruct(v.shape, v.dtype) for v in xs],
        in_specs=_comm_specs(n), out_specs=_comm_specs(n),
        scratch_shapes=[pltpu.SemaphoreType.DMA((n,)), pltpu.SemaphoreType.DMA((n,))],
    )(*xs)


def _scatter_and_gather(scatter, gather, name):
    ns, ng = len(scatter), len(gather)

    def body(*refs):
        s_in, g_in = refs[:ns], refs[ns:ns + ng]
        s_out, g_out = refs[ns + ng:2 * ns + ng], refs[2 * ns + ng:2 * (ns + ng)]
        s_send, s_recv, g_send, g_recv = refs[2 * (ns + ng):]
        g = _Gather(g_in, g_out, g_send, g_recv)
        t = _AllToAll(s_in, s_out, s_send, s_recv)
        g.start()
        t.start()
        g.finish()
        t.finish()

    out = pl.pallas_call(
        body, name=name,
        out_shape=[jax.ShapeDtypeStruct(v.shape, v.dtype) for v in scatter] + _gathered_shapes(gather),
        in_specs=_comm_specs(ns + ng), out_specs=_comm_specs(ns + ng),
        scratch_shapes=_comm_scratch(ns) + _comm_scratch(ng),
    )(*scatter, *gather)
    return out[:ns], out[ns:]


def _row_tile(seq, want):
    return min(want, seq)


def _local_step(x, ctx, tgt, mod_x, mod_c, wb, sink, gmlp_g, gmlp_b, w_s, b_s, ln1_g, ln1_b, ln2_g, ln2_b,
                later=None, me=None):
    seq = x.shape[0]
    on_mesh = me is not None
    modx1 = jnp.concatenate([mod_x[0:2], jnp.zeros((6, D), F32)], axis=0)
    modc = jnp.concatenate([mod_c[0:2], jnp.zeros((6, D), F32)], axis=0)
    vec = jnp.concatenate([mod_x[2:3], ln1_g, ln1_b, mod_x[3:6], ln2_g, ln2_b], axis=0)
    gp = jnp.concatenate([gmlp_g, gmlp_b, jnp.zeros((6, G_W), F32)], axis=0)
    ws_stack = w_s.reshape(N_GRP * BLK, BLK).astype(BF16)
    ws_stack_t = jnp.transpose(w_s, (2, 0, 1)).reshape(BLK, N_GRP * BLK).astype(BF16)
    bias_full = jnp.repeat(b_s.T, GRP_D, axis=1)
    cos, sin = _rope_tables(seq)
    w_in = wb["w_in"]
    w_kv = w_in[Q_W:Q_W + 2 * KV_W, :]
    tm_big = _row_tile(seq, 512)
    tm_ffn = _row_tile(seq, 256)

    hc, kvc, vac = _ctx_fwd(ctx, modc, w_kv)
    behind_proj = ("w_a", "w_b", "w_o") if on_mesh else ()
    behind_attn = ("w_fi",) if on_mesh else ()
    behind_mix = ("w_fo",) if on_mesh else ()
    wb = dict(wb)

    def whole(names, gathered):
        for n, g in zip(names, _with_own(list(gathered), [later[n] for n in names], me)):
            wb[n] = g.reshape(-1, g.shape[2]) if n in ROW_SHARDED else g.reshape(N_SHARD, 2 * g.shape[1], g.shape[2])

    (h, q, kv, va, uv, gab), got = _proj_fwd(x, modx1, w_in, cos, sin, tm_big, gather=[later[n] for n in behind_proj])
    whole(behind_proj, got)
    if on_mesh:
        for n in ("w_a", "w_b"):
            wb[n] = wb[n].transpose(1, 0, 2).reshape(wb[n].shape[1], D)
    (ya, lse), got = _attn_fwd(q, kv, va, kvc, vac, sink, gather=[later[n] for n in behind_attn])
    whole(behind_attn, got)
    (a, b, mix, merged, yb), got = _mix_fwd(uv, gab, ya, gp, ws_stack, bias_full, wb["w_a"], wb["w_b"], wb["w_o"], tm_big,
                                            gather=[later[n] for n in behind_mix])
    whole(behind_mix, got)
    act, h2, dff, df, dr1, st_ffn = _ffn(x, mix, tgt, vec, wb["w_fi"], wb["w_fo"], tm_ffn)
    blocks, recv = {}, {}
    blocks["w_fo"] = _eighths(_tn_matmul(act, df, 512, "tn_w_ffn_out", BF16, tk=2048))
    if on_mesh:
        g_w_fi, (recv["w_fo"],) = _tn_matmul(h2, dff, FH_SHARD, "tn_w_ffn_in", BF16, shard_major=True, tk=2048,
                                            scatter=[blocks["w_fo"]])
    else:
        g_w_fi = _tn_matmul(h2, dff, FH_SHARD, "tn_w_ffn_in", BF16, shard_major=True, tk=2048)
    blocks["w_fi"] = _eighths(g_w_fi)
    (dya, dpb, dws, dbs_full, st4, g_w_o, g_w_a, g_w_b), got = _mix_bwd(
        dr1, a, b, gab, uv, merged, ya, yb, vec, gp, ws_stack, ws_stack_t, bias_full, wb["w_a"], wb["w_b"], wb["w_o"],
        tm_big, scatter=[blocks["w_fi"]] if on_mesh else ())
    recv.update(zip(("w_fi",), got))
    shard_major = [g.reshape(g.shape[0], N_SHARD, D // N_SHARD).transpose(1, 0, 2) for g in (g_w_a, g_w_b)]
    blocks.update(w_o=_eighths(g_w_o), w_a=_eighths(shard_major[0]), w_b=_eighths(shard_major[1]))
    mixer = ("w_o", "w_a", "w_b") if on_mesh else ()
    (dq, dkv, dkvc, dsink), got = _attn_bwd(q, kv, kvc, sink, dya, ya, lse, scatter=[blocks[n] for n in mixer])
    recv.update(zip(mixer, got))
    g_wkv_ctx, st0 = _ctx_bwd(dkvc, ctx, hc, w_kv)
    (dqkv, grad_x, st1), _ = _proj_bwd(dq, dkv, dpb, x, dr1, modx1, w_in, cos, sin, tm_big)
    dbs = jnp.sum(dbs_full.reshape(BLK, N_GRP, GRP_D), axis=2).T
    early = [jnp.concatenate([st_ffn, st0], axis=0), st4, dsink, dws, dbs]
    init = jnp.pad(g_wkv_ctx, ((Q_W, 0), (0, 0)))
    g_qkv = _tn_matmul(dqkv, h, D, "tn_w_in_qkv", BF16, init=init, tk=1024)
    if on_mesh:
        g_rest, early_gathered = _tn_matmul(dpb, h, D, "tn_w_in_rest", BF16, tk=1024, gather=early)
    else:
        g_rest, early_gathered = _tn_matmul(dpb, h, D, "tn_w_in_rest", BF16, tk=1024), None
    blocks["w_in"] = _eighths(jnp.concatenate([g_qkv, g_rest], axis=0))
    return grad_x, dict(early=early, early_gathered=early_gathered, late=st1), blocks, recv


BIG = ("w_in", "w_a", "w_b", "w_o", "w_fi", "w_fo")
ROW_SHARDED = ("w_o", "w_fo")


def _half_of_shard(shard, c):
    r = shard.shape[0]
    return lax.dynamic_slice_in_dim(shard, c * (r // 2), r // 2, axis=0)


def _eighths(v):
    rows = v.shape[-2] * (v.shape[0] if v.ndim == 3 else 1)
    return v.reshape(N_DEV, rows // N_DEV, v.shape[-1])


def kernel(x, c, ctx, c_ctx, w_ada, b_ada, w_in, attn_sink, gmlp_ln_g, gmlp_ln_b, w_spatial, b_spatial, w_branch_a, w_branch_b, w_out, ln1_g, ln1_b, w_ffn_in, w_ffn_out, ln2_g, ln2_b, loss_target, m_c_ctx, m_w_ada, m_b_ada, m_w_in, m_attn_sink, m_gmlp_ln_g, m_gmlp_ln_b, m_w_spatial, m_b_spatial, m_w_branch_a, m_w_branch_b, m_w_out, m_ln1_g, m_ln1_b, m_w_ffn_in, m_w_ffn_out, m_ln2_g, m_ln2_b, v_c_ctx, v_w_ada, v_b_ada, v_w_in, v_attn_sink, v_gmlp_ln_g, v_gmlp_ln_b, v_w_spatial, v_b_spatial, v_w_branch_a, v_w_branch_b, v_w_out, v_ln1_g, v_ln1_b, v_w_ffn_in, v_w_ffn_out, v_ln2_g, v_ln2_b):
    mx, my, mc = _my_pos()
    me = 4 * mx + 2 * my + mc
    chip = 2 * mx + my
    shards = dict(w_in=w_in[0].T, w_a=w_branch_a[0], w_b=w_branch_b[0], w_o=w_out[0], w_fi=w_ffn_in[0], w_fo=w_ffn_out[0])

    halves = {n: _half_of_shard(shards[n], mc).astype(BF16) for n in BIG}
    c_rows = jnp.concatenate([c, jnp.zeros((7, D), F32)], axis=0)
    g_in, c_g = _all_gather([halves["w_in"], c_rows], me, "gather_w_in")
    wb = dict(w_in=g_in.reshape(IN_W, D))

    c_all = c_g[:, 0, :]
    cc = jnp.concatenate([c_all, c_ctx[None, :], jnp.zeros((7, D), F32)], axis=0)
    sig_cc = jax.nn.sigmoid(cc)
    sc_all = cc * sig_cc
    mod_shard = _ada_fwd(sc_all, w_ada[0])
    mod_g = _all_gather([mod_shard], me, "gather_mod")[0]
    mod_all = jnp.concatenate([mod_g[2 * s] for s in range(4)], axis=1) + b_ada
    mod_x = lax.dynamic_slice_in_dim(mod_all, me, 1, axis=0).reshape(6, D)
    mod_c = mod_all[8].reshape(6, D)[0:2]

    grad_x, small, blocks, recv = _local_step(
        x[0], ctx[0], loss_target[0], mod_x, mod_c, wb, attn_sink, gmlp_ln_g, gmlp_ln_b, w_spatial[0], b_spatial[0],
        ln1_g, ln1_b, ln2_g, ln2_b, later=halves, me=me)

    (recv["w_in"],), late = _scatter_and_gather([blocks["w_in"]], [small["late"]], "scatter_w_in_gather_small")
    late = _with_own(late, [small["late"]], me)[0]
    gathered = _with_own(small["early_gathered"], small["early"], me)
    gathered[0] = jnp.concatenate([gathered[0][:, :16], late, gathered[0][:, 16:]], axis=1)

    me_arr = jnp.reshape(me, (1,)).astype(jnp.int32)
    summed = {n: _sum_blocks(recv[n], blocks[n], me_arr, "sum_grads_" + n) for n in BIG}
    theirs = dict(zip(BIG, _sibling_exchange([summed[n] for n in BIG], "exchange_grads")))

    sums = _sum8_many(gathered, "sum_small")
    stats = sums[0]
    loss = 0.5 * jnp.sum(stats[ROW_LOSS]) / D
    dmod_x_all = jnp.concatenate([gathered[0][:, r_, :] for r_ in ROWS_DMOD_X], axis=1)
    dmod_c_full = jnp.concatenate([stats[r_] for r_ in ROWS_DMOD_C] + [jnp.zeros((4 * D,), F32)])
    dm_rows = jnp.concatenate([dmod_x_all, dmod_c_full[None, :], jnp.zeros((7, 6 * D), F32)], axis=0)
    cs = w_ada.shape[2]
    dm_shard = lax.dynamic_slice_in_dim(dm_rows, chip * cs, cs, axis=1)
    dmc_shard = jnp.concatenate([dm_shard[8:9], jnp.zeros((7, cs), F32)], axis=0)
    g_w_ada, part = _ada_bwd(sc_all.T, dm_shard, dmc_shard, w_ada[0])
    part_all = _all_gather([part * (mc == 0).astype(F32)], me, "gather_c_ctx")[0]
    dsc = _sum8(part_all, "sum_c_ctx")

    grads = dict(w_ada=g_w_ada[None])
    weights = dict(c_ctx=c_ctx, w_ada=w_ada, b_ada=b_ada, w_in=w_in, attn_sink=attn_sink, gmlp_ln_g=gmlp_ln_g,
                   gmlp_ln_b=gmlp_ln_b, w_spatial=w_spatial, b_spatial=b_spatial, w_branch_a=w_branch_a,
                   w_branch_b=w_branch_b, w_out=w_out, ln1_g=ln1_g, ln1_b=ln1_b, w_ffn_in=w_ffn_in, w_ffn_out=w_ffn_out,
                   ln2_g=ln2_g, ln2_b=ln2_b)
    ms = dict(c_ctx=m_c_ctx, w_ada=m_w_ada, b_ada=m_b_ada, w_in=m_w_in, attn_sink=m_attn_sink, gmlp_ln_g=m_gmlp_ln_g,
              gmlp_ln_b=m_gmlp_ln_b, w_spatial=m_w_spatial, b_spatial=m_b_spatial, w_branch_a=m_w_branch_a,
              w_branch_b=m_w_branch_b, w_out=m_w_out, ln1_g=m_ln1_g, ln1_b=m_ln1_b, w_ffn_in=m_w_ffn_in,
              w_ffn_out=m_w_ffn_out, ln2_g=m_ln2_g, ln2_b=m_ln2_b)
    vs = dict(c_ctx=v_c_ctx, w_ada=v_w_ada, b_ada=v_b_ada, w_in=v_w_in, attn_sink=v_attn_sink, gmlp_ln_g=v_gmlp_ln_g,
              gmlp_ln_b=v_gmlp_ln_b, w_spatial=v_w_spatial, b_spatial=v_b_spatial, w_branch_a=v_w_branch_a,
              w_branch_b=v_w_branch_b, w_out=v_w_out, ln1_g=v_ln1_g, ln1_b=v_ln1_b, w_ffn_in=v_w_ffn_in,
              w_ffn_out=v_w_ffn_out, ln2_g=v_ln2_g, ln2_b=v_ln2_b)
    order = list(weights)
    delta, new_m, new_v = {}, {}, {}
    d_, m_, v_ = _adamw(w_ada[0], g_w_ada, m_w_ada[0], v_w_ada[0], "adamw_w_ada")
    delta["w_ada"], new_m["w_ada"], new_v["w_ada"] = d_[None], m_[None], v_[None]
    c_arr = jnp.reshape(mc, (1,)).astype(jnp.int32)
    names = dict(w_in="w_in", w_a="w_branch_a", w_b="w_branch_b", w_o="w_out", w_fi="w_ffn_in", w_fo="w_ffn_out")
    for k, n in names.items():
        flip = (lambda t: t.T) if k == "w_in" else (lambda t: t)
        outs = _adamw_halves(flip(weights[n][0]), summed[k], theirs[k], flip(ms[n][0]), flip(vs[n][0]), c_arr, "adamw_" + n)
        grads[n], delta[n], new_m[n], new_v[n] = [flip(t)[None] for t in outs]

    def view(a):
        return a.reshape(-1, a.shape[-1]) if a.ndim != 1 else a.reshape(1, -1)

    small = _adamw_small(sums, dsc, *[{n: view(d[n]) for n in SMALL} for d in (weights, ms, vs)])
    for out, src in zip((grads, delta, new_m, new_v), small):
        for n in SMALL:
            out[n] = src[n].reshape(weights[n].shape)

    return (loss, grad_x[None], *[grads[n] for n in order], *[delta[n] for n in order],
            *[new_m[n] for n in order], *[new_v[n] for n in order])
```

```python
import functools
import math

import jax
import jax.numpy as jnp
from jax import lax
from jax.experimental import pallas as pl
from jax.experimental.pallas import tpu as pltpu

F32 = jnp.float32
BF16 = jnp.bfloat16

D = 1024
HEAD = 64
N_KV = 2
GROUP = 4
Q_W = 512
KV_W = 128
G_W = 512
BLK = 128
N_GRP = 8
GRP_D = 64
FH = 2816
IN_W = 3840
GRID_W = 64
ROPE_BASE = 10000.0
LN_EPS = 1e-5
NEG = -1e30
ALPHA = (2 * 1) ** 0.25
SCALE = HEAD ** -0.5
GELU_K = math.sqrt(2.0 / math.pi)
GELU_A = 0.044715
ADAM_LR = 0.001
ADAM_B1 = 0.9
ADAM_B2 = 0.999
ADAM_EPS = 1e-08
ADAM_WD = 0.01
ADAM_STEP = 10
N_DEV = 8
N_SHARD = 4
FH_SHARD = FH // 2
LANES = 128
VMEM_LIMIT = 56 * 1024 * 1024
MESH = pl.DeviceIdType.MESH


def _cp(*sem):
    return pltpu.CompilerParams(dimension_semantics=sem, vmem_limit_bytes=VMEM_LIMIT)


def _resident(shape):
    return pl.BlockSpec(shape, lambda *_: (0,) * len(shape), pipeline_mode=pl.Buffered(1))


def _rows(tm, width):
    return pl.BlockSpec((tm, width), lambda i: (i, 0))


def _acc(shape):
    return pl.BlockSpec(shape, lambda *_: (0,) * len(shape))


def _dot(a, b):
    return jnp.dot(a, b, preferred_element_type=F32)


def _dot_nt(a, b):
    return lax.dot_general(a, b, (((1,), (1,)), ((), ())), preferred_element_type=F32)


def _dot_tn(a, b):
    return lax.dot_general(a, b, (((0,), (0,)), ((), ())), preferred_element_type=F32)


def _ln(x):
    mu = jnp.mean(x, axis=-1, keepdims=True)
    xc = x - mu
    var = jnp.mean(xc * xc, axis=-1, keepdims=True)
    rstd = lax.rsqrt(var + LN_EPS)
    return xc * rstd, rstd


def _ln_bwd(dxhat, xhat, rstd):
    return (dxhat - jnp.mean(dxhat, axis=-1, keepdims=True)
            - xhat * jnp.mean(dxhat * xhat, axis=-1, keepdims=True)) * rstd


def _sig(x):
    return 0.5 + 0.5 * jnp.tanh(0.5 * x)


def _gelu(x):
    t = jnp.tanh(x * (GELU_K + (GELU_K * GELU_A) * (x * x)))
    hx = 0.5 * x
    return hx + hx * t, t


def _gelu_grad(x, t):
    return 0.5 + 0.5 * t + (0.5 * x) * (1.0 - t * t) * (GELU_K + (3.0 * GELU_K * GELU_A) * (x * x))


def _colsum(v):
    return jnp.sum(v, axis=0, keepdims=True)


def _partner(x):
    w = x.shape[1]
    lane = lax.broadcasted_iota(jnp.int32, x.shape, 1)
    return jnp.where((lane & 31) < 16, pltpu.roll(x, w - 16, 1), pltpu.roll(x, 16, 1))


def _rope(x, cos, sin):
    return x * cos + _partner(x) * sin


def _unrope(g, cos, sin):
    return g * cos + _partner(g * sin)


def _rope_tables(seq):
    inv = ROPE_BASE ** (-jnp.arange(HEAD // 4, dtype=F32) / (HEAD // 4))
    pos = jnp.arange(seq, dtype=jnp.int32)
    ar = (pos // GRID_W).astype(F32)[:, None] * inv
    ac = (pos % GRID_W).astype(F32)[:, None] * inv
    cos = jnp.concatenate([jnp.cos(ar), jnp.cos(ar), jnp.cos(ac), jnp.cos(ac)], axis=-1)
    sin = jnp.concatenate([-jnp.sin(ar), jnp.sin(ar), -jnp.sin(ac), jnp.sin(ac)], axis=-1)
    return jnp.tile(cos, (1, LANES // HEAD)), jnp.tile(sin, (1, LANES // HEAD))


def _ctx_fwd(ctx, modc, w_kv):
    n_ctx = ctx.shape[0]

    def body(ctx_ref, mod_ref, w_ref, hc_ref, kvc_ref, vac_ref):
        xhat, _ = _ln(ctx_ref[...])
        hc = (xhat * (1.0 + mod_ref[1:2, :]) + mod_ref[0:1, :]).astype(BF16)
        hc_ref[...] = hc
        kvc = _dot_nt(hc, w_ref[...]).astype(BF16)
        kvc_ref[...] = kvc
        vac_ref[...] = _with_ones(kvc[:, KV_W:])

    return pl.pallas_call(
        body, name="ctx_fwd", grid=(1,),
        in_specs=[_acc((n_ctx, D)), _acc((8, D)), _acc((2 * KV_W, D))],
        out_specs=[_acc((n_ctx, D)), _acc((n_ctx, 2 * KV_W)), _acc((n_ctx, 2 * LANES))],
        out_shape=[jax.ShapeDtypeStruct((n_ctx, D), BF16), jax.ShapeDtypeStruct((n_ctx, 2 * KV_W), BF16),
                   jax.ShapeDtypeStruct((n_ctx, 2 * LANES), BF16)],
        compiler_params=_cp("arbitrary"),
    )(ctx, modc, w_kv)


def _host_start(step, comm):
    if comm is not None:
        @pl.when(step == 0)
        def _():
            comm.start()


def _host_finish(step, last, comm, forward_at=None):
    if comm is None:
        return
    if forward_at is None or forward_at >= last:
        @pl.when(step == last)
        def _():
            comm.finish()
    else:
        @pl.when(step == forward_at)
        def _():
            comm.forward()

        @pl.when(step == last)
        def _():
            comm.drain()


def _proj_fwd(x, modx, w_in, cos, sin, tm, gather=()):
    seq = x.shape[0]
    ng = len(gather)

    def body(x_ref, mod_ref, w_ref, cos_ref, sin_ref, *rest):
        h_ref, q_ref, kv_ref, va_ref, uv_ref, gab_ref = rest[ng:ng + 6]
        comm = _Gather(rest[:ng], rest[ng + 6:2 * ng + 6], *rest[2 * ng + 6:]) if ng else None
        _host_start(pl.program_id(0), comm)
        xhat, _ = _ln(x_ref[...])
        h = (xhat * (1.0 + mod_ref[1:2, :]) + mod_ref[0:1, :]).astype(BF16)
        h_ref[...] = h
        cos1, sin1 = cos_ref[...], sin_ref[...]
        cos2 = jnp.concatenate([cos1, cos1], axis=1)
        sin2 = jnp.concatenate([sin1, sin1], axis=1)
        for j in range(Q_W // 256):
            t = _dot_nt(h, w_ref[256 * j:256 * (j + 1), :])
            q_ref[:, 256 * j:256 * (j + 1)] = (_rope(t, cos2, sin2) * SCALE).astype(BF16)
        t = _dot_nt(h, w_ref[Q_W:Q_W + 2 * KV_W, :])
        kv_ref[:, :KV_W] = _rope(t[:, :KV_W], cos1, sin1).astype(BF16)
        v = t[:, KV_W:].astype(BF16)
        kv_ref[:, KV_W:] = v
        va_ref[...] = _with_ones(v)
        o = Q_W + 2 * KV_W
        for j in range(2):
            uv_ref[:, G_W * j:G_W * (j + 1)] = _dot_nt(h, w_ref[o + G_W * j:o + G_W * (j + 1), :]).astype(BF16)
        o += 2 * G_W
        for j in range(4):
            gab_ref[:, 512 * j:512 * (j + 1)] = _dot_nt(h, w_ref[o + 512 * j:o + 512 * (j + 1), :]).astype(BF16)
        _host_finish(pl.program_id(0), seq // tm - 1, comm, forward_at=(seq // tm) // 2)

    out = pl.pallas_call(
        body, name="proj_fwd", grid=(seq // tm,),
        in_specs=[_rows(tm, D), _acc((8, D)), _resident((IN_W, D)), _rows(tm, LANES), _rows(tm, LANES)] + _comm_specs(ng),
        out_specs=[_rows(tm, D), _rows(tm, Q_W), _rows(tm, 2 * KV_W), _rows(tm, 2 * LANES), _rows(tm, 2 * G_W),
                   _rows(tm, 2 * D)] + _comm_specs(ng),
        out_shape=[jax.ShapeDtypeStruct((seq, D), BF16), jax.ShapeDtypeStruct((seq, Q_W), BF16),
                   jax.ShapeDtypeStruct((seq, 2 * KV_W), BF16), jax.ShapeDtypeStruct((seq, 2 * LANES), BF16),
                   jax.ShapeDtypeStruct((seq, 2 * G_W), BF16), jax.ShapeDtypeStruct((seq, 2 * D), BF16)] + _gathered_shapes(gather),
        scratch_shapes=_comm_scratch(ng) if ng else [],
        compiler_params=_cp("arbitrary"),
    )(x, modx, w_in, cos, sin, *gather)
    return out[:6], out[6:]


def _stack_heads(x, hk):
    return jnp.concatenate([x[:, (hk * GROUP + g) * HEAD:(hk * GROUP + g + 1) * HEAD] for g in range(GROUP)], axis=0)


def _band_masks(n, nb):
    rows = GROUP * BLK
    qi = lax.broadcasted_iota(jnp.int32, (rows, BLK), 0) & (BLK - 1)
    kj = lax.broadcasted_iota(jnp.int32, (rows, BLK), 1)
    return (kj >= qi) & (n > 0), (kj <= qi) & (n < nb - 1)


def _attn_scores(q, k_refs, hk, masks):
    q4 = _stack_heads(q, hk)
    ks = [r[:, hk * HEAD:(hk + 1) * HEAD] for r in k_refs]
    s = [_dot_nt(q4, k) for k in ks]
    s[1] = jnp.where(masks[0], s[1], NEG)
    s[3] = jnp.where(masks[1], s[3], NEG)
    return q4, ks, s


def _sink_rows(sink_ref, hk):
    rows = GROUP * BLK
    rg = lax.broadcasted_iota(jnp.int32, (rows, 1), 0) >> 7
    sink_v = jnp.full((rows, 1), sink_ref[0, hk * GROUP], F32)
    for g in range(1, GROUP):
        sink_v = jnp.where(rg == g, sink_ref[0, hk * GROUP + g], sink_v)
    return sink_v


def _with_ones(v):
    ones = jnp.ones((v.shape[0], HEAD), v.dtype)
    return jnp.concatenate([v[:, :HEAD], ones, v[:, HEAD:], ones], axis=1)


def _kv_specs(nb, qb):
    def spec(d):
        return pl.BlockSpec((BLK, 2 * KV_W), lambda n: (jnp.clip(qb * n + d, 0, nb - 1), 0))
    return [spec(d) for d in range(-1, qb + 1)]


def _attn_fwd(q, kv, va, kvc, vac, sink, gather=()):
    seq = q.shape[0]
    nb = seq // BLK
    n_ctx = kvc.shape[0]
    ng = len(gather)
    Q_BLOCKS = 1
    nkv = Q_BLOCKS + 2
    steps = nb // Q_BLOCKS

    def body(q_ref, *rest):
        kv_refs, va_refs = rest[:nkv], rest[nkv:2 * nkv]
        kvc_ref, vac_ref, sink_ref = rest[2 * nkv:2 * nkv + 3]
        rest = rest[2 * nkv + 3:]
        o_ref, lse_ref = rest[ng:ng + 2]
        comm = _Gather(rest[:ng], rest[ng + 2:2 * ng + 2], *rest[2 * ng + 2:]) if ng else None
        n = pl.program_id(0)
        _host_start(n, comm)
        lane = lax.broadcasted_iota(jnp.int32, (BLK, LANES), 1)
        for sub in range(Q_BLOCKS):
            rs = slice(sub * BLK, (sub + 1) * BLK)
            q = q_ref[rs, :]
            outs = []
            lse_all = jnp.zeros((BLK, LANES), F32)
            masks = _band_masks(Q_BLOCKS * n + sub, nb)
            for hk in range(N_KV):
                _, _, s = _attn_scores(q, (kvc_ref,) + kv_refs[sub:sub + 3], hk, masks)
                sink_v = _sink_rows(sink_ref, hk)
                tile_max = s[1]
                for t in [s[0][:, i * LANES:(i + 1) * LANES] for i in range(n_ctx // LANES)] + s[2:]:
                    tile_max = jnp.maximum(tile_max, t)
                m = jnp.maximum(sink_v, jnp.max(tile_max, axis=-1, keepdims=True))
                o = jnp.zeros((GROUP * BLK, LANES), F32)
                for t, va_ref in zip(s, (vac_ref,) + va_refs[sub:sub + 3]):
                    o = o + _dot(jnp.exp((t - m).astype(BF16)), va_ref[:, hk * LANES:(hk + 1) * LANES])
                denom = o[:, HEAD:HEAD + 1] + jnp.exp(sink_v - m)
                o4 = o[:, :HEAD] * (1.0 / denom)
                lse4 = m + jnp.log(denom)
                for g in range(GROUP):
                    outs.append(o4[g * BLK:(g + 1) * BLK, :])
                    lse_all = jnp.where(lane == hk * GROUP + g, lse4[g * BLK:(g + 1) * BLK, :], lse_all)
            o_ref[rs, :] = jnp.concatenate(outs, axis=1).astype(BF16)
            lse_ref[rs, :] = lse_all
        _host_finish(n, steps - 1, comm, forward_at=(3 * steps) // 4)

    tq = Q_BLOCKS * BLK
    out = pl.pallas_call(
        body, name="attn_fwd", grid=(steps,),
        in_specs=[_rows(tq, Q_W)] + _kv_specs(nb, Q_BLOCKS) + _kv_specs(nb, Q_BLOCKS)
        + [_acc((n_ctx, 2 * KV_W)), _acc((n_ctx, 2 * LANES)), pl.BlockSpec(memory_space=pltpu.SMEM)] + _comm_specs(ng),
        out_specs=[_rows(tq, Q_W), _rows(tq, LANES)] + _comm_specs(ng),
        out_shape=[jax.ShapeDtypeStruct((seq, Q_W), BF16), jax.ShapeDtypeStruct((seq, LANES), F32)] + _gathered_shapes(gather),
        scratch_shapes=_comm_scratch(ng) if ng else [],
        compiler_params=_cp("arbitrary"),
    )(q, *([kv] * nkv), *([va] * nkv), kvc, vac, sink, *gather)
    return out[:2], out[2:]


def _gmlp_chunk(u, vb, gp_ref, ws_ref, bias_ref):
    gu, tu = _gelu(u)
    gv, tv = _gelu(vb)
    vhat, rstd = _ln(gv)
    vn = (vhat * gp_ref[0:1, :] + gp_ref[1:2, :]).astype(BF16)
    s = bias_ref[...] + jnp.concatenate(
        [_dot(ws_ref[g * BLK:(g + 1) * BLK, :], vn[:, g * GRP_D:(g + 1) * GRP_D]) for g in range(N_GRP)], axis=1)
    return gu, tu, tv, vhat, rstd, vn, s


def _mix_fwd(uv, gab, ya, gp, ws_stack, bias_full, w_a, w_b, w_o, tm, gather=()):
    seq = uv.shape[0]
    ng = len(gather)
    steps = seq // tm

    def body(uv_ref, gab_ref, ya_ref, gp_ref, ws_ref, bias_ref, wa_ref, wb_ref, wo_ref, *rest):
        a_ref, b_ref, mix_ref, merged_ref, yb_ref = rest[ng:ng + 5]
        comm = _Gather(rest[:ng], rest[ng + 5:2 * ng + 5], *rest[2 * ng + 5:]) if ng else None
        _host_start(pl.program_id(0), comm)
        for c in range(tm // BLK):
            rs = slice(c * BLK, (c + 1) * BLK)
            gu, _, _, _, _, _, s = _gmlp_chunk(uv_ref[rs, :G_W].astype(F32), uv_ref[rs, G_W:].astype(F32), gp_ref, ws_ref, bias_ref)
            yb_ref[rs, :] = (gu * s).astype(BF16)
        a = _dot(ya_ref[...], wa_ref[...])
        b = _dot(yb_ref[...], wb_ref[...])
        a_ref[...] = a.astype(BF16)
        b_ref[...] = b.astype(BF16)
        merged = (_sig(gab_ref[:, :D].astype(F32)) * a + _sig(gab_ref[:, D:].astype(F32)) * b).astype(BF16)
        merged_ref[...] = merged
        mix_ref[...] = _dot(merged, wo_ref[...])
        _host_finish(pl.program_id(0), steps - 1, comm, forward_at=(3 * steps) // 4)

    out = pl.pallas_call(
        body, name="mix_fwd", grid=(steps,),
        in_specs=[_rows(tm, 2 * G_W), _rows(tm, 2 * D), _rows(tm, Q_W), _acc((8, G_W)),
                  _resident((N_GRP * BLK, BLK)), _acc((BLK, G_W)),
                  _resident((Q_W, D)), _resident((G_W, D)), _resident((D, D))] + _comm_specs(ng),
        out_specs=[_rows(tm, D), _rows(tm, D), _rows(tm, D), _rows(tm, D), _rows(tm, G_W)] + _comm_specs(ng),
        out_shape=[jax.ShapeDtypeStruct((seq, D), BF16), jax.ShapeDtypeStruct((seq, D), BF16),
                   jax.ShapeDtypeStruct((seq, D), F32), jax.ShapeDtypeStruct((seq, D), BF16),
                   jax.ShapeDtypeStruct((seq, G_W), BF16)] + _gathered_shapes(gather),
        scratch_shapes=_comm_scratch(ng) if ng else [],
        compiler_params=_cp("arbitrary"),
    )(uv, gab, ya, gp, ws_stack, bias_full, w_a, w_b, w_o, *gather)
    return out[:5], out[5:]


FFN_CHUNK = 512


def _ffn_chunks():
    out = []
    for hh in range(2):
        off = 0
        while off < FH_SHARD:
            w = min(FFN_CHUNK, FH_SHARD - off)
            out.append((hh, off, w))
            off += w
    return out


def _mid_recompute(x_ref, mix_ref, vec_ref):
    r1 = ALPHA * x_ref[...] + vec_ref[0:1, :] * mix_ref[...]
    xh1, rstd1 = _ln(r1)
    xmid = xh1 * vec_ref[1:2, :] + vec_ref[2:3, :]
    xh2, rstd2 = _ln(xmid)
    return xh1, rstd1, xmid, xh2, rstd2


def _ffn(x, mix, tgt, vec, w_fi, w_fo, tm):
    seq = x.shape[0]

    def body(x_ref, mix_ref, tgt_ref, vec_ref, wi_ref, wo_ref, act_ref, h2_ref, dff_ref, df_ref, dr1_ref, st_ref, gu_ref):
        @pl.when(pl.program_id(0) == 0)
        def _():
            st_ref[...] = jnp.zeros_like(st_ref)

        xh1, rstd1, xmid, xh2, rstd2 = _mid_recompute(x_ref, mix_ref, vec_ref)
        h2 = (xh2 * (1.0 + vec_ref[4:5, :]) + vec_ref[3:4, :]).astype(BF16)
        h2_ref[...] = h2
        halves = [(slice(hh * FH_SHARD, (hh + 1) * FH_SHARD), slice(FH + hh * FH_SHARD, FH + (hh + 1) * FH_SHARD))
                  for hh in range(2)]
        for hh, (cs, cu) in enumerate(halves):
            g = _dot(h2, wi_ref[hh])
            u = _dot(h2, wi_ref[2 + hh])
            gu_ref[:, cs] = g
            gu_ref[:, cu] = u
            act_ref[:, cs] = (g * _sig(g) * u).astype(BF16)
        f = _dot(act_ref[...], wo_ref[...])
        r2 = ALPHA * xmid + vec_ref[5:6, :] * f
        yh, rstd = _ln(r2)
        y = yh * vec_ref[6:7, :] + vec_ref[7:8, :]
        err = y - tgt_ref[...]
        dy = err / D
        dr2 = _ln_bwd(dy * vec_ref[6:7, :], yh, rstd)
        st_ref[0:1, :] += _colsum(err * err)
        st_ref[1:2, :] += _colsum(dy * yh)
        st_ref[2:3, :] += _colsum(dy)
        st_ref[3:4, :] += _colsum(dr2 * f)

        df = (dr2 * vec_ref[5:6, :]).astype(BF16)
        df_ref[...] = df
        da_all = _dot_nt(df, wo_ref[...])
        for cs, cu in halves:
            da = da_all[:, cs]
            g = gu_ref[:, cs]
            u = gu_ref[:, cu]
            sg = _sig(g)
            dff_ref[:, cs] = (da * u * sg * (1.0 + g * (1.0 - sg))).astype(BF16)
            dff_ref[:, cu] = (da * g * sg).astype(BF16)
        dh2 = _dot_nt(dff_ref[:, :FH_SHARD], wi_ref[0])
        for s in range(1, N_SHARD):
            dh2 = dh2 + _dot_nt(dff_ref[:, s * FH_SHARD:(s + 1) * FH_SHARD], wi_ref[s])
        dxmid = _ln_bwd(dh2 * (1.0 + vec_ref[4:5, :]), xh2, rstd2) + ALPHA * dr2
        dr1 = _ln_bwd(dxmid * vec_ref[1:2, :], xh1, rstd1)
        dr1_ref[...] = dr1
        st_ref[8:9, :] += _colsum(dh2 * xh2)
        st_ref[9:10, :] += _colsum(dh2)
        st_ref[10:11, :] += _colsum(dxmid * xh1)
        st_ref[11:12, :] += _colsum(dxmid)
        st_ref[12:13, :] += _colsum(dr1 * mix_ref[...])

    return pl.pallas_call(
        body, name="ffn", grid=(seq // tm,),
        in_specs=[_rows(tm, D), _rows(tm, D), _rows(tm, D), _acc((8, D)), _resident((N_SHARD, D, FH_SHARD)), _resident((FH, D))],
        out_specs=[_rows(tm, FH), _rows(tm, D), _rows(tm, 2 * FH), _rows(tm, D), _rows(tm, D), _acc((16, D))],
        out_shape=[jax.ShapeDtypeStruct((seq, FH), BF16), jax.ShapeDtypeStruct((seq, D), BF16),
                   jax.ShapeDtypeStruct((seq, 2 * FH), BF16), jax.ShapeDtypeStruct((seq, D), BF16),
                   jax.ShapeDtypeStruct((seq, D), F32), jax.ShapeDtypeStruct((16, D), F32)],
        scratch_shapes=[pltpu.VMEM((tm, 2 * FH), F32)],
        compiler_params=_cp("arbitrary"),
    )(x, mix, tgt, vec, w_fi, w_fo)


def _mix_bwd(dr1, a, b, gab, uv, merged, ya, yb, vec, gp, ws_stack, ws_stack_t, bias_full, w_a, w_b, w_o, tm, scatter=()):
    seq = dr1.shape[0]
    last = seq // tm - 1
    ns = len(scatter)

    def body(dr1_ref, a_ref, b_ref, gab_ref, uv_ref, mg_ref, ya_ref, yb_ref, vec_ref, gp_ref, ws_ref, wst_ref, bias_ref,
             wa_ref, wb_ref, wo_ref, *rest):
        dya_ref, dp_ref, dws_ref, dbs_ref, st_ref, gwo_ref, gwa_ref, gwb_ref = rest[ns:ns + 8]
        acc_o, acc_a, acc_b = rest[2 * ns + 8:2 * ns + 11]
        comm = _AllToAll(rest[:ns], rest[ns + 8:2 * ns + 8], *rest[2 * ns + 11:]) if ns else None
        _host_start(pl.program_id(0), comm)

        @pl.when(pl.program_id(0) == 0)
        def _():
            dws_ref[...] = jnp.zeros_like(dws_ref)
            dbs_ref[...] = jnp.zeros_like(dbs_ref)
            st_ref[...] = jnp.zeros_like(st_ref)
            acc_o[...] = jnp.zeros_like(acc_o)
            acc_a[...] = jnp.zeros_like(acc_a)
            acc_b[...] = jnp.zeros_like(acc_b)

        dmix = (dr1_ref[...] * vec_ref[0:1, :]).astype(BF16)
        acc_o[...] += _dot_tn(mg_ref[...], dmix)
        dmerged = _dot_nt(dmix, wo_ref[...])
        sa = _sig(gab_ref[:, :D].astype(F32))
        sb = _sig(gab_ref[:, D:].astype(F32))
        da = (dmerged * sa).astype(BF16)
        db = (dmerged * sb).astype(BF16)
        dp_ref[:, 2 * G_W:2 * G_W + D] = (dmerged * a_ref[...].astype(F32) * sa * (1.0 - sa)).astype(BF16)
        dp_ref[:, 2 * G_W + D:] = (dmerged * b_ref[...].astype(F32) * sb * (1.0 - sb)).astype(BF16)
        dya_ref[...] = _dot_nt(da, wa_ref[...]).astype(BF16)
        dyb = _dot_nt(db, wb_ref[...])
        acc_a[...] += _dot_tn(ya_ref[...], da)
        acc_b[...] += _dot_tn(yb_ref[...], db)

        @pl.when(pl.program_id(0) == last)
        def _():
            gwo_ref[...] = acc_o[...].astype(BF16)
            gwa_ref[...] = acc_a[...].astype(BF16)
            gwb_ref[...] = acc_b[...].astype(BF16)

        for c in range(tm // BLK):
            rs = slice(c * BLK, (c + 1) * BLK)
            u = uv_ref[rs, :G_W].astype(F32)
            vb = uv_ref[rs, G_W:].astype(F32)
            gu, tu, tv, vhat, rstd, vn, s = _gmlp_chunk(u, vb, gp_ref, ws_ref, bias_ref)
            dyb_c = dyb[rs, :]
            ds = dyb_c * gu
            du = dyb_c * s * _gelu_grad(u, tu)
            ds_b = ds.astype(BF16)
            dvn_g = []
            for g in range(N_GRP):
                cg = slice(g * GRP_D, (g + 1) * GRP_D)
                dvn_g.append(_dot(wst_ref[:, g * BLK:(g + 1) * BLK], ds_b[:, cg]))
                dws_ref[g * BLK:(g + 1) * BLK, :] += _dot_nt(ds_b[:, cg], vn[:, cg])
            dvn = jnp.concatenate(dvn_g, axis=1)
            dbs_ref[...] += ds
            st_ref[0:1, :] += _colsum(dvn * vhat)
            st_ref[1:2, :] += _colsum(dvn)
            dgv = _ln_bwd(dvn * gp_ref[0:1, :], vhat, rstd)
            dvb = dgv * _gelu_grad(vb, tv)
            dp_ref[rs, :G_W] = du.astype(BF16)
            dp_ref[rs, G_W:2 * G_W] = dvb.astype(BF16)
        _host_finish(pl.program_id(0), last, comm)

    pw = 2 * G_W + 2 * D
    out = pl.pallas_call(
        body, name="mix_bwd", grid=(seq // tm,),
        in_specs=[_rows(tm, D), _rows(tm, D), _rows(tm, D), _rows(tm, 2 * D), _rows(tm, 2 * G_W), _rows(tm, D), _rows(tm, Q_W),
                  _rows(tm, G_W), _acc((8, D)), _acc((8, G_W)),
                  _resident((N_GRP * BLK, BLK)), _resident((BLK, N_GRP * BLK)), _acc((BLK, G_W)),
                  _resident((Q_W, D)), _resident((G_W, D)), _resident((D, D))] + _comm_specs(ns),
        out_specs=[_rows(tm, Q_W), _rows(tm, pw), _acc((N_GRP * BLK, BLK)), _acc((BLK, G_W)), _acc((8, G_W)),
                   _acc((D, D)), _acc((Q_W, D)), _acc((G_W, D))] + _comm_specs(ns),
        out_shape=[jax.ShapeDtypeStruct((seq, Q_W), BF16), jax.ShapeDtypeStruct((seq, pw), BF16),
                   jax.ShapeDtypeStruct((N_GRP * BLK, BLK), F32), jax.ShapeDtypeStruct((BLK, G_W), F32),
                   jax.ShapeDtypeStruct((8, G_W), F32), jax.ShapeDtypeStruct((D, D), BF16),
                   jax.ShapeDtypeStruct((Q_W, D), BF16), jax.ShapeDtypeStruct((G_W, D), BF16)]
        + [jax.ShapeDtypeStruct(v.shape, v.dtype) for v in scatter],
        scratch_shapes=[pltpu.VMEM((D, D), F32), pltpu.VMEM((Q_W, D), F32), pltpu.VMEM((G_W, D), F32)]
        + (_comm_scratch(ns) if ns else []),
        compiler_params=_cp("arbitrary"),
    )(dr1, a, b, gab, uv, merged, ya, yb, vec, gp, ws_stack, ws_stack_t, bias_full, w_a, w_b, w_o, *scatter)
    return out[:8], out[8:]


def _attn_bwd(q, kv, kvc, sink, dya, ya, lse, scatter=()):
    seq = q.shape[0]
    nb = seq // BLK
    n_ctx = kvc.shape[0]
    ns = len(scatter)
    Q_BLOCKS = 2
    nkv = Q_BLOCKS + 2
    steps = nb // Q_BLOCKS

    def body(q_ref, *rest):
        kv_refs = rest[:nkv]
        kvc_ref, sink_ref, do_ref, o_ref, lse_ref = rest[nkv:nkv + 5]
        rest = rest[nkv + 5:]
        dq_ref, dkv_ref, dkvc_ref, dsink_ref = rest[ns:ns + 4]
        comm = _AllToAll(rest[:ns], rest[ns + 4:2 * ns + 4], *rest[2 * ns + 4:]) if ns else None
        n = pl.program_id(0)
        _host_start(n, comm)

        @pl.when(n == 0)
        def _():
            dkv_ref[...] = jnp.zeros_like(dkv_ref)
            dkvc_ref[...] = jnp.zeros_like(dkvc_ref)
            dsink_ref[...] = jnp.zeros_like(dsink_ref)

        lane = lax.broadcasted_iota(jnp.int32, (1, LANES), 1)
        for sub in range(Q_BLOCKS):
            rs = slice(sub * BLK, (sub + 1) * BLK)
            blk = Q_BLOCKS * n + sub
            q = q_ref[rs, :]
            do = do_ref[rs, :]
            out = o_ref[rs, :]
            lse_all = lse_ref[rs, :]
            k_refs = (kvc_ref,) + kv_refs[sub:sub + 3]
            masks = _band_masks(blk, nb)
            dqs, dks, dvs = [], [], []
            for hk in range(N_KV):
                q4, ks, s = _attn_scores(q, k_refs, hk, masks)
                vs = [r[:, KV_W + hk * HEAD:KV_W + (hk + 1) * HEAD] for r in k_refs]
                lse4 = jnp.concatenate([lse_all[:, hk * GROUP + g:hk * GROUP + g + 1] for g in range(GROUP)], axis=0)
                do4 = _stack_heads(do, hk)
                delta = jnp.sum(do4.astype(F32) * _stack_heads(out, hk).astype(F32), axis=-1, keepdims=True)
                p = [jnp.exp((t - lse4).astype(BF16)) for t in s]
                ds = [t * (_dot_nt(do4, v) - delta).astype(BF16) for t, v in zip(p, vs)]
                dq4 = _dot(ds[0], ks[0])
                for t, k in zip(ds[1:], ks[1:]):
                    dq4 = dq4 + _dot(t, k)
                dq4 = dq4 * SCALE
                dqs += [dq4[g * BLK:(g + 1) * BLK, :] for g in range(GROUP)]
                dks.append([_dot_tn(t, q4) for t in ds])
                dvs.append([_dot_tn(t, do4) for t in p])
                ps = jnp.exp(_sink_rows(sink_ref, hk) - lse4) * delta
                for g in range(GROUP):
                    part = -jnp.sum(ps[g * BLK:(g + 1) * BLK, :], axis=0, keepdims=True)
                    dsink_ref[0:1, :] += jnp.where(lane == hk * GROUP + g, part, 0.0)
            dq_ref[rs, :] = jnp.concatenate(dqs, axis=1)

            def piece(i):
                return jnp.concatenate([dks[0][i], dks[1][i], dvs[0][i], dvs[1][i]], axis=1)

            dkvc_ref[...] += piece(0)
            starts = (jnp.maximum(blk - 1, 0), blk, jnp.minimum(blk + 1, nb - 1))
            for i, st in enumerate(starts):
                r = pl.ds(pl.multiple_of(st * BLK, BLK), BLK)
                dkv_ref[r, :] += piece(i + 1)
        _host_finish(n, steps - 1, comm)

    tq = Q_BLOCKS * BLK
    out = pl.pallas_call(
        body, name="attn_bwd", grid=(steps,),
        in_specs=[_rows(tq, Q_W)] + _kv_specs(nb, Q_BLOCKS) + [_acc((n_ctx, 2 * KV_W)), pl.BlockSpec(memory_space=pltpu.SMEM),
                                                     _rows(tq, Q_W), _rows(tq, Q_W), _rows(tq, LANES)] + _comm_specs(ns),
        out_specs=[_rows(tq, Q_W), _acc((seq, 2 * KV_W)), _acc((n_ctx, 2 * KV_W)), _acc((8, LANES))] + _comm_specs(ns),
        out_shape=[jax.ShapeDtypeStruct((seq, Q_W), F32), jax.ShapeDtypeStruct((seq, 2 * KV_W), F32),
                   jax.ShapeDtypeStruct((n_ctx, 2 * KV_W), F32), jax.ShapeDtypeStruct((8, LANES), F32)]
        + [jax.ShapeDtypeStruct(v.shape, v.dtype) for v in scatter],
        scratch_shapes=_comm_scratch(ns) if ns else [],
        compiler_params=_cp("arbitrary"),
    )(q, *([kv] * nkv), kvc, sink, dya, ya, lse, *scatter)
    return out[:4], out[4:]


def _proj_bwd(dq, dkv, dpb, x, dr1, modx, w_in, cos, sin, tm, scatter=()):
    seq = x.shape[0]
    pw = IN_W - Q_W - 2 * KV_W
    ns = len(scatter)

    def body(dq_ref, dkv_ref, dpb_ref, x_ref, dr1_ref, mod_ref, w_ref, cos_ref, sin_ref, *rest):
        dqkv_ref, gx_ref, st_ref = rest[ns:ns + 3]
        comm = _AllToAll(rest[:ns], rest[ns + 3:2 * ns + 3], *rest[2 * ns + 3:]) if ns else None
        _host_start(pl.program_id(0), comm)

        @pl.when(pl.program_id(0) == 0)
        def _():
            st_ref[...] = jnp.zeros_like(st_ref)

        cos1, sin1 = cos_ref[...], sin_ref[...]
        cos2 = jnp.concatenate([cos1, cos1], axis=1)
        sin2 = jnp.concatenate([sin1, sin1], axis=1)
        for j in range(Q_W // 256):
            cs = slice(256 * j, 256 * (j + 1))
            dqkv_ref[:, cs] = _unrope(dq_ref[:, cs], cos2, sin2).astype(BF16)
        dqkv_ref[:, Q_W:Q_W + KV_W] = _unrope(dkv_ref[:, :KV_W], cos1, sin1).astype(BF16)
        dqkv_ref[:, Q_W + KV_W:] = dkv_ref[:, KV_W:].astype(BF16)
        o = Q_W + 2 * KV_W
        dh = _dot(dqkv_ref[...], w_ref[:o, :]) + _dot(dpb_ref[...], w_ref[o:, :])
        xhat, rstd = _ln(x_ref[...])
        st_ref[0:1, :] += _colsum(dh)
        st_ref[1:2, :] += _colsum(dh * xhat)
        gx_ref[...] = _ln_bwd(dh * (1.0 + mod_ref[1:2, :]), xhat, rstd) + ALPHA * dr1_ref[...]
        _host_finish(pl.program_id(0), seq // tm - 1, comm)

    out = pl.pallas_call(
        body, name="proj_bwd", grid=(seq // tm,),
        in_specs=[_rows(tm, Q_W), _rows(tm, 2 * KV_W), _rows(tm, pw), _rows(tm, D), _rows(tm, D), _acc((8, D)),
                  _resident((IN_W, D)), _rows(tm, LANES), _rows(tm, LANES)] + _comm_specs(ns),
        out_specs=[_rows(tm, Q_W + 2 * KV_W), _rows(tm, D), _acc((8, D))] + _comm_specs(ns),
        out_shape=[jax.ShapeDtypeStruct((seq, Q_W + 2 * KV_W), BF16), jax.ShapeDtypeStruct((seq, D), F32),
                   jax.ShapeDtypeStruct((8, D), F32)] + [jax.ShapeDtypeStruct(v.shape, v.dtype) for v in scatter],
        scratch_shapes=_comm_scratch(ns) if ns else [],
        compiler_params=_cp("arbitrary"),
    )(dq, dkv, dpb, x, dr1, modx, w_in, cos, sin, *scatter)
    return out[:3], out[3:]


def _ctx_bwd(dkvc, ctx, hc, w_kv):
    n_ctx = ctx.shape[0]

    def body(dkvc_ref, ctx_ref, hc_ref, w_ref, dw_ref, st_ref):
        d = dkvc_ref[...].astype(BF16)
        dw_ref[...] = _dot_tn(d, hc_ref[...])
        dhc = _dot(d, w_ref[...])
        xhat, _ = _ln(ctx_ref[...])
        st_ref[...] = jnp.zeros_like(st_ref)
        st_ref[0:1, :] = _colsum(dhc)
        st_ref[1:2, :] = _colsum(dhc * xhat)

    return pl.pallas_call(
        body, name="ctx_bwd", grid=(1,),
        in_specs=[_acc((n_ctx, 2 * KV_W)), _acc((n_ctx, D)), _acc((n_ctx, D)), _acc((2 * KV_W, D))],
        out_specs=[_acc((2 * KV_W, D)), _acc((8, D))],
        out_shape=[jax.ShapeDtypeStruct((2 * KV_W, D), F32), jax.ShapeDtypeStruct((8, D), F32)],
        compiler_params=_cp("arbitrary"),
    )(dkvc, ctx, hc, w_kv)


def _tn_matmul(a, b, tn, name, out_dtype, shard_major=False, init=None, tk=512, scatter=(), gather=()):
    t, ka = a.shape
    n = b.shape[1]
    tk = min(tk, t)
    nk = t // tk
    nj = n // tn
    has_init = init is not None
    assert not (scatter and gather)
    moved = list(scatter) + list(gather)
    pattern = _AllToAll if scatter else _Gather
    ns = len(moved)
    n_in = 3 if has_init else 2

    def body(*refs):
        a_ref, b_ref = refs[:2]
        i_ref = refs[2] if has_init else None
        rest = refs[n_in:]
        o_ref = rest[ns]
        acc_ref = rest[2 * ns + 1]
        comm = pattern(rest[:ns], rest[ns + 1:2 * ns + 1], *rest[2 * ns + 2:]) if ns else None
        k = pl.program_id(1)
        step = pl.program_id(0) * nk + k
        _host_start(step, comm)

        @pl.when(k == 0)
        def _():
            acc_ref[...] = i_ref[...] if has_init else jnp.zeros_like(acc_ref)

        acc_ref[...] += _dot_tn(a_ref[...], b_ref[...])

        @pl.when(k == nk - 1)
        def _():
            o_ref[...] = acc_ref[...].astype(out_dtype)

        _host_finish(step, nj * nk - 1, comm, forward_at=(3 * nj * nk) // 4 if gather else None)

    in_specs = [pl.BlockSpec((tk, ka), lambda j, k: (k, 0)), pl.BlockSpec((tk, tn), lambda j, k: (k, j))]
    args = [a, b]
    if has_init:
        in_specs.append(pl.BlockSpec((ka, tn), lambda j, k: (0, j)))
        args.append(init)
    if shard_major:
        out_spec = pl.BlockSpec((None, ka, tn), lambda j, k: (j, 0, 0))
        out_shape = jax.ShapeDtypeStruct((nj, ka, tn), out_dtype)
    else:
        out_spec = pl.BlockSpec((ka, tn), lambda j, k: (0, j))
        out_shape = jax.ShapeDtypeStruct((ka, n), out_dtype)
    out = pl.pallas_call(
        body, name=name, grid=(nj, nk), in_specs=in_specs + _comm_specs(ns), out_specs=[out_spec] + _comm_specs(ns),
        out_shape=[out_shape] + [jax.ShapeDtypeStruct(v.shape, v.dtype) for v in scatter] + _gathered_shapes(gather),
        scratch_shapes=[pltpu.VMEM((ka, tn), F32)] + (_comm_scratch(ns) if ns else []),
        compiler_params=_cp("arbitrary", "arbitrary"),
    )(*args, *moved)
    return (out[0], out[1:]) if ns else out[0]


ADA_TILE = 512


def _ada_fwd(sc_all, w_ada):
    cs = w_ada.shape[1]

    def body(s_ref, w_ref, o_ref):
        o_ref[...] = _dot(s_ref[...].astype(BF16), w_ref[...].astype(BF16))

    return pl.pallas_call(
        body, name="ada_fwd", grid=(cs // ADA_TILE,),
        in_specs=[_acc((16, D)), pl.BlockSpec((D, ADA_TILE), lambda j: (0, j))],
        out_specs=pl.BlockSpec((16, ADA_TILE), lambda j: (0, j)),
        out_shape=jax.ShapeDtypeStruct((16, cs), F32),
        compiler_params=_cp("arbitrary"),
    )(sc_all, w_ada)


def _ada_bwd(sc_all_t, dm_all, dmc, w_ada):
    cs = w_ada.shape[1]

    def body(st_ref, dm_ref, dmc_ref, w_ref, gw_ref, part_ref):
        @pl.when(pl.program_id(0) == 0)
        def _():
            part_ref[...] = jnp.zeros_like(part_ref)

        gw_ref[...] = _dot(st_ref[...].astype(BF16), dm_ref[...].astype(BF16))
        part_ref[...] += _dot_nt(dmc_ref[...].astype(BF16), w_ref[...].astype(BF16))

    return pl.pallas_call(
        body, name="ada_bwd", grid=(cs // ADA_TILE,),
        in_specs=[_acc((D, 16)), pl.BlockSpec((16, ADA_TILE), lambda j: (0, j)), pl.BlockSpec((8, ADA_TILE), lambda j: (0, j)),
                  pl.BlockSpec((D, ADA_TILE), lambda j: (0, j))],
        out_specs=[pl.BlockSpec((D, ADA_TILE), lambda j: (0, j)), _acc((8, D))],
        out_shape=[jax.ShapeDtypeStruct((D, cs), F32), jax.ShapeDtypeStruct((8, D), F32)],
        compiler_params=_cp("arbitrary"),
    )(sc_all_t, dm_all, dmc, w_ada)


def _sum8(x, name, tr=256):
    _, r, c = x.shape
    tr = min(tr, r)
    while r % tr:
        tr -= 16

    def body(x_ref, o_ref):
        acc = x_ref[0].astype(F32)
        for i in range(1, N_DEV):
            acc = acc + x_ref[i].astype(F32)
        o_ref[...] = acc

    return pl.pallas_call(
        body, name=name, grid=(r // tr,),
        in_specs=[pl.BlockSpec((N_DEV, tr, c), lambda i: (0, i, 0))],
        out_specs=pl.BlockSpec((tr, c), lambda i: (i, 0)),
        out_shape=jax.ShapeDtypeStruct((r, c), F32),
        compiler_params=_cp("arbitrary"),
    )(x)


def _sum_blocks(recv, src, me, name, tr=256):
    _, r, c = recv.shape
    tr = min(tr, r)
    while r % tr:
        tr -= 16

    def body(me_ref, recv_ref, own_ref, o_ref):
        acc = own_ref[...].astype(F32)
        for k in range(1, N_DEV):
            acc = acc + recv_ref[me_ref[0] ^ k].astype(F32)
        o_ref[...] = acc

    return pl.pallas_call(
        body, name=name,
        grid_spec=pltpu.PrefetchScalarGridSpec(
            num_scalar_prefetch=1, grid=(r // tr,),
            in_specs=[pl.BlockSpec((N_DEV, tr, c), lambda i, me_ref: (0, i, 0)),
                      pl.BlockSpec((None, tr, c), lambda i, me_ref: (me_ref[0], i, 0))],
            out_specs=pl.BlockSpec((tr, c), lambda i, me_ref: (i, 0))),
        out_shape=jax.ShapeDtypeStruct((r, c), F32),
        compiler_params=_cp("arbitrary"),
    )(me, recv, src)


def _sum8_many(xs, name):
    n = len(xs)

    def body(*refs):
        for x_ref, o_ref in zip(refs[:n], refs[n:]):
            acc = x_ref[0]
            for i in range(1, N_DEV):
                acc = acc + x_ref[i]
            o_ref[...] = acc

    vmem = pl.BlockSpec(memory_space=pltpu.VMEM)
    return pl.pallas_call(
        body, name=name, in_specs=[vmem] * n, out_specs=[vmem] * n,
        out_shape=[jax.ShapeDtypeStruct(v.shape[1:], v.dtype) for v in xs],
        compiler_params=pltpu.CompilerParams(vmem_limit_bytes=VMEM_LIMIT),
    )(*xs)


def _adam_update(w, g, m, v):
    nm = ADAM_B1 * m + (1.0 - ADAM_B1) * g
    nv = ADAM_B2 * v + (1.0 - ADAM_B2) * (g * g)
    m_hat = nm / (1.0 - ADAM_B1 ** ADAM_STEP)
    v_hat = nv / (1.0 - ADAM_B2 ** ADAM_STEP)
    return -ADAM_LR * (m_hat / (jnp.sqrt(v_hat) + ADAM_EPS) + ADAM_WD * w), nm, nv


ROW_LOSS, ROW_LN2_G, ROW_LN2_B, ROW_LN1_G, ROW_LN1_B = 0, 1, 2, 10, 11
ROWS_DMOD_X = (16, 17, 12, 9, 8, 3)
ROWS_DMOD_C = (24, 25)
SMALL = ("c_ctx", "b_ada", "attn_sink", "gmlp_ln_g", "gmlp_ln_b", "w_spatial", "b_spatial", "ln1_g", "ln1_b", "ln2_g", "ln2_b")


def _adamw_small(sums, dsc, w, m, v):
    n = len(SMALL)

    def body(*refs):
        st_ref, gm_ref, sk_ref, ws_ref, bs_ref, dsc_ref = refs[:6]
        w_refs = dict(zip(SMALL, refs[6:6 + n]))
        m_refs = dict(zip(SMALL, refs[6 + n:6 + 2 * n]))
        v_refs = dict(zip(SMALL, refs[6 + 2 * n:6 + 3 * n]))
        outs = refs[6 + 3 * n:]
        c = w_refs["c_ctx"][...]
        sg = _sig(c)
        dmod = [st_ref[r:r + 1, :] for r in ROWS_DMOD_X]
        dmod[0] = dmod[0] + st_ref[ROWS_DMOD_C[0]:ROWS_DMOD_C[0] + 1, :]
        dmod[1] = dmod[1] + st_ref[ROWS_DMOD_C[1]:ROWS_DMOD_C[1] + 1, :]
        grads = dict(
            c_ctx=dsc_ref[0:1, :] * (sg * (1.0 + c * (1.0 - sg))),
            b_ada=jnp.concatenate(dmod, axis=1),
            attn_sink=sk_ref[0:1, 0:N_KV * GROUP],
            gmlp_ln_g=gm_ref[0:1, :], gmlp_ln_b=gm_ref[1:2, :],
            w_spatial=ws_ref[...], b_spatial=bs_ref[...],
            ln1_g=st_ref[ROW_LN1_G:ROW_LN1_G + 1, :], ln1_b=st_ref[ROW_LN1_B:ROW_LN1_B + 1, :],
            ln2_g=st_ref[ROW_LN2_G:ROW_LN2_G + 1, :], ln2_b=st_ref[ROW_LN2_B:ROW_LN2_B + 1, :])
        for i, name in enumerate(SMALL):
            g = grads[name]
            d, nm, nv = _adam_update(w_refs[name][...], g, m_refs[name][...], v_refs[name][...])
            outs[i][...] = g
            outs[n + i][...] = d
            outs[2 * n + i][...] = nm
            outs[3 * n + i][...] = nv

    vmem = pl.BlockSpec(memory_space=pltpu.VMEM)
    args = list(sums) + [dsc] + [w[k] for k in SMALL] + [m[k] for k in SMALL] + [v[k] for k in SMALL]
    shapes = [jax.ShapeDtypeStruct(w[k].shape, F32) for k in SMALL]
    out = pl.pallas_call(
        body, name="adamw_small", in_specs=[vmem] * len(args), out_specs=[vmem] * (4 * n), out_shape=shapes * 4,
        compiler_params=pltpu.CompilerParams(vmem_limit_bytes=VMEM_LIMIT),
    )(*args)
    return [dict(zip(SMALL, out[i * n:(i + 1) * n])) for i in range(4)]


def _adamw_halves(w, mine, theirs, m, v, c_arr, name):
    r, c = w.shape
    tr = min(256, r // 2)
    while (r // 2) % tr:
        tr -= 8
    nt = (r // 2) // tr

    def body(c_ref, w_ref, mine_ref, theirs_ref, m_ref, v_ref, g_ref, d_ref, nm_ref, nv_ref):
        g = jnp.where(pl.program_id(0) == c_ref[0], mine_ref[...], theirs_ref[...])
        g_ref[...] = g
        d_ref[...], nm_ref[...], nv_ref[...] = _adam_update(w_ref[...], g, m_ref[...], v_ref[...])

    whole = pl.BlockSpec((tr, c), lambda hb, i, c_ref: (hb * nt + i, 0))
    mine_spec = pl.BlockSpec((tr, c), lambda hb, i, c_ref: (jnp.where(hb == c_ref[0], i, 0), 0))
    theirs_spec = pl.BlockSpec((tr, c), lambda hb, i, c_ref: (jnp.where(hb == c_ref[0], 0, i), 0))
    shp = jax.ShapeDtypeStruct((r, c), F32)
    return pl.pallas_call(
        body, name=name,
        grid_spec=pltpu.PrefetchScalarGridSpec(
            num_scalar_prefetch=1, grid=(2, nt), in_specs=[whole, mine_spec, theirs_spec, whole, whole],
            out_specs=[whole] * 4),
        out_shape=[shp] * 4,
        compiler_params=_cp("arbitrary", "arbitrary"),
    )(c_arr, w, mine, theirs, m, v)


def _adamw(w, g, m, v, name):
    r, c = w.shape
    tr = r if r * c <= 256 * 1024 else min(256, r)
    while r % tr:
        tr -= 8

    def body(w_ref, g_ref, m_ref, v_ref, d_ref, nm_ref, nv_ref):
        d_ref[...], nm_ref[...], nv_ref[...] = _adam_update(w_ref[...], g_ref[...], m_ref[...], v_ref[...])

    spec = pl.BlockSpec((tr, c), lambda i: (i, 0))
    shp = jax.ShapeDtypeStruct((r, c), F32)
    return pl.pallas_call(
        body, name=name, grid=(r // tr,), in_specs=[spec] * 4, out_specs=[spec] * 3, out_shape=[shp] * 3,
        compiler_params=_cp("arbitrary"),
    )(w, g, m, v)


def _my_pos():
    return lax.axis_index("x"), lax.axis_index("y"), lax.axis_index("c")


N_COPY = 7


class _Gather:
    def __init__(self, x_refs, out_refs, send_sems, recv_sems):
        self.x_refs, self.out_refs = x_refs, out_refs
        self.send_sems, self.recv_sems = send_sems, recv_sems
        x, y, c = _my_pos()
        self.c = c
        self.me, self.sibling = (x, y, c), (x, y, 1 - c)
        self.chips = [(1 - x, y), (x, 1 - y), (1 - x, 1 - y)]

    def _copy(self, a, k, block, to, from_input=False):
        px, py, pc = block
        rows = self.out_refs[a].at[4 * px + 2 * py + pc]
        return pltpu.make_async_remote_copy(
            src_ref=self.x_refs[a] if from_input else rows, dst_ref=rows,
            send_sem=self.send_sems.at[a * N_COPY + k], recv_sem=self.recv_sems.at[a * N_COPY + k],
            device_id=to, device_id_type=MESH)

    def start(self):
        n = len(self.x_refs)
        for a in range(n):
            self._copy(a, 0, self.me, self.sibling, from_input=True).start()
        for j, chip in enumerate(self.chips):
            for a in range(n):
                self._copy(a, 1 + j, self.me, (*chip, self.c), from_input=True).start()

    def forward(self):
        c = self.c
        for j, chip in enumerate(self.chips):
            for a in range(len(self.x_refs)):
                self._copy(a, 1 + j, (*chip, c), self.me).wait_recv()
                self._copy(a, 4 + j, (*chip, c), self.sibling).start()

    def finish(self):
        self.forward()
        self.drain()

    def drain(self):
        n = len(self.x_refs)
        c = self.c
        for a in range(n):
            self._copy(a, 0, self.sibling, self.me).wait_recv()
        for j, chip in enumerate(self.chips):
            for a in range(n):
                self._copy(a, 4 + j, (*chip, 1 - c), self.me).wait_recv()
        for a in range(n):
            self._copy(a, 0, self.me, self.sibling, from_input=True).wait_send()
            for j, chip in enumerate(self.chips):
                self._copy(a, 1 + j, self.me, (*chip, c), from_input=True).wait_send()
                self._copy(a, 4 + j, (*chip, c), self.sibling).wait_send()


def _comm_scratch(n):
    return [pltpu.SemaphoreType.DMA((n * N_COPY,)), pltpu.SemaphoreType.DMA((n * N_COPY,))]


def _comm_specs(n):
    return [pl.BlockSpec(memory_space=pl.ANY)] * n


def _gathered_shapes(xs):
    return [jax.ShapeDtypeStruct((N_DEV,) + v.shape, v.dtype) for v in xs]


def _with_own(gathered, xs, me):
    return [lax.dynamic_update_index_in_dim(g, v, me, 0) for g, v in zip(gathered, xs)]


def _all_gather(xs, me, name):
    n = len(xs)

    def body(*refs):
        g = _Gather(refs[:n], refs[n:2 * n], *refs[2 * n:])
        g.start()
        g.finish()

    out = pl.pallas_call(
        body, name=name, out_shape=_gathered_shapes(xs), in_specs=_comm_specs(n), out_specs=_comm_specs(n),
        scratch_shapes=_comm_scratch(n),
    )(*xs)
    return _with_own(out, xs, me)


class _AllToAll:
    def __init__(self, x_refs, out_refs, send_sems, recv_sems):
        self.x_refs, self.out_refs = x_refs, out_refs
        self.send_sems, self.recv_sems = send_sems, recv_sems
        self.pos = _my_pos()
        x, y, c = self.pos
        self.me = 4 * x + 2 * y + c

    def _peer(self, k):
        x, y, c = self.pos
        return (x ^ ((k >> 2) & 1), y ^ ((k >> 1) & 1), c ^ (k & 1))

    def _copy(self, a, k):
        p = self._peer(k)
        return pltpu.make_async_remote_copy(
            src_ref=self.x_refs[a].at[4 * p[0] + 2 * p[1] + p[2]], dst_ref=self.out_refs[a].at[self.me],
            send_sem=self.send_sems.at[a * N_COPY + k - 1], recv_sem=self.recv_sems.at[a * N_COPY + k - 1],
            device_id=p, device_id_type=MESH)

    def start(self):
        for k in range(1, N_DEV):
            for a in range(len(self.x_refs)):
                self._copy(a, k).start()

    def finish(self):
        for a in range(len(self.x_refs)):
            for k in range(1, N_DEV):
                self._copy(a, k).wait_recv()
            for k in range(1, N_DEV):
                self._copy(a, k).wait_send()


def _all_to_all(blocks, name):
    n = len(blocks)

    def body(*refs):
        t = _AllToAll(refs[:n], refs[n:2 * n], *refs[2 * n:])
        t.start()
        t.finish()

    return pl.pallas_call(
        body, name=name, out_shape=[jax.ShapeDtypeStruct(v.shape, v.dtype) for v in blocks],
        in_specs=_comm_specs(n), out_specs=_comm_specs(n), scratch_shapes=_comm_scratch(n),
    )(*blocks)


def _sibling_exchange(xs, name):
    n = len(xs)

    def body(*refs):
        x_refs, out_refs = refs[:n], refs[n:2 * n]
        send_sems, recv_sems = refs[2 * n:]
        x, y, c = _my_pos()

        def push(a):
            return pltpu.make_async_remote_copy(
                src_ref=x_refs[a], dst_ref=out_refs[a], send_sem=send_sems.at[a], recv_sem=recv_sems.at[a],
                device_id=(x, y, 1 - c), device_id_type=MESH)

        for a in range(n):
            push(a).start()
        for a in range(n):
            push(a).wait_recv()
            push(a).wait_send()

    return pl.pallas_call(
        body, name=name, out_shape=[jax.ShapeDtypeStruct(v.shape, v.dtype) for v in xs],
        in_specs=_comm_specs(n), out_specs=_comm_specs(n),
        scratch_shapes=[pltpu.SemaphoreType.DMA((n,)), pltpu.SemaphoreType.DMA((n,))],
    )(*xs)


def _scatter_and_gather(scatter, gather, name):
    ns, ng = len(scatter), len(gather)

    def body(*refs):
        s_in, g_in = refs[:ns], refs[ns:ns + ng]
        s_out, g_out = refs[ns + ng:2 * ns + ng], refs[2 * ns + ng:2 * (ns + ng)]
        s_send, s_recv, g_send, g_recv = refs[2 * (ns + ng):]
        g = _Gather(g_in, g_out, g_send, g_recv)
        t = _AllToAll(s_in, s_out, s_send, s_recv)
        g.start()
        t.start()
        g.finish()
        t.finish()

    out = pl.pallas_call(
        body, name=name,
        out_shape=[jax.ShapeDtypeStruct(v.shape, v.dtype) for v in scatter] + _gathered_shapes(gather),
        in_specs=_comm_specs(ns + ng), out_specs=_comm_specs(ns + ng),
        scratch_shapes=_comm_scratch(ns) + _comm_scratch(ng),
    )(*scatter, *gather)
    return out[:ns], out[ns:]


def _row_tile(seq, want):
    return min(want, seq)


def _local_step(x, ctx, tgt, mod_x, mod_c, wb, sink, gmlp_g, gmlp_b, w_s, b_s, ln1_g, ln1_b, ln2_g, ln2_b,
                later=None, me=None):
    seq = x.shape[0]
    on_mesh = me is not None
    modx1 = jnp.concatenate([mod_x[0:2], jnp.zeros((6, D), F32)], axis=0)
    modc = jnp.concatenate([mod_c[0:2], jnp.zeros((6, D), F32)], axis=0)
    vec = jnp.concatenate([mod_x[2:3], ln1_g, ln1_b, mod_x[3:6], ln2_g, ln2_b], axis=0)
    gp = jnp.concatenate([gmlp_g, gmlp_b, jnp.zeros((6, G_W), F32)], axis=0)
    ws_stack = w_s.reshape(N_GRP * BLK, BLK).astype(BF16)
    ws_stack_t = jnp.transpose(w_s, (2, 0, 1)).reshape(BLK, N_GRP * BLK).astype(BF16)
    bias_full = jnp.repeat(b_s.T, GRP_D, axis=1)
    cos, sin = _rope_tables(seq)
    w_in = wb["w_in"]
    w_kv = w_in[Q_W:Q_W + 2 * KV_W, :]
    tm_big = _row_tile(seq, 512)
    tm_ffn = _row_tile(seq, 256)

    hc, kvc, vac = _ctx_fwd(ctx, modc, w_kv)
    behind_proj = ("w_a", "w_b", "w_o") if on_mesh else ()
    behind_attn = ("w_fi",) if on_mesh else ()
    behind_mix = ("w_fo",) if on_mesh else ()
    wb = dict(wb)

    def whole(names, gathered):
        for n, g in zip(names, _with_own(list(gathered), [later[n] for n in names], me)):
            wb[n] = g.reshape(-1, g.shape[2]) if n in ROW_SHARDED else g.reshape(N_SHARD, 2 * g.shape[1], g.shape[2])

    (h, q, kv, va, uv, gab), got = _proj_fwd(x, modx1, w_in, cos, sin, tm_big, gather=[later[n] for n in behind_proj])
    whole(behind_proj, got)
    if on_mesh:
        for n in ("w_a", "w_b"):
            wb[n] = wb[n].transpose(1, 0, 2).reshape(wb[n].shape[1], D)
    (ya, lse), got = _attn_fwd(q, kv, va, kvc, vac, sink, gather=[later[n] for n in behind_attn])
    whole(behind_attn, got)
    (a, b, mix, merged, yb), got = _mix_fwd(uv, gab, ya, gp, ws_stack, bias_full, wb["w_a"], wb["w_b"], wb["w_o"], tm_big,
                                            gather=[later[n] for n in behind_mix])
    whole(behind_mix, got)
    act, h2, dff, df, dr1, st_ffn = _ffn(x, mix, tgt, vec, wb["w_fi"], wb["w_fo"], tm_ffn)
    blocks, recv = {}, {}
    blocks["w_fo"] = _eighths(_tn_matmul(act, df, 512, "tn_w_ffn_out", BF16, tk=2048))
    if on_mesh:
        g_w_fi, (recv["w_fo"],) = _tn_matmul(h2, dff, FH_SHARD, "tn_w_ffn_in", BF16, shard_major=True, tk=2048,
                                            scatter=[blocks["w_fo"]])
    else:
        g_w_fi = _tn_matmul(h2, dff, FH_SHARD, "tn_w_ffn_in", BF16, shard_major=True, tk=2048)
    blocks["w_fi"] = _eighths(g_w_fi)
    (dya, dpb, dws, dbs_full, st4, g_w_o, g_w_a, g_w_b), got = _mix_bwd(
        dr1, a, b, gab, uv, merged, ya, yb, vec, gp, ws_stack, ws_stack_t, bias_full, wb["w_a"], wb["w_b"], wb["w_o"],
        tm_big, scatter=[blocks["w_fi"]] if on_mesh else ())
    recv.update(zip(("w_fi",), got))
    shard_major = [g.reshape(g.shape[0], N_SHARD, D // N_SHARD).transpose(1, 0, 2) for g in (g_w_a, g_w_b)]
    blocks.update(w_o=_eighths(g_w_o), w_a=_eighths(shard_major[0]), w_b=_eighths(shard_major[1]))
    mixer = ("w_o", "w_a", "w_b") if on_mesh else ()
    (dq, dkv, dkvc, dsink), got = _attn_bwd(q, kv, kvc, sink, dya, ya, lse, scatter=[blocks[n] for n in mixer])
    recv.update(zip(mixer, got))
    g_wkv_ctx, st0 = _ctx_bwd(dkvc, ctx, hc, w_kv)
    (dqkv, grad_x, st1), _ = _proj_bwd(dq, dkv, dpb, x, dr1, modx1, w_in, cos, sin, tm_big)
    dbs = jnp.sum(dbs_full.reshape(BLK, N_GRP, GRP_D), axis=2).T
    early = [jnp.concatenate([st_ffn, st0], axis=0), st4, dsink, dws, dbs]
    init = jnp.pad(g_wkv_ctx, ((Q_W, 0), (0, 0)))
    g_qkv = _tn_matmul(dqkv, h, D, "tn_w_in_qkv", BF16, init=init, tk=1024)
    if on_mesh:
        g_rest, early_gathered = _tn_matmul(dpb, h, D, "tn_w_in_rest", BF16, tk=1024, gather=early)
    else:
        g_rest, early_gathered = _tn_matmul(dpb, h, D, "tn_w_in_rest", BF16, tk=1024), None
    blocks["w_in"] = _eighths(jnp.concatenate([g_qkv, g_rest], axis=0))
    return grad_x, dict(early=early, early_gathered=early_gathered, late=st1), blocks, recv


BIG = ("w_in", "w_a", "w_b", "w_o", "w_fi", "w_fo")
ROW_SHARDED = ("w_o", "w_fo")


def _half_of_shard(shard, c):
    r = shard.shape[0]
    return lax.dynamic_slice_in_dim(shard, c * (r // 2), r // 2, axis=0)


def _eighths(v):
    rows = v.shape[-2] * (v.shape[0] if v.ndim == 3 else 1)
    return v.reshape(N_DEV, rows // N_DEV, v.shape[-1])


def kernel(x, c, ctx, c_ctx, w_ada, b_ada, w_in, attn_sink, gmlp_ln_g, gmlp_ln_b, w_spatial, b_spatial, w_branch_a, w_branch_b, w_out, ln1_g, ln1_b, w_ffn_in, w_ffn_out, ln2_g, ln2_b, loss_target, m_c_ctx, m_w_ada, m_b_ada, m_w_in, m_attn_sink, m_gmlp_ln_g, m_gmlp_ln_b, m_w_spatial, m_b_spatial, m_w_branch_a, m_w_branch_b, m_w_out, m_ln1_g, m_ln1_b, m_w_ffn_in, m_w_ffn_out, m_ln2_g, m_ln2_b, v_c_ctx, v_w_ada, v_b_ada, v_w_in, v_attn_sink, v_gmlp_ln_g, v_gmlp_ln_b, v_w_spatial, v_b_spatial, v_w_branch_a, v_w_branch_b, v_w_out, v_ln1_g, v_ln1_b, v_w_ffn_in, v_w_ffn_out, v_ln2_g, v_ln2_b):
    mx, my, mc = _my_pos()
    me = 4 * mx + 2 * my + mc
    chip = 2 * mx + my
    shards = dict(w_in=w_in[0].T, w_a=w_branch_a[0], w_b=w_branch_b[0], w_o=w_out[0], w_fi=w_ffn_in[0], w_fo=w_ffn_out[0])

    halves = {n: _half_of_shard(shards[n], mc).astype(BF16) for n in BIG}
    c_rows = jnp.concatenate([c, jnp.zeros((7, D), F32)], axis=0)
    g_in, c_g = _all_gather([halves["w_in"], c_rows], me, "gather_w_in")
    wb = dict(w_in=g_in.reshape(IN_W, D))

    c_all = c_g[:, 0, :]
    cc = jnp.concatenate([c_all, c_ctx[None, :], jnp.zeros((7, D), F32)], axis=0)
    sig_cc = jax.nn.sigmoid(cc)
    sc_all = cc * sig_cc
    mod_shard = _ada_fwd(sc_all, w_ada[0])
    mod_g = _all_gather([mod_shard], me, "gather_mod")[0]
    mod_all = jnp.concatenate([mod_g[2 * s] for s in range(4)], axis=1) + b_ada
    mod_x = lax.dynamic_slice_in_dim(mod_all, me, 1, axis=0).reshape(6, D)
    mod_c = mod_all[8].reshape(6, D)[0:2]

    grad_x, small, blocks, recv = _local_step(
        x[0], ctx[0], loss_target[0], mod_x, mod_c, wb, attn_sink, gmlp_ln_g, gmlp_ln_b, w_spatial[0], b_spatial[0],
        ln1_g, ln1_b, ln2_g, ln2_b, later=halves, me=me)

    (recv["w_in"],), late = _scatter_and_gather([blocks["w_in"]], [small["late"]], "scatter_w_in_gather_small")
    late = _with_own(late, [small["late"]], me)[0]
    gathered = _with_own(small["early_gathered"], small["early"], me)
    gathered[0] = jnp.concatenate([gathered[0][:, :16], late, gathered[0][:, 16:]], axis=1)

    me_arr = jnp.reshape(me, (1,)).astype(jnp.int32)
    summed = {n: _sum_blocks(recv[n], blocks[n], me_arr, "sum_grads_" + n) for n in BIG}
    theirs = dict(zip(BIG, _sibling_exchange([summed[n] for n in BIG], "exchange_grads")))

    sums = _sum8_many(gathered, "sum_small")
    stats = sums[0]
    loss = 0.5 * jnp.sum(stats[ROW_LOSS]) / D
    dmod_x_all = jnp.concatenate([gathered[0][:, r_, :] for r_ in ROWS_DMOD_X], axis=1)
    dmod_c_full = jnp.concatenate([stats[r_] for r_ in ROWS_DMOD_C] + [jnp.zeros((4 * D,), F32)])
    dm_rows = jnp.concatenate([dmod_x_all, dmod_c_full[None, :], jnp.zeros((7, 6 * D), F32)], axis=0)
    cs = w_ada.shape[2]
    dm_shard = lax.dynamic_slice_in_dim(dm_rows, chip * cs, cs, axis=1)
    dmc_shard = jnp.concatenate([dm_shard[8:9], jnp.zeros((7, cs), F32)], axis=0)
    g_w_ada, part = _ada_bwd(sc_all.T, dm_shard, dmc_shard, w_ada[0])
    part_all = _all_gather([part * (mc == 0).astype(F32)], me, "gather_c_ctx")[0]
    dsc = _sum8(part_all, "sum_c_ctx")

    grads = dict(w_ada=g_w_ada[None])
    weights = dict(c_ctx=c_ctx, w_ada=w_ada, b_ada=b_ada, w_in=w_in, attn_sink=attn_sink, gmlp_ln_g=gmlp_ln_g,
                   gmlp_ln_b=gmlp_ln_b, w_spatial=w_spatial, b_spatial=b_spatial, w_branch_a=w_branch_a,
                   w_branch_b=w_branch_b, w_out=w_out, ln1_g=ln1_g, ln1_b=ln1_b, w_ffn_in=w_ffn_in, w_ffn_out=w_ffn_out,
                   ln2_g=ln2_g, ln2_b=ln2_b)
    ms = dict(c_ctx=m_c_ctx, w_ada=m_w_ada, b_ada=m_b_ada, w_in=m_w_in, attn_sink=m_attn_sink, gmlp_ln_g=m_gmlp_ln_g,
              gmlp_ln_b=m_gmlp_ln_b, w_spatial=m_w_spatial, b_spatial=m_b_spatial, w_branch_a=m_w_branch_a,
              w_branch_b=m_w_branch_b, w_out=m_w_out, ln1_g=m_ln1_g, ln1_b=m_ln1_b, w_ffn_in=m_w_ffn_in,
              w_ffn_out=m_w_ffn_out, ln2_g=m_ln2_g, ln2_b=m_ln2_b)
    vs = dict(c_ctx=v_c_ctx, w_ada=v_w_ada, b_ada=v_b_ada, w_in=v_w_in, attn_sink=v_attn_sink, gmlp_ln_g=v_gmlp_ln_g,
              gmlp_ln_b=v_gmlp_ln_b, w_spatial=v_w_spatial, b_spatial=v_b_spatial, w_branch_a=v_w_branch_a,
              w_branch_b=v_w_branch_b, w_out=v_w_out, ln1_g=v_ln1_g, ln1_b=v_ln1_b, w_ffn_in=v_w_ffn_in,
              w_ffn_out=v_w_ffn_out, ln2_g=v_ln2_g, ln2_b=v_ln2_b)
    order = list(weights)
    delta, new_m, new_v = {}, {}, {}
    d_, m_, v_ = _adamw(w_ada[0], g_w_ada, m_w_ada[0], v_w_ada[0], "adamw_w_ada")
    delta["w_ada"], new_m["w_ada"], new_v["w_ada"] = d_[None], m_[None], v_[None]
    c_arr = jnp.reshape(mc, (1,)).astype(jnp.int32)
    names = dict(w_in="w_in", w_a="w_branch_a", w_b="w_branch_b", w_o="w_out", w_fi="w_ffn_in", w_fo="w_ffn_out")
    for k, n in names.items():
        flip = (lambda t: t.T) if k == "w_in" else (lambda t: t)
        outs = _adamw_halves(flip(weights[n][0]), summed[k], theirs[k], flip(ms[n][0]), flip(vs[n][0]), c_arr, "adamw_" + n)
        grads[n], delta[n], new_m[n], new_v[n] = [flip(t)[None] for t in outs]

    def view(a):
        return a.reshape(-1, a.shape[-1]) if a.ndim != 1 else a.reshape(1, -1)

    small = _adamw_small(sums, dsc, *[{n: view(d[n]) for n in SMALL} for d in (weights, ms, vs)])
    for out, src in zip((grads, delta, new_m, new_v), small):
        for n in SMALL:
            out[n] = src[n].reshape(weights[n].shape)

    return (loss, grad_x[None], *[grads[n] for n in order], *[delta[n] for n in order],
            *[new_m[n] for n in order], *[new_v[n] for n in order])
```

```python
import functools
import math

import jax
import jax.numpy as jnp
from jax import lax
from jax.experimental import pallas as pl
from jax.experimental.pallas import tpu as pltpu

F32 = jnp.float32
BF16 = jnp.bfloat16

D = 1024
HEAD = 64
N_KV = 2
GROUP = 4
Q_W = 512
KV_W = 128
G_W = 512
BLK = 128
N_GRP = 8
GRP_D = 64
FH = 2816
IN_W = 3840
GRID_W = 64
ROPE_BASE = 10000.0
LN_EPS = 1e-5
NEG = -1e30
ALPHA = (2 * 1) ** 0.25
SCALE = HEAD ** -0.5
GELU_K = math.sqrt(2.0 / math.pi)
GELU_A = 0.044715
ADAM_LR = 0.001
ADAM_B1 = 0.9
ADAM_B2 = 0.999
ADAM_EPS = 1e-08
ADAM_WD = 0.01
ADAM_STEP = 10
N_DEV = 8
N_SHARD = 4
FH_SHARD = FH // 2
LANES = 128
VMEM_LIMIT = 56 * 1024 * 1024
MESH = pl.DeviceIdType.MESH


def _cp(*sem):
    return pltpu.CompilerParams(dimension_semantics=sem, vmem_limit_bytes=VMEM_LIMIT)


def _resident(shape):
    return pl.BlockSpec(shape, lambda *_: (0,) * len(shape), pipeline_mode=pl.Buffered(1))


def _rows(tm, width):
    return pl.BlockSpec((tm, width), lambda i: (i, 0))


def _acc(shape):
    return pl.BlockSpec(shape, lambda *_: (0,) * len(shape))


def _dot(a, b):
    return jnp.dot(a, b, preferred_element_type=F32)


def _dot_nt(a, b):
    return lax.dot_general(a, b, (((1,), (1,)), ((), ())), preferred_element_type=F32)


def _dot_tn(a, b):
    return lax.dot_general(a, b, (((0,), (0,)), ((), ())), preferred_element_type=F32)


def _ln(x):
    mu = jnp.mean(x, axis=-1, keepdims=True)
    xc = x - mu
    var = jnp.mean(xc * xc, axis=-1, keepdims=True)
    rstd = lax.rsqrt(var + LN_EPS)
    return xc * rstd, rstd


def _ln_bwd(dxhat, xhat, rstd):
    return (dxhat - jnp.mean(dxhat, axis=-1, keepdims=True)
            - xhat * jnp.mean(dxhat * xhat, axis=-1, keepdims=True)) * rstd


def _sig(x):
    return 0.5 + 0.5 * jnp.tanh(0.5 * x)


def _gelu(x):
    t = jnp.tanh(x * (GELU_K + (GELU_K * GELU_A) * (x * x)))
    hx = 0.5 * x
    return hx + hx * t, t


def _gelu_grad(x, t):
    return 0.5 + 0.5 * t + (0.5 * x) * (1.0 - t * t) * (GELU_K + (3.0 * GELU_K * GELU_A) * (x * x))


def _colsum(v):
    return jnp.sum(v, axis=0, keepdims=True)


def _partner(x):
    w = x.shape[1]
    lane = lax.broadcasted_iota(jnp.int32, x.shape, 1)
    return jnp.where((lane & 31) < 16, pltpu.roll(x, w - 16, 1), pltpu.roll(x, 16, 1))


def _rope(x, cos, sin):
    return x * cos + _partner(x) * sin


def _unrope(g, cos, sin):
    return g * cos + _partner(g * sin)


def _rope_tables(seq):
    inv = ROPE_BASE ** (-jnp.arange(HEAD // 4, dtype=F32) / (HEAD // 4))
    pos = jnp.arange(seq, dtype=jnp.int32)
    ar = (pos // GRID_W).astype(F32)[:, None] * inv
    ac = (pos % GRID_W).astype(F32)[:, None] * inv
    cos = jnp.concatenate([jnp.cos(ar), jnp.cos(ar), jnp.cos(ac), jnp.cos(ac)], axis=-1)
    sin = jnp.concatenate([-jnp.sin(ar), jnp.sin(ar), -jnp.sin(ac), jnp.sin(ac)], axis=-1)
    return jnp.tile(cos, (1, LANES // HEAD)), jnp.tile(sin, (1, LANES // HEAD))


def _ctx_fwd(ctx, modc, w_kv):
    n_ctx = ctx.shape[0]

    def body(ctx_ref, mod_ref, w_ref, hc_ref, kvc_ref, vac_ref):
        xhat, _ = _ln(ctx_ref[...])
        hc = (xhat * (1.0 + mod_ref[1:2, :]) + mod_ref[0:1, :]).astype(BF16)
        hc_ref[...] = hc
        kvc = _dot_nt(hc, w_ref[...]).astype(BF16)
        kvc_ref[...] = kvc
        vac_ref[...] = _with_ones(kvc[:, KV_W:])

    return pl.pallas_call(
        body, name="ctx_fwd", grid=(1,),
        in_specs=[_acc((n_ctx, D)), _acc((8, D)), _acc((2 * KV_W, D))],
        out_specs=[_acc((n_ctx, D)), _acc((n_ctx, 2 * KV_W)), _acc((n_ctx, 2 * LANES))],
        out_shape=[jax.ShapeDtypeStruct((n_ctx, D), BF16), jax.ShapeDtypeStruct((n_ctx, 2 * KV_W), BF16),
                   jax.ShapeDtypeStruct((n_ctx, 2 * LANES), BF16)],
        compiler_params=_cp("arbitrary"),
    )(ctx, modc, w_kv)


def _host_start(step, comm):
    if comm is not None:
        @pl.when(step == 0)
        def _():
            comm.start()


def _host_finish(step, last, comm, forward_at=None):
    if comm is None:
        return
    if forward_at is None or forward_at >= last:
        @pl.when(step == last)
        def _():
            comm.finish()
    else:
        @pl.when(step == forward_at)
        def _():
            comm.forward()

        @pl.when(step == last)
        def _():
            comm.drain()


def _proj_fwd(x, modx, w_in, cos, sin, tm, gather=()):
    seq = x.shape[0]
    ng = len(gather)

    def body(x_ref, mod_ref, w_ref, cos_ref, sin_ref, *rest):
        h_ref, q_ref, kv_ref, va_ref, uv_ref, gab_ref = rest[ng:ng + 6]
        comm = _Gather(rest[:ng], rest[ng + 6:2 * ng + 6], *rest[2 * ng + 6:]) if ng else None
        _host_start(pl.program_id(0), comm)
        xhat, _ = _ln(x_ref[...])
        h = (xhat * (1.0 + mod_ref[1:2, :]) + mod_ref[0:1, :]).astype(BF16)
        h_ref[...] = h
        cos1, sin1 = cos_ref[...], sin_ref[...]
        cos2 = jnp.concatenate([cos1, cos1], axis=1)
        sin2 = jnp.concatenate([sin1, sin1], axis=1)
        for j in range(Q_W // 256):
            t = _dot_nt(h, w_ref[256 * j:256 * (j + 1), :])
            q_ref[:, 256 * j:256 * (j + 1)] = (_rope(t, cos2, sin2) * SCALE).astype(BF16)
        t = _dot_nt(h, w_ref[Q_W:Q_W + 2 * KV_W, :])
        kv_ref[:, :KV_W] = _rope(t[:, :KV_W], cos1, sin1).astype(BF16)
        v = t[:, KV_W:].astype(BF16)
        kv_ref[:, KV_W:] = v
        va_ref[...] = _with_ones(v)
        o = Q_W + 2 * KV_W
        for j in range(2):
            uv_ref[:, G_W * j:G_W * (j + 1)] = _dot_nt(h, w_ref[o + G_W * j:o + G_W * (j + 1), :]).astype(BF16)
        o += 2 * G_W
        for j in range(4):
            gab_ref[:, 512 * j:512 * (j + 1)] = _dot_nt(h, w_ref[o + 512 * j:o + 512 * (j + 1), :]).astype(BF16)
        _host_finish(pl.program_id(0), seq // tm - 1, comm, forward_at=(seq // tm) // 2)

    out = pl.pallas_call(
        body, name="proj_fwd", grid=(seq // tm,),
        in_specs=[_rows(tm, D), _acc((8, D)), _resident((IN_W, D)), _rows(tm, LANES), _rows(tm, LANES)] + _comm_specs(ng),
        out_specs=[_rows(tm, D), _rows(tm, Q_W), _rows(tm, 2 * KV_W), _rows(tm, 2 * LANES), _rows(tm, 2 * G_W),
                   _rows(tm, 2 * D)] + _comm_specs(ng),
        out_shape=[jax.ShapeDtypeStruct((seq, D), BF16), jax.ShapeDtypeStruct((seq, Q_W), BF16),
                   jax.ShapeDtypeStruct((seq, 2 * KV_W), BF16), jax.ShapeDtypeStruct((seq, 2 * LANES), BF16),
                   jax.ShapeDtypeStruct((seq, 2 * G_W), BF16), jax.ShapeDtypeStruct((seq, 2 * D), BF16)] + _gathered_shapes(gather),
        scratch_shapes=_comm_scratch(ng) if ng else [],
        compiler_params=_cp("arbitrary"),
    )(x, modx, w_in, cos, sin, *gather)
    return out[:6], out[6:]


def _stack_heads(x, hk):
    return jnp.concatenate([x[:, (hk * GROUP + g) * HEAD:(hk * GROUP + g + 1) * HEAD] for g in range(GROUP)], axis=0)


def _band_masks(n, nb):
    rows = GROUP * BLK
    qi = lax.broadcasted_iota(jnp.int32, (rows, BLK), 0) & (BLK - 1)
    kj = lax.broadcasted_iota(jnp.int32, (rows, BLK), 1)
    return (kj >= qi) & (n > 0), (kj <= qi) & (n < nb - 1)


def _attn_scores(q, k_refs, hk, masks):
    q4 = _stack_heads(q, hk)
    ks = [r[:, hk * HEAD:(hk + 1) * HEAD] for r in k_refs]
    s = [_dot_nt(q4, k) for k in ks]
    s[1] = jnp.where(masks[0], s[1], NEG)
    s[3] = jnp.where(masks[1], s[3], NEG)
    return q4, ks, s


def _sink_rows(sink_ref, hk):
    rows = GROUP * BLK
    rg = lax.broadcasted_iota(jnp.int32, (rows, 1), 0) >> 7
    sink_v = jnp.full((rows, 1), sink_ref[0, hk * GROUP], F32)
    for g in range(1, GROUP):
        sink_v = jnp.where(rg == g, sink_ref[0, hk * GROUP + g], sink_v)
    return sink_v


def _with_ones(v):
    ones = jnp.ones((v.shape[0], HEAD), v.dtype)
    return jnp.concatenate([v[:, :HEAD], ones, v[:, HEAD:], ones], axis=1)


def _kv_specs(nb, qb):
    def spec(d):
        return pl.BlockSpec((BLK, 2 * KV_W), lambda n: (jnp.clip(qb * n + d, 0, nb - 1), 0))
    return [spec(d) for d in range(-1, qb + 1)]


def _attn_fwd(q, kv, va, kvc, vac, sink, gather=()):
    seq = q.shape[0]
    nb = seq // BLK
    n_ctx = kvc.shape[0]
    ng = len(gather)
    Q_BLOCKS = 1
    nkv = Q_BLOCKS + 2
    steps = nb // Q_BLOCKS

    def body(q_ref, *rest):
        kv_refs, va_refs = rest[:nkv], rest[nkv:2 * nkv]
        kvc_ref, vac_ref, sink_ref = rest[2 * nkv:2 * nkv + 3]
        rest = rest[2 * nkv + 3:]
        o_ref, lse_ref = rest[ng:ng + 2]
        comm = _Gather(rest[:ng], rest[ng + 2:2 * ng + 2], *rest[2 * ng + 2:]) if ng else None
        n = pl.program_id(0)
        _host_start(n, comm)
        lane = lax.broadcasted_iota(jnp.int32, (BLK, LANES), 1)
        for sub in range(Q_BLOCKS):
            rs = slice(sub * BLK, (sub + 1) * BLK)
            q = q_ref[rs, :]
            outs = []
            lse_all = jnp.zeros((BLK, LANES), F32)
            masks = _band_masks(Q_BLOCKS * n + sub, nb)
            for hk in range(N_KV):
                _, _, s = _attn_scores(q, (kvc_ref,) + kv_refs[sub:sub + 3], hk, masks)
                sink_v = _sink_rows(sink_ref, hk)
                tile_max = s[1]
                for t in [s[0][:, i * LANES:(i + 1) * LANES] for i in range(n_ctx // LANES)] + s[2:]:
                    tile_max = jnp.maximum(tile_max, t)
                m = jnp.maximum(sink_v, jnp.max(tile_max, axis=-1, keepdims=True))
                o = jnp.zeros((GROUP * BLK, LANES), F32)
                for t, va_ref in zip(s, (vac_ref,) + va_refs[sub:sub + 3]):
                    o = o + _dot(jnp.exp((t - m).astype(BF16)), va_ref[:, hk * LANES:(hk + 1) * LANES])
                denom = o[:, HEAD:HEAD + 1] + jnp.exp(sink_v - m)
                o4 = o[:, :HEAD] * (1.0 / denom)
                lse4 = m + jnp.log(denom)
                for g in range(GROUP):
                    outs.append(o4[g * BLK:(g + 1) * BLK, :])
                    lse_all = jnp.where(lane == hk * GROUP + g, lse4[g * BLK:(g + 1) * BLK, :], lse_all)
            o_ref[rs, :] = jnp.concatenate(outs, axis=1).astype(BF16)
            lse_ref[rs, :] = lse_all
        _host_finish(n, steps - 1, comm, forward_at=(3 * steps) // 4)

    tq = Q_BLOCKS * BLK
    out = pl.pallas_call(
        body, name="attn_fwd", grid=(steps,),
        in_specs=[_rows(tq, Q_W)] + _kv_specs(nb, Q_BLOCKS) + _kv_specs(nb, Q_BLOCKS)
        + [_acc((n_ctx, 2 * KV_W)), _acc((n_ctx, 2 * LANES)), pl.BlockSpec(memory_space=pltpu.SMEM)] + _comm_specs(ng),
        out_specs=[_rows(tq, Q_W), _rows(tq, LANES)] + _comm_specs(ng),
        out_shape=[jax.ShapeDtypeStruct((seq, Q_W), BF16), jax.ShapeDtypeStruct((seq, LANES), F32)] + _gathered_shapes(gather),
        scratch_shapes=_comm_scratch(ng) if ng else [],
        compiler_params=_cp("arbitrary"),
    )(q, *([kv] * nkv), *([va] * nkv), kvc, vac, sink, *gather)
    return out[:2], out[2:]


def _gmlp_chunk(u, vb, gp_ref, ws_ref, bias_ref):
    gu, tu = _gelu(u)
    gv, tv = _gelu(vb)
    vhat, rstd = _ln(gv)
    vn = (vhat * gp_ref[0:1, :] + gp_ref[1:2, :]).astype(BF16)
    s = bias_ref[...] + jnp.concatenate(
        [_dot(ws_ref[g * BLK:(g + 1) * BLK, :], vn[:, g * GRP_D:(g + 1) * GRP_D]) for g in range(N_GRP)], axis=1)
    return gu, tu, tv, vhat, rstd, vn, s


def _mix_fwd(uv, gab, ya, gp, ws_stack, bias_full, w_a, w_b, w_o, tm, gather=()):
    seq = uv.shape[0]
    ng = len(gather)
    steps = seq // tm

    def body(uv_ref, gab_ref, ya_ref, gp_ref, ws_ref, bias_ref, wa_ref, wb_ref, wo_ref, *rest):
        a_ref, b_ref, mix_ref, merged_ref, yb_ref = rest[ng:ng + 5]
        comm = _Gather(rest[:ng], rest[ng + 5:2 * ng + 5], *rest[2 * ng + 5:]) if ng else None
        _host_start(pl.program_id(0), comm)
        for c in range(tm // BLK):
            rs = slice(c * BLK, (c + 1) * BLK)
            gu, _, _, _, _, _, s = _gmlp_chunk(uv_ref[rs, :G_W].astype(F32), uv_ref[rs, G_W:].astype(F32), gp_ref, ws_ref, bias_ref)
            yb_ref[rs, :] = (gu * s).astype(BF16)
        a = _dot(ya_ref[...], wa_ref[...])
        b = _dot(yb_ref[...], wb_ref[...])
        a_ref[...] = a.astype(BF16)
        b_ref[...] = b.astype(BF16)
        merged = (_sig(gab_ref[:, :D].astype(F32)) * a + _sig(gab_ref[:, D:].astype(F32)) * b).astype(BF16)
        merged_ref[...] = merged
        mix_ref[...] = _dot(merged, wo_ref[...])
        _host_finish(pl.program_id(0), steps - 1, comm, forward_at=(3 * steps) // 4)

    out = pl.pallas_call(
        body, name="mix_fwd", grid=(steps,),
        in_specs=[_rows(tm, 2 * G_W), _rows(tm, 2 * D), _rows(tm, Q_W), _acc((8, G_W)),
                  _resident((N_GRP * BLK, BLK)), _acc((BLK, G_W)),
                  _resident((Q_W, D)), _resident((G_W, D)), _resident((D, D))] + _comm_specs(ng),
        out_specs=[_rows(tm, D), _rows(tm, D), _rows(tm, D), _rows(tm, D), _rows(tm, G_W)] + _comm_specs(ng),
        out_shape=[jax.ShapeDtypeStruct((seq, D), BF16), jax.ShapeDtypeStruct((seq, D), BF16),
                   jax.ShapeDtypeStruct((seq, D), F32), jax.ShapeDtypeStruct((seq, D), BF16),
                   jax.ShapeDtypeStruct((seq, G_W), BF16)] + _gathered_shapes(gather),
        scratch_shapes=_comm_scratch(ng) if ng else [],
        compiler_params=_cp("arbitrary"),
    )(uv, gab, ya, gp, ws_stack, bias_full, w_a, w_b, w_o, *gather)
    return out[:5], out[5:]


FFN_CHUNK = 512


def _ffn_chunks():
    out = []
    for hh in range(2):
        off = 0
        while off < FH_SHARD:
            w = min(FFN_CHUNK, FH_SHARD - off)
            out.append((hh, off, w))
            off += w
    return out


def _mid_recompute(x_ref, mix_ref, vec_ref):
    r1 = ALPHA * x_ref[...] + vec_ref[0:1, :] * mix_ref[...]
    xh1, rstd1 = _ln(r1)
    xmid = xh1 * vec_ref[1:2, :] + vec_ref[2:3, :]
    xh2, rstd2 = _ln(xmid)
    return xh1, rstd1, xmid, xh2, rstd2


def _ffn(x, mix, tgt, vec, w_fi, w_fo, tm):
    seq = x.shape[0]

    def body(x_ref, mix_ref, tgt_ref, vec_ref, wi_ref, wo_ref, act_ref, h2_ref, dff_ref, df_ref, dr1_ref, st_ref, gu_ref):
        @pl.when(pl.program_id(0) == 0)
        def _():
            st_ref[...] = jnp.zeros_like(st_ref)

        xh1, rstd1, xmid, xh2, rstd2 = _mid_recompute(x_ref, mix_ref, vec_ref)
        h2 = (xh2 * (1.0 + vec_ref[4:5, :]) + vec_ref[3:4, :]).astype(BF16)
        h2_ref[...] = h2
        halves = [(slice(hh * FH_SHARD, (hh + 1) * FH_SHARD), slice(FH + hh * FH_SHARD, FH + (hh + 1) * FH_SHARD))
                  for hh in range(2)]
        for hh, (cs, cu) in enumerate(halves):
            g = _dot(h2, wi_ref[hh])
            u = _dot(h2, wi_ref[2 + hh])
            gu_ref[:, cs] = g
            gu_ref[:, cu] = u
            act_ref[:, cs] = (g * _sig(g) * u).astype(BF16)
        f = _dot(act_ref[...], wo_ref[...])
        r2 = ALPHA * xmid + vec_ref[5:6, :] * f
        yh, rstd = _ln(r2)
        y = yh * vec_ref[6:7, :] + vec_ref[7:8, :]
        err = y - tgt_ref[...]
        dy = err / D
        dr2 = _ln_bwd(dy * vec_ref[6:7, :], yh, rstd)
        st_ref[0:1, :] += _colsum(err * err)
        st_ref[1:2, :] += _colsum(dy * yh)
        st_ref[2:3, :] += _colsum(dy)
        st_ref[3:4, :] += _colsum(dr2 * f)

        df = (dr2 * vec_ref[5:6, :]).astype(BF16)
        df_ref[...] = df
        da_all = _dot_nt(df, wo_ref[...])
        for cs, cu in halves:
            da = da_all[:, cs]
            g = gu_ref[:, cs]
            u = gu_ref[:, cu]
            sg = _sig(g)
            dff_ref[:, cs] = (da * u * sg * (1.0 + g * (1.0 - sg))).astype(BF16)
            dff_ref[:, cu] = (da * g * sg).astype(BF16)
        dh2 = _dot_nt(dff_ref[:, :FH_SHARD], wi_ref[0])
        for s in range(1, N_SHARD):
            dh2 = dh2 + _dot_nt(dff_ref[:, s * FH_SHARD:(s + 1) * FH_SHARD], wi_ref[s])
        dxmid = _ln_bwd(dh2 * (1.0 + vec_ref[4:5, :]), xh2, rstd2) + ALPHA * dr2
        dr1 = _ln_bwd(dxmid * vec_ref[1:2, :], xh1, rstd1)
        dr1_ref[...] = dr1
        st_ref[8:9, :] += _colsum(dh2 * xh2)
        st_ref[9:10, :] += _colsum(dh2)
        st_ref[10:11, :] += _colsum(dxmid * xh1)
        st_ref[11:12, :] += _colsum(dxmid)
        st_ref[12:13, :] += _colsum(dr1 * mix_ref[...])

    return pl.pallas_call(
        body, name="ffn", grid=(seq // tm,),
        in_specs=[_rows(tm, D), _rows(tm, D), _rows(tm, D), _acc((8, D)), _resident((N_SHARD, D, FH_SHARD)), _resident((FH, D))],
        out_specs=[_rows(tm, FH), _rows(tm, D), _rows(tm, 2 * FH), _rows(tm, D), _rows(tm, D), _acc((16, D))],
        out_shape=[jax.ShapeDtypeStruct((seq, FH), BF16), jax.ShapeDtypeStruct((seq, D), BF16),
                   jax.ShapeDtypeStruct((seq, 2 * FH), BF16), jax.ShapeDtypeStruct((seq, D), BF16),
                   jax.ShapeDtypeStruct((seq, D), F32), jax.ShapeDtypeStruct((16, D), F32)],
        scratch_shapes=[pltpu.VMEM((tm, 2 * FH), F32)],
        compiler_params=_cp("arbitrary"),
    )(x, mix, tgt, vec, w_fi, w_fo)


def _mix_bwd(dr1, a, b, gab, uv, merged, ya, yb, vec, gp, ws_stack, ws_stack_t, bias_full, w_a, w_b, w_o, tm, scatter=()):
    seq = dr1.shape[0]
    last = seq // tm - 1
    ns = len(scatter)

    def body(dr1_ref, a_ref, b_ref, gab_ref, uv_ref, mg_ref, ya_ref, yb_ref, vec_ref, gp_ref, ws_ref, wst_ref, bias_ref,
             wa_ref, wb_ref, wo_ref, *rest):
        dya_ref, dp_ref, dws_ref, dbs_ref, st_ref, gwo_ref, gwa_ref, gwb_ref = rest[ns:ns + 8]
        acc_o, acc_a, acc_b = rest[2 * ns + 8:2 * ns + 11]
        comm = _AllToAll(rest[:ns], rest[ns + 8:2 * ns + 8], *rest[2 * ns + 11:]) if ns else None
        _host_start(pl.program_id(0), comm)

        @pl.when(pl.program_id(0) == 0)
        def _():
            dws_ref[...] = jnp.zeros_like(dws_ref)
            dbs_ref[...] = jnp.zeros_like(dbs_ref)
            st_ref[...] = jnp.zeros_like(st_ref)
            acc_o[...] = jnp.zeros_like(acc_o)
            acc_a[...] = jnp.zeros_like(acc_a)
            acc_b[...] = jnp.zeros_like(acc_b)

        dmix = (dr1_ref[...] * vec_ref[0:1, :]).astype(BF16)
        acc_o[...] += _dot_tn(mg_ref[...], dmix)
        dmerged = _dot_nt(dmix, wo_ref[...])
        sa = _sig(gab_ref[:, :D].astype(F32))
        sb = _sig(gab_ref[:, D:].astype(F32))
        da = (dmerged * sa).astype(BF16)
        db = (dmerged * sb).astype(BF16)
        dp_ref[:, 2 * G_W:2 * G_W + D] = (dmerged * a_ref[...].astype(F32) * sa * (1.0 - sa)).astype(BF16)
        dp_ref[:, 2 * G_W + D:] = (dmerged * b_ref[...].astype(F32) * sb * (1.0 - sb)).astype(BF16)
        dya_ref[...] = _dot_nt(da, wa_ref[...]).astype(BF16)
        dyb = _dot_nt(db, wb_ref[...])
        acc_a[...] += _dot_tn(ya_ref[...], da)
        acc_b[...] += _dot_tn(yb_ref[...], db)

        @pl.when(pl.program_id(0) == last)
        def _():
            gwo_ref[...] = acc_o[...].astype(BF16)
            gwa_ref[...] = acc_a[...].astype(BF16)
            gwb_ref[...] = acc_b[...].astype(BF16)

        for c in range(tm // BLK):
            rs = slice(c * BLK, (c + 1) * BLK)
            u = uv_ref[rs, :G_W].astype(F32)
            vb = uv_ref[rs, G_W:].astype(F32)
            gu, tu, tv, vhat, rstd, vn, s = _gmlp_chunk(u, vb, gp_ref, ws_ref, bias_ref)
            dyb_c = dyb[rs, :]
            ds = dyb_c * gu
            du = dyb_c * s * _gelu_grad(u, tu)
            ds_b = ds.astype(BF16)
            dvn_g = []
            for g in range(N_GRP):
                cg = slice(g * GRP_D, (g + 1) * GRP_D)
                dvn_g.append(_dot(wst_ref[:, g * BLK:(g + 1) * BLK], ds_b[:, cg]))
                dws_ref[g * BLK:(g + 1) * BLK, :] += _dot_nt(ds_b[:, cg], vn[:, cg])
            dvn = jnp.concatenate(dvn_g, axis=1)
            dbs_ref[...] += ds
            st_ref[0:1, :] += _colsum(dvn * vhat)
            st_ref[1:2, :] += _colsum(dvn)
            dgv = _ln_bwd(dvn * gp_ref[0:1, :], vhat, rstd)
            dvb = dgv * _gelu_grad(vb, tv)
            dp_ref[rs, :G_W] = du.astype(BF16)
            dp_ref[rs, G_W:2 * G_W] = dvb.astype(BF16)
        _host_finish(pl.program_id(0), last, comm)

    pw = 2 * G_W + 2 * D
    out = pl.pallas_call(
        body, name="mix_bwd", grid=(seq // tm,),
        in_specs=[_rows(tm, D), _rows(tm, D), _rows(tm, D), _rows(tm, 2 * D), _rows(tm, 2 * G_W), _rows(tm, D), _rows(tm, Q_W),
                  _rows(tm, G_W), _acc((8, D)), _acc((8, G_W)),
                  _resident((N_GRP * BLK, BLK)), _resident((BLK, N_GRP * BLK)), _acc((BLK, G_W)),
                  _resident((Q_W, D)), _resident((G_W, D)), _resident((D, D))] + _comm_specs(ns),
        out_specs=[_rows(tm, Q_W), _rows(tm, pw), _acc((N_GRP * BLK, BLK)), _acc((BLK, G_W)), _acc((8, G_W)),
                   _acc((D, D)), _acc((Q_W, D)), _acc((G_W, D))] + _comm_specs(ns),
        out_shape=[jax.ShapeDtypeStruct((seq, Q_W), BF16), jax.ShapeDtypeStruct((seq, pw), BF16),
                   jax.ShapeDtypeStruct((N_GRP * BLK, BLK), F32), jax.ShapeDtypeStruct((BLK, G_W), F32),
                   jax.ShapeDtypeStruct((8, G_W), F32), jax.ShapeDtypeStruct((D, D), BF16),
                   jax.ShapeDtypeStruct((Q_W, D), BF16), jax.ShapeDtypeStruct((G_W, D), BF16)]
        + [jax.ShapeDtypeStruct(v.shape, v.dtype) for v in scatter],
        scratch_shapes=[pltpu.VMEM((D, D), F32), pltpu.VMEM((Q_W, D), F32), pltpu.VMEM((G_W, D), F32)]
        + (_comm_scratch(ns) if ns else []),
        compiler_params=_cp("arbitrary"),
    )(dr1, a, b, gab, uv, merged, ya, yb, vec, gp, ws_stack, ws_stack_t, bias_full, w_a, w_b, w_o, *scatter)
    return out[:8], out[8:]


def _attn_bwd(q, kv, kvc, sink, dya, ya, lse, scatter=()):
    seq = q.shape[0]
    nb = seq // BLK
    n_ctx = kvc.shape[0]
    ns = len(scatter)
    Q_BLOCKS = 2
    nkv = Q_BLOCKS + 2
    steps = nb // Q_BLOCKS

    def body(q_ref, *rest):
        kv_refs = rest[:nkv]
        kvc_ref, sink_ref, do_ref, o_ref, lse_ref = rest[nkv:nkv + 5]
        rest = rest[nkv + 5:]
        dq_ref, dkv_ref, dkvc_ref, dsink_ref = rest[ns:ns + 4]
        comm = _AllToAll(rest[:ns], rest[ns + 4:2 * ns + 4], *rest[2 * ns + 4:]) if ns else None
        n = pl.program_id(0)
        _host_start(n, comm)

        @pl.when(n == 0)
        def _():
            dkv_ref[...] = jnp.zeros_like(dkv_ref)
            dkvc_ref[...] = jnp.zeros_like(dkvc_ref)
            dsink_ref[...] = jnp.zeros_like(dsink_ref)

        lane = lax.broadcasted_iota(jnp.int32, (1, LANES), 1)
        for sub in range(Q_BLOCKS):
            rs = slice(sub * BLK, (sub + 1) * BLK)
            blk = Q_BLOCKS * n + sub
            q = q_ref[rs, :]
            do = do_ref[rs, :]
            out = o_ref[rs, :]
            lse_all = lse_ref[rs, :]
            k_refs = (kvc_ref,) + kv_refs[sub:sub + 3]
            masks = _band_masks(blk, nb)
            dqs, dks, dvs = [], [], []
            for hk in range(N_KV):
                q4, ks, s = _attn_scores(q, k_refs, hk, masks)
                vs = [r[:, KV_W + hk * HEAD:KV_W + (hk + 1) * HEAD] for r in k_refs]
                lse4 = jnp.concatenate([lse_all[:, hk * GROUP + g:hk * GROUP + g + 1] for g in range(GROUP)], axis=0)
                do4 = _stack_heads(do, hk)
                delta = jnp.sum(do4.astype(F32) * _stack_heads(out, hk).astype(F32), axis=-1, keepdims=True)
                p = [jnp.exp((t - lse4).astype(BF16)) for t in s]
                ds = [t * (_dot_nt(do4, v) - delta).astype(BF16) for t, v in zip(p, vs)]
                dq4 = _dot(ds[0], ks[0])
                for t, k in zip(ds[1:], ks[1:]):
                    dq4 = dq4 + _dot(t, k)
                dq4 = dq4 * SCALE
                dqs += [dq4[g * BLK:(g + 1) * BLK, :] for g in range(GROUP)]
                dks.append([_dot_tn(t, q4) for t in ds])
                dvs.append([_dot_tn(t, do4) for t in p])
                ps = jnp.exp(_sink_rows(sink_ref, hk) - lse4) * delta
                for g in range(GROUP):
                    part = -jnp.sum(ps[g * BLK:(g + 1) * BLK, :], axis=0, keepdims=True)
                    dsink_ref[0:1, :] += jnp.where(lane == hk * GROUP + g, part, 0.0)
            dq_ref[rs, :] = jnp.concatenate(dqs, axis=1)

            def piece(i):
                return jnp.concatenate([dks[0][i], dks[1][i], dvs[0][i], dvs[1][i]], axis=1)

            dkvc_ref[...] += piece(0)
            starts = (jnp.maximum(blk - 1, 0), blk, jnp.minimum(blk + 1, nb - 1))
            for i, st in enumerate(starts):
                r = pl.ds(pl.multiple_of(st * BLK, BLK), BLK)
                dkv_ref[r, :] += piece(i + 1)
        _host_finish(n, steps - 1, comm)

    tq = Q_BLOCKS * BLK
    out = pl.pallas_call(
        body, name="attn_bwd", grid=(steps,),
        in_specs=[_rows(tq, Q_W)] + _kv_specs(nb, Q_BLOCKS) + [_acc((n_ctx, 2 * KV_W)), pl.BlockSpec(memory_space=pltpu.SMEM),
                                                     _rows(tq, Q_W), _rows(tq, Q_W), _rows(tq, LANES)] + _comm_specs(ns),
        out_specs=[_rows(tq, Q_W), _acc((seq, 2 * KV_W)), _acc((n_ctx, 2 * KV_W)), _acc((8, LANES))] + _comm_specs(ns),
        out_shape=[jax.ShapeDtypeStruct((seq, Q_W), F32), jax.ShapeDtypeStruct((seq, 2 * KV_W), F32),
                   jax.ShapeDtypeStruct((n_ctx, 2 * KV_W), F32), jax.ShapeDtypeStruct((8, LANES), F32)]
        + [jax.ShapeDtypeStruct(v.shape, v.dtype) for v in scatter],
        scratch_shapes=_comm_scratch(ns) if ns else [],
        compiler_params=_cp("arbitrary"),
    )(q, *([kv] * nkv), kvc, sink, dya, ya, lse, *scatter)
    return out[:4], out[4:]


def _proj_bwd(dq, dkv, dpb, x, dr1, modx, w_in, cos, sin, tm, scatter=()):
    seq = x.shape[0]
    pw = IN_W - Q_W - 2 * KV_W
    ns = len(scatter)

    def body(dq_ref, dkv_ref, dpb_ref, x_ref, dr1_ref, mod_ref, w_ref, cos_ref, sin_ref, *rest):
        dqkv_ref, gx_ref, st_ref = rest[ns:ns + 3]
        comm = _AllToAll(rest[:ns], rest[ns + 3:2 * ns + 3], *rest[2 * ns + 3:]) if ns else None
        _host_start(pl.program_id(0), comm)

        @pl.when(pl.program_id(0) == 0)
        def _():
            st_ref[...] = jnp.zeros_like(st_ref)

        cos1, sin1 = cos_ref[...], sin_ref[...]
        cos2 = jnp.concatenate([cos1, cos1], axis=1)
        sin2 = jnp.concatenate([sin1, sin1], axis=1)
        for j in range(Q_W // 256):
            cs = slice(256 * j, 256 * (j + 1))
            dqkv_ref[:, cs] = _unrope(dq_ref[:, cs], cos2, sin2).astype(BF16)
        dqkv_ref[:, Q_W:Q_W + KV_W] = _unrope(dkv_ref[:, :KV_W], cos1, sin1).astype(BF16)
        dqkv_ref[:, Q_W + KV_W:] = dkv_ref[:, KV_W:].astype(BF16)
        o = Q_W + 2 * KV_W
        dh = _dot(dqkv_ref[...], w_ref[:o, :]) + _dot(dpb_ref[...], w_ref[o:, :])
        xhat, rstd = _ln(x_ref[...])
        st_ref[0:1, :] += _colsum(dh)
        st_ref[1:2, :] += _colsum(dh * xhat)
        gx_ref[...] = _ln_bwd(dh * (1.0 + mod_ref[1:2, :]), xhat, rstd) + ALPHA * dr1_ref[...]
        _host_finish(pl.program_id(0), seq // tm - 1, comm)

    out = pl.pallas_call(
        body, name="proj_bwd", grid=(seq // tm,),
        in_specs=[_rows(tm, Q_W), _rows(tm, 2 * KV_W), _rows(tm, pw), _rows(tm, D), _rows(tm, D), _acc((8, D)),
                  _resident((IN_W, D)), _rows(tm, LANES), _rows(tm, LANES)] + _comm_specs(ns),
        out_specs=[_rows(tm, Q_W + 2 * KV_W), _rows(tm, D), _acc((8, D))] + _comm_specs(ns),
        out_shape=[jax.ShapeDtypeStruct((seq, Q_W + 2 * KV_W), BF16), jax.ShapeDtypeStruct((seq, D), F32),
                   jax.ShapeDtypeStruct((8, D), F32)] + [jax.ShapeDtypeStruct(v.shape, v.dtype) for v in scatter],
        scratch_shapes=_comm_scratch(ns) if ns else [],
        compiler_params=_cp("arbitrary"),
    )(dq, dkv, dpb, x, dr1, modx, w_in, cos, sin, *scatter)
    return out[:3], out[3:]


def _ctx_bwd(dkvc, ctx, hc, w_kv):
    n_ctx = ctx.shape[0]

    def body(dkvc_ref, ctx_ref, hc_ref, w_ref, dw_ref, st_ref):
        d = dkvc_ref[...].astype(BF16)
        dw_ref[...] = _dot_tn(d, hc_ref[...])
        dhc = _dot(d, w_ref[...])
        xhat, _ = _ln(ctx_ref[...])
        st_ref[...] = jnp.zeros_like(st_ref)
        st_ref[0:1, :] = _colsum(dhc)
        st_ref[1:2, :] = _colsum(dhc * xhat)

    return pl.pallas_call(
        body, name="ctx_bwd", grid=(1,),
        in_specs=[_acc((n_ctx, 2 * KV_W)), _acc((n_ctx, D)), _acc((n_ctx, D)), _acc((2 * KV_W, D))],
        out_specs=[_acc((2 * KV_W, D)), _acc((8, D))],
        out_shape=[jax.ShapeDtypeStruct((2 * KV_W, D), F32), jax.ShapeDtypeStruct((8, D), F32)],
        compiler_params=_cp("arbitrary"),
    )(dkvc, ctx, hc, w_kv)


def _tn_matmul(a, b, tn, name, out_dtype, shard_major=False, init=None, tk=512, scatter=(), gather=()):
    t, ka = a.shape
    n = b.shape[1]
    tk = min(tk, t)
    nk = t // tk
    nj = n // tn
    has_init = init is not None
    assert not (scatter and gather)
    moved = list(scatter) + list(gather)
    pattern = _AllToAll if scatter else _Gather
    ns = len(moved)
    n_in = 3 if has_init else 2

    def body(*refs):
        a_ref, b_ref = refs[:2]
        i_ref = refs[2] if has_init else None
        rest = refs[n_in:]
        o_ref = rest[ns]
        acc_ref = rest[2 * ns + 1]
        comm = pattern(rest[:ns], rest[ns + 1:2 * ns + 1], *rest[2 * ns + 2:]) if ns else None
        k = pl.program_id(1)
        step = pl.program_id(0) * nk + k
        _host_start(step, comm)

        @pl.when(k == 0)
        def _():
            acc_ref[...] = i_ref[...] if has_init else jnp.zeros_like(acc_ref)

        acc_ref[...] += _dot_tn(a_ref[...], b_ref[...])

        @pl.when(k == nk - 1)
        def _():
            o_ref[...] = acc_ref[...].astype(out_dtype)

        _host_finish(step, nj * nk - 1, comm, forward_at=(3 * nj * nk) // 4 if gather else None)

    in_specs = [pl.BlockSpec((tk, ka), lambda j, k: (k, 0)), pl.BlockSpec((tk, tn), lambda j, k: (k, j))]
    args = [a, b]
    if has_init:
        in_specs.append(pl.BlockSpec((ka, tn), lambda j, k: (0, j)))
        args.append(init)
    if shard_major:
        out_spec = pl.BlockSpec((None, ka, tn), lambda j, k: (j, 0, 0))
        out_shape = jax.ShapeDtypeStruct((nj, ka, tn), out_dtype)
    else:
        out_spec = pl.BlockSpec((ka, tn), lambda j, k: (0, j))
        out_shape = jax.ShapeDtypeStruct((ka, n), out_dtype)
    out = pl.pallas_call(
        body, name=name, grid=(nj, nk), in_specs=in_specs + _comm_specs(ns), out_specs=[out_spec] + _comm_specs(ns),
        out_shape=[out_shape] + [jax.ShapeDtypeStruct(v.shape, v.dtype) for v in scatter] + _gathered_shapes(gather),
        scratch_shapes=[pltpu.VMEM((ka, tn), F32)] + (_comm_scratch(ns) if ns else []),
        compiler_params=_cp("arbitrary", "arbitrary"),
    )(*args, *moved)
    return (out[0], out[1:]) if ns else out[0]


ADA_TILE = 512


def _ada_fwd(sc_all, w_ada):
    cs = w_ada.shape[1]

    def body(s_ref, w_ref, o_ref):
        o_ref[...] = _dot(s_ref[...].astype(BF16), w_ref[...].astype(BF16))

    return pl.pallas_call(
        body, name="ada_fwd", grid=(cs // ADA_TILE,),
        in_specs=[_acc((16, D)), pl.BlockSpec((D, ADA_TILE), lambda j: (0, j))],
        out_specs=pl.BlockSpec((16, ADA_TILE), lambda j: (0, j)),
        out_shape=jax.ShapeDtypeStruct((16, cs), F32),
        compiler_params=_cp("arbitrary"),
    )(sc_all, w_ada)


def _ada_bwd(sc_all_t, dm_all, dmc, w_ada):
    cs = w_ada.shape[1]

    def body(st_ref, dm_ref, dmc_ref, w_ref, gw_ref, part_ref):
        @pl.when(pl.program_id(0) == 0)
        def _():
            part_ref[...] = jnp.zeros_like(part_ref)

        gw_ref[...] = _dot(st_ref[...].astype(BF16), dm_ref[...].astype(BF16))
        part_ref[...] += _dot_nt(dmc_ref[...].astype(BF16), w_ref[...].astype(BF16))

    return pl.pallas_call(
        body, name="ada_bwd", grid=(cs // ADA_TILE,),
        in_specs=[_acc((D, 16)), pl.BlockSpec((16, ADA_TILE), lambda j: (0, j)), pl.BlockSpec((8, ADA_TILE), lambda j: (0, j)),
                  pl.BlockSpec((D, ADA_TILE), lambda j: (0, j))],
        out_specs=[pl.BlockSpec((D, ADA_TILE), lambda j: (0, j)), _acc((8, D))],
        out_shape=[jax.ShapeDtypeStruct((D, cs), F32), jax.ShapeDtypeStruct((8, D), F32)],
        compiler_params=_cp("arbitrary"),
    )(sc_all_t, dm_all, dmc, w_ada)


def _sum8(x, name, tr=256):
    _, r, c = x.shape
    tr = min(tr, r)
    while r % tr:
        tr -= 16

    def body(x_ref, o_ref):
        acc = x_ref[0].astype(F32)
        for i in range(1, N_DEV):
            acc = acc + x_ref[i].astype(F32)
        o_ref[...] = acc

    return pl.pallas_call(
        body, name=name, grid=(r // tr,),
        in_specs=[pl.BlockSpec((N_DEV, tr, c), lambda i: (0, i, 0))],
        out_specs=pl.BlockSpec((tr, c), lambda i: (i, 0)),
        out_shape=jax.ShapeDtypeStruct((r, c), F32),
        compiler_params=_cp("arbitrary"),
    )(x)


def _sum_blocks(recv, src, me, name, tr=256):
    _, r, c = recv.shape
    tr = min(tr, r)
    while r % tr:
        tr -= 16

    def body(me_ref, recv_ref, own_ref, o_ref):
        acc = own_ref[...].astype(F32)
        for k in range(1, N_DEV):
            acc = acc + recv_ref[me_ref[0] ^ k].astype(F32)
        o_ref[...] = acc

    return pl.pallas_call(
        body, name=name,
        grid_spec=pltpu.PrefetchScalarGridSpec(
            num_scalar_prefetch=1, grid=(r // tr,),
            in_specs=[pl.BlockSpec((N_DEV, tr, c), lambda i, me_ref: (0, i, 0)),
                      pl.BlockSpec((None, tr, c), lambda i, me_ref: (me_ref[0], i, 0))],
            out_specs=pl.BlockSpec((tr, c), lambda i, me_ref: (i, 0))),
        out_shape=jax.ShapeDtypeStruct((r, c), F32),
        compiler_params=_cp("arbitrary"),
    )(me, recv, src)


def _sum8_many(xs, name):
    n = len(xs)

    def body(*refs):
        for x_ref, o_ref in zip(refs[:n], refs[n:]):
            acc = x_ref[0]
            for i in range(1, N_DEV):
                acc = acc + x_ref[i]
            o_ref[...] = acc

    vmem = pl.BlockSpec(memory_space=pltpu.VMEM)
    return pl.pallas_call(
        body, name=name, in_specs=[vmem] * n, out_specs=[vmem] * n,
        out_shape=[jax.ShapeDtypeStruct(v.shape[1:], v.dtype) for v in xs],
        compiler_params=pltpu.CompilerParams(vmem_limit_bytes=VMEM_LIMIT),
    )(*xs)


def _adam_update(w, g, m, v):
    nm = ADAM_B1 * m + (1.0 - ADAM_B1) * g
    nv = ADAM_B2 * v + (1.0 - ADAM_B2) * (g * g)
    m_hat = nm / (1.0 - ADAM_B1 ** ADAM_STEP)
    v_hat = nv / (1.0 - ADAM_B2 ** ADAM_STEP)
    return -ADAM_LR * (m_hat / (jnp.sqrt(v_hat) + ADAM_EPS) + ADAM_WD * w), nm, nv


ROW_LOSS, ROW_LN2_G, ROW_LN2_B, ROW_LN1_G, ROW_LN1_B = 0, 1, 2, 10, 11
ROWS_DMOD_X = (16, 17, 12, 9, 8, 3)
ROWS_DMOD_C = (24, 25)
SMALL = ("c_ctx", "b_ada", "attn_sink", "gmlp_ln_g", "gmlp_ln_b", "w_spatial", "b_spatial", "ln1_g", "ln1_b", "ln2_g", "ln2_b")


def _adamw_small(sums, dsc, w, m, v):
    n = len(SMALL)

    def body(*refs):
        st_ref, gm_ref, sk_ref, ws_ref, bs_ref, dsc_ref = refs[:6]
        w_refs = dict(zip(SMALL, refs[6:6 + n]))
        m_refs = dict(zip(SMALL, refs[6 + n:6 + 2 * n]))
        v_refs = dict(zip(SMALL, refs[6 + 2 * n:6 + 3 * n]))
        outs = refs[6 + 3 * n:]
        c = w_refs["c_ctx"][...]
        sg = _sig(c)
        dmod = [st_ref[r:r + 1, :] for r in ROWS_DMOD_X]
        dmod[0] = dmod[0] + st_ref[ROWS_DMOD_C[0]:ROWS_DMOD_C[0] + 1, :]
        dmod[1] = dmod[1] + st_ref[ROWS_DMOD_C[1]:ROWS_DMOD_C[1] + 1, :]
        grads = dict(
            c_ctx=dsc_ref[0:1, :] * (sg * (1.0 + c * (1.0 - sg))),
            b_ada=jnp.concatenate(dmod, axis=1),
            attn_sink=sk_ref[0:1, 0:N_KV * GROUP],
            gmlp_ln_g=gm_ref[0:1, :], gmlp_ln_b=gm_ref[1:2, :],
            w_spatial=ws_ref[...], b_spatial=bs_ref[...],
            ln1_g=st_ref[ROW_LN1_G:ROW_LN1_G + 1, :], ln1_b=st_ref[ROW_LN1_B:ROW_LN1_B + 1, :],
            ln2_g=st_ref[ROW_LN2_G:ROW_LN2_G + 1, :], ln2_b=st_ref[ROW_LN2_B:ROW_LN2_B + 1, :])
        for i, name in enumerate(SMALL):
            g = grads[name]
            d, nm, nv = _adam_update(w_refs[name][...], g, m_refs[name][...], v_refs[name][...])
            outs[i][...] = g
            outs[n + i][...] = d
            outs[2 * n + i][...] = nm
            outs[3 * n + i][...] = nv

    vmem = pl.BlockSpec(memory_space=pltpu.VMEM)
    args = list(sums) + [dsc] + [w[k] for k in SMALL] + [m[k] for k in SMALL] + [v[k] for k in SMALL]
    shapes = [jax.ShapeDtypeStruct(w[k].shape, F32) for k in SMALL]
    out = pl.pallas_call(
        body, name="adamw_small", in_specs=[vmem] * len(args), out_specs=[vmem] * (4 * n), out_shape=shapes * 4,
        compiler_params=pltpu.CompilerParams(vmem_limit_bytes=VMEM_LIMIT),
    )(*args)
    return [dict(zip(SMALL, out[i * n:(i + 1) * n])) for i in range(4)]


def _adamw_halves(w, mine, theirs, m, v, c_arr, name):
    r, c = w.shape
    tr = min(256, r // 2)
    while (r // 2) % tr:
        tr -= 8
    nt = (r // 2) // tr

    def body(c_ref, w_ref, mine_ref, theirs_ref, m_ref, v_ref, g_ref, d_ref, nm_ref, nv_ref):
        g = jnp.where(pl.program_id(0) == c_ref[0], mine_ref[...], theirs_ref[...])
        g_ref[...] = g
        d_ref[...], nm_ref[...], nv_ref[...] = _adam_update(w_ref[...], g, m_ref[...], v_ref[...])

    whole = pl.BlockSpec((tr, c), lambda hb, i, c_ref: (hb * nt + i, 0))
    mine_spec = pl.BlockSpec((tr, c), lambda hb, i, c_ref: (jnp.where(hb == c_ref[0], i, 0), 0))
    theirs_spec = pl.BlockSpec((tr, c), lambda hb, i, c_ref: (jnp.where(hb == c_ref[0], 0, i), 0))
    shp = jax.ShapeDtypeStruct((r, c), F32)
    return pl.pallas_call(
        body, name=name,
        grid_spec=pltpu.PrefetchScalarGridSpec(
            num_scalar_prefetch=1, grid=(2, nt), in_specs=[whole, mine_spec, theirs_spec, whole, whole],
            out_specs=[whole] * 4),
        out_shape=[shp] * 4,
        compiler_params=_cp("arbitrary", "arbitrary"),
    )(c_arr, w, mine, theirs, m, v)


def _adamw(w, g, m, v, name):
    r, c = w.shape
    tr = r if r * c <= 256 * 1024 else min(256, r)
    while r % tr:
        tr -= 8

    def body(w_ref, g_ref, m_ref, v_ref, d_ref, nm_ref, nv_ref):
        d_ref[...], nm_ref[...], nv_ref[...] = _adam_update(w_ref[...], g_ref[...], m_ref[...], v_ref[...])

    spec = pl.BlockSpec((tr, c), lambda i: (i, 0))
    shp = jax.ShapeDtypeStruct((r, c), F32)
    return pl.pallas_call(
        body, name=name, grid=(r // tr,), in_specs=[spec] * 4, out_specs=[spec] * 3, out_shape=[shp] * 3,
        compiler_params=_cp("arbitrary"),
    )(w, g, m, v)


def _my_pos():
    return lax.axis_index("x"), lax.axis_index("y"), lax.axis_index("c")


N_COPY = 7


class _Gather:
    def __init__(self, x_refs, out_refs, send_sems, recv_sems):
        self.x_refs, self.out_refs = x_refs, out_refs
        self.send_sems, self.recv_sems = send_sems, recv_sems
        x, y, c = _my_pos()
        self.c = c
        self.me, self.sibling = (x, y, c), (x, y, 1 - c)
        self.chips = [(1 - x, y), (x, 1 - y), (1 - x, 1 - y)]

    def _copy(self, a, k, block, to, from_input=False):
        px, py, pc = block
        rows = self.out_refs[a].at[4 * px + 2 * py + pc]
        return pltpu.make_async_remote_copy(
            src_ref=self.x_refs[a] if from_input else rows, dst_ref=rows,
            send_sem=self.send_sems.at[a * N_COPY + k], recv_sem=self.recv_sems.at[a * N_COPY + k],
            device_id=to, device_id_type=MESH)

    def start(self):
        n = len(self.x_refs)
        for a in range(n):
            self._copy(a, 0, self.me, self.sibling, from_input=True).start()
        for j, chip in enumerate(self.chips):
            for a in range(n):
                self._copy(a, 1 + j, self.me, (*chip, self.c), from_input=True).start()

    def forward(self):
        c = self.c
        for j, chip in enumerate(self.chips):
            for a in range(len(self.x_refs)):
                self._copy(a, 1 + j, (*chip, c), self.me).wait_recv()
                self._copy(a, 4 + j, (*chip, c), self.sibling).start()

    def finish(self):
        self.forward()
        self.drain()

    def drain(self):
        n = len(self.x_refs)
        c = self.c
        for a in range(n):
            self._copy(a, 0, self.sibling, self.me).wait_recv()
        for j, chip in enumerate(self.chips):
            for a in range(n):
                self._copy(a, 4 + j, (*chip, 1 - c), self.me).wait_recv()
        for a in range(n):
            self._copy(a, 0, self.me, self.sibling, from_input=True).wait_send()
            for j, chip in enumerate(self.chips):
                self._copy(a, 1 + j, self.me, (*chip, c), from_input=True).wait_send()
                self._copy(a, 4 + j, (*chip, c), self.sibling).wait_send()


def _comm_scratch(n):
    return [pltpu.SemaphoreType.DMA((n * N_COPY,)), pltpu.SemaphoreType.DMA((n * N_COPY,))]


def _comm_specs(n):
    return [pl.BlockSpec(memory_space=pl.ANY)] * n


def _gathered_shapes(xs):
    return [jax.ShapeDtypeStruct((N_DEV,) + v.shape, v.dtype) for v in xs]


def _with_own(gathered, xs, me):
    return [lax.dynamic_update_index_in_dim(g, v, me, 0) for g, v in zip(gathered, xs)]


def _all_gather(xs, me, name):
    n = len(xs)

    def body(*refs):
        g = _Gather(refs[:n], refs[n:2 * n], *refs[2 * n:])
        g.start()
        g.finish()

    out = pl.pallas_call(
        body, name=name, out_shape=_gathered_shapes(xs), in_specs=_comm_specs(n), out_specs=_comm_specs(n),
        scratch_shapes=_comm_scratch(n),
    )(*xs)
    return _with_own(out, xs, me)


class _AllToAll:
    def __init__(self, x_refs, out_refs, send_sems, recv_sems):
        self.x_refs, self.out_refs = x_refs, out_refs
        self.send_sems, self.recv_sems = send_sems, recv_sems
        self.pos = _my_pos()
        x, y, c = self.pos
        self.me = 4 * x + 2 * y + c

    def _peer(self, k):
        x, y, c = self.pos
        return (x ^ ((k >> 2) & 1), y ^ ((k >> 1) & 1), c ^ (k & 1))

    def _copy(self, a, k):
        p = self._peer(k)
        return pltpu.make_async_remote_copy(
            src_ref=self.x_refs[a].at[4 * p[0] + 2 * p[1] + p[2]], dst_ref=self.out_refs[a].at[self.me],
            send_sem=self.send_sems.at[a * N_COPY + k - 1], recv_sem=self.recv_sems.at[a * N_COPY + k - 1],
            device_id=p, device_id_type=MESH)

    def start(self):
        for k in range(1, N_DEV):
            for a in range(len(self.x_refs)):
                self._copy(a, k).start()

    def finish(self):
        for a in range(len(self.x_refs)):
            for k in range(1, N_DEV):
                self._copy(a, k).wait_recv()
            for k in range(1, N_DEV):
                self._copy(a, k).wait_send()


def _all_to_all(blocks, name):
    n = len(blocks)

    def body(*refs):
        t = _AllToAll(refs[:n], refs[n:2 * n], *refs[2 * n:])
        t.start()
        t.finish()

    return pl.pallas_call(
        body, name=name, out_shape=[jax.ShapeDtypeStruct(v.shape, v.dtype) for v in blocks],
        in_specs=_comm_specs(n), out_specs=_comm_specs(n), scratch_shapes=_comm_scratch(n),
    )(*blocks)


def _sibling_exchange(xs, name):
    n = len(xs)

    def body(*refs):
        x_refs, out_refs = refs[:n], refs[n:2 * n]
        send_sems, recv_sems = refs[2 * n:]
        x, y, c = _my_pos()

        def push(a):
            return pltpu.make_async_remote_copy(
                src_ref=x_refs[a], dst_ref=out_refs[a], send_sem=send_sems.at[a], recv_sem=recv_sems.at[a],
                device_id=(x, y, 1 - c), device_id_type=MESH)

        for a in range(n):
            push(a).start()
        for a in range(n):
            push(a).wait_recv()
            push(a).wait_send()

    return pl.pallas_call(
        body, name=name, out_shape=[jax.ShapeDtypeStruct(v.shape, v.dtype) for v in xs],
        in_specs=_comm_specs(n), out_specs=_comm_specs(n),
        scratch_shapes=[pltpu.SemaphoreType.DMA((n,)), pltpu.SemaphoreType.DMA((n,))],
    )(*xs)


def _scatter_and_gather(scatter, gather, name):
    ns, ng = len(scatter), len(gather)

    def body(*refs):
        s_in, g_in = refs[:ns], refs[ns:ns + ng]
        s_out, g_out = refs[ns + ng:2 * ns + ng], refs[2 * ns + ng:2 * (ns + ng)]
        s_send, s_recv, g_send, g_recv = refs[2 * (ns + ng):]
        g = _Gather(g_in, g_out, g_send, g_recv)
        t = _AllToAll(s_in, s_out, s_send, s_recv)
        g.start()
        t.start()
        g.finish()
        t.finish()

    out = pl.pallas_call(
        body, name=name,
        out_shape=[jax.ShapeDtypeStruct(v.shape, v.dtype) for v in scatter] + _gathered_shapes(gather),
        in_specs=_comm_specs(ns + ng), out_specs=_comm_specs(ns + ng),
        scratch_shapes=_comm_scratch(ns) + _comm_scratch(ng),
    )(*scatter, *gather)
    return out[:ns], out[ns:]


def _row_tile(seq, want):
    return min(want, seq)


def _local_step(x, ctx, tgt, mod_x, mod_c, wb, sink, gmlp_g, gmlp_b, w_s, b_s, ln1_g, ln1_b, ln2_g, ln2_b,
                later=None, me=None):
    seq = x.shape[0]
    on_mesh = me is not None
    modx1 = jnp.concatenate([mod_x[0:2], jnp.zeros((6, D), F32)], axis=0)
    modc = jnp.concatenate([mod_c[0:2], jnp.zeros((6, D), F32)], axis=0)
    vec = jnp.concatenate([mod_x[2:3], ln1_g, ln1_b, mod_x[3:6], ln2_g, ln2_b], axis=0)
    gp = jnp.concatenate([gmlp_g, gmlp_b, jnp.zeros((6, G_W), F32)], axis=0)
    ws_stack = w_s.reshape(N_GRP * BLK, BLK).astype(BF16)
    ws_stack_t = jnp.transpose(w_s, (2, 0, 1)).reshape(BLK, N_GRP * BLK).astype(BF16)
    bias_full = jnp.repeat(b_s.T, GRP_D, axis=1)
    cos, sin = _rope_tables(seq)
    w_in = wb["w_in"]
    w_kv = w_in[Q_W:Q_W + 2 * KV_W, :]
    tm_big = _row_tile(seq, 512)
    tm_ffn = _row_tile(seq, 256)

    hc, kvc, vac = _ctx_fwd(ctx, modc, w_kv)
    behind_proj = ("w_a", "w_b", "w_o") if on_mesh else ()
    behind_attn = ("w_fi",) if on_mesh else ()
    behind_mix = ("w_fo",) if on_mesh else ()
    wb = dict(wb)

    def whole(names, gathered):
        for n, g in zip(names, _with_own(list(gathered), [later[n] for n in names], me)):
            wb[n] = g.reshape(-1, g.shape[2]) if n in ROW_SHARDED else g.reshape(N_SHARD, 2 * g.shape[1], g.shape[2])

    (h, q, kv, va, uv, gab), got = _proj_fwd(x, modx1, w_in, cos, sin, _row_tile(seq, 1024), gather=[later[n] for n in behind_proj])
    whole(behind_proj, got)
    if on_mesh:
        for n in ("w_a", "w_b"):
            wb[n] = wb[n].transpose(1, 0, 2).reshape(wb[n].shape[1], D)
    (ya, lse), got = _attn_fwd(q, kv, va, kvc, vac, sink, gather=[later[n] for n in behind_attn])
    whole(behind_attn, got)
    (a, b, mix, merged, yb), got = _mix_fwd(uv, gab, ya, gp, ws_stack, bias_full, wb["w_a"], wb["w_b"], wb["w_o"], tm_big,
                                            gather=[later[n] for n in behind_mix])
    whole(behind_mix, got)
    act, h2, dff, df, dr1, st_ffn = _ffn(x, mix, tgt, vec, wb["w_fi"], wb["w_fo"], tm_ffn)
    blocks, recv = {}, {}
    blocks["w_fo"] = _eighths(_tn_matmul(act, df, 512, "tn_w_ffn_out", BF16, tk=2048))
    if on_mesh:
        g_w_fi, (recv["w_fo"],) = _tn_matmul(h2, dff, FH_SHARD, "tn_w_ffn_in", BF16, shard_major=True, tk=2048,
                                            scatter=[blocks["w_fo"]])
    else:
        g_w_fi = _tn_matmul(h2, dff, FH_SHARD, "tn_w_ffn_in", BF16, shard_major=True, tk=2048)
    blocks["w_fi"] = _eighths(g_w_fi)
    (dya, dpb, dws, dbs_full, st4, g_w_o, g_w_a, g_w_b), got = _mix_bwd(
        dr1, a, b, gab, uv, merged, ya, yb, vec, gp, ws_stack, ws_stack_t, bias_full, wb["w_a"], wb["w_b"], wb["w_o"],
        tm_big, scatter=[blocks["w_fi"]] if on_mesh else ())
    recv.update(zip(("w_fi",), got))
    shard_major = [g.reshape(g.shape[0], N_SHARD, D // N_SHARD).transpose(1, 0, 2) for g in (g_w_a, g_w_b)]
    blocks.update(w_o=_eighths(g_w_o), w_a=_eighths(shard_major[0]), w_b=_eighths(shard_major[1]))
    mixer = ("w_o", "w_a", "w_b") if on_mesh else ()
    (dq, dkv, dkvc, dsink), got = _attn_bwd(q, kv, kvc, sink, dya, ya, lse, scatter=[blocks[n] for n in mixer])
    recv.update(zip(mixer, got))
    g_wkv_ctx, st0 = _ctx_bwd(dkvc, ctx, hc, w_kv)
    (dqkv, grad_x, st1), _ = _proj_bwd(dq, dkv, dpb, x, dr1, modx1, w_in, cos, sin, tm_big)
    dbs = jnp.sum(dbs_full.reshape(BLK, N_GRP, GRP_D), axis=2).T
    early = [jnp.concatenate([st_ffn, st0], axis=0), st4, dsink, dws, dbs]
    init = jnp.pad(g_wkv_ctx, ((Q_W, 0), (0, 0)))
    g_qkv = _tn_matmul(dqkv, h, D, "tn_w_in_qkv", BF16, init=init, tk=1024)
    if on_mesh:
        g_rest, early_gathered = _tn_matmul(dpb, h, D, "tn_w_in_rest", BF16, tk=1024, gather=early)
    else:
        g_rest, early_gathered = _tn_matmul(dpb, h, D, "tn_w_in_rest", BF16, tk=1024), None
    blocks["w_in"] = _eighths(jnp.concatenate([g_qkv, g_rest], axis=0))
    return grad_x, dict(early=early, early_gathered=early_gathered, late=st1), blocks, recv


BIG = ("w_in", "w_a", "w_b", "w_o", "w_fi", "w_fo")
ROW_SHARDED = ("w_o", "w_fo")


def _half_of_shard(shard, c):
    r = shard.shape[0]
    return lax.dynamic_slice_in_dim(shard, c * (r // 2), r // 2, axis=0)


def _eighths(v):
    rows = v.shape[-2] * (v.shape[0] if v.ndim == 3 else 1)
    return v.reshape(N_DEV, rows // N_DEV, v.shape[-1])


def kernel(x, c, ctx, c_ctx, w_ada, b_ada, w_in, attn_sink, gmlp_ln_g, gmlp_ln_b, w_spatial, b_spatial, w_branch_a, w_branch_b, w_out, ln1_g, ln1_b, w_ffn_in, w_ffn_out, ln2_g, ln2_b, loss_target, m_c_ctx, m_w_ada, m_b_ada, m_w_in, m_attn_sink, m_gmlp_ln_g, m_gmlp_ln_b, m_w_spatial, m_b_spatial, m_w_branch_a, m_w_branch_b, m_w_out, m_ln1_g, m_ln1_b, m_w_ffn_in, m_w_ffn_out, m_ln2_g, m_ln2_b, v_c_ctx, v_w_ada, v_b_ada, v_w_in, v_attn_sink, v_gmlp_ln_g, v_gmlp_ln_b, v_w_spatial, v_b_spatial, v_w_branch_a, v_w_branch_b, v_w_out, v_ln1_g, v_ln1_b, v_w_ffn_in, v_w_ffn_out, v_ln2_g, v_ln2_b):
    mx, my, mc = _my_pos()
    me = 4 * mx + 2 * my + mc
    chip = 2 * mx + my
    shards = dict(w_in=w_in[0].T, w_a=w_branch_a[0], w_b=w_branch_b[0], w_o=w_out[0], w_fi=w_ffn_in[0], w_fo=w_ffn_out[0])

    halves = {n: _half_of_shard(shards[n], mc).astype(BF16) for n in BIG}
    c_rows = jnp.concatenate([c, jnp.zeros((7, D), F32)], axis=0)
    g_in, c_g = _all_gather([halves["w_in"], c_rows], me, "gather_w_in")
    wb = dict(w_in=g_in.reshape(IN_W, D))

    c_all = c_g[:, 0, :]
    cc = jnp.concatenate([c_all, c_ctx[None, :], jnp.zeros((7, D), F32)], axis=0)
    sig_cc = jax.nn.sigmoid(cc)
    sc_all = cc * sig_cc
    mod_shard = _ada_fwd(sc_all, w_ada[0])
    mod_g = _all_gather([mod_shard], me, "gather_mod")[0]
    mod_all = jnp.concatenate([mod_g[2 * s] for s in range(4)], axis=1) + b_ada
    mod_x = lax.dynamic_slice_in_dim(mod_all, me, 1, axis=0).reshape(6, D)
    mod_c = mod_all[8].reshape(6, D)[0:2]

    grad_x, small, blocks, recv = _local_step(
        x[0], ctx[0], loss_target[0], mod_x, mod_c, wb, attn_sink, gmlp_ln_g, gmlp_ln_b, w_spatial[0], b_spatial[0],
        ln1_g, ln1_b, ln2_g, ln2_b, later=halves, me=me)

    (recv["w_in"],), late = _scatter_and_gather([blocks["w_in"]], [small["late"]], "scatter_w_in_gather_small")
    late = _with_own(late, [small["late"]], me)[0]
    gathered = _with_own(small["early_gathered"], small["early"], me)
    gathered[0] = jnp.concatenate([gathered[0][:, :16], late, gathered[0][:, 16:]], axis=1)

    me_arr = jnp.reshape(me, (1,)).astype(jnp.int32)
    summed = {n: _sum_blocks(recv[n], blocks[n], me_arr, "sum_grads_" + n) for n in BIG}
    theirs = dict(zip(BIG, _sibling_exchange([summed[n] for n in BIG], "exchange_grads")))

    sums = _sum8_many(gathered, "sum_small")
    stats = sums[0]
    loss = 0.5 * jnp.sum(stats[ROW_LOSS]) / D
    dmod_x_all = jnp.concatenate([gathered[0][:, r_, :] for r_ in ROWS_DMOD_X], axis=1)
    dmod_c_full = jnp.concatenate([stats[r_] for r_ in ROWS_DMOD_C] + [jnp.zeros((4 * D,), F32)])
    dm_rows = jnp.concatenate([dmod_x_all, dmod_c_full[None, :], jnp.zeros((7, 6 * D), F32)], axis=0)
    cs = w_ada.shape[2]
    dm_shard = lax.dynamic_slice_in_dim(dm_rows, chip * cs, cs, axis=1)
    dmc_shard = jnp.concatenate([dm_shard[8:9], jnp.zeros((7, cs), F32)], axis=0)
    g_w_ada, part = _ada_bwd(sc_all.T, dm_shard, dmc_shard, w_ada[0])
    part_all = _all_gather([part * (mc == 0).astype(F32)], me, "gather_c_ctx")[0]
    dsc = _sum8(part_all, "sum_c_ctx")

    grads = dict(w_ada=g_w_ada[None])
    weights = dict(c_ctx=c_ctx, w_ada=w_ada, b_ada=b_ada, w_in=w_in, attn_sink=attn_sink, gmlp_ln_g=gmlp_ln_g,
                   gmlp_ln_b=gmlp_ln_b, w_spatial=w_spatial, b_spatial=b_spatial, w_branch_a=w_branch_a,
                   w_branch_b=w_branch_b, w_out=w_out, ln1_g=ln1_g, ln1_b=ln1_b, w_ffn_in=w_ffn_in, w_ffn_out=w_ffn_out,
                   ln2_g=ln2_g, ln2_b=ln2_b)
    ms = dict(c_ctx=m_c_ctx, w_ada=m_w_ada, b_ada=m_b_ada, w_in=m_w_in, attn_sink=m_attn_sink, gmlp_ln_g=m_gmlp_ln_g,
              gmlp_ln_b=m_gmlp_ln_b, w_spatial=m_w_spatial, b_spatial=m_b_spatial, w_branch_a=m_w_branch_a,
              w_branch_b=m_w_branch_b, w_out=m_w_out, ln1_g=m_ln1_g, ln1_b=m_ln1_b, w_ffn_in=m_w_ffn_in,
              w_ffn_out=m_w_ffn_out, ln2_g=m_ln2_g, ln2_b=m_ln2_b)
    vs = dict(c_ctx=v_c_ctx, w_ada=v_w_ada, b_ada=v_b_ada, w_in=v_w_in, attn_sink=v_attn_sink, gmlp_ln_g=v_gmlp_ln_g,
              gmlp_ln_b=v_gmlp_ln_b, w_spatial=v_w_spatial, b_spatial=v_b_spatial, w_branch_a=v_w_branch_a,
              w_branch_b=v_w_branch_b, w_out=v_w_out, ln1_g=v_ln1_g, ln1_b=v_ln1_b, w_ffn_in=v_w_ffn_in,
              w_ffn_out=v_w_ffn_out, ln2_g=v_ln2_g, ln2_b=v_ln2_b)
    order = list(weights)
    delta, new_m, new_v = {}, {}, {}
    d_, m_, v_ = _adamw(w_ada[0], g_w_ada, m_w_ada[0], v_w_ada[0], "adamw_w_ada")
    delta["w_ada"], new_m["w_ada"], new_v["w_ada"] = d_[None], m_[None], v_[None]
    c_arr = jnp.reshape(mc, (1,)).astype(jnp.int32)
    names = dict(w_in="w_in", w_a="w_branch_a", w_b="w_branch_b", w_o="w_out", w_fi="w_ffn_in", w_fo="w_ffn_out")
    for k, n in names.items():
        flip = (lambda t: t.T) if k == "w_in" else (lambda t: t)
        outs = _adamw_halves(flip(weights[n][0]), summed[k], theirs[k], flip(ms[n][0]), flip(vs[n][0]), c_arr, "adamw_" + n)
        grads[n], delta[n], new_m[n], new_v[n] = [flip(t)[None] for t in outs]

    def view(a):
        return a.reshape(-1, a.shape[-1]) if a.ndim != 1 else a.reshape(1, -1)

    small = _adamw_small(sums, dsc, *[{n: view(d[n]) for n in SMALL} for d in (weights, ms, vs)])
    for out, src in zip((grads, delta, new_m, new_v), small):
        for n in SMALL:
            out[n] = src[n].reshape(weights[n].shape)

    return (loss, grad_x[None], *[grads[n] for n in order], *[delta[n] for n in order],
            *[new_m[n] for n in order], *[new_v[n] for n in order])
```

```python
import math

import jax
import jax.numpy as jnp
from jax import lax
from jax.experimental import pallas as pl
from jax.experimental.pallas import tpu as pltpu

F32 = jnp.float32
BF16 = jnp.bfloat16

D = 1024
HEAD = 64
N_KV = 2
GROUP = 4
Q_W = 512
KV_W = 128
G_W = 512
BLK = 128
N_GRP = 8
GRP_D = 64
FH = 2816
IN_W = 3840
GRID_W = 64
ROPE_BASE = 10000.0
LN_EPS = 1e-5
NEG = -1e30
ALPHA = (2 * 1) ** 0.25
SCALE = HEAD ** -0.5
GELU_K = math.sqrt(2.0 / math.pi)
GELU_A = 0.044715
ADAM_LR = 0.001
ADAM_B1 = 0.9
ADAM_B2 = 0.999
ADAM_EPS = 1e-08
ADAM_WD = 0.01
ADAM_STEP = 10
N_DEV = 8
N_SHARD = 4
FH_SHARD = FH // 2
LANES = 128
VMEM_LIMIT = 56 * 1024 * 1024
MESH = pl.DeviceIdType.MESH


def _cp(*sem):
    return pltpu.CompilerParams(dimension_semantics=sem, vmem_limit_bytes=VMEM_LIMIT)


def _resident(shape):
    return pl.BlockSpec(shape, lambda *_: (0,) * len(shape), pipeline_mode=pl.Buffered(1))


def _rows(tm, width):
    return pl.BlockSpec((tm, width), lambda i: (i, 0))


def _acc(shape):
    return pl.BlockSpec(shape, lambda *_: (0,) * len(shape))


def _dot(a, b):
    return jnp.dot(a, b, preferred_element_type=F32)


def _dot_nt(a, b):
    return lax.dot_general(a, b, (((1,), (1,)), ((), ())), preferred_element_type=F32)


def _dot_tn(a, b):
    return lax.dot_general(a, b, (((0,), (0,)), ((), ())), preferred_element_type=F32)


def _ln(x):
    mu = jnp.mean(x, axis=-1, keepdims=True)
    xc = x - mu
    var = jnp.mean(xc * xc, axis=-1, keepdims=True)
    rstd = lax.rsqrt(var + LN_EPS)
    return xc * rstd, rstd


def _ln_bwd(dxhat, xhat, rstd):
    return (dxhat - jnp.mean(dxhat, axis=-1, keepdims=True)
            - xhat * jnp.mean(dxhat * xhat, axis=-1, keepdims=True)) * rstd


def _sig(x):
    return 0.5 + 0.5 * jnp.tanh(0.5 * x)


def _gelu(x):
    t = jnp.tanh(x * (GELU_K + (GELU_K * GELU_A) * (x * x)))
    hx = 0.5 * x
    return hx + hx * t, t


def _gelu_grad(x, t):
    return 0.5 + 0.5 * t + (0.5 * x) * (1.0 - t * t) * (GELU_K + (3.0 * GELU_K * GELU_A) * (x * x))


def _colsum(v):
    return jnp.sum(v, axis=0, keepdims=True)


def _partner(x):
    w = x.shape[1]
    lane = lax.broadcasted_iota(jnp.int32, x.shape, 1)
    return jnp.where((lane & 31) < 16, pltpu.roll(x, w - 16, 1), pltpu.roll(x, 16, 1))


def _rope(x, cos, sin):
    return x * cos + _partner(x) * sin


def _unrope(g, cos, sin):
    return g * cos + _partner(g * sin)


def _rope_tables(seq):
    inv = ROPE_BASE ** (-jnp.arange(HEAD // 4, dtype=F32) / (HEAD // 4))
    pos = jnp.arange(seq, dtype=jnp.int32)
    ar = (pos // GRID_W).astype(F32)[:, None] * inv
    ac = (pos % GRID_W).astype(F32)[:, None] * inv
    cos = jnp.concatenate([jnp.cos(ar), jnp.cos(ar), jnp.cos(ac), jnp.cos(ac)], axis=-1)
    sin = jnp.concatenate([-jnp.sin(ar), jnp.sin(ar), -jnp.sin(ac), jnp.sin(ac)], axis=-1)
    return jnp.tile(cos, (1, LANES // HEAD)), jnp.tile(sin, (1, LANES // HEAD))


def _ctx_fwd(ctx, modc, w_kv):
    n_ctx = ctx.shape[0]

    def body(ctx_ref, mod_ref, w_ref, hc_ref, kvc_ref, vac_ref):
        xhat, _ = _ln(ctx_ref[...])
        hc = (xhat * (1.0 + mod_ref[1:2, :]) + mod_ref[0:1, :]).astype(BF16)
        hc_ref[...] = hc
        kvc = _dot_nt(hc, w_ref[...]).astype(BF16)
        kvc_ref[...] = kvc
        vac_ref[...] = _with_ones(kvc[:, KV_W:])

    return pl.pallas_call(
        body, name="ctx_fwd", grid=(1,),
        in_specs=[_acc((n_ctx, D)), _acc((8, D)), _acc((2 * KV_W, D))],
        out_specs=[_acc((n_ctx, D)), _acc((n_ctx, 2 * KV_W)), _acc((n_ctx, 2 * LANES))],
        out_shape=[jax.ShapeDtypeStruct((n_ctx, D), BF16), jax.ShapeDtypeStruct((n_ctx, 2 * KV_W), BF16),
                   jax.ShapeDtypeStruct((n_ctx, 2 * LANES), BF16)],
        compiler_params=_cp("arbitrary"),
    )(ctx, modc, w_kv)


def _host_start(step, comm):
    if comm is not None:
        @pl.when(step == 0)
        def _():
            comm.start()


def _host_finish(step, last, comm, forward_at=None):
    if comm is None:
        return
    if forward_at is None or forward_at >= last:
        @pl.when(step == last)
        def _():
            comm.finish()
    else:
        @pl.when(step == forward_at)
        def _():
            comm.forward()

        @pl.when(step == last)
        def _():
            comm.drain()


def _proj_fwd(x, modx, w_in, cos, sin, tm, gather=()):
    seq = x.shape[0]
    ng = len(gather)

    def body(x_ref, mod_ref, w_ref, cos_ref, sin_ref, *rest):
        h_ref, q_ref, kv_ref, va_ref, uv_ref, gab_ref = rest[ng:ng + 6]
        comm = _Gather(rest[:ng], rest[ng + 6:2 * ng + 6], *rest[2 * ng + 6:]) if ng else None
        _host_start(pl.program_id(0), comm)
        xhat, _ = _ln(x_ref[...])
        h = (xhat * (1.0 + mod_ref[1:2, :]) + mod_ref[0:1, :]).astype(BF16)
        h_ref[...] = h
        cos1, sin1 = cos_ref[...], sin_ref[...]
        cos2 = jnp.concatenate([cos1, cos1], axis=1)
        sin2 = jnp.concatenate([sin1, sin1], axis=1)
        for j in range(Q_W // 256):
            t = _dot_nt(h, w_ref[256 * j:256 * (j + 1), :])
            q_ref[:, 256 * j:256 * (j + 1)] = (_rope(t, cos2, sin2) * SCALE).astype(BF16)
        t = _dot_nt(h, w_ref[Q_W:Q_W + 2 * KV_W, :])
        kv_ref[:, :KV_W] = _rope(t[:, :KV_W], cos1, sin1).astype(BF16)
        v = t[:, KV_W:].astype(BF16)
        kv_ref[:, KV_W:] = v
        va_ref[...] = _with_ones(v)
        o = Q_W + 2 * KV_W
        for j in range(2):
            uv_ref[:, G_W * j:G_W * (j + 1)] = _dot_nt(h, w_ref[o + G_W * j:o + G_W * (j + 1), :]).astype(BF16)
        o += 2 * G_W
        for j in range(4):
            gab_ref[:, 512 * j:512 * (j + 1)] = _dot_nt(h, w_ref[o + 512 * j:o + 512 * (j + 1), :]).astype(BF16)
        _host_finish(pl.program_id(0), seq // tm - 1, comm, forward_at=(seq // tm) // 2)

    out = pl.pallas_call(
        body, name="proj_fwd", grid=(seq // tm,),
        in_specs=[_rows(tm, D), _acc((8, D)), _resident((IN_W, D)), _rows(tm, LANES), _rows(tm, LANES)] + _comm_specs(ng),
        out_specs=[_rows(tm, D), _rows(tm, Q_W), _rows(tm, 2 * KV_W), _rows(tm, 2 * LANES), _rows(tm, 2 * G_W),
                   _rows(tm, 2 * D)] + _comm_specs(ng),
        out_shape=[jax.ShapeDtypeStruct((seq, D), BF16), jax.ShapeDtypeStruct((seq, Q_W), BF16),
                   jax.ShapeDtypeStruct((seq, 2 * KV_W), BF16), jax.ShapeDtypeStruct((seq, 2 * LANES), BF16),
                   jax.ShapeDtypeStruct((seq, 2 * G_W), BF16), jax.ShapeDtypeStruct((seq, 2 * D), BF16)] + _gathered_shapes(gather),
        scratch_shapes=_comm_scratch(ng) if ng else [],
        compiler_params=_cp("arbitrary"),
    )(x, modx, w_in, cos, sin, *gather)
    return out[:6], out[6:]


def _stack_heads(x, hk):
    return jnp.concatenate([x[:, (hk * GROUP + g) * HEAD:(hk * GROUP + g + 1) * HEAD] for g in range(GROUP)], axis=0)


def _band_masks(n, nb):
    rows = GROUP * BLK
    qi = lax.broadcasted_iota(jnp.int32, (rows, BLK), 0) & (BLK - 1)
    kj = lax.broadcasted_iota(jnp.int32, (rows, BLK), 1)
    return (kj >= qi) & (n > 0), (kj <= qi) & (n < nb - 1)


def _attn_scores(q, k_refs, hk, masks):
    q4 = _stack_heads(q, hk)
    ks = [r[:, hk * HEAD:(hk + 1) * HEAD] for r in k_refs]
    s = [_dot_nt(q4, k) for k in ks]
    s[1] = jnp.where(masks[0], s[1], NEG)
    s[3] = jnp.where(masks[1], s[3], NEG)
    return q4, ks, s


def _sink_rows(sink_ref, hk):
    rows = GROUP * BLK
    rg = lax.broadcasted_iota(jnp.int32, (rows, 1), 0) >> 7
    sink_v = jnp.full((rows, 1), sink_ref[0, hk * GROUP], F32)
    for g in range(1, GROUP):
        sink_v = jnp.where(rg == g, sink_ref[0, hk * GROUP + g], sink_v)
    return sink_v


def _with_ones(v):
    ones = jnp.ones((v.shape[0], HEAD), v.dtype)
    return jnp.concatenate([v[:, :HEAD], ones, v[:, HEAD:], ones], axis=1)


def _kv_specs(nb, qb):
    def spec(d):
        return pl.BlockSpec((BLK, 2 * KV_W), lambda n: (jnp.clip(qb * n + d, 0, nb - 1), 0))
    return [spec(d) for d in range(-1, qb + 1)]


def _attn_fwd(q, kv, va, kvc, vac, sink, gather=()):
    seq = q.shape[0]
    nb = seq // BLK
    n_ctx = kvc.shape[0]
    ng = len(gather)
    Q_BLOCKS = 1
    nkv = Q_BLOCKS + 2
    steps = nb // Q_BLOCKS

    def body(q_ref, *rest):
        kv_refs, va_refs = rest[:nkv], rest[nkv:2 * nkv]
        kvc_ref, vac_ref, sink_ref = rest[2 * nkv:2 * nkv + 3]
        rest = rest[2 * nkv + 3:]
        o_ref, lse_ref = rest[ng:ng + 2]
        comm = _Gather(rest[:ng], rest[ng + 2:2 * ng + 2], *rest[2 * ng + 2:]) if ng else None
        n = pl.program_id(0)
        _host_start(n, comm)
        lane = lax.broadcasted_iota(jnp.int32, (BLK, LANES), 1)
        for sub in range(Q_BLOCKS):
            rs = slice(sub * BLK, (sub + 1) * BLK)
            q = q_ref[rs, :]
            outs = []
            lse_all = jnp.zeros((BLK, LANES), F32)
            masks = _band_masks(Q_BLOCKS * n + sub, nb)
            for hk in range(N_KV):
                _, _, s = _attn_scores(q, (kvc_ref,) + kv_refs[sub:sub + 3], hk, masks)
                sink_v = _sink_rows(sink_ref, hk)
                tile_max = s[1]
                for t in [s[0][:, i * LANES:(i + 1) * LANES] for i in range(n_ctx // LANES)] + s[2:]:
                    tile_max = jnp.maximum(tile_max, t)
                m = jnp.maximum(sink_v, jnp.max(tile_max, axis=-1, keepdims=True))
                o = jnp.zeros((GROUP * BLK, LANES), F32)
                for t, va_ref in zip(s, (vac_ref,) + va_refs[sub:sub + 3]):
                    o = o + _dot(jnp.exp((t - m).astype(BF16)), va_ref[:, hk * LANES:(hk + 1) * LANES])
                denom = o[:, HEAD:HEAD + 1] + jnp.exp(sink_v - m)
                o4 = o[:, :HEAD] * (1.0 / denom)
                lse4 = m + jnp.log(denom)
                for g in range(GROUP):
                    outs.append(o4[g * BLK:(g + 1) * BLK, :])
                    lse_all = jnp.where(lane == hk * GROUP + g, lse4[g * BLK:(g + 1) * BLK, :], lse_all)
            o_ref[rs, :] = jnp.concatenate(outs, axis=1).astype(BF16)
            lse_ref[rs, :] = lse_all
        _host_finish(n, steps - 1, comm, forward_at=(3 * steps) // 4)

    tq = Q_BLOCKS * BLK
    out = pl.pallas_call(
        body, name="attn_fwd", grid=(steps,),
        in_specs=[_rows(tq, Q_W)] + _kv_specs(nb, Q_BLOCKS) + _kv_specs(nb, Q_BLOCKS)
        + [_acc((n_ctx, 2 * KV_W)), _acc((n_ctx, 2 * LANES)), pl.BlockSpec(memory_space=pltpu.SMEM)] + _comm_specs(ng),
        out_specs=[_rows(tq, Q_W), _rows(tq, LANES)] + _comm_specs(ng),
        out_shape=[jax.ShapeDtypeStruct((seq, Q_W), BF16), jax.ShapeDtypeStruct((seq, LANES), F32)] + _gathered_shapes(gather),
        scratch_shapes=_comm_scratch(ng) if ng else [],
        compiler_params=_cp("arbitrary"),
    )(q, *([kv] * nkv), *([va] * nkv), kvc, vac, sink, *gather)
    return out[:2], out[2:]


def _gmlp_chunk(u, vb, gp_ref, ws_ref, bias_ref):
    gu, tu = _gelu(u)
    gv, tv = _gelu(vb)
    vhat, rstd = _ln(gv)
    vn = (vhat * gp_ref[0:1, :] + gp_ref[1:2, :]).astype(BF16)
    s = bias_ref[...] + jnp.concatenate(
        [_dot(ws_ref[g * BLK:(g + 1) * BLK, :], vn[:, g * GRP_D:(g + 1) * GRP_D]) for g in range(N_GRP)], axis=1)
    return gu, tu, tv, vhat, rstd, vn, s


def _mix_fwd(uv, gab, ya, gp, ws_stack, bias_full, w_a, w_b, w_o, tm, gather=()):
    seq = uv.shape[0]
    ng = len(gather)
    steps = seq // tm

    def body(uv_ref, gab_ref, ya_ref, gp_ref, ws_ref, bias_ref, wa_ref, wb_ref, wo_ref, *rest):
        a_ref, b_ref, mix_ref, merged_ref, yb_ref = rest[ng:ng + 5]
        comm = _Gather(rest[:ng], rest[ng + 5:2 * ng + 5], *rest[2 * ng + 5:]) if ng else None
        _host_start(pl.program_id(0), comm)
        for c in range(tm // BLK):
            rs = slice(c * BLK, (c + 1) * BLK)
            gu, _, _, _, _, _, s = _gmlp_chunk(uv_ref[rs, :G_W].astype(F32), uv_ref[rs, G_W:].astype(F32), gp_ref, ws_ref, bias_ref)
            yb_ref[rs, :] = (gu * s).astype(BF16)
        a = _dot(ya_ref[...], wa_ref[...])
        b = _dot(yb_ref[...], wb_ref[...])
        a_ref[...] = a.astype(BF16)
        b_ref[...] = b.astype(BF16)
        merged = (_sig(gab_ref[:, :D].astype(F32)) * a + _sig(gab_ref[:, D:].astype(F32)) * b).astype(BF16)
        merged_ref[...] = merged
        mix_ref[...] = _dot(merged, wo_ref[...])
        _host_finish(pl.program_id(0), steps - 1, comm, forward_at=(3 * steps) // 4)

    out = pl.pallas_call(
        body, name="mix_fwd", grid=(steps,),
        in_specs=[_rows(tm, 2 * G_W), _rows(tm, 2 * D), _rows(tm, Q_W), _acc((8, G_W)),
                  _resident((N_GRP * BLK, BLK)), _acc((BLK, G_W)),
                  _resident((Q_W, D)), _resident((G_W, D)), _resident((D, D))] + _comm_specs(ng),
        out_specs=[_rows(tm, D), _rows(tm, D), _rows(tm, D), _rows(tm, D), _rows(tm, G_W)] + _comm_specs(ng),
        out_shape=[jax.ShapeDtypeStruct((seq, D), BF16), jax.ShapeDtypeStruct((seq, D), BF16),
                   jax.ShapeDtypeStruct((seq, D), F32), jax.ShapeDtypeStruct((seq, D), BF16),
                   jax.ShapeDtypeStruct((seq, G_W), BF16)] + _gathered_shapes(gather),
        scratch_shapes=_comm_scratch(ng) if ng else [],
        compiler_params=_cp("arbitrary"),
    )(uv, gab, ya, gp, ws_stack, bias_full, w_a, w_b, w_o, *gather)
    return out[:5], out[5:]


def _mid_recompute(x_ref, mix_ref, vec_ref):
    r1 = ALPHA * x_ref[...] + vec_ref[0:1, :] * mix_ref[...]
    xh1, rstd1 = _ln(r1)
    xmid = xh1 * vec_ref[1:2, :] + vec_ref[2:3, :]
    xh2, rstd2 = _ln(xmid)
    return xh1, rstd1, xmid, xh2, rstd2


def _ffn(x, mix, tgt, vec, w_fi, w_fo, tm):
    seq = x.shape[0]

    def body(x_ref, mix_ref, tgt_ref, vec_ref, wi_ref, wo_ref, act_ref, h2_ref, dff_ref, df_ref, dr1_ref, st_ref, gu_ref):
        @pl.when(pl.program_id(0) == 0)
        def _():
            st_ref[...] = jnp.zeros_like(st_ref)

        xh1, rstd1, xmid, xh2, rstd2 = _mid_recompute(x_ref, mix_ref, vec_ref)
        h2 = (xh2 * (1.0 + vec_ref[4:5, :]) + vec_ref[3:4, :]).astype(BF16)
        h2_ref[...] = h2
        halves = [(slice(hh * FH_SHARD, (hh + 1) * FH_SHARD), slice(FH + hh * FH_SHARD, FH + (hh + 1) * FH_SHARD))
                  for hh in range(2)]
        for hh, (cs, cu) in enumerate(halves):
            g = _dot(h2, wi_ref[hh])
            u = _dot(h2, wi_ref[2 + hh])
            gu_ref[:, cs] = g
            gu_ref[:, cu] = u
            act_ref[:, cs] = (g * _sig(g) * u).astype(BF16)
        f = _dot(act_ref[...], wo_ref[...])
        r2 = ALPHA * xmid + vec_ref[5:6, :] * f
        yh, rstd = _ln(r2)
        y = yh * vec_ref[6:7, :] + vec_ref[7:8, :]
        err = y - tgt_ref[...]
        dy = err / D
        dr2 = _ln_bwd(dy * vec_ref[6:7, :], yh, rstd)
        st_ref[0:1, :] += _colsum(err * err)
        st_ref[1:2, :] += _colsum(dy * yh)
        st_ref[2:3, :] += _colsum(dy)
        st_ref[3:4, :] += _colsum(dr2 * f)

        df = (dr2 * vec_ref[5:6, :]).astype(BF16)
        df_ref[...] = df
        da_all = _dot_nt(df, wo_ref[...])
        for cs, cu in halves:
            da = da_all[:, cs]
            g = gu_ref[:, cs]
            u = gu_ref[:, cu]
            sg = _sig(g)
            dff_ref[:, cs] = (da * u * sg * (1.0 + g * (1.0 - sg))).astype(BF16)
            dff_ref[:, cu] = (da * g * sg).astype(BF16)
        dh2 = _dot_nt(dff_ref[:, :FH_SHARD], wi_ref[0])
        for s in range(1, N_SHARD):
            dh2 = dh2 + _dot_nt(dff_ref[:, s * FH_SHARD:(s + 1) * FH_SHARD], wi_ref[s])
        dxmid = _ln_bwd(dh2 * (1.0 + vec_ref[4:5, :]), xh2, rstd2) + ALPHA * dr2
        dr1 = _ln_bwd(dxmid * vec_ref[1:2, :], xh1, rstd1)
        dr1_ref[...] = dr1
        st_ref[8:9, :] += _colsum(dh2 * xh2)
        st_ref[9:10, :] += _colsum(dh2)
        st_ref[10:11, :] += _colsum(dxmid * xh1)
        st_ref[11:12, :] += _colsum(dxmid)
        st_ref[12:13, :] += _colsum(dr1 * mix_ref[...])

    return pl.pallas_call(
        body, name="ffn", grid=(seq // tm,),
        in_specs=[_rows(tm, D), _rows(tm, D), _rows(tm, D), _acc((8, D)), _resident((N_SHARD, D, FH_SHARD)), _resident((FH, D))],
        out_specs=[_rows(tm, FH), _rows(tm, D), _rows(tm, 2 * FH), _rows(tm, D), _rows(tm, D), _acc((16, D))],
        out_shape=[jax.ShapeDtypeStruct((seq, FH), BF16), jax.ShapeDtypeStruct((seq, D), BF16),
                   jax.ShapeDtypeStruct((seq, 2 * FH), BF16), jax.ShapeDtypeStruct((seq, D), BF16),
                   jax.ShapeDtypeStruct((seq, D), F32), jax.ShapeDtypeStruct((16, D), F32)],
        scratch_shapes=[pltpu.VMEM((tm, 2 * FH), F32)],
        compiler_params=_cp("arbitrary"),
    )(x, mix, tgt, vec, w_fi, w_fo)


def _mix_bwd(dr1, a, b, gab, uv, merged, ya, yb, vec, gp, ws_stack, ws_stack_t, bias_full, w_a, w_b, w_o, tm, scatter=()):
    seq = dr1.shape[0]
    last = seq // tm - 1
    ns = len(scatter)

    def body(dr1_ref, a_ref, b_ref, gab_ref, uv_ref, mg_ref, ya_ref, yb_ref, vec_ref, gp_ref, ws_ref, wst_ref, bias_ref,
             wa_ref, wb_ref, wo_ref, *rest):
        dya_ref, dp_ref, dws_ref, dbs_ref, st_ref, gwo_ref, gwa_ref, gwb_ref = rest[ns:ns + 8]
        acc_o, acc_a, acc_b = rest[2 * ns + 8:2 * ns + 11]
        comm = _AllToAll(rest[:ns], rest[ns + 8:2 * ns + 8], *rest[2 * ns + 11:]) if ns else None
        _host_start(pl.program_id(0), comm)

        @pl.when(pl.program_id(0) == 0)
        def _():
            dws_ref[...] = jnp.zeros_like(dws_ref)
            dbs_ref[...] = jnp.zeros_like(dbs_ref)
            st_ref[...] = jnp.zeros_like(st_ref)
            acc_o[...] = jnp.zeros_like(acc_o)
            acc_a[...] = jnp.zeros_like(acc_a)
            acc_b[...] = jnp.zeros_like(acc_b)

        dmix = (dr1_ref[...] * vec_ref[0:1, :]).astype(BF16)
        acc_o[...] += _dot_tn(mg_ref[...], dmix)
        dmerged = _dot_nt(dmix, wo_ref[...])
        sa = _sig(gab_ref[:, :D].astype(F32))
        sb = _sig(gab_ref[:, D:].astype(F32))
        da_f = dmerged * sa
        db_f = dmerged * sb
        da = da_f.astype(BF16)
        db = db_f.astype(BF16)
        dp_ref[:, 2 * G_W:2 * G_W + D] = (da_f * a_ref[...].astype(F32) * (1.0 - sa)).astype(BF16)
        dp_ref[:, 2 * G_W + D:] = (db_f * b_ref[...].astype(F32) * (1.0 - sb)).astype(BF16)
        dya_ref[...] = _dot_nt(da, wa_ref[...]).astype(BF16)
        dyb = _dot_nt(db, wb_ref[...])
        acc_a[...] += _dot_tn(ya_ref[...], da)
        acc_b[...] += _dot_tn(yb_ref[...], db)

        @pl.when(pl.program_id(0) == last)
        def _():
            gwo_ref[...] = acc_o[...].astype(BF16)
            gwa_ref[...] = acc_a[...].astype(BF16)
            gwb_ref[...] = acc_b[...].astype(BF16)

        for c in range(tm // BLK):
            rs = slice(c * BLK, (c + 1) * BLK)
            u = uv_ref[rs, :G_W].astype(F32)
            vb = uv_ref[rs, G_W:].astype(F32)
            gu, tu, tv, vhat, rstd, vn, s = _gmlp_chunk(u, vb, gp_ref, ws_ref, bias_ref)
            dyb_c = dyb[rs, :]
            ds = dyb_c * gu
            du = dyb_c * s * _gelu_grad(u, tu)
            ds_b = ds.astype(BF16)
            dvn_g = []
            for g in range(N_GRP):
                cg = slice(g * GRP_D, (g + 1) * GRP_D)
                dvn_g.append(_dot(wst_ref[:, g * BLK:(g + 1) * BLK], ds_b[:, cg]))
                dws_ref[g * BLK:(g + 1) * BLK, :] += _dot_nt(ds_b[:, cg], vn[:, cg])
            dvn = jnp.concatenate(dvn_g, axis=1)
            dbs_ref[...] += ds
            st_ref[0:1, :] += _colsum(dvn * vhat)
            st_ref[1:2, :] += _colsum(dvn)
            dgv = _ln_bwd(dvn * gp_ref[0:1, :], vhat, rstd)
            dvb = dgv * _gelu_grad(vb, tv)
            dp_ref[rs, :G_W] = du.astype(BF16)
            dp_ref[rs, G_W:2 * G_W] = dvb.astype(BF16)
        _host_finish(pl.program_id(0), last, comm)

    pw = 2 * G_W + 2 * D
    out = pl.pallas_call(
        body, name="mix_bwd", grid=(seq // tm,),
        in_specs=[_rows(tm, D), _rows(tm, D), _rows(tm, D), _rows(tm, 2 * D), _rows(tm, 2 * G_W), _rows(tm, D), _rows(tm, Q_W),
                  _rows(tm, G_W), _acc((8, D)), _acc((8, G_W)),
                  _resident((N_GRP * BLK, BLK)), _resident((BLK, N_GRP * BLK)), _acc((BLK, G_W)),
                  _resident((Q_W, D)), _resident((G_W, D)), _resident((D, D))] + _comm_specs(ns),
        out_specs=[_rows(tm, Q_W), _rows(tm, pw), _acc((N_GRP * BLK, BLK)), _acc((BLK, G_W)), _acc((8, G_W)),
                   _acc((D, D)), _acc((Q_W, D)), _acc((G_W, D))] + _comm_specs(ns),
        out_shape=[jax.ShapeDtypeStruct((seq, Q_W), BF16), jax.ShapeDtypeStruct((seq, pw), BF16),
                   jax.ShapeDtypeStruct((N_GRP * BLK, BLK), F32), jax.ShapeDtypeStruct((BLK, G_W), F32),
                   jax.ShapeDtypeStruct((8, G_W), F32), jax.ShapeDtypeStruct((D, D), BF16),
                   jax.ShapeDtypeStruct((Q_W, D), BF16), jax.ShapeDtypeStruct((G_W, D), BF16)]
        + [jax.ShapeDtypeStruct(v.shape, v.dtype) for v in scatter],
        scratch_shapes=[pltpu.VMEM((D, D), F32), pltpu.VMEM((Q_W, D), F32), pltpu.VMEM((G_W, D), F32)]
        + (_comm_scratch(ns) if ns else []),
        compiler_params=_cp("arbitrary"),
    )(dr1, a, b, gab, uv, merged, ya, yb, vec, gp, ws_stack, ws_stack_t, bias_full, w_a, w_b, w_o, *scatter)
    return out[:8], out[8:]


def _attn_bwd(q, kv, kvc, sink, dya, ya, lse, scatter=()):
    seq = q.shape[0]
    nb = seq // BLK
    n_ctx = kvc.shape[0]
    ns = len(scatter)
    Q_BLOCKS = 2
    nkv = Q_BLOCKS + 2
    steps = nb // Q_BLOCKS

    def body(q_ref, *rest):
        kv_refs = rest[:nkv]
        kvc_ref, sink_ref, do_ref, o_ref, lse_ref = rest[nkv:nkv + 5]
        rest = rest[nkv + 5:]
        dq_ref, dkv_ref, dkvc_ref, dsink_ref = rest[ns:ns + 4]
        comm = _AllToAll(rest[:ns], rest[ns + 4:2 * ns + 4], *rest[2 * ns + 4:]) if ns else None
        n = pl.program_id(0)
        _host_start(n, comm)

        @pl.when(n == 0)
        def _():
            dkv_ref[...] = jnp.zeros_like(dkv_ref)
            dkvc_ref[...] = jnp.zeros_like(dkvc_ref)
            dsink_ref[...] = jnp.zeros_like(dsink_ref)

        lane = lax.broadcasted_iota(jnp.int32, (1, LANES), 1)
        for sub in range(Q_BLOCKS):
            rs = slice(sub * BLK, (sub + 1) * BLK)
            blk = Q_BLOCKS * n + sub
            q = q_ref[rs, :]
            do = do_ref[rs, :]
            out = o_ref[rs, :]
            lse_all = lse_ref[rs, :]
            k_refs = (kvc_ref,) + kv_refs[sub:sub + 3]
            masks = _band_masks(blk, nb)
            dqs, dks, dvs = [], [], []
            for hk in range(N_KV):
                q4, ks, s = _attn_scores(q, k_refs, hk, masks)
                vs = [r[:, KV_W + hk * HEAD:KV_W + (hk + 1) * HEAD] for r in k_refs]
                lse4 = jnp.concatenate([lse_all[:, hk * GROUP + g:hk * GROUP + g + 1] for g in range(GROUP)], axis=0)
                do4 = _stack_heads(do, hk)
                delta = jnp.sum(do4.astype(F32) * _stack_heads(out, hk).astype(F32), axis=-1, keepdims=True)
                p = [jnp.exp((t - lse4).astype(BF16)) for t in s]
                ds = [t * (_dot_nt(do4, v) - delta).astype(BF16) for t, v in zip(p, vs)]
                dq4 = _dot(ds[0], ks[0])
                for t, k in zip(ds[1:], ks[1:]):
                    dq4 = dq4 + _dot(t, k)
                dq4 = dq4 * SCALE
                dqs += [dq4[g * BLK:(g + 1) * BLK, :] for g in range(GROUP)]
                dks.append([_dot_tn(t, q4) for t in ds])
                dvs.append([_dot_tn(t, do4) for t in p])
                ps = jnp.exp(_sink_rows(sink_ref, hk) - lse4) * delta
                for g in range(GROUP):
                    part = -jnp.sum(ps[g * BLK:(g + 1) * BLK, :], axis=0, keepdims=True)
                    dsink_ref[0:1, :] += jnp.where(lane == hk * GROUP + g, part, 0.0)
            dq_ref[rs, :] = jnp.concatenate(dqs, axis=1)

            def piece(i):
                return jnp.concatenate([dks[0][i], dks[1][i], dvs[0][i], dvs[1][i]], axis=1)

            dkvc_ref[...] += piece(0)
            starts = (jnp.maximum(blk - 1, 0), blk, jnp.minimum(blk + 1, nb - 1))
            for i, st in enumerate(starts):
                r = pl.ds(pl.multiple_of(st * BLK, BLK), BLK)
                dkv_ref[r, :] += piece(i + 1)
        _host_finish(n, steps - 1, comm)

    tq = Q_BLOCKS * BLK
    out = pl.pallas_call(
        body, name="attn_bwd", grid=(steps,),
        in_specs=[_rows(tq, Q_W)] + _kv_specs(nb, Q_BLOCKS) + [_acc((n_ctx, 2 * KV_W)), pl.BlockSpec(memory_space=pltpu.SMEM),
                                                     _rows(tq, Q_W), _rows(tq, Q_W), _rows(tq, LANES)] + _comm_specs(ns),
        out_specs=[_rows(tq, Q_W), _acc((seq, 2 * KV_W)), _acc((n_ctx, 2 * KV_W)), _acc((8, LANES))] + _comm_specs(ns),
        out_shape=[jax.ShapeDtypeStruct((seq, Q_W), F32), jax.ShapeDtypeStruct((seq, 2 * KV_W), F32),
                   jax.ShapeDtypeStruct((n_ctx, 2 * KV_W), F32), jax.ShapeDtypeStruct((8, LANES), F32)]
        + [jax.ShapeDtypeStruct(v.shape, v.dtype) for v in scatter],
        scratch_shapes=_comm_scratch(ns) if ns else [],
        compiler_params=_cp("arbitrary"),
    )(q, *([kv] * nkv), kvc, sink, dya, ya, lse, *scatter)
    return out[:4], out[4:]


def _proj_bwd(dq, dkv, dpb, x, dr1, modx, w_in, cos, sin, tm, scatter=()):
    seq = x.shape[0]
    pw = IN_W - Q_W - 2 * KV_W
    ns = len(scatter)

    def body(dq_ref, dkv_ref, dpb_ref, x_ref, dr1_ref, mod_ref, w_ref, cos_ref, sin_ref, *rest):
        dqkv_ref, gx_ref, st_ref = rest[ns:ns + 3]
        comm = _AllToAll(rest[:ns], rest[ns + 3:2 * ns + 3], *rest[2 * ns + 3:]) if ns else None
        _host_start(pl.program_id(0), comm)

        @pl.when(pl.program_id(0) == 0)
        def _():
            st_ref[...] = jnp.zeros_like(st_ref)

        cos1, sin1 = cos_ref[...], sin_ref[...]
        cos2 = jnp.concatenate([cos1, cos1], axis=1)
        sin2 = jnp.concatenate([sin1, sin1], axis=1)
        for j in range(Q_W // 256):
            cs = slice(256 * j, 256 * (j + 1))
            dqkv_ref[:, cs] = _unrope(dq_ref[:, cs], cos2, sin2).astype(BF16)
        dqkv_ref[:, Q_W:Q_W + KV_W] = _unrope(dkv_ref[:, :KV_W], cos1, sin1).astype(BF16)
        dqkv_ref[:, Q_W + KV_W:] = dkv_ref[:, KV_W:].astype(BF16)
        o = Q_W + 2 * KV_W
        dh = _dot(dqkv_ref[...], w_ref[:o, :]) + _dot(dpb_ref[...], w_ref[o:, :])
        xhat, rstd = _ln(x_ref[...])
        st_ref[0:1, :] += _colsum(dh)
        st_ref[1:2, :] += _colsum(dh * xhat)
        gx_ref[...] = _ln_bwd(dh * (1.0 + mod_ref[1:2, :]), xhat, rstd) + ALPHA * dr1_ref[...]
        _host_finish(pl.program_id(0), seq // tm - 1, comm)

    out = pl.pallas_call(
        body, name="proj_bwd", grid=(seq // tm,),
        in_specs=[_rows(tm, Q_W), _rows(tm, 2 * KV_W), _rows(tm, pw), _rows(tm, D), _rows(tm, D), _acc((8, D)),
                  _resident((IN_W, D)), _rows(tm, LANES), _rows(tm, LANES)] + _comm_specs(ns),
        out_specs=[_rows(tm, Q_W + 2 * KV_W), _rows(tm, D), _acc((8, D))] + _comm_specs(ns),
        out_shape=[jax.ShapeDtypeStruct((seq, Q_W + 2 * KV_W), BF16), jax.ShapeDtypeStruct((seq, D), F32),
                   jax.ShapeDtypeStruct((8, D), F32)] + [jax.ShapeDtypeStruct(v.shape, v.dtype) for v in scatter],
        scratch_shapes=_comm_scratch(ns) if ns else [],
        compiler_params=_cp("arbitrary"),
    )(dq, dkv, dpb, x, dr1, modx, w_in, cos, sin, *scatter)
    return out[:3], out[3:]


def _ctx_bwd(dkvc, ctx, hc, w_kv):
    n_ctx = ctx.shape[0]

    def body(dkvc_ref, ctx_ref, hc_ref, w_ref, dw_ref, st_ref):
        d = dkvc_ref[...].astype(BF16)
        dw_ref[...] = _dot_tn(d, hc_ref[...])
        dhc = _dot(d, w_ref[...])
        xhat, _ = _ln(ctx_ref[...])
        st_ref[...] = jnp.zeros_like(st_ref)
        st_ref[0:1, :] = _colsum(dhc)
        st_ref[1:2, :] = _colsum(dhc * xhat)

    return pl.pallas_call(
        body, name="ctx_bwd", grid=(1,),
        in_specs=[_acc((n_ctx, 2 * KV_W)), _acc((n_ctx, D)), _acc((n_ctx, D)), _acc((2 * KV_W, D))],
        out_specs=[_acc((2 * KV_W, D)), _acc((8, D))],
        out_shape=[jax.ShapeDtypeStruct((2 * KV_W, D), F32), jax.ShapeDtypeStruct((8, D), F32)],
        compiler_params=_cp("arbitrary"),
    )(dkvc, ctx, hc, w_kv)


def _tn_matmul(a, b, tn, name, out_dtype, shard_major=False, init=None, tk=512, scatter=(), gather=()):
    t, ka = a.shape
    n = b.shape[1]
    tk = min(tk, t)
    nk = t // tk
    nj = n // tn
    has_init = init is not None
    assert not (scatter and gather)
    moved = list(scatter) + list(gather)
    pattern = _AllToAll if scatter else _Gather
    ns = len(moved)
    n_in = 3 if has_init else 2

    def body(*refs):
        a_ref, b_ref = refs[:2]
        i_ref = refs[2] if has_init else None
        rest = refs[n_in:]
        o_ref = rest[ns]
        acc_ref = rest[2 * ns + 1]
        comm = pattern(rest[:ns], rest[ns + 1:2 * ns + 1], *rest[2 * ns + 2:]) if ns else None
        k = pl.program_id(1)
        step = pl.program_id(0) * nk + k
        _host_start(step, comm)

        @pl.when(k == 0)
        def _():
            acc_ref[...] = i_ref[...] if has_init else jnp.zeros_like(acc_ref)

        acc_ref[...] += _dot_tn(a_ref[...], b_ref[...])

        @pl.when(k == nk - 1)
        def _():
            o_ref[...] = acc_ref[...].astype(out_dtype)

        _host_finish(step, nj * nk - 1, comm, forward_at=(3 * nj * nk) // 4 if gather else None)

    in_specs = [pl.BlockSpec((tk, ka), lambda j, k: (k, 0)), pl.BlockSpec((tk, tn), lambda j, k: (k, j))]
    args = [a, b]
    if has_init:
        in_specs.append(pl.BlockSpec((ka, tn), lambda j, k: (0, j)))
        args.append(init)
    if shard_major:
        out_spec = pl.BlockSpec((None, ka, tn), lambda j, k: (j, 0, 0))
        out_shape = jax.ShapeDtypeStruct((nj, ka, tn), out_dtype)
    else:
        out_spec = pl.BlockSpec((ka, tn), lambda j, k: (0, j))
        out_shape = jax.ShapeDtypeStruct((ka, n), out_dtype)
    out = pl.pallas_call(
        body, name=name, grid=(nj, nk), in_specs=in_specs + _comm_specs(ns), out_specs=[out_spec] + _comm_specs(ns),
        out_shape=[out_shape] + [jax.ShapeDtypeStruct(v.shape, v.dtype) for v in scatter] + _gathered_shapes(gather),
        scratch_shapes=[pltpu.VMEM((ka, tn), F32)] + (_comm_scratch(ns) if ns else []),
        compiler_params=_cp("arbitrary", "arbitrary"),
    )(*args, *moved)
    return (out[0], out[1:]) if ns else out[0]


ADA_TILE = 512


def _ada_fwd(sc_all, w_ada):
    cs = w_ada.shape[1]

    def body(s_ref, w_ref, o_ref):
        o_ref[...] = _dot(s_ref[...].astype(BF16), w_ref[...].astype(BF16))

    return pl.pallas_call(
        body, name="ada_fwd", grid=(cs // ADA_TILE,),
        in_specs=[_acc((16, D)), pl.BlockSpec((D, ADA_TILE), lambda j: (0, j))],
        out_specs=pl.BlockSpec((16, ADA_TILE), lambda j: (0, j)),
        out_shape=jax.ShapeDtypeStruct((16, cs), F32),
        compiler_params=_cp("arbitrary"),
    )(sc_all, w_ada)


def _ada_bwd(sc_all_t, dm_all, dmc, w_ada):
    cs = w_ada.shape[1]

    def body(st_ref, dm_ref, dmc_ref, w_ref, gw_ref, part_ref):
        @pl.when(pl.program_id(0) == 0)
        def _():
            part_ref[...] = jnp.zeros_like(part_ref)

        gw_ref[...] = _dot(st_ref[...].astype(BF16), dm_ref[...].astype(BF16))
        part_ref[...] += _dot_nt(dmc_ref[...].astype(BF16), w_ref[...].astype(BF16))

    return pl.pallas_call(
        body, name="ada_bwd", grid=(cs // ADA_TILE,),
        in_specs=[_acc((D, 16)), pl.BlockSpec((16, ADA_TILE), lambda j: (0, j)), pl.BlockSpec((8, ADA_TILE), lambda j: (0, j)),
                  pl.BlockSpec((D, ADA_TILE), lambda j: (0, j))],
        out_specs=[pl.BlockSpec((D, ADA_TILE), lambda j: (0, j)), _acc((8, D))],
        out_shape=[jax.ShapeDtypeStruct((D, cs), F32), jax.ShapeDtypeStruct((8, D), F32)],
        compiler_params=_cp("arbitrary"),
    )(sc_all_t, dm_all, dmc, w_ada)


def _sum8(x, name, tr=256):
    _, r, c = x.shape
    tr = min(tr, r)
    while r % tr:
        tr -= 16

    def body(x_ref, o_ref):
        acc = x_ref[0].astype(F32)
        for i in range(1, N_DEV):
            acc = acc + x_ref[i].astype(F32)
        o_ref[...] = acc

    return pl.pallas_call(
        body, name=name, grid=(r // tr,),
        in_specs=[pl.BlockSpec((N_DEV, tr, c), lambda i: (0, i, 0))],
        out_specs=pl.BlockSpec((tr, c), lambda i: (i, 0)),
        out_shape=jax.ShapeDtypeStruct((r, c), F32),
        compiler_params=_cp("arbitrary"),
    )(x)


def _sum_blocks(recv, src, me, name, tr=256):
    _, r, c = recv.shape
    tr = min(tr, r)
    while r % tr:
        tr -= 16

    def body(me_ref, recv_ref, own_ref, o_ref):
        acc = own_ref[...].astype(F32)
        for k in range(1, N_DEV):
            acc = acc + recv_ref[me_ref[0] ^ k].astype(F32)
        o_ref[...] = acc

    return pl.pallas_call(
        body, name=name,
        grid_spec=pltpu.PrefetchScalarGridSpec(
            num_scalar_prefetch=1, grid=(r // tr,),
            in_specs=[pl.BlockSpec((N_DEV, tr, c), lambda i, me_ref: (0, i, 0)),
                      pl.BlockSpec((None, tr, c), lambda i, me_ref: (me_ref[0], i, 0))],
            out_specs=pl.BlockSpec((tr, c), lambda i, me_ref: (i, 0))),
        out_shape=jax.ShapeDtypeStruct((r, c), F32),
        compiler_params=_cp("arbitrary"),
    )(me, recv, src)


def _sum8_many(xs, name):
    n = len(xs)

    def body(*refs):
        for x_ref, o_ref in zip(refs[:n], refs[n:]):
            acc = x_ref[0]
            for i in range(1, N_DEV):
                acc = acc + x_ref[i]
            o_ref[...] = acc

    vmem = pl.BlockSpec(memory_space=pltpu.VMEM)
    return pl.pallas_call(
        body, name=name, in_specs=[vmem] * n, out_specs=[vmem] * n,
        out_shape=[jax.ShapeDtypeStruct(v.shape[1:], v.dtype) for v in xs],
        compiler_params=pltpu.CompilerParams(vmem_limit_bytes=VMEM_LIMIT),
    )(*xs)


def _adam_update(w, g, m, v):
    nm = ADAM_B1 * m + (1.0 - ADAM_B1) * g
    nv = ADAM_B2 * v + (1.0 - ADAM_B2) * (g * g)
    m_hat = nm / (1.0 - ADAM_B1 ** ADAM_STEP)
    v_hat = nv / (1.0 - ADAM_B2 ** ADAM_STEP)
    return -ADAM_LR * (m_hat / (jnp.sqrt(v_hat) + ADAM_EPS) + ADAM_WD * w), nm, nv


ROW_LOSS, ROW_LN2_G, ROW_LN2_B, ROW_LN1_G, ROW_LN1_B = 0, 1, 2, 10, 11
ROWS_DMOD_X = (16, 17, 12, 9, 8, 3)
ROWS_DMOD_C = (24, 25)
SMALL = ("c_ctx", "b_ada", "attn_sink", "gmlp_ln_g", "gmlp_ln_b", "w_spatial", "b_spatial", "ln1_g", "ln1_b", "ln2_g", "ln2_b")


def _adamw_small(sums, dsc, w, m, v):
    n = len(SMALL)

    def body(*refs):
        st_ref, gm_ref, sk_ref, ws_ref, bs_ref, dsc_ref = refs[:6]
        w_refs = dict(zip(SMALL, refs[6:6 + n]))
        m_refs = dict(zip(SMALL, refs[6 + n:6 + 2 * n]))
        v_refs = dict(zip(SMALL, refs[6 + 2 * n:6 + 3 * n]))
        outs = refs[6 + 3 * n:]
        c = w_refs["c_ctx"][...]
        sg = _sig(c)
        dmod = [st_ref[r:r + 1, :] for r in ROWS_DMOD_X]
        dmod[0] = dmod[0] + st_ref[ROWS_DMOD_C[0]:ROWS_DMOD_C[0] + 1, :]
        dmod[1] = dmod[1] + st_ref[ROWS_DMOD_C[1]:ROWS_DMOD_C[1] + 1, :]
        grads = dict(
            c_ctx=dsc_ref[0:1, :] * (sg * (1.0 + c * (1.0 - sg))),
            b_ada=jnp.concatenate(dmod, axis=1),
            attn_sink=sk_ref[0:1, 0:N_KV * GROUP],
            gmlp_ln_g=gm_ref[0:1, :], gmlp_ln_b=gm_ref[1:2, :],
            w_spatial=ws_ref[...], b_spatial=bs_ref[...],
            ln1_g=st_ref[ROW_LN1_G:ROW_LN1_G + 1, :], ln1_b=st_ref[ROW_LN1_B:ROW_LN1_B + 1, :],
            ln2_g=st_ref[ROW_LN2_G:ROW_LN2_G + 1, :], ln2_b=st_ref[ROW_LN2_B:ROW_LN2_B + 1, :])
        for i, name in enumerate(SMALL):
            g = grads[name]
            d, nm, nv = _adam_update(w_refs[name][...], g, m_refs[name][...], v_refs[name][...])
            outs[i][...] = g
            outs[n + i][...] = d
            outs[2 * n + i][...] = nm
            outs[3 * n + i][...] = nv

    vmem = pl.BlockSpec(memory_space=pltpu.VMEM)
    args = list(sums) + [dsc] + [w[k] for k in SMALL] + [m[k] for k in SMALL] + [v[k] for k in SMALL]
    shapes = [jax.ShapeDtypeStruct(w[k].shape, F32) for k in SMALL]
    out = pl.pallas_call(
        body, name="adamw_small", in_specs=[vmem] * len(args), out_specs=[vmem] * (4 * n), out_shape=shapes * 4,
        compiler_params=pltpu.CompilerParams(vmem_limit_bytes=VMEM_LIMIT),
    )(*args)
    return [dict(zip(SMALL, out[i * n:(i + 1) * n])) for i in range(4)]


def _adamw_halves(w, mine, theirs, m, v, c_arr, name):
    r, c = w.shape
    tr = min(256, r // 2)
    while (r // 2) % tr:
        tr -= 8
    nt = (r // 2) // tr

    def body(c_ref, w_ref, mine_ref, theirs_ref, m_ref, v_ref, g_ref, d_ref, nm_ref, nv_ref):
        g = jnp.where(pl.program_id(0) == c_ref[0], mine_ref[...], theirs_ref[...])
        g_ref[...] = g
        d_ref[...], nm_ref[...], nv_ref[...] = _adam_update(w_ref[...], g, m_ref[...], v_ref[...])

    whole = pl.BlockSpec((tr, c), lambda hb, i, c_ref: (hb * nt + i, 0))
    mine_spec = pl.BlockSpec((tr, c), lambda hb, i, c_ref: (jnp.where(hb == c_ref[0], i, 0), 0))
    theirs_spec = pl.BlockSpec((tr, c), lambda hb, i, c_ref: (jnp.where(hb == c_ref[0], 0, i), 0))
    shp = jax.ShapeDtypeStruct((r, c), F32)
    return pl.pallas_call(
        body, name=name,
        grid_spec=pltpu.PrefetchScalarGridSpec(
            num_scalar_prefetch=1, grid=(2, nt), in_specs=[whole, mine_spec, theirs_spec, whole, whole],
            out_specs=[whole] * 4),
        out_shape=[shp] * 4,
        compiler_params=_cp("arbitrary", "arbitrary"),
    )(c_arr, w, mine, theirs, m, v)


def _adamw(w, g, m, v, name):
    r, c = w.shape
    tr = r if r * c <= 256 * 1024 else min(256, r)
    while r % tr:
        tr -= 8

    def body(w_ref, g_ref, m_ref, v_ref, d_ref, nm_ref, nv_ref):
        d_ref[...], nm_ref[...], nv_ref[...] = _adam_update(w_ref[...], g_ref[...], m_ref[...], v_ref[...])

    spec = pl.BlockSpec((tr, c), lambda i: (i, 0))
    shp = jax.ShapeDtypeStruct((r, c), F32)
    return pl.pallas_call(
        body, name=name, grid=(r // tr,), in_specs=[spec] * 4, out_specs=[spec] * 3, out_shape=[shp] * 3,
        compiler_params=_cp("arbitrary"),
    )(w, g, m, v)


def _my_pos():
    return lax.axis_index("x"), lax.axis_index("y"), lax.axis_index("c")


N_COPY = 7


class _Gather:
    def __init__(self, x_refs, out_refs, send_sems, recv_sems):
        self.x_refs, self.out_refs = x_refs, out_refs
        self.send_sems, self.recv_sems = send_sems, recv_sems
        x, y, c = _my_pos()
        self.c = c
        self.me, self.sibling = (x, y, c), (x, y, 1 - c)
        self.chips = [(1 - x, y), (x, 1 - y), (1 - x, 1 - y)]

    def _copy(self, a, k, block, to, from_input=False):
        px, py, pc = block
        rows = self.out_refs[a].at[4 * px + 2 * py + pc]
        return pltpu.make_async_remote_copy(
            src_ref=self.x_refs[a] if from_input else rows, dst_ref=rows,
            send_sem=self.send_sems.at[a * N_COPY + k], recv_sem=self.recv_sems.at[a * N_COPY + k],
            device_id=to, device_id_type=MESH)

    def start(self):
        n = len(self.x_refs)
        for a in range(n):
            self._copy(a, 0, self.me, self.sibling, from_input=True).start()
        for j, chip in enumerate(self.chips):
            for a in range(n):
                self._copy(a, 1 + j, self.me, (*chip, self.c), from_input=True).start()

    def forward(self):
        c = self.c
        for j, chip in enumerate(self.chips):
            for a in range(len(self.x_refs)):
                self._copy(a, 1 + j, (*chip, c), self.me).wait_recv()
                self._copy(a, 4 + j, (*chip, c), self.sibling).start()

    def finish(self):
        self.forward()
        self.drain()

    def drain(self):
        n = len(self.x_refs)
        c = self.c
        for a in range(n):
            self._copy(a, 0, self.sibling, self.me).wait_recv()
        for j, chip in enumerate(self.chips):
            for a in range(n):
                self._copy(a, 4 + j, (*chip, 1 - c), self.me).wait_recv()
        for a in range(n):
            self._copy(a, 0, self.me, self.sibling, from_input=True).wait_send()
            for j, chip in enumerate(self.chips):
                self._copy(a, 1 + j, self.me, (*chip, c), from_input=True).wait_send()
                self._copy(a, 4 + j, (*chip, c), self.sibling).wait_send()


def _comm_scratch(n):
    return [pltpu.SemaphoreType.DMA((n * N_COPY,)), pltpu.SemaphoreType.DMA((n * N_COPY,))]


def _comm_specs(n):
    return [pl.BlockSpec(memory_space=pl.ANY)] * n


def _gathered_shapes(xs):
    return [jax.ShapeDtypeStruct((N_DEV,) + v.shape, v.dtype) for v in xs]


def _with_own(gathered, xs, me):
    return [lax.dynamic_update_index_in_dim(g, v, me, 0) for g, v in zip(gathered, xs)]


def _all_gather(xs, me, name):
    n = len(xs)

    def body(*refs):
        g = _Gather(refs[:n], refs[n:2 * n], *refs[2 * n:])
        g.start()
        g.finish()

    out = pl.pallas_call(
        body, name=name, out_shape=_gathered_shapes(xs), in_specs=_comm_specs(n), out_specs=_comm_specs(n),
        scratch_shapes=_comm_scratch(n),
    )(*xs)
    return _with_own(out, xs, me)


class _AllToAll:
    def __init__(self, x_refs, out_refs, send_sems, recv_sems):
        self.x_refs, self.out_refs = x_refs, out_refs
        self.send_sems, self.recv_sems = send_sems, recv_sems
        self.pos = _my_pos()
        x, y, c = self.pos
        self.me = 4 * x + 2 * y + c

    def _peer(self, k):
        x, y, c = self.pos
        return (x ^ ((k >> 2) & 1), y ^ ((k >> 1) & 1), c ^ (k & 1))

    def _copy(self, a, k):
        p = self._peer(k)
        return pltpu.make_async_remote_copy(
            src_ref=self.x_refs[a].at[4 * p[0] + 2 * p[1] + p[2]], dst_ref=self.out_refs[a].at[self.me],
            send_sem=self.send_sems.at[a * N_COPY + k - 1], recv_sem=self.recv_sems.at[a * N_COPY + k - 1],
            device_id=p, device_id_type=MESH)

    def start(self):
        for k in range(1, N_DEV):
            for a in range(len(self.x_refs)):
                self._copy(a, k).start()

    def finish(self):
        for a in range(len(self.x_refs)):
            for k in range(1, N_DEV):
                self._copy(a, k).wait_recv()
            for k in range(1, N_DEV):
                self._copy(a, k).wait_send()


def _sibling_exchange(xs, name):
    n = len(xs)

    def body(*refs):
        x_refs, out_refs = refs[:n], refs[n:2 * n]
        send_sems, recv_sems = refs[2 * n:]
        x, y, c = _my_pos()

        def push(a):
            return pltpu.make_async_remote_copy(
                src_ref=x_refs[a], dst_ref=out_refs[a], send_sem=send_sems.at[a], recv_sem=recv_sems.at[a],
                device_id=(x, y, 1 - c), device_id_type=MESH)

        for a in range(n):
            push(a).start()
        for a in range(n):
            push(a).wait_recv()
            push(a).wait_send()

    return pl.pallas_call(
        body, name=name, out_shape=[jax.ShapeDtypeStruct(v.shape, v.dtype) for v in xs],
        in_specs=_comm_specs(n), out_specs=_comm_specs(n),
        scratch_shapes=[pltpu.SemaphoreType.DMA((n,)), pltpu.SemaphoreType.DMA((n,))],
    )(*xs)


def _scatter_and_gather(scatter, gather, name):
    ns, ng = len(scatter), len(gather)

    def body(*refs):
        s_in, g_in = refs[:ns], refs[ns:ns + ng]
        s_out, g_out = refs[ns + ng:2 * ns + ng], refs[2 * ns + ng:2 * (ns + ng)]
        s_send, s_recv, g_send, g_recv = refs[2 * (ns + ng):]
        g = _Gather(g_in, g_out, g_send, g_recv)
        t = _AllToAll(s_in, s_out, s_send, s_recv)
        g.start()
        t.start()
        g.finish()
        t.finish()

    out = pl.pallas_call(
        body, name=name,
        out_shape=[jax.ShapeDtypeStruct(v.shape, v.dtype) for v in scatter] + _gathered_shapes(gather),
        in_specs=_comm_specs(ns + ng), out_specs=_comm_specs(ns + ng),
        scratch_shapes=_comm_scratch(ns) + _comm_scratch(ng),
    )(*scatter, *gather)
    return out[:ns], out[ns:]


def _row_tile(seq, want):
    return min(want, seq)


def _local_step(x, ctx, tgt, mod_x, mod_c, wb, sink, gmlp_g, gmlp_b, w_s, b_s, ln1_g, ln1_b, ln2_g, ln2_b,
                later=None, me=None):
    seq = x.shape[0]
    on_mesh = me is not None
    modx1 = jnp.concatenate([mod_x[0:2], jnp.zeros((6, D), F32)], axis=0)
    modc = jnp.concatenate([mod_c[0:2], jnp.zeros((6, D), F32)], axis=0)
    vec = jnp.concatenate([mod_x[2:3], ln1_g, ln1_b, mod_x[3:6], ln2_g, ln2_b], axis=0)
    gp = jnp.concatenate([gmlp_g, gmlp_b, jnp.zeros((6, G_W), F32)], axis=0)
    ws_stack = w_s.reshape(N_GRP * BLK, BLK).astype(BF16)
    ws_stack_t = jnp.transpose(w_s, (2, 0, 1)).reshape(BLK, N_GRP * BLK).astype(BF16)
    bias_full = jnp.repeat(b_s.T, GRP_D, axis=1)
    cos, sin = _rope_tables(seq)
    w_in = wb["w_in"]
    w_kv = w_in[Q_W:Q_W + 2 * KV_W, :]
    tm_big = _row_tile(seq, 512)
    tm_ffn = _row_tile(seq, 256)

    hc, kvc, vac = _ctx_fwd(ctx, modc, w_kv)
    behind_proj = ("w_a", "w_b", "w_o") if on_mesh else ()
    behind_attn = ("w_fi",) if on_mesh else ()
    behind_mix = ("w_fo",) if on_mesh else ()
    wb = dict(wb)

    def whole(names, gathered):
        for n, g in zip(names, _with_own(list(gathered), [later[n] for n in names], me)):
            wb[n] = g.reshape(-1, g.shape[2]) if n in ROW_SHARDED else g.reshape(N_SHARD, 2 * g.shape[1], g.shape[2])

    (h, q, kv, va, uv, gab), got = _proj_fwd(x, modx1, w_in, cos, sin, _row_tile(seq, 1024), gather=[later[n] for n in behind_proj])
    whole(behind_proj, got)
    if on_mesh:
        for n in ("w_a", "w_b"):
            wb[n] = wb[n].transpose(1, 0, 2).reshape(wb[n].shape[1], D)
    (ya, lse), got = _attn_fwd(q, kv, va, kvc, vac, sink, gather=[later[n] for n in behind_attn])
    whole(behind_attn, got)
    (a, b, mix, merged, yb), got = _mix_fwd(uv, gab, ya, gp, ws_stack, bias_full, wb["w_a"], wb["w_b"], wb["w_o"], tm_big,
                                            gather=[later[n] for n in behind_mix])
    whole(behind_mix, got)
    act, h2, dff, df, dr1, st_ffn = _ffn(x, mix, tgt, vec, wb["w_fi"], wb["w_fo"], tm_ffn)
    blocks, recv = {}, {}
    blocks["w_fo"] = _eighths(_tn_matmul(act, df, 512, "tn_w_ffn_out", BF16, tk=2048))
    if on_mesh:
        g_w_fi, (recv["w_fo"],) = _tn_matmul(h2, dff, FH_SHARD, "tn_w_ffn_in", BF16, shard_major=True, tk=2048,
                                            scatter=[blocks["w_fo"]])
    else:
        g_w_fi = _tn_matmul(h2, dff, FH_SHARD, "tn_w_ffn_in", BF16, shard_major=True, tk=2048)
    blocks["w_fi"] = _eighths(g_w_fi)
    (dya, dpb, dws, dbs_full, st4, g_w_o, g_w_a, g_w_b), got = _mix_bwd(
        dr1, a, b, gab, uv, merged, ya, yb, vec, gp, ws_stack, ws_stack_t, bias_full, wb["w_a"], wb["w_b"], wb["w_o"],
        tm_big, scatter=[blocks["w_fi"]] if on_mesh else ())
    recv.update(zip(("w_fi",), got))
    shard_major = [g.reshape(g.shape[0], N_SHARD, D // N_SHARD).transpose(1, 0, 2) for g in (g_w_a, g_w_b)]
    blocks.update(w_o=_eighths(g_w_o), w_a=_eighths(shard_major[0]), w_b=_eighths(shard_major[1]))
    mixer = ("w_o", "w_a", "w_b") if on_mesh else ()
    (dq, dkv, dkvc, dsink), got = _attn_bwd(q, kv, kvc, sink, dya, ya, lse, scatter=[blocks[n] for n in mixer])
    recv.update(zip(mixer, got))
    g_wkv_ctx, st0 = _ctx_bwd(dkvc, ctx, hc, w_kv)
    (dqkv, grad_x, st1), _ = _proj_bwd(dq, dkv, dpb, x, dr1, modx1, w_in, cos, sin, tm_big)
    dbs = jnp.sum(dbs_full.reshape(BLK, N_GRP, GRP_D), axis=2).T
    early = [jnp.concatenate([st_ffn, st0], axis=0), st4, dsink, dws, dbs]
    init = jnp.pad(g_wkv_ctx, ((Q_W, 0), (0, 0)))
    g_qkv = _tn_matmul(dqkv, h, D, "tn_w_in_qkv", BF16, init=init, tk=1024)
    if on_mesh:
        g_rest, early_gathered = _tn_matmul(dpb, h, D, "tn_w_in_rest", BF16, tk=1024, gather=early)
    else:
        g_rest, early_gathered = _tn_matmul(dpb, h, D, "tn_w_in_rest", BF16, tk=1024), None
    blocks["w_in"] = _eighths(jnp.concatenate([g_qkv, g_rest], axis=0))
    return grad_x, dict(early=early, early_gathered=early_gathered, late=st1), blocks, recv


BIG = ("w_in", "w_a", "w_b", "w_o", "w_fi", "w_fo")
ROW_SHARDED = ("w_o", "w_fo")


def _half_of_shard(shard, c):
    r = shard.shape[0]
    return lax.dynamic_slice_in_dim(shard, c * (r // 2), r // 2, axis=0)


def _eighths(v):
    rows = v.shape[-2] * (v.shape[0] if v.ndim == 3 else 1)
    return v.reshape(N_DEV, rows // N_DEV, v.shape[-1])


def kernel(x, c, ctx, c_ctx, w_ada, b_ada, w_in, attn_sink, gmlp_ln_g, gmlp_ln_b, w_spatial, b_spatial, w_branch_a, w_branch_b, w_out, ln1_g, ln1_b, w_ffn_in, w_ffn_out, ln2_g, ln2_b, loss_target, m_c_ctx, m_w_ada, m_b_ada, m_w_in, m_attn_sink, m_gmlp_ln_g, m_gmlp_ln_b, m_w_spatial, m_b_spatial, m_w_branch_a, m_w_branch_b, m_w_out, m_ln1_g, m_ln1_b, m_w_ffn_in, m_w_ffn_out, m_ln2_g, m_ln2_b, v_c_ctx, v_w_ada, v_b_ada, v_w_in, v_attn_sink, v_gmlp_ln_g, v_gmlp_ln_b, v_w_spatial, v_b_spatial, v_w_branch_a, v_w_branch_b, v_w_out, v_ln1_g, v_ln1_b, v_w_ffn_in, v_w_ffn_out, v_ln2_g, v_ln2_b):
    mx, my, mc = _my_pos()
    me = 4 * mx + 2 * my + mc
    chip = 2 * mx + my
    shards = dict(w_in=w_in[0].T, w_a=w_branch_a[0], w_b=w_branch_b[0], w_o=w_out[0], w_fi=w_ffn_in[0], w_fo=w_ffn_out[0])

    halves = {n: _half_of_shard(shards[n], mc).astype(BF16) for n in BIG}
    c_rows = jnp.concatenate([c, jnp.zeros((7, D), F32)], axis=0)
    g_in, c_g = _all_gather([halves["w_in"], c_rows], me, "gather_w_in")
    wb = dict(w_in=g_in.reshape(IN_W, D))

    c_all = c_g[:, 0, :]
    cc = jnp.concatenate([c_all, c_ctx[None, :], jnp.zeros((7, D), F32)], axis=0)
    sig_cc = jax.nn.sigmoid(cc)
    sc_all = cc * sig_cc
    mod_shard = _ada_fwd(sc_all, w_ada[0])
    mod_g = _all_gather([mod_shard], me, "gather_mod")[0]
    mod_all = jnp.concatenate([mod_g[2 * s] for s in range(4)], axis=1) + b_ada
    mod_x = lax.dynamic_slice_in_dim(mod_all, me, 1, axis=0).reshape(6, D)
    mod_c = mod_all[8].reshape(6, D)[0:2]

    grad_x, small, blocks, recv = _local_step(
        x[0], ctx[0], loss_target[0], mod_x, mod_c, wb, attn_sink, gmlp_ln_g, gmlp_ln_b, w_spatial[0], b_spatial[0],
        ln1_g, ln1_b, ln2_g, ln2_b, later=halves, me=me)

    (recv["w_in"],), late = _scatter_and_gather([blocks["w_in"]], [small["late"]], "scatter_w_in_gather_small")
    late = _with_own(late, [small["late"]], me)[0]
    gathered = _with_own(small["early_gathered"], small["early"], me)
    gathered[0] = jnp.concatenate([gathered[0][:, :16], late, gathered[0][:, 16:]], axis=1)

    me_arr = jnp.reshape(me, (1,)).astype(jnp.int32)
    summed = {n: _sum_blocks(recv[n], blocks[n], me_arr, "sum_grads_" + n) for n in BIG}
    theirs = dict(zip(BIG, _sibling_exchange([summed[n] for n in BIG], "exchange_grads")))

    sums = _sum8_many(gathered, "sum_small")
    stats = sums[0]
    loss = 0.5 * jnp.sum(stats[ROW_LOSS]) / D
    dmod_x_all = jnp.concatenate([gathered[0][:, r_, :] for r_ in ROWS_DMOD_X], axis=1)
    dmod_c_full = jnp.concatenate([stats[r_] for r_ in ROWS_DMOD_C] + [jnp.zeros((4 * D,), F32)])
    dm_rows = jnp.concatenate([dmod_x_all, dmod_c_full[None, :], jnp.zeros((7, 6 * D), F32)], axis=0)
    cs = w_ada.shape[2]
    dm_shard = lax.dynamic_slice_in_dim(dm_rows, chip * cs, cs, axis=1)
    dmc_shard = jnp.concatenate([dm_shard[8:9], jnp.zeros((7, cs), F32)], axis=0)
    g_w_ada, part = _ada_bwd(sc_all.T, dm_shard, dmc_shard, w_ada[0])
    part_all = _all_gather([part * (mc == 0).astype(F32)], me, "gather_c_ctx")[0]
    dsc = _sum8(part_all, "sum_c_ctx")

    grads = dict(w_ada=g_w_ada[None])
    weights = dict(c_ctx=c_ctx, w_ada=w_ada, b_ada=b_ada, w_in=w_in, attn_sink=attn_sink, gmlp_ln_g=gmlp_ln_g,
                   gmlp_ln_b=gmlp_ln_b, w_spatial=w_spatial, b_spatial=b_spatial, w_branch_a=w_branch_a,
                   w_branch_b=w_branch_b, w_out=w_out, ln1_g=ln1_g, ln1_b=ln1_b, w_ffn_in=w_ffn_in, w_ffn_out=w_ffn_out,
                   ln2_g=ln2_g, ln2_b=ln2_b)
    ms = dict(c_ctx=m_c_ctx, w_ada=m_w_ada, b_ada=m_b_ada, w_in=m_w_in, attn_sink=m_attn_sink, gmlp_ln_g=m_gmlp_ln_g,
              gmlp_ln_b=m_gmlp_ln_b, w_spatial=m_w_spatial, b_spatial=m_b_spatial, w_branch_a=m_w_branch_a,
              w_branch_b=m_w_branch_b, w_out=m_w_out, ln1_g=m_ln1_g, ln1_b=m_ln1_b, w_ffn_in=m_w_ffn_in,
              w_ffn_out=m_w_ffn_out, ln2_g=m_ln2_g, ln2_b=m_ln2_b)
    vs = dict(c_ctx=v_c_ctx, w_ada=v_w_ada, b_ada=v_b_ada, w_in=v_w_in, attn_sink=v_attn_sink, gmlp_ln_g=v_gmlp_ln_g,
              gmlp_ln_b=v_gmlp_ln_b, w_spatial=v_w_spatial, b_spatial=v_b_spatial, w_branch_a=v_w_branch_a,
              w_branch_b=v_w_branch_b, w_out=v_w_out, ln1_g=v_ln1_g, ln1_b=v_ln1_b, w_ffn_in=v_w_ffn_in,
              w_ffn_out=v_w_ffn_out, ln2_g=v_ln2_g, ln2_b=v_ln2_b)
    order = list(weights)
    delta, new_m, new_v = {}, {}, {}
    d_, m_, v_ = _adamw(w_ada[0], g_w_ada, m_w_ada[0], v_w_ada[0], "adamw_w_ada")
    delta["w_ada"], new_m["w_ada"], new_v["w_ada"] = d_[None], m_[None], v_[None]
    c_arr = jnp.reshape(mc, (1,)).astype(jnp.int32)
    names = dict(w_in="w_in", w_a="w_branch_a", w_b="w_branch_b", w_o="w_out", w_fi="w_ffn_in", w_fo="w_ffn_out")
    for k, n in names.items():
        flip = (lambda t: t.T) if k == "w_in" else (lambda t: t)
        outs = _adamw_halves(flip(weights[n][0]), summed[k], theirs[k], flip(ms[n][0]), flip(vs[n][0]), c_arr, "adamw_" + n)
        grads[n], delta[n], new_m[n], new_v[n] = [flip(t)[None] for t in outs]

    def view(a):
        return a.reshape(-1, a.shape[-1]) if a.ndim != 1 else a.reshape(1, -1)

    small = _adamw_small(sums, dsc, *[{n: view(d[n]) for n in SMALL} for d in (weights, ms, vs)])
    for out, src in zip((grads, delta, new_m, new_v), small):
        for n in SMALL:
            out[n] = src[n].reshape(weights[n].shape)

    return (loss, grad_x[None], *[grads[n] for n in order], *[delta[n] for n in order],
            *[new_m[n] for n in order], *[new_v[n] for n in order])
```

```python
import math

import jax
import jax.numpy as jnp
from jax import lax
from jax.experimental import pallas as pl
from jax.experimental.pallas import tpu as pltpu

F32 = jnp.float32
BF16 = jnp.bfloat16

D = 1024
HEAD = 64
N_KV = 2
GROUP = 4
Q_W = 512
KV_W = 128
G_W = 512
BLK = 128
N_GRP = 8
GRP_D = 64
FH = 2816
IN_W = 3840
GRID_W = 64
ROPE_BASE = 10000.0
LN_EPS = 1e-5
NEG = -1e30
ALPHA = (2 * 1) ** 0.25
SCALE = HEAD ** -0.5
GELU_K = math.sqrt(2.0 / math.pi)
GELU_A = 0.044715
ADAM_LR = 0.001
ADAM_B1 = 0.9
ADAM_B2 = 0.999
ADAM_EPS = 1e-08
ADAM_WD = 0.01
ADAM_STEP = 10
N_DEV = 8
N_SHARD = 4
FH_SHARD = FH // 2
LANES = 128
VMEM_LIMIT = 56 * 1024 * 1024
MESH = pl.DeviceIdType.MESH


def _cp(*sem):
    return pltpu.CompilerParams(dimension_semantics=sem, vmem_limit_bytes=VMEM_LIMIT)


def _resident(shape):
    return pl.BlockSpec(shape, lambda *_: (0,) * len(shape), pipeline_mode=pl.Buffered(1))


def _rows(tm, width):
    return pl.BlockSpec((tm, width), lambda i: (i, 0))


def _acc(shape):
    return pl.BlockSpec(shape, lambda *_: (0,) * len(shape))


def _dot(a, b):
    return jnp.dot(a, b, preferred_element_type=F32)


def _dot_nt(a, b):
    return lax.dot_general(a, b, (((1,), (1,)), ((), ())), preferred_element_type=F32)


def _dot_tn(a, b):
    return lax.dot_general(a, b, (((0,), (0,)), ((), ())), preferred_element_type=F32)


def _ln(x):
    mu = jnp.mean(x, axis=-1, keepdims=True)
    xc = x - mu
    var = jnp.mean(xc * xc, axis=-1, keepdims=True)
    rstd = lax.rsqrt(var + LN_EPS)
    return xc * rstd, rstd


def _ln_bwd(dxhat, xhat, rstd):
    return (dxhat - jnp.mean(dxhat, axis=-1, keepdims=True)
            - xhat * jnp.mean(dxhat * xhat, axis=-1, keepdims=True)) * rstd


def _sig(x):
    return 0.5 + 0.5 * jnp.tanh(0.5 * x)


def _gelu(x):
    t = jnp.tanh(x * (GELU_K + (GELU_K * GELU_A) * (x * x)))
    hx = 0.5 * x
    return hx + hx * t, t


def _gelu_grad(x, t):
    return 0.5 + 0.5 * t + (0.5 * x) * (1.0 - t * t) * (GELU_K + (3.0 * GELU_K * GELU_A) * (x * x))


def _colsum(v):
    return jnp.sum(v, axis=0, keepdims=True)


def _partner(x):
    w = x.shape[1]
    lane = lax.broadcasted_iota(jnp.int32, x.shape, 1)
    return jnp.where((lane & 31) < 16, pltpu.roll(x, w - 16, 1), pltpu.roll(x, 16, 1))


def _rope(x, cos, sin):
    return x * cos + _partner(x) * sin


def _unrope(g, cos, sin):
    return g * cos + _partner(g * sin)


def _rope_tables(seq):
    inv = ROPE_BASE ** (-jnp.arange(HEAD // 4, dtype=F32) / (HEAD // 4))
    pos = jnp.arange(seq, dtype=jnp.int32)
    ar = (pos // GRID_W).astype(F32)[:, None] * inv
    ac = (pos % GRID_W).astype(F32)[:, None] * inv
    cos = jnp.concatenate([jnp.cos(ar), jnp.cos(ar), jnp.cos(ac), jnp.cos(ac)], axis=-1)
    sin = jnp.concatenate([-jnp.sin(ar), jnp.sin(ar), -jnp.sin(ac), jnp.sin(ac)], axis=-1)
    return jnp.tile(cos, (1, LANES // HEAD)), jnp.tile(sin, (1, LANES // HEAD))


def _ctx_fwd(ctx, modc, w_kv):
    n_ctx = ctx.shape[0]

    def body(ctx_ref, mod_ref, w_ref, hc_ref, kvc_ref, vac_ref):
        xhat, _ = _ln(ctx_ref[...])
        hc = (xhat * (1.0 + mod_ref[1:2, :]) + mod_ref[0:1, :]).astype(BF16)
        hc_ref[...] = hc
        kvc = _dot_nt(hc, w_ref[...]).astype(BF16)
        kvc_ref[...] = kvc
        vac_ref[...] = _with_ones(kvc[:, KV_W:])

    return pl.pallas_call(
        body, name="ctx_fwd", grid=(1,),
        in_specs=[_acc((n_ctx, D)), _acc((8, D)), _acc((2 * KV_W, D))],
        out_specs=[_acc((n_ctx, D)), _acc((n_ctx, 2 * KV_W)), _acc((n_ctx, 2 * LANES))],
        out_shape=[jax.ShapeDtypeStruct((n_ctx, D), BF16), jax.ShapeDtypeStruct((n_ctx, 2 * KV_W), BF16),
                   jax.ShapeDtypeStruct((n_ctx, 2 * LANES), BF16)],
        compiler_params=_cp("arbitrary"),
    )(ctx, modc, w_kv)


def _host_start(step, comm):
    if comm is not None:
        @pl.when(step == 0)
        def _():
            comm.start()


def _host_finish(step, last, comm, forward_at=None):
    if comm is None:
        return
    if forward_at is None or forward_at >= last:
        @pl.when(step == last)
        def _():
            comm.finish()
    else:
        @pl.when(step == forward_at)
        def _():
            comm.forward()

        @pl.when(step == last)
        def _():
            comm.drain()


def _proj_fwd(x, modx, w_in, cos, sin, tm, gather=()):
    seq = x.shape[0]
    ng = len(gather)

    def body(x_ref, mod_ref, w_ref, cos_ref, sin_ref, *rest):
        h_ref, q_ref, kv_ref, va_ref, uv_ref, gab_ref = rest[ng:ng + 6]
        comm = _Gather(rest[:ng], rest[ng + 6:2 * ng + 6], *rest[2 * ng + 6:]) if ng else None
        _host_start(pl.program_id(0), comm)
        xhat, _ = _ln(x_ref[...])
        h = (xhat * (1.0 + mod_ref[1:2, :]) + mod_ref[0:1, :]).astype(BF16)
        h_ref[...] = h
        cos1, sin1 = cos_ref[...], sin_ref[...]
        cos2 = jnp.concatenate([cos1, cos1], axis=1)
        sin2 = jnp.concatenate([sin1, sin1], axis=1)
        for j in range(Q_W // 256):
            t = _dot_nt(h, w_ref[256 * j:256 * (j + 1), :])
            q_ref[:, 256 * j:256 * (j + 1)] = (_rope(t, cos2, sin2) * SCALE).astype(BF16)
        t = _dot_nt(h, w_ref[Q_W:Q_W + 2 * KV_W, :])
        kv_ref[:, :KV_W] = _rope(t[:, :KV_W], cos1, sin1).astype(BF16)
        v = t[:, KV_W:].astype(BF16)
        kv_ref[:, KV_W:] = v
        va_ref[...] = _with_ones(v)
        o = Q_W + 2 * KV_W
        for j in range(2):
            uv_ref[:, G_W * j:G_W * (j + 1)] = _dot_nt(h, w_ref[o + G_W * j:o + G_W * (j + 1), :]).astype(BF16)
        o += 2 * G_W
        for j in range(4):
            gab_ref[:, 512 * j:512 * (j + 1)] = _dot_nt(h, w_ref[o + 512 * j:o + 512 * (j + 1), :]).astype(BF16)
        _host_finish(pl.program_id(0), seq // tm - 1, comm, forward_at=(seq // tm) // 2)

    out = pl.pallas_call(
        body, name="proj_fwd", grid=(seq // tm,),
        in_specs=[_rows(tm, D), _acc((8, D)), _resident((IN_W, D)), _rows(tm, LANES), _rows(tm, LANES)] + _comm_specs(ng),
        out_specs=[_rows(tm, D), _rows(tm, Q_W), _rows(tm, 2 * KV_W), _rows(tm, 2 * LANES), _rows(tm, 2 * G_W),
                   _rows(tm, 2 * D)] + _comm_specs(ng),
        out_shape=[jax.ShapeDtypeStruct((seq, D), BF16), jax.ShapeDtypeStruct((seq, Q_W), BF16),
                   jax.ShapeDtypeStruct((seq, 2 * KV_W), BF16), jax.ShapeDtypeStruct((seq, 2 * LANES), BF16),
                   jax.ShapeDtypeStruct((seq, 2 * G_W), BF16), jax.ShapeDtypeStruct((seq, 2 * D), BF16)] + _gathered_shapes(gather),
        scratch_shapes=_comm_scratch(ng) if ng else [],
        compiler_params=_cp("arbitrary"),
    )(x, modx, w_in, cos, sin, *gather)
    return out[:6], out[6:]


def _stack_heads(x, hk):
    return jnp.concatenate([x[:, (hk * GROUP + g) * HEAD:(hk * GROUP + g + 1) * HEAD] for g in range(GROUP)], axis=0)


def _band_masks(n, nb):
    rows = GROUP * BLK
    qi = lax.broadcasted_iota(jnp.int32, (rows, BLK), 0) & (BLK - 1)
    kj = lax.broadcasted_iota(jnp.int32, (rows, BLK), 1)
    return (kj >= qi) & (n > 0), (kj <= qi) & (n < nb - 1)


def _attn_scores(q, k_refs, hk, masks):
    q4 = _stack_heads(q, hk)
    ks = [r[:, hk * HEAD:(hk + 1) * HEAD] for r in k_refs]
    s = [_dot_nt(q4, k) for k in ks]
    s[1] = jnp.where(masks[0], s[1], NEG)
    s[3] = jnp.where(masks[1], s[3], NEG)
    return q4, ks, s


def _sink_rows(sink_ref, hk):
    rows = GROUP * BLK
    rg = lax.broadcasted_iota(jnp.int32, (rows, 1), 0) >> 7
    sink_v = jnp.full((rows, 1), sink_ref[0, hk * GROUP], F32)
    for g in range(1, GROUP):
        sink_v = jnp.where(rg == g, sink_ref[0, hk * GROUP + g], sink_v)
    return sink_v


def _with_ones(v):
    ones = jnp.ones((v.shape[0], HEAD), v.dtype)
    return jnp.concatenate([v[:, :HEAD], ones, v[:, HEAD:], ones], axis=1)


def _kv_specs(nb, qb):
    def spec(d):
        return pl.BlockSpec((BLK, 2 * KV_W), lambda n: (jnp.clip(qb * n + d, 0, nb - 1), 0))
    return [spec(d) for d in range(-1, qb + 1)]


def _attn_fwd(q, kv, va, kvc, vac, sink, gather=()):
    seq = q.shape[0]
    nb = seq // BLK
    n_ctx = kvc.shape[0]
    ng = len(gather)
    Q_BLOCKS = 1
    nkv = Q_BLOCKS + 2
    steps = nb // Q_BLOCKS

    def body(q_ref, *rest):
        kv_refs, va_refs = rest[:nkv], rest[nkv:2 * nkv]
        kvc_ref, vac_ref, sink_ref = rest[2 * nkv:2 * nkv + 3]
        rest = rest[2 * nkv + 3:]
        o_ref, lse_ref = rest[ng:ng + 2]
        comm = _Gather(rest[:ng], rest[ng + 2:2 * ng + 2], *rest[2 * ng + 2:]) if ng else None
        n = pl.program_id(0)
        _host_start(n, comm)
        lane = lax.broadcasted_iota(jnp.int32, (BLK, LANES), 1)
        for sub in range(Q_BLOCKS):
            rs = slice(sub * BLK, (sub + 1) * BLK)
            q = q_ref[rs, :]
            outs = []
            lse_all = jnp.zeros((BLK, LANES), F32)
            masks = _band_masks(Q_BLOCKS * n + sub, nb)
            for hk in range(N_KV):
                _, _, s = _attn_scores(q, (kvc_ref,) + kv_refs[sub:sub + 3], hk, masks)
                sink_v = _sink_rows(sink_ref, hk)
                tile_max = s[1]
                for t in [s[0][:, i * LANES:(i + 1) * LANES] for i in range(n_ctx // LANES)] + s[2:]:
                    tile_max = jnp.maximum(tile_max, t)
                m = jnp.maximum(sink_v, jnp.max(tile_max, axis=-1, keepdims=True))
                o = jnp.zeros((GROUP * BLK, LANES), F32)
                for t, va_ref in zip(s, (vac_ref,) + va_refs[sub:sub + 3]):
                    o = o + _dot(jnp.exp((t - m).astype(BF16)), va_ref[:, hk * LANES:(hk + 1) * LANES])
                denom = o[:, HEAD:HEAD + 1] + jnp.exp(sink_v - m)
                o4 = o[:, :HEAD] * (1.0 / denom)
                lse4 = m + jnp.log(denom)
                for g in range(GROUP):
                    outs.append(o4[g * BLK:(g + 1) * BLK, :])
                    lse_all = jnp.where(lane == hk * GROUP + g, lse4[g * BLK:(g + 1) * BLK, :], lse_all)
            o_ref[rs, :] = jnp.concatenate(outs, axis=1).astype(BF16)
            lse_ref[rs, :] = lse_all
        _host_finish(n, steps - 1, comm, forward_at=(3 * steps) // 4)

    tq = Q_BLOCKS * BLK
    out = pl.pallas_call(
        body, name="attn_fwd", grid=(steps,),
        in_specs=[_rows(tq, Q_W)] + _kv_specs(nb, Q_BLOCKS) + _kv_specs(nb, Q_BLOCKS)
        + [_acc((n_ctx, 2 * KV_W)), _acc((n_ctx, 2 * LANES)), pl.BlockSpec(memory_space=pltpu.SMEM)] + _comm_specs(ng),
        out_specs=[_rows(tq, Q_W), _rows(tq, LANES)] + _comm_specs(ng),
        out_shape=[jax.ShapeDtypeStruct((seq, Q_W), BF16), jax.ShapeDtypeStruct((seq, LANES), F32)] + _gathered_shapes(gather),
        scratch_shapes=_comm_scratch(ng) if ng else [],
        compiler_params=_cp("arbitrary"),
    )(q, *([kv] * nkv), *([va] * nkv), kvc, vac, sink, *gather)
    return out[:2], out[2:]


def _gmlp_chunk(u, vb, gp_ref, ws_ref, bias_ref):
    gu, tu = _gelu(u)
    gv, tv = _gelu(vb)
    vhat, rstd = _ln(gv)
    vn = (vhat * gp_ref[0:1, :] + gp_ref[1:2, :]).astype(BF16)
    s = bias_ref[...] + jnp.concatenate(
        [_dot(ws_ref[g * BLK:(g + 1) * BLK, :], vn[:, g * GRP_D:(g + 1) * GRP_D]) for g in range(N_GRP)], axis=1)
    return gu, tu, tv, vhat, rstd, vn, s


def _mix_fwd(uv, gab, ya, gp, ws_stack, bias_full, w_a, w_b, w_o, tm, gather=()):
    seq = uv.shape[0]
    ng = len(gather)
    steps = seq // tm

    def body(uv_ref, gab_ref, ya_ref, gp_ref, ws_ref, bias_ref, wa_ref, wb_ref, wo_ref, *rest):
        a_ref, b_ref, mix_ref, merged_ref, yb_ref = rest[ng:ng + 5]
        comm = _Gather(rest[:ng], rest[ng + 5:2 * ng + 5], *rest[2 * ng + 5:]) if ng else None
        _host_start(pl.program_id(0), comm)
        for c in range(tm // BLK):
            rs = slice(c * BLK, (c + 1) * BLK)
            gu, _, _, _, _, _, s = _gmlp_chunk(uv_ref[rs, :G_W].astype(F32), uv_ref[rs, G_W:].astype(F32), gp_ref, ws_ref, bias_ref)
            yb_ref[rs, :] = (gu * s).astype(BF16)
        a = _dot(ya_ref[...], wa_ref[...])
        b = _dot(yb_ref[...], wb_ref[...])
        a_ref[...] = a.astype(BF16)
        b_ref[...] = b.astype(BF16)
        merged = (_sig(gab_ref[:, :D].astype(F32)) * a + _sig(gab_ref[:, D:].astype(F32)) * b).astype(BF16)
        merged_ref[...] = merged
        mix_ref[...] = _dot(merged, wo_ref[...])
        _host_finish(pl.program_id(0), steps - 1, comm, forward_at=(3 * steps) // 4)

    out = pl.pallas_call(
        body, name="mix_fwd", grid=(steps,),
        in_specs=[_rows(tm, 2 * G_W), _rows(tm, 2 * D), _rows(tm, Q_W), _acc((8, G_W)),
                  _resident((N_GRP * BLK, BLK)), _acc((BLK, G_W)),
                  _resident((Q_W, D)), _resident((G_W, D)), _resident((D, D))] + _comm_specs(ng),
        out_specs=[_rows(tm, D), _rows(tm, D), _rows(tm, D), _rows(tm, D), _rows(tm, G_W)] + _comm_specs(ng),
        out_shape=[jax.ShapeDtypeStruct((seq, D), BF16), jax.ShapeDtypeStruct((seq, D), BF16),
                   jax.ShapeDtypeStruct((seq, D), F32), jax.ShapeDtypeStruct((seq, D), BF16),
                   jax.ShapeDtypeStruct((seq, G_W), BF16)] + _gathered_shapes(gather),
        scratch_shapes=_comm_scratch(ng) if ng else [],
        compiler_params=_cp("arbitrary"),
    )(uv, gab, ya, gp, ws_stack, bias_full, w_a, w_b, w_o, *gather)
    return out[:5], out[5:]


def _mid_recompute(x_ref, mix_ref, vec_ref):
    r1 = ALPHA * x_ref[...] + vec_ref[0:1, :] * mix_ref[...]
    xh1, rstd1 = _ln(r1)
    xmid = xh1 * vec_ref[1:2, :] + vec_ref[2:3, :]
    xh2, rstd2 = _ln(xmid)
    return xh1, rstd1, xmid, xh2, rstd2


def _ffn(x, mix, tgt, vec, w_fi, w_fo, tm):
    seq = x.shape[0]

    def body(x_ref, mix_ref, tgt_ref, vec_ref, wi_ref, wo_ref, act_ref, h2_ref, dff_ref, df_ref, dr1_ref, st_ref, gu_ref):
        @pl.when(pl.program_id(0) == 0)
        def _():
            st_ref[...] = jnp.zeros_like(st_ref)

        xh1, rstd1, xmid, xh2, rstd2 = _mid_recompute(x_ref, mix_ref, vec_ref)
        h2 = (xh2 * (1.0 + vec_ref[4:5, :]) + vec_ref[3:4, :]).astype(BF16)
        h2_ref[...] = h2
        halves = [(slice(hh * FH_SHARD, (hh + 1) * FH_SHARD), slice(FH + hh * FH_SHARD, FH + (hh + 1) * FH_SHARD))
                  for hh in range(2)]
        for hh, (cs, cu) in enumerate(halves):
            g = _dot(h2, wi_ref[hh])
            u = _dot(h2, wi_ref[2 + hh])
            gu_ref[:, cs] = g
            gu_ref[:, cu] = u
            act_ref[:, cs] = (g * _sig(g) * u).astype(BF16)
        f = _dot(act_ref[...], wo_ref[...])
        r2 = ALPHA * xmid + vec_ref[5:6, :] * f
        yh, rstd = _ln(r2)
        y = yh * vec_ref[6:7, :] + vec_ref[7:8, :]
        err = y - tgt_ref[...]
        dy = err / D
        dr2 = _ln_bwd(dy * vec_ref[6:7, :], yh, rstd)
        st_ref[0:1, :] += _colsum(err * err)
        st_ref[1:2, :] += _colsum(dy * yh)
        st_ref[2:3, :] += _colsum(dy)
        st_ref[3:4, :] += _colsum(dr2 * f)

        df = (dr2 * vec_ref[5:6, :]).astype(BF16)
        df_ref[...] = df
        da_all = _dot_nt(df, wo_ref[...])
        for cs, cu in halves:
            da = da_all[:, cs]
            g = gu_ref[:, cs]
            u = gu_ref[:, cu]
            sg = _sig(g)
            dff_ref[:, cs] = (da * u * sg * (1.0 + g * (1.0 - sg))).astype(BF16)
            dff_ref[:, cu] = (da * g * sg).astype(BF16)
        dh2 = _dot_nt(dff_ref[:, :FH_SHARD], wi_ref[0])
        for s in range(1, N_SHARD):
            dh2 = dh2 + _dot_nt(dff_ref[:, s * FH_SHARD:(s + 1) * FH_SHARD], wi_ref[s])
        dxmid = _ln_bwd(dh2 * (1.0 + vec_ref[4:5, :]), xh2, rstd2) + ALPHA * dr2
        dr1 = _ln_bwd(dxmid * vec_ref[1:2, :], xh1, rstd1)
        dr1_ref[...] = dr1
        st_ref[8:9, :] += _colsum(dh2 * xh2)
        st_ref[9:10, :] += _colsum(dh2)
        st_ref[10:11, :] += _colsum(dxmid * xh1)
        st_ref[11:12, :] += _colsum(dxmid)
        st_ref[12:13, :] += _colsum(dr1 * mix_ref[...])

    return pl.pallas_call(
        body, name="ffn", grid=(seq // tm,),
        in_specs=[_rows(tm, D), _rows(tm, D), _rows(tm, D), _acc((8, D)), _resident((N_SHARD, D, FH_SHARD)), _resident((FH, D))],
        out_specs=[_rows(tm, FH), _rows(tm, D), _rows(tm, 2 * FH), _rows(tm, D), _rows(tm, D), _acc((16, D))],
        out_shape=[jax.ShapeDtypeStruct((seq, FH), BF16), jax.ShapeDtypeStruct((seq, D), BF16),
                   jax.ShapeDtypeStruct((seq, 2 * FH), BF16), jax.ShapeDtypeStruct((seq, D), BF16),
                   jax.ShapeDtypeStruct((seq, D), F32), jax.ShapeDtypeStruct((16, D), F32)],
        scratch_shapes=[pltpu.VMEM((tm, 2 * FH), F32)],
        compiler_params=_cp("arbitrary"),
    )(x, mix, tgt, vec, w_fi, w_fo)


def _mix_bwd(dr1, a, b, gab, uv, merged, ya, yb, vec, gp, ws_stack, ws_stack_t, bias_full, w_a, w_b, w_o, tm, scatter=()):
    seq = dr1.shape[0]
    last = seq // tm - 1
    ns = len(scatter)

    def body(dr1_ref, a_ref, b_ref, gab_ref, uv_ref, mg_ref, ya_ref, yb_ref, vec_ref, gp_ref, ws_ref, wst_ref, bias_ref,
             wa_ref, wb_ref, wo_ref, *rest):
        dya_ref, dp_ref, dws_ref, dbs_ref, st_ref, gwo_ref, gwa_ref, gwb_ref = rest[ns:ns + 8]
        acc_o, acc_a, acc_b = rest[2 * ns + 8:2 * ns + 11]
        comm = _AllToAll(rest[:ns], rest[ns + 8:2 * ns + 8], *rest[2 * ns + 11:]) if ns else None
        _host_start(pl.program_id(0), comm)

        @pl.when(pl.program_id(0) == 0)
        def _():
            dws_ref[...] = jnp.zeros_like(dws_ref)
            dbs_ref[...] = jnp.zeros_like(dbs_ref)
            st_ref[...] = jnp.zeros_like(st_ref)
            acc_o[...] = jnp.zeros_like(acc_o)
            acc_a[...] = jnp.zeros_like(acc_a)
            acc_b[...] = jnp.zeros_like(acc_b)

        dmix = (dr1_ref[...] * vec_ref[0:1, :]).astype(BF16)
        acc_o[...] += _dot_tn(mg_ref[...], dmix)
        dmerged = _dot_nt(dmix, wo_ref[...])
        sa = _sig(gab_ref[:, :D].astype(F32))
        sb = _sig(gab_ref[:, D:].astype(F32))
        da_f = dmerged * sa
        db_f = dmerged * sb
        da = da_f.astype(BF16)
        db = db_f.astype(BF16)
        dp_ref[:, 2 * G_W:2 * G_W + D] = (da_f * a_ref[...].astype(F32) * (1.0 - sa)).astype(BF16)
        dp_ref[:, 2 * G_W + D:] = (db_f * b_ref[...].astype(F32) * (1.0 - sb)).astype(BF16)
        dya_ref[...] = _dot_nt(da, wa_ref[...]).astype(BF16)
        dyb = _dot_nt(db, wb_ref[...])
        acc_a[...] += _dot_tn(ya_ref[...], da)
        acc_b[...] += _dot_tn(yb_ref[...], db)

        @pl.when(pl.program_id(0) == last)
        def _():
            gwo_ref[...] = acc_o[...].astype(BF16)
            gwa_ref[...] = acc_a[...].astype(BF16)
            gwb_ref[...] = acc_b[...].astype(BF16)

        for c in range(tm // BLK):
            rs = slice(c * BLK, (c + 1) * BLK)
            u = uv_ref[rs, :G_W].astype(F32)
            vb = uv_ref[rs, G_W:].astype(F32)
            gu, tu, tv, vhat, rstd, vn, s = _gmlp_chunk(u, vb, gp_ref, ws_ref, bias_ref)
            dyb_c = dyb[rs, :]
            ds = dyb_c * gu
            du = dyb_c * s * _gelu_grad(u, tu)
            ds_b = ds.astype(BF16)
            dvn_g = []
            for g in range(N_GRP):
                cg = slice(g * GRP_D, (g + 1) * GRP_D)
                dvn_g.append(_dot(wst_ref[:, g * BLK:(g + 1) * BLK], ds_b[:, cg]))
                dws_ref[g * BLK:(g + 1) * BLK, :] += _dot_nt(ds_b[:, cg], vn[:, cg])
            dvn = jnp.concatenate(dvn_g, axis=1)
            dbs_ref[...] += ds
            st_ref[0:1, :] += _colsum(dvn * vhat)
            st_ref[1:2, :] += _colsum(dvn)
            dgv = _ln_bwd(dvn * gp_ref[0:1, :], vhat, rstd)
            dvb = dgv * _gelu_grad(vb, tv)
            dp_ref[rs, :G_W] = du.astype(BF16)
            dp_ref[rs, G_W:2 * G_W] = dvb.astype(BF16)
        _host_finish(pl.program_id(0), last, comm)

    pw = 2 * G_W + 2 * D
    out = pl.pallas_call(
        body, name="mix_bwd", grid=(seq // tm,),
        in_specs=[_rows(tm, D), _rows(tm, D), _rows(tm, D), _rows(tm, 2 * D), _rows(tm, 2 * G_W), _rows(tm, D), _rows(tm, Q_W),
                  _rows(tm, G_W), _acc((8, D)), _acc((8, G_W)),
                  _resident((N_GRP * BLK, BLK)), _resident((BLK, N_GRP * BLK)), _acc((BLK, G_W)),
                  _resident((Q_W, D)), _resident((G_W, D)), _resident((D, D))] + _comm_specs(ns),
        out_specs=[_rows(tm, Q_W), _rows(tm, pw), _acc((N_GRP * BLK, BLK)), _acc((BLK, G_W)), _acc((8, G_W)),
                   _acc((D, D)), _acc((Q_W, D)), _acc((G_W, D))] + _comm_specs(ns),
        out_shape=[jax.ShapeDtypeStruct((seq, Q_W), BF16), jax.ShapeDtypeStruct((seq, pw), BF16),
                   jax.ShapeDtypeStruct((N_GRP * BLK, BLK), F32), jax.ShapeDtypeStruct((BLK, G_W), F32),
                   jax.ShapeDtypeStruct((8, G_W), F32), jax.ShapeDtypeStruct((D, D), BF16),
                   jax.ShapeDtypeStruct((Q_W, D), BF16), jax.ShapeDtypeStruct((G_W, D), BF16)]
        + [jax.ShapeDtypeStruct(v.shape, v.dtype) for v in scatter],
        scratch_shapes=[pltpu.VMEM((D, D), F32), pltpu.VMEM((Q_W, D), F32), pltpu.VMEM((G_W, D), F32)]
        + (_comm_scratch(ns) if ns else []),
        compiler_params=_cp("arbitrary"),
    )(dr1, a, b, gab, uv, merged, ya, yb, vec, gp, ws_stack, ws_stack_t, bias_full, w_a, w_b, w_o, *scatter)
    return out[:8], out[8:]


def _attn_bwd(q, kv, kvc, sink, dya, ya, lse, scatter=()):
    seq = q.shape[0]
    nb = seq // BLK
    n_ctx = kvc.shape[0]
    ns = len(scatter)
    Q_BLOCKS = 2
    nkv = Q_BLOCKS + 2
    steps = nb // Q_BLOCKS

    def body(q_ref, *rest):
        kv_refs = rest[:nkv]
        kvc_ref, sink_ref, do_ref, o_ref, lse_ref = rest[nkv:nkv + 5]
        rest = rest[nkv + 5:]
        dq_ref, dkv_ref, dkvc_ref, dsink_ref = rest[ns:ns + 4]
        comm = _AllToAll(rest[:ns], rest[ns + 4:2 * ns + 4], *rest[2 * ns + 4:]) if ns else None
        n = pl.program_id(0)
        _host_start(n, comm)

        @pl.when(n == 0)
        def _():
            dkv_ref[...] = jnp.zeros_like(dkv_ref)
            dkvc_ref[...] = jnp.zeros_like(dkvc_ref)
            dsink_ref[...] = jnp.zeros_like(dsink_ref)

        lane = lax.broadcasted_iota(jnp.int32, (1, LANES), 1)
        for sub in range(Q_BLOCKS):
            rs = slice(sub * BLK, (sub + 1) * BLK)
            blk = Q_BLOCKS * n + sub
            q = q_ref[rs, :]
            do = do_ref[rs, :]
            out = o_ref[rs, :]
            lse_all = lse_ref[rs, :]
            k_refs = (kvc_ref,) + kv_refs[sub:sub + 3]
            masks = _band_masks(blk, nb)
            dqs, dks, dvs = [], [], []
            for hk in range(N_KV):
                q4, ks, s = _attn_scores(q, k_refs, hk, masks)
                vs = [r[:, KV_W + hk * HEAD:KV_W + (hk + 1) * HEAD] for r in k_refs]
                lse4 = jnp.concatenate([lse_all[:, hk * GROUP + g:hk * GROUP + g + 1] for g in range(GROUP)], axis=0)
                do4 = _stack_heads(do, hk)
                delta = jnp.sum(do4.astype(F32) * _stack_heads(out, hk).astype(F32), axis=-1, keepdims=True)
                p = [jnp.exp((t - lse4).astype(BF16)) for t in s]
                ds = [t * (_dot_nt(do4, v) - delta).astype(BF16) for t, v in zip(p, vs)]
                dq4 = _dot(ds[0], ks[0])
                for t, k in zip(ds[1:], ks[1:]):
                    dq4 = dq4 + _dot(t, k)
                dq4 = dq4 * SCALE
                dqs += [dq4[g * BLK:(g + 1) * BLK, :] for g in range(GROUP)]
                dks.append([_dot_tn(t, q4) for t in ds])
                dvs.append([_dot_tn(t, do4) for t in p])
                ps = jnp.exp(_sink_rows(sink_ref, hk) - lse4) * delta
                for g in range(GROUP):
                    part = -jnp.sum(ps[g * BLK:(g + 1) * BLK, :], axis=0, keepdims=True)
                    dsink_ref[0:1, :] += jnp.where(lane == hk * GROUP + g, part, 0.0)
            dq_ref[rs, :] = jnp.concatenate(dqs, axis=1)

            def piece(i):
                return jnp.concatenate([dks[0][i], dks[1][i], dvs[0][i], dvs[1][i]], axis=1)

            dkvc_ref[...] += piece(0)
            starts = (jnp.maximum(blk - 1, 0), blk, jnp.minimum(blk + 1, nb - 1))
            for i, st in enumerate(starts):
                r = pl.ds(pl.multiple_of(st * BLK, BLK), BLK)
                dkv_ref[r, :] += piece(i + 1)
        _host_finish(n, steps - 1, comm)

    tq = Q_BLOCKS * BLK
    out = pl.pallas_call(
        body, name="attn_bwd", grid=(steps,),
        in_specs=[_rows(tq, Q_W)] + _kv_specs(nb, Q_BLOCKS) + [_acc((n_ctx, 2 * KV_W)), pl.BlockSpec(memory_space=pltpu.SMEM),
                                                     _rows(tq, Q_W), _rows(tq, Q_W), _rows(tq, LANES)] + _comm_specs(ns),
        out_specs=[_rows(tq, Q_W), _acc((seq, 2 * KV_W)), _acc((n_ctx, 2 * KV_W)), _acc((8, LANES))] + _comm_specs(ns),
        out_shape=[jax.ShapeDtypeStruct((seq, Q_W), F32), jax.ShapeDtypeStruct((seq, 2 * KV_W), F32),
                   jax.ShapeDtypeStruct((n_ctx, 2 * KV_W), F32), jax.ShapeDtypeStruct((8, LANES), F32)]
        + [jax.ShapeDtypeStruct(v.shape, v.dtype) for v in scatter],
        scratch_shapes=_comm_scratch(ns) if ns else [],
        compiler_params=_cp("arbitrary"),
    )(q, *([kv] * nkv), kvc, sink, dya, ya, lse, *scatter)
    return out[:4], out[4:]


def _proj_bwd(dq, dkv, dpb, x, dr1, modx, w_in, cos, sin, tm, scatter=()):
    seq = x.shape[0]
    pw = IN_W - Q_W - 2 * KV_W
    ns = len(scatter)

    def body(dq_ref, dkv_ref, dpb_ref, x_ref, dr1_ref, mod_ref, w_ref, cos_ref, sin_ref, *rest):
        dqkv_ref, gx_ref, st_ref = rest[ns:ns + 3]
        comm = _AllToAll(rest[:ns], rest[ns + 3:2 * ns + 3], *rest[2 * ns + 3:]) if ns else None
        _host_start(pl.program_id(0), comm)

        @pl.when(pl.program_id(0) == 0)
        def _():
            st_ref[...] = jnp.zeros_like(st_ref)

        cos1, sin1 = cos_ref[...], sin_ref[...]
        cos2 = jnp.concatenate([cos1, cos1], axis=1)
        sin2 = jnp.concatenate([sin1, sin1], axis=1)
        for j in range(Q_W // 256):
            cs = slice(256 * j, 256 * (j + 1))
            dqkv_ref[:, cs] = _unrope(dq_ref[:, cs], cos2, sin2).astype(BF16)
        dqkv_ref[:, Q_W:Q_W + KV_W] = _unrope(dkv_ref[:, :KV_W], cos1, sin1).astype(BF16)
        dqkv_ref[:, Q_W + KV_W:] = dkv_ref[:, KV_W:].astype(BF16)
        o = Q_W + 2 * KV_W
        dh = _dot(dqkv_ref[...], w_ref[:o, :]) + _dot(dpb_ref[...], w_ref[o:, :])
        xhat, rstd = _ln(x_ref[...])
        st_ref[0:1, :] += _colsum(dh)
        st_ref[1:2, :] += _colsum(dh * xhat)
        gx_ref[...] = _ln_bwd(dh * (1.0 + mod_ref[1:2, :]), xhat, rstd) + ALPHA * dr1_ref[...]
        _host_finish(pl.program_id(0), seq // tm - 1, comm)

    out = pl.pallas_call(
        body, name="proj_bwd", grid=(seq // tm,),
        in_specs=[_rows(tm, Q_W), _rows(tm, 2 * KV_W), _rows(tm, pw), _rows(tm, D), _rows(tm, D), _acc((8, D)),
                  _resident((IN_W, D)), _rows(tm, LANES), _rows(tm, LANES)] + _comm_specs(ns),
        out_specs=[_rows(tm, Q_W + 2 * KV_W), _rows(tm, D), _acc((8, D))] + _comm_specs(ns),
        out_shape=[jax.ShapeDtypeStruct((seq, Q_W + 2 * KV_W), BF16), jax.ShapeDtypeStruct((seq, D), F32),
                   jax.ShapeDtypeStruct((8, D), F32)] + [jax.ShapeDtypeStruct(v.shape, v.dtype) for v in scatter],
        scratch_shapes=_comm_scratch(ns) if ns else [],
        compiler_params=_cp("arbitrary"),
    )(dq, dkv, dpb, x, dr1, modx, w_in, cos, sin, *scatter)
    return out[:3], out[3:]


def _ctx_bwd(dkvc, ctx, hc, w_kv):
    n_ctx = ctx.shape[0]

    def body(dkvc_ref, ctx_ref, hc_ref, w_ref, dw_ref, st_ref):
        d = dkvc_ref[...].astype(BF16)
        dw_ref[...] = _dot_tn(d, hc_ref[...])
        dhc = _dot(d, w_ref[...])
        xhat, _ = _ln(ctx_ref[...])
        st_ref[...] = jnp.zeros_like(st_ref)
        st_ref[0:1, :] = _colsum(dhc)
        st_ref[1:2, :] = _colsum(dhc * xhat)

    return pl.pallas_call(
        body, name="ctx_bwd", grid=(1,),
        in_specs=[_acc((n_ctx, 2 * KV_W)), _acc((n_ctx, D)), _acc((n_ctx, D)), _acc((2 * KV_W, D))],
        out_specs=[_acc((2 * KV_W, D)), _acc((8, D))],
        out_shape=[jax.ShapeDtypeStruct((2 * KV_W, D), F32), jax.ShapeDtypeStruct((8, D), F32)],
        compiler_params=_cp("arbitrary"),
    )(dkvc, ctx, hc, w_kv)


def _tn_matmul(a, b, tn, name, out_dtype, shard_major=False, init=None, tk=512, scatter=(), gather=()):
    t, ka = a.shape
    n = b.shape[1]
    tk = min(tk, t)
    nk = t // tk
    nj = n // tn
    has_init = init is not None
    assert not (scatter and gather)
    moved = list(scatter) + list(gather)
    pattern = _AllToAll if scatter else _Gather
    ns = len(moved)
    n_in = 3 if has_init else 2

    def body(*refs):
        a_ref, b_ref = refs[:2]
        i_ref = refs[2] if has_init else None
        rest = refs[n_in:]
        o_ref = rest[ns]
        acc_ref = rest[2 * ns + 1]
        comm = pattern(rest[:ns], rest[ns + 1:2 * ns + 1], *rest[2 * ns + 2:]) if ns else None
        k = pl.program_id(1)
        step = pl.program_id(0) * nk + k
        _host_start(step, comm)

        @pl.when(k == 0)
        def _():
            acc_ref[...] = i_ref[...] if has_init else jnp.zeros_like(acc_ref)

        acc_ref[...] += _dot_tn(a_ref[...], b_ref[...])

        @pl.when(k == nk - 1)
        def _():
            o_ref[...] = acc_ref[...].astype(out_dtype)

        _host_finish(step, nj * nk - 1, comm, forward_at=(3 * nj * nk) // 4 if gather else None)

    in_specs = [pl.BlockSpec((tk, ka), lambda j, k: (k, 0)), pl.BlockSpec((tk, tn), lambda j, k: (k, j))]
    args = [a, b]
    if has_init:
        in_specs.append(pl.BlockSpec((ka, tn), lambda j, k: (0, j)))
        args.append(init)
    if shard_major:
        out_spec = pl.BlockSpec((None, ka, tn), lambda j, k: (j, 0, 0))
        out_shape = jax.ShapeDtypeStruct((nj, ka, tn), out_dtype)
    else:
        out_spec = pl.BlockSpec((ka, tn), lambda j, k: (0, j))
        out_shape = jax.ShapeDtypeStruct((ka, n), out_dtype)
    out = pl.pallas_call(
        body, name=name, grid=(nj, nk), in_specs=in_specs + _comm_specs(ns), out_specs=[out_spec] + _comm_specs(ns),
        out_shape=[out_shape] + [jax.ShapeDtypeStruct(v.shape, v.dtype) for v in scatter] + _gathered_shapes(gather),
        scratch_shapes=[pltpu.VMEM((ka, tn), F32)] + (_comm_scratch(ns) if ns else []),
        compiler_params=_cp("arbitrary", "arbitrary"),
    )(*args, *moved)
    return (out[0], out[1:]) if ns else out[0]


ADA_TILE = 512


def _gather_and_modulate(c_rows, c_ctx, w_half, w_ada):
    cs = w_ada.shape[1]
    vmem = pl.BlockSpec(memory_space=pltpu.VMEM)

    def body(c_ref, cctx_ref, wh_ref, wada_ref, cg_ref, wg_ref, mod_ref, modg_ref, sc_ref,
             c_v, wada_v, mod_v, send_c, recv_c, send_w, recv_w, send_m, recv_m, local):
        gc = _Gather([c_ref], [cg_ref], send_c, recv_c)
        gw = _Gather([wh_ref], [wg_ref], send_w, recv_w)
        gm = _Gather([mod_ref], [modg_ref], send_m, recv_m)
        px, py, pc = _my_pos()
        mine = 4 * px + 2 * py + pc
        gc.start()
        gw.start()
        own_c = pltpu.make_async_copy(c_ref, cg_ref.at[mine], local.at[0])
        load_w = pltpu.make_async_copy(wada_ref, wada_v, local.at[1])
        own_c.start()
        load_w.start()
        gc.finish()
        own_c.wait()
        load_c = pltpu.make_async_copy(cg_ref, c_v, local.at[2])
        load_c.start()
        load_c.wait()
        cc = jnp.concatenate([c_v[i, 0:1, :] for i in range(N_DEV)] + [cctx_ref[...], jnp.zeros((7, D), F32)], axis=0)
        sc = cc * _sig(cc)
        sc_ref[...] = sc
        load_w.wait()
        for j in range(cs // ADA_TILE):
            cols = slice(j * ADA_TILE, (j + 1) * ADA_TILE)
            mod_v[:, cols] = _dot(sc.astype(BF16), wada_v[:, cols].astype(BF16))
        store_m = pltpu.make_async_copy(mod_v, mod_ref, local.at[3])
        store_m.start()
        store_m.wait()
        own_m = pltpu.make_async_copy(mod_ref, modg_ref.at[mine], local.at[4])
        own_m.start()
        gm.start()
        gm.finish()
        own_m.wait()
        gw.finish()

    any_ = pl.BlockSpec(memory_space=pl.ANY)
    return pl.pallas_call(
        body, name="gather_and_modulate",
        in_specs=[any_, vmem, any_, any_], out_specs=[any_, any_, any_, any_, vmem],
        scratch_shapes=[pltpu.VMEM((N_DEV, 8, D), F32), pltpu.VMEM((D, cs), F32), pltpu.VMEM((16, cs), F32)]
        + _comm_scratch(1) + _comm_scratch(1) + _comm_scratch(1) + [pltpu.SemaphoreType.DMA((5,))],
        out_shape=[jax.ShapeDtypeStruct((N_DEV, 8, D), F32), jax.ShapeDtypeStruct((N_DEV,) + w_half.shape, w_half.dtype),
                   jax.ShapeDtypeStruct((16, cs), F32), jax.ShapeDtypeStruct((N_DEV, 16, cs), F32),
                   jax.ShapeDtypeStruct((16, D), F32)],
        compiler_params=pltpu.CompilerParams(vmem_limit_bytes=VMEM_LIMIT),
    )(c_rows, c_ctx, w_half, w_ada)


def _ada_bwd(sc_all_t, dm_all, dmc, w_ada):
    cs = w_ada.shape[1]

    def body(st_ref, dm_ref, dmc_ref, w_ref, gw_ref, part_ref):
        @pl.when(pl.program_id(0) == 0)
        def _():
            part_ref[...] = jnp.zeros_like(part_ref)

        gw_ref[...] = _dot(st_ref[...].astype(BF16), dm_ref[...].astype(BF16))
        part_ref[...] += _dot_nt(dmc_ref[...].astype(BF16), w_ref[...].astype(BF16))

    return pl.pallas_call(
        body, name="ada_bwd", grid=(cs // ADA_TILE,),
        in_specs=[_acc((D, 16)), pl.BlockSpec((16, ADA_TILE), lambda j: (0, j)), pl.BlockSpec((8, ADA_TILE), lambda j: (0, j)),
                  pl.BlockSpec((D, ADA_TILE), lambda j: (0, j))],
        out_specs=[pl.BlockSpec((D, ADA_TILE), lambda j: (0, j)), _acc((8, D))],
        out_shape=[jax.ShapeDtypeStruct((D, cs), F32), jax.ShapeDtypeStruct((8, D), F32)],
        compiler_params=_cp("arbitrary"),
    )(sc_all_t, dm_all, dmc, w_ada)


def _sum8(x, name, tr=256):
    _, r, c = x.shape
    tr = min(tr, r)
    while r % tr:
        tr -= 16

    def body(x_ref, o_ref):
        acc = x_ref[0].astype(F32)
        for i in range(1, N_DEV):
            acc = acc + x_ref[i].astype(F32)
        o_ref[...] = acc

    return pl.pallas_call(
        body, name=name, grid=(r // tr,),
        in_specs=[pl.BlockSpec((N_DEV, tr, c), lambda i: (0, i, 0))],
        out_specs=pl.BlockSpec((tr, c), lambda i: (i, 0)),
        out_shape=jax.ShapeDtypeStruct((r, c), F32),
        compiler_params=_cp("arbitrary"),
    )(x)


def _sum_blocks(recv, src, me, name, tr=256):
    _, r, c = recv.shape
    tr = min(tr, r)
    while r % tr:
        tr -= 16

    def body(me_ref, recv_ref, own_ref, o_ref):
        acc = own_ref[...].astype(F32)
        for k in range(1, N_DEV):
            acc = acc + recv_ref[me_ref[0] ^ k].astype(F32)
        o_ref[...] = acc

    return pl.pallas_call(
        body, name=name,
        grid_spec=pltpu.PrefetchScalarGridSpec(
            num_scalar_prefetch=1, grid=(r // tr,),
            in_specs=[pl.BlockSpec((N_DEV, tr, c), lambda i, me_ref: (0, i, 0)),
                      pl.BlockSpec((None, tr, c), lambda i, me_ref: (me_ref[0], i, 0))],
            out_specs=pl.BlockSpec((tr, c), lambda i, me_ref: (i, 0))),
        out_shape=jax.ShapeDtypeStruct((r, c), F32),
        compiler_params=_cp("arbitrary"),
    )(me, recv, src)


def _sum8_many(xs, name):
    n = len(xs)

    def body(*refs):
        for x_ref, o_ref in zip(refs[:n], refs[n:]):
            acc = x_ref[0]
            for i in range(1, N_DEV):
                acc = acc + x_ref[i]
            o_ref[...] = acc

    vmem = pl.BlockSpec(memory_space=pltpu.VMEM)
    return pl.pallas_call(
        body, name=name, in_specs=[vmem] * n, out_specs=[vmem] * n,
        out_shape=[jax.ShapeDtypeStruct(v.shape[1:], v.dtype) for v in xs],
        compiler_params=pltpu.CompilerParams(vmem_limit_bytes=VMEM_LIMIT),
    )(*xs)


def _adam_update(w, g, m, v):
    nm = ADAM_B1 * m + (1.0 - ADAM_B1) * g
    nv = ADAM_B2 * v + (1.0 - ADAM_B2) * (g * g)
    m_hat = nm / (1.0 - ADAM_B1 ** ADAM_STEP)
    v_hat = nv / (1.0 - ADAM_B2 ** ADAM_STEP)
    return -ADAM_LR * (m_hat / (jnp.sqrt(v_hat) + ADAM_EPS) + ADAM_WD * w), nm, nv


ROW_LOSS, ROW_LN2_G, ROW_LN2_B, ROW_LN1_G, ROW_LN1_B = 0, 1, 2, 10, 11
ROWS_DMOD_X = (16, 17, 12, 9, 8, 3)
ROWS_DMOD_C = (24, 25)
SMALL = ("c_ctx", "b_ada", "attn_sink", "gmlp_ln_g", "gmlp_ln_b", "w_spatial", "b_spatial", "ln1_g", "ln1_b", "ln2_g", "ln2_b")


def _adamw_small(sums, dsc, w, m, v):
    n = len(SMALL)

    def body(*refs):
        st_ref, gm_ref, sk_ref, ws_ref, bs_ref, dsc_ref = refs[:6]
        w_refs = dict(zip(SMALL, refs[6:6 + n]))
        m_refs = dict(zip(SMALL, refs[6 + n:6 + 2 * n]))
        v_refs = dict(zip(SMALL, refs[6 + 2 * n:6 + 3 * n]))
        outs = refs[6 + 3 * n:]
        c = w_refs["c_ctx"][...]
        sg = _sig(c)
        dmod = [st_ref[r:r + 1, :] for r in ROWS_DMOD_X]
        dmod[0] = dmod[0] + st_ref[ROWS_DMOD_C[0]:ROWS_DMOD_C[0] + 1, :]
        dmod[1] = dmod[1] + st_ref[ROWS_DMOD_C[1]:ROWS_DMOD_C[1] + 1, :]
        grads = dict(
            c_ctx=dsc_ref[0:1, :] * (sg * (1.0 + c * (1.0 - sg))),
            b_ada=jnp.concatenate(dmod, axis=1),
            attn_sink=sk_ref[0:1, 0:N_KV * GROUP],
            gmlp_ln_g=gm_ref[0:1, :], gmlp_ln_b=gm_ref[1:2, :],
            w_spatial=ws_ref[...], b_spatial=bs_ref[...],
            ln1_g=st_ref[ROW_LN1_G:ROW_LN1_G + 1, :], ln1_b=st_ref[ROW_LN1_B:ROW_LN1_B + 1, :],
            ln2_g=st_ref[ROW_LN2_G:ROW_LN2_G + 1, :], ln2_b=st_ref[ROW_LN2_B:ROW_LN2_B + 1, :])
        for i, name in enumerate(SMALL):
            g = grads[name]
            d, nm, nv = _adam_update(w_refs[name][...], g, m_refs[name][...], v_refs[name][...])
            outs[i][...] = g
            outs[n + i][...] = d
            outs[2 * n + i][...] = nm
            outs[3 * n + i][...] = nv

    vmem = pl.BlockSpec(memory_space=pltpu.VMEM)
    args = list(sums) + [dsc] + [w[k] for k in SMALL] + [m[k] for k in SMALL] + [v[k] for k in SMALL]
    shapes = [jax.ShapeDtypeStruct(w[k].shape, F32) for k in SMALL]
    out = pl.pallas_call(
        body, name="adamw_small", in_specs=[vmem] * len(args), out_specs=[vmem] * (4 * n), out_shape=shapes * 4,
        compiler_params=pltpu.CompilerParams(vmem_limit_bytes=VMEM_LIMIT),
    )(*args)
    return [dict(zip(SMALL, out[i * n:(i + 1) * n])) for i in range(4)]


def _adamw_halves(w, mine, theirs, m, v, c_arr, name):
    r, c = w.shape
    tr = min(256, r // 2)
    while (r // 2) % tr:
        tr -= 8
    nt = (r // 2) // tr

    def body(c_ref, w_ref, mine_ref, theirs_ref, m_ref, v_ref, g_ref, d_ref, nm_ref, nv_ref):
        g = jnp.where(pl.program_id(0) == c_ref[0], mine_ref[...], theirs_ref[...])
        g_ref[...] = g
        d_ref[...], nm_ref[...], nv_ref[...] = _adam_update(w_ref[...], g, m_ref[...], v_ref[...])

    whole = pl.BlockSpec((tr, c), lambda hb, i, c_ref: (hb * nt + i, 0))
    mine_spec = pl.BlockSpec((tr, c), lambda hb, i, c_ref: (jnp.where(hb == c_ref[0], i, 0), 0))
    theirs_spec = pl.BlockSpec((tr, c), lambda hb, i, c_ref: (jnp.where(hb == c_ref[0], 0, i), 0))
    shp = jax.ShapeDtypeStruct((r, c), F32)
    return pl.pallas_call(
        body, name=name,
        grid_spec=pltpu.PrefetchScalarGridSpec(
            num_scalar_prefetch=1, grid=(2, nt), in_specs=[whole, mine_spec, theirs_spec, whole, whole],
            out_specs=[whole] * 4),
        out_shape=[shp] * 4,
        compiler_params=_cp("arbitrary", "arbitrary"),
    )(c_arr, w, mine, theirs, m, v)


def _adamw(w, g, m, v, name):
    r, c = w.shape
    tr = r if r * c <= 256 * 1024 else min(256, r)
    while r % tr:
        tr -= 8

    def body(w_ref, g_ref, m_ref, v_ref, d_ref, nm_ref, nv_ref):
        d_ref[...], nm_ref[...], nv_ref[...] = _adam_update(w_ref[...], g_ref[...], m_ref[...], v_ref[...])

    spec = pl.BlockSpec((tr, c), lambda i: (i, 0))
    shp = jax.ShapeDtypeStruct((r, c), F32)
    return pl.pallas_call(
        body, name=name, grid=(r // tr,), in_specs=[spec] * 4, out_specs=[spec] * 3, out_shape=[shp] * 3,
        compiler_params=_cp("arbitrary"),
    )(w, g, m, v)


def _my_pos():
    return lax.axis_index("x"), lax.axis_index("y"), lax.axis_index("c")


N_COPY = 7


class _Gather:
    def __init__(self, x_refs, out_refs, send_sems, recv_sems):
        self.x_refs, self.out_refs = x_refs, out_refs
        self.send_sems, self.recv_sems = send_sems, recv_sems
        x, y, c = _my_pos()
        self.c = c
        self.me, self.sibling = (x, y, c), (x, y, 1 - c)
        self.chips = [(1 - x, y), (x, 1 - y), (1 - x, 1 - y)]

    def _copy(self, a, k, block, to, from_input=False):
        px, py, pc = block
        rows = self.out_refs[a].at[4 * px + 2 * py + pc]
        return pltpu.make_async_remote_copy(
            src_ref=self.x_refs[a] if from_input else rows, dst_ref=rows,
            send_sem=self.send_sems.at[a * N_COPY + k], recv_sem=self.recv_sems.at[a * N_COPY + k],
            device_id=to, device_id_type=MESH)

    def start(self):
        n = len(self.x_refs)
        for a in range(n):
            self._copy(a, 0, self.me, self.sibling, from_input=True).start()
        for j, chip in enumerate(self.chips):
            for a in range(n):
                self._copy(a, 1 + j, self.me, (*chip, self.c), from_input=True).start()

    def forward(self):
        c = self.c
        for j, chip in enumerate(self.chips):
            for a in range(len(self.x_refs)):
                self._copy(a, 1 + j, (*chip, c), self.me).wait_recv()
                self._copy(a, 4 + j, (*chip, c), self.sibling).start()

    def finish(self):
        self.forward()
        self.drain()

    def drain(self):
        n = len(self.x_refs)
        c = self.c
        for a in range(n):
            self._copy(a, 0, self.sibling, self.me).wait_recv()
        for j, chip in enumerate(self.chips):
            for a in range(n):
                self._copy(a, 4 + j, (*chip, 1 - c), self.me).wait_recv()
        for a in range(n):
            self._copy(a, 0, self.me, self.sibling, from_input=True).wait_send()
            for j, chip in enumerate(self.chips):
                self._copy(a, 1 + j, self.me, (*chip, c), from_input=True).wait_send()
                self._copy(a, 4 + j, (*chip, c), self.sibling).wait_send()


def _comm_scratch(n):
    return [pltpu.SemaphoreType.DMA((n * N_COPY,)), pltpu.SemaphoreType.DMA((n * N_COPY,))]


def _comm_specs(n):
    return [pl.BlockSpec(memory_space=pl.ANY)] * n


def _gathered_shapes(xs):
    return [jax.ShapeDtypeStruct((N_DEV,) + v.shape, v.dtype) for v in xs]


def _with_own(gathered, xs, me):
    return [lax.dynamic_update_index_in_dim(g, v, me, 0) for g, v in zip(gathered, xs)]


def _all_gather(xs, me, name):
    n = len(xs)

    def body(*refs):
        g = _Gather(refs[:n], refs[n:2 * n], *refs[2 * n:])
        g.start()
        g.finish()

    out = pl.pallas_call(
        body, name=name, out_shape=_gathered_shapes(xs), in_specs=_comm_specs(n), out_specs=_comm_specs(n),
        scratch_shapes=_comm_scratch(n),
    )(*xs)
    return _with_own(out, xs, me)


class _AllToAll:
    def __init__(self, x_refs, out_refs, send_sems, recv_sems):
        self.x_refs, self.out_refs = x_refs, out_refs
        self.send_sems, self.recv_sems = send_sems, recv_sems
        self.pos = _my_pos()
        x, y, c = self.pos
        self.me = 4 * x + 2 * y + c

    def _peer(self, k):
        x, y, c = self.pos
        return (x ^ ((k >> 2) & 1), y ^ ((k >> 1) & 1), c ^ (k & 1))

    def _copy(self, a, k):
        p = self._peer(k)
        return pltpu.make_async_remote_copy(
            src_ref=self.x_refs[a].at[4 * p[0] + 2 * p[1] + p[2]], dst_ref=self.out_refs[a].at[self.me],
            send_sem=self.send_sems.at[a * N_COPY + k - 1], recv_sem=self.recv_sems.at[a * N_COPY + k - 1],
            device_id=p, device_id_type=MESH)

    def start(self):
        for k in range(1, N_DEV):
            for a in range(len(self.x_refs)):
                self._copy(a, k).start()

    def finish(self):
        for a in range(len(self.x_refs)):
            for k in range(1, N_DEV):
                self._copy(a, k).wait_recv()
            for k in range(1, N_DEV):
                self._copy(a, k).wait_send()


def _sibling_exchange(xs, name):
    n = len(xs)

    def body(*refs):
        x_refs, out_refs = refs[:n], refs[n:2 * n]
        send_sems, recv_sems = refs[2 * n:]
        x, y, c = _my_pos()

        def push(a):
            return pltpu.make_async_remote_copy(
                src_ref=x_refs[a], dst_ref=out_refs[a], send_sem=send_sems.at[a], recv_sem=recv_sems.at[a],
                device_id=(x, y, 1 - c), device_id_type=MESH)

        for a in range(n):
            push(a).start()
        for a in range(n):
            push(a).wait_recv()
            push(a).wait_send()

    return pl.pallas_call(
        body, name=name, out_shape=[jax.ShapeDtypeStruct(v.shape, v.dtype) for v in xs],
        in_specs=_comm_specs(n), out_specs=_comm_specs(n),
        scratch_shapes=[pltpu.SemaphoreType.DMA((n,)), pltpu.SemaphoreType.DMA((n,))],
    )(*xs)


def _scatter_and_gather(scatter, gather, name):
    ns, ng = len(scatter), len(gather)

    def body(*refs):
        s_in, g_in = refs[:ns], refs[ns:ns + ng]
        s_out, g_out = refs[ns + ng:2 * ns + ng], refs[2 * ns + ng:2 * (ns + ng)]
        s_send, s_recv, g_send, g_recv = refs[2 * (ns + ng):]
        g = _Gather(g_in, g_out, g_send, g_recv)
        t = _AllToAll(s_in, s_out, s_send, s_recv)
        g.start()
        t.start()
        g.finish()
        t.finish()

    out = pl.pallas_call(
        body, name=name,
        out_shape=[jax.ShapeDtypeStruct(v.shape, v.dtype) for v in scatter] + _gathered_shapes(gather),
        in_specs=_comm_specs(ns + ng), out_specs=_comm_specs(ns + ng),
        scratch_shapes=_comm_scratch(ns) + _comm_scratch(ng),
    )(*scatter, *gather)
    return out[:ns], out[ns:]


def _row_tile(seq, want):
    return min(want, seq)


def _local_step(x, ctx, tgt, mod_x, mod_c, wb, sink, gmlp_g, gmlp_b, w_s, b_s, ln1_g, ln1_b, ln2_g, ln2_b,
                later=None, me=None):
    seq = x.shape[0]
    on_mesh = me is not None
    modx1 = jnp.concatenate([mod_x[0:2], jnp.zeros((6, D), F32)], axis=0)
    modc = jnp.concatenate([mod_c[0:2], jnp.zeros((6, D), F32)], axis=0)
    vec = jnp.concatenate([mod_x[2:3], ln1_g, ln1_b, mod_x[3:6], ln2_g, ln2_b], axis=0)
    gp = jnp.concatenate([gmlp_g, gmlp_b, jnp.zeros((6, G_W), F32)], axis=0)
    ws_stack = w_s.reshape(N_GRP * BLK, BLK).astype(BF16)
    ws_stack_t = jnp.transpose(w_s, (2, 0, 1)).reshape(BLK, N_GRP * BLK).astype(BF16)
    bias_full = jnp.repeat(b_s.T, GRP_D, axis=1)
    cos, sin = _rope_tables(seq)
    w_in = wb["w_in"]
    w_kv = w_in[Q_W:Q_W + 2 * KV_W, :]
    tm_big = _row_tile(seq, 512)
    tm_ffn = _row_tile(seq, 256)

    hc, kvc, vac = _ctx_fwd(ctx, modc, w_kv)
    behind_proj = ("w_a", "w_b", "w_o") if on_mesh else ()
    behind_attn = ("w_fi",) if on_mesh else ()
    behind_mix = ("w_fo",) if on_mesh else ()
    wb = dict(wb)

    def whole(names, gathered):
        for n, g in zip(names, _with_own(list(gathered), [later[n] for n in names], me)):
            wb[n] = g.reshape(-1, g.shape[2]) if n in ROW_SHARDED else g.reshape(N_SHARD, 2 * g.shape[1], g.shape[2])

    (h, q, kv, va, uv, gab), got = _proj_fwd(x, modx1, w_in, cos, sin, _row_tile(seq, 1024), gather=[later[n] for n in behind_proj])
    whole(behind_proj, got)
    if on_mesh:
        for n in ("w_a", "w_b"):
            wb[n] = wb[n].transpose(1, 0, 2).reshape(wb[n].shape[1], D)
    (ya, lse), got = _attn_fwd(q, kv, va, kvc, vac, sink, gather=[later[n] for n in behind_attn])
    whole(behind_attn, got)
    (a, b, mix, merged, yb), got = _mix_fwd(uv, gab, ya, gp, ws_stack, bias_full, wb["w_a"], wb["w_b"], wb["w_o"], tm_big,
                                            gather=[later[n] for n in behind_mix])
    whole(behind_mix, got)
    act, h2, dff, df, dr1, st_ffn = _ffn(x, mix, tgt, vec, wb["w_fi"], wb["w_fo"], tm_ffn)
    blocks, recv = {}, {}
    blocks["w_fo"] = _eighths(_tn_matmul(act, df, 512, "tn_w_ffn_out", BF16, tk=2048))
    if on_mesh:
        g_w_fi, (recv["w_fo"],) = _tn_matmul(h2, dff, FH_SHARD, "tn_w_ffn_in", BF16, shard_major=True, tk=2048,
                                            scatter=[blocks["w_fo"]])
    else:
        g_w_fi = _tn_matmul(h2, dff, FH_SHARD, "tn_w_ffn_in", BF16, shard_major=True, tk=2048)
    blocks["w_fi"] = _eighths(g_w_fi)
    (dya, dpb, dws, dbs_full, st4, g_w_o, g_w_a, g_w_b), got = _mix_bwd(
        dr1, a, b, gab, uv, merged, ya, yb, vec, gp, ws_stack, ws_stack_t, bias_full, wb["w_a"], wb["w_b"], wb["w_o"],
        tm_big, scatter=[blocks["w_fi"]] if on_mesh else ())
    recv.update(zip(("w_fi",), got))
    shard_major = [g.reshape(g.shape[0], N_SHARD, D // N_SHARD).transpose(1, 0, 2) for g in (g_w_a, g_w_b)]
    blocks.update(w_o=_eighths(g_w_o), w_a=_eighths(shard_major[0]), w_b=_eighths(shard_major[1]))
    mixer = ("w_o", "w_a", "w_b") if on_mesh else ()
    (dq, dkv, dkvc, dsink), got = _attn_bwd(q, kv, kvc, sink, dya, ya, lse, scatter=[blocks[n] for n in mixer])
    recv.update(zip(mixer, got))
    g_wkv_ctx, st0 = _ctx_bwd(dkvc, ctx, hc, w_kv)
    (dqkv, grad_x, st1), _ = _proj_bwd(dq, dkv, dpb, x, dr1, modx1, w_in, cos, sin, tm_big)
    dbs = jnp.sum(dbs_full.reshape(BLK, N_GRP, GRP_D), axis=2).T
    early = [jnp.concatenate([st_ffn, st0], axis=0), st4, dsink, dws, dbs]
    init = jnp.pad(g_wkv_ctx, ((Q_W, 0), (0, 0)))
    g_qkv = _tn_matmul(dqkv, h, D, "tn_w_in_qkv", BF16, init=init, tk=1024)
    if on_mesh:
        g_rest, early_gathered = _tn_matmul(dpb, h, D, "tn_w_in_rest", BF16, tk=1024, gather=early)
    else:
        g_rest, early_gathered = _tn_matmul(dpb, h, D, "tn_w_in_rest", BF16, tk=1024), None
    blocks["w_in"] = _eighths(jnp.concatenate([g_qkv, g_rest], axis=0))
    return grad_x, dict(early=early, early_gathered=early_gathered, late=st1), blocks, recv


BIG = ("w_in", "w_a", "w_b", "w_o", "w_fi", "w_fo")
ROW_SHARDED = ("w_o", "w_fo")


def _half_of_shard(shard, c):
    r = shard.shape[0]
    return lax.dynamic_slice_in_dim(shard, c * (r // 2), r // 2, axis=0)


def _eighths(v):
    rows = v.shape[-2] * (v.shape[0] if v.ndim == 3 else 1)
    return v.reshape(N_DEV, rows // N_DEV, v.shape[-1])


def kernel(x, c, ctx, c_ctx, w_ada, b_ada, w_in, attn_sink, gmlp_ln_g, gmlp_ln_b, w_spatial, b_spatial, w_branch_a, w_branch_b, w_out, ln1_g, ln1_b, w_ffn_in, w_ffn_out, ln2_g, ln2_b, loss_target, m_c_ctx, m_w_ada, m_b_ada, m_w_in, m_attn_sink, m_gmlp_ln_g, m_gmlp_ln_b, m_w_spatial, m_b_spatial, m_w_branch_a, m_w_branch_b, m_w_out, m_ln1_g, m_ln1_b, m_w_ffn_in, m_w_ffn_out, m_ln2_g, m_ln2_b, v_c_ctx, v_w_ada, v_b_ada, v_w_in, v_attn_sink, v_gmlp_ln_g, v_gmlp_ln_b, v_w_spatial, v_b_spatial, v_w_branch_a, v_w_branch_b, v_w_out, v_ln1_g, v_ln1_b, v_w_ffn_in, v_w_ffn_out, v_ln2_g, v_ln2_b):
    mx, my, mc = _my_pos()
    me = 4 * mx + 2 * my + mc
    chip = 2 * mx + my
    shards = dict(w_in=w_in[0].T, w_a=w_branch_a[0], w_b=w_branch_b[0], w_o=w_out[0], w_fi=w_ffn_in[0], w_fo=w_ffn_out[0])

    halves = {n: _half_of_shard(shards[n], mc).astype(BF16) for n in BIG}
    c_rows = jnp.concatenate([c, jnp.zeros((7, D), F32)], axis=0)
    _, g_in, _, mod_g, sc_all = _gather_and_modulate(c_rows, c_ctx[None, :], halves["w_in"], w_ada[0])
    wb = dict(w_in=_with_own([g_in], [halves["w_in"]], me)[0].reshape(IN_W, D))
    mod_all = jnp.concatenate([mod_g[2 * s] for s in range(4)], axis=1) + b_ada
    mod_x = lax.dynamic_slice_in_dim(mod_all, me, 1, axis=0).reshape(6, D)
    mod_c = mod_all[8].reshape(6, D)[0:2]

    grad_x, small, blocks, recv = _local_step(
        x[0], ctx[0], loss_target[0], mod_x, mod_c, wb, attn_sink, gmlp_ln_g, gmlp_ln_b, w_spatial[0], b_spatial[0],
        ln1_g, ln1_b, ln2_g, ln2_b, later=halves, me=me)

    (recv["w_in"],), late = _scatter_and_gather([blocks["w_in"]], [small["late"]], "scatter_w_in_gather_small")
    late = _with_own(late, [small["late"]], me)[0]
    gathered = _with_own(small["early_gathered"], small["early"], me)
    gathered[0] = jnp.concatenate([gathered[0][:, :16], late, gathered[0][:, 16:]], axis=1)

    me_arr = jnp.reshape(me, (1,)).astype(jnp.int32)
    summed = {n: _sum_blocks(recv[n], blocks[n], me_arr, "sum_grads_" + n) for n in BIG}
    theirs = dict(zip(BIG, _sibling_exchange([summed[n] for n in BIG], "exchange_grads")))

    sums = _sum8_many(gathered, "sum_small")
    stats = sums[0]
    loss = 0.5 * jnp.sum(stats[ROW_LOSS]) / D
    dmod_x_all = jnp.concatenate([gathered[0][:, r_, :] for r_ in ROWS_DMOD_X], axis=1)
    dmod_c_full = jnp.concatenate([stats[r_] for r_ in ROWS_DMOD_C] + [jnp.zeros((4 * D,), F32)])
    dm_rows = jnp.concatenate([dmod_x_all, dmod_c_full[None, :], jnp.zeros((7, 6 * D), F32)], axis=0)
    cs = w_ada.shape[2]
    dm_shard = lax.dynamic_slice_in_dim(dm_rows, chip * cs, cs, axis=1)
    dmc_shard = jnp.concatenate([dm_shard[8:9], jnp.zeros((7, cs), F32)], axis=0)
    g_w_ada, part = _ada_bwd(sc_all.T, dm_shard, dmc_shard, w_ada[0])
    part_all = _all_gather([part * (mc == 0).astype(F32)], me, "gather_c_ctx")[0]
    dsc = _sum8(part_all, "sum_c_ctx")

    grads = dict(w_ada=g_w_ada[None])
    weights = dict(c_ctx=c_ctx, w_ada=w_ada, b_ada=b_ada, w_in=w_in, attn_sink=attn_sink, gmlp_ln_g=gmlp_ln_g,
                   gmlp_ln_b=gmlp_ln_b, w_spatial=w_spatial, b_spatial=b_spatial, w_branch_a=w_branch_a,
                   w_branch_b=w_branch_b, w_out=w_out, ln1_g=ln1_g, ln1_b=ln1_b, w_ffn_in=w_ffn_in, w_ffn_out=w_ffn_out,
                   ln2_g=ln2_g, ln2_b=ln2_b)
    ms = dict(c_ctx=m_c_ctx, w_ada=m_w_ada, b_ada=m_b_ada, w_in=m_w_in, attn_sink=m_attn_sink, gmlp_ln_g=m_gmlp_ln_g,
              gmlp_ln_b=m_gmlp_ln_b, w_spatial=m_w_spatial, b_spatial=m_b_spatial, w_branch_a=m_w_branch_a,
              w_branch_b=m_w_branch_b, w_out=m_w_out, ln1_g=m_ln1_g, ln1_b=m_ln1_b, w_ffn_in=m_w_ffn_in,
              w_ffn_out=m_w_ffn_out, ln2_g=m_ln2_g, ln2_b=m_ln2_b)
    vs = dict(c_ctx=v_c_ctx, w_ada=v_w_ada, b_ada=v_b_ada, w_in=v_w_in, attn_sink=v_attn_sink, gmlp_ln_g=v_gmlp_ln_g,
              gmlp_ln_b=v_gmlp_ln_b, w_spatial=v_w_spatial, b_spatial=v_b_spatial, w_branch_a=v_w_branch_a,
              w_branch_b=v_w_branch_b, w_out=v_w_out, ln1_g=v_ln1_g, ln1_b=v_ln1_b, w_ffn_in=v_w_ffn_in,
              w_ffn_out=v_w_ffn_out, ln2_g=v_ln2_g, ln2_b=v_ln2_b)
    order = list(weights)
    delta, new_m, new_v = {}, {}, {}
    d_, m_, v_ = _adamw(w_ada[0], g_w_ada, m_w_ada[0], v_w_ada[0], "adamw_w_ada")
    delta["w_ada"], new_m["w_ada"], new_v["w_ada"] = d_[None], m_[None], v_[None]
    c_arr = jnp.reshape(mc, (1,)).astype(jnp.int32)
    names = dict(w_in="w_in", w_a="w_branch_a", w_b="w_branch_b", w_o="w_out", w_fi="w_ffn_in", w_fo="w_ffn_out")
    for k, n in names.items():
        flip = (lambda t: t.T) if k == "w_in" else (lambda t: t)
        outs = _adamw_halves(flip(weights[n][0]), summed[k], theirs[k], flip(ms[n][0]), flip(vs[n][0]), c_arr, "adamw_" + n)
        grads[n], delta[n], new_m[n], new_v[n] = [flip(t)[None] for t in outs]

    def view(a):
        return a.reshape(-1, a.shape[-1]) if a.ndim != 1 else a.reshape(1, -1)

    small = _adamw_small(sums, dsc, *[{n: view(d[n]) for n in SMALL} for d in (weights, ms, vs)])
    for out, src in zip((grads, delta, new_m, new_v), small):
        for n in SMALL:
            out[n] = src[n].reshape(weights[n].shape)

    return (loss, grad_x[None], *[grads[n] for n in order], *[delta[n] for n in order],
            *[new_m[n] for n in order], *[new_v[n] for n in order])
```

```python
import math

import jax
import jax.numpy as jnp
from jax import lax
from jax.experimental import pallas as pl
from jax.experimental.pallas import tpu as pltpu

F32 = jnp.float32
BF16 = jnp.bfloat16

D = 1024
HEAD = 64
N_KV = 2
GROUP = 4
Q_W = 512
KV_W = 128
G_W = 512
BLK = 128
N_GRP = 8
GRP_D = 64
FH = 2816
IN_W = 3840
GRID_W = 64
ROPE_BASE = 10000.0
LN_EPS = 1e-5
NEG = -1e30
ALPHA = (2 * 1) ** 0.25
SCALE = HEAD ** -0.5
GELU_K = math.sqrt(2.0 / math.pi)
GELU_A = 0.044715
ADAM_LR = 0.001
ADAM_B1 = 0.9
ADAM_B2 = 0.999
ADAM_EPS = 1e-08
ADAM_WD = 0.01
ADAM_STEP = 10
N_DEV = 8
N_SHARD = 4
FH_SHARD = FH // 2
LANES = 128
VMEM_LIMIT = 56 * 1024 * 1024
MESH = pl.DeviceIdType.MESH


def _cp(*sem):
    return pltpu.CompilerParams(dimension_semantics=sem, vmem_limit_bytes=VMEM_LIMIT)


def _resident(shape):
    return pl.BlockSpec(shape, lambda *_: (0,) * len(shape), pipeline_mode=pl.Buffered(1))


def _rows(tm, width):
    return pl.BlockSpec((tm, width), lambda i: (i, 0))


def _acc(shape):
    return pl.BlockSpec(shape, lambda *_: (0,) * len(shape))


def _dot(a, b):
    return jnp.dot(a, b, preferred_element_type=F32)


def _dot_nt(a, b):
    return lax.dot_general(a, b, (((1,), (1,)), ((), ())), preferred_element_type=F32)


def _dot_tn(a, b):
    return lax.dot_general(a, b, (((0,), (0,)), ((), ())), preferred_element_type=F32)


def _ln(x):
    mu = jnp.mean(x, axis=-1, keepdims=True)
    xc = x - mu
    var = jnp.mean(xc * xc, axis=-1, keepdims=True)
    rstd = lax.rsqrt(var + LN_EPS)
    return xc * rstd, rstd


def _ln_bwd(dxhat, xhat, rstd):
    return (dxhat - jnp.mean(dxhat, axis=-1, keepdims=True)
            - xhat * jnp.mean(dxhat * xhat, axis=-1, keepdims=True)) * rstd


def _sig(x):
    return 0.5 + 0.5 * jnp.tanh(0.5 * x)


def _gelu(x):
    t = jnp.tanh(x * (GELU_K + (GELU_K * GELU_A) * (x * x)))
    hx = 0.5 * x
    return hx + hx * t, t


def _gelu_grad(x, t):
    return 0.5 + 0.5 * t + (0.5 * x) * (1.0 - t * t) * (GELU_K + (3.0 * GELU_K * GELU_A) * (x * x))


def _colsum(v):
    return jnp.sum(v, axis=0, keepdims=True)


def _partner(x):
    w = x.shape[1]
    lane = lax.broadcasted_iota(jnp.int32, x.shape, 1)
    return jnp.where((lane & 31) < 16, pltpu.roll(x, w - 16, 1), pltpu.roll(x, 16, 1))


def _rope(x, cos, sin):
    return x * cos + _partner(x) * sin


def _unrope(g, cos, sin):
    return g * cos + _partner(g * sin)


def _rope_tables(seq):
    inv = ROPE_BASE ** (-jnp.arange(HEAD // 4, dtype=F32) / (HEAD // 4))
    pos = jnp.arange(seq, dtype=jnp.int32)
    ar = (pos // GRID_W).astype(F32)[:, None] * inv
    ac = (pos % GRID_W).astype(F32)[:, None] * inv
    cos = jnp.concatenate([jnp.cos(ar), jnp.cos(ar), jnp.cos(ac), jnp.cos(ac)], axis=-1)
    sin = jnp.concatenate([-jnp.sin(ar), jnp.sin(ar), -jnp.sin(ac), jnp.sin(ac)], axis=-1)
    return jnp.tile(cos, (1, LANES // HEAD)), jnp.tile(sin, (1, LANES // HEAD))


def _ctx_fwd(ctx, modc, w_kv):
    n_ctx = ctx.shape[0]

    def body(ctx_ref, mod_ref, w_ref, hc_ref, kvc_ref, vac_ref):
        xhat, _ = _ln(ctx_ref[...])
        hc = (xhat * (1.0 + mod_ref[1:2, :]) + mod_ref[0:1, :]).astype(BF16)
        hc_ref[...] = hc
        kvc = _dot_nt(hc, w_ref[...]).astype(BF16)
        kvc_ref[...] = kvc
        vac_ref[...] = _with_ones(kvc[:, KV_W:])

    return pl.pallas_call(
        body, name="ctx_fwd", grid=(1,),
        in_specs=[_acc((n_ctx, D)), _acc((8, D)), _acc((2 * KV_W, D))],
        out_specs=[_acc((n_ctx, D)), _acc((n_ctx, 2 * KV_W)), _acc((n_ctx, 2 * LANES))],
        out_shape=[jax.ShapeDtypeStruct((n_ctx, D), BF16), jax.ShapeDtypeStruct((n_ctx, 2 * KV_W), BF16),
                   jax.ShapeDtypeStruct((n_ctx, 2 * LANES), BF16)],
        compiler_params=_cp("arbitrary"),
    )(ctx, modc, w_kv)


def _host_start(step, comm):
    if comm is not None:
        @pl.when(step == 0)
        def _():
            comm.start()


def _host_finish(step, last, comm, forward_at=None):
    if comm is None:
        return
    if forward_at is None or forward_at >= last:
        @pl.when(step == last)
        def _():
            comm.finish()
    else:
        @pl.when(step == forward_at)
        def _():
            comm.forward()

        @pl.when(step == last)
        def _():
            comm.drain()


def _proj_fwd(x, modx, w_in, cos, sin, tm, gather=()):
    seq = x.shape[0]
    ng = len(gather)

    def body(x_ref, mod_ref, w_ref, cos_ref, sin_ref, *rest):
        h_ref, q_ref, kv_ref, va_ref, uv_ref, gab_ref = rest[ng:ng + 6]
        comm = _Gather(rest[:ng], rest[ng + 6:2 * ng + 6], *rest[2 * ng + 6:]) if ng else None
        _host_start(pl.program_id(0), comm)
        xhat, _ = _ln(x_ref[...])
        h = (xhat * (1.0 + mod_ref[1:2, :]) + mod_ref[0:1, :]).astype(BF16)
        h_ref[...] = h
        cos1, sin1 = cos_ref[...], sin_ref[...]
        cos2 = jnp.concatenate([cos1, cos1], axis=1)
        sin2 = jnp.concatenate([sin1, sin1], axis=1)
        for j in range(Q_W // 256):
            t = _dot_nt(h, w_ref[256 * j:256 * (j + 1), :])
            q_ref[:, 256 * j:256 * (j + 1)] = (_rope(t, cos2, sin2) * SCALE).astype(BF16)
        t = _dot_nt(h, w_ref[Q_W:Q_W + 2 * KV_W, :])
        kv_ref[:, :KV_W] = _rope(t[:, :KV_W], cos1, sin1).astype(BF16)
        v = t[:, KV_W:].astype(BF16)
        kv_ref[:, KV_W:] = v
        va_ref[...] = _with_ones(v)
        o = Q_W + 2 * KV_W
        for j in range(2):
            uv_ref[:, G_W * j:G_W * (j + 1)] = _dot_nt(h, w_ref[o + G_W * j:o + G_W * (j + 1), :]).astype(BF16)
        o += 2 * G_W
        for j in range(4):
            gab_ref[:, 512 * j:512 * (j + 1)] = _dot_nt(h, w_ref[o + 512 * j:o + 512 * (j + 1), :]).astype(BF16)
        _host_finish(pl.program_id(0), seq // tm - 1, comm, forward_at=(seq // tm) // 2)

    out = pl.pallas_call(
        body, name="proj_fwd", grid=(seq // tm,),
        in_specs=[_rows(tm, D), _acc((8, D)), _resident((IN_W, D)), _rows(tm, LANES), _rows(tm, LANES)] + _comm_specs(ng),
        out_specs=[_rows(tm, D), _rows(tm, Q_W), _rows(tm, 2 * KV_W), _rows(tm, 2 * LANES), _rows(tm, 2 * G_W),
                   _rows(tm, 2 * D)] + _comm_specs(ng),
        out_shape=[jax.ShapeDtypeStruct((seq, D), BF16), jax.ShapeDtypeStruct((seq, Q_W), BF16),
                   jax.ShapeDtypeStruct((seq, 2 * KV_W), BF16), jax.ShapeDtypeStruct((seq, 2 * LANES), BF16),
                   jax.ShapeDtypeStruct((seq, 2 * G_W), BF16), jax.ShapeDtypeStruct((seq, 2 * D), BF16)] + _gathered_shapes(gather),
        scratch_shapes=_comm_scratch(ng) if ng else [],
        compiler_params=_cp("arbitrary"),
    )(x, modx, w_in, cos, sin, *gather)
    return out[:6], out[6:]


def _stack_heads(x, hk):
    return jnp.concatenate([x[:, (hk * GROUP + g) * HEAD:(hk * GROUP + g + 1) * HEAD] for g in range(GROUP)], axis=0)


def _band_masks(n, nb):
    rows = GROUP * BLK
    qi = lax.broadcasted_iota(jnp.int32, (rows, BLK), 0) & (BLK - 1)
    kj = lax.broadcasted_iota(jnp.int32, (rows, BLK), 1)
    return (kj >= qi) & (n > 0), (kj <= qi) & (n < nb - 1)


def _attn_scores(q, k_refs, hk, masks):
    q4 = _stack_heads(q, hk)
    ks = [r[:, hk * HEAD:(hk + 1) * HEAD] for r in k_refs]
    s = [_dot_nt(q4, k) for k in ks]
    s[1] = jnp.where(masks[0], s[1], NEG)
    s[3] = jnp.where(masks[1], s[3], NEG)
    return q4, ks, s


def _sink_rows(sink_ref, hk):
    rows = GROUP * BLK
    rg = lax.broadcasted_iota(jnp.int32, (rows, 1), 0) >> 7
    sink_v = jnp.full((rows, 1), sink_ref[0, hk * GROUP], F32)
    for g in range(1, GROUP):
        sink_v = jnp.where(rg == g, sink_ref[0, hk * GROUP + g], sink_v)
    return sink_v


def _with_ones(v):
    ones = jnp.ones((v.shape[0], HEAD), v.dtype)
    return jnp.concatenate([v[:, :HEAD], ones, v[:, HEAD:], ones], axis=1)


def _kv_specs(nb, qb):
    def spec(d):
        return pl.BlockSpec((BLK, 2 * KV_W), lambda n: (jnp.clip(qb * n + d, 0, nb - 1), 0))
    return [spec(d) for d in range(-1, qb + 1)]


def _attn_fwd(q, kv, va, kvc, vac, sink, gather=()):
    seq = q.shape[0]
    nb = seq // BLK
    n_ctx = kvc.shape[0]
    ng = len(gather)
    Q_BLOCKS = 1
    nkv = Q_BLOCKS + 2
    steps = nb // Q_BLOCKS

    def body(q_ref, *rest):
        kv_refs, va_refs = rest[:nkv], rest[nkv:2 * nkv]
        kvc_ref, vac_ref, sink_ref = rest[2 * nkv:2 * nkv + 3]
        rest = rest[2 * nkv + 3:]
        o_ref, lse_ref = rest[ng:ng + 2]
        comm = _Gather(rest[:ng], rest[ng + 2:2 * ng + 2], *rest[2 * ng + 2:]) if ng else None
        n = pl.program_id(0)
        _host_start(n, comm)
        lane = lax.broadcasted_iota(jnp.int32, (BLK, LANES), 1)
        for sub in range(Q_BLOCKS):
            rs = slice(sub * BLK, (sub + 1) * BLK)
            q = q_ref[rs, :]
            outs = []
            lse_all = jnp.zeros((BLK, LANES), F32)
            masks = _band_masks(Q_BLOCKS * n + sub, nb)
            for hk in range(N_KV):
                _, _, s = _attn_scores(q, (kvc_ref,) + kv_refs[sub:sub + 3], hk, masks)
                sink_v = _sink_rows(sink_ref, hk)
                tile_max = s[1]
                for t in [s[0][:, i * LANES:(i + 1) * LANES] for i in range(n_ctx // LANES)] + s[2:]:
                    tile_max = jnp.maximum(tile_max, t)
                m = jnp.maximum(sink_v, jnp.max(tile_max, axis=-1, keepdims=True))
                o = jnp.zeros((GROUP * BLK, LANES), F32)
                for t, va_ref in zip(s, (vac_ref,) + va_refs[sub:sub + 3]):
                    o = o + _dot(jnp.exp((t - m).astype(BF16)), va_ref[:, hk * LANES:(hk + 1) * LANES])
                denom = o[:, HEAD:HEAD + 1] + jnp.exp(sink_v - m)
                o4 = o[:, :HEAD] * (1.0 / denom)
                lse4 = m + jnp.log(denom)
                for g in range(GROUP):
                    outs.append(o4[g * BLK:(g + 1) * BLK, :])
                    lse_all = jnp.where(lane == hk * GROUP + g, lse4[g * BLK:(g + 1) * BLK, :], lse_all)
            o_ref[rs, :] = jnp.concatenate(outs, axis=1).astype(BF16)
            lse_ref[rs, :] = lse_all
        _host_finish(n, steps - 1, comm, forward_at=(3 * steps) // 4)

    tq = Q_BLOCKS * BLK
    out = pl.pallas_call(
        body, name="attn_fwd", grid=(steps,),
        in_specs=[_rows(tq, Q_W)] + _kv_specs(nb, Q_BLOCKS) + _kv_specs(nb, Q_BLOCKS)
        + [_acc((n_ctx, 2 * KV_W)), _acc((n_ctx, 2 * LANES)), pl.BlockSpec(memory_space=pltpu.SMEM)] + _comm_specs(ng),
        out_specs=[_rows(tq, Q_W), _rows(tq, LANES)] + _comm_specs(ng),
        out_shape=[jax.ShapeDtypeStruct((seq, Q_W), BF16), jax.ShapeDtypeStruct((seq, LANES), F32)] + _gathered_shapes(gather),
        scratch_shapes=_comm_scratch(ng) if ng else [],
        compiler_params=_cp("arbitrary"),
    )(q, *([kv] * nkv), *([va] * nkv), kvc, vac, sink, *gather)
    return out[:2], out[2:]


def _gmlp_chunk(u, vb, gp_ref, ws_ref, bias_ref):
    gu, tu = _gelu(u)
    gv, tv = _gelu(vb)
    vhat, rstd = _ln(gv)
    vn = (vhat * gp_ref[0:1, :] + gp_ref[1:2, :]).astype(BF16)
    s = bias_ref[...] + jnp.concatenate(
        [_dot(ws_ref[g * BLK:(g + 1) * BLK, :], vn[:, g * GRP_D:(g + 1) * GRP_D]) for g in range(N_GRP)], axis=1)
    return gu, tu, tv, vhat, rstd, vn, s


def _mix_fwd(uv, gab, ya, gp, ws_stack, bias_full, w_a, w_b, w_o, tm, gather=()):
    seq = uv.shape[0]
    ng = len(gather)
    steps = seq // tm

    def body(uv_ref, gab_ref, ya_ref, gp_ref, ws_ref, bias_ref, wa_ref, wb_ref, wo_ref, *rest):
        a_ref, b_ref, mix_ref, merged_ref, yb_ref = rest[ng:ng + 5]
        comm = _Gather(rest[:ng], rest[ng + 5:2 * ng + 5], *rest[2 * ng + 5:]) if ng else None
        _host_start(pl.program_id(0), comm)
        for c in range(tm // BLK):
            rs = slice(c * BLK, (c + 1) * BLK)
            gu, _, _, _, _, _, s = _gmlp_chunk(uv_ref[rs, :G_W].astype(F32), uv_ref[rs, G_W:].astype(F32), gp_ref, ws_ref, bias_ref)
            yb_ref[rs, :] = (gu * s).astype(BF16)
        a = _dot(ya_ref[...], wa_ref[...])
        b = _dot(yb_ref[...], wb_ref[...])
        a_ref[...] = a.astype(BF16)
        b_ref[...] = b.astype(BF16)
        merged = (_sig(gab_ref[:, :D].astype(F32)) * a + _sig(gab_ref[:, D:].astype(F32)) * b).astype(BF16)
        merged_ref[...] = merged
        mix_ref[...] = _dot(merged, wo_ref[...])
        _host_finish(pl.program_id(0), steps - 1, comm, forward_at=(3 * steps) // 4)

    out = pl.pallas_call(
        body, name="mix_fwd", grid=(steps,),
        in_specs=[_rows(tm, 2 * G_W), _rows(tm, 2 * D), _rows(tm, Q_W), _acc((8, G_W)),
                  _resident((N_GRP * BLK, BLK)), _acc((BLK, G_W)),
                  _resident((Q_W, D)), _resident((G_W, D)), _resident((D, D))] + _comm_specs(ng),
        out_specs=[_rows(tm, D), _rows(tm, D), _rows(tm, D), _rows(tm, D), _rows(tm, G_W)] + _comm_specs(ng),
        out_shape=[jax.ShapeDtypeStruct((seq, D), BF16), jax.ShapeDtypeStruct((seq, D), BF16),
                   jax.ShapeDtypeStruct((seq, D), F32), jax.ShapeDtypeStruct((seq, D), BF16),
                   jax.ShapeDtypeStruct((seq, G_W), BF16)] + _gathered_shapes(gather),
        scratch_shapes=_comm_scratch(ng) if ng else [],
        compiler_params=_cp("arbitrary"),
    )(uv, gab, ya, gp, ws_stack, bias_full, w_a, w_b, w_o, *gather)
    return out[:5], out[5:]


def _mid_recompute(x_ref, mix_ref, vec_ref):
    r1 = ALPHA * x_ref[...] + vec_ref[0:1, :] * mix_ref[...]
    xh1, rstd1 = _ln(r1)
    xmid = xh1 * vec_ref[1:2, :] + vec_ref[2:3, :]
    xh2, rstd2 = _ln(xmid)
    return xh1, rstd1, xmid, xh2, rstd2


def _ffn(x, mix, tgt, vec, w_fi, w_fo, tm):
    seq = x.shape[0]

    def body(x_ref, mix_ref, tgt_ref, vec_ref, wi_ref, wo_ref, act_ref, h2_ref, dff_ref, df_ref, dr1_ref, st_ref, gu_ref):
        @pl.when(pl.program_id(0) == 0)
        def _():
            st_ref[...] = jnp.zeros_like(st_ref)

        xh1, rstd1, xmid, xh2, rstd2 = _mid_recompute(x_ref, mix_ref, vec_ref)
        h2 = (xh2 * (1.0 + vec_ref[4:5, :]) + vec_ref[3:4, :]).astype(BF16)
        h2_ref[...] = h2
        halves = [(slice(hh * FH_SHARD, (hh + 1) * FH_SHARD), slice(FH + hh * FH_SHARD, FH + (hh + 1) * FH_SHARD))
                  for hh in range(2)]
        for hh, (cs, cu) in enumerate(halves):
            g = _dot(h2, wi_ref[hh])
            u = _dot(h2, wi_ref[2 + hh])
            gu_ref[:, cs] = g
            gu_ref[:, cu] = u
            act_ref[:, cs] = (g * _sig(g) * u).astype(BF16)
        f = _dot(act_ref[...], wo_ref[...])
        r2 = ALPHA * xmid + vec_ref[5:6, :] * f
        yh, rstd = _ln(r2)
        y = yh * vec_ref[6:7, :] + vec_ref[7:8, :]
        err = y - tgt_ref[...]
        dy = err / D
        dr2 = _ln_bwd(dy * vec_ref[6:7, :], yh, rstd)
        st_ref[0:1, :] += _colsum(err * err)
        st_ref[1:2, :] += _colsum(dy * yh)
        st_ref[2:3, :] += _colsum(dy)
        st_ref[3:4, :] += _colsum(dr2 * f)

        df = (dr2 * vec_ref[5:6, :]).astype(BF16)
        df_ref[...] = df
        da_all = _dot_nt(df, wo_ref[...])
        for cs, cu in halves:
            da = da_all[:, cs]
            g = gu_ref[:, cs]
            u = gu_ref[:, cu]
            sg = _sig(g)
            dff_ref[:, cs] = (da * u * sg * (1.0 + g * (1.0 - sg))).astype(BF16)
            dff_ref[:, cu] = (da * g * sg).astype(BF16)
        dh2 = _dot_nt(dff_ref[:, :FH_SHARD], wi_ref[0])
        for s in range(1, N_SHARD):
            dh2 = dh2 + _dot_nt(dff_ref[:, s * FH_SHARD:(s + 1) * FH_SHARD], wi_ref[s])
        dxmid = _ln_bwd(dh2 * (1.0 + vec_ref[4:5, :]), xh2, rstd2) + ALPHA * dr2
        dr1 = _ln_bwd(dxmid * vec_ref[1:2, :], xh1, rstd1)
        dr1_ref[...] = dr1
        st_ref[8:9, :] += _colsum(dh2 * xh2)
        st_ref[9:10, :] += _colsum(dh2)
        st_ref[10:11, :] += _colsum(dxmid * xh1)
        st_ref[11:12, :] += _colsum(dxmid)
        st_ref[12:13, :] += _colsum(dr1 * mix_ref[...])

    return pl.pallas_call(
        body, name="ffn", grid=(seq // tm,),
        in_specs=[_rows(tm, D), _rows(tm, D), _rows(tm, D), _acc((8, D)), _resident((N_SHARD, D, FH_SHARD)), _resident((FH, D))],
        out_specs=[_rows(tm, FH), _rows(tm, D), _rows(tm, 2 * FH), _rows(tm, D), _rows(tm, D), _acc((16, D))],
        out_shape=[jax.ShapeDtypeStruct((seq, FH), BF16), jax.ShapeDtypeStruct((seq, D), BF16),
                   jax.ShapeDtypeStruct((seq, 2 * FH), BF16), jax.ShapeDtypeStruct((seq, D), BF16),
                   jax.ShapeDtypeStruct((seq, D), F32), jax.ShapeDtypeStruct((16, D), F32)],
        scratch_shapes=[pltpu.VMEM((tm, 2 * FH), F32)],
        compiler_params=_cp("arbitrary"),
    )(x, mix, tgt, vec, w_fi, w_fo)


def _mix_bwd(dr1, a, b, gab, uv, merged, ya, yb, vec, gp, ws_stack, ws_stack_t, bias_full, w_a, w_b, w_o, tm, scatter=()):
    seq = dr1.shape[0]
    last = seq // tm - 1
    ns = len(scatter)

    def body(dr1_ref, a_ref, b_ref, gab_ref, uv_ref, mg_ref, ya_ref, yb_ref, vec_ref, gp_ref, ws_ref, wst_ref, bias_ref,
             wa_ref, wb_ref, wo_ref, *rest):
        dya_ref, dp_ref, dws_ref, dbs_ref, st_ref, gwo_ref, gwa_ref, gwb_ref = rest[ns:ns + 8]
        acc_o, acc_a, acc_b = rest[2 * ns + 8:2 * ns + 11]
        comm = _AllToAll(rest[:ns], rest[ns + 8:2 * ns + 8], *rest[2 * ns + 11:]) if ns else None
        _host_start(pl.program_id(0), comm)

        @pl.when(pl.program_id(0) == 0)
        def _():
            dws_ref[...] = jnp.zeros_like(dws_ref)
            dbs_ref[...] = jnp.zeros_like(dbs_ref)
            st_ref[...] = jnp.zeros_like(st_ref)
            acc_o[...] = jnp.zeros_like(acc_o)
            acc_a[...] = jnp.zeros_like(acc_a)
            acc_b[...] = jnp.zeros_like(acc_b)

        dmix = (dr1_ref[...] * vec_ref[0:1, :]).astype(BF16)
        acc_o[...] += _dot_tn(mg_ref[...], dmix)
        dmerged = _dot_nt(dmix, wo_ref[...])
        sa = _sig(gab_ref[:, :D].astype(F32))
        sb = _sig(gab_ref[:, D:].astype(F32))
        da_f = dmerged * sa
        db_f = dmerged * sb
        da = da_f.astype(BF16)
        db = db_f.astype(BF16)
        dp_ref[:, 2 * G_W:2 * G_W + D] = (da_f * a_ref[...].astype(F32) * (1.0 - sa)).astype(BF16)
        dp_ref[:, 2 * G_W + D:] = (db_f * b_ref[...].astype(F32) * (1.0 - sb)).astype(BF16)
        dya_ref[...] = _dot_nt(da, wa_ref[...]).astype(BF16)
        dyb = _dot_nt(db, wb_ref[...])
        acc_a[...] += _dot_tn(ya_ref[...], da)
        acc_b[...] += _dot_tn(yb_ref[...], db)

        @pl.when(pl.program_id(0) == last)
        def _():
            gwo_ref[...] = acc_o[...].astype(BF16)
            gwa_ref[...] = acc_a[...].astype(BF16)
            gwb_ref[...] = acc_b[...].astype(BF16)

        for c in range(tm // BLK):
            rs = slice(c * BLK, (c + 1) * BLK)
            u = uv_ref[rs, :G_W].astype(F32)
            vb = uv_ref[rs, G_W:].astype(F32)
            gu, tu, tv, vhat, rstd, vn, s = _gmlp_chunk(u, vb, gp_ref, ws_ref, bias_ref)
            dyb_c = dyb[rs, :]
            ds = dyb_c * gu
            du = dyb_c * s * _gelu_grad(u, tu)
            ds_b = ds.astype(BF16)
            dvn_g = []
            for g in range(N_GRP):
                cg = slice(g * GRP_D, (g + 1) * GRP_D)
                dvn_g.append(_dot(wst_ref[:, g * BLK:(g + 1) * BLK], ds_b[:, cg]))
                dws_ref[g * BLK:(g + 1) * BLK, :] += _dot_nt(ds_b[:, cg], vn[:, cg])
            dvn = jnp.concatenate(dvn_g, axis=1)
            dbs_ref[...] += ds
            st_ref[0:1, :] += _colsum(dvn * vhat)
            st_ref[1:2, :] += _colsum(dvn)
            dgv = _ln_bwd(dvn * gp_ref[0:1, :], vhat, rstd)
            dvb = dgv * _gelu_grad(vb, tv)
            dp_ref[rs, :G_W] = du.astype(BF16)
            dp_ref[rs, G_W:2 * G_W] = dvb.astype(BF16)
        _host_finish(pl.program_id(0), last, comm)

    pw = 2 * G_W + 2 * D
    out = pl.pallas_call(
        body, name="mix_bwd", grid=(seq // tm,),
        in_specs=[_rows(tm, D), _rows(tm, D), _rows(tm, D), _rows(tm, 2 * D), _rows(tm, 2 * G_W), _rows(tm, D), _rows(tm, Q_W),
                  _rows(tm, G_W), _acc((8, D)), _acc((8, G_W)),
                  _resident((N_GRP * BLK, BLK)), _resident((BLK, N_GRP * BLK)), _acc((BLK, G_W)),
                  _resident((Q_W, D)), _resident((G_W, D)), _resident((D, D))] + _comm_specs(ns),
        out_specs=[_rows(tm, Q_W), _rows(tm, pw), _acc((N_GRP * BLK, BLK)), _acc((BLK, G_W)), _acc((8, G_W)),
                   _acc((D, D)), _acc((Q_W, D)), _acc((G_W, D))] + _comm_specs(ns),
        out_shape=[jax.ShapeDtypeStruct((seq, Q_W), BF16), jax.ShapeDtypeStruct((seq, pw), BF16),
                   jax.ShapeDtypeStruct((N_GRP * BLK, BLK), F32), jax.ShapeDtypeStruct((BLK, G_W), F32),
                   jax.ShapeDtypeStruct((8, G_W), F32), jax.ShapeDtypeStruct((D, D), BF16),
                   jax.ShapeDtypeStruct((Q_W, D), BF16), jax.ShapeDtypeStruct((G_W, D), BF16)]
        + [jax.ShapeDtypeStruct(v.shape, v.dtype) for v in scatter],
        scratch_shapes=[pltpu.VMEM((D, D), F32), pltpu.VMEM((Q_W, D), F32), pltpu.VMEM((G_W, D), F32)]
        + (_comm_scratch(ns) if ns else []),
        compiler_params=_cp("arbitrary"),
    )(dr1, a, b, gab, uv, merged, ya, yb, vec, gp, ws_stack, ws_stack_t, bias_full, w_a, w_b, w_o, *scatter)
    return out[:8], out[8:]


def _attn_bwd(q, kv, kvc, sink, dya, ya, lse, scatter=()):
    seq = q.shape[0]
    nb = seq // BLK
    n_ctx = kvc.shape[0]
    ns = len(scatter)
    Q_BLOCKS = 2
    nkv = Q_BLOCKS + 2
    steps = nb // Q_BLOCKS

    def body(q_ref, *rest):
        kv_refs = rest[:nkv]
        kvc_ref, sink_ref, do_ref, o_ref, lse_ref = rest[nkv:nkv + 5]
        rest = rest[nkv + 5:]
        dq_ref, dkv_ref, dkvc_ref, dsink_ref = rest[ns:ns + 4]
        comm = _AllToAll(rest[:ns], rest[ns + 4:2 * ns + 4], *rest[2 * ns + 4:]) if ns else None
        n = pl.program_id(0)
        _host_start(n, comm)

        @pl.when(n == 0)
        def _():
            dkv_ref[...] = jnp.zeros_like(dkv_ref)
            dkvc_ref[...] = jnp.zeros_like(dkvc_ref)
            dsink_ref[...] = jnp.zeros_like(dsink_ref)

        lane = lax.broadcasted_iota(jnp.int32, (1, LANES), 1)
        for sub in range(Q_BLOCKS):
            rs = slice(sub * BLK, (sub + 1) * BLK)
            blk = Q_BLOCKS * n + sub
            q = q_ref[rs, :]
            do = do_ref[rs, :]
            out = o_ref[rs, :]
            lse_all = lse_ref[rs, :]
            k_refs = (kvc_ref,) + kv_refs[sub:sub + 3]
            masks = _band_masks(blk, nb)
            dqs, dks, dvs = [], [], []
            for hk in range(N_KV):
                q4, ks, s = _attn_scores(q, k_refs, hk, masks)
                vs = [r[:, KV_W + hk * HEAD:KV_W + (hk + 1) * HEAD] for r in k_refs]
                lse4 = jnp.concatenate([lse_all[:, hk * GROUP + g:hk * GROUP + g + 1] for g in range(GROUP)], axis=0)
                do4 = _stack_heads(do, hk)
                delta = jnp.sum(do4.astype(F32) * _stack_heads(out, hk).astype(F32), axis=-1, keepdims=True)
                p = [jnp.exp((t - lse4).astype(BF16)) for t in s]
                ds = [t * (_dot_nt(do4, v) - delta).astype(BF16) for t, v in zip(p, vs)]
                dq4 = _dot(ds[0], ks[0])
                for t, k in zip(ds[1:], ks[1:]):
                    dq4 = dq4 + _dot(t, k)
                dq4 = dq4 * SCALE
                dqs += [dq4[g * BLK:(g + 1) * BLK, :] for g in range(GROUP)]
                dks.append([_dot_tn(t, q4) for t in ds])
                dvs.append([_dot_tn(t, do4) for t in p])
                ps = jnp.exp(_sink_rows(sink_ref, hk) - lse4) * delta
                for g in range(GROUP):
                    part = -jnp.sum(ps[g * BLK:(g + 1) * BLK, :], axis=0, keepdims=True)
                    dsink_ref[0:1, :] += jnp.where(lane == hk * GROUP + g, part, 0.0)
            dq_ref[rs, :] = jnp.concatenate(dqs, axis=1)

            def piece(i):
                return jnp.concatenate([dks[0][i], dks[1][i], dvs[0][i], dvs[1][i]], axis=1)

            dkvc_ref[...] += piece(0)
            starts = (jnp.maximum(blk - 1, 0), blk, jnp.minimum(blk + 1, nb - 1))
            for i, st in enumerate(starts):
                r = pl.ds(pl.multiple_of(st * BLK, BLK), BLK)
                dkv_ref[r, :] += piece(i + 1)
        _host_finish(n, steps - 1, comm)

    tq = Q_BLOCKS * BLK
    out = pl.pallas_call(
        body, name="attn_bwd", grid=(steps,),
        in_specs=[_rows(tq, Q_W)] + _kv_specs(nb, Q_BLOCKS) + [_acc((n_ctx, 2 * KV_W)), pl.BlockSpec(memory_space=pltpu.SMEM),
                                                     _rows(tq, Q_W), _rows(tq, Q_W), _rows(tq, LANES)] + _comm_specs(ns),
        out_specs=[_rows(tq, Q_W), _acc((seq, 2 * KV_W)), _acc((n_ctx, 2 * KV_W)), _acc((8, LANES))] + _comm_specs(ns),
        out_shape=[jax.ShapeDtypeStruct((seq, Q_W), F32), jax.ShapeDtypeStruct((seq, 2 * KV_W), F32),
                   jax.ShapeDtypeStruct((n_ctx, 2 * KV_W), F32), jax.ShapeDtypeStruct((8, LANES), F32)]
        + [jax.ShapeDtypeStruct(v.shape, v.dtype) for v in scatter],
        scratch_shapes=_comm_scratch(ns) if ns else [],
        compiler_params=_cp("arbitrary"),
    )(q, *([kv] * nkv), kvc, sink, dya, ya, lse, *scatter)
    return out[:4], out[4:]


def _proj_bwd(dq, dkv, dpb, x, dr1, modx, w_in, cos, sin, tm, scatter=()):
    seq = x.shape[0]
    pw = IN_W - Q_W - 2 * KV_W
    ns = len(scatter)

    def body(dq_ref, dkv_ref, dpb_ref, x_ref, dr1_ref, mod_ref, w_ref, cos_ref, sin_ref, *rest):
        dqkv_ref, gx_ref, st_ref = rest[ns:ns + 3]
        comm = _AllToAll(rest[:ns], rest[ns + 3:2 * ns + 3], *rest[2 * ns + 3:]) if ns else None
        _host_start(pl.program_id(0), comm)

        @pl.when(pl.program_id(0) == 0)
        def _():
            st_ref[...] = jnp.zeros_like(st_ref)

        cos1, sin1 = cos_ref[...], sin_ref[...]
        cos2 = jnp.concatenate([cos1, cos1], axis=1)
        sin2 = jnp.concatenate([sin1, sin1], axis=1)
        for j in range(Q_W // 256):
            cs = slice(256 * j, 256 * (j + 1))
            dqkv_ref[:, cs] = _unrope(dq_ref[:, cs], cos2, sin2).astype(BF16)
        dqkv_ref[:, Q_W:Q_W + KV_W] = _unrope(dkv_ref[:, :KV_W], cos1, sin1).astype(BF16)
        dqkv_ref[:, Q_W + KV_W:] = dkv_ref[:, KV_W:].astype(BF16)
        o = Q_W + 2 * KV_W
        dh = _dot(dqkv_ref[...], w_ref[:o, :]) + _dot(dpb_ref[...], w_ref[o:, :])
        xhat, rstd = _ln(x_ref[...])
        st_ref[0:1, :] += _colsum(dh)
        st_ref[1:2, :] += _colsum(dh * xhat)
        gx_ref[...] = _ln_bwd(dh * (1.0 + mod_ref[1:2, :]), xhat, rstd) + ALPHA * dr1_ref[...]
        _host_finish(pl.program_id(0), seq // tm - 1, comm)

    out = pl.pallas_call(
        body, name="proj_bwd", grid=(seq // tm,),
        in_specs=[_rows(tm, Q_W), _rows(tm, 2 * KV_W), _rows(tm, pw), _rows(tm, D), _rows(tm, D), _acc((8, D)),
                  _resident((IN_W, D)), _rows(tm, LANES), _rows(tm, LANES)] + _comm_specs(ns),
        out_specs=[_rows(tm, Q_W + 2 * KV_W), _rows(tm, D), _acc((8, D))] + _comm_specs(ns),
        out_shape=[jax.ShapeDtypeStruct((seq, Q_W + 2 * KV_W), BF16), jax.ShapeDtypeStruct((seq, D), F32),
                   jax.ShapeDtypeStruct((8, D), F32)] + [jax.ShapeDtypeStruct(v.shape, v.dtype) for v in scatter],
        scratch_shapes=_comm_scratch(ns) if ns else [],
        compiler_params=_cp("arbitrary"),
    )(dq, dkv, dpb, x, dr1, modx, w_in, cos, sin, *scatter)
    return out[:3], out[3:]


def _ctx_bwd(dkvc, ctx, hc, w_kv):
    n_ctx = ctx.shape[0]

    def body(dkvc_ref, ctx_ref, hc_ref, w_ref, dw_ref, st_ref):
        d = dkvc_ref[...].astype(BF16)
        dw_ref[...] = _dot_tn(d, hc_ref[...])
        dhc = _dot(d, w_ref[...])
        xhat, _ = _ln(ctx_ref[...])
        st_ref[...] = jnp.zeros_like(st_ref)
        st_ref[0:1, :] = _colsum(dhc)
        st_ref[1:2, :] = _colsum(dhc * xhat)

    return pl.pallas_call(
        body, name="ctx_bwd", grid=(1,),
        in_specs=[_acc((n_ctx, 2 * KV_W)), _acc((n_ctx, D)), _acc((n_ctx, D)), _acc((2 * KV_W, D))],
        out_specs=[_acc((2 * KV_W, D)), _acc((8, D))],
        out_shape=[jax.ShapeDtypeStruct((2 * KV_W, D), F32), jax.ShapeDtypeStruct((8, D), F32)],
        compiler_params=_cp("arbitrary"),
    )(dkvc, ctx, hc, w_kv)


def _tn_matmul(a, b, tn, name, out_dtype, shard_major=False, init=None, tk=512, scatter=(), gather=()):
    t, ka = a.shape
    n = b.shape[1]
    tk = min(tk, t)
    nk = t // tk
    nj = n // tn
    has_init = init is not None
    assert not (scatter and gather)
    moved = list(scatter) + list(gather)
    pattern = _AllToAll if scatter else _Gather
    ns = len(moved)
    n_in = 3 if has_init else 2

    def body(*refs):
        a_ref, b_ref = refs[:2]
        i_ref = refs[2] if has_init else None
        rest = refs[n_in:]
        o_ref = rest[ns]
        acc_ref = rest[2 * ns + 1]
        comm = pattern(rest[:ns], rest[ns + 1:2 * ns + 1], *rest[2 * ns + 2:]) if ns else None
        k = pl.program_id(1)
        step = pl.program_id(0) * nk + k
        _host_start(step, comm)

        @pl.when(k == 0)
        def _():
            acc_ref[...] = i_ref[...] if has_init else jnp.zeros_like(acc_ref)

        acc_ref[...] += _dot_tn(a_ref[...], b_ref[...])

        @pl.when(k == nk - 1)
        def _():
            o_ref[...] = acc_ref[...].astype(out_dtype)

        _host_finish(step, nj * nk - 1, comm, forward_at=(3 * nj * nk) // 4 if gather else None)

    in_specs = [pl.BlockSpec((tk, ka), lambda j, k: (k, 0)), pl.BlockSpec((tk, tn), lambda j, k: (k, j))]
    args = [a, b]
    if has_init:
        in_specs.append(pl.BlockSpec((ka, tn), lambda j, k: (0, j)))
        args.append(init)
    if shard_major:
        out_spec = pl.BlockSpec((None, ka, tn), lambda j, k: (j, 0, 0))
        out_shape = jax.ShapeDtypeStruct((nj, ka, tn), out_dtype)
    else:
        out_spec = pl.BlockSpec((ka, tn), lambda j, k: (0, j))
        out_shape = jax.ShapeDtypeStruct((ka, n), out_dtype)
    out = pl.pallas_call(
        body, name=name, grid=(nj, nk), in_specs=in_specs + _comm_specs(ns), out_specs=[out_spec] + _comm_specs(ns),
        out_shape=[out_shape] + [jax.ShapeDtypeStruct(v.shape, v.dtype) for v in scatter] + _gathered_shapes(gather),
        scratch_shapes=[pltpu.VMEM((ka, tn), F32)] + (_comm_scratch(ns) if ns else []),
        compiler_params=_cp("arbitrary", "arbitrary"),
    )(*args, *moved)
    return (out[0], out[1:]) if ns else out[0]


ADA_TILE = 512


def _gather_and_modulate(c_rows, c_ctx, w_half, w_ada):
    cs = w_ada.shape[1]
    vmem = pl.BlockSpec(memory_space=pltpu.VMEM)

    def body(c_ref, cctx_ref, wh_ref, wada_ref, cg_ref, wg_ref, mod_ref, modg_ref, sc_ref,
             c_v, wada_v, mod_v, send_c, recv_c, send_w, recv_w, send_m, recv_m, local):
        gc = _Gather([c_ref], [cg_ref], send_c, recv_c)
        gw = _Gather([wh_ref], [wg_ref], send_w, recv_w)
        gm = _Gather([mod_ref], [modg_ref], send_m, recv_m)
        px, py, pc = _my_pos()
        mine = 4 * px + 2 * py + pc
        gc.start()
        gw.start()
        own_c = pltpu.make_async_copy(c_ref, cg_ref.at[mine], local.at[0])
        load_w = pltpu.make_async_copy(wada_ref, wada_v, local.at[1])
        own_c.start()
        load_w.start()
        gc.finish()
        own_c.wait()
        load_c = pltpu.make_async_copy(cg_ref, c_v, local.at[2])
        load_c.start()
        load_c.wait()
        cc = jnp.concatenate([c_v[i, 0:1, :] for i in range(N_DEV)] + [cctx_ref[...], jnp.zeros((7, D), F32)], axis=0)
        sc = cc * _sig(cc)
        sc_ref[...] = sc
        load_w.wait()
        for j in range(cs // ADA_TILE):
            cols = slice(j * ADA_TILE, (j + 1) * ADA_TILE)
            mod_v[:, cols] = _dot(sc.astype(BF16), wada_v[:, cols].astype(BF16))
        store_m = pltpu.make_async_copy(mod_v, mod_ref, local.at[3])
        store_m.start()
        store_m.wait()
        own_m = pltpu.make_async_copy(mod_ref, modg_ref.at[mine], local.at[4])
        own_m.start()
        gm.start()
        gm.finish()
        own_m.wait()
        gw.finish()

    any_ = pl.BlockSpec(memory_space=pl.ANY)
    return pl.pallas_call(
        body, name="gather_and_modulate",
        in_specs=[any_, vmem, any_, any_], out_specs=[any_, any_, any_, any_, vmem],
        scratch_shapes=[pltpu.VMEM((N_DEV, 8, D), F32), pltpu.VMEM((D, cs), F32), pltpu.VMEM((16, cs), F32)]
        + _comm_scratch(1) + _comm_scratch(1) + _comm_scratch(1) + [pltpu.SemaphoreType.DMA((5,))],
        out_shape=[jax.ShapeDtypeStruct((N_DEV, 8, D), F32), jax.ShapeDtypeStruct((N_DEV,) + w_half.shape, w_half.dtype),
                   jax.ShapeDtypeStruct((16, cs), F32), jax.ShapeDtypeStruct((N_DEV, 16, cs), F32),
                   jax.ShapeDtypeStruct((16, D), F32)],
        compiler_params=pltpu.CompilerParams(vmem_limit_bytes=VMEM_LIMIT),
    )(c_rows, c_ctx, w_half, w_ada)


def _ada_bwd(sc_all_t, dm_all, dmc, w_ada):
    cs = w_ada.shape[1]

    def body(st_ref, dm_ref, dmc_ref, w_ref, gw_ref, part_ref):
        @pl.when(pl.program_id(0) == 0)
        def _():
            part_ref[...] = jnp.zeros_like(part_ref)

        gw_ref[...] = _dot(st_ref[...].astype(BF16), dm_ref[...].astype(BF16))
        part_ref[...] += _dot_nt(dmc_ref[...].astype(BF16), w_ref[...].astype(BF16))

    return pl.pallas_call(
        body, name="ada_bwd", grid=(cs // ADA_TILE,),
        in_specs=[_acc((D, 16)), pl.BlockSpec((16, ADA_TILE), lambda j: (0, j)), pl.BlockSpec((8, ADA_TILE), lambda j: (0, j)),
                  pl.BlockSpec((D, ADA_TILE), lambda j: (0, j))],
        out_specs=[pl.BlockSpec((D, ADA_TILE), lambda j: (0, j)), _acc((8, D))],
        out_shape=[jax.ShapeDtypeStruct((D, cs), F32), jax.ShapeDtypeStruct((8, D), F32)],
        compiler_params=_cp("arbitrary"),
    )(sc_all_t, dm_all, dmc, w_ada)


def _sum8(x, name, tr=256):
    _, r, c = x.shape
    tr = min(tr, r)
    while r % tr:
        tr -= 16

    def body(x_ref, o_ref):
        acc = x_ref[0].astype(F32)
        for i in range(1, N_DEV):
            acc = acc + x_ref[i].astype(F32)
        o_ref[...] = acc

    return pl.pallas_call(
        body, name=name, grid=(r // tr,),
        in_specs=[pl.BlockSpec((N_DEV, tr, c), lambda i: (0, i, 0))],
        out_specs=pl.BlockSpec((tr, c), lambda i: (i, 0)),
        out_shape=jax.ShapeDtypeStruct((r, c), F32),
        compiler_params=_cp("arbitrary"),
    )(x)


def _sum_blocks(recv, src, me, name, tr=256):
    _, r, c = recv.shape
    tr = min(tr, r)
    while r % tr:
        tr -= 16

    def body(me_ref, recv_ref, own_ref, o_ref):
        acc = own_ref[...].astype(F32)
        for k in range(1, N_DEV):
            acc = acc + recv_ref[me_ref[0] ^ k].astype(F32)
        o_ref[...] = acc

    return pl.pallas_call(
        body, name=name,
        grid_spec=pltpu.PrefetchScalarGridSpec(
            num_scalar_prefetch=1, grid=(r // tr,),
            in_specs=[pl.BlockSpec((N_DEV, tr, c), lambda i, me_ref: (0, i, 0)),
                      pl.BlockSpec((None, tr, c), lambda i, me_ref: (me_ref[0], i, 0))],
            out_specs=pl.BlockSpec((tr, c), lambda i, me_ref: (i, 0))),
        out_shape=jax.ShapeDtypeStruct((r, c), F32),
        compiler_params=_cp("arbitrary"),
    )(me, recv, src)


def _sum8_many(xs, name):
    n = len(xs)

    def body(*refs):
        for x_ref, o_ref in zip(refs[:n], refs[n:]):
            acc = x_ref[0]
            for i in range(1, N_DEV):
                acc = acc + x_ref[i]
            o_ref[...] = acc

    vmem = pl.BlockSpec(memory_space=pltpu.VMEM)
    return pl.pallas_call(
        body, name=name, in_specs=[vmem] * n, out_specs=[vmem] * n,
        out_shape=[jax.ShapeDtypeStruct(v.shape[1:], v.dtype) for v in xs],
        compiler_params=pltpu.CompilerParams(vmem_limit_bytes=VMEM_LIMIT),
    )(*xs)


def _adam_update(w, g, m, v):
    nm = ADAM_B1 * m + (1.0 - ADAM_B1) * g
    nv = ADAM_B2 * v + (1.0 - ADAM_B2) * (g * g)
    m_hat = nm / (1.0 - ADAM_B1 ** ADAM_STEP)
    v_hat = nv / (1.0 - ADAM_B2 ** ADAM_STEP)
    return -ADAM_LR * (m_hat / (jnp.sqrt(v_hat) + ADAM_EPS) + ADAM_WD * w), nm, nv


ROW_LOSS, ROW_LN2_G, ROW_LN2_B, ROW_LN1_G, ROW_LN1_B = 0, 1, 2, 10, 11
ROWS_DMOD_X = (16, 17, 12, 9, 8, 3)
ROWS_DMOD_C = (24, 25)
SMALL = ("c_ctx", "b_ada", "attn_sink", "gmlp_ln_g", "gmlp_ln_b", "w_spatial", "b_spatial", "ln1_g", "ln1_b", "ln2_g", "ln2_b")


def _adamw_small(sums, dsc, w, m, v):
    n = len(SMALL)

    def body(*refs):
        st_ref, gm_ref, sk_ref, ws_ref, bs_ref, dsc_ref = refs[:6]
        w_refs = dict(zip(SMALL, refs[6:6 + n]))
        m_refs = dict(zip(SMALL, refs[6 + n:6 + 2 * n]))
        v_refs = dict(zip(SMALL, refs[6 + 2 * n:6 + 3 * n]))
        outs = refs[6 + 3 * n:]
        c = w_refs["c_ctx"][...]
        sg = _sig(c)
        dmod = [st_ref[r:r + 1, :] for r in ROWS_DMOD_X]
        dmod[0] = dmod[0] + st_ref[ROWS_DMOD_C[0]:ROWS_DMOD_C[0] + 1, :]
        dmod[1] = dmod[1] + st_ref[ROWS_DMOD_C[1]:ROWS_DMOD_C[1] + 1, :]
        grads = dict(
            c_ctx=dsc_ref[0:1, :] * (sg * (1.0 + c * (1.0 - sg))),
            b_ada=jnp.concatenate(dmod, axis=1),
            attn_sink=sk_ref[0:1, 0:N_KV * GROUP],
            gmlp_ln_g=gm_ref[0:1, :], gmlp_ln_b=gm_ref[1:2, :],
            w_spatial=ws_ref[...], b_spatial=bs_ref[...],
            ln1_g=st_ref[ROW_LN1_G:ROW_LN1_G + 1, :], ln1_b=st_ref[ROW_LN1_B:ROW_LN1_B + 1, :],
            ln2_g=st_ref[ROW_LN2_G:ROW_LN2_G + 1, :], ln2_b=st_ref[ROW_LN2_B:ROW_LN2_B + 1, :])
        for i, name in enumerate(SMALL):
            g = grads[name]
            d, nm, nv = _adam_update(w_refs[name][...], g, m_refs[name][...], v_refs[name][...])
            outs[i][...] = g
            outs[n + i][...] = d
            outs[2 * n + i][...] = nm
            outs[3 * n + i][...] = nv

    vmem = pl.BlockSpec(memory_space=pltpu.VMEM)
    args = list(sums) + [dsc] + [w[k] for k in SMALL] + [m[k] for k in SMALL] + [v[k] for k in SMALL]
    shapes = [jax.ShapeDtypeStruct(w[k].shape, F32) for k in SMALL]
    out = pl.pallas_call(
        body, name="adamw_small", in_specs=[vmem] * len(args), out_specs=[vmem] * (4 * n), out_shape=shapes * 4,
        compiler_params=pltpu.CompilerParams(vmem_limit_bytes=VMEM_LIMIT),
    )(*args)
    return [dict(zip(SMALL, out[i * n:(i + 1) * n])) for i in range(4)]


def _adamw_halves(w, mine, theirs, m, v, c_arr, name):
    r, c = w.shape
    tr = min(256, r // 2)
    while (r // 2) % tr:
        tr -= 8
    nt = (r // 2) // tr

    def body(c_ref, w_ref, mine_ref, theirs_ref, m_ref, v_ref, g_ref, d_ref, nm_ref, nv_ref):
        g = jnp.where(pl.program_id(0) == c_ref[0], mine_ref[...], theirs_ref[...])
        g_ref[...] = g
        d_ref[...], nm_ref[...], nv_ref[...] = _adam_update(w_ref[...], g, m_ref[...], v_ref[...])

    whole = pl.BlockSpec((tr, c), lambda hb, i, c_ref: (hb * nt + i, 0))
    mine_spec = pl.BlockSpec((tr, c), lambda hb, i, c_ref: (jnp.where(hb == c_ref[0], i, 0), 0))
    theirs_spec = pl.BlockSpec((tr, c), lambda hb, i, c_ref: (jnp.where(hb == c_ref[0], 0, i), 0))
    shp = jax.ShapeDtypeStruct((r, c), F32)
    return pl.pallas_call(
        body, name=name,
        grid_spec=pltpu.PrefetchScalarGridSpec(
            num_scalar_prefetch=1, grid=(2, nt), in_specs=[whole, mine_spec, theirs_spec, whole, whole],
            out_specs=[whole] * 4),
        out_shape=[shp] * 4,
        compiler_params=_cp("arbitrary", "arbitrary"),
    )(c_arr, w, mine, theirs, m, v)


def _adamw(w, g, m, v, name):
    r, c = w.shape
    tr = r if r * c <= 256 * 1024 else min(256, r)
    while r % tr:
        tr -= 8

    def body(w_ref, g_ref, m_ref, v_ref, d_ref, nm_ref, nv_ref):
        d_ref[...], nm_ref[...], nv_ref[...] = _adam_update(w_ref[...], g_ref[...], m_ref[...], v_ref[...])

    spec = pl.BlockSpec((tr, c), lambda i: (i, 0))
    shp = jax.ShapeDtypeStruct((r, c), F32)
    return pl.pallas_call(
        body, name=name, grid=(r // tr,), in_specs=[spec] * 4, out_specs=[spec] * 3, out_shape=[shp] * 3,
        compiler_params=_cp("arbitrary"),
    )(w, g, m, v)


def _my_pos():
    return lax.axis_index("x"), lax.axis_index("y"), lax.axis_index("c")


N_COPY = 7


class _Gather:
    def __init__(self, x_refs, out_refs, send_sems, recv_sems):
        self.x_refs, self.out_refs = x_refs, out_refs
        self.send_sems, self.recv_sems = send_sems, recv_sems
        x, y, c = _my_pos()
        self.c = c
        self.me, self.sibling = (x, y, c), (x, y, 1 - c)
        self.chips = [(1 - x, y), (x, 1 - y), (1 - x, 1 - y)]

    def _copy(self, a, k, block, to, from_input=False):
        px, py, pc = block
        rows = self.out_refs[a].at[4 * px + 2 * py + pc]
        return pltpu.make_async_remote_copy(
            src_ref=self.x_refs[a] if from_input else rows, dst_ref=rows,
            send_sem=self.send_sems.at[a * N_COPY + k], recv_sem=self.recv_sems.at[a * N_COPY + k],
            device_id=to, device_id_type=MESH)

    def start(self):
        n = len(self.x_refs)
        for a in range(n):
            self._copy(a, 0, self.me, self.sibling, from_input=True).start()
        for j, chip in enumerate(self.chips):
            for a in range(n):
                self._copy(a, 1 + j, self.me, (*chip, self.c), from_input=True).start()

    def forward(self):
        c = self.c
        for j, chip in enumerate(self.chips):
            for a in range(len(self.x_refs)):
                self._copy(a, 1 + j, (*chip, c), self.me).wait_recv()
                self._copy(a, 4 + j, (*chip, c), self.sibling).start()

    def finish(self):
        self.forward()
        self.drain()

    def drain(self):
        n = len(self.x_refs)
        c = self.c
        for a in range(n):
            self._copy(a, 0, self.sibling, self.me).wait_recv()
        for j, chip in enumerate(self.chips):
            for a in range(n):
                self._copy(a, 4 + j, (*chip, 1 - c), self.me).wait_recv()
        for a in range(n):
            self._copy(a, 0, self.me, self.sibling, from_input=True).wait_send()
            for j, chip in enumerate(self.chips):
                self._copy(a, 1 + j, self.me, (*chip, c), from_input=True).wait_send()
                self._copy(a, 4 + j, (*chip, c), self.sibling).wait_send()


def _comm_scratch(n):
    return [pltpu.SemaphoreType.DMA((n * N_COPY,)), pltpu.SemaphoreType.DMA((n * N_COPY,))]


def _comm_specs(n):
    return [pl.BlockSpec(memory_space=pl.ANY)] * n


def _gathered_shapes(xs):
    return [jax.ShapeDtypeStruct((N_DEV,) + v.shape, v.dtype) for v in xs]


def _with_own(gathered, xs, me):
    return [lax.dynamic_update_index_in_dim(g, v, me, 0) for g, v in zip(gathered, xs)]


class _AllToAll:
    def __init__(self, x_refs, out_refs, send_sems, recv_sems):
        self.x_refs, self.out_refs = x_refs, out_refs
        self.send_sems, self.recv_sems = send_sems, recv_sems
        self.pos = _my_pos()
        x, y, c = self.pos
        self.me = 4 * x + 2 * y + c

    def _peer(self, k):
        x, y, c = self.pos
        return (x ^ ((k >> 2) & 1), y ^ ((k >> 1) & 1), c ^ (k & 1))

    def _copy(self, a, k):
        p = self._peer(k)
        return pltpu.make_async_remote_copy(
            src_ref=self.x_refs[a].at[4 * p[0] + 2 * p[1] + p[2]], dst_ref=self.out_refs[a].at[self.me],
            send_sem=self.send_sems.at[a * N_COPY + k - 1], recv_sem=self.recv_sems.at[a * N_COPY + k - 1],
            device_id=p, device_id_type=MESH)

    def start(self):
        for k in range(1, N_DEV):
            for a in range(len(self.x_refs)):
                self._copy(a, k).start()

    def finish(self):
        for a in range(len(self.x_refs)):
            for k in range(1, N_DEV):
                self._copy(a, k).wait_recv()
            for k in range(1, N_DEV):
                self._copy(a, k).wait_send()


def _sibling_exchange(xs, name, gather=()):
    n, ng = len(xs), len(gather)

    def body(*refs):
        x_refs, g_in = refs[:n], refs[n:n + ng]
        out_refs, g_out = refs[n + ng:2 * n + ng], refs[2 * n + ng:2 * (n + ng)]
        send_sems, recv_sems = refs[2 * (n + ng):2 * (n + ng) + 2]
        g = _Gather(g_in, g_out, *refs[2 * (n + ng) + 2:]) if ng else None
        x, y, c = _my_pos()

        def push(a):
            return pltpu.make_async_remote_copy(
                src_ref=x_refs[a], dst_ref=out_refs[a], send_sem=send_sems.at[a], recv_sem=recv_sems.at[a],
                device_id=(x, y, 1 - c), device_id_type=MESH)

        if ng:
            g.start()
        for a in range(n):
            push(a).start()
        if ng:
            g.finish()
        for a in range(n):
            push(a).wait_recv()
            push(a).wait_send()

    out = pl.pallas_call(
        body, name=name, out_shape=[jax.ShapeDtypeStruct(v.shape, v.dtype) for v in xs] + _gathered_shapes(gather),
        in_specs=_comm_specs(n + ng), out_specs=_comm_specs(n + ng),
        scratch_shapes=[pltpu.SemaphoreType.DMA((n,)), pltpu.SemaphoreType.DMA((n,))] + (_comm_scratch(ng) if ng else []),
    )(*xs, *gather)
    return out[:n], out[n:]


def _scatter_and_gather(scatter, gather, name):
    ns, ng = len(scatter), len(gather)

    def body(*refs):
        s_in, g_in = refs[:ns], refs[ns:ns + ng]
        s_out, g_out = refs[ns + ng:2 * ns + ng], refs[2 * ns + ng:2 * (ns + ng)]
        s_send, s_recv, g_send, g_recv = refs[2 * (ns + ng):]
        g = _Gather(g_in, g_out, g_send, g_recv)
        t = _AllToAll(s_in, s_out, s_send, s_recv)
        g.start()
        t.start()
        g.finish()
        t.finish()

    out = pl.pallas_call(
        body, name=name,
        out_shape=[jax.ShapeDtypeStruct(v.shape, v.dtype) for v in scatter] + _gathered_shapes(gather),
        in_specs=_comm_specs(ns + ng), out_specs=_comm_specs(ns + ng),
        scratch_shapes=_comm_scratch(ns) + _comm_scratch(ng),
    )(*scatter, *gather)
    return out[:ns], out[ns:]


def _row_tile(seq, want):
    return min(want, seq)


def _local_step(x, ctx, tgt, mod_x, mod_c, wb, sink, gmlp_g, gmlp_b, w_s, b_s, ln1_g, ln1_b, ln2_g, ln2_b,
                later=None, me=None):
    seq = x.shape[0]
    on_mesh = me is not None
    modx1 = jnp.concatenate([mod_x[0:2], jnp.zeros((6, D), F32)], axis=0)
    modc = jnp.concatenate([mod_c[0:2], jnp.zeros((6, D), F32)], axis=0)
    vec = jnp.concatenate([mod_x[2:3], ln1_g, ln1_b, mod_x[3:6], ln2_g, ln2_b], axis=0)
    gp = jnp.concatenate([gmlp_g, gmlp_b, jnp.zeros((6, G_W), F32)], axis=0)
    ws_stack = w_s.reshape(N_GRP * BLK, BLK).astype(BF16)
    ws_stack_t = jnp.transpose(w_s, (2, 0, 1)).reshape(BLK, N_GRP * BLK).astype(BF16)
    bias_full = jnp.repeat(b_s.T, GRP_D, axis=1)
    cos, sin = _rope_tables(seq)
    w_in = wb["w_in"]
    w_kv = w_in[Q_W:Q_W + 2 * KV_W, :]
    tm_big = _row_tile(seq, 512)
    tm_ffn = _row_tile(seq, 256)

    hc, kvc, vac = _ctx_fwd(ctx, modc, w_kv)
    behind_proj = ("w_a", "w_b", "w_o") if on_mesh else ()
    behind_attn = ("w_fi",) if on_mesh else ()
    behind_mix = ("w_fo",) if on_mesh else ()
    wb = dict(wb)

    def whole(names, gathered):
        for n, g in zip(names, _with_own(list(gathered), [later[n] for n in names], me)):
            wb[n] = g.reshape(-1, g.shape[2]) if n in ROW_SHARDED else g.reshape(N_SHARD, 2 * g.shape[1], g.shape[2])

    (h, q, kv, va, uv, gab), got = _proj_fwd(x, modx1, w_in, cos, sin, _row_tile(seq, 1024), gather=[later[n] for n in behind_proj])
    whole(behind_proj, got)
    if on_mesh:
        for n in ("w_a", "w_b"):
            wb[n] = wb[n].transpose(1, 0, 2).reshape(wb[n].shape[1], D)
    (ya, lse), got = _attn_fwd(q, kv, va, kvc, vac, sink, gather=[later[n] for n in behind_attn])
    whole(behind_attn, got)
    (a, b, mix, merged, yb), got = _mix_fwd(uv, gab, ya, gp, ws_stack, bias_full, wb["w_a"], wb["w_b"], wb["w_o"], tm_big,
                                            gather=[later[n] for n in behind_mix])
    whole(behind_mix, got)
    act, h2, dff, df, dr1, st_ffn = _ffn(x, mix, tgt, vec, wb["w_fi"], wb["w_fo"], tm_ffn)
    blocks, recv = {}, {}
    blocks["w_fo"] = _eighths(_tn_matmul(act, df, 512, "tn_w_ffn_out", BF16, tk=2048))
    if on_mesh:
        g_w_fi, (recv["w_fo"],) = _tn_matmul(h2, dff, FH_SHARD, "tn_w_ffn_in", BF16, shard_major=True, tk=2048,
                                            scatter=[blocks["w_fo"]])
    else:
        g_w_fi = _tn_matmul(h2, dff, FH_SHARD, "tn_w_ffn_in", BF16, shard_major=True, tk=2048)
    blocks["w_fi"] = _eighths(g_w_fi)
    (dya, dpb, dws, dbs_full, st4, g_w_o, g_w_a, g_w_b), got = _mix_bwd(
        dr1, a, b, gab, uv, merged, ya, yb, vec, gp, ws_stack, ws_stack_t, bias_full, wb["w_a"], wb["w_b"], wb["w_o"],
        tm_big, scatter=[blocks["w_fi"]] if on_mesh else ())
    recv.update(zip(("w_fi",), got))
    shard_major = [g.reshape(g.shape[0], N_SHARD, D // N_SHARD).transpose(1, 0, 2) for g in (g_w_a, g_w_b)]
    blocks.update(w_o=_eighths(g_w_o), w_a=_eighths(shard_major[0]), w_b=_eighths(shard_major[1]))
    mixer = ("w_o", "w_a", "w_b") if on_mesh else ()
    (dq, dkv, dkvc, dsink), got = _attn_bwd(q, kv, kvc, sink, dya, ya, lse, scatter=[blocks[n] for n in mixer])
    recv.update(zip(mixer, got))
    g_wkv_ctx, st0 = _ctx_bwd(dkvc, ctx, hc, w_kv)
    (dqkv, grad_x, st1), _ = _proj_bwd(dq, dkv, dpb, x, dr1, modx1, w_in, cos, sin, tm_big)
    dbs = jnp.sum(dbs_full.reshape(BLK, N_GRP, GRP_D), axis=2).T
    early = [jnp.concatenate([st_ffn, st0], axis=0), st4, dsink, dws, dbs]
    init = jnp.pad(g_wkv_ctx, ((Q_W, 0), (0, 0)))
    g_qkv = _tn_matmul(dqkv, h, D, "tn_w_in_qkv", BF16, init=init, tk=1024)
    if on_mesh:
        g_rest, early_gathered = _tn_matmul(dpb, h, D, "tn_w_in_rest", BF16, tk=1024, gather=early)
    else:
        g_rest, early_gathered = _tn_matmul(dpb, h, D, "tn_w_in_rest", BF16, tk=1024), None
    blocks["w_in"] = _eighths(jnp.concatenate([g_qkv, g_rest], axis=0))
    return grad_x, dict(early=early, early_gathered=early_gathered, late=st1), blocks, recv


BIG = ("w_in", "w_a", "w_b", "w_o", "w_fi", "w_fo")
ROW_SHARDED = ("w_o", "w_fo")


def _half_of_shard(shard, c):
    r = shard.shape[0]
    return lax.dynamic_slice_in_dim(shard, c * (r // 2), r // 2, axis=0)


def _eighths(v):
    rows = v.shape[-2] * (v.shape[0] if v.ndim == 3 else 1)
    return v.reshape(N_DEV, rows // N_DEV, v.shape[-1])


def kernel(x, c, ctx, c_ctx, w_ada, b_ada, w_in, attn_sink, gmlp_ln_g, gmlp_ln_b, w_spatial, b_spatial, w_branch_a, w_branch_b, w_out, ln1_g, ln1_b, w_ffn_in, w_ffn_out, ln2_g, ln2_b, loss_target, m_c_ctx, m_w_ada, m_b_ada, m_w_in, m_attn_sink, m_gmlp_ln_g, m_gmlp_ln_b, m_w_spatial, m_b_spatial, m_w_branch_a, m_w_branch_b, m_w_out, m_ln1_g, m_ln1_b, m_w_ffn_in, m_w_ffn_out, m_ln2_g, m_ln2_b, v_c_ctx, v_w_ada, v_b_ada, v_w_in, v_attn_sink, v_gmlp_ln_g, v_gmlp_ln_b, v_w_spatial, v_b_spatial, v_w_branch_a, v_w_branch_b, v_w_out, v_ln1_g, v_ln1_b, v_w_ffn_in, v_w_ffn_out, v_ln2_g, v_ln2_b):
    mx, my, mc = _my_pos()
    me = 4 * mx + 2 * my + mc
    chip = 2 * mx + my
    shards = dict(w_in=w_in[0].T, w_a=w_branch_a[0], w_b=w_branch_b[0], w_o=w_out[0], w_fi=w_ffn_in[0], w_fo=w_ffn_out[0])

    halves = {n: _half_of_shard(shards[n], mc).astype(BF16) for n in BIG}
    c_rows = jnp.concatenate([c, jnp.zeros((7, D), F32)], axis=0)
    _, g_in, _, mod_g, sc_all = _gather_and_modulate(c_rows, c_ctx[None, :], halves["w_in"], w_ada[0])
    wb = dict(w_in=_with_own([g_in], [halves["w_in"]], me)[0].reshape(IN_W, D))
    mod_all = jnp.concatenate([mod_g[2 * s] for s in range(4)], axis=1) + b_ada
    mod_x = lax.dynamic_slice_in_dim(mod_all, me, 1, axis=0).reshape(6, D)
    mod_c = mod_all[8].reshape(6, D)[0:2]

    grad_x, small, blocks, recv = _local_step(
        x[0], ctx[0], loss_target[0], mod_x, mod_c, wb, attn_sink, gmlp_ln_g, gmlp_ln_b, w_spatial[0], b_spatial[0],
        ln1_g, ln1_b, ln2_g, ln2_b, later=halves, me=me)

    (recv["w_in"],), late = _scatter_and_gather([blocks["w_in"]], [small["late"]], "scatter_w_in_gather_small")
    late = _with_own(late, [small["late"]], me)[0]
    gathered = _with_own(small["early_gathered"], small["early"], me)
    gathered[0] = jnp.concatenate([gathered[0][:, :16], late, gathered[0][:, 16:]], axis=1)

    me_arr = jnp.reshape(me, (1,)).astype(jnp.int32)
    summed = {n: _sum_blocks(recv[n], blocks[n], me_arr, "sum_grads_" + n) for n in BIG}

    sums = _sum8_many(gathered, "sum_small")
    stats = sums[0]
    loss = 0.5 * jnp.sum(stats[ROW_LOSS]) / D
    dmod_x_all = jnp.concatenate([gathered[0][:, r_, :] for r_ in ROWS_DMOD_X], axis=1)
    dmod_c_full = jnp.concatenate([stats[r_] for r_ in ROWS_DMOD_C] + [jnp.zeros((4 * D,), F32)])
    dm_rows = jnp.concatenate([dmod_x_all, dmod_c_full[None, :], jnp.zeros((7, 6 * D), F32)], axis=0)
    cs = w_ada.shape[2]
    dm_shard = lax.dynamic_slice_in_dim(dm_rows, chip * cs, cs, axis=1)
    dmc_shard = jnp.concatenate([dm_shard[8:9], jnp.zeros((7, cs), F32)], axis=0)
    g_w_ada, part = _ada_bwd(sc_all.T, dm_shard, dmc_shard, w_ada[0])
    part = part * (mc == 0).astype(F32)
    theirs, part_all = _sibling_exchange([summed[n] for n in BIG], "exchange_grads", gather=[part])
    theirs = dict(zip(BIG, theirs))
    dsc = _sum8(_with_own(part_all, [part], me)[0], "sum_c_ctx")

    grads = dict(w_ada=g_w_ada[None])
    weights = dict(c_ctx=c_ctx, w_ada=w_ada, b_ada=b_ada, w_in=w_in, attn_sink=attn_sink, gmlp_ln_g=gmlp_ln_g,
                   gmlp_ln_b=gmlp_ln_b, w_spatial=w_spatial, b_spatial=b_spatial, w_branch_a=w_branch_a,
                   w_branch_b=w_branch_b, w_out=w_out, ln1_g=ln1_g, ln1_b=ln1_b, w_ffn_in=w_ffn_in, w_ffn_out=w_ffn_out,
                   ln2_g=ln2_g, ln2_b=ln2_b)
    ms = dict(c_ctx=m_c_ctx, w_ada=m_w_ada, b_ada=m_b_ada, w_in=m_w_in, attn_sink=m_attn_sink, gmlp_ln_g=m_gmlp_ln_g,
              gmlp_ln_b=m_gmlp_ln_b, w_spatial=m_w_spatial, b_spatial=m_b_spatial, w_branch_a=m_w_branch_a,
              w_branch_b=m_w_branch_b, w_out=m_w_out, ln1_g=m_ln1_g, ln1_b=m_ln1_b, w_ffn_in=m_w_ffn_in,
              w_ffn_out=m_w_ffn_out, ln2_g=m_ln2_g, ln2_b=m_ln2_b)
    vs = dict(c_ctx=v_c_ctx, w_ada=v_w_ada, b_ada=v_b_ada, w_in=v_w_in, attn_sink=v_attn_sink, gmlp_ln_g=v_gmlp_ln_g,
              gmlp_ln_b=v_gmlp_ln_b, w_spatial=v_w_spatial, b_spatial=v_b_spatial, w_branch_a=v_w_branch_a,
              w_branch_b=v_w_branch_b, w_out=v_w_out, ln1_g=v_ln1_g, ln1_b=v_ln1_b, w_ffn_in=v_w_ffn_in,
              w_ffn_out=v_w_ffn_out, ln2_g=v_ln2_g, ln2_b=v_ln2_b)
    order = list(weights)
    delta, new_m, new_v = {}, {}, {}
    d_, m_, v_ = _adamw(w_ada[0], g_w_ada, m_w_ada[0], v_w_ada[0], "adamw_w_ada")
    delta["w_ada"], new_m["w_ada"], new_v["w_ada"] = d_[None], m_[None], v_[None]
    c_arr = jnp.reshape(mc, (1,)).astype(jnp.int32)
    names = dict(w_in="w_in", w_a="w_branch_a", w_b="w_branch_b", w_o="w_out", w_fi="w_ffn_in", w_fo="w_ffn_out")
    for k, n in names.items():
        flip = (lambda t: t.T) if k == "w_in" else (lambda t: t)
        outs = _adamw_halves(flip(weights[n][0]), summed[k], theirs[k], flip(ms[n][0]), flip(vs[n][0]), c_arr, "adamw_" + n)
        grads[n], delta[n], new_m[n], new_v[n] = [flip(t)[None] for t in outs]

    def view(a):
        return a.reshape(-1, a.shape[-1]) if a.ndim != 1 else a.reshape(1, -1)

    small = _adamw_small(sums, dsc, *[{n: view(d[n]) for n in SMALL} for d in (weights, ms, vs)])
    for out, src in zip((grads, delta, new_m, new_v), small):
        for n in SMALL:
            out[n] = src[n].reshape(weights[n].shape)

    return (loss, grad_x[None], *[grads[n] for n in order], *[delta[n] for n in order],
            *[new_m[n] for n in order], *[new_v[n] for n in order])
```

```python
import math

import jax
import jax.numpy as jnp
from jax import lax
from jax.experimental import pallas as pl
from jax.experimental.pallas import tpu as pltpu

F32 = jnp.float32
BF16 = jnp.bfloat16

D = 1024
HEAD = 64
N_KV = 2
GROUP = 4
Q_W = 512
KV_W = 128
G_W = 512
BLK = 128
N_GRP = 8
GRP_D = 64
FH = 2816
IN_W = 3840
GRID_W = 64
ROPE_BASE = 10000.0
LN_EPS = 1e-5
NEG = -1e30
ALPHA = (2 * 1) ** 0.25
SCALE = HEAD ** -0.5
GELU_K = math.sqrt(2.0 / math.pi)
GELU_A = 0.044715
ADAM_LR = 0.001
ADAM_B1 = 0.9
ADAM_B2 = 0.999
ADAM_EPS = 1e-08
ADAM_WD = 0.01
ADAM_STEP = 10
N_DEV = 8
N_SHARD = 4
FH_SHARD = FH // 2
LANES = 128
VMEM_LIMIT = 56 * 1024 * 1024
MESH = pl.DeviceIdType.MESH


def _cp(*sem):
    return pltpu.CompilerParams(dimension_semantics=sem, vmem_limit_bytes=VMEM_LIMIT)


def _resident(shape):
    return pl.BlockSpec(shape, lambda *_: (0,) * len(shape), pipeline_mode=pl.Buffered(1))


def _rows(tm, width):
    return pl.BlockSpec((tm, width), lambda i: (i, 0))


def _acc(shape):
    return pl.BlockSpec(shape, lambda *_: (0,) * len(shape))


def _dot(a, b):
    return jnp.dot(a, b, preferred_element_type=F32)


def _dot_nt(a, b):
    return lax.dot_general(a, b, (((1,), (1,)), ((), ())), preferred_element_type=F32)


def _dot_tn(a, b):
    return lax.dot_general(a, b, (((0,), (0,)), ((), ())), preferred_element_type=F32)


def _ln(x):
    mu = jnp.mean(x, axis=-1, keepdims=True)
    xc = x - mu
    var = jnp.mean(xc * xc, axis=-1, keepdims=True)
    rstd = lax.rsqrt(var + LN_EPS)
    return xc * rstd, rstd


def _ln_bwd(dxhat, xhat, rstd):
    return (dxhat - jnp.mean(dxhat, axis=-1, keepdims=True)
            - xhat * jnp.mean(dxhat * xhat, axis=-1, keepdims=True)) * rstd


def _sig(x):
    return 0.5 + 0.5 * jnp.tanh(0.5 * x)


def _gelu(x):
    t = jnp.tanh(x * (GELU_K + (GELU_K * GELU_A) * (x * x)))
    hx = 0.5 * x
    return hx + hx * t, t


def _gelu_grad(x, t):
    return 0.5 + 0.5 * t + (0.5 * x) * (1.0 - t * t) * (GELU_K + (3.0 * GELU_K * GELU_A) * (x * x))


def _colsum(v):
    return jnp.sum(v, axis=0, keepdims=True)


def _partner(x):
    w = x.shape[1]
    lane = lax.broadcasted_iota(jnp.int32, x.shape, 1)
    return jnp.where((lane & 31) < 16, pltpu.roll(x, w - 16, 1), pltpu.roll(x, 16, 1))


def _rope(x, cos, sin):
    return x * cos + _partner(x) * sin


def _unrope(g, cos, sin):
    return g * cos + _partner(g * sin)


def _rope_tables(seq):
    inv = ROPE_BASE ** (-jnp.arange(HEAD // 4, dtype=F32) / (HEAD // 4))
    pos = jnp.arange(seq, dtype=jnp.int32)
    ar = (pos // GRID_W).astype(F32)[:, None] * inv
    ac = (pos % GRID_W).astype(F32)[:, None] * inv
    cos = jnp.concatenate([jnp.cos(ar), jnp.cos(ar), jnp.cos(ac), jnp.cos(ac)], axis=-1)
    sin = jnp.concatenate([-jnp.sin(ar), jnp.sin(ar), -jnp.sin(ac), jnp.sin(ac)], axis=-1)
    return jnp.tile(cos, (1, LANES // HEAD)), jnp.tile(sin, (1, LANES // HEAD))


def _ctx_fwd(ctx, modc, w_kv):
    n_ctx = ctx.shape[0]

    def body(ctx_ref, mod_ref, w_ref, hc_ref, kvc_ref, vac_ref):
        xhat, _ = _ln(ctx_ref[...])
        hc = (xhat * (1.0 + mod_ref[1:2, :]) + mod_ref[0:1, :]).astype(BF16)
        hc_ref[...] = hc
        kvc = _dot_nt(hc, w_ref[...]).astype(BF16)
        kvc_ref[...] = kvc
        vac_ref[...] = _with_ones(kvc[:, KV_W:])

    return pl.pallas_call(
        body, name="ctx_fwd", grid=(1,),
        in_specs=[_acc((n_ctx, D)), _acc((8, D)), _acc((2 * KV_W, D))],
        out_specs=[_acc((n_ctx, D)), _acc((n_ctx, 2 * KV_W)), _acc((n_ctx, 2 * LANES))],
        out_shape=[jax.ShapeDtypeStruct((n_ctx, D), BF16), jax.ShapeDtypeStruct((n_ctx, 2 * KV_W), BF16),
                   jax.ShapeDtypeStruct((n_ctx, 2 * LANES), BF16)],
        compiler_params=_cp("arbitrary"),
    )(ctx, modc, w_kv)


def _host_start(step, comm):
    if comm is not None:
        @pl.when(step == 0)
        def _():
            comm.start()


def _host_finish(step, last, comm, forward_at=None):
    if comm is None:
        return
    if forward_at is None or forward_at >= last:
        @pl.when(step == last)
        def _():
            comm.finish()
    else:
        @pl.when(step == forward_at)
        def _():
            comm.forward()

        @pl.when(step == last)
        def _():
            comm.drain()


def _proj_fwd(x, modx, w_in, cos, sin, tm, gather=()):
    seq = x.shape[0]
    ng = len(gather)

    def body(x_ref, mod_ref, w_ref, cos_ref, sin_ref, *rest):
        h_ref, q_ref, kv_ref, va_ref, uv_ref, gab_ref = rest[ng:ng + 6]
        comm = _Gather(rest[:ng], rest[ng + 6:2 * ng + 6], *rest[2 * ng + 6:]) if ng else None
        _host_start(pl.program_id(0), comm)
        xhat, _ = _ln(x_ref[...])
        h = (xhat * (1.0 + mod_ref[1:2, :]) + mod_ref[0:1, :]).astype(BF16)
        h_ref[...] = h
        cos1, sin1 = cos_ref[...], sin_ref[...]
        cos2 = jnp.concatenate([cos1, cos1], axis=1)
        sin2 = jnp.concatenate([sin1, sin1], axis=1)
        for j in range(Q_W // 256):
            t = _dot_nt(h, w_ref[256 * j:256 * (j + 1), :])
            q_ref[:, 256 * j:256 * (j + 1)] = (_rope(t, cos2, sin2) * SCALE).astype(BF16)
        t = _dot_nt(h, w_ref[Q_W:Q_W + 2 * KV_W, :])
        kv_ref[:, :KV_W] = _rope(t[:, :KV_W], cos1, sin1).astype(BF16)
        v = t[:, KV_W:].astype(BF16)
        kv_ref[:, KV_W:] = v
        va_ref[...] = _with_ones(v)
        o = Q_W + 2 * KV_W
        for j in range(2):
            uv_ref[:, G_W * j:G_W * (j + 1)] = _dot_nt(h, w_ref[o + G_W * j:o + G_W * (j + 1), :]).astype(BF16)
        o += 2 * G_W
        for j in range(4):
            gab_ref[:, 512 * j:512 * (j + 1)] = _dot_nt(h, w_ref[o + 512 * j:o + 512 * (j + 1), :]).astype(BF16)
        _host_finish(pl.program_id(0), seq // tm - 1, comm, forward_at=(seq // tm) // 2)

    out = pl.pallas_call(
        body, name="proj_fwd", grid=(seq // tm,),
        in_specs=[_rows(tm, D), _acc((8, D)), _resident((IN_W, D)), _rows(tm, LANES), _rows(tm, LANES)] + _comm_specs(ng),
        out_specs=[_rows(tm, D), _rows(tm, Q_W), _rows(tm, 2 * KV_W), _rows(tm, 2 * LANES), _rows(tm, 2 * G_W),
                   _rows(tm, 2 * D)] + _comm_specs(ng),
        out_shape=[jax.ShapeDtypeStruct((seq, D), BF16), jax.ShapeDtypeStruct((seq, Q_W), BF16),
                   jax.ShapeDtypeStruct((seq, 2 * KV_W), BF16), jax.ShapeDtypeStruct((seq, 2 * LANES), BF16),
                   jax.ShapeDtypeStruct((seq, 2 * G_W), BF16), jax.ShapeDtypeStruct((seq, 2 * D), BF16)] + _gathered_shapes(gather),
        scratch_shapes=_comm_scratch(ng) if ng else [],
        compiler_params=_cp("arbitrary"),
    )(x, modx, w_in, cos, sin, *gather)
    return out[:6], out[6:]


def _stack_heads(x, hk):
    return jnp.concatenate([x[:, (hk * GROUP + g) * HEAD:(hk * GROUP + g + 1) * HEAD] for g in range(GROUP)], axis=0)


def _band_masks(n, nb):
    rows = GROUP * BLK
    qi = lax.broadcasted_iota(jnp.int32, (rows, BLK), 0) & (BLK - 1)
    kj = lax.broadcasted_iota(jnp.int32, (rows, BLK), 1)
    return (kj >= qi) & (n > 0), (kj <= qi) & (n < nb - 1)


def _attn_scores(q, k_refs, hk, masks):
    q4 = _stack_heads(q, hk)
    ks = [r[:, hk * HEAD:(hk + 1) * HEAD] for r in k_refs]
    s = [_dot_nt(q4, k) for k in ks]
    s[1] = jnp.where(masks[0], s[1], NEG)
    s[3] = jnp.where(masks[1], s[3], NEG)
    return q4, ks, s


def _sink_rows(sink_ref, hk):
    rows = GROUP * BLK
    rg = lax.broadcasted_iota(jnp.int32, (rows, 1), 0) >> 7
    sink_v = jnp.full((rows, 1), sink_ref[0, hk * GROUP], F32)
    for g in range(1, GROUP):
        sink_v = jnp.where(rg == g, sink_ref[0, hk * GROUP + g], sink_v)
    return sink_v


def _with_ones(v):
    ones = jnp.ones((v.shape[0], HEAD), v.dtype)
    return jnp.concatenate([v[:, :HEAD], ones, v[:, HEAD:], ones], axis=1)


def _kv_specs(nb, qb):
    def spec(d):
        return pl.BlockSpec((BLK, 2 * KV_W), lambda n: (jnp.clip(qb * n + d, 0, nb - 1), 0))
    return [spec(d) for d in range(-1, qb + 1)]


def _attn_fwd(q, kv, va, kvc, vac, sink, gather=()):
    seq = q.shape[0]
    nb = seq // BLK
    n_ctx = kvc.shape[0]
    ng = len(gather)
    Q_BLOCKS = 1
    nkv = Q_BLOCKS + 2
    steps = nb // Q_BLOCKS

    def body(q_ref, *rest):
        kv_refs, va_refs = rest[:nkv], rest[nkv:2 * nkv]
        kvc_ref, vac_ref, sink_ref = rest[2 * nkv:2 * nkv + 3]
        rest = rest[2 * nkv + 3:]
        o_ref, lse_ref = rest[ng:ng + 2]
        comm = _Gather(rest[:ng], rest[ng + 2:2 * ng + 2], *rest[2 * ng + 2:]) if ng else None
        n = pl.program_id(0)
        _host_start(n, comm)
        lane = lax.broadcasted_iota(jnp.int32, (BLK, LANES), 1)
        for sub in range(Q_BLOCKS):
            rs = slice(sub * BLK, (sub + 1) * BLK)
            q = q_ref[rs, :]
            outs = []
            lse_all = jnp.zeros((BLK, LANES), F32)
            masks = _band_masks(Q_BLOCKS * n + sub, nb)
            for hk in range(N_KV):
                _, _, s = _attn_scores(q, (kvc_ref,) + kv_refs[sub:sub + 3], hk, masks)
                sink_v = _sink_rows(sink_ref, hk)
                tile_max = s[1]
                for t in [s[0][:, i * LANES:(i + 1) * LANES] for i in range(n_ctx // LANES)] + s[2:]:
                    tile_max = jnp.maximum(tile_max, t)
                m = jnp.maximum(sink_v, jnp.max(tile_max, axis=-1, keepdims=True))
                o = jnp.zeros((GROUP * BLK, LANES), F32)
                for t, va_ref in zip(s, (vac_ref,) + va_refs[sub:sub + 3]):
                    o = o + _dot(jnp.exp((t - m).astype(BF16)), va_ref[:, hk * LANES:(hk + 1) * LANES])
                denom = o[:, HEAD:HEAD + 1] + jnp.exp(sink_v - m)
                o4 = o[:, :HEAD] * (1.0 / denom)
                lse4 = m + jnp.log(denom)
                for g in range(GROUP):
                    outs.append(o4[g * BLK:(g + 1) * BLK, :])
                    lse_all = jnp.where(lane == hk * GROUP + g, lse4[g * BLK:(g + 1) * BLK, :], lse_all)
            o_ref[rs, :] = jnp.concatenate(outs, axis=1).astype(BF16)
            lse_ref[rs, :] = lse_all
        _host_finish(n, steps - 1, comm, forward_at=(3 * steps) // 4)

    tq = Q_BLOCKS * BLK
    out = pl.pallas_call(
        body, name="attn_fwd", grid=(steps,),
        in_specs=[_rows(tq, Q_W)] + _kv_specs(nb, Q_BLOCKS) + _kv_specs(nb, Q_BLOCKS)
        + [_acc((n_ctx, 2 * KV_W)), _acc((n_ctx, 2 * LANES)), pl.BlockSpec(memory_space=pltpu.SMEM)] + _comm_specs(ng),
        out_specs=[_rows(tq, Q_W), _rows(tq, LANES)] + _comm_specs(ng),
        out_shape=[jax.ShapeDtypeStruct((seq, Q_W), BF16), jax.ShapeDtypeStruct((seq, LANES), F32)] + _gathered_shapes(gather),
        scratch_shapes=_comm_scratch(ng) if ng else [],
        compiler_params=_cp("arbitrary"),
    )(q, *([kv] * nkv), *([va] * nkv), kvc, vac, sink, *gather)
    return out[:2], out[2:]


def _gmlp_chunk(u, vb, gp_ref, ws_ref, bias_ref):
    gu, tu = _gelu(u)
    gv, tv = _gelu(vb)
    vhat, rstd = _ln(gv)
    vn = (vhat * gp_ref[0:1, :] + gp_ref[1:2, :]).astype(BF16)
    s = bias_ref[...] + jnp.concatenate(
        [_dot(ws_ref[g * BLK:(g + 1) * BLK, :], vn[:, g * GRP_D:(g + 1) * GRP_D]) for g in range(N_GRP)], axis=1)
    return gu, tu, tv, vhat, rstd, vn, s


def _mix_fwd(uv, gab, ya, gp, ws_stack, bias_full, w_a, w_b, w_o, tm, gather=()):
    seq = uv.shape[0]
    ng = len(gather)
    steps = seq // tm

    def body(uv_ref, gab_ref, ya_ref, gp_ref, ws_ref, bias_ref, wa_ref, wb_ref, wo_ref, *rest):
        a_ref, b_ref, mix_ref, merged_ref, yb_ref = rest[ng:ng + 5]
        comm = _Gather(rest[:ng], rest[ng + 5:2 * ng + 5], *rest[2 * ng + 5:]) if ng else None
        _host_start(pl.program_id(0), comm)
        for c in range(tm // BLK):
            rs = slice(c * BLK, (c + 1) * BLK)
            gu, _, _, _, _, _, s = _gmlp_chunk(uv_ref[rs, :G_W].astype(F32), uv_ref[rs, G_W:].astype(F32), gp_ref, ws_ref, bias_ref)
            yb_ref[rs, :] = (gu * s).astype(BF16)
        a = _dot(ya_ref[...], wa_ref[...])
        b = _dot(yb_ref[...], wb_ref[...])
        a_ref[...] = a.astype(BF16)
        b_ref[...] = b.astype(BF16)
        merged = (_sig(gab_ref[:, :D].astype(F32)) * a + _sig(gab_ref[:, D:].astype(F32)) * b).astype(BF16)
        merged_ref[...] = merged
        mix_ref[...] = _dot(merged, wo_ref[...])
        _host_finish(pl.program_id(0), steps - 1, comm, forward_at=(3 * steps) // 4)

    out = pl.pallas_call(
        body, name="mix_fwd", grid=(steps,),
        in_specs=[_rows(tm, 2 * G_W), _rows(tm, 2 * D), _rows(tm, Q_W), _acc((8, G_W)),
                  _resident((N_GRP * BLK, BLK)), _acc((BLK, G_W)),
                  _resident((Q_W, D)), _resident((G_W, D)), _resident((D, D))] + _comm_specs(ng),
        out_specs=[_rows(tm, D), _rows(tm, D), _rows(tm, D), _rows(tm, D), _rows(tm, G_W)] + _comm_specs(ng),
        out_shape=[jax.ShapeDtypeStruct((seq, D), BF16), jax.ShapeDtypeStruct((seq, D), BF16),
                   jax.ShapeDtypeStruct((seq, D), F32), jax.ShapeDtypeStruct((seq, D), BF16),
                   jax.ShapeDtypeStruct((seq, G_W), BF16)] + _gathered_shapes(gather),
        scratch_shapes=_comm_scratch(ng) if ng else [],
        compiler_params=_cp("arbitrary"),
    )(uv, gab, ya, gp, ws_stack, bias_full, w_a, w_b, w_o, *gather)
    return out[:5], out[5:]


def _mid_recompute(x_ref, mix_ref, vec_ref):
    r1 = ALPHA * x_ref[...] + vec_ref[0:1, :] * mix_ref[...]
    xh1, rstd1 = _ln(r1)
    xmid = xh1 * vec_ref[1:2, :] + vec_ref[2:3, :]
    xh2, rstd2 = _ln(xmid)
    return xh1, rstd1, xmid, xh2, rstd2


def _ffn(x, mix, tgt, vec, w_fi, w_fo, tm):
    seq = x.shape[0]

    def body(x_ref, mix_ref, tgt_ref, vec_ref, wi_ref, wo_ref, act_ref, h2_ref, dff_ref, df_ref, dr1_ref, st_ref, gu_ref):
        @pl.when(pl.program_id(0) == 0)
        def _():
            st_ref[...] = jnp.zeros_like(st_ref)

        xh1, rstd1, xmid, xh2, rstd2 = _mid_recompute(x_ref, mix_ref, vec_ref)
        h2 = (xh2 * (1.0 + vec_ref[4:5, :]) + vec_ref[3:4, :]).astype(BF16)
        h2_ref[...] = h2
        halves = [(slice(hh * FH_SHARD, (hh + 1) * FH_SHARD), slice(FH + hh * FH_SHARD, FH + (hh + 1) * FH_SHARD))
                  for hh in range(2)]
        for hh, (cs, cu) in enumerate(halves):
            g = _dot(h2, wi_ref[hh])
            u = _dot(h2, wi_ref[2 + hh])
            gu_ref[:, cs] = g
            gu_ref[:, cu] = u
            act_ref[:, cs] = (g * _sig(g) * u).astype(BF16)
        f = _dot(act_ref[...], wo_ref[...])
        r2 = ALPHA * xmid + vec_ref[5:6, :] * f
        yh, rstd = _ln(r2)
        y = yh * vec_ref[6:7, :] + vec_ref[7:8, :]
        err = y - tgt_ref[...]
        dy = err / D
        dr2 = _ln_bwd(dy * vec_ref[6:7, :], yh, rstd)
        st_ref[0:1, :] += _colsum(err * err)
        st_ref[1:2, :] += _colsum(dy * yh)
        st_ref[2:3, :] += _colsum(dy)
        st_ref[3:4, :] += _colsum(dr2 * f)

        df = (dr2 * vec_ref[5:6, :]).astype(BF16)
        df_ref[...] = df
        da_all = _dot_nt(df, wo_ref[...])
        for cs, cu in halves:
            da = da_all[:, cs]
            g = gu_ref[:, cs]
            u = gu_ref[:, cu]
            sg = _sig(g)
            dff_ref[:, cs] = (da * u * sg * (1.0 + g * (1.0 - sg))).astype(BF16)
            dff_ref[:, cu] = (da * g * sg).astype(BF16)
        dh2 = _dot_nt(dff_ref[:, :FH_SHARD], wi_ref[0])
        for s in range(1, N_SHARD):
            dh2 = dh2 + _dot_nt(dff_ref[:, s * FH_SHARD:(s + 1) * FH_SHARD], wi_ref[s])
        dxmid = _ln_bwd(dh2 * (1.0 + vec_ref[4:5, :]), xh2, rstd2) + ALPHA * dr2
        dr1 = _ln_bwd(dxmid * vec_ref[1:2, :], xh1, rstd1)
        dr1_ref[...] = dr1
        st_ref[8:9, :] += _colsum(dh2 * xh2)
        st_ref[9:10, :] += _colsum(dh2)
        st_ref[10:11, :] += _colsum(dxmid * xh1)
        st_ref[11:12, :] += _colsum(dxmid)
        st_ref[12:13, :] += _colsum(dr1 * mix_ref[...])

    return pl.pallas_call(
        body, name="ffn", grid=(seq // tm,),
        in_specs=[_rows(tm, D), _rows(tm, D), _rows(tm, D), _acc((8, D)), _resident((N_SHARD, D, FH_SHARD)), _resident((FH, D))],
        out_specs=[_rows(tm, FH), _rows(tm, D), _rows(tm, 2 * FH), _rows(tm, D), _rows(tm, D), _acc((16, D))],
        out_shape=[jax.ShapeDtypeStruct((seq, FH), BF16), jax.ShapeDtypeStruct((seq, D), BF16),
                   jax.ShapeDtypeStruct((seq, 2 * FH), BF16), jax.ShapeDtypeStruct((seq, D), BF16),
                   jax.ShapeDtypeStruct((seq, D), F32), jax.ShapeDtypeStruct((16, D), F32)],
        scratch_shapes=[pltpu.VMEM((tm, 2 * FH), F32)],
        compiler_params=_cp("arbitrary"),
    )(x, mix, tgt, vec, w_fi, w_fo)


def _mix_bwd(dr1, a, b, gab, uv, merged, ya, yb, vec, gp, ws_stack, ws_stack_t, bias_full, w_a, w_b, w_o, tm, scatter=()):
    seq = dr1.shape[0]
    last = seq // tm - 1
    ns = len(scatter)

    def body(dr1_ref, a_ref, b_ref, gab_ref, uv_ref, mg_ref, ya_ref, yb_ref, vec_ref, gp_ref, ws_ref, wst_ref, bias_ref,
             wa_ref, wb_ref, wo_ref, *rest):
        dya_ref, dp_ref, dws_ref, dbs_ref, st_ref, gwo_ref, gwa_ref, gwb_ref = rest[ns:ns + 8]
        acc_o, acc_a, acc_b = rest[2 * ns + 8:2 * ns + 11]
        comm = _AllToAll(rest[:ns], rest[ns + 8:2 * ns + 8], *rest[2 * ns + 11:]) if ns else None
        _host_start(pl.program_id(0), comm)

        @pl.when(pl.program_id(0) == 0)
        def _():
            dws_ref[...] = jnp.zeros_like(dws_ref)
            dbs_ref[...] = jnp.zeros_like(dbs_ref)
            st_ref[...] = jnp.zeros_like(st_ref)
            acc_o[...] = jnp.zeros_like(acc_o)
            acc_a[...] = jnp.zeros_like(acc_a)
            acc_b[...] = jnp.zeros_like(acc_b)

        dmix = (dr1_ref[...] * vec_ref[0:1, :]).astype(BF16)
        acc_o[...] += _dot_tn(mg_ref[...], dmix)
        dmerged = _dot_nt(dmix, wo_ref[...])
        sa = _sig(gab_ref[:, :D].astype(F32))
        sb = _sig(gab_ref[:, D:].astype(F32))
        da_f = dmerged * sa
        db_f = dmerged * sb
        da = da_f.astype(BF16)
        db = db_f.astype(BF16)
        dp_ref[:, 2 * G_W:2 * G_W + D] = (da_f * a_ref[...].astype(F32) * (1.0 - sa)).astype(BF16)
        dp_ref[:, 2 * G_W + D:] = (db_f * b_ref[...].astype(F32) * (1.0 - sb)).astype(BF16)
        dya_ref[...] = _dot_nt(da, wa_ref[...]).astype(BF16)
        dyb = _dot_nt(db, wb_ref[...])
        acc_a[...] += _dot_tn(ya_ref[...], da)
        acc_b[...] += _dot_tn(yb_ref[...], db)

        @pl.when(pl.program_id(0) == last)
        def _():
            gwo_ref[...] = acc_o[...].astype(BF16)
            gwa_ref[...] = acc_a[...].astype(BF16)
            gwb_ref[...] = acc_b[...].astype(BF16)

        for c in range(tm // BLK):
            rs = slice(c * BLK, (c + 1) * BLK)
            u = uv_ref[rs, :G_W].astype(F32)
            vb = uv_ref[rs, G_W:].astype(F32)
            gu, tu, tv, vhat, rstd, vn, s = _gmlp_chunk(u, vb, gp_ref, ws_ref, bias_ref)
            dyb_c = dyb[rs, :]
            ds = dyb_c * gu
            du = dyb_c * s * _gelu_grad(u, tu)
            ds_b = ds.astype(BF16)
            dvn_g = []
            for g in range(N_GRP):
                cg = slice(g * GRP_D, (g + 1) * GRP_D)
                dvn_g.append(_dot(wst_ref[:, g * BLK:(g + 1) * BLK], ds_b[:, cg]))
                dws_ref[g * BLK:(g + 1) * BLK, :] += _dot_nt(ds_b[:, cg], vn[:, cg])
            dvn = jnp.concatenate(dvn_g, axis=1)
            dbs_ref[...] += ds
            st_ref[0:1, :] += _colsum(dvn * vhat)
            st_ref[1:2, :] += _colsum(dvn)
            dgv = _ln_bwd(dvn * gp_ref[0:1, :], vhat, rstd)
            dvb = dgv * _gelu_grad(vb, tv)
            dp_ref[rs, :G_W] = du.astype(BF16)
            dp_ref[rs, G_W:2 * G_W] = dvb.astype(BF16)
        _host_finish(pl.program_id(0), last, comm)

    pw = 2 * G_W + 2 * D
    out = pl.pallas_call(
        body, name="mix_bwd", grid=(seq // tm,),
        in_specs=[_rows(tm, D), _rows(tm, D), _rows(tm, D), _rows(tm, 2 * D), _rows(tm, 2 * G_W), _rows(tm, D), _rows(tm, Q_W),
                  _rows(tm, G_W), _acc((8, D)), _acc((8, G_W)),
                  _resident((N_GRP * BLK, BLK)), _resident((BLK, N_GRP * BLK)), _acc((BLK, G_W)),
                  _resident((Q_W, D)), _resident((G_W, D)), _resident((D, D))] + _comm_specs(ns),
        out_specs=[_rows(tm, Q_W), _rows(tm, pw), _acc((N_GRP * BLK, BLK)), _acc((BLK, G_W)), _acc((8, G_W)),
                   _acc((D, D)), _acc((Q_W, D)), _acc((G_W, D))] + _comm_specs(ns),
        out_shape=[jax.ShapeDtypeStruct((seq, Q_W), BF16), jax.ShapeDtypeStruct((seq, pw), BF16),
                   jax.ShapeDtypeStruct((N_GRP * BLK, BLK), F32), jax.ShapeDtypeStruct((BLK, G_W), F32),
                   jax.ShapeDtypeStruct((8, G_W), F32), jax.ShapeDtypeStruct((D, D), BF16),
                   jax.ShapeDtypeStruct((Q_W, D), BF16), jax.ShapeDtypeStruct((G_W, D), BF16)]
        + [jax.ShapeDtypeStruct(v.shape, v.dtype) for v in scatter],
        scratch_shapes=[pltpu.VMEM((D, D), F32), pltpu.VMEM((Q_W, D), F32), pltpu.VMEM((G_W, D), F32)]
        + (_comm_scratch(ns) if ns else []),
        compiler_params=_cp("arbitrary"),
    )(dr1, a, b, gab, uv, merged, ya, yb, vec, gp, ws_stack, ws_stack_t, bias_full, w_a, w_b, w_o, *scatter)
    return out[:8], out[8:]


def _attn_bwd(q, kv, kvc, sink, dya, ya, lse, scatter=()):
    seq = q.shape[0]
    nb = seq // BLK
    n_ctx = kvc.shape[0]
    ns = len(scatter)
    Q_BLOCKS = 2
    nkv = Q_BLOCKS + 2
    steps = nb // Q_BLOCKS

    def body(q_ref, *rest):
        kv_refs = rest[:nkv]
        kvc_ref, sink_ref, do_ref, o_ref, lse_ref = rest[nkv:nkv + 5]
        rest = rest[nkv + 5:]
        dq_ref, dkv_ref, dkvc_ref, dsink_ref = rest[ns:ns + 4]
        comm = _AllToAll(rest[:ns], rest[ns + 4:2 * ns + 4], *rest[2 * ns + 4:]) if ns else None
        n = pl.program_id(0)
        _host_start(n, comm)

        @pl.when(n == 0)
        def _():
            dkv_ref[...] = jnp.zeros_like(dkv_ref)
            dkvc_ref[...] = jnp.zeros_like(dkvc_ref)
            dsink_ref[...] = jnp.zeros_like(dsink_ref)

        lane = lax.broadcasted_iota(jnp.int32, (1, LANES), 1)
        for sub in range(Q_BLOCKS):
            rs = slice(sub * BLK, (sub + 1) * BLK)
            blk = Q_BLOCKS * n + sub
            q = q_ref[rs, :]
            do = do_ref[rs, :]
            out = o_ref[rs, :]
            lse_all = lse_ref[rs, :]
            k_refs = (kvc_ref,) + kv_refs[sub:sub + 3]
            masks = _band_masks(blk, nb)
            dqs, dks, dvs = [], [], []
            for hk in range(N_KV):
                q4, ks, s = _attn_scores(q, k_refs, hk, masks)
                vs = [r[:, KV_W + hk * HEAD:KV_W + (hk + 1) * HEAD] for r in k_refs]
                lse4 = jnp.concatenate([lse_all[:, hk * GROUP + g:hk * GROUP + g + 1] for g in range(GROUP)], axis=0)
                do4 = _stack_heads(do, hk)
                delta = jnp.sum(do4.astype(F32) * _stack_heads(out, hk).astype(F32), axis=-1, keepdims=True)
                p = [jnp.exp((t - lse4).astype(BF16)) for t in s]
                ds = [t * (_dot_nt(do4, v) - delta).astype(BF16) for t, v in zip(p, vs)]
                dq4 = _dot(ds[0], ks[0])
                for t, k in zip(ds[1:], ks[1:]):
                    dq4 = dq4 + _dot(t, k)
                dq4 = dq4 * SCALE
                dqs += [dq4[g * BLK:(g + 1) * BLK, :] for g in range(GROUP)]
                dks.append([_dot_tn(t, q4) for t in ds])
                dvs.append([_dot_tn(t, do4) for t in p])
                ps = jnp.exp(_sink_rows(sink_ref, hk) - lse4) * delta
                for g in range(GROUP):
                    part = -jnp.sum(ps[g * BLK:(g + 1) * BLK, :], axis=0, keepdims=True)
                    dsink_ref[0:1, :] += jnp.where(lane == hk * GROUP + g, part, 0.0)
            dq_ref[rs, :] = jnp.concatenate(dqs, axis=1)

            def piece(i):
                return jnp.concatenate([dks[0][i], dks[1][i], dvs[0][i], dvs[1][i]], axis=1)

            dkvc_ref[...] += piece(0)
            starts = (jnp.maximum(blk - 1, 0), blk, jnp.minimum(blk + 1, nb - 1))
            for i, st in enumerate(starts):
                r = pl.ds(pl.multiple_of(st * BLK, BLK), BLK)
                dkv_ref[r, :] += piece(i + 1)
        _host_finish(n, steps - 1, comm)

    tq = Q_BLOCKS * BLK
    out = pl.pallas_call(
        body, name="attn_bwd", grid=(steps,),
        in_specs=[_rows(tq, Q_W)] + _kv_specs(nb, Q_BLOCKS) + [_acc((n_ctx, 2 * KV_W)), pl.BlockSpec(memory_space=pltpu.SMEM),
                                                     _rows(tq, Q_W), _rows(tq, Q_W), _rows(tq, LANES)] + _comm_specs(ns),
        out_specs=[_rows(tq, Q_W), _acc((seq, 2 * KV_W)), _acc((n_ctx, 2 * KV_W)), _acc((8, LANES))] + _comm_specs(ns),
        out_shape=[jax.ShapeDtypeStruct((seq, Q_W), F32), jax.ShapeDtypeStruct((seq, 2 * KV_W), F32),
                   jax.ShapeDtypeStruct((n_ctx, 2 * KV_W), F32), jax.ShapeDtypeStruct((8, LANES), F32)]
        + [jax.ShapeDtypeStruct(v.shape, v.dtype) for v in scatter],
        scratch_shapes=_comm_scratch(ns) if ns else [],
        compiler_params=_cp("arbitrary"),
    )(q, *([kv] * nkv), kvc, sink, dya, ya, lse, *scatter)
    return out[:4], out[4:]


def _proj_bwd(dq, dkv, dpb, x, dr1, modx, w_in, cos, sin, tm, scatter=()):
    seq = x.shape[0]
    pw = IN_W - Q_W - 2 * KV_W
    ns = len(scatter)

    def body(dq_ref, dkv_ref, dpb_ref, x_ref, dr1_ref, mod_ref, w_ref, cos_ref, sin_ref, *rest):
        dqkv_ref, gx_ref, st_ref = rest[ns:ns + 3]
        comm = _AllToAll(rest[:ns], rest[ns + 3:2 * ns + 3], *rest[2 * ns + 3:]) if ns else None
        _host_start(pl.program_id(0), comm)

        @pl.when(pl.program_id(0) == 0)
        def _():
            st_ref[...] = jnp.zeros_like(st_ref)

        cos1, sin1 = cos_ref[...], sin_ref[...]
        cos2 = jnp.concatenate([cos1, cos1], axis=1)
        sin2 = jnp.concatenate([sin1, sin1], axis=1)
        for j in range(Q_W // 256):
            cs = slice(256 * j, 256 * (j + 1))
            dqkv_ref[:, cs] = _unrope(dq_ref[:, cs], cos2, sin2).astype(BF16)
        dqkv_ref[:, Q_W:Q_W + KV_W] = _unrope(dkv_ref[:, :KV_W], cos1, sin1).astype(BF16)
        dqkv_ref[:, Q_W + KV_W:] = dkv_ref[:, KV_W:].astype(BF16)
        o = Q_W + 2 * KV_W
        dh = _dot(dqkv_ref[...], w_ref[:o, :]) + _dot(dpb_ref[...], w_ref[o:, :])
        xhat, rstd = _ln(x_ref[...])
        st_ref[0:1, :] += _colsum(dh)
        st_ref[1:2, :] += _colsum(dh * xhat)
        gx_ref[...] = _ln_bwd(dh * (1.0 + mod_ref[1:2, :]), xhat, rstd) + ALPHA * dr1_ref[...]
        _host_finish(pl.program_id(0), seq // tm - 1, comm)

    out = pl.pallas_call(
        body, name="proj_bwd", grid=(seq // tm,),
        in_specs=[_rows(tm, Q_W), _rows(tm, 2 * KV_W), _rows(tm, pw), _rows(tm, D), _rows(tm, D), _acc((8, D)),
                  _resident((IN_W, D)), _rows(tm, LANES), _rows(tm, LANES)] + _comm_specs(ns),
        out_specs=[_rows(tm, Q_W + 2 * KV_W), _rows(tm, D), _acc((8, D))] + _comm_specs(ns),
        out_shape=[jax.ShapeDtypeStruct((seq, Q_W + 2 * KV_W), BF16), jax.ShapeDtypeStruct((seq, D), F32),
                   jax.ShapeDtypeStruct((8, D), F32)] + [jax.ShapeDtypeStruct(v.shape, v.dtype) for v in scatter],
        scratch_shapes=_comm_scratch(ns) if ns else [],
        compiler_params=_cp("arbitrary"),
    )(dq, dkv, dpb, x, dr1, modx, w_in, cos, sin, *scatter)
    return out[:3], out[3:]


def _ctx_bwd(dkvc, ctx, hc, w_kv):
    n_ctx = ctx.shape[0]

    def body(dkvc_ref, ctx_ref, hc_ref, w_ref, dw_ref, st_ref):
        d = dkvc_ref[...].astype(BF16)
        dw_ref[...] = _dot_tn(d, hc_ref[...])
        dhc = _dot(d, w_ref[...])
        xhat, _ = _ln(ctx_ref[...])
        st_ref[...] = jnp.zeros_like(st_ref)
        st_ref[0:1, :] = _colsum(dhc)
        st_ref[1:2, :] = _colsum(dhc * xhat)

    return pl.pallas_call(
        body, name="ctx_bwd", grid=(1,),
        in_specs=[_acc((n_ctx, 2 * KV_W)), _acc((n_ctx, D)), _acc((n_ctx, D)), _acc((2 * KV_W, D))],
        out_specs=[_acc((2 * KV_W, D)), _acc((8, D))],
        out_shape=[jax.ShapeDtypeStruct((2 * KV_W, D), F32), jax.ShapeDtypeStruct((8, D), F32)],
        compiler_params=_cp("arbitrary"),
    )(dkvc, ctx, hc, w_kv)


def _tn_matmul(a, b, tn, name, out_dtype, shard_major=False, init=None, tk=512, scatter=(), gather=()):
    t, ka = a.shape
    n = b.shape[1]
    tk = min(tk, t)
    nk = t // tk
    nj = n // tn
    has_init = init is not None
    assert not (scatter and gather)
    moved = list(scatter) + list(gather)
    pattern = _AllToAll if scatter else _Gather
    ns = len(moved)
    n_in = 3 if has_init else 2

    def body(*refs):
        a_ref, b_ref = refs[:2]
        i_ref = refs[2] if has_init else None
        rest = refs[n_in:]
        o_ref = rest[ns]
        acc_ref = rest[2 * ns + 1]
        comm = pattern(rest[:ns], rest[ns + 1:2 * ns + 1], *rest[2 * ns + 2:]) if ns else None
        k = pl.program_id(1)
        step = pl.program_id(0) * nk + k
        _host_start(step, comm)

        @pl.when(k == 0)
        def _():
            acc_ref[...] = i_ref[...] if has_init else jnp.zeros_like(acc_ref)

        acc_ref[...] += _dot_tn(a_ref[...], b_ref[...])

        @pl.when(k == nk - 1)
        def _():
            o_ref[...] = acc_ref[...].astype(out_dtype)

        _host_finish(step, nj * nk - 1, comm, forward_at=(3 * nj * nk) // 4 if gather else None)

    in_specs = [pl.BlockSpec((tk, ka), lambda j, k: (k, 0)), pl.BlockSpec((tk, tn), lambda j, k: (k, j))]
    args = [a, b]
    if has_init:
        in_specs.append(pl.BlockSpec((ka, tn), lambda j, k: (0, j)))
        args.append(init)
    if shard_major:
        out_spec = pl.BlockSpec((None, ka, tn), lambda j, k: (j, 0, 0))
        out_shape = jax.ShapeDtypeStruct((nj, ka, tn), out_dtype)
    else:
        out_spec = pl.BlockSpec((ka, tn), lambda j, k: (0, j))
        out_shape = jax.ShapeDtypeStruct((ka, n), out_dtype)
    out = pl.pallas_call(
        body, name=name, grid=(nj, nk), in_specs=in_specs + _comm_specs(ns), out_specs=[out_spec] + _comm_specs(ns),
        out_shape=[out_shape] + [jax.ShapeDtypeStruct(v.shape, v.dtype) for v in scatter] + _gathered_shapes(gather),
        scratch_shapes=[pltpu.VMEM((ka, tn), F32)] + (_comm_scratch(ns) if ns else []),
        compiler_params=_cp("arbitrary", "arbitrary"),
    )(*args, *moved)
    return (out[0], out[1:]) if ns else out[0]


ADA_TILE = 512


def _gather_and_modulate(c_rows, c_ctx, w_half, w_ada):
    cs = w_ada.shape[1]
    vmem = pl.BlockSpec(memory_space=pltpu.VMEM)

    def body(c_ref, cctx_ref, wh_ref, wada_ref, cg_ref, wg_ref, mod_ref, modg_ref, sc_ref,
             c_v, wada_v, mod_v, send_c, recv_c, send_w, recv_w, send_m, recv_m, local):
        gc = _Gather([c_ref], [cg_ref], send_c, recv_c)
        gw = _Gather([wh_ref], [wg_ref], send_w, recv_w)
        gm = _Gather([mod_ref], [modg_ref], send_m, recv_m)
        px, py, pc = _my_pos()
        mine = 4 * px + 2 * py + pc
        gc.start()
        gw.start()
        own_c = pltpu.make_async_copy(c_ref, cg_ref.at[mine], local.at[0])
        load_w = pltpu.make_async_copy(wada_ref, wada_v, local.at[1])
        own_c.start()
        load_w.start()
        gc.finish()
        own_c.wait()
        load_c = pltpu.make_async_copy(cg_ref, c_v, local.at[2])
        load_c.start()
        load_c.wait()
        cc = jnp.concatenate([c_v[i, 0:1, :] for i in range(N_DEV)] + [cctx_ref[...], jnp.zeros((7, D), F32)], axis=0)
        sc = cc * _sig(cc)
        sc_ref[...] = sc
        load_w.wait()
        for j in range(cs // ADA_TILE):
            cols = slice(j * ADA_TILE, (j + 1) * ADA_TILE)
            mod_v[:, cols] = _dot(sc.astype(BF16), wada_v[:, cols].astype(BF16))
        store_m = pltpu.make_async_copy(mod_v, mod_ref, local.at[3])
        store_m.start()
        store_m.wait()
        own_m = pltpu.make_async_copy(mod_ref, modg_ref.at[mine], local.at[4])
        own_m.start()
        gm.start()
        gm.finish()
        own_m.wait()
        gw.finish()

    any_ = pl.BlockSpec(memory_space=pl.ANY)
    return pl.pallas_call(
        body, name="gather_and_modulate",
        in_specs=[any_, vmem, any_, any_], out_specs=[any_, any_, any_, any_, vmem],
        scratch_shapes=[pltpu.VMEM((N_DEV, 8, D), F32), pltpu.VMEM((D, cs), F32), pltpu.VMEM((16, cs), F32)]
        + _comm_scratch(1) + _comm_scratch(1) + _comm_scratch(1) + [pltpu.SemaphoreType.DMA((5,))],
        out_shape=[jax.ShapeDtypeStruct((N_DEV, 8, D), F32), jax.ShapeDtypeStruct((N_DEV,) + w_half.shape, w_half.dtype),
                   jax.ShapeDtypeStruct((16, cs), F32), jax.ShapeDtypeStruct((N_DEV, 16, cs), F32),
                   jax.ShapeDtypeStruct((16, D), F32)],
        compiler_params=pltpu.CompilerParams(vmem_limit_bytes=VMEM_LIMIT),
    )(c_rows, c_ctx, w_half, w_ada)


def _ada_bwd(sc_all_t, dm_all, dmc, w_ada, m, v):
    cs = w_ada.shape[1]

    def body(st_ref, dm_ref, dmc_ref, w_ref, m_ref, v_ref, gw_ref, d_ref, nm_ref, nv_ref, part_ref):
        @pl.when(pl.program_id(0) == 0)
        def _():
            part_ref[...] = jnp.zeros_like(part_ref)

        g = _dot(st_ref[...].astype(BF16), dm_ref[...].astype(BF16))
        gw_ref[...] = g
        d_ref[...], nm_ref[...], nv_ref[...] = _adam_update(w_ref[...], g, m_ref[...], v_ref[...])
        part_ref[...] += _dot_nt(dmc_ref[...].astype(BF16), w_ref[...].astype(BF16))

    cols = pl.BlockSpec((D, ADA_TILE), lambda j: (0, j))
    shp = jax.ShapeDtypeStruct((D, cs), F32)
    return pl.pallas_call(
        body, name="ada_bwd_adamw", grid=(cs // ADA_TILE,),
        in_specs=[_acc((D, 16)), pl.BlockSpec((16, ADA_TILE), lambda j: (0, j)), pl.BlockSpec((8, ADA_TILE), lambda j: (0, j)),
                  cols, cols, cols],
        out_specs=[cols] * 4 + [_acc((8, D))],
        out_shape=[shp] * 4 + [jax.ShapeDtypeStruct((8, D), F32)],
        compiler_params=_cp("arbitrary"),
    )(sc_all_t, dm_all, dmc, w_ada, m, v)


def _sum8(x, name, tr=256):
    _, r, c = x.shape
    tr = min(tr, r)
    while r % tr:
        tr -= 16

    def body(x_ref, o_ref):
        acc = x_ref[0].astype(F32)
        for i in range(1, N_DEV):
            acc = acc + x_ref[i].astype(F32)
        o_ref[...] = acc

    return pl.pallas_call(
        body, name=name, grid=(r // tr,),
        in_specs=[pl.BlockSpec((N_DEV, tr, c), lambda i: (0, i, 0))],
        out_specs=pl.BlockSpec((tr, c), lambda i: (i, 0)),
        out_shape=jax.ShapeDtypeStruct((r, c), F32),
        compiler_params=_cp("arbitrary"),
    )(x)


def _sum_blocks(recv, src, me, name, tr=256):
    _, r, c = recv.shape
    tr = min(tr, r)
    while r % tr:
        tr -= 16

    def body(me_ref, recv_ref, own_ref, o_ref):
        acc = own_ref[...].astype(F32)
        for k in range(1, N_DEV):
            acc = acc + recv_ref[me_ref[0] ^ k].astype(F32)
        o_ref[...] = acc

    return pl.pallas_call(
        body, name=name,
        grid_spec=pltpu.PrefetchScalarGridSpec(
            num_scalar_prefetch=1, grid=(r // tr,),
            in_specs=[pl.BlockSpec((N_DEV, tr, c), lambda i, me_ref: (0, i, 0)),
                      pl.BlockSpec((None, tr, c), lambda i, me_ref: (me_ref[0], i, 0))],
            out_specs=pl.BlockSpec((tr, c), lambda i, me_ref: (i, 0))),
        out_shape=jax.ShapeDtypeStruct((r, c), F32),
        compiler_params=_cp("arbitrary"),
    )(me, recv, src)


def _sum8_many(xs, name):
    n = len(xs)

    def body(*refs):
        for x_ref, o_ref in zip(refs[:n], refs[n:]):
            acc = x_ref[0]
            for i in range(1, N_DEV):
                acc = acc + x_ref[i]
            o_ref[...] = acc

    vmem = pl.BlockSpec(memory_space=pltpu.VMEM)
    return pl.pallas_call(
        body, name=name, in_specs=[vmem] * n, out_specs=[vmem] * n,
        out_shape=[jax.ShapeDtypeStruct(v.shape[1:], v.dtype) for v in xs],
        compiler_params=pltpu.CompilerParams(vmem_limit_bytes=VMEM_LIMIT),
    )(*xs)


def _adam_update(w, g, m, v):
    nm = ADAM_B1 * m + (1.0 - ADAM_B1) * g
    nv = ADAM_B2 * v + (1.0 - ADAM_B2) * (g * g)
    m_hat = nm / (1.0 - ADAM_B1 ** ADAM_STEP)
    v_hat = nv / (1.0 - ADAM_B2 ** ADAM_STEP)
    return -ADAM_LR * (m_hat / (jnp.sqrt(v_hat) + ADAM_EPS) + ADAM_WD * w), nm, nv


ROW_LOSS, ROW_LN2_G, ROW_LN2_B, ROW_LN1_G, ROW_LN1_B = 0, 1, 2, 10, 11
ROWS_DMOD_X = (16, 17, 12, 9, 8, 3)
ROWS_DMOD_C = (24, 25)
SMALL = ("c_ctx", "b_ada", "attn_sink", "gmlp_ln_g", "gmlp_ln_b", "w_spatial", "b_spatial", "ln1_g", "ln1_b", "ln2_g", "ln2_b")


def _adamw_small(sums, dsc, w, m, v):
    n = len(SMALL)

    def body(*refs):
        st_ref, gm_ref, sk_ref, ws_ref, bs_ref, dsc_ref = refs[:6]
        w_refs = dict(zip(SMALL, refs[6:6 + n]))
        m_refs = dict(zip(SMALL, refs[6 + n:6 + 2 * n]))
        v_refs = dict(zip(SMALL, refs[6 + 2 * n:6 + 3 * n]))
        outs = refs[6 + 3 * n:]
        c = w_refs["c_ctx"][...]
        sg = _sig(c)
        dmod = [st_ref[r:r + 1, :] for r in ROWS_DMOD_X]
        dmod[0] = dmod[0] + st_ref[ROWS_DMOD_C[0]:ROWS_DMOD_C[0] + 1, :]
        dmod[1] = dmod[1] + st_ref[ROWS_DMOD_C[1]:ROWS_DMOD_C[1] + 1, :]
        grads = dict(
            c_ctx=dsc_ref[0:1, :] * (sg * (1.0 + c * (1.0 - sg))),
            b_ada=jnp.concatenate(dmod, axis=1),
            attn_sink=sk_ref[0:1, 0:N_KV * GROUP],
            gmlp_ln_g=gm_ref[0:1, :], gmlp_ln_b=gm_ref[1:2, :],
            w_spatial=ws_ref[...], b_spatial=bs_ref[...],
            ln1_g=st_ref[ROW_LN1_G:ROW_LN1_G + 1, :], ln1_b=st_ref[ROW_LN1_B:ROW_LN1_B + 1, :],
            ln2_g=st_ref[ROW_LN2_G:ROW_LN2_G + 1, :], ln2_b=st_ref[ROW_LN2_B:ROW_LN2_B + 1, :])
        for i, name in enumerate(SMALL):
            g = grads[name]
            d, nm, nv = _adam_update(w_refs[name][...], g, m_refs[name][...], v_refs[name][...])
            outs[i][...] = g
            outs[n + i][...] = d
            outs[2 * n + i][...] = nm
            outs[3 * n + i][...] = nv

    vmem = pl.BlockSpec(memory_space=pltpu.VMEM)
    args = list(sums) + [dsc] + [w[k] for k in SMALL] + [m[k] for k in SMALL] + [v[k] for k in SMALL]
    shapes = [jax.ShapeDtypeStruct(w[k].shape, F32) for k in SMALL]
    out = pl.pallas_call(
        body, name="adamw_small", in_specs=[vmem] * len(args), out_specs=[vmem] * (4 * n), out_shape=shapes * 4,
        compiler_params=pltpu.CompilerParams(vmem_limit_bytes=VMEM_LIMIT),
    )(*args)
    return [dict(zip(SMALL, out[i * n:(i + 1) * n])) for i in range(4)]


def _adamw_halves(w, mine, theirs, m, v, c_arr, name):
    r, c = w.shape
    tr = min(256, r // 2)
    while (r // 2) % tr:
        tr -= 8
    nt = (r // 2) // tr

    def body(c_ref, w_ref, mine_ref, theirs_ref, m_ref, v_ref, g_ref, d_ref, nm_ref, nv_ref):
        g = jnp.where(pl.program_id(0) == c_ref[0], mine_ref[...], theirs_ref[...])
        g_ref[...] = g
        d_ref[...], nm_ref[...], nv_ref[...] = _adam_update(w_ref[...], g, m_ref[...], v_ref[...])

    whole = pl.BlockSpec((tr, c), lambda hb, i, c_ref: (hb * nt + i, 0))
    mine_spec = pl.BlockSpec((tr, c), lambda hb, i, c_ref: (jnp.where(hb == c_ref[0], i, 0), 0))
    theirs_spec = pl.BlockSpec((tr, c), lambda hb, i, c_ref: (jnp.where(hb == c_ref[0], 0, i), 0))
    shp = jax.ShapeDtypeStruct((r, c), F32)
    return pl.pallas_call(
        body, name=name,
        grid_spec=pltpu.PrefetchScalarGridSpec(
            num_scalar_prefetch=1, grid=(2, nt), in_specs=[whole, mine_spec, theirs_spec, whole, whole],
            out_specs=[whole] * 4),
        out_shape=[shp] * 4,
        compiler_params=_cp("arbitrary", "arbitrary"),
    )(c_arr, w, mine, theirs, m, v)


def _my_pos():
    return lax.axis_index("x"), lax.axis_index("y"), lax.axis_index("c")


N_COPY = 7


class _Gather:
    def __init__(self, x_refs, out_refs, send_sems, recv_sems):
        self.x_refs, self.out_refs = x_refs, out_refs
        self.send_sems, self.recv_sems = send_sems, recv_sems
        x, y, c = _my_pos()
        self.c = c
        self.me, self.sibling = (x, y, c), (x, y, 1 - c)
        self.chips = [(1 - x, y), (x, 1 - y), (1 - x, 1 - y)]

    def _copy(self, a, k, block, to, from_input=False):
        px, py, pc = block
        rows = self.out_refs[a].at[4 * px + 2 * py + pc]
        return pltpu.make_async_remote_copy(
            src_ref=self.x_refs[a] if from_input else rows, dst_ref=rows,
            send_sem=self.send_sems.at[a * N_COPY + k], recv_sem=self.recv_sems.at[a * N_COPY + k],
            device_id=to, device_id_type=MESH)

    def start(self):
        n = len(self.x_refs)
        for a in range(n):
            self._copy(a, 0, self.me, self.sibling, from_input=True).start()
        for j, chip in enumerate(self.chips):
            for a in range(n):
                self._copy(a, 1 + j, self.me, (*chip, self.c), from_input=True).start()

    def forward(self):
        c = self.c
        for j, chip in enumerate(self.chips):
            for a in range(len(self.x_refs)):
                self._copy(a, 1 + j, (*chip, c), self.me).wait_recv()
                self._copy(a, 4 + j, (*chip, c), self.sibling).start()

    def finish(self):
        self.forward()
        self.drain()

    def drain(self):
        n = len(self.x_refs)
        c = self.c
        for a in range(n):
            self._copy(a, 0, self.sibling, self.me).wait_recv()
        for j, chip in enumerate(self.chips):
            for a in range(n):
                self._copy(a, 4 + j, (*chip, 1 - c), self.me).wait_recv()
        for a in range(n):
            self._copy(a, 0, self.me, self.sibling, from_input=True).wait_send()
            for j, chip in enumerate(self.chips):
                self._copy(a, 1 + j, self.me, (*chip, c), from_input=True).wait_send()
                self._copy(a, 4 + j, (*chip, c), self.sibling).wait_send()


def _comm_scratch(n):
    return [pltpu.SemaphoreType.DMA((n * N_COPY,)), pltpu.SemaphoreType.DMA((n * N_COPY,))]


def _comm_specs(n):
    return [pl.BlockSpec(memory_space=pl.ANY)] * n


def _gathered_shapes(xs):
    return [jax.ShapeDtypeStruct((N_DEV,) + v.shape, v.dtype) for v in xs]


def _with_own(gathered, xs, me):
    return [lax.dynamic_update_index_in_dim(g, v, me, 0) for g, v in zip(gathered, xs)]


class _AllToAll:
    def __init__(self, x_refs, out_refs, send_sems, recv_sems):
        self.x_refs, self.out_refs = x_refs, out_refs
        self.send_sems, self.recv_sems = send_sems, recv_sems
        self.pos = _my_pos()
        x, y, c = self.pos
        self.me = 4 * x + 2 * y + c

    def _peer(self, k):
        x, y, c = self.pos
        return (x ^ ((k >> 2) & 1), y ^ ((k >> 1) & 1), c ^ (k & 1))

    def _copy(self, a, k):
        p = self._peer(k)
        return pltpu.make_async_remote_copy(
            src_ref=self.x_refs[a].at[4 * p[0] + 2 * p[1] + p[2]], dst_ref=self.out_refs[a].at[self.me],
            send_sem=self.send_sems.at[a * N_COPY + k - 1], recv_sem=self.recv_sems.at[a * N_COPY + k - 1],
            device_id=p, device_id_type=MESH)

    def start(self):
        for k in range(1, N_DEV):
            for a in range(len(self.x_refs)):
                self._copy(a, k).start()

    def finish(self):
        for a in range(len(self.x_refs)):
            for k in range(1, N_DEV):
                self._copy(a, k).wait_recv()
            for k in range(1, N_DEV):
                self._copy(a, k).wait_send()


def _exchange(name, scatter=(), gather=(), sibling=()):
    ns, ng, nx = len(scatter), len(gather), len(sibling)
    n = ns + ng + nx

    def body(*refs):
        ins, outs, sems = refs[:n], refs[n:2 * n], list(refs[2 * n:])
        t = _AllToAll(ins[:ns], outs[:ns], sems.pop(0), sems.pop(0)) if ns else None
        g = _Gather(ins[ns:ns + ng], outs[ns:ns + ng], sems.pop(0), sems.pop(0)) if ng else None
        x, y, c = _my_pos()

        def push(a):
            return pltpu.make_async_remote_copy(
                src_ref=ins[ns + ng + a], dst_ref=outs[ns + ng + a], send_sem=sems[0].at[a], recv_sem=sems[1].at[a],
                device_id=(x, y, 1 - c), device_id_type=MESH)

        for comm in (g, t):
            if comm is not None:
                comm.start()
        for a in range(nx):
            push(a).start()
        for comm in (g, t):
            if comm is not None:
                comm.finish()
        for a in range(nx):
            push(a).wait_recv()
            push(a).wait_send()

    scratch = (_comm_scratch(ns) if ns else []) + (_comm_scratch(ng) if ng else [])
    scratch += [pltpu.SemaphoreType.DMA((nx,)), pltpu.SemaphoreType.DMA((nx,))] if nx else []
    out = pl.pallas_call(
        body, name=name,
        out_shape=[jax.ShapeDtypeStruct(v.shape, v.dtype) for v in scatter] + _gathered_shapes(gather)
        + [jax.ShapeDtypeStruct(v.shape, v.dtype) for v in sibling],
        in_specs=_comm_specs(n), out_specs=_comm_specs(n), scratch_shapes=scratch,
    )(*scatter, *gather, *sibling)
    return out[:ns], out[ns:ns + ng], out[ns + ng:]


def _row_tile(seq, want):
    return min(want, seq)


def _local_step(x, ctx, tgt, mod_x, mod_c, wb, sink, gmlp_g, gmlp_b, w_s, b_s, ln1_g, ln1_b, ln2_g, ln2_b,
                later=None, me=None):
    seq = x.shape[0]
    on_mesh = me is not None
    modx1 = jnp.concatenate([mod_x[0:2], jnp.zeros((6, D), F32)], axis=0)
    modc = jnp.concatenate([mod_c[0:2], jnp.zeros((6, D), F32)], axis=0)
    vec = jnp.concatenate([mod_x[2:3], ln1_g, ln1_b, mod_x[3:6], ln2_g, ln2_b], axis=0)
    gp = jnp.concatenate([gmlp_g, gmlp_b, jnp.zeros((6, G_W), F32)], axis=0)
    ws_stack = w_s.reshape(N_GRP * BLK, BLK).astype(BF16)
    ws_stack_t = jnp.transpose(w_s, (2, 0, 1)).reshape(BLK, N_GRP * BLK).astype(BF16)
    bias_full = jnp.repeat(b_s.T, GRP_D, axis=1)
    cos, sin = _rope_tables(seq)
    w_in = wb["w_in"]
    w_kv = w_in[Q_W:Q_W + 2 * KV_W, :]
    tm_big = _row_tile(seq, 512)
    tm_ffn = _row_tile(seq, 256)

    hc, kvc, vac = _ctx_fwd(ctx, modc, w_kv)
    behind_proj = ("w_a", "w_b", "w_o") if on_mesh else ()
    behind_attn = ("w_fi",) if on_mesh else ()
    behind_mix = ("w_fo",) if on_mesh else ()
    wb = dict(wb)

    def whole(names, gathered):
        for n, g in zip(names, _with_own(list(gathered), [later[n] for n in names], me)):
            wb[n] = g.reshape(-1, g.shape[2]) if n in ROW_SHARDED else g.reshape(N_SHARD, 2 * g.shape[1], g.shape[2])

    (h, q, kv, va, uv, gab), got = _proj_fwd(x, modx1, w_in, cos, sin, _row_tile(seq, 1024), gather=[later[n] for n in behind_proj])
    whole(behind_proj, got)
    if on_mesh:
        for n in ("w_a", "w_b"):
            wb[n] = wb[n].transpose(1, 0, 2).reshape(wb[n].shape[1], D)
    (ya, lse), got = _attn_fwd(q, kv, va, kvc, vac, sink, gather=[later[n] for n in behind_attn])
    whole(behind_attn, got)
    (a, b, mix, merged, yb), got = _mix_fwd(uv, gab, ya, gp, ws_stack, bias_full, wb["w_a"], wb["w_b"], wb["w_o"], tm_big,
                                            gather=[later[n] for n in behind_mix])
    whole(behind_mix, got)
    act, h2, dff, df, dr1, st_ffn = _ffn(x, mix, tgt, vec, wb["w_fi"], wb["w_fo"], tm_ffn)
    blocks, recv = {}, {}
    blocks["w_fo"] = _eighths(_tn_matmul(act, df, 512, "tn_w_ffn_out", BF16, tk=2048))
    if on_mesh:
        g_w_fi, (recv["w_fo"],) = _tn_matmul(h2, dff, FH_SHARD, "tn_w_ffn_in", BF16, shard_major=True, tk=2048,
                                            scatter=[blocks["w_fo"]])
    else:
        g_w_fi = _tn_matmul(h2, dff, FH_SHARD, "tn_w_ffn_in", BF16, shard_major=True, tk=2048)
    blocks["w_fi"] = _eighths(g_w_fi)
    (dya, dpb, dws, dbs_full, st4, g_w_o, g_w_a, g_w_b), got = _mix_bwd(
        dr1, a, b, gab, uv, merged, ya, yb, vec, gp, ws_stack, ws_stack_t, bias_full, wb["w_a"], wb["w_b"], wb["w_o"],
        tm_big, scatter=[blocks["w_fi"]] if on_mesh else ())
    recv.update(zip(("w_fi",), got))
    shard_major = [g.reshape(g.shape[0], N_SHARD, D // N_SHARD).transpose(1, 0, 2) for g in (g_w_a, g_w_b)]
    blocks.update(w_o=_eighths(g_w_o), w_a=_eighths(shard_major[0]), w_b=_eighths(shard_major[1]))
    mixer = ("w_o", "w_a", "w_b") if on_mesh else ()
    (dq, dkv, dkvc, dsink), got = _attn_bwd(q, kv, kvc, sink, dya, ya, lse, scatter=[blocks[n] for n in mixer])
    recv.update(zip(mixer, got))
    g_wkv_ctx, st0 = _ctx_bwd(dkvc, ctx, hc, w_kv)
    (dqkv, grad_x, st1), _ = _proj_bwd(dq, dkv, dpb, x, dr1, modx1, w_in, cos, sin, tm_big)
    dbs = jnp.sum(dbs_full.reshape(BLK, N_GRP, GRP_D), axis=2).T
    early = [jnp.concatenate([st_ffn, st0], axis=0), st4, dsink, dws, dbs]
    init = jnp.pad(g_wkv_ctx, ((Q_W, 0), (0, 0)))
    g_qkv = _tn_matmul(dqkv, h, D, "tn_w_in_qkv", BF16, init=init, tk=1024)
    if on_mesh:
        g_rest, early_gathered = _tn_matmul(dpb, h, D, "tn_w_in_rest", BF16, tk=1024, gather=early)
    else:
        g_rest, early_gathered = _tn_matmul(dpb, h, D, "tn_w_in_rest", BF16, tk=1024), None
    blocks["w_in"] = _eighths(jnp.concatenate([g_qkv, g_rest], axis=0))
    return grad_x, dict(early=early, early_gathered=early_gathered, late=st1), blocks, recv


BIG = ("w_in", "w_a", "w_b", "w_o", "w_fi", "w_fo")
ROW_SHARDED = ("w_o", "w_fo")


def _half_of_shard(shard, c):
    r = shard.shape[0]
    return lax.dynamic_slice_in_dim(shard, c * (r // 2), r // 2, axis=0)


def _eighths(v):
    rows = v.shape[-2] * (v.shape[0] if v.ndim == 3 else 1)
    return v.reshape(N_DEV, rows // N_DEV, v.shape[-1])


def kernel(x, c, ctx, c_ctx, w_ada, b_ada, w_in, attn_sink, gmlp_ln_g, gmlp_ln_b, w_spatial, b_spatial, w_branch_a, w_branch_b, w_out, ln1_g, ln1_b, w_ffn_in, w_ffn_out, ln2_g, ln2_b, loss_target, m_c_ctx, m_w_ada, m_b_ada, m_w_in, m_attn_sink, m_gmlp_ln_g, m_gmlp_ln_b, m_w_spatial, m_b_spatial, m_w_branch_a, m_w_branch_b, m_w_out, m_ln1_g, m_ln1_b, m_w_ffn_in, m_w_ffn_out, m_ln2_g, m_ln2_b, v_c_ctx, v_w_ada, v_b_ada, v_w_in, v_attn_sink, v_gmlp_ln_g, v_gmlp_ln_b, v_w_spatial, v_b_spatial, v_w_branch_a, v_w_branch_b, v_w_out, v_ln1_g, v_ln1_b, v_w_ffn_in, v_w_ffn_out, v_ln2_g, v_ln2_b):
    mx, my, mc = _my_pos()
    me = 4 * mx + 2 * my + mc
    chip = 2 * mx + my
    shards = dict(w_in=w_in[0].T, w_a=w_branch_a[0], w_b=w_branch_b[0], w_o=w_out[0], w_fi=w_ffn_in[0], w_fo=w_ffn_out[0])

    halves = {n: _half_of_shard(shards[n], mc).astype(BF16) for n in BIG}
    c_rows = jnp.concatenate([c, jnp.zeros((7, D), F32)], axis=0)
    _, g_in, _, mod_g, sc_all = _gather_and_modulate(c_rows, c_ctx[None, :], halves["w_in"], w_ada[0])
    wb = dict(w_in=_with_own([g_in], [halves["w_in"]], me)[0].reshape(IN_W, D))
    mod_all = jnp.concatenate([mod_g[2 * s] for s in range(4)], axis=1) + b_ada
    mod_x = lax.dynamic_slice_in_dim(mod_all, me, 1, axis=0).reshape(6, D)
    mod_c = mod_all[8].reshape(6, D)[0:2]

    grad_x, small, blocks, recv = _local_step(
        x[0], ctx[0], loss_target[0], mod_x, mod_c, wb, attn_sink, gmlp_ln_g, gmlp_ln_b, w_spatial[0], b_spatial[0],
        ln1_g, ln1_b, ln2_g, ln2_b, later=halves, me=me)

    me_arr = jnp.reshape(me, (1,)).astype(jnp.int32)
    summed = {n: _sum_blocks(recv[n], blocks[n], me_arr, "sum_grads_" + n) for n in BIG[1:]}
    (recv["w_in"],), late, theirs = _exchange("scatter_w_in_gather_small_exchange_grads", scatter=[blocks["w_in"]],
                                              gather=[small["late"]], sibling=[summed[n] for n in BIG[1:]])
    theirs = dict(zip(BIG[1:], theirs))
    summed["w_in"] = _sum_blocks(recv["w_in"], blocks["w_in"], me_arr, "sum_grads_w_in")
    late = _with_own(late, [small["late"]], me)[0]
    gathered = _with_own(small["early_gathered"], small["early"], me)
    gathered[0] = jnp.concatenate([gathered[0][:, :16], late, gathered[0][:, 16:]], axis=1)

    sums = _sum8_many(gathered, "sum_small")
    stats = sums[0]
    loss = 0.5 * jnp.sum(stats[ROW_LOSS]) / D
    dmod_x_all = jnp.concatenate([gathered[0][:, r_, :] for r_ in ROWS_DMOD_X], axis=1)
    dmod_c_full = jnp.concatenate([stats[r_] for r_ in ROWS_DMOD_C] + [jnp.zeros((4 * D,), F32)])
    dm_rows = jnp.concatenate([dmod_x_all, dmod_c_full[None, :], jnp.zeros((7, 6 * D), F32)], axis=0)
    cs = w_ada.shape[2]
    dm_shard = lax.dynamic_slice_in_dim(dm_rows, chip * cs, cs, axis=1)
    dmc_shard = jnp.concatenate([dm_shard[8:9], jnp.zeros((7, cs), F32)], axis=0)
    ada = _ada_bwd(sc_all.T, dm_shard, dmc_shard, w_ada[0], m_w_ada[0], v_w_ada[0])
    part = ada[4]
    part = part * (mc == 0).astype(F32)
    _, part_all, (theirs["w_in"],) = _exchange("exchange_w_in_gather_c_ctx", gather=[part], sibling=[summed["w_in"]])
    dsc = _sum8(_with_own(part_all, [part], me)[0], "sum_c_ctx")

    weights = dict(c_ctx=c_ctx, w_ada=w_ada, b_ada=b_ada, w_in=w_in, attn_sink=attn_sink, gmlp_ln_g=gmlp_ln_g,
                   gmlp_ln_b=gmlp_ln_b, w_spatial=w_spatial, b_spatial=b_spatial, w_branch_a=w_branch_a,
                   w_branch_b=w_branch_b, w_out=w_out, ln1_g=ln1_g, ln1_b=ln1_b, w_ffn_in=w_ffn_in, w_ffn_out=w_ffn_out,
                   ln2_g=ln2_g, ln2_b=ln2_b)
    ms = dict(c_ctx=m_c_ctx, w_ada=m_w_ada, b_ada=m_b_ada, w_in=m_w_in, attn_sink=m_attn_sink, gmlp_ln_g=m_gmlp_ln_g,
              gmlp_ln_b=m_gmlp_ln_b, w_spatial=m_w_spatial, b_spatial=m_b_spatial, w_branch_a=m_w_branch_a,
              w_branch_b=m_w_branch_b, w_out=m_w_out, ln1_g=m_ln1_g, ln1_b=m_ln1_b, w_ffn_in=m_w_ffn_in,
              w_ffn_out=m_w_ffn_out, ln2_g=m_ln2_g, ln2_b=m_ln2_b)
    vs = dict(c_ctx=v_c_ctx, w_ada=v_w_ada, b_ada=v_b_ada, w_in=v_w_in, attn_sink=v_attn_sink, gmlp_ln_g=v_gmlp_ln_g,
              gmlp_ln_b=v_gmlp_ln_b, w_spatial=v_w_spatial, b_spatial=v_b_spatial, w_branch_a=v_w_branch_a,
              w_branch_b=v_w_branch_b, w_out=v_w_out, ln1_g=v_ln1_g, ln1_b=v_ln1_b, w_ffn_in=v_w_ffn_in,
              w_ffn_out=v_w_ffn_out, ln2_g=v_ln2_g, ln2_b=v_ln2_b)
    order = list(weights)
    grads, delta, new_m, new_v = [dict(w_ada=t[None]) for t in ada[:4]]
    c_arr = jnp.reshape(mc, (1,)).astype(jnp.int32)
    names = dict(w_in="w_in", w_a="w_branch_a", w_b="w_branch_b", w_o="w_out", w_fi="w_ffn_in", w_fo="w_ffn_out")
    for k, n in names.items():
        flip = (lambda t: t.T) if k == "w_in" else (lambda t: t)
        outs = _adamw_halves(flip(weights[n][0]), summed[k], theirs[k], flip(ms[n][0]), flip(vs[n][0]), c_arr, "adamw_" + n)
        grads[n], delta[n], new_m[n], new_v[n] = [flip(t)[None] for t in outs]

    def view(a):
        return a.reshape(-1, a.shape[-1]) if a.ndim != 1 else a.reshape(1, -1)

    small = _adamw_small(sums, dsc, *[{n: view(d[n]) for n in SMALL} for d in (weights, ms, vs)])
    for out, src in zip((grads, delta, new_m, new_v), small):
        for n in SMALL:
            out[n] = src[n].reshape(weights[n].shape)

    return (loss, grad_x[None], *[grads[n] for n in order], *[delta[n] for n in order],
            *[new_m[n] for n in order], *[new_v[n] for n in order])
```

```python
import math

import jax
import jax.numpy as jnp
from jax import lax
from jax.experimental import pallas as pl
from jax.experimental.pallas import tpu as pltpu

F32 = jnp.float32
BF16 = jnp.bfloat16

D = 1024
HEAD = 64
N_KV = 2
GROUP = 4
Q_W = 512
KV_W = 128
G_W = 512
BLK = 128
N_GRP = 8
GRP_D = 64
FH = 2816
IN_W = 3840
GRID_W = 64
ROPE_BASE = 10000.0
LN_EPS = 1e-5
NEG = -1e30
ALPHA = (2 * 1) ** 0.25
SCALE = HEAD ** -0.5
GELU_K = math.sqrt(2.0 / math.pi)
GELU_A = 0.044715
ADAM_LR = 0.001
ADAM_B1 = 0.9
ADAM_B2 = 0.999
ADAM_EPS = 1e-08
ADAM_WD = 0.01
ADAM_STEP = 10
N_DEV = 8
N_SHARD = 4
FH_SHARD = FH // 2
LANES = 128
VMEM_LIMIT = 56 * 1024 * 1024
MESH = pl.DeviceIdType.MESH


def _cp(*sem):
    return pltpu.CompilerParams(dimension_semantics=sem, vmem_limit_bytes=VMEM_LIMIT)


def _resident(shape):
    return pl.BlockSpec(shape, lambda *_: (0,) * len(shape), pipeline_mode=pl.Buffered(1))


def _rows(tm, width):
    return pl.BlockSpec((tm, width), lambda i: (i, 0))


def _acc(shape):
    return pl.BlockSpec(shape, lambda *_: (0,) * len(shape))


def _dot(a, b):
    return jnp.dot(a, b, preferred_element_type=F32)


def _dot_nt(a, b):
    return lax.dot_general(a, b, (((1,), (1,)), ((), ())), preferred_element_type=F32)


def _dot_tn(a, b):
    return lax.dot_general(a, b, (((0,), (0,)), ((), ())), preferred_element_type=F32)


def _ln(x):
    mu = jnp.mean(x, axis=-1, keepdims=True)
    xc = x - mu
    var = jnp.mean(xc * xc, axis=-1, keepdims=True)
    rstd = lax.rsqrt(var + LN_EPS)
    return xc * rstd, rstd


def _ln_bwd(dxhat, xhat, rstd):
    return (dxhat - jnp.mean(dxhat, axis=-1, keepdims=True)
            - xhat * jnp.mean(dxhat * xhat, axis=-1, keepdims=True)) * rstd


def _sig(x):
    return 0.5 + 0.5 * jnp.tanh(0.5 * x)


def _gelu(x):
    t = jnp.tanh(x * (GELU_K + (GELU_K * GELU_A) * (x * x)))
    hx = 0.5 * x
    return hx + hx * t, t


def _gelu_grad(x, t):
    return 0.5 + 0.5 * t + (0.5 * x) * (1.0 - t * t) * (GELU_K + (3.0 * GELU_K * GELU_A) * (x * x))


def _colsum(v):
    return jnp.sum(v, axis=0, keepdims=True)


def _partner(x):
    w = x.shape[1]
    lane = lax.broadcasted_iota(jnp.int32, x.shape, 1)
    return jnp.where((lane & 31) < 16, pltpu.roll(x, w - 16, 1), pltpu.roll(x, 16, 1))


def _rope(x, cos, sin):
    return x * cos + _partner(x) * sin


def _unrope(g, cos, sin):
    return g * cos + _partner(g * sin)


def _rope_tables(seq):
    inv = ROPE_BASE ** (-jnp.arange(HEAD // 4, dtype=F32) / (HEAD // 4))
    pos = jnp.arange(seq, dtype=jnp.int32)
    ar = (pos // GRID_W).astype(F32)[:, None] * inv
    ac = (pos % GRID_W).astype(F32)[:, None] * inv
    cos = jnp.concatenate([jnp.cos(ar), jnp.cos(ar), jnp.cos(ac), jnp.cos(ac)], axis=-1)
    sin = jnp.concatenate([-jnp.sin(ar), jnp.sin(ar), -jnp.sin(ac), jnp.sin(ac)], axis=-1)
    return jnp.tile(cos, (1, LANES // HEAD)), jnp.tile(sin, (1, LANES // HEAD))


def _ctx_fwd(ctx, modc, w_kv):
    n_ctx = ctx.shape[0]

    def body(ctx_ref, mod_ref, w_ref, hc_ref, kvc_ref, vac_ref):
        xhat, _ = _ln(ctx_ref[...])
        hc = (xhat * (1.0 + mod_ref[1:2, :]) + mod_ref[0:1, :]).astype(BF16)
        hc_ref[...] = hc
        kvc = _dot_nt(hc, w_ref[...]).astype(BF16)
        kvc_ref[...] = kvc
        vac_ref[...] = _with_ones(kvc[:, KV_W:])

    return pl.pallas_call(
        body, name="ctx_fwd", grid=(1,),
        in_specs=[_acc((n_ctx, D)), _acc((8, D)), _acc((2 * KV_W, D))],
        out_specs=[_acc((n_ctx, D)), _acc((n_ctx, 2 * KV_W)), _acc((n_ctx, 2 * LANES))],
        out_shape=[jax.ShapeDtypeStruct((n_ctx, D), BF16), jax.ShapeDtypeStruct((n_ctx, 2 * KV_W), BF16),
                   jax.ShapeDtypeStruct((n_ctx, 2 * LANES), BF16)],
        compiler_params=_cp("arbitrary"),
    )(ctx, modc, w_kv)


def _host_start(step, comm):
    if comm is not None:
        @pl.when(step == 0)
        def _():
            comm.start()


def _host_finish(step, last, comm, forward_at=None):
    if comm is None:
        return
    if forward_at is None or forward_at >= last:
        @pl.when(step == last)
        def _():
            comm.finish()
    else:
        @pl.when(step == forward_at)
        def _():
            comm.forward()

        @pl.when(step == last)
        def _():
            comm.drain()


def _proj_fwd(x, modx, w_in, cos, sin, tm, gather=()):
    seq = x.shape[0]
    ng = len(gather)

    def body(x_ref, mod_ref, w_ref, cos_ref, sin_ref, *rest):
        h_ref, q_ref, kv_ref, va_ref, uv_ref, gab_ref = rest[ng:ng + 6]
        comm = _Gather(rest[:ng], rest[ng + 6:2 * ng + 6], *rest[2 * ng + 6:]) if ng else None
        _host_start(pl.program_id(0), comm)
        xhat, _ = _ln(x_ref[...])
        h = (xhat * (1.0 + mod_ref[1:2, :]) + mod_ref[0:1, :]).astype(BF16)
        h_ref[...] = h
        cos1, sin1 = cos_ref[...], sin_ref[...]
        cos2 = jnp.concatenate([cos1, cos1], axis=1)
        sin2 = jnp.concatenate([sin1, sin1], axis=1)
        for j in range(Q_W // 256):
            t = _dot_nt(h, w_ref[256 * j:256 * (j + 1), :])
            q_ref[:, 256 * j:256 * (j + 1)] = (_rope(t, cos2, sin2) * SCALE).astype(BF16)
        t = _dot_nt(h, w_ref[Q_W:Q_W + 2 * KV_W, :])
        kv_ref[:, :KV_W] = _rope(t[:, :KV_W], cos1, sin1).astype(BF16)
        v = t[:, KV_W:].astype(BF16)
        kv_ref[:, KV_W:] = v
        va_ref[...] = _with_ones(v)
        o = Q_W + 2 * KV_W
        for j in range(2):
            uv_ref[:, G_W * j:G_W * (j + 1)] = _dot_nt(h, w_ref[o + G_W * j:o + G_W * (j + 1), :]).astype(BF16)
        o += 2 * G_W
        for j in range(4):
            gab_ref[:, 512 * j:512 * (j + 1)] = _dot_nt(h, w_ref[o + 512 * j:o + 512 * (j + 1), :]).astype(BF16)
        _host_finish(pl.program_id(0), seq // tm - 1, comm, forward_at=(seq // tm) // 2)

    out = pl.pallas_call(
        body, name="proj_fwd", grid=(seq // tm,),
        in_specs=[_rows(tm, D), _acc((8, D)), _resident((IN_W, D)), _rows(tm, LANES), _rows(tm, LANES)] + _comm_specs(ng),
        out_specs=[_rows(tm, D), _rows(tm, Q_W), _rows(tm, 2 * KV_W), _rows(tm, 2 * LANES), _rows(tm, 2 * G_W),
                   _rows(tm, 2 * D)] + _comm_specs(ng),
        out_shape=[jax.ShapeDtypeStruct((seq, D), BF16), jax.ShapeDtypeStruct((seq, Q_W), BF16),
                   jax.ShapeDtypeStruct((seq, 2 * KV_W), BF16), jax.ShapeDtypeStruct((seq, 2 * LANES), BF16),
                   jax.ShapeDtypeStruct((seq, 2 * G_W), BF16), jax.ShapeDtypeStruct((seq, 2 * D), BF16)] + _gathered_shapes(gather),
        scratch_shapes=_comm_scratch(ng) if ng else [],
        compiler_params=_cp("arbitrary"),
    )(x, modx, w_in, cos, sin, *gather)
    return out[:6], out[6:]


def _stack_heads(x, hk):
    return jnp.concatenate([x[:, (hk * GROUP + g) * HEAD:(hk * GROUP + g + 1) * HEAD] for g in range(GROUP)], axis=0)


def _band_masks(n, nb):
    rows = GROUP * BLK
    qi = lax.broadcasted_iota(jnp.int32, (rows, BLK), 0) & (BLK - 1)
    kj = lax.broadcasted_iota(jnp.int32, (rows, BLK), 1)
    return (kj >= qi) & (n > 0), (kj <= qi) & (n < nb - 1)


def _attn_scores(q, k_refs, hk, masks):
    q4 = _stack_heads(q, hk)
    ks = [r[:, hk * HEAD:(hk + 1) * HEAD] for r in k_refs]
    s = [_dot_nt(q4, k) for k in ks]
    s[1] = jnp.where(masks[0], s[1], NEG)
    s[3] = jnp.where(masks[1], s[3], NEG)
    return q4, ks, s


def _sink_rows(sink_ref, hk):
    rows = GROUP * BLK
    rg = lax.broadcasted_iota(jnp.int32, (rows, 1), 0) >> 7
    sink_v = jnp.full((rows, 1), sink_ref[0, hk * GROUP], F32)
    for g in range(1, GROUP):
        sink_v = jnp.where(rg == g, sink_ref[0, hk * GROUP + g], sink_v)
    return sink_v


def _with_ones(v):
    ones = jnp.ones((v.shape[0], HEAD), v.dtype)
    return jnp.concatenate([v[:, :HEAD], ones, v[:, HEAD:], ones], axis=1)


def _kv_specs(nb, qb):
    def spec(d):
        return pl.BlockSpec((BLK, 2 * KV_W), lambda n: (jnp.clip(qb * n + d, 0, nb - 1), 0))
    return [spec(d) for d in range(-1, qb + 1)]


def _attn_fwd(q, kv, va, kvc, vac, sink, gather=()):
    seq = q.shape[0]
    nb = seq // BLK
    n_ctx = kvc.shape[0]
    ng = len(gather)
    Q_BLOCKS = 1
    nkv = Q_BLOCKS + 2
    steps = nb // Q_BLOCKS

    def body(q_ref, *rest):
        kv_refs, va_refs = rest[:nkv], rest[nkv:2 * nkv]
        kvc_ref, vac_ref, sink_ref = rest[2 * nkv:2 * nkv + 3]
        rest = rest[2 * nkv + 3:]
        o_ref, lse_ref = rest[ng:ng + 2]
        comm = _Gather(rest[:ng], rest[ng + 2:2 * ng + 2], *rest[2 * ng + 2:]) if ng else None
        n = pl.program_id(0)
        _host_start(n, comm)
        lane = lax.broadcasted_iota(jnp.int32, (BLK, LANES), 1)
        for sub in range(Q_BLOCKS):
            rs = slice(sub * BLK, (sub + 1) * BLK)
            q = q_ref[rs, :]
            outs = []
            lse_all = jnp.zeros((BLK, LANES), F32)
            masks = _band_masks(Q_BLOCKS * n + sub, nb)
            for hk in range(N_KV):
                _, _, s = _attn_scores(q, (kvc_ref,) + kv_refs[sub:sub + 3], hk, masks)
                sink_v = _sink_rows(sink_ref, hk)
                tile_max = s[1]
                for t in [s[0][:, i * LANES:(i + 1) * LANES] for i in range(n_ctx // LANES)] + s[2:]:
                    tile_max = jnp.maximum(tile_max, t)
                m = jnp.maximum(sink_v, jnp.max(tile_max, axis=-1, keepdims=True))
                o = jnp.zeros((GROUP * BLK, LANES), F32)
                for t, va_ref in zip(s, (vac_ref,) + va_refs[sub:sub + 3]):
                    o = o + _dot(jnp.exp((t - m).astype(BF16)), va_ref[:, hk * LANES:(hk + 1) * LANES])
                denom = o[:, HEAD:HEAD + 1] + jnp.exp(sink_v - m)
                o4 = o[:, :HEAD] * (1.0 / denom)
                lse4 = m + jnp.log(denom)
                for g in range(GROUP):
                    outs.append(o4[g * BLK:(g + 1) * BLK, :])
                    lse_all = jnp.where(lane == hk * GROUP + g, lse4[g * BLK:(g + 1) * BLK, :], lse_all)
            o_ref[rs, :] = jnp.concatenate(outs, axis=1).astype(BF16)
            lse_ref[rs, :] = lse_all
        _host_finish(n, steps - 1, comm, forward_at=(3 * steps) // 4)

    tq = Q_BLOCKS * BLK
    out = pl.pallas_call(
        body, name="attn_fwd", grid=(steps,),
        in_specs=[_rows(tq, Q_W)] + _kv_specs(nb, Q_BLOCKS) + _kv_specs(nb, Q_BLOCKS)
        + [_acc((n_ctx, 2 * KV_W)), _acc((n_ctx, 2 * LANES)), pl.BlockSpec(memory_space=pltpu.SMEM)] + _comm_specs(ng),
        out_specs=[_rows(tq, Q_W), _rows(tq, LANES)] + _comm_specs(ng),
        out_shape=[jax.ShapeDtypeStruct((seq, Q_W), BF16), jax.ShapeDtypeStruct((seq, LANES), F32)] + _gathered_shapes(gather),
        scratch_shapes=_comm_scratch(ng) if ng else [],
        compiler_params=_cp("arbitrary"),
    )(q, *([kv] * nkv), *([va] * nkv), kvc, vac, sink, *gather)
    return out[:2], out[2:]


def _gmlp_chunk(u, vb, gp_ref, ws_ref, bias_ref):
    gu, tu = _gelu(u)
    gv, tv = _gelu(vb)
    vhat, rstd = _ln(gv)
    vn = (vhat * gp_ref[0:1, :] + gp_ref[1:2, :]).astype(BF16)
    s = bias_ref[...] + jnp.concatenate(
        [_dot(ws_ref[g * BLK:(g + 1) * BLK, :], vn[:, g * GRP_D:(g + 1) * GRP_D]) for g in range(N_GRP)], axis=1)
    return gu, tu, tv, vhat, rstd, vn, s


def _mix_fwd(uv, gab, ya, gp, ws_stack, bias_full, w_a, w_b, w_o, tm, gather=()):
    seq = uv.shape[0]
    ng = len(gather)
    steps = seq // tm

    def body(uv_ref, gab_ref, ya_ref, gp_ref, ws_ref, bias_ref, wa_ref, wb_ref, wo_ref, *rest):
        a_ref, b_ref, mix_ref, merged_ref, yb_ref = rest[ng:ng + 5]
        comm = _Gather(rest[:ng], rest[ng + 5:2 * ng + 5], *rest[2 * ng + 5:]) if ng else None
        _host_start(pl.program_id(0), comm)
        for c in range(tm // BLK):
            rs = slice(c * BLK, (c + 1) * BLK)
            gu, _, _, _, _, _, s = _gmlp_chunk(uv_ref[rs, :G_W].astype(F32), uv_ref[rs, G_W:].astype(F32), gp_ref, ws_ref, bias_ref)
            yb_ref[rs, :] = (gu * s).astype(BF16)
        a = _dot(ya_ref[...], wa_ref[...])
        b = _dot(yb_ref[...], wb_ref[...])
        a_ref[...] = a.astype(BF16)
        b_ref[...] = b.astype(BF16)
        merged = (_sig(gab_ref[:, :D].astype(F32)) * a + _sig(gab_ref[:, D:].astype(F32)) * b).astype(BF16)
        merged_ref[...] = merged
        mix_ref[...] = _dot(merged, wo_ref[...])
        _host_finish(pl.program_id(0), steps - 1, comm, forward_at=(3 * steps) // 4)

    out = pl.pallas_call(
        body, name="mix_fwd", grid=(steps,),
        in_specs=[_rows(tm, 2 * G_W), _rows(tm, 2 * D), _rows(tm, Q_W), _acc((8, G_W)),
                  _resident((N_GRP * BLK, BLK)), _acc((BLK, G_W)),
                  _resident((Q_W, D)), _resident((G_W, D)), _resident((D, D))] + _comm_specs(ng),
        out_specs=[_rows(tm, D), _rows(tm, D), _rows(tm, D), _rows(tm, D), _rows(tm, G_W)] + _comm_specs(ng),
        out_shape=[jax.ShapeDtypeStruct((seq, D), BF16), jax.ShapeDtypeStruct((seq, D), BF16),
                   jax.ShapeDtypeStruct((seq, D), F32), jax.ShapeDtypeStruct((seq, D), BF16),
                   jax.ShapeDtypeStruct((seq, G_W), BF16)] + _gathered_shapes(gather),
        scratch_shapes=_comm_scratch(ng) if ng else [],
        compiler_params=_cp("arbitrary"),
    )(uv, gab, ya, gp, ws_stack, bias_full, w_a, w_b, w_o, *gather)
    return out[:5], out[5:]


def _mid_recompute(x_ref, mix_ref, vec_ref):
    r1 = ALPHA * x_ref[...] + vec_ref[0:1, :] * mix_ref[...]
    xh1, rstd1 = _ln(r1)
    xmid = xh1 * vec_ref[1:2, :] + vec_ref[2:3, :]
    xh2, rstd2 = _ln(xmid)
    return xh1, rstd1, xmid, xh2, rstd2


def _ffn(x, mix, tgt, vec, w_fi, w_fo, tm):
    seq = x.shape[0]

    def body(x_ref, mix_ref, tgt_ref, vec_ref, wi_ref, wo_ref, act_ref, h2_ref, dff_ref, df_ref, dr1_ref, st_ref, gu_ref):
        @pl.when(pl.program_id(0) == 0)
        def _():
            st_ref[...] = jnp.zeros_like(st_ref)

        xh1, rstd1, xmid, xh2, rstd2 = _mid_recompute(x_ref, mix_ref, vec_ref)
        h2 = (xh2 * (1.0 + vec_ref[4:5, :]) + vec_ref[3:4, :]).astype(BF16)
        h2_ref[...] = h2
        halves = [(slice(hh * FH_SHARD, (hh + 1) * FH_SHARD), slice(FH + hh * FH_SHARD, FH + (hh + 1) * FH_SHARD))
                  for hh in range(2)]
        for hh, (cs, cu) in enumerate(halves):
            g = _dot(h2, wi_ref[hh])
            u = _dot(h2, wi_ref[2 + hh])
            gu_ref[:, cs] = g
            gu_ref[:, cu] = u
            act_ref[:, cs] = (g * _sig(g) * u).astype(BF16)
        f = _dot(act_ref[...], wo_ref[...])
        r2 = ALPHA * xmid + vec_ref[5:6, :] * f
        yh, rstd = _ln(r2)
        y = yh * vec_ref[6:7, :] + vec_ref[7:8, :]
        err = y - tgt_ref[...]
        dy = err / D
        dr2 = _ln_bwd(dy * vec_ref[6:7, :], yh, rstd)
        st_ref[0:1, :] += _colsum(err * err)
        st_ref[1:2, :] += _colsum(dy * yh)
        st_ref[2:3, :] += _colsum(dy)
        st_ref[3:4, :] += _colsum(dr2 * f)

        df = (dr2 * vec_ref[5:6, :]).astype(BF16)
        df_ref[...] = df
        da_all = _dot_nt(df, wo_ref[...])
        for cs, cu in halves:
            da = da_all[:, cs]
            g = gu_ref[:, cs]
            u = gu_ref[:, cu]
            sg = _sig(g)
            dff_ref[:, cs] = (da * u * sg * (1.0 + g * (1.0 - sg))).astype(BF16)
            dff_ref[:, cu] = (da * g * sg).astype(BF16)
        dh2 = _dot_nt(dff_ref[:, :FH_SHARD], wi_ref[0])
        for s in range(1, N_SHARD):
            dh2 = dh2 + _dot_nt(dff_ref[:, s * FH_SHARD:(s + 1) * FH_SHARD], wi_ref[s])
        dxmid = _ln_bwd(dh2 * (1.0 + vec_ref[4:5, :]), xh2, rstd2) + ALPHA * dr2
        dr1 = _ln_bwd(dxmid * vec_ref[1:2, :], xh1, rstd1)
        dr1_ref[...] = dr1
        st_ref[8:9, :] += _colsum(dh2 * xh2)
        st_ref[9:10, :] += _colsum(dh2)
        st_ref[10:11, :] += _colsum(dxmid * xh1)
        st_ref[11:12, :] += _colsum(dxmid)
        st_ref[12:13, :] += _colsum(dr1 * mix_ref[...])

    return pl.pallas_call(
        body, name="ffn", grid=(seq // tm,),
        in_specs=[_rows(tm, D), _rows(tm, D), _rows(tm, D), _acc((8, D)), _resident((N_SHARD, D, FH_SHARD)), _resident((FH, D))],
        out_specs=[_rows(tm, FH), _rows(tm, D), _rows(tm, 2 * FH), _rows(tm, D), _rows(tm, D), _acc((16, D))],
        out_shape=[jax.ShapeDtypeStruct((seq, FH), BF16), jax.ShapeDtypeStruct((seq, D), BF16),
                   jax.ShapeDtypeStruct((seq, 2 * FH), BF16), jax.ShapeDtypeStruct((seq, D), BF16),
                   jax.ShapeDtypeStruct((seq, D), F32), jax.ShapeDtypeStruct((16, D), F32)],
        scratch_shapes=[pltpu.VMEM((tm, 2 * FH), F32)],
        compiler_params=_cp("arbitrary"),
    )(x, mix, tgt, vec, w_fi, w_fo)


def _mix_bwd(dr1, a, b, gab, uv, merged, ya, yb, vec, gp, ws_stack, ws_stack_t, bias_full, w_a, w_b, w_o, tm, scatter=()):
    seq = dr1.shape[0]
    last = seq // tm - 1
    ns = len(scatter)

    def body(dr1_ref, a_ref, b_ref, gab_ref, uv_ref, mg_ref, ya_ref, yb_ref, vec_ref, gp_ref, ws_ref, wst_ref, bias_ref,
             wa_ref, wb_ref, wo_ref, *rest):
        dya_ref, dp_ref, dws_ref, dbs_ref, st_ref, gwo_ref, gwa_ref, gwb_ref = rest[ns:ns + 8]
        acc_o, acc_a, acc_b = rest[2 * ns + 8:2 * ns + 11]
        comm = _AllToAll(rest[:ns], rest[ns + 8:2 * ns + 8], *rest[2 * ns + 11:]) if ns else None
        _host_start(pl.program_id(0), comm)

        @pl.when(pl.program_id(0) == 0)
        def _():
            dws_ref[...] = jnp.zeros_like(dws_ref)
            dbs_ref[...] = jnp.zeros_like(dbs_ref)
            st_ref[...] = jnp.zeros_like(st_ref)
            acc_o[...] = jnp.zeros_like(acc_o)
            acc_a[...] = jnp.zeros_like(acc_a)
            acc_b[...] = jnp.zeros_like(acc_b)

        dmix = (dr1_ref[...] * vec_ref[0:1, :]).astype(BF16)
        acc_o[...] += _dot_tn(mg_ref[...], dmix)
        dmerged = _dot_nt(dmix, wo_ref[...])
        sa = _sig(gab_ref[:, :D].astype(F32))
        sb = _sig(gab_ref[:, D:].astype(F32))
        da_f = dmerged * sa
        db_f = dmerged * sb
        da = da_f.astype(BF16)
        db = db_f.astype(BF16)
        dp_ref[:, 2 * G_W:2 * G_W + D] = (da_f * a_ref[...].astype(F32) * (1.0 - sa)).astype(BF16)
        dp_ref[:, 2 * G_W + D:] = (db_f * b_ref[...].astype(F32) * (1.0 - sb)).astype(BF16)
        dya_ref[...] = _dot_nt(da, wa_ref[...]).astype(BF16)
        dyb = _dot_nt(db, wb_ref[...])
        acc_a[...] += _dot_tn(ya_ref[...], da)
        acc_b[...] += _dot_tn(yb_ref[...], db)

        @pl.when(pl.program_id(0) == last)
        def _():
            gwo_ref[...] = acc_o[...].astype(BF16)
            gwa_ref[...] = acc_a[...].astype(BF16)
            gwb_ref[...] = acc_b[...].astype(BF16)

        for c in range(tm // BLK):
            rs = slice(c * BLK, (c + 1) * BLK)
            u = uv_ref[rs, :G_W].astype(F32)
            vb = uv_ref[rs, G_W:].astype(F32)
            gu, tu, tv, vhat, rstd, vn, s = _gmlp_chunk(u, vb, gp_ref, ws_ref, bias_ref)
            dyb_c = dyb[rs, :]
            ds = dyb_c * gu
            du = dyb_c * s * _gelu_grad(u, tu)
            ds_b = ds.astype(BF16)
            dvn_g = []
            for g in range(N_GRP):
                cg = slice(g * GRP_D, (g + 1) * GRP_D)
                dvn_g.append(_dot(wst_ref[:, g * BLK:(g + 1) * BLK], ds_b[:, cg]))
                dws_ref[g * BLK:(g + 1) * BLK, :] += _dot_nt(ds_b[:, cg], vn[:, cg])
            dvn = jnp.concatenate(dvn_g, axis=1)
            dbs_ref[...] += ds
            st_ref[0:1, :] += _colsum(dvn * vhat)
            st_ref[1:2, :] += _colsum(dvn)
            dgv = _ln_bwd(dvn * gp_ref[0:1, :], vhat, rstd)
            dvb = dgv * _gelu_grad(vb, tv)
            dp_ref[rs, :G_W] = du.astype(BF16)
            dp_ref[rs, G_W:2 * G_W] = dvb.astype(BF16)
        _host_finish(pl.program_id(0), last, comm)

    pw = 2 * G_W + 2 * D
    out = pl.pallas_call(
        body, name="mix_bwd", grid=(seq // tm,),
        in_specs=[_rows(tm, D), _rows(tm, D), _rows(tm, D), _rows(tm, 2 * D), _rows(tm, 2 * G_W), _rows(tm, D), _rows(tm, Q_W),
                  _rows(tm, G_W), _acc((8, D)), _acc((8, G_W)),
                  _resident((N_GRP * BLK, BLK)), _resident((BLK, N_GRP * BLK)), _acc((BLK, G_W)),
                  _resident((Q_W, D)), _resident((G_W, D)), _resident((D, D))] + _comm_specs(ns),
        out_specs=[_rows(tm, Q_W), _rows(tm, pw), _acc((N_GRP * BLK, BLK)), _acc((BLK, G_W)), _acc((8, G_W)),
                   _acc((D, D)), _acc((Q_W, D)), _acc((G_W, D))] + _comm_specs(ns),
        out_shape=[jax.ShapeDtypeStruct((seq, Q_W), BF16), jax.ShapeDtypeStruct((seq, pw), BF16),
                   jax.ShapeDtypeStruct((N_GRP * BLK, BLK), F32), jax.ShapeDtypeStruct((BLK, G_W), F32),
                   jax.ShapeDtypeStruct((8, G_W), F32), jax.ShapeDtypeStruct((D, D), BF16),
                   jax.ShapeDtypeStruct((Q_W, D), BF16), jax.ShapeDtypeStruct((G_W, D), BF16)]
        + [jax.ShapeDtypeStruct(v.shape, v.dtype) for v in scatter],
        scratch_shapes=[pltpu.VMEM((D, D), F32), pltpu.VMEM((Q_W, D), F32), pltpu.VMEM((G_W, D), F32)]
        + (_comm_scratch(ns) if ns else []),
        compiler_params=_cp("arbitrary"),
    )(dr1, a, b, gab, uv, merged, ya, yb, vec, gp, ws_stack, ws_stack_t, bias_full, w_a, w_b, w_o, *scatter)
    return out[:8], out[8:]


def _attn_bwd(q, kv, kvc, sink, dya, ya, lse, scatter=()):
    seq = q.shape[0]
    nb = seq // BLK
    n_ctx = kvc.shape[0]
    ns = len(scatter)
    Q_BLOCKS = 2
    nkv = Q_BLOCKS + 2
    steps = nb // Q_BLOCKS

    def body(q_ref, *rest):
        kv_refs = rest[:nkv]
        kvc_ref, sink_ref, do_ref, o_ref, lse_ref = rest[nkv:nkv + 5]
        rest = rest[nkv + 5:]
        dq_ref, dkv_ref, dkvc_ref, dsink_ref = rest[ns:ns + 4]
        comm = _AllToAll(rest[:ns], rest[ns + 4:2 * ns + 4], *rest[2 * ns + 4:]) if ns else None
        n = pl.program_id(0)
        _host_start(n, comm)

        @pl.when(n == 0)
        def _():
            dkv_ref[...] = jnp.zeros_like(dkv_ref)
            dkvc_ref[...] = jnp.zeros_like(dkvc_ref)
            dsink_ref[...] = jnp.zeros_like(dsink_ref)

        lane = lax.broadcasted_iota(jnp.int32, (1, LANES), 1)
        for sub in range(Q_BLOCKS):
            rs = slice(sub * BLK, (sub + 1) * BLK)
            blk = Q_BLOCKS * n + sub
            q = q_ref[rs, :]
            do = do_ref[rs, :]
            out = o_ref[rs, :]
            lse_all = lse_ref[rs, :]
            k_refs = (kvc_ref,) + kv_refs[sub:sub + 3]
            masks = _band_masks(blk, nb)
            dqs, dks, dvs = [], [], []
            for hk in range(N_KV):
                q4, ks, s = _attn_scores(q, k_refs, hk, masks)
                vs = [r[:, KV_W + hk * HEAD:KV_W + (hk + 1) * HEAD] for r in k_refs]
                lse4 = jnp.concatenate([lse_all[:, hk * GROUP + g:hk * GROUP + g + 1] for g in range(GROUP)], axis=0)
                do4 = _stack_heads(do, hk)
                delta = jnp.sum(do4.astype(F32) * _stack_heads(out, hk).astype(F32), axis=-1, keepdims=True)
                p = [jnp.exp((t - lse4).astype(BF16)) for t in s]
                ds = [t * (_dot_nt(do4, v) - delta).astype(BF16) for t, v in zip(p, vs)]
                dq4 = _dot(ds[0], ks[0])
                for t, k in zip(ds[1:], ks[1:]):
                    dq4 = dq4 + _dot(t, k)
                dq4 = dq4 * SCALE
                dqs += [dq4[g * BLK:(g + 1) * BLK, :] for g in range(GROUP)]
                dks.append([_dot_tn(t, q4) for t in ds])
                dvs.append([_dot_tn(t, do4) for t in p])
                ps = jnp.exp(_sink_rows(sink_ref, hk) - lse4) * delta
                for g in range(GROUP):
                    part = -jnp.sum(ps[g * BLK:(g + 1) * BLK, :], axis=0, keepdims=True)
                    dsink_ref[0:1, :] += jnp.where(lane == hk * GROUP + g, part, 0.0)
            dq_ref[rs, :] = jnp.concatenate(dqs, axis=1)

            def piece(i):
                return jnp.concatenate([dks[0][i], dks[1][i], dvs[0][i], dvs[1][i]], axis=1)

            dkvc_ref[...] += piece(0)
            starts = (jnp.maximum(blk - 1, 0), blk, jnp.minimum(blk + 1, nb - 1))
            for i, st in enumerate(starts):
                r = pl.ds(pl.multiple_of(st * BLK, BLK), BLK)
                dkv_ref[r, :] += piece(i + 1)
        _host_finish(n, steps - 1, comm)

    tq = Q_BLOCKS * BLK
    out = pl.pallas_call(
        body, name="attn_bwd", grid=(steps,),
        in_specs=[_rows(tq, Q_W)] + _kv_specs(nb, Q_BLOCKS) + [_acc((n_ctx, 2 * KV_W)), pl.BlockSpec(memory_space=pltpu.SMEM),
                                                     _rows(tq, Q_W), _rows(tq, Q_W), _rows(tq, LANES)] + _comm_specs(ns),
        out_specs=[_rows(tq, Q_W), _acc((seq, 2 * KV_W)), _acc((n_ctx, 2 * KV_W)), _acc((8, LANES))] + _comm_specs(ns),
        out_shape=[jax.ShapeDtypeStruct((seq, Q_W), F32), jax.ShapeDtypeStruct((seq, 2 * KV_W), F32),
                   jax.ShapeDtypeStruct((n_ctx, 2 * KV_W), F32), jax.ShapeDtypeStruct((8, LANES), F32)]
        + [jax.ShapeDtypeStruct(v.shape, v.dtype) for v in scatter],
        scratch_shapes=_comm_scratch(ns) if ns else [],
        compiler_params=_cp("arbitrary"),
    )(q, *([kv] * nkv), kvc, sink, dya, ya, lse, *scatter)
    return out[:4], out[4:]


def _proj_bwd(dq, dkv, dpb, x, dr1, modx, w_in, cos, sin, tm, scatter=()):
    seq = x.shape[0]
    pw = IN_W - Q_W - 2 * KV_W
    ns = len(scatter)

    def body(dq_ref, dkv_ref, dpb_ref, x_ref, dr1_ref, mod_ref, w_ref, cos_ref, sin_ref, *rest):
        dqkv_ref, gx_ref, st_ref = rest[ns:ns + 3]
        comm = _AllToAll(rest[:ns], rest[ns + 3:2 * ns + 3], *rest[2 * ns + 3:]) if ns else None
        _host_start(pl.program_id(0), comm)

        @pl.when(pl.program_id(0) == 0)
        def _():
            st_ref[...] = jnp.zeros_like(st_ref)

        cos1, sin1 = cos_ref[...], sin_ref[...]
        cos2 = jnp.concatenate([cos1, cos1], axis=1)
        sin2 = jnp.concatenate([sin1, sin1], axis=1)
        for j in range(Q_W // 256):
            cs = slice(256 * j, 256 * (j + 1))
            dqkv_ref[:, cs] = _unrope(dq_ref[:, cs], cos2, sin2).astype(BF16)
        dqkv_ref[:, Q_W:Q_W + KV_W] = _unrope(dkv_ref[:, :KV_W], cos1, sin1).astype(BF16)
        dqkv_ref[:, Q_W + KV_W:] = dkv_ref[:, KV_W:].astype(BF16)
        o = Q_W + 2 * KV_W
        dh = _dot(dqkv_ref[...], w_ref[:o, :]) + _dot(dpb_ref[...], w_ref[o:, :])
        xhat, rstd = _ln(x_ref[...])
        st_ref[0:1, :] += _colsum(dh)
        st_ref[1:2, :] += _colsum(dh * xhat)
        gx_ref[...] = _ln_bwd(dh * (1.0 + mod_ref[1:2, :]), xhat, rstd) + ALPHA * dr1_ref[...]
        _host_finish(pl.program_id(0), seq // tm - 1, comm)

    out = pl.pallas_call(
        body, name="proj_bwd", grid=(seq // tm,),
        in_specs=[_rows(tm, Q_W), _rows(tm, 2 * KV_W), _rows(tm, pw), _rows(tm, D), _rows(tm, D), _acc((8, D)),
                  _resident((IN_W, D)), _rows(tm, LANES), _rows(tm, LANES)] + _comm_specs(ns),
        out_specs=[_rows(tm, Q_W + 2 * KV_W), _rows(tm, D), _acc((8, D))] + _comm_specs(ns),
        out_shape=[jax.ShapeDtypeStruct((seq, Q_W + 2 * KV_W), BF16), jax.ShapeDtypeStruct((seq, D), F32),
                   jax.ShapeDtypeStruct((8, D), F32)] + [jax.ShapeDtypeStruct(v.shape, v.dtype) for v in scatter],
        scratch_shapes=_comm_scratch(ns) if ns else [],
        compiler_params=_cp("arbitrary"),
    )(dq, dkv, dpb, x, dr1, modx, w_in, cos, sin, *scatter)
    return out[:3], out[3:]


def _ctx_bwd(dkvc, ctx, hc, w_kv):
    n_ctx = ctx.shape[0]

    def body(dkvc_ref, ctx_ref, hc_ref, w_ref, dw_ref, st_ref):
        d = dkvc_ref[...].astype(BF16)
        dw_ref[...] = _dot_tn(d, hc_ref[...])
        dhc = _dot(d, w_ref[...])
        xhat, _ = _ln(ctx_ref[...])
        st_ref[...] = jnp.zeros_like(st_ref)
        st_ref[0:1, :] = _colsum(dhc)
        st_ref[1:2, :] = _colsum(dhc * xhat)

    return pl.pallas_call(
        body, name="ctx_bwd", grid=(1,),
        in_specs=[_acc((n_ctx, 2 * KV_W)), _acc((n_ctx, D)), _acc((n_ctx, D)), _acc((2 * KV_W, D))],
        out_specs=[_acc((2 * KV_W, D)), _acc((8, D))],
        out_shape=[jax.ShapeDtypeStruct((2 * KV_W, D), F32), jax.ShapeDtypeStruct((8, D), F32)],
        compiler_params=_cp("arbitrary"),
    )(dkvc, ctx, hc, w_kv)


def _tn_matmul(a, b, tn, name, out_dtype, shard_major=False, init=None, tk=512, scatter=(), gather=()):
    t, ka = a.shape
    n = b.shape[1]
    tk = min(tk, t)
    nk = t // tk
    nj = n // tn
    has_init = init is not None
    assert not (scatter and gather)
    moved = list(scatter) + list(gather)
    pattern = _AllToAll if scatter else _Gather
    ns = len(moved)
    n_in = 3 if has_init else 2

    def body(*refs):
        a_ref, b_ref = refs[:2]
        i_ref = refs[2] if has_init else None
        rest = refs[n_in:]
        o_ref = rest[ns]
        acc_ref = rest[2 * ns + 1]
        comm = pattern(rest[:ns], rest[ns + 1:2 * ns + 1], *rest[2 * ns + 2:]) if ns else None
        k = pl.program_id(1)
        step = pl.program_id(0) * nk + k
        _host_start(step, comm)

        @pl.when(k == 0)
        def _():
            acc_ref[...] = i_ref[...] if has_init else jnp.zeros_like(acc_ref)

        acc_ref[...] += _dot_tn(a_ref[...], b_ref[...])

        @pl.when(k == nk - 1)
        def _():
            o_ref[...] = acc_ref[...].astype(out_dtype)

        _host_finish(step, nj * nk - 1, comm, forward_at=(3 * nj * nk) // 4 if gather else None)

    in_specs = [pl.BlockSpec((tk, ka), lambda j, k: (k, 0)), pl.BlockSpec((tk, tn), lambda j, k: (k, j))]
    args = [a, b]
    if has_init:
        in_specs.append(pl.BlockSpec((ka, tn), lambda j, k: (0, j)))
        args.append(init)
    if shard_major:
        out_spec = pl.BlockSpec((None, ka, tn), lambda j, k: (j, 0, 0))
        out_shape = jax.ShapeDtypeStruct((nj, ka, tn), out_dtype)
    else:
        out_spec = pl.BlockSpec((ka, tn), lambda j, k: (0, j))
        out_shape = jax.ShapeDtypeStruct((ka, n), out_dtype)
    out = pl.pallas_call(
        body, name=name, grid=(nj, nk), in_specs=in_specs + _comm_specs(ns), out_specs=[out_spec] + _comm_specs(ns),
        out_shape=[out_shape] + [jax.ShapeDtypeStruct(v.shape, v.dtype) for v in scatter] + _gathered_shapes(gather),
        scratch_shapes=[pltpu.VMEM((ka, tn), F32)] + (_comm_scratch(ns) if ns else []),
        compiler_params=_cp("arbitrary", "arbitrary"),
    )(*args, *moved)
    return (out[0], out[1:]) if ns else out[0]


ADA_TILE = 512


def _gather_and_modulate(c_rows, c_ctx, w_half, w_ada):
    cs = w_ada.shape[1]
    vmem = pl.BlockSpec(memory_space=pltpu.VMEM)

    def body(c_ref, cctx_ref, wh_ref, wada_ref, cg_ref, wg_ref, mod_ref, modg_ref, sc_ref,
             c_v, wada_v, mod_v, send_c, recv_c, send_w, recv_w, send_m, recv_m, local):
        gc = _Gather([c_ref], [cg_ref], send_c, recv_c)
        gw = _Gather([wh_ref], [wg_ref], send_w, recv_w)
        gm = _Gather([mod_ref], [modg_ref], send_m, recv_m)
        px, py, pc = _my_pos()
        mine = 4 * px + 2 * py + pc
        gc.start()
        gw.start()
        own_c = pltpu.make_async_copy(c_ref, cg_ref.at[mine], local.at[0])
        load_w = pltpu.make_async_copy(wada_ref, wada_v, local.at[1])
        own_c.start()
        load_w.start()
        gc.finish()
        own_c.wait()
        load_c = pltpu.make_async_copy(cg_ref, c_v, local.at[2])
        load_c.start()
        load_c.wait()
        cc = jnp.concatenate([c_v[i, 0:1, :] for i in range(N_DEV)] + [cctx_ref[...], jnp.zeros((7, D), F32)], axis=0)
        sc = cc * _sig(cc)
        sc_ref[...] = sc
        load_w.wait()
        for j in range(cs // ADA_TILE):
            cols = slice(j * ADA_TILE, (j + 1) * ADA_TILE)
            mod_v[:, cols] = _dot(sc.astype(BF16), wada_v[:, cols].astype(BF16))
        store_m = pltpu.make_async_copy(mod_v, mod_ref, local.at[3])
        store_m.start()
        store_m.wait()
        own_m = pltpu.make_async_copy(mod_ref, modg_ref.at[mine], local.at[4])
        own_m.start()
        gm.start()
        gm.finish()
        own_m.wait()
        gw.finish()

    any_ = pl.BlockSpec(memory_space=pl.ANY)
    return pl.pallas_call(
        body, name="gather_and_modulate",
        in_specs=[any_, vmem, any_, any_], out_specs=[any_, any_, any_, any_, vmem],
        scratch_shapes=[pltpu.VMEM((N_DEV, 8, D), F32), pltpu.VMEM((D, cs), F32), pltpu.VMEM((16, cs), F32)]
        + _comm_scratch(1) + _comm_scratch(1) + _comm_scratch(1) + [pltpu.SemaphoreType.DMA((5,))],
        out_shape=[jax.ShapeDtypeStruct((N_DEV, 8, D), F32), jax.ShapeDtypeStruct((N_DEV,) + w_half.shape, w_half.dtype),
                   jax.ShapeDtypeStruct((16, cs), F32), jax.ShapeDtypeStruct((N_DEV, 16, cs), F32),
                   jax.ShapeDtypeStruct((16, D), F32)],
        compiler_params=pltpu.CompilerParams(vmem_limit_bytes=VMEM_LIMIT),
    )(c_rows, c_ctx, w_half, w_ada)


def _ada_bwd(sc_all_t, dm_all, dmc, w_ada, m, v):
    cs = w_ada.shape[1]
    tr = LANES

    def body(st_ref, dm_ref, dmc_ref, w_ref, m_ref, v_ref, gw_ref, d_ref, nm_ref, nv_ref, part_ref):
        g = _dot(st_ref[...].astype(BF16), dm_ref[...].astype(BF16))
        gw_ref[...] = g
        d_ref[...], nm_ref[...], nv_ref[...] = _adam_update(w_ref[...], g, m_ref[...], v_ref[...])
        part_ref[...] = _dot_nt(dmc_ref[...].astype(BF16), w_ref[...].astype(BF16))

    rows = _rows(tr, cs)
    shp = jax.ShapeDtypeStruct((D, cs), F32)
    return pl.pallas_call(
        body, name="ada_bwd_adamw", grid=(D // tr,),
        in_specs=[_rows(tr, 16), _acc((16, cs)), _acc((8, cs)), rows, rows, rows],
        out_specs=[rows] * 4 + [pl.BlockSpec((8, tr), lambda i: (0, i))],
        out_shape=[shp] * 4 + [jax.ShapeDtypeStruct((8, D), F32)],
        compiler_params=_cp("arbitrary"),
    )(sc_all_t, dm_all, dmc, w_ada, m, v)


def _sum8(x, name, tr=256):
    _, r, c = x.shape
    tr = min(tr, r)
    while r % tr:
        tr -= 16

    def body(x_ref, o_ref):
        acc = x_ref[0].astype(F32)
        for i in range(1, N_DEV):
            acc = acc + x_ref[i].astype(F32)
        o_ref[...] = acc

    return pl.pallas_call(
        body, name=name, grid=(r // tr,),
        in_specs=[pl.BlockSpec((N_DEV, tr, c), lambda i: (0, i, 0))],
        out_specs=pl.BlockSpec((tr, c), lambda i: (i, 0)),
        out_shape=jax.ShapeDtypeStruct((r, c), F32),
        compiler_params=_cp("arbitrary"),
    )(x)


def _sum_blocks(recv, src, me, name, tr=64):
    _, r, c = recv.shape
    tr = min(tr, r)
    while r % tr:
        tr -= 16

    def body(me_ref, recv_ref, own_ref, o_ref):
        acc = own_ref[...].astype(F32)
        for k in range(1, N_DEV):
            acc = acc + recv_ref[me_ref[0] ^ k].astype(F32)
        o_ref[...] = acc

    return pl.pallas_call(
        body, name=name,
        grid_spec=pltpu.PrefetchScalarGridSpec(
            num_scalar_prefetch=1, grid=(r // tr,),
            in_specs=[pl.BlockSpec((N_DEV, tr, c), lambda i, me_ref: (0, i, 0)),
                      pl.BlockSpec((None, tr, c), lambda i, me_ref: (me_ref[0], i, 0))],
            out_specs=pl.BlockSpec((tr, c), lambda i, me_ref: (i, 0))),
        out_shape=jax.ShapeDtypeStruct((r, c), F32),
        compiler_params=_cp("arbitrary"),
    )(me, recv, src)


def _sum8_many(xs, name):
    n = len(xs)

    def body(*refs):
        for x_ref, o_ref in zip(refs[:n], refs[n:]):
            acc = x_ref[0]
            for i in range(1, N_DEV):
                acc = acc + x_ref[i]
            o_ref[...] = acc

    vmem = pl.BlockSpec(memory_space=pltpu.VMEM)
    return pl.pallas_call(
        body, name=name, in_specs=[vmem] * n, out_specs=[vmem] * n,
        out_shape=[jax.ShapeDtypeStruct(v.shape[1:], v.dtype) for v in xs],
        compiler_params=pltpu.CompilerParams(vmem_limit_bytes=VMEM_LIMIT),
    )(*xs)


def _adam_update(w, g, m, v):
    nm = ADAM_B1 * m + (1.0 - ADAM_B1) * g
    nv = ADAM_B2 * v + (1.0 - ADAM_B2) * (g * g)
    m_hat = nm / (1.0 - ADAM_B1 ** ADAM_STEP)
    v_hat = nv / (1.0 - ADAM_B2 ** ADAM_STEP)
    return -ADAM_LR * (m_hat / (jnp.sqrt(v_hat) + ADAM_EPS) + ADAM_WD * w), nm, nv


ROW_LOSS, ROW_LN2_G, ROW_LN2_B, ROW_LN1_G, ROW_LN1_B = 0, 1, 2, 10, 11
ROWS_DMOD_X = (16, 17, 12, 9, 8, 3)
ROWS_DMOD_C = (24, 25)
SMALL = ("c_ctx", "b_ada", "attn_sink", "gmlp_ln_g", "gmlp_ln_b", "w_spatial", "b_spatial", "ln1_g", "ln1_b", "ln2_g", "ln2_b")


def _adamw_small(sums, dsc, w, m, v):
    n = len(SMALL)

    def body(*refs):
        st_ref, gm_ref, sk_ref, ws_ref, bs_ref, dsc_ref = refs[:6]
        w_refs = dict(zip(SMALL, refs[6:6 + n]))
        m_refs = dict(zip(SMALL, refs[6 + n:6 + 2 * n]))
        v_refs = dict(zip(SMALL, refs[6 + 2 * n:6 + 3 * n]))
        outs = refs[6 + 3 * n:]
        c = w_refs["c_ctx"][...]
        sg = _sig(c)
        dmod = [st_ref[r:r + 1, :] for r in ROWS_DMOD_X]
        dmod[0] = dmod[0] + st_ref[ROWS_DMOD_C[0]:ROWS_DMOD_C[0] + 1, :]
        dmod[1] = dmod[1] + st_ref[ROWS_DMOD_C[1]:ROWS_DMOD_C[1] + 1, :]
        grads = dict(
            c_ctx=dsc_ref[0:1, :] * (sg * (1.0 + c * (1.0 - sg))),
            b_ada=jnp.concatenate(dmod, axis=1),
            attn_sink=sk_ref[0:1, 0:N_KV * GROUP],
            gmlp_ln_g=gm_ref[0:1, :], gmlp_ln_b=gm_ref[1:2, :],
            w_spatial=ws_ref[...], b_spatial=bs_ref[...],
            ln1_g=st_ref[ROW_LN1_G:ROW_LN1_G + 1, :], ln1_b=st_ref[ROW_LN1_B:ROW_LN1_B + 1, :],
            ln2_g=st_ref[ROW_LN2_G:ROW_LN2_G + 1, :], ln2_b=st_ref[ROW_LN2_B:ROW_LN2_B + 1, :])
        for i, name in enumerate(SMALL):
            g = grads[name]
            d, nm, nv = _adam_update(w_refs[name][...], g, m_refs[name][...], v_refs[name][...])
            outs[i][...] = g
            outs[n + i][...] = d
            outs[2 * n + i][...] = nm
            outs[3 * n + i][...] = nv

    vmem = pl.BlockSpec(memory_space=pltpu.VMEM)
    args = list(sums) + [dsc] + [w[k] for k in SMALL] + [m[k] for k in SMALL] + [v[k] for k in SMALL]
    shapes = [jax.ShapeDtypeStruct(w[k].shape, F32) for k in SMALL]
    out = pl.pallas_call(
        body, name="adamw_small", in_specs=[vmem] * len(args), out_specs=[vmem] * (4 * n), out_shape=shapes * 4,
        compiler_params=pltpu.CompilerParams(vmem_limit_bytes=VMEM_LIMIT),
    )(*args)
    return [dict(zip(SMALL, out[i * n:(i + 1) * n])) for i in range(4)]


def _adamw_halves(w, mine, theirs, m, v, c_arr, name):
    r, c = w.shape
    tr = min(128, r // 2)
    while (r // 2) % tr:
        tr -= 8
    nt = (r // 2) // tr

    def body(c_ref, w_ref, mine_ref, theirs_ref, m_ref, v_ref, g_ref, d_ref, nm_ref, nv_ref):
        g = jnp.where(pl.program_id(0) == c_ref[0], mine_ref[...], theirs_ref[...])
        g_ref[...] = g
        d_ref[...], nm_ref[...], nv_ref[...] = _adam_update(w_ref[...], g, m_ref[...], v_ref[...])

    whole = pl.BlockSpec((tr, c), lambda hb, i, c_ref: (hb * nt + i, 0))
    mine_spec = pl.BlockSpec((tr, c), lambda hb, i, c_ref: (jnp.where(hb == c_ref[0], i, 0), 0))
    theirs_spec = pl.BlockSpec((tr, c), lambda hb, i, c_ref: (jnp.where(hb == c_ref[0], 0, i), 0))
    shp = jax.ShapeDtypeStruct((r, c), F32)
    return pl.pallas_call(
        body, name=name,
        grid_spec=pltpu.PrefetchScalarGridSpec(
            num_scalar_prefetch=1, grid=(2, nt), in_specs=[whole, mine_spec, theirs_spec, whole, whole],
            out_specs=[whole] * 4),
        out_shape=[shp] * 4,
        compiler_params=_cp("arbitrary", "arbitrary"),
    )(c_arr, w, mine, theirs, m, v)


def _my_pos():
    return lax.axis_index("x"), lax.axis_index("y"), lax.axis_index("c")


N_COPY = 7


class _Gather:
    def __init__(self, x_refs, out_refs, send_sems, recv_sems):
        self.x_refs, self.out_refs = x_refs, out_refs
        self.send_sems, self.recv_sems = send_sems, recv_sems
        x, y, c = _my_pos()
        self.c = c
        self.me, self.sibling = (x, y, c), (x, y, 1 - c)
        self.chips = [(1 - x, y), (x, 1 - y), (1 - x, 1 - y)]

    def _copy(self, a, k, block, to, from_input=False):
        px, py, pc = block
        rows = self.out_refs[a].at[4 * px + 2 * py + pc]
        return pltpu.make_async_remote_copy(
            src_ref=self.x_refs[a] if from_input else rows, dst_ref=rows,
            send_sem=self.send_sems.at[a * N_COPY + k], recv_sem=self.recv_sems.at[a * N_COPY + k],
            device_id=to, device_id_type=MESH)

    def start(self):
        n = len(self.x_refs)
        for a in range(n):
            self._copy(a, 0, self.me, self.sibling, from_input=True).start()
        for j, chip in enumerate(self.chips):
            for a in range(n):
                self._copy(a, 1 + j, self.me, (*chip, self.c), from_input=True).start()

    def forward(self):
        c = self.c
        for j, chip in enumerate(self.chips):
            for a in range(len(self.x_refs)):
                self._copy(a, 1 + j, (*chip, c), self.me).wait_recv()
                self._copy(a, 4 + j, (*chip, c), self.sibling).start()

    def finish(self):
        self.forward()
        self.drain()

    def drain(self):
        n = len(self.x_refs)
        c = self.c
        for a in range(n):
            self._copy(a, 0, self.sibling, self.me).wait_recv()
        for j, chip in enumerate(self.chips):
            for a in range(n):
                self._copy(a, 4 + j, (*chip, 1 - c), self.me).wait_recv()
        for a in range(n):
            self._copy(a, 0, self.me, self.sibling, from_input=True).wait_send()
            for j, chip in enumerate(self.chips):
                self._copy(a, 1 + j, self.me, (*chip, c), from_input=True).wait_send()
                self._copy(a, 4 + j, (*chip, c), self.sibling).wait_send()


def _comm_scratch(n):
    return [pltpu.SemaphoreType.DMA((n * N_COPY,)), pltpu.SemaphoreType.DMA((n * N_COPY,))]


def _comm_specs(n):
    return [pl.BlockSpec(memory_space=pl.ANY)] * n


def _gathered_shapes(xs):
    return [jax.ShapeDtypeStruct((N_DEV,) + v.shape, v.dtype) for v in xs]


def _with_own(gathered, xs, me):
    return [lax.dynamic_update_index_in_dim(g, v, me, 0) for g, v in zip(gathered, xs)]


class _AllToAll:
    def __init__(self, x_refs, out_refs, send_sems, recv_sems):
        self.x_refs, self.out_refs = x_refs, out_refs
        self.send_sems, self.recv_sems = send_sems, recv_sems
        self.pos = _my_pos()
        x, y, c = self.pos
        self.me = 4 * x + 2 * y + c

    def _peer(self, k):
        x, y, c = self.pos
        return (x ^ ((k >> 2) & 1), y ^ ((k >> 1) & 1), c ^ (k & 1))

    def _copy(self, a, k):
        p = self._peer(k)
        return pltpu.make_async_remote_copy(
            src_ref=self.x_refs[a].at[4 * p[0] + 2 * p[1] + p[2]], dst_ref=self.out_refs[a].at[self.me],
            send_sem=self.send_sems.at[a * N_COPY + k - 1], recv_sem=self.recv_sems.at[a * N_COPY + k - 1],
            device_id=p, device_id_type=MESH)

    def start(self):
        for k in range(1, N_DEV):
            for a in range(len(self.x_refs)):
                self._copy(a, k).start()

    def finish(self):
        for a in range(len(self.x_refs)):
            for k in range(1, N_DEV):
                self._copy(a, k).wait_recv()
            for k in range(1, N_DEV):
                self._copy(a, k).wait_send()


def _exchange(name, scatter=(), gather=(), sibling=()):
    ns, ng, nx = len(scatter), len(gather), len(sibling)
    n = ns + ng + nx

    def body(*refs):
        ins, outs, sems = refs[:n], refs[n:2 * n], list(refs[2 * n:])
        t = _AllToAll(ins[:ns], outs[:ns], sems.pop(0), sems.pop(0)) if ns else None
        g = _Gather(ins[ns:ns + ng], outs[ns:ns + ng], sems.pop(0), sems.pop(0)) if ng else None
        x, y, c = _my_pos()

        def push(a):
            return pltpu.make_async_remote_copy(
                src_ref=ins[ns + ng + a], dst_ref=outs[ns + ng + a], send_sem=sems[0].at[a], recv_sem=sems[1].at[a],
                device_id=(x, y, 1 - c), device_id_type=MESH)

        for comm in (g, t):
            if comm is not None:
                comm.start()
        for a in range(nx):
            push(a).start()
        for comm in (g, t):
            if comm is not None:
                comm.finish()
        for a in range(nx):
            push(a).wait_recv()
            push(a).wait_send()

    scratch = (_comm_scratch(ns) if ns else []) + (_comm_scratch(ng) if ng else [])
    scratch += [pltpu.SemaphoreType.DMA((nx,)), pltpu.SemaphoreType.DMA((nx,))] if nx else []
    out = pl.pallas_call(
        body, name=name,
        out_shape=[jax.ShapeDtypeStruct(v.shape, v.dtype) for v in scatter] + _gathered_shapes(gather)
        + [jax.ShapeDtypeStruct(v.shape, v.dtype) for v in sibling],
        in_specs=_comm_specs(n), out_specs=_comm_specs(n), scratch_shapes=scratch,
    )(*scatter, *gather, *sibling)
    return out[:ns], out[ns:ns + ng], out[ns + ng:]


def _row_tile(seq, want):
    return min(want, seq)


def _local_step(x, ctx, tgt, mod_x, mod_c, wb, sink, gmlp_g, gmlp_b, w_s, b_s, ln1_g, ln1_b, ln2_g, ln2_b,
                later=None, me=None):
    seq = x.shape[0]
    on_mesh = me is not None
    modx1 = jnp.concatenate([mod_x[0:2], jnp.zeros((6, D), F32)], axis=0)
    modc = jnp.concatenate([mod_c[0:2], jnp.zeros((6, D), F32)], axis=0)
    vec = jnp.concatenate([mod_x[2:3], ln1_g, ln1_b, mod_x[3:6], ln2_g, ln2_b], axis=0)
    gp = jnp.concatenate([gmlp_g, gmlp_b, jnp.zeros((6, G_W), F32)], axis=0)
    ws_stack = w_s.reshape(N_GRP * BLK, BLK).astype(BF16)
    ws_stack_t = jnp.transpose(w_s, (2, 0, 1)).reshape(BLK, N_GRP * BLK).astype(BF16)
    bias_full = jnp.repeat(b_s.T, GRP_D, axis=1)
    cos, sin = _rope_tables(seq)
    w_in = wb["w_in"]
    w_kv = w_in[Q_W:Q_W + 2 * KV_W, :]
    tm_big = _row_tile(seq, 512)
    tm_ffn = _row_tile(seq, 256)

    hc, kvc, vac = _ctx_fwd(ctx, modc, w_kv)
    behind_proj = ("w_a", "w_b", "w_o") if on_mesh else ()
    behind_attn = ("w_fi",) if on_mesh else ()
    behind_mix = ("w_fo",) if on_mesh else ()
    wb = dict(wb)

    def whole(names, gathered):
        for n, g in zip(names, _with_own(list(gathered), [later[n] for n in names], me)):
            wb[n] = g.reshape(-1, g.shape[2]) if n in ROW_SHARDED else g.reshape(N_SHARD, 2 * g.shape[1], g.shape[2])

    (h, q, kv, va, uv, gab), got = _proj_fwd(x, modx1, w_in, cos, sin, _row_tile(seq, 1024), gather=[later[n] for n in behind_proj])
    whole(behind_proj, got)
    if on_mesh:
        for n in ("w_a", "w_b"):
            wb[n] = wb[n].transpose(1, 0, 2).reshape(wb[n].shape[1], D)
    (ya, lse), got = _attn_fwd(q, kv, va, kvc, vac, sink, gather=[later[n] for n in behind_attn])
    whole(behind_attn, got)
    (a, b, mix, merged, yb), got = _mix_fwd(uv, gab, ya, gp, ws_stack, bias_full, wb["w_a"], wb["w_b"], wb["w_o"], tm_big,
                                            gather=[later[n] for n in behind_mix])
    whole(behind_mix, got)
    act, h2, dff, df, dr1, st_ffn = _ffn(x, mix, tgt, vec, wb["w_fi"], wb["w_fo"], tm_ffn)
    blocks, recv = {}, {}
    blocks["w_fo"] = _eighths(_tn_matmul(act, df, 512, "tn_w_ffn_out", BF16, tk=2048))
    if on_mesh:
        g_w_fi, (recv["w_fo"],) = _tn_matmul(h2, dff, FH_SHARD, "tn_w_ffn_in", BF16, shard_major=True, tk=2048,
                                            scatter=[blocks["w_fo"]])
    else:
        g_w_fi = _tn_matmul(h2, dff, FH_SHARD, "tn_w_ffn_in", BF16, shard_major=True, tk=2048)
    blocks["w_fi"] = _eighths(g_w_fi)
    (dya, dpb, dws, dbs_full, st4, g_w_o, g_w_a, g_w_b), got = _mix_bwd(
        dr1, a, b, gab, uv, merged, ya, yb, vec, gp, ws_stack, ws_stack_t, bias_full, wb["w_a"], wb["w_b"], wb["w_o"],
        tm_big, scatter=[blocks["w_fi"]] if on_mesh else ())
    recv.update(zip(("w_fi",), got))
    shard_major = [g.reshape(g.shape[0], N_SHARD, D // N_SHARD).transpose(1, 0, 2) for g in (g_w_a, g_w_b)]
    blocks.update(w_o=_eighths(g_w_o), w_a=_eighths(shard_major[0]), w_b=_eighths(shard_major[1]))
    mixer = ("w_o", "w_a", "w_b") if on_mesh else ()
    (dq, dkv, dkvc, dsink), got = _attn_bwd(q, kv, kvc, sink, dya, ya, lse, scatter=[blocks[n] for n in mixer])
    recv.update(zip(mixer, got))
    g_wkv_ctx, st0 = _ctx_bwd(dkvc, ctx, hc, w_kv)
    (dqkv, grad_x, st1), _ = _proj_bwd(dq, dkv, dpb, x, dr1, modx1, w_in, cos, sin, tm_big)
    dbs = jnp.sum(dbs_full.reshape(BLK, N_GRP, GRP_D), axis=2).T
    early = [jnp.concatenate([st_ffn, st0], axis=0), st4, dsink, dws, dbs]
    init = jnp.pad(g_wkv_ctx, ((Q_W, 0), (0, 0)))
    g_qkv = _tn_matmul(dqkv, h, D, "tn_w_in_qkv", BF16, init=init, tk=1024)
    if on_mesh:
        g_rest, early_gathered = _tn_matmul(dpb, h, D, "tn_w_in_rest", BF16, tk=1024, gather=early)
    else:
        g_rest, early_gathered = _tn_matmul(dpb, h, D, "tn_w_in_rest", BF16, tk=1024), None
    blocks["w_in"] = _eighths(jnp.concatenate([g_qkv, g_rest], axis=0))
    return grad_x, dict(early=early, early_gathered=early_gathered, late=st1), blocks, recv


BIG = ("w_in", "w_a", "w_b", "w_o", "w_fi", "w_fo")
ROW_SHARDED = ("w_o", "w_fo")


def _half_of_shard(shard, c):
    r = shard.shape[0]
    return lax.dynamic_slice_in_dim(shard, c * (r // 2), r // 2, axis=0)


def _eighths(v):
    rows = v.shape[-2] * (v.shape[0] if v.ndim == 3 else 1)
    return v.reshape(N_DEV, rows // N_DEV, v.shape[-1])


def kernel(x, c, ctx, c_ctx, w_ada, b_ada, w_in, attn_sink, gmlp_ln_g, gmlp_ln_b, w_spatial, b_spatial, w_branch_a, w_branch_b, w_out, ln1_g, ln1_b, w_ffn_in, w_ffn_out, ln2_g, ln2_b, loss_target, m_c_ctx, m_w_ada, m_b_ada, m_w_in, m_attn_sink, m_gmlp_ln_g, m_gmlp_ln_b, m_w_spatial, m_b_spatial, m_w_branch_a, m_w_branch_b, m_w_out, m_ln1_g, m_ln1_b, m_w_ffn_in, m_w_ffn_out, m_ln2_g, m_ln2_b, v_c_ctx, v_w_ada, v_b_ada, v_w_in, v_attn_sink, v_gmlp_ln_g, v_gmlp_ln_b, v_w_spatial, v_b_spatial, v_w_branch_a, v_w_branch_b, v_w_out, v_ln1_g, v_ln1_b, v_w_ffn_in, v_w_ffn_out, v_ln2_g, v_ln2_b):
    mx, my, mc = _my_pos()
    me = 4 * mx + 2 * my + mc
    chip = 2 * mx + my
    shards = dict(w_in=w_in[0].T, w_a=w_branch_a[0], w_b=w_branch_b[0], w_o=w_out[0], w_fi=w_ffn_in[0], w_fo=w_ffn_out[0])

    halves = {n: _half_of_shard(shards[n], mc).astype(BF16) for n in BIG}
    c_rows = jnp.concatenate([c, jnp.zeros((7, D), F32)], axis=0)
    _, g_in, _, mod_g, sc_all = _gather_and_modulate(c_rows, c_ctx[None, :], halves["w_in"], w_ada[0])
    wb = dict(w_in=_with_own([g_in], [halves["w_in"]], me)[0].reshape(IN_W, D))
    mod_all = jnp.concatenate([mod_g[2 * s] for s in range(4)], axis=1) + b_ada
    mod_x = lax.dynamic_slice_in_dim(mod_all, me, 1, axis=0).reshape(6, D)
    mod_c = mod_all[8].reshape(6, D)[0:2]

    grad_x, small, blocks, recv = _local_step(
        x[0], ctx[0], loss_target[0], mod_x, mod_c, wb, attn_sink, gmlp_ln_g, gmlp_ln_b, w_spatial[0], b_spatial[0],
        ln1_g, ln1_b, ln2_g, ln2_b, later=halves, me=me)

    me_arr = jnp.reshape(me, (1,)).astype(jnp.int32)
    summed = {n: _sum_blocks(recv[n], blocks[n], me_arr, "sum_grads_" + n) for n in BIG[1:]}
    (recv["w_in"],), late, theirs = _exchange("scatter_w_in_gather_small_exchange_grads", scatter=[blocks["w_in"]],
                                              gather=[small["late"]], sibling=[summed[n] for n in BIG[1:]])
    theirs = dict(zip(BIG[1:], theirs))
    summed["w_in"] = _sum_blocks(recv["w_in"], blocks["w_in"], me_arr, "sum_grads_w_in")
    late = _with_own(late, [small["late"]], me)[0]
    gathered = _with_own(small["early_gathered"], small["early"], me)
    gathered[0] = jnp.concatenate([gathered[0][:, :16], late, gathered[0][:, 16:]], axis=1)

    sums = _sum8_many(gathered, "sum_small")
    stats = sums[0]
    loss = 0.5 * jnp.sum(stats[ROW_LOSS]) / D
    dmod_x_all = jnp.concatenate([gathered[0][:, r_, :] for r_ in ROWS_DMOD_X], axis=1)
    dmod_c_full = jnp.concatenate([stats[r_] for r_ in ROWS_DMOD_C] + [jnp.zeros((4 * D,), F32)])
    dm_rows = jnp.concatenate([dmod_x_all, dmod_c_full[None, :], jnp.zeros((7, 6 * D), F32)], axis=0)
    cs = w_ada.shape[2]
    dm_shard = lax.dynamic_slice_in_dim(dm_rows, chip * cs, cs, axis=1)
    dmc_shard = jnp.concatenate([dm_shard[8:9], jnp.zeros((7, cs), F32)], axis=0)
    ada = _ada_bwd(sc_all.T, dm_shard, dmc_shard, w_ada[0], m_w_ada[0], v_w_ada[0])
    part = ada[4]
    part = part * (mc == 0).astype(F32)
    _, part_all, (theirs["w_in"],) = _exchange("exchange_w_in_gather_c_ctx", gather=[part], sibling=[summed["w_in"]])
    dsc = _sum8(_with_own(part_all, [part], me)[0], "sum_c_ctx")

    weights = dict(c_ctx=c_ctx, w_ada=w_ada, b_ada=b_ada, w_in=w_in, attn_sink=attn_sink, gmlp_ln_g=gmlp_ln_g,
                   gmlp_ln_b=gmlp_ln_b, w_spatial=w_spatial, b_spatial=b_spatial, w_branch_a=w_branch_a,
                   w_branch_b=w_branch_b, w_out=w_out, ln1_g=ln1_g, ln1_b=ln1_b, w_ffn_in=w_ffn_in, w_ffn_out=w_ffn_out,
                   ln2_g=ln2_g, ln2_b=ln2_b)
    ms = dict(c_ctx=m_c_ctx, w_ada=m_w_ada, b_ada=m_b_ada, w_in=m_w_in, attn_sink=m_attn_sink, gmlp_ln_g=m_gmlp_ln_g,
              gmlp_ln_b=m_gmlp_ln_b, w_spatial=m_w_spatial, b_spatial=m_b_spatial, w_branch_a=m_w_branch_a,
              w_branch_b=m_w_branch_b, w_out=m_w_out, ln1_g=m_ln1_g, ln1_b=m_ln1_b, w_ffn_in=m_w_ffn_in,
              w_ffn_out=m_w_ffn_out, ln2_g=m_ln2_g, ln2_b=m_ln2_b)
    vs = dict(c_ctx=v_c_ctx, w_ada=v_w_ada, b_ada=v_b_ada, w_in=v_w_in, attn_sink=v_attn_sink, gmlp_ln_g=v_gmlp_ln_g,
              gmlp_ln_b=v_gmlp_ln_b, w_spatial=v_w_spatial, b_spatial=v_b_spatial, w_branch_a=v_w_branch_a,
              w_branch_b=v_w_branch_b, w_out=v_w_out, ln1_g=v_ln1_g, ln1_b=v_ln1_b, w_ffn_in=v_w_ffn_in,
              w_ffn_out=v_w_ffn_out, ln2_g=v_ln2_g, ln2_b=v_ln2_b)
    order = list(weights)
    grads, delta, new_m, new_v = [dict(w_ada=t[None]) for t in ada[:4]]
    c_arr = jnp.reshape(mc, (1,)).astype(jnp.int32)
    names = dict(w_in="w_in", w_a="w_branch_a", w_b="w_branch_b", w_o="w_out", w_fi="w_ffn_in", w_fo="w_ffn_out")
    for k, n in names.items():
        flip = (lambda t: t.T) if k == "w_in" else (lambda t: t)
        outs = _adamw_halves(flip(weights[n][0]), summed[k], theirs[k], flip(ms[n][0]), flip(vs[n][0]), c_arr, "adamw_" + n)
        grads[n], delta[n], new_m[n], new_v[n] = [flip(t)[None] for t in outs]

    def view(a):
        return a.reshape(-1, a.shape[-1]) if a.ndim != 1 else a.reshape(1, -1)

    small = _adamw_small(sums, dsc, *[{n: view(d[n]) for n in SMALL} for d in (weights, ms, vs)])
    for out, src in zip((grads, delta, new_m, new_v), small):
        for n in SMALL:
            out[n] = src[n].reshape(weights[n].shape)

    return (loss, grad_x[None], *[grads[n] for n in order], *[delta[n] for n in order],
            *[new_m[n] for n in order], *[new_v[n] for n in order])
```

```python
import math

import jax
import jax.numpy as jnp
from jax import lax
from jax.experimental import pallas as pl
from jax.experimental.pallas import tpu as pltpu

F32 = jnp.float32
BF16 = jnp.bfloat16

D = 1024
HEAD = 64
N_KV = 2
GROUP = 4
Q_W = 512
KV_W = 128
G_W = 512
BLK = 128
N_GRP = 8
GRP_D = 64
FH = 2816
IN_W = 3840
GRID_W = 64
ROPE_BASE = 10000.0
LN_EPS = 1e-5
NEG = -1e30
ALPHA = (2 * 1) ** 0.25
SCALE = HEAD ** -0.5
GELU_K = math.sqrt(2.0 / math.pi)
GELU_A = 0.044715
ADAM_LR = 0.001
ADAM_B1 = 0.9
ADAM_B2 = 0.999
ADAM_EPS = 1e-08
ADAM_WD = 0.01
ADAM_STEP = 10
N_DEV = 8
N_SHARD = 4
FH_SHARD = FH // 2
LANES = 128
VMEM_LIMIT = 56 * 1024 * 1024
MESH = pl.DeviceIdType.MESH


def _cp(*sem):
    return pltpu.CompilerParams(dimension_semantics=sem, vmem_limit_bytes=VMEM_LIMIT)


def _resident(shape):
    return pl.BlockSpec(shape, lambda *_: (0,) * len(shape), pipeline_mode=pl.Buffered(1))


def _rows(tm, width):
    return pl.BlockSpec((tm, width), lambda i: (i, 0))


def _acc(shape):
    return pl.BlockSpec(shape, lambda *_: (0,) * len(shape))


def _dot(a, b):
    return jnp.dot(a, b, preferred_element_type=F32)


def _dot_nt(a, b):
    return lax.dot_general(a, b, (((1,), (1,)), ((), ())), preferred_element_type=F32)


def _dot_tn(a, b):
    return lax.dot_general(a, b, (((0,), (0,)), ((), ())), preferred_element_type=F32)


def _ln(x):
    mu = jnp.mean(x, axis=-1, keepdims=True)
    xc = x - mu
    var = jnp.mean(xc * xc, axis=-1, keepdims=True)
    rstd = lax.rsqrt(var + LN_EPS)
    return xc * rstd, rstd


def _ln_bwd(dxhat, xhat, rstd):
    return (dxhat - jnp.mean(dxhat, axis=-1, keepdims=True)
            - xhat * jnp.mean(dxhat * xhat, axis=-1, keepdims=True)) * rstd


def _sig(x):
    return 0.5 + 0.5 * jnp.tanh(0.5 * x)


def _gelu(x):
    t = jnp.tanh(x * (GELU_K + (GELU_K * GELU_A) * (x * x)))
    hx = 0.5 * x
    return hx + hx * t, t


def _gelu_grad(x, t):
    return 0.5 + 0.5 * t + (0.5 * x) * (1.0 - t * t) * (GELU_K + (3.0 * GELU_K * GELU_A) * (x * x))


def _colsum(v):
    return jnp.sum(v, axis=0, keepdims=True)


def _partner(x):
    w = x.shape[1]
    lane = lax.broadcasted_iota(jnp.int32, x.shape, 1)
    return jnp.where((lane & 31) < 16, pltpu.roll(x, w - 16, 1), pltpu.roll(x, 16, 1))


def _rope(x, cos, sin):
    return x * cos + _partner(x) * sin


def _unrope(g, cos, sin):
    return g * cos + _partner(g * sin)


def _rope_tables(seq):
    inv = ROPE_BASE ** (-jnp.arange(HEAD // 4, dtype=F32) / (HEAD // 4))
    pos = jnp.arange(seq, dtype=jnp.int32)
    ar = (pos // GRID_W).astype(F32)[:, None] * inv
    ac = (pos % GRID_W).astype(F32)[:, None] * inv
    cos = jnp.concatenate([jnp.cos(ar), jnp.cos(ar), jnp.cos(ac), jnp.cos(ac)], axis=-1)
    sin = jnp.concatenate([-jnp.sin(ar), jnp.sin(ar), -jnp.sin(ac), jnp.sin(ac)], axis=-1)
    return jnp.tile(cos, (1, LANES // HEAD)), jnp.tile(sin, (1, LANES // HEAD))


def _ctx_fwd(ctx, modc, w_kv):
    n_ctx = ctx.shape[0]

    def body(ctx_ref, mod_ref, w_ref, hc_ref, kvc_ref, vac_ref):
        xhat, _ = _ln(ctx_ref[...])
        hc = (xhat * (1.0 + mod_ref[1:2, :]) + mod_ref[0:1, :]).astype(BF16)
        hc_ref[...] = hc
        kvc = _dot_nt(hc, w_ref[...]).astype(BF16)
        kvc_ref[...] = kvc
        vac_ref[...] = _with_ones(kvc[:, KV_W:])

    return pl.pallas_call(
        body, name="ctx_fwd", grid=(1,),
        in_specs=[_acc((n_ctx, D)), _acc((8, D)), _acc((2 * KV_W, D))],
        out_specs=[_acc((n_ctx, D)), _acc((n_ctx, 2 * KV_W)), _acc((n_ctx, 2 * LANES))],
        out_shape=[jax.ShapeDtypeStruct((n_ctx, D), BF16), jax.ShapeDtypeStruct((n_ctx, 2 * KV_W), BF16),
                   jax.ShapeDtypeStruct((n_ctx, 2 * LANES), BF16)],
        compiler_params=_cp("arbitrary"),
    )(ctx, modc, w_kv)


def _host_start(step, comm):
    if comm is not None:
        @pl.when(step == 0)
        def _():
            comm.start()


def _host_finish(step, last, comm, forward_at=None):
    if comm is None:
        return
    if forward_at is None or forward_at >= last:
        @pl.when(step == last)
        def _():
            comm.finish()
    else:
        @pl.when(step == forward_at)
        def _():
            comm.forward()

        @pl.when(step == last)
        def _():
            comm.drain()


def _proj_fwd(x, modx, w_in, cos, sin, tm, gather=()):
    seq = x.shape[0]
    ng = len(gather)

    def body(x_ref, mod_ref, w_ref, cos_ref, sin_ref, *rest):
        h_ref, q_ref, kv_ref, va_ref, uv_ref, gab_ref = rest[ng:ng + 6]
        comm = _Gather(rest[:ng], rest[ng + 6:2 * ng + 6], *rest[2 * ng + 6:]) if ng else None
        _host_start(pl.program_id(0), comm)
        xhat, _ = _ln(x_ref[...])
        h = (xhat * (1.0 + mod_ref[1:2, :]) + mod_ref[0:1, :]).astype(BF16)
        h_ref[...] = h
        cos1, sin1 = cos_ref[...], sin_ref[...]
        cos2 = jnp.concatenate([cos1, cos1], axis=1)
        sin2 = jnp.concatenate([sin1, sin1], axis=1)
        for j in range(Q_W // 256):
            t = _dot_nt(h, w_ref[256 * j:256 * (j + 1), :])
            q_ref[:, 256 * j:256 * (j + 1)] = (_rope(t, cos2, sin2) * SCALE).astype(BF16)
        t = _dot_nt(h, w_ref[Q_W:Q_W + 2 * KV_W, :])
        kv_ref[:, :KV_W] = _rope(t[:, :KV_W], cos1, sin1).astype(BF16)
        v = t[:, KV_W:].astype(BF16)
        kv_ref[:, KV_W:] = v
        va_ref[...] = _with_ones(v)
        o = Q_W + 2 * KV_W
        for j in range(2):
            uv_ref[:, G_W * j:G_W * (j + 1)] = _dot_nt(h, w_ref[o + G_W * j:o + G_W * (j + 1), :]).astype(BF16)
        o += 2 * G_W
        for j in range(4):
            gab_ref[:, 512 * j:512 * (j + 1)] = _dot_nt(h, w_ref[o + 512 * j:o + 512 * (j + 1), :]).astype(BF16)
        _host_finish(pl.program_id(0), seq // tm - 1, comm, forward_at=(seq // tm) // 2)

    out = pl.pallas_call(
        body, name="proj_fwd", grid=(seq // tm,),
        in_specs=[_rows(tm, D), _acc((8, D)), _resident((IN_W, D)), _rows(tm, LANES), _rows(tm, LANES)] + _comm_specs(ng),
        out_specs=[_rows(tm, D), _rows(tm, Q_W), _rows(tm, 2 * KV_W), _rows(tm, 2 * LANES), _rows(tm, 2 * G_W),
                   _rows(tm, 2 * D)] + _comm_specs(ng),
        out_shape=[jax.ShapeDtypeStruct((seq, D), BF16), jax.ShapeDtypeStruct((seq, Q_W), BF16),
                   jax.ShapeDtypeStruct((seq, 2 * KV_W), BF16), jax.ShapeDtypeStruct((seq, 2 * LANES), BF16),
                   jax.ShapeDtypeStruct((seq, 2 * G_W), BF16), jax.ShapeDtypeStruct((seq, 2 * D), BF16)] + _gathered_shapes(gather),
        scratch_shapes=_comm_scratch(ng) if ng else [],
        compiler_params=_cp("arbitrary"),
    )(x, modx, w_in, cos, sin, *gather)
    return out[:6], out[6:]


def _stack_heads(x, hk):
    return jnp.concatenate([x[:, (hk * GROUP + g) * HEAD:(hk * GROUP + g + 1) * HEAD] for g in range(GROUP)], axis=0)


def _band_masks(n, nb):
    rows = GROUP * BLK
    qi = lax.broadcasted_iota(jnp.int32, (rows, BLK), 0) & (BLK - 1)
    kj = lax.broadcasted_iota(jnp.int32, (rows, BLK), 1)
    return (kj >= qi) & (n > 0), (kj <= qi) & (n < nb - 1)


def _attn_scores(q, k_refs, hk, masks):
    q4 = _stack_heads(q, hk)
    ks = [r[:, hk * HEAD:(hk + 1) * HEAD] for r in k_refs]
    s = [_dot_nt(q4, k) for k in ks]
    s[1] = jnp.where(masks[0], s[1], NEG)
    s[3] = jnp.where(masks[1], s[3], NEG)
    return q4, ks, s


def _sink_rows(sink_ref, hk):
    rows = GROUP * BLK
    rg = lax.broadcasted_iota(jnp.int32, (rows, 1), 0) >> 7
    sink_v = jnp.full((rows, 1), sink_ref[0, hk * GROUP], F32)
    for g in range(1, GROUP):
        sink_v = jnp.where(rg == g, sink_ref[0, hk * GROUP + g], sink_v)
    return sink_v


def _with_ones(v):
    ones = jnp.ones((v.shape[0], HEAD), v.dtype)
    return jnp.concatenate([v[:, :HEAD], ones, v[:, HEAD:], ones], axis=1)


def _kv_specs(nb, qb):
    def spec(d):
        return pl.BlockSpec((BLK, 2 * KV_W), lambda n: (jnp.clip(qb * n + d, 0, nb - 1), 0))
    return [spec(d) for d in range(-1, qb + 1)]


def _attn_fwd(q, kv, va, kvc, vac, sink, gather=()):
    seq = q.shape[0]
    nb = seq // BLK
    n_ctx = kvc.shape[0]
    ng = len(gather)
    Q_BLOCKS = 1
    nkv = Q_BLOCKS + 2
    steps = nb // Q_BLOCKS

    def body(q_ref, *rest):
        kv_refs, va_refs = rest[:nkv], rest[nkv:2 * nkv]
        kvc_ref, vac_ref, sink_ref = rest[2 * nkv:2 * nkv + 3]
        rest = rest[2 * nkv + 3:]
        o_ref, lse_ref = rest[ng:ng + 2]
        comm = _Gather(rest[:ng], rest[ng + 2:2 * ng + 2], *rest[2 * ng + 2:]) if ng else None
        n = pl.program_id(0)
        _host_start(n, comm)
        lane = lax.broadcasted_iota(jnp.int32, (BLK, LANES), 1)
        for sub in range(Q_BLOCKS):
            rs = slice(sub * BLK, (sub + 1) * BLK)
            q = q_ref[rs, :]
            outs = []
            lse_all = jnp.zeros((BLK, LANES), F32)
            masks = _band_masks(Q_BLOCKS * n + sub, nb)
            for hk in range(N_KV):
                _, _, s = _attn_scores(q, (kvc_ref,) + kv_refs[sub:sub + 3], hk, masks)
                sink_v = _sink_rows(sink_ref, hk)
                tile_max = s[1]
                for t in [s[0][:, i * LANES:(i + 1) * LANES] for i in range(n_ctx // LANES)] + s[2:]:
                    tile_max = jnp.maximum(tile_max, t)
                m = jnp.maximum(sink_v, jnp.max(tile_max, axis=-1, keepdims=True))
                o = jnp.zeros((GROUP * BLK, LANES), F32)
                for t, va_ref in zip(s, (vac_ref,) + va_refs[sub:sub + 3]):
                    o = o + _dot(jnp.exp((t - m).astype(BF16)), va_ref[:, hk * LANES:(hk + 1) * LANES])
                denom = o[:, HEAD:HEAD + 1] + jnp.exp(sink_v - m)
                o4 = o[:, :HEAD] * (1.0 / denom)
                lse4 = m + jnp.log(denom)
                for g in range(GROUP):
                    outs.append(o4[g * BLK:(g + 1) * BLK, :])
                    lse_all = jnp.where(lane == hk * GROUP + g, lse4[g * BLK:(g + 1) * BLK, :], lse_all)
            o_ref[rs, :] = jnp.concatenate(outs, axis=1).astype(BF16)
            lse_ref[rs, :] = lse_all
        _host_finish(n, steps - 1, comm, forward_at=(3 * steps) // 4)

    tq = Q_BLOCKS * BLK
    out = pl.pallas_call(
        body, name="attn_fwd", grid=(steps,),
        in_specs=[_rows(tq, Q_W)] + _kv_specs(nb, Q_BLOCKS) + _kv_specs(nb, Q_BLOCKS)
        + [_acc((n_ctx, 2 * KV_W)), _acc((n_ctx, 2 * LANES)), pl.BlockSpec(memory_space=pltpu.SMEM)] + _comm_specs(ng),
        out_specs=[_rows(tq, Q_W), _rows(tq, LANES)] + _comm_specs(ng),
        out_shape=[jax.ShapeDtypeStruct((seq, Q_W), BF16), jax.ShapeDtypeStruct((seq, LANES), F32)] + _gathered_shapes(gather),
        scratch_shapes=_comm_scratch(ng) if ng else [],
        compiler_params=_cp("arbitrary"),
    )(q, *([kv] * nkv), *([va] * nkv), kvc, vac, sink, *gather)
    return out[:2], out[2:]


def _gmlp_chunk(u, vb, gp_ref, ws_ref, bias_ref):
    gu, tu = _gelu(u)
    gv, tv = _gelu(vb)
    vhat, rstd = _ln(gv)
    vn = (vhat * gp_ref[0:1, :] + gp_ref[1:2, :]).astype(BF16)
    s = bias_ref[...] + jnp.concatenate(
        [_dot(ws_ref[g * BLK:(g + 1) * BLK, :], vn[:, g * GRP_D:(g + 1) * GRP_D]) for g in range(N_GRP)], axis=1)
    return gu, tu, tv, vhat, rstd, vn, s


def _mix_fwd(uv, gab, ya, gp, ws_stack, bias_full, w_a, w_b, w_o, tm, gather=()):
    seq = uv.shape[0]
    ng = len(gather)
    steps = seq // tm

    def body(uv_ref, gab_ref, ya_ref, gp_ref, ws_ref, bias_ref, wa_ref, wb_ref, wo_ref, *rest):
        a_ref, b_ref, mix_ref, merged_ref, yb_ref = rest[ng:ng + 5]
        comm = _Gather(rest[:ng], rest[ng + 5:2 * ng + 5], *rest[2 * ng + 5:]) if ng else None
        _host_start(pl.program_id(0), comm)
        for c in range(tm // BLK):
            rs = slice(c * BLK, (c + 1) * BLK)
            gu, _, _, _, _, _, s = _gmlp_chunk(uv_ref[rs, :G_W].astype(F32), uv_ref[rs, G_W:].astype(F32), gp_ref, ws_ref, bias_ref)
            yb_ref[rs, :] = (gu * s).astype(BF16)
        a = _dot(ya_ref[...], wa_ref[...])
        b = _dot(yb_ref[...], wb_ref[...])
        a_ref[...] = a.astype(BF16)
        b_ref[...] = b.astype(BF16)
        merged = (_sig(gab_ref[:, :D].astype(F32)) * a + _sig(gab_ref[:, D:].astype(F32)) * b).astype(BF16)
        merged_ref[...] = merged
        mix_ref[...] = _dot(merged, wo_ref[...])
        _host_finish(pl.program_id(0), steps - 1, comm, forward_at=(3 * steps) // 4)

    out = pl.pallas_call(
        body, name="mix_fwd", grid=(steps,),
        in_specs=[_rows(tm, 2 * G_W), _rows(tm, 2 * D), _rows(tm, Q_W), _acc((8, G_W)),
                  _resident((N_GRP * BLK, BLK)), _acc((BLK, G_W)),
                  _resident((Q_W, D)), _resident((G_W, D)), _resident((D, D))] + _comm_specs(ng),
        out_specs=[_rows(tm, D), _rows(tm, D), _rows(tm, D), _rows(tm, D), _rows(tm, G_W)] + _comm_specs(ng),
        out_shape=[jax.ShapeDtypeStruct((seq, D), BF16), jax.ShapeDtypeStruct((seq, D), BF16),
                   jax.ShapeDtypeStruct((seq, D), F32), jax.ShapeDtypeStruct((seq, D), BF16),
                   jax.ShapeDtypeStruct((seq, G_W), BF16)] + _gathered_shapes(gather),
        scratch_shapes=_comm_scratch(ng) if ng else [],
        compiler_params=_cp("arbitrary"),
    )(uv, gab, ya, gp, ws_stack, bias_full, w_a, w_b, w_o, *gather)
    return out[:5], out[5:]


def _mid_recompute(x_ref, mix_ref, vec_ref):
    r1 = ALPHA * x_ref[...] + vec_ref[0:1, :] * mix_ref[...]
    xh1, rstd1 = _ln(r1)
    xmid = xh1 * vec_ref[1:2, :] + vec_ref[2:3, :]
    xh2, rstd2 = _ln(xmid)
    return xh1, rstd1, xmid, xh2, rstd2


def _ffn(x, mix, tgt, vec, w_fi, w_fo, tm):
    seq = x.shape[0]

    def body(x_ref, mix_ref, tgt_ref, vec_ref, wi_ref, wo_ref, act_ref, h2_ref, dff_ref, df_ref, dr1_ref, st_ref, gu_ref):
        @pl.when(pl.program_id(0) == 0)
        def _():
            st_ref[...] = jnp.zeros_like(st_ref)

        xh1, rstd1, xmid, xh2, rstd2 = _mid_recompute(x_ref, mix_ref, vec_ref)
        h2 = (xh2 * (1.0 + vec_ref[4:5, :]) + vec_ref[3:4, :]).astype(BF16)
        h2_ref[...] = h2
        halves = [(slice(hh * FH_SHARD, (hh + 1) * FH_SHARD), slice(FH + hh * FH_SHARD, FH + (hh + 1) * FH_SHARD))
                  for hh in range(2)]
        for hh, (cs, cu) in enumerate(halves):
            g = _dot(h2, wi_ref[hh])
            u = _dot(h2, wi_ref[2 + hh])
            gu_ref[:, cs] = g
            gu_ref[:, cu] = u
            act_ref[:, cs] = (g * _sig(g) * u).astype(BF16)
        f = _dot(act_ref[...], wo_ref[...])
        r2 = ALPHA * xmid + vec_ref[5:6, :] * f
        yh, rstd = _ln(r2)
        y = yh * vec_ref[6:7, :] + vec_ref[7:8, :]
        err = y - tgt_ref[...]
        dy = err / D
        dr2 = _ln_bwd(dy * vec_ref[6:7, :], yh, rstd)
        st_ref[0:1, :] += _colsum(err * err)
        st_ref[1:2, :] += _colsum(dy * yh)
        st_ref[2:3, :] += _colsum(dy)
        st_ref[3:4, :] += _colsum(dr2 * f)

        df = (dr2 * vec_ref[5:6, :]).astype(BF16)
        df_ref[...] = df
        da_all = _dot_nt(df, wo_ref[...])
        for cs, cu in halves:
            da = da_all[:, cs]
            g = gu_ref[:, cs]
            u = gu_ref[:, cu]
            sg = _sig(g)
            dff_ref[:, cs] = (da * u * sg * (1.0 + g * (1.0 - sg))).astype(BF16)
            dff_ref[:, cu] = (da * g * sg).astype(BF16)
        dh2 = _dot_nt(dff_ref[:, :FH_SHARD], wi_ref[0])
        for s in range(1, N_SHARD):
            dh2 = dh2 + _dot_nt(dff_ref[:, s * FH_SHARD:(s + 1) * FH_SHARD], wi_ref[s])
        dxmid = _ln_bwd(dh2 * (1.0 + vec_ref[4:5, :]), xh2, rstd2) + ALPHA * dr2
        dr1 = _ln_bwd(dxmid * vec_ref[1:2, :], xh1, rstd1)
        dr1_ref[...] = dr1
        st_ref[8:9, :] += _colsum(dh2 * xh2)
        st_ref[9:10, :] += _colsum(dh2)
        st_ref[10:11, :] += _colsum(dxmid * xh1)
        st_ref[11:12, :] += _colsum(dxmid)
        st_ref[12:13, :] += _colsum(dr1 * mix_ref[...])

    return pl.pallas_call(
        body, name="ffn", grid=(seq // tm,),
        in_specs=[_rows(tm, D), _rows(tm, D), _rows(tm, D), _acc((8, D)), _resident((N_SHARD, D, FH_SHARD)), _resident((FH, D))],
        out_specs=[_rows(tm, FH), _rows(tm, D), _rows(tm, 2 * FH), _rows(tm, D), _rows(tm, D), _acc((16, D))],
        out_shape=[jax.ShapeDtypeStruct((seq, FH), BF16), jax.ShapeDtypeStruct((seq, D), BF16),
                   jax.ShapeDtypeStruct((seq, 2 * FH), BF16), jax.ShapeDtypeStruct((seq, D), BF16),
                   jax.ShapeDtypeStruct((seq, D), F32), jax.ShapeDtypeStruct((16, D), F32)],
        scratch_shapes=[pltpu.VMEM((tm, 2 * FH), F32)],
        compiler_params=_cp("arbitrary"),
    )(x, mix, tgt, vec, w_fi, w_fo)


def _mix_bwd(dr1, a, b, gab, uv, merged, ya, yb, vec, gp, ws_stack, ws_stack_t, bias_full, w_a, w_b, w_o, tm, scatter=()):
    seq = dr1.shape[0]
    last = seq // tm - 1
    ns = len(scatter)

    def body(dr1_ref, a_ref, b_ref, gab_ref, uv_ref, mg_ref, ya_ref, yb_ref, vec_ref, gp_ref, ws_ref, wst_ref, bias_ref,
             wa_ref, wb_ref, wo_ref, *rest):
        dya_ref, dp_ref, dws_ref, dbs_ref, st_ref, gwo_ref, gwa_ref, gwb_ref = rest[ns:ns + 8]
        acc_o, acc_a, acc_b = rest[2 * ns + 8:2 * ns + 11]
        comm = _AllToAll(rest[:ns], rest[ns + 8:2 * ns + 8], *rest[2 * ns + 11:]) if ns else None
        _host_start(pl.program_id(0), comm)

        @pl.when(pl.program_id(0) == 0)
        def _():
            dws_ref[...] = jnp.zeros_like(dws_ref)
            dbs_ref[...] = jnp.zeros_like(dbs_ref)
            st_ref[...] = jnp.zeros_like(st_ref)
            acc_o[...] = jnp.zeros_like(acc_o)
            acc_a[...] = jnp.zeros_like(acc_a)
            acc_b[...] = jnp.zeros_like(acc_b)

        dmix = (dr1_ref[...] * vec_ref[0:1, :]).astype(BF16)
        acc_o[...] += _dot_tn(mg_ref[...], dmix)
        dmerged = _dot_nt(dmix, wo_ref[...])
        sa = _sig(gab_ref[:, :D].astype(F32))
        sb = _sig(gab_ref[:, D:].astype(F32))
        da_f = dmerged * sa
        db_f = dmerged * sb
        da = da_f.astype(BF16)
        db = db_f.astype(BF16)
        dp_ref[:, 2 * G_W:2 * G_W + D] = (da_f * a_ref[...].astype(F32) * (1.0 - sa)).astype(BF16)
        dp_ref[:, 2 * G_W + D:] = (db_f * b_ref[...].astype(F32) * (1.0 - sb)).astype(BF16)
        dya_ref[...] = _dot_nt(da, wa_ref[...]).astype(BF16)
        dyb = _dot_nt(db, wb_ref[...])
        acc_a[...] += _dot_tn(ya_ref[...], da)
        acc_b[...] += _dot_tn(yb_ref[...], db)

        @pl.when(pl.program_id(0) == last)
        def _():
            gwo_ref[...] = acc_o[...].astype(BF16)
            gwa_ref[...] = acc_a[...].astype(BF16)
            gwb_ref[...] = acc_b[...].astype(BF16)

        for c in range(tm // BLK):
            rs = slice(c * BLK, (c + 1) * BLK)
            u = uv_ref[rs, :G_W].astype(F32)
            vb = uv_ref[rs, G_W:].astype(F32)
            gu, tu, tv, vhat, rstd, vn, s = _gmlp_chunk(u, vb, gp_ref, ws_ref, bias_ref)
            dyb_c = dyb[rs, :]
            ds = dyb_c * gu
            du = dyb_c * s * _gelu_grad(u, tu)
            ds_b = ds.astype(BF16)
            dvn_g = []
            for g in range(N_GRP):
                cg = slice(g * GRP_D, (g + 1) * GRP_D)
                dvn_g.append(_dot(wst_ref[:, g * BLK:(g + 1) * BLK], ds_b[:, cg]))
                dws_ref[g * BLK:(g + 1) * BLK, :] += _dot_nt(ds_b[:, cg], vn[:, cg])
            dvn = jnp.concatenate(dvn_g, axis=1)
            dbs_ref[...] += ds
            st_ref[0:1, :] += _colsum(dvn * vhat)
            st_ref[1:2, :] += _colsum(dvn)
            dgv = _ln_bwd(dvn * gp_ref[0:1, :], vhat, rstd)
            dvb = dgv * _gelu_grad(vb, tv)
            dp_ref[rs, :G_W] = du.astype(BF16)
            dp_ref[rs, G_W:2 * G_W] = dvb.astype(BF16)
        _host_finish(pl.program_id(0), last, comm)

    pw = 2 * G_W + 2 * D
    out = pl.pallas_call(
        body, name="mix_bwd", grid=(seq // tm,),
        in_specs=[_rows(tm, D), _rows(tm, D), _rows(tm, D), _rows(tm, 2 * D), _rows(tm, 2 * G_W), _rows(tm, D), _rows(tm, Q_W),
                  _rows(tm, G_W), _acc((8, D)), _acc((8, G_W)),
                  _resident((N_GRP * BLK, BLK)), _resident((BLK, N_GRP * BLK)), _acc((BLK, G_W)),
                  _resident((Q_W, D)), _resident((G_W, D)), _resident((D, D))] + _comm_specs(ns),
        out_specs=[_rows(tm, Q_W), _rows(tm, pw), _acc((N_GRP * BLK, BLK)), _acc((BLK, G_W)), _acc((8, G_W)),
                   _acc((D, D)), _acc((Q_W, D)), _acc((G_W, D))] + _comm_specs(ns),
        out_shape=[jax.ShapeDtypeStruct((seq, Q_W), BF16), jax.ShapeDtypeStruct((seq, pw), BF16),
                   jax.ShapeDtypeStruct((N_GRP * BLK, BLK), F32), jax.ShapeDtypeStruct((BLK, G_W), F32),
                   jax.ShapeDtypeStruct((8, G_W), F32), jax.ShapeDtypeStruct((D, D), BF16),
                   jax.ShapeDtypeStruct((Q_W, D), BF16), jax.ShapeDtypeStruct((G_W, D), BF16)]
        + [jax.ShapeDtypeStruct(v.shape, v.dtype) for v in scatter],
        scratch_shapes=[pltpu.VMEM((D, D), F32), pltpu.VMEM((Q_W, D), F32), pltpu.VMEM((G_W, D), F32)]
        + (_comm_scratch(ns) if ns else []),
        compiler_params=_cp("arbitrary"),
    )(dr1, a, b, gab, uv, merged, ya, yb, vec, gp, ws_stack, ws_stack_t, bias_full, w_a, w_b, w_o, *scatter)
    return out[:8], out[8:]


def _attn_bwd(q, kv, kvc, sink, dya, ya, lse, scatter=()):
    seq = q.shape[0]
    nb = seq // BLK
    n_ctx = kvc.shape[0]
    ns = len(scatter)
    Q_BLOCKS = 2
    nkv = Q_BLOCKS + 2
    steps = nb // Q_BLOCKS

    def body(q_ref, *rest):
        kv_refs = rest[:nkv]
        kvc_ref, sink_ref, do_ref, o_ref, lse_ref = rest[nkv:nkv + 5]
        rest = rest[nkv + 5:]
        dq_ref, dkv_ref, dkvc_ref, dsink_ref = rest[ns:ns + 4]
        comm = _AllToAll(rest[:ns], rest[ns + 4:2 * ns + 4], *rest[2 * ns + 4:]) if ns else None
        n = pl.program_id(0)
        _host_start(n, comm)

        @pl.when(n == 0)
        def _():
            dkv_ref[...] = jnp.zeros_like(dkv_ref)
            dkvc_ref[...] = jnp.zeros_like(dkvc_ref)
            dsink_ref[...] = jnp.zeros_like(dsink_ref)

        lane = lax.broadcasted_iota(jnp.int32, (1, LANES), 1)
        for sub in range(Q_BLOCKS):
            rs = slice(sub * BLK, (sub + 1) * BLK)
            blk = Q_BLOCKS * n + sub
            q = q_ref[rs, :]
            do = do_ref[rs, :]
            out = o_ref[rs, :]
            lse_all = lse_ref[rs, :]
            k_refs = (kvc_ref,) + kv_refs[sub:sub + 3]
            masks = _band_masks(blk, nb)
            dqs, dks, dvs = [], [], []
            for hk in range(N_KV):
                q4, ks, s = _attn_scores(q, k_refs, hk, masks)
                vs = [r[:, KV_W + hk * HEAD:KV_W + (hk + 1) * HEAD] for r in k_refs]
                lse4 = jnp.concatenate([lse_all[:, hk * GROUP + g:hk * GROUP + g + 1] for g in range(GROUP)], axis=0)
                do4 = _stack_heads(do, hk)
                delta = jnp.sum(do4.astype(F32) * _stack_heads(out, hk).astype(F32), axis=-1, keepdims=True)
                p = [jnp.exp((t - lse4).astype(BF16)) for t in s]
                ds = [t * (_dot_nt(do4, v) - delta).astype(BF16) for t, v in zip(p, vs)]
                dq4 = _dot(ds[0], ks[0])
                for t, k in zip(ds[1:], ks[1:]):
                    dq4 = dq4 + _dot(t, k)
                dq4 = dq4 * SCALE
                dqs += [dq4[g * BLK:(g + 1) * BLK, :] for g in range(GROUP)]
                dks.append([_dot_tn(t, q4) for t in ds])
                dvs.append([_dot_tn(t, do4) for t in p])
                ps = jnp.exp(_sink_rows(sink_ref, hk) - lse4) * delta
                for g in range(GROUP):
                    part = -jnp.sum(ps[g * BLK:(g + 1) * BLK, :], axis=0, keepdims=True)
                    dsink_ref[0:1, :] += jnp.where(lane == hk * GROUP + g, part, 0.0)
            dq_ref[rs, :] = jnp.concatenate(dqs, axis=1)

            def piece(i):
                return jnp.concatenate([dks[0][i], dks[1][i], dvs[0][i], dvs[1][i]], axis=1)

            dkvc_ref[...] += piece(0)
            starts = (jnp.maximum(blk - 1, 0), blk, jnp.minimum(blk + 1, nb - 1))
            for i, st in enumerate(starts):
                r = pl.ds(pl.multiple_of(st * BLK, BLK), BLK)
                dkv_ref[r, :] += piece(i + 1)
        _host_finish(n, steps - 1, comm)

    tq = Q_BLOCKS * BLK
    out = pl.pallas_call(
        body, name="attn_bwd", grid=(steps,),
        in_specs=[_rows(tq, Q_W)] + _kv_specs(nb, Q_BLOCKS) + [_acc((n_ctx, 2 * KV_W)), pl.BlockSpec(memory_space=pltpu.SMEM),
                                                     _rows(tq, Q_W), _rows(tq, Q_W), _rows(tq, LANES)] + _comm_specs(ns),
        out_specs=[_rows(tq, Q_W), _acc((seq, 2 * KV_W)), _acc((n_ctx, 2 * KV_W)), _acc((8, LANES))] + _comm_specs(ns),
        out_shape=[jax.ShapeDtypeStruct((seq, Q_W), F32), jax.ShapeDtypeStruct((seq, 2 * KV_W), F32),
                   jax.ShapeDtypeStruct((n_ctx, 2 * KV_W), F32), jax.ShapeDtypeStruct((8, LANES), F32)]
        + [jax.ShapeDtypeStruct(v.shape, v.dtype) for v in scatter],
        scratch_shapes=_comm_scratch(ns) if ns else [],
        compiler_params=_cp("arbitrary"),
    )(q, *([kv] * nkv), kvc, sink, dya, ya, lse, *scatter)
    return out[:4], out[4:]


def _proj_bwd(dq, dkv, dpb, x, dr1, modx, w_in, cos, sin, tm, scatter=()):
    seq = x.shape[0]
    pw = IN_W - Q_W - 2 * KV_W
    ns = len(scatter)

    def body(dq_ref, dkv_ref, dpb_ref, x_ref, dr1_ref, mod_ref, w_ref, cos_ref, sin_ref, *rest):
        dqkv_ref, gx_ref, st_ref = rest[ns:ns + 3]
        comm = _AllToAll(rest[:ns], rest[ns + 3:2 * ns + 3], *rest[2 * ns + 3:]) if ns else None
        _host_start(pl.program_id(0), comm)

        @pl.when(pl.program_id(0) == 0)
        def _():
            st_ref[...] = jnp.zeros_like(st_ref)

        cos1, sin1 = cos_ref[...], sin_ref[...]
        cos2 = jnp.concatenate([cos1, cos1], axis=1)
        sin2 = jnp.concatenate([sin1, sin1], axis=1)
        for j in range(Q_W // 256):
            cs = slice(256 * j, 256 * (j + 1))
            dqkv_ref[:, cs] = _unrope(dq_ref[:, cs], cos2, sin2).astype(BF16)
        dqkv_ref[:, Q_W:Q_W + KV_W] = _unrope(dkv_ref[:, :KV_W], cos1, sin1).astype(BF16)
        dqkv_ref[:, Q_W + KV_W:] = dkv_ref[:, KV_W:].astype(BF16)
        o = Q_W + 2 * KV_W
        dh = _dot(dqkv_ref[...], w_ref[:o, :]) + _dot(dpb_ref[...], w_ref[o:, :])
        xhat, rstd = _ln(x_ref[...])
        st_ref[0:1, :] += _colsum(dh)
        st_ref[1:2, :] += _colsum(dh * xhat)
        gx_ref[...] = _ln_bwd(dh * (1.0 + mod_ref[1:2, :]), xhat, rstd) + ALPHA * dr1_ref[...]
        _host_finish(pl.program_id(0), seq // tm - 1, comm)

    out = pl.pallas_call(
        body, name="proj_bwd", grid=(seq // tm,),
        in_specs=[_rows(tm, Q_W), _rows(tm, 2 * KV_W), _rows(tm, pw), _rows(tm, D), _rows(tm, D), _acc((8, D)),
                  _resident((IN_W, D)), _rows(tm, LANES), _rows(tm, LANES)] + _comm_specs(ns),
        out_specs=[_rows(tm, Q_W + 2 * KV_W), _rows(tm, D), _acc((8, D))] + _comm_specs(ns),
        out_shape=[jax.ShapeDtypeStruct((seq, Q_W + 2 * KV_W), BF16), jax.ShapeDtypeStruct((seq, D), F32),
                   jax.ShapeDtypeStruct((8, D), F32)] + [jax.ShapeDtypeStruct(v.shape, v.dtype) for v in scatter],
        scratch_shapes=_comm_scratch(ns) if ns else [],
        compiler_params=_cp("arbitrary"),
    )(dq, dkv, dpb, x, dr1, modx, w_in, cos, sin, *scatter)
    return out[:3], out[3:]


def _ctx_bwd(dkvc, ctx, hc, w_kv):
    n_ctx = ctx.shape[0]

    def body(dkvc_ref, ctx_ref, hc_ref, w_ref, dw_ref, st_ref):
        d = dkvc_ref[...].astype(BF16)
        dw_ref[...] = _dot_tn(d, hc_ref[...])
        dhc = _dot(d, w_ref[...])
        xhat, _ = _ln(ctx_ref[...])
        st_ref[...] = jnp.zeros_like(st_ref)
        st_ref[0:1, :] = _colsum(dhc)
        st_ref[1:2, :] = _colsum(dhc * xhat)

    return pl.pallas_call(
        body, name="ctx_bwd", grid=(1,),
        in_specs=[_acc((n_ctx, 2 * KV_W)), _acc((n_ctx, D)), _acc((n_ctx, D)), _acc((2 * KV_W, D))],
        out_specs=[_acc((2 * KV_W, D)), _acc((8, D))],
        out_shape=[jax.ShapeDtypeStruct((2 * KV_W, D), F32), jax.ShapeDtypeStruct((8, D), F32)],
        compiler_params=_cp("arbitrary"),
    )(dkvc, ctx, hc, w_kv)


def _tn_matmul(a, b, tn, name, out_dtype, shard_major=False, init=None, tk=512, out_rows=None, into=None,
               scatter=(), gather=()):
    t, ka = a.shape
    n = b.shape[1]
    tk = min(tk, t)
    nk = t // tk
    nj = n // tn
    has_init = init is not None
    in_place = into is not None
    assert not (scatter and gather) and not (in_place and (nj != 1 or shard_major or out_rows))
    moved = list(scatter) + list(gather)
    pattern = _AllToAll if scatter else _Gather
    ns = len(moved)
    n_in = 2 + has_init + in_place
    n_scr = 3 if in_place else 1

    def body(*refs):
        a_ref, b_ref = refs[:2]
        i_ref = refs[2] if has_init else None
        rest = refs[n_in:]
        o_ref = rest[ns]
        acc_ref = rest[2 * ns + 1]
        comm = pattern(rest[:ns], rest[ns + 1:2 * ns + 1], *rest[2 * ns + 1 + n_scr:]) if ns else None
        k = pl.program_id(1)
        step = pl.program_id(0) * nk + k
        _host_start(step, comm)

        @pl.when(k == 0)
        def _():
            acc_ref[...] = jnp.zeros_like(acc_ref)
            if has_init:
                acc_ref[ka - init.shape[0]:, :] = i_ref[...]

        acc_ref[...] += _dot_tn(a_ref[...], b_ref[...])

        @pl.when(k == nk - 1)
        def _():
            if in_place:
                stage_ref, sem = rest[2 * ns + 2:2 * ns + 4]
                stage_ref[...] = acc_ref[...].astype(out_dtype)
                write = pltpu.make_async_copy(stage_ref, o_ref.at[pl.ds(into[1], ka), :], sem.at[0])
                write.start()
                write.wait()
            else:
                o_ref[...] = acc_ref[...].astype(out_dtype)

        _host_finish(step, nj * nk - 1, comm, forward_at=(3 * nj * nk) // 4 if gather else None)

    in_specs = [pl.BlockSpec((tk, ka), lambda j, k: (k, 0)), pl.BlockSpec((tk, tn), lambda j, k: (k, j))]
    args = [a, b]
    if has_init:
        in_specs.append(pl.BlockSpec((init.shape[0], tn), lambda j, k: (0, j)))
        args.append(init)
    scratch = [pltpu.VMEM((ka, tn), F32)]
    aliases = {}
    if in_place:
        in_specs.append(pl.BlockSpec(memory_space=pl.ANY))
        args.append(into[0])
        aliases = {n_in - 1: 0}
        out_spec = pl.BlockSpec(memory_space=pl.ANY)
        out_shape = jax.ShapeDtypeStruct(into[0].shape, into[0].dtype)
        scratch += [pltpu.VMEM((ka, tn), out_dtype), pltpu.SemaphoreType.DMA((1,))]
    elif shard_major:
        out_spec = pl.BlockSpec((None, ka, tn), lambda j, k: (j, 0, 0))
        out_shape = jax.ShapeDtypeStruct((nj, ka, tn), out_dtype)
    else:
        out_spec = pl.BlockSpec((ka, tn), lambda j, k: (0, j))
        out_shape = jax.ShapeDtypeStruct((out_rows or ka, n), out_dtype)
    out = pl.pallas_call(
        body, name=name, grid=(nj, nk), in_specs=in_specs + _comm_specs(ns), out_specs=[out_spec] + _comm_specs(ns),
        out_shape=[out_shape] + [jax.ShapeDtypeStruct(v.shape, v.dtype) for v in scatter] + _gathered_shapes(gather),
        scratch_shapes=scratch + (_comm_scratch(ns) if ns else []), input_output_aliases=aliases,
        compiler_params=_cp("arbitrary", "arbitrary"),
    )(*args, *moved)
    return (out[0], out[1:]) if ns else out[0]


ADA_TILE = 512


def _gather_and_modulate(c_rows, c_ctx, w_half, w_ada):
    cs = w_ada.shape[1]
    vmem = pl.BlockSpec(memory_space=pltpu.VMEM)

    def body(c_ref, cctx_ref, wh_ref, wada_ref, cg_ref, wg_ref, mod_ref, modg_ref, sc_ref,
             c_v, wada_v, mod_v, send_c, recv_c, send_w, recv_w, send_m, recv_m, local):
        gc = _Gather([c_ref], [cg_ref], send_c, recv_c)
        gw = _Gather([wh_ref], [wg_ref], send_w, recv_w)
        gm = _Gather([mod_ref], [modg_ref], send_m, recv_m)
        px, py, pc = _my_pos()
        mine = 4 * px + 2 * py + pc
        gc.start()
        gw.start()
        own_c = pltpu.make_async_copy(c_ref, cg_ref.at[mine], local.at[0])
        load_w = pltpu.make_async_copy(wada_ref, wada_v, local.at[1])
        own_c.start()
        load_w.start()
        gc.finish()
        own_c.wait()
        load_c = pltpu.make_async_copy(cg_ref, c_v, local.at[2])
        load_c.start()
        load_c.wait()
        cc = jnp.concatenate([c_v[i, 0:1, :] for i in range(N_DEV)] + [cctx_ref[...], jnp.zeros((7, D), F32)], axis=0)
        sc = cc * _sig(cc)
        sc_ref[...] = sc
        load_w.wait()
        for j in range(cs // ADA_TILE):
            cols = slice(j * ADA_TILE, (j + 1) * ADA_TILE)
            mod_v[:, cols] = _dot(sc.astype(BF16), wada_v[:, cols].astype(BF16))
        store_m = pltpu.make_async_copy(mod_v, mod_ref, local.at[3])
        store_m.start()
        store_m.wait()
        own_m = pltpu.make_async_copy(mod_ref, modg_ref.at[mine], local.at[4])
        own_m.start()
        gm.start()
        gm.finish()
        own_m.wait()
        gw.finish()

    any_ = pl.BlockSpec(memory_space=pl.ANY)
    return pl.pallas_call(
        body, name="gather_and_modulate",
        in_specs=[any_, vmem, any_, any_], out_specs=[any_, any_, any_, any_, vmem],
        scratch_shapes=[pltpu.VMEM((N_DEV, 8, D), F32), pltpu.VMEM((D, cs), F32), pltpu.VMEM((16, cs), F32)]
        + _comm_scratch(1) + _comm_scratch(1) + _comm_scratch(1) + [pltpu.SemaphoreType.DMA((5,))],
        out_shape=[jax.ShapeDtypeStruct((N_DEV, 8, D), F32), jax.ShapeDtypeStruct((N_DEV,) + w_half.shape, w_half.dtype),
                   jax.ShapeDtypeStruct((16, cs), F32), jax.ShapeDtypeStruct((N_DEV, 16, cs), F32),
                   jax.ShapeDtypeStruct((16, D), F32)],
        compiler_params=pltpu.CompilerParams(vmem_limit_bytes=VMEM_LIMIT),
    )(c_rows, c_ctx, w_half, w_ada)


def _ada_bwd(sc_all_t, dm_all, dmc, w_ada, m, v):
    cs = w_ada.shape[1]

    def body(st_ref, dm_ref, dmc_ref, w_ref, m_ref, v_ref, gw_ref, d_ref, nm_ref, nv_ref, part_ref):
        @pl.when(pl.program_id(0) == 0)
        def _():
            part_ref[...] = jnp.zeros_like(part_ref)

        g = _dot(st_ref[...].astype(BF16), dm_ref[...].astype(BF16))
        gw_ref[...] = g
        d_ref[...], nm_ref[...], nv_ref[...] = _adam_update(w_ref[...], g, m_ref[...], v_ref[...])
        part_ref[...] += _dot_nt(dmc_ref[...].astype(BF16), w_ref[...].astype(BF16))

    cols = pl.BlockSpec((D, ADA_TILE), lambda j: (0, j))
    shp = jax.ShapeDtypeStruct((D, cs), F32)
    return pl.pallas_call(
        body, name="ada_bwd_adamw", grid=(cs // ADA_TILE,),
        in_specs=[_acc((D, 16)), pl.BlockSpec((16, ADA_TILE), lambda j: (0, j)), pl.BlockSpec((8, ADA_TILE), lambda j: (0, j)),
                  cols, cols, cols],
        out_specs=[cols] * 4 + [_acc((8, D))],
        out_shape=[shp] * 4 + [jax.ShapeDtypeStruct((8, D), F32)],
        compiler_params=_cp("arbitrary"),
    )(sc_all_t, dm_all, dmc, w_ada, m, v)


def _sum8(x, name, tr=256):
    _, r, c = x.shape
    tr = min(tr, r)
    while r % tr:
        tr -= 16

    def body(x_ref, o_ref):
        acc = x_ref[0].astype(F32)
        for i in range(1, N_DEV):
            acc = acc + x_ref[i].astype(F32)
        o_ref[...] = acc

    return pl.pallas_call(
        body, name=name, grid=(r // tr,),
        in_specs=[pl.BlockSpec((N_DEV, tr, c), lambda i: (0, i, 0))],
        out_specs=pl.BlockSpec((tr, c), lambda i: (i, 0)),
        out_shape=jax.ShapeDtypeStruct((r, c), F32),
        compiler_params=_cp("arbitrary"),
    )(x)


def _sum_blocks(recv, src, me, name, tr=256):
    _, r, c = recv.shape
    tr = min(tr, r)
    while r % tr:
        tr -= 16

    def body(me_ref, recv_ref, own_ref, o_ref):
        acc = own_ref[...].astype(F32)
        for k in range(1, N_DEV):
            acc = acc + recv_ref[me_ref[0] ^ k].astype(F32)
        o_ref[...] = acc

    return pl.pallas_call(
        body, name=name,
        grid_spec=pltpu.PrefetchScalarGridSpec(
            num_scalar_prefetch=1, grid=(r // tr,),
            in_specs=[pl.BlockSpec((N_DEV, tr, c), lambda i, me_ref: (0, i, 0)),
                      pl.BlockSpec((None, tr, c), lambda i, me_ref: (me_ref[0], i, 0))],
            out_specs=pl.BlockSpec((tr, c), lambda i, me_ref: (i, 0))),
        out_shape=jax.ShapeDtypeStruct((r, c), F32),
        compiler_params=_cp("arbitrary"),
    )(me, recv, src)


def _sum8_many(xs, name):
    n = len(xs)

    def body(*refs):
        for x_ref, o_ref in zip(refs[:n], refs[n:]):
            acc = x_ref[0]
            for i in range(1, N_DEV):
                acc = acc + x_ref[i]
            o_ref[...] = acc

    vmem = pl.BlockSpec(memory_space=pltpu.VMEM)
    return pl.pallas_call(
        body, name=name, in_specs=[vmem] * n, out_specs=[vmem] * n,
        out_shape=[jax.ShapeDtypeStruct(v.shape[1:], v.dtype) for v in xs],
        compiler_params=pltpu.CompilerParams(vmem_limit_bytes=VMEM_LIMIT),
    )(*xs)


def _adam_update(w, g, m, v):
    nm = ADAM_B1 * m + (1.0 - ADAM_B1) * g
    nv = ADAM_B2 * v + (1.0 - ADAM_B2) * (g * g)
    m_hat = nm / (1.0 - ADAM_B1 ** ADAM_STEP)
    v_hat = nv / (1.0 - ADAM_B2 ** ADAM_STEP)
    return -ADAM_LR * (m_hat / (jnp.sqrt(v_hat) + ADAM_EPS) + ADAM_WD * w), nm, nv


ROW_LOSS, ROW_LN2_G, ROW_LN2_B, ROW_LN1_G, ROW_LN1_B = 0, 1, 2, 10, 11
ROWS_DMOD_X = (16, 17, 12, 9, 8, 3)
ROWS_DMOD_C = (24, 25)
SMALL = ("c_ctx", "b_ada", "attn_sink", "gmlp_ln_g", "gmlp_ln_b", "w_spatial", "b_spatial", "ln1_g", "ln1_b", "ln2_g", "ln2_b")


def _adamw_small(sums, dsc, w, m, v):
    n = len(SMALL)

    def body(*refs):
        st_ref, gm_ref, sk_ref, ws_ref, bs_ref, dsc_ref = refs[:6]
        w_refs = dict(zip(SMALL, refs[6:6 + n]))
        m_refs = dict(zip(SMALL, refs[6 + n:6 + 2 * n]))
        v_refs = dict(zip(SMALL, refs[6 + 2 * n:6 + 3 * n]))
        outs = refs[6 + 3 * n:]
        c = w_refs["c_ctx"][...]
        sg = _sig(c)
        dmod = [st_ref[r:r + 1, :] for r in ROWS_DMOD_X]
        dmod[0] = dmod[0] + st_ref[ROWS_DMOD_C[0]:ROWS_DMOD_C[0] + 1, :]
        dmod[1] = dmod[1] + st_ref[ROWS_DMOD_C[1]:ROWS_DMOD_C[1] + 1, :]
        grads = dict(
            c_ctx=dsc_ref[0:1, :] * (sg * (1.0 + c * (1.0 - sg))),
            b_ada=jnp.concatenate(dmod, axis=1),
            attn_sink=sk_ref[0:1, 0:N_KV * GROUP],
            gmlp_ln_g=gm_ref[0:1, :], gmlp_ln_b=gm_ref[1:2, :],
            w_spatial=ws_ref[...], b_spatial=bs_ref[...],
            ln1_g=st_ref[ROW_LN1_G:ROW_LN1_G + 1, :], ln1_b=st_ref[ROW_LN1_B:ROW_LN1_B + 1, :],
            ln2_g=st_ref[ROW_LN2_G:ROW_LN2_G + 1, :], ln2_b=st_ref[ROW_LN2_B:ROW_LN2_B + 1, :])
        for i, name in enumerate(SMALL):
            g = grads[name]
            d, nm, nv = _adam_update(w_refs[name][...], g, m_refs[name][...], v_refs[name][...])
            outs[i][...] = g
            outs[n + i][...] = d
            outs[2 * n + i][...] = nm
            outs[3 * n + i][...] = nv

    vmem = pl.BlockSpec(memory_space=pltpu.VMEM)
    args = list(sums) + [dsc] + [w[k] for k in SMALL] + [m[k] for k in SMALL] + [v[k] for k in SMALL]
    shapes = [jax.ShapeDtypeStruct(w[k].shape, F32) for k in SMALL]
    out = pl.pallas_call(
        body, name="adamw_small", in_specs=[vmem] * len(args), out_specs=[vmem] * (4 * n), out_shape=shapes * 4,
        compiler_params=pltpu.CompilerParams(vmem_limit_bytes=VMEM_LIMIT),
    )(*args)
    return [dict(zip(SMALL, out[i * n:(i + 1) * n])) for i in range(4)]


def _adamw_halves(w, mine, theirs, m, v, c_arr, name):
    r, c = w.shape
    tr = min(256, r // 2)
    while (r // 2) % tr:
        tr -= 8
    nt = (r // 2) // tr

    def body(c_ref, w_ref, mine_ref, theirs_ref, m_ref, v_ref, g_ref, d_ref, nm_ref, nv_ref):
        g = jnp.where(pl.program_id(0) == c_ref[0], mine_ref[...], theirs_ref[...])
        g_ref[...] = g
        d_ref[...], nm_ref[...], nv_ref[...] = _adam_update(w_ref[...], g, m_ref[...], v_ref[...])

    whole = pl.BlockSpec((tr, c), lambda hb, i, c_ref: (hb * nt + i, 0))
    mine_spec = pl.BlockSpec((tr, c), lambda hb, i, c_ref: (jnp.where(hb == c_ref[0], i, 0), 0))
    theirs_spec = pl.BlockSpec((tr, c), lambda hb, i, c_ref: (jnp.where(hb == c_ref[0], 0, i), 0))
    shp = jax.ShapeDtypeStruct((r, c), F32)
    return pl.pallas_call(
        body, name=name,
        grid_spec=pltpu.PrefetchScalarGridSpec(
            num_scalar_prefetch=1, grid=(2, nt), in_specs=[whole, mine_spec, theirs_spec, whole, whole],
            out_specs=[whole] * 4),
        out_shape=[shp] * 4,
        compiler_params=_cp("arbitrary", "arbitrary"),
    )(c_arr, w, mine, theirs, m, v)


def _my_pos():
    return lax.axis_index("x"), lax.axis_index("y"), lax.axis_index("c")


N_COPY = 7


class _Gather:
    def __init__(self, x_refs, out_refs, send_sems, recv_sems):
        self.x_refs, self.out_refs = x_refs, out_refs
        self.send_sems, self.recv_sems = send_sems, recv_sems
        x, y, c = _my_pos()
        self.c = c
        self.me, self.sibling = (x, y, c), (x, y, 1 - c)
        self.chips = [(1 - x, y), (x, 1 - y), (1 - x, 1 - y)]

    def _copy(self, a, k, block, to, from_input=False):
        px, py, pc = block
        rows = self.out_refs[a].at[4 * px + 2 * py + pc]
        return pltpu.make_async_remote_copy(
            src_ref=self.x_refs[a] if from_input else rows, dst_ref=rows,
            send_sem=self.send_sems.at[a * N_COPY + k], recv_sem=self.recv_sems.at[a * N_COPY + k],
            device_id=to, device_id_type=MESH)

    def start(self):
        n = len(self.x_refs)
        for a in range(n):
            self._copy(a, 0, self.me, self.sibling, from_input=True).start()
        for j, chip in enumerate(self.chips):
            for a in range(n):
                self._copy(a, 1 + j, self.me, (*chip, self.c), from_input=True).start()

    def forward(self):
        c = self.c
        for j, chip in enumerate(self.chips):
            for a in range(len(self.x_refs)):
                self._copy(a, 1 + j, (*chip, c), self.me).wait_recv()
                self._copy(a, 4 + j, (*chip, c), self.sibling).start()

    def finish(self):
        self.forward()
        self.drain()

    def drain(self):
        n = len(self.x_refs)
        c = self.c
        for a in range(n):
            self._copy(a, 0, self.sibling, self.me).wait_recv()
        for j, chip in enumerate(self.chips):
            for a in range(n):
                self._copy(a, 4 + j, (*chip, 1 - c), self.me).wait_recv()
        for a in range(n):
            self._copy(a, 0, self.me, self.sibling, from_input=True).wait_send()
            for j, chip in enumerate(self.chips):
                self._copy(a, 1 + j, self.me, (*chip, c), from_input=True).wait_send()
                self._copy(a, 4 + j, (*chip, c), self.sibling).wait_send()


def _comm_scratch(n):
    return [pltpu.SemaphoreType.DMA((n * N_COPY,)), pltpu.SemaphoreType.DMA((n * N_COPY,))]


def _comm_specs(n):
    return [pl.BlockSpec(memory_space=pl.ANY)] * n


def _gathered_shapes(xs):
    return [jax.ShapeDtypeStruct((N_DEV,) + v.shape, v.dtype) for v in xs]


def _with_own(gathered, xs, me):
    return [lax.dynamic_update_index_in_dim(g, v, me, 0) for g, v in zip(gathered, xs)]


class _AllToAll:
    def __init__(self, x_refs, out_refs, send_sems, recv_sems):
        self.x_refs, self.out_refs = x_refs, out_refs
        self.send_sems, self.recv_sems = send_sems, recv_sems
        self.pos = _my_pos()
        x, y, c = self.pos
        self.me = 4 * x + 2 * y + c

    def _peer(self, k):
        x, y, c = self.pos
        return (x ^ ((k >> 2) & 1), y ^ ((k >> 1) & 1), c ^ (k & 1))

    def _copy(self, a, k):
        p = self._peer(k)
        return pltpu.make_async_remote_copy(
            src_ref=self.x_refs[a].at[4 * p[0] + 2 * p[1] + p[2]], dst_ref=self.out_refs[a].at[self.me],
            send_sem=self.send_sems.at[a * N_COPY + k - 1], recv_sem=self.recv_sems.at[a * N_COPY + k - 1],
            device_id=p, device_id_type=MESH)

    def start(self):
        for k in range(1, N_DEV):
            for a in range(len(self.x_refs)):
                self._copy(a, k).start()

    def finish(self):
        for a in range(len(self.x_refs)):
            for k in range(1, N_DEV):
                self._copy(a, k).wait_recv()
            for k in range(1, N_DEV):
                self._copy(a, k).wait_send()


def _exchange(name, scatter=(), gather=(), sibling=()):
    ns, ng, nx = len(scatter), len(gather), len(sibling)
    n = ns + ng + nx

    def body(*refs):
        ins, outs, sems = refs[:n], refs[n:2 * n], list(refs[2 * n:])
        t = _AllToAll(ins[:ns], outs[:ns], sems.pop(0), sems.pop(0)) if ns else None
        g = _Gather(ins[ns:ns + ng], outs[ns:ns + ng], sems.pop(0), sems.pop(0)) if ng else None
        x, y, c = _my_pos()

        def push(a):
            return pltpu.make_async_remote_copy(
                src_ref=ins[ns + ng + a], dst_ref=outs[ns + ng + a], send_sem=sems[0].at[a], recv_sem=sems[1].at[a],
                device_id=(x, y, 1 - c), device_id_type=MESH)

        for comm in (g, t):
            if comm is not None:
                comm.start()
        for a in range(nx):
            push(a).start()
        for comm in (g, t):
            if comm is not None:
                comm.finish()
        for a in range(nx):
            push(a).wait_recv()
            push(a).wait_send()

    scratch = (_comm_scratch(ns) if ns else []) + (_comm_scratch(ng) if ng else [])
    scratch += [pltpu.SemaphoreType.DMA((nx,)), pltpu.SemaphoreType.DMA((nx,))] if nx else []
    out = pl.pallas_call(
        body, name=name,
        out_shape=[jax.ShapeDtypeStruct(v.shape, v.dtype) for v in scatter] + _gathered_shapes(gather)
        + [jax.ShapeDtypeStruct(v.shape, v.dtype) for v in sibling],
        in_specs=_comm_specs(n), out_specs=_comm_specs(n), scratch_shapes=scratch,
    )(*scatter, *gather, *sibling)
    return out[:ns], out[ns:ns + ng], out[ns + ng:]


def _row_tile(seq, want):
    return min(want, seq)


def _local_step(x, ctx, tgt, mod_x, mod_c, wb, sink, gmlp_g, gmlp_b, w_s, b_s, ln1_g, ln1_b, ln2_g, ln2_b,
                later=None, me=None):
    seq = x.shape[0]
    on_mesh = me is not None
    modx1 = jnp.concatenate([mod_x[0:2], jnp.zeros((6, D), F32)], axis=0)
    modc = jnp.concatenate([mod_c[0:2], jnp.zeros((6, D), F32)], axis=0)
    vec = jnp.concatenate([mod_x[2:3], ln1_g, ln1_b, mod_x[3:6], ln2_g, ln2_b], axis=0)
    gp = jnp.concatenate([gmlp_g, gmlp_b, jnp.zeros((6, G_W), F32)], axis=0)
    ws_stack = w_s.reshape(N_GRP * BLK, BLK).astype(BF16)
    ws_stack_t = jnp.transpose(w_s, (2, 0, 1)).reshape(BLK, N_GRP * BLK).astype(BF16)
    bias_full = jnp.repeat(b_s.T, GRP_D, axis=1)
    cos, sin = _rope_tables(seq)
    w_in = wb["w_in"]
    w_kv = w_in[Q_W:Q_W + 2 * KV_W, :]
    tm_big = _row_tile(seq, 512)
    tm_ffn = _row_tile(seq, 256)

    hc, kvc, vac = _ctx_fwd(ctx, modc, w_kv)
    behind_proj = ("w_a", "w_b", "w_o") if on_mesh else ()
    behind_attn = ("w_fi",) if on_mesh else ()
    behind_mix = ("w_fo",) if on_mesh else ()
    wb = dict(wb)

    def whole(names, gathered):
        for n, g in zip(names, _with_own(list(gathered), [later[n] for n in names], me)):
            wb[n] = g.reshape(-1, g.shape[2]) if n in ROW_SHARDED else g.reshape(N_SHARD, 2 * g.shape[1], g.shape[2])

    (h, q, kv, va, uv, gab), got = _proj_fwd(x, modx1, w_in, cos, sin, _row_tile(seq, 1024), gather=[later[n] for n in behind_proj])
    whole(behind_proj, got)
    if on_mesh:
        for n in ("w_a", "w_b"):
            wb[n] = wb[n].transpose(1, 0, 2).reshape(wb[n].shape[1], D)
    (ya, lse), got = _attn_fwd(q, kv, va, kvc, vac, sink, gather=[later[n] for n in behind_attn])
    whole(behind_attn, got)
    (a, b, mix, merged, yb), got = _mix_fwd(uv, gab, ya, gp, ws_stack, bias_full, wb["w_a"], wb["w_b"], wb["w_o"], tm_big,
                                            gather=[later[n] for n in behind_mix])
    whole(behind_mix, got)
    act, h2, dff, df, dr1, st_ffn = _ffn(x, mix, tgt, vec, wb["w_fi"], wb["w_fo"], tm_ffn)
    blocks, recv = {}, {}
    blocks["w_fo"] = _eighths(_tn_matmul(act, df, 512, "tn_w_ffn_out", BF16, tk=2048))
    if on_mesh:
        g_w_fi, (recv["w_fo"],) = _tn_matmul(h2, dff, FH_SHARD, "tn_w_ffn_in", BF16, shard_major=True, tk=2048,
                                            scatter=[blocks["w_fo"]])
    else:
        g_w_fi = _tn_matmul(h2, dff, FH_SHARD, "tn_w_ffn_in", BF16, shard_major=True, tk=2048)
    blocks["w_fi"] = _eighths(g_w_fi)
    (dya, dpb, dws, dbs_full, st4, g_w_o, g_w_a, g_w_b), got = _mix_bwd(
        dr1, a, b, gab, uv, merged, ya, yb, vec, gp, ws_stack, ws_stack_t, bias_full, wb["w_a"], wb["w_b"], wb["w_o"],
        tm_big, scatter=[blocks["w_fi"]] if on_mesh else ())
    recv.update(zip(("w_fi",), got))
    shard_major = [g.reshape(g.shape[0], N_SHARD, D // N_SHARD).transpose(1, 0, 2) for g in (g_w_a, g_w_b)]
    blocks.update(w_o=_eighths(g_w_o), w_a=_eighths(shard_major[0]), w_b=_eighths(shard_major[1]))
    mixer = ("w_o", "w_a", "w_b") if on_mesh else ()
    (dq, dkv, dkvc, dsink), got = _attn_bwd(q, kv, kvc, sink, dya, ya, lse, scatter=[blocks[n] for n in mixer])
    recv.update(zip(mixer, got))
    g_wkv_ctx, st0 = _ctx_bwd(dkvc, ctx, hc, w_kv)
    (dqkv, grad_x, st1), _ = _proj_bwd(dq, dkv, dpb, x, dr1, modx1, w_in, cos, sin, tm_big)
    dbs = jnp.sum(dbs_full.reshape(BLK, N_GRP, GRP_D), axis=2).T
    early = [jnp.concatenate([st_ffn, st0], axis=0), st4, dsink, dws, dbs]
    g_in = _tn_matmul(dqkv, h, D, "tn_w_in_qkv", BF16, init=g_wkv_ctx, tk=1024, out_rows=IN_W)
    into = (g_in, dqkv.shape[1])
    if on_mesh:
        g_in, early_gathered = _tn_matmul(dpb, h, D, "tn_w_in_rest", BF16, tk=1024, into=into, gather=early)
    else:
        g_in, early_gathered = _tn_matmul(dpb, h, D, "tn_w_in_rest", BF16, tk=1024, into=into), None
    blocks["w_in"] = _eighths(g_in)
    return grad_x, dict(early=early, early_gathered=early_gathered, late=st1), blocks, recv


BIG = ("w_in", "w_a", "w_b", "w_o", "w_fi", "w_fo")
ROW_SHARDED = ("w_o", "w_fo")


def _half_of_shard(shard, c):
    r = shard.shape[0]
    return lax.dynamic_slice_in_dim(shard, c * (r // 2), r // 2, axis=0)


def _eighths(v):
    rows = v.shape[-2] * (v.shape[0] if v.ndim == 3 else 1)
    return v.reshape(N_DEV, rows // N_DEV, v.shape[-1])


def kernel(x, c, ctx, c_ctx, w_ada, b_ada, w_in, attn_sink, gmlp_ln_g, gmlp_ln_b, w_spatial, b_spatial, w_branch_a, w_branch_b, w_out, ln1_g, ln1_b, w_ffn_in, w_ffn_out, ln2_g, ln2_b, loss_target, m_c_ctx, m_w_ada, m_b_ada, m_w_in, m_attn_sink, m_gmlp_ln_g, m_gmlp_ln_b, m_w_spatial, m_b_spatial, m_w_branch_a, m_w_branch_b, m_w_out, m_ln1_g, m_ln1_b, m_w_ffn_in, m_w_ffn_out, m_ln2_g, m_ln2_b, v_c_ctx, v_w_ada, v_b_ada, v_w_in, v_attn_sink, v_gmlp_ln_g, v_gmlp_ln_b, v_w_spatial, v_b_spatial, v_w_branch_a, v_w_branch_b, v_w_out, v_ln1_g, v_ln1_b, v_w_ffn_in, v_w_ffn_out, v_ln2_g, v_ln2_b):
    mx, my, mc = _my_pos()
    me = 4 * mx + 2 * my + mc
    chip = 2 * mx + my
    shards = dict(w_in=w_in[0].T, w_a=w_branch_a[0], w_b=w_branch_b[0], w_o=w_out[0], w_fi=w_ffn_in[0], w_fo=w_ffn_out[0])

    halves = {n: _half_of_shard(shards[n], mc).astype(BF16) for n in BIG}
    c_rows = jnp.concatenate([c, jnp.zeros((7, D), F32)], axis=0)
    _, g_in, _, mod_g, sc_all = _gather_and_modulate(c_rows, c_ctx[None, :], halves["w_in"], w_ada[0])
    wb = dict(w_in=_with_own([g_in], [halves["w_in"]], me)[0].reshape(IN_W, D))
    mod_all = jnp.concatenate([mod_g[2 * s] for s in range(4)], axis=1) + b_ada
    mod_x = lax.dynamic_slice_in_dim(mod_all, me, 1, axis=0).reshape(6, D)
    mod_c = mod_all[8].reshape(6, D)[0:2]

    grad_x, small, blocks, recv = _local_step(
        x[0], ctx[0], loss_target[0], mod_x, mod_c, wb, attn_sink, gmlp_ln_g, gmlp_ln_b, w_spatial[0], b_spatial[0],
        ln1_g, ln1_b, ln2_g, ln2_b, later=halves, me=me)

    me_arr = jnp.reshape(me, (1,)).astype(jnp.int32)
    summed = {n: _sum_blocks(recv[n], blocks[n], me_arr, "sum_grads_" + n) for n in BIG[1:]}
    (recv["w_in"],), late, theirs = _exchange("scatter_w_in_gather_small_exchange_grads", scatter=[blocks["w_in"]],
                                              gather=[small["late"]], sibling=[summed[n] for n in BIG[1:]])
    theirs = dict(zip(BIG[1:], theirs))
    summed["w_in"] = _sum_blocks(recv["w_in"], blocks["w_in"], me_arr, "sum_grads_w_in")
    late = _with_own(late, [small["late"]], me)[0]
    gathered = _with_own(small["early_gathered"], small["early"], me)
    gathered[0] = jnp.concatenate([gathered[0][:, :16], late, gathered[0][:, 16:]], axis=1)

    sums = _sum8_many(gathered, "sum_small")
    stats = sums[0]
    loss = 0.5 * jnp.sum(stats[ROW_LOSS]) / D
    dmod_x_all = jnp.concatenate([gathered[0][:, r_, :] for r_ in ROWS_DMOD_X], axis=1)
    dmod_c_full = jnp.concatenate([stats[r_] for r_ in ROWS_DMOD_C] + [jnp.zeros((4 * D,), F32)])
    dm_rows = jnp.concatenate([dmod_x_all, dmod_c_full[None, :], jnp.zeros((7, 6 * D), F32)], axis=0)
    cs = w_ada.shape[2]
    dm_shard = lax.dynamic_slice_in_dim(dm_rows, chip * cs, cs, axis=1)
    dmc_shard = jnp.concatenate([dm_shard[8:9], jnp.zeros((7, cs), F32)], axis=0)
    ada = _ada_bwd(sc_all.T, dm_shard, dmc_shard, w_ada[0], m_w_ada[0], v_w_ada[0])
    part = ada[4]
    part = part * (mc == 0).astype(F32)
    _, part_all, (theirs["w_in"],) = _exchange("exchange_w_in_gather_c_ctx", gather=[part], sibling=[summed["w_in"]])
    dsc = _sum8(_with_own(part_all, [part], me)[0], "sum_c_ctx")

    weights = dict(c_ctx=c_ctx, w_ada=w_ada, b_ada=b_ada, w_in=w_in, attn_sink=attn_sink, gmlp_ln_g=gmlp_ln_g,
                   gmlp_ln_b=gmlp_ln_b, w_spatial=w_spatial, b_spatial=b_spatial, w_branch_a=w_branch_a,
                   w_branch_b=w_branch_b, w_out=w_out, ln1_g=ln1_g, ln1_b=ln1_b, w_ffn_in=w_ffn_in, w_ffn_out=w_ffn_out,
                   ln2_g=ln2_g, ln2_b=ln2_b)
    ms = dict(c_ctx=m_c_ctx, w_ada=m_w_ada, b_ada=m_b_ada, w_in=m_w_in, attn_sink=m_attn_sink, gmlp_ln_g=m_gmlp_ln_g,
              gmlp_ln_b=m_gmlp_ln_b, w_spatial=m_w_spatial, b_spatial=m_b_spatial, w_branch_a=m_w_branch_a,
              w_branch_b=m_w_branch_b, w_out=m_w_out, ln1_g=m_ln1_g, ln1_b=m_ln1_b, w_ffn_in=m_w_ffn_in,
              w_ffn_out=m_w_ffn_out, ln2_g=m_ln2_g, ln2_b=m_ln2_b)
    vs = dict(c_ctx=v_c_ctx, w_ada=v_w_ada, b_ada=v_b_ada, w_in=v_w_in, attn_sink=v_attn_sink, gmlp_ln_g=v_gmlp_ln_g,
              gmlp_ln_b=v_gmlp_ln_b, w_spatial=v_w_spatial, b_spatial=v_b_spatial, w_branch_a=v_w_branch_a,
              w_branch_b=v_w_branch_b, w_out=v_w_out, ln1_g=v_ln1_g, ln1_b=v_ln1_b, w_ffn_in=v_w_ffn_in,
              w_ffn_out=v_w_ffn_out, ln2_g=v_ln2_g, ln2_b=v_ln2_b)
    order = list(weights)
    grads, delta, new_m, new_v = [dict(w_ada=t[None]) for t in ada[:4]]
    c_arr = jnp.reshape(mc, (1,)).astype(jnp.int32)
    names = dict(w_in="w_in", w_a="w_branch_a", w_b="w_branch_b", w_o="w_out", w_fi="w_ffn_in", w_fo="w_ffn_out")
    for k, n in names.items():
        flip = (lambda t: t.T) if k == "w_in" else (lambda t: t)
        outs = _adamw_halves(flip(weights[n][0]), summed[k], theirs[k], flip(ms[n][0]), flip(vs[n][0]), c_arr, "adamw_" + n)
        grads[n], delta[n], new_m[n], new_v[n] = [flip(t)[None] for t in outs]

    def view(a):
        return a.reshape(-1, a.shape[-1]) if a.ndim != 1 else a.reshape(1, -1)

    small = _adamw_small(sums, dsc, *[{n: view(d[n]) for n in SMALL} for d in (weights, ms, vs)])
    for out, src in zip((grads, delta, new_m, new_v), small):
        for n in SMALL:
            out[n] = src[n].reshape(weights[n].shape)

    return (loss, grad_x[None], *[grads[n] for n in order], *[delta[n] for n in order],
            *[new_m[n] for n in order], *[new_v[n] for n in order])
```

```python
import math

import jax
import jax.numpy as jnp
import numpy as np
from jax import lax
from jax.experimental import pallas as pl
from jax.experimental.pallas import tpu as pltpu

F32 = jnp.float32
BF16 = jnp.bfloat16

D = 1024
HEAD = 64
N_KV = 2
GROUP = 4
Q_W = 512
KV_W = 128
G_W = 512
BLK = 128
N_GRP = 8
GRP_D = 64
FH = 2816
IN_W = 3840
GRID_W = 64
ROPE_BASE = 10000.0
LN_EPS = 1e-5
NEG = -1e30
ALPHA = (2 * 1) ** 0.25
SCALE = HEAD ** -0.5
GELU_K = math.sqrt(2.0 / math.pi)
GELU_A = 0.044715
ADAM_LR = 0.001
ADAM_B1 = 0.9
ADAM_B2 = 0.999
ADAM_EPS = 1e-08
ADAM_WD = 0.01
ADAM_STEP = 10
N_DEV = 8
N_SHARD = 4
FH_SHARD = FH // 2
LANES = 128
VMEM_LIMIT = 56 * 1024 * 1024
MESH = pl.DeviceIdType.MESH


def _cp(*sem):
    return pltpu.CompilerParams(dimension_semantics=sem, vmem_limit_bytes=VMEM_LIMIT)


def _resident(shape):
    return pl.BlockSpec(shape, lambda *_: (0,) * len(shape), pipeline_mode=pl.Buffered(1))


def _rows(tm, width):
    return pl.BlockSpec((tm, width), lambda i: (i, 0))


def _acc(shape):
    return pl.BlockSpec(shape, lambda *_: (0,) * len(shape))


def _dot(a, b):
    return jnp.dot(a, b, preferred_element_type=F32)


def _dot_nt(a, b):
    return lax.dot_general(a, b, (((1,), (1,)), ((), ())), preferred_element_type=F32)


def _dot_tn(a, b):
    return lax.dot_general(a, b, (((0,), (0,)), ((), ())), preferred_element_type=F32)


def _ln(x):
    mu = jnp.mean(x, axis=-1, keepdims=True)
    xc = x - mu
    var = jnp.mean(xc * xc, axis=-1, keepdims=True)
    rstd = lax.rsqrt(var + LN_EPS)
    return xc * rstd, rstd


def _ln_bwd(dxhat, xhat, rstd):
    return (dxhat - jnp.mean(dxhat, axis=-1, keepdims=True)
            - xhat * jnp.mean(dxhat * xhat, axis=-1, keepdims=True)) * rstd


def _sig(x):
    return 0.5 + 0.5 * jnp.tanh(0.5 * x)


def _gelu(x):
    t = jnp.tanh(x * (GELU_K + (GELU_K * GELU_A) * (x * x)))
    hx = 0.5 * x
    return hx + hx * t, t


def _gelu_grad(x, t):
    return 0.5 + 0.5 * t + (0.5 * x) * (1.0 - t * t) * (GELU_K + (3.0 * GELU_K * GELU_A) * (x * x))


def _colsum(v):
    return jnp.sum(v, axis=0, keepdims=True)


def _partner(x):
    w = x.shape[1]
    lane = lax.broadcasted_iota(jnp.int32, x.shape, 1)
    return jnp.where((lane & 31) < 16, pltpu.roll(x, w - 16, 1), pltpu.roll(x, 16, 1))


def _rope(x, cos, sin):
    return x * cos + _partner(x) * sin


def _unrope(g, cos, sin):
    return g * cos + _partner(g * sin)


def _rope_tables(seq):
    inv = np.float32(ROPE_BASE) ** (-np.arange(HEAD // 4, dtype=np.float32) / np.float32(HEAD // 4))
    pos = np.arange(seq)
    ar = (pos // GRID_W).astype(np.float32)[:, None] * inv
    ac = (pos % GRID_W).astype(np.float32)[:, None] * inv
    cos = np.concatenate([np.cos(ar), np.cos(ar), np.cos(ac), np.cos(ac)], axis=-1)
    sin = np.concatenate([-np.sin(ar), np.sin(ar), -np.sin(ac), np.sin(ac)], axis=-1)
    reps = (1, LANES // HEAD)
    return jnp.asarray(np.tile(cos, reps), F32), jnp.asarray(np.tile(sin, reps), F32)


def _kv_rows_of_w_in():
    return pl.BlockSpec((2 * KV_W, D), lambda *_: (Q_W // (2 * KV_W), 0))


def _ctx_fwd(ctx, modc, w_in):
    n_ctx = ctx.shape[0]

    def body(ctx_ref, mod_ref, w_ref, hc_ref, kvc_ref, vac_ref):
        xhat, _ = _ln(ctx_ref[...])
        hc = (xhat * (1.0 + mod_ref[1:2, :]) + mod_ref[0:1, :]).astype(BF16)
        hc_ref[...] = hc
        kvc = _dot_nt(hc, w_ref[...]).astype(BF16)
        kvc_ref[...] = kvc
        vac_ref[...] = _with_ones(kvc[:, KV_W:])

    return pl.pallas_call(
        body, name="ctx_fwd", grid=(1,),
        in_specs=[_acc((n_ctx, D)), _acc((8, D)), _kv_rows_of_w_in()],
        out_specs=[_acc((n_ctx, D)), _acc((n_ctx, 2 * KV_W)), _acc((n_ctx, 2 * LANES))],
        out_shape=[jax.ShapeDtypeStruct((n_ctx, D), BF16), jax.ShapeDtypeStruct((n_ctx, 2 * KV_W), BF16),
                   jax.ShapeDtypeStruct((n_ctx, 2 * LANES), BF16)],
        compiler_params=_cp("arbitrary"),
    )(ctx, modc, w_in)


def _host_start(step, comm):
    if comm is not None:
        @pl.when(step == 0)
        def _():
            comm.start()


def _host_finish(step, last, comm, forward_at=None):
    if comm is None:
        return
    if forward_at is None or forward_at >= last:
        @pl.when(step == last)
        def _():
            comm.finish()
    else:
        @pl.when(step == forward_at)
        def _():
            comm.forward()

        @pl.when(step == last)
        def _():
            comm.drain()


def _proj_fwd(x, modx, w_in, cos, sin, tm, gather=()):
    seq = x.shape[0]
    ng = len(gather)

    def body(x_ref, mod_ref, w_ref, cos_ref, sin_ref, *rest):
        h_ref, q_ref, kv_ref, va_ref, uv_ref, gab_ref = rest[ng:ng + 6]
        comm = _Gather(rest[:ng], rest[ng + 6:2 * ng + 6], *rest[2 * ng + 6:]) if ng else None
        _host_start(pl.program_id(0), comm)
        xhat, _ = _ln(x_ref[...])
        h = (xhat * (1.0 + mod_ref[1:2, :]) + mod_ref[0:1, :]).astype(BF16)
        h_ref[...] = h
        cos1, sin1 = cos_ref[...], sin_ref[...]
        cos2 = jnp.concatenate([cos1, cos1], axis=1)
        sin2 = jnp.concatenate([sin1, sin1], axis=1)
        for j in range(Q_W // 256):
            t = _dot_nt(h, w_ref[256 * j:256 * (j + 1), :])
            q_ref[:, 256 * j:256 * (j + 1)] = (_rope(t, cos2, sin2) * SCALE).astype(BF16)
        t = _dot_nt(h, w_ref[Q_W:Q_W + 2 * KV_W, :])
        kv_ref[:, :KV_W] = _rope(t[:, :KV_W], cos1, sin1).astype(BF16)
        v = t[:, KV_W:].astype(BF16)
        kv_ref[:, KV_W:] = v
        va_ref[...] = _with_ones(v)
        o = Q_W + 2 * KV_W
        for j in range(2):
            uv_ref[:, G_W * j:G_W * (j + 1)] = _dot_nt(h, w_ref[o + G_W * j:o + G_W * (j + 1), :]).astype(BF16)
        o += 2 * G_W
        for j in range(4):
            gab_ref[:, 512 * j:512 * (j + 1)] = _dot_nt(h, w_ref[o + 512 * j:o + 512 * (j + 1), :]).astype(BF16)
        _host_finish(pl.program_id(0), seq // tm - 1, comm, forward_at=(seq // tm) // 2)

    out = pl.pallas_call(
        body, name="proj_fwd", grid=(seq // tm,),
        in_specs=[_rows(tm, D), _acc((8, D)), _resident((IN_W, D)), _rows(tm, LANES), _rows(tm, LANES)] + _comm_specs(ng),
        out_specs=[_rows(tm, D), _rows(tm, Q_W), _rows(tm, 2 * KV_W), _rows(tm, 2 * LANES), _rows(tm, 2 * G_W),
                   _rows(tm, 2 * D)] + _comm_specs(ng),
        out_shape=[jax.ShapeDtypeStruct((seq, D), BF16), jax.ShapeDtypeStruct((seq, Q_W), BF16),
                   jax.ShapeDtypeStruct((seq, 2 * KV_W), BF16), jax.ShapeDtypeStruct((seq, 2 * LANES), BF16),
                   jax.ShapeDtypeStruct((seq, 2 * G_W), BF16), jax.ShapeDtypeStruct((seq, 2 * D), BF16)] + _gathered_shapes(gather),
        scratch_shapes=_comm_scratch(ng) if ng else [],
        compiler_params=_cp("arbitrary"),
    )(x, modx, w_in, cos, sin, *gather)
    return out[:6], out[6:]


def _stack_heads(x, hk):
    return jnp.concatenate([x[:, (hk * GROUP + g) * HEAD:(hk * GROUP + g + 1) * HEAD] for g in range(GROUP)], axis=0)


def _band_masks(n, nb):
    rows = GROUP * BLK
    qi = lax.broadcasted_iota(jnp.int32, (rows, BLK), 0) & (BLK - 1)
    kj = lax.broadcasted_iota(jnp.int32, (rows, BLK), 1)
    return (kj >= qi) & (n > 0), (kj <= qi) & (n < nb - 1)


def _attn_scores(q, k_refs, hk, masks):
    q4 = _stack_heads(q, hk)
    ks = [r[:, hk * HEAD:(hk + 1) * HEAD] for r in k_refs]
    s = [_dot_nt(q4, k) for k in ks]
    s[1] = jnp.where(masks[0], s[1], NEG)
    s[3] = jnp.where(masks[1], s[3], NEG)
    return q4, ks, s


def _sink_rows(sink_ref, hk):
    rows = GROUP * BLK
    rg = lax.broadcasted_iota(jnp.int32, (rows, 1), 0) >> 7
    sink_v = jnp.full((rows, 1), sink_ref[0, hk * GROUP], F32)
    for g in range(1, GROUP):
        sink_v = jnp.where(rg == g, sink_ref[0, hk * GROUP + g], sink_v)
    return sink_v


def _with_ones(v):
    ones = jnp.ones((v.shape[0], HEAD), v.dtype)
    return jnp.concatenate([v[:, :HEAD], ones, v[:, HEAD:], ones], axis=1)


def _kv_specs(nb, qb):
    def spec(d):
        return pl.BlockSpec((BLK, 2 * KV_W), lambda n: (jnp.clip(qb * n + d, 0, nb - 1), 0))
    return [spec(d) for d in range(-1, qb + 1)]


def _attn_fwd(q, kv, va, kvc, vac, sink, gather=()):
    seq = q.shape[0]
    nb = seq // BLK
    n_ctx = kvc.shape[0]
    ng = len(gather)
    Q_BLOCKS = 1
    nkv = Q_BLOCKS + 2
    steps = nb // Q_BLOCKS

    def body(q_ref, *rest):
        kv_refs, va_refs = rest[:nkv], rest[nkv:2 * nkv]
        kvc_ref, vac_ref, sink_ref = rest[2 * nkv:2 * nkv + 3]
        rest = rest[2 * nkv + 3:]
        o_ref, lse_ref = rest[ng:ng + 2]
        comm = _Gather(rest[:ng], rest[ng + 2:2 * ng + 2], *rest[2 * ng + 2:]) if ng else None
        n = pl.program_id(0)
        _host_start(n, comm)
        lane = lax.broadcasted_iota(jnp.int32, (BLK, LANES), 1)
        for sub in range(Q_BLOCKS):
            rs = slice(sub * BLK, (sub + 1) * BLK)
            q = q_ref[rs, :]
            outs = []
            lse_all = jnp.zeros((BLK, LANES), F32)
            masks = _band_masks(Q_BLOCKS * n + sub, nb)
            for hk in range(N_KV):
                _, _, s = _attn_scores(q, (kvc_ref,) + kv_refs[sub:sub + 3], hk, masks)
                sink_v = _sink_rows(sink_ref, hk)
                tile_max = s[1]
                for t in [s[0][:, i * LANES:(i + 1) * LANES] for i in range(n_ctx // LANES)] + s[2:]:
                    tile_max = jnp.maximum(tile_max, t)
                m = jnp.maximum(sink_v, jnp.max(tile_max, axis=-1, keepdims=True))
                o = jnp.zeros((GROUP * BLK, LANES), F32)
                for t, va_ref in zip(s, (vac_ref,) + va_refs[sub:sub + 3]):
                    o = o + _dot(jnp.exp((t - m).astype(BF16)), va_ref[:, hk * LANES:(hk + 1) * LANES])
                denom = o[:, HEAD:HEAD + 1] + jnp.exp(sink_v - m)
                o4 = o[:, :HEAD] * (1.0 / denom)
                lse4 = m + jnp.log(denom)
                for g in range(GROUP):
                    outs.append(o4[g * BLK:(g + 1) * BLK, :])
                    lse_all = jnp.where(lane == hk * GROUP + g, lse4[g * BLK:(g + 1) * BLK, :], lse_all)
            o_ref[rs, :] = jnp.concatenate(outs, axis=1).astype(BF16)
            lse_ref[rs, :] = lse_all
        _host_finish(n, steps - 1, comm, forward_at=(3 * steps) // 4)

    tq = Q_BLOCKS * BLK
    out = pl.pallas_call(
        body, name="attn_fwd", grid=(steps,),
        in_specs=[_rows(tq, Q_W)] + _kv_specs(nb, Q_BLOCKS) + _kv_specs(nb, Q_BLOCKS)
        + [_acc((n_ctx, 2 * KV_W)), _acc((n_ctx, 2 * LANES)), pl.BlockSpec(memory_space=pltpu.SMEM)] + _comm_specs(ng),
        out_specs=[_rows(tq, Q_W), _rows(tq, LANES)] + _comm_specs(ng),
        out_shape=[jax.ShapeDtypeStruct((seq, Q_W), BF16), jax.ShapeDtypeStruct((seq, LANES), F32)] + _gathered_shapes(gather),
        scratch_shapes=_comm_scratch(ng) if ng else [],
        compiler_params=_cp("arbitrary"),
    )(q, *([kv] * nkv), *([va] * nkv), kvc, vac, sink, *gather)
    return out[:2], out[2:]


def _gmlp_chunk(u, vb, gp_ref, ws_ref, bias_ref):
    gu, tu = _gelu(u)
    gv, tv = _gelu(vb)
    vhat, rstd = _ln(gv)
    vn = (vhat * gp_ref[0:1, :] + gp_ref[1:2, :]).astype(BF16)
    s = bias_ref[...] + jnp.concatenate(
        [_dot(ws_ref[g * BLK:(g + 1) * BLK, :], vn[:, g * GRP_D:(g + 1) * GRP_D]) for g in range(N_GRP)], axis=1)
    return gu, tu, tv, vhat, rstd, vn, s


def _mix_fwd(uv, gab, ya, gp, ws_stack, bias_full, w_a, w_b, w_o, tm, gather=()):
    seq = uv.shape[0]
    ng = len(gather)
    steps = seq // tm

    def body(uv_ref, gab_ref, ya_ref, gp_ref, ws_ref, bias_ref, wa_ref, wb_ref, wo_ref, *rest):
        a_ref, b_ref, mix_ref, merged_ref, yb_ref = rest[ng:ng + 5]
        comm = _Gather(rest[:ng], rest[ng + 5:2 * ng + 5], *rest[2 * ng + 5:]) if ng else None
        _host_start(pl.program_id(0), comm)
        for c in range(tm // BLK):
            rs = slice(c * BLK, (c + 1) * BLK)
            gu, _, _, _, _, _, s = _gmlp_chunk(uv_ref[rs, :G_W].astype(F32), uv_ref[rs, G_W:].astype(F32), gp_ref, ws_ref, bias_ref)
            yb_ref[rs, :] = (gu * s).astype(BF16)
        a = _dot(ya_ref[...], wa_ref[...])
        b = _dot(yb_ref[...], wb_ref[...])
        a_ref[...] = a.astype(BF16)
        b_ref[...] = b.astype(BF16)
        merged = (_sig(gab_ref[:, :D].astype(F32)) * a + _sig(gab_ref[:, D:].astype(F32)) * b).astype(BF16)
        merged_ref[...] = merged
        mix_ref[...] = _dot(merged, wo_ref[...])
        _host_finish(pl.program_id(0), steps - 1, comm, forward_at=(3 * steps) // 4)

    out = pl.pallas_call(
        body, name="mix_fwd", grid=(steps,),
        in_specs=[_rows(tm, 2 * G_W), _rows(tm, 2 * D), _rows(tm, Q_W), _acc((8, G_W)),
                  _resident((N_GRP * BLK, BLK)), _acc((BLK, G_W)),
                  _resident((Q_W, D)), _resident((G_W, D)), _resident((D, D))] + _comm_specs(ng),
        out_specs=[_rows(tm, D), _rows(tm, D), _rows(tm, D), _rows(tm, D), _rows(tm, G_W)] + _comm_specs(ng),
        out_shape=[jax.ShapeDtypeStruct((seq, D), BF16), jax.ShapeDtypeStruct((seq, D), BF16),
                   jax.ShapeDtypeStruct((seq, D), F32), jax.ShapeDtypeStruct((seq, D), BF16),
                   jax.ShapeDtypeStruct((seq, G_W), BF16)] + _gathered_shapes(gather),
        scratch_shapes=_comm_scratch(ng) if ng else [],
        compiler_params=_cp("arbitrary"),
    )(uv, gab, ya, gp, ws_stack, bias_full, w_a, w_b, w_o, *gather)
    return out[:5], out[5:]


def _mid_recompute(x_ref, mix_ref, vec_ref):
    r1 = ALPHA * x_ref[...] + vec_ref[0:1, :] * mix_ref[...]
    xh1, rstd1 = _ln(r1)
    xmid = xh1 * vec_ref[1:2, :] + vec_ref[2:3, :]
    xh2, rstd2 = _ln(xmid)
    return xh1, rstd1, xmid, xh2, rstd2


def _ffn(x, mix, tgt, vec, w_fi, w_fo, tm):
    seq = x.shape[0]

    def body(x_ref, mix_ref, tgt_ref, vec_ref, wi_ref, wo_ref, act_ref, h2_ref, dff_ref, df_ref, dr1_ref, st_ref, gu_ref):
        @pl.when(pl.program_id(0) == 0)
        def _():
            st_ref[...] = jnp.zeros_like(st_ref)

        xh1, rstd1, xmid, xh2, rstd2 = _mid_recompute(x_ref, mix_ref, vec_ref)
        h2 = (xh2 * (1.0 + vec_ref[4:5, :]) + vec_ref[3:4, :]).astype(BF16)
        h2_ref[...] = h2
        halves = [(slice(hh * FH_SHARD, (hh + 1) * FH_SHARD), slice(FH + hh * FH_SHARD, FH + (hh + 1) * FH_SHARD))
                  for hh in range(2)]
        for hh, (cs, cu) in enumerate(halves):
            g = _dot(h2, wi_ref[hh])
            u = _dot(h2, wi_ref[2 + hh])
            gu_ref[:, cs] = g
            gu_ref[:, cu] = u
            act_ref[:, cs] = (g * _sig(g) * u).astype(BF16)
        f = _dot(act_ref[...], wo_ref[...])
        r2 = ALPHA * xmid + vec_ref[5:6, :] * f
        yh, rstd = _ln(r2)
        y = yh * vec_ref[6:7, :] + vec_ref[7:8, :]
        err = y - tgt_ref[...]
        dy = err / D
        dr2 = _ln_bwd(dy * vec_ref[6:7, :], yh, rstd)
        st_ref[0:1, :] += _colsum(err * err)
        st_ref[1:2, :] += _colsum(dy * yh)
        st_ref[2:3, :] += _colsum(dy)
        st_ref[3:4, :] += _colsum(dr2 * f)

        df = (dr2 * vec_ref[5:6, :]).astype(BF16)
        df_ref[...] = df
        da_all = _dot_nt(df, wo_ref[...])
        for cs, cu in halves:
            da = da_all[:, cs]
            g = gu_ref[:, cs]
            u = gu_ref[:, cu]
            sg = _sig(g)
            dff_ref[:, cs] = (da * u * sg * (1.0 + g * (1.0 - sg))).astype(BF16)
            dff_ref[:, cu] = (da * g * sg).astype(BF16)
        dh2 = _dot_nt(dff_ref[:, :FH_SHARD], wi_ref[0])
        for s in range(1, N_SHARD):
            dh2 = dh2 + _dot_nt(dff_ref[:, s * FH_SHARD:(s + 1) * FH_SHARD], wi_ref[s])
        dxmid = _ln_bwd(dh2 * (1.0 + vec_ref[4:5, :]), xh2, rstd2) + ALPHA * dr2
        dr1 = _ln_bwd(dxmid * vec_ref[1:2, :], xh1, rstd1)
        dr1_ref[...] = dr1
        st_ref[8:9, :] += _colsum(dh2 * xh2)
        st_ref[9:10, :] += _colsum(dh2)
        st_ref[10:11, :] += _colsum(dxmid * xh1)
        st_ref[11:12, :] += _colsum(dxmid)
        st_ref[12:13, :] += _colsum(dr1 * mix_ref[...])

    return pl.pallas_call(
        body, name="ffn", grid=(seq // tm,),
        in_specs=[_rows(tm, D), _rows(tm, D), _rows(tm, D), _acc((8, D)), _resident((N_SHARD, D, FH_SHARD)), _resident((FH, D))],
        out_specs=[_rows(tm, FH), _rows(tm, D), _rows(tm, 2 * FH), _rows(tm, D), _rows(tm, D), _acc((16, D))],
        out_shape=[jax.ShapeDtypeStruct((seq, FH), BF16), jax.ShapeDtypeStruct((seq, D), BF16),
                   jax.ShapeDtypeStruct((seq, 2 * FH), BF16), jax.ShapeDtypeStruct((seq, D), BF16),
                   jax.ShapeDtypeStruct((seq, D), F32), jax.ShapeDtypeStruct((16, D), F32)],
        scratch_shapes=[pltpu.VMEM((tm, 2 * FH), F32)],
        compiler_params=_cp("arbitrary"),
    )(x, mix, tgt, vec, w_fi, w_fo)


def _mix_bwd(dr1, a, b, gab, uv, merged, ya, yb, vec, gp, ws_stack, ws_stack_t, bias_full, w_a, w_b, w_o, tm, scatter=()):
    seq = dr1.shape[0]
    last = seq // tm - 1
    ns = len(scatter)

    def body(dr1_ref, a_ref, b_ref, gab_ref, uv_ref, mg_ref, ya_ref, yb_ref, vec_ref, gp_ref, ws_ref, wst_ref, bias_ref,
             wa_ref, wb_ref, wo_ref, *rest):
        dya_ref, dp_ref, dws_ref, dbs_ref, st_ref, gwo_ref, gwa_ref, gwb_ref = rest[ns:ns + 8]
        acc_o, acc_a, acc_b = rest[2 * ns + 8:2 * ns + 11]
        comm = _AllToAll(rest[:ns], rest[ns + 8:2 * ns + 8], *rest[2 * ns + 11:]) if ns else None
        _host_start(pl.program_id(0), comm)

        @pl.when(pl.program_id(0) == 0)
        def _():
            dws_ref[...] = jnp.zeros_like(dws_ref)
            dbs_ref[...] = jnp.zeros_like(dbs_ref)
            st_ref[...] = jnp.zeros_like(st_ref)
            acc_o[...] = jnp.zeros_like(acc_o)
            acc_a[...] = jnp.zeros_like(acc_a)
            acc_b[...] = jnp.zeros_like(acc_b)

        dmix = (dr1_ref[...] * vec_ref[0:1, :]).astype(BF16)
        acc_o[...] += _dot_tn(mg_ref[...], dmix)
        dmerged = _dot_nt(dmix, wo_ref[...])
        sa = _sig(gab_ref[:, :D].astype(F32))
        sb = _sig(gab_ref[:, D:].astype(F32))
        da_f = dmerged * sa
        db_f = dmerged * sb
        da = da_f.astype(BF16)
        db = db_f.astype(BF16)
        dp_ref[:, 2 * G_W:2 * G_W + D] = (da_f * a_ref[...].astype(F32) * (1.0 - sa)).astype(BF16)
        dp_ref[:, 2 * G_W + D:] = (db_f * b_ref[...].astype(F32) * (1.0 - sb)).astype(BF16)
        dya_ref[...] = _dot_nt(da, wa_ref[...]).astype(BF16)
        dyb = _dot_nt(db, wb_ref[...])
        acc_a[...] += _dot_tn(ya_ref[...], da)
        acc_b[...] += _dot_tn(yb_ref[...], db)

        @pl.when(pl.program_id(0) == last)
        def _():
            gwo_ref[...] = acc_o[...].astype(BF16)
            gwa_ref[...] = acc_a[...].astype(BF16)
            gwb_ref[...] = acc_b[...].astype(BF16)

        for c in range(tm // BLK):
            rs = slice(c * BLK, (c + 1) * BLK)
            u = uv_ref[rs, :G_W].astype(F32)
            vb = uv_ref[rs, G_W:].astype(F32)
            gu, tu, tv, vhat, rstd, vn, s = _gmlp_chunk(u, vb, gp_ref, ws_ref, bias_ref)
            dyb_c = dyb[rs, :]
            ds = dyb_c * gu
            du = dyb_c * s * _gelu_grad(u, tu)
            ds_b = ds.astype(BF16)
            dvn_g = []
            for g in range(N_GRP):
                cg = slice(g * GRP_D, (g + 1) * GRP_D)
                dvn_g.append(_dot(wst_ref[:, g * BLK:(g + 1) * BLK], ds_b[:, cg]))
                dws_ref[g * BLK:(g + 1) * BLK, :] += _dot_nt(ds_b[:, cg], vn[:, cg])
            dvn = jnp.concatenate(dvn_g, axis=1)
            dbs_ref[...] += ds
            st_ref[0:1, :] += _colsum(dvn * vhat)
            st_ref[1:2, :] += _colsum(dvn)
            dgv = _ln_bwd(dvn * gp_ref[0:1, :], vhat, rstd)
            dvb = dgv * _gelu_grad(vb, tv)
            dp_ref[rs, :G_W] = du.astype(BF16)
            dp_ref[rs, G_W:2 * G_W] = dvb.astype(BF16)
        _host_finish(pl.program_id(0), last, comm)

    pw = 2 * G_W + 2 * D
    out = pl.pallas_call(
        body, name="mix_bwd", grid=(seq // tm,),
        in_specs=[_rows(tm, D), _rows(tm, D), _rows(tm, D), _rows(tm, 2 * D), _rows(tm, 2 * G_W), _rows(tm, D), _rows(tm, Q_W),
                  _rows(tm, G_W), _acc((8, D)), _acc((8, G_W)),
                  _resident((N_GRP * BLK, BLK)), _resident((BLK, N_GRP * BLK)), _acc((BLK, G_W)),
                  _resident((Q_W, D)), _resident((G_W, D)), _resident((D, D))] + _comm_specs(ns),
        out_specs=[_rows(tm, Q_W), _rows(tm, pw), _acc((N_GRP * BLK, BLK)), _acc((BLK, G_W)), _acc((8, G_W)),
                   _acc((D, D)), _acc((Q_W, D)), _acc((G_W, D))] + _comm_specs(ns),
        out_shape=[jax.ShapeDtypeStruct((seq, Q_W), BF16), jax.ShapeDtypeStruct((seq, pw), BF16),
                   jax.ShapeDtypeStruct((N_GRP * BLK, BLK), F32), jax.ShapeDtypeStruct((BLK, G_W), F32),
                   jax.ShapeDtypeStruct((8, G_W), F32), jax.ShapeDtypeStruct((D, D), BF16),
                   jax.ShapeDtypeStruct((Q_W, D), BF16), jax.ShapeDtypeStruct((G_W, D), BF16)]
        + [jax.ShapeDtypeStruct(v.shape, v.dtype) for v in scatter],
        scratch_shapes=[pltpu.VMEM((D, D), F32), pltpu.VMEM((Q_W, D), F32), pltpu.VMEM((G_W, D), F32)]
        + (_comm_scratch(ns) if ns else []),
        compiler_params=_cp("arbitrary"),
    )(dr1, a, b, gab, uv, merged, ya, yb, vec, gp, ws_stack, ws_stack_t, bias_full, w_a, w_b, w_o, *scatter)
    return out[:8], out[8:]


def _attn_bwd(q, kv, kvc, sink, dya, ya, lse, scatter=()):
    seq = q.shape[0]
    nb = seq // BLK
    n_ctx = kvc.shape[0]
    ns = len(scatter)
    Q_BLOCKS = 2
    nkv = Q_BLOCKS + 2
    steps = nb // Q_BLOCKS

    def body(q_ref, *rest):
        kv_refs = rest[:nkv]
        kvc_ref, sink_ref, do_ref, o_ref, lse_ref = rest[nkv:nkv + 5]
        rest = rest[nkv + 5:]
        dq_ref, dkv_ref, dkvc_ref, dsink_ref = rest[ns:ns + 4]
        comm = _AllToAll(rest[:ns], rest[ns + 4:2 * ns + 4], *rest[2 * ns + 4:]) if ns else None
        n = pl.program_id(0)
        _host_start(n, comm)

        @pl.when(n == 0)
        def _():
            dkv_ref[...] = jnp.zeros_like(dkv_ref)
            dkvc_ref[...] = jnp.zeros_like(dkvc_ref)
            dsink_ref[...] = jnp.zeros_like(dsink_ref)

        lane = lax.broadcasted_iota(jnp.int32, (1, LANES), 1)
        for sub in range(Q_BLOCKS):
            rs = slice(sub * BLK, (sub + 1) * BLK)
            blk = Q_BLOCKS * n + sub
            q = q_ref[rs, :]
            do = do_ref[rs, :]
            out = o_ref[rs, :]
            lse_all = lse_ref[rs, :]
            k_refs = (kvc_ref,) + kv_refs[sub:sub + 3]
            masks = _band_masks(blk, nb)
            dqs, dks, dvs = [], [], []
            for hk in range(N_KV):
                q4, ks, s = _attn_scores(q, k_refs, hk, masks)
                vs = [r[:, KV_W + hk * HEAD:KV_W + (hk + 1) * HEAD] for r in k_refs]
                lse4 = jnp.concatenate([lse_all[:, hk * GROUP + g:hk * GROUP + g + 1] for g in range(GROUP)], axis=0)
                do4 = _stack_heads(do, hk)
                delta = jnp.sum(do4.astype(F32) * _stack_heads(out, hk).astype(F32), axis=-1, keepdims=True)
                p = [jnp.exp((t - lse4).astype(BF16)) for t in s]
                ds = [t * (_dot_nt(do4, v) - delta).astype(BF16) for t, v in zip(p, vs)]
                dq4 = _dot(ds[0], ks[0])
                for t, k in zip(ds[1:], ks[1:]):
                    dq4 = dq4 + _dot(t, k)
                dq4 = dq4 * SCALE
                dqs += [dq4[g * BLK:(g + 1) * BLK, :] for g in range(GROUP)]
                dks.append([_dot_tn(t, q4) for t in ds])
                dvs.append([_dot_tn(t, do4) for t in p])
                ps = jnp.exp(_sink_rows(sink_ref, hk) - lse4) * delta
                for g in range(GROUP):
                    part = -jnp.sum(ps[g * BLK:(g + 1) * BLK, :], axis=0, keepdims=True)
                    dsink_ref[0:1, :] += jnp.where(lane == hk * GROUP + g, part, 0.0)
            dq_ref[rs, :] = jnp.concatenate(dqs, axis=1)

            def piece(i):
                return jnp.concatenate([dks[0][i], dks[1][i], dvs[0][i], dvs[1][i]], axis=1)

            dkvc_ref[...] += piece(0)
            starts = (jnp.maximum(blk - 1, 0), blk, jnp.minimum(blk + 1, nb - 1))
            for i, st in enumerate(starts):
                r = pl.ds(pl.multiple_of(st * BLK, BLK), BLK)
                dkv_ref[r, :] += piece(i + 1)
        _host_finish(n, steps - 1, comm)

    tq = Q_BLOCKS * BLK
    out = pl.pallas_call(
        body, name="attn_bwd", grid=(steps,),
        in_specs=[_rows(tq, Q_W)] + _kv_specs(nb, Q_BLOCKS) + [_acc((n_ctx, 2 * KV_W)), pl.BlockSpec(memory_space=pltpu.SMEM),
                                                     _rows(tq, Q_W), _rows(tq, Q_W), _rows(tq, LANES)] + _comm_specs(ns),
        out_specs=[_rows(tq, Q_W), _acc((seq, 2 * KV_W)), _acc((n_ctx, 2 * KV_W)), _acc((8, LANES))] + _comm_specs(ns),
        out_shape=[jax.ShapeDtypeStruct((seq, Q_W), F32), jax.ShapeDtypeStruct((seq, 2 * KV_W), F32),
                   jax.ShapeDtypeStruct((n_ctx, 2 * KV_W), F32), jax.ShapeDtypeStruct((8, LANES), F32)]
        + [jax.ShapeDtypeStruct(v.shape, v.dtype) for v in scatter],
        scratch_shapes=_comm_scratch(ns) if ns else [],
        compiler_params=_cp("arbitrary"),
    )(q, *([kv] * nkv), kvc, sink, dya, ya, lse, *scatter)
    return out[:4], out[4:]


def _proj_bwd(dq, dkv, dpb, x, dr1, modx, w_in, cos, sin, tm, scatter=()):
    seq = x.shape[0]
    pw = IN_W - Q_W - 2 * KV_W
    ns = len(scatter)

    def body(dq_ref, dkv_ref, dpb_ref, x_ref, dr1_ref, mod_ref, w_ref, cos_ref, sin_ref, *rest):
        dqkv_ref, gx_ref, st_ref = rest[ns:ns + 3]
        comm = _AllToAll(rest[:ns], rest[ns + 3:2 * ns + 3], *rest[2 * ns + 3:]) if ns else None
        _host_start(pl.program_id(0), comm)

        @pl.when(pl.program_id(0) == 0)
        def _():
            st_ref[...] = jnp.zeros_like(st_ref)

        cos1, sin1 = cos_ref[...], sin_ref[...]
        cos2 = jnp.concatenate([cos1, cos1], axis=1)
        sin2 = jnp.concatenate([sin1, sin1], axis=1)
        for j in range(Q_W // 256):
            cs = slice(256 * j, 256 * (j + 1))
            dqkv_ref[:, cs] = _unrope(dq_ref[:, cs], cos2, sin2).astype(BF16)
        dqkv_ref[:, Q_W:Q_W + KV_W] = _unrope(dkv_ref[:, :KV_W], cos1, sin1).astype(BF16)
        dqkv_ref[:, Q_W + KV_W:] = dkv_ref[:, KV_W:].astype(BF16)
        o = Q_W + 2 * KV_W
        dh = _dot(dqkv_ref[...], w_ref[:o, :]) + _dot(dpb_ref[...], w_ref[o:, :])
        xhat, rstd = _ln(x_ref[...])
        st_ref[0:1, :] += _colsum(dh)
        st_ref[1:2, :] += _colsum(dh * xhat)
        gx_ref[...] = _ln_bwd(dh * (1.0 + mod_ref[1:2, :]), xhat, rstd) + ALPHA * dr1_ref[...]
        _host_finish(pl.program_id(0), seq // tm - 1, comm)

    out = pl.pallas_call(
        body, name="proj_bwd", grid=(seq // tm,),
        in_specs=[_rows(tm, Q_W), _rows(tm, 2 * KV_W), _rows(tm, pw), _rows(tm, D), _rows(tm, D), _acc((8, D)),
                  _resident((IN_W, D)), _rows(tm, LANES), _rows(tm, LANES)] + _comm_specs(ns),
        out_specs=[_rows(tm, Q_W + 2 * KV_W), _rows(tm, D), _acc((8, D))] + _comm_specs(ns),
        out_shape=[jax.ShapeDtypeStruct((seq, Q_W + 2 * KV_W), BF16), jax.ShapeDtypeStruct((seq, D), F32),
                   jax.ShapeDtypeStruct((8, D), F32)] + [jax.ShapeDtypeStruct(v.shape, v.dtype) for v in scatter],
        scratch_shapes=_comm_scratch(ns) if ns else [],
        compiler_params=_cp("arbitrary"),
    )(dq, dkv, dpb, x, dr1, modx, w_in, cos, sin, *scatter)
    return out[:3], out[3:]


def _ctx_bwd(dkvc, ctx, hc, w_in):
    n_ctx = ctx.shape[0]

    def body(dkvc_ref, ctx_ref, hc_ref, w_ref, dw_ref, st_ref):
        d = dkvc_ref[...].astype(BF16)
        dw_ref[...] = _dot_tn(d, hc_ref[...])
        dhc = _dot(d, w_ref[...])
        xhat, _ = _ln(ctx_ref[...])
        st_ref[...] = jnp.zeros_like(st_ref)
        st_ref[0:1, :] = _colsum(dhc)
        st_ref[1:2, :] = _colsum(dhc * xhat)

    return pl.pallas_call(
        body, name="ctx_bwd", grid=(1,),
        in_specs=[_acc((n_ctx, 2 * KV_W)), _acc((n_ctx, D)), _acc((n_ctx, D)), _kv_rows_of_w_in()],
        out_specs=[_acc((2 * KV_W, D)), _acc((8, D))],
        out_shape=[jax.ShapeDtypeStruct((2 * KV_W, D), F32), jax.ShapeDtypeStruct((8, D), F32)],
        compiler_params=_cp("arbitrary"),
    )(dkvc, ctx, hc, w_in)


def _tn_matmul(a, b, tn, name, out_dtype, shard_major=False, init=None, tk=512, out_rows=None, into=None,
               scatter=(), gather=()):
    t, ka = a.shape
    n = b.shape[1]
    tk = min(tk, t)
    nk = t // tk
    nj = n // tn
    has_init = init is not None
    in_place = into is not None
    assert not (scatter and gather) and not (in_place and (nj != 1 or shard_major or out_rows))
    moved = list(scatter) + list(gather)
    pattern = _AllToAll if scatter else _Gather
    ns = len(moved)
    n_in = 2 + has_init + in_place
    n_scr = 3 if in_place else 1

    def body(*refs):
        a_ref, b_ref = refs[:2]
        i_ref = refs[2] if has_init else None
        rest = refs[n_in:]
        o_ref = rest[ns]
        acc_ref = rest[2 * ns + 1]
        comm = pattern(rest[:ns], rest[ns + 1:2 * ns + 1], *rest[2 * ns + 1 + n_scr:]) if ns else None
        k = pl.program_id(1)
        step = pl.program_id(0) * nk + k
        _host_start(step, comm)

        @pl.when(k == 0)
        def _():
            acc_ref[...] = jnp.zeros_like(acc_ref)
            if has_init:
                acc_ref[ka - init.shape[0]:, :] = i_ref[...]

        acc_ref[...] += _dot_tn(a_ref[...], b_ref[...])

        @pl.when(k == nk - 1)
        def _():
            if in_place:
                stage_ref, sem = rest[2 * ns + 2:2 * ns + 4]
                stage_ref[...] = acc_ref[...].astype(out_dtype)
                write = pltpu.make_async_copy(stage_ref, o_ref.at[pl.ds(into[1], ka), :], sem.at[0])
                write.start()
                write.wait()
            else:
                o_ref[...] = acc_ref[...].astype(out_dtype)

        _host_finish(step, nj * nk - 1, comm, forward_at=(3 * nj * nk) // 4 if gather else None)

    in_specs = [pl.BlockSpec((tk, ka), lambda j, k: (k, 0)), pl.BlockSpec((tk, tn), lambda j, k: (k, j))]
    args = [a, b]
    if has_init:
        in_specs.append(pl.BlockSpec((init.shape[0], tn), lambda j, k: (0, j)))
        args.append(init)
    scratch = [pltpu.VMEM((ka, tn), F32)]
    aliases = {}
    if in_place:
        in_specs.append(pl.BlockSpec(memory_space=pl.ANY))
        args.append(into[0])
        aliases = {n_in - 1: 0}
        out_spec = pl.BlockSpec(memory_space=pl.ANY)
        out_shape = jax.ShapeDtypeStruct(into[0].shape, into[0].dtype)
        scratch += [pltpu.VMEM((ka, tn), out_dtype), pltpu.SemaphoreType.DMA((1,))]
    elif shard_major:
        out_spec = pl.BlockSpec((None, ka, tn), lambda j, k: (j, 0, 0))
        out_shape = jax.ShapeDtypeStruct((nj, ka, tn), out_dtype)
    else:
        out_spec = pl.BlockSpec((ka, tn), lambda j, k: (0, j))
        out_shape = jax.ShapeDtypeStruct((out_rows or ka, n), out_dtype)
    out = pl.pallas_call(
        body, name=name, grid=(nj, nk), in_specs=in_specs + _comm_specs(ns), out_specs=[out_spec] + _comm_specs(ns),
        out_shape=[out_shape] + [jax.ShapeDtypeStruct(v.shape, v.dtype) for v in scatter] + _gathered_shapes(gather),
        scratch_shapes=scratch + (_comm_scratch(ns) if ns else []), input_output_aliases=aliases,
        compiler_params=_cp("arbitrary", "arbitrary"),
    )(*args, *moved)
    return (out[0], out[1:]) if ns else out[0]


ADA_TILE = 512


def _gather_and_modulate(c_rows, c_ctx, w_half, w_ada):
    cs = w_ada.shape[1]
    vmem = pl.BlockSpec(memory_space=pltpu.VMEM)

    def body(c_ref, cctx_ref, wh_ref, wada_ref, cg_ref, wg_ref, mod_ref, modg_ref, sc_ref,
             c_v, wada_v, mod_v, send_c, recv_c, send_w, recv_w, send_m, recv_m, local):
        gc = _Gather([c_ref], [cg_ref], send_c, recv_c)
        gw = _Gather([wh_ref], [wg_ref], send_w, recv_w)
        gm = _Gather([mod_ref], [modg_ref], send_m, recv_m)
        px, py, pc = _my_pos()
        mine = 4 * px + 2 * py + pc
        gc.start()
        gw.start()
        own_c = pltpu.make_async_copy(c_ref, cg_ref.at[mine], local.at[0])
        load_w = pltpu.make_async_copy(wada_ref, wada_v, local.at[1])
        own_c.start()
        load_w.start()
        gc.finish()
        own_c.wait()
        load_c = pltpu.make_async_copy(cg_ref, c_v, local.at[2])
        load_c.start()
        load_c.wait()
        cc = jnp.concatenate([c_v[i, 0:1, :] for i in range(N_DEV)] + [cctx_ref[...], jnp.zeros((7, D), F32)], axis=0)
        sc = cc * _sig(cc)
        sc_ref[...] = sc
        load_w.wait()
        for j in range(cs // ADA_TILE):
            cols = slice(j * ADA_TILE, (j + 1) * ADA_TILE)
            mod_v[:, cols] = _dot(sc.astype(BF16), wada_v[:, cols].astype(BF16))
        store_m = pltpu.make_async_copy(mod_v, mod_ref, local.at[3])
        store_m.start()
        store_m.wait()
        own_m = pltpu.make_async_copy(mod_ref, modg_ref.at[mine], local.at[4])
        own_m.start()
        gm.start()
        gm.finish()
        own_m.wait()
        gw.finish()

    any_ = pl.BlockSpec(memory_space=pl.ANY)
    return pl.pallas_call(
        body, name="gather_and_modulate",
        in_specs=[any_, vmem, any_, any_], out_specs=[any_, any_, any_, any_, vmem],
        scratch_shapes=[pltpu.VMEM((N_DEV, 8, D), F32), pltpu.VMEM((D, cs), F32), pltpu.VMEM((16, cs), F32)]
        + _comm_scratch(1) + _comm_scratch(1) + _comm_scratch(1) + [pltpu.SemaphoreType.DMA((5,))],
        out_shape=[jax.ShapeDtypeStruct((N_DEV, 8, D), F32), jax.ShapeDtypeStruct((N_DEV,) + w_half.shape, w_half.dtype),
                   jax.ShapeDtypeStruct((16, cs), F32), jax.ShapeDtypeStruct((N_DEV, 16, cs), F32),
                   jax.ShapeDtypeStruct((16, D), F32)],
        compiler_params=pltpu.CompilerParams(vmem_limit_bytes=VMEM_LIMIT),
    )(c_rows, c_ctx, w_half, w_ada)


def _ada_bwd(sc_all_t, dm_all, dmc, w_ada, m, v):
    cs = w_ada.shape[1]

    def body(st_ref, dm_ref, dmc_ref, w_ref, m_ref, v_ref, gw_ref, d_ref, nm_ref, nv_ref, part_ref):
        @pl.when(pl.program_id(0) == 0)
        def _():
            part_ref[...] = jnp.zeros_like(part_ref)

        g = _dot(st_ref[...].astype(BF16), dm_ref[...].astype(BF16))
        gw_ref[...] = g
        d_ref[...], nm_ref[...], nv_ref[...] = _adam_update(w_ref[...], g, m_ref[...], v_ref[...])
        part_ref[...] += _dot_nt(dmc_ref[...].astype(BF16), w_ref[...].astype(BF16))

    cols = pl.BlockSpec((D, ADA_TILE), lambda j: (0, j))
    shp = jax.ShapeDtypeStruct((D, cs), F32)
    return pl.pallas_call(
        body, name="ada_bwd_adamw", grid=(cs // ADA_TILE,),
        in_specs=[_acc((D, 16)), pl.BlockSpec((16, ADA_TILE), lambda j: (0, j)), pl.BlockSpec((8, ADA_TILE), lambda j: (0, j)),
                  cols, cols, cols],
        out_specs=[cols] * 4 + [_acc((8, D))],
        out_shape=[shp] * 4 + [jax.ShapeDtypeStruct((8, D), F32)],
        compiler_params=_cp("arbitrary"),
    )(sc_all_t, dm_all, dmc, w_ada, m, v)


def _sum8(x, name, tr=256):
    _, r, c = x.shape
    tr = min(tr, r)
    while r % tr:
        tr -= 16

    def body(x_ref, o_ref):
        acc = x_ref[0].astype(F32)
        for i in range(1, N_DEV):
            acc = acc + x_ref[i].astype(F32)
        o_ref[...] = acc

    return pl.pallas_call(
        body, name=name, grid=(r // tr,),
        in_specs=[pl.BlockSpec((N_DEV, tr, c), lambda i: (0, i, 0))],
        out_specs=pl.BlockSpec((tr, c), lambda i: (i, 0)),
        out_shape=jax.ShapeDtypeStruct((r, c), F32),
        compiler_params=_cp("arbitrary"),
    )(x)


def _sum_blocks(recv, src, me, name, tr=256):
    _, r, c = recv.shape
    tr = min(tr, r)
    while r % tr:
        tr -= 16

    def body(me_ref, recv_ref, own_ref, o_ref):
        acc = own_ref[...].astype(F32)
        for k in range(1, N_DEV):
            acc = acc + recv_ref[me_ref[0] ^ k].astype(F32)
        o_ref[...] = acc

    return pl.pallas_call(
        body, name=name,
        grid_spec=pltpu.PrefetchScalarGridSpec(
            num_scalar_prefetch=1, grid=(r // tr,),
            in_specs=[pl.BlockSpec((N_DEV, tr, c), lambda i, me_ref: (0, i, 0)),
                      pl.BlockSpec((None, tr, c), lambda i, me_ref: (me_ref[0], i, 0))],
            out_specs=pl.BlockSpec((tr, c), lambda i, me_ref: (i, 0))),
        out_shape=jax.ShapeDtypeStruct((r, c), F32),
        compiler_params=_cp("arbitrary"),
    )(me, recv, src)


def _sum8_many(xs, name):
    n = len(xs)

    def body(*refs):
        for x_ref, o_ref in zip(refs[:n], refs[n:]):
            acc = x_ref[0]
            for i in range(1, N_DEV):
                acc = acc + x_ref[i]
            o_ref[...] = acc

    vmem = pl.BlockSpec(memory_space=pltpu.VMEM)
    return pl.pallas_call(
        body, name=name, in_specs=[vmem] * n, out_specs=[vmem] * n,
        out_shape=[jax.ShapeDtypeStruct(v.shape[1:], v.dtype) for v in xs],
        compiler_params=pltpu.CompilerParams(vmem_limit_bytes=VMEM_LIMIT),
    )(*xs)


def _adam_update(w, g, m, v):
    nm = ADAM_B1 * m + (1.0 - ADAM_B1) * g
    nv = ADAM_B2 * v + (1.0 - ADAM_B2) * (g * g)
    m_hat = nm / (1.0 - ADAM_B1 ** ADAM_STEP)
    v_hat = nv / (1.0 - ADAM_B2 ** ADAM_STEP)
    return -ADAM_LR * (m_hat / (jnp.sqrt(v_hat) + ADAM_EPS) + ADAM_WD * w), nm, nv


ROW_LOSS, ROW_LN2_G, ROW_LN2_B, ROW_LN1_G, ROW_LN1_B = 0, 1, 2, 10, 11
ROWS_DMOD_X = (16, 17, 12, 9, 8, 3)
ROWS_DMOD_C = (24, 25)
SMALL = ("c_ctx", "b_ada", "attn_sink", "gmlp_ln_g", "gmlp_ln_b", "w_spatial", "b_spatial", "ln1_g", "ln1_b", "ln2_g", "ln2_b")


def _adamw_small(sums, dsc, w, m, v):
    n = len(SMALL)

    def body(*refs):
        st_ref, gm_ref, sk_ref, ws_ref, bs_ref, dsc_ref = refs[:6]
        w_refs = dict(zip(SMALL, refs[6:6 + n]))
        m_refs = dict(zip(SMALL, refs[6 + n:6 + 2 * n]))
        v_refs = dict(zip(SMALL, refs[6 + 2 * n:6 + 3 * n]))
        outs = refs[6 + 3 * n:]
        c = w_refs["c_ctx"][...]
        sg = _sig(c)
        dmod = [st_ref[r:r + 1, :] for r in ROWS_DMOD_X]
        dmod[0] = dmod[0] + st_ref[ROWS_DMOD_C[0]:ROWS_DMOD_C[0] + 1, :]
        dmod[1] = dmod[1] + st_ref[ROWS_DMOD_C[1]:ROWS_DMOD_C[1] + 1, :]
        grads = dict(
            c_ctx=dsc_ref[0:1, :] * (sg * (1.0 + c * (1.0 - sg))),
            b_ada=jnp.concatenate(dmod, axis=1),
            attn_sink=sk_ref[0:1, 0:N_KV * GROUP],
            gmlp_ln_g=gm_ref[0:1, :], gmlp_ln_b=gm_ref[1:2, :],
            w_spatial=ws_ref[...], b_spatial=bs_ref[...],
            ln1_g=st_ref[ROW_LN1_G:ROW_LN1_G + 1, :], ln1_b=st_ref[ROW_LN1_B:ROW_LN1_B + 1, :],
            ln2_g=st_ref[ROW_LN2_G:ROW_LN2_G + 1, :], ln2_b=st_ref[ROW_LN2_B:ROW_LN2_B + 1, :])
        for i, name in enumerate(SMALL):
            g = grads[name]
            d, nm, nv = _adam_update(w_refs[name][...], g, m_refs[name][...], v_refs[name][...])
            outs[i][...] = g
            outs[n + i][...] = d
            outs[2 * n + i][...] = nm
            outs[3 * n + i][...] = nv

    vmem = pl.BlockSpec(memory_space=pltpu.VMEM)
    args = list(sums) + [dsc] + [w[k] for k in SMALL] + [m[k] for k in SMALL] + [v[k] for k in SMALL]
    shapes = [jax.ShapeDtypeStruct(w[k].shape, F32) for k in SMALL]
    out = pl.pallas_call(
        body, name="adamw_small", in_specs=[vmem] * len(args), out_specs=[vmem] * (4 * n), out_shape=shapes * 4,
        compiler_params=pltpu.CompilerParams(vmem_limit_bytes=VMEM_LIMIT),
    )(*args)
    return [dict(zip(SMALL, out[i * n:(i + 1) * n])) for i in range(4)]


def _adamw_halves(w, mine, theirs, m, v, c_arr, name):
    r, c = w.shape
    tr = min(256, r // 2)
    while (r // 2) % tr:
        tr -= 8
    nt = (r // 2) // tr

    def body(c_ref, w_ref, mine_ref, theirs_ref, m_ref, v_ref, g_ref, d_ref, nm_ref, nv_ref):
        g = jnp.where(pl.program_id(0) == c_ref[0], mine_ref[...], theirs_ref[...])
        g_ref[...] = g
        d_ref[...], nm_ref[...], nv_ref[...] = _adam_update(w_ref[...], g, m_ref[...], v_ref[...])

    whole = pl.BlockSpec((tr, c), lambda hb, i, c_ref: (hb * nt + i, 0))
    mine_spec = pl.BlockSpec((tr, c), lambda hb, i, c_ref: (jnp.where(hb == c_ref[0], i, 0), 0))
    theirs_spec = pl.BlockSpec((tr, c), lambda hb, i, c_ref: (jnp.where(hb == c_ref[0], 0, i), 0))
    shp = jax.ShapeDtypeStruct((r, c), F32)
    return pl.pallas_call(
        body, name=name,
        grid_spec=pltpu.PrefetchScalarGridSpec(
            num_scalar_prefetch=1, grid=(2, nt), in_specs=[whole, mine_spec, theirs_spec, whole, whole],
            out_specs=[whole] * 4),
        out_shape=[shp] * 4,
        compiler_params=_cp("arbitrary", "arbitrary"),
    )(c_arr, w, mine, theirs, m, v)


def _my_pos():
    return lax.axis_index("x"), lax.axis_index("y"), lax.axis_index("c")


N_COPY = 7


class _Gather:
    def __init__(self, x_refs, out_refs, send_sems, recv_sems):
        self.x_refs, self.out_refs = x_refs, out_refs
        self.send_sems, self.recv_sems = send_sems, recv_sems
        x, y, c = _my_pos()
        self.c = c
        self.me, self.sibling = (x, y, c), (x, y, 1 - c)
        self.chips = [(1 - x, y), (x, 1 - y), (1 - x, 1 - y)]

    def _copy(self, a, k, block, to, from_input=False):
        px, py, pc = block
        rows = self.out_refs[a].at[4 * px + 2 * py + pc]
        return pltpu.make_async_remote_copy(
            src_ref=self.x_refs[a] if from_input else rows, dst_ref=rows,
            send_sem=self.send_sems.at[a * N_COPY + k], recv_sem=self.recv_sems.at[a * N_COPY + k],
            device_id=to, device_id_type=MESH)

    def start(self):
        n = len(self.x_refs)
        for a in range(n):
            self._copy(a, 0, self.me, self.sibling, from_input=True).start()
        for j, chip in enumerate(self.chips):
            for a in range(n):
                self._copy(a, 1 + j, self.me, (*chip, self.c), from_input=True).start()

    def forward(self):
        c = self.c
        for j, chip in enumerate(self.chips):
            for a in range(len(self.x_refs)):
                self._copy(a, 1 + j, (*chip, c), self.me).wait_recv()
                self._copy(a, 4 + j, (*chip, c), self.sibling).start()

    def finish(self):
        self.forward()
        self.drain()

    def drain(self):
        n = len(self.x_refs)
        c = self.c
        for a in range(n):
            self._copy(a, 0, self.sibling, self.me).wait_recv()
        for j, chip in enumerate(self.chips):
            for a in range(n):
                self._copy(a, 4 + j, (*chip, 1 - c), self.me).wait_recv()
        for a in range(n):
            self._copy(a, 0, self.me, self.sibling, from_input=True).wait_send()
            for j, chip in enumerate(self.chips):
                self._copy(a, 1 + j, self.me, (*chip, c), from_input=True).wait_send()
                self._copy(a, 4 + j, (*chip, c), self.sibling).wait_send()


def _comm_scratch(n):
    return [pltpu.SemaphoreType.DMA((n * N_COPY,)), pltpu.SemaphoreType.DMA((n * N_COPY,))]


def _comm_specs(n):
    return [pl.BlockSpec(memory_space=pl.ANY)] * n


def _gathered_shapes(xs):
    return [jax.ShapeDtypeStruct((N_DEV,) + v.shape, v.dtype) for v in xs]


def _with_own(gathered, xs, me):
    return [lax.dynamic_update_index_in_dim(g, v, me, 0) for g, v in zip(gathered, xs)]


class _AllToAll:
    def __init__(self, x_refs, out_refs, send_sems, recv_sems):
        self.x_refs, self.out_refs = x_refs, out_refs
        self.send_sems, self.recv_sems = send_sems, recv_sems
        self.pos = _my_pos()
        x, y, c = self.pos
        self.me = 4 * x + 2 * y + c

    def _peer(self, k):
        x, y, c = self.pos
        return (x ^ ((k >> 2) & 1), y ^ ((k >> 1) & 1), c ^ (k & 1))

    def _copy(self, a, k):
        p = self._peer(k)
        return pltpu.make_async_remote_copy(
            src_ref=self.x_refs[a].at[4 * p[0] + 2 * p[1] + p[2]], dst_ref=self.out_refs[a].at[self.me],
            send_sem=self.send_sems.at[a * N_COPY + k - 1], recv_sem=self.recv_sems.at[a * N_COPY + k - 1],
            device_id=p, device_id_type=MESH)

    def start(self):
        for k in range(1, N_DEV):
            for a in range(len(self.x_refs)):
                self._copy(a, k).start()

    def finish(self):
        for a in range(len(self.x_refs)):
            for k in range(1, N_DEV):
                self._copy(a, k).wait_recv()
            for k in range(1, N_DEV):
                self._copy(a, k).wait_send()


def _exchange(name, scatter=(), gather=(), sibling=()):
    ns, ng, nx = len(scatter), len(gather), len(sibling)
    n = ns + ng + nx

    def body(*refs):
        ins, outs, sems = refs[:n], refs[n:2 * n], list(refs[2 * n:])
        t = _AllToAll(ins[:ns], outs[:ns], sems.pop(0), sems.pop(0)) if ns else None
        g = _Gather(ins[ns:ns + ng], outs[ns:ns + ng], sems.pop(0), sems.pop(0)) if ng else None
        x, y, c = _my_pos()

        def push(a):
            return pltpu.make_async_remote_copy(
                src_ref=ins[ns + ng + a], dst_ref=outs[ns + ng + a], send_sem=sems[0].at[a], recv_sem=sems[1].at[a],
                device_id=(x, y, 1 - c), device_id_type=MESH)

        for comm in (g, t):
            if comm is not None:
                comm.start()
        for a in range(nx):
            push(a).start()
        for comm in (g, t):
            if comm is not None:
                comm.finish()
        for a in range(nx):
            push(a).wait_recv()
            push(a).wait_send()

    scratch = (_comm_scratch(ns) if ns else []) + (_comm_scratch(ng) if ng else [])
    scratch += [pltpu.SemaphoreType.DMA((nx,)), pltpu.SemaphoreType.DMA((nx,))] if nx else []
    out = pl.pallas_call(
        body, name=name,
        out_shape=[jax.ShapeDtypeStruct(v.shape, v.dtype) for v in scatter] + _gathered_shapes(gather)
        + [jax.ShapeDtypeStruct(v.shape, v.dtype) for v in sibling],
        in_specs=_comm_specs(n), out_specs=_comm_specs(n), scratch_shapes=scratch,
    )(*scatter, *gather, *sibling)
    return out[:ns], out[ns:ns + ng], out[ns + ng:]


def _row_tile(seq, want):
    return min(want, seq)


def _local_step(x, ctx, tgt, mod_x, mod_c, wb, sink, gmlp_g, gmlp_b, w_s, b_s, ln1_g, ln1_b, ln2_g, ln2_b,
                later=None, me=None):
    seq = x.shape[0]
    on_mesh = me is not None
    modx1 = jnp.concatenate([mod_x[0:2], jnp.zeros((6, D), F32)], axis=0)
    modc = jnp.concatenate([mod_c[0:2], jnp.zeros((6, D), F32)], axis=0)
    vec = jnp.concatenate([mod_x[2:3], ln1_g, ln1_b, mod_x[3:6], ln2_g, ln2_b], axis=0)
    gp = jnp.concatenate([gmlp_g, gmlp_b, jnp.zeros((6, G_W), F32)], axis=0)
    ws_stack = w_s.reshape(N_GRP * BLK, BLK).astype(BF16)
    ws_stack_t = jnp.transpose(w_s, (2, 0, 1)).reshape(BLK, N_GRP * BLK).astype(BF16)
    bias_full = jnp.repeat(b_s.T, GRP_D, axis=1)
    cos, sin = _rope_tables(seq)
    w_in = wb["w_in"]
    tm_big = _row_tile(seq, 512)
    tm_ffn = _row_tile(seq, 256)

    hc, kvc, vac = _ctx_fwd(ctx, modc, w_in)
    behind_proj = ("w_a", "w_b", "w_o") if on_mesh else ()
    behind_attn = ("w_fi",) if on_mesh else ()
    behind_mix = ("w_fo",) if on_mesh else ()
    wb = dict(wb)

    def whole(names, gathered):
        for n, g in zip(names, _with_own(list(gathered), [later[n] for n in names], me)):
            wb[n] = g.reshape(-1, g.shape[2]) if n in ROW_SHARDED else g.reshape(N_SHARD, 2 * g.shape[1], g.shape[2])

    (h, q, kv, va, uv, gab), got = _proj_fwd(x, modx1, w_in, cos, sin, _row_tile(seq, 1024), gather=[later[n] for n in behind_proj])
    whole(behind_proj, got)
    if on_mesh:
        for n in ("w_a", "w_b"):
            wb[n] = wb[n].transpose(1, 0, 2).reshape(wb[n].shape[1], D)
    (ya, lse), got = _attn_fwd(q, kv, va, kvc, vac, sink, gather=[later[n] for n in behind_attn])
    whole(behind_attn, got)
    (a, b, mix, merged, yb), got = _mix_fwd(uv, gab, ya, gp, ws_stack, bias_full, wb["w_a"], wb["w_b"], wb["w_o"], tm_big,
                                            gather=[later[n] for n in behind_mix])
    whole(behind_mix, got)
    act, h2, dff, df, dr1, st_ffn = _ffn(x, mix, tgt, vec, wb["w_fi"], wb["w_fo"], tm_ffn)
    blocks, recv = {}, {}
    blocks["w_fo"] = _eighths(_tn_matmul(act, df, 512, "tn_w_ffn_out", BF16, tk=2048))
    if on_mesh:
        g_w_fi, (recv["w_fo"],) = _tn_matmul(h2, dff, FH_SHARD, "tn_w_ffn_in", BF16, shard_major=True, tk=2048,
                                            scatter=[blocks["w_fo"]])
    else:
        g_w_fi = _tn_matmul(h2, dff, FH_SHARD, "tn_w_ffn_in", BF16, shard_major=True, tk=2048)
    blocks["w_fi"] = _eighths(g_w_fi)
    (dya, dpb, dws, dbs_full, st4, g_w_o, g_w_a, g_w_b), got = _mix_bwd(
        dr1, a, b, gab, uv, merged, ya, yb, vec, gp, ws_stack, ws_stack_t, bias_full, wb["w_a"], wb["w_b"], wb["w_o"],
        tm_big, scatter=[blocks["w_fi"]] if on_mesh else ())
    recv.update(zip(("w_fi",), got))
    shard_major = [g.reshape(g.shape[0], N_SHARD, D // N_SHARD).transpose(1, 0, 2) for g in (g_w_a, g_w_b)]
    blocks.update(w_o=_eighths(g_w_o), w_a=_eighths(shard_major[0]), w_b=_eighths(shard_major[1]))
    mixer = ("w_o", "w_a", "w_b") if on_mesh else ()
    (dq, dkv, dkvc, dsink), got = _attn_bwd(q, kv, kvc, sink, dya, ya, lse, scatter=[blocks[n] for n in mixer])
    recv.update(zip(mixer, got))
    g_wkv_ctx, st0 = _ctx_bwd(dkvc, ctx, hc, w_in)
    (dqkv, grad_x, st1), _ = _proj_bwd(dq, dkv, dpb, x, dr1, modx1, w_in, cos, sin, tm_big)
    dbs = jnp.sum(dbs_full.reshape(BLK, N_GRP, GRP_D), axis=2).T
    early = [jnp.concatenate([st_ffn, st0], axis=0), st4, dsink, dws, dbs]
    g_in = _tn_matmul(dqkv, h, D, "tn_w_in_qkv", BF16, init=g_wkv_ctx, tk=1024, out_rows=IN_W)
    into = (g_in, dqkv.shape[1])
    if on_mesh:
        g_in, early_gathered = _tn_matmul(dpb, h, D, "tn_w_in_rest", BF16, tk=1024, into=into, gather=early)
    else:
        g_in, early_gathered = _tn_matmul(dpb, h, D, "tn_w_in_rest", BF16, tk=1024, into=into), None
    blocks["w_in"] = _eighths(g_in)
    return grad_x, dict(early=early, early_gathered=early_gathered, late=st1), blocks, recv


BIG = ("w_in", "w_a", "w_b", "w_o", "w_fi", "w_fo")
ROW_SHARDED = ("w_o", "w_fo")


def _half_of_shard(shard, c):
    r = shard.shape[0]
    return lax.dynamic_slice_in_dim(shard, c * (r // 2), r // 2, axis=0)


def _eighths(v):
    rows = v.shape[-2] * (v.shape[0] if v.ndim == 3 else 1)
    return v.reshape(N_DEV, rows // N_DEV, v.shape[-1])


def kernel(x, c, ctx, c_ctx, w_ada, b_ada, w_in, attn_sink, gmlp_ln_g, gmlp_ln_b, w_spatial, b_spatial, w_branch_a, w_branch_b, w_out, ln1_g, ln1_b, w_ffn_in, w_ffn_out, ln2_g, ln2_b, loss_target, m_c_ctx, m_w_ada, m_b_ada, m_w_in, m_attn_sink, m_gmlp_ln_g, m_gmlp_ln_b, m_w_spatial, m_b_spatial, m_w_branch_a, m_w_branch_b, m_w_out, m_ln1_g, m_ln1_b, m_w_ffn_in, m_w_ffn_out, m_ln2_g, m_ln2_b, v_c_ctx, v_w_ada, v_b_ada, v_w_in, v_attn_sink, v_gmlp_ln_g, v_gmlp_ln_b, v_w_spatial, v_b_spatial, v_w_branch_a, v_w_branch_b, v_w_out, v_ln1_g, v_ln1_b, v_w_ffn_in, v_w_ffn_out, v_ln2_g, v_ln2_b):
    mx, my, mc = _my_pos()
    me = 4 * mx + 2 * my + mc
    chip = 2 * mx + my
    shards = dict(w_in=w_in[0].T, w_a=w_branch_a[0], w_b=w_branch_b[0], w_o=w_out[0], w_fi=w_ffn_in[0], w_fo=w_ffn_out[0])

    halves = {n: _half_of_shard(shards[n], mc).astype(BF16) for n in BIG}
    c_rows = jnp.concatenate([c, jnp.zeros((7, D), F32)], axis=0)
    _, g_in, _, mod_g, sc_all = _gather_and_modulate(c_rows, c_ctx[None, :], halves["w_in"], w_ada[0])
    wb = dict(w_in=_with_own([g_in], [halves["w_in"]], me)[0].reshape(IN_W, D))
    mod_all = jnp.concatenate([mod_g[2 * s] for s in range(4)], axis=1) + b_ada
    mod_x = lax.dynamic_slice_in_dim(mod_all, me, 1, axis=0).reshape(6, D)
    mod_c = mod_all[8].reshape(6, D)[0:2]

    grad_x, small, blocks, recv = _local_step(
        x[0], ctx[0], loss_target[0], mod_x, mod_c, wb, attn_sink, gmlp_ln_g, gmlp_ln_b, w_spatial[0], b_spatial[0],
        ln1_g, ln1_b, ln2_g, ln2_b, later=halves, me=me)

    me_arr = jnp.reshape(me, (1,)).astype(jnp.int32)
    summed = {n: _sum_blocks(recv[n], blocks[n], me_arr, "sum_grads_" + n) for n in BIG[1:]}
    (recv["w_in"],), late, theirs = _exchange("scatter_w_in_gather_small_exchange_grads", scatter=[blocks["w_in"]],
                                              gather=[small["late"]], sibling=[summed[n] for n in BIG[1:]])
    theirs = dict(zip(BIG[1:], theirs))
    summed["w_in"] = _sum_blocks(recv["w_in"], blocks["w_in"], me_arr, "sum_grads_w_in")
    late = _with_own(late, [small["late"]], me)[0]
    gathered = _with_own(small["early_gathered"], small["early"], me)
    gathered[0] = jnp.concatenate([gathered[0][:, :16], late, gathered[0][:, 16:]], axis=1)

    sums = _sum8_many(gathered, "sum_small")
    stats = sums[0]
    loss = 0.5 * jnp.sum(stats[ROW_LOSS]) / D
    dmod_x_all = jnp.concatenate([gathered[0][:, r_, :] for r_ in ROWS_DMOD_X], axis=1)
    dmod_c_full = jnp.concatenate([stats[r_] for r_ in ROWS_DMOD_C] + [jnp.zeros((4 * D,), F32)])
    dm_rows = jnp.concatenate([dmod_x_all, dmod_c_full[None, :], jnp.zeros((7, 6 * D), F32)], axis=0)
    cs = w_ada.shape[2]
    dm_shard = lax.dynamic_slice_in_dim(dm_rows, chip * cs, cs, axis=1)
    dmc_shard = jnp.concatenate([dm_shard[8:9], jnp.zeros((7, cs), F32)], axis=0)
    ada = _ada_bwd(sc_all.T, dm_shard, dmc_shard, w_ada[0], m_w_ada[0], v_w_ada[0])
    part = ada[4]
    part = part * (mc == 0).astype(F32)
    _, part_all, (theirs["w_in"],) = _exchange("exchange_w_in_gather_c_ctx", gather=[part], sibling=[summed["w_in"]])
    dsc = _sum8(_with_own(part_all, [part], me)[0], "sum_c_ctx")

    weights = dict(c_ctx=c_ctx, w_ada=w_ada, b_ada=b_ada, w_in=w_in, attn_sink=attn_sink, gmlp_ln_g=gmlp_ln_g,
                   gmlp_ln_b=gmlp_ln_b, w_spatial=w_spatial, b_spatial=b_spatial, w_branch_a=w_branch_a,
                   w_branch_b=w_branch_b, w_out=w_out, ln1_g=ln1_g, ln1_b=ln1_b, w_ffn_in=w_ffn_in, w_ffn_out=w_ffn_out,
                   ln2_g=ln2_g, ln2_b=ln2_b)
    ms = dict(c_ctx=m_c_ctx, w_ada=m_w_ada, b_ada=m_b_ada, w_in=m_w_in, attn_sink=m_attn_sink, gmlp_ln_g=m_gmlp_ln_g,
              gmlp_ln_b=m_gmlp_ln_b, w_spatial=m_w_spatial, b_spatial=m_b_spatial, w_branch_a=m_w_branch_a,
              w_branch_b=m_w_branch_b, w_out=m_w_out, ln1_g=m_ln1_g, ln1_b=m_ln1_b, w_ffn_in=m_w_ffn_in,
              w_ffn_out=m_w_ffn_out, ln2_g=m_ln2_g, ln2_b=m_ln2_b)
    vs = dict(c_ctx=v_c_ctx, w_ada=v_w_ada, b_ada=v_b_ada, w_in=v_w_in, attn_sink=v_attn_sink, gmlp_ln_g=v_gmlp_ln_g,
              gmlp_ln_b=v_gmlp_ln_b, w_spatial=v_w_spatial, b_spatial=v_b_spatial, w_branch_a=v_w_branch_a,
              w_branch_b=v_w_branch_b, w_out=v_w_out, ln1_g=v_ln1_g, ln1_b=v_ln1_b, w_ffn_in=v_w_ffn_in,
              w_ffn_out=v_w_ffn_out, ln2_g=v_ln2_g, ln2_b=v_ln2_b)
    order = list(weights)
    grads, delta, new_m, new_v = [dict(w_ada=t[None]) for t in ada[:4]]
    c_arr = jnp.reshape(mc, (1,)).astype(jnp.int32)
    names = dict(w_in="w_in", w_a="w_branch_a", w_b="w_branch_b", w_o="w_out", w_fi="w_ffn_in", w_fo="w_ffn_out")
    for k, n in names.items():
        flip = (lambda t: t.T) if k == "w_in" else (lambda t: t)
        outs = _adamw_halves(flip(weights[n][0]), summed[k], theirs[k], flip(ms[n][0]), flip(vs[n][0]), c_arr, "adamw_" + n)
        grads[n], delta[n], new_m[n], new_v[n] = [flip(t)[None] for t in outs]

    def view(a):
        return a.reshape(-1, a.shape[-1]) if a.ndim != 1 else a.reshape(1, -1)

    small = _adamw_small(sums, dsc, *[{n: view(d[n]) for n in SMALL} for d in (weights, ms, vs)])
    for out, src in zip((grads, delta, new_m, new_v), small):
        for n in SMALL:
            out[n] = src[n].reshape(weights[n].shape)

    return (loss, grad_x[None], *[grads[n] for n in order], *[delta[n] for n in order],
            *[new_m[n] for n in order], *[new_v[n] for n in order])
```

```python
import math

import jax
import jax.numpy as jnp
import numpy as np
from jax import lax
from jax.experimental import pallas as pl
from jax.experimental.pallas import tpu as pltpu

F32 = jnp.float32
BF16 = jnp.bfloat16

D = 1024
HEAD = 64
N_KV = 2
GROUP = 4
Q_W = 512
KV_W = 128
G_W = 512
BLK = 128
N_GRP = 8
GRP_D = 64
FH = 2816
IN_W = 3840
GRID_W = 64
ROPE_BASE = 10000.0
LN_EPS = 1e-5
NEG = -1e30
ALPHA = (2 * 1) ** 0.25
SCALE = HEAD ** -0.5
GELU_K = math.sqrt(2.0 / math.pi)
GELU_A = 0.044715
ADAM_LR = 0.001
ADAM_B1 = 0.9
ADAM_B2 = 0.999
ADAM_EPS = 1e-08
ADAM_WD = 0.01
ADAM_STEP = 10
N_DEV = 8
N_SHARD = 4
FH_SHARD = FH // 2
LANES = 128
VMEM_LIMIT = 56 * 1024 * 1024
MESH = pl.DeviceIdType.MESH


def _cp(*sem):
    return pltpu.CompilerParams(dimension_semantics=sem, vmem_limit_bytes=VMEM_LIMIT)


def _resident(shape):
    return pl.BlockSpec(shape, lambda *_: (0,) * len(shape), pipeline_mode=pl.Buffered(1))


def _rows(tm, width):
    return pl.BlockSpec((tm, width), lambda i: (i, 0))


def _acc(shape):
    return pl.BlockSpec(shape, lambda *_: (0,) * len(shape))


def _dot(a, b):
    return jnp.dot(a, b, preferred_element_type=F32)


def _dot_nt(a, b):
    return lax.dot_general(a, b, (((1,), (1,)), ((), ())), preferred_element_type=F32)


def _dot_tn(a, b):
    return lax.dot_general(a, b, (((0,), (0,)), ((), ())), preferred_element_type=F32)


def _ln(x):
    mu = jnp.mean(x, axis=-1, keepdims=True)
    xc = x - mu
    var = jnp.mean(xc * xc, axis=-1, keepdims=True)
    rstd = lax.rsqrt(var + LN_EPS)
    return xc * rstd, rstd


def _ln_bwd(dxhat, xhat, rstd):
    return (dxhat - jnp.mean(dxhat, axis=-1, keepdims=True)
            - xhat * jnp.mean(dxhat * xhat, axis=-1, keepdims=True)) * rstd


def _sig(x):
    return 0.5 + 0.5 * jnp.tanh(0.5 * x)


def _gelu(x):
    t = jnp.tanh(x * (GELU_K + (GELU_K * GELU_A) * (x * x)))
    hx = 0.5 * x
    return hx + hx * t, t


def _gelu_grad(x, t):
    return 0.5 + 0.5 * t + (0.5 * x) * (1.0 - t * t) * (GELU_K + (3.0 * GELU_K * GELU_A) * (x * x))


def _colsum(v):
    return jnp.sum(v, axis=0, keepdims=True)


def _partner(x):
    w = x.shape[1]
    lane = lax.broadcasted_iota(jnp.int32, x.shape, 1)
    return jnp.where((lane & 31) < 16, pltpu.roll(x, w - 16, 1), pltpu.roll(x, 16, 1))


def _rope(x, cos, sin):
    return x * cos + _partner(x) * sin


def _unrope(g, cos, sin):
    return g * cos + _partner(g * sin)


def _rope_tables(seq):
    inv = np.float32(ROPE_BASE) ** (-np.arange(HEAD // 4, dtype=np.float32) / np.float32(HEAD // 4))
    pos = np.arange(seq)
    ar = (pos // GRID_W).astype(np.float32)[:, None] * inv
    ac = (pos % GRID_W).astype(np.float32)[:, None] * inv
    cos = np.concatenate([np.cos(ar), np.cos(ar), np.cos(ac), np.cos(ac)], axis=-1)
    sin = np.concatenate([-np.sin(ar), np.sin(ar), -np.sin(ac), np.sin(ac)], axis=-1)
    reps = (1, LANES // HEAD)
    return jnp.asarray(np.tile(cos, reps), F32), jnp.asarray(np.tile(sin, reps), F32)


def _kv_rows_of_w_in():
    return pl.BlockSpec((2 * KV_W, D), lambda *_: (Q_W // (2 * KV_W), 0))


def _ctx_fwd(ctx, modc, w_in):
    n_ctx = ctx.shape[0]

    def body(ctx_ref, mod_ref, w_ref, hc_ref, kvc_ref, vac_ref):
        xhat, _ = _ln(ctx_ref[...])
        hc = (xhat * (1.0 + mod_ref[1:2, :]) + mod_ref[0:1, :]).astype(BF16)
        hc_ref[...] = hc
        kvc = _dot_nt(hc, w_ref[...]).astype(BF16)
        kvc_ref[...] = kvc
        vac_ref[...] = _with_ones(kvc[:, KV_W:])

    return pl.pallas_call(
        body, name="ctx_fwd", grid=(1,),
        in_specs=[_acc((n_ctx, D)), _acc((8, D)), _kv_rows_of_w_in()],
        out_specs=[_acc((n_ctx, D)), _acc((n_ctx, 2 * KV_W)), _acc((n_ctx, 2 * LANES))],
        out_shape=[jax.ShapeDtypeStruct((n_ctx, D), BF16), jax.ShapeDtypeStruct((n_ctx, 2 * KV_W), BF16),
                   jax.ShapeDtypeStruct((n_ctx, 2 * LANES), BF16)],
        compiler_params=_cp("arbitrary"),
    )(ctx, modc, w_in)


def _host_start(step, comm):
    if comm is not None:
        @pl.when(step == 0)
        def _():
            comm.start()


def _host_finish(step, last, comm, forward_at=None):
    if comm is None:
        return
    if forward_at is None or forward_at >= last:
        @pl.when(step == last)
        def _():
            comm.finish()
    else:
        @pl.when(step == forward_at)
        def _():
            comm.forward()

        @pl.when(step == last)
        def _():
            comm.drain()


def _proj_fwd(x, modx, w_in, cos, sin, tm, gather=()):
    seq = x.shape[0]
    ng = len(gather)

    def body(x_ref, mod_ref, w_ref, cos_ref, sin_ref, *rest):
        h_ref, q_ref, kv_ref, va_ref, uv_ref, gab_ref = rest[ng:ng + 6]
        comm = _Gather(rest[:ng], rest[ng + 6:2 * ng + 6], *rest[2 * ng + 6:]) if ng else None
        _host_start(pl.program_id(0), comm)
        xhat, _ = _ln(x_ref[...])
        h = (xhat * (1.0 + mod_ref[1:2, :]) + mod_ref[0:1, :]).astype(BF16)
        h_ref[...] = h
        cos1, sin1 = cos_ref[...], sin_ref[...]
        cos2 = jnp.concatenate([cos1, cos1], axis=1)
        sin2 = jnp.concatenate([sin1, sin1], axis=1)
        for j in range(Q_W // 256):
            t = _dot_nt(h, w_ref[256 * j:256 * (j + 1), :])
            q_ref[:, 256 * j:256 * (j + 1)] = (_rope(t, cos2, sin2) * SCALE).astype(BF16)
        t = _dot_nt(h, w_ref[Q_W:Q_W + 2 * KV_W, :])
        kv_ref[:, :KV_W] = _rope(t[:, :KV_W], cos1, sin1).astype(BF16)
        v = t[:, KV_W:].astype(BF16)
        kv_ref[:, KV_W:] = v
        va_ref[...] = _with_ones(v)
        o = Q_W + 2 * KV_W
        for j in range(2):
            uv_ref[:, G_W * j:G_W * (j + 1)] = _dot_nt(h, w_ref[o + G_W * j:o + G_W * (j + 1), :]).astype(BF16)
        o += 2 * G_W
        for j in range(4):
            gab_ref[:, 512 * j:512 * (j + 1)] = _dot_nt(h, w_ref[o + 512 * j:o + 512 * (j + 1), :]).astype(BF16)
        _host_finish(pl.program_id(0), seq // tm - 1, comm, forward_at=(seq // tm) // 2)

    out = pl.pallas_call(
        body, name="proj_fwd", grid=(seq // tm,),
        in_specs=[_rows(tm, D), _acc((8, D)), _resident((IN_W, D)), _rows(tm, LANES), _rows(tm, LANES)] + _comm_specs(ng),
        out_specs=[_rows(tm, D), _rows(tm, Q_W), _rows(tm, 2 * KV_W), _rows(tm, 2 * LANES), _rows(tm, 2 * G_W),
                   _rows(tm, 2 * D)] + _comm_specs(ng),
        out_shape=[jax.ShapeDtypeStruct((seq, D), BF16), jax.ShapeDtypeStruct((seq, Q_W), BF16),
                   jax.ShapeDtypeStruct((seq, 2 * KV_W), BF16), jax.ShapeDtypeStruct((seq, 2 * LANES), BF16),
                   jax.ShapeDtypeStruct((seq, 2 * G_W), BF16), jax.ShapeDtypeStruct((seq, 2 * D), BF16)] + _gathered_shapes(gather),
        scratch_shapes=_comm_scratch(ng) if ng else [],
        compiler_params=_cp("arbitrary"),
    )(x, modx, w_in, cos, sin, *gather)
    return out[:6], out[6:]


def _stack_heads(x, hk):
    return jnp.concatenate([x[:, (hk * GROUP + g) * HEAD:(hk * GROUP + g + 1) * HEAD] for g in range(GROUP)], axis=0)


def _band_masks(n, nb):
    rows = GROUP * BLK
    qi = lax.broadcasted_iota(jnp.int32, (rows, BLK), 0) & (BLK - 1)
    kj = lax.broadcasted_iota(jnp.int32, (rows, BLK), 1)
    return (kj >= qi) & (n > 0), (kj <= qi) & (n < nb - 1)


def _attn_scores(q, k_refs, hk, masks):
    q4 = _stack_heads(q, hk)
    ks = [r[:, hk * HEAD:(hk + 1) * HEAD] for r in k_refs]
    s = [_dot_nt(q4, k) for k in ks]
    s[1] = jnp.where(masks[0], s[1], NEG)
    s[3] = jnp.where(masks[1], s[3], NEG)
    return q4, ks, s


def _sink_rows(sink_ref, hk):
    rows = GROUP * BLK
    rg = lax.broadcasted_iota(jnp.int32, (rows, 1), 0) >> 7
    sink_v = jnp.full((rows, 1), sink_ref[0, hk * GROUP], F32)
    for g in range(1, GROUP):
        sink_v = jnp.where(rg == g, sink_ref[0, hk * GROUP + g], sink_v)
    return sink_v


def _with_ones(v):
    ones = jnp.ones((v.shape[0], HEAD), v.dtype)
    return jnp.concatenate([v[:, :HEAD], ones, v[:, HEAD:], ones], axis=1)


def _kv_specs(nb, qb):
    def spec(d):
        return pl.BlockSpec((BLK, 2 * KV_W), lambda n: (jnp.clip(qb * n + d, 0, nb - 1), 0))
    return [spec(d) for d in range(-1, qb + 1)]


def _attn_fwd(q, kv, va, kvc, vac, sink, gather=()):
    seq = q.shape[0]
    nb = seq // BLK
    n_ctx = kvc.shape[0]
    ng = len(gather)
    Q_BLOCKS = 1
    nkv = Q_BLOCKS + 2
    steps = nb // Q_BLOCKS

    def body(q_ref, *rest):
        kv_refs, va_refs = rest[:nkv], rest[nkv:2 * nkv]
        kvc_ref, vac_ref, sink_ref = rest[2 * nkv:2 * nkv + 3]
        rest = rest[2 * nkv + 3:]
        o_ref, lse_ref = rest[ng:ng + 2]
        comm = _Gather(rest[:ng], rest[ng + 2:2 * ng + 2], *rest[2 * ng + 2:]) if ng else None
        n = pl.program_id(0)
        _host_start(n, comm)
        lane = lax.broadcasted_iota(jnp.int32, (BLK, LANES), 1)
        for sub in range(Q_BLOCKS):
            rs = slice(sub * BLK, (sub + 1) * BLK)
            q = q_ref[rs, :]
            outs = []
            lse_all = jnp.zeros((BLK, LANES), F32)
            masks = _band_masks(Q_BLOCKS * n + sub, nb)
            for hk in range(N_KV):
                _, _, s = _attn_scores(q, (kvc_ref,) + kv_refs[sub:sub + 3], hk, masks)
                sink_v = _sink_rows(sink_ref, hk)
                tile_max = s[1]
                for t in [s[0][:, i * LANES:(i + 1) * LANES] for i in range(n_ctx // LANES)] + s[2:]:
                    tile_max = jnp.maximum(tile_max, t)
                m = jnp.maximum(sink_v, jnp.max(tile_max, axis=-1, keepdims=True))
                o = jnp.zeros((GROUP * BLK, LANES), F32)
                for t, va_ref in zip(s, (vac_ref,) + va_refs[sub:sub + 3]):
                    o = o + _dot(jnp.exp((t - m).astype(BF16)), va_ref[:, hk * LANES:(hk + 1) * LANES])
                denom = o[:, HEAD:HEAD + 1] + jnp.exp(sink_v - m)
                o4 = o[:, :HEAD] * (1.0 / denom)
                lse4 = m + jnp.log(denom)
                for g in range(GROUP):
                    outs.append(o4[g * BLK:(g + 1) * BLK, :])
                    lse_all = jnp.where(lane == hk * GROUP + g, lse4[g * BLK:(g + 1) * BLK, :], lse_all)
            o_ref[rs, :] = jnp.concatenate(outs, axis=1).astype(BF16)
            lse_ref[rs, :] = lse_all
        _host_finish(n, steps - 1, comm, forward_at=(3 * steps) // 4)

    tq = Q_BLOCKS * BLK
    out = pl.pallas_call(
        body, name="attn_fwd", grid=(steps,),
        in_specs=[_rows(tq, Q_W)] + _kv_specs(nb, Q_BLOCKS) + _kv_specs(nb, Q_BLOCKS)
        + [_acc((n_ctx, 2 * KV_W)), _acc((n_ctx, 2 * LANES)), pl.BlockSpec(memory_space=pltpu.SMEM)] + _comm_specs(ng),
        out_specs=[_rows(tq, Q_W), _rows(tq, LANES)] + _comm_specs(ng),
        out_shape=[jax.ShapeDtypeStruct((seq, Q_W), BF16), jax.ShapeDtypeStruct((seq, LANES), F32)] + _gathered_shapes(gather),
        scratch_shapes=_comm_scratch(ng) if ng else [],
        compiler_params=_cp("arbitrary"),
    )(q, *([kv] * nkv), *([va] * nkv), kvc, vac, sink, *gather)
    return out[:2], out[2:]


def _gmlp_chunk(u, vb, gp_ref, ws_ref, bias_ref):
    gu, tu = _gelu(u)
    gv, tv = _gelu(vb)
    vhat, rstd = _ln(gv)
    vn = (vhat * gp_ref[0:1, :] + gp_ref[1:2, :]).astype(BF16)
    s = bias_ref[...] + jnp.concatenate(
        [_dot(ws_ref[g * BLK:(g + 1) * BLK, :], vn[:, g * GRP_D:(g + 1) * GRP_D]) for g in range(N_GRP)], axis=1)
    return gu, tu, tv, vhat, rstd, vn, s


def _mix_fwd(uv, gab, ya, gp, ws_stack, bias_full, w_a, w_b, w_o, tm, gather=()):
    seq = uv.shape[0]
    ng = len(gather)
    steps = seq // tm

    def body(uv_ref, gab_ref, ya_ref, gp_ref, ws_ref, bias_ref, wa_ref, wb_ref, wo_ref, *rest):
        a_ref, b_ref, mix_ref, merged_ref, yb_ref = rest[ng:ng + 5]
        comm = _Gather(rest[:ng], rest[ng + 5:2 * ng + 5], *rest[2 * ng + 5:]) if ng else None
        _host_start(pl.program_id(0), comm)
        for c in range(tm // BLK):
            rs = slice(c * BLK, (c + 1) * BLK)
            gu, _, _, _, _, _, s = _gmlp_chunk(uv_ref[rs, :G_W].astype(F32), uv_ref[rs, G_W:].astype(F32), gp_ref, ws_ref, bias_ref)
            yb_ref[rs, :] = (gu * s).astype(BF16)
        a = _dot(ya_ref[...], wa_ref[...])
        b = _dot(yb_ref[...], wb_ref[...])
        a_ref[...] = a.astype(BF16)
        b_ref[...] = b.astype(BF16)
        merged = (_sig(gab_ref[:, :D].astype(F32)) * a + _sig(gab_ref[:, D:].astype(F32)) * b).astype(BF16)
        merged_ref[...] = merged
        mix_ref[...] = _dot(merged, wo_ref[...])
        _host_finish(pl.program_id(0), steps - 1, comm, forward_at=(3 * steps) // 4)

    out = pl.pallas_call(
        body, name="mix_fwd", grid=(steps,),
        in_specs=[_rows(tm, 2 * G_W), _rows(tm, 2 * D), _rows(tm, Q_W), _acc((8, G_W)),
                  _resident((N_GRP * BLK, BLK)), _acc((BLK, G_W)),
                  _resident((Q_W, D)), _resident((G_W, D)), _resident((D, D))] + _comm_specs(ng),
        out_specs=[_rows(tm, D), _rows(tm, D), _rows(tm, D), _rows(tm, D), _rows(tm, G_W)] + _comm_specs(ng),
        out_shape=[jax.ShapeDtypeStruct((seq, D), BF16), jax.ShapeDtypeStruct((seq, D), BF16),
                   jax.ShapeDtypeStruct((seq, D), F32), jax.ShapeDtypeStruct((seq, D), BF16),
                   jax.ShapeDtypeStruct((seq, G_W), BF16)] + _gathered_shapes(gather),
        scratch_shapes=_comm_scratch(ng) if ng else [],
        compiler_params=_cp("arbitrary"),
    )(uv, gab, ya, gp, ws_stack, bias_full, w_a, w_b, w_o, *gather)
    return out[:5], out[5:]


def _mid_recompute(x_ref, mix_ref, vec_ref):
    r1 = ALPHA * x_ref[...] + vec_ref[0:1, :] * mix_ref[...]
    xh1, rstd1 = _ln(r1)
    xmid = xh1 * vec_ref[1:2, :] + vec_ref[2:3, :]
    xh2, rstd2 = _ln(xmid)
    return xh1, rstd1, xmid, xh2, rstd2


def _ffn(x, mix, tgt, vec, w_fi, w_fo, tm):
    seq = x.shape[0]

    def body(x_ref, mix_ref, tgt_ref, vec_ref, wi_ref, wo_ref, act_ref, h2_ref, dff_ref, df_ref, dr1_ref, st_ref, gu_ref):
        @pl.when(pl.program_id(0) == 0)
        def _():
            st_ref[...] = jnp.zeros_like(st_ref)

        xh1, rstd1, xmid, xh2, rstd2 = _mid_recompute(x_ref, mix_ref, vec_ref)
        h2 = (xh2 * (1.0 + vec_ref[4:5, :]) + vec_ref[3:4, :]).astype(BF16)
        h2_ref[...] = h2
        halves = [(slice(hh * FH_SHARD, (hh + 1) * FH_SHARD), slice(FH + hh * FH_SHARD, FH + (hh + 1) * FH_SHARD))
                  for hh in range(2)]
        for hh, (cs, cu) in enumerate(halves):
            g = _dot(h2, wi_ref[hh])
            u = _dot(h2, wi_ref[2 + hh])
            gu_ref[:, cs] = g
            gu_ref[:, cu] = u
            act_ref[:, cs] = (g * _sig(g) * u).astype(BF16)
        f = _dot(act_ref[...], wo_ref[...])
        r2 = ALPHA * xmid + vec_ref[5:6, :] * f
        yh, rstd = _ln(r2)
        y = yh * vec_ref[6:7, :] + vec_ref[7:8, :]
        err = y - tgt_ref[...]
        dy = err / D
        dr2 = _ln_bwd(dy * vec_ref[6:7, :], yh, rstd)
        st_ref[0:1, :] += _colsum(err * err)
        st_ref[1:2, :] += _colsum(dy * yh)
        st_ref[2:3, :] += _colsum(dy)
        st_ref[3:4, :] += _colsum(dr2 * f)

        df = (dr2 * vec_ref[5:6, :]).astype(BF16)
        df_ref[...] = df
        da_all = _dot_nt(df, wo_ref[...])
        for cs, cu in halves:
            da = da_all[:, cs]
            g = gu_ref[:, cs]
            u = gu_ref[:, cu]
            sg = _sig(g)
            dff_ref[:, cs] = (da * u * sg * (1.0 + g * (1.0 - sg))).astype(BF16)
            dff_ref[:, cu] = (da * g * sg).astype(BF16)
        dh2 = _dot_nt(dff_ref[:, :FH_SHARD], wi_ref[0])
        for s in range(1, N_SHARD):
            dh2 = dh2 + _dot_nt(dff_ref[:, s * FH_SHARD:(s + 1) * FH_SHARD], wi_ref[s])
        dxmid = _ln_bwd(dh2 * (1.0 + vec_ref[4:5, :]), xh2, rstd2) + ALPHA * dr2
        dr1 = _ln_bwd(dxmid * vec_ref[1:2, :], xh1, rstd1)
        dr1_ref[...] = dr1
        st_ref[8:9, :] += _colsum(dh2 * xh2)
        st_ref[9:10, :] += _colsum(dh2)
        st_ref[10:11, :] += _colsum(dxmid * xh1)
        st_ref[11:12, :] += _colsum(dxmid)
        st_ref[12:13, :] += _colsum(dr1 * mix_ref[...])

    return pl.pallas_call(
        body, name="ffn", grid=(seq // tm,),
        in_specs=[_rows(tm, D), _rows(tm, D), _rows(tm, D), _acc((8, D)), _resident((N_SHARD, D, FH_SHARD)), _resident((FH, D))],
        out_specs=[_rows(tm, FH), _rows(tm, D), _rows(tm, 2 * FH), _rows(tm, D), _rows(tm, D), _acc((16, D))],
        out_shape=[jax.ShapeDtypeStruct((seq, FH), BF16), jax.ShapeDtypeStruct((seq, D), BF16),
                   jax.ShapeDtypeStruct((seq, 2 * FH), BF16), jax.ShapeDtypeStruct((seq, D), BF16),
                   jax.ShapeDtypeStruct((seq, D), F32), jax.ShapeDtypeStruct((16, D), F32)],
        scratch_shapes=[pltpu.VMEM((tm, 2 * FH), F32)],
        compiler_params=_cp("arbitrary"),
    )(x, mix, tgt, vec, w_fi, w_fo)


def _mix_bwd(dr1, a, b, gab, uv, merged, ya, yb, vec, gp, ws_stack, ws_stack_t, bias_full, w_a, w_b, w_o, tm, scatter=()):
    seq = dr1.shape[0]
    last = seq // tm - 1
    ns = len(scatter)

    def body(dr1_ref, a_ref, b_ref, gab_ref, uv_ref, mg_ref, ya_ref, yb_ref, vec_ref, gp_ref, ws_ref, wst_ref, bias_ref,
             wa_ref, wb_ref, wo_ref, *rest):
        dya_ref, dp_ref, dws_ref, dbs_ref, st_ref, gwo_ref, gwa_ref, gwb_ref = rest[ns:ns + 8]
        acc_o, acc_a, acc_b = rest[2 * ns + 8:2 * ns + 11]
        comm = _AllToAll(rest[:ns], rest[ns + 8:2 * ns + 8], *rest[2 * ns + 11:]) if ns else None
        _host_start(pl.program_id(0), comm)

        @pl.when(pl.program_id(0) == 0)
        def _():
            dws_ref[...] = jnp.zeros_like(dws_ref)
            dbs_ref[...] = jnp.zeros_like(dbs_ref)
            st_ref[...] = jnp.zeros_like(st_ref)
            acc_o[...] = jnp.zeros_like(acc_o)
            acc_a[...] = jnp.zeros_like(acc_a)
            acc_b[...] = jnp.zeros_like(acc_b)

        dmix = (dr1_ref[...] * vec_ref[0:1, :]).astype(BF16)
        acc_o[...] += _dot_tn(mg_ref[...], dmix)
        dmerged = _dot_nt(dmix, wo_ref[...])
        sa = _sig(gab_ref[:, :D].astype(F32))
        sb = _sig(gab_ref[:, D:].astype(F32))
        da_f = dmerged * sa
        db_f = dmerged * sb
        da = da_f.astype(BF16)
        db = db_f.astype(BF16)
        dp_ref[:, 2 * G_W:2 * G_W + D] = (da_f * a_ref[...].astype(F32) * (1.0 - sa)).astype(BF16)
        dp_ref[:, 2 * G_W + D:] = (db_f * b_ref[...].astype(F32) * (1.0 - sb)).astype(BF16)
        dya_ref[...] = _dot_nt(da, wa_ref[...]).astype(BF16)
        dyb = _dot_nt(db, wb_ref[...])
        acc_a[...] += _dot_tn(ya_ref[...], da)
        acc_b[...] += _dot_tn(yb_ref[...], db)

        @pl.when(pl.program_id(0) == last)
        def _():
            gwo_ref[...] = acc_o[...].astype(BF16)
            gwa_ref[...] = acc_a[...].astype(BF16)
            gwb_ref[...] = acc_b[...].astype(BF16)

        for c in range(tm // BLK):
            rs = slice(c * BLK, (c + 1) * BLK)
            u = uv_ref[rs, :G_W].astype(F32)
            vb = uv_ref[rs, G_W:].astype(F32)
            gu, tu, tv, vhat, rstd, vn, s = _gmlp_chunk(u, vb, gp_ref, ws_ref, bias_ref)
            dyb_c = dyb[rs, :]
            ds = dyb_c * gu
            du = dyb_c * s * _gelu_grad(u, tu)
            ds_b = ds.astype(BF16)
            dvn_g = []
            for g in range(N_GRP):
                cg = slice(g * GRP_D, (g + 1) * GRP_D)
                dvn_g.append(_dot(wst_ref[:, g * BLK:(g + 1) * BLK], ds_b[:, cg]))
                dws_ref[g * BLK:(g + 1) * BLK, :] += _dot_nt(ds_b[:, cg], vn[:, cg])
            dvn = jnp.concatenate(dvn_g, axis=1)
            dbs_ref[...] += ds
            st_ref[0:1, :] += _colsum(dvn * vhat)
            st_ref[1:2, :] += _colsum(dvn)
            dgv = _ln_bwd(dvn * gp_ref[0:1, :], vhat, rstd)
            dvb = dgv * _gelu_grad(vb, tv)
            dp_ref[rs, :G_W] = du.astype(BF16)
            dp_ref[rs, G_W:2 * G_W] = dvb.astype(BF16)
        _host_finish(pl.program_id(0), last, comm)

    pw = 2 * G_W + 2 * D
    out = pl.pallas_call(
        body, name="mix_bwd", grid=(seq // tm,),
        in_specs=[_rows(tm, D), _rows(tm, D), _rows(tm, D), _rows(tm, 2 * D), _rows(tm, 2 * G_W), _rows(tm, D), _rows(tm, Q_W),
                  _rows(tm, G_W), _acc((8, D)), _acc((8, G_W)),
                  _resident((N_GRP * BLK, BLK)), _resident((BLK, N_GRP * BLK)), _acc((BLK, G_W)),
                  _resident((Q_W, D)), _resident((G_W, D)), _resident((D, D))] + _comm_specs(ns),
        out_specs=[_rows(tm, Q_W), _rows(tm, pw), _acc((N_GRP * BLK, BLK)), _acc((BLK, G_W)), _acc((8, G_W)),
                   _acc((D, D)), _acc((Q_W, D)), _acc((G_W, D))] + _comm_specs(ns),
        out_shape=[jax.ShapeDtypeStruct((seq, Q_W), BF16), jax.ShapeDtypeStruct((seq, pw), BF16),
                   jax.ShapeDtypeStruct((N_GRP * BLK, BLK), F32), jax.ShapeDtypeStruct((BLK, G_W), F32),
                   jax.ShapeDtypeStruct((8, G_W), F32), jax.ShapeDtypeStruct((D, D), BF16),
                   jax.ShapeDtypeStruct((Q_W, D), BF16), jax.ShapeDtypeStruct((G_W, D), BF16)]
        + [jax.ShapeDtypeStruct(v.shape, v.dtype) for v in scatter],
        scratch_shapes=[pltpu.VMEM((D, D), F32), pltpu.VMEM((Q_W, D), F32), pltpu.VMEM((G_W, D), F32)]
        + (_comm_scratch(ns) if ns else []),
        compiler_params=_cp("arbitrary"),
    )(dr1, a, b, gab, uv, merged, ya, yb, vec, gp, ws_stack, ws_stack_t, bias_full, w_a, w_b, w_o, *scatter)
    return out[:8], out[8:]


def _attn_bwd(q, kv, kvc, sink, dya, ya, lse, scatter=()):
    seq = q.shape[0]
    nb = seq // BLK
    n_ctx = kvc.shape[0]
    ns = len(scatter)
    Q_BLOCKS = 2
    nkv = Q_BLOCKS + 2
    steps = nb // Q_BLOCKS

    def body(q_ref, *rest):
        kv_refs = rest[:nkv]
        kvc_ref, sink_ref, do_ref, o_ref, lse_ref = rest[nkv:nkv + 5]
        rest = rest[nkv + 5:]
        dq_ref, dkv_ref, dkvc_ref, dsink_ref = rest[ns:ns + 4]
        comm = _AllToAll(rest[:ns], rest[ns + 4:2 * ns + 4], *rest[2 * ns + 4:]) if ns else None
        n = pl.program_id(0)
        _host_start(n, comm)

        @pl.when(n == 0)
        def _():
            dkv_ref[...] = jnp.zeros_like(dkv_ref)
            dkvc_ref[...] = jnp.zeros_like(dkvc_ref)
            dsink_ref[...] = jnp.zeros_like(dsink_ref)

        lane = lax.broadcasted_iota(jnp.int32, (1, LANES), 1)
        for sub in range(Q_BLOCKS):
            rs = slice(sub * BLK, (sub + 1) * BLK)
            blk = Q_BLOCKS * n + sub
            q = q_ref[rs, :]
            do = do_ref[rs, :]
            out = o_ref[rs, :]
            lse_all = lse_ref[rs, :]
            k_refs = (kvc_ref,) + kv_refs[sub:sub + 3]
            masks = _band_masks(blk, nb)
            dqs, dks, dvs = [], [], []
            for hk in range(N_KV):
                q4, ks, s = _attn_scores(q, k_refs, hk, masks)
                vs = [r[:, KV_W + hk * HEAD:KV_W + (hk + 1) * HEAD] for r in k_refs]
                lse4 = jnp.concatenate([lse_all[:, hk * GROUP + g:hk * GROUP + g + 1] for g in range(GROUP)], axis=0)
                do4 = _stack_heads(do, hk)
                delta = jnp.sum(do4.astype(F32) * _stack_heads(out, hk).astype(F32), axis=-1, keepdims=True)
                p = [jnp.exp((t - lse4).astype(BF16)) for t in s]
                ds = [t * (_dot_nt(do4, v) - delta).astype(BF16) for t, v in zip(p, vs)]
                dq4 = _dot(ds[0], ks[0])
                for t, k in zip(ds[1:], ks[1:]):
                    dq4 = dq4 + _dot(t, k)
                dq4 = dq4 * SCALE
                dqs += [dq4[g * BLK:(g + 1) * BLK, :] for g in range(GROUP)]
                dks.append([_dot_tn(t, q4) for t in ds])
                dvs.append([_dot_tn(t, do4) for t in p])
                ps = jnp.exp(_sink_rows(sink_ref, hk) - lse4) * delta
                for g in range(GROUP):
                    part = -jnp.sum(ps[g * BLK:(g + 1) * BLK, :], axis=0, keepdims=True)
                    dsink_ref[0:1, :] += jnp.where(lane == hk * GROUP + g, part, 0.0)
            dq_ref[rs, :] = jnp.concatenate(dqs, axis=1)

            def piece(i):
                return jnp.concatenate([dks[0][i], dks[1][i], dvs[0][i], dvs[1][i]], axis=1)

            dkvc_ref[...] += piece(0)
            starts = (jnp.maximum(blk - 1, 0), blk, jnp.minimum(blk + 1, nb - 1))
            for i, st in enumerate(starts):
                r = pl.ds(pl.multiple_of(st * BLK, BLK), BLK)
                dkv_ref[r, :] += piece(i + 1)
        _host_finish(n, steps - 1, comm)

    tq = Q_BLOCKS * BLK
    out = pl.pallas_call(
        body, name="attn_bwd", grid=(steps,),
        in_specs=[_rows(tq, Q_W)] + _kv_specs(nb, Q_BLOCKS) + [_acc((n_ctx, 2 * KV_W)), pl.BlockSpec(memory_space=pltpu.SMEM),
                                                     _rows(tq, Q_W), _rows(tq, Q_W), _rows(tq, LANES)] + _comm_specs(ns),
        out_specs=[_rows(tq, Q_W), _acc((seq, 2 * KV_W)), _acc((n_ctx, 2 * KV_W)), _acc((8, LANES))] + _comm_specs(ns),
        out_shape=[jax.ShapeDtypeStruct((seq, Q_W), F32), jax.ShapeDtypeStruct((seq, 2 * KV_W), F32),
                   jax.ShapeDtypeStruct((n_ctx, 2 * KV_W), F32), jax.ShapeDtypeStruct((8, LANES), F32)]
        + [jax.ShapeDtypeStruct(v.shape, v.dtype) for v in scatter],
        scratch_shapes=_comm_scratch(ns) if ns else [],
        compiler_params=_cp("arbitrary"),
    )(q, *([kv] * nkv), kvc, sink, dya, ya, lse, *scatter)
    return out[:4], out[4:]


def _proj_bwd(dq, dkv, dpb, x, dr1, modx, w_in, cos, sin, tm, scatter=()):
    seq = x.shape[0]
    pw = IN_W - Q_W - 2 * KV_W
    ns = len(scatter)

    def body(dq_ref, dkv_ref, dpb_ref, x_ref, dr1_ref, mod_ref, w_ref, cos_ref, sin_ref, *rest):
        dqkv_ref, gx_ref, st_ref = rest[ns:ns + 3]
        comm = _AllToAll(rest[:ns], rest[ns + 3:2 * ns + 3], *rest[2 * ns + 3:]) if ns else None
        _host_start(pl.program_id(0), comm)

        @pl.when(pl.program_id(0) == 0)
        def _():
            st_ref[...] = jnp.zeros_like(st_ref)

        cos1, sin1 = cos_ref[...], sin_ref[...]
        cos2 = jnp.concatenate([cos1, cos1], axis=1)
        sin2 = jnp.concatenate([sin1, sin1], axis=1)
        for j in range(Q_W // 256):
            cs = slice(256 * j, 256 * (j + 1))
            dqkv_ref[:, cs] = _unrope(dq_ref[:, cs], cos2, sin2).astype(BF16)
        dqkv_ref[:, Q_W:Q_W + KV_W] = _unrope(dkv_ref[:, :KV_W], cos1, sin1).astype(BF16)
        dqkv_ref[:, Q_W + KV_W:] = dkv_ref[:, KV_W:].astype(BF16)
        o = Q_W + 2 * KV_W
        dh = _dot(dqkv_ref[...], w_ref[:o, :]) + _dot(dpb_ref[...], w_ref[o:, :])
        xhat, rstd = _ln(x_ref[...])
        st_ref[0:1, :] += _colsum(dh)
        st_ref[1:2, :] += _colsum(dh * xhat)
        gx_ref[...] = _ln_bwd(dh * (1.0 + mod_ref[1:2, :]), xhat, rstd) + ALPHA * dr1_ref[...]
        _host_finish(pl.program_id(0), seq // tm - 1, comm)

    out = pl.pallas_call(
        body, name="proj_bwd", grid=(seq // tm,),
        in_specs=[_rows(tm, Q_W), _rows(tm, 2 * KV_W), _rows(tm, pw), _rows(tm, D), _rows(tm, D), _acc((8, D)),
                  _resident((IN_W, D)), _rows(tm, LANES), _rows(tm, LANES)] + _comm_specs(ns),
        out_specs=[_rows(tm, Q_W + 2 * KV_W), _rows(tm, D), _acc((8, D))] + _comm_specs(ns),
        out_shape=[jax.ShapeDtypeStruct((seq, Q_W + 2 * KV_W), BF16), jax.ShapeDtypeStruct((seq, D), F32),
                   jax.ShapeDtypeStruct((8, D), F32)] + [jax.ShapeDtypeStruct(v.shape, v.dtype) for v in scatter],
        scratch_shapes=_comm_scratch(ns) if ns else [],
        compiler_params=_cp("arbitrary"),
    )(dq, dkv, dpb, x, dr1, modx, w_in, cos, sin, *scatter)
    return out[:3], out[3:]


def _ctx_bwd(dkvc, ctx, hc, w_in):
    n_ctx = ctx.shape[0]

    def body(dkvc_ref, ctx_ref, hc_ref, w_ref, dw_ref, st_ref):
        d = dkvc_ref[...].astype(BF16)
        dw_ref[...] = _dot_tn(d, hc_ref[...])
        dhc = _dot(d, w_ref[...])
        xhat, _ = _ln(ctx_ref[...])
        st_ref[...] = jnp.zeros_like(st_ref)
        st_ref[0:1, :] = _colsum(dhc)
        st_ref[1:2, :] = _colsum(dhc * xhat)

    return pl.pallas_call(
        body, name="ctx_bwd", grid=(1,),
        in_specs=[_acc((n_ctx, 2 * KV_W)), _acc((n_ctx, D)), _acc((n_ctx, D)), _kv_rows_of_w_in()],
        out_specs=[_acc((2 * KV_W, D)), _acc((8, D))],
        out_shape=[jax.ShapeDtypeStruct((2 * KV_W, D), F32), jax.ShapeDtypeStruct((8, D), F32)],
        compiler_params=_cp("arbitrary"),
    )(dkvc, ctx, hc, w_in)


def _tn_matmul(a, b, tn, name, out_dtype, shard_major=False, init=None, tk=512, out_rows=None, into=None,
               scatter=(), gather=()):
    t, ka = a.shape
    n = b.shape[1]
    tk = min(tk, t)
    nk = t // tk
    nj = n // tn
    has_init = init is not None
    in_place = into is not None
    assert not (scatter and gather) and not (in_place and (nj != 1 or shard_major or out_rows))
    moved = list(scatter) + list(gather)
    pattern = _AllToAll if scatter else _Gather
    ns = len(moved)
    n_in = 2 + has_init + in_place
    n_scr = 3 if in_place else 1

    def body(*refs):
        a_ref, b_ref = refs[:2]
        i_ref = refs[2] if has_init else None
        rest = refs[n_in:]
        o_ref = rest[ns]
        acc_ref = rest[2 * ns + 1]
        comm = pattern(rest[:ns], rest[ns + 1:2 * ns + 1], *rest[2 * ns + 1 + n_scr:]) if ns else None
        k = pl.program_id(1)
        step = pl.program_id(0) * nk + k
        _host_start(step, comm)

        @pl.when(k == 0)
        def _():
            acc_ref[...] = jnp.zeros_like(acc_ref)
            if has_init:
                acc_ref[ka - init.shape[0]:, :] = i_ref[...]

        acc_ref[...] += _dot_tn(a_ref[...], b_ref[...])

        @pl.when(k == nk - 1)
        def _():
            if in_place:
                stage_ref, sem = rest[2 * ns + 2:2 * ns + 4]
                stage_ref[...] = acc_ref[...].astype(out_dtype)
                write = pltpu.make_async_copy(stage_ref, o_ref.at[pl.ds(into[1], ka), :], sem.at[0])
                write.start()
                write.wait()
            else:
                o_ref[...] = acc_ref[...].astype(out_dtype)

        _host_finish(step, nj * nk - 1, comm, forward_at=(3 * nj * nk) // 4 if gather else None)

    in_specs = [pl.BlockSpec((tk, ka), lambda j, k: (k, 0)), pl.BlockSpec((tk, tn), lambda j, k: (k, j))]
    args = [a, b]
    if has_init:
        in_specs.append(pl.BlockSpec((init.shape[0], tn), lambda j, k: (0, j)))
        args.append(init)
    scratch = [pltpu.VMEM((ka, tn), F32)]
    aliases = {}
    if in_place:
        in_specs.append(pl.BlockSpec(memory_space=pl.ANY))
        args.append(into[0])
        aliases = {n_in - 1: 0}
        out_spec = pl.BlockSpec(memory_space=pl.ANY)
        out_shape = jax.ShapeDtypeStruct(into[0].shape, into[0].dtype)
        scratch += [pltpu.VMEM((ka, tn), out_dtype), pltpu.SemaphoreType.DMA((1,))]
    elif shard_major:
        out_spec = pl.BlockSpec((None, ka, tn), lambda j, k: (j, 0, 0))
        out_shape = jax.ShapeDtypeStruct((nj, ka, tn), out_dtype)
    else:
        out_spec = pl.BlockSpec((ka, tn), lambda j, k: (0, j))
        out_shape = jax.ShapeDtypeStruct((out_rows or ka, n), out_dtype)
    out = pl.pallas_call(
        body, name=name, grid=(nj, nk), in_specs=in_specs + _comm_specs(ns), out_specs=[out_spec] + _comm_specs(ns),
        out_shape=[out_shape] + [jax.ShapeDtypeStruct(v.shape, v.dtype) for v in scatter] + _gathered_shapes(gather),
        scratch_shapes=scratch + (_comm_scratch(ns) if ns else []), input_output_aliases=aliases,
        compiler_params=_cp("arbitrary", "arbitrary"),
    )(*args, *moved)
    return (out[0], out[1:]) if ns else out[0]


ADA_TILE = 512


def _gather_and_modulate(c_rows, c_ctx, w_half, w_ada):
    cs = w_ada.shape[1]
    vmem = pl.BlockSpec(memory_space=pltpu.VMEM)

    def body(c_ref, cctx_ref, wh_ref, wada_ref, cg_ref, wg_ref, mod_ref, modg_ref, sc_ref,
             c_v, wada_v, mod_v, send_c, recv_c, send_w, recv_w, send_m, recv_m, local):
        gc = _Gather([c_ref], [cg_ref], send_c, recv_c)
        gw = _Gather([wh_ref], [wg_ref], send_w, recv_w)
        gm = _Gather([mod_ref], [modg_ref], send_m, recv_m)
        px, py, pc = _my_pos()
        mine = 4 * px + 2 * py + pc
        gc.start()
        gw.start()
        own_c = pltpu.make_async_copy(c_ref, cg_ref.at[mine], local.at[0])
        load_w = pltpu.make_async_copy(wada_ref, wada_v, local.at[1])
        own_c.start()
        load_w.start()
        gc.finish()
        own_c.wait()
        load_c = pltpu.make_async_copy(cg_ref, c_v, local.at[2])
        load_c.start()
        load_c.wait()
        cc = jnp.concatenate([c_v[i, 0:1, :] for i in range(N_DEV)] + [cctx_ref[...], jnp.zeros((7, D), F32)], axis=0)
        sc = cc * _sig(cc)
        sc_ref[...] = sc
        load_w.wait()
        for j in range(cs // ADA_TILE):
            cols = slice(j * ADA_TILE, (j + 1) * ADA_TILE)
            mod_v[:, cols] = _dot(sc.astype(BF16), wada_v[:, cols].astype(BF16))
        store_m = pltpu.make_async_copy(mod_v, mod_ref, local.at[3])
        store_m.start()
        store_m.wait()
        own_m = pltpu.make_async_copy(mod_ref, modg_ref.at[mine], local.at[4])
        own_m.start()
        gm.start()
        gm.finish()
        own_m.wait()
        gw.finish()

    any_ = pl.BlockSpec(memory_space=pl.ANY)
    return pl.pallas_call(
        body, name="gather_and_modulate",
        in_specs=[any_, vmem, any_, any_], out_specs=[any_, any_, any_, any_, vmem],
        scratch_shapes=[pltpu.VMEM((N_DEV, 8, D), F32), pltpu.VMEM((D, cs), F32), pltpu.VMEM((16, cs), F32)]
        + _comm_scratch(1) + _comm_scratch(1) + _comm_scratch(1) + [pltpu.SemaphoreType.DMA((5,))],
        out_shape=[jax.ShapeDtypeStruct((N_DEV, 8, D), F32), jax.ShapeDtypeStruct((N_DEV,) + w_half.shape, w_half.dtype),
                   jax.ShapeDtypeStruct((16, cs), F32), jax.ShapeDtypeStruct((N_DEV, 16, cs), F32),
                   jax.ShapeDtypeStruct((16, D), F32)],
        compiler_params=pltpu.CompilerParams(vmem_limit_bytes=VMEM_LIMIT),
    )(c_rows, c_ctx, w_half, w_ada)


def _ada_bwd(sc_all_t, dm_all, dmc, w_ada, m, v):
    cs = w_ada.shape[1]

    def body(st_ref, dm_ref, dmc_ref, w_ref, m_ref, v_ref, gw_ref, d_ref, nm_ref, nv_ref, part_ref):
        @pl.when(pl.program_id(0) == 0)
        def _():
            part_ref[...] = jnp.zeros_like(part_ref)

        g = _dot(st_ref[...].astype(BF16), dm_ref[...].astype(BF16))
        gw_ref[...] = g
        d_ref[...], nm_ref[...], nv_ref[...] = _adam_update(w_ref[...], g, m_ref[...], v_ref[...])
        part_ref[...] += _dot_nt(dmc_ref[...].astype(BF16), w_ref[...].astype(BF16))

    cols = pl.BlockSpec((D, ADA_TILE), lambda j: (0, j))
    shp = jax.ShapeDtypeStruct((D, cs), F32)
    return pl.pallas_call(
        body, name="ada_bwd_adamw", grid=(cs // ADA_TILE,),
        in_specs=[_acc((D, 16)), pl.BlockSpec((16, ADA_TILE), lambda j: (0, j)), pl.BlockSpec((8, ADA_TILE), lambda j: (0, j)),
                  cols, cols, cols],
        out_specs=[cols] * 4 + [_acc((8, D))],
        out_shape=[shp] * 4 + [jax.ShapeDtypeStruct((8, D), F32)],
        compiler_params=_cp("arbitrary"),
    )(sc_all_t, dm_all, dmc, w_ada, m, v)


def _sum8(x, name, tr=256):
    _, r, c = x.shape
    tr = min(tr, r)
    while r % tr:
        tr -= 16

    def body(x_ref, o_ref):
        acc = x_ref[0].astype(F32)
        for i in range(1, N_DEV):
            acc = acc + x_ref[i].astype(F32)
        o_ref[...] = acc

    return pl.pallas_call(
        body, name=name, grid=(r // tr,),
        in_specs=[pl.BlockSpec((N_DEV, tr, c), lambda i: (0, i, 0))],
        out_specs=pl.BlockSpec((tr, c), lambda i: (i, 0)),
        out_shape=jax.ShapeDtypeStruct((r, c), F32),
        compiler_params=_cp("arbitrary"),
    )(x)


def _sum_blocks(recv, src, me, name, tr=256):
    _, r, c = recv.shape
    tr = min(tr, r)
    while r % tr:
        tr -= 16

    def body(me_ref, recv_ref, own_ref, o_ref):
        acc = own_ref[...].astype(F32)
        for k in range(1, N_DEV):
            acc = acc + recv_ref[me_ref[0] ^ k].astype(F32)
        o_ref[...] = acc

    return pl.pallas_call(
        body, name=name,
        grid_spec=pltpu.PrefetchScalarGridSpec(
            num_scalar_prefetch=1, grid=(r // tr,),
            in_specs=[pl.BlockSpec((N_DEV, tr, c), lambda i, me_ref: (0, i, 0)),
                      pl.BlockSpec((None, tr, c), lambda i, me_ref: (me_ref[0], i, 0))],
            out_specs=pl.BlockSpec((tr, c), lambda i, me_ref: (i, 0))),
        out_shape=jax.ShapeDtypeStruct((r, c), F32),
        compiler_params=_cp("arbitrary"),
    )(me, recv, src)


def _sum_blocks_hosting(recvs, srcs, me, name, scatter, gather):
    n, ns, ng = len(recvs), len(scatter), len(gather)

    def body(me_ref, *refs):
        recv_refs, own_refs = refs[:n], refs[n:2 * n]
        moved_in = refs[2 * n:2 * n + ns + ng]
        outs = refs[2 * n + ns + ng:]
        moved_out, sems = outs[n:n + ns + ng], outs[n + ns + ng:]
        t = _AllToAll(moved_in[:ns], moved_out[:ns], sems[0], sems[1])
        g = _Gather(moved_in[ns:], moved_out[ns:], sems[2], sems[3])
        g.start()
        t.start()
        for recv_ref, own_ref, o_ref in zip(recv_refs, own_refs, outs[:n]):
            acc = own_ref[...].astype(F32)
            for k in range(1, N_DEV):
                acc = acc + recv_ref[me_ref[0] ^ k].astype(F32)
            o_ref[...] = acc
        g.finish()
        t.finish()

    def once(shape, index_map):
        return pl.BlockSpec(shape, index_map, pipeline_mode=pl.Buffered(1))

    shapes = [v.shape[1:] for v in recvs]
    out = pl.pallas_call(
        body, name=name,
        grid_spec=pltpu.PrefetchScalarGridSpec(
            num_scalar_prefetch=1, grid=(1,),
            in_specs=[once((N_DEV,) + s, lambda i, me_ref: (0, 0, 0)) for s in shapes]
            + [once((None,) + s, lambda i, me_ref: (me_ref[0], 0, 0)) for s in shapes] + _comm_specs(ns + ng),
            out_specs=[once(s, lambda i, me_ref: (0, 0)) for s in shapes] + _comm_specs(ns + ng),
            scratch_shapes=_comm_scratch(ns) + _comm_scratch(ng)),
        out_shape=[jax.ShapeDtypeStruct(s, F32) for s in shapes]
        + [jax.ShapeDtypeStruct(v.shape, v.dtype) for v in scatter] + _gathered_shapes(gather),
        compiler_params=_cp("arbitrary"),
    )(me, *recvs, *srcs, *scatter, *gather)
    return out[:n], out[n:n + ns], out[n + ns:]


def _sum8_many(xs, name):
    n = len(xs)

    def body(*refs):
        for x_ref, o_ref in zip(refs[:n], refs[n:]):
            acc = x_ref[0]
            for i in range(1, N_DEV):
                acc = acc + x_ref[i]
            o_ref[...] = acc

    vmem = pl.BlockSpec(memory_space=pltpu.VMEM)
    return pl.pallas_call(
        body, name=name, in_specs=[vmem] * n, out_specs=[vmem] * n,
        out_shape=[jax.ShapeDtypeStruct(v.shape[1:], v.dtype) for v in xs],
        compiler_params=pltpu.CompilerParams(vmem_limit_bytes=VMEM_LIMIT),
    )(*xs)


def _adam_update(w, g, m, v):
    nm = ADAM_B1 * m + (1.0 - ADAM_B1) * g
    nv = ADAM_B2 * v + (1.0 - ADAM_B2) * (g * g)
    m_hat = nm / (1.0 - ADAM_B1 ** ADAM_STEP)
    v_hat = nv / (1.0 - ADAM_B2 ** ADAM_STEP)
    return -ADAM_LR * (m_hat / (jnp.sqrt(v_hat) + ADAM_EPS) + ADAM_WD * w), nm, nv


ROW_LOSS, ROW_LN2_G, ROW_LN2_B, ROW_LN1_G, ROW_LN1_B = 0, 1, 2, 10, 11
ROWS_DMOD_X = (16, 17, 12, 9, 8, 3)
ROWS_DMOD_C = (24, 25)
SMALL = ("c_ctx", "b_ada", "attn_sink", "gmlp_ln_g", "gmlp_ln_b", "w_spatial", "b_spatial", "ln1_g", "ln1_b", "ln2_g", "ln2_b")


def _adamw_small(sums, dsc, w, m, v):
    n = len(SMALL)

    def body(*refs):
        st_ref, gm_ref, sk_ref, ws_ref, bs_ref, dsc_ref = refs[:6]
        w_refs = dict(zip(SMALL, refs[6:6 + n]))
        m_refs = dict(zip(SMALL, refs[6 + n:6 + 2 * n]))
        v_refs = dict(zip(SMALL, refs[6 + 2 * n:6 + 3 * n]))
        outs = refs[6 + 3 * n:]
        c = w_refs["c_ctx"][...]
        sg = _sig(c)
        dmod = [st_ref[r:r + 1, :] for r in ROWS_DMOD_X]
        dmod[0] = dmod[0] + st_ref[ROWS_DMOD_C[0]:ROWS_DMOD_C[0] + 1, :]
        dmod[1] = dmod[1] + st_ref[ROWS_DMOD_C[1]:ROWS_DMOD_C[1] + 1, :]
        grads = dict(
            c_ctx=dsc_ref[0:1, :] * (sg * (1.0 + c * (1.0 - sg))),
            b_ada=jnp.concatenate(dmod, axis=1),
            attn_sink=sk_ref[0:1, 0:N_KV * GROUP],
            gmlp_ln_g=gm_ref[0:1, :], gmlp_ln_b=gm_ref[1:2, :],
            w_spatial=ws_ref[...], b_spatial=bs_ref[...],
            ln1_g=st_ref[ROW_LN1_G:ROW_LN1_G + 1, :], ln1_b=st_ref[ROW_LN1_B:ROW_LN1_B + 1, :],
            ln2_g=st_ref[ROW_LN2_G:ROW_LN2_G + 1, :], ln2_b=st_ref[ROW_LN2_B:ROW_LN2_B + 1, :])
        for i, name in enumerate(SMALL):
            g = grads[name]
            d, nm, nv = _adam_update(w_refs[name][...], g, m_refs[name][...], v_refs[name][...])
            outs[i][...] = g
            outs[n + i][...] = d
            outs[2 * n + i][...] = nm
            outs[3 * n + i][...] = nv

    vmem = pl.BlockSpec(memory_space=pltpu.VMEM)
    args = list(sums) + [dsc] + [w[k] for k in SMALL] + [m[k] for k in SMALL] + [v[k] for k in SMALL]
    shapes = [jax.ShapeDtypeStruct(w[k].shape, F32) for k in SMALL]
    out = pl.pallas_call(
        body, name="adamw_small", in_specs=[vmem] * len(args), out_specs=[vmem] * (4 * n), out_shape=shapes * 4,
        compiler_params=pltpu.CompilerParams(vmem_limit_bytes=VMEM_LIMIT),
    )(*args)
    return [dict(zip(SMALL, out[i * n:(i + 1) * n])) for i in range(4)]


def _adamw_halves(w, mine, theirs, m, v, c_arr, name):
    r, c = w.shape
    tr = min(256, r // 2)
    while (r // 2) % tr:
        tr -= 8
    nt = (r // 2) // tr

    def body(c_ref, w_ref, mine_ref, theirs_ref, m_ref, v_ref, g_ref, d_ref, nm_ref, nv_ref):
        g = jnp.where(pl.program_id(0) == c_ref[0], mine_ref[...], theirs_ref[...])
        g_ref[...] = g
        d_ref[...], nm_ref[...], nv_ref[...] = _adam_update(w_ref[...], g, m_ref[...], v_ref[...])

    whole = pl.BlockSpec((tr, c), lambda hb, i, c_ref: (hb * nt + i, 0))
    mine_spec = pl.BlockSpec((tr, c), lambda hb, i, c_ref: (jnp.where(hb == c_ref[0], i, 0), 0))
    theirs_spec = pl.BlockSpec((tr, c), lambda hb, i, c_ref: (jnp.where(hb == c_ref[0], 0, i), 0))
    shp = jax.ShapeDtypeStruct((r, c), F32)
    return pl.pallas_call(
        body, name=name,
        grid_spec=pltpu.PrefetchScalarGridSpec(
            num_scalar_prefetch=1, grid=(2, nt), in_specs=[whole, mine_spec, theirs_spec, whole, whole],
            out_specs=[whole] * 4),
        out_shape=[shp] * 4,
        compiler_params=_cp("arbitrary", "arbitrary"),
    )(c_arr, w, mine, theirs, m, v)


def _my_pos():
    return lax.axis_index("x"), lax.axis_index("y"), lax.axis_index("c")


N_COPY = 7


class _Gather:
    def __init__(self, x_refs, out_refs, send_sems, recv_sems):
        self.x_refs, self.out_refs = x_refs, out_refs
        self.send_sems, self.recv_sems = send_sems, recv_sems
        x, y, c = _my_pos()
        self.c = c
        self.me, self.sibling = (x, y, c), (x, y, 1 - c)
        self.chips = [(1 - x, y), (x, 1 - y), (1 - x, 1 - y)]

    def _copy(self, a, k, block, to, from_input=False):
        px, py, pc = block
        rows = self.out_refs[a].at[4 * px + 2 * py + pc]
        return pltpu.make_async_remote_copy(
            src_ref=self.x_refs[a] if from_input else rows, dst_ref=rows,
            send_sem=self.send_sems.at[a * N_COPY + k], recv_sem=self.recv_sems.at[a * N_COPY + k],
            device_id=to, device_id_type=MESH)

    def start(self):
        n = len(self.x_refs)
        for a in range(n):
            self._copy(a, 0, self.me, self.sibling, from_input=True).start()
        for j, chip in enumerate(self.chips):
            for a in range(n):
                self._copy(a, 1 + j, self.me, (*chip, self.c), from_input=True).start()

    def forward(self):
        c = self.c
        for j, chip in enumerate(self.chips):
            for a in range(len(self.x_refs)):
                self._copy(a, 1 + j, (*chip, c), self.me).wait_recv()
                self._copy(a, 4 + j, (*chip, c), self.sibling).start()

    def finish(self):
        self.forward()
        self.drain()

    def drain(self):
        n = len(self.x_refs)
        c = self.c
        for a in range(n):
            self._copy(a, 0, self.sibling, self.me).wait_recv()
        for j, chip in enumerate(self.chips):
            for a in range(n):
                self._copy(a, 4 + j, (*chip, 1 - c), self.me).wait_recv()
        for a in range(n):
            self._copy(a, 0, self.me, self.sibling, from_input=True).wait_send()
            for j, chip in enumerate(self.chips):
                self._copy(a, 1 + j, self.me, (*chip, c), from_input=True).wait_send()
                self._copy(a, 4 + j, (*chip, c), self.sibling).wait_send()


def _comm_scratch(n):
    return [pltpu.SemaphoreType.DMA((n * N_COPY,)), pltpu.SemaphoreType.DMA((n * N_COPY,))]


def _comm_specs(n):
    return [pl.BlockSpec(memory_space=pl.ANY)] * n


def _gathered_shapes(xs):
    return [jax.ShapeDtypeStruct((N_DEV,) + v.shape, v.dtype) for v in xs]


def _with_own(gathered, xs, me):
    return [lax.dynamic_update_index_in_dim(g, v, me, 0) for g, v in zip(gathered, xs)]


class _AllToAll:
    def __init__(self, x_refs, out_refs, send_sems, recv_sems):
        self.x_refs, self.out_refs = x_refs, out_refs
        self.send_sems, self.recv_sems = send_sems, recv_sems
        self.pos = _my_pos()
        x, y, c = self.pos
        self.me = 4 * x + 2 * y + c

    def _peer(self, k):
        x, y, c = self.pos
        return (x ^ ((k >> 2) & 1), y ^ ((k >> 1) & 1), c ^ (k & 1))

    def _copy(self, a, k):
        p = self._peer(k)
        return pltpu.make_async_remote_copy(
            src_ref=self.x_refs[a].at[4 * p[0] + 2 * p[1] + p[2]], dst_ref=self.out_refs[a].at[self.me],
            send_sem=self.send_sems.at[a * N_COPY + k - 1], recv_sem=self.recv_sems.at[a * N_COPY + k - 1],
            device_id=p, device_id_type=MESH)

    def start(self):
        for k in range(1, N_DEV):
            for a in range(len(self.x_refs)):
                self._copy(a, k).start()

    def finish(self):
        for a in range(len(self.x_refs)):
            for k in range(1, N_DEV):
                self._copy(a, k).wait_recv()
            for k in range(1, N_DEV):
                self._copy(a, k).wait_send()


def _exchange(name, scatter=(), gather=(), sibling=()):
    ns, ng, nx = len(scatter), len(gather), len(sibling)
    n = ns + ng + nx

    def body(*refs):
        ins, outs, sems = refs[:n], refs[n:2 * n], list(refs[2 * n:])
        t = _AllToAll(ins[:ns], outs[:ns], sems.pop(0), sems.pop(0)) if ns else None
        g = _Gather(ins[ns:ns + ng], outs[ns:ns + ng], sems.pop(0), sems.pop(0)) if ng else None
        x, y, c = _my_pos()

        def push(a):
            return pltpu.make_async_remote_copy(
                src_ref=ins[ns + ng + a], dst_ref=outs[ns + ng + a], send_sem=sems[0].at[a], recv_sem=sems[1].at[a],
                device_id=(x, y, 1 - c), device_id_type=MESH)

        for comm in (g, t):
            if comm is not None:
                comm.start()
        for a in range(nx):
            push(a).start()
        for comm in (g, t):
            if comm is not None:
                comm.finish()
        for a in range(nx):
            push(a).wait_recv()
            push(a).wait_send()

    scratch = (_comm_scratch(ns) if ns else []) + (_comm_scratch(ng) if ng else [])
    scratch += [pltpu.SemaphoreType.DMA((nx,)), pltpu.SemaphoreType.DMA((nx,))] if nx else []
    out = pl.pallas_call(
        body, name=name,
        out_shape=[jax.ShapeDtypeStruct(v.shape, v.dtype) for v in scatter] + _gathered_shapes(gather)
        + [jax.ShapeDtypeStruct(v.shape, v.dtype) for v in sibling],
        in_specs=_comm_specs(n), out_specs=_comm_specs(n), scratch_shapes=scratch,
    )(*scatter, *gather, *sibling)
    return out[:ns], out[ns:ns + ng], out[ns + ng:]


def _row_tile(seq, want):
    return min(want, seq)


def _local_step(x, ctx, tgt, mod_x, mod_c, wb, sink, gmlp_g, gmlp_b, w_s, b_s, ln1_g, ln1_b, ln2_g, ln2_b,
                later=None, me=None):
    seq = x.shape[0]
    on_mesh = me is not None
    modx1 = jnp.concatenate([mod_x[0:2], jnp.zeros((6, D), F32)], axis=0)
    modc = jnp.concatenate([mod_c[0:2], jnp.zeros((6, D), F32)], axis=0)
    vec = jnp.concatenate([mod_x[2:3], ln1_g, ln1_b, mod_x[3:6], ln2_g, ln2_b], axis=0)
    gp = jnp.concatenate([gmlp_g, gmlp_b, jnp.zeros((6, G_W), F32)], axis=0)
    ws_stack = w_s.reshape(N_GRP * BLK, BLK).astype(BF16)
    ws_stack_t = jnp.transpose(w_s, (2, 0, 1)).reshape(BLK, N_GRP * BLK).astype(BF16)
    bias_full = jnp.repeat(b_s.T, GRP_D, axis=1)
    cos, sin = _rope_tables(seq)
    w_in = wb["w_in"]
    tm_big = _row_tile(seq, 512)
    tm_ffn = _row_tile(seq, 256)

    hc, kvc, vac = _ctx_fwd(ctx, modc, w_in)
    behind_proj = ("w_a", "w_b", "w_o") if on_mesh else ()
    behind_attn = ("w_fi",) if on_mesh else ()
    behind_mix = ("w_fo",) if on_mesh else ()
    wb = dict(wb)

    def whole(names, gathered):
        for n, g in zip(names, _with_own(list(gathered), [later[n] for n in names], me)):
            wb[n] = g.reshape(-1, g.shape[2]) if n in ROW_SHARDED else g.reshape(N_SHARD, 2 * g.shape[1], g.shape[2])

    (h, q, kv, va, uv, gab), got = _proj_fwd(x, modx1, w_in, cos, sin, _row_tile(seq, 1024), gather=[later[n] for n in behind_proj])
    whole(behind_proj, got)
    if on_mesh:
        for n in ("w_a", "w_b"):
            wb[n] = wb[n].transpose(1, 0, 2).reshape(wb[n].shape[1], D)
    (ya, lse), got = _attn_fwd(q, kv, va, kvc, vac, sink, gather=[later[n] for n in behind_attn])
    whole(behind_attn, got)
    (a, b, mix, merged, yb), got = _mix_fwd(uv, gab, ya, gp, ws_stack, bias_full, wb["w_a"], wb["w_b"], wb["w_o"], tm_big,
                                            gather=[later[n] for n in behind_mix])
    whole(behind_mix, got)
    act, h2, dff, df, dr1, st_ffn = _ffn(x, mix, tgt, vec, wb["w_fi"], wb["w_fo"], tm_ffn)
    blocks, recv = {}, {}
    blocks["w_fo"] = _eighths(_tn_matmul(act, df, 512, "tn_w_ffn_out", BF16, tk=2048))
    if on_mesh:
        g_w_fi, (recv["w_fo"],) = _tn_matmul(h2, dff, FH_SHARD, "tn_w_ffn_in", BF16, shard_major=True, tk=2048,
                                            scatter=[blocks["w_fo"]])
    else:
        g_w_fi = _tn_matmul(h2, dff, FH_SHARD, "tn_w_ffn_in", BF16, shard_major=True, tk=2048)
    blocks["w_fi"] = _eighths(g_w_fi)
    (dya, dpb, dws, dbs_full, st4, g_w_o, g_w_a, g_w_b), got = _mix_bwd(
        dr1, a, b, gab, uv, merged, ya, yb, vec, gp, ws_stack, ws_stack_t, bias_full, wb["w_a"], wb["w_b"], wb["w_o"],
        tm_big, scatter=[blocks["w_fi"]] if on_mesh else ())
    recv.update(zip(("w_fi",), got))
    shard_major = [g.reshape(g.shape[0], N_SHARD, D // N_SHARD).transpose(1, 0, 2) for g in (g_w_a, g_w_b)]
    blocks.update(w_o=_eighths(g_w_o), w_a=_eighths(shard_major[0]), w_b=_eighths(shard_major[1]))
    mixer = ("w_o", "w_a", "w_b") if on_mesh else ()
    (dq, dkv, dkvc, dsink), got = _attn_bwd(q, kv, kvc, sink, dya, ya, lse, scatter=[blocks[n] for n in mixer])
    recv.update(zip(mixer, got))
    g_wkv_ctx, st0 = _ctx_bwd(dkvc, ctx, hc, w_in)
    (dqkv, grad_x, st1), _ = _proj_bwd(dq, dkv, dpb, x, dr1, modx1, w_in, cos, sin, tm_big)
    dbs = jnp.sum(dbs_full.reshape(BLK, N_GRP, GRP_D), axis=2).T
    early = [jnp.concatenate([st_ffn, st0], axis=0), st4, dsink, dws, dbs]
    g_in = _tn_matmul(dqkv, h, D, "tn_w_in_qkv", BF16, init=g_wkv_ctx, tk=1024, out_rows=IN_W)
    into = (g_in, dqkv.shape[1])
    if on_mesh:
        g_in, early_gathered = _tn_matmul(dpb, h, D, "tn_w_in_rest", BF16, tk=1024, into=into, gather=early)
    else:
        g_in, early_gathered = _tn_matmul(dpb, h, D, "tn_w_in_rest", BF16, tk=1024, into=into), None
    blocks["w_in"] = _eighths(g_in)
    return grad_x, dict(early=early, early_gathered=early_gathered, late=st1), blocks, recv


BIG = ("w_in", "w_a", "w_b", "w_o", "w_fi", "w_fo")
ROW_SHARDED = ("w_o", "w_fo")


def _half_of_shard(shard, c):
    r = shard.shape[0]
    return lax.dynamic_slice_in_dim(shard, c * (r // 2), r // 2, axis=0)


def _eighths(v):
    rows = v.shape[-2] * (v.shape[0] if v.ndim == 3 else 1)
    return v.reshape(N_DEV, rows // N_DEV, v.shape[-1])


def kernel(x, c, ctx, c_ctx, w_ada, b_ada, w_in, attn_sink, gmlp_ln_g, gmlp_ln_b, w_spatial, b_spatial, w_branch_a, w_branch_b, w_out, ln1_g, ln1_b, w_ffn_in, w_ffn_out, ln2_g, ln2_b, loss_target, m_c_ctx, m_w_ada, m_b_ada, m_w_in, m_attn_sink, m_gmlp_ln_g, m_gmlp_ln_b, m_w_spatial, m_b_spatial, m_w_branch_a, m_w_branch_b, m_w_out, m_ln1_g, m_ln1_b, m_w_ffn_in, m_w_ffn_out, m_ln2_g, m_ln2_b, v_c_ctx, v_w_ada, v_b_ada, v_w_in, v_attn_sink, v_gmlp_ln_g, v_gmlp_ln_b, v_w_spatial, v_b_spatial, v_w_branch_a, v_w_branch_b, v_w_out, v_ln1_g, v_ln1_b, v_w_ffn_in, v_w_ffn_out, v_ln2_g, v_ln2_b):
    mx, my, mc = _my_pos()
    me = 4 * mx + 2 * my + mc
    chip = 2 * mx + my
    shards = dict(w_in=w_in[0].T, w_a=w_branch_a[0], w_b=w_branch_b[0], w_o=w_out[0], w_fi=w_ffn_in[0], w_fo=w_ffn_out[0])

    halves = {n: _half_of_shard(shards[n], mc).astype(BF16) for n in BIG}
    c_rows = jnp.concatenate([c, jnp.zeros((7, D), F32)], axis=0)
    _, g_in, _, mod_g, sc_all = _gather_and_modulate(c_rows, c_ctx[None, :], halves["w_in"], w_ada[0])
    wb = dict(w_in=_with_own([g_in], [halves["w_in"]], me)[0].reshape(IN_W, D))
    mod_all = jnp.concatenate([mod_g[2 * s] for s in range(4)], axis=1) + b_ada
    mod_x = lax.dynamic_slice_in_dim(mod_all, me, 1, axis=0).reshape(6, D)
    mod_c = mod_all[8].reshape(6, D)[0:2]

    grad_x, small, blocks, recv = _local_step(
        x[0], ctx[0], loss_target[0], mod_x, mod_c, wb, attn_sink, gmlp_ln_g, gmlp_ln_b, w_spatial[0], b_spatial[0],
        ln1_g, ln1_b, ln2_g, ln2_b, later=halves, me=me)

    me_arr = jnp.reshape(me, (1,)).astype(jnp.int32)
    sums_early, (recv["w_in"],), late = _sum_blocks_hosting(
        [recv[n] for n in BIG[1:]], [blocks[n] for n in BIG[1:]], me_arr, "sum_grads_scatter_w_in_gather_small",
        scatter=[blocks["w_in"]], gather=[small["late"]])
    summed = dict(zip(BIG[1:], sums_early))
    summed["w_in"] = _sum_blocks(recv["w_in"], blocks["w_in"], me_arr, "sum_grads_w_in")
    late = _with_own(late, [small["late"]], me)[0]
    gathered = _with_own(small["early_gathered"], small["early"], me)
    gathered[0] = jnp.concatenate([gathered[0][:, :16], late, gathered[0][:, 16:]], axis=1)

    sums = _sum8_many(gathered, "sum_small")
    stats = sums[0]
    loss = 0.5 * jnp.sum(stats[ROW_LOSS]) / D
    dmod_x_all = jnp.concatenate([gathered[0][:, r_, :] for r_ in ROWS_DMOD_X], axis=1)
    dmod_c_full = jnp.concatenate([stats[r_] for r_ in ROWS_DMOD_C] + [jnp.zeros((4 * D,), F32)])
    dm_rows = jnp.concatenate([dmod_x_all, dmod_c_full[None, :], jnp.zeros((7, 6 * D), F32)], axis=0)
    cs = w_ada.shape[2]
    dm_shard = lax.dynamic_slice_in_dim(dm_rows, chip * cs, cs, axis=1)
    dmc_shard = jnp.concatenate([dm_shard[8:9], jnp.zeros((7, cs), F32)], axis=0)
    ada = _ada_bwd(sc_all.T, dm_shard, dmc_shard, w_ada[0], m_w_ada[0], v_w_ada[0])
    part = ada[4]
    part = part * (mc == 0).astype(F32)
    _, part_all, theirs = _exchange("exchange_grads_gather_c_ctx", gather=[part], sibling=[summed[n] for n in BIG])
    theirs = dict(zip(BIG, theirs))
    dsc = _sum8(_with_own(part_all, [part], me)[0], "sum_c_ctx")

    weights = dict(c_ctx=c_ctx, w_ada=w_ada, b_ada=b_ada, w_in=w_in, attn_sink=attn_sink, gmlp_ln_g=gmlp_ln_g,
                   gmlp_ln_b=gmlp_ln_b, w_spatial=w_spatial, b_spatial=b_spatial, w_branch_a=w_branch_a,
                   w_branch_b=w_branch_b, w_out=w_out, ln1_g=ln1_g, ln1_b=ln1_b, w_ffn_in=w_ffn_in, w_ffn_out=w_ffn_out,
                   ln2_g=ln2_g, ln2_b=ln2_b)
    ms = dict(c_ctx=m_c_ctx, w_ada=m_w_ada, b_ada=m_b_ada, w_in=m_w_in, attn_sink=m_attn_sink, gmlp_ln_g=m_gmlp_ln_g,
              gmlp_ln_b=m_gmlp_ln_b, w_spatial=m_w_spatial, b_spatial=m_b_spatial, w_branch_a=m_w_branch_a,
              w_branch_b=m_w_branch_b, w_out=m_w_out, ln1_g=m_ln1_g, ln1_b=m_ln1_b, w_ffn_in=m_w_ffn_in,
              w_ffn_out=m_w_ffn_out, ln2_g=m_ln2_g, ln2_b=m_ln2_b)
    vs = dict(c_ctx=v_c_ctx, w_ada=v_w_ada, b_ada=v_b_ada, w_in=v_w_in, attn_sink=v_attn_sink, gmlp_ln_g=v_gmlp_ln_g,
              gmlp_ln_b=v_gmlp_ln_b, w_spatial=v_w_spatial, b_spatial=v_b_spatial, w_branch_a=v_w_branch_a,
              w_branch_b=v_w_branch_b, w_out=v_w_out, ln1_g=v_ln1_g, ln1_b=v_ln1_b, w_ffn_in=v_w_ffn_in,
              w_ffn_out=v_w_ffn_out, ln2_g=v_ln2_g, ln2_b=v_ln2_b)
    order = list(weights)
    grads, delta, new_m, new_v = [dict(w_ada=t[None]) for t in ada[:4]]
    c_arr = jnp.reshape(mc, (1,)).astype(jnp.int32)
    names = dict(w_in="w_in", w_a="w_branch_a", w_b="w_branch_b", w_o="w_out", w_fi="w_ffn_in", w_fo="w_ffn_out")
    for k, n in names.items():
        flip = (lambda t: t.T) if k == "w_in" else (lambda t: t)
        outs = _adamw_halves(flip(weights[n][0]), summed[k], theirs[k], flip(ms[n][0]), flip(vs[n][0]), c_arr, "adamw_" + n)
        grads[n], delta[n], new_m[n], new_v[n] = [flip(t)[None] for t in outs]

    def view(a):
        return a.reshape(-1, a.shape[-1]) if a.ndim != 1 else a.reshape(1, -1)

    small = _adamw_small(sums, dsc, *[{n: view(d[n]) for n in SMALL} for d in (weights, ms, vs)])
    for out, src in zip((grads, delta, new_m, new_v), small):
        for n in SMALL:
            out[n] = src[n].reshape(weights[n].shape)

    return (loss, grad_x[None], *[grads[n] for n in order], *[delta[n] for n in order],
            *[new_m[n] for n in order], *[new_v[n] for n in order])
```

```python
import math

import jax
import jax.numpy as jnp
import numpy as np
from jax import lax
from jax.experimental import pallas as pl
from jax.experimental.pallas import tpu as pltpu

F32 = jnp.float32
BF16 = jnp.bfloat16

D = 1024
HEAD = 64
N_KV = 2
GROUP = 4
Q_W = 512
KV_W = 128
G_W = 512
BLK = 128
N_GRP = 8
GRP_D = 64
FH = 2816
IN_W = 3840
GRID_W = 64
ROPE_BASE = 10000.0
LN_EPS = 1e-5
NEG = -1e30
ALPHA = (2 * 1) ** 0.25
SCALE = HEAD ** -0.5
GELU_K = math.sqrt(2.0 / math.pi)
GELU_A = 0.044715
ADAM_LR = 0.001
ADAM_B1 = 0.9
ADAM_B2 = 0.999
ADAM_EPS = 1e-08
ADAM_WD = 0.01
ADAM_STEP = 10
N_DEV = 8
N_SHARD = 4
FH_SHARD = FH // 2
LANES = 128
VMEM_LIMIT = 56 * 1024 * 1024
MESH = pl.DeviceIdType.MESH


def _cp(*sem):
    return pltpu.CompilerParams(dimension_semantics=sem, vmem_limit_bytes=VMEM_LIMIT)


def _resident(shape):
    return pl.BlockSpec(shape, lambda *_: (0,) * len(shape), pipeline_mode=pl.Buffered(1))


def _rows(tm, width):
    return pl.BlockSpec((tm, width), lambda i: (i, 0))


def _acc(shape):
    return pl.BlockSpec(shape, lambda *_: (0,) * len(shape))


def _dot(a, b):
    return jnp.dot(a, b, preferred_element_type=F32)


def _dot_nt(a, b):
    return lax.dot_general(a, b, (((1,), (1,)), ((), ())), preferred_element_type=F32)


def _dot_tn(a, b):
    return lax.dot_general(a, b, (((0,), (0,)), ((), ())), preferred_element_type=F32)


def _ln(x):
    mu = jnp.mean(x, axis=-1, keepdims=True)
    xc = x - mu
    var = jnp.mean(xc * xc, axis=-1, keepdims=True)
    rstd = lax.rsqrt(var + LN_EPS)
    return xc * rstd, rstd


def _ln_bwd(dxhat, xhat, rstd):
    return (dxhat - jnp.mean(dxhat, axis=-1, keepdims=True)
            - xhat * jnp.mean(dxhat * xhat, axis=-1, keepdims=True)) * rstd


def _sig(x):
    return 0.5 + 0.5 * jnp.tanh(0.5 * x)


def _gelu(x):
    t = jnp.tanh(x * (GELU_K + (GELU_K * GELU_A) * (x * x)))
    hx = 0.5 * x
    return hx + hx * t, t


def _gelu_grad(x, t):
    return 0.5 + 0.5 * t + (0.5 * x) * (1.0 - t * t) * (GELU_K + (3.0 * GELU_K * GELU_A) * (x * x))


def _colsum(v):
    return jnp.sum(v, axis=0, keepdims=True)


def _partner(x):
    w = x.shape[1]
    lane = lax.broadcasted_iota(jnp.int32, x.shape, 1)
    return jnp.where((lane & 31) < 16, pltpu.roll(x, w - 16, 1), pltpu.roll(x, 16, 1))


def _rope(x, cos, sin):
    return x * cos + _partner(x) * sin


def _unrope(g, cos, sin):
    return g * cos + _partner(g * sin)


def _rope_tables(seq):
    inv = np.float32(ROPE_BASE) ** (-np.arange(HEAD // 4, dtype=np.float32) / np.float32(HEAD // 4))
    pos = np.arange(seq)
    ar = (pos // GRID_W).astype(np.float32)[:, None] * inv
    ac = (pos % GRID_W).astype(np.float32)[:, None] * inv
    cos = np.concatenate([np.cos(ar), np.cos(ar), np.cos(ac), np.cos(ac)], axis=-1)
    sin = np.concatenate([-np.sin(ar), np.sin(ar), -np.sin(ac), np.sin(ac)], axis=-1)
    reps = (1, LANES // HEAD)
    return jnp.asarray(np.tile(cos, reps), F32), jnp.asarray(np.tile(sin, reps), F32)


def _kv_rows_of_w_in():
    return pl.BlockSpec((2 * KV_W, D), lambda *_: (Q_W // (2 * KV_W), 0))


def _ctx_fwd(ctx, modc, w_in):
    n_ctx = ctx.shape[0]

    def body(ctx_ref, mod_ref, w_ref, hc_ref, kvc_ref, vac_ref):
        xhat, _ = _ln(ctx_ref[...])
        hc = (xhat * (1.0 + mod_ref[1:2, :]) + mod_ref[0:1, :]).astype(BF16)
        hc_ref[...] = hc
        kvc = _dot_nt(hc, w_ref[...]).astype(BF16)
        kvc_ref[...] = kvc
        vac_ref[...] = _with_ones(kvc[:, KV_W:])

    return pl.pallas_call(
        body, name="ctx_fwd", grid=(1,),
        in_specs=[_acc((n_ctx, D)), _acc((8, D)), _kv_rows_of_w_in()],
        out_specs=[_acc((n_ctx, D)), _acc((n_ctx, 2 * KV_W)), _acc((n_ctx, 2 * LANES))],
        out_shape=[jax.ShapeDtypeStruct((n_ctx, D), BF16), jax.ShapeDtypeStruct((n_ctx, 2 * KV_W), BF16),
                   jax.ShapeDtypeStruct((n_ctx, 2 * LANES), BF16)],
        compiler_params=_cp("arbitrary"),
    )(ctx, modc, w_in)


def _host_start(step, comm):
    if comm is not None:
        @pl.when(step == 0)
        def _():
            comm.start()


def _host_finish(step, last, comm, forward_at=None):
    if comm is None:
        return
    if forward_at is None or forward_at >= last:
        @pl.when(step == last)
        def _():
            comm.finish()
    else:
        @pl.when(step == forward_at)
        def _():
            comm.forward()

        @pl.when(step == last)
        def _():
            comm.drain()


def _proj_fwd(x, modx, w_in, cos, sin, tm, gather=()):
    seq = x.shape[0]
    ng = len(gather)

    def body(x_ref, mod_ref, w_ref, cos_ref, sin_ref, *rest):
        h_ref, q_ref, kv_ref, va_ref, uv_ref, gab_ref = rest[ng:ng + 6]
        comm = _Gather(rest[:ng], rest[ng + 6:2 * ng + 6], *rest[2 * ng + 6:]) if ng else None
        _host_start(pl.program_id(0), comm)
        xhat, _ = _ln(x_ref[...])
        h = (xhat * (1.0 + mod_ref[1:2, :]) + mod_ref[0:1, :]).astype(BF16)
        h_ref[...] = h
        cos1, sin1 = cos_ref[...], sin_ref[...]
        cos2 = jnp.concatenate([cos1, cos1], axis=1)
        sin2 = jnp.concatenate([sin1, sin1], axis=1)
        for j in range(Q_W // 256):
            t = _dot_nt(h, w_ref[256 * j:256 * (j + 1), :])
            q_ref[:, 256 * j:256 * (j + 1)] = (_rope(t, cos2, sin2) * SCALE).astype(BF16)
        t = _dot_nt(h, w_ref[Q_W:Q_W + 2 * KV_W, :])
        kv_ref[:, :KV_W] = _rope(t[:, :KV_W], cos1, sin1).astype(BF16)
        v = t[:, KV_W:].astype(BF16)
        kv_ref[:, KV_W:] = v
        va_ref[...] = _with_ones(v)
        o = Q_W + 2 * KV_W
        for j in range(2):
            uv_ref[:, G_W * j:G_W * (j + 1)] = _dot_nt(h, w_ref[o + G_W * j:o + G_W * (j + 1), :]).astype(BF16)
        o += 2 * G_W
        for j in range(4):
            gab_ref[:, 512 * j:512 * (j + 1)] = _dot_nt(h, w_ref[o + 512 * j:o + 512 * (j + 1), :]).astype(BF16)
        _host_finish(pl.program_id(0), seq // tm - 1, comm, forward_at=(seq // tm) // 2)

    out = pl.pallas_call(
        body, name="proj_fwd", grid=(seq // tm,),
        in_specs=[_rows(tm, D), _acc((8, D)), _resident((IN_W, D)), _rows(tm, LANES), _rows(tm, LANES)] + _comm_specs(ng),
        out_specs=[_rows(tm, D), _rows(tm, Q_W), _rows(tm, 2 * KV_W), _rows(tm, 2 * LANES), _rows(tm, 2 * G_W),
                   _rows(tm, 2 * D)] + _comm_specs(ng),
        out_shape=[jax.ShapeDtypeStruct((seq, D), BF16), jax.ShapeDtypeStruct((seq, Q_W), BF16),
                   jax.ShapeDtypeStruct((seq, 2 * KV_W), BF16), jax.ShapeDtypeStruct((seq, 2 * LANES), BF16),
                   jax.ShapeDtypeStruct((seq, 2 * G_W), BF16), jax.ShapeDtypeStruct((seq, 2 * D), BF16)] + _gathered_shapes(gather),
        scratch_shapes=_comm_scratch(ng) if ng else [],
        compiler_params=_cp("arbitrary"),
    )(x, modx, w_in, cos, sin, *gather)
    return out[:6], out[6:]


def _stack_heads(x, hk):
    return jnp.concatenate([x[:, (hk * GROUP + g) * HEAD:(hk * GROUP + g + 1) * HEAD] for g in range(GROUP)], axis=0)


def _band_masks(n, nb):
    rows = GROUP * BLK
    qi = lax.broadcasted_iota(jnp.int32, (rows, BLK), 0) & (BLK - 1)
    kj = lax.broadcasted_iota(jnp.int32, (rows, BLK), 1)
    return (kj >= qi) & (n > 0), (kj <= qi) & (n < nb - 1)


def _attn_scores(q, k_refs, hk, masks):
    q4 = _stack_heads(q, hk)
    ks = [r[:, hk * HEAD:(hk + 1) * HEAD] for r in k_refs]
    s = [_dot_nt(q4, k) for k in ks]
    s[1] = jnp.where(masks[0], s[1], NEG)
    s[3] = jnp.where(masks[1], s[3], NEG)
    return q4, ks, s


def _sink_rows(sink_ref, hk):
    rows = GROUP * BLK
    rg = lax.broadcasted_iota(jnp.int32, (rows, 1), 0) >> 7
    sink_v = jnp.full((rows, 1), sink_ref[0, hk * GROUP], F32)
    for g in range(1, GROUP):
        sink_v = jnp.where(rg == g, sink_ref[0, hk * GROUP + g], sink_v)
    return sink_v


def _with_ones(v):
    ones = jnp.ones((v.shape[0], HEAD), v.dtype)
    return jnp.concatenate([v[:, :HEAD], ones, v[:, HEAD:], ones], axis=1)


def _kv_specs(nb, qb):
    def spec(d):
        return pl.BlockSpec((BLK, 2 * KV_W), lambda n: (jnp.clip(qb * n + d, 0, nb - 1), 0))
    return [spec(d) for d in range(-1, qb + 1)]


def _attn_fwd(q, kv, va, kvc, vac, sink, gather=()):
    seq = q.shape[0]
    nb = seq // BLK
    n_ctx = kvc.shape[0]
    ng = len(gather)
    Q_BLOCKS = 1
    nkv = Q_BLOCKS + 2
    steps = nb // Q_BLOCKS

    def body(q_ref, *rest):
        kv_refs, va_refs = rest[:nkv], rest[nkv:2 * nkv]
        kvc_ref, vac_ref, sink_ref = rest[2 * nkv:2 * nkv + 3]
        rest = rest[2 * nkv + 3:]
        o_ref, lse_ref = rest[ng:ng + 2]
        comm = _Gather(rest[:ng], rest[ng + 2:2 * ng + 2], *rest[2 * ng + 2:]) if ng else None
        n = pl.program_id(0)
        _host_start(n, comm)
        lane = lax.broadcasted_iota(jnp.int32, (BLK, LANES), 1)
        for sub in range(Q_BLOCKS):
            rs = slice(sub * BLK, (sub + 1) * BLK)
            q = q_ref[rs, :]
            outs = []
            lse_all = jnp.zeros((BLK, LANES), F32)
            masks = _band_masks(Q_BLOCKS * n + sub, nb)
            for hk in range(N_KV):
                _, _, s = _attn_scores(q, (kvc_ref,) + kv_refs[sub:sub + 3], hk, masks)
                sink_v = _sink_rows(sink_ref, hk)
                tile_max = s[1]
                for t in [s[0][:, i * LANES:(i + 1) * LANES] for i in range(n_ctx // LANES)] + s[2:]:
                    tile_max = jnp.maximum(tile_max, t)
                m = jnp.maximum(sink_v, jnp.max(tile_max, axis=-1, keepdims=True))
                o = jnp.zeros((GROUP * BLK, LANES), F32)
                for t, va_ref in zip(s, (vac_ref,) + va_refs[sub:sub + 3]):
                    o = o + _dot(jnp.exp((t - m).astype(BF16)), va_ref[:, hk * LANES:(hk + 1) * LANES])
                denom = o[:, HEAD:HEAD + 1] + jnp.exp(sink_v - m)
                o4 = o[:, :HEAD] * (1.0 / denom)
                lse4 = m + jnp.log(denom)
                for g in range(GROUP):
                    outs.append(o4[g * BLK:(g + 1) * BLK, :])
                    lse_all = jnp.where(lane == hk * GROUP + g, lse4[g * BLK:(g + 1) * BLK, :], lse_all)
            o_ref[rs, :] = jnp.concatenate(outs, axis=1).astype(BF16)
            lse_ref[rs, :] = lse_all
        _host_finish(n, steps - 1, comm, forward_at=(3 * steps) // 4)

    tq = Q_BLOCKS * BLK
    out = pl.pallas_call(
        body, name="attn_fwd", grid=(steps,),
        in_specs=[_rows(tq, Q_W)] + _kv_specs(nb, Q_BLOCKS) + _kv_specs(nb, Q_BLOCKS)
        + [_acc((n_ctx, 2 * KV_W)), _acc((n_ctx, 2 * LANES)), pl.BlockSpec(memory_space=pltpu.SMEM)] + _comm_specs(ng),
        out_specs=[_rows(tq, Q_W), _rows(tq, LANES)] + _comm_specs(ng),
        out_shape=[jax.ShapeDtypeStruct((seq, Q_W), BF16), jax.ShapeDtypeStruct((seq, LANES), F32)] + _gathered_shapes(gather),
        scratch_shapes=_comm_scratch(ng) if ng else [],
        compiler_params=_cp("arbitrary"),
    )(q, *([kv] * nkv), *([va] * nkv), kvc, vac, sink, *gather)
    return out[:2], out[2:]


def _gmlp_chunk(u, vb, gp_ref, ws_ref, bias_ref):
    gu, tu = _gelu(u)
    gv, tv = _gelu(vb)
    vhat, rstd = _ln(gv)
    vn = (vhat * gp_ref[0:1, :] + gp_ref[1:2, :]).astype(BF16)
    s = bias_ref[...] + jnp.concatenate(
        [_dot(ws_ref[g * BLK:(g + 1) * BLK, :], vn[:, g * GRP_D:(g + 1) * GRP_D]) for g in range(N_GRP)], axis=1)
    return gu, tu, tv, vhat, rstd, vn, s


def _mix_fwd(uv, gab, ya, gp, ws_stack, bias_full, w_a, w_b, w_o, tm, gather=()):
    seq = uv.shape[0]
    ng = len(gather)
    steps = seq // tm

    def body(uv_ref, gab_ref, ya_ref, gp_ref, ws_ref, bias_ref, wa_ref, wb_ref, wo_ref, *rest):
        a_ref, b_ref, mix_ref, merged_ref, yb_ref = rest[ng:ng + 5]
        comm = _Gather(rest[:ng], rest[ng + 5:2 * ng + 5], *rest[2 * ng + 5:]) if ng else None
        _host_start(pl.program_id(0), comm)
        for c in range(tm // BLK):
            rs = slice(c * BLK, (c + 1) * BLK)
            gu, _, _, _, _, _, s = _gmlp_chunk(uv_ref[rs, :G_W].astype(F32), uv_ref[rs, G_W:].astype(F32), gp_ref, ws_ref, bias_ref)
            yb_ref[rs, :] = (gu * s).astype(BF16)
        a = _dot(ya_ref[...], wa_ref[...])
        b = _dot(yb_ref[...], wb_ref[...])
        a_ref[...] = a.astype(BF16)
        b_ref[...] = b.astype(BF16)
        merged = (_sig(gab_ref[:, :D].astype(F32)) * a + _sig(gab_ref[:, D:].astype(F32)) * b).astype(BF16)
        merged_ref[...] = merged
        mix_ref[...] = _dot(merged, wo_ref[...])
        _host_finish(pl.program_id(0), steps - 1, comm, forward_at=(3 * steps) // 4)

    out = pl.pallas_call(
        body, name="mix_fwd", grid=(steps,),
        in_specs=[_rows(tm, 2 * G_W), _rows(tm, 2 * D), _rows(tm, Q_W), _acc((8, G_W)),
                  _resident((N_GRP * BLK, BLK)), _acc((BLK, G_W)),
                  _resident((Q_W, D)), _resident((G_W, D)), _resident((D, D))] + _comm_specs(ng),
        out_specs=[_rows(tm, D), _rows(tm, D), _rows(tm, D), _rows(tm, D), _rows(tm, G_W)] + _comm_specs(ng),
        out_shape=[jax.ShapeDtypeStruct((seq, D), BF16), jax.ShapeDtypeStruct((seq, D), BF16),
                   jax.ShapeDtypeStruct((seq, D), F32), jax.ShapeDtypeStruct((seq, D), BF16),
                   jax.ShapeDtypeStruct((seq, G_W), BF16)] + _gathered_shapes(gather),
        scratch_shapes=_comm_scratch(ng) if ng else [],
        compiler_params=_cp("arbitrary"),
    )(uv, gab, ya, gp, ws_stack, bias_full, w_a, w_b, w_o, *gather)
    return out[:5], out[5:]


def _mid_recompute(x_ref, mix_ref, vec_ref):
    r1 = ALPHA * x_ref[...] + vec_ref[0:1, :] * mix_ref[...]
    xh1, rstd1 = _ln(r1)
    xmid = xh1 * vec_ref[1:2, :] + vec_ref[2:3, :]
    xh2, rstd2 = _ln(xmid)
    return xh1, rstd1, xmid, xh2, rstd2


def _ffn(x, mix, tgt, vec, w_fi, w_fo, tm):
    seq = x.shape[0]

    def body(x_ref, mix_ref, tgt_ref, vec_ref, wi_ref, wo_ref, act_ref, h2_ref, dff_ref, df_ref, dr1_ref, st_ref, gu_ref):
        @pl.when(pl.program_id(0) == 0)
        def _():
            st_ref[...] = jnp.zeros_like(st_ref)

        xh1, rstd1, xmid, xh2, rstd2 = _mid_recompute(x_ref, mix_ref, vec_ref)
        h2 = (xh2 * (1.0 + vec_ref[4:5, :]) + vec_ref[3:4, :]).astype(BF16)
        h2_ref[...] = h2
        halves = [(slice(hh * FH_SHARD, (hh + 1) * FH_SHARD), slice(FH + hh * FH_SHARD, FH + (hh + 1) * FH_SHARD))
                  for hh in range(2)]
        for hh, (cs, cu) in enumerate(halves):
            g = _dot(h2, wi_ref[hh])
            u = _dot(h2, wi_ref[2 + hh])
            gu_ref[:, cs] = g
            gu_ref[:, cu] = u
            act_ref[:, cs] = (g * _sig(g) * u).astype(BF16)
        f = _dot(act_ref[...], wo_ref[...])
        r2 = ALPHA * xmid + vec_ref[5:6, :] * f
        yh, rstd = _ln(r2)
        y = yh * vec_ref[6:7, :] + vec_ref[7:8, :]
        err = y - tgt_ref[...]
        dy = err / D
        dr2 = _ln_bwd(dy * vec_ref[6:7, :], yh, rstd)
        st_ref[0:1, :] += _colsum(err * err)
        st_ref[1:2, :] += _colsum(dy * yh)
        st_ref[2:3, :] += _colsum(dy)
        st_ref[3:4, :] += _colsum(dr2 * f)

        df = (dr2 * vec_ref[5:6, :]).astype(BF16)
        df_ref[...] = df
        da_all = _dot_nt(df, wo_ref[...])
        for cs, cu in halves:
            da = da_all[:, cs]
            g = gu_ref[:, cs]
            u = gu_ref[:, cu]
            sg = _sig(g)
            dff_ref[:, cs] = (da * u * sg * (1.0 + g * (1.0 - sg))).astype(BF16)
            dff_ref[:, cu] = (da * g * sg).astype(BF16)
        dh2 = _dot_nt(dff_ref[:, :FH_SHARD], wi_ref[0])
        for s in range(1, N_SHARD):
            dh2 = dh2 + _dot_nt(dff_ref[:, s * FH_SHARD:(s + 1) * FH_SHARD], wi_ref[s])
        dxmid = _ln_bwd(dh2 * (1.0 + vec_ref[4:5, :]), xh2, rstd2) + ALPHA * dr2
        dr1 = _ln_bwd(dxmid * vec_ref[1:2, :], xh1, rstd1)
        dr1_ref[...] = dr1
        st_ref[8:9, :] += _colsum(dh2 * xh2)
        st_ref[9:10, :] += _colsum(dh2)
        st_ref[10:11, :] += _colsum(dxmid * xh1)
        st_ref[11:12, :] += _colsum(dxmid)
        st_ref[12:13, :] += _colsum(dr1 * mix_ref[...])

    return pl.pallas_call(
        body, name="ffn", grid=(seq // tm,),
        in_specs=[_rows(tm, D), _rows(tm, D), _rows(tm, D), _acc((8, D)), _resident((N_SHARD, D, FH_SHARD)), _resident((FH, D))],
        out_specs=[_rows(tm, FH), _rows(tm, D), _rows(tm, 2 * FH), _rows(tm, D), _rows(tm, D), _acc((16, D))],
        out_shape=[jax.ShapeDtypeStruct((seq, FH), BF16), jax.ShapeDtypeStruct((seq, D), BF16),
                   jax.ShapeDtypeStruct((seq, 2 * FH), BF16), jax.ShapeDtypeStruct((seq, D), BF16),
                   jax.ShapeDtypeStruct((seq, D), F32), jax.ShapeDtypeStruct((16, D), F32)],
        scratch_shapes=[pltpu.VMEM((tm, 2 * FH), F32)],
        compiler_params=_cp("arbitrary"),
    )(x, mix, tgt, vec, w_fi, w_fo)


def _mix_bwd(dr1, a, b, gab, uv, merged, ya, yb, vec, gp, ws_stack, ws_stack_t, bias_full, w_a, w_b, w_o, tm, scatter=()):
    seq = dr1.shape[0]
    last = seq // tm - 1
    ns = len(scatter)

    def body(dr1_ref, a_ref, b_ref, gab_ref, uv_ref, mg_ref, ya_ref, yb_ref, vec_ref, gp_ref, ws_ref, wst_ref, bias_ref,
             wa_ref, wb_ref, wo_ref, *rest):
        dya_ref, dp_ref, dws_ref, dbs_ref, st_ref, gwo_ref, gwa_ref, gwb_ref = rest[ns:ns + 8]
        acc_o, acc_a, acc_b = rest[2 * ns + 8:2 * ns + 11]
        comm = _AllToAll(rest[:ns], rest[ns + 8:2 * ns + 8], *rest[2 * ns + 11:]) if ns else None
        _host_start(pl.program_id(0), comm)

        @pl.when(pl.program_id(0) == 0)
        def _():
            dws_ref[...] = jnp.zeros_like(dws_ref)
            dbs_ref[...] = jnp.zeros_like(dbs_ref)
            st_ref[...] = jnp.zeros_like(st_ref)
            acc_o[...] = jnp.zeros_like(acc_o)
            acc_a[...] = jnp.zeros_like(acc_a)
            acc_b[...] = jnp.zeros_like(acc_b)

        dmix = (dr1_ref[...] * vec_ref[0:1, :]).astype(BF16)
        acc_o[...] += _dot_tn(mg_ref[...], dmix)
        dmerged = _dot_nt(dmix, wo_ref[...])
        sa = _sig(gab_ref[:, :D].astype(F32))
        sb = _sig(gab_ref[:, D:].astype(F32))
        da_f = dmerged * sa
        db_f = dmerged * sb
        da = da_f.astype(BF16)
        db = db_f.astype(BF16)
        dp_ref[:, 2 * G_W:2 * G_W + D] = (da_f * a_ref[...].astype(F32) * (1.0 - sa)).astype(BF16)
        dp_ref[:, 2 * G_W + D:] = (db_f * b_ref[...].astype(F32) * (1.0 - sb)).astype(BF16)
        dya_ref[...] = _dot_nt(da, wa_ref[...]).astype(BF16)
        dyb = _dot_nt(db, wb_ref[...])
        acc_a[...] += _dot_tn(ya_ref[...], da)
        acc_b[...] += _dot_tn(yb_ref[...], db)

        @pl.when(pl.program_id(0) == last)
        def _():
            gwo_ref[...] = acc_o[...].astype(BF16)
            gwa_ref[...] = acc_a[...].astype(BF16)
            gwb_ref[...] = acc_b[...].astype(BF16)

        for c in range(tm // BLK):
            rs = slice(c * BLK, (c + 1) * BLK)
            u = uv_ref[rs, :G_W].astype(F32)
            vb = uv_ref[rs, G_W:].astype(F32)
            gu, tu, tv, vhat, rstd, vn, s = _gmlp_chunk(u, vb, gp_ref, ws_ref, bias_ref)
            dyb_c = dyb[rs, :]
            ds = dyb_c * gu
            du = dyb_c * s * _gelu_grad(u, tu)
            ds_b = ds.astype(BF16)
            dvn_g = []
            for g in range(N_GRP):
                cg = slice(g * GRP_D, (g + 1) * GRP_D)
                dvn_g.append(_dot(wst_ref[:, g * BLK:(g + 1) * BLK], ds_b[:, cg]))
                dws_ref[g * BLK:(g + 1) * BLK, :] += _dot_nt(ds_b[:, cg], vn[:, cg])
            dvn = jnp.concatenate(dvn_g, axis=1)
            dbs_ref[...] += ds
            st_ref[0:1, :] += _colsum(dvn * vhat)
            st_ref[1:2, :] += _colsum(dvn)
            dgv = _ln_bwd(dvn * gp_ref[0:1, :], vhat, rstd)
            dvb = dgv * _gelu_grad(vb, tv)
            dp_ref[rs, :G_W] = du.astype(BF16)
            dp_ref[rs, G_W:2 * G_W] = dvb.astype(BF16)
        _host_finish(pl.program_id(0), last, comm)

    pw = 2 * G_W + 2 * D
    out = pl.pallas_call(
        body, name="mix_bwd", grid=(seq // tm,),
        in_specs=[_rows(tm, D), _rows(tm, D), _rows(tm, D), _rows(tm, 2 * D), _rows(tm, 2 * G_W), _rows(tm, D), _rows(tm, Q_W),
                  _rows(tm, G_W), _acc((8, D)), _acc((8, G_W)),
                  _resident((N_GRP * BLK, BLK)), _resident((BLK, N_GRP * BLK)), _acc((BLK, G_W)),
                  _resident((Q_W, D)), _resident((G_W, D)), _resident((D, D))] + _comm_specs(ns),
        out_specs=[_rows(tm, Q_W), _rows(tm, pw), _acc((N_GRP * BLK, BLK)), _acc((BLK, G_W)), _acc((8, G_W)),
                   _acc((D, D)), _acc((Q_W, D)), _acc((G_W, D))] + _comm_specs(ns),
        out_shape=[jax.ShapeDtypeStruct((seq, Q_W), BF16), jax.ShapeDtypeStruct((seq, pw), BF16),
                   jax.ShapeDtypeStruct((N_GRP * BLK, BLK), F32), jax.ShapeDtypeStruct((BLK, G_W), F32),
                   jax.ShapeDtypeStruct((8, G_W), F32), jax.ShapeDtypeStruct((D, D), BF16),
                   jax.ShapeDtypeStruct((Q_W, D), BF16), jax.ShapeDtypeStruct((G_W, D), BF16)]
        + [jax.ShapeDtypeStruct(v.shape, v.dtype) for v in scatter],
        scratch_shapes=[pltpu.VMEM((D, D), F32), pltpu.VMEM((Q_W, D), F32), pltpu.VMEM((G_W, D), F32)]
        + (_comm_scratch(ns) if ns else []),
        compiler_params=_cp("arbitrary"),
    )(dr1, a, b, gab, uv, merged, ya, yb, vec, gp, ws_stack, ws_stack_t, bias_full, w_a, w_b, w_o, *scatter)
    return out[:8], out[8:]


def _attn_bwd(q, kv, kvc, sink, dya, ya, lse, scatter=()):
    seq = q.shape[0]
    nb = seq // BLK
    n_ctx = kvc.shape[0]
    ns = len(scatter)
    Q_BLOCKS = 2
    nkv = Q_BLOCKS + 2
    steps = nb // Q_BLOCKS

    def body(q_ref, *rest):
        kv_refs = rest[:nkv]
        kvc_ref, sink_ref, do_ref, o_ref, lse_ref = rest[nkv:nkv + 5]
        rest = rest[nkv + 5:]
        dq_ref, dkv_ref, dkvc_ref, dsink_ref = rest[ns:ns + 4]
        comm = _AllToAll(rest[:ns], rest[ns + 4:2 * ns + 4], *rest[2 * ns + 4:]) if ns else None
        n = pl.program_id(0)
        _host_start(n, comm)

        @pl.when(n == 0)
        def _():
            dkv_ref[...] = jnp.zeros_like(dkv_ref)
            dkvc_ref[...] = jnp.zeros_like(dkvc_ref)
            dsink_ref[...] = jnp.zeros_like(dsink_ref)

        lane = lax.broadcasted_iota(jnp.int32, (1, LANES), 1)
        for sub in range(Q_BLOCKS):
            rs = slice(sub * BLK, (sub + 1) * BLK)
            blk = Q_BLOCKS * n + sub
            q = q_ref[rs, :]
            do = do_ref[rs, :]
            out = o_ref[rs, :]
            lse_all = lse_ref[rs, :]
            k_refs = (kvc_ref,) + kv_refs[sub:sub + 3]
            masks = _band_masks(blk, nb)
            dqs, dks, dvs = [], [], []
            for hk in range(N_KV):
                q4, ks, s = _attn_scores(q, k_refs, hk, masks)
                vs = [r[:, KV_W + hk * HEAD:KV_W + (hk + 1) * HEAD] for r in k_refs]
                lse4 = jnp.concatenate([lse_all[:, hk * GROUP + g:hk * GROUP + g + 1] for g in range(GROUP)], axis=0)
                do4 = _stack_heads(do, hk)
                delta = jnp.sum(do4.astype(F32) * _stack_heads(out, hk).astype(F32), axis=-1, keepdims=True)
                p = [jnp.exp((t - lse4).astype(BF16)) for t in s]
                ds = [t * (_dot_nt(do4, v) - delta).astype(BF16) for t, v in zip(p, vs)]
                dq4 = _dot(ds[0], ks[0])
                for t, k in zip(ds[1:], ks[1:]):
                    dq4 = dq4 + _dot(t, k)
                dq4 = dq4 * SCALE
                dqs += [dq4[g * BLK:(g + 1) * BLK, :] for g in range(GROUP)]
                dks.append([_dot_tn(t, q4) for t in ds])
                dvs.append([_dot_tn(t, do4) for t in p])
                ps = jnp.exp(_sink_rows(sink_ref, hk) - lse4) * delta
                for g in range(GROUP):
                    part = -jnp.sum(ps[g * BLK:(g + 1) * BLK, :], axis=0, keepdims=True)
                    dsink_ref[0:1, :] += jnp.where(lane == hk * GROUP + g, part, 0.0)
            dq_ref[rs, :] = jnp.concatenate(dqs, axis=1)

            def piece(i):
                return jnp.concatenate([dks[0][i], dks[1][i], dvs[0][i], dvs[1][i]], axis=1)

            dkvc_ref[...] += piece(0)
            starts = (jnp.maximum(blk - 1, 0), blk, jnp.minimum(blk + 1, nb - 1))
            for i, st in enumerate(starts):
                r = pl.ds(pl.multiple_of(st * BLK, BLK), BLK)
                dkv_ref[r, :] += piece(i + 1)
        _host_finish(n, steps - 1, comm)

    tq = Q_BLOCKS * BLK
    out = pl.pallas_call(
        body, name="attn_bwd", grid=(steps,),
        in_specs=[_rows(tq, Q_W)] + _kv_specs(nb, Q_BLOCKS) + [_acc((n_ctx, 2 * KV_W)), pl.BlockSpec(memory_space=pltpu.SMEM),
                                                     _rows(tq, Q_W), _rows(tq, Q_W), _rows(tq, LANES)] + _comm_specs(ns),
        out_specs=[_rows(tq, Q_W), _acc((seq, 2 * KV_W)), _acc((n_ctx, 2 * KV_W)), _acc((8, LANES))] + _comm_specs(ns),
        out_shape=[jax.ShapeDtypeStruct((seq, Q_W), F32), jax.ShapeDtypeStruct((seq, 2 * KV_W), F32),
                   jax.ShapeDtypeStruct((n_ctx, 2 * KV_W), F32), jax.ShapeDtypeStruct((8, LANES), F32)]
        + [jax.ShapeDtypeStruct(v.shape, v.dtype) for v in scatter],
        scratch_shapes=_comm_scratch(ns) if ns else [],
        compiler_params=_cp("arbitrary"),
    )(q, *([kv] * nkv), kvc, sink, dya, ya, lse, *scatter)
    return out[:4], out[4:]


def _proj_bwd(dq, dkv, dpb, x, dr1, modx, w_in, cos, sin, tm, scatter=()):
    seq = x.shape[0]
    pw = IN_W - Q_W - 2 * KV_W
    ns = len(scatter)

    def body(dq_ref, dkv_ref, dpb_ref, x_ref, dr1_ref, mod_ref, w_ref, cos_ref, sin_ref, *rest):
        dqkv_ref, gx_ref, st_ref = rest[ns:ns + 3]
        comm = _AllToAll(rest[:ns], rest[ns + 3:2 * ns + 3], *rest[2 * ns + 3:]) if ns else None
        _host_start(pl.program_id(0), comm)

        @pl.when(pl.program_id(0) == 0)
        def _():
            st_ref[...] = jnp.zeros_like(st_ref)

        cos1, sin1 = cos_ref[...], sin_ref[...]
        cos2 = jnp.concatenate([cos1, cos1], axis=1)
        sin2 = jnp.concatenate([sin1, sin1], axis=1)
        for j in range(Q_W // 256):
            cs = slice(256 * j, 256 * (j + 1))
            dqkv_ref[:, cs] = _unrope(dq_ref[:, cs], cos2, sin2).astype(BF16)
        dqkv_ref[:, Q_W:Q_W + KV_W] = _unrope(dkv_ref[:, :KV_W], cos1, sin1).astype(BF16)
        dqkv_ref[:, Q_W + KV_W:] = dkv_ref[:, KV_W:].astype(BF16)
        o = Q_W + 2 * KV_W
        dh = _dot(dqkv_ref[...], w_ref[:o, :]) + _dot(dpb_ref[...], w_ref[o:, :])
        xhat, rstd = _ln(x_ref[...])
        st_ref[0:1, :] += _colsum(dh)
        st_ref[1:2, :] += _colsum(dh * xhat)
        gx_ref[...] = _ln_bwd(dh * (1.0 + mod_ref[1:2, :]), xhat, rstd) + ALPHA * dr1_ref[...]
        _host_finish(pl.program_id(0), seq // tm - 1, comm)

    out = pl.pallas_call(
        body, name="proj_bwd", grid=(seq // tm,),
        in_specs=[_rows(tm, Q_W), _rows(tm, 2 * KV_W), _rows(tm, pw), _rows(tm, D), _rows(tm, D), _acc((8, D)),
                  _resident((IN_W, D)), _rows(tm, LANES), _rows(tm, LANES)] + _comm_specs(ns),
        out_specs=[_rows(tm, Q_W + 2 * KV_W), _rows(tm, D), _acc((8, D))] + _comm_specs(ns),
        out_shape=[jax.ShapeDtypeStruct((seq, Q_W + 2 * KV_W), BF16), jax.ShapeDtypeStruct((seq, D), F32),
                   jax.ShapeDtypeStruct((8, D), F32)] + [jax.ShapeDtypeStruct(v.shape, v.dtype) for v in scatter],
        scratch_shapes=_comm_scratch(ns) if ns else [],
        compiler_params=_cp("arbitrary"),
    )(dq, dkv, dpb, x, dr1, modx, w_in, cos, sin, *scatter)
    return out[:3], out[3:]


def _ctx_bwd(dkvc, ctx, hc, w_in):
    n_ctx = ctx.shape[0]

    def body(dkvc_ref, ctx_ref, hc_ref, w_ref, dw_ref, st_ref):
        d = dkvc_ref[...].astype(BF16)
        dw_ref[...] = _dot_tn(d, hc_ref[...])
        dhc = _dot(d, w_ref[...])
        xhat, _ = _ln(ctx_ref[...])
        st_ref[...] = jnp.zeros_like(st_ref)
        st_ref[0:1, :] = _colsum(dhc)
        st_ref[1:2, :] = _colsum(dhc * xhat)

    return pl.pallas_call(
        body, name="ctx_bwd", grid=(1,),
        in_specs=[_acc((n_ctx, 2 * KV_W)), _acc((n_ctx, D)), _acc((n_ctx, D)), _kv_rows_of_w_in()],
        out_specs=[_acc((2 * KV_W, D)), _acc((8, D))],
        out_shape=[jax.ShapeDtypeStruct((2 * KV_W, D), F32), jax.ShapeDtypeStruct((8, D), F32)],
        compiler_params=_cp("arbitrary"),
    )(dkvc, ctx, hc, w_in)


def _tn_matmul(a, b, tn, name, out_dtype, shard_major=False, init=None, tk=512, out_rows=None, into=None,
               scatter=(), gather=()):
    t, ka = a.shape
    n = b.shape[1]
    tk = min(tk, t)
    nk = t // tk
    nj = n // tn
    has_init = init is not None
    in_place = into is not None
    assert not (scatter and gather) and not (in_place and (nj != 1 or shard_major or out_rows))
    moved = list(scatter) + list(gather)
    pattern = _AllToAll if scatter else _Gather
    ns = len(moved)
    n_in = 2 + has_init + in_place
    n_scr = 3 if in_place else 1

    def body(*refs):
        a_ref, b_ref = refs[:2]
        i_ref = refs[2] if has_init else None
        rest = refs[n_in:]
        o_ref = rest[ns]
        acc_ref = rest[2 * ns + 1]
        comm = pattern(rest[:ns], rest[ns + 1:2 * ns + 1], *rest[2 * ns + 1 + n_scr:]) if ns else None
        k = pl.program_id(1)
        step = pl.program_id(0) * nk + k
        _host_start(step, comm)

        @pl.when(k == 0)
        def _():
            acc_ref[...] = jnp.zeros_like(acc_ref)
            if has_init:
                acc_ref[ka - init.shape[0]:, :] = i_ref[...]

        acc_ref[...] += _dot_tn(a_ref[...], b_ref[...])

        @pl.when(k == nk - 1)
        def _():
            if in_place:
                stage_ref, sem = rest[2 * ns + 2:2 * ns + 4]
                stage_ref[...] = acc_ref[...].astype(out_dtype)
                write = pltpu.make_async_copy(stage_ref, o_ref.at[pl.ds(into[1], ka), :], sem.at[0])
                write.start()
                write.wait()
            else:
                o_ref[...] = acc_ref[...].astype(out_dtype)

        _host_finish(step, nj * nk - 1, comm, forward_at=(3 * nj * nk) // 4 if gather else None)

    in_specs = [pl.BlockSpec((tk, ka), lambda j, k: (k, 0)), pl.BlockSpec((tk, tn), lambda j, k: (k, j))]
    args = [a, b]
    if has_init:
        in_specs.append(pl.BlockSpec((init.shape[0], tn), lambda j, k: (0, j)))
        args.append(init)
    scratch = [pltpu.VMEM((ka, tn), F32)]
    aliases = {}
    if in_place:
        in_specs.append(pl.BlockSpec(memory_space=pl.ANY))
        args.append(into[0])
        aliases = {n_in - 1: 0}
        out_spec = pl.BlockSpec(memory_space=pl.ANY)
        out_shape = jax.ShapeDtypeStruct(into[0].shape, into[0].dtype)
        scratch += [pltpu.VMEM((ka, tn), out_dtype), pltpu.SemaphoreType.DMA((1,))]
    elif shard_major:
        out_spec = pl.BlockSpec((None, ka, tn), lambda j, k: (j, 0, 0))
        out_shape = jax.ShapeDtypeStruct((nj, ka, tn), out_dtype)
    else:
        out_spec = pl.BlockSpec((ka, tn), lambda j, k: (0, j))
        out_shape = jax.ShapeDtypeStruct((out_rows or ka, n), out_dtype)
    out = pl.pallas_call(
        body, name=name, grid=(nj, nk), in_specs=in_specs + _comm_specs(ns), out_specs=[out_spec] + _comm_specs(ns),
        out_shape=[out_shape] + [jax.ShapeDtypeStruct(v.shape, v.dtype) for v in scatter] + _gathered_shapes(gather),
        scratch_shapes=scratch + (_comm_scratch(ns) if ns else []), input_output_aliases=aliases,
        compiler_params=_cp("arbitrary", "arbitrary"),
    )(*args, *moved)
    return (out[0], out[1:]) if ns else out[0]


ADA_TILE = 512


def _gather_and_modulate(c_rows, c_ctx, w_half, w_ada):
    cs = w_ada.shape[1]
    vmem = pl.BlockSpec(memory_space=pltpu.VMEM)

    def body(c_ref, cctx_ref, wh_ref, wada_ref, cg_ref, wg_ref, mod_ref, modg_ref, sc_ref,
             c_v, wada_v, mod_v, send_c, recv_c, send_w, recv_w, send_m, recv_m, local):
        gc = _Gather([c_ref], [cg_ref], send_c, recv_c)
        gw = _Gather([wh_ref], [wg_ref], send_w, recv_w)
        gm = _Gather([mod_ref], [modg_ref], send_m, recv_m)
        px, py, pc = _my_pos()
        mine = 4 * px + 2 * py + pc
        gc.start()
        gw.start()
        own_c = pltpu.make_async_copy(c_ref, cg_ref.at[mine], local.at[0])
        load_w = pltpu.make_async_copy(wada_ref, wada_v, local.at[1])
        own_c.start()
        load_w.start()
        gc.finish()
        own_c.wait()
        load_c = pltpu.make_async_copy(cg_ref, c_v, local.at[2])
        load_c.start()
        load_c.wait()
        cc = jnp.concatenate([c_v[i, 0:1, :] for i in range(N_DEV)] + [cctx_ref[...], jnp.zeros((7, D), F32)], axis=0)
        sc = cc * _sig(cc)
        sc_ref[...] = sc
        load_w.wait()
        for j in range(cs // ADA_TILE):
            cols = slice(j * ADA_TILE, (j + 1) * ADA_TILE)
            mod_v[:, cols] = _dot(sc.astype(BF16), wada_v[:, cols].astype(BF16))
        store_m = pltpu.make_async_copy(mod_v, mod_ref, local.at[3])
        store_m.start()
        store_m.wait()
        own_m = pltpu.make_async_copy(mod_ref, modg_ref.at[mine], local.at[4])
        own_m.start()
        gm.start()
        gm.finish()
        own_m.wait()
        gw.finish()

    any_ = pl.BlockSpec(memory_space=pl.ANY)
    return pl.pallas_call(
        body, name="gather_and_modulate",
        in_specs=[any_, vmem, any_, any_], out_specs=[any_, any_, any_, any_, vmem],
        scratch_shapes=[pltpu.VMEM((N_DEV, 8, D), F32), pltpu.VMEM((D, cs), F32), pltpu.VMEM((16, cs), F32)]
        + _comm_scratch(1) + _comm_scratch(1) + _comm_scratch(1) + [pltpu.SemaphoreType.DMA((5,))],
        out_shape=[jax.ShapeDtypeStruct((N_DEV, 8, D), F32), jax.ShapeDtypeStruct((N_DEV,) + w_half.shape, w_half.dtype),
                   jax.ShapeDtypeStruct((16, cs), F32), jax.ShapeDtypeStruct((N_DEV, 16, cs), F32),
                   jax.ShapeDtypeStruct((16, D), F32)],
        compiler_params=pltpu.CompilerParams(vmem_limit_bytes=VMEM_LIMIT),
    )(c_rows, c_ctx, w_half, w_ada)


def _ada_bwd(sc_all_t, dm_all, dmc, w_ada, m, v):
    cs = w_ada.shape[1]

    def body(st_ref, dm_ref, dmc_ref, w_ref, m_ref, v_ref, gw_ref, d_ref, nm_ref, nv_ref, part_ref):
        @pl.when(pl.program_id(0) == 0)
        def _():
            part_ref[...] = jnp.zeros_like(part_ref)

        g = _dot(st_ref[...].astype(BF16), dm_ref[...].astype(BF16))
        gw_ref[...] = g
        d_ref[...], nm_ref[...], nv_ref[...] = _adam_update(w_ref[...], g, m_ref[...], v_ref[...])
        part_ref[...] += _dot_nt(dmc_ref[...].astype(BF16), w_ref[...].astype(BF16))

    cols = pl.BlockSpec((D, ADA_TILE), lambda j: (0, j))
    shp = jax.ShapeDtypeStruct((D, cs), F32)
    return pl.pallas_call(
        body, name="ada_bwd_adamw", grid=(cs // ADA_TILE,),
        in_specs=[_acc((D, 16)), pl.BlockSpec((16, ADA_TILE), lambda j: (0, j)), pl.BlockSpec((8, ADA_TILE), lambda j: (0, j)),
                  cols, cols, cols],
        out_specs=[cols] * 4 + [_acc((8, D))],
        out_shape=[shp] * 4 + [jax.ShapeDtypeStruct((8, D), F32)],
        compiler_params=_cp("arbitrary"),
    )(sc_all_t, dm_all, dmc, w_ada, m, v)


def _sum8(x, name, tr=256):
    _, r, c = x.shape
    tr = min(tr, r)
    while r % tr:
        tr -= 16

    def body(x_ref, o_ref):
        acc = x_ref[0].astype(F32)
        for i in range(1, N_DEV):
            acc = acc + x_ref[i].astype(F32)
        o_ref[...] = acc

    return pl.pallas_call(
        body, name=name, grid=(r // tr,),
        in_specs=[pl.BlockSpec((N_DEV, tr, c), lambda i: (0, i, 0))],
        out_specs=pl.BlockSpec((tr, c), lambda i: (i, 0)),
        out_shape=jax.ShapeDtypeStruct((r, c), F32),
        compiler_params=_cp("arbitrary"),
    )(x)


def _sum_blocks(recv, src, me, name, tr=256):
    _, r, c = recv.shape
    tr = min(tr, r)
    while r % tr:
        tr -= 16

    def body(me_ref, recv_ref, own_ref, o_ref):
        acc = own_ref[...].astype(F32)
        for k in range(1, N_DEV):
            acc = acc + recv_ref[me_ref[0] ^ k].astype(F32)
        o_ref[...] = acc

    return pl.pallas_call(
        body, name=name,
        grid_spec=pltpu.PrefetchScalarGridSpec(
            num_scalar_prefetch=1, grid=(r // tr,),
            in_specs=[pl.BlockSpec((N_DEV, tr, c), lambda i, me_ref: (0, i, 0)),
                      pl.BlockSpec((None, tr, c), lambda i, me_ref: (me_ref[0], i, 0))],
            out_specs=pl.BlockSpec((tr, c), lambda i, me_ref: (i, 0))),
        out_shape=jax.ShapeDtypeStruct((r, c), F32),
        compiler_params=_cp("arbitrary"),
    )(me, recv, src)


def _sum_blocks_hosting(recvs, srcs, me, name, scatter, gather):
    n, ns, ng = len(recvs), len(scatter), len(gather)

    def body(me_ref, *refs):
        recv_hbm, own_hbm = refs[:n], refs[n:2 * n]
        moved_in = refs[2 * n:2 * n + ns + ng]
        outs = refs[2 * n + ns + ng:]
        moved_out, scratch = outs[n:n + ns + ng], outs[n + ns + ng:]
        recv_v, own_v, load_sems, sems = scratch[:n], scratch[n:2 * n], scratch[2 * n], scratch[2 * n + 1:]
        t = _AllToAll(moved_in[:ns], moved_out[:ns], sems[0], sems[1])
        g = _Gather(moved_in[ns:], moved_out[ns:], sems[2], sems[3])
        g.start()
        t.start()
        me = me_ref[0]
        loads = [(pltpu.make_async_copy(recv_hbm[i], recv_v[i], load_sems.at[2 * i]),
                  pltpu.make_async_copy(own_hbm[i].at[me], own_v[i], load_sems.at[2 * i + 1])) for i in range(n)]
        for load in loads:
            load[0].start()
            load[1].start()
        for i in range(n):
            loads[i][0].wait()
            loads[i][1].wait()
            acc = own_v[i][...].astype(F32)
            for k in range(1, N_DEV):
                acc = acc + recv_v[i][me ^ k].astype(F32)
            outs[i][...] = acc
        g.finish()
        t.finish()

    def once(shape, index_map):
        return pl.BlockSpec(shape, index_map, pipeline_mode=pl.Buffered(1))

    shapes = [v.shape[1:] for v in recvs]
    out = pl.pallas_call(
        body, name=name,
        grid_spec=pltpu.PrefetchScalarGridSpec(
            num_scalar_prefetch=1, grid=(1,),
            in_specs=_comm_specs(2 * n + ns + ng),
            out_specs=[once(s, lambda i, me_ref: (0, 0)) for s in shapes] + _comm_specs(ns + ng),
            scratch_shapes=[pltpu.VMEM(v.shape, v.dtype) for v in recvs] + [pltpu.VMEM(v.shape[1:], v.dtype) for v in recvs]
            + [pltpu.SemaphoreType.DMA((2 * n,))] + _comm_scratch(ns) + _comm_scratch(ng)),
        out_shape=[jax.ShapeDtypeStruct(s, F32) for s in shapes]
        + [jax.ShapeDtypeStruct(v.shape, v.dtype) for v in scatter] + _gathered_shapes(gather),
        compiler_params=_cp("arbitrary"),
    )(me, *recvs, *srcs, *scatter, *gather)
    return out[:n], out[n:n + ns], out[n + ns:]


def _sum8_many(xs, name):
    n = len(xs)

    def body(*refs):
        for x_ref, o_ref in zip(refs[:n], refs[n:]):
            acc = x_ref[0]
            for i in range(1, N_DEV):
                acc = acc + x_ref[i]
            o_ref[...] = acc

    vmem = pl.BlockSpec(memory_space=pltpu.VMEM)
    return pl.pallas_call(
        body, name=name, in_specs=[vmem] * n, out_specs=[vmem] * n,
        out_shape=[jax.ShapeDtypeStruct(v.shape[1:], v.dtype) for v in xs],
        compiler_params=pltpu.CompilerParams(vmem_limit_bytes=VMEM_LIMIT),
    )(*xs)


def _adam_update(w, g, m, v):
    nm = ADAM_B1 * m + (1.0 - ADAM_B1) * g
    nv = ADAM_B2 * v + (1.0 - ADAM_B2) * (g * g)
    m_hat = nm / (1.0 - ADAM_B1 ** ADAM_STEP)
    v_hat = nv / (1.0 - ADAM_B2 ** ADAM_STEP)
    return -ADAM_LR * (m_hat / (jnp.sqrt(v_hat) + ADAM_EPS) + ADAM_WD * w), nm, nv


ROW_LOSS, ROW_LN2_G, ROW_LN2_B, ROW_LN1_G, ROW_LN1_B = 0, 1, 2, 10, 11
ROWS_DMOD_X = (16, 17, 12, 9, 8, 3)
ROWS_DMOD_C = (24, 25)
SMALL = ("c_ctx", "b_ada", "attn_sink", "gmlp_ln_g", "gmlp_ln_b", "w_spatial", "b_spatial", "ln1_g", "ln1_b", "ln2_g", "ln2_b")


def _adamw_small(sums, dsc, w, m, v):
    n = len(SMALL)

    def body(*refs):
        st_ref, gm_ref, sk_ref, ws_ref, bs_ref, dsc_ref = refs[:6]
        w_refs = dict(zip(SMALL, refs[6:6 + n]))
        m_refs = dict(zip(SMALL, refs[6 + n:6 + 2 * n]))
        v_refs = dict(zip(SMALL, refs[6 + 2 * n:6 + 3 * n]))
        outs = refs[6 + 3 * n:]
        c = w_refs["c_ctx"][...]
        sg = _sig(c)
        dmod = [st_ref[r:r + 1, :] for r in ROWS_DMOD_X]
        dmod[0] = dmod[0] + st_ref[ROWS_DMOD_C[0]:ROWS_DMOD_C[0] + 1, :]
        dmod[1] = dmod[1] + st_ref[ROWS_DMOD_C[1]:ROWS_DMOD_C[1] + 1, :]
        grads = dict(
            c_ctx=dsc_ref[0:1, :] * (sg * (1.0 + c * (1.0 - sg))),
            b_ada=jnp.concatenate(dmod, axis=1),
            attn_sink=sk_ref[0:1, 0:N_KV * GROUP],
            gmlp_ln_g=gm_ref[0:1, :], gmlp_ln_b=gm_ref[1:2, :],
            w_spatial=ws_ref[...], b_spatial=bs_ref[...],
            ln1_g=st_ref[ROW_LN1_G:ROW_LN1_G + 1, :], ln1_b=st_ref[ROW_LN1_B:ROW_LN1_B + 1, :],
            ln2_g=st_ref[ROW_LN2_G:ROW_LN2_G + 1, :], ln2_b=st_ref[ROW_LN2_B:ROW_LN2_B + 1, :])
        for i, name in enumerate(SMALL):
            g = grads[name]
            d, nm, nv = _adam_update(w_refs[name][...], g, m_refs[name][...], v_refs[name][...])
            outs[i][...] = g
            outs[n + i][...] = d
            outs[2 * n + i][...] = nm
            outs[3 * n + i][...] = nv

    vmem = pl.BlockSpec(memory_space=pltpu.VMEM)
    args = list(sums) + [dsc] + [w[k] for k in SMALL] + [m[k] for k in SMALL] + [v[k] for k in SMALL]
    shapes = [jax.ShapeDtypeStruct(w[k].shape, F32) for k in SMALL]
    out = pl.pallas_call(
        body, name="adamw_small", in_specs=[vmem] * len(args), out_specs=[vmem] * (4 * n), out_shape=shapes * 4,
        compiler_params=pltpu.CompilerParams(vmem_limit_bytes=VMEM_LIMIT),
    )(*args)
    return [dict(zip(SMALL, out[i * n:(i + 1) * n])) for i in range(4)]


def _adamw_halves(w, mine, theirs, m, v, c_arr, name):
    r, c = w.shape
    tr = min(256, r // 2)
    while (r // 2) % tr:
        tr -= 8
    nt = (r // 2) // tr

    def body(c_ref, w_ref, mine_ref, theirs_ref, m_ref, v_ref, g_ref, d_ref, nm_ref, nv_ref):
        g = jnp.where(pl.program_id(0) == c_ref[0], mine_ref[...], theirs_ref[...])
        g_ref[...] = g
        d_ref[...], nm_ref[...], nv_ref[...] = _adam_update(w_ref[...], g, m_ref[...], v_ref[...])

    whole = pl.BlockSpec((tr, c), lambda hb, i, c_ref: (hb * nt + i, 0))
    mine_spec = pl.BlockSpec((tr, c), lambda hb, i, c_ref: (jnp.where(hb == c_ref[0], i, 0), 0))
    theirs_spec = pl.BlockSpec((tr, c), lambda hb, i, c_ref: (jnp.where(hb == c_ref[0], 0, i), 0))
    shp = jax.ShapeDtypeStruct((r, c), F32)
    return pl.pallas_call(
        body, name=name,
        grid_spec=pltpu.PrefetchScalarGridSpec(
            num_scalar_prefetch=1, grid=(2, nt), in_specs=[whole, mine_spec, theirs_spec, whole, whole],
            out_specs=[whole] * 4),
        out_shape=[shp] * 4,
        compiler_params=_cp("arbitrary", "arbitrary"),
    )(c_arr, w, mine, theirs, m, v)


def _my_pos():
    return lax.axis_index("x"), lax.axis_index("y"), lax.axis_index("c")


N_COPY = 7


class _Gather:
    def __init__(self, x_refs, out_refs, send_sems, recv_sems):
        self.x_refs, self.out_refs = x_refs, out_refs
        self.send_sems, self.recv_sems = send_sems, recv_sems
        x, y, c = _my_pos()
        self.c = c
        self.me, self.sibling = (x, y, c), (x, y, 1 - c)
        self.chips = [(1 - x, y), (x, 1 - y), (1 - x, 1 - y)]

    def _copy(self, a, k, block, to, from_input=False):
        px, py, pc = block
        rows = self.out_refs[a].at[4 * px + 2 * py + pc]
        return pltpu.make_async_remote_copy(
            src_ref=self.x_refs[a] if from_input else rows, dst_ref=rows,
            send_sem=self.send_sems.at[a * N_COPY + k], recv_sem=self.recv_sems.at[a * N_COPY + k],
            device_id=to, device_id_type=MESH)

    def start(self):
        n = len(self.x_refs)
        for a in range(n):
            self._copy(a, 0, self.me, self.sibling, from_input=True).start()
        for j, chip in enumerate(self.chips):
            for a in range(n):
                self._copy(a, 1 + j, self.me, (*chip, self.c), from_input=True).start()

    def forward(self):
        c = self.c
        for j, chip in enumerate(self.chips):
            for a in range(len(self.x_refs)):
                self._copy(a, 1 + j, (*chip, c), self.me).wait_recv()
                self._copy(a, 4 + j, (*chip, c), self.sibling).start()

    def finish(self):
        self.forward()
        self.drain()

    def drain(self):
        n = len(self.x_refs)
        c = self.c
        for a in range(n):
            self._copy(a, 0, self.sibling, self.me).wait_recv()
        for j, chip in enumerate(self.chips):
            for a in range(n):
                self._copy(a, 4 + j, (*chip, 1 - c), self.me).wait_recv()
        for a in range(n):
            self._copy(a, 0, self.me, self.sibling, from_input=True).wait_send()
            for j, chip in enumerate(self.chips):
                self._copy(a, 1 + j, self.me, (*chip, c), from_input=True).wait_send()
                self._copy(a, 4 + j, (*chip, c), self.sibling).wait_send()


def _comm_scratch(n):
    return [pltpu.SemaphoreType.DMA((n * N_COPY,)), pltpu.SemaphoreType.DMA((n * N_COPY,))]


def _comm_specs(n):
    return [pl.BlockSpec(memory_space=pl.ANY)] * n


def _gathered_shapes(xs):
    return [jax.ShapeDtypeStruct((N_DEV,) + v.shape, v.dtype) for v in xs]


def _with_own(gathered, xs, me):
    return [lax.dynamic_update_index_in_dim(g, v, me, 0) for g, v in zip(gathered, xs)]


class _AllToAll:
    def __init__(self, x_refs, out_refs, send_sems, recv_sems):
        self.x_refs, self.out_refs = x_refs, out_refs
        self.send_sems, self.recv_sems = send_sems, recv_sems
        self.pos = _my_pos()
        x, y, c = self.pos
        self.me = 4 * x + 2 * y + c

    def _peer(self, k):
        x, y, c = self.pos
        return (x ^ ((k >> 2) & 1), y ^ ((k >> 1) & 1), c ^ (k & 1))

    def _copy(self, a, k):
        p = self._peer(k)
        return pltpu.make_async_remote_copy(
            src_ref=self.x_refs[a].at[4 * p[0] + 2 * p[1] + p[2]], dst_ref=self.out_refs[a].at[self.me],
            send_sem=self.send_sems.at[a * N_COPY + k - 1], recv_sem=self.recv_sems.at[a * N_COPY + k - 1],
            device_id=p, device_id_type=MESH)

    def start(self):
        for k in range(1, N_DEV):
            for a in range(len(self.x_refs)):
                self._copy(a, k).start()

    def finish(self):
        for a in range(len(self.x_refs)):
            for k in range(1, N_DEV):
                self._copy(a, k).wait_recv()
            for k in range(1, N_DEV):
                self._copy(a, k).wait_send()


def _exchange(name, scatter=(), gather=(), sibling=()):
    ns, ng, nx = len(scatter), len(gather), len(sibling)
    n = ns + ng + nx

    def body(*refs):
        ins, outs, sems = refs[:n], refs[n:2 * n], list(refs[2 * n:])
        t = _AllToAll(ins[:ns], outs[:ns], sems.pop(0), sems.pop(0)) if ns else None
        g = _Gather(ins[ns:ns + ng], outs[ns:ns + ng], sems.pop(0), sems.pop(0)) if ng else None
        x, y, c = _my_pos()

        def push(a):
            return pltpu.make_async_remote_copy(
                src_ref=ins[ns + ng + a], dst_ref=outs[ns + ng + a], send_sem=sems[0].at[a], recv_sem=sems[1].at[a],
                device_id=(x, y, 1 - c), device_id_type=MESH)

        for comm in (g, t):
            if comm is not None:
                comm.start()
        for a in range(nx):
            push(a).start()
        for comm in (g, t):
            if comm is not None:
                comm.finish()
        for a in range(nx):
            push(a).wait_recv()
            push(a).wait_send()

    scratch = (_comm_scratch(ns) if ns else []) + (_comm_scratch(ng) if ng else [])
    scratch += [pltpu.SemaphoreType.DMA((nx,)), pltpu.SemaphoreType.DMA((nx,))] if nx else []
    out = pl.pallas_call(
        body, name=name,
        out_shape=[jax.ShapeDtypeStruct(v.shape, v.dtype) for v in scatter] + _gathered_shapes(gather)
        + [jax.ShapeDtypeStruct(v.shape, v.dtype) for v in sibling],
        in_specs=_comm_specs(n), out_specs=_comm_specs(n), scratch_shapes=scratch,
    )(*scatter, *gather, *sibling)
    return out[:ns], out[ns:ns + ng], out[ns + ng:]


def _row_tile(seq, want):
    return min(want, seq)


def _local_step(x, ctx, tgt, mod_x, mod_c, wb, sink, gmlp_g, gmlp_b, w_s, b_s, ln1_g, ln1_b, ln2_g, ln2_b,
                later=None, me=None):
    seq = x.shape[0]
    on_mesh = me is not None
    modx1 = jnp.concatenate([mod_x[0:2], jnp.zeros((6, D), F32)], axis=0)
    modc = jnp.concatenate([mod_c[0:2], jnp.zeros((6, D), F32)], axis=0)
    vec = jnp.concatenate([mod_x[2:3], ln1_g, ln1_b, mod_x[3:6], ln2_g, ln2_b], axis=0)
    gp = jnp.concatenate([gmlp_g, gmlp_b, jnp.zeros((6, G_W), F32)], axis=0)
    ws_stack = w_s.reshape(N_GRP * BLK, BLK).astype(BF16)
    ws_stack_t = jnp.transpose(w_s, (2, 0, 1)).reshape(BLK, N_GRP * BLK).astype(BF16)
    bias_full = jnp.repeat(b_s.T, GRP_D, axis=1)
    cos, sin = _rope_tables(seq)
    w_in = wb["w_in"]
    tm_big = _row_tile(seq, 512)
    tm_ffn = _row_tile(seq, 256)

    hc, kvc, vac = _ctx_fwd(ctx, modc, w_in)
    behind_proj = ("w_a", "w_b", "w_o") if on_mesh else ()
    behind_attn = ("w_fi",) if on_mesh else ()
    behind_mix = ("w_fo",) if on_mesh else ()
    wb = dict(wb)

    def whole(names, gathered):
        for n, g in zip(names, _with_own(list(gathered), [later[n] for n in names], me)):
            wb[n] = g.reshape(-1, g.shape[2]) if n in ROW_SHARDED else g.reshape(N_SHARD, 2 * g.shape[1], g.shape[2])

    (h, q, kv, va, uv, gab), got = _proj_fwd(x, modx1, w_in, cos, sin, _row_tile(seq, 1024), gather=[later[n] for n in behind_proj])
    whole(behind_proj, got)
    if on_mesh:
        for n in ("w_a", "w_b"):
            wb[n] = wb[n].transpose(1, 0, 2).reshape(wb[n].shape[1], D)
    (ya, lse), got = _attn_fwd(q, kv, va, kvc, vac, sink, gather=[later[n] for n in behind_attn])
    whole(behind_attn, got)
    (a, b, mix, merged, yb), got = _mix_fwd(uv, gab, ya, gp, ws_stack, bias_full, wb["w_a"], wb["w_b"], wb["w_o"], tm_big,
                                            gather=[later[n] for n in behind_mix])
    whole(behind_mix, got)
    act, h2, dff, df, dr1, st_ffn = _ffn(x, mix, tgt, vec, wb["w_fi"], wb["w_fo"], tm_ffn)
    blocks, recv = {}, {}
    blocks["w_fo"] = _eighths(_tn_matmul(act, df, 512, "tn_w_ffn_out", BF16, tk=2048))
    if on_mesh:
        g_w_fi, (recv["w_fo"],) = _tn_matmul(h2, dff, FH_SHARD, "tn_w_ffn_in", BF16, shard_major=True, tk=2048,
                                            scatter=[blocks["w_fo"]])
    else:
        g_w_fi = _tn_matmul(h2, dff, FH_SHARD, "tn_w_ffn_in", BF16, shard_major=True, tk=2048)
    blocks["w_fi"] = _eighths(g_w_fi)
    (dya, dpb, dws, dbs_full, st4, g_w_o, g_w_a, g_w_b), got = _mix_bwd(
        dr1, a, b, gab, uv, merged, ya, yb, vec, gp, ws_stack, ws_stack_t, bias_full, wb["w_a"], wb["w_b"], wb["w_o"],
        tm_big, scatter=[blocks["w_fi"]] if on_mesh else ())
    recv.update(zip(("w_fi",), got))
    shard_major = [g.reshape(g.shape[0], N_SHARD, D // N_SHARD).transpose(1, 0, 2) for g in (g_w_a, g_w_b)]
    blocks.update(w_o=_eighths(g_w_o), w_a=_eighths(shard_major[0]), w_b=_eighths(shard_major[1]))
    mixer = ("w_o", "w_a", "w_b") if on_mesh else ()
    (dq, dkv, dkvc, dsink), got = _attn_bwd(q, kv, kvc, sink, dya, ya, lse, scatter=[blocks[n] for n in mixer])
    recv.update(zip(mixer, got))
    g_wkv_ctx, st0 = _ctx_bwd(dkvc, ctx, hc, w_in)
    (dqkv, grad_x, st1), _ = _proj_bwd(dq, dkv, dpb, x, dr1, modx1, w_in, cos, sin, tm_big)
    dbs = jnp.sum(dbs_full.reshape(BLK, N_GRP, GRP_D), axis=2).T
    early = [jnp.concatenate([st_ffn, st0], axis=0), st4, dsink, dws, dbs]
    g_in = _tn_matmul(dqkv, h, D, "tn_w_in_qkv", BF16, init=g_wkv_ctx, tk=1024, out_rows=IN_W)
    into = (g_in, dqkv.shape[1])
    if on_mesh:
        g_in, early_gathered = _tn_matmul(dpb, h, D, "tn_w_in_rest", BF16, tk=1024, into=into, gather=early)
    else:
        g_in, early_gathered = _tn_matmul(dpb, h, D, "tn_w_in_rest", BF16, tk=1024, into=into), None
    blocks["w_in"] = _eighths(g_in)
    return grad_x, dict(early=early, early_gathered=early_gathered, late=st1), blocks, recv


BIG = ("w_in", "w_a", "w_b", "w_o", "w_fi", "w_fo")
ROW_SHARDED = ("w_o", "w_fo")


def _half_of_shard(shard, c):
    r = shard.shape[0]
    return lax.dynamic_slice_in_dim(shard, c * (r // 2), r // 2, axis=0)


def _eighths(v):
    rows = v.shape[-2] * (v.shape[0] if v.ndim == 3 else 1)
    return v.reshape(N_DEV, rows // N_DEV, v.shape[-1])


def kernel(x, c, ctx, c_ctx, w_ada, b_ada, w_in, attn_sink, gmlp_ln_g, gmlp_ln_b, w_spatial, b_spatial, w_branch_a, w_branch_b, w_out, ln1_g, ln1_b, w_ffn_in, w_ffn_out, ln2_g, ln2_b, loss_target, m_c_ctx, m_w_ada, m_b_ada, m_w_in, m_attn_sink, m_gmlp_ln_g, m_gmlp_ln_b, m_w_spatial, m_b_spatial, m_w_branch_a, m_w_branch_b, m_w_out, m_ln1_g, m_ln1_b, m_w_ffn_in, m_w_ffn_out, m_ln2_g, m_ln2_b, v_c_ctx, v_w_ada, v_b_ada, v_w_in, v_attn_sink, v_gmlp_ln_g, v_gmlp_ln_b, v_w_spatial, v_b_spatial, v_w_branch_a, v_w_branch_b, v_w_out, v_ln1_g, v_ln1_b, v_w_ffn_in, v_w_ffn_out, v_ln2_g, v_ln2_b):
    mx, my, mc = _my_pos()
    me = 4 * mx + 2 * my + mc
    chip = 2 * mx + my
    shards = dict(w_in=w_in[0].T, w_a=w_branch_a[0], w_b=w_branch_b[0], w_o=w_out[0], w_fi=w_ffn_in[0], w_fo=w_ffn_out[0])

    halves = {n: _half_of_shard(shards[n], mc).astype(BF16) for n in BIG}
    c_rows = jnp.concatenate([c, jnp.zeros((7, D), F32)], axis=0)
    _, g_in, _, mod_g, sc_all = _gather_and_modulate(c_rows, c_ctx[None, :], halves["w_in"], w_ada[0])
    wb = dict(w_in=_with_own([g_in], [halves["w_in"]], me)[0].reshape(IN_W, D))
    mod_all = jnp.concatenate([mod_g[2 * s] for s in range(4)], axis=1) + b_ada
    mod_x = lax.dynamic_slice_in_dim(mod_all, me, 1, axis=0).reshape(6, D)
    mod_c = mod_all[8].reshape(6, D)[0:2]

    grad_x, small, blocks, recv = _local_step(
        x[0], ctx[0], loss_target[0], mod_x, mod_c, wb, attn_sink, gmlp_ln_g, gmlp_ln_b, w_spatial[0], b_spatial[0],
        ln1_g, ln1_b, ln2_g, ln2_b, later=halves, me=me)

    me_arr = jnp.reshape(me, (1,)).astype(jnp.int32)
    sums_early, (recv["w_in"],), late = _sum_blocks_hosting(
        [recv[n] for n in BIG[1:]], [blocks[n] for n in BIG[1:]], me_arr, "sum_grads_scatter_w_in_gather_small",
        scatter=[blocks["w_in"]], gather=[small["late"]])
    summed = dict(zip(BIG[1:], sums_early))
    summed["w_in"] = _sum_blocks(recv["w_in"], blocks["w_in"], me_arr, "sum_grads_w_in")
    late = _with_own(late, [small["late"]], me)[0]
    gathered = _with_own(small["early_gathered"], small["early"], me)
    gathered[0] = jnp.concatenate([gathered[0][:, :16], late, gathered[0][:, 16:]], axis=1)

    sums = _sum8_many(gathered, "sum_small")
    stats = sums[0]
    loss = 0.5 * jnp.sum(stats[ROW_LOSS]) / D
    dmod_x_all = jnp.concatenate([gathered[0][:, r_, :] for r_ in ROWS_DMOD_X], axis=1)
    dmod_c_full = jnp.concatenate([stats[r_] for r_ in ROWS_DMOD_C] + [jnp.zeros((4 * D,), F32)])
    dm_rows = jnp.concatenate([dmod_x_all, dmod_c_full[None, :], jnp.zeros((7, 6 * D), F32)], axis=0)
    cs = w_ada.shape[2]
    dm_shard = lax.dynamic_slice_in_dim(dm_rows, chip * cs, cs, axis=1)
    dmc_shard = jnp.concatenate([dm_shard[8:9], jnp.zeros((7, cs), F32)], axis=0)
    ada = _ada_bwd(sc_all.T, dm_shard, dmc_shard, w_ada[0], m_w_ada[0], v_w_ada[0])
    part = ada[4]
    part = part * (mc == 0).astype(F32)
    _, part_all, theirs = _exchange("exchange_grads_gather_c_ctx", gather=[part], sibling=[summed[n] for n in BIG])
    theirs = dict(zip(BIG, theirs))
    dsc = _sum8(_with_own(part_all, [part], me)[0], "sum_c_ctx")

    weights = dict(c_ctx=c_ctx, w_ada=w_ada, b_ada=b_ada, w_in=w_in, attn_sink=attn_sink, gmlp_ln_g=gmlp_ln_g,
                   gmlp_ln_b=gmlp_ln_b, w_spatial=w_spatial, b_spatial=b_spatial, w_branch_a=w_branch_a,
                   w_branch_b=w_branch_b, w_out=w_out, ln1_g=ln1_g, ln1_b=ln1_b, w_ffn_in=w_ffn_in, w_ffn_out=w_ffn_out,
                   ln2_g=ln2_g, ln2_b=ln2_b)
    ms = dict(c_ctx=m_c_ctx, w_ada=m_w_ada, b_ada=m_b_ada, w_in=m_w_in, attn_sink=m_attn_sink, gmlp_ln_g=m_gmlp_ln_g,
              gmlp_ln_b=m_gmlp_ln_b, w_spatial=m_w_spatial, b_spatial=m_b_spatial, w_branch_a=m_w_branch_a,
              w_branch_b=m_w_branch_b, w_out=m_w_out, ln1_g=m_ln1_g, ln1_b=m_ln1_b, w_ffn_in=m_w_ffn_in,
              w_ffn_out=m_w_ffn_out, ln2_g=m_ln2_g, ln2_b=m_ln2_b)
    vs = dict(c_ctx=v_c_ctx, w_ada=v_w_ada, b_ada=v_b_ada, w_in=v_w_in, attn_sink=v_attn_sink, gmlp_ln_g=v_gmlp_ln_g,
              gmlp_ln_b=v_gmlp_ln_b, w_spatial=v_w_spatial, b_spatial=v_b_spatial, w_branch_a=v_w_branch_a,
              w_branch_b=v_w_branch_b, w_out=v_w_out, ln1_g=v_ln1_g, ln1_b=v_ln1_b, w_ffn_in=v_w_ffn_in,
              w_ffn_out=v_w_ffn_out, ln2_g=v_ln2_g, ln2_b=v_ln2_b)
    order = list(weights)
    grads, delta, new_m, new_v = [dict(w_ada=t[None]) for t in ada[:4]]
    c_arr = jnp.reshape(mc, (1,)).astype(jnp.int32)
    names = dict(w_in="w_in", w_a="w_branch_a", w_b="w_branch_b", w_o="w_out", w_fi="w_ffn_in", w_fo="w_ffn_out")
    for k, n in names.items():
        flip = (lambda t: t.T) if k == "w_in" else (lambda t: t)
        outs = _adamw_halves(flip(weights[n][0]), summed[k], theirs[k], flip(ms[n][0]), flip(vs[n][0]), c_arr, "adamw_" + n)
        grads[n], delta[n], new_m[n], new_v[n] = [flip(t)[None] for t in outs]

    def view(a):
        return a.reshape(-1, a.shape[-1]) if a.ndim != 1 else a.reshape(1, -1)

    small = _adamw_small(sums, dsc, *[{n: view(d[n]) for n in SMALL} for d in (weights, ms, vs)])
    for out, src in zip((grads, delta, new_m, new_v), small):
        for n in SMALL:
            out[n] = src[n].reshape(weights[n].shape)

    return (loss, grad_x[None], *[grads[n] for n in order], *[delta[n] for n in order],
            *[new_m[n] for n in order], *[new_v[n] for n in order])
```

```python
import math

import jax
import jax.numpy as jnp
import numpy as np
from jax import lax
from jax.experimental import pallas as pl
from jax.experimental.pallas import tpu as pltpu

F32 = jnp.float32
BF16 = jnp.bfloat16

D = 1024
HEAD = 64
N_KV = 2
GROUP = 4
Q_W = 512
KV_W = 128
G_W = 512
BLK = 128
N_GRP = 8
GRP_D = 64
FH = 2816
IN_W = 3840
GRID_W = 64
ROPE_BASE = 10000.0
LN_EPS = 1e-5
NEG = -1e30
ALPHA = (2 * 1) ** 0.25
SCALE = HEAD ** -0.5
GELU_K = math.sqrt(2.0 / math.pi)
GELU_A = 0.044715
ADAM_LR = 0.001
ADAM_B1 = 0.9
ADAM_B2 = 0.999
ADAM_EPS = 1e-08
ADAM_WD = 0.01
ADAM_STEP = 10
N_DEV = 8
N_SHARD = 4
FH_SHARD = FH // 2
LANES = 128
VMEM_LIMIT = 56 * 1024 * 1024
MESH = pl.DeviceIdType.MESH


def _cp(*sem):
    return pltpu.CompilerParams(dimension_semantics=sem, vmem_limit_bytes=VMEM_LIMIT)


def _resident(shape):
    return pl.BlockSpec(shape, lambda *_: (0,) * len(shape), pipeline_mode=pl.Buffered(1))


def _rows(tm, width):
    return pl.BlockSpec((tm, width), lambda i: (i, 0))


def _acc(shape):
    return pl.BlockSpec(shape, lambda *_: (0,) * len(shape))


def _dot(a, b):
    return jnp.dot(a, b, preferred_element_type=F32)


def _dot_nt(a, b):
    return lax.dot_general(a, b, (((1,), (1,)), ((), ())), preferred_element_type=F32)


def _dot_tn(a, b):
    return lax.dot_general(a, b, (((0,), (0,)), ((), ())), preferred_element_type=F32)


def _ln(x):
    mu = jnp.mean(x, axis=-1, keepdims=True)
    xc = x - mu
    var = jnp.mean(xc * xc, axis=-1, keepdims=True)
    rstd = lax.rsqrt(var + LN_EPS)
    return xc * rstd, rstd


def _ln_bwd(dxhat, xhat, rstd):
    return (dxhat - jnp.mean(dxhat, axis=-1, keepdims=True)
            - xhat * jnp.mean(dxhat * xhat, axis=-1, keepdims=True)) * rstd


def _sig(x):
    return 0.5 + 0.5 * jnp.tanh(0.5 * x)


def _gelu(x):
    t = jnp.tanh(x * (GELU_K + (GELU_K * GELU_A) * (x * x)))
    hx = 0.5 * x
    return hx + hx * t, t


def _gelu_grad(x, t):
    return 0.5 + 0.5 * t + (0.5 * x) * (1.0 - t * t) * (GELU_K + (3.0 * GELU_K * GELU_A) * (x * x))


def _colsum(v):
    return jnp.sum(v, axis=0, keepdims=True)


def _partner(x):
    w = x.shape[1]
    lane = lax.broadcasted_iota(jnp.int32, x.shape, 1)
    return jnp.where((lane & 31) < 16, pltpu.roll(x, w - 16, 1), pltpu.roll(x, 16, 1))


def _rope(x, cos, sin):
    return x * cos + _partner(x) * sin


def _unrope(g, cos, sin):
    return g * cos + _partner(g * sin)


def _rope_tables(seq):
    inv = np.float32(ROPE_BASE) ** (-np.arange(HEAD // 4, dtype=np.float32) / np.float32(HEAD // 4))
    pos = np.arange(seq)
    ar = (pos // GRID_W).astype(np.float32)[:, None] * inv
    ac = (pos % GRID_W).astype(np.float32)[:, None] * inv
    cos = np.concatenate([np.cos(ar), np.cos(ar), np.cos(ac), np.cos(ac)], axis=-1)
    sin = np.concatenate([-np.sin(ar), np.sin(ar), -np.sin(ac), np.sin(ac)], axis=-1)
    reps = (1, LANES // HEAD)
    return jnp.asarray(np.tile(cos, reps), F32), jnp.asarray(np.tile(sin, reps), F32)


def _kv_rows_of_w_in():
    return pl.BlockSpec((2 * KV_W, D), lambda *_: (Q_W // (2 * KV_W), 0))


def _ctx_fwd(ctx, modc, w_in):
    n_ctx = ctx.shape[0]

    def body(ctx_ref, mod_ref, w_ref, hc_ref, kvc_ref, vac_ref):
        xhat, _ = _ln(ctx_ref[...])
        hc = (xhat * (1.0 + mod_ref[1:2, :]) + mod_ref[0:1, :]).astype(BF16)
        hc_ref[...] = hc
        kvc = _dot_nt(hc, w_ref[...]).astype(BF16)
        kvc_ref[...] = kvc
        vac_ref[...] = _with_ones(kvc[:, KV_W:])

    return pl.pallas_call(
        body, name="ctx_fwd", grid=(1,),
        in_specs=[_acc((n_ctx, D)), _acc((8, D)), _kv_rows_of_w_in()],
        out_specs=[_acc((n_ctx, D)), _acc((n_ctx, 2 * KV_W)), _acc((n_ctx, 2 * LANES))],
        out_shape=[jax.ShapeDtypeStruct((n_ctx, D), BF16), jax.ShapeDtypeStruct((n_ctx, 2 * KV_W), BF16),
                   jax.ShapeDtypeStruct((n_ctx, 2 * LANES), BF16)],
        compiler_params=_cp("arbitrary"),
    )(ctx, modc, w_in)


def _host_start(step, comm):
    if comm is not None:
        @pl.when(step == 0)
        def _():
            comm.start()


def _host_finish(step, last, comm, forward_at=None):
    if comm is None:
        return
    if forward_at is None or forward_at >= last:
        @pl.when(step == last)
        def _():
            comm.finish()
    else:
        @pl.when(step == forward_at)
        def _():
            comm.forward()

        @pl.when(step == last)
        def _():
            comm.drain()


def _proj_fwd(x, modx, w_in, cos, sin, tm, gather=()):
    seq = x.shape[0]
    ng = len(gather)

    def body(x_ref, mod_ref, w_ref, cos_ref, sin_ref, *rest):
        h_ref, q_ref, kv_ref, va_ref, uv_ref, gab_ref = rest[ng:ng + 6]
        comm = _Gather(rest[:ng], rest[ng + 6:2 * ng + 6], *rest[2 * ng + 6:]) if ng else None
        _host_start(pl.program_id(0), comm)
        xhat, _ = _ln(x_ref[...])
        h = (xhat * (1.0 + mod_ref[1:2, :]) + mod_ref[0:1, :]).astype(BF16)
        h_ref[...] = h
        cos1, sin1 = cos_ref[...], sin_ref[...]
        cos2 = jnp.concatenate([cos1, cos1], axis=1)
        sin2 = jnp.concatenate([sin1, sin1], axis=1)
        for j in range(Q_W // 256):
            t = _dot_nt(h, w_ref[256 * j:256 * (j + 1), :])
            q_ref[:, 256 * j:256 * (j + 1)] = (_rope(t, cos2, sin2) * SCALE).astype(BF16)
        t = _dot_nt(h, w_ref[Q_W:Q_W + 2 * KV_W, :])
        kv_ref[:, :KV_W] = _rope(t[:, :KV_W], cos1, sin1).astype(BF16)
        v = t[:, KV_W:].astype(BF16)
        kv_ref[:, KV_W:] = v
        va_ref[...] = _with_ones(v)
        o = Q_W + 2 * KV_W
        for j in range(2):
            uv_ref[:, G_W * j:G_W * (j + 1)] = _dot_nt(h, w_ref[o + G_W * j:o + G_W * (j + 1), :]).astype(BF16)
        o += 2 * G_W
        for j in range(4):
            gab_ref[:, 512 * j:512 * (j + 1)] = _dot_nt(h, w_ref[o + 512 * j:o + 512 * (j + 1), :]).astype(BF16)
        _host_finish(pl.program_id(0), seq // tm - 1, comm, forward_at=(seq // tm) // 2)

    out = pl.pallas_call(
        body, name="proj_fwd", grid=(seq // tm,),
        in_specs=[_rows(tm, D), _acc((8, D)), _resident((IN_W, D)), _rows(tm, LANES), _rows(tm, LANES)] + _comm_specs(ng),
        out_specs=[_rows(tm, D), _rows(tm, Q_W), _rows(tm, 2 * KV_W), _rows(tm, 2 * LANES), _rows(tm, 2 * G_W),
                   _rows(tm, 2 * D)] + _comm_specs(ng),
        out_shape=[jax.ShapeDtypeStruct((seq, D), BF16), jax.ShapeDtypeStruct((seq, Q_W), BF16),
                   jax.ShapeDtypeStruct((seq, 2 * KV_W), BF16), jax.ShapeDtypeStruct((seq, 2 * LANES), BF16),
                   jax.ShapeDtypeStruct((seq, 2 * G_W), BF16), jax.ShapeDtypeStruct((seq, 2 * D), BF16)] + _gathered_shapes(gather),
        scratch_shapes=_comm_scratch(ng) if ng else [],
        compiler_params=_cp("arbitrary"),
    )(x, modx, w_in, cos, sin, *gather)
    return out[:6], out[6:]


def _stack_heads(x, hk):
    return jnp.concatenate([x[:, (hk * GROUP + g) * HEAD:(hk * GROUP + g + 1) * HEAD] for g in range(GROUP)], axis=0)


def _band_masks(n, nb):
    rows = GROUP * BLK
    qi = lax.broadcasted_iota(jnp.int32, (rows, BLK), 0) & (BLK - 1)
    kj = lax.broadcasted_iota(jnp.int32, (rows, BLK), 1)
    return (kj >= qi) & (n > 0), (kj <= qi) & (n < nb - 1)


def _attn_scores(q, k_refs, hk, masks):
    q4 = _stack_heads(q, hk)
    ks = [r[:, hk * HEAD:(hk + 1) * HEAD] for r in k_refs]
    s = [_dot_nt(q4, k) for k in ks]
    s[1] = jnp.where(masks[0], s[1], NEG)
    s[3] = jnp.where(masks[1], s[3], NEG)
    return q4, ks, s


def _sink_rows(sink_ref, hk):
    rows = GROUP * BLK
    rg = lax.broadcasted_iota(jnp.int32, (rows, 1), 0) >> 7
    sink_v = jnp.full((rows, 1), sink_ref[0, hk * GROUP], F32)
    for g in range(1, GROUP):
        sink_v = jnp.where(rg == g, sink_ref[0, hk * GROUP + g], sink_v)
    return sink_v


def _with_ones(v):
    ones = jnp.ones((v.shape[0], HEAD), v.dtype)
    return jnp.concatenate([v[:, :HEAD], ones, v[:, HEAD:], ones], axis=1)


def _kv_specs(nb, qb):
    def spec(d):
        return pl.BlockSpec((BLK, 2 * KV_W), lambda n: (jnp.clip(qb * n + d, 0, nb - 1), 0))
    return [spec(d) for d in range(-1, qb + 1)]


def _attn_fwd(q, kv, va, kvc, vac, sink, gather=()):
    seq = q.shape[0]
    nb = seq // BLK
    n_ctx = kvc.shape[0]
    ng = len(gather)
    Q_BLOCKS = 1
    nkv = Q_BLOCKS + 2
    steps = nb // Q_BLOCKS

    def body(q_ref, *rest):
        kv_refs, va_refs = rest[:nkv], rest[nkv:2 * nkv]
        kvc_ref, vac_ref, sink_ref = rest[2 * nkv:2 * nkv + 3]
        rest = rest[2 * nkv + 3:]
        o_ref, lse_ref = rest[ng:ng + 2]
        comm = _Gather(rest[:ng], rest[ng + 2:2 * ng + 2], *rest[2 * ng + 2:]) if ng else None
        n = pl.program_id(0)
        _host_start(n, comm)
        lane = lax.broadcasted_iota(jnp.int32, (BLK, LANES), 1)
        for sub in range(Q_BLOCKS):
            rs = slice(sub * BLK, (sub + 1) * BLK)
            q = q_ref[rs, :]
            outs = []
            lse_all = jnp.zeros((BLK, LANES), F32)
            masks = _band_masks(Q_BLOCKS * n + sub, nb)
            for hk in range(N_KV):
                _, _, s = _attn_scores(q, (kvc_ref,) + kv_refs[sub:sub + 3], hk, masks)
                sink_v = _sink_rows(sink_ref, hk)
                tile_max = s[1]
                for t in [s[0][:, i * LANES:(i + 1) * LANES] for i in range(n_ctx // LANES)] + s[2:]:
                    tile_max = jnp.maximum(tile_max, t)
                m = jnp.maximum(sink_v, jnp.max(tile_max, axis=-1, keepdims=True))
                o = jnp.zeros((GROUP * BLK, LANES), F32)
                for t, va_ref in zip(s, (vac_ref,) + va_refs[sub:sub + 3]):
                    o = o + _dot(jnp.exp((t - m).astype(BF16)), va_ref[:, hk * LANES:(hk + 1) * LANES])
                denom = o[:, HEAD:HEAD + 1] + jnp.exp(sink_v - m)
                o4 = o[:, :HEAD] * (1.0 / denom)
                lse4 = m + jnp.log(denom)
                for g in range(GROUP):
                    outs.append(o4[g * BLK:(g + 1) * BLK, :])
                    lse_all = jnp.where(lane == hk * GROUP + g, lse4[g * BLK:(g + 1) * BLK, :], lse_all)
            o_ref[rs, :] = jnp.concatenate(outs, axis=1).astype(BF16)
            lse_ref[rs, :] = lse_all
        _host_finish(n, steps - 1, comm, forward_at=(3 * steps) // 4)

    tq = Q_BLOCKS * BLK
    out = pl.pallas_call(
        body, name="attn_fwd", grid=(steps,),
        in_specs=[_rows(tq, Q_W)] + _kv_specs(nb, Q_BLOCKS) + _kv_specs(nb, Q_BLOCKS)
        + [_acc((n_ctx, 2 * KV_W)), _acc((n_ctx, 2 * LANES)), pl.BlockSpec(memory_space=pltpu.SMEM)] + _comm_specs(ng),
        out_specs=[_rows(tq, Q_W), _rows(tq, LANES)] + _comm_specs(ng),
        out_shape=[jax.ShapeDtypeStruct((seq, Q_W), BF16), jax.ShapeDtypeStruct((seq, LANES), F32)] + _gathered_shapes(gather),
        scratch_shapes=_comm_scratch(ng) if ng else [],
        compiler_params=_cp("arbitrary"),
    )(q, *([kv] * nkv), *([va] * nkv), kvc, vac, sink, *gather)
    return out[:2], out[2:]


def _gmlp_chunk(u, vb, gp_ref, ws_ref, bias_ref):
    gu, tu = _gelu(u)
    gv, tv = _gelu(vb)
    vhat, rstd = _ln(gv)
    vn = (vhat * gp_ref[0:1, :] + gp_ref[1:2, :]).astype(BF16)
    s = bias_ref[...] + jnp.concatenate(
        [_dot(ws_ref[g * BLK:(g + 1) * BLK, :], vn[:, g * GRP_D:(g + 1) * GRP_D]) for g in range(N_GRP)], axis=1)
    return gu, tu, tv, vhat, rstd, vn, s


def _mix_fwd(uv, gab, ya, gp, ws_stack, bias_full, w_a, w_b, w_o, tm, gather=()):
    seq = uv.shape[0]
    ng = len(gather)
    steps = seq // tm

    def body(uv_ref, gab_ref, ya_ref, gp_ref, ws_ref, bias_ref, wa_ref, wb_ref, wo_ref, *rest):
        a_ref, b_ref, mix_ref, merged_ref, yb_ref = rest[ng:ng + 5]
        comm = _Gather(rest[:ng], rest[ng + 5:2 * ng + 5], *rest[2 * ng + 5:]) if ng else None
        _host_start(pl.program_id(0), comm)
        for c in range(tm // BLK):
            rs = slice(c * BLK, (c + 1) * BLK)
            gu, _, _, _, _, _, s = _gmlp_chunk(uv_ref[rs, :G_W].astype(F32), uv_ref[rs, G_W:].astype(F32), gp_ref, ws_ref, bias_ref)
            yb_ref[rs, :] = (gu * s).astype(BF16)
        a = _dot(ya_ref[...], wa_ref[...])
        b = _dot(yb_ref[...], wb_ref[...])
        a_ref[...] = a.astype(BF16)
        b_ref[...] = b.astype(BF16)
        merged = (_sig(gab_ref[:, :D].astype(F32)) * a + _sig(gab_ref[:, D:].astype(F32)) * b).astype(BF16)
        merged_ref[...] = merged
        mix_ref[...] = _dot(merged, wo_ref[...])
        _host_finish(pl.program_id(0), steps - 1, comm, forward_at=(3 * steps) // 4)

    out = pl.pallas_call(
        body, name="mix_fwd", grid=(steps,),
        in_specs=[_rows(tm, 2 * G_W), _rows(tm, 2 * D), _rows(tm, Q_W), _acc((8, G_W)),
                  _resident((N_GRP * BLK, BLK)), _acc((BLK, G_W)),
                  _resident((Q_W, D)), _resident((G_W, D)), _resident((D, D))] + _comm_specs(ng),
        out_specs=[_rows(tm, D), _rows(tm, D), _rows(tm, D), _rows(tm, D), _rows(tm, G_W)] + _comm_specs(ng),
        out_shape=[jax.ShapeDtypeStruct((seq, D), BF16), jax.ShapeDtypeStruct((seq, D), BF16),
                   jax.ShapeDtypeStruct((seq, D), F32), jax.ShapeDtypeStruct((seq, D), BF16),
                   jax.ShapeDtypeStruct((seq, G_W), BF16)] + _gathered_shapes(gather),
        scratch_shapes=_comm_scratch(ng) if ng else [],
        compiler_params=_cp("arbitrary"),
    )(uv, gab, ya, gp, ws_stack, bias_full, w_a, w_b, w_o, *gather)
    return out[:5], out[5:]


def _mid_recompute(x_ref, mix_ref, vec_ref):
    r1 = ALPHA * x_ref[...] + vec_ref[0:1, :] * mix_ref[...]
    xh1, rstd1 = _ln(r1)
    xmid = xh1 * vec_ref[1:2, :] + vec_ref[2:3, :]
    xh2, rstd2 = _ln(xmid)
    return xh1, rstd1, xmid, xh2, rstd2


def _ffn(x, mix, tgt, vec, w_fi, w_fo, tm):
    seq = x.shape[0]

    def body(x_ref, mix_ref, tgt_ref, vec_ref, wi_ref, wo_ref, act_ref, h2_ref, dff_ref, df_ref, dr1_ref, st_ref, gu_ref):
        @pl.when(pl.program_id(0) == 0)
        def _():
            st_ref[...] = jnp.zeros_like(st_ref)

        xh1, rstd1, xmid, xh2, rstd2 = _mid_recompute(x_ref, mix_ref, vec_ref)
        h2 = (xh2 * (1.0 + vec_ref[4:5, :]) + vec_ref[3:4, :]).astype(BF16)
        h2_ref[...] = h2
        halves = [(slice(hh * FH_SHARD, (hh + 1) * FH_SHARD), slice(FH + hh * FH_SHARD, FH + (hh + 1) * FH_SHARD))
                  for hh in range(2)]
        for hh, (cs, cu) in enumerate(halves):
            g = _dot(h2, wi_ref[hh])
            u = _dot(h2, wi_ref[2 + hh])
            gu_ref[:, cs] = g
            gu_ref[:, cu] = u
            act_ref[:, cs] = (g * _sig(g) * u).astype(BF16)
        f = _dot(act_ref[...], wo_ref[...])
        r2 = ALPHA * xmid + vec_ref[5:6, :] * f
        yh, rstd = _ln(r2)
        y = yh * vec_ref[6:7, :] + vec_ref[7:8, :]
        err = y - tgt_ref[...]
        dy = err / D
        dr2 = _ln_bwd(dy * vec_ref[6:7, :], yh, rstd)
        st_ref[0:1, :] += _colsum(err * err)
        st_ref[1:2, :] += _colsum(dy * yh)
        st_ref[2:3, :] += _colsum(dy)
        st_ref[3:4, :] += _colsum(dr2 * f)

        df = (dr2 * vec_ref[5:6, :]).astype(BF16)
        df_ref[...] = df
        da_all = _dot_nt(df, wo_ref[...])
        for cs, cu in halves:
            da = da_all[:, cs]
            g = gu_ref[:, cs]
            u = gu_ref[:, cu]
            sg = _sig(g)
            dff_ref[:, cs] = (da * u * sg * (1.0 + g * (1.0 - sg))).astype(BF16)
            dff_ref[:, cu] = (da * g * sg).astype(BF16)
        dh2 = _dot_nt(dff_ref[:, :FH_SHARD], wi_ref[0])
        for s in range(1, N_SHARD):
            dh2 = dh2 + _dot_nt(dff_ref[:, s * FH_SHARD:(s + 1) * FH_SHARD], wi_ref[s])
        dxmid = _ln_bwd(dh2 * (1.0 + vec_ref[4:5, :]), xh2, rstd2) + ALPHA * dr2
        dr1 = _ln_bwd(dxmid * vec_ref[1:2, :], xh1, rstd1)
        dr1_ref[...] = dr1
        st_ref[8:9, :] += _colsum(dh2 * xh2)
        st_ref[9:10, :] += _colsum(dh2)
        st_ref[10:11, :] += _colsum(dxmid * xh1)
        st_ref[11:12, :] += _colsum(dxmid)
        st_ref[12:13, :] += _colsum(dr1 * mix_ref[...])

    return pl.pallas_call(
        body, name="ffn", grid=(seq // tm,),
        in_specs=[_rows(tm, D), _rows(tm, D), _rows(tm, D), _acc((8, D)), _resident((N_SHARD, D, FH_SHARD)), _resident((FH, D))],
        out_specs=[_rows(tm, FH), _rows(tm, D), _rows(tm, 2 * FH), _rows(tm, D), _rows(tm, D), _acc((16, D))],
        out_shape=[jax.ShapeDtypeStruct((seq, FH), BF16), jax.ShapeDtypeStruct((seq, D), BF16),
                   jax.ShapeDtypeStruct((seq, 2 * FH), BF16), jax.ShapeDtypeStruct((seq, D), BF16),
                   jax.ShapeDtypeStruct((seq, D), F32), jax.ShapeDtypeStruct((16, D), F32)],
        scratch_shapes=[pltpu.VMEM((tm, 2 * FH), F32)],
        compiler_params=_cp("arbitrary"),
    )(x, mix, tgt, vec, w_fi, w_fo)


def _mix_bwd(dr1, a, b, gab, uv, merged, ya, yb, vec, gp, ws_stack, ws_stack_t, bias_full, w_a, w_b, w_o, tm, scatter=()):
    seq = dr1.shape[0]
    last = seq // tm - 1
    ns = len(scatter)

    def body(dr1_ref, a_ref, b_ref, gab_ref, uv_ref, mg_ref, ya_ref, yb_ref, vec_ref, gp_ref, ws_ref, wst_ref, bias_ref,
             wa_ref, wb_ref, wo_ref, *rest):
        dya_ref, dp_ref, dws_ref, dbs_ref, st_ref, gwo_ref, gwa_ref, gwb_ref = rest[ns:ns + 8]
        acc_o, acc_a, acc_b = rest[2 * ns + 8:2 * ns + 11]
        comm = _AllToAll(rest[:ns], rest[ns + 8:2 * ns + 8], *rest[2 * ns + 11:]) if ns else None
        _host_start(pl.program_id(0), comm)

        @pl.when(pl.program_id(0) == 0)
        def _():
            dws_ref[...] = jnp.zeros_like(dws_ref)
            dbs_ref[...] = jnp.zeros_like(dbs_ref)
            st_ref[...] = jnp.zeros_like(st_ref)
            acc_o[...] = jnp.zeros_like(acc_o)
            acc_a[...] = jnp.zeros_like(acc_a)
            acc_b[...] = jnp.zeros_like(acc_b)

        dmix = (dr1_ref[...] * vec_ref[0:1, :]).astype(BF16)
        acc_o[...] += _dot_tn(mg_ref[...], dmix)
        dmerged = _dot_nt(dmix, wo_ref[...])
        sa = _sig(gab_ref[:, :D].astype(F32))
        sb = _sig(gab_ref[:, D:].astype(F32))
        da_f = dmerged * sa
        db_f = dmerged * sb
        da = da_f.astype(BF16)
        db = db_f.astype(BF16)
        dp_ref[:, 2 * G_W:2 * G_W + D] = (da_f * a_ref[...].astype(F32) * (1.0 - sa)).astype(BF16)
        dp_ref[:, 2 * G_W + D:] = (db_f * b_ref[...].astype(F32) * (1.0 - sb)).astype(BF16)
        dya_ref[...] = _dot_nt(da, wa_ref[...]).astype(BF16)
        dyb = _dot_nt(db, wb_ref[...])
        acc_a[...] += _dot_tn(ya_ref[...], da)
        acc_b[...] += _dot_tn(yb_ref[...], db)

        @pl.when(pl.program_id(0) == last)
        def _():
            gwo_ref[...] = acc_o[...].astype(BF16)
            gwa_ref[...] = acc_a[...].astype(BF16)
            gwb_ref[...] = acc_b[...].astype(BF16)

        for c in range(tm // BLK):
            rs = slice(c * BLK, (c + 1) * BLK)
            u = uv_ref[rs, :G_W].astype(F32)
            vb = uv_ref[rs, G_W:].astype(F32)
            gu, tu, tv, vhat, rstd, vn, s = _gmlp_chunk(u, vb, gp_ref, ws_ref, bias_ref)
            dyb_c = dyb[rs, :]
            ds = dyb_c * gu
            du = dyb_c * s * _gelu_grad(u, tu)
            ds_b = ds.astype(BF16)
            dvn_g = []
            for g in range(N_GRP):
                cg = slice(g * GRP_D, (g + 1) * GRP_D)
                dvn_g.append(_dot(wst_ref[:, g * BLK:(g + 1) * BLK], ds_b[:, cg]))
                dws_ref[g * BLK:(g + 1) * BLK, :] += _dot_nt(ds_b[:, cg], vn[:, cg])
            dvn = jnp.concatenate(dvn_g, axis=1)
            dbs_ref[...] += ds
            st_ref[0:1, :] += _colsum(dvn * vhat)
            st_ref[1:2, :] += _colsum(dvn)
            dgv = _ln_bwd(dvn * gp_ref[0:1, :], vhat, rstd)
            dvb = dgv * _gelu_grad(vb, tv)
            dp_ref[rs, :G_W] = du.astype(BF16)
            dp_ref[rs, G_W:2 * G_W] = dvb.astype(BF16)
        _host_finish(pl.program_id(0), last, comm)

    pw = 2 * G_W + 2 * D
    out = pl.pallas_call(
        body, name="mix_bwd", grid=(seq // tm,),
        in_specs=[_rows(tm, D), _rows(tm, D), _rows(tm, D), _rows(tm, 2 * D), _rows(tm, 2 * G_W), _rows(tm, D), _rows(tm, Q_W),
                  _rows(tm, G_W), _acc((8, D)), _acc((8, G_W)),
                  _resident((N_GRP * BLK, BLK)), _resident((BLK, N_GRP * BLK)), _acc((BLK, G_W)),
                  _resident((Q_W, D)), _resident((G_W, D)), _resident((D, D))] + _comm_specs(ns),
        out_specs=[_rows(tm, Q_W), _rows(tm, pw), _acc((N_GRP * BLK, BLK)), _acc((BLK, G_W)), _acc((8, G_W)),
                   _acc((D, D)), _acc((Q_W, D)), _acc((G_W, D))] + _comm_specs(ns),
        out_shape=[jax.ShapeDtypeStruct((seq, Q_W), BF16), jax.ShapeDtypeStruct((seq, pw), BF16),
                   jax.ShapeDtypeStruct((N_GRP * BLK, BLK), F32), jax.ShapeDtypeStruct((BLK, G_W), F32),
                   jax.ShapeDtypeStruct((8, G_W), F32), jax.ShapeDtypeStruct((D, D), BF16),
                   jax.ShapeDtypeStruct((Q_W, D), BF16), jax.ShapeDtypeStruct((G_W, D), BF16)]
        + [jax.ShapeDtypeStruct(v.shape, v.dtype) for v in scatter],
        scratch_shapes=[pltpu.VMEM((D, D), F32), pltpu.VMEM((Q_W, D), F32), pltpu.VMEM((G_W, D), F32)]
        + (_comm_scratch(ns) if ns else []),
        compiler_params=_cp("arbitrary"),
    )(dr1, a, b, gab, uv, merged, ya, yb, vec, gp, ws_stack, ws_stack_t, bias_full, w_a, w_b, w_o, *scatter)
    return out[:8], out[8:]


def _attn_bwd(q, kv, kvc, sink, dya, ya, lse, scatter=()):
    seq = q.shape[0]
    nb = seq // BLK
    n_ctx = kvc.shape[0]
    ns = len(scatter)
    Q_BLOCKS = 2
    nkv = Q_BLOCKS + 2
    steps = nb // Q_BLOCKS

    def body(q_ref, *rest):
        kv_refs = rest[:nkv]
        kvc_ref, sink_ref, do_ref, o_ref, lse_ref = rest[nkv:nkv + 5]
        rest = rest[nkv + 5:]
        dq_ref, dkv_ref, dkvc_ref, dsink_ref = rest[ns:ns + 4]
        comm = _AllToAll(rest[:ns], rest[ns + 4:2 * ns + 4], *rest[2 * ns + 4:]) if ns else None
        n = pl.program_id(0)
        _host_start(n, comm)

        @pl.when(n == 0)
        def _():
            dkv_ref[...] = jnp.zeros_like(dkv_ref)
            dkvc_ref[...] = jnp.zeros_like(dkvc_ref)
            dsink_ref[...] = jnp.zeros_like(dsink_ref)

        lane = lax.broadcasted_iota(jnp.int32, (1, LANES), 1)
        for sub in range(Q_BLOCKS):
            rs = slice(sub * BLK, (sub + 1) * BLK)
            blk = Q_BLOCKS * n + sub
            q = q_ref[rs, :]
            do = do_ref[rs, :]
            out = o_ref[rs, :]
            lse_all = lse_ref[rs, :]
            k_refs = (kvc_ref,) + kv_refs[sub:sub + 3]
            masks = _band_masks(blk, nb)
            dqs, dks, dvs = [], [], []
            for hk in range(N_KV):
                q4, ks, s = _attn_scores(q, k_refs, hk, masks)
                vs = [r[:, KV_W + hk * HEAD:KV_W + (hk + 1) * HEAD] for r in k_refs]
                lse4 = jnp.concatenate([lse_all[:, hk * GROUP + g:hk * GROUP + g + 1] for g in range(GROUP)], axis=0)
                do4 = _stack_heads(do, hk)
                delta = jnp.sum(do4.astype(F32) * _stack_heads(out, hk).astype(F32), axis=-1, keepdims=True)
                p = [jnp.exp((t - lse4).astype(BF16)) for t in s]
                ds = [t * (_dot_nt(do4, v) - delta).astype(BF16) for t, v in zip(p, vs)]
                dq4 = _dot(ds[0], ks[0])
                for t, k in zip(ds[1:], ks[1:]):
                    dq4 = dq4 + _dot(t, k)
                dq4 = dq4 * SCALE
                dqs += [dq4[g * BLK:(g + 1) * BLK, :] for g in range(GROUP)]
                dks.append([_dot_tn(t, q4) for t in ds])
                dvs.append([_dot_tn(t, do4) for t in p])
                ps = jnp.exp(_sink_rows(sink_ref, hk) - lse4) * delta
                for g in range(GROUP):
                    part = -jnp.sum(ps[g * BLK:(g + 1) * BLK, :], axis=0, keepdims=True)
                    dsink_ref[0:1, :] += jnp.where(lane == hk * GROUP + g, part, 0.0)
            dq_ref[rs, :] = jnp.concatenate(dqs, axis=1)

            def piece(i):
                return jnp.concatenate([dks[0][i], dks[1][i], dvs[0][i], dvs[1][i]], axis=1)

            dkvc_ref[...] += piece(0)
            starts = (jnp.maximum(blk - 1, 0), blk, jnp.minimum(blk + 1, nb - 1))
            for i, st in enumerate(starts):
                r = pl.ds(pl.multiple_of(st * BLK, BLK), BLK)
                dkv_ref[r, :] += piece(i + 1)
        _host_finish(n, steps - 1, comm)

    tq = Q_BLOCKS * BLK
    out = pl.pallas_call(
        body, name="attn_bwd", grid=(steps,),
        in_specs=[_rows(tq, Q_W)] + _kv_specs(nb, Q_BLOCKS) + [_acc((n_ctx, 2 * KV_W)), pl.BlockSpec(memory_space=pltpu.SMEM),
                                                     _rows(tq, Q_W), _rows(tq, Q_W), _rows(tq, LANES)] + _comm_specs(ns),
        out_specs=[_rows(tq, Q_W), _acc((seq, 2 * KV_W)), _acc((n_ctx, 2 * KV_W)), _acc((8, LANES))] + _comm_specs(ns),
        out_shape=[jax.ShapeDtypeStruct((seq, Q_W), F32), jax.ShapeDtypeStruct((seq, 2 * KV_W), F32),
                   jax.ShapeDtypeStruct((n_ctx, 2 * KV_W), F32), jax.ShapeDtypeStruct((8, LANES), F32)]
        + [jax.ShapeDtypeStruct(v.shape, v.dtype) for v in scatter],
        scratch_shapes=_comm_scratch(ns) if ns else [],
        compiler_params=_cp("arbitrary"),
    )(q, *([kv] * nkv), kvc, sink, dya, ya, lse, *scatter)
    return out[:4], out[4:]


def _proj_bwd(dq, dkv, dpb, x, dr1, modx, w_in, cos, sin, tm, scatter=()):
    seq = x.shape[0]
    pw = IN_W - Q_W - 2 * KV_W
    ns = len(scatter)

    def body(dq_ref, dkv_ref, dpb_ref, x_ref, dr1_ref, mod_ref, w_ref, cos_ref, sin_ref, *rest):
        dqkv_ref, gx_ref, st_ref = rest[ns:ns + 3]
        comm = _AllToAll(rest[:ns], rest[ns + 3:2 * ns + 3], *rest[2 * ns + 3:]) if ns else None
        _host_start(pl.program_id(0), comm)

        @pl.when(pl.program_id(0) == 0)
        def _():
            st_ref[...] = jnp.zeros_like(st_ref)

        cos1, sin1 = cos_ref[...], sin_ref[...]
        cos2 = jnp.concatenate([cos1, cos1], axis=1)
        sin2 = jnp.concatenate([sin1, sin1], axis=1)
        for j in range(Q_W // 256):
            cs = slice(256 * j, 256 * (j + 1))
            dqkv_ref[:, cs] = _unrope(dq_ref[:, cs], cos2, sin2).astype(BF16)
        dqkv_ref[:, Q_W:Q_W + KV_W] = _unrope(dkv_ref[:, :KV_W], cos1, sin1).astype(BF16)
        dqkv_ref[:, Q_W + KV_W:] = dkv_ref[:, KV_W:].astype(BF16)
        o = Q_W + 2 * KV_W
        dh = _dot(dqkv_ref[...], w_ref[:o, :]) + _dot(dpb_ref[...], w_ref[o:, :])
        xhat, rstd = _ln(x_ref[...])
        st_ref[0:1, :] += _colsum(dh)
        st_ref[1:2, :] += _colsum(dh * xhat)
        gx_ref[...] = _ln_bwd(dh * (1.0 + mod_ref[1:2, :]), xhat, rstd) + ALPHA * dr1_ref[...]
        _host_finish(pl.program_id(0), seq // tm - 1, comm)

    out = pl.pallas_call(
        body, name="proj_bwd", grid=(seq // tm,),
        in_specs=[_rows(tm, Q_W), _rows(tm, 2 * KV_W), _rows(tm, pw), _rows(tm, D), _rows(tm, D), _acc((8, D)),
                  _resident((IN_W, D)), _rows(tm, LANES), _rows(tm, LANES)] + _comm_specs(ns),
        out_specs=[_rows(tm, Q_W + 2 * KV_W), _rows(tm, D), _acc((8, D))] + _comm_specs(ns),
        out_shape=[jax.ShapeDtypeStruct((seq, Q_W + 2 * KV_W), BF16), jax.ShapeDtypeStruct((seq, D), F32),
                   jax.ShapeDtypeStruct((8, D), F32)] + [jax.ShapeDtypeStruct(v.shape, v.dtype) for v in scatter],
        scratch_shapes=_comm_scratch(ns) if ns else [],
        compiler_params=_cp("arbitrary"),
    )(dq, dkv, dpb, x, dr1, modx, w_in, cos, sin, *scatter)
    return out[:3], out[3:]


def _ctx_bwd(dkvc, ctx, hc, w_in):
    n_ctx = ctx.shape[0]

    def body(dkvc_ref, ctx_ref, hc_ref, w_ref, dw_ref, st_ref):
        d = dkvc_ref[...].astype(BF16)
        dw_ref[...] = _dot_tn(d, hc_ref[...])
        dhc = _dot(d, w_ref[...])
        xhat, _ = _ln(ctx_ref[...])
        st_ref[...] = jnp.zeros_like(st_ref)
        st_ref[0:1, :] = _colsum(dhc)
        st_ref[1:2, :] = _colsum(dhc * xhat)

    return pl.pallas_call(
        body, name="ctx_bwd", grid=(1,),
        in_specs=[_acc((n_ctx, 2 * KV_W)), _acc((n_ctx, D)), _acc((n_ctx, D)), _kv_rows_of_w_in()],
        out_specs=[_acc((2 * KV_W, D)), _acc((8, D))],
        out_shape=[jax.ShapeDtypeStruct((2 * KV_W, D), F32), jax.ShapeDtypeStruct((8, D), F32)],
        compiler_params=_cp("arbitrary"),
    )(dkvc, ctx, hc, w_in)


def _tn_matmul(a, b, tn, name, out_dtype, shard_major=False, init=None, tk=512, out_rows=None, into=None,
               scatter=(), gather=()):
    t, ka = a.shape
    n = b.shape[1]
    tk = min(tk, t)
    nk = t // tk
    nj = n // tn
    has_init = init is not None
    in_place = into is not None
    assert not (scatter and gather) and not (in_place and (nj != 1 or shard_major or out_rows))
    moved = list(scatter) + list(gather)
    pattern = _AllToAll if scatter else _Gather
    ns = len(moved)
    n_in = 2 + has_init + in_place
    n_scr = 3 if in_place else 1

    def body(*refs):
        a_ref, b_ref = refs[:2]
        i_ref = refs[2] if has_init else None
        rest = refs[n_in:]
        o_ref = rest[ns]
        acc_ref = rest[2 * ns + 1]
        comm = pattern(rest[:ns], rest[ns + 1:2 * ns + 1], *rest[2 * ns + 1 + n_scr:]) if ns else None
        k = pl.program_id(1)
        step = pl.program_id(0) * nk + k
        _host_start(step, comm)

        @pl.when(k == 0)
        def _():
            acc_ref[...] = jnp.zeros_like(acc_ref)
            if has_init:
                acc_ref[ka - init.shape[0]:, :] = i_ref[...]

        acc_ref[...] += _dot_tn(a_ref[...], b_ref[...])

        @pl.when(k == nk - 1)
        def _():
            if in_place:
                stage_ref, sem = rest[2 * ns + 2:2 * ns + 4]
                stage_ref[...] = acc_ref[...].astype(out_dtype)
                write = pltpu.make_async_copy(stage_ref, o_ref.at[pl.ds(into[1], ka), :], sem.at[0])
                write.start()
                write.wait()
            else:
                o_ref[...] = acc_ref[...].astype(out_dtype)

        _host_finish(step, nj * nk - 1, comm, forward_at=(3 * nj * nk) // 4 if gather else None)

    in_specs = [pl.BlockSpec((tk, ka), lambda j, k: (k, 0)), pl.BlockSpec((tk, tn), lambda j, k: (k, j))]
    args = [a, b]
    if has_init:
        in_specs.append(pl.BlockSpec((init.shape[0], tn), lambda j, k: (0, j)))
        args.append(init)
    scratch = [pltpu.VMEM((ka, tn), F32)]
    aliases = {}
    if in_place:
        in_specs.append(pl.BlockSpec(memory_space=pl.ANY))
        args.append(into[0])
        aliases = {n_in - 1: 0}
        out_spec = pl.BlockSpec(memory_space=pl.ANY)
        out_shape = jax.ShapeDtypeStruct(into[0].shape, into[0].dtype)
        scratch += [pltpu.VMEM((ka, tn), out_dtype), pltpu.SemaphoreType.DMA((1,))]
    elif shard_major:
        out_spec = pl.BlockSpec((None, ka, tn), lambda j, k: (j, 0, 0))
        out_shape = jax.ShapeDtypeStruct((nj, ka, tn), out_dtype)
    else:
        out_spec = pl.BlockSpec((ka, tn), lambda j, k: (0, j))
        out_shape = jax.ShapeDtypeStruct((out_rows or ka, n), out_dtype)
    out = pl.pallas_call(
        body, name=name, grid=(nj, nk), in_specs=in_specs + _comm_specs(ns), out_specs=[out_spec] + _comm_specs(ns),
        out_shape=[out_shape] + [jax.ShapeDtypeStruct(v.shape, v.dtype) for v in scatter] + _gathered_shapes(gather),
        scratch_shapes=scratch + (_comm_scratch(ns) if ns else []), input_output_aliases=aliases,
        compiler_params=_cp("arbitrary", "arbitrary"),
    )(*args, *moved)
    return (out[0], out[1:]) if ns else out[0]


ADA_TILE = 512


def _gather_and_modulate(c_rows, c_ctx, w_half, w_ada):
    cs = w_ada.shape[1]
    vmem = pl.BlockSpec(memory_space=pltpu.VMEM)

    def body(c_ref, cctx_ref, wh_ref, wada_ref, cg_ref, wg_ref, mod_ref, modg_ref, sc_ref,
             c_v, wada_v, mod_v, send_c, recv_c, send_w, recv_w, send_m, recv_m, local):
        gc = _Gather([c_ref], [cg_ref], send_c, recv_c)
        gw = _Gather([wh_ref], [wg_ref], send_w, recv_w)
        gm = _Gather([mod_ref], [modg_ref], send_m, recv_m)
        px, py, pc = _my_pos()
        mine = 4 * px + 2 * py + pc
        gc.start()
        gw.start()
        own_c = pltpu.make_async_copy(c_ref, cg_ref.at[mine], local.at[0])
        load_w = pltpu.make_async_copy(wada_ref, wada_v, local.at[1])
        own_c.start()
        load_w.start()
        gc.finish()
        own_c.wait()
        load_c = pltpu.make_async_copy(cg_ref, c_v, local.at[2])
        load_c.start()
        load_c.wait()
        cc = jnp.concatenate([c_v[i, 0:1, :] for i in range(N_DEV)] + [cctx_ref[...], jnp.zeros((7, D), F32)], axis=0)
        sc = cc * _sig(cc)
        sc_ref[...] = sc
        load_w.wait()
        for j in range(cs // ADA_TILE):
            cols = slice(j * ADA_TILE, (j + 1) * ADA_TILE)
            mod_v[:, cols] = _dot(sc.astype(BF16), wada_v[:, cols].astype(BF16))
        store_m = pltpu.make_async_copy(mod_v, mod_ref, local.at[3])
        store_m.start()
        store_m.wait()
        own_m = pltpu.make_async_copy(mod_ref, modg_ref.at[mine], local.at[4])
        own_m.start()
        gm.start()
        gm.finish()
        own_m.wait()
        gw.finish()

    any_ = pl.BlockSpec(memory_space=pl.ANY)
    return pl.pallas_call(
        body, name="gather_and_modulate",
        in_specs=[any_, vmem, any_, any_], out_specs=[any_, any_, any_, any_, vmem],
        scratch_shapes=[pltpu.VMEM((N_DEV, 8, D), F32), pltpu.VMEM((D, cs), F32), pltpu.VMEM((16, cs), F32)]
        + _comm_scratch(1) + _comm_scratch(1) + _comm_scratch(1) + [pltpu.SemaphoreType.DMA((5,))],
        out_shape=[jax.ShapeDtypeStruct((N_DEV, 8, D), F32), jax.ShapeDtypeStruct((N_DEV,) + w_half.shape, w_half.dtype),
                   jax.ShapeDtypeStruct((16, cs), F32), jax.ShapeDtypeStruct((N_DEV, 16, cs), F32),
                   jax.ShapeDtypeStruct((16, D), F32)],
        compiler_params=pltpu.CompilerParams(vmem_limit_bytes=VMEM_LIMIT),
    )(c_rows, c_ctx, w_half, w_ada)


def _ada_bwd(sc_all_t, dm_all, dmc, w_ada, m, v):
    cs = w_ada.shape[1]

    def body(st_ref, dm_ref, dmc_ref, w_ref, m_ref, v_ref, gw_ref, d_ref, nm_ref, nv_ref, part_ref):
        @pl.when(pl.program_id(0) == 0)
        def _():
            part_ref[...] = jnp.zeros_like(part_ref)

        g = _dot(st_ref[...].astype(BF16), dm_ref[...].astype(BF16))
        gw_ref[...] = g
        d_ref[...], nm_ref[...], nv_ref[...] = _adam_update(w_ref[...], g, m_ref[...], v_ref[...])
        part_ref[...] += _dot_nt(dmc_ref[...].astype(BF16), w_ref[...].astype(BF16))

    cols = pl.BlockSpec((D, ADA_TILE), lambda j: (0, j))
    shp = jax.ShapeDtypeStruct((D, cs), F32)
    return pl.pallas_call(
        body, name="ada_bwd_adamw", grid=(cs // ADA_TILE,),
        in_specs=[_acc((D, 16)), pl.BlockSpec((16, ADA_TILE), lambda j: (0, j)), pl.BlockSpec((8, ADA_TILE), lambda j: (0, j)),
                  cols, cols, cols],
        out_specs=[cols] * 4 + [_acc((8, D))],
        out_shape=[shp] * 4 + [jax.ShapeDtypeStruct((8, D), F32)],
        compiler_params=_cp("arbitrary"),
    )(sc_all_t, dm_all, dmc, w_ada, m, v)


def _sum8(x, name, tr=256):
    _, r, c = x.shape
    tr = min(tr, r)
    while r % tr:
        tr -= 16

    def body(x_ref, o_ref):
        acc = x_ref[0].astype(F32)
        for i in range(1, N_DEV):
            acc = acc + x_ref[i].astype(F32)
        o_ref[...] = acc

    return pl.pallas_call(
        body, name=name, grid=(r // tr,),
        in_specs=[pl.BlockSpec((N_DEV, tr, c), lambda i: (0, i, 0))],
        out_specs=pl.BlockSpec((tr, c), lambda i: (i, 0)),
        out_shape=jax.ShapeDtypeStruct((r, c), F32),
        compiler_params=_cp("arbitrary"),
    )(x)


def _sum_blocks(recv, src, me, name, tr=256):
    _, r, c = recv.shape
    tr = min(tr, r)
    while r % tr:
        tr -= 16

    def body(me_ref, recv_ref, own_ref, o_ref):
        acc = own_ref[...].astype(F32)
        for k in range(1, N_DEV):
            acc = acc + recv_ref[me_ref[0] ^ k].astype(F32)
        o_ref[...] = acc

    return pl.pallas_call(
        body, name=name,
        grid_spec=pltpu.PrefetchScalarGridSpec(
            num_scalar_prefetch=1, grid=(r // tr,),
            in_specs=[pl.BlockSpec((N_DEV, tr, c), lambda i, me_ref: (0, i, 0)),
                      pl.BlockSpec((None, tr, c), lambda i, me_ref: (me_ref[0], i, 0))],
            out_specs=pl.BlockSpec((tr, c), lambda i, me_ref: (i, 0))),
        out_shape=jax.ShapeDtypeStruct((r, c), F32),
        compiler_params=_cp("arbitrary"),
    )(me, recv, src)


def _sum_blocks_hosting(recvs, srcs, me, name, scatter, gather):
    n, ns, ng = len(recvs), len(scatter), len(gather)

    def body(me_ref, *refs):
        recv_hbm, own_hbm = refs[:n], refs[n:2 * n]
        moved_in = refs[2 * n:2 * n + ns + ng]
        outs = refs[2 * n + ns + ng:]
        moved_out, scratch = outs[n:n + ns + ng], outs[n + ns + ng:]
        recv_v, own_v, load_sems, sems = scratch[:n], scratch[n:2 * n], scratch[2 * n], scratch[2 * n + 1:]
        t = _AllToAll(moved_in[:ns], moved_out[:ns], sems[0], sems[1])
        g = _Gather(moved_in[ns:], moved_out[ns:], sems[2], sems[3])
        g.start()
        t.start()
        me = me_ref[0]
        loads = [(pltpu.make_async_copy(recv_hbm[i], recv_v[i], load_sems.at[2 * i]),
                  pltpu.make_async_copy(own_hbm[i].at[me], own_v[i], load_sems.at[2 * i + 1])) for i in range(n)]
        for load in loads:
            load[0].start()
            load[1].start()
        for i in range(n):
            loads[i][0].wait()
            loads[i][1].wait()
            acc = own_v[i][...].astype(F32)
            for k in range(1, N_DEV):
                acc = acc + recv_v[i][me ^ k].astype(F32)
            outs[i][...] = acc
        g.finish()
        t.finish()

    def once(shape, index_map):
        return pl.BlockSpec(shape, index_map, pipeline_mode=pl.Buffered(1))

    shapes = [v.shape[1:] for v in recvs]
    out = pl.pallas_call(
        body, name=name,
        grid_spec=pltpu.PrefetchScalarGridSpec(
            num_scalar_prefetch=1, grid=(1,),
            in_specs=_comm_specs(2 * n + ns + ng),
            out_specs=[once(s, lambda i, me_ref: (0, 0)) for s in shapes] + _comm_specs(ns + ng),
            scratch_shapes=[pltpu.VMEM(v.shape, v.dtype) for v in recvs] + [pltpu.VMEM(v.shape[1:], v.dtype) for v in recvs]
            + [pltpu.SemaphoreType.DMA((2 * n,))] + _comm_scratch(ns) + _comm_scratch(ng)),
        out_shape=[jax.ShapeDtypeStruct(s, F32) for s in shapes]
        + [jax.ShapeDtypeStruct(v.shape, v.dtype) for v in scatter] + _gathered_shapes(gather),
        compiler_params=_cp("arbitrary"),
    )(me, *recvs, *srcs, *scatter, *gather)
    return out[:n], out[n:n + ns], out[n + ns:]


def _sum8_many(xs, name):
    n = len(xs)

    def body(*refs):
        for x_ref, o_ref in zip(refs[:n], refs[n:]):
            acc = x_ref[0]
            for i in range(1, N_DEV):
                acc = acc + x_ref[i]
            o_ref[...] = acc

    vmem = pl.BlockSpec(memory_space=pltpu.VMEM)
    return pl.pallas_call(
        body, name=name, in_specs=[vmem] * n, out_specs=[vmem] * n,
        out_shape=[jax.ShapeDtypeStruct(v.shape[1:], v.dtype) for v in xs],
        compiler_params=pltpu.CompilerParams(vmem_limit_bytes=VMEM_LIMIT),
    )(*xs)


def _adam_update(w, g, m, v):
    nm = ADAM_B1 * m + (1.0 - ADAM_B1) * g
    nv = ADAM_B2 * v + (1.0 - ADAM_B2) * (g * g)
    m_hat = nm / (1.0 - ADAM_B1 ** ADAM_STEP)
    v_hat = nv / (1.0 - ADAM_B2 ** ADAM_STEP)
    return -ADAM_LR * (m_hat / (jnp.sqrt(v_hat) + ADAM_EPS) + ADAM_WD * w), nm, nv


ROW_LOSS, ROW_LN2_G, ROW_LN2_B, ROW_LN1_G, ROW_LN1_B = 0, 1, 2, 10, 11
ROWS_DMOD_X = (16, 17, 12, 9, 8, 3)
ROWS_DMOD_C = (24, 25)
SMALL = ("c_ctx", "b_ada", "attn_sink", "gmlp_ln_g", "gmlp_ln_b", "w_spatial", "b_spatial", "ln1_g", "ln1_b", "ln2_g", "ln2_b")


def _adamw_small(sums, dsc, w, m, v):
    n = len(SMALL)

    def body(*refs):
        st_ref, gm_ref, sk_ref, ws_ref, bs_ref, dsc_ref = refs[:6]
        w_refs = dict(zip(SMALL, refs[6:6 + n]))
        m_refs = dict(zip(SMALL, refs[6 + n:6 + 2 * n]))
        v_refs = dict(zip(SMALL, refs[6 + 2 * n:6 + 3 * n]))
        outs = refs[6 + 3 * n:]
        c = w_refs["c_ctx"][...]
        sg = _sig(c)
        dmod = [st_ref[r:r + 1, :] for r in ROWS_DMOD_X]
        dmod[0] = dmod[0] + st_ref[ROWS_DMOD_C[0]:ROWS_DMOD_C[0] + 1, :]
        dmod[1] = dmod[1] + st_ref[ROWS_DMOD_C[1]:ROWS_DMOD_C[1] + 1, :]
        grads = dict(
            c_ctx=dsc_ref[0:1, :] * (sg * (1.0 + c * (1.0 - sg))),
            b_ada=jnp.concatenate(dmod, axis=1),
            attn_sink=sk_ref[0:1, 0:N_KV * GROUP],
            gmlp_ln_g=gm_ref[0:1, :], gmlp_ln_b=gm_ref[1:2, :],
            w_spatial=ws_ref[...], b_spatial=bs_ref[...],
            ln1_g=st_ref[ROW_LN1_G:ROW_LN1_G + 1, :], ln1_b=st_ref[ROW_LN1_B:ROW_LN1_B + 1, :],
            ln2_g=st_ref[ROW_LN2_G:ROW_LN2_G + 1, :], ln2_b=st_ref[ROW_LN2_B:ROW_LN2_B + 1, :])
        for i, name in enumerate(SMALL):
            g = grads[name]
            d, nm, nv = _adam_update(w_refs[name][...], g, m_refs[name][...], v_refs[name][...])
            outs[i][...] = g
            outs[n + i][...] = d
            outs[2 * n + i][...] = nm
            outs[3 * n + i][...] = nv

    vmem = pl.BlockSpec(memory_space=pltpu.VMEM)
    args = list(sums) + [dsc] + [w[k] for k in SMALL] + [m[k] for k in SMALL] + [v[k] for k in SMALL]
    shapes = [jax.ShapeDtypeStruct(w[k].shape, F32) for k in SMALL]
    out = pl.pallas_call(
        body, name="adamw_small", in_specs=[vmem] * len(args), out_specs=[vmem] * (4 * n), out_shape=shapes * 4,
        compiler_params=pltpu.CompilerParams(vmem_limit_bytes=VMEM_LIMIT),
    )(*args)
    return [dict(zip(SMALL, out[i * n:(i + 1) * n])) for i in range(4)]


def _adamw_halves(w, mine, theirs, m, v, c_arr, name):
    r, c = w.shape
    tr = min(256, r // 2)
    while (r // 2) % tr:
        tr -= 8
    nt = (r // 2) // tr

    def body(c_ref, w_ref, mine_ref, theirs_ref, m_ref, v_ref, g_ref, d_ref, nm_ref, nv_ref):
        g = jnp.where(pl.program_id(0) == c_ref[0], mine_ref[...], theirs_ref[...])
        g_ref[...] = g
        d_ref[...], nm_ref[...], nv_ref[...] = _adam_update(w_ref[...], g, m_ref[...], v_ref[...])

    whole = pl.BlockSpec((tr, c), lambda hb, i, c_ref: (hb * nt + i, 0))
    mine_spec = pl.BlockSpec((tr, c), lambda hb, i, c_ref: (jnp.where(hb == c_ref[0], i, 0), 0))
    theirs_spec = pl.BlockSpec((tr, c), lambda hb, i, c_ref: (jnp.where(hb == c_ref[0], 0, i), 0))
    shp = jax.ShapeDtypeStruct((r, c), F32)
    return pl.pallas_call(
        body, name=name,
        grid_spec=pltpu.PrefetchScalarGridSpec(
            num_scalar_prefetch=1, grid=(2, nt), in_specs=[whole, mine_spec, theirs_spec, whole, whole],
            out_specs=[whole] * 4),
        out_shape=[shp] * 4,
        compiler_params=_cp("arbitrary", "arbitrary"),
    )(c_arr, w, mine, theirs, m, v)


def _my_pos():
    return lax.axis_index("x"), lax.axis_index("y"), lax.axis_index("c")


N_COPY = 7


class _Gather:
    def __init__(self, x_refs, out_refs, send_sems, recv_sems):
        self.x_refs, self.out_refs = x_refs, out_refs
        self.send_sems, self.recv_sems = send_sems, recv_sems
        x, y, c = _my_pos()
        self.c = c
        self.me, self.sibling = (x, y, c), (x, y, 1 - c)
        self.chips = [(1 - x, y), (x, 1 - y), (1 - x, 1 - y)]

    def _copy(self, a, k, block, to, from_input=False):
        px, py, pc = block
        rows = self.out_refs[a].at[4 * px + 2 * py + pc]
        return pltpu.make_async_remote_copy(
            src_ref=self.x_refs[a] if from_input else rows, dst_ref=rows,
            send_sem=self.send_sems.at[a * N_COPY + k], recv_sem=self.recv_sems.at[a * N_COPY + k],
            device_id=to, device_id_type=MESH)

    def start(self):
        n = len(self.x_refs)
        for a in range(n):
            self._copy(a, 0, self.me, self.sibling, from_input=True).start()
        for j, chip in enumerate(self.chips):
            for a in range(n):
                self._copy(a, 1 + j, self.me, (*chip, self.c), from_input=True).start()

    def forward(self):
        c = self.c
        for j, chip in enumerate(self.chips):
            for a in range(len(self.x_refs)):
                self._copy(a, 1 + j, (*chip, c), self.me).wait_recv()
                self._copy(a, 4 + j, (*chip, c), self.sibling).start()

    def finish(self):
        self.forward()
        self.drain()

    def drain(self):
        n = len(self.x_refs)
        c = self.c
        for a in range(n):
            self._copy(a, 0, self.sibling, self.me).wait_recv()
        for j, chip in enumerate(self.chips):
            for a in range(n):
                self._copy(a, 4 + j, (*chip, 1 - c), self.me).wait_recv()
        for a in range(n):
            self._copy(a, 0, self.me, self.sibling, from_input=True).wait_send()
            for j, chip in enumerate(self.chips):
                self._copy(a, 1 + j, self.me, (*chip, c), from_input=True).wait_send()
                self._copy(a, 4 + j, (*chip, c), self.sibling).wait_send()


def _comm_scratch(n):
    return [pltpu.SemaphoreType.DMA((n * N_COPY,)), pltpu.SemaphoreType.DMA((n * N_COPY,))]


def _comm_specs(n):
    return [pl.BlockSpec(memory_space=pl.ANY)] * n


def _gathered_shapes(xs):
    return [jax.ShapeDtypeStruct((N_DEV,) + v.shape, v.dtype) for v in xs]


def _with_own(gathered, xs, me):
    return [lax.dynamic_update_index_in_dim(g, v, me, 0) for g, v in zip(gathered, xs)]


class _AllToAll:
    def __init__(self, x_refs, out_refs, send_sems, recv_sems):
        self.x_refs, self.out_refs = x_refs, out_refs
        self.send_sems, self.recv_sems = send_sems, recv_sems
        self.pos = _my_pos()
        x, y, c = self.pos
        self.me = 4 * x + 2 * y + c

    def _peer(self, k):
        x, y, c = self.pos
        return (x ^ ((k >> 2) & 1), y ^ ((k >> 1) & 1), c ^ (k & 1))

    def _copy(self, a, k):
        p = self._peer(k)
        return pltpu.make_async_remote_copy(
            src_ref=self.x_refs[a].at[4 * p[0] + 2 * p[1] + p[2]], dst_ref=self.out_refs[a].at[self.me],
            send_sem=self.send_sems.at[a * N_COPY + k - 1], recv_sem=self.recv_sems.at[a * N_COPY + k - 1],
            device_id=p, device_id_type=MESH)

    def start(self):
        for k in range(1, N_DEV):
            for a in range(len(self.x_refs)):
                self._copy(a, k).start()

    def finish(self):
        for a in range(len(self.x_refs)):
            for k in range(1, N_DEV):
                self._copy(a, k).wait_recv()
            for k in range(1, N_DEV):
                self._copy(a, k).wait_send()


def _exchange(name, scatter=(), gather=(), sibling=()):
    ns, ng, nx = len(scatter), len(gather), len(sibling)
    n = ns + ng + nx

    def body(*refs):
        ins, outs, sems = refs[:n], refs[n:2 * n], list(refs[2 * n:])
        t = _AllToAll(ins[:ns], outs[:ns], sems.pop(0), sems.pop(0)) if ns else None
        g = _Gather(ins[ns:ns + ng], outs[ns:ns + ng], sems.pop(0), sems.pop(0)) if ng else None
        x, y, c = _my_pos()

        def push(a):
            return pltpu.make_async_remote_copy(
                src_ref=ins[ns + ng + a], dst_ref=outs[ns + ng + a], send_sem=sems[0].at[a], recv_sem=sems[1].at[a],
                device_id=(x, y, 1 - c), device_id_type=MESH)

        for comm in (g, t):
            if comm is not None:
                comm.start()
        for a in range(nx):
            push(a).start()
        for comm in (g, t):
            if comm is not None:
                comm.finish()
        for a in range(nx):
            push(a).wait_recv()
            push(a).wait_send()

    scratch = (_comm_scratch(ns) if ns else []) + (_comm_scratch(ng) if ng else [])
    scratch += [pltpu.SemaphoreType.DMA((nx,)), pltpu.SemaphoreType.DMA((nx,))] if nx else []
    out = pl.pallas_call(
        body, name=name,
        out_shape=[jax.ShapeDtypeStruct(v.shape, v.dtype) for v in scatter] + _gathered_shapes(gather)
        + [jax.ShapeDtypeStruct(v.shape, v.dtype) for v in sibling],
        in_specs=_comm_specs(n), out_specs=_comm_specs(n), scratch_shapes=scratch,
    )(*scatter, *gather, *sibling)
    return out[:ns], out[ns:ns + ng], out[ns + ng:]


def _sum_and_exchange(recv, src, me, name, gather, sibling):
    ng, nx = len(gather), len(sibling)
    shape = recv.shape[1:]

    def body(me_ref, recv_hbm, own_hbm, *refs):
        g_in, x_in = refs[:ng], refs[ng:ng + nx]
        outs = refs[ng + nx:]
        sum_ref, theirs_ref = outs[:2]
        g_out, x_out = outs[2:2 + ng], outs[2 + ng:2 + ng + nx]
        recv_v, own_v, sum_v, local, send_sems, recv_sems, g_send, g_recv = outs[2 + ng + nx:]
        g = _Gather(g_in, g_out, g_send, g_recv)
        x, y, c = _my_pos()

        def push(a):
            src_ref, dst_ref = (x_in[a], x_out[a]) if a < nx else (sum_ref, theirs_ref)
            return pltpu.make_async_remote_copy(
                src_ref=src_ref, dst_ref=dst_ref, send_sem=send_sems.at[a], recv_sem=recv_sems.at[a],
                device_id=(x, y, 1 - c), device_id_type=MESH)

        g.start()
        for a in range(nx):
            push(a).start()
        me = me_ref[0]
        loads = [pltpu.make_async_copy(recv_hbm, recv_v, local.at[0]),
                 pltpu.make_async_copy(own_hbm.at[me], own_v, local.at[1])]
        for load in loads:
            load.start()
        for load in loads:
            load.wait()
        acc = own_v[...].astype(F32)
        for k in range(1, N_DEV):
            acc = acc + recv_v[me ^ k].astype(F32)
        sum_v[...] = acc
        store = pltpu.make_async_copy(sum_v, sum_ref, local.at[2])
        store.start()
        store.wait()
        push(nx).start()
        g.finish()
        for a in range(nx + 1):
            push(a).wait_recv()
            push(a).wait_send()

    f32 = jax.ShapeDtypeStruct(shape, F32)
    out = pl.pallas_call(
        body, name=name,
        grid_spec=pltpu.PrefetchScalarGridSpec(
            num_scalar_prefetch=1, grid=(1,), in_specs=_comm_specs(2 + ng + nx), out_specs=_comm_specs(2 + ng + nx),
            scratch_shapes=[pltpu.VMEM(recv.shape, recv.dtype), pltpu.VMEM(shape, recv.dtype), pltpu.VMEM(shape, F32),
                            pltpu.SemaphoreType.DMA((3,)), pltpu.SemaphoreType.DMA((nx + 1,)),
                            pltpu.SemaphoreType.DMA((nx + 1,))] + _comm_scratch(ng)),
        out_shape=[f32, f32] + _gathered_shapes(gather) + [jax.ShapeDtypeStruct(v.shape, v.dtype) for v in sibling],
        compiler_params=_cp("arbitrary"),
    )(me, recv, src, *gather, *sibling)
    return out[0], out[1], out[2:2 + ng], out[2 + ng:]


def _row_tile(seq, want):
    return min(want, seq)


def _local_step(x, ctx, tgt, mod_x, mod_c, wb, sink, gmlp_g, gmlp_b, w_s, b_s, ln1_g, ln1_b, ln2_g, ln2_b,
                later=None, me=None):
    seq = x.shape[0]
    on_mesh = me is not None
    modx1 = jnp.concatenate([mod_x[0:2], jnp.zeros((6, D), F32)], axis=0)
    modc = jnp.concatenate([mod_c[0:2], jnp.zeros((6, D), F32)], axis=0)
    vec = jnp.concatenate([mod_x[2:3], ln1_g, ln1_b, mod_x[3:6], ln2_g, ln2_b], axis=0)
    gp = jnp.concatenate([gmlp_g, gmlp_b, jnp.zeros((6, G_W), F32)], axis=0)
    ws_stack = w_s.reshape(N_GRP * BLK, BLK).astype(BF16)
    ws_stack_t = jnp.transpose(w_s, (2, 0, 1)).reshape(BLK, N_GRP * BLK).astype(BF16)
    bias_full = jnp.repeat(b_s.T, GRP_D, axis=1)
    cos, sin = _rope_tables(seq)
    w_in = wb["w_in"]
    tm_big = _row_tile(seq, 512)
    tm_ffn = _row_tile(seq, 256)

    hc, kvc, vac = _ctx_fwd(ctx, modc, w_in)
    behind_proj = ("w_a", "w_b", "w_o") if on_mesh else ()
    behind_attn = ("w_fi",) if on_mesh else ()
    behind_mix = ("w_fo",) if on_mesh else ()
    wb = dict(wb)

    def whole(names, gathered):
        for n, g in zip(names, _with_own(list(gathered), [later[n] for n in names], me)):
            wb[n] = g.reshape(-1, g.shape[2]) if n in ROW_SHARDED else g.reshape(N_SHARD, 2 * g.shape[1], g.shape[2])

    (h, q, kv, va, uv, gab), got = _proj_fwd(x, modx1, w_in, cos, sin, _row_tile(seq, 1024), gather=[later[n] for n in behind_proj])
    whole(behind_proj, got)
    if on_mesh:
        for n in ("w_a", "w_b"):
            wb[n] = wb[n].transpose(1, 0, 2).reshape(wb[n].shape[1], D)
    (ya, lse), got = _attn_fwd(q, kv, va, kvc, vac, sink, gather=[later[n] for n in behind_attn])
    whole(behind_attn, got)
    (a, b, mix, merged, yb), got = _mix_fwd(uv, gab, ya, gp, ws_stack, bias_full, wb["w_a"], wb["w_b"], wb["w_o"], tm_big,
                                            gather=[later[n] for n in behind_mix])
    whole(behind_mix, got)
    act, h2, dff, df, dr1, st_ffn = _ffn(x, mix, tgt, vec, wb["w_fi"], wb["w_fo"], tm_ffn)
    blocks, recv = {}, {}
    blocks["w_fo"] = _eighths(_tn_matmul(act, df, 512, "tn_w_ffn_out", BF16, tk=2048))
    if on_mesh:
        g_w_fi, (recv["w_fo"],) = _tn_matmul(h2, dff, FH_SHARD, "tn_w_ffn_in", BF16, shard_major=True, tk=2048,
                                            scatter=[blocks["w_fo"]])
    else:
        g_w_fi = _tn_matmul(h2, dff, FH_SHARD, "tn_w_ffn_in", BF16, shard_major=True, tk=2048)
    blocks["w_fi"] = _eighths(g_w_fi)
    (dya, dpb, dws, dbs_full, st4, g_w_o, g_w_a, g_w_b), got = _mix_bwd(
        dr1, a, b, gab, uv, merged, ya, yb, vec, gp, ws_stack, ws_stack_t, bias_full, wb["w_a"], wb["w_b"], wb["w_o"],
        tm_big, scatter=[blocks["w_fi"]] if on_mesh else ())
    recv.update(zip(("w_fi",), got))
    shard_major = [g.reshape(g.shape[0], N_SHARD, D // N_SHARD).transpose(1, 0, 2) for g in (g_w_a, g_w_b)]
    blocks.update(w_o=_eighths(g_w_o), w_a=_eighths(shard_major[0]), w_b=_eighths(shard_major[1]))
    mixer = ("w_o", "w_a", "w_b") if on_mesh else ()
    (dq, dkv, dkvc, dsink), got = _attn_bwd(q, kv, kvc, sink, dya, ya, lse, scatter=[blocks[n] for n in mixer])
    recv.update(zip(mixer, got))
    g_wkv_ctx, st0 = _ctx_bwd(dkvc, ctx, hc, w_in)
    (dqkv, grad_x, st1), _ = _proj_bwd(dq, dkv, dpb, x, dr1, modx1, w_in, cos, sin, tm_big)
    dbs = jnp.sum(dbs_full.reshape(BLK, N_GRP, GRP_D), axis=2).T
    early = [jnp.concatenate([st_ffn, st0], axis=0), st4, dsink, dws, dbs]
    g_in = _tn_matmul(dqkv, h, D, "tn_w_in_qkv", BF16, init=g_wkv_ctx, tk=1024, out_rows=IN_W)
    into = (g_in, dqkv.shape[1])
    if on_mesh:
        g_in, early_gathered = _tn_matmul(dpb, h, D, "tn_w_in_rest", BF16, tk=1024, into=into, gather=early)
    else:
        g_in, early_gathered = _tn_matmul(dpb, h, D, "tn_w_in_rest", BF16, tk=1024, into=into), None
    blocks["w_in"] = _eighths(g_in)
    return grad_x, dict(early=early, early_gathered=early_gathered, late=st1), blocks, recv


BIG = ("w_in", "w_a", "w_b", "w_o", "w_fi", "w_fo")
ROW_SHARDED = ("w_o", "w_fo")


def _half_of_shard(shard, c):
    r = shard.shape[0]
    return lax.dynamic_slice_in_dim(shard, c * (r // 2), r // 2, axis=0)


def _eighths(v):
    rows = v.shape[-2] * (v.shape[0] if v.ndim == 3 else 1)
    return v.reshape(N_DEV, rows // N_DEV, v.shape[-1])


def kernel(x, c, ctx, c_ctx, w_ada, b_ada, w_in, attn_sink, gmlp_ln_g, gmlp_ln_b, w_spatial, b_spatial, w_branch_a, w_branch_b, w_out, ln1_g, ln1_b, w_ffn_in, w_ffn_out, ln2_g, ln2_b, loss_target, m_c_ctx, m_w_ada, m_b_ada, m_w_in, m_attn_sink, m_gmlp_ln_g, m_gmlp_ln_b, m_w_spatial, m_b_spatial, m_w_branch_a, m_w_branch_b, m_w_out, m_ln1_g, m_ln1_b, m_w_ffn_in, m_w_ffn_out, m_ln2_g, m_ln2_b, v_c_ctx, v_w_ada, v_b_ada, v_w_in, v_attn_sink, v_gmlp_ln_g, v_gmlp_ln_b, v_w_spatial, v_b_spatial, v_w_branch_a, v_w_branch_b, v_w_out, v_ln1_g, v_ln1_b, v_w_ffn_in, v_w_ffn_out, v_ln2_g, v_ln2_b):
    mx, my, mc = _my_pos()
    me = 4 * mx + 2 * my + mc
    chip = 2 * mx + my
    shards = dict(w_in=w_in[0].T, w_a=w_branch_a[0], w_b=w_branch_b[0], w_o=w_out[0], w_fi=w_ffn_in[0], w_fo=w_ffn_out[0])

    halves = {n: _half_of_shard(shards[n], mc).astype(BF16) for n in BIG}
    c_rows = jnp.concatenate([c, jnp.zeros((7, D), F32)], axis=0)
    _, g_in, _, mod_g, sc_all = _gather_and_modulate(c_rows, c_ctx[None, :], halves["w_in"], w_ada[0])
    wb = dict(w_in=_with_own([g_in], [halves["w_in"]], me)[0].reshape(IN_W, D))
    mod_all = jnp.concatenate([mod_g[2 * s] for s in range(4)], axis=1) + b_ada
    mod_x = lax.dynamic_slice_in_dim(mod_all, me, 1, axis=0).reshape(6, D)
    mod_c = mod_all[8].reshape(6, D)[0:2]

    grad_x, small, blocks, recv = _local_step(
        x[0], ctx[0], loss_target[0], mod_x, mod_c, wb, attn_sink, gmlp_ln_g, gmlp_ln_b, w_spatial[0], b_spatial[0],
        ln1_g, ln1_b, ln2_g, ln2_b, later=halves, me=me)

    me_arr = jnp.reshape(me, (1,)).astype(jnp.int32)
    sums_early, (recv["w_in"],), late = _sum_blocks_hosting(
        [recv[n] for n in BIG[1:]], [blocks[n] for n in BIG[1:]], me_arr, "sum_grads_scatter_w_in_gather_small",
        scatter=[blocks["w_in"]], gather=[small["late"]])
    summed = dict(zip(BIG[1:], sums_early))
    late = _with_own(late, [small["late"]], me)[0]
    gathered = _with_own(small["early_gathered"], small["early"], me)
    gathered[0] = jnp.concatenate([gathered[0][:, :16], late, gathered[0][:, 16:]], axis=1)

    sums = _sum8_many(gathered, "sum_small")
    stats = sums[0]
    loss = 0.5 * jnp.sum(stats[ROW_LOSS]) / D
    dmod_x_all = jnp.concatenate([gathered[0][:, r_, :] for r_ in ROWS_DMOD_X], axis=1)
    dmod_c_full = jnp.concatenate([stats[r_] for r_ in ROWS_DMOD_C] + [jnp.zeros((4 * D,), F32)])
    dm_rows = jnp.concatenate([dmod_x_all, dmod_c_full[None, :], jnp.zeros((7, 6 * D), F32)], axis=0)
    cs = w_ada.shape[2]
    dm_shard = lax.dynamic_slice_in_dim(dm_rows, chip * cs, cs, axis=1)
    dmc_shard = jnp.concatenate([dm_shard[8:9], jnp.zeros((7, cs), F32)], axis=0)
    ada = _ada_bwd(sc_all.T, dm_shard, dmc_shard, w_ada[0], m_w_ada[0], v_w_ada[0])
    part = ada[4]
    part = part * (mc == 0).astype(F32)
    summed["w_in"], theirs_w_in, part_all, theirs = _sum_and_exchange(
        recv["w_in"], blocks["w_in"], me_arr, "sum_w_in_exchange_grads_gather_c_ctx", gather=[part],
        sibling=[summed[n] for n in BIG[1:]])
    theirs = dict(zip(BIG[1:], theirs), w_in=theirs_w_in)
    dsc = _sum8(_with_own(part_all, [part], me)[0], "sum_c_ctx")

    weights = dict(c_ctx=c_ctx, w_ada=w_ada, b_ada=b_ada, w_in=w_in, attn_sink=attn_sink, gmlp_ln_g=gmlp_ln_g,
                   gmlp_ln_b=gmlp_ln_b, w_spatial=w_spatial, b_spatial=b_spatial, w_branch_a=w_branch_a,
                   w_branch_b=w_branch_b, w_out=w_out, ln1_g=ln1_g, ln1_b=ln1_b, w_ffn_in=w_ffn_in, w_ffn_out=w_ffn_out,
                   ln2_g=ln2_g, ln2_b=ln2_b)
    ms = dict(c_ctx=m_c_ctx, w_ada=m_w_ada, b_ada=m_b_ada, w_in=m_w_in, attn_sink=m_attn_sink, gmlp_ln_g=m_gmlp_ln_g,
              gmlp_ln_b=m_gmlp_ln_b, w_spatial=m_w_spatial, b_spatial=m_b_spatial, w_branch_a=m_w_branch_a,
              w_branch_b=m_w_branch_b, w_out=m_w_out, ln1_g=m_ln1_g, ln1_b=m_ln1_b, w_ffn_in=m_w_ffn_in,
              w_ffn_out=m_w_ffn_out, ln2_g=m_ln2_g, ln2_b=m_ln2_b)
    vs = dict(c_ctx=v_c_ctx, w_ada=v_w_ada, b_ada=v_b_ada, w_in=v_w_in, attn_sink=v_attn_sink, gmlp_ln_g=v_gmlp_ln_g,
              gmlp_ln_b=v_gmlp_ln_b, w_spatial=v_w_spatial, b_spatial=v_b_spatial, w_branch_a=v_w_branch_a,
              w_branch_b=v_w_branch_b, w_out=v_w_out, ln1_g=v_ln1_g, ln1_b=v_ln1_b, w_ffn_in=v_w_ffn_in,
              w_ffn_out=v_w_ffn_out, ln2_g=v_ln2_g, ln2_b=v_ln2_b)
    order = list(weights)
    grads, delta, new_m, new_v = [dict(w_ada=t[None]) for t in ada[:4]]
    c_arr = jnp.reshape(mc, (1,)).astype(jnp.int32)
    names = dict(w_in="w_in", w_a="w_branch_a", w_b="w_branch_b", w_o="w_out", w_fi="w_ffn_in", w_fo="w_ffn_out")
    for k, n in names.items():
        flip = (lambda t: t.T) if k == "w_in" else (lambda t: t)
        outs = _adamw_halves(flip(weights[n][0]), summed[k], theirs[k], flip(ms[n][0]), flip(vs[n][0]), c_arr, "adamw_" + n)
        grads[n], delta[n], new_m[n], new_v[n] = [flip(t)[None] for t in outs]

    def view(a):
        return a.reshape(-1, a.shape[-1]) if a.ndim != 1 else a.reshape(1, -1)

    small = _adamw_small(sums, dsc, *[{n: view(d[n]) for n in SMALL} for d in (weights, ms, vs)])
    for out, src in zip((grads, delta, new_m, new_v), small):
        for n in SMALL:
            out[n] = src[n].reshape(weights[n].shape)

    return (loss, grad_x[None], *[grads[n] for n in order], *[delta[n] for n in order],
            *[new_m[n] for n in order], *[new_v[n] for n in order])
```

```python
import math

import jax
import jax.numpy as jnp
import numpy as np
from jax import lax
from jax.experimental import pallas as pl
from jax.experimental.pallas import tpu as pltpu

F32 = jnp.float32
BF16 = jnp.bfloat16

D = 1024
HEAD = 64
N_KV = 2
GROUP = 4
Q_W = 512
KV_W = 128
G_W = 512
BLK = 128
N_GRP = 8
GRP_D = 64
FH = 2816
IN_W = 3840
GRID_W = 64
ROPE_BASE = 10000.0
LN_EPS = 1e-5
NEG = -1e30
ALPHA = (2 * 1) ** 0.25
SCALE = HEAD ** -0.5
GELU_K = math.sqrt(2.0 / math.pi)
GELU_A = 0.044715
ADAM_LR = 0.001
ADAM_B1 = 0.9
ADAM_B2 = 0.999
ADAM_EPS = 1e-08
ADAM_WD = 0.01
ADAM_STEP = 10
N_DEV = 8
N_SHARD = 4
FH_SHARD = FH // 2
LANES = 128
VMEM_LIMIT = 56 * 1024 * 1024
MESH = pl.DeviceIdType.MESH


def _cp(*sem):
    return pltpu.CompilerParams(dimension_semantics=sem, vmem_limit_bytes=VMEM_LIMIT)


def _resident(shape):
    return pl.BlockSpec(shape, lambda *_: (0,) * len(shape), pipeline_mode=pl.Buffered(1))


def _rows(tm, width):
    return pl.BlockSpec((tm, width), lambda i: (i, 0))


def _acc(shape):
    return pl.BlockSpec(shape, lambda *_: (0,) * len(shape))


def _dot(a, b):
    return jnp.dot(a, b, preferred_element_type=F32)


def _dot_nt(a, b):
    return lax.dot_general(a, b, (((1,), (1,)), ((), ())), preferred_element_type=F32)


def _dot_tn(a, b):
    return lax.dot_general(a, b, (((0,), (0,)), ((), ())), preferred_element_type=F32)


def _ln(x):
    mu = jnp.mean(x, axis=-1, keepdims=True)
    xc = x - mu
    var = jnp.mean(xc * xc, axis=-1, keepdims=True)
    rstd = lax.rsqrt(var + LN_EPS)
    return xc * rstd, rstd


def _ln_bwd(dxhat, xhat, rstd):
    return (dxhat - jnp.mean(dxhat, axis=-1, keepdims=True)
            - xhat * jnp.mean(dxhat * xhat, axis=-1, keepdims=True)) * rstd


def _sig(x):
    return 0.5 + 0.5 * jnp.tanh(0.5 * x)


def _gelu(x):
    t = jnp.tanh(x * (GELU_K + (GELU_K * GELU_A) * (x * x)))
    hx = 0.5 * x
    return hx + hx * t, t


def _gelu_grad(x, t):
    return 0.5 + 0.5 * t + (0.5 * x) * (1.0 - t * t) * (GELU_K + (3.0 * GELU_K * GELU_A) * (x * x))


def _colsum(v):
    return jnp.sum(v, axis=0, keepdims=True)


def _partner(x):
    w = x.shape[1]
    lane = lax.broadcasted_iota(jnp.int32, x.shape, 1)
    return jnp.where((lane & 31) < 16, pltpu.roll(x, w - 16, 1), pltpu.roll(x, 16, 1))


def _rope(x, cos, sin):
    return x * cos + _partner(x) * sin


def _unrope(g, cos, sin):
    return g * cos + _partner(g * sin)


def _rope_tables(seq):
    inv = np.float32(ROPE_BASE) ** (-np.arange(HEAD // 4, dtype=np.float32) / np.float32(HEAD // 4))
    pos = np.arange(seq)
    ar = (pos // GRID_W).astype(np.float32)[:, None] * inv
    ac = (pos % GRID_W).astype(np.float32)[:, None] * inv
    cos = np.concatenate([np.cos(ar), np.cos(ar), np.cos(ac), np.cos(ac)], axis=-1)
    sin = np.concatenate([-np.sin(ar), np.sin(ar), -np.sin(ac), np.sin(ac)], axis=-1)
    reps = (1, LANES // HEAD)
    return jnp.asarray(np.tile(cos, reps), F32), jnp.asarray(np.tile(sin, reps), F32)


def _kv_rows_of_w_in():
    return pl.BlockSpec((2 * KV_W, D), lambda *_: (Q_W // (2 * KV_W), 0))


def _ctx_fwd(ctx, modc, w_in):
    n_ctx = ctx.shape[0]

    def body(ctx_ref, mod_ref, w_ref, hc_ref, kvc_ref, vac_ref):
        xhat, _ = _ln(ctx_ref[...])
        hc = (xhat * (1.0 + mod_ref[1:2, :]) + mod_ref[0:1, :]).astype(BF16)
        hc_ref[...] = hc
        kvc = _dot_nt(hc, w_ref[...]).astype(BF16)
        kvc_ref[...] = kvc
        vac_ref[...] = _with_ones(kvc[:, KV_W:])

    return pl.pallas_call(
        body, name="ctx_fwd", grid=(1,),
        in_specs=[_acc((n_ctx, D)), _acc((8, D)), _kv_rows_of_w_in()],
        out_specs=[_acc((n_ctx, D)), _acc((n_ctx, 2 * KV_W)), _acc((n_ctx, 2 * LANES))],
        out_shape=[jax.ShapeDtypeStruct((n_ctx, D), BF16), jax.ShapeDtypeStruct((n_ctx, 2 * KV_W), BF16),
                   jax.ShapeDtypeStruct((n_ctx, 2 * LANES), BF16)],
        compiler_params=_cp("arbitrary"),
    )(ctx, modc, w_in)


def _host_start(step, comm):
    if comm is not None:
        @pl.when(step == 0)
        def _():
            comm.start()


def _host_finish(step, last, comm, forward_at=None):
    if comm is None:
        return
    if forward_at is None or forward_at >= last:
        @pl.when(step == last)
        def _():
            comm.finish()
    else:
        @pl.when(step == forward_at)
        def _():
            comm.forward()

        @pl.when(step == last)
        def _():
            comm.drain()


def _proj_fwd(x, modx, w_in, cos, sin, tm, gather=()):
    seq = x.shape[0]
    ng = len(gather)

    def body(x_ref, mod_ref, w_ref, cos_ref, sin_ref, *rest):
        h_ref, q_ref, kv_ref, va_ref, uv_ref, gab_ref = rest[ng:ng + 6]
        comm = _Gather(rest[:ng], rest[ng + 6:2 * ng + 6], *rest[2 * ng + 6:]) if ng else None
        _host_start(pl.program_id(0), comm)
        xhat, _ = _ln(x_ref[...])
        h = (xhat * (1.0 + mod_ref[1:2, :]) + mod_ref[0:1, :]).astype(BF16)
        h_ref[...] = h
        cos1, sin1 = cos_ref[...], sin_ref[...]
        cos2 = jnp.concatenate([cos1, cos1], axis=1)
        sin2 = jnp.concatenate([sin1, sin1], axis=1)
        for j in range(Q_W // 256):
            t = _dot_nt(h, w_ref[256 * j:256 * (j + 1), :])
            q_ref[:, 256 * j:256 * (j + 1)] = (_rope(t, cos2, sin2) * SCALE).astype(BF16)
        t = _dot_nt(h, w_ref[Q_W:Q_W + 2 * KV_W, :])
        kv_ref[:, :KV_W] = _rope(t[:, :KV_W], cos1, sin1).astype(BF16)
        v = t[:, KV_W:].astype(BF16)
        kv_ref[:, KV_W:] = v
        va_ref[...] = _with_ones(v)
        o = Q_W + 2 * KV_W
        for j in range(2):
            uv_ref[:, G_W * j:G_W * (j + 1)] = _dot_nt(h, w_ref[o + G_W * j:o + G_W * (j + 1), :]).astype(BF16)
        o += 2 * G_W
        for j in range(4):
            gab_ref[:, 512 * j:512 * (j + 1)] = _dot_nt(h, w_ref[o + 512 * j:o + 512 * (j + 1), :]).astype(BF16)
        _host_finish(pl.program_id(0), seq // tm - 1, comm, forward_at=(seq // tm) // 2)

    out = pl.pallas_call(
        body, name="proj_fwd", grid=(seq // tm,),
        in_specs=[_rows(tm, D), _acc((8, D)), _resident((IN_W, D)), _rows(tm, LANES), _rows(tm, LANES)] + _comm_specs(ng),
        out_specs=[_rows(tm, D), _rows(tm, Q_W), _rows(tm, 2 * KV_W), _rows(tm, 2 * LANES), _rows(tm, 2 * G_W),
                   _rows(tm, 2 * D)] + _comm_specs(ng),
        out_shape=[jax.ShapeDtypeStruct((seq, D), BF16), jax.ShapeDtypeStruct((seq, Q_W), BF16),
                   jax.ShapeDtypeStruct((seq, 2 * KV_W), BF16), jax.ShapeDtypeStruct((seq, 2 * LANES), BF16),
                   jax.ShapeDtypeStruct((seq, 2 * G_W), BF16), jax.ShapeDtypeStruct((seq, 2 * D), BF16)] + _gathered_shapes(gather),
        scratch_shapes=_comm_scratch(ng) if ng else [],
        compiler_params=_cp("arbitrary"),
    )(x, modx, w_in, cos, sin, *gather)
    return out[:6], out[6:]


def _stack_heads(x, hk):
    return jnp.concatenate([x[:, (hk * GROUP + g) * HEAD:(hk * GROUP + g + 1) * HEAD] for g in range(GROUP)], axis=0)


def _band_masks(n, nb):
    rows = GROUP * BLK
    qi = lax.broadcasted_iota(jnp.int32, (rows, BLK), 0) & (BLK - 1)
    kj = lax.broadcasted_iota(jnp.int32, (rows, BLK), 1)
    return (kj >= qi) & (n > 0), (kj <= qi) & (n < nb - 1)


def _attn_scores(q, k_refs, hk, masks):
    q4 = _stack_heads(q, hk)
    ks = [r[:, hk * HEAD:(hk + 1) * HEAD] for r in k_refs]
    s = [_dot_nt(q4, k) for k in ks]
    s[1] = jnp.where(masks[0], s[1], NEG)
    s[3] = jnp.where(masks[1], s[3], NEG)
    return q4, ks, s


def _sink_rows(sink_ref, hk):
    rows = GROUP * BLK
    rg = lax.broadcasted_iota(jnp.int32, (rows, 1), 0) >> 7
    sink_v = jnp.full((rows, 1), sink_ref[0, hk * GROUP], F32)
    for g in range(1, GROUP):
        sink_v = jnp.where(rg == g, sink_ref[0, hk * GROUP + g], sink_v)
    return sink_v


def _with_ones(v):
    ones = jnp.ones((v.shape[0], HEAD), v.dtype)
    return jnp.concatenate([v[:, :HEAD], ones, v[:, HEAD:], ones], axis=1)


def _kv_specs(nb, qb):
    def spec(d):
        return pl.BlockSpec((BLK, 2 * KV_W), lambda n: (jnp.clip(qb * n + d, 0, nb - 1), 0))
    return [spec(d) for d in range(-1, qb + 1)]


def _attn_fwd(q, kv, va, kvc, vac, sink, gather=()):
    seq = q.shape[0]
    nb = seq // BLK
    n_ctx = kvc.shape[0]
    ng = len(gather)
    Q_BLOCKS = 1
    nkv = Q_BLOCKS + 2
    steps = nb // Q_BLOCKS

    def body(q_ref, *rest):
        kv_refs, va_refs = rest[:nkv], rest[nkv:2 * nkv]
        kvc_ref, vac_ref, sink_ref = rest[2 * nkv:2 * nkv + 3]
        rest = rest[2 * nkv + 3:]
        o_ref, lse_ref = rest[ng:ng + 2]
        comm = _Gather(rest[:ng], rest[ng + 2:2 * ng + 2], *rest[2 * ng + 2:]) if ng else None
        n = pl.program_id(0)
        _host_start(n, comm)
        lane = lax.broadcasted_iota(jnp.int32, (BLK, LANES), 1)
        for sub in range(Q_BLOCKS):
            rs = slice(sub * BLK, (sub + 1) * BLK)
            q = q_ref[rs, :]
            outs = []
            lse_all = jnp.zeros((BLK, LANES), F32)
            masks = _band_masks(Q_BLOCKS * n + sub, nb)
            for hk in range(N_KV):
                _, _, s = _attn_scores(q, (kvc_ref,) + kv_refs[sub:sub + 3], hk, masks)
                sink_v = _sink_rows(sink_ref, hk)
                tile_max = s[1]
                for t in [s[0][:, i * LANES:(i + 1) * LANES] for i in range(n_ctx // LANES)] + s[2:]:
                    tile_max = jnp.maximum(tile_max, t)
                m = jnp.maximum(sink_v, jnp.max(tile_max, axis=-1, keepdims=True))
                o = jnp.zeros((GROUP * BLK, LANES), F32)
                for t, va_ref in zip(s, (vac_ref,) + va_refs[sub:sub + 3]):
                    o = o + _dot(jnp.exp((t - m).astype(BF16)), va_ref[:, hk * LANES:(hk + 1) * LANES])
                denom = o[:, HEAD:HEAD + 1] + jnp.exp(sink_v - m)
                o4 = o[:, :HEAD] * (1.0 / denom)
                lse4 = m + jnp.log(denom)
                for g in range(GROUP):
                    outs.append(o4[g * BLK:(g + 1) * BLK, :])
                    lse_all = jnp.where(lane == hk * GROUP + g, lse4[g * BLK:(g + 1) * BLK, :], lse_all)
            o_ref[rs, :] = jnp.concatenate(outs, axis=1).astype(BF16)
            lse_ref[rs, :] = lse_all
        _host_finish(n, steps - 1, comm, forward_at=(3 * steps) // 4)

    tq = Q_BLOCKS * BLK
    out = pl.pallas_call(
        body, name="attn_fwd", grid=(steps,),
        in_specs=[_rows(tq, Q_W)] + _kv_specs(nb, Q_BLOCKS) + _kv_specs(nb, Q_BLOCKS)
        + [_acc((n_ctx, 2 * KV_W)), _acc((n_ctx, 2 * LANES)), pl.BlockSpec(memory_space=pltpu.SMEM)] + _comm_specs(ng),
        out_specs=[_rows(tq, Q_W), _rows(tq, LANES)] + _comm_specs(ng),
        out_shape=[jax.ShapeDtypeStruct((seq, Q_W), BF16), jax.ShapeDtypeStruct((seq, LANES), F32)] + _gathered_shapes(gather),
        scratch_shapes=_comm_scratch(ng) if ng else [],
        compiler_params=_cp("arbitrary"),
    )(q, *([kv] * nkv), *([va] * nkv), kvc, vac, sink, *gather)
    return out[:2], out[2:]


def _gmlp_chunk(u, vb, gp_ref, ws_ref, bias_ref):
    gu, tu = _gelu(u)
    gv, tv = _gelu(vb)
    vhat, rstd = _ln(gv)
    vn = (vhat * gp_ref[0:1, :] + gp_ref[1:2, :]).astype(BF16)
    s = bias_ref[...] + jnp.concatenate(
        [_dot(ws_ref[g * BLK:(g + 1) * BLK, :], vn[:, g * GRP_D:(g + 1) * GRP_D]) for g in range(N_GRP)], axis=1)
    return gu, tu, tv, vhat, rstd, vn, s


def _mix_fwd(uv, gab, ya, gp, ws_stack, bias_full, w_a, w_b, w_o, tm, gather=()):
    seq = uv.shape[0]
    ng = len(gather)
    steps = seq // tm

    def body(uv_ref, gab_ref, ya_ref, gp_ref, ws_ref, bias_ref, wa_ref, wb_ref, wo_ref, *rest):
        a_ref, b_ref, mix_ref, merged_ref, yb_ref = rest[ng:ng + 5]
        comm = _Gather(rest[:ng], rest[ng + 5:2 * ng + 5], *rest[2 * ng + 5:]) if ng else None
        _host_start(pl.program_id(0), comm)
        for c in range(tm // BLK):
            rs = slice(c * BLK, (c + 1) * BLK)
            gu, _, _, _, _, _, s = _gmlp_chunk(uv_ref[rs, :G_W].astype(F32), uv_ref[rs, G_W:].astype(F32), gp_ref, ws_ref, bias_ref)
            yb_ref[rs, :] = (gu * s).astype(BF16)
        a = _dot(ya_ref[...], wa_ref[...])
        b = _dot(yb_ref[...], wb_ref[...])
        a_ref[...] = a.astype(BF16)
        b_ref[...] = b.astype(BF16)
        merged = (_sig(gab_ref[:, :D].astype(F32)) * a + _sig(gab_ref[:, D:].astype(F32)) * b).astype(BF16)
        merged_ref[...] = merged
        mix_ref[...] = _dot(merged, wo_ref[...])
        _host_finish(pl.program_id(0), steps - 1, comm, forward_at=(3 * steps) // 4)

    out = pl.pallas_call(
        body, name="mix_fwd", grid=(steps,),
        in_specs=[_rows(tm, 2 * G_W), _rows(tm, 2 * D), _rows(tm, Q_W), _acc((8, G_W)),
                  _resident((N_GRP * BLK, BLK)), _acc((BLK, G_W)),
                  _resident((Q_W, D)), _resident((G_W, D)), _resident((D, D))] + _comm_specs(ng),
        out_specs=[_rows(tm, D), _rows(tm, D), _rows(tm, D), _rows(tm, D), _rows(tm, G_W)] + _comm_specs(ng),
        out_shape=[jax.ShapeDtypeStruct((seq, D), BF16), jax.ShapeDtypeStruct((seq, D), BF16),
                   jax.ShapeDtypeStruct((seq, D), F32), jax.ShapeDtypeStruct((seq, D), BF16),
                   jax.ShapeDtypeStruct((seq, G_W), BF16)] + _gathered_shapes(gather),
        scratch_shapes=_comm_scratch(ng) if ng else [],
        compiler_params=_cp("arbitrary"),
    )(uv, gab, ya, gp, ws_stack, bias_full, w_a, w_b, w_o, *gather)
    return out[:5], out[5:]


def _mid_recompute(x_ref, mix_ref, vec_ref):
    r1 = ALPHA * x_ref[...] + vec_ref[0:1, :] * mix_ref[...]
    xh1, rstd1 = _ln(r1)
    xmid = xh1 * vec_ref[1:2, :] + vec_ref[2:3, :]
    xh2, rstd2 = _ln(xmid)
    return xh1, rstd1, xmid, xh2, rstd2


def _ffn(x, mix, tgt, vec, w_fi, w_fo, tm):
    seq = x.shape[0]

    def body(x_ref, mix_ref, tgt_ref, vec_ref, wi_ref, wo_ref, act_ref, h2_ref, dff_ref, df_ref, dr1_ref, st_ref, gu_ref):
        @pl.when(pl.program_id(0) == 0)
        def _():
            st_ref[...] = jnp.zeros_like(st_ref)

        xh1, rstd1, xmid, xh2, rstd2 = _mid_recompute(x_ref, mix_ref, vec_ref)
        h2 = (xh2 * (1.0 + vec_ref[4:5, :]) + vec_ref[3:4, :]).astype(BF16)
        h2_ref[...] = h2
        halves = [(slice(hh * FH_SHARD, (hh + 1) * FH_SHARD), slice(FH + hh * FH_SHARD, FH + (hh + 1) * FH_SHARD))
                  for hh in range(2)]
        for hh, (cs, cu) in enumerate(halves):
            g = _dot(h2, wi_ref[hh])
            u = _dot(h2, wi_ref[2 + hh])
            gu_ref[:, cs] = g
            gu_ref[:, cu] = u
            act_ref[:, cs] = (g * _sig(g) * u).astype(BF16)
        f = _dot(act_ref[...], wo_ref[...])
        r2 = ALPHA * xmid + vec_ref[5:6, :] * f
        yh, rstd = _ln(r2)
        y = yh * vec_ref[6:7, :] + vec_ref[7:8, :]
        err = y - tgt_ref[...]
        dy = err / D
        dr2 = _ln_bwd(dy * vec_ref[6:7, :], yh, rstd)
        st_ref[0:1, :] += _colsum(err * err)
        st_ref[1:2, :] += _colsum(dy * yh)
        st_ref[2:3, :] += _colsum(dy)
        st_ref[3:4, :] += _colsum(dr2 * f)

        df = (dr2 * vec_ref[5:6, :]).astype(BF16)
        df_ref[...] = df
        da_all = _dot_nt(df, wo_ref[...])
        for cs, cu in halves:
            da = da_all[:, cs]
            g = gu_ref[:, cs]
            u = gu_ref[:, cu]
            sg = _sig(g)
            dff_ref[:, cs] = (da * u * sg * (1.0 + g * (1.0 - sg))).astype(BF16)
            dff_ref[:, cu] = (da * g * sg).astype(BF16)
        dh2 = _dot_nt(dff_ref[:, :FH_SHARD], wi_ref[0])
        for s in range(1, N_SHARD):
            dh2 = dh2 + _dot_nt(dff_ref[:, s * FH_SHARD:(s + 1) * FH_SHARD], wi_ref[s])
        dxmid = _ln_bwd(dh2 * (1.0 + vec_ref[4:5, :]), xh2, rstd2) + ALPHA * dr2
        dr1 = _ln_bwd(dxmid * vec_ref[1:2, :], xh1, rstd1)
        dr1_ref[...] = dr1
        st_ref[8:9, :] += _colsum(dh2 * xh2)
        st_ref[9:10, :] += _colsum(dh2)
        st_ref[10:11, :] += _colsum(dxmid * xh1)
        st_ref[11:12, :] += _colsum(dxmid)
        st_ref[12:13, :] += _colsum(dr1 * mix_ref[...])

    return pl.pallas_call(
        body, name="ffn", grid=(seq // tm,),
        in_specs=[_rows(tm, D), _rows(tm, D), _rows(tm, D), _acc((8, D)), _resident((N_SHARD, D, FH_SHARD)), _resident((FH, D))],
        out_specs=[_rows(tm, FH), _rows(tm, D), _rows(tm, 2 * FH), _rows(tm, D), _rows(tm, D), _acc((16, D))],
        out_shape=[jax.ShapeDtypeStruct((seq, FH), BF16), jax.ShapeDtypeStruct((seq, D), BF16),
                   jax.ShapeDtypeStruct((seq, 2 * FH), BF16), jax.ShapeDtypeStruct((seq, D), BF16),
                   jax.ShapeDtypeStruct((seq, D), F32), jax.ShapeDtypeStruct((16, D), F32)],
        scratch_shapes=[pltpu.VMEM((tm, 2 * FH), F32)],
        compiler_params=_cp("arbitrary"),
    )(x, mix, tgt, vec, w_fi, w_fo)


def _mix_bwd(dr1, a, b, gab, uv, merged, ya, yb, vec, gp, ws_stack, ws_stack_t, bias_full, w_a, w_b, w_o, tm, scatter=()):
    seq = dr1.shape[0]
    last = seq // tm - 1
    ns = len(scatter)

    def body(dr1_ref, a_ref, b_ref, gab_ref, uv_ref, mg_ref, ya_ref, yb_ref, vec_ref, gp_ref, ws_ref, wst_ref, bias_ref,
             wa_ref, wb_ref, wo_ref, *rest):
        dya_ref, dp_ref, dws_ref, dbs_ref, st_ref, gwo_ref, gwa_ref, gwb_ref = rest[ns:ns + 8]
        acc_o, acc_a, acc_b = rest[2 * ns + 8:2 * ns + 11]
        comm = _AllToAll(rest[:ns], rest[ns + 8:2 * ns + 8], *rest[2 * ns + 11:]) if ns else None
        _host_start(pl.program_id(0), comm)

        @pl.when(pl.program_id(0) == 0)
        def _():
            dws_ref[...] = jnp.zeros_like(dws_ref)
            dbs_ref[...] = jnp.zeros_like(dbs_ref)
            st_ref[...] = jnp.zeros_like(st_ref)
            acc_o[...] = jnp.zeros_like(acc_o)
            acc_a[...] = jnp.zeros_like(acc_a)
            acc_b[...] = jnp.zeros_like(acc_b)

        dmix = (dr1_ref[...] * vec_ref[0:1, :]).astype(BF16)
        acc_o[...] += _dot_tn(mg_ref[...], dmix)
        dmerged = _dot_nt(dmix, wo_ref[...])
        sa = _sig(gab_ref[:, :D].astype(F32))
        sb = _sig(gab_ref[:, D:].astype(F32))
        da_f = dmerged * sa
        db_f = dmerged * sb
        da = da_f.astype(BF16)
        db = db_f.astype(BF16)
        dp_ref[:, 2 * G_W:2 * G_W + D] = (da_f * a_ref[...].astype(F32) * (1.0 - sa)).astype(BF16)
        dp_ref[:, 2 * G_W + D:] = (db_f * b_ref[...].astype(F32) * (1.0 - sb)).astype(BF16)
        dya_ref[...] = _dot_nt(da, wa_ref[...]).astype(BF16)
        dyb = _dot_nt(db, wb_ref[...])
        acc_a[...] += _dot_tn(ya_ref[...], da)
        acc_b[...] += _dot_tn(yb_ref[...], db)

        @pl.when(pl.program_id(0) == last)
        def _():
            gwo_ref[...] = acc_o[...].astype(BF16)
            gwa_ref[...] = acc_a[...].astype(BF16)
            gwb_ref[...] = acc_b[...].astype(BF16)

        for c in range(tm // BLK):
            rs = slice(c * BLK, (c + 1) * BLK)
            u = uv_ref[rs, :G_W].astype(F32)
            vb = uv_ref[rs, G_W:].astype(F32)
            gu, tu, tv, vhat, rstd, vn, s = _gmlp_chunk(u, vb, gp_ref, ws_ref, bias_ref)
            dyb_c = dyb[rs, :]
            ds = dyb_c * gu
            du = dyb_c * s * _gelu_grad(u, tu)
            ds_b = ds.astype(BF16)
            dvn_g = []
            for g in range(N_GRP):
                cg = slice(g * GRP_D, (g + 1) * GRP_D)
                dvn_g.append(_dot(wst_ref[:, g * BLK:(g + 1) * BLK], ds_b[:, cg]))
                dws_ref[g * BLK:(g + 1) * BLK, :] += _dot_nt(ds_b[:, cg], vn[:, cg])
            dvn = jnp.concatenate(dvn_g, axis=1)
            dbs_ref[...] += ds
            st_ref[0:1, :] += _colsum(dvn * vhat)
            st_ref[1:2, :] += _colsum(dvn)
            dgv = _ln_bwd(dvn * gp_ref[0:1, :], vhat, rstd)
            dvb = dgv * _gelu_grad(vb, tv)
            dp_ref[rs, :G_W] = du.astype(BF16)
            dp_ref[rs, G_W:2 * G_W] = dvb.astype(BF16)
        _host_finish(pl.program_id(0), last, comm)

    pw = 2 * G_W + 2 * D
    out = pl.pallas_call(
        body, name="mix_bwd", grid=(seq // tm,),
        in_specs=[_rows(tm, D), _rows(tm, D), _rows(tm, D), _rows(tm, 2 * D), _rows(tm, 2 * G_W), _rows(tm, D), _rows(tm, Q_W),
                  _rows(tm, G_W), _acc((8, D)), _acc((8, G_W)),
                  _resident((N_GRP * BLK, BLK)), _resident((BLK, N_GRP * BLK)), _acc((BLK, G_W)),
                  _resident((Q_W, D)), _resident((G_W, D)), _resident((D, D))] + _comm_specs(ns),
        out_specs=[_rows(tm, Q_W), _rows(tm, pw), _acc((N_GRP * BLK, BLK)), _acc((BLK, G_W)), _acc((8, G_W)),
                   _acc((D, D)), _acc((Q_W, D)), _acc((G_W, D))] + _comm_specs(ns),
        out_shape=[jax.ShapeDtypeStruct((seq, Q_W), BF16), jax.ShapeDtypeStruct((seq, pw), BF16),
                   jax.ShapeDtypeStruct((N_GRP * BLK, BLK), F32), jax.ShapeDtypeStruct((BLK, G_W), F32),
                   jax.ShapeDtypeStruct((8, G_W), F32), jax.ShapeDtypeStruct((D, D), BF16),
                   jax.ShapeDtypeStruct((Q_W, D), BF16), jax.ShapeDtypeStruct((G_W, D), BF16)]
        + [jax.ShapeDtypeStruct(v.shape, v.dtype) for v in scatter],
        scratch_shapes=[pltpu.VMEM((D, D), F32), pltpu.VMEM((Q_W, D), F32), pltpu.VMEM((G_W, D), F32)]
        + (_comm_scratch(ns) if ns else []),
        compiler_params=_cp("arbitrary"),
    )(dr1, a, b, gab, uv, merged, ya, yb, vec, gp, ws_stack, ws_stack_t, bias_full, w_a, w_b, w_o, *scatter)
    return out[:8], out[8:]


def _attn_bwd(q, kv, kvc, sink, dya, ya, lse, scatter=()):
    seq = q.shape[0]
    nb = seq // BLK
    n_ctx = kvc.shape[0]
    ns = len(scatter)
    Q_BLOCKS = 2
    nkv = Q_BLOCKS + 2
    steps = nb // Q_BLOCKS

    def body(q_ref, *rest):
        kv_refs = rest[:nkv]
        kvc_ref, sink_ref, do_ref, o_ref, lse_ref = rest[nkv:nkv + 5]
        rest = rest[nkv + 5:]
        dq_ref, dkv_ref, dkvc_ref, dsink_ref = rest[ns:ns + 4]
        comm = _AllToAll(rest[:ns], rest[ns + 4:2 * ns + 4], *rest[2 * ns + 4:]) if ns else None
        n = pl.program_id(0)
        _host_start(n, comm)

        @pl.when(n == 0)
        def _():
            dkv_ref[...] = jnp.zeros_like(dkv_ref)
            dkvc_ref[...] = jnp.zeros_like(dkvc_ref)
            dsink_ref[...] = jnp.zeros_like(dsink_ref)

        lane = lax.broadcasted_iota(jnp.int32, (1, LANES), 1)
        for sub in range(Q_BLOCKS):
            rs = slice(sub * BLK, (sub + 1) * BLK)
            blk = Q_BLOCKS * n + sub
            q = q_ref[rs, :]
            do = do_ref[rs, :]
            out = o_ref[rs, :]
            lse_all = lse_ref[rs, :]
            k_refs = (kvc_ref,) + kv_refs[sub:sub + 3]
            masks = _band_masks(blk, nb)
            dqs, dks, dvs = [], [], []
            for hk in range(N_KV):
                q4, ks, s = _attn_scores(q, k_refs, hk, masks)
                vs = [r[:, KV_W + hk * HEAD:KV_W + (hk + 1) * HEAD] for r in k_refs]
                lse4 = jnp.concatenate([lse_all[:, hk * GROUP + g:hk * GROUP + g + 1] for g in range(GROUP)], axis=0)
                do4 = _stack_heads(do, hk)
                delta = jnp.sum(do4.astype(F32) * _stack_heads(out, hk).astype(F32), axis=-1, keepdims=True)
                p = [jnp.exp((t - lse4).astype(BF16)) for t in s]
                ds = [t * (_dot_nt(do4, v) - delta).astype(BF16) for t, v in zip(p, vs)]
                dq4 = _dot(ds[0], ks[0])
                for t, k in zip(ds[1:], ks[1:]):
                    dq4 = dq4 + _dot(t, k)
                dq4 = dq4 * SCALE
                dqs += [dq4[g * BLK:(g + 1) * BLK, :] for g in range(GROUP)]
                dks.append([_dot_tn(t, q4) for t in ds])
                dvs.append([_dot_tn(t, do4) for t in p])
                ps = jnp.exp(_sink_rows(sink_ref, hk) - lse4) * delta
                for g in range(GROUP):
                    part = -jnp.sum(ps[g * BLK:(g + 1) * BLK, :], axis=0, keepdims=True)
                    dsink_ref[0:1, :] += jnp.where(lane == hk * GROUP + g, part, 0.0)
            dq_ref[rs, :] = jnp.concatenate(dqs, axis=1)

            def piece(i):
                return jnp.concatenate([dks[0][i], dks[1][i], dvs[0][i], dvs[1][i]], axis=1)

            dkvc_ref[...] += piece(0)
            starts = (jnp.maximum(blk - 1, 0), blk, jnp.minimum(blk + 1, nb - 1))
            for i, st in enumerate(starts):
                r = pl.ds(pl.multiple_of(st * BLK, BLK), BLK)
                dkv_ref[r, :] += piece(i + 1)
        _host_finish(n, steps - 1, comm)

    tq = Q_BLOCKS * BLK
    out = pl.pallas_call(
        body, name="attn_bwd", grid=(steps,),
        in_specs=[_rows(tq, Q_W)] + _kv_specs(nb, Q_BLOCKS) + [_acc((n_ctx, 2 * KV_W)), pl.BlockSpec(memory_space=pltpu.SMEM),
                                                     _rows(tq, Q_W), _rows(tq, Q_W), _rows(tq, LANES)] + _comm_specs(ns),
        out_specs=[_rows(tq, Q_W), _acc((seq, 2 * KV_W)), _acc((n_ctx, 2 * KV_W)), _acc((8, LANES))] + _comm_specs(ns),
        out_shape=[jax.ShapeDtypeStruct((seq, Q_W), F32), jax.ShapeDtypeStruct((seq, 2 * KV_W), F32),
                   jax.ShapeDtypeStruct((n_ctx, 2 * KV_W), F32), jax.ShapeDtypeStruct((8, LANES), F32)]
        + [jax.ShapeDtypeStruct(v.shape, v.dtype) for v in scatter],
        scratch_shapes=_comm_scratch(ns) if ns else [],
        compiler_params=_cp("arbitrary"),
    )(q, *([kv] * nkv), kvc, sink, dya, ya, lse, *scatter)
    return out[:4], out[4:]


def _proj_bwd(dq, dkv, dpb, x, dr1, modx, w_in, cos, sin, tm, scatter=()):
    seq = x.shape[0]
    pw = IN_W - Q_W - 2 * KV_W
    ns = len(scatter)

    def body(dq_ref, dkv_ref, dpb_ref, x_ref, dr1_ref, mod_ref, w_ref, cos_ref, sin_ref, *rest):
        dqkv_ref, gx_ref, st_ref = rest[ns:ns + 3]
        comm = _AllToAll(rest[:ns], rest[ns + 3:2 * ns + 3], *rest[2 * ns + 3:]) if ns else None
        _host_start(pl.program_id(0), comm)

        @pl.when(pl.program_id(0) == 0)
        def _():
            st_ref[...] = jnp.zeros_like(st_ref)

        cos1, sin1 = cos_ref[...], sin_ref[...]
        cos2 = jnp.concatenate([cos1, cos1], axis=1)
        sin2 = jnp.concatenate([sin1, sin1], axis=1)
        for j in range(Q_W // 256):
            cs = slice(256 * j, 256 * (j + 1))
            dqkv_ref[:, cs] = _unrope(dq_ref[:, cs], cos2, sin2).astype(BF16)
        dqkv_ref[:, Q_W:Q_W + KV_W] = _unrope(dkv_ref[:, :KV_W], cos1, sin1).astype(BF16)
        dqkv_ref[:, Q_W + KV_W:] = dkv_ref[:, KV_W:].astype(BF16)
        o = Q_W + 2 * KV_W
        dh = _dot(dqkv_ref[...], w_ref[:o, :]) + _dot(dpb_ref[...], w_ref[o:, :])
        xhat, rstd = _ln(x_ref[...])
        st_ref[0:1, :] += _colsum(dh)
        st_ref[1:2, :] += _colsum(dh * xhat)
        gx_ref[...] = _ln_bwd(dh * (1.0 + mod_ref[1:2, :]), xhat, rstd) + ALPHA * dr1_ref[...]
        _host_finish(pl.program_id(0), seq // tm - 1, comm)

    out = pl.pallas_call(
        body, name="proj_bwd", grid=(seq // tm,),
        in_specs=[_rows(tm, Q_W), _rows(tm, 2 * KV_W), _rows(tm, pw), _rows(tm, D), _rows(tm, D), _acc((8, D)),
                  _resident((IN_W, D)), _rows(tm, LANES), _rows(tm, LANES)] + _comm_specs(ns),
        out_specs=[_rows(tm, Q_W + 2 * KV_W), _rows(tm, D), _acc((8, D))] + _comm_specs(ns),
        out_shape=[jax.ShapeDtypeStruct((seq, Q_W + 2 * KV_W), BF16), jax.ShapeDtypeStruct((seq, D), F32),
                   jax.ShapeDtypeStruct((8, D), F32)] + [jax.ShapeDtypeStruct(v.shape, v.dtype) for v in scatter],
        scratch_shapes=_comm_scratch(ns) if ns else [],
        compiler_params=_cp("arbitrary"),
    )(dq, dkv, dpb, x, dr1, modx, w_in, cos, sin, *scatter)
    return out[:3], out[3:]


def _ctx_bwd(dkvc, ctx, hc, w_in):
    n_ctx = ctx.shape[0]

    def body(dkvc_ref, ctx_ref, hc_ref, w_ref, dw_ref, st_ref):
        d = dkvc_ref[...].astype(BF16)
        dw_ref[...] = _dot_tn(d, hc_ref[...])
        dhc = _dot(d, w_ref[...])
        xhat, _ = _ln(ctx_ref[...])
        st_ref[...] = jnp.zeros_like(st_ref)
        st_ref[0:1, :] = _colsum(dhc)
        st_ref[1:2, :] = _colsum(dhc * xhat)

    return pl.pallas_call(
        body, name="ctx_bwd", grid=(1,),
        in_specs=[_acc((n_ctx, 2 * KV_W)), _acc((n_ctx, D)), _acc((n_ctx, D)), _kv_rows_of_w_in()],
        out_specs=[_acc((2 * KV_W, D)), _acc((8, D))],
        out_shape=[jax.ShapeDtypeStruct((2 * KV_W, D), F32), jax.ShapeDtypeStruct((8, D), F32)],
        compiler_params=_cp("arbitrary"),
    )(dkvc, ctx, hc, w_in)


def _tn_matmul(a, b, tn, name, out_dtype, shard_major=False, init=None, tk=512, out_rows=None, into=None,
               scatter=(), gather=()):
    t, ka = a.shape
    n = b.shape[1]
    tk = min(tk, t)
    nk = t // tk
    nj = n // tn
    has_init = init is not None
    in_place = into is not None
    assert not (scatter and gather) and not (in_place and (nj != 1 or shard_major or out_rows))
    moved = list(scatter) + list(gather)
    pattern = _AllToAll if scatter else _Gather
    ns = len(moved)
    n_in = 2 + has_init + in_place
    n_scr = 3 if in_place else 1

    def body(*refs):
        a_ref, b_ref = refs[:2]
        i_ref = refs[2] if has_init else None
        rest = refs[n_in:]
        o_ref = rest[ns]
        acc_ref = rest[2 * ns + 1]
        comm = pattern(rest[:ns], rest[ns + 1:2 * ns + 1], *rest[2 * ns + 1 + n_scr:]) if ns else None
        k = pl.program_id(1)
        step = pl.program_id(0) * nk + k
        _host_start(step, comm)

        @pl.when(k == 0)
        def _():
            acc_ref[...] = jnp.zeros_like(acc_ref)
            if has_init:
                acc_ref[ka - init.shape[0]:, :] = i_ref[...]

        acc_ref[...] += _dot_tn(a_ref[...], b_ref[...])

        @pl.when(k == nk - 1)
        def _():
            if in_place:
                stage_ref, sem = rest[2 * ns + 2:2 * ns + 4]
                stage_ref[...] = acc_ref[...].astype(out_dtype)
                write = pltpu.make_async_copy(stage_ref, o_ref.at[pl.ds(into[1], ka), :], sem.at[0])
                write.start()
                write.wait()
            else:
                o_ref[...] = acc_ref[...].astype(out_dtype)

        _host_finish(step, nj * nk - 1, comm, forward_at=(3 * nj * nk) // 4 if gather else None)

    in_specs = [pl.BlockSpec((tk, ka), lambda j, k: (k, 0)), pl.BlockSpec((tk, tn), lambda j, k: (k, j))]
    args = [a, b]
    if has_init:
        in_specs.append(pl.BlockSpec((init.shape[0], tn), lambda j, k: (0, j)))
        args.append(init)
    scratch = [pltpu.VMEM((ka, tn), F32)]
    aliases = {}
    if in_place:
        in_specs.append(pl.BlockSpec(memory_space=pl.ANY))
        args.append(into[0])
        aliases = {n_in - 1: 0}
        out_spec = pl.BlockSpec(memory_space=pl.ANY)
        out_shape = jax.ShapeDtypeStruct(into[0].shape, into[0].dtype)
        scratch += [pltpu.VMEM((ka, tn), out_dtype), pltpu.SemaphoreType.DMA((1,))]
    elif shard_major:
        out_spec = pl.BlockSpec((None, ka, tn), lambda j, k: (j, 0, 0))
        out_shape = jax.ShapeDtypeStruct((nj, ka, tn), out_dtype)
    else:
        out_spec = pl.BlockSpec((ka, tn), lambda j, k: (0, j))
        out_shape = jax.ShapeDtypeStruct((out_rows or ka, n), out_dtype)
    out = pl.pallas_call(
        body, name=name, grid=(nj, nk), in_specs=in_specs + _comm_specs(ns), out_specs=[out_spec] + _comm_specs(ns),
        out_shape=[out_shape] + [jax.ShapeDtypeStruct(v.shape, v.dtype) for v in scatter] + _gathered_shapes(gather),
        scratch_shapes=scratch + (_comm_scratch(ns) if ns else []), input_output_aliases=aliases,
        compiler_params=_cp("arbitrary", "arbitrary"),
    )(*args, *moved)
    return (out[0], out[1:]) if ns else out[0]


ADA_TILE = 512


def _gather_and_modulate(c_rows, c_ctx, w_half, w_ada):
    cs = w_ada.shape[1]
    vmem = pl.BlockSpec(memory_space=pltpu.VMEM)

    def body(c_ref, cctx_ref, wh_ref, wada_ref, cg_ref, wg_ref, mod_ref, modg_ref, sc_ref,
             c_v, wada_v, mod_v, send_c, recv_c, send_w, recv_w, send_m, recv_m, local):
        gc = _Gather([c_ref], [cg_ref], send_c, recv_c)
        gw = _Gather([wh_ref], [wg_ref], send_w, recv_w)
        gm = _Gather([mod_ref], [modg_ref], send_m, recv_m)
        px, py, pc = _my_pos()
        mine = 4 * px + 2 * py + pc
        gc.start()
        gw.start()
        own_c = pltpu.make_async_copy(c_ref, cg_ref.at[mine], local.at[0])
        load_w = pltpu.make_async_copy(wada_ref, wada_v, local.at[1])
        own_c.start()
        load_w.start()
        gc.finish()
        own_c.wait()
        load_c = pltpu.make_async_copy(cg_ref, c_v, local.at[2])
        load_c.start()
        load_c.wait()
        cc = jnp.concatenate([c_v[i, 0:1, :] for i in range(N_DEV)] + [cctx_ref[...], jnp.zeros((7, D), F32)], axis=0)
        sc = cc * _sig(cc)
        sc_ref[...] = sc
        load_w.wait()
        for j in range(cs // ADA_TILE):
            cols = slice(j * ADA_TILE, (j + 1) * ADA_TILE)
            mod_v[:, cols] = _dot(sc.astype(BF16), wada_v[:, cols].astype(BF16))
        store_m = pltpu.make_async_copy(mod_v, mod_ref, local.at[3])
        store_m.start()
        store_m.wait()
        own_m = pltpu.make_async_copy(mod_ref, modg_ref.at[mine], local.at[4])
        own_m.start()
        gm.start()
        gm.finish()
        own_m.wait()
        gw.finish()

    any_ = pl.BlockSpec(memory_space=pl.ANY)
    return pl.pallas_call(
        body, name="gather_and_modulate",
        in_specs=[any_, vmem, any_, any_], out_specs=[any_, any_, any_, any_, vmem],
        scratch_shapes=[pltpu.VMEM((N_DEV, 8, D), F32), pltpu.VMEM((D, cs), F32), pltpu.VMEM((16, cs), F32)]
        + _comm_scratch(1) + _comm_scratch(1) + _comm_scratch(1) + [pltpu.SemaphoreType.DMA((5,))],
        out_shape=[jax.ShapeDtypeStruct((N_DEV, 8, D), F32), jax.ShapeDtypeStruct((N_DEV,) + w_half.shape, w_half.dtype),
                   jax.ShapeDtypeStruct((16, cs), F32), jax.ShapeDtypeStruct((N_DEV, 16, cs), F32),
                   jax.ShapeDtypeStruct((16, D), F32)],
        compiler_params=pltpu.CompilerParams(vmem_limit_bytes=VMEM_LIMIT),
    )(c_rows, c_ctx, w_half, w_ada)


def _ada_bwd(sc_all_t, dm_all, dmc, w_ada, m, v):
    cs = w_ada.shape[1]

    def body(st_ref, dm_ref, dmc_ref, w_ref, m_ref, v_ref, gw_ref, d_ref, nm_ref, nv_ref, part_ref):
        @pl.when(pl.program_id(0) == 0)
        def _():
            part_ref[...] = jnp.zeros_like(part_ref)

        g = _dot(st_ref[...].astype(BF16), dm_ref[...].astype(BF16))
        gw_ref[...] = g
        d_ref[...], nm_ref[...], nv_ref[...] = _adam_update(w_ref[...], g, m_ref[...], v_ref[...])
        part_ref[...] += _dot_nt(dmc_ref[...].astype(BF16), w_ref[...].astype(BF16))

    cols = pl.BlockSpec((D, ADA_TILE), lambda j: (0, j))
    shp = jax.ShapeDtypeStruct((D, cs), F32)
    return pl.pallas_call(
        body, name="ada_bwd_adamw", grid=(cs // ADA_TILE,),
        in_specs=[_acc((D, 16)), pl.BlockSpec((16, ADA_TILE), lambda j: (0, j)), pl.BlockSpec((8, ADA_TILE), lambda j: (0, j)),
                  cols, cols, cols],
        out_specs=[cols] * 4 + [_acc((8, D))],
        out_shape=[shp] * 4 + [jax.ShapeDtypeStruct((8, D), F32)],
        compiler_params=_cp("arbitrary"),
    )(sc_all_t, dm_all, dmc, w_ada, m, v)


def _sum8(x, name, tr=256):
    _, r, c = x.shape
    tr = min(tr, r)
    while r % tr:
        tr -= 16

    def body(x_ref, o_ref):
        acc = x_ref[0].astype(F32)
        for i in range(1, N_DEV):
            acc = acc + x_ref[i].astype(F32)
        o_ref[...] = acc

    return pl.pallas_call(
        body, name=name, grid=(r // tr,),
        in_specs=[pl.BlockSpec((N_DEV, tr, c), lambda i: (0, i, 0))],
        out_specs=pl.BlockSpec((tr, c), lambda i: (i, 0)),
        out_shape=jax.ShapeDtypeStruct((r, c), F32),
        compiler_params=_cp("arbitrary"),
    )(x)


def _sum_blocks(recv, src, me, name, tr=256):
    _, r, c = recv.shape
    tr = min(tr, r)
    while r % tr:
        tr -= 16

    def body(me_ref, recv_ref, own_ref, o_ref):
        acc = own_ref[...].astype(F32)
        for k in range(1, N_DEV):
            acc = acc + recv_ref[me_ref[0] ^ k].astype(F32)
        o_ref[...] = acc

    return pl.pallas_call(
        body, name=name,
        grid_spec=pltpu.PrefetchScalarGridSpec(
            num_scalar_prefetch=1, grid=(r // tr,),
            in_specs=[pl.BlockSpec((N_DEV, tr, c), lambda i, me_ref: (0, i, 0)),
                      pl.BlockSpec((None, tr, c), lambda i, me_ref: (me_ref[0], i, 0))],
            out_specs=pl.BlockSpec((tr, c), lambda i, me_ref: (i, 0))),
        out_shape=jax.ShapeDtypeStruct((r, c), F32),
        compiler_params=_cp("arbitrary"),
    )(me, recv, src)


def _sum_blocks_hosting(recvs, srcs, me, name, scatter, gather):
    n, ns, ng = len(recvs), len(scatter), len(gather)

    def body(me_ref, *refs):
        recv_hbm, own_hbm = refs[:n], refs[n:2 * n]
        moved_in = refs[2 * n:2 * n + ns + ng]
        outs = refs[2 * n + ns + ng:]
        moved_out, theirs_out, scratch = outs[n:n + ns + ng], outs[n + ns + ng:2 * n + ns + ng], outs[2 * n + ns + ng:]
        recv_v, own_v, load_sems, sems = scratch[:n], scratch[n:2 * n], scratch[2 * n], scratch[2 * n + 1:]
        t = _AllToAll(moved_in[:ns], moved_out[:ns], sems[0], sems[1])
        g = _Gather(moved_in[ns:], moved_out[ns:], sems[2], sems[3])
        x, y, c = _my_pos()

        def push(i):
            return pltpu.make_async_remote_copy(
                src_ref=outs[i], dst_ref=theirs_out[i], send_sem=sems[4].at[i], recv_sem=sems[5].at[i],
                device_id=(x, y, 1 - c), device_id_type=MESH)

        g.start()
        t.start()
        me = me_ref[0]
        loads = [(pltpu.make_async_copy(recv_hbm[i], recv_v[i], load_sems.at[2 * i]),
                  pltpu.make_async_copy(own_hbm[i].at[me], own_v[i], load_sems.at[2 * i + 1])) for i in range(n)]
        for load in loads:
            load[0].start()
            load[1].start()
        for i in range(n):
            loads[i][0].wait()
            loads[i][1].wait()
            acc = own_v[i][...].astype(F32)
            for k in range(1, N_DEV):
                acc = acc + recv_v[i][me ^ k].astype(F32)
            outs[i][...] = acc
            push(i).start()
        g.finish()
        t.finish()
        for i in range(n):
            push(i).wait_recv()
            push(i).wait_send()

    def once(shape, index_map):
        return pl.BlockSpec(shape, index_map, pipeline_mode=pl.Buffered(1))

    shapes = [v.shape[1:] for v in recvs]
    sums = [jax.ShapeDtypeStruct(s, F32) for s in shapes]
    out = pl.pallas_call(
        body, name=name,
        grid_spec=pltpu.PrefetchScalarGridSpec(
            num_scalar_prefetch=1, grid=(1,),
            in_specs=_comm_specs(2 * n + ns + ng),
            out_specs=[once(s, lambda i, me_ref: (0, 0)) for s in shapes] + _comm_specs(ns + ng + n),
            scratch_shapes=[pltpu.VMEM(v.shape, v.dtype) for v in recvs] + [pltpu.VMEM(v.shape[1:], v.dtype) for v in recvs]
            + [pltpu.SemaphoreType.DMA((2 * n,))] + _comm_scratch(ns) + _comm_scratch(ng)
            + [pltpu.SemaphoreType.DMA((n,)), pltpu.SemaphoreType.DMA((n,))]),
        out_shape=sums + [jax.ShapeDtypeStruct(v.shape, v.dtype) for v in scatter] + _gathered_shapes(gather) + sums,
        compiler_params=_cp("arbitrary"),
    )(me, *recvs, *srcs, *scatter, *gather)
    return out[:n], out[n:n + ns], out[n + ns:n + ns + ng], out[n + ns + ng:]


def _sum8_many(xs, name):
    n = len(xs)

    def body(*refs):
        for x_ref, o_ref in zip(refs[:n], refs[n:]):
            acc = x_ref[0]
            for i in range(1, N_DEV):
                acc = acc + x_ref[i]
            o_ref[...] = acc

    vmem = pl.BlockSpec(memory_space=pltpu.VMEM)
    return pl.pallas_call(
        body, name=name, in_specs=[vmem] * n, out_specs=[vmem] * n,
        out_shape=[jax.ShapeDtypeStruct(v.shape[1:], v.dtype) for v in xs],
        compiler_params=pltpu.CompilerParams(vmem_limit_bytes=VMEM_LIMIT),
    )(*xs)


def _adam_update(w, g, m, v):
    nm = ADAM_B1 * m + (1.0 - ADAM_B1) * g
    nv = ADAM_B2 * v + (1.0 - ADAM_B2) * (g * g)
    m_hat = nm / (1.0 - ADAM_B1 ** ADAM_STEP)
    v_hat = nv / (1.0 - ADAM_B2 ** ADAM_STEP)
    return -ADAM_LR * (m_hat / (jnp.sqrt(v_hat) + ADAM_EPS) + ADAM_WD * w), nm, nv


ROW_LOSS, ROW_LN2_G, ROW_LN2_B, ROW_LN1_G, ROW_LN1_B = 0, 1, 2, 10, 11
ROWS_DMOD_X = (16, 17, 12, 9, 8, 3)
ROWS_DMOD_C = (24, 25)
SMALL = ("c_ctx", "b_ada", "attn_sink", "gmlp_ln_g", "gmlp_ln_b", "w_spatial", "b_spatial", "ln1_g", "ln1_b", "ln2_g", "ln2_b")


def _adamw_small(sums, dsc, w, m, v):
    n = len(SMALL)

    def body(*refs):
        st_ref, gm_ref, sk_ref, ws_ref, bs_ref, dsc_ref = refs[:6]
        w_refs = dict(zip(SMALL, refs[6:6 + n]))
        m_refs = dict(zip(SMALL, refs[6 + n:6 + 2 * n]))
        v_refs = dict(zip(SMALL, refs[6 + 2 * n:6 + 3 * n]))
        outs = refs[6 + 3 * n:]
        c = w_refs["c_ctx"][...]
        sg = _sig(c)
        dmod = [st_ref[r:r + 1, :] for r in ROWS_DMOD_X]
        dmod[0] = dmod[0] + st_ref[ROWS_DMOD_C[0]:ROWS_DMOD_C[0] + 1, :]
        dmod[1] = dmod[1] + st_ref[ROWS_DMOD_C[1]:ROWS_DMOD_C[1] + 1, :]
        grads = dict(
            c_ctx=dsc_ref[0:1, :] * (sg * (1.0 + c * (1.0 - sg))),
            b_ada=jnp.concatenate(dmod, axis=1),
            attn_sink=sk_ref[0:1, 0:N_KV * GROUP],
            gmlp_ln_g=gm_ref[0:1, :], gmlp_ln_b=gm_ref[1:2, :],
            w_spatial=ws_ref[...], b_spatial=bs_ref[...],
            ln1_g=st_ref[ROW_LN1_G:ROW_LN1_G + 1, :], ln1_b=st_ref[ROW_LN1_B:ROW_LN1_B + 1, :],
            ln2_g=st_ref[ROW_LN2_G:ROW_LN2_G + 1, :], ln2_b=st_ref[ROW_LN2_B:ROW_LN2_B + 1, :])
        for i, name in enumerate(SMALL):
            g = grads[name]
            d, nm, nv = _adam_update(w_refs[name][...], g, m_refs[name][...], v_refs[name][...])
            outs[i][...] = g
            outs[n + i][...] = d
            outs[2 * n + i][...] = nm
            outs[3 * n + i][...] = nv

    vmem = pl.BlockSpec(memory_space=pltpu.VMEM)
    args = list(sums) + [dsc] + [w[k] for k in SMALL] + [m[k] for k in SMALL] + [v[k] for k in SMALL]
    shapes = [jax.ShapeDtypeStruct(w[k].shape, F32) for k in SMALL]
    out = pl.pallas_call(
        body, name="adamw_small", in_specs=[vmem] * len(args), out_specs=[vmem] * (4 * n), out_shape=shapes * 4,
        compiler_params=pltpu.CompilerParams(vmem_limit_bytes=VMEM_LIMIT),
    )(*args)
    return [dict(zip(SMALL, out[i * n:(i + 1) * n])) for i in range(4)]


def _adamw_halves(w, mine, theirs, m, v, c_arr, name):
    r, c = w.shape
    tr = min(256, r // 2)
    while (r // 2) % tr:
        tr -= 8
    nt = (r // 2) // tr

    def body(c_ref, w_ref, mine_ref, theirs_ref, m_ref, v_ref, g_ref, d_ref, nm_ref, nv_ref):
        g = jnp.where(pl.program_id(0) == c_ref[0], mine_ref[...], theirs_ref[...])
        g_ref[...] = g
        d_ref[...], nm_ref[...], nv_ref[...] = _adam_update(w_ref[...], g, m_ref[...], v_ref[...])

    whole = pl.BlockSpec((tr, c), lambda hb, i, c_ref: (hb * nt + i, 0))
    mine_spec = pl.BlockSpec((tr, c), lambda hb, i, c_ref: (jnp.where(hb == c_ref[0], i, 0), 0))
    theirs_spec = pl.BlockSpec((tr, c), lambda hb, i, c_ref: (jnp.where(hb == c_ref[0], 0, i), 0))
    shp = jax.ShapeDtypeStruct((r, c), F32)
    return pl.pallas_call(
        body, name=name,
        grid_spec=pltpu.PrefetchScalarGridSpec(
            num_scalar_prefetch=1, grid=(2, nt), in_specs=[whole, mine_spec, theirs_spec, whole, whole],
            out_specs=[whole] * 4),
        out_shape=[shp] * 4,
        compiler_params=_cp("arbitrary", "arbitrary"),
    )(c_arr, w, mine, theirs, m, v)


def _my_pos():
    return lax.axis_index("x"), lax.axis_index("y"), lax.axis_index("c")


N_COPY = 7


class _Gather:
    def __init__(self, x_refs, out_refs, send_sems, recv_sems):
        self.x_refs, self.out_refs = x_refs, out_refs
        self.send_sems, self.recv_sems = send_sems, recv_sems
        x, y, c = _my_pos()
        self.c = c
        self.me, self.sibling = (x, y, c), (x, y, 1 - c)
        self.chips = [(1 - x, y), (x, 1 - y), (1 - x, 1 - y)]

    def _copy(self, a, k, block, to, from_input=False):
        px, py, pc = block
        rows = self.out_refs[a].at[4 * px + 2 * py + pc]
        return pltpu.make_async_remote_copy(
            src_ref=self.x_refs[a] if from_input else rows, dst_ref=rows,
            send_sem=self.send_sems.at[a * N_COPY + k], recv_sem=self.recv_sems.at[a * N_COPY + k],
            device_id=to, device_id_type=MESH)

    def start(self):
        n = len(self.x_refs)
        for a in range(n):
            self._copy(a, 0, self.me, self.sibling, from_input=True).start()
        for j, chip in enumerate(self.chips):
            for a in range(n):
                self._copy(a, 1 + j, self.me, (*chip, self.c), from_input=True).start()

    def forward(self):
        c = self.c
        for j, chip in enumerate(self.chips):
            for a in range(len(self.x_refs)):
                self._copy(a, 1 + j, (*chip, c), self.me).wait_recv()
                self._copy(a, 4 + j, (*chip, c), self.sibling).start()

    def finish(self):
        self.forward()
        self.drain()

    def drain(self):
        n = len(self.x_refs)
        c = self.c
        for a in range(n):
            self._copy(a, 0, self.sibling, self.me).wait_recv()
        for j, chip in enumerate(self.chips):
            for a in range(n):
                self._copy(a, 4 + j, (*chip, 1 - c), self.me).wait_recv()
        for a in range(n):
            self._copy(a, 0, self.me, self.sibling, from_input=True).wait_send()
            for j, chip in enumerate(self.chips):
                self._copy(a, 1 + j, self.me, (*chip, c), from_input=True).wait_send()
                self._copy(a, 4 + j, (*chip, c), self.sibling).wait_send()


def _comm_scratch(n):
    return [pltpu.SemaphoreType.DMA((n * N_COPY,)), pltpu.SemaphoreType.DMA((n * N_COPY,))]


def _comm_specs(n):
    return [pl.BlockSpec(memory_space=pl.ANY)] * n


def _gathered_shapes(xs):
    return [jax.ShapeDtypeStruct((N_DEV,) + v.shape, v.dtype) for v in xs]


def _with_own(gathered, xs, me):
    return [lax.dynamic_update_index_in_dim(g, v, me, 0) for g, v in zip(gathered, xs)]


class _AllToAll:
    def __init__(self, x_refs, out_refs, send_sems, recv_sems):
        self.x_refs, self.out_refs = x_refs, out_refs
        self.send_sems, self.recv_sems = send_sems, recv_sems
        self.pos = _my_pos()
        x, y, c = self.pos
        self.me = 4 * x + 2 * y + c

    def _peer(self, k):
        x, y, c = self.pos
        return (x ^ ((k >> 2) & 1), y ^ ((k >> 1) & 1), c ^ (k & 1))

    def _copy(self, a, k):
        p = self._peer(k)
        return pltpu.make_async_remote_copy(
            src_ref=self.x_refs[a].at[4 * p[0] + 2 * p[1] + p[2]], dst_ref=self.out_refs[a].at[self.me],
            send_sem=self.send_sems.at[a * N_COPY + k - 1], recv_sem=self.recv_sems.at[a * N_COPY + k - 1],
            device_id=p, device_id_type=MESH)

    def start(self):
        for k in range(1, N_DEV):
            for a in range(len(self.x_refs)):
                self._copy(a, k).start()

    def finish(self):
        for a in range(len(self.x_refs)):
            for k in range(1, N_DEV):
                self._copy(a, k).wait_recv()
            for k in range(1, N_DEV):
                self._copy(a, k).wait_send()


def _exchange(name, scatter=(), gather=(), sibling=()):
    ns, ng, nx = len(scatter), len(gather), len(sibling)
    n = ns + ng + nx

    def body(*refs):
        ins, outs, sems = refs[:n], refs[n:2 * n], list(refs[2 * n:])
        t = _AllToAll(ins[:ns], outs[:ns], sems.pop(0), sems.pop(0)) if ns else None
        g = _Gather(ins[ns:ns + ng], outs[ns:ns + ng], sems.pop(0), sems.pop(0)) if ng else None
        x, y, c = _my_pos()

        def push(a):
            return pltpu.make_async_remote_copy(
                src_ref=ins[ns + ng + a], dst_ref=outs[ns + ng + a], send_sem=sems[0].at[a], recv_sem=sems[1].at[a],
                device_id=(x, y, 1 - c), device_id_type=MESH)

        for comm in (g, t):
            if comm is not None:
                comm.start()
        for a in range(nx):
            push(a).start()
        for comm in (g, t):
            if comm is not None:
                comm.finish()
        for a in range(nx):
            push(a).wait_recv()
            push(a).wait_send()

    scratch = (_comm_scratch(ns) if ns else []) + (_comm_scratch(ng) if ng else [])
    scratch += [pltpu.SemaphoreType.DMA((nx,)), pltpu.SemaphoreType.DMA((nx,))] if nx else []
    out = pl.pallas_call(
        body, name=name,
        out_shape=[jax.ShapeDtypeStruct(v.shape, v.dtype) for v in scatter] + _gathered_shapes(gather)
        + [jax.ShapeDtypeStruct(v.shape, v.dtype) for v in sibling],
        in_specs=_comm_specs(n), out_specs=_comm_specs(n), scratch_shapes=scratch,
    )(*scatter, *gather, *sibling)
    return out[:ns], out[ns:ns + ng], out[ns + ng:]


def _sum_and_exchange(recv, src, me, name, gather, sibling):
    ng, nx = len(gather), len(sibling)
    shape = recv.shape[1:]

    def body(me_ref, recv_hbm, own_hbm, *refs):
        g_in, x_in = refs[:ng], refs[ng:ng + nx]
        outs = refs[ng + nx:]
        sum_ref, theirs_ref = outs[:2]
        g_out, x_out = outs[2:2 + ng], outs[2 + ng:2 + ng + nx]
        recv_v, own_v, sum_v, local, send_sems, recv_sems, g_send, g_recv = outs[2 + ng + nx:]
        g = _Gather(g_in, g_out, g_send, g_recv)
        x, y, c = _my_pos()

        def push(a):
            src_ref, dst_ref = (x_in[a], x_out[a]) if a < nx else (sum_ref, theirs_ref)
            return pltpu.make_async_remote_copy(
                src_ref=src_ref, dst_ref=dst_ref, send_sem=send_sems.at[a], recv_sem=recv_sems.at[a],
                device_id=(x, y, 1 - c), device_id_type=MESH)

        g.start()
        for a in range(nx):
            push(a).start()
        me = me_ref[0]
        loads = [pltpu.make_async_copy(recv_hbm, recv_v, local.at[0]),
                 pltpu.make_async_copy(own_hbm.at[me], own_v, local.at[1])]
        for load in loads:
            load.start()
        for load in loads:
            load.wait()
        acc = own_v[...].astype(F32)
        for k in range(1, N_DEV):
            acc = acc + recv_v[me ^ k].astype(F32)
        sum_v[...] = acc
        store = pltpu.make_async_copy(sum_v, sum_ref, local.at[2])
        store.start()
        store.wait()
        push(nx).start()
        g.finish()
        for a in range(nx + 1):
            push(a).wait_recv()
            push(a).wait_send()

    f32 = jax.ShapeDtypeStruct(shape, F32)
    out = pl.pallas_call(
        body, name=name,
        grid_spec=pltpu.PrefetchScalarGridSpec(
            num_scalar_prefetch=1, grid=(1,), in_specs=_comm_specs(2 + ng + nx), out_specs=_comm_specs(2 + ng + nx),
            scratch_shapes=[pltpu.VMEM(recv.shape, recv.dtype), pltpu.VMEM(shape, recv.dtype), pltpu.VMEM(shape, F32),
                            pltpu.SemaphoreType.DMA((3,)), pltpu.SemaphoreType.DMA((nx + 1,)),
                            pltpu.SemaphoreType.DMA((nx + 1,))] + _comm_scratch(ng)),
        out_shape=[f32, f32] + _gathered_shapes(gather) + [jax.ShapeDtypeStruct(v.shape, v.dtype) for v in sibling],
        compiler_params=_cp("arbitrary"),
    )(me, recv, src, *gather, *sibling)
    return out[0], out[1], out[2:2 + ng], out[2 + ng:]


def _row_tile(seq, want):
    return min(want, seq)


def _local_step(x, ctx, tgt, mod_x, mod_c, wb, sink, gmlp_g, gmlp_b, w_s, b_s, ln1_g, ln1_b, ln2_g, ln2_b,
                later=None, me=None):
    seq = x.shape[0]
    on_mesh = me is not None
    modx1 = jnp.concatenate([mod_x[0:2], jnp.zeros((6, D), F32)], axis=0)
    modc = jnp.concatenate([mod_c[0:2], jnp.zeros((6, D), F32)], axis=0)
    vec = jnp.concatenate([mod_x[2:3], ln1_g, ln1_b, mod_x[3:6], ln2_g, ln2_b], axis=0)
    gp = jnp.concatenate([gmlp_g, gmlp_b, jnp.zeros((6, G_W), F32)], axis=0)
    ws_stack = w_s.reshape(N_GRP * BLK, BLK).astype(BF16)
    ws_stack_t = jnp.transpose(w_s, (2, 0, 1)).reshape(BLK, N_GRP * BLK).astype(BF16)
    bias_full = jnp.repeat(b_s.T, GRP_D, axis=1)
    cos, sin = _rope_tables(seq)
    w_in = wb["w_in"]
    tm_big = _row_tile(seq, 512)
    tm_ffn = _row_tile(seq, 256)

    hc, kvc, vac = _ctx_fwd(ctx, modc, w_in)
    behind_proj = ("w_a", "w_b", "w_o") if on_mesh else ()
    behind_attn = ("w_fi",) if on_mesh else ()
    behind_mix = ("w_fo",) if on_mesh else ()
    wb = dict(wb)

    def whole(names, gathered):
        for n, g in zip(names, _with_own(list(gathered), [later[n] for n in names], me)):
            wb[n] = g.reshape(-1, g.shape[2]) if n in ROW_SHARDED else g.reshape(N_SHARD, 2 * g.shape[1], g.shape[2])

    (h, q, kv, va, uv, gab), got = _proj_fwd(x, modx1, w_in, cos, sin, _row_tile(seq, 1024), gather=[later[n] for n in behind_proj])
    whole(behind_proj, got)
    if on_mesh:
        for n in ("w_a", "w_b"):
            wb[n] = wb[n].transpose(1, 0, 2).reshape(wb[n].shape[1], D)
    (ya, lse), got = _attn_fwd(q, kv, va, kvc, vac, sink, gather=[later[n] for n in behind_attn])
    whole(behind_attn, got)
    (a, b, mix, merged, yb), got = _mix_fwd(uv, gab, ya, gp, ws_stack, bias_full, wb["w_a"], wb["w_b"], wb["w_o"], tm_big,
                                            gather=[later[n] for n in behind_mix])
    whole(behind_mix, got)
    act, h2, dff, df, dr1, st_ffn = _ffn(x, mix, tgt, vec, wb["w_fi"], wb["w_fo"], tm_ffn)
    blocks, recv = {}, {}
    blocks["w_fo"] = _eighths(_tn_matmul(act, df, 512, "tn_w_ffn_out", BF16, tk=2048))
    if on_mesh:
        g_w_fi, (recv["w_fo"],) = _tn_matmul(h2, dff, FH_SHARD, "tn_w_ffn_in", BF16, shard_major=True, tk=2048,
                                            scatter=[blocks["w_fo"]])
    else:
        g_w_fi = _tn_matmul(h2, dff, FH_SHARD, "tn_w_ffn_in", BF16, shard_major=True, tk=2048)
    blocks["w_fi"] = _eighths(g_w_fi)
    (dya, dpb, dws, dbs_full, st4, g_w_o, g_w_a, g_w_b), got = _mix_bwd(
        dr1, a, b, gab, uv, merged, ya, yb, vec, gp, ws_stack, ws_stack_t, bias_full, wb["w_a"], wb["w_b"], wb["w_o"],
        tm_big, scatter=[blocks["w_fi"]] if on_mesh else ())
    recv.update(zip(("w_fi",), got))
    shard_major = [g.reshape(g.shape[0], N_SHARD, D // N_SHARD).transpose(1, 0, 2) for g in (g_w_a, g_w_b)]
    blocks.update(w_o=_eighths(g_w_o), w_a=_eighths(shard_major[0]), w_b=_eighths(shard_major[1]))
    mixer = ("w_o", "w_a", "w_b") if on_mesh else ()
    (dq, dkv, dkvc, dsink), got = _attn_bwd(q, kv, kvc, sink, dya, ya, lse, scatter=[blocks[n] for n in mixer])
    recv.update(zip(mixer, got))
    g_wkv_ctx, st0 = _ctx_bwd(dkvc, ctx, hc, w_in)
    (dqkv, grad_x, st1), _ = _proj_bwd(dq, dkv, dpb, x, dr1, modx1, w_in, cos, sin, tm_big)
    dbs = jnp.sum(dbs_full.reshape(BLK, N_GRP, GRP_D), axis=2).T
    early = [jnp.concatenate([st_ffn, st0], axis=0), st4, dsink, dws, dbs]
    g_in = _tn_matmul(dqkv, h, D, "tn_w_in_qkv", BF16, init=g_wkv_ctx, tk=1024, out_rows=IN_W)
    into = (g_in, dqkv.shape[1])
    if on_mesh:
        g_in, early_gathered = _tn_matmul(dpb, h, D, "tn_w_in_rest", BF16, tk=1024, into=into, gather=early)
    else:
        g_in, early_gathered = _tn_matmul(dpb, h, D, "tn_w_in_rest", BF16, tk=1024, into=into), None
    blocks["w_in"] = _eighths(g_in)
    return grad_x, dict(early=early, early_gathered=early_gathered, late=st1), blocks, recv


BIG = ("w_in", "w_a", "w_b", "w_o", "w_fi", "w_fo")
ROW_SHARDED = ("w_o", "w_fo")


def _half_of_shard(shard, c):
    r = shard.shape[0]
    return lax.dynamic_slice_in_dim(shard, c * (r // 2), r // 2, axis=0)


def _eighths(v):
    rows = v.shape[-2] * (v.shape[0] if v.ndim == 3 else 1)
    return v.reshape(N_DEV, rows // N_DEV, v.shape[-1])


def kernel(x, c, ctx, c_ctx, w_ada, b_ada, w_in, attn_sink, gmlp_ln_g, gmlp_ln_b, w_spatial, b_spatial, w_branch_a, w_branch_b, w_out, ln1_g, ln1_b, w_ffn_in, w_ffn_out, ln2_g, ln2_b, loss_target, m_c_ctx, m_w_ada, m_b_ada, m_w_in, m_attn_sink, m_gmlp_ln_g, m_gmlp_ln_b, m_w_spatial, m_b_spatial, m_w_branch_a, m_w_branch_b, m_w_out, m_ln1_g, m_ln1_b, m_w_ffn_in, m_w_ffn_out, m_ln2_g, m_ln2_b, v_c_ctx, v_w_ada, v_b_ada, v_w_in, v_attn_sink, v_gmlp_ln_g, v_gmlp_ln_b, v_w_spatial, v_b_spatial, v_w_branch_a, v_w_branch_b, v_w_out, v_ln1_g, v_ln1_b, v_w_ffn_in, v_w_ffn_out, v_ln2_g, v_ln2_b):
    mx, my, mc = _my_pos()
    me = 4 * mx + 2 * my + mc
    chip = 2 * mx + my
    shards = dict(w_in=w_in[0].T, w_a=w_branch_a[0], w_b=w_branch_b[0], w_o=w_out[0], w_fi=w_ffn_in[0], w_fo=w_ffn_out[0])

    halves = {n: _half_of_shard(shards[n], mc).astype(BF16) for n in BIG}
    c_rows = jnp.concatenate([c, jnp.zeros((7, D), F32)], axis=0)
    _, g_in, _, mod_g, sc_all = _gather_and_modulate(c_rows, c_ctx[None, :], halves["w_in"], w_ada[0])
    wb = dict(w_in=_with_own([g_in], [halves["w_in"]], me)[0].reshape(IN_W, D))
    mod_all = jnp.concatenate([mod_g[2 * s] for s in range(4)], axis=1) + b_ada
    mod_x = lax.dynamic_slice_in_dim(mod_all, me, 1, axis=0).reshape(6, D)
    mod_c = mod_all[8].reshape(6, D)[0:2]

    grad_x, small, blocks, recv = _local_step(
        x[0], ctx[0], loss_target[0], mod_x, mod_c, wb, attn_sink, gmlp_ln_g, gmlp_ln_b, w_spatial[0], b_spatial[0],
        ln1_g, ln1_b, ln2_g, ln2_b, later=halves, me=me)

    me_arr = jnp.reshape(me, (1,)).astype(jnp.int32)
    sums_early, (recv["w_in"],), late, theirs_early = _sum_blocks_hosting(
        [recv[n] for n in BIG[1:]], [blocks[n] for n in BIG[1:]], me_arr, "sum_grads_scatter_w_in_gather_small",
        scatter=[blocks["w_in"]], gather=[small["late"]])
    summed = dict(zip(BIG[1:], sums_early))
    late = _with_own(late, [small["late"]], me)[0]
    gathered = _with_own(small["early_gathered"], small["early"], me)
    gathered[0] = jnp.concatenate([gathered[0][:, :16], late, gathered[0][:, 16:]], axis=1)

    sums = _sum8_many(gathered, "sum_small")
    stats = sums[0]
    loss = 0.5 * jnp.sum(stats[ROW_LOSS]) / D
    dmod_x_all = jnp.concatenate([gathered[0][:, r_, :] for r_ in ROWS_DMOD_X], axis=1)
    dmod_c_full = jnp.concatenate([stats[r_] for r_ in ROWS_DMOD_C] + [jnp.zeros((4 * D,), F32)])
    dm_rows = jnp.concatenate([dmod_x_all, dmod_c_full[None, :], jnp.zeros((7, 6 * D), F32)], axis=0)
    cs = w_ada.shape[2]
    dm_shard = lax.dynamic_slice_in_dim(dm_rows, chip * cs, cs, axis=1)
    dmc_shard = jnp.concatenate([dm_shard[8:9], jnp.zeros((7, cs), F32)], axis=0)
    ada = _ada_bwd(sc_all.T, dm_shard, dmc_shard, w_ada[0], m_w_ada[0], v_w_ada[0])
    part = ada[4]
    part = part * (mc == 0).astype(F32)
    summed["w_in"], theirs_w_in, part_all, _ = _sum_and_exchange(
        recv["w_in"], blocks["w_in"], me_arr, "sum_w_in_exchange_gather_c_ctx", gather=[part], sibling=[])
    theirs = dict(zip(BIG[1:], theirs_early), w_in=theirs_w_in)
    dsc = _sum8(_with_own(part_all, [part], me)[0], "sum_c_ctx")

    weights = dict(c_ctx=c_ctx, w_ada=w_ada, b_ada=b_ada, w_in=w_in, attn_sink=attn_sink, gmlp_ln_g=gmlp_ln_g,
                   gmlp_ln_b=gmlp_ln_b, w_spatial=w_spatial, b_spatial=b_spatial, w_branch_a=w_branch_a,
                   w_branch_b=w_branch_b, w_out=w_out, ln1_g=ln1_g, ln1_b=ln1_b, w_ffn_in=w_ffn_in, w_ffn_out=w_ffn_out,
                   ln2_g=ln2_g, ln2_b=ln2_b)
    ms = dict(c_ctx=m_c_ctx, w_ada=m_w_ada, b_ada=m_b_ada, w_in=m_w_in, attn_sink=m_attn_sink, gmlp_ln_g=m_gmlp_ln_g,
              gmlp_ln_b=m_gmlp_ln_b, w_spatial=m_w_spatial, b_spatial=m_b_spatial, w_branch_a=m_w_branch_a,
              w_branch_b=m_w_branch_b, w_out=m_w_out, ln1_g=m_ln1_g, ln1_b=m_ln1_b, w_ffn_in=m_w_ffn_in,
              w_ffn_out=m_w_ffn_out, ln2_g=m_ln2_g, ln2_b=m_ln2_b)
    vs = dict(c_ctx=v_c_ctx, w_ada=v_w_ada, b_ada=v_b_ada, w_in=v_w_in, attn_sink=v_attn_sink, gmlp_ln_g=v_gmlp_ln_g,
              gmlp_ln_b=v_gmlp_ln_b, w_spatial=v_w_spatial, b_spatial=v_b_spatial, w_branch_a=v_w_branch_a,
              w_branch_b=v_w_branch_b, w_out=v_w_out, ln1_g=v_ln1_g, ln1_b=v_ln1_b, w_ffn_in=v_w_ffn_in,
              w_ffn_out=v_w_ffn_out, ln2_g=v_ln2_g, ln2_b=v_ln2_b)
    order = list(weights)
    grads, delta, new_m, new_v = [dict(w_ada=t[None]) for t in ada[:4]]
    c_arr = jnp.reshape(mc, (1,)).astype(jnp.int32)
    names = dict(w_in="w_in", w_a="w_branch_a", w_b="w_branch_b", w_o="w_out", w_fi="w_ffn_in", w_fo="w_ffn_out")
    for k, n in names.items():
        flip = (lambda t: t.T) if k == "w_in" else (lambda t: t)
        outs = _adamw_halves(flip(weights[n][0]), summed[k], theirs[k], flip(ms[n][0]), flip(vs[n][0]), c_arr, "adamw_" + n)
        grads[n], delta[n], new_m[n], new_v[n] = [flip(t)[None] for t in outs]

    def view(a):
        return a.reshape(-1, a.shape[-1]) if a.ndim != 1 else a.reshape(1, -1)

    small = _adamw_small(sums, dsc, *[{n: view(d[n]) for n in SMALL} for d in (weights, ms, vs)])
    for out, src in zip((grads, delta, new_m, new_v), small):
        for n in SMALL:
            out[n] = src[n].reshape(weights[n].shape)

    return (loss, grad_x[None], *[grads[n] for n in order], *[delta[n] for n in order],
            *[new_m[n] for n in order], *[new_v[n] for n in order])
```
